```python
import math
import jax, jax.numpy as jnp
from jax import lax
import numpy as np

D_MODEL = 2048
BATCH = 8
SEQ = 4096
DEPTH = 1

CHUNK = 64
N_META = 16
PAD_LEAD = CHUNK - N_META
EPS = 1e-6

SSD_HEADS = 32
SSD_HEAD_DIM = 64
D_SSD = SSD_HEADS * SSD_HEAD_DIM
SSD_GROUPS = 8
SSD_HEADS_PER_GROUP = SSD_HEADS // SSD_GROUPS
D_STATE = 128
CONV_WIDTH = 4
D_CONV = D_SSD + 2 * SSD_GROUPS * D_STATE

ATT_Q_HEADS = 16
ATT_KV_HEADS = 4
ATT_REP = ATT_Q_HEADS // ATT_KV_HEADS
ATT_HEAD_DIM = 64
D_ATT = ATT_Q_HEADS * ATT_HEAD_DIM
D_KV = ATT_KV_HEADS * ATT_HEAD_DIM
WINDOW = 128
WINDOW_CHUNKS = WINDOW // CHUNK
BAND = (WINDOW_CHUNKS + 1) * CHUNK
ROPE_THETA = 10000.0

D_MIX = D_SSD + D_ATT
D_IN_PROJ = D_SSD + D_CONV + SSD_HEADS + D_ATT + 2 * D_KV + D_ATT

kernel_name = "hymba_ssd_swa_sink_streaming_layer"


def rmsnorm(x, w):
    x32 = x.astype(jnp.float32)
    y = x32 * lax.rsqrt(jnp.mean(x32 * x32, axis=-1, keepdims=True) + EPS)
    return (y * w.astype(jnp.float32)).astype(x.dtype)


def causal_depthwise_conv(u, w, b):
    out = lax.conv_general_dilated(
        u.astype(jnp.float32), w.astype(jnp.float32)[:, None, :],
        window_strides=(1,), padding=[(CONV_WIDTH - 1, 0)],
        dimension_numbers=("NWC", "WIO", "NWC"), feature_group_count=u.shape[-1])
    return out + b.astype(jnp.float32)


def ssd_chunked_scan(xs, dt, a, b_mat, c_mat):
    bsz, lp, g, r, p = xs.shape
    n = b_mat.shape[-1]
    nc = lp // CHUNK
    xs = xs.reshape(bsz, nc, CHUNK, g, r, p)
    dt = dt.reshape(bsz, nc, CHUNK, g, r)
    b_mat = b_mat.reshape(bsz, nc, CHUNK, g, n)
    c_mat = c_mat.reshape(bsz, nc, CHUNK, g, n)
    a_cs = jnp.cumsum(dt * a, axis=2)
    x_dt = xs * dt[..., None]
    causal = jnp.tril(jnp.ones((CHUNK, CHUNK), dtype=bool))[:, :, None, None]
    seg = a_cs[:, :, :, None] - a_cs[:, :, None, :]
    decay_ls = jnp.exp(jnp.where(causal, seg, -jnp.inf))
    cb = jnp.einsum("bclgn,bcsgn->bclsg", c_mat, b_mat)
    y_diag = jnp.einsum("bclsg,bclsgr,bcsgrp->bclgrp", cb, decay_ls, x_dt)
    decay_to_end = jnp.exp(a_cs[:, :, -1:] - a_cs)
    states = jnp.einsum("bclgn,bclgr,bclgrp->bcgrpn", b_mat, decay_to_end, x_dt)
    chunk_decay = jnp.exp(a_cs[:, :, -1])

    def step(h, inp):
        st, dec = inp
        return h * dec[..., None, None] + st, h

    h0 = jnp.zeros((bsz, g, r, p, n), xs.dtype)
    _, prev = lax.scan(step, h0, (jnp.moveaxis(states, 1, 0), jnp.moveaxis(chunk_decay, 1, 0)))
    prev = jnp.moveaxis(prev, 0, 1)
    y_off = jnp.einsum("bclgn,bcgrpn,bclgr->bclgrp", c_mat, prev, jnp.exp(a_cs))
    return (y_diag + y_off).reshape(bsz, lp, g, r, p)


def rope(t, pos):
    half = t.shape[-1] // 2
    inv = ROPE_THETA ** (-jnp.arange(half, dtype=jnp.float32) / half)
    ang = pos.astype(jnp.float32)[:, None] * inv[None, :]
    cos = jnp.cos(ang)[None, :, None, :]
    sin = jnp.sin(ang)[None, :, None, :]
    t1, t2 = t[..., :half], t[..., half:]
    return jnp.concatenate([t1 * cos - t2 * sin, t1 * sin + t2 * cos], axis=-1)


def banded_sink_attention(q, k, v, sinks):
    bsz, lp = q.shape[:2]
    nc = lp // CHUNK
    q = q.reshape(bsz, nc, CHUNK, ATT_KV_HEADS, ATT_REP, ATT_HEAD_DIM)
    k = k.reshape(bsz, nc, CHUNK, ATT_KV_HEADS, ATT_HEAD_DIM)
    v = v.reshape(bsz, nc, CHUNK, ATT_KV_HEADS, ATT_HEAD_DIM)
    padw = ((0, 0), (WINDOW_CHUNKS, 0), (0, 0), (0, 0), (0, 0))
    kp, vp = jnp.pad(k, padw), jnp.pad(v, padw)
    k_band = jnp.concatenate([kp[:, j:j + nc] for j in range(WINDOW_CHUNKS + 1)], axis=2)
    v_band = jnp.concatenate([vp[:, j:j + nc] for j in range(WINDOW_CHUNKS + 1)], axis=2)
    scale = ATT_HEAD_DIM ** -0.5
    s = jnp.einsum("bclhrd,bcshd->bchrls", q, k_band) * scale
    key_abs = (jnp.arange(nc)[:, None] - WINDOW_CHUNKS) * CHUNK + jnp.arange(BAND)[None, :]
    valid = key_abs >= PAD_LEAD
    s = jnp.where(valid[None, :, None, None, None, :], s, -jnp.inf)
    sink = sinks.astype(jnp.float32).reshape(ATT_KV_HEADS, ATT_REP)[None, None, :, :, None, None]
    m = jnp.maximum(jnp.max(s, axis=-1, keepdims=True), sink)
    pr = jnp.exp(s - m)
    denom = jnp.sum(pr, axis=-1, keepdims=True) + jnp.exp(sink - m)
    out = jnp.einsum("bchrls,bcshd->bclhrd", pr / denom, v_band)
    return out.reshape(bsz, lp, D_ATT)


def _fwd_setup_inputs(seed: int = 0) -> dict:
    key = jax.random.key(seed)
    ks = jax.random.split(key, 14)
    f32 = jnp.float32
    x = jax.random.normal(ks[0], (BATCH, SEQ, D_MODEL), f32)
    meta_tokens = jax.random.normal(ks[1], (N_META, D_MODEL), f32)
    norm_pre_w = 1.0 + 0.01 * jax.random.normal(ks[2], (DEPTH, D_MODEL), f32)
    w_in = jax.random.normal(ks[3], (DEPTH, D_MODEL, D_IN_PROJ), f32) * D_MODEL ** -0.5
    conv_w = jax.random.normal(ks[4], (DEPTH, CONV_WIDTH, D_CONV), f32) * CONV_WIDTH ** -0.5
    conv_b = 0.02 * jax.random.normal(ks[5], (DEPTH, D_CONV), f32)
    dt0 = jnp.exp(jax.random.uniform(ks[6], (DEPTH, SSD_HEADS), f32,
                                     minval=math.log(1e-3), maxval=math.log(1e-1)))
    dt_bias = dt0 + jnp.log(-jnp.expm1(-dt0))
    a_log = jnp.log(jax.random.uniform(ks[7], (DEPTH, SSD_HEADS), f32, minval=1.0, maxval=16.0))
    d_skip = 1.0 + 0.01 * jax.random.normal(ks[8], (DEPTH, SSD_HEADS), f32)
    ssd_norm_w = 1.0 + 0.01 * jax.random.normal(ks[9], (DEPTH, D_SSD), f32)
    attn_sinks = 0.5 * jax.random.normal(ks[10], (DEPTH, ATT_Q_HEADS), f32)
    w_out = jax.random.normal(ks[11], (DEPTH, D_MIX, D_MODEL), f32) * D_MIX ** -0.5
    norm_post_w = 1.0 + 0.01 * jax.random.normal(ks[12], (DEPTH, D_MODEL), f32)
    return {"x": x, "meta_tokens": meta_tokens, "norm_pre_w": norm_pre_w, "w_in": w_in,
            "conv_w": conv_w, "conv_b": conv_b, "dt_bias": dt_bias, "a_log": a_log,
            "d_skip": d_skip, "ssd_norm_w": ssd_norm_w, "attn_sinks": attn_sinks,
            "w_out": w_out, "norm_post_w": norm_post_w}


def _fwd_reference(x, meta_tokens, norm_pre_w, w_in, conv_w, conv_b, dt_bias, a_log, d_skip,
              ssd_norm_w, attn_sinks, w_out, norm_post_w):
    bsz, seq, _ = x.shape
    meta = jnp.broadcast_to(meta_tokens[None].astype(x.dtype), (bsz, N_META, D_MODEL))
    h = jnp.concatenate([meta, x], axis=1)
    lp = N_META + seq + PAD_LEAD
    idx = jnp.arange(lp)
    pos = idx - PAD_LEAD
    valid = (idx >= PAD_LEAD).astype(jnp.float32)
    split_pts = [D_SSD, D_SSD + D_CONV, D_SSD + D_CONV + SSD_HEADS,
                 D_SSD + D_CONV + SSD_HEADS + D_ATT,
                 D_SSD + D_CONV + SSD_HEADS + D_ATT + D_KV,
                 D_SSD + D_CONV + SSD_HEADS + D_ATT + 2 * D_KV]
    for layer in range(DEPTH):
        hn = rmsnorm(h, norm_pre_w[layer])
        proj = jnp.matmul(hn, w_in[layer]).astype(jnp.float32)
        proj = jnp.pad(proj, ((0, 0), (PAD_LEAD, 0), (0, 0)))
        z, xbc, dt_raw, q, k, v, g_att = jnp.split(proj, split_pts, axis=-1)

        xbc = jax.nn.silu(causal_depthwise_conv(xbc, conv_w[layer], conv_b[layer]))
        xs, b_mat, c_mat = jnp.split(xbc, [D_SSD, D_SSD + SSD_GROUPS * D_STATE], axis=-1)
        dt = jax.nn.softplus(dt_raw + dt_bias[layer].astype(jnp.float32)) * valid[None, :, None]
        a = -jnp.exp(a_log[layer].astype(jnp.float32))
        xs = xs.reshape(bsz, lp, SSD_GROUPS, SSD_HEADS_PER_GROUP, SSD_HEAD_DIM)
        y = ssd_chunked_scan(xs, dt.reshape(bsz, lp, SSD_GROUPS, SSD_HEADS_PER_GROUP),
                             a.reshape(SSD_GROUPS, SSD_HEADS_PER_GROUP),
                             b_mat.reshape(bsz, lp, SSD_GROUPS, D_STATE),
                             c_mat.reshape(bsz, lp, SSD_GROUPS, D_STATE))
        y = y + d_skip[layer].astype(jnp.float32).reshape(SSD_GROUPS, SSD_HEADS_PER_GROUP)[..., None] * xs
        y = y.reshape(bsz, lp, D_SSD) * jax.nn.silu(z)
        y = rmsnorm(y.reshape(bsz, lp, SSD_GROUPS, D_SSD // SSD_GROUPS),
                    jnp.ones((D_SSD // SSD_GROUPS,), jnp.float32)).reshape(bsz, lp, D_SSD)
        y = y * ssd_norm_w[layer].astype(jnp.float32)

        q = rope(q.reshape(bsz, lp, ATT_Q_HEADS, ATT_HEAD_DIM), pos)
        k = rope(k.reshape(bsz, lp, ATT_KV_HEADS, ATT_HEAD_DIM), pos)
        v = v.reshape(bsz, lp, ATT_KV_HEADS, ATT_HEAD_DIM)
        att = banded_sink_attention(q, k, v, attn_sinks[layer]) * jax.nn.silu(g_att)

        mix = jnp.concatenate([y, att], axis=-1)[:, PAD_LEAD:].astype(h.dtype)
        out = jnp.matmul(mix, w_out[layer])
        h = h + rmsnorm(out, norm_post_w[layer])
    return h[:, N_META:]


import jax as _jax
import jax.numpy as _jnp

TWIN_FORMAT = 'train_step'
FWD_PARAMS = ['x', 'meta_tokens', 'norm_pre_w', 'w_in', 'conv_w', 'conv_b', 'dt_bias', 'a_log', 'd_skip', 'ssd_norm_w', 'attn_sinks', 'w_out', 'norm_post_w']
TWIN_WEIGHTS = ['meta_tokens', 'norm_pre_w', 'w_in', 'conv_w', 'conv_b', 'dt_bias', 'a_log', 'd_skip', 'ssd_norm_w', 'attn_sinks', 'w_out', 'norm_post_w']
TWIN_DIFF_INPUT = 'x'
TWIN_INPUTS = ['x', 'meta_tokens', 'norm_pre_w', 'w_in', 'conv_w', 'conv_b', 'dt_bias', 'a_log', 'd_skip', 'ssd_norm_w', 'attn_sinks', 'w_out', 'norm_post_w', 'loss_target', 'm_meta_tokens', 'm_norm_pre_w', 'm_w_in', 'm_conv_w', 'm_conv_b', 'm_dt_bias', 'm_a_log', 'm_d_skip', 'm_ssd_norm_w', 'm_attn_sinks', 'm_w_out', 'm_norm_post_w', 'v_meta_tokens', 'v_norm_pre_w', 'v_w_in', 'v_conv_w', 'v_conv_b', 'v_dt_bias', 'v_a_log', 'v_d_skip', 'v_ssd_norm_w', 'v_attn_sinks', 'v_w_out', 'v_norm_post_w']
TWIN_OUTPUTS = ['loss', 'grad_x', 'grad_meta_tokens', 'grad_norm_pre_w', 'grad_w_in', 'grad_conv_w', 'grad_conv_b', 'grad_dt_bias', 'grad_a_log', 'grad_d_skip', 'grad_ssd_norm_w', 'grad_attn_sinks', 'grad_w_out', 'grad_norm_post_w', 'delta_meta_tokens', 'delta_norm_pre_w', 'delta_w_in', 'delta_conv_w', 'delta_conv_b', 'delta_dt_bias', 'delta_a_log', 'delta_d_skip', 'delta_ssd_norm_w', 'delta_attn_sinks', 'delta_w_out', 'delta_norm_post_w', 'new_m_meta_tokens', 'new_m_norm_pre_w', 'new_m_w_in', 'new_m_conv_w', 'new_m_conv_b', 'new_m_dt_bias', 'new_m_a_log', 'new_m_d_skip', 'new_m_ssd_norm_w', 'new_m_attn_sinks', 'new_m_w_out', 'new_m_norm_post_w', 'new_v_meta_tokens', 'new_v_norm_pre_w', 'new_v_w_in', 'new_v_conv_w', 'new_v_conv_b', 'new_v_dt_bias', 'new_v_a_log', 'new_v_d_skip', 'new_v_ssd_norm_w', 'new_v_attn_sinks', 'new_v_w_out', 'new_v_norm_post_w']
TWIN_LEAF_KINDS = {'loss': 'loss', 'grad_x': 'grad_x', 'grad_meta_tokens': 'grad_w', 'grad_norm_pre_w': 'grad_w', 'grad_w_in': 'grad_w', 'grad_conv_w': 'grad_w', 'grad_conv_b': 'grad_w', 'grad_dt_bias': 'grad_w', 'grad_a_log': 'grad_w', 'grad_d_skip': 'grad_w', 'grad_ssd_norm_w': 'grad_w', 'grad_attn_sinks': 'grad_w', 'grad_w_out': 'grad_w', 'grad_norm_post_w': 'grad_w', 'delta_meta_tokens': 'delta_w', 'delta_norm_pre_w': 'delta_w', 'delta_w_in': 'delta_w', 'delta_conv_w': 'delta_w', 'delta_conv_b': 'delta_w', 'delta_dt_bias': 'delta_w', 'delta_a_log': 'delta_w', 'delta_d_skip': 'delta_w', 'delta_ssd_norm_w': 'delta_w', 'delta_attn_sinks': 'delta_w', 'delta_w_out': 'delta_w', 'delta_norm_post_w': 'delta_w', 'new_m_meta_tokens': 'new_m', 'new_m_norm_pre_w': 'new_m', 'new_m_w_in': 'new_m', 'new_m_conv_w': 'new_m', 'new_m_conv_b': 'new_m', 'new_m_dt_bias': 'new_m', 'new_m_a_log': 'new_m', 'new_m_d_skip': 'new_m', 'new_m_ssd_norm_w': 'new_m', 'new_m_attn_sinks': 'new_m', 'new_m_w_out': 'new_m', 'new_m_norm_post_w': 'new_m', 'new_v_meta_tokens': 'new_v', 'new_v_norm_pre_w': 'new_v', 'new_v_w_in': 'new_v', 'new_v_conv_w': 'new_v', 'new_v_conv_b': 'new_v', 'new_v_dt_bias': 'new_v', 'new_v_a_log': 'new_v', 'new_v_d_skip': 'new_v', 'new_v_ssd_norm_w': 'new_v', 'new_v_attn_sinks': 'new_v', 'new_v_w_out': 'new_v', 'new_v_norm_post_w': 'new_v'}


def _forward(args):
    return _fwd_reference(*[args[k] for k in FWD_PARAMS])


def _output_shape():
    def fwd():
        inp = _fwd_setup_inputs(0)
        return _fwd_reference(*[inp[k] for k in FWD_PARAMS])
    out = _jax.eval_shape(fwd)
    return out.shape, out.dtype

N_MICROBATCH = 1
ADAM_LR = 0.001
ADAM_B1 = 0.9
ADAM_B2 = 0.999
ADAM_EPS = 1e-08
ADAM_WD = 0.01
ADAM_STEP = 10
PER_EXAMPLE_BATCH_AXIS = {'x': 0, 'loss_target': 0}
SHARED_INPUTS = []
_WEIGHT_DTYPES = {'meta_tokens': _jnp.float32, 'norm_pre_w': _jnp.float32, 'w_in': _jnp.float32, 'conv_w': _jnp.float32, 'conv_b': _jnp.float32, 'dt_bias': _jnp.float32, 'a_log': _jnp.float32, 'd_skip': _jnp.float32, 'ssd_norm_w': _jnp.float32, 'attn_sinks': _jnp.float32, 'w_out': _jnp.float32, 'norm_post_w': _jnp.float32}
MOMENT_SCALE = {'meta_tokens': 3.690181e-03, 'norm_pre_w': 2.196586e-01, 'w_in': 1.071011e-01, 'conv_w': 1.117711e-01, 'conv_b': 2.599335e-01, 'dt_bias': 2.335668e-01, 'a_log': 4.099496e-01, 'd_skip': 8.389508e-01, 'ssd_norm_w': 1.869906e-01, 'attn_sinks': 7.751197e-04, 'w_out': 1.762717e-01, 'norm_post_w': 1.601959e+01}


def _to_microbatches(a, axis):
    t = _jnp.moveaxis(a, axis, 0)
    t = t.reshape((N_MICROBATCH, t.shape[0] // N_MICROBATCH) + t.shape[1:])
    return _jnp.moveaxis(t, 1, axis + 1)


def setup_inputs(seed: int = 0) -> dict:
    inp = _fwd_setup_inputs(seed)
    key = _jax.random.fold_in(_jax.random.key(seed), 7919)
    shape, _ = _output_shape()
    out = dict(inp)
    out["loss_target"] = _jax.random.normal(_jax.random.fold_in(key, 0), shape, _jnp.float32)
    for i, name in enumerate(TWIN_WEIGHTS):
        w = inp[name].astype(_jnp.float32)
        if MOMENT_SCALE is None:
            s = _jnp.sqrt(_jnp.mean(_jnp.square(w)) + 1e-30)
        else:
            s = MOMENT_SCALE[name]
        km, kv = _jax.random.split(_jax.random.fold_in(key, i + 1))
        out[name] = w
        out["m_" + name] = s * _jax.random.normal(km, w.shape, _jnp.float32)
        out["v_" + name] = (s * s) * _jax.random.uniform(kv, w.shape, _jnp.float32, 0.5, 1.5)
    if N_MICROBATCH > 1:
        for name, axis in PER_EXAMPLE_BATCH_AXIS.items():
            out[name] = _to_microbatches(out[name], axis)
    return {'x': out['x'], 'meta_tokens': out['meta_tokens'], 'norm_pre_w': out['norm_pre_w'], 'w_in': out['w_in'], 'conv_w': out['conv_w'], 'conv_b': out['conv_b'], 'dt_bias': out['dt_bias'], 'a_log': out['a_log'], 'd_skip': out['d_skip'], 'ssd_norm_w': out['ssd_norm_w'], 'attn_sinks': out['attn_sinks'], 'w_out': out['w_out'], 'norm_post_w': out['norm_post_w'], 'loss_target': out['loss_target'], 'm_meta_tokens': out['m_meta_tokens'], 'm_norm_pre_w': out['m_norm_pre_w'], 'm_w_in': out['m_w_in'], 'm_conv_w': out['m_conv_w'], 'm_conv_b': out['m_conv_b'], 'm_dt_bias': out['m_dt_bias'], 'm_a_log': out['m_a_log'], 'm_d_skip': out['m_d_skip'], 'm_ssd_norm_w': out['m_ssd_norm_w'], 'm_attn_sinks': out['m_attn_sinks'], 'm_w_out': out['m_w_out'], 'm_norm_post_w': out['m_norm_post_w'], 'v_meta_tokens': out['v_meta_tokens'], 'v_norm_pre_w': out['v_norm_pre_w'], 'v_w_in': out['v_w_in'], 'v_conv_w': out['v_conv_w'], 'v_conv_b': out['v_conv_b'], 'v_dt_bias': out['v_dt_bias'], 'v_a_log': out['v_a_log'], 'v_d_skip': out['v_d_skip'], 'v_ssd_norm_w': out['v_ssd_norm_w'], 'v_attn_sinks': out['v_attn_sinks'], 'v_w_out': out['v_w_out'], 'v_norm_post_w': out['v_norm_post_w']}


def _loss(weights, diff, rest, loss_target):
    with _jax.named_scope("forward"):
        args = {**rest, TWIN_DIFF_INPUT: diff, **{k: w.astype(_WEIGHT_DTYPES[k]) for k, w in weights.items()}}
        y = _forward(args)
    with _jax.named_scope("loss_head"):
        err = _jnp.square(y.astype(_jnp.float32) - loss_target)
        return 0.5 * _jnp.sum(_jnp.mean(err, axis=-1)) if err.ndim else 0.5 * err


def _adamw(w, g, m, v):
    m = ADAM_B1 * m + (1.0 - ADAM_B1) * g
    v = ADAM_B2 * v + (1.0 - ADAM_B2) * _jnp.square(g)
    m_hat = m / (1.0 - ADAM_B1 ** ADAM_STEP)
    v_hat = v / (1.0 - ADAM_B2 ** ADAM_STEP)
    delta = -ADAM_LR * (m_hat / (_jnp.sqrt(v_hat) + ADAM_EPS) + ADAM_WD * w)
    return delta, m, v


def reference(x, meta_tokens, norm_pre_w, w_in, conv_w, conv_b, dt_bias, a_log, d_skip, ssd_norm_w, attn_sinks, w_out, norm_post_w, loss_target, m_meta_tokens, m_norm_pre_w, m_w_in, m_conv_w, m_conv_b, m_dt_bias, m_a_log, m_d_skip, m_ssd_norm_w, m_attn_sinks, m_w_out, m_norm_post_w, v_meta_tokens, v_norm_pre_w, v_w_in, v_conv_w, v_conv_b, v_dt_bias, v_a_log, v_d_skip, v_ssd_norm_w, v_attn_sinks, v_w_out, v_norm_post_w):
    given = dict(x=x, meta_tokens=meta_tokens, norm_pre_w=norm_pre_w, w_in=w_in, conv_w=conv_w, conv_b=conv_b, dt_bias=dt_bias, a_log=a_log, d_skip=d_skip, ssd_norm_w=ssd_norm_w, attn_sinks=attn_sinks, w_out=w_out, norm_post_w=norm_post_w, loss_target=loss_target, m_meta_tokens=m_meta_tokens, m_norm_pre_w=m_norm_pre_w, m_w_in=m_w_in, m_conv_w=m_conv_w, m_conv_b=m_conv_b, m_dt_bias=m_dt_bias, m_a_log=m_a_log, m_d_skip=m_d_skip, m_ssd_norm_w=m_ssd_norm_w, m_attn_sinks=m_attn_sinks, m_w_out=m_w_out, m_norm_post_w=m_norm_post_w, v_meta_tokens=v_meta_tokens, v_norm_pre_w=v_norm_pre_w, v_w_in=v_w_in, v_conv_w=v_conv_w, v_conv_b=v_conv_b, v_dt_bias=v_dt_bias, v_a_log=v_a_log, v_d_skip=v_d_skip, v_ssd_norm_w=v_ssd_norm_w, v_attn_sinks=v_attn_sinks, v_w_out=v_w_out, v_norm_post_w=v_norm_post_w)
    weights = {n: given[n] for n in TWIN_WEIGHTS}
    shared = {n: given[n] for n in SHARED_INPUTS}
    per_example = {n: given[n] for n in ['x']}
    grad_fn = _jax.value_and_grad(_loss, argnums=(0, 1))

    def one_microbatch(ex, loss_target):
        ex = dict(ex)
        diff = ex.pop(TWIN_DIFF_INPUT)
        return grad_fn(weights, diff, {**shared, **ex}, loss_target)

    if N_MICROBATCH == 1:
        loss, (grad_w, grad_x) = one_microbatch(per_example, given["loss_target"])
    else:
        def body(carry, xs):
            loss_sum, grad_sum = carry
            l_k, (gw_k, gx_k) = one_microbatch(xs[0], xs[1])
            with _jax.named_scope("update"):
                return (loss_sum + l_k, _jax.tree.map(_jnp.add, grad_sum, gw_k)), gx_k

        init = (_jnp.zeros((), _jnp.float32), _jax.tree.map(_jnp.zeros_like, weights))
        (loss, grad_w), grad_x = _jax.lax.scan(body, init, (per_example, given["loss_target"]))
    with _jax.named_scope("update"):
        delta_w, new_m, new_v = {}, {}, {}
        for n in TWIN_WEIGHTS:
            delta_w[n], new_m[n], new_v[n] = _adamw(weights[n], grad_w[n], given["m_" + n], given["v_" + n])
    return (loss, grad_x, *[grad_w[n] for n in TWIN_WEIGHTS], *[delta_w[n] for n in TWIN_WEIGHTS],
            *[new_m[n] for n in TWIN_WEIGHTS], *[new_v[n] for n in TWIN_WEIGHTS])
```

```python
import functools

import jax
import jax.numpy as jnp
from jax import lax
from jax.experimental import pallas as pl
from jax.experimental.pallas import tpu as pltpu

F32 = jnp.float32
BF16 = jnp.bfloat16

D_MODEL = 2048
CHUNK = 64
N_META = 16
PAD_LEAD = CHUNK - N_META
ROW0 = PAD_LEAD + N_META
EPS = 1e-6
SSD_HEADS = 32
HEAD_DIM = 64
GROUPS = 8
HPG = SSD_HEADS // GROUPS
D_STATE = 128
D_SSD = 2048
GROUP_W = D_SSD // GROUPS
CONV_WIDTH = 4
D_CONV = 4096
Q_HEADS = 16
KV_HEADS = 4
REP = Q_HEADS // KV_HEADS
D_ATT = 1024
D_KV = 256
BAND_CHUNKS = 3
ROPE_THETA = 10000.0
D_MIX = D_SSD + D_ATT
D_IN = 8736
N_SHARD = 4
W_IN_SHARD = D_IN // N_SHARD
W_OUT_SHARD = D_MIX // N_SHARD

OZ, OXS, OB, OC, OQ, OG, OK, OV, ODT = 0, 2048, 4096, 5120, 6144, 7168, 8192, 8448, 8704
DT_SLAB = 512
N_RE = ODT + DT_SLAB
LANES = 128

ADAM_LR, ADAM_B1, ADAM_B2, ADAM_EPS, ADAM_WD, ADAM_STEP = 0.001, 0.9, 0.999, 1e-08, 0.01, 10

VMEM_LIMIT = 52 * 1024 * 1024
NEG = -1e30
HI = lax.Precision.HIGHEST


def _pallas(body, **kw):
    return pl.pallas_call(body, **kw)


def _cp(*sem):
    return pltpu.CompilerParams(dimension_semantics=sem, vmem_limit_bytes=VMEM_LIMIT)


def _tile(n, cap, mult=16):
    best = None
    for d in range(mult, min(n, cap) + 1, mult):
        if n % d == 0:
            best = d
    assert best is not None, (n, cap)
    return best


def _nt(a, b):
    return lax.dot_general(a, b, (((1,), (1,)), ((), ())), preferred_element_type=F32)


def _tn(a, b):
    return lax.dot_general(a, b, (((0,), (0,)), ((), ())), preferred_element_type=F32)


def _mm(a, b):
    return jnp.dot(a, b, preferred_element_type=F32)


def _sigmoid(x):
    return 1.0 / (1.0 + jnp.exp(-x))


def _bf(x):
    return x.astype(BF16)


def _inproj(hpad, norm_w, w_re):
    t, d = hpad.shape
    n = w_re.shape[1]
    tm, tn = _tile(t, 832), 512

    def body(h_ref, nw_ref, w_ref, proj_ref, hn_ref, hn_s):
        @pl.when(pl.program_id(1) == 0)
        def _():
            h = h_ref[...]
            ms = jnp.mean(h * h, axis=-1, keepdims=True)
            hn = _bf(h * lax.rsqrt(ms + EPS) * nw_ref[...])
            hn_s[...] = hn
            hn_ref[...] = hn
        proj_ref[...] = _mm(hn_s[...], w_ref[...])

    return _pallas(
        body, name="inproj", grid=(t // tm, n // tn),
        in_specs=[pl.BlockSpec((tm, d), lambda i, j: (i, 0)), pl.BlockSpec((1, d), lambda i, j: (0, 0)),
                  pl.BlockSpec((d, tn), lambda i, j: (0, j))],
        out_specs=[pl.BlockSpec((tm, tn), lambda i, j: (i, j)), pl.BlockSpec((tm, d), lambda i, j: (i, 0))],
        out_shape=[jax.ShapeDtypeStruct((t, n), F32), jax.ShapeDtypeStruct((t, d), BF16)],
        scratch_shapes=[pltpu.VMEM((tm, d), BF16)],
        compiler_params=_cp("parallel", "arbitrary"))(hpad, norm_w, w_re)


def _conv_fwd(proj, conv_w, conv_b):
    t = proj.shape[0]
    tc = 256
    off = OXS // tc

    def body(x_ref, w_ref, b_ref, o_ref, xp):
        xp[0:8, :] = jnp.zeros((8, tc), F32)
        xp[8:t + 8, :] = x_ref[...]
        w = w_ref[...]
        u = (b_ref[...] + w[3:4, :] * xp[8:t + 8, :] + w[2:3, :] * xp[7:t + 7, :]
             + w[1:2, :] * xp[6:t + 6, :] + w[0:1, :] * xp[5:t + 5, :])
        o_ref[...] = u * _sigmoid(u)

    return _pallas(
        body, name="conv_fwd", grid=(D_CONV // tc,),
        in_specs=[pl.BlockSpec((t, tc), lambda j: (0, j + off)), pl.BlockSpec((CONV_WIDTH, tc), lambda j: (0, j)),
                  pl.BlockSpec((1, tc), lambda j: (0, j))],
        out_specs=pl.BlockSpec((t, tc), lambda j: (0, j)),
        out_shape=jax.ShapeDtypeStruct((t, D_CONV), F32),
        scratch_shapes=[pltpu.VMEM((t + 8, tc), F32)],
        compiler_params=_cp("parallel"))(proj, conv_w, conv_b)


def _softplus(u):
    e = jnp.exp(-jnp.abs(u))
    w = 1.0 + e
    l1p = jnp.where(w == 1.0, e, jnp.log(w) * (e / jnp.where(w == 1.0, 1.0, w - 1.0)))
    return jnp.maximum(u, 0.0) + l1p


def _dt_prep(proj, dt_bias_l, a_log_l):
    t = proj.shape[0]
    nc = t // CHUNK
    q = CHUNK

    def body(raw_ref, bias_ref, alog_ref, dt_ref, acs_ref, acst_ref):
        c = pl.program_id(0)
        sp = _softplus(raw_ref[...] + bias_ref[...])
        row = c * q + lax.broadcasted_iota(jnp.int32, (q, LANES), 0)
        dt = jnp.where(row >= PAD_LEAD, sp, 0.0)
        da = dt * (-jnp.exp(alog_ref[...]))
        ri = lax.broadcasted_iota(jnp.int32, (q, q), 0)
        ci = lax.broadcasted_iota(jnp.int32, (q, q), 1)
        tri = (ri >= ci).astype(F32)
        acs = jnp.dot(tri, da, preferred_element_type=F32, precision=HI)
        dt_ref[...] = dt
        acs_ref[...] = acs
        acst_ref[0] = acs.T

    return _pallas(
        body, name="dt_prep", grid=(nc,),
        in_specs=[pl.BlockSpec((q, LANES), lambda c: (c, ODT // LANES)), pl.BlockSpec((1, LANES), lambda c: (0, 0)),
                  pl.BlockSpec((1, LANES), lambda c: (0, 0))],
        out_specs=[pl.BlockSpec((q, LANES), lambda c: (c, 0)), pl.BlockSpec((q, LANES), lambda c: (c, 0)),
                   pl.BlockSpec((1, LANES, q), lambda c: (c, 0, 0))],
        out_shape=[jax.ShapeDtypeStruct((t, LANES), F32), jax.ShapeDtypeStruct((t, LANES), F32),
                   jax.ShapeDtypeStruct((nc, LANES, q), F32)],
        compiler_params=_cp("parallel"))(proj, dt_bias_l, a_log_l)


def _head_cols(blk, idx):
    lane = lax.broadcasted_iota(jnp.int32, blk.shape, 1)
    return jnp.sum(jnp.where(lane == idx, blk, 0.0), axis=1, keepdims=True)


def _ssd_fwd(xbc, proj, dt, acs, acst, d_skip_l, ssd_norm_w):
    t = xbc.shape[0]
    q = CHUNK
    nc = t // q

    def body(xs_ref, b_ref, c_ref, dt_ref, acs_ref, acst_ref, z_ref, dsk_ref, nw_ref,
             y_ref, ymix_ref, st_ref, state):
        g = pl.program_id(0)

        @pl.when(pl.program_id(1) == 0)
        def _():
            state[...] = jnp.zeros_like(state)

        x = xs_ref[...]
        bm = b_ref[...]
        cm = c_ref[...]
        cb = _nt(_bf(cm), _bf(bm))
        ri = lax.broadcasted_iota(jnp.int32, (q, q), 0)
        ci = lax.broadcasted_iota(jnp.int32, (q, q), 1)
        causal = ri >= ci
        dtb = dt_ref[...]
        acsb = acs_ref[...]
        ys = []
        for r in range(HPG):
            idx = GROUPS * g + r
            dt_c = _head_cols(dtb, idx)
            acs_c = _head_cols(acsb, idx)
            acs_r = acst_ref[0, r:r + 1, :]
            acs_last = acs_r[:, q - 1:q]
            xh = x[:, HEAD_DIM * r:HEAD_DIM * (r + 1)]
            xdt = _bf(xh * dt_c)
            decay = jnp.exp(jnp.where(causal, acs_c - acs_r, NEG))
            m = _bf(cb * decay)
            s_prev = state[r]
            st_ref[0, r] = s_prev
            y_h = _mm(m, xdt) + _nt(_bf(cm), _bf(s_prev)) * jnp.exp(acs_c) + _head_cols(dsk_ref[...], idx) * xh
            bd = _bf(bm * jnp.exp(acs_last - acs_c))
            state[r] = jnp.exp(acs_last) * s_prev + _tn(xdt, bd)
            ys.append(y_h)
        y = jnp.concatenate(ys, axis=1)
        y_ref[...] = y
        z = z_ref[...]
        yg = y * (z * _sigmoid(z))
        ms = jnp.mean(yg * yg, axis=-1, keepdims=True)
        ymix_ref[...] = _bf(yg * lax.rsqrt(ms + EPS) * nw_ref[...])

    return _pallas(
        body, name="ssd_fwd", grid=(GROUPS, nc),
        in_specs=[pl.BlockSpec((q, GROUP_W), lambda g, c: (c, g)),
                  pl.BlockSpec((q, D_STATE), lambda g, c: (c, D_SSD // D_STATE + g)),
                  pl.BlockSpec((q, D_STATE), lambda g, c: (c, D_SSD // D_STATE + GROUPS + g)),
                  pl.BlockSpec((q, LANES), lambda g, c: (c, 0)), pl.BlockSpec((q, LANES), lambda g, c: (c, 0)),
                  pl.BlockSpec((1, GROUPS, q), lambda g, c: (c, g, 0)),
                  pl.BlockSpec((q, GROUP_W), lambda g, c: (c, g)),
                  pl.BlockSpec((1, LANES), lambda g, c: (0, 0)), pl.BlockSpec((1, GROUP_W), lambda g, c: (0, g))],
        out_specs=[pl.BlockSpec((q, GROUP_W), lambda g, c: (c, g)), pl.BlockSpec((q, GROUP_W), lambda g, c: (c, g)),
                   pl.BlockSpec((1, HPG, HEAD_DIM, D_STATE), lambda g, c: (c, g, 0, 0))],
        out_shape=[jax.ShapeDtypeStruct((t, D_SSD), F32), jax.ShapeDtypeStruct((t, D_SSD), BF16),
                   jax.ShapeDtypeStruct((nc, SSD_HEADS, HEAD_DIM, D_STATE), F32)],
        scratch_shapes=[pltpu.VMEM((HPG, HEAD_DIM, D_STATE), F32)],
        compiler_params=_cp("parallel", "arbitrary"))(xbc, xbc, xbc, dt, acs, acst, proj, d_skip_l, ssd_norm_w)


def _swap_halves(v):
    lane = lax.broadcasted_iota(jnp.int32, v.shape, 1)
    return jnp.where((lane & (HEAD_DIM - 1)) < HEAD_DIM // 2, pltpu.roll(v, LANES - HEAD_DIM // 2, 1),
                     pltpu.roll(v, HEAD_DIM // 2, 1))


def _rope(qsrc, q_off, ksrc, k_off, cos_t, sin_t):
    t = qsrc.shape[0]
    tr = _tile(t, 832)

    def body(q_ref, k_ref, cos_ref, sin_ref, qo_ref, ko_ref):
        cs = cos_ref[...]
        sn = sin_ref[...]
        for src, dst, width in ((q_ref, qo_ref, D_ATT), (k_ref, ko_ref, D_KV)):
            for s in range(width // LANES):
                v = src[:, LANES * s:LANES * (s + 1)].astype(F32)
                dst[:, LANES * s:LANES * (s + 1)] = _bf(v * cs + _swap_halves(v) * sn)

    return _pallas(
        body, name="rope", grid=(t // tr,),
        in_specs=[pl.BlockSpec((tr, D_ATT), lambda i: (i, q_off // D_ATT)),
                  pl.BlockSpec((tr, D_KV), lambda i: (i, k_off // D_KV)),
                  pl.BlockSpec((tr, LANES), lambda i: (i, 0)), pl.BlockSpec((tr, LANES), lambda i: (i, 0))],
        out_specs=[pl.BlockSpec((tr, D_ATT), lambda i: (i, 0)), pl.BlockSpec((tr, D_KV), lambda i: (i, 0))],
        out_shape=[jax.ShapeDtypeStruct((t, D_ATT), BF16), jax.ShapeDtypeStruct((t, D_KV), BF16)],
        compiler_params=_cp("parallel"))(qsrc, ksrc, cos_t, sin_t)


def _band_specs(width, col_block):
    return [pl.BlockSpec((CHUNK, width), functools.partial(lambda c, j: (jnp.maximum(c - j, 0), col_block), j=j))
            for j in (2, 1, 0)]


def _attn_probs(qh, kb, sink_col, valid):
    s = _nt(qh, kb) * (HEAD_DIM ** -0.5)
    s = jnp.where(valid, s, NEG)
    m = jnp.maximum(jnp.max(s, axis=1, keepdims=True), sink_col)
    p = jnp.exp(s - m)
    psink = jnp.exp(sink_col - m)
    inv = 1.0 / (jnp.sum(p, axis=1, keepdims=True) + psink)
    return p * inv, psink * inv


def _attn_operands(c, q_ref, k_refs, v_refs, sink_ref, h):
    q = q_ref[...]
    qh = jnp.concatenate([q[:, HEAD_DIM * (REP * h + r):HEAD_DIM * (REP * h + r + 1)] for r in range(REP)], axis=0)
    kb = jnp.concatenate([k[:, HEAD_DIM * h:HEAD_DIM * (h + 1)] for k in k_refs], axis=0)
    vb = jnp.concatenate([_bf(v[:, HEAD_DIM * h:HEAD_DIM * (h + 1)]) for v in v_refs], axis=0)
    rows = lax.broadcasted_iota(jnp.int32, (REP * CHUNK, 1), 0) >> 6
    sink_col = jnp.zeros((REP * CHUNK, 1), F32)
    for r in range(REP):
        sink_col = jnp.where(rows == r, sink_ref[REP * h + r], sink_col)
    key_abs = (c - (BAND_CHUNKS - 1)) * CHUNK + lax.broadcasted_iota(jnp.int32, (1, BAND_CHUNKS * CHUNK), 1)
    return qh, kb, vb, sink_col, key_abs >= PAD_LEAD


def _attn_fwd(qr, kr, proj, sinks):
    t = qr.shape[0]
    nc = t // CHUNK

    def body(q_ref, k2, k1, k0, v2, v1, v0, g_ref, sink_ref, o_ref):
        c = pl.program_id(0)
        ks = [k2[...], k1[...], k0[...]]
        vs = [v2[...], v1[...], v0[...]]
        outs = []
        for h in range(KV_HEADS):
            qh, kb, vb, sink_col, valid = _attn_operands(c, q_ref, ks, vs, sink_ref, h)
            p, _ = _attn_probs(qh, kb, sink_col, valid)
            o = _mm(_bf(p), vb)
            outs += [o[CHUNK * r:CHUNK * (r + 1)] for r in range(REP)]
        att = jnp.concatenate(outs, axis=1)
        gate = g_ref[...]
        o_ref[...] = _bf(att * (gate * _sigmoid(gate)))

    return _pallas(
        body, name="attn_fwd", grid=(nc,),
        in_specs=[pl.BlockSpec((CHUNK, D_ATT), lambda c: (c, 0))] + _band_specs(D_KV, 0)
        + _band_specs(D_KV, OV // D_KV) + [pl.BlockSpec((CHUNK, D_ATT), lambda c: (c, OG // D_ATT)),
                                           pl.BlockSpec(memory_space=pltpu.SMEM)],
        out_specs=pl.BlockSpec((CHUNK, D_ATT), lambda c: (c, 0)),
        out_shape=jax.ShapeDtypeStruct((t, D_ATT), BF16),
        compiler_params=_cp("parallel"))(qr, kr, kr, kr, proj, proj, proj, proj, sinks)


def _outproj(ymix, amix, w_out):
    t = ymix.shape[0]
    tm, tn = _tile(t, 832), 512

    def body(y_ref, a_ref, wy_ref, wa_ref, o_ref):
        o_ref[...] = _mm(y_ref[...], wy_ref[...]) + _mm(a_ref[...], wa_ref[...])

    return _pallas(
        body, name="outproj", grid=(t // tm, D_MODEL // tn),
        in_specs=[pl.BlockSpec((tm, D_SSD), lambda i, j: (i, 0)), pl.BlockSpec((tm, D_ATT), lambda i, j: (i, 0)),
                  pl.BlockSpec((D_SSD, tn), lambda i, j: (0, j)),
                  pl.BlockSpec((D_ATT, tn), lambda i, j: (D_SSD // D_ATT, j))],
        out_specs=pl.BlockSpec((tm, tn), lambda i, j: (i, j)),
        out_shape=jax.ShapeDtypeStruct((t, D_MODEL), F32),
        compiler_params=_cp("parallel", "parallel"))(ymix, amix, w_out, w_out)


def _post_loss(out, x, target, norm_post_w):
    t = out.shape[0]
    nc = t // CHUNK

    def body(o_ref, x_ref, tg_ref, nw_ref, dout_ref, dy_ref, loss_ref, gnw_ref):
        i = pl.program_id(0)

        @pl.when(i == 0)
        def _():
            dout_ref[...] = jnp.zeros_like(dout_ref)
            dy_ref[...] = jnp.zeros_like(dy_ref)
            loss_ref[...] = jnp.zeros_like(loss_ref)
            gnw_ref[...] = jnp.zeros_like(gnw_ref)

        @pl.when(i > 0)
        def _():
            o = o_ref[...]
            nw = nw_ref[...]
            rstd = lax.rsqrt(jnp.mean(o * o, axis=-1, keepdims=True) + EPS)
            n = o * rstd
            err = x_ref[...] + n * nw - tg_ref[...]
            loss_ref[...] += jnp.sum(err * err) * (0.5 / D_MODEL)
            dy = err * (1.0 / D_MODEL)
            dy_ref[...] = dy
            gnw_ref[...] += jnp.sum(dy * n, axis=0, keepdims=True)
            dn = dy * nw
            dout_ref[...] = _bf(rstd * (dn - n * jnp.mean(dn * n, axis=-1, keepdims=True)))

    prev = lambda i: (jnp.maximum(i - 1, 0), 0)
    return _pallas(
        body, name="post_loss", grid=(nc,),
        in_specs=[pl.BlockSpec((CHUNK, D_MODEL), lambda i: (i, 0)), pl.BlockSpec((CHUNK, D_MODEL), prev),
                  pl.BlockSpec((CHUNK, D_MODEL), prev), pl.BlockSpec((1, D_MODEL), lambda i: (0, 0))],
        out_specs=[pl.BlockSpec((CHUNK, D_MODEL), lambda i: (i, 0)), pl.BlockSpec((CHUNK, D_MODEL), lambda i: (i, 0)),
                   pl.BlockSpec((8, LANES), lambda i: (0, 0)), pl.BlockSpec((1, D_MODEL), lambda i: (0, 0))],
        out_shape=[jax.ShapeDtypeStruct((t, D_MODEL), BF16), jax.ShapeDtypeStruct((t, D_MODEL), F32),
                   jax.ShapeDtypeStruct((8, LANES), F32), jax.ShapeDtypeStruct((1, D_MODEL), F32)],
        compiler_params=_cp("arbitrary"))(out, x, target, norm_post_w)


def _nt_matmul(a, b, name):
    t, k = a.shape
    n = b.shape[0]
    tm, tn = _tile(t, 832), 512

    def body(a_ref, b_ref, o_ref):
        o_ref[...] = _nt(a_ref[...], b_ref[...])

    return _pallas(
        body, name=name, grid=(t // tm, n // tn),
        in_specs=[pl.BlockSpec((tm, k), lambda i, j: (i, 0)), pl.BlockSpec((tn, k), lambda i, j: (j, 0))],
        out_specs=pl.BlockSpec((tm, tn), lambda i, j: (i, j)),
        out_shape=jax.ShapeDtypeStruct((t, n), F32),
        compiler_params=_cp("parallel", "parallel"))(a, b)


def _tn_matmul(a, b, name):
    t, m = a.shape
    n = b.shape[1]
    tk, tm, tn = _tile(t, 832), min(m, 1024), min(n, 1024)
    nk = t // tk

    def body(a_ref, b_ref, o_ref):
        @pl.when(pl.program_id(2) == 0)
        def _():
            o_ref[...] = jnp.zeros_like(o_ref)
        o_ref[...] += _tn(a_ref[...], b_ref[...])

    return _pallas(
        body, name=name, grid=(m // tm, n // tn, nk),
        in_specs=[pl.BlockSpec((tk, tm), lambda i, j, k: (k, i)), pl.BlockSpec((tk, tn), lambda i, j, k: (k, j))],
        out_specs=pl.BlockSpec((tm, tn), lambda i, j, k: (i, j)),
        out_shape=jax.ShapeDtypeStruct((m, n), F32),
        compiler_params=_cp("parallel", "parallel", "arbitrary"))(a, b)


def _attn_bwd(qr, kr, proj, dmix, sinks):
    t = qr.shape[0]
    nc = t // CHUNK
    scale = HEAD_DIM ** -0.5

    def body(q_ref, k2, k1, k0, v2, v1, v0, g_ref, da_ref, sink_ref, dq_ref, dg_ref, dk_ref, dv_ref, gs_ref):
        c = pl.program_id(0)

        @pl.when(c == 0)
        def _():
            dk_ref[...] = jnp.zeros_like(dk_ref)
            dv_ref[...] = jnp.zeros_like(dv_ref)
            gs_ref[...] = jnp.zeros_like(gs_ref)

        ks = [k2[...], k1[...], k0[...]]
        vs = [v2[...], v1[...], v0[...]]
        gate = g_ref[...]
        sg = _sigmoid(gate)
        da = da_ref[...]
        datt = da * (gate * sg)
        lane = lax.broadcasted_iota(jnp.int32, (1, LANES), 1)
        rows = lax.broadcasted_iota(jnp.int32, (REP * CHUNK, 1), 0) >> 6
        dqs, atts, dks, dvs = [], [], [], []
        gs = jnp.zeros((1, LANES), F32)
        for h in range(KV_HEADS):
            qh, kb, vb, sink_col, valid = _attn_operands(c, q_ref, ks, vs, sink_ref, h)
            p, psink = _attn_probs(qh, kb, sink_col, valid)
            pb = _bf(p)
            o = _mm(pb, vb)
            do = jnp.concatenate([datt[:, HEAD_DIM * (REP * h + r):HEAD_DIM * (REP * h + r + 1)] for r in range(REP)],
                                 axis=0)
            dob = _bf(do)
            delta = jnp.sum(do * o, axis=1, keepdims=True)
            ds = _bf(p * (_nt(dob, vb) - delta) * scale)
            gsink = -psink * delta
            for r in range(REP):
                gs = gs + jnp.where(lane == REP * h + r, jnp.sum(jnp.where(rows == r, gsink, 0.0)), 0.0)
            dqh = _mm(ds, kb)
            dqs += [dqh[CHUNK * r:CHUNK * (r + 1)] for r in range(REP)]
            atts += [o[CHUNK * r:CHUNK * (r + 1)] for r in range(REP)]
            dks.append(_tn(ds, qh))
            dvs.append(_tn(pb, dob))
        dq_ref[...] = jnp.concatenate(dqs, axis=1)
        att = jnp.concatenate(atts, axis=1)
        dg_ref[...] = _bf(da * att * (sg * (1.0 + gate * (1.0 - sg))))
        gs_ref[0:1, :] += gs
        dkf = jnp.concatenate(dks, axis=1)
        dvf = jnp.concatenate(dvs, axis=1)
        for j in range(BAND_CHUNKS):
            r0 = pl.multiple_of(jnp.maximum(c - (BAND_CHUNKS - 1) + j, 0) * CHUNK, CHUNK)
            dk_ref[pl.ds(r0, CHUNK), :] += dkf[CHUNK * j:CHUNK * (j + 1)]
            dv_ref[pl.ds(r0, CHUNK), :] += dvf[CHUNK * j:CHUNK * (j + 1)]

    return _pallas(
        body, name="attn_bwd", grid=(nc,),
        in_specs=[pl.BlockSpec((CHUNK, D_ATT), lambda c: (c, 0))] + _band_specs(D_KV, 0)
        + _band_specs(D_KV, OV // D_KV) + [pl.BlockSpec((CHUNK, D_ATT), lambda c: (c, OG // D_ATT)),
                                           pl.BlockSpec((CHUNK, D_ATT), lambda c: (c, D_SSD // D_ATT)),
                                           pl.BlockSpec(memory_space=pltpu.SMEM)],
        out_specs=[pl.BlockSpec((CHUNK, D_ATT), lambda c: (c, 0)), pl.BlockSpec((CHUNK, D_ATT), lambda c: (c, 0)),
                   pl.BlockSpec((t, D_KV), lambda c: (0, 0)), pl.BlockSpec((t, D_KV), lambda c: (0, 0)),
                   pl.BlockSpec((8, LANES), lambda c: (0, 0))],
        out_shape=[jax.ShapeDtypeStruct((t, D_ATT), F32), jax.ShapeDtypeStruct((t, D_ATT), BF16),
                   jax.ShapeDtypeStruct((t, D_KV), F32), jax.ShapeDtypeStruct((t, D_KV), F32),
                   jax.ShapeDtypeStruct((8, LANES), F32)],
        compiler_params=_cp("arbitrary"))(qr, kr, kr, kr, proj, proj, proj, proj, dmix, sinks)


def _ssd_bwd(dmix, y_ssd, xbc, proj, dt, acs, acst, states, d_skip_l, ssd_norm_w):
    t = xbc.shape[0]
    q = CHUNK
    nc = t // q

    def body(dmix_ref, y_ref, z_ref, nw_ref, xs_ref, b_ref, c_ref, dt_ref, acs_ref, acst_ref, st_ref, dsk_ref,
             dz_ref, dxs_ref, db_ref, dc_ref, dacs_ref, ddt_ref, gnw_ref, gdsk_ref, dstate):
        g = pl.program_id(0)

        @pl.when(pl.program_id(1) == 0)
        def _():
            dstate[...] = jnp.zeros_like(dstate)
            gnw_ref[...] = jnp.zeros_like(gnw_ref)
            gdsk_ref[...] = jnp.zeros_like(gdsk_ref)

        y = y_ref[...]
        z = z_ref[...]
        sz = _sigmoid(z)
        silu_z = z * sz
        yg = y * silu_z
        rstd = lax.rsqrt(jnp.mean(yg * yg, axis=-1, keepdims=True) + EPS)
        n = yg * rstd
        dout = dmix_ref[...]
        gnw_ref[...] += jnp.sum(dout * n, axis=0, keepdims=True)
        dn = dout * nw_ref[...]
        dyg = rstd * (dn - n * jnp.mean(dn * n, axis=-1, keepdims=True))
        dy = dyg * silu_z
        dz_ref[...] = _bf(dyg * y * (sz * (1.0 + z * (1.0 - sz))))

        x = xs_ref[...]
        bm = b_ref[...]
        cm = c_ref[...]
        bmb, cmb = _bf(bm), _bf(cm)
        cb = _nt(cmb, bmb)
        cbt = _nt(bmb, cmb)
        ri = lax.broadcasted_iota(jnp.int32, (q, q), 0)
        ci = lax.broadcasted_iota(jnp.int32, (q, q), 1)
        lower = ri >= ci
        upper = ri <= ci
        last_row = lax.broadcasted_iota(jnp.int32, (q, 1), 0) == q - 1
        lane = lax.broadcasted_iota(jnp.int32, (q, LANES), 1)
        lane1 = lax.broadcasted_iota(jnp.int32, (8, LANES), 1)
        dtb = dt_ref[...]
        acsb = acs_ref[...]
        dcb = jnp.zeros((q, q), F32)
        dcbt = jnp.zeros((q, q), F32)
        db = jnp.zeros((q, D_STATE), F32)
        dc = jnp.zeros((q, D_STATE), F32)
        dacs_out = jnp.zeros((q, LANES), F32)
        ddt_out = jnp.zeros((q, LANES), F32)
        gdsk = jnp.zeros((8, LANES), F32)
        dxs = []
        for r in range(HPG):
            idx = GROUPS * g + r
            dt_c = _head_cols(dtb, idx)
            acs_c = _head_cols(acsb, idx)
            acs_r = acst_ref[0, r:r + 1, :]
            acs_last = acs_r[:, q - 1:q]
            xh = x[:, HEAD_DIM * r:HEAD_DIM * (r + 1)]
            dyh = dy[:, HEAD_DIM * r:HEAD_DIM * (r + 1)]
            xdt = _bf(xh * dt_c)
            dyb = _bf(dyh)
            dec = jnp.exp(jnp.where(lower, acs_c - acs_r, NEG))
            dect = jnp.exp(jnp.where(upper, acs_r - acs_c, NEG))
            m = cb * dec
            mt = cbt * dect
            s_prev = st_ref[0, r]
            spb = _bf(s_prev)
            ds_new = dstate[r]
            dsb = _bf(ds_new)
            e = jnp.exp(acs_c)
            elast = jnp.exp(acs_last)
            dte = jnp.exp(acs_last - acs_c)
            dxdt = _mm(_bf(mt), dyb) + _nt(_bf(bm * dte), dsb)
            dm = _nt(dyb, xdt)
            dmt = _nt(xdt, dyb)
            dcb = dcb + dm * dec
            dcbt = dcbt + dmt * dect
            dacs = jnp.sum(dm * m, axis=1, keepdims=True) - jnp.sum(dmt * mt, axis=1, keepdims=True)
            cs = _nt(cmb, spb)
            dye = _bf(dyh * e)
            dc = dc + _mm(dye, spb)
            dacs = dacs + jnp.sum(dyh * cs, axis=1, keepdims=True) * e
            dstate[r] = elast * ds_new + _tn(dye, cmb)
            gmat = _mm(xdt, dsb)
            db = db + dte * gmat
            ddte_dte = jnp.sum(bm * gmat, axis=1, keepdims=True) * dte
            dacs = dacs - ddte_dte
            dlast = jnp.sum(ddte_dte) + jnp.sum(s_prev * ds_new) * elast
            dacs = dacs + jnp.where(last_row, dlast, 0.0)
            dsk = _head_cols(dsk_ref[...], idx)
            dxs.append(dxdt * dt_c + dsk * dyh)
            ddt = jnp.sum(dxdt * xh, axis=1, keepdims=True)
            dacs_out = jnp.where(lane == r, dacs, dacs_out)
            ddt_out = jnp.where(lane == r, ddt, ddt_out)
            gdsk = gdsk + jnp.where(lane1 == r, jnp.sum(dyh * xh), 0.0)
        dc = dc + _mm(_bf(dcb), bmb)
        db = db + _mm(_bf(dcbt), cmb)
        dxs_ref[...] = jnp.concatenate(dxs, axis=1)
        db_ref[...] = db
        dc_ref[...] = dc
        dacs_ref[...] = dacs_out
        ddt_ref[...] = ddt_out
        gdsk_ref[0] += gdsk

    rev = lambda c: nc - 1 - c
    return _pallas(
        body, name="ssd_bwd", grid=(GROUPS, nc),
        in_specs=[pl.BlockSpec((q, GROUP_W), lambda g, c: (rev(c), g)), pl.BlockSpec((q, GROUP_W), lambda g, c: (rev(c), g)),
                  pl.BlockSpec((q, GROUP_W), lambda g, c: (rev(c), g)), pl.BlockSpec((1, GROUP_W), lambda g, c: (0, g)),
                  pl.BlockSpec((q, GROUP_W), lambda g, c: (rev(c), g)),
                  pl.BlockSpec((q, D_STATE), lambda g, c: (rev(c), D_SSD // D_STATE + g)),
                  pl.BlockSpec((q, D_STATE), lambda g, c: (rev(c), D_SSD // D_STATE + GROUPS + g)),
                  pl.BlockSpec((q, LANES), lambda g, c: (rev(c), 0)), pl.BlockSpec((q, LANES), lambda g, c: (rev(c), 0)),
                  pl.BlockSpec((1, GROUPS, q), lambda g, c: (rev(c), g, 0)),
                  pl.BlockSpec((1, HPG, HEAD_DIM, D_STATE), lambda g, c: (rev(c), g, 0, 0)),
                  pl.BlockSpec((1, LANES), lambda g, c: (0, 0))],
        out_specs=[pl.BlockSpec((q, GROUP_W), lambda g, c: (rev(c), g)), pl.BlockSpec((q, GROUP_W), lambda g, c: (rev(c), g)),
                   pl.BlockSpec((q, D_STATE), lambda g, c: (rev(c), g)), pl.BlockSpec((q, D_STATE), lambda g, c: (rev(c), g)),
                   pl.BlockSpec((q, LANES), lambda g, c: (rev(c), g)), pl.BlockSpec((q, LANES), lambda g, c: (rev(c), g)),
                   pl.BlockSpec((1, GROUP_W), lambda g, c: (0, g)), pl.BlockSpec((1, 8, LANES), lambda g, c: (g, 0, 0))],
        out_shape=[jax.ShapeDtypeStruct((t, D_SSD), BF16), jax.ShapeDtypeStruct((t, D_SSD), F32),
                   jax.ShapeDtypeStruct((t, GROUPS * D_STATE), F32), jax.ShapeDtypeStruct((t, GROUPS * D_STATE), F32),
                   jax.ShapeDtypeStruct((t, GROUPS * LANES), F32), jax.ShapeDtypeStruct((t, GROUPS * LANES), F32),
                   jax.ShapeDtypeStruct((1, D_SSD), F32), jax.ShapeDtypeStruct((GROUPS, 8, LANES), F32)],
        scratch_shapes=[pltpu.VMEM((HPG, HEAD_DIM, D_STATE), F32)],
        compiler_params=_cp("parallel", "arbitrary"))(dmix, y_ssd, proj, ssd_norm_w, xbc, xbc, xbc, dt, acs, acst,
                                                      states, d_skip_l)


def _dt_bwd(dacs_g, ddt_g, dt, proj, dt_bias_l, a_log_l):
    t = dt.shape[0]
    q = CHUNK
    nc = t // q

    def body(dacs_ref, ddt_ref, dt_ref, raw_ref, bias_ref, alog_ref, draw_ref, ga_ref, gb_ref):
        c = pl.program_id(0)

        @pl.when(c == 0)
        def _():
            ga_ref[...] = jnp.zeros_like(ga_ref)
            gb_ref[...] = jnp.zeros_like(gb_ref)

        lane = lax.broadcasted_iota(jnp.int32, (q, LANES), 1)
        dacs = jnp.zeros((q, LANES), F32)
        ddt = jnp.zeros((q, LANES), F32)
        for g in range(GROUPS):
            mask = (lane >= GROUPS * g) & (lane < GROUPS * g + HPG)
            sl = slice(LANES * g, LANES * (g + 1))
            if g == 0:
                dacs = jnp.where(mask, dacs_ref[:, sl], dacs)
                ddt = jnp.where(mask, ddt_ref[:, sl], ddt)
            else:
                dacs = jnp.where(mask, pltpu.roll(dacs_ref[:, sl], GROUPS * g, 1), dacs)
                ddt = jnp.where(mask, pltpu.roll(ddt_ref[:, sl], GROUPS * g, 1), ddt)
        ri = lax.broadcasted_iota(jnp.int32, (q, q), 0)
        ci = lax.broadcasted_iota(jnp.int32, (q, q), 1)
        triu = (ri <= ci).astype(F32)
        dda = jnp.dot(triu, dacs, preferred_element_type=F32, precision=HI)
        a = -jnp.exp(alog_ref[...])
        dtv = dt_ref[...]
        row = c * q + lax.broadcasted_iota(jnp.int32, (q, LANES), 0)
        used = (lane & (GROUPS - 1)) < HPG
        dsp = jnp.where((row >= PAD_LEAD) & used, dda * a + ddt, 0.0)
        draw = dsp * _sigmoid(raw_ref[...] + bias_ref[...])
        draw_ref[...] = _bf(draw)
        gb_ref[0:1, :] += jnp.sum(draw, axis=0, keepdims=True)
        ga_ref[0:1, :] += jnp.sum(jnp.where(used, dda * dtv, 0.0), axis=0, keepdims=True) * a

    return _pallas(
        body, name="dt_bwd", grid=(nc,),
        in_specs=[pl.BlockSpec((q, GROUPS * LANES), lambda c: (c, 0)), pl.BlockSpec((q, GROUPS * LANES), lambda c: (c, 0)),
                  pl.BlockSpec((q, LANES), lambda c: (c, 0)), pl.BlockSpec((q, LANES), lambda c: (c, ODT // LANES)),
                  pl.BlockSpec((1, LANES), lambda c: (0, 0)), pl.BlockSpec((1, LANES), lambda c: (0, 0))],
        out_specs=[pl.BlockSpec((q, LANES), lambda c: (c, 0)), pl.BlockSpec((8, LANES), lambda c: (0, 0)),
                   pl.BlockSpec((8, LANES), lambda c: (0, 0))],
        out_shape=[jax.ShapeDtypeStruct((t, LANES), BF16), jax.ShapeDtypeStruct((8, LANES), F32),
                   jax.ShapeDtypeStruct((8, LANES), F32)],
        compiler_params=_cp("arbitrary"))(dacs_g, ddt_g, dt, proj, dt_bias_l, a_log_l)


def _conv_bwd(dseg, proj, conv_w, conv_b, col_off, name):
    t, width = dseg.shape
    tc = 128
    off_p = (OXS + col_off) // tc
    off_w = col_off // tc

    def body(d_ref, x_ref, w_ref, b_ref, dx_ref, gw_ref, gb_ref, xp, dup):
        xp[0:8, :] = jnp.zeros((8, tc), F32)
        xp[8:t + 8, :] = x_ref[...]
        w = w_ref[...]
        u = (b_ref[...] + w[3:4, :] * xp[8:t + 8, :] + w[2:3, :] * xp[7:t + 7, :]
             + w[1:2, :] * xp[6:t + 6, :] + w[0:1, :] * xp[5:t + 5, :])
        su = _sigmoid(u)
        du = d_ref[...] * (su * (1.0 + u * (1.0 - su)))
        dup[0:t, :] = du
        dup[t:t + 8, :] = jnp.zeros((8, tc), F32)
        dx_ref[...] = _bf(w[3:4, :] * du + w[2:3, :] * dup[1:t + 1, :] + w[1:2, :] * dup[2:t + 2, :]
                          + w[0:1, :] * dup[3:t + 3, :])
        gb_ref[...] = jnp.sum(du, axis=0, keepdims=True)
        gw_ref[...] = jnp.concatenate(
            [jnp.sum(du * xp[5 + k:t + 5 + k, :], axis=0, keepdims=True) for k in range(CONV_WIDTH)], axis=0)

    return _pallas(
        body, name=name, grid=(width // tc,),
        in_specs=[pl.BlockSpec((t, tc), lambda j: (0, j)), pl.BlockSpec((t, tc), lambda j: (0, j + off_p)),
                  pl.BlockSpec((CONV_WIDTH, tc), lambda j: (0, j + off_w)), pl.BlockSpec((1, tc), lambda j: (0, j + off_w))],
        out_specs=[pl.BlockSpec((t, tc), lambda j: (0, j)), pl.BlockSpec((CONV_WIDTH, tc), lambda j: (0, j)),
                   pl.BlockSpec((1, tc), lambda j: (0, j))],
        out_shape=[jax.ShapeDtypeStruct((t, width), BF16), jax.ShapeDtypeStruct((CONV_WIDTH, width), F32),
                   jax.ShapeDtypeStruct((1, width), F32)],
        scratch_shapes=[pltpu.VMEM((t + 8, tc), F32), pltpu.VMEM((t + 8, tc), F32)],
        compiler_params=_cp("parallel"))(dseg, proj, conv_w, conv_b)


def _dinproj(dproj, w_re, hpad, norm_w, dy_t):
    t, n = dproj.shape
    d = hpad.shape[1]
    tm, tk = _tile(t, 416), 1024
    nk = n // tk

    def body(dp_ref, w_ref, h_ref, nw_ref, dy_ref, dh_ref, gnw_ref, acc):
        i, k = pl.program_id(0), pl.program_id(1)

        @pl.when((i == 0) & (k == 0))
        def _():
            gnw_ref[...] = jnp.zeros_like(gnw_ref)

        @pl.when(k == 0)
        def _():
            acc[...] = jnp.zeros_like(acc)

        acc[...] += _nt(dp_ref[...], w_ref[...])

        @pl.when(k == nk - 1)
        def _():
            h = h_ref[...]
            rstd = lax.rsqrt(jnp.mean(h * h, axis=-1, keepdims=True) + EPS)
            nrm = h * rstd
            dhn = acc[...]
            gnw_ref[...] += jnp.sum(dhn * nrm, axis=0, keepdims=True)
            dn = dhn * nw_ref[...]
            dh_ref[...] = rstd * (dn - nrm * jnp.mean(dn * nrm, axis=-1, keepdims=True)) + dy_ref[...]

    return _pallas(
        body, name="dinproj", grid=(t // tm, nk),
        in_specs=[pl.BlockSpec((tm, tk), lambda i, k: (i, k)), pl.BlockSpec((d, tk), lambda i, k: (0, k)),
                  pl.BlockSpec((tm, d), lambda i, k: (i, 0)), pl.BlockSpec((1, d), lambda i, k: (0, 0)),
                  pl.BlockSpec((tm, d), lambda i, k: (i, 0))],
        out_specs=[pl.BlockSpec((tm, d), lambda i, k: (i, 0)), pl.BlockSpec((1, d), lambda i, k: (0, 0))],
        out_shape=[jax.ShapeDtypeStruct((t, d), F32), jax.ShapeDtypeStruct((1, d), F32)],
        scratch_shapes=[pltpu.VMEM((tm, d), F32)],
        compiler_params=_cp("arbitrary", "arbitrary"))(dproj, w_re, hpad, norm_w, dy_t)


def _spread_heads(v):
    v = jnp.pad(v.reshape(GROUPS, HPG), ((0, 0), (0, GROUPS - HPG))).reshape(1, GROUPS * GROUPS)
    return jnp.pad(v, ((0, 0), (0, LANES - GROUPS * GROUPS)))


def _gather_heads(v):
    return v[0:1, :GROUPS * GROUPS].reshape(GROUPS, GROUPS)[:, :HPG].reshape(1, SSD_HEADS)


def _rope_tables(t):
    half = HEAD_DIM // 2
    inv = ROPE_THETA ** (-jnp.arange(half, dtype=F32) / half)
    pos = (jnp.arange(t) - PAD_LEAD).astype(F32)
    ang = pos[:, None] * inv[None, :]
    cos, sin = jnp.cos(ang), jnp.sin(ang)
    cos_t = jnp.concatenate([cos, cos, cos, cos], axis=1)
    sin_t = jnp.concatenate([-sin, sin, -sin, sin], axis=1)
    return cos_t, sin_t


def _w_in_to_re(w):
    k = w.shape[0]
    dtc = w[:, 6144:6176].reshape(k, GROUPS, HPG)
    dtc = jnp.pad(dtc, ((0, 0), (0, 0), (0, GROUPS - HPG))).reshape(k, GROUPS * GROUPS)
    return jnp.concatenate([w[:, :6144], w[:, 6176:7200], w[:, 7712:8736], w[:, 7200:7456], w[:, 7456:7712], dtc,
                            jnp.zeros((k, DT_SLAB - GROUPS * GROUPS), w.dtype)], axis=1)


def _re_to_w_in(g):
    k = g.shape[0]
    dtc = g[:, ODT:ODT + GROUPS * GROUPS].reshape(k, GROUPS, GROUPS)[:, :, :HPG].reshape(k, SSD_HEADS)
    return jnp.concatenate([g[:, :6144], dtc, g[:, OQ:OQ + D_ATT], g[:, OK:OK + D_KV], g[:, OV:OV + D_KV],
                            g[:, OG:OG + D_ATT]], axis=1)


def _local_step(x, target, meta, norm_pre_w, w_re, conv_w, conv_b, dt_bias, a_log, d_skip, ssd_norm_w, sinks,
                w_out, norm_post_w):
    seq = x.shape[0]
    t = PAD_LEAD + N_META + seq
    hpad = jnp.concatenate([jnp.zeros((PAD_LEAD, D_MODEL), F32), meta, x], axis=0)
    dt_bias_l, a_log_l, d_skip_l = _spread_heads(dt_bias), _spread_heads(a_log), _spread_heads(d_skip)
    cos_t, sin_t = _rope_tables(t)
    sink_v = sinks.reshape(Q_HEADS)

    proj, hn = _inproj(hpad, norm_pre_w, w_re)
    xbc = _conv_fwd(proj, conv_w, conv_b)
    dt, acs, acst = _dt_prep(proj, dt_bias_l, a_log_l)
    y_ssd, ymix, states = _ssd_fwd(xbc, proj, dt, acs, acst, d_skip_l, ssd_norm_w)
    qr, kr = _rope(proj, OQ, proj, OK, cos_t, sin_t)
    amix = _attn_fwd(qr, kr, proj, sink_v)
    out = _outproj(ymix, amix, w_out)
    dout, dy_t, loss_blk, g_norm_post = _post_loss(out, x, target, norm_post_w)

    dmix = _nt_matmul(dout, w_out, "dmix")
    g_w_out = jnp.concatenate([_tn_matmul(ymix, dout, "gw_out_y"), _tn_matmul(amix, dout, "gw_out_a")], axis=0)
    dq_r, dg, dk_r, dv, gs = _attn_bwd(qr, kr, proj, dmix, sink_v)
    dq, dk = _rope(dq_r, 0, dk_r, 0, cos_t, -sin_t)
    dz, dxs, db, dc, dacs_g, ddt_g, g_ssd_norm, gdsk = _ssd_bwd(dmix, y_ssd, xbc, proj, dt, acs, acst, states,
                                                                d_skip_l, ssd_norm_w)
    draw, ga, gb = _dt_bwd(dacs_g, ddt_g, dt, proj, dt_bias_l, a_log_l)
    dxs_p, gcw0, gcb0 = _conv_bwd(dxs, proj, conv_w, conv_b, 0, "conv_bwd_x")
    db_p, gcw1, gcb1 = _conv_bwd(db, proj, conv_w, conv_b, D_SSD, "conv_bwd_b")
    dc_p, gcw2, gcb2 = _conv_bwd(dc, proj, conv_w, conv_b, D_SSD + GROUPS * D_STATE, "conv_bwd_c")
    dproj = jnp.concatenate([dz, dxs_p, db_p, dc_p, dq, dg, dk, _bf(dv), draw,
                             jnp.zeros((t, DT_SLAB - LANES), BF16)], axis=1)
    dh, g_norm_pre = _dinproj(dproj, w_re, hpad, norm_pre_w, dy_t)
    g_w_re = _tn_matmul(hn, dproj, "gw_in")

    gdsk_l = jnp.concatenate([gdsk[g, 0:1, 0:GROUPS] for g in range(GROUPS)], axis=1)
    gdsk_l = jnp.pad(gdsk_l, ((0, 0), (0, LANES - GROUPS * GROUPS)))
    grads = dict(
        meta_tokens=dh[PAD_LEAD:ROW0], norm_pre_w=g_norm_pre, w_re=g_w_re,
        conv_w=jnp.concatenate([gcw0, gcw1, gcw2], axis=1), conv_b=jnp.concatenate([gcb0, gcb1, gcb2], axis=1),
        dt_bias=_gather_heads(gb), a_log=_gather_heads(ga), d_skip=_gather_heads(gdsk_l), ssd_norm_w=g_ssd_norm,
        attn_sinks=gs[0:1, :Q_HEADS], w_out=g_w_out, norm_post_w=g_norm_post)
    return loss_blk[0, 0], dh[ROW0:], grads


ANY = pl.BlockSpec(memory_space=pl.ANY)
MESH = pl.DeviceIdType.MESH


def _rcopy(src, dst, ssem, rsem, dev):
    return pltpu.make_async_remote_copy(src_ref=src, dst_ref=dst, send_sem=ssem, recv_sem=rsem, device_id=dev,
                                        device_id_type=MESH)


def _place():
    x, y, c = lax.axis_index("x"), lax.axis_index("y"), lax.axis_index("c")
    chips = [(1 - x, y), (x, 1 - y), (1 - x, 1 - y)]
    return x, y, c, chips


def _gather_shards(shard, name):
    r, n = shard.shape
    hr = r // 2

    def body(x_ref, out_ref, send_sems, recv_sems, local_sem):
        x, y, c, chips = _place()
        me = 2 * x + y

        def half(chip, hc):
            return out_ref.at[chip, pl.ds(hc * hr, hr), :]

        mine = pltpu.make_async_copy(x_ref, out_ref.at[me], local_sem)
        mine.start()
        first = [_rcopy(x_ref.at[pl.ds(c * hr, hr), :], half(me, c), send_sems.at[j], recv_sems.at[j], (*chip, c))
                 for j, chip in enumerate(chips)]
        for cp in first:
            cp.start()
        passed = []
        for j, chip in enumerate(chips):
            cj = 2 * chip[0] + chip[1]
            _rcopy(half(cj, c), half(cj, c), send_sems.at[j], recv_sems.at[j], (*chip, c)).wait_recv()
            fw = _rcopy(half(cj, c), half(cj, c), send_sems.at[3 + j], recv_sems.at[3 + j], (x, y, 1 - c))
            fw.start()
            passed.append(fw)
        for j, chip in enumerate(chips):
            cj = 2 * chip[0] + chip[1]
            _rcopy(half(cj, 1 - c), half(cj, 1 - c), send_sems.at[3 + j], recv_sems.at[3 + j], (x, y, 1 - c)).wait_recv()
        for cp in first + passed:
            cp.wait_send()
        mine.wait()

    return _pallas(
        body, name=name, in_specs=[ANY], out_specs=ANY,
        out_shape=jax.ShapeDtypeStruct((N_SHARD, r, n), shard.dtype),
        scratch_shapes=[pltpu.SemaphoreType.DMA((6,)), pltpu.SemaphoreType.DMA((6,)), pltpu.SemaphoreType.DMA])(shard)


def _pair_split(g4, name):
    _, r, n = g4.shape
    hr = r // 2

    def body(g_ref, own_ref, got_ref, send_sem, recv_sem, local_sem):
        x, y, c, _ = _place()
        mine = pltpu.make_async_copy(g_ref.at[:, pl.ds(c * hr, hr), :], own_ref, local_sem)
        mine.start()
        cp = _rcopy(g_ref.at[:, pl.ds((1 - c) * hr, hr), :], got_ref, send_sem, recv_sem, (x, y, 1 - c))
        cp.start()
        cp.wait()
        mine.wait()

    return _pallas(
        body, name=name, in_specs=[ANY], out_specs=[ANY, ANY],
        out_shape=[jax.ShapeDtypeStruct((N_SHARD, hr, n), F32), jax.ShapeDtypeStruct((N_SHARD, hr, n), F32)],
        scratch_shapes=[pltpu.SemaphoreType.DMA, pltpu.SemaphoreType.DMA, pltpu.SemaphoreType.DMA])(g4)


def _add_to_bf16(a, b, name):
    k, r, n = a.shape
    tr = _tile(r, 256)

    def body(a_ref, b_ref, o_ref):
        o_ref[...] = _bf(a_ref[...] + b_ref[...])

    spec = pl.BlockSpec((1, tr, n), lambda j, i: (j, i, 0))
    return _pallas(body, name=name, grid=(k, r // tr), in_specs=[spec, spec], out_specs=spec,
                   out_shape=jax.ShapeDtypeStruct((k, r, n), BF16), compiler_params=_cp("parallel", "parallel"))(a, b)


def _chip_exchange(ga, name):
    _, hr, n = ga.shape

    def body(g_ref, got_ref, send_sems, recv_sems, local_sem):
        x, y, c, chips = _place()
        me = 2 * x + y
        mine = pltpu.make_async_copy(g_ref.at[me], got_ref.at[me], local_sem)
        mine.start()
        cps = [_rcopy(g_ref.at[2 * chip[0] + chip[1]], got_ref.at[me], send_sems.at[j], recv_sems.at[j], (*chip, c))
               for j, chip in enumerate(chips)]
        for cp in cps:
            cp.start()
        for j, chip in enumerate(chips):
            cj = 2 * chip[0] + chip[1]
            _rcopy(g_ref.at[cj], got_ref.at[cj], send_sems.at[j], recv_sems.at[j], (*chip, c)).wait_recv()
        for cp in cps:
            cp.wait_send()
        mine.wait()

    return _pallas(
        body, name=name, in_specs=[ANY], out_specs=ANY, out_shape=jax.ShapeDtypeStruct(ga.shape, ga.dtype),
        scratch_shapes=[pltpu.SemaphoreType.DMA((3,)), pltpu.SemaphoreType.DMA((3,)), pltpu.SemaphoreType.DMA])(ga)


def _sum_slabs(got, name):
    k, r, n = got.shape
    tr = _tile(r, 256)

    def body(g_ref, o_ref):
        acc = g_ref[0].astype(F32)
        for j in range(1, k):
            acc = acc + g_ref[j].astype(F32)
        o_ref[...] = acc

    return _pallas(body, name=name, grid=(r // tr,), in_specs=[pl.BlockSpec((k, tr, n), lambda i: (0, i, 0))],
                   out_specs=pl.BlockSpec((tr, n), lambda i: (i, 0)), out_shape=jax.ShapeDtypeStruct((r, n), F32),
                   compiler_params=_cp("parallel"))(got)


def _pair_join(half, name):
    hr, n = half.shape

    def body(h_ref, out_ref, send_sem, recv_sem, local_sem):
        x, y, c, _ = _place()
        mine = pltpu.make_async_copy(h_ref, out_ref.at[pl.ds(c * hr, hr), :], local_sem)
        mine.start()
        cp = _rcopy(h_ref, out_ref.at[pl.ds(c * hr, hr), :], send_sem, recv_sem, (x, y, 1 - c))
        cp.start()
        _rcopy(h_ref, out_ref.at[pl.ds((1 - c) * hr, hr), :], send_sem, recv_sem, (x, y, 1 - c)).wait_recv()
        cp.wait_send()
        mine.wait()

    return _pallas(
        body, name=name, in_specs=[ANY], out_specs=ANY, out_shape=jax.ShapeDtypeStruct((2 * hr, n), F32),
        scratch_shapes=[pltpu.SemaphoreType.DMA, pltpu.SemaphoreType.DMA, pltpu.SemaphoreType.DMA])(half)


def _reduce_to_shard(g4, tag):
    own, got = _pair_split(g4, tag + "_pair_split")
    ga = _add_to_bf16(own, got, tag + "_pair_add")
    slabs = _chip_exchange(ga, tag + "_chip_exchange")
    half = _sum_slabs(slabs, tag + "_chip_sum")
    return _pair_join(half, tag + "_pair_join")


def _allreduce_small(p, name):
    rows, n = p.shape
    ndev = 8

    def body(p_ref, out_ref, slots, send_sems, recv_sems):
        x, y, c, _ = _place()
        my = 4 * x + 2 * y + c
        slots[my] = p_ref[...]
        cps = []
        for k in range(1, ndev):
            kx, ky, kc = (k >> 2) & 1, (k >> 1) & 1, k & 1
            peer = (x ^ kx, y ^ ky, c ^ kc)
            cp = _rcopy(p_ref, slots.at[my], send_sems.at[k - 1], recv_sems.at[k - 1], peer)
            cp.start()
            cps.append(cp)
        for k in range(1, ndev):
            _rcopy(p_ref, slots.at[my ^ k], send_sems.at[k - 1], recv_sems.at[k - 1], (x, y, c)).wait_recv()
        for cp in cps:
            cp.wait_send()
        acc = slots[0]
        for j in range(1, ndev):
            acc = acc + slots[j]
        out_ref[...] = acc

    vm = pl.BlockSpec(memory_space=pltpu.VMEM)
    return _pallas(
        body, name=name, in_specs=[vm], out_specs=vm, out_shape=jax.ShapeDtypeStruct((rows, n), F32),
        scratch_shapes=[pltpu.VMEM((ndev, rows, n), F32), pltpu.SemaphoreType.DMA((ndev - 1,)),
                        pltpu.SemaphoreType.DMA((ndev - 1,))])(p)


def _adamw(w, g, m, v, name):
    r, n = w.shape
    tr = _tile(r, 256, 8)
    c1 = 1.0 / (1.0 - ADAM_B1 ** ADAM_STEP)
    c2 = 1.0 / (1.0 - ADAM_B2 ** ADAM_STEP)

    def body(w_ref, g_ref, m_ref, v_ref, d_ref, mo_ref, vo_ref):
        gv = g_ref[...]
        mn = ADAM_B1 * m_ref[...] + (1.0 - ADAM_B1) * gv
        vn = ADAM_B2 * v_ref[...] + (1.0 - ADAM_B2) * (gv * gv)
        d_ref[...] = -ADAM_LR * ((mn * c1) / (jnp.sqrt(vn * c2) + ADAM_EPS) + ADAM_WD * w_ref[...])
        mo_ref[...] = mn
        vo_ref[...] = vn

    spec = pl.BlockSpec((tr, n), lambda i: (i, 0))
    shp = jax.ShapeDtypeStruct((r, n), F32)
    return _pallas(body, name=name, grid=(r // tr,), in_specs=[spec] * 4, out_specs=[spec] * 3, out_shape=[shp] * 3,
                   compiler_params=_cp("parallel"))(w, g, m, v)


PACK_W = 1024
SMALL_REPL = ("norm_pre_w", "conv_b", "ssd_norm_w", "norm_post_w")
SMALL_HEAD = ("dt_bias", "a_log", "d_skip", "attn_sinks")


def _rows(a):
    return a.reshape(-1, PACK_W)


def _head_row(vals, extra=None):
    parts = [vals[n].reshape(1, -1) for n in SMALL_HEAD]
    if extra is not None:
        parts.append(extra.reshape(1, 1))
    row = jnp.concatenate(parts, axis=1)
    return jnp.pad(row, ((0, 0), (0, PACK_W - row.shape[1])))


def _pad_rows(a, rows):
    return jnp.pad(a, ((0, rows - a.shape[0]), (0, 0)))


def _pack_repl(vals, extra=None):
    body = jnp.concatenate([_rows(vals[n]) for n in SMALL_REPL] + [_head_row(vals, extra)], axis=0)
    return _pad_rows(body, 16)


def _unpack_repl(buf):
    out, r = {}, 0
    for n, k in zip(SMALL_REPL, (2, 4, 2, 2)):
        out[n] = buf[r:r + k].reshape(1, k * PACK_W)
        r += k
    col = 0
    for n, k in zip(SMALL_HEAD, (32, 32, 32, 16)):
        out[n] = buf[r:r + 1, col:col + k]
        col += k
    return out, buf[r, col]


def kernel(x, meta_tokens, norm_pre_w, w_in, conv_w, conv_b, dt_bias, a_log, d_skip, ssd_norm_w, attn_sinks, w_out, norm_post_w, loss_target, m_meta_tokens, m_norm_pre_w, m_w_in, m_conv_w, m_conv_b, m_dt_bias, m_a_log, m_d_skip, m_ssd_norm_w, m_attn_sinks, m_w_out, m_norm_post_w, v_meta_tokens, v_norm_pre_w, v_w_in, v_conv_w, v_conv_b, v_dt_bias, v_a_log, v_d_skip, v_ssd_norm_w, v_attn_sinks, v_w_out, v_norm_post_w):
    names = ("meta_tokens", "norm_pre_w", "w_in", "conv_w", "conv_b", "dt_bias", "a_log", "d_skip", "ssd_norm_w",
             "attn_sinks", "w_out", "norm_post_w")
    w = dict(zip(names, (meta_tokens, norm_pre_w, w_in, conv_w, conv_b, dt_bias, a_log, d_skip, ssd_norm_w, attn_sinks,
                         w_out, norm_post_w)))
    m = dict(zip(names, (m_meta_tokens, m_norm_pre_w, m_w_in, m_conv_w, m_conv_b, m_dt_bias, m_a_log, m_d_skip,
                         m_ssd_norm_w, m_attn_sinks, m_w_out, m_norm_post_w)))
    v = dict(zip(names, (v_meta_tokens, v_norm_pre_w, v_w_in, v_conv_w, v_conv_b, v_dt_bias, v_a_log, v_d_skip,
                         v_ssd_norm_w, v_attn_sinks, v_w_out, v_norm_post_w)))
    cx, cy, cc = lax.axis_index("x"), lax.axis_index("y"), lax.axis_index("c")
    chip = 2 * cx + cy
    meta_cols = D_MODEL // N_SHARD
    conv_cols = D_CONV // N_SHARD

    w_in_all = _gather_shards(_bf(w_in[0]), "gather_w_in")
    w_re = _w_in_to_re(jnp.transpose(w_in_all, (1, 0, 2)).reshape(D_MODEL, D_IN))
    w_out_full = _gather_shards(_bf(w_out[0]), "gather_w_out").reshape(D_MIX, D_MODEL)
    conv_z = lax.dynamic_update_slice(jnp.zeros((CONV_WIDTH, D_CONV), F32), conv_w[0], (0, chip * conv_cols))
    meta_z = lax.dynamic_update_slice(jnp.zeros((N_META, D_MODEL), F32), meta_tokens, (0, chip * meta_cols))
    small = jnp.concatenate([_rows(conv_z), _rows(meta_z)], axis=0)
    small = _allreduce_small(jnp.where(cc == 0, small, 0.0), "gather_small")
    conv_full = small[0:16].reshape(CONV_WIDTH, D_CONV)
    meta_full = small[16:48].reshape(N_META, D_MODEL)

    loss_dev, grad_x, g = _local_step(x[0], loss_target[0], meta_full, norm_pre_w, w_re, conv_full, conv_b, dt_bias,
                                      a_log, d_skip, ssd_norm_w, attn_sinks, w_out_full, norm_post_w)

    g_in4 = jnp.transpose(_re_to_w_in(g["w_re"]).reshape(D_MODEL, N_SHARD, W_IN_SHARD), (1, 0, 2))
    g_w_in = _reduce_to_shard(g_in4, "gw_in")
    g_w_out = _reduce_to_shard(g["w_out"].reshape(N_SHARD, W_OUT_SHARD, D_MODEL), "gw_out")
    packed = jnp.concatenate([_rows(g["conv_w"]), _rows(g["meta_tokens"]), _pack_repl(g, loss_dev)], axis=0)
    red = _allreduce_small(packed, "reduce_small")
    g_conv_full = red[0:16].reshape(CONV_WIDTH, D_CONV)
    g_meta_full = red[16:48].reshape(N_META, D_MODEL)
    g_small, loss = _unpack_repl(red[48:64])
    grads = dict(g_small)
    grads["w_in"] = g_w_in
    grads["w_out"] = g_w_out
    grads["conv_w"] = lax.dynamic_slice(g_conv_full, (0, chip * conv_cols), (CONV_WIDTH, conv_cols))
    grads["meta_tokens"] = lax.dynamic_slice(g_meta_full, (0, chip * meta_cols), (N_META, meta_cols))

    upd = {}
    upd["w_in"] = _adamw(w_in[0], g_w_in, m_w_in[0], v_w_in[0], "adamw_w_in")
    upd["w_out"] = _adamw(w_out[0], g_w_out, m_w_out[0], v_w_out[0], "adamw_w_out")

    def pack_small(vals, conv, meta):
        return jnp.concatenate([_pad_rows(conv.reshape(CONV_WIDTH, conv_cols), 8), _rows(meta), _pack_repl(vals)], axis=0)

    sm = _adamw(pack_small(w, w["conv_w"], w["meta_tokens"]), pack_small(grads, grads["conv_w"], grads["meta_tokens"]),
                pack_small(m, m["conv_w"], m["meta_tokens"]), pack_small(v, v["conv_w"], v["meta_tokens"]),
                "adamw_small")
    for n in names:
        if n not in ("w_in", "w_out"):
            upd[n] = [None, None, None]
    for k, buf in enumerate(sm):
        upd["conv_w"][k] = buf[0:CONV_WIDTH]
        upd["meta_tokens"][k] = buf[8:16].reshape(N_META, meta_cols)
        rest, _ = _unpack_repl(buf[16:32])
        for n in SMALL_REPL + SMALL_HEAD:
            upd[n][k] = rest[n]

    def shaped(n, a):
        return a.reshape(w[n].shape)

    outs = [loss, grad_x[None]]
    outs += [shaped(n, grads[n]) for n in names]
    for k in range(3):
        outs += [shaped(n, upd[n][k]) for n in names]
    return tuple(outs)
```

```python
import functools

import jax
import jax.numpy as jnp
from jax import lax
from jax.experimental import pallas as pl
from jax.experimental.pallas import tpu as pltpu

F32 = jnp.float32
BF16 = jnp.bfloat16

D_MODEL = 2048
CHUNK = 64
N_META = 16
PAD_LEAD = CHUNK - N_META
ROW0 = PAD_LEAD + N_META
EPS = 1e-6
SSD_HEADS = 32
HEAD_DIM = 64
GROUPS = 8
HPG = SSD_HEADS // GROUPS
D_STATE = 128
D_SSD = 2048
GROUP_W = D_SSD // GROUPS
CONV_WIDTH = 4
D_CONV = 4096
Q_HEADS = 16
KV_HEADS = 4
REP = Q_HEADS // KV_HEADS
D_ATT = 1024
D_KV = 256
BAND_CHUNKS = 3
ROPE_THETA = 10000.0
D_MIX = D_SSD + D_ATT
D_IN = 8736
N_SHARD = 4
W_IN_SHARD = D_IN // N_SHARD
W_OUT_SHARD = D_MIX // N_SHARD

OZ, OXS, OB, OC, OQ, OG, OK, OV, ODT = 0, 2048, 4096, 5120, 6144, 7168, 8192, 8448, 8704
DT_SLAB = 512
N_RE = ODT + DT_SLAB
LANES = 128

ADAM_LR, ADAM_B1, ADAM_B2, ADAM_EPS, ADAM_WD, ADAM_STEP = 0.001, 0.9, 0.999, 1e-08, 0.01, 10

VMEM_LIMIT = 52 * 1024 * 1024
NEG = -1e30
HI = lax.Precision.HIGHEST


def _pallas(body, **kw):
    return pl.pallas_call(body, **kw)


def _cp(*sem):
    return pltpu.CompilerParams(dimension_semantics=sem, vmem_limit_bytes=VMEM_LIMIT)


def _tile(n, cap, mult=16):
    best = None
    for d in range(mult, min(n, cap) + 1, mult):
        if n % d == 0:
            best = d
    assert best is not None, (n, cap)
    return best


def _nt(a, b):
    return lax.dot_general(a, b, (((1,), (1,)), ((), ())), preferred_element_type=F32)


def _tn(a, b):
    return lax.dot_general(a, b, (((0,), (0,)), ((), ())), preferred_element_type=F32)


def _mm(a, b):
    return jnp.dot(a, b, preferred_element_type=F32)


def _sigmoid(x):
    return 1.0 / (1.0 + jnp.exp(-x))


def _bf(x):
    return x.astype(BF16)


def _inproj(hpad, norm_w, w_re):
    t, d = hpad.shape
    n = w_re.shape[1]
    tm, tn = _tile(t, 832), 512

    def body(h_ref, nw_ref, w_ref, proj_ref, hn_ref, hn_s):
        @pl.when(pl.program_id(1) == 0)
        def _():
            h = h_ref[...]
            ms = jnp.mean(h * h, axis=-1, keepdims=True)
            hn = _bf(h * lax.rsqrt(ms + EPS) * nw_ref[...])
            hn_s[...] = hn
            hn_ref[...] = hn
        proj_ref[...] = _mm(hn_s[...], w_ref[...])

    return _pallas(
        body, name="inproj", grid=(t // tm, n // tn),
        in_specs=[pl.BlockSpec((tm, d), lambda i, j: (i, 0)), pl.BlockSpec((1, d), lambda i, j: (0, 0)),
                  pl.BlockSpec((d, tn), lambda i, j: (0, j))],
        out_specs=[pl.BlockSpec((tm, tn), lambda i, j: (i, j)), pl.BlockSpec((tm, d), lambda i, j: (i, 0))],
        out_shape=[jax.ShapeDtypeStruct((t, n), F32), jax.ShapeDtypeStruct((t, d), BF16)],
        scratch_shapes=[pltpu.VMEM((tm, d), BF16)],
        compiler_params=_cp("parallel", "arbitrary"))(hpad, norm_w, w_re)


def _conv_fwd(proj, conv_w, conv_b):
    t = proj.shape[0]
    tc = 256
    off = OXS // tc

    def body(x_ref, w_ref, b_ref, o_ref, xp):
        xp[0:8, :] = jnp.zeros((8, tc), F32)
        xp[8:t + 8, :] = x_ref[...]
        w = w_ref[...]
        u = (b_ref[...] + w[3:4, :] * xp[8:t + 8, :] + w[2:3, :] * xp[7:t + 7, :]
             + w[1:2, :] * xp[6:t + 6, :] + w[0:1, :] * xp[5:t + 5, :])
        o_ref[...] = u * _sigmoid(u)

    return _pallas(
        body, name="conv_fwd", grid=(D_CONV // tc,),
        in_specs=[pl.BlockSpec((t, tc), lambda j: (0, j + off)), pl.BlockSpec((CONV_WIDTH, tc), lambda j: (0, j)),
                  pl.BlockSpec((1, tc), lambda j: (0, j))],
        out_specs=pl.BlockSpec((t, tc), lambda j: (0, j)),
        out_shape=jax.ShapeDtypeStruct((t, D_CONV), F32),
        scratch_shapes=[pltpu.VMEM((t + 8, tc), F32)],
        compiler_params=_cp("parallel"))(proj, conv_w, conv_b)


def _softplus(u):
    e = jnp.exp(-jnp.abs(u))
    w = 1.0 + e
    l1p = jnp.where(w == 1.0, e, jnp.log(w) * (e / jnp.where(w == 1.0, 1.0, w - 1.0)))
    return jnp.maximum(u, 0.0) + l1p


def _dt_prep(proj, dt_bias_l, a_log_l):
    t = proj.shape[0]
    nc = t // CHUNK
    q = CHUNK

    def body(raw_ref, bias_ref, alog_ref, dt_ref, acs_ref, acst_ref):
        c = pl.program_id(0)
        sp = _softplus(raw_ref[...] + bias_ref[...])
        row = c * q + lax.broadcasted_iota(jnp.int32, (q, LANES), 0)
        dt = jnp.where(row >= PAD_LEAD, sp, 0.0)
        da = dt * (-jnp.exp(alog_ref[...]))
        ri = lax.broadcasted_iota(jnp.int32, (q, q), 0)
        ci = lax.broadcasted_iota(jnp.int32, (q, q), 1)
        tri = (ri >= ci).astype(F32)
        acs = jnp.dot(tri, da, preferred_element_type=F32, precision=HI)
        dt_ref[...] = dt
        acs_ref[...] = acs
        acst_ref[0] = acs.T

    return _pallas(
        body, name="dt_prep", grid=(nc,),
        in_specs=[pl.BlockSpec((q, LANES), lambda c: (c, ODT // LANES)), pl.BlockSpec((1, LANES), lambda c: (0, 0)),
                  pl.BlockSpec((1, LANES), lambda c: (0, 0))],
        out_specs=[pl.BlockSpec((q, LANES), lambda c: (c, 0)), pl.BlockSpec((q, LANES), lambda c: (c, 0)),
                   pl.BlockSpec((1, LANES, q), lambda c: (c, 0, 0))],
        out_shape=[jax.ShapeDtypeStruct((t, LANES), F32), jax.ShapeDtypeStruct((t, LANES), F32),
                   jax.ShapeDtypeStruct((nc, LANES, q), F32)],
        compiler_params=_cp("parallel"))(proj, dt_bias_l, a_log_l)


def _head_cols(blk, idx):
    lane = lax.broadcasted_iota(jnp.int32, blk.shape, 1)
    return jnp.sum(jnp.where(lane == idx, blk, 0.0), axis=1, keepdims=True)


def _ssd_fwd(xbc, proj, dt, acs, acst, d_skip_l, ssd_norm_w):
    t = xbc.shape[0]
    q = CHUNK
    nc = t // q

    def body(xs_ref, b_ref, c_ref, dt_ref, acs_ref, acst_ref, z_ref, dsk_ref, nw_ref,
             y_ref, ymix_ref, st_ref, state):
        g = pl.program_id(0)

        @pl.when(pl.program_id(1) == 0)
        def _():
            state[...] = jnp.zeros_like(state)

        x = xs_ref[...]
        bm = b_ref[...]
        cm = c_ref[...]
        cb = _nt(_bf(cm), _bf(bm))
        ri = lax.broadcasted_iota(jnp.int32, (q, q), 0)
        ci = lax.broadcasted_iota(jnp.int32, (q, q), 1)
        causal = ri >= ci
        dtb = dt_ref[...]
        acsb = acs_ref[...]
        ys = []
        for r in range(HPG):
            idx = GROUPS * g + r
            dt_c = _head_cols(dtb, idx)
            acs_c = _head_cols(acsb, idx)
            acs_r = acst_ref[0, r:r + 1, :]
            acs_last = acs_r[:, q - 1:q]
            xh = x[:, HEAD_DIM * r:HEAD_DIM * (r + 1)]
            xdt = _bf(xh * dt_c)
            decay = jnp.exp(jnp.where(causal, acs_c - acs_r, NEG))
            m = _bf(cb * decay)
            s_prev = state[r]
            st_ref[0, r] = s_prev
            y_h = _mm(m, xdt) + _nt(_bf(cm), _bf(s_prev)) * jnp.exp(acs_c) + _head_cols(dsk_ref[...], idx) * xh
            bd = _bf(bm * jnp.exp(acs_last - acs_c))
            state[r] = jnp.exp(acs_last) * s_prev + _tn(xdt, bd)
            ys.append(y_h)
        y = jnp.concatenate(ys, axis=1)
        y_ref[...] = y
        z = z_ref[...]
        yg = y * (z * _sigmoid(z))
        ms = jnp.mean(yg * yg, axis=-1, keepdims=True)
        ymix_ref[...] = _bf(yg * lax.rsqrt(ms + EPS) * nw_ref[...])

    return _pallas(
        body, name="ssd_fwd", grid=(GROUPS, nc),
        in_specs=[pl.BlockSpec((q, GROUP_W), lambda g, c: (c, g)),
                  pl.BlockSpec((q, D_STATE), lambda g, c: (c, D_SSD // D_STATE + g)),
                  pl.BlockSpec((q, D_STATE), lambda g, c: (c, D_SSD // D_STATE + GROUPS + g)),
                  pl.BlockSpec((q, LANES), lambda g, c: (c, 0)), pl.BlockSpec((q, LANES), lambda g, c: (c, 0)),
                  pl.BlockSpec((1, GROUPS, q), lambda g, c: (c, g, 0)),
                  pl.BlockSpec((q, GROUP_W), lambda g, c: (c, g)),
                  pl.BlockSpec((1, LANES), lambda g, c: (0, 0)), pl.BlockSpec((1, GROUP_W), lambda g, c: (0, g))],
        out_specs=[pl.BlockSpec((q, GROUP_W), lambda g, c: (c, g)), pl.BlockSpec((q, GROUP_W), lambda g, c: (c, g)),
                   pl.BlockSpec((1, HPG, HEAD_DIM, D_STATE), lambda g, c: (c, g, 0, 0))],
        out_shape=[jax.ShapeDtypeStruct((t, D_SSD), F32), jax.ShapeDtypeStruct((t, D_SSD), BF16),
                   jax.ShapeDtypeStruct((nc, SSD_HEADS, HEAD_DIM, D_STATE), F32)],
        scratch_shapes=[pltpu.VMEM((HPG, HEAD_DIM, D_STATE), F32)],
        compiler_params=_cp("parallel", "arbitrary"))(xbc, xbc, xbc, dt, acs, acst, proj, d_skip_l, ssd_norm_w)


def _swap_halves(v):
    lane = lax.broadcasted_iota(jnp.int32, v.shape, 1)
    return jnp.where((lane & (HEAD_DIM - 1)) < HEAD_DIM // 2, pltpu.roll(v, LANES - HEAD_DIM // 2, 1),
                     pltpu.roll(v, HEAD_DIM // 2, 1))


def _rope(qsrc, q_off, ksrc, k_off, cos_t, sin_t):
    t = qsrc.shape[0]
    tr = _tile(t, 832)

    def body(q_ref, k_ref, cos_ref, sin_ref, qo_ref, ko_ref):
        cs = cos_ref[...]
        sn = sin_ref[...]
        for src, dst, width in ((q_ref, qo_ref, D_ATT), (k_ref, ko_ref, D_KV)):
            for s in range(width // LANES):
                v = src[:, LANES * s:LANES * (s + 1)].astype(F32)
                dst[:, LANES * s:LANES * (s + 1)] = _bf(v * cs + _swap_halves(v) * sn)

    return _pallas(
        body, name="rope", grid=(t // tr,),
        in_specs=[pl.BlockSpec((tr, D_ATT), lambda i: (i, q_off // D_ATT)),
                  pl.BlockSpec((tr, D_KV), lambda i: (i, k_off // D_KV)),
                  pl.BlockSpec((tr, LANES), lambda i: (i, 0)), pl.BlockSpec((tr, LANES), lambda i: (i, 0))],
        out_specs=[pl.BlockSpec((tr, D_ATT), lambda i: (i, 0)), pl.BlockSpec((tr, D_KV), lambda i: (i, 0))],
        out_shape=[jax.ShapeDtypeStruct((t, D_ATT), BF16), jax.ShapeDtypeStruct((t, D_KV), BF16)],
        compiler_params=_cp("parallel"))(qsrc, ksrc, cos_t, sin_t)


def _band_specs(width, col_block):
    return [pl.BlockSpec((CHUNK, width), functools.partial(lambda c, j: (jnp.maximum(c - j, 0), col_block), j=j))
            for j in (2, 1, 0)]


def _attn_probs(qh, kb, sink_col, valid):
    s = _nt(qh, kb) * (HEAD_DIM ** -0.5)
    s = jnp.where(valid, s, NEG)
    m = jnp.maximum(jnp.max(s, axis=1, keepdims=True), sink_col)
    p = jnp.exp(s - m)
    psink = jnp.exp(sink_col - m)
    inv = 1.0 / (jnp.sum(p, axis=1, keepdims=True) + psink)
    return p * inv, psink * inv


def _attn_operands(c, q_ref, k_refs, v_refs, sink_ref, h):
    q = q_ref[...]
    qh = jnp.concatenate([q[:, HEAD_DIM * (REP * h + r):HEAD_DIM * (REP * h + r + 1)] for r in range(REP)], axis=0)
    kb = jnp.concatenate([k[:, HEAD_DIM * h:HEAD_DIM * (h + 1)] for k in k_refs], axis=0)
    vb = jnp.concatenate([_bf(v[:, HEAD_DIM * h:HEAD_DIM * (h + 1)]) for v in v_refs], axis=0)
    rows = lax.broadcasted_iota(jnp.int32, (REP * CHUNK, 1), 0) >> 6
    sink_col = jnp.zeros((REP * CHUNK, 1), F32)
    for r in range(REP):
        sink_col = jnp.where(rows == r, sink_ref[REP * h + r], sink_col)
    key_abs = (c - (BAND_CHUNKS - 1)) * CHUNK + lax.broadcasted_iota(jnp.int32, (1, BAND_CHUNKS * CHUNK), 1)
    return qh, kb, vb, sink_col, key_abs >= PAD_LEAD


def _attn_fwd(qr, kr, proj, sinks):
    t = qr.shape[0]
    nc = t // CHUNK

    def body(q_ref, k2, k1, k0, v2, v1, v0, g_ref, sink_ref, o_ref):
        c = pl.program_id(0)
        ks = [k2[...], k1[...], k0[...]]
        vs = [v2[...], v1[...], v0[...]]
        outs = []
        for h in range(KV_HEADS):
            qh, kb, vb, sink_col, valid = _attn_operands(c, q_ref, ks, vs, sink_ref, h)
            p, _ = _attn_probs(qh, kb, sink_col, valid)
            o = _mm(_bf(p), vb)
            outs += [o[CHUNK * r:CHUNK * (r + 1)] for r in range(REP)]
        att = jnp.concatenate(outs, axis=1)
        gate = g_ref[...]
        o_ref[...] = _bf(att * (gate * _sigmoid(gate)))

    return _pallas(
        body, name="attn_fwd", grid=(nc,),
        in_specs=[pl.BlockSpec((CHUNK, D_ATT), lambda c: (c, 0))] + _band_specs(D_KV, 0)
        + _band_specs(D_KV, OV // D_KV) + [pl.BlockSpec((CHUNK, D_ATT), lambda c: (c, OG // D_ATT)),
                                           pl.BlockSpec(memory_space=pltpu.SMEM)],
        out_specs=pl.BlockSpec((CHUNK, D_ATT), lambda c: (c, 0)),
        out_shape=jax.ShapeDtypeStruct((t, D_ATT), BF16),
        compiler_params=_cp("parallel"))(qr, kr, kr, kr, proj, proj, proj, proj, sinks)


def _outproj(ymix, amix, w_out):
    t = ymix.shape[0]
    tm, tn = _tile(t, 832), 512

    def body(y_ref, a_ref, wy_ref, wa_ref, o_ref):
        o_ref[...] = _mm(y_ref[...], wy_ref[...]) + _mm(a_ref[...], wa_ref[...])

    return _pallas(
        body, name="outproj", grid=(t // tm, D_MODEL // tn),
        in_specs=[pl.BlockSpec((tm, D_SSD), lambda i, j: (i, 0)), pl.BlockSpec((tm, D_ATT), lambda i, j: (i, 0)),
                  pl.BlockSpec((D_SSD, tn), lambda i, j: (0, j)),
                  pl.BlockSpec((D_ATT, tn), lambda i, j: (D_SSD // D_ATT, j))],
        out_specs=pl.BlockSpec((tm, tn), lambda i, j: (i, j)),
        out_shape=jax.ShapeDtypeStruct((t, D_MODEL), F32),
        compiler_params=_cp("parallel", "parallel"))(ymix, amix, w_out, w_out)


def _post_loss(out, x, target, norm_post_w):
    t = out.shape[0]
    nc = t // CHUNK

    def body(o_ref, x_ref, tg_ref, nw_ref, dout_ref, dy_ref, loss_ref, gnw_ref):
        i = pl.program_id(0)

        @pl.when(i == 0)
        def _():
            dout_ref[...] = jnp.zeros_like(dout_ref)
            dy_ref[...] = jnp.zeros_like(dy_ref)
            loss_ref[...] = jnp.zeros_like(loss_ref)
            gnw_ref[...] = jnp.zeros_like(gnw_ref)

        @pl.when(i > 0)
        def _():
            o = o_ref[...]
            nw = nw_ref[...]
            rstd = lax.rsqrt(jnp.mean(o * o, axis=-1, keepdims=True) + EPS)
            n = o * rstd
            err = x_ref[...] + n * nw - tg_ref[...]
            loss_ref[...] += jnp.sum(err * err) * (0.5 / D_MODEL)
            dy = err * (1.0 / D_MODEL)
            dy_ref[...] = dy
            gnw_ref[...] += jnp.sum(dy * n, axis=0, keepdims=True)
            dn = dy * nw
            dout_ref[...] = _bf(rstd * (dn - n * jnp.mean(dn * n, axis=-1, keepdims=True)))

    prev = lambda i: (jnp.maximum(i - 1, 0), 0)
    return _pallas(
        body, name="post_loss", grid=(nc,),
        in_specs=[pl.BlockSpec((CHUNK, D_MODEL), lambda i: (i, 0)), pl.BlockSpec((CHUNK, D_MODEL), prev),
                  pl.BlockSpec((CHUNK, D_MODEL), prev), pl.BlockSpec((1, D_MODEL), lambda i: (0, 0))],
        out_specs=[pl.BlockSpec((CHUNK, D_MODEL), lambda i: (i, 0)), pl.BlockSpec((CHUNK, D_MODEL), lambda i: (i, 0)),
                   pl.BlockSpec((8, LANES), lambda i: (0, 0)), pl.BlockSpec((1, D_MODEL), lambda i: (0, 0))],
        out_shape=[jax.ShapeDtypeStruct((t, D_MODEL), BF16), jax.ShapeDtypeStruct((t, D_MODEL), F32),
                   jax.ShapeDtypeStruct((8, LANES), F32), jax.ShapeDtypeStruct((1, D_MODEL), F32)],
        compiler_params=_cp("arbitrary"))(out, x, target, norm_post_w)


def _nt_matmul(a, b, name):
    t, k = a.shape
    n = b.shape[0]
    tm, tn = _tile(t, 832), 512

    def body(a_ref, b_ref, o_ref):
        o_ref[...] = _nt(a_ref[...], b_ref[...])

    return _pallas(
        body, name=name, grid=(t // tm, n // tn),
        in_specs=[pl.BlockSpec((tm, k), lambda i, j: (i, 0)), pl.BlockSpec((tn, k), lambda i, j: (j, 0))],
        out_specs=pl.BlockSpec((tm, tn), lambda i, j: (i, j)),
        out_shape=jax.ShapeDtypeStruct((t, n), F32),
        compiler_params=_cp("parallel", "parallel"))(a, b)


def _tn_matmul(a, b, name):
    t, m = a.shape
    n = b.shape[1]
    tk, tm, tn = _tile(t, 832), min(m, 1024), min(n, 1024)
    nk = t // tk

    def body(a_ref, b_ref, o_ref):
        @pl.when(pl.program_id(2) == 0)
        def _():
            o_ref[...] = jnp.zeros_like(o_ref)
        o_ref[...] += _tn(a_ref[...], b_ref[...])

    return _pallas(
        body, name=name, grid=(m // tm, n // tn, nk),
        in_specs=[pl.BlockSpec((tk, tm), lambda i, j, k: (k, i)), pl.BlockSpec((tk, tn), lambda i, j, k: (k, j))],
        out_specs=pl.BlockSpec((tm, tn), lambda i, j, k: (i, j)),
        out_shape=jax.ShapeDtypeStruct((m, n), F32),
        compiler_params=_cp("parallel", "parallel", "arbitrary"))(a, b)


def _attn_bwd(qr, kr, proj, dmix, sinks):
    t = qr.shape[0]
    nc = t // CHUNK
    scale = HEAD_DIM ** -0.5

    def body(q_ref, k2, k1, k0, v2, v1, v0, g_ref, da_ref, sink_ref, dq_ref, dg_ref, dk_ref, dv_ref, gs_ref):
        c = pl.program_id(0)

        @pl.when(c == 0)
        def _():
            dk_ref[...] = jnp.zeros_like(dk_ref)
            dv_ref[...] = jnp.zeros_like(dv_ref)
            gs_ref[...] = jnp.zeros_like(gs_ref)

        ks = [k2[...], k1[...], k0[...]]
        vs = [v2[...], v1[...], v0[...]]
        gate = g_ref[...]
        sg = _sigmoid(gate)
        da = da_ref[...]
        datt = da * (gate * sg)
        lane = lax.broadcasted_iota(jnp.int32, (1, LANES), 1)
        rows = lax.broadcasted_iota(jnp.int32, (REP * CHUNK, 1), 0) >> 6
        dqs, atts, dks, dvs = [], [], [], []
        gs = jnp.zeros((1, LANES), F32)
        for h in range(KV_HEADS):
            qh, kb, vb, sink_col, valid = _attn_operands(c, q_ref, ks, vs, sink_ref, h)
            p, psink = _attn_probs(qh, kb, sink_col, valid)
            pb = _bf(p)
            o = _mm(pb, vb)
            do = jnp.concatenate([datt[:, HEAD_DIM * (REP * h + r):HEAD_DIM * (REP * h + r + 1)] for r in range(REP)],
                                 axis=0)
            dob = _bf(do)
            delta = jnp.sum(do * o, axis=1, keepdims=True)
            ds = _bf(p * (_nt(dob, vb) - delta) * scale)
            gsink = -psink * delta
            for r in range(REP):
                gs = gs + jnp.where(lane == REP * h + r, jnp.sum(jnp.where(rows == r, gsink, 0.0)), 0.0)
            dqh = _mm(ds, kb)
            dqs += [dqh[CHUNK * r:CHUNK * (r + 1)] for r in range(REP)]
            atts += [o[CHUNK * r:CHUNK * (r + 1)] for r in range(REP)]
            dks.append(_tn(ds, qh))
            dvs.append(_tn(pb, dob))
        dq_ref[...] = jnp.concatenate(dqs, axis=1)
        att = jnp.concatenate(atts, axis=1)
        dg_ref[...] = _bf(da * att * (sg * (1.0 + gate * (1.0 - sg))))
        gs_ref[0:1, :] += gs
        dkf = jnp.concatenate(dks, axis=1)
        dvf = jnp.concatenate(dvs, axis=1)
        for j in range(BAND_CHUNKS):
            r0 = pl.multiple_of(jnp.maximum(c - (BAND_CHUNKS - 1) + j, 0) * CHUNK, CHUNK)
            dk_ref[pl.ds(r0, CHUNK), :] += dkf[CHUNK * j:CHUNK * (j + 1)]
            dv_ref[pl.ds(r0, CHUNK), :] += dvf[CHUNK * j:CHUNK * (j + 1)]

    return _pallas(
        body, name="attn_bwd", grid=(nc,),
        in_specs=[pl.BlockSpec((CHUNK, D_ATT), lambda c: (c, 0))] + _band_specs(D_KV, 0)
        + _band_specs(D_KV, OV // D_KV) + [pl.BlockSpec((CHUNK, D_ATT), lambda c: (c, OG // D_ATT)),
                                           pl.BlockSpec((CHUNK, D_ATT), lambda c: (c, D_SSD // D_ATT)),
                                           pl.BlockSpec(memory_space=pltpu.SMEM)],
        out_specs=[pl.BlockSpec((CHUNK, D_ATT), lambda c: (c, 0)), pl.BlockSpec((CHUNK, D_ATT), lambda c: (c, 0)),
                   pl.BlockSpec((t, D_KV), lambda c: (0, 0)), pl.BlockSpec((t, D_KV), lambda c: (0, 0)),
                   pl.BlockSpec((8, LANES), lambda c: (0, 0))],
        out_shape=[jax.ShapeDtypeStruct((t, D_ATT), F32), jax.ShapeDtypeStruct((t, D_ATT), BF16),
                   jax.ShapeDtypeStruct((t, D_KV), F32), jax.ShapeDtypeStruct((t, D_KV), F32),
                   jax.ShapeDtypeStruct((8, LANES), F32)],
        compiler_params=_cp("arbitrary"))(qr, kr, kr, kr, proj, proj, proj, proj, dmix, sinks)


def _ssd_bwd(dmix, y_ssd, xbc, proj, dt, acs, acst, states, d_skip_l, ssd_norm_w):
    t = xbc.shape[0]
    q = CHUNK
    nc = t // q

    def body(dmix_ref, y_ref, z_ref, nw_ref, xs_ref, b_ref, c_ref, dt_ref, acs_ref, acst_ref, st_ref, dsk_ref,
             dz_ref, dxs_ref, db_ref, dc_ref, dacs_ref, ddt_ref, gnw_ref, gdsk_ref, dstate):
        g = pl.program_id(0)

        @pl.when(pl.program_id(1) == 0)
        def _():
            dstate[...] = jnp.zeros_like(dstate)
            gnw_ref[...] = jnp.zeros_like(gnw_ref)
            gdsk_ref[...] = jnp.zeros_like(gdsk_ref)

        y = y_ref[...]
        z = z_ref[...]
        sz = _sigmoid(z)
        silu_z = z * sz
        yg = y * silu_z
        rstd = lax.rsqrt(jnp.mean(yg * yg, axis=-1, keepdims=True) + EPS)
        n = yg * rstd
        dout = dmix_ref[...]
        gnw_ref[...] += jnp.sum(dout * n, axis=0, keepdims=True)
        dn = dout * nw_ref[...]
        dyg = rstd * (dn - n * jnp.mean(dn * n, axis=-1, keepdims=True))
        dy = dyg * silu_z
        dz_ref[...] = _bf(dyg * y * (sz * (1.0 + z * (1.0 - sz))))

        x = xs_ref[...]
        bm = b_ref[...]
        cm = c_ref[...]
        bmb, cmb = _bf(bm), _bf(cm)
        cb = _nt(cmb, bmb)
        cbt = _nt(bmb, cmb)
        ri = lax.broadcasted_iota(jnp.int32, (q, q), 0)
        ci = lax.broadcasted_iota(jnp.int32, (q, q), 1)
        lower = ri >= ci
        upper = ri <= ci
        last_row = lax.broadcasted_iota(jnp.int32, (q, 1), 0) == q - 1
        lane = lax.broadcasted_iota(jnp.int32, (q, LANES), 1)
        lane1 = lax.broadcasted_iota(jnp.int32, (8, LANES), 1)
        dtb = dt_ref[...]
        acsb = acs_ref[...]
        dcb = jnp.zeros((q, q), F32)
        dcbt = jnp.zeros((q, q), F32)
        db = jnp.zeros((q, D_STATE), F32)
        dc = jnp.zeros((q, D_STATE), F32)
        dacs_out = jnp.zeros((q, LANES), F32)
        ddt_out = jnp.zeros((q, LANES), F32)
        gdsk = jnp.zeros((8, LANES), F32)
        dxs = []
        for r in range(HPG):
            idx = GROUPS * g + r
            dt_c = _head_cols(dtb, idx)
            acs_c = _head_cols(acsb, idx)
            acs_r = acst_ref[0, r:r + 1, :]
            acs_last = acs_r[:, q - 1:q]
            xh = x[:, HEAD_DIM * r:HEAD_DIM * (r + 1)]
            dyh = dy[:, HEAD_DIM * r:HEAD_DIM * (r + 1)]
            xdt = _bf(xh * dt_c)
            dyb = _bf(dyh)
            dec = jnp.exp(jnp.where(lower, acs_c - acs_r, NEG))
            dect = jnp.exp(jnp.where(upper, acs_r - acs_c, NEG))
            m = cb * dec
            mt = cbt * dect
            s_prev = st_ref[0, r]
            spb = _bf(s_prev)
            ds_new = dstate[r]
            dsb = _bf(ds_new)
            e = jnp.exp(acs_c)
            elast = jnp.exp(acs_last)
            dte = jnp.exp(acs_last - acs_c)
            dxdt = _mm(_bf(mt), dyb) + _nt(_bf(bm * dte), dsb)
            dm = _nt(dyb, xdt)
            dmt = _nt(xdt, dyb)
            dcb = dcb + dm * dec
            dcbt = dcbt + dmt * dect
            dacs = jnp.sum(dm * m, axis=1, keepdims=True) - jnp.sum(dmt * mt, axis=1, keepdims=True)
            cs = _nt(cmb, spb)
            dye = _bf(dyh * e)
            dc = dc + _mm(dye, spb)
            dacs = dacs + jnp.sum(dyh * cs, axis=1, keepdims=True) * e
            dstate[r] = elast * ds_new + _tn(dye, cmb)
            gmat = _mm(xdt, dsb)
            db = db + dte * gmat
            ddte_dte = jnp.sum(bm * gmat, axis=1, keepdims=True) * dte
            dacs = dacs - ddte_dte
            dlast = jnp.sum(ddte_dte) + jnp.sum(s_prev * ds_new) * elast
            dacs = dacs + jnp.where(last_row, dlast, 0.0)
            dsk = _head_cols(dsk_ref[...], idx)
            dxs.append(dxdt * dt_c + dsk * dyh)
            ddt = jnp.sum(dxdt * xh, axis=1, keepdims=True)
            dacs_out = jnp.where(lane == r, dacs, dacs_out)
            ddt_out = jnp.where(lane == r, ddt, ddt_out)
            gdsk = gdsk + jnp.where(lane1 == r, jnp.sum(dyh * xh), 0.0)
        dc = dc + _mm(_bf(dcb), bmb)
        db = db + _mm(_bf(dcbt), cmb)
        dxs_ref[...] = jnp.concatenate(dxs, axis=1)
        db_ref[...] = db
        dc_ref[...] = dc
        dacs_ref[...] = dacs_out
        ddt_ref[...] = ddt_out
        gdsk_ref[0] += gdsk

    rev = lambda c: nc - 1 - c
    return _pallas(
        body, name="ssd_bwd", grid=(GROUPS, nc),
        in_specs=[pl.BlockSpec((q, GROUP_W), lambda g, c: (rev(c), g)), pl.BlockSpec((q, GROUP_W), lambda g, c: (rev(c), g)),
                  pl.BlockSpec((q, GROUP_W), lambda g, c: (rev(c), g)), pl.BlockSpec((1, GROUP_W), lambda g, c: (0, g)),
                  pl.BlockSpec((q, GROUP_W), lambda g, c: (rev(c), g)),
                  pl.BlockSpec((q, D_STATE), lambda g, c: (rev(c), D_SSD // D_STATE + g)),
                  pl.BlockSpec((q, D_STATE), lambda g, c: (rev(c), D_SSD // D_STATE + GROUPS + g)),
                  pl.BlockSpec((q, LANES), lambda g, c: (rev(c), 0)), pl.BlockSpec((q, LANES), lambda g, c: (rev(c), 0)),
                  pl.BlockSpec((1, GROUPS, q), lambda g, c: (rev(c), g, 0)),
                  pl.BlockSpec((1, HPG, HEAD_DIM, D_STATE), lambda g, c: (rev(c), g, 0, 0)),
                  pl.BlockSpec((1, LANES), lambda g, c: (0, 0))],
        out_specs=[pl.BlockSpec((q, GROUP_W), lambda g, c: (rev(c), g)), pl.BlockSpec((q, GROUP_W), lambda g, c: (rev(c), g)),
                   pl.BlockSpec((q, D_STATE), lambda g, c: (rev(c), g)), pl.BlockSpec((q, D_STATE), lambda g, c: (rev(c), g)),
                   pl.BlockSpec((q, LANES), lambda g, c: (rev(c), g)), pl.BlockSpec((q, LANES), lambda g, c: (rev(c), g)),
                   pl.BlockSpec((1, GROUP_W), lambda g, c: (0, g)), pl.BlockSpec((1, 8, LANES), lambda g, c: (g, 0, 0))],
        out_shape=[jax.ShapeDtypeStruct((t, D_SSD), BF16), jax.ShapeDtypeStruct((t, D_SSD), F32),
                   jax.ShapeDtypeStruct((t, GROUPS * D_STATE), F32), jax.ShapeDtypeStruct((t, GROUPS * D_STATE), F32),
                   jax.ShapeDtypeStruct((t, GROUPS * LANES), F32), jax.ShapeDtypeStruct((t, GROUPS * LANES), F32),
                   jax.ShapeDtypeStruct((1, D_SSD), F32), jax.ShapeDtypeStruct((GROUPS, 8, LANES), F32)],
        scratch_shapes=[pltpu.VMEM((HPG, HEAD_DIM, D_STATE), F32)],
        compiler_params=_cp("parallel", "arbitrary"))(dmix, y_ssd, proj, ssd_norm_w, xbc, xbc, xbc, dt, acs, acst,
                                                      states, d_skip_l)


def _dt_bwd(dacs_g, ddt_g, dt, proj, dt_bias_l, a_log_l):
    t = dt.shape[0]
    q = CHUNK
    nc = t // q

    def body(dacs_ref, ddt_ref, dt_ref, raw_ref, bias_ref, alog_ref, draw_ref, ga_ref, gb_ref):
        c = pl.program_id(0)

        @pl.when(c == 0)
        def _():
            ga_ref[...] = jnp.zeros_like(ga_ref)
            gb_ref[...] = jnp.zeros_like(gb_ref)

        lane = lax.broadcasted_iota(jnp.int32, (q, LANES), 1)
        dacs = jnp.zeros((q, LANES), F32)
        ddt = jnp.zeros((q, LANES), F32)
        for g in range(GROUPS):
            mask = (lane >= GROUPS * g) & (lane < GROUPS * g + HPG)
            sl = slice(LANES * g, LANES * (g + 1))
            if g == 0:
                dacs = jnp.where(mask, dacs_ref[:, sl], dacs)
                ddt = jnp.where(mask, ddt_ref[:, sl], ddt)
            else:
                dacs = jnp.where(mask, pltpu.roll(dacs_ref[:, sl], GROUPS * g, 1), dacs)
                ddt = jnp.where(mask, pltpu.roll(ddt_ref[:, sl], GROUPS * g, 1), ddt)
        ri = lax.broadcasted_iota(jnp.int32, (q, q), 0)
        ci = lax.broadcasted_iota(jnp.int32, (q, q), 1)
        triu = (ri <= ci).astype(F32)
        dda = jnp.dot(triu, dacs, preferred_element_type=F32, precision=HI)
        a = -jnp.exp(alog_ref[...])
        dtv = dt_ref[...]
        row = c * q + lax.broadcasted_iota(jnp.int32, (q, LANES), 0)
        used = (lane & (GROUPS - 1)) < HPG
        dsp = jnp.where((row >= PAD_LEAD) & used, dda * a + ddt, 0.0)
        draw = dsp * _sigmoid(raw_ref[...] + bias_ref[...])
        draw_ref[...] = _bf(draw)
        gb_ref[0:1, :] += jnp.sum(draw, axis=0, keepdims=True)
        ga_ref[0:1, :] += jnp.sum(jnp.where(used, dda * dtv, 0.0), axis=0, keepdims=True) * a

    return _pallas(
        body, name="dt_bwd", grid=(nc,),
        in_specs=[pl.BlockSpec((q, GROUPS * LANES), lambda c: (c, 0)), pl.BlockSpec((q, GROUPS * LANES), lambda c: (c, 0)),
                  pl.BlockSpec((q, LANES), lambda c: (c, 0)), pl.BlockSpec((q, LANES), lambda c: (c, ODT // LANES)),
                  pl.BlockSpec((1, LANES), lambda c: (0, 0)), pl.BlockSpec((1, LANES), lambda c: (0, 0))],
        out_specs=[pl.BlockSpec((q, LANES), lambda c: (c, 0)), pl.BlockSpec((8, LANES), lambda c: (0, 0)),
                   pl.BlockSpec((8, LANES), lambda c: (0, 0))],
        out_shape=[jax.ShapeDtypeStruct((t, LANES), BF16), jax.ShapeDtypeStruct((8, LANES), F32),
                   jax.ShapeDtypeStruct((8, LANES), F32)],
        compiler_params=_cp("arbitrary"))(dacs_g, ddt_g, dt, proj, dt_bias_l, a_log_l)


def _conv_bwd(dseg, proj, conv_w, conv_b, col_off, name):
    t, width = dseg.shape
    tc = 128
    off_p = (OXS + col_off) // tc
    off_w = col_off // tc

    def body(d_ref, x_ref, w_ref, b_ref, dx_ref, gw_ref, gb_ref, xp, dup):
        xp[0:8, :] = jnp.zeros((8, tc), F32)
        xp[8:t + 8, :] = x_ref[...]
        w = w_ref[...]
        u = (b_ref[...] + w[3:4, :] * xp[8:t + 8, :] + w[2:3, :] * xp[7:t + 7, :]
             + w[1:2, :] * xp[6:t + 6, :] + w[0:1, :] * xp[5:t + 5, :])
        su = _sigmoid(u)
        du = d_ref[...] * (su * (1.0 + u * (1.0 - su)))
        dup[0:t, :] = du
        dup[t:t + 8, :] = jnp.zeros((8, tc), F32)
        dx_ref[...] = _bf(w[3:4, :] * du + w[2:3, :] * dup[1:t + 1, :] + w[1:2, :] * dup[2:t + 2, :]
                          + w[0:1, :] * dup[3:t + 3, :])
        gb_ref[...] = jnp.sum(du, axis=0, keepdims=True)
        gw_ref[...] = jnp.concatenate(
            [jnp.sum(du * xp[5 + k:t + 5 + k, :], axis=0, keepdims=True) for k in range(CONV_WIDTH)], axis=0)

    return _pallas(
        body, name=name, grid=(width // tc,),
        in_specs=[pl.BlockSpec((t, tc), lambda j: (0, j)), pl.BlockSpec((t, tc), lambda j: (0, j + off_p)),
                  pl.BlockSpec((CONV_WIDTH, tc), lambda j: (0, j + off_w)), pl.BlockSpec((1, tc), lambda j: (0, j + off_w))],
        out_specs=[pl.BlockSpec((t, tc), lambda j: (0, j)), pl.BlockSpec((CONV_WIDTH, tc), lambda j: (0, j)),
                   pl.BlockSpec((1, tc), lambda j: (0, j))],
        out_shape=[jax.ShapeDtypeStruct((t, width), BF16), jax.ShapeDtypeStruct((CONV_WIDTH, width), F32),
                   jax.ShapeDtypeStruct((1, width), F32)],
        scratch_shapes=[pltpu.VMEM((t + 8, tc), F32), pltpu.VMEM((t + 8, tc), F32)],
        compiler_params=_cp("parallel"))(dseg, proj, conv_w, conv_b)


def _dinproj(dproj, w_re, hpad, norm_w, dy_t):
    t, n = dproj.shape
    d = hpad.shape[1]
    tm, tk = _tile(t, 416), 1024
    nk = n // tk

    def body(dp_ref, w_ref, h_ref, nw_ref, dy_ref, dh_ref, gnw_ref, acc):
        i, k = pl.program_id(0), pl.program_id(1)

        @pl.when((i == 0) & (k == 0))
        def _():
            gnw_ref[...] = jnp.zeros_like(gnw_ref)

        @pl.when(k == 0)
        def _():
            acc[...] = jnp.zeros_like(acc)

        acc[...] += _nt(dp_ref[...], w_ref[...])

        @pl.when(k == nk - 1)
        def _():
            h = h_ref[...]
            rstd = lax.rsqrt(jnp.mean(h * h, axis=-1, keepdims=True) + EPS)
            nrm = h * rstd
            dhn = acc[...]
            gnw_ref[...] += jnp.sum(dhn * nrm, axis=0, keepdims=True)
            dn = dhn * nw_ref[...]
            dh_ref[...] = rstd * (dn - nrm * jnp.mean(dn * nrm, axis=-1, keepdims=True)) + dy_ref[...]

    return _pallas(
        body, name="dinproj", grid=(t // tm, nk),
        in_specs=[pl.BlockSpec((tm, tk), lambda i, k: (i, k)), pl.BlockSpec((d, tk), lambda i, k: (0, k)),
                  pl.BlockSpec((tm, d), lambda i, k: (i, 0)), pl.BlockSpec((1, d), lambda i, k: (0, 0)),
                  pl.BlockSpec((tm, d), lambda i, k: (i, 0))],
        out_specs=[pl.BlockSpec((tm, d), lambda i, k: (i, 0)), pl.BlockSpec((1, d), lambda i, k: (0, 0))],
        out_shape=[jax.ShapeDtypeStruct((t, d), F32), jax.ShapeDtypeStruct((1, d), F32)],
        scratch_shapes=[pltpu.VMEM((tm, d), F32)],
        compiler_params=_cp("arbitrary", "arbitrary"))(dproj, w_re, hpad, norm_w, dy_t)


def _spread_heads(v):
    v = jnp.pad(v.reshape(GROUPS, HPG), ((0, 0), (0, GROUPS - HPG))).reshape(1, GROUPS * GROUPS)
    return jnp.pad(v, ((0, 0), (0, LANES - GROUPS * GROUPS)))


def _gather_heads(v):
    return v[0:1, :GROUPS * GROUPS].reshape(GROUPS, GROUPS)[:, :HPG].reshape(1, SSD_HEADS)


def _rope_tables(t):
    half = HEAD_DIM // 2
    inv = ROPE_THETA ** (-jnp.arange(half, dtype=F32) / half)
    pos = (jnp.arange(t) - PAD_LEAD).astype(F32)
    ang = pos[:, None] * inv[None, :]
    cos, sin = jnp.cos(ang), jnp.sin(ang)
    cos_t = jnp.concatenate([cos, cos, cos, cos], axis=1)
    sin_t = jnp.concatenate([-sin, sin, -sin, sin], axis=1)
    return cos_t, sin_t


def _w_in_to_re(w):
    k = w.shape[0]
    dtc = w[:, 6144:6176].reshape(k, GROUPS, HPG)
    dtc = jnp.pad(dtc, ((0, 0), (0, 0), (0, GROUPS - HPG))).reshape(k, GROUPS * GROUPS)
    return jnp.concatenate([w[:, :6144], w[:, 6176:7200], w[:, 7712:8736], w[:, 7200:7456], w[:, 7456:7712], dtc,
                            jnp.zeros((k, DT_SLAB - GROUPS * GROUPS), w.dtype)], axis=1)


def _re_to_w_in(g):
    k = g.shape[0]
    dtc = g[:, ODT:ODT + GROUPS * GROUPS].reshape(k, GROUPS, GROUPS)[:, :, :HPG].reshape(k, SSD_HEADS)
    return jnp.concatenate([g[:, :6144], dtc, g[:, OQ:OQ + D_ATT], g[:, OK:OK + D_KV], g[:, OV:OV + D_KV],
                            g[:, OG:OG + D_ATT]], axis=1)


def _local_step(x, target, meta, norm_pre_w, w_re, conv_w, conv_b, dt_bias, a_log, d_skip, ssd_norm_w, sinks,
                w_out, norm_post_w):
    seq = x.shape[0]
    t = PAD_LEAD + N_META + seq
    hpad = jnp.concatenate([jnp.zeros((PAD_LEAD, D_MODEL), F32), meta, x], axis=0)
    dt_bias_l, a_log_l, d_skip_l = _spread_heads(dt_bias), _spread_heads(a_log), _spread_heads(d_skip)
    cos_t, sin_t = _rope_tables(t)
    sink_v = sinks.reshape(Q_HEADS)

    proj, hn = _inproj(hpad, norm_pre_w, w_re)
    xbc = _conv_fwd(proj, conv_w, conv_b)
    dt, acs, acst = _dt_prep(proj, dt_bias_l, a_log_l)
    y_ssd, ymix, states = _ssd_fwd(xbc, proj, dt, acs, acst, d_skip_l, ssd_norm_w)
    qr, kr = _rope(proj, OQ, proj, OK, cos_t, sin_t)
    amix = _attn_fwd(qr, kr, proj, sink_v)
    out = _outproj(ymix, amix, w_out)
    dout, dy_t, loss_blk, g_norm_post = _post_loss(out, x, target, norm_post_w)

    dmix = _nt_matmul(dout, w_out, "dmix")
    g_w_out = jnp.concatenate([_tn_matmul(ymix, dout, "gw_out_y"), _tn_matmul(amix, dout, "gw_out_a")], axis=0)
    dq_r, dg, dk_r, dv, gs = _attn_bwd(qr, kr, proj, dmix, sink_v)
    dq, dk = _rope(dq_r, 0, dk_r, 0, cos_t, -sin_t)
    dz, dxs, db, dc, dacs_g, ddt_g, g_ssd_norm, gdsk = _ssd_bwd(dmix, y_ssd, xbc, proj, dt, acs, acst, states,
                                                                d_skip_l, ssd_norm_w)
    draw, ga, gb = _dt_bwd(dacs_g, ddt_g, dt, proj, dt_bias_l, a_log_l)
    dxs_p, gcw0, gcb0 = _conv_bwd(dxs, proj, conv_w, conv_b, 0, "conv_bwd_x")
    db_p, gcw1, gcb1 = _conv_bwd(db, proj, conv_w, conv_b, D_SSD, "conv_bwd_b")
    dc_p, gcw2, gcb2 = _conv_bwd(dc, proj, conv_w, conv_b, D_SSD + GROUPS * D_STATE, "conv_bwd_c")
    dproj = jnp.concatenate([dz, dxs_p, db_p, dc_p, dq, dg, dk, _bf(dv), draw,
                             jnp.zeros((t, DT_SLAB - LANES), BF16)], axis=1)
    dh, g_norm_pre = _dinproj(dproj, w_re, hpad, norm_pre_w, dy_t)
    g_w_re = _tn_matmul(hn, dproj, "gw_in")

    gdsk_l = jnp.concatenate([gdsk[g, 0:1, 0:GROUPS] for g in range(GROUPS)], axis=1)
    gdsk_l = jnp.pad(gdsk_l, ((0, 0), (0, LANES - GROUPS * GROUPS)))
    grads = dict(
        meta_tokens=dh[PAD_LEAD:ROW0], norm_pre_w=g_norm_pre, w_re=g_w_re,
        conv_w=jnp.concatenate([gcw0, gcw1, gcw2], axis=1), conv_b=jnp.concatenate([gcb0, gcb1, gcb2], axis=1),
        dt_bias=_gather_heads(gb), a_log=_gather_heads(ga), d_skip=_gather_heads(gdsk_l), ssd_norm_w=g_ssd_norm,
        attn_sinks=gs[0:1, :Q_HEADS], w_out=g_w_out, norm_post_w=g_norm_post)
    return loss_blk[0, 0], dh[ROW0:], grads


ANY = pl.BlockSpec(memory_space=pl.ANY)
MESH = pl.DeviceIdType.MESH
GATHER_CHUNKS = 4
PAIR_CHUNKS = 16
JOIN_CHUNKS = 8


def _rcopy(src, dst, ssem, rsem, dev):
    return pltpu.make_async_remote_copy(src_ref=src, dst_ref=dst, send_sem=ssem, recv_sem=rsem, device_id=dev,
                                        device_id_type=MESH)


def _place():
    x, y, c = lax.axis_index("x"), lax.axis_index("y"), lax.axis_index("c")
    chips = [(1 - x, y), (x, 1 - y), (1 - x, 1 - y)]
    return x, y, c, chips


def _gather_shards(shard, name):
    r, n = shard.shape
    hr = r // 2
    kc = GATHER_CHUNKS
    ch = hr // kc
    assert ch * kc == hr and ch % 16 == 0

    def body(x_ref, out_ref, send_sems, recv_sems, local_sems):
        x, y, c, chips = _place()
        me = 2 * x + y

        def piece(chip, hc, k):
            return out_ref.at[chip, pl.ds(hc * hr + k * ch, ch), :]

        local = [pltpu.make_async_copy(x_ref.at[pl.ds(k * ch, ch), :], out_ref.at[me, pl.ds(k * ch, ch), :],
                                       local_sems.at[k]) for k in range(2 * kc)]
        for cp in local:
            cp.start()
        first = [_rcopy(x_ref.at[pl.ds(c * hr + k * ch, ch), :], piece(me, c, k), send_sems.at[j * kc + k],
                        recv_sems.at[j * kc + k], (*chip, c)) for j, chip in enumerate(chips) for k in range(kc)]
        for cp in first:
            cp.start()
        passed = []
        for j, chip in enumerate(chips):
            cj = 2 * chip[0] + chip[1]
            for k in range(kc):
                s = j * kc + k
                _rcopy(piece(cj, c, k), piece(cj, c, k), send_sems.at[s], recv_sems.at[s], (*chip, c)).wait_recv()
                fw = _rcopy(piece(cj, c, k), piece(cj, c, k), send_sems.at[3 * kc + s], recv_sems.at[3 * kc + s],
                            (x, y, 1 - c))
                fw.start()
                passed.append(fw)
        for j, chip in enumerate(chips):
            cj = 2 * chip[0] + chip[1]
            for k in range(kc):
                s = 3 * kc + j * kc + k
                _rcopy(piece(cj, 1 - c, k), piece(cj, 1 - c, k), send_sems.at[s], recv_sems.at[s],
                       (x, y, 1 - c)).wait_recv()
        for cp in first + passed:
            cp.wait_send()
        for cp in local:
            cp.wait()

    return _pallas(
        body, name=name, in_specs=[ANY], out_specs=ANY,
        out_shape=jax.ShapeDtypeStruct((N_SHARD, r, n), shard.dtype),
        scratch_shapes=[pltpu.SemaphoreType.DMA((6 * kc,)), pltpu.SemaphoreType.DMA((6 * kc,)),
                        pltpu.SemaphoreType.DMA((2 * kc,))])(shard)


def _pair_send(g4, name):
    _, r, n = g4.shape
    hr = r // 2
    kc = PAIR_CHUNKS
    ch = hr // kc
    assert ch * kc == hr and ch % 8 == 0

    def body(g_ref, got_ref, send_sems, recv_sems):
        x, y, c, _ = _place()
        cps = [_rcopy(g_ref.at[:, pl.ds((1 - c) * hr + k * ch, ch), :], got_ref.at[:, pl.ds(k * ch, ch), :],
                      send_sems.at[k], recv_sems.at[k], (x, y, 1 - c)) for k in range(kc)]
        for cp in cps:
            cp.start()
        for cp in cps:
            cp.wait()

    return _pallas(
        body, name=name, in_specs=[ANY], out_specs=ANY, out_shape=jax.ShapeDtypeStruct((N_SHARD, hr, n), F32),
        scratch_shapes=[pltpu.SemaphoreType.DMA((kc,)), pltpu.SemaphoreType.DMA((kc,))])(g4)


def _pair_add(g4, got, core, name):
    k, r, n = g4.shape
    hr = r // 2
    tr = _tile(hr, 256)
    nt = hr // tr

    def body(core_ref, a_ref, b_ref, o_ref):
        o_ref[...] = _bf(a_ref[...] + b_ref[...])

    spec = pl.BlockSpec((1, tr, n), lambda j, i, core_ref: (j, i, 0))
    return _pallas(
        body, name=name,
        grid_spec=pltpu.PrefetchScalarGridSpec(
            num_scalar_prefetch=1, grid=(k, nt),
            in_specs=[pl.BlockSpec((1, tr, n), lambda j, i, core_ref: (j, core_ref[0] * nt + i, 0)), spec],
            out_specs=spec),
        out_shape=jax.ShapeDtypeStruct((k, hr, n), BF16), compiler_params=_cp("parallel", "parallel"))(core, g4, got)


def _chip_exchange(ga, name):
    _, hr, n = ga.shape
    kc = GATHER_CHUNKS
    ch = hr // kc
    assert ch * kc == hr and ch % 16 == 0

    def body(g_ref, got_ref, send_sems, recv_sems):
        x, y, c, chips = _place()
        cps = [_rcopy(g_ref.at[2 * chip[0] + chip[1], pl.ds(k * ch, ch), :], got_ref.at[j, pl.ds(k * ch, ch), :],
                      send_sems.at[j * kc + k], recv_sems.at[j * kc + k], (*chip, c))
               for j, chip in enumerate(chips) for k in range(kc)]
        for cp in cps:
            cp.start()
        for cp in cps:
            cp.wait()

    return _pallas(
        body, name=name, in_specs=[ANY], out_specs=ANY, out_shape=jax.ShapeDtypeStruct((3, hr, n), ga.dtype),
        scratch_shapes=[pltpu.SemaphoreType.DMA((3 * kc,)), pltpu.SemaphoreType.DMA((3 * kc,))])(ga)


def _chip_sum(ga, got, place, name):
    _, hr, n = ga.shape
    tr = _tile(hr, 256)
    nt = hr // tr

    def body(place_ref, own_ref, got_ref, o_ref):
        acc = own_ref[0].astype(F32)
        for j in range(3):
            acc = acc + got_ref[j].astype(F32)
        o_ref[...] = acc

    return _pallas(
        body, name=name,
        grid_spec=pltpu.PrefetchScalarGridSpec(
            num_scalar_prefetch=1, grid=(nt,),
            in_specs=[pl.BlockSpec((1, tr, n), lambda i, place_ref: (place_ref[0], i, 0)),
                      pl.BlockSpec((3, tr, n), lambda i, place_ref: (0, i, 0))],
            out_specs=pl.BlockSpec((tr, n), lambda i, place_ref: (place_ref[1] * nt + i, 0))),
        out_shape=jax.ShapeDtypeStruct((2 * hr, n), F32), compiler_params=_cp("parallel"))(place, ga, got)


def _pair_join(buf, name):
    r, n = buf.shape
    hr = r // 2
    kc = JOIN_CHUNKS
    ch = hr // kc
    assert ch * kc == hr and ch % 8 == 0

    def body(in_ref, out_ref, send_sems, recv_sems):
        x, y, c, _ = _place()
        cps = [_rcopy(out_ref.at[pl.ds(c * hr + k * ch, ch), :], out_ref.at[pl.ds(c * hr + k * ch, ch), :],
                      send_sems.at[k], recv_sems.at[k], (x, y, 1 - c)) for k in range(kc)]
        for cp in cps:
            cp.start()
        for k in range(kc):
            rows = out_ref.at[pl.ds((1 - c) * hr + k * ch, ch), :]
            _rcopy(rows, rows, send_sems.at[k], recv_sems.at[k], (x, y, 1 - c)).wait_recv()
        for cp in cps:
            cp.wait_send()

    return _pallas(
        body, name=name, in_specs=[ANY], out_specs=ANY, out_shape=jax.ShapeDtypeStruct((r, n), F32),
        input_output_aliases={0: 0},
        scratch_shapes=[pltpu.SemaphoreType.DMA((kc,)), pltpu.SemaphoreType.DMA((kc,))])(buf)


def _reduce_to_shard(g4, place, tag):
    got = _pair_send(g4, tag + "_pair_send")
    ga = _pair_add(g4, got, place[1:2], tag + "_pair_add")
    slabs = _chip_exchange(ga, tag + "_chip_exchange")
    return _pair_join(_chip_sum(ga, slabs, place, tag + "_chip_sum"), tag + "_pair_join")


def _allreduce_small(p, name):
    rows, n = p.shape
    ndev = 8

    def body(p_ref, out_ref, slots, send_sems, recv_sems):
        x, y, c, _ = _place()
        my = 4 * x + 2 * y + c
        slots[my] = p_ref[...]
        cps = []
        for k in range(1, ndev):
            kx, ky, kc = (k >> 2) & 1, (k >> 1) & 1, k & 1
            peer = (x ^ kx, y ^ ky, c ^ kc)
            cp = _rcopy(p_ref, slots.at[my], send_sems.at[k - 1], recv_sems.at[k - 1], peer)
            cp.start()
            cps.append(cp)
        for k in range(1, ndev):
            _rcopy(p_ref, slots.at[my ^ k], send_sems.at[k - 1], recv_sems.at[k - 1], (x, y, c)).wait_recv()
        for cp in cps:
            cp.wait_send()
        acc = slots[0]
        for j in range(1, ndev):
            acc = acc + slots[j]
        out_ref[...] = acc

    vm = pl.BlockSpec(memory_space=pltpu.VMEM)
    return _pallas(
        body, name=name, in_specs=[vm], out_specs=vm, out_shape=jax.ShapeDtypeStruct((rows, n), F32),
        scratch_shapes=[pltpu.VMEM((ndev, rows, n), F32), pltpu.SemaphoreType.DMA((ndev - 1,)),
                        pltpu.SemaphoreType.DMA((ndev - 1,))])(p)


def _adamw(w, g, m, v, name):
    r, n = w.shape
    tr = _tile(r, 256, 8)
    c1 = 1.0 / (1.0 - ADAM_B1 ** ADAM_STEP)
    c2 = 1.0 / (1.0 - ADAM_B2 ** ADAM_STEP)

    def body(w_ref, g_ref, m_ref, v_ref, d_ref, mo_ref, vo_ref):
        gv = g_ref[...]
        mn = ADAM_B1 * m_ref[...] + (1.0 - ADAM_B1) * gv
        vn = ADAM_B2 * v_ref[...] + (1.0 - ADAM_B2) * (gv * gv)
        d_ref[...] = -ADAM_LR * ((mn * c1) / (jnp.sqrt(vn * c2) + ADAM_EPS) + ADAM_WD * w_ref[...])
        mo_ref[...] = mn
        vo_ref[...] = vn

    spec = pl.BlockSpec((tr, n), lambda i: (i, 0))
    shp = jax.ShapeDtypeStruct((r, n), F32)
    return _pallas(body, name=name, grid=(r // tr,), in_specs=[spec] * 4, out_specs=[spec] * 3, out_shape=[shp] * 3,
                   compiler_params=_cp("parallel"))(w, g, m, v)


PACK_W = 1024
SMALL_REPL = ("norm_pre_w", "conv_b", "ssd_norm_w", "norm_post_w")
SMALL_HEAD = ("dt_bias", "a_log", "d_skip", "attn_sinks")


def _rows(a):
    return a.reshape(-1, PACK_W)


def _head_row(vals, extra=None):
    parts = [vals[n].reshape(1, -1) for n in SMALL_HEAD]
    if extra is not None:
        parts.append(extra.reshape(1, 1))
    row = jnp.concatenate(parts, axis=1)
    return jnp.pad(row, ((0, 0), (0, PACK_W - row.shape[1])))


def _pad_rows(a, rows):
    return jnp.pad(a, ((0, rows - a.shape[0]), (0, 0)))


def _pack_repl(vals, extra=None):
    body = jnp.concatenate([_rows(vals[n]) for n in SMALL_REPL] + [_head_row(vals, extra)], axis=0)
    return _pad_rows(body, 16)


def _unpack_repl(buf):
    out, r = {}, 0
    for n, k in zip(SMALL_REPL, (2, 4, 2, 2)):
        out[n] = buf[r:r + k].reshape(1, k * PACK_W)
        r += k
    col = 0
    for n, k in zip(SMALL_HEAD, (32, 32, 32, 16)):
        out[n] = buf[r:r + 1, col:col + k]
        col += k
    return out, buf[r, col]


def kernel(x, meta_tokens, norm_pre_w, w_in, conv_w, conv_b, dt_bias, a_log, d_skip, ssd_norm_w, attn_sinks, w_out, norm_post_w, loss_target, m_meta_tokens, m_norm_pre_w, m_w_in, m_conv_w, m_conv_b, m_dt_bias, m_a_log, m_d_skip, m_ssd_norm_w, m_attn_sinks, m_w_out, m_norm_post_w, v_meta_tokens, v_norm_pre_w, v_w_in, v_conv_w, v_conv_b, v_dt_bias, v_a_log, v_d_skip, v_ssd_norm_w, v_attn_sinks, v_w_out, v_norm_post_w):
    names = ("meta_tokens", "norm_pre_w", "w_in", "conv_w", "conv_b", "dt_bias", "a_log", "d_skip", "ssd_norm_w",
             "attn_sinks", "w_out", "norm_post_w")
    w = dict(zip(names, (meta_tokens, norm_pre_w, w_in, conv_w, conv_b, dt_bias, a_log, d_skip, ssd_norm_w, attn_sinks,
                         w_out, norm_post_w)))
    m = dict(zip(names, (m_meta_tokens, m_norm_pre_w, m_w_in, m_conv_w, m_conv_b, m_dt_bias, m_a_log, m_d_skip,
                         m_ssd_norm_w, m_attn_sinks, m_w_out, m_norm_post_w)))
    v = dict(zip(names, (v_meta_tokens, v_norm_pre_w, v_w_in, v_conv_w, v_conv_b, v_dt_bias, v_a_log, v_d_skip,
                         v_ssd_norm_w, v_attn_sinks, v_w_out, v_norm_post_w)))
    cx, cy, cc = lax.axis_index("x"), lax.axis_index("y"), lax.axis_index("c")
    chip = 2 * cx + cy
    meta_cols = D_MODEL // N_SHARD
    conv_cols = D_CONV // N_SHARD

    w_in_all = _gather_shards(_bf(w_in[0]), "gather_w_in")
    w_re = _w_in_to_re(jnp.transpose(w_in_all, (1, 0, 2)).reshape(D_MODEL, D_IN))
    w_out_full = _gather_shards(_bf(w_out[0]), "gather_w_out").reshape(D_MIX, D_MODEL)
    conv_z = lax.dynamic_update_slice(jnp.zeros((CONV_WIDTH, D_CONV), F32), conv_w[0], (0, chip * conv_cols))
    meta_z = lax.dynamic_update_slice(jnp.zeros((N_META, D_MODEL), F32), meta_tokens, (0, chip * meta_cols))
    small = jnp.concatenate([_rows(conv_z), _rows(meta_z)], axis=0)
    small = _allreduce_small(jnp.where(cc == 0, small, 0.0), "gather_small")
    conv_full = small[0:16].reshape(CONV_WIDTH, D_CONV)
    meta_full = small[16:48].reshape(N_META, D_MODEL)

    loss_dev, grad_x, g = _local_step(x[0], loss_target[0], meta_full, norm_pre_w, w_re, conv_full, conv_b, dt_bias,
                                      a_log, d_skip, ssd_norm_w, attn_sinks, w_out_full, norm_post_w)

    g_in4 = jnp.transpose(_re_to_w_in(g["w_re"]).reshape(D_MODEL, N_SHARD, W_IN_SHARD), (1, 0, 2))
    place = jnp.stack([chip, cc]).astype(jnp.int32)
    g_w_in = _reduce_to_shard(g_in4, place, "gw_in")
    g_w_out = _reduce_to_shard(g["w_out"].reshape(N_SHARD, W_OUT_SHARD, D_MODEL), place, "gw_out")
    packed = jnp.concatenate([_rows(g["conv_w"]), _rows(g["meta_tokens"]), _pack_repl(g, loss_dev)], axis=0)
    red = _allreduce_small(packed, "reduce_small")
    g_conv_full = red[0:16].reshape(CONV_WIDTH, D_CONV)
    g_meta_full = red[16:48].reshape(N_META, D_MODEL)
    g_small, loss = _unpack_repl(red[48:64])
    grads = dict(g_small)
    grads["w_in"] = g_w_in
    grads["w_out"] = g_w_out
    grads["conv_w"] = lax.dynamic_slice(g_conv_full, (0, chip * conv_cols), (CONV_WIDTH, conv_cols))
    grads["meta_tokens"] = lax.dynamic_slice(g_meta_full, (0, chip * meta_cols), (N_META, meta_cols))

    upd = {}
    upd["w_in"] = _adamw(w_in[0], g_w_in, m_w_in[0], v_w_in[0], "adamw_w_in")
    upd["w_out"] = _adamw(w_out[0], g_w_out, m_w_out[0], v_w_out[0], "adamw_w_out")

    def pack_small(vals, conv, meta):
        return jnp.concatenate([_pad_rows(conv.reshape(CONV_WIDTH, conv_cols), 8), _rows(meta), _pack_repl(vals)], axis=0)

    sm = _adamw(pack_small(w, w["conv_w"], w["meta_tokens"]), pack_small(grads, grads["conv_w"], grads["meta_tokens"]),
                pack_small(m, m["conv_w"], m["meta_tokens"]), pack_small(v, v["conv_w"], v["meta_tokens"]),
                "adamw_small")
    for n in names:
        if n not in ("w_in", "w_out"):
            upd[n] = [None, None, None]
    for k, buf in enumerate(sm):
        upd["conv_w"][k] = buf[0:CONV_WIDTH]
        upd["meta_tokens"][k] = buf[8:16].reshape(N_META, meta_cols)
        rest, _ = _unpack_repl(buf[16:32])
        for n in SMALL_REPL + SMALL_HEAD:
            upd[n][k] = rest[n]

    def shaped(n, a):
        return a.reshape(w[n].shape)

    outs = [loss, grad_x[None]]
    outs += [shaped(n, grads[n]) for n in names]
    for k in range(3):
        outs += [shaped(n, upd[n][k]) for n in names]
    return tuple(outs)
```

```python
import functools

import jax
import jax.numpy as jnp
from jax import lax
from jax.experimental import pallas as pl
from jax.experimental.pallas import tpu as pltpu

F32 = jnp.float32
BF16 = jnp.bfloat16

D_MODEL = 2048
CHUNK = 64
N_META = 16
PAD_LEAD = CHUNK - N_META
ROW0 = PAD_LEAD + N_META
EPS = 1e-6
SSD_HEADS = 32
HEAD_DIM = 64
GROUPS = 8
HPG = SSD_HEADS // GROUPS
D_STATE = 128
D_SSD = 2048
GROUP_W = D_SSD // GROUPS
CONV_WIDTH = 4
D_CONV = 4096
Q_HEADS = 16
KV_HEADS = 4
REP = Q_HEADS // KV_HEADS
D_ATT = 1024
D_KV = 256
BAND_CHUNKS = 3
ROPE_THETA = 10000.0
D_MIX = D_SSD + D_ATT
D_IN = 8736
N_SHARD = 4
W_IN_SHARD = D_IN // N_SHARD
W_OUT_SHARD = D_MIX // N_SHARD

OZ, OXS, OB, OC, OQ, OG, OK, OV, ODT = 0, 2048, 4096, 5120, 6144, 7168, 8192, 8448, 8704
DT_SLAB = 512
N_RE = ODT + DT_SLAB
LANES = 128

ADAM_LR, ADAM_B1, ADAM_B2, ADAM_EPS, ADAM_WD, ADAM_STEP = 0.001, 0.9, 0.999, 1e-08, 0.01, 10

SSD_GROUPS_PER_STEP = 4
VMEM_LIMIT = 52 * 1024 * 1024
NEG = -1e30
HI = lax.Precision.HIGHEST


def _pallas(body, **kw):
    return pl.pallas_call(body, **kw)


def _cp(*sem):
    return pltpu.CompilerParams(dimension_semantics=sem, vmem_limit_bytes=VMEM_LIMIT)


def _tile(n, cap, mult=16):
    best = None
    for d in range(mult, min(n, cap) + 1, mult):
        if n % d == 0:
            best = d
    assert best is not None, (n, cap)
    return best


def _nt(a, b):
    return lax.dot_general(a, b, (((1,), (1,)), ((), ())), preferred_element_type=F32)


def _tn(a, b):
    return lax.dot_general(a, b, (((0,), (0,)), ((), ())), preferred_element_type=F32)


def _mm(a, b):
    return jnp.dot(a, b, preferred_element_type=F32)


def _sigmoid(x):
    return 1.0 / (1.0 + jnp.exp(-x))


def _bf(x):
    return x.astype(BF16)


def _inproj(hpad, norm_w, w_re):
    t, d = hpad.shape
    n = w_re.shape[1]
    tm, tn = _tile(t, 832), 512

    def body(h_ref, nw_ref, w_ref, proj_ref, hn_ref, hn_s):
        @pl.when(pl.program_id(1) == 0)
        def _():
            h = h_ref[...]
            ms = jnp.mean(h * h, axis=-1, keepdims=True)
            hn = _bf(h * lax.rsqrt(ms + EPS) * nw_ref[...])
            hn_s[...] = hn
            hn_ref[...] = hn
        proj_ref[...] = _mm(hn_s[...], w_ref[...])

    return _pallas(
        body, name="inproj", grid=(t // tm, n // tn),
        in_specs=[pl.BlockSpec((tm, d), lambda i, j: (i, 0)), pl.BlockSpec((1, d), lambda i, j: (0, 0)),
                  pl.BlockSpec((d, tn), lambda i, j: (0, j))],
        out_specs=[pl.BlockSpec((tm, tn), lambda i, j: (i, j)), pl.BlockSpec((tm, d), lambda i, j: (i, 0))],
        out_shape=[jax.ShapeDtypeStruct((t, n), F32), jax.ShapeDtypeStruct((t, d), BF16)],
        scratch_shapes=[pltpu.VMEM((tm, d), BF16)],
        compiler_params=_cp("parallel", "arbitrary"))(hpad, norm_w, w_re)


def _conv_fwd(proj, conv_w, conv_b):
    t = proj.shape[0]
    tc = 256
    off = OXS // tc

    def body(x_ref, w_ref, b_ref, o_ref, xp):
        xp[0:8, :] = jnp.zeros((8, tc), F32)
        xp[8:t + 8, :] = x_ref[...]
        w = w_ref[...]
        u = (b_ref[...] + w[3:4, :] * xp[8:t + 8, :] + w[2:3, :] * xp[7:t + 7, :]
             + w[1:2, :] * xp[6:t + 6, :] + w[0:1, :] * xp[5:t + 5, :])
        o_ref[...] = u * _sigmoid(u)

    return _pallas(
        body, name="conv_fwd", grid=(D_CONV // tc,),
        in_specs=[pl.BlockSpec((t, tc), lambda j: (0, j + off)), pl.BlockSpec((CONV_WIDTH, tc), lambda j: (0, j)),
                  pl.BlockSpec((1, tc), lambda j: (0, j))],
        out_specs=pl.BlockSpec((t, tc), lambda j: (0, j)),
        out_shape=jax.ShapeDtypeStruct((t, D_CONV), F32),
        scratch_shapes=[pltpu.VMEM((t + 8, tc), F32)],
        compiler_params=_cp("parallel"))(proj, conv_w, conv_b)


def _softplus(u):
    e = jnp.exp(-jnp.abs(u))
    w = 1.0 + e
    l1p = jnp.where(w == 1.0, e, jnp.log(w) * (e / jnp.where(w == 1.0, 1.0, w - 1.0)))
    return jnp.maximum(u, 0.0) + l1p


def _dt_prep(proj, dt_bias_l, a_log_l):
    t = proj.shape[0]
    nc = t // CHUNK
    q = CHUNK

    def body(raw_ref, bias_ref, alog_ref, dt_ref, acs_ref, acst_ref):
        c = pl.program_id(0)
        sp = _softplus(raw_ref[...] + bias_ref[...])
        row = c * q + lax.broadcasted_iota(jnp.int32, (q, LANES), 0)
        dt = jnp.where(row >= PAD_LEAD, sp, 0.0)
        da = dt * (-jnp.exp(alog_ref[...]))
        ri = lax.broadcasted_iota(jnp.int32, (q, q), 0)
        ci = lax.broadcasted_iota(jnp.int32, (q, q), 1)
        tri = (ri >= ci).astype(F32)
        acs = jnp.dot(tri, da, preferred_element_type=F32, precision=HI)
        dt_ref[...] = dt
        acs_ref[...] = acs
        acst_ref[0] = acs.T

    return _pallas(
        body, name="dt_prep", grid=(nc,),
        in_specs=[pl.BlockSpec((q, LANES), lambda c: (c, ODT // LANES)), pl.BlockSpec((1, LANES), lambda c: (0, 0)),
                  pl.BlockSpec((1, LANES), lambda c: (0, 0))],
        out_specs=[pl.BlockSpec((q, LANES), lambda c: (c, 0)), pl.BlockSpec((q, LANES), lambda c: (c, 0)),
                   pl.BlockSpec((1, LANES, q), lambda c: (c, 0, 0))],
        out_shape=[jax.ShapeDtypeStruct((t, LANES), F32), jax.ShapeDtypeStruct((t, LANES), F32),
                   jax.ShapeDtypeStruct((nc, LANES, q), F32)],
        compiler_params=_cp("parallel"))(proj, dt_bias_l, a_log_l)


def _head_cols(blk, idx):
    lane = lax.broadcasted_iota(jnp.int32, blk.shape, 1)
    return jnp.sum(jnp.where(lane == idx, blk, 0.0), axis=1, keepdims=True)


class _HeadVals:
    pass


def _lane_head(shape):
    return lax.broadcasted_iota(jnp.int32, shape, len(shape) - 1) >> 6


def _group_heads(g, gi, dtb, acsb, acst_ref, dskb):
    q = dtb.shape[0]
    hv = _HeadVals()
    lh = _lane_head((1, GROUP_W))
    hv.dt = jnp.zeros((q, GROUP_W), F32)
    hv.acs = jnp.zeros((q, GROUP_W), F32)
    hv.acs_last = jnp.zeros((1, GROUP_W), F32)
    hv.dsk = jnp.zeros((1, GROUP_W), F32)
    rows = []
    for r in range(HPG):
        idx = GROUPS * g + r
        sel = lh == r
        acs_r = acst_ref[0, GROUPS * gi + r:GROUPS * gi + r + 1, :]
        rows.append(acs_r)
        hv.dt = jnp.where(sel, _head_cols(dtb, idx), hv.dt)
        hv.acs = jnp.where(sel, _head_cols(acsb, idx), hv.acs)
        hv.acs_last = jnp.where(sel, acs_r[:, q - 1:q], hv.acs_last)
        hv.dsk = jnp.where(sel, _head_cols(dskb, idx), hv.dsk)
    hv.acs_row = jnp.concatenate(rows, axis=1)
    return hv


def _head_tri(q, lower):
    ri = lax.broadcasted_iota(jnp.int32, (q, GROUP_W), 0)
    li = lax.broadcasted_iota(jnp.int32, (q, GROUP_W), 1) & (HEAD_DIM - 1)
    return ri >= li if lower else ri <= li


def _block_diag(v):
    rb = lax.broadcasted_iota(jnp.int32, (GROUP_W, GROUP_W), 0) >> 6
    cb = lax.broadcasted_iota(jnp.int32, (GROUP_W, GROUP_W), 1) >> 6
    return jnp.where(rb == cb, jnp.concatenate([v] * HPG, axis=0), jnp.zeros((), v.dtype))


def _head_sums(v, r):
    return jnp.sum(jnp.where(_lane_head((1, GROUP_W)) == r, v, 0.0), axis=1, keepdims=True)


def _ssd_fwd(xbc, proj, dt, acs, acst, d_skip_l, ssd_norm_w):
    t = xbc.shape[0]
    q = CHUNK
    nc = t // q

    gps = SSD_GROUPS_PER_STEP
    gw, sw = gps * GROUP_W, gps * D_STATE

    def body(xs_ref, b_ref, c_ref, dt_ref, acs_ref, acst_ref, z_ref, dsk_ref, nw_ref,
             y_ref, ymix_ref, st_ref, state):
        @pl.when(pl.program_id(1) == 0)
        def _():
            state[...] = jnp.zeros_like(state)

        for gi in range(gps):
            g = gps * pl.program_id(0) + gi
            cols = slice(GROUP_W * gi, GROUP_W * (gi + 1))
            x = xs_ref[:, cols]
            bmb = _bf(b_ref[:, D_STATE * gi:D_STATE * (gi + 1)])
            cmb = _bf(c_ref[:, D_STATE * gi:D_STATE * (gi + 1)])
            hv = _group_heads(g, gi, dt_ref[...], acs_ref[...], acst_ref, dsk_ref[...])
            decay = jnp.exp(jnp.where(_head_tri(q, True), hv.acs - hv.acs_row, NEG))
            m_all = _bf(_nt(cmb, jnp.concatenate([bmb] * HPG, axis=0)) * decay)
            xdt = x * hv.dt
            s_prev = state[gi]
            st_ref[0, gi] = s_prev
            y = (_mm(m_all, _block_diag(_bf(xdt))) + _mm(cmb, _bf(s_prev)) * jnp.exp(hv.acs) + hv.dsk * x)
            state[gi] = jnp.exp(hv.acs_last) * s_prev + _tn(bmb, _bf(xdt * jnp.exp(hv.acs_last - hv.acs)))
            y_ref[:, cols] = y
            z = z_ref[:, cols]
            yg = y * (z * _sigmoid(z))
            ms = jnp.mean(yg * yg, axis=-1, keepdims=True)
            ymix_ref[:, cols] = _bf(yg * lax.rsqrt(ms + EPS) * nw_ref[:, cols])

    return _pallas(
        body, name="ssd_fwd", grid=(GROUPS // gps, nc),
        in_specs=[pl.BlockSpec((q, gw), lambda g, c: (c, g)),
                  pl.BlockSpec((q, sw), lambda g, c: (c, D_SSD // sw + g)),
                  pl.BlockSpec((q, sw), lambda g, c: (c, (D_SSD + GROUPS * D_STATE) // sw + g)),
                  pl.BlockSpec((q, LANES), lambda g, c: (c, 0)), pl.BlockSpec((q, LANES), lambda g, c: (c, 0)),
                  pl.BlockSpec((1, gps * GROUPS, q), lambda g, c: (c, g, 0)),
                  pl.BlockSpec((q, gw), lambda g, c: (c, g)),
                  pl.BlockSpec((1, LANES), lambda g, c: (0, 0)), pl.BlockSpec((1, gw), lambda g, c: (0, g))],
        out_specs=[pl.BlockSpec((q, gw), lambda g, c: (c, g)), pl.BlockSpec((q, gw), lambda g, c: (c, g)),
                   pl.BlockSpec((1, gps, D_STATE, GROUP_W), lambda g, c: (c, g, 0, 0))],
        out_shape=[jax.ShapeDtypeStruct((t, D_SSD), F32), jax.ShapeDtypeStruct((t, D_SSD), BF16),
                   jax.ShapeDtypeStruct((nc, GROUPS, D_STATE, GROUP_W), F32)],
        scratch_shapes=[pltpu.VMEM((gps, D_STATE, GROUP_W), F32)],
        compiler_params=_cp("parallel", "arbitrary"))(xbc, xbc, xbc, dt, acs, acst, proj, d_skip_l, ssd_norm_w)


def _swap_halves(v):
    lane = lax.broadcasted_iota(jnp.int32, v.shape, 1)
    return jnp.where((lane & (HEAD_DIM - 1)) < HEAD_DIM // 2, pltpu.roll(v, LANES - HEAD_DIM // 2, 1),
                     pltpu.roll(v, HEAD_DIM // 2, 1))


def _rope(qsrc, q_off, ksrc, k_off, cos_t, sin_t):
    t = qsrc.shape[0]
    tr = _tile(t, 832)

    def body(q_ref, k_ref, cos_ref, sin_ref, qo_ref, ko_ref):
        cs = cos_ref[...]
        sn = sin_ref[...]
        for src, dst, width in ((q_ref, qo_ref, D_ATT), (k_ref, ko_ref, D_KV)):
            for s in range(width // LANES):
                v = src[:, LANES * s:LANES * (s + 1)].astype(F32)
                dst[:, LANES * s:LANES * (s + 1)] = _bf(v * cs + _swap_halves(v) * sn)

    return _pallas(
        body, name="rope", grid=(t // tr,),
        in_specs=[pl.BlockSpec((tr, D_ATT), lambda i: (i, q_off // D_ATT)),
                  pl.BlockSpec((tr, D_KV), lambda i: (i, k_off // D_KV)),
                  pl.BlockSpec((tr, LANES), lambda i: (i, 0)), pl.BlockSpec((tr, LANES), lambda i: (i, 0))],
        out_specs=[pl.BlockSpec((tr, D_ATT), lambda i: (i, 0)), pl.BlockSpec((tr, D_KV), lambda i: (i, 0))],
        out_shape=[jax.ShapeDtypeStruct((t, D_ATT), BF16), jax.ShapeDtypeStruct((t, D_KV), BF16)],
        compiler_params=_cp("parallel"))(qsrc, ksrc, cos_t, sin_t)


def _band_specs(width, col_block):
    return [pl.BlockSpec((CHUNK, width), functools.partial(lambda c, j: (jnp.maximum(c - j, 0), col_block), j=j))
            for j in (2, 1, 0)]


def _attn_probs(qh, kb, sink_col, valid):
    s = _nt(qh, kb) * (HEAD_DIM ** -0.5)
    s = jnp.where(valid, s, NEG)
    m = jnp.maximum(jnp.max(s, axis=1, keepdims=True), sink_col)
    p = jnp.exp(s - m)
    psink = jnp.exp(sink_col - m)
    inv = 1.0 / (jnp.sum(p, axis=1, keepdims=True) + psink)
    return p * inv, psink * inv


def _attn_operands(c, q_ref, k_refs, v_refs, sink_ref, h):
    q = q_ref[...]
    qh = jnp.concatenate([q[:, HEAD_DIM * (REP * h + r):HEAD_DIM * (REP * h + r + 1)] for r in range(REP)], axis=0)
    kb = jnp.concatenate([k[:, HEAD_DIM * h:HEAD_DIM * (h + 1)] for k in k_refs], axis=0)
    vb = jnp.concatenate([_bf(v[:, HEAD_DIM * h:HEAD_DIM * (h + 1)]) for v in v_refs], axis=0)
    rows = lax.broadcasted_iota(jnp.int32, (REP * CHUNK, 1), 0) >> 6
    sink_col = jnp.zeros((REP * CHUNK, 1), F32)
    for r in range(REP):
        sink_col = jnp.where(rows == r, sink_ref[REP * h + r], sink_col)
    key_abs = (c - (BAND_CHUNKS - 1)) * CHUNK + lax.broadcasted_iota(jnp.int32, (1, BAND_CHUNKS * CHUNK), 1)
    return qh, kb, vb, sink_col, key_abs >= PAD_LEAD


def _attn_fwd(qr, kr, proj, sinks):
    t = qr.shape[0]
    nc = t // CHUNK

    def body(q_ref, k2, k1, k0, v2, v1, v0, g_ref, sink_ref, o_ref):
        c = pl.program_id(0)
        ks = [k2[...], k1[...], k0[...]]
        vs = [v2[...], v1[...], v0[...]]
        outs = []
        for h in range(KV_HEADS):
            qh, kb, vb, sink_col, valid = _attn_operands(c, q_ref, ks, vs, sink_ref, h)
            p, _ = _attn_probs(qh, kb, sink_col, valid)
            o = _mm(_bf(p), vb)
            outs += [o[CHUNK * r:CHUNK * (r + 1)] for r in range(REP)]
        att = jnp.concatenate(outs, axis=1)
        gate = g_ref[...]
        o_ref[...] = _bf(att * (gate * _sigmoid(gate)))

    return _pallas(
        body, name="attn_fwd", grid=(nc,),
        in_specs=[pl.BlockSpec((CHUNK, D_ATT), lambda c: (c, 0))] + _band_specs(D_KV, 0)
        + _band_specs(D_KV, OV // D_KV) + [pl.BlockSpec((CHUNK, D_ATT), lambda c: (c, OG // D_ATT)),
                                           pl.BlockSpec(memory_space=pltpu.SMEM)],
        out_specs=pl.BlockSpec((CHUNK, D_ATT), lambda c: (c, 0)),
        out_shape=jax.ShapeDtypeStruct((t, D_ATT), BF16),
        compiler_params=_cp("parallel"))(qr, kr, kr, kr, proj, proj, proj, proj, sinks)


def _outproj(ymix, amix, w_out):
    t = ymix.shape[0]
    tm, tn = _tile(t, 832), 512

    def body(y_ref, a_ref, wy_ref, wa_ref, o_ref):
        o_ref[...] = _mm(y_ref[...], wy_ref[...]) + _mm(a_ref[...], wa_ref[...])

    return _pallas(
        body, name="outproj", grid=(t // tm, D_MODEL // tn),
        in_specs=[pl.BlockSpec((tm, D_SSD), lambda i, j: (i, 0)), pl.BlockSpec((tm, D_ATT), lambda i, j: (i, 0)),
                  pl.BlockSpec((D_SSD, tn), lambda i, j: (0, j)),
                  pl.BlockSpec((D_ATT, tn), lambda i, j: (D_SSD // D_ATT, j))],
        out_specs=pl.BlockSpec((tm, tn), lambda i, j: (i, j)),
        out_shape=jax.ShapeDtypeStruct((t, D_MODEL), F32),
        compiler_params=_cp("parallel", "parallel"))(ymix, amix, w_out, w_out)


def _post_loss(out, x, target, norm_post_w):
    t = out.shape[0]
    nc = t // CHUNK

    def body(o_ref, x_ref, tg_ref, nw_ref, dout_ref, dy_ref, loss_ref, gnw_ref):
        i = pl.program_id(0)

        @pl.when(i == 0)
        def _():
            dout_ref[...] = jnp.zeros_like(dout_ref)
            dy_ref[...] = jnp.zeros_like(dy_ref)
            loss_ref[...] = jnp.zeros_like(loss_ref)
            gnw_ref[...] = jnp.zeros_like(gnw_ref)

        @pl.when(i > 0)
        def _():
            o = o_ref[...]
            nw = nw_ref[...]
            rstd = lax.rsqrt(jnp.mean(o * o, axis=-1, keepdims=True) + EPS)
            n = o * rstd
            err = x_ref[...] + n * nw - tg_ref[...]
            loss_ref[...] += jnp.sum(err * err) * (0.5 / D_MODEL)
            dy = err * (1.0 / D_MODEL)
            dy_ref[...] = dy
            gnw_ref[...] += jnp.sum(dy * n, axis=0, keepdims=True)
            dn = dy * nw
            dout_ref[...] = _bf(rstd * (dn - n * jnp.mean(dn * n, axis=-1, keepdims=True)))

    prev = lambda i: (jnp.maximum(i - 1, 0), 0)
    return _pallas(
        body, name="post_loss", grid=(nc,),
        in_specs=[pl.BlockSpec((CHUNK, D_MODEL), lambda i: (i, 0)), pl.BlockSpec((CHUNK, D_MODEL), prev),
                  pl.BlockSpec((CHUNK, D_MODEL), prev), pl.BlockSpec((1, D_MODEL), lambda i: (0, 0))],
        out_specs=[pl.BlockSpec((CHUNK, D_MODEL), lambda i: (i, 0)), pl.BlockSpec((CHUNK, D_MODEL), lambda i: (i, 0)),
                   pl.BlockSpec((8, LANES), lambda i: (0, 0)), pl.BlockSpec((1, D_MODEL), lambda i: (0, 0))],
        out_shape=[jax.ShapeDtypeStruct((t, D_MODEL), BF16), jax.ShapeDtypeStruct((t, D_MODEL), F32),
                   jax.ShapeDtypeStruct((8, LANES), F32), jax.ShapeDtypeStruct((1, D_MODEL), F32)],
        compiler_params=_cp("arbitrary"))(out, x, target, norm_post_w)


def _nt_matmul(a, b, name):
    t, k = a.shape
    n = b.shape[0]
    tm, tn = _tile(t, 832), 512

    def body(a_ref, b_ref, o_ref):
        o_ref[...] = _nt(a_ref[...], b_ref[...])

    return _pallas(
        body, name=name, grid=(t // tm, n // tn),
        in_specs=[pl.BlockSpec((tm, k), lambda i, j: (i, 0)), pl.BlockSpec((tn, k), lambda i, j: (j, 0))],
        out_specs=pl.BlockSpec((tm, tn), lambda i, j: (i, j)),
        out_shape=jax.ShapeDtypeStruct((t, n), F32),
        compiler_params=_cp("parallel", "parallel"))(a, b)


def _tn_matmul(a, b, name):
    t, m = a.shape
    n = b.shape[1]
    tk, tm, tn = _tile(t, 832), min(m, 1024), min(n, 1024)
    nk = t // tk

    def body(a_ref, b_ref, o_ref):
        @pl.when(pl.program_id(2) == 0)
        def _():
            o_ref[...] = jnp.zeros_like(o_ref)
        o_ref[...] += _tn(a_ref[...], b_ref[...])

    return _pallas(
        body, name=name, grid=(m // tm, n // tn, nk),
        in_specs=[pl.BlockSpec((tk, tm), lambda i, j, k: (k, i)), pl.BlockSpec((tk, tn), lambda i, j, k: (k, j))],
        out_specs=pl.BlockSpec((tm, tn), lambda i, j, k: (i, j)),
        out_shape=jax.ShapeDtypeStruct((m, n), F32),
        compiler_params=_cp("parallel", "parallel", "arbitrary"))(a, b)


def _attn_bwd(qr, kr, proj, dmix, sinks):
    t = qr.shape[0]
    nc = t // CHUNK
    scale = HEAD_DIM ** -0.5

    def body(q_ref, k2, k1, k0, v2, v1, v0, g_ref, da_ref, sink_ref, dq_ref, dg_ref, dk_ref, dv_ref, gs_ref):
        c = pl.program_id(0)

        @pl.when(c == 0)
        def _():
            dk_ref[...] = jnp.zeros_like(dk_ref)
            dv_ref[...] = jnp.zeros_like(dv_ref)
            gs_ref[...] = jnp.zeros_like(gs_ref)

        ks = [k2[...], k1[...], k0[...]]
        vs = [v2[...], v1[...], v0[...]]
        gate = g_ref[...]
        sg = _sigmoid(gate)
        da = da_ref[...]
        datt = da * (gate * sg)
        lane = lax.broadcasted_iota(jnp.int32, (1, LANES), 1)
        rows = lax.broadcasted_iota(jnp.int32, (REP * CHUNK, 1), 0) >> 6
        dqs, atts, dks, dvs = [], [], [], []
        gs = jnp.zeros((1, LANES), F32)
        for h in range(KV_HEADS):
            qh, kb, vb, sink_col, valid = _attn_operands(c, q_ref, ks, vs, sink_ref, h)
            p, psink = _attn_probs(qh, kb, sink_col, valid)
            pb = _bf(p)
            o = _mm(pb, vb)
            do = jnp.concatenate([datt[:, HEAD_DIM * (REP * h + r):HEAD_DIM * (REP * h + r + 1)] for r in range(REP)],
                                 axis=0)
            dob = _bf(do)
            delta = jnp.sum(do * o, axis=1, keepdims=True)
            ds = _bf(p * (_nt(dob, vb) - delta) * scale)
            gsink = -psink * delta
            for r in range(REP):
                gs = gs + jnp.where(lane == REP * h + r, jnp.sum(jnp.where(rows == r, gsink, 0.0)), 0.0)
            dqh = _mm(ds, kb)
            dqs += [dqh[CHUNK * r:CHUNK * (r + 1)] for r in range(REP)]
            atts += [o[CHUNK * r:CHUNK * (r + 1)] for r in range(REP)]
            dks.append(_tn(ds, qh))
            dvs.append(_tn(pb, dob))
        dq_ref[...] = jnp.concatenate(dqs, axis=1)
        att = jnp.concatenate(atts, axis=1)
        dg_ref[...] = _bf(da * att * (sg * (1.0 + gate * (1.0 - sg))))
        gs_ref[0:1, :] += gs
        dkf = jnp.concatenate(dks, axis=1)
        dvf = jnp.concatenate(dvs, axis=1)
        for j in range(BAND_CHUNKS):
            r0 = pl.multiple_of(jnp.maximum(c - (BAND_CHUNKS - 1) + j, 0) * CHUNK, CHUNK)
            dk_ref[pl.ds(r0, CHUNK), :] += dkf[CHUNK * j:CHUNK * (j + 1)]
            dv_ref[pl.ds(r0, CHUNK), :] += dvf[CHUNK * j:CHUNK * (j + 1)]

    return _pallas(
        body, name="attn_bwd", grid=(nc,),
        in_specs=[pl.BlockSpec((CHUNK, D_ATT), lambda c: (c, 0))] + _band_specs(D_KV, 0)
        + _band_specs(D_KV, OV // D_KV) + [pl.BlockSpec((CHUNK, D_ATT), lambda c: (c, OG // D_ATT)),
                                           pl.BlockSpec((CHUNK, D_ATT), lambda c: (c, D_SSD // D_ATT)),
                                           pl.BlockSpec(memory_space=pltpu.SMEM)],
        out_specs=[pl.BlockSpec((CHUNK, D_ATT), lambda c: (c, 0)), pl.BlockSpec((CHUNK, D_ATT), lambda c: (c, 0)),
                   pl.BlockSpec((t, D_KV), lambda c: (0, 0)), pl.BlockSpec((t, D_KV), lambda c: (0, 0)),
                   pl.BlockSpec((8, LANES), lambda c: (0, 0))],
        out_shape=[jax.ShapeDtypeStruct((t, D_ATT), F32), jax.ShapeDtypeStruct((t, D_ATT), BF16),
                   jax.ShapeDtypeStruct((t, D_KV), F32), jax.ShapeDtypeStruct((t, D_KV), F32),
                   jax.ShapeDtypeStruct((8, LANES), F32)],
        compiler_params=_cp("arbitrary"))(qr, kr, kr, kr, proj, proj, proj, proj, dmix, sinks)


def _ssd_bwd(dmix, y_ssd, xbc, proj, dt, acs, acst, states, d_skip_l, ssd_norm_w):
    t = xbc.shape[0]
    q = CHUNK
    nc = t // q
    gps = SSD_GROUPS_PER_STEP
    gw, sw = gps * GROUP_W, gps * D_STATE

    def body(dmix_ref, y_ref, z_ref, nw_ref, xs_ref, b_ref, c_ref, dt_ref, acs_ref, acst_ref, st_ref, dsk_ref,
             dz_ref, dxs_ref, db_ref, dc_ref, dacs_ref, ddt_ref, gnw_ref, gdsk_ref, dstate):
        @pl.when(pl.program_id(1) == 0)
        def _():
            dstate[...] = jnp.zeros_like(dstate)
            gnw_ref[...] = jnp.zeros_like(gnw_ref)
            gdsk_ref[...] = jnp.zeros_like(gdsk_ref)

        last_row = lax.broadcasted_iota(jnp.int32, (q, 1), 0) == q - 1
        lane = lax.broadcasted_iota(jnp.int32, (q, LANES), 1)
        lane1 = lax.broadcasted_iota(jnp.int32, (8, LANES), 1)
        for gi in range(gps):
            g = gps * pl.program_id(0) + gi
            cols = slice(GROUP_W * gi, GROUP_W * (gi + 1))
            scols = slice(D_STATE * gi, D_STATE * (gi + 1))
            y = y_ref[:, cols]
            z = z_ref[:, cols]
            sz = _sigmoid(z)
            silu_z = z * sz
            yg = y * silu_z
            rstd = lax.rsqrt(jnp.mean(yg * yg, axis=-1, keepdims=True) + EPS)
            n = yg * rstd
            dout = dmix_ref[:, cols]
            gnw_ref[:, cols] += jnp.sum(dout * n, axis=0, keepdims=True)
            dn = dout * nw_ref[:, cols]
            dyg = rstd * (dn - n * jnp.mean(dn * n, axis=-1, keepdims=True))
            dy = dyg * silu_z
            dz_ref[:, cols] = _bf(dyg * y * (sz * (1.0 + z * (1.0 - sz))))

            x = xs_ref[:, cols]
            bmb, cmb = _bf(b_ref[:, scols]), _bf(c_ref[:, scols])
            hv = _group_heads(g, gi, dt_ref[...], acs_ref[...], acst_ref, dsk_ref[...])
            dec = jnp.exp(jnp.where(_head_tri(q, True), hv.acs - hv.acs_row, NEG))
            dect = jnp.exp(jnp.where(_head_tri(q, False), hv.acs_row - hv.acs, NEG))
            b4 = jnp.concatenate([bmb] * HPG, axis=0)
            c4 = jnp.concatenate([cmb] * HPG, axis=0)
            m_all = _nt(cmb, b4) * dec
            mt_all = _nt(bmb, c4) * dect
            xdt = x * hv.dt
            xdt_b, dyb = _bf(xdt), _bf(dy)
            x_bd, dy_bd = _block_diag(xdt_b), _block_diag(dyb)
            s_prev = st_ref[0, gi]
            spb = _bf(s_prev)
            ds_new = dstate[gi]
            dsb = _bf(ds_new)
            e = jnp.exp(hv.acs)
            elast = jnp.exp(hv.acs_last)
            dte = jnp.exp(hv.acs_last - hv.acs)
            bds = _mm(bmb, dsb)
            dxdt = _mm(_bf(mt_all), dy_bd) + bds * dte
            dm = _nt(dyb, x_bd)
            dmt = _nt(xdt_b, dy_bd)
            dye = _bf(dy * e)
            dc_ref[:, scols] = _mm(_bf(dm * dec), b4) + _nt(dye, spb)
            db_ref[:, scols] = _mm(_bf(dmt * dect), c4) + _nt(_bf(xdt * dte), dsb)
            dstate[gi] = elast * ds_new + _tn(cmb, dye)
            dxs_ref[:, cols] = dxdt * hv.dt + hv.dsk * dy
            ddte_dte = bds * xdt * dte
            dacs_l = dm * m_all - dmt * mt_all + dy * _mm(cmb, spb) * e - ddte_dte
            dlast_l = (jnp.sum(ddte_dte, axis=0, keepdims=True)
                       + jnp.sum(s_prev * ds_new, axis=0, keepdims=True) * elast)
            ddt_l = dxdt * x
            gdsk_l = jnp.sum(dy * x, axis=0, keepdims=True)
            dacs_out = jnp.zeros((q, LANES), F32)
            ddt_out = jnp.zeros((q, LANES), F32)
            gdsk = jnp.zeros((8, LANES), F32)
            for r in range(HPG):
                dacs = _head_sums(dacs_l, r) + jnp.where(last_row, _head_sums(dlast_l, r), 0.0)
                dacs_out = jnp.where(lane == r, dacs, dacs_out)
                ddt_out = jnp.where(lane == r, _head_sums(ddt_l, r), ddt_out)
                gdsk = gdsk + jnp.where(lane1 == r, _head_sums(gdsk_l, r), 0.0)
            dacs_ref[:, LANES * gi:LANES * (gi + 1)] = dacs_out
            ddt_ref[:, LANES * gi:LANES * (gi + 1)] = ddt_out
            gdsk_ref[gi] += gdsk

    rev = lambda c: nc - 1 - c
    wide = pl.BlockSpec((q, gw), lambda g, c: (rev(c), g))
    return _pallas(
        body, name="ssd_bwd", grid=(GROUPS // gps, nc),
        in_specs=[wide, wide, wide, pl.BlockSpec((1, gw), lambda g, c: (0, g)), wide,
                  pl.BlockSpec((q, sw), lambda g, c: (rev(c), D_SSD // sw + g)),
                  pl.BlockSpec((q, sw), lambda g, c: (rev(c), (D_SSD + GROUPS * D_STATE) // sw + g)),
                  pl.BlockSpec((q, LANES), lambda g, c: (rev(c), 0)), pl.BlockSpec((q, LANES), lambda g, c: (rev(c), 0)),
                  pl.BlockSpec((1, gps * GROUPS, q), lambda g, c: (rev(c), g, 0)),
                  pl.BlockSpec((1, gps, D_STATE, GROUP_W), lambda g, c: (rev(c), g, 0, 0)),
                  pl.BlockSpec((1, LANES), lambda g, c: (0, 0))],
        out_specs=[wide, wide,
                   pl.BlockSpec((q, sw), lambda g, c: (rev(c), g)), pl.BlockSpec((q, sw), lambda g, c: (rev(c), g)),
                   pl.BlockSpec((q, gps * LANES), lambda g, c: (rev(c), g)),
                   pl.BlockSpec((q, gps * LANES), lambda g, c: (rev(c), g)),
                   pl.BlockSpec((1, gw), lambda g, c: (0, g)), pl.BlockSpec((gps, 8, LANES), lambda g, c: (g, 0, 0))],
        out_shape=[jax.ShapeDtypeStruct((t, D_SSD), BF16), jax.ShapeDtypeStruct((t, D_SSD), F32),
                   jax.ShapeDtypeStruct((t, GROUPS * D_STATE), F32), jax.ShapeDtypeStruct((t, GROUPS * D_STATE), F32),
                   jax.ShapeDtypeStruct((t, GROUPS * LANES), F32), jax.ShapeDtypeStruct((t, GROUPS * LANES), F32),
                   jax.ShapeDtypeStruct((1, D_SSD), F32), jax.ShapeDtypeStruct((GROUPS, 8, LANES), F32)],
        scratch_shapes=[pltpu.VMEM((gps, D_STATE, GROUP_W), F32)],
        compiler_params=_cp("parallel", "arbitrary"))(dmix, y_ssd, proj, ssd_norm_w, xbc, xbc, xbc, dt, acs, acst,
                                                      states, d_skip_l)


def _dt_bwd(dacs_g, ddt_g, dt, proj, dt_bias_l, a_log_l):
    t = dt.shape[0]
    q = CHUNK
    nc = t // q

    def body(dacs_ref, ddt_ref, dt_ref, raw_ref, bias_ref, alog_ref, draw_ref, ga_ref, gb_ref):
        c = pl.program_id(0)

        @pl.when(c == 0)
        def _():
            ga_ref[...] = jnp.zeros_like(ga_ref)
            gb_ref[...] = jnp.zeros_like(gb_ref)

        lane = lax.broadcasted_iota(jnp.int32, (q, LANES), 1)
        dacs = jnp.zeros((q, LANES), F32)
        ddt = jnp.zeros((q, LANES), F32)
        for g in range(GROUPS):
            mask = (lane >= GROUPS * g) & (lane < GROUPS * g + HPG)
            sl = slice(LANES * g, LANES * (g + 1))
            if g == 0:
                dacs = jnp.where(mask, dacs_ref[:, sl], dacs)
                ddt = jnp.where(mask, ddt_ref[:, sl], ddt)
            else:
                dacs = jnp.where(mask, pltpu.roll(dacs_ref[:, sl], GROUPS * g, 1), dacs)
                ddt = jnp.where(mask, pltpu.roll(ddt_ref[:, sl], GROUPS * g, 1), ddt)
        ri = lax.broadcasted_iota(jnp.int32, (q, q), 0)
        ci = lax.broadcasted_iota(jnp.int32, (q, q), 1)
        triu = (ri <= ci).astype(F32)
        dda = jnp.dot(triu, dacs, preferred_element_type=F32, precision=HI)
        a = -jnp.exp(alog_ref[...])
        dtv = dt_ref[...]
        row = c * q + lax.broadcasted_iota(jnp.int32, (q, LANES), 0)
        used = (lane & (GROUPS - 1)) < HPG
        dsp = jnp.where((row >= PAD_LEAD) & used, dda * a + ddt, 0.0)
        draw = dsp * _sigmoid(raw_ref[...] + bias_ref[...])
        draw_ref[...] = _bf(draw)
        gb_ref[0:1, :] += jnp.sum(draw, axis=0, keepdims=True)
        ga_ref[0:1, :] += jnp.sum(jnp.where(used, dda * dtv, 0.0), axis=0, keepdims=True) * a

    return _pallas(
        body, name="dt_bwd", grid=(nc,),
        in_specs=[pl.BlockSpec((q, GROUPS * LANES), lambda c: (c, 0)), pl.BlockSpec((q, GROUPS * LANES), lambda c: (c, 0)),
                  pl.BlockSpec((q, LANES), lambda c: (c, 0)), pl.BlockSpec((q, LANES), lambda c: (c, ODT // LANES)),
                  pl.BlockSpec((1, LANES), lambda c: (0, 0)), pl.BlockSpec((1, LANES), lambda c: (0, 0))],
        out_specs=[pl.BlockSpec((q, LANES), lambda c: (c, 0)), pl.BlockSpec((8, LANES), lambda c: (0, 0)),
                   pl.BlockSpec((8, LANES), lambda c: (0, 0))],
        out_shape=[jax.ShapeDtypeStruct((t, LANES), BF16), jax.ShapeDtypeStruct((8, LANES), F32),
                   jax.ShapeDtypeStruct((8, LANES), F32)],
        compiler_params=_cp("arbitrary"))(dacs_g, ddt_g, dt, proj, dt_bias_l, a_log_l)


def _conv_bwd(dseg, proj, conv_w, conv_b, col_off, name):
    t, width = dseg.shape
    tc = 128
    off_p = (OXS + col_off) // tc
    off_w = col_off // tc

    def body(d_ref, x_ref, w_ref, b_ref, dx_ref, gw_ref, gb_ref, xp, dup):
        xp[0:8, :] = jnp.zeros((8, tc), F32)
        xp[8:t + 8, :] = x_ref[...]
        w = w_ref[...]
        u = (b_ref[...] + w[3:4, :] * xp[8:t + 8, :] + w[2:3, :] * xp[7:t + 7, :]
             + w[1:2, :] * xp[6:t + 6, :] + w[0:1, :] * xp[5:t + 5, :])
        su = _sigmoid(u)
        du = d_ref[...] * (su * (1.0 + u * (1.0 - su)))
        dup[0:t, :] = du
        dup[t:t + 8, :] = jnp.zeros((8, tc), F32)
        dx_ref[...] = _bf(w[3:4, :] * du + w[2:3, :] * dup[1:t + 1, :] + w[1:2, :] * dup[2:t + 2, :]
                          + w[0:1, :] * dup[3:t + 3, :])
        gb_ref[...] = jnp.sum(du, axis=0, keepdims=True)
        gw_ref[...] = jnp.concatenate(
            [jnp.sum(du * xp[5 + k:t + 5 + k, :], axis=0, keepdims=True) for k in range(CONV_WIDTH)], axis=0)

    return _pallas(
        body, name=name, grid=(width // tc,),
        in_specs=[pl.BlockSpec((t, tc), lambda j: (0, j)), pl.BlockSpec((t, tc), lambda j: (0, j + off_p)),
                  pl.BlockSpec((CONV_WIDTH, tc), lambda j: (0, j + off_w)), pl.BlockSpec((1, tc), lambda j: (0, j + off_w))],
        out_specs=[pl.BlockSpec((t, tc), lambda j: (0, j)), pl.BlockSpec((CONV_WIDTH, tc), lambda j: (0, j)),
                   pl.BlockSpec((1, tc), lambda j: (0, j))],
        out_shape=[jax.ShapeDtypeStruct((t, width), BF16), jax.ShapeDtypeStruct((CONV_WIDTH, width), F32),
                   jax.ShapeDtypeStruct((1, width), F32)],
        scratch_shapes=[pltpu.VMEM((t + 8, tc), F32), pltpu.VMEM((t + 8, tc), F32)],
        compiler_params=_cp("parallel"))(dseg, proj, conv_w, conv_b)


def _dinproj(dproj, w_re, hpad, norm_w, dy_t):
    t, n = dproj.shape
    d = hpad.shape[1]
    tm, tk = _tile(t, 416), 1024
    nk = n // tk

    def body(dp_ref, w_ref, h_ref, nw_ref, dy_ref, dh_ref, gnw_ref, acc):
        i, k = pl.program_id(0), pl.program_id(1)

        @pl.when((i == 0) & (k == 0))
        def _():
            gnw_ref[...] = jnp.zeros_like(gnw_ref)

        @pl.when(k == 0)
        def _():
            acc[...] = jnp.zeros_like(acc)

        acc[...] += _nt(dp_ref[...], w_ref[...])

        @pl.when(k == nk - 1)
        def _():
            h = h_ref[...]
            rstd = lax.rsqrt(jnp.mean(h * h, axis=-1, keepdims=True) + EPS)
            nrm = h * rstd
            dhn = acc[...]
            gnw_ref[...] += jnp.sum(dhn * nrm, axis=0, keepdims=True)
            dn = dhn * nw_ref[...]
            dh_ref[...] = rstd * (dn - nrm * jnp.mean(dn * nrm, axis=-1, keepdims=True)) + dy_ref[...]

    return _pallas(
        body, name="dinproj", grid=(t // tm, nk),
        in_specs=[pl.BlockSpec((tm, tk), lambda i, k: (i, k)), pl.BlockSpec((d, tk), lambda i, k: (0, k)),
                  pl.BlockSpec((tm, d), lambda i, k: (i, 0)), pl.BlockSpec((1, d), lambda i, k: (0, 0)),
                  pl.BlockSpec((tm, d), lambda i, k: (i, 0))],
        out_specs=[pl.BlockSpec((tm, d), lambda i, k: (i, 0)), pl.BlockSpec((1, d), lambda i, k: (0, 0))],
        out_shape=[jax.ShapeDtypeStruct((t, d), F32), jax.ShapeDtypeStruct((1, d), F32)],
        scratch_shapes=[pltpu.VMEM((tm, d), F32)],
        compiler_params=_cp("arbitrary", "arbitrary"))(dproj, w_re, hpad, norm_w, dy_t)


def _spread_heads(v):
    v = jnp.pad(v.reshape(GROUPS, HPG), ((0, 0), (0, GROUPS - HPG))).reshape(1, GROUPS * GROUPS)
    return jnp.pad(v, ((0, 0), (0, LANES - GROUPS * GROUPS)))


def _gather_heads(v):
    return v[0:1, :GROUPS * GROUPS].reshape(GROUPS, GROUPS)[:, :HPG].reshape(1, SSD_HEADS)


def _rope_tables(t):
    half = HEAD_DIM // 2
    inv = ROPE_THETA ** (-jnp.arange(half, dtype=F32) / half)
    pos = (jnp.arange(t) - PAD_LEAD).astype(F32)
    ang = pos[:, None] * inv[None, :]
    cos, sin = jnp.cos(ang), jnp.sin(ang)
    cos_t = jnp.concatenate([cos, cos, cos, cos], axis=1)
    sin_t = jnp.concatenate([-sin, sin, -sin, sin], axis=1)
    return cos_t, sin_t


def _w_in_to_re(w):
    k = w.shape[0]
    dtc = w[:, 6144:6176].reshape(k, GROUPS, HPG)
    dtc = jnp.pad(dtc, ((0, 0), (0, 0), (0, GROUPS - HPG))).reshape(k, GROUPS * GROUPS)
    return jnp.concatenate([w[:, :6144], w[:, 6176:7200], w[:, 7712:8736], w[:, 7200:7456], w[:, 7456:7712], dtc,
                            jnp.zeros((k, DT_SLAB - GROUPS * GROUPS), w.dtype)], axis=1)


def _re_to_w_in(g):
    k = g.shape[0]
    dtc = g[:, ODT:ODT + GROUPS * GROUPS].reshape(k, GROUPS, GROUPS)[:, :, :HPG].reshape(k, SSD_HEADS)
    return jnp.concatenate([g[:, :6144], dtc, g[:, OQ:OQ + D_ATT], g[:, OK:OK + D_KV], g[:, OV:OV + D_KV],
                            g[:, OG:OG + D_ATT]], axis=1)


def _local_step(x, target, meta, norm_pre_w, w_re, conv_w, conv_b, dt_bias, a_log, d_skip, ssd_norm_w, sinks,
                w_out, norm_post_w):
    seq = x.shape[0]
    t = PAD_LEAD + N_META + seq
    hpad = jnp.concatenate([jnp.zeros((PAD_LEAD, D_MODEL), F32), meta, x], axis=0)
    dt_bias_l, a_log_l, d_skip_l = _spread_heads(dt_bias), _spread_heads(a_log), _spread_heads(d_skip)
    cos_t, sin_t = _rope_tables(t)
    sink_v = sinks.reshape(Q_HEADS)

    proj, hn = _inproj(hpad, norm_pre_w, w_re)
    xbc = _conv_fwd(proj, conv_w, conv_b)
    dt, acs, acst = _dt_prep(proj, dt_bias_l, a_log_l)
    y_ssd, ymix, states = _ssd_fwd(xbc, proj, dt, acs, acst, d_skip_l, ssd_norm_w)
    qr, kr = _rope(proj, OQ, proj, OK, cos_t, sin_t)
    amix = _attn_fwd(qr, kr, proj, sink_v)
    out = _outproj(ymix, amix, w_out)
    dout, dy_t, loss_blk, g_norm_post = _post_loss(out, x, target, norm_post_w)

    dmix = _nt_matmul(dout, w_out, "dmix")
    g_w_out = jnp.concatenate([_tn_matmul(ymix, dout, "gw_out_y"), _tn_matmul(amix, dout, "gw_out_a")], axis=0)
    dq_r, dg, dk_r, dv, gs = _attn_bwd(qr, kr, proj, dmix, sink_v)
    dq, dk = _rope(dq_r, 0, dk_r, 0, cos_t, -sin_t)
    dz, dxs, db, dc, dacs_g, ddt_g, g_ssd_norm, gdsk = _ssd_bwd(dmix, y_ssd, xbc, proj, dt, acs, acst, states,
                                                                d_skip_l, ssd_norm_w)
    draw, ga, gb = _dt_bwd(dacs_g, ddt_g, dt, proj, dt_bias_l, a_log_l)
    dxs_p, gcw0, gcb0 = _conv_bwd(dxs, proj, conv_w, conv_b, 0, "conv_bwd_x")
    db_p, gcw1, gcb1 = _conv_bwd(db, proj, conv_w, conv_b, D_SSD, "conv_bwd_b")
    dc_p, gcw2, gcb2 = _conv_bwd(dc, proj, conv_w, conv_b, D_SSD + GROUPS * D_STATE, "conv_bwd_c")
    dproj = jnp.concatenate([dz, dxs_p, db_p, dc_p, dq, dg, dk, _bf(dv), draw,
                             jnp.zeros((t, DT_SLAB - LANES), BF16)], axis=1)
    dh, g_norm_pre = _dinproj(dproj, w_re, hpad, norm_pre_w, dy_t)
    g_w_re = _tn_matmul(hn, dproj, "gw_in")

    gdsk_l = jnp.concatenate([gdsk[g, 0:1, 0:GROUPS] for g in range(GROUPS)], axis=1)
    gdsk_l = jnp.pad(gdsk_l, ((0, 0), (0, LANES - GROUPS * GROUPS)))
    grads = dict(
        meta_tokens=dh[PAD_LEAD:ROW0], norm_pre_w=g_norm_pre, w_re=g_w_re,
        conv_w=jnp.concatenate([gcw0, gcw1, gcw2], axis=1), conv_b=jnp.concatenate([gcb0, gcb1, gcb2], axis=1),
        dt_bias=_gather_heads(gb), a_log=_gather_heads(ga), d_skip=_gather_heads(gdsk_l), ssd_norm_w=g_ssd_norm,
        attn_sinks=gs[0:1, :Q_HEADS], w_out=g_w_out, norm_post_w=g_norm_post)
    return loss_blk[0, 0], dh[ROW0:], grads


ANY = pl.BlockSpec(memory_space=pl.ANY)
MESH = pl.DeviceIdType.MESH
GATHER_CHUNKS = 4
PAIR_CHUNKS = 16
JOIN_CHUNKS = 8


def _rcopy(src, dst, ssem, rsem, dev):
    return pltpu.make_async_remote_copy(src_ref=src, dst_ref=dst, send_sem=ssem, recv_sem=rsem, device_id=dev,
                                        device_id_type=MESH)


def _place():
    x, y, c = lax.axis_index("x"), lax.axis_index("y"), lax.axis_index("c")
    chips = [(1 - x, y), (x, 1 - y), (1 - x, 1 - y)]
    return x, y, c, chips


def _gather_shards(shard, name):
    r, n = shard.shape
    hr = r // 2
    kc = GATHER_CHUNKS
    ch = hr // kc
    assert ch * kc == hr and ch % 16 == 0

    def body(x_ref, out_ref, send_sems, recv_sems, local_sems):
        x, y, c, chips = _place()
        me = 2 * x + y

        def piece(chip, hc, k):
            return out_ref.at[chip, pl.ds(hc * hr + k * ch, ch), :]

        local = [pltpu.make_async_copy(x_ref.at[pl.ds(k * ch, ch), :], out_ref.at[me, pl.ds(k * ch, ch), :],
                                       local_sems.at[k]) for k in range(2 * kc)]
        for cp in local:
            cp.start()
        first = [_rcopy(x_ref.at[pl.ds(c * hr + k * ch, ch), :], piece(me, c, k), send_sems.at[j * kc + k],
                        recv_sems.at[j * kc + k], (*chip, c)) for j, chip in enumerate(chips) for k in range(kc)]
        for cp in first:
            cp.start()
        passed = []
        for j, chip in enumerate(chips):
            cj = 2 * chip[0] + chip[1]
            for k in range(kc):
                s = j * kc + k
                _rcopy(piece(cj, c, k), piece(cj, c, k), send_sems.at[s], recv_sems.at[s], (*chip, c)).wait_recv()
                fw = _rcopy(piece(cj, c, k), piece(cj, c, k), send_sems.at[3 * kc + s], recv_sems.at[3 * kc + s],
                            (x, y, 1 - c))
                fw.start()
                passed.append(fw)
        for j, chip in enumerate(chips):
            cj = 2 * chip[0] + chip[1]
            for k in range(kc):
                s = 3 * kc + j * kc + k
                _rcopy(piece(cj, 1 - c, k), piece(cj, 1 - c, k), send_sems.at[s], recv_sems.at[s],
                       (x, y, 1 - c)).wait_recv()
        for cp in first + passed:
            cp.wait_send()
        for cp in local:
            cp.wait()

    return _pallas(
        body, name=name, in_specs=[ANY], out_specs=ANY,
        out_shape=jax.ShapeDtypeStruct((N_SHARD, r, n), shard.dtype),
        scratch_shapes=[pltpu.SemaphoreType.DMA((6 * kc,)), pltpu.SemaphoreType.DMA((6 * kc,)),
                        pltpu.SemaphoreType.DMA((2 * kc,))])(shard)


def _pair_send(g4, name):
    _, r, n = g4.shape
    hr = r // 2
    kc = PAIR_CHUNKS
    ch = hr // kc
    assert ch * kc == hr and ch % 8 == 0

    def body(g_ref, got_ref, send_sems, recv_sems):
        x, y, c, _ = _place()
        cps = [_rcopy(g_ref.at[:, pl.ds((1 - c) * hr + k * ch, ch), :], got_ref.at[:, pl.ds(k * ch, ch), :],
                      send_sems.at[k], recv_sems.at[k], (x, y, 1 - c)) for k in range(kc)]
        for cp in cps:
            cp.start()
        for cp in cps:
            cp.wait()

    return _pallas(
        body, name=name, in_specs=[ANY], out_specs=ANY, out_shape=jax.ShapeDtypeStruct((N_SHARD, hr, n), F32),
        scratch_shapes=[pltpu.SemaphoreType.DMA((kc,)), pltpu.SemaphoreType.DMA((kc,))])(g4)


def _pair_add(g4, got, core, name):
    k, r, n = g4.shape
    hr = r // 2
    tr = _tile(hr, 256)
    nt = hr // tr

    def body(core_ref, a_ref, b_ref, o_ref):
        o_ref[...] = _bf(a_ref[...] + b_ref[...])

    spec = pl.BlockSpec((1, tr, n), lambda j, i, core_ref: (j, i, 0))
    return _pallas(
        body, name=name,
        grid_spec=pltpu.PrefetchScalarGridSpec(
            num_scalar_prefetch=1, grid=(k, nt),
            in_specs=[pl.BlockSpec((1, tr, n), lambda j, i, core_ref: (j, core_ref[0] * nt + i, 0)), spec],
            out_specs=spec),
        out_shape=jax.ShapeDtypeStruct((k, hr, n), BF16), compiler_params=_cp("parallel", "parallel"))(core, g4, got)


def _chip_exchange(ga, name):
    _, hr, n = ga.shape
    kc = GATHER_CHUNKS
    ch = hr // kc
    assert ch * kc == hr and ch % 16 == 0

    def body(g_ref, got_ref, send_sems, recv_sems):
        x, y, c, chips = _place()
        cps = [_rcopy(g_ref.at[2 * chip[0] + chip[1], pl.ds(k * ch, ch), :], got_ref.at[j, pl.ds(k * ch, ch), :],
                      send_sems.at[j * kc + k], recv_sems.at[j * kc + k], (*chip, c))
               for j, chip in enumerate(chips) for k in range(kc)]
        for cp in cps:
            cp.start()
        for cp in cps:
            cp.wait()

    return _pallas(
        body, name=name, in_specs=[ANY], out_specs=ANY, out_shape=jax.ShapeDtypeStruct((3, hr, n), ga.dtype),
        scratch_shapes=[pltpu.SemaphoreType.DMA((3 * kc,)), pltpu.SemaphoreType.DMA((3 * kc,))])(ga)


def _chip_sum(ga, got, place, name):
    _, hr, n = ga.shape
    tr = _tile(hr, 256)
    nt = hr // tr

    def body(place_ref, own_ref, got_ref, o_ref):
        acc = own_ref[0].astype(F32)
        for j in range(3):
            acc = acc + got_ref[j].astype(F32)
        o_ref[...] = acc

    return _pallas(
        body, name=name,
        grid_spec=pltpu.PrefetchScalarGridSpec(
            num_scalar_prefetch=1, grid=(nt,),
            in_specs=[pl.BlockSpec((1, tr, n), lambda i, place_ref: (place_ref[0], i, 0)),
                      pl.BlockSpec((3, tr, n), lambda i, place_ref: (0, i, 0))],
            out_specs=pl.BlockSpec((tr, n), lambda i, place_ref: (place_ref[1] * nt + i, 0))),
        out_shape=jax.ShapeDtypeStruct((2 * hr, n), F32), compiler_params=_cp("parallel"))(place, ga, got)


def _pair_join(buf, name):
    r, n = buf.shape
    hr = r // 2
    kc = JOIN_CHUNKS
    ch = hr // kc
    assert ch * kc == hr and ch % 8 == 0

    def body(in_ref, out_ref, send_sems, recv_sems):
        x, y, c, _ = _place()
        cps = [_rcopy(out_ref.at[pl.ds(c * hr + k * ch, ch), :], out_ref.at[pl.ds(c * hr + k * ch, ch), :],
                      send_sems.at[k], recv_sems.at[k], (x, y, 1 - c)) for k in range(kc)]
        for cp in cps:
            cp.start()
        for k in range(kc):
            rows = out_ref.at[pl.ds((1 - c) * hr + k * ch, ch), :]
            _rcopy(rows, rows, send_sems.at[k], recv_sems.at[k], (x, y, 1 - c)).wait_recv()
        for cp in cps:
            cp.wait_send()

    return _pallas(
        body, name=name, in_specs=[ANY], out_specs=ANY, out_shape=jax.ShapeDtypeStruct((r, n), F32),
        input_output_aliases={0: 0},
        scratch_shapes=[pltpu.SemaphoreType.DMA((kc,)), pltpu.SemaphoreType.DMA((kc,))])(buf)


def _reduce_to_shard(g4, place, tag):
    got = _pair_send(g4, tag + "_pair_send")
    ga = _pair_add(g4, got, place[1:2], tag + "_pair_add")
    slabs = _chip_exchange(ga, tag + "_chip_exchange")
    return _pair_join(_chip_sum(ga, slabs, place, tag + "_chip_sum"), tag + "_pair_join")


def _allreduce_small(p, name):
    rows, n = p.shape
    ndev = 8

    def body(p_ref, out_ref, slots, send_sems, recv_sems):
        x, y, c, _ = _place()
        my = 4 * x + 2 * y + c
        slots[my] = p_ref[...]
        cps = []
        for k in range(1, ndev):
            kx, ky, kc = (k >> 2) & 1, (k >> 1) & 1, k & 1
            peer = (x ^ kx, y ^ ky, c ^ kc)
            cp = _rcopy(p_ref, slots.at[my], send_sems.at[k - 1], recv_sems.at[k - 1], peer)
            cp.start()
            cps.append(cp)
        for k in range(1, ndev):
            _rcopy(p_ref, slots.at[my ^ k], send_sems.at[k - 1], recv_sems.at[k - 1], (x, y, c)).wait_recv()
        for cp in cps:
            cp.wait_send()
        acc = slots[0]
        for j in range(1, ndev):
            acc = acc + slots[j]
        out_ref[...] = acc

    vm = pl.BlockSpec(memory_space=pltpu.VMEM)
    return _pallas(
        body, name=name, in_specs=[vm], out_specs=vm, out_shape=jax.ShapeDtypeStruct((rows, n), F32),
        scratch_shapes=[pltpu.VMEM((ndev, rows, n), F32), pltpu.SemaphoreType.DMA((ndev - 1,)),
                        pltpu.SemaphoreType.DMA((ndev - 1,))])(p)


def _adamw(w, g, m, v, name):
    r, n = w.shape
    tr = _tile(r, 256, 8)
    c1 = 1.0 / (1.0 - ADAM_B1 ** ADAM_STEP)
    c2 = 1.0 / (1.0 - ADAM_B2 ** ADAM_STEP)

    def body(w_ref, g_ref, m_ref, v_ref, d_ref, mo_ref, vo_ref):
        gv = g_ref[...]
        mn = ADAM_B1 * m_ref[...] + (1.0 - ADAM_B1) * gv
        vn = ADAM_B2 * v_ref[...] + (1.0 - ADAM_B2) * (gv * gv)
        d_ref[...] = -ADAM_LR * ((mn * c1) / (jnp.sqrt(vn * c2) + ADAM_EPS) + ADAM_WD * w_ref[...])
        mo_ref[...] = mn
        vo_ref[...] = vn

    spec = pl.BlockSpec((tr, n), lambda i: (i, 0))
    shp = jax.ShapeDtypeStruct((r, n), F32)
    return _pallas(body, name=name, grid=(r // tr,), in_specs=[spec] * 4, out_specs=[spec] * 3, out_shape=[shp] * 3,
                   compiler_params=_cp("parallel"))(w, g, m, v)


PACK_W = 1024
SMALL_REPL = ("norm_pre_w", "conv_b", "ssd_norm_w", "norm_post_w")
SMALL_HEAD = ("dt_bias", "a_log", "d_skip", "attn_sinks")


def _rows(a):
    return a.reshape(-1, PACK_W)


def _head_row(vals, extra=None):
    parts = [vals[n].reshape(1, -1) for n in SMALL_HEAD]
    if extra is not None:
        parts.append(extra.reshape(1, 1))
    row = jnp.concatenate(parts, axis=1)
    return jnp.pad(row, ((0, 0), (0, PACK_W - row.shape[1])))


def _pad_rows(a, rows):
    return jnp.pad(a, ((0, rows - a.shape[0]), (0, 0)))


def _pack_repl(vals, extra=None):
    body = jnp.concatenate([_rows(vals[n]) for n in SMALL_REPL] + [_head_row(vals, extra)], axis=0)
    return _pad_rows(body, 16)


def _unpack_repl(buf):
    out, r = {}, 0
    for n, k in zip(SMALL_REPL, (2, 4, 2, 2)):
        out[n] = buf[r:r + k].reshape(1, k * PACK_W)
        r += k
    col = 0
    for n, k in zip(SMALL_HEAD, (32, 32, 32, 16)):
        out[n] = buf[r:r + 1, col:col + k]
        col += k
    return out, buf[r, col]


def kernel(x, meta_tokens, norm_pre_w, w_in, conv_w, conv_b, dt_bias, a_log, d_skip, ssd_norm_w, attn_sinks, w_out, norm_post_w, loss_target, m_meta_tokens, m_norm_pre_w, m_w_in, m_conv_w, m_conv_b, m_dt_bias, m_a_log, m_d_skip, m_ssd_norm_w, m_attn_sinks, m_w_out, m_norm_post_w, v_meta_tokens, v_norm_pre_w, v_w_in, v_conv_w, v_conv_b, v_dt_bias, v_a_log, v_d_skip, v_ssd_norm_w, v_attn_sinks, v_w_out, v_norm_post_w):
    names = ("meta_tokens", "norm_pre_w", "w_in", "conv_w", "conv_b", "dt_bias", "a_log", "d_skip", "ssd_norm_w",
             "attn_sinks", "w_out", "norm_post_w")
    w = dict(zip(names, (meta_tokens, norm_pre_w, w_in, conv_w, conv_b, dt_bias, a_log, d_skip, ssd_norm_w, attn_sinks,
                         w_out, norm_post_w)))
    m = dict(zip(names, (m_meta_tokens, m_norm_pre_w, m_w_in, m_conv_w, m_conv_b, m_dt_bias, m_a_log, m_d_skip,
                         m_ssd_norm_w, m_attn_sinks, m_w_out, m_norm_post_w)))
    v = dict(zip(names, (v_meta_tokens, v_norm_pre_w, v_w_in, v_conv_w, v_conv_b, v_dt_bias, v_a_log, v_d_skip,
                         v_ssd_norm_w, v_attn_sinks, v_w_out, v_norm_post_w)))
    cx, cy, cc = lax.axis_index("x"), lax.axis_index("y"), lax.axis_index("c")
    chip = 2 * cx + cy
    meta_cols = D_MODEL // N_SHARD
    conv_cols = D_CONV // N_SHARD

    w_in_all = _gather_shards(_bf(w_in[0]), "gather_w_in")
    w_re = _w_in_to_re(jnp.transpose(w_in_all, (1, 0, 2)).reshape(D_MODEL, D_IN))
    w_out_full = _gather_shards(_bf(w_out[0]), "gather_w_out").reshape(D_MIX, D_MODEL)
    conv_z = lax.dynamic_update_slice(jnp.zeros((CONV_WIDTH, D_CONV), F32), conv_w[0], (0, chip * conv_cols))
    meta_z = lax.dynamic_update_slice(jnp.zeros((N_META, D_MODEL), F32), meta_tokens, (0, chip * meta_cols))
    small = jnp.concatenate([_rows(conv_z), _rows(meta_z)], axis=0)
    small = _allreduce_small(jnp.where(cc == 0, small, 0.0), "gather_small")
    conv_full = small[0:16].reshape(CONV_WIDTH, D_CONV)
    meta_full = small[16:48].reshape(N_META, D_MODEL)

    loss_dev, grad_x, g = _local_step(x[0], loss_target[0], meta_full, norm_pre_w, w_re, conv_full, conv_b, dt_bias,
                                      a_log, d_skip, ssd_norm_w, attn_sinks, w_out_full, norm_post_w)

    g_in4 = jnp.transpose(_re_to_w_in(g["w_re"]).reshape(D_MODEL, N_SHARD, W_IN_SHARD), (1, 0, 2))
    place = jnp.stack([chip, cc]).astype(jnp.int32)
    g_w_in = _reduce_to_shard(g_in4, place, "gw_in")
    g_w_out = _reduce_to_shard(g["w_out"].reshape(N_SHARD, W_OUT_SHARD, D_MODEL), place, "gw_out")
    packed = jnp.concatenate([_rows(g["conv_w"]), _rows(g["meta_tokens"]), _pack_repl(g, loss_dev)], axis=0)
    red = _allreduce_small(packed, "reduce_small")
    g_conv_full = red[0:16].reshape(CONV_WIDTH, D_CONV)
    g_meta_full = red[16:48].reshape(N_META, D_MODEL)
    g_small, loss = _unpack_repl(red[48:64])
    grads = dict(g_small)
    grads["w_in"] = g_w_in
    grads["w_out"] = g_w_out
    grads["conv_w"] = lax.dynamic_slice(g_conv_full, (0, chip * conv_cols), (CONV_WIDTH, conv_cols))
    grads["meta_tokens"] = lax.dynamic_slice(g_meta_full, (0, chip * meta_cols), (N_META, meta_cols))

    upd = {}
    upd["w_in"] = _adamw(w_in[0], g_w_in, m_w_in[0], v_w_in[0], "adamw_w_in")
    upd["w_out"] = _adamw(w_out[0], g_w_out, m_w_out[0], v_w_out[0], "adamw_w_out")

    def pack_small(vals, conv, meta):
        return jnp.concatenate([_pad_rows(conv.reshape(CONV_WIDTH, conv_cols), 8), _rows(meta), _pack_repl(vals)], axis=0)

    sm = _adamw(pack_small(w, w["conv_w"], w["meta_tokens"]), pack_small(grads, grads["conv_w"], grads["meta_tokens"]),
                pack_small(m, m["conv_w"], m["meta_tokens"]), pack_small(v, v["conv_w"], v["meta_tokens"]),
                "adamw_small")
    for n in names:
        if n not in ("w_in", "w_out"):
            upd[n] = [None, None, None]
    for k, buf in enumerate(sm):
        upd["conv_w"][k] = buf[0:CONV_WIDTH]
        upd["meta_tokens"][k] = buf[8:16].reshape(N_META, meta_cols)
        rest, _ = _unpack_repl(buf[16:32])
        for n in SMALL_REPL + SMALL_HEAD:
            upd[n][k] = rest[n]

    def shaped(n, a):
        return a.reshape(w[n].shape)

    outs = [loss, grad_x[None]]
    outs += [shaped(n, grads[n]) for n in names]
    for k in range(3):
        outs += [shaped(n, upd[n][k]) for n in names]
    return tuple(outs)
```

```python
import functools

import jax
import jax.numpy as jnp
from jax import lax
from jax.experimental import pallas as pl
from jax.experimental.pallas import tpu as pltpu

F32 = jnp.float32
BF16 = jnp.bfloat16

D_MODEL = 2048
CHUNK = 64
N_META = 16
PAD_LEAD = CHUNK - N_META
ROW0 = PAD_LEAD + N_META
EPS = 1e-6
SSD_HEADS = 32
HEAD_DIM = 64
GROUPS = 8
HPG = SSD_HEADS // GROUPS
D_STATE = 128
D_SSD = 2048
GROUP_W = D_SSD // GROUPS
CONV_WIDTH = 4
D_CONV = 4096
Q_HEADS = 16
KV_HEADS = 4
REP = Q_HEADS // KV_HEADS
D_ATT = 1024
D_KV = 256
BAND_CHUNKS = 3
ROPE_THETA = 10000.0
D_MIX = D_SSD + D_ATT
D_IN = 8736
N_SHARD = 4
W_IN_SHARD = D_IN // N_SHARD
W_OUT_SHARD = D_MIX // N_SHARD

OZ, OXS, OB, OC, OQ, OG, OK, OV, ODT = 0, 2048, 4096, 5120, 6144, 7168, 8192, 8448, 8704
DT_SLAB = 512
N_RE = ODT + DT_SLAB
LANES = 128

ADAM_LR, ADAM_B1, ADAM_B2, ADAM_EPS, ADAM_WD, ADAM_STEP = 0.001, 0.9, 0.999, 1e-08, 0.01, 10

SSD_GROUPS_PER_STEP = 4
VMEM_LIMIT = 52 * 1024 * 1024
NEG = -1e30
HI = lax.Precision.HIGHEST


def _pallas(body, **kw):
    return pl.pallas_call(body, **kw)


def _cp(*sem):
    return pltpu.CompilerParams(dimension_semantics=sem, vmem_limit_bytes=VMEM_LIMIT)


def _tile(n, cap, mult=16):
    best = None
    for d in range(mult, min(n, cap) + 1, mult):
        if n % d == 0:
            best = d
    assert best is not None, (n, cap)
    return best


def _nt(a, b):
    return lax.dot_general(a, b, (((1,), (1,)), ((), ())), preferred_element_type=F32)


def _tn(a, b):
    return lax.dot_general(a, b, (((0,), (0,)), ((), ())), preferred_element_type=F32)


def _mm(a, b):
    return jnp.dot(a, b, preferred_element_type=F32)


def _sigmoid(x):
    return 1.0 / (1.0 + jnp.exp(-x))


def _bf(x):
    return x.astype(BF16)


def _inproj(hpad, norm_w, w_re, w_out_shard):
    t, d = hpad.shape
    n = w_re.shape[1]
    tm, tn = _tile(t, 832), 512
    ni, nj = t // tm, n // tn
    r_out, n_out = w_out_shard.shape
    kc = GATHER_CHUNKS

    def body(h_ref, nw_ref, w_ref, ws_ref, proj_ref, hn_ref, wall_ref, hn_s, send_sems, recv_sems, local_sems):
        i, j = pl.program_id(0), pl.program_id(1)
        start, forward, finish = _gather_plan(ws_ref, wall_ref, send_sems, recv_sems, local_sems, r_out // 2, kc)
        pl.when((i == 0) & (j == 0))(start)
        pl.when((i == ni // 2) & (j == 0))(forward)

        @pl.when(j == 0)
        def _():
            h = h_ref[...]
            ms = jnp.mean(h * h, axis=-1, keepdims=True)
            hn = _bf(h * lax.rsqrt(ms + EPS) * nw_ref[...])
            hn_s[...] = hn
            hn_ref[...] = hn
        proj_ref[...] = _mm(hn_s[...], w_ref[...])
        pl.when((i == ni - 1) & (j == nj - 1))(finish)

    return _pallas(
        body, name="inproj", grid=(ni, nj),
        in_specs=[pl.BlockSpec((tm, d), lambda i, j: (i, 0)), pl.BlockSpec((1, d), lambda i, j: (0, 0)),
                  pl.BlockSpec((d, tn), lambda i, j: (0, j)), ANY],
        out_specs=[pl.BlockSpec((tm, tn), lambda i, j: (i, j)), pl.BlockSpec((tm, d), lambda i, j: (i, 0)), ANY],
        out_shape=[jax.ShapeDtypeStruct((t, n), F32), jax.ShapeDtypeStruct((t, d), BF16),
                   jax.ShapeDtypeStruct((N_SHARD, r_out, n_out), w_out_shard.dtype)],
        scratch_shapes=[pltpu.VMEM((tm, d), BF16), pltpu.SemaphoreType.DMA((6 * kc,)), pltpu.SemaphoreType.DMA((6 * kc,)),
                        pltpu.SemaphoreType.DMA((2 * kc,))],
        compiler_params=_cp("arbitrary", "arbitrary"))(hpad, norm_w, w_re, w_out_shard)


def _conv_fwd(proj, conv_w, conv_b):
    t = proj.shape[0]
    tc = 256
    off = OXS // tc

    def body(x_ref, w_ref, b_ref, o_ref, xp):
        xp[0:8, :] = jnp.zeros((8, tc), F32)
        xp[8:t + 8, :] = x_ref[...]
        w = w_ref[...]
        u = (b_ref[...] + w[3:4, :] * xp[8:t + 8, :] + w[2:3, :] * xp[7:t + 7, :]
             + w[1:2, :] * xp[6:t + 6, :] + w[0:1, :] * xp[5:t + 5, :])
        o_ref[...] = u * _sigmoid(u)

    return _pallas(
        body, name="conv_fwd", grid=(D_CONV // tc,),
        in_specs=[pl.BlockSpec((t, tc), lambda j: (0, j + off)), pl.BlockSpec((CONV_WIDTH, tc), lambda j: (0, j)),
                  pl.BlockSpec((1, tc), lambda j: (0, j))],
        out_specs=pl.BlockSpec((t, tc), lambda j: (0, j)),
        out_shape=jax.ShapeDtypeStruct((t, D_CONV), F32),
        scratch_shapes=[pltpu.VMEM((t + 8, tc), F32)],
        compiler_params=_cp("parallel"))(proj, conv_w, conv_b)


def _softplus(u):
    e = jnp.exp(-jnp.abs(u))
    w = 1.0 + e
    l1p = jnp.where(w == 1.0, e, jnp.log(w) * (e / jnp.where(w == 1.0, 1.0, w - 1.0)))
    return jnp.maximum(u, 0.0) + l1p


def _dt_prep(proj, dt_bias_l, a_log_l):
    t = proj.shape[0]
    nc = t // CHUNK
    q = CHUNK

    def body(raw_ref, bias_ref, alog_ref, dt_ref, acs_ref, acst_ref):
        c = pl.program_id(0)
        sp = _softplus(raw_ref[...] + bias_ref[...])
        row = c * q + lax.broadcasted_iota(jnp.int32, (q, LANES), 0)
        dt = jnp.where(row >= PAD_LEAD, sp, 0.0)
        da = dt * (-jnp.exp(alog_ref[...]))
        ri = lax.broadcasted_iota(jnp.int32, (q, q), 0)
        ci = lax.broadcasted_iota(jnp.int32, (q, q), 1)
        tri = (ri >= ci).astype(F32)
        acs = jnp.dot(tri, da, preferred_element_type=F32, precision=HI)
        dt_ref[...] = dt
        acs_ref[...] = acs
        acst_ref[0] = acs.T

    return _pallas(
        body, name="dt_prep", grid=(nc,),
        in_specs=[pl.BlockSpec((q, LANES), lambda c: (c, ODT // LANES)), pl.BlockSpec((1, LANES), lambda c: (0, 0)),
                  pl.BlockSpec((1, LANES), lambda c: (0, 0))],
        out_specs=[pl.BlockSpec((q, LANES), lambda c: (c, 0)), pl.BlockSpec((q, LANES), lambda c: (c, 0)),
                   pl.BlockSpec((1, LANES, q), lambda c: (c, 0, 0))],
        out_shape=[jax.ShapeDtypeStruct((t, LANES), F32), jax.ShapeDtypeStruct((t, LANES), F32),
                   jax.ShapeDtypeStruct((nc, LANES, q), F32)],
        compiler_params=_cp("parallel"))(proj, dt_bias_l, a_log_l)


def _head_cols(blk, idx):
    lane = lax.broadcasted_iota(jnp.int32, blk.shape, 1)
    return jnp.sum(jnp.where(lane == idx, blk, 0.0), axis=1, keepdims=True)


class _HeadVals:
    pass


def _lane_head(shape):
    return lax.broadcasted_iota(jnp.int32, shape, len(shape) - 1) >> 6


def _group_heads(g, gi, dtb, acsb, acst_ref, dskb):
    q = dtb.shape[0]
    hv = _HeadVals()
    lh = _lane_head((1, GROUP_W))
    hv.dt = jnp.zeros((q, GROUP_W), F32)
    hv.acs = jnp.zeros((q, GROUP_W), F32)
    hv.acs_last = jnp.zeros((1, GROUP_W), F32)
    hv.dsk = jnp.zeros((1, GROUP_W), F32)
    rows = []
    for r in range(HPG):
        idx = GROUPS * g + r
        sel = lh == r
        acs_r = acst_ref[0, GROUPS * gi + r:GROUPS * gi + r + 1, :]
        rows.append(acs_r)
        hv.dt = jnp.where(sel, _head_cols(dtb, idx), hv.dt)
        hv.acs = jnp.where(sel, _head_cols(acsb, idx), hv.acs)
        hv.acs_last = jnp.where(sel, acs_r[:, q - 1:q], hv.acs_last)
        hv.dsk = jnp.where(sel, _head_cols(dskb, idx), hv.dsk)
    hv.acs_row = jnp.concatenate(rows, axis=1)
    return hv


def _head_tri(q, lower):
    ri = lax.broadcasted_iota(jnp.int32, (q, GROUP_W), 0)
    li = lax.broadcasted_iota(jnp.int32, (q, GROUP_W), 1) & (HEAD_DIM - 1)
    return ri >= li if lower else ri <= li


def _block_diag(v):
    rb = lax.broadcasted_iota(jnp.int32, (GROUP_W, GROUP_W), 0) >> 6
    cb = lax.broadcasted_iota(jnp.int32, (GROUP_W, GROUP_W), 1) >> 6
    return jnp.where(rb == cb, jnp.concatenate([v] * HPG, axis=0), jnp.zeros((), v.dtype))


def _head_sums(v, r):
    return jnp.sum(jnp.where(_lane_head((1, GROUP_W)) == r, v, 0.0), axis=1, keepdims=True)


def _ssd_fwd(xbc, proj, dt, acs, acst, d_skip_l, ssd_norm_w):
    t = xbc.shape[0]
    q = CHUNK
    nc = t // q

    gps = SSD_GROUPS_PER_STEP
    gw, sw = gps * GROUP_W, gps * D_STATE

    def body(xs_ref, b_ref, c_ref, dt_ref, acs_ref, acst_ref, z_ref, dsk_ref, nw_ref,
             y_ref, ymix_ref, st_ref, state):
        @pl.when(pl.program_id(1) == 0)
        def _():
            state[...] = jnp.zeros_like(state)

        for gi in range(gps):
            g = gps * pl.program_id(0) + gi
            cols = slice(GROUP_W * gi, GROUP_W * (gi + 1))
            x = xs_ref[:, cols]
            bmb = _bf(b_ref[:, D_STATE * gi:D_STATE * (gi + 1)])
            cmb = _bf(c_ref[:, D_STATE * gi:D_STATE * (gi + 1)])
            hv = _group_heads(g, gi, dt_ref[...], acs_ref[...], acst_ref, dsk_ref[...])
            decay = jnp.exp(jnp.where(_head_tri(q, True), hv.acs - hv.acs_row, NEG))
            m_all = _bf(_nt(cmb, jnp.concatenate([bmb] * HPG, axis=0)) * decay)
            xdt = x * hv.dt
            s_prev = state[gi]
            st_ref[0, gi] = s_prev
            y = (_mm(m_all, _block_diag(_bf(xdt))) + _mm(cmb, _bf(s_prev)) * jnp.exp(hv.acs) + hv.dsk * x)
            state[gi] = jnp.exp(hv.acs_last) * s_prev + _tn(bmb, _bf(xdt * jnp.exp(hv.acs_last - hv.acs)))
            y_ref[:, cols] = y
            z = z_ref[:, cols]
            yg = y * (z * _sigmoid(z))
            ms = jnp.mean(yg * yg, axis=-1, keepdims=True)
            ymix_ref[:, cols] = _bf(yg * lax.rsqrt(ms + EPS) * nw_ref[:, cols])

    return _pallas(
        body, name="ssd_fwd", grid=(GROUPS // gps, nc),
        in_specs=[pl.BlockSpec((q, gw), lambda g, c: (c, g)),
                  pl.BlockSpec((q, sw), lambda g, c: (c, D_SSD // sw + g)),
                  pl.BlockSpec((q, sw), lambda g, c: (c, (D_SSD + GROUPS * D_STATE) // sw + g)),
                  pl.BlockSpec((q, LANES), lambda g, c: (c, 0)), pl.BlockSpec((q, LANES), lambda g, c: (c, 0)),
                  pl.BlockSpec((1, gps * GROUPS, q), lambda g, c: (c, g, 0)),
                  pl.BlockSpec((q, gw), lambda g, c: (c, g)),
                  pl.BlockSpec((1, LANES), lambda g, c: (0, 0)), pl.BlockSpec((1, gw), lambda g, c: (0, g))],
        out_specs=[pl.BlockSpec((q, gw), lambda g, c: (c, g)), pl.BlockSpec((q, gw), lambda g, c: (c, g)),
                   pl.BlockSpec((1, gps, D_STATE, GROUP_W), lambda g, c: (c, g, 0, 0))],
        out_shape=[jax.ShapeDtypeStruct((t, D_SSD), F32), jax.ShapeDtypeStruct((t, D_SSD), BF16),
                   jax.ShapeDtypeStruct((nc, GROUPS, D_STATE, GROUP_W), F32)],
        scratch_shapes=[pltpu.VMEM((gps, D_STATE, GROUP_W), F32)],
        compiler_params=_cp("parallel", "arbitrary"))(xbc, xbc, xbc, dt, acs, acst, proj, d_skip_l, ssd_norm_w)


def _swap_halves(v):
    lane = lax.broadcasted_iota(jnp.int32, v.shape, 1)
    return jnp.where((lane & (HEAD_DIM - 1)) < HEAD_DIM // 2, pltpu.roll(v, LANES - HEAD_DIM // 2, 1),
                     pltpu.roll(v, HEAD_DIM // 2, 1))


def _rope(qsrc, q_off, ksrc, k_off, cos_t, sin_t):
    t = qsrc.shape[0]
    tr = _tile(t, 832)

    def body(q_ref, k_ref, cos_ref, sin_ref, qo_ref, ko_ref):
        cs = cos_ref[...]
        sn = sin_ref[...]
        for src, dst, width in ((q_ref, qo_ref, D_ATT), (k_ref, ko_ref, D_KV)):
            for s in range(width // LANES):
                v = src[:, LANES * s:LANES * (s + 1)].astype(F32)
                dst[:, LANES * s:LANES * (s + 1)] = _bf(v * cs + _swap_halves(v) * sn)

    return _pallas(
        body, name="rope", grid=(t // tr,),
        in_specs=[pl.BlockSpec((tr, D_ATT), lambda i: (i, q_off // D_ATT)),
                  pl.BlockSpec((tr, D_KV), lambda i: (i, k_off // D_KV)),
                  pl.BlockSpec((tr, LANES), lambda i: (i, 0)), pl.BlockSpec((tr, LANES), lambda i: (i, 0))],
        out_specs=[pl.BlockSpec((tr, D_ATT), lambda i: (i, 0)), pl.BlockSpec((tr, D_KV), lambda i: (i, 0))],
        out_shape=[jax.ShapeDtypeStruct((t, D_ATT), BF16), jax.ShapeDtypeStruct((t, D_KV), BF16)],
        compiler_params=_cp("parallel"))(qsrc, ksrc, cos_t, sin_t)


def _band_specs(width, col_block):
    return [pl.BlockSpec((CHUNK, width), functools.partial(lambda c, j: (jnp.maximum(c - j, 0), col_block), j=j))
            for j in (2, 1, 0)]


def _attn_probs(qh, kb, sink_col, valid):
    s = _nt(qh, kb) * (HEAD_DIM ** -0.5)
    s = jnp.where(valid, s, NEG)
    m = jnp.maximum(jnp.max(s, axis=1, keepdims=True), sink_col)
    p = jnp.exp(s - m)
    psink = jnp.exp(sink_col - m)
    inv = 1.0 / (jnp.sum(p, axis=1, keepdims=True) + psink)
    return p * inv, psink * inv


def _attn_operands(c, q_ref, k_refs, v_refs, sink_ref, h):
    q = q_ref[...]
    qh = jnp.concatenate([q[:, HEAD_DIM * (REP * h + r):HEAD_DIM * (REP * h + r + 1)] for r in range(REP)], axis=0)
    kb = jnp.concatenate([k[:, HEAD_DIM * h:HEAD_DIM * (h + 1)] for k in k_refs], axis=0)
    vb = jnp.concatenate([_bf(v[:, HEAD_DIM * h:HEAD_DIM * (h + 1)]) for v in v_refs], axis=0)
    rows = lax.broadcasted_iota(jnp.int32, (REP * CHUNK, 1), 0) >> 6
    sink_col = jnp.zeros((REP * CHUNK, 1), F32)
    for r in range(REP):
        sink_col = jnp.where(rows == r, sink_ref[REP * h + r], sink_col)
    key_abs = (c - (BAND_CHUNKS - 1)) * CHUNK + lax.broadcasted_iota(jnp.int32, (1, BAND_CHUNKS * CHUNK), 1)
    return qh, kb, vb, sink_col, key_abs >= PAD_LEAD


def _attn_fwd(qr, kr, proj, sinks):
    t = qr.shape[0]
    nc = t // CHUNK

    def body(q_ref, k2, k1, k0, v2, v1, v0, g_ref, sink_ref, o_ref):
        c = pl.program_id(0)
        ks = [k2[...], k1[...], k0[...]]
        vs = [v2[...], v1[...], v0[...]]
        outs = []
        for h in range(KV_HEADS):
            qh, kb, vb, sink_col, valid = _attn_operands(c, q_ref, ks, vs, sink_ref, h)
            p, _ = _attn_probs(qh, kb, sink_col, valid)
            o = _mm(_bf(p), vb)
            outs += [o[CHUNK * r:CHUNK * (r + 1)] for r in range(REP)]
        att = jnp.concatenate(outs, axis=1)
        gate = g_ref[...]
        o_ref[...] = _bf(att * (gate * _sigmoid(gate)))

    return _pallas(
        body, name="attn_fwd", grid=(nc,),
        in_specs=[pl.BlockSpec((CHUNK, D_ATT), lambda c: (c, 0))] + _band_specs(D_KV, 0)
        + _band_specs(D_KV, OV // D_KV) + [pl.BlockSpec((CHUNK, D_ATT), lambda c: (c, OG // D_ATT)),
                                           pl.BlockSpec(memory_space=pltpu.SMEM)],
        out_specs=pl.BlockSpec((CHUNK, D_ATT), lambda c: (c, 0)),
        out_shape=jax.ShapeDtypeStruct((t, D_ATT), BF16),
        compiler_params=_cp("parallel"))(qr, kr, kr, kr, proj, proj, proj, proj, sinks)


def _outproj(ymix, amix, w_out):
    t = ymix.shape[0]
    tm, tn = _tile(t, 832), 512

    def body(y_ref, a_ref, wy_ref, wa_ref, o_ref):
        o_ref[...] = _mm(y_ref[...], wy_ref[...]) + _mm(a_ref[...], wa_ref[...])

    return _pallas(
        body, name="outproj", grid=(t // tm, D_MODEL // tn),
        in_specs=[pl.BlockSpec((tm, D_SSD), lambda i, j: (i, 0)), pl.BlockSpec((tm, D_ATT), lambda i, j: (i, 0)),
                  pl.BlockSpec((D_SSD, tn), lambda i, j: (0, j)),
                  pl.BlockSpec((D_ATT, tn), lambda i, j: (D_SSD // D_ATT, j))],
        out_specs=pl.BlockSpec((tm, tn), lambda i, j: (i, j)),
        out_shape=jax.ShapeDtypeStruct((t, D_MODEL), F32),
        compiler_params=_cp("parallel", "parallel"))(ymix, amix, w_out, w_out)


def _post_loss(out, x, target, norm_post_w):
    t = out.shape[0]
    nc = t // CHUNK

    def body(o_ref, x_ref, tg_ref, nw_ref, dout_ref, dy_ref, loss_ref, gnw_ref):
        i = pl.program_id(0)

        @pl.when(i == 0)
        def _():
            dout_ref[...] = jnp.zeros_like(dout_ref)
            dy_ref[...] = jnp.zeros_like(dy_ref)
            loss_ref[...] = jnp.zeros_like(loss_ref)
            gnw_ref[...] = jnp.zeros_like(gnw_ref)

        @pl.when(i > 0)
        def _():
            o = o_ref[...]
            nw = nw_ref[...]
            rstd = lax.rsqrt(jnp.mean(o * o, axis=-1, keepdims=True) + EPS)
            n = o * rstd
            err = x_ref[...] + n * nw - tg_ref[...]
            loss_ref[...] += jnp.sum(err * err) * (0.5 / D_MODEL)
            dy = err * (1.0 / D_MODEL)
            dy_ref[...] = dy
            gnw_ref[...] += jnp.sum(dy * n, axis=0, keepdims=True)
            dn = dy * nw
            dout_ref[...] = _bf(rstd * (dn - n * jnp.mean(dn * n, axis=-1, keepdims=True)))

    prev = lambda i: (jnp.maximum(i - 1, 0), 0)
    return _pallas(
        body, name="post_loss", grid=(nc,),
        in_specs=[pl.BlockSpec((CHUNK, D_MODEL), lambda i: (i, 0)), pl.BlockSpec((CHUNK, D_MODEL), prev),
                  pl.BlockSpec((CHUNK, D_MODEL), prev), pl.BlockSpec((1, D_MODEL), lambda i: (0, 0))],
        out_specs=[pl.BlockSpec((CHUNK, D_MODEL), lambda i: (i, 0)), pl.BlockSpec((CHUNK, D_MODEL), lambda i: (i, 0)),
                   pl.BlockSpec((8, LANES), lambda i: (0, 0)), pl.BlockSpec((1, D_MODEL), lambda i: (0, 0))],
        out_shape=[jax.ShapeDtypeStruct((t, D_MODEL), BF16), jax.ShapeDtypeStruct((t, D_MODEL), F32),
                   jax.ShapeDtypeStruct((8, LANES), F32), jax.ShapeDtypeStruct((1, D_MODEL), F32)],
        compiler_params=_cp("arbitrary"))(out, x, target, norm_post_w)


def _nt_matmul(a, b, name):
    t, k = a.shape
    n = b.shape[0]
    tm, tn = _tile(t, 832), 512

    def body(a_ref, b_ref, o_ref):
        o_ref[...] = _nt(a_ref[...], b_ref[...])

    return _pallas(
        body, name=name, grid=(t // tm, n // tn),
        in_specs=[pl.BlockSpec((tm, k), lambda i, j: (i, 0)), pl.BlockSpec((tn, k), lambda i, j: (j, 0))],
        out_specs=pl.BlockSpec((tm, tn), lambda i, j: (i, j)),
        out_shape=jax.ShapeDtypeStruct((t, n), F32),
        compiler_params=_cp("parallel", "parallel"))(a, b)


def _tn_matmul(a, b, name):
    t, m = a.shape
    n = b.shape[1]
    tk, tm, tn = _tile(t, 832), min(m, 1024), min(n, 1024)
    nk = t // tk

    def body(a_ref, b_ref, o_ref):
        @pl.when(pl.program_id(2) == 0)
        def _():
            o_ref[...] = jnp.zeros_like(o_ref)
        o_ref[...] += _tn(a_ref[...], b_ref[...])

    return _pallas(
        body, name=name, grid=(m // tm, n // tn, nk),
        in_specs=[pl.BlockSpec((tk, tm), lambda i, j, k: (k, i)), pl.BlockSpec((tk, tn), lambda i, j, k: (k, j))],
        out_specs=pl.BlockSpec((tm, tn), lambda i, j, k: (i, j)),
        out_shape=jax.ShapeDtypeStruct((m, n), F32),
        compiler_params=_cp("parallel", "parallel", "arbitrary"))(a, b)


def _attn_bwd(qr, kr, proj, dmix, sinks, ga):
    t = qr.shape[0]
    nc = t // CHUNK
    scale = HEAD_DIM ** -0.5

    def body(q_ref, k2, k1, k0, v2, v1, v0, g_ref, da_ref, sink_ref, ga_ref, dq_ref, dg_ref, dk_ref, dv_ref, gs_ref,
             got_ref, send_sems, recv_sems):
        c = pl.program_id(0)

        @pl.when(c == 0)
        def _():
            for cp in _exchange_copies(ga_ref, got_ref, send_sems, recv_sems):
                cp.start()
            dk_ref[...] = jnp.zeros_like(dk_ref)
            dv_ref[...] = jnp.zeros_like(dv_ref)
            gs_ref[...] = jnp.zeros_like(gs_ref)

        ks = [k2[...], k1[...], k0[...]]
        vs = [v2[...], v1[...], v0[...]]
        gate = g_ref[...]
        sg = _sigmoid(gate)
        da = da_ref[...]
        datt = da * (gate * sg)
        lane = lax.broadcasted_iota(jnp.int32, (1, LANES), 1)
        rows = lax.broadcasted_iota(jnp.int32, (REP * CHUNK, 1), 0) >> 6
        dqs, atts, dks, dvs = [], [], [], []
        gs = jnp.zeros((1, LANES), F32)
        for h in range(KV_HEADS):
            qh, kb, vb, sink_col, valid = _attn_operands(c, q_ref, ks, vs, sink_ref, h)
            p, psink = _attn_probs(qh, kb, sink_col, valid)
            pb = _bf(p)
            o = _mm(pb, vb)
            do = jnp.concatenate([datt[:, HEAD_DIM * (REP * h + r):HEAD_DIM * (REP * h + r + 1)] for r in range(REP)],
                                 axis=0)
            dob = _bf(do)
            delta = jnp.sum(do * o, axis=1, keepdims=True)
            ds = _bf(p * (_nt(dob, vb) - delta) * scale)
            gsink = -psink * delta
            for r in range(REP):
                gs = gs + jnp.where(lane == REP * h + r, jnp.sum(jnp.where(rows == r, gsink, 0.0)), 0.0)
            dqh = _mm(ds, kb)
            dqs += [dqh[CHUNK * r:CHUNK * (r + 1)] for r in range(REP)]
            atts += [o[CHUNK * r:CHUNK * (r + 1)] for r in range(REP)]
            dks.append(_tn(ds, qh))
            dvs.append(_tn(pb, dob))
        dq_ref[...] = jnp.concatenate(dqs, axis=1)
        att = jnp.concatenate(atts, axis=1)
        dg_ref[...] = _bf(da * att * (sg * (1.0 + gate * (1.0 - sg))))
        gs_ref[0:1, :] += gs
        dkf = jnp.concatenate(dks, axis=1)
        dvf = jnp.concatenate(dvs, axis=1)
        for j in range(BAND_CHUNKS):
            r0 = pl.multiple_of(jnp.maximum(c - (BAND_CHUNKS - 1) + j, 0) * CHUNK, CHUNK)
            dk_ref[pl.ds(r0, CHUNK), :] += dkf[CHUNK * j:CHUNK * (j + 1)]
            dv_ref[pl.ds(r0, CHUNK), :] += dvf[CHUNK * j:CHUNK * (j + 1)]

        @pl.when(c == nc - 1)
        def _():
            for cp in _exchange_copies(ga_ref, got_ref, send_sems, recv_sems):
                cp.wait()

    return _pallas(
        body, name="attn_bwd", grid=(nc,),
        in_specs=[pl.BlockSpec((CHUNK, D_ATT), lambda c: (c, 0))] + _band_specs(D_KV, 0)
        + _band_specs(D_KV, OV // D_KV) + [pl.BlockSpec((CHUNK, D_ATT), lambda c: (c, OG // D_ATT)),
                                           pl.BlockSpec((CHUNK, D_ATT), lambda c: (c, D_SSD // D_ATT)),
                                           pl.BlockSpec(memory_space=pltpu.SMEM), ANY],
        out_specs=[pl.BlockSpec((CHUNK, D_ATT), lambda c: (c, 0)), pl.BlockSpec((CHUNK, D_ATT), lambda c: (c, 0)),
                   pl.BlockSpec((t, D_KV), lambda c: (0, 0)), pl.BlockSpec((t, D_KV), lambda c: (0, 0)),
                   pl.BlockSpec((8, LANES), lambda c: (0, 0)), ANY],
        out_shape=[jax.ShapeDtypeStruct((t, D_ATT), F32), jax.ShapeDtypeStruct((t, D_ATT), BF16),
                   jax.ShapeDtypeStruct((t, D_KV), F32), jax.ShapeDtypeStruct((t, D_KV), F32),
                   jax.ShapeDtypeStruct((8, LANES), F32), _exchange_shape(ga)],
        scratch_shapes=_exchange_scratch(),
        compiler_params=_cp("arbitrary"))(qr, kr, kr, kr, proj, proj, proj, proj, dmix, sinks, ga)


def _ssd_bwd(dmix, y_ssd, xbc, proj, dt, acs, acst, states, d_skip_l, ssd_norm_w):
    t = xbc.shape[0]
    q = CHUNK
    nc = t // q
    gps = SSD_GROUPS_PER_STEP
    gw, sw = gps * GROUP_W, gps * D_STATE

    def body(dmix_ref, y_ref, z_ref, nw_ref, xs_ref, b_ref, c_ref, dt_ref, acs_ref, acst_ref, st_ref, dsk_ref,
             dz_ref, dxs_ref, db_ref, dc_ref, dacs_ref, ddt_ref, gnw_ref, gdsk_ref, dstate):
        @pl.when(pl.program_id(1) == 0)
        def _():
            dstate[...] = jnp.zeros_like(dstate)
            gnw_ref[...] = jnp.zeros_like(gnw_ref)
            gdsk_ref[...] = jnp.zeros_like(gdsk_ref)

        last_row = lax.broadcasted_iota(jnp.int32, (q, 1), 0) == q - 1
        lane = lax.broadcasted_iota(jnp.int32, (q, LANES), 1)
        lane1 = lax.broadcasted_iota(jnp.int32, (8, LANES), 1)
        for gi in range(gps):
            g = gps * pl.program_id(0) + gi
            cols = slice(GROUP_W * gi, GROUP_W * (gi + 1))
            scols = slice(D_STATE * gi, D_STATE * (gi + 1))
            y = y_ref[:, cols]
            z = z_ref[:, cols]
            sz = _sigmoid(z)
            silu_z = z * sz
            yg = y * silu_z
            rstd = lax.rsqrt(jnp.mean(yg * yg, axis=-1, keepdims=True) + EPS)
            n = yg * rstd
            dout = dmix_ref[:, cols]
            gnw_ref[:, cols] += jnp.sum(dout * n, axis=0, keepdims=True)
            dn = dout * nw_ref[:, cols]
            dyg = rstd * (dn - n * jnp.mean(dn * n, axis=-1, keepdims=True))
            dy = dyg * silu_z
            dz_ref[:, cols] = _bf(dyg * y * (sz * (1.0 + z * (1.0 - sz))))

            x = xs_ref[:, cols]
            bmb, cmb = _bf(b_ref[:, scols]), _bf(c_ref[:, scols])
            hv = _group_heads(g, gi, dt_ref[...], acs_ref[...], acst_ref, dsk_ref[...])
            dec = jnp.exp(jnp.where(_head_tri(q, True), hv.acs - hv.acs_row, NEG))
            dect = jnp.exp(jnp.where(_head_tri(q, False), hv.acs_row - hv.acs, NEG))
            b4 = jnp.concatenate([bmb] * HPG, axis=0)
            c4 = jnp.concatenate([cmb] * HPG, axis=0)
            m_all = _nt(cmb, b4) * dec
            mt_all = _nt(bmb, c4) * dect
            xdt = x * hv.dt
            xdt_b, dyb = _bf(xdt), _bf(dy)
            x_bd, dy_bd = _block_diag(xdt_b), _block_diag(dyb)
            s_prev = st_ref[0, gi]
            spb = _bf(s_prev)
            ds_new = dstate[gi]
            dsb = _bf(ds_new)
            e = jnp.exp(hv.acs)
            elast = jnp.exp(hv.acs_last)
            dte = jnp.exp(hv.acs_last - hv.acs)
            bds = _mm(bmb, dsb)
            dxdt = _mm(_bf(mt_all), dy_bd) + bds * dte
            dm = _nt(dyb, x_bd)
            dmt = _nt(xdt_b, dy_bd)
            dye = _bf(dy * e)
            dc_ref[:, scols] = _mm(_bf(dm * dec), b4) + _nt(dye, spb)
            db_ref[:, scols] = _mm(_bf(dmt * dect), c4) + _nt(_bf(xdt * dte), dsb)
            dstate[gi] = elast * ds_new + _tn(cmb, dye)
            dxs_ref[:, cols] = dxdt * hv.dt + hv.dsk * dy
            ddte_dte = bds * xdt * dte
            dacs_l = dm * m_all - dmt * mt_all + dy * _mm(cmb, spb) * e - ddte_dte
            dlast_l = (jnp.sum(ddte_dte, axis=0, keepdims=True)
                       + jnp.sum(s_prev * ds_new, axis=0, keepdims=True) * elast)
            ddt_l = dxdt * x
            gdsk_l = jnp.sum(dy * x, axis=0, keepdims=True)
            dacs_out = jnp.zeros((q, LANES), F32)
            ddt_out = jnp.zeros((q, LANES), F32)
            gdsk = jnp.zeros((8, LANES), F32)
            for r in range(HPG):
                dacs = _head_sums(dacs_l, r) + jnp.where(last_row, _head_sums(dlast_l, r), 0.0)
                dacs_out = jnp.where(lane == r, dacs, dacs_out)
                ddt_out = jnp.where(lane == r, _head_sums(ddt_l, r), ddt_out)
                gdsk = gdsk + jnp.where(lane1 == r, _head_sums(gdsk_l, r), 0.0)
            dacs_ref[:, LANES * gi:LANES * (gi + 1)] = dacs_out
            ddt_ref[:, LANES * gi:LANES * (gi + 1)] = ddt_out
            gdsk_ref[gi] += gdsk

    rev = lambda c: nc - 1 - c
    wide = pl.BlockSpec((q, gw), lambda g, c: (rev(c), g))
    return _pallas(
        body, name="ssd_bwd", grid=(GROUPS // gps, nc),
        in_specs=[wide, wide, wide, pl.BlockSpec((1, gw), lambda g, c: (0, g)), wide,
                  pl.BlockSpec((q, sw), lambda g, c: (rev(c), D_SSD // sw + g)),
                  pl.BlockSpec((q, sw), lambda g, c: (rev(c), (D_SSD + GROUPS * D_STATE) // sw + g)),
                  pl.BlockSpec((q, LANES), lambda g, c: (rev(c), 0)), pl.BlockSpec((q, LANES), lambda g, c: (rev(c), 0)),
                  pl.BlockSpec((1, gps * GROUPS, q), lambda g, c: (rev(c), g, 0)),
                  pl.BlockSpec((1, gps, D_STATE, GROUP_W), lambda g, c: (rev(c), g, 0, 0)),
                  pl.BlockSpec((1, LANES), lambda g, c: (0, 0))],
        out_specs=[wide, wide,
                   pl.BlockSpec((q, sw), lambda g, c: (rev(c), g)), pl.BlockSpec((q, sw), lambda g, c: (rev(c), g)),
                   pl.BlockSpec((q, gps * LANES), lambda g, c: (rev(c), g)),
                   pl.BlockSpec((q, gps * LANES), lambda g, c: (rev(c), g)),
                   pl.BlockSpec((1, gw), lambda g, c: (0, g)), pl.BlockSpec((gps, 8, LANES), lambda g, c: (g, 0, 0))],
        out_shape=[jax.ShapeDtypeStruct((t, D_SSD), BF16), jax.ShapeDtypeStruct((t, D_SSD), F32),
                   jax.ShapeDtypeStruct((t, GROUPS * D_STATE), F32), jax.ShapeDtypeStruct((t, GROUPS * D_STATE), F32),
                   jax.ShapeDtypeStruct((t, GROUPS * LANES), F32), jax.ShapeDtypeStruct((t, GROUPS * LANES), F32),
                   jax.ShapeDtypeStruct((1, D_SSD), F32), jax.ShapeDtypeStruct((GROUPS, 8, LANES), F32)],
        scratch_shapes=[pltpu.VMEM((gps, D_STATE, GROUP_W), F32)],
        compiler_params=_cp("parallel", "arbitrary"))(dmix, y_ssd, proj, ssd_norm_w, xbc, xbc, xbc, dt, acs, acst,
                                                      states, d_skip_l)


def _dt_bwd(dacs_g, ddt_g, dt, proj, dt_bias_l, a_log_l):
    t = dt.shape[0]
    q = CHUNK
    nc = t // q

    def body(dacs_ref, ddt_ref, dt_ref, raw_ref, bias_ref, alog_ref, draw_ref, ga_ref, gb_ref):
        c = pl.program_id(0)

        @pl.when(c == 0)
        def _():
            ga_ref[...] = jnp.zeros_like(ga_ref)
            gb_ref[...] = jnp.zeros_like(gb_ref)

        lane = lax.broadcasted_iota(jnp.int32, (q, LANES), 1)
        dacs = jnp.zeros((q, LANES), F32)
        ddt = jnp.zeros((q, LANES), F32)
        for g in range(GROUPS):
            mask = (lane >= GROUPS * g) & (lane < GROUPS * g + HPG)
            sl = slice(LANES * g, LANES * (g + 1))
            if g == 0:
                dacs = jnp.where(mask, dacs_ref[:, sl], dacs)
                ddt = jnp.where(mask, ddt_ref[:, sl], ddt)
            else:
                dacs = jnp.where(mask, pltpu.roll(dacs_ref[:, sl], GROUPS * g, 1), dacs)
                ddt = jnp.where(mask, pltpu.roll(ddt_ref[:, sl], GROUPS * g, 1), ddt)
        ri = lax.broadcasted_iota(jnp.int32, (q, q), 0)
        ci = lax.broadcasted_iota(jnp.int32, (q, q), 1)
        triu = (ri <= ci).astype(F32)
        dda = jnp.dot(triu, dacs, preferred_element_type=F32, precision=HI)
        a = -jnp.exp(alog_ref[...])
        dtv = dt_ref[...]
        row = c * q + lax.broadcasted_iota(jnp.int32, (q, LANES), 0)
        used = (lane & (GROUPS - 1)) < HPG
        dsp = jnp.where((row >= PAD_LEAD) & used, dda * a + ddt, 0.0)
        draw = dsp * _sigmoid(raw_ref[...] + bias_ref[...])
        draw_ref[...] = _bf(draw)
        gb_ref[0:1, :] += jnp.sum(draw, axis=0, keepdims=True)
        ga_ref[0:1, :] += jnp.sum(jnp.where(used, dda * dtv, 0.0), axis=0, keepdims=True) * a

    return _pallas(
        body, name="dt_bwd", grid=(nc,),
        in_specs=[pl.BlockSpec((q, GROUPS * LANES), lambda c: (c, 0)), pl.BlockSpec((q, GROUPS * LANES), lambda c: (c, 0)),
                  pl.BlockSpec((q, LANES), lambda c: (c, 0)), pl.BlockSpec((q, LANES), lambda c: (c, ODT // LANES)),
                  pl.BlockSpec((1, LANES), lambda c: (0, 0)), pl.BlockSpec((1, LANES), lambda c: (0, 0))],
        out_specs=[pl.BlockSpec((q, LANES), lambda c: (c, 0)), pl.BlockSpec((8, LANES), lambda c: (0, 0)),
                   pl.BlockSpec((8, LANES), lambda c: (0, 0))],
        out_shape=[jax.ShapeDtypeStruct((t, LANES), BF16), jax.ShapeDtypeStruct((8, LANES), F32),
                   jax.ShapeDtypeStruct((8, LANES), F32)],
        compiler_params=_cp("arbitrary"))(dacs_g, ddt_g, dt, proj, dt_bias_l, a_log_l)


def _conv_bwd(dseg, proj, conv_w, conv_b, col_off, name):
    t, width = dseg.shape
    tc = 128
    off_p = (OXS + col_off) // tc
    off_w = col_off // tc

    def body(d_ref, x_ref, w_ref, b_ref, dx_ref, gw_ref, gb_ref, xp, dup):
        xp[0:8, :] = jnp.zeros((8, tc), F32)
        xp[8:t + 8, :] = x_ref[...]
        w = w_ref[...]
        u = (b_ref[...] + w[3:4, :] * xp[8:t + 8, :] + w[2:3, :] * xp[7:t + 7, :]
             + w[1:2, :] * xp[6:t + 6, :] + w[0:1, :] * xp[5:t + 5, :])
        su = _sigmoid(u)
        du = d_ref[...] * (su * (1.0 + u * (1.0 - su)))
        dup[0:t, :] = du
        dup[t:t + 8, :] = jnp.zeros((8, tc), F32)
        dx_ref[...] = _bf(w[3:4, :] * du + w[2:3, :] * dup[1:t + 1, :] + w[1:2, :] * dup[2:t + 2, :]
                          + w[0:1, :] * dup[3:t + 3, :])
        gb_ref[...] = jnp.sum(du, axis=0, keepdims=True)
        gw_ref[...] = jnp.concatenate(
            [jnp.sum(du * xp[5 + k:t + 5 + k, :], axis=0, keepdims=True) for k in range(CONV_WIDTH)], axis=0)

    return _pallas(
        body, name=name, grid=(width // tc,),
        in_specs=[pl.BlockSpec((t, tc), lambda j: (0, j)), pl.BlockSpec((t, tc), lambda j: (0, j + off_p)),
                  pl.BlockSpec((CONV_WIDTH, tc), lambda j: (0, j + off_w)), pl.BlockSpec((1, tc), lambda j: (0, j + off_w))],
        out_specs=[pl.BlockSpec((t, tc), lambda j: (0, j)), pl.BlockSpec((CONV_WIDTH, tc), lambda j: (0, j)),
                   pl.BlockSpec((1, tc), lambda j: (0, j))],
        out_shape=[jax.ShapeDtypeStruct((t, width), BF16), jax.ShapeDtypeStruct((CONV_WIDTH, width), F32),
                   jax.ShapeDtypeStruct((1, width), F32)],
        scratch_shapes=[pltpu.VMEM((t + 8, tc), F32), pltpu.VMEM((t + 8, tc), F32)],
        compiler_params=_cp("parallel"))(dseg, proj, conv_w, conv_b)


def _dinproj(dproj, w_re, hpad, norm_w, dy_t, ga):
    t, n = dproj.shape
    d = hpad.shape[1]
    tm, tk = _tile(t, 416), 1024
    nk = n // tk
    ni = t // tm

    def body(dp_ref, w_ref, h_ref, nw_ref, dy_ref, ga_ref, dh_ref, gnw_ref, got_ref, acc, send_sems, recv_sems):
        i, k = pl.program_id(0), pl.program_id(1)

        @pl.when((i == 0) & (k == 0))
        def _():
            for cp in _exchange_copies(ga_ref, got_ref, send_sems, recv_sems):
                cp.start()
            gnw_ref[...] = jnp.zeros_like(gnw_ref)

        @pl.when(k == 0)
        def _():
            acc[...] = jnp.zeros_like(acc)

        acc[...] += _nt(dp_ref[...], w_ref[...])

        @pl.when(k == nk - 1)
        def _():
            h = h_ref[...]
            rstd = lax.rsqrt(jnp.mean(h * h, axis=-1, keepdims=True) + EPS)
            nrm = h * rstd
            dhn = acc[...]
            gnw_ref[...] += jnp.sum(dhn * nrm, axis=0, keepdims=True)
            dn = dhn * nw_ref[...]
            dh_ref[...] = rstd * (dn - nrm * jnp.mean(dn * nrm, axis=-1, keepdims=True)) + dy_ref[...]

        @pl.when((i == ni - 1) & (k == nk - 1))
        def _():
            for cp in _exchange_copies(ga_ref, got_ref, send_sems, recv_sems):
                cp.wait()

    return _pallas(
        body, name="dinproj", grid=(ni, nk),
        in_specs=[pl.BlockSpec((tm, tk), lambda i, k: (i, k)), pl.BlockSpec((d, tk), lambda i, k: (0, k)),
                  pl.BlockSpec((tm, d), lambda i, k: (i, 0)), pl.BlockSpec((1, d), lambda i, k: (0, 0)),
                  pl.BlockSpec((tm, d), lambda i, k: (i, 0)), ANY],
        out_specs=[pl.BlockSpec((tm, d), lambda i, k: (i, 0)), pl.BlockSpec((1, d), lambda i, k: (0, 0)), ANY],
        out_shape=[jax.ShapeDtypeStruct((t, d), F32), jax.ShapeDtypeStruct((1, d), F32), _exchange_shape(ga)],
        scratch_shapes=[pltpu.VMEM((tm, d), F32)] + _exchange_scratch(),
        compiler_params=_cp("arbitrary", "arbitrary"))(dproj, w_re, hpad, norm_w, dy_t, ga)


def _spread_heads(v):
    v = jnp.pad(v.reshape(GROUPS, HPG), ((0, 0), (0, GROUPS - HPG))).reshape(1, GROUPS * GROUPS)
    return jnp.pad(v, ((0, 0), (0, LANES - GROUPS * GROUPS)))


def _gather_heads(v):
    return v[0:1, :GROUPS * GROUPS].reshape(GROUPS, GROUPS)[:, :HPG].reshape(1, SSD_HEADS)


def _rope_tables(t):
    half = HEAD_DIM // 2
    inv = ROPE_THETA ** (-jnp.arange(half, dtype=F32) / half)
    pos = (jnp.arange(t) - PAD_LEAD).astype(F32)
    ang = pos[:, None] * inv[None, :]
    cos, sin = jnp.cos(ang), jnp.sin(ang)
    cos_t = jnp.concatenate([cos, cos, cos, cos], axis=1)
    sin_t = jnp.concatenate([-sin, sin, -sin, sin], axis=1)
    return cos_t, sin_t


def _column_pieces():
    runs = [(0, OB + 2 * GROUPS * D_STATE, 0)]
    o = OB + 2 * GROUPS * D_STATE
    runs += [(o + HPG * g, HPG, ODT + GROUPS * g) for g in range(GROUPS)]
    o += SSD_HEADS
    for width, dst in ((D_ATT, OQ), (D_KV, OK), (D_KV, OV), (D_ATT, OG)):
        runs.append((o, width, dst))
        o += width
    assert o == D_IN
    pieces = []
    for o0, width, dst in runs:
        for j in range(N_SHARD):
            lo, hi = max(o0, W_IN_SHARD * j), min(o0 + width, W_IN_SHARD * (j + 1))
            if lo < hi:
                pieces.append((j, lo - W_IN_SHARD * j, hi - W_IN_SHARD * j, dst + lo - o0))
    return pieces


def _shards_to_re(w_all):
    _, k, _ = w_all.shape
    tr = 256

    def body(x_ref, o_ref):
        o_ref[:, ODT:ODT + DT_SLAB] = jnp.zeros((tr, DT_SLAB), o_ref.dtype)
        for j, c0, c1, d0 in _column_pieces():
            o_ref[:, d0:d0 + c1 - c0] = x_ref[j, :, c0:c1]

    return _pallas(body, name="shards_to_re", grid=(k // tr,),
                   in_specs=[pl.BlockSpec((N_SHARD, tr, W_IN_SHARD), lambda i: (0, i, 0))],
                   out_specs=pl.BlockSpec((tr, N_RE), lambda i: (i, 0)),
                   out_shape=jax.ShapeDtypeStruct((k, N_RE), w_all.dtype), compiler_params=_cp("parallel"))(w_all)


def _re_to_shards(g):
    k = g.shape[0]
    tr = 128

    def body(g_ref, o_ref):
        for j, c0, c1, d0 in _column_pieces():
            o_ref[j, :, c0:c1] = g_ref[:, d0:d0 + c1 - c0]

    return _pallas(body, name="re_to_shards", grid=(k // tr,),
                   in_specs=[pl.BlockSpec((tr, N_RE), lambda i: (i, 0))],
                   out_specs=pl.BlockSpec((N_SHARD, tr, W_IN_SHARD), lambda i: (0, i, 0)),
                   out_shape=jax.ShapeDtypeStruct((N_SHARD, k, W_IN_SHARD), g.dtype), compiler_params=_cp("parallel"))(g)


def _local_step(x, target, meta, norm_pre_w, w_re, conv_w, conv_b, dt_bias, a_log, d_skip, ssd_norm_w, sinks,
                w_out_shard, norm_post_w, place):
    seq = x.shape[0]
    t = PAD_LEAD + N_META + seq
    hpad = jnp.concatenate([jnp.zeros((PAD_LEAD, D_MODEL), F32), meta, x], axis=0)
    dt_bias_l, a_log_l, d_skip_l = _spread_heads(dt_bias), _spread_heads(a_log), _spread_heads(d_skip)
    cos_t, sin_t = _rope_tables(t)
    sink_v = sinks.reshape(Q_HEADS)

    proj, hn, w_out_all = _inproj(hpad, norm_pre_w, w_re, w_out_shard)
    w_out = w_out_all.reshape(D_MIX, D_MODEL)
    xbc = _conv_fwd(proj, conv_w, conv_b)
    dt, acs, acst = _dt_prep(proj, dt_bias_l, a_log_l)
    y_ssd, ymix, states = _ssd_fwd(xbc, proj, dt, acs, acst, d_skip_l, ssd_norm_w)
    qr, kr = _rope(proj, OQ, proj, OK, cos_t, sin_t)
    amix = _attn_fwd(qr, kr, proj, sink_v)
    out = _outproj(ymix, amix, w_out)
    dout, dy_t, loss_blk, g_norm_post = _post_loss(out, x, target, norm_post_w)

    dmix = _nt_matmul(dout, w_out, "dmix")
    g_w_out = jnp.concatenate([_tn_matmul(ymix, dout, "gw_out_y"), _tn_matmul(amix, dout, "gw_out_a")], axis=0)
    ga_out = _reduce_pair(g_w_out.reshape(N_SHARD, W_OUT_SHARD, D_MODEL), place, "gw_out")
    dq_r, dg, dk_r, dv, gs, slabs_out = _attn_bwd(qr, kr, proj, dmix, sink_v, ga_out)
    g_w_out = _reduce_finish(ga_out, slabs_out, place, "gw_out")
    dq, dk = _rope(dq_r, 0, dk_r, 0, cos_t, -sin_t)
    dz, dxs, db, dc, dacs_g, ddt_g, g_ssd_norm, gdsk = _ssd_bwd(dmix, y_ssd, xbc, proj, dt, acs, acst, states,
                                                                d_skip_l, ssd_norm_w)
    draw, ga, gb = _dt_bwd(dacs_g, ddt_g, dt, proj, dt_bias_l, a_log_l)
    dxs_p, gcw0, gcb0 = _conv_bwd(dxs, proj, conv_w, conv_b, 0, "conv_bwd_x")
    db_p, gcw1, gcb1 = _conv_bwd(db, proj, conv_w, conv_b, D_SSD, "conv_bwd_b")
    dc_p, gcw2, gcb2 = _conv_bwd(dc, proj, conv_w, conv_b, D_SSD + GROUPS * D_STATE, "conv_bwd_c")
    dproj = jnp.concatenate([dz, dxs_p, db_p, dc_p, dq, dg, dk, _bf(dv), draw,
                             jnp.zeros((t, DT_SLAB - LANES), BF16)], axis=1)
    ga_in = _reduce_pair(_re_to_shards(_tn_matmul(hn, dproj, "gw_in")), place, "gw_in")
    dh, g_norm_pre, slabs_in = _dinproj(dproj, w_re, hpad, norm_pre_w, dy_t, ga_in)
    g_w_in = _reduce_finish(ga_in, slabs_in, place, "gw_in")

    gdsk_l = jnp.concatenate([gdsk[g, 0:1, 0:GROUPS] for g in range(GROUPS)], axis=1)
    gdsk_l = jnp.pad(gdsk_l, ((0, 0), (0, LANES - GROUPS * GROUPS)))
    grads = dict(
        meta_tokens=dh[PAD_LEAD:ROW0], norm_pre_w=g_norm_pre, w_in=g_w_in,
        conv_w=jnp.concatenate([gcw0, gcw1, gcw2], axis=1), conv_b=jnp.concatenate([gcb0, gcb1, gcb2], axis=1),
        dt_bias=_gather_heads(gb), a_log=_gather_heads(ga), d_skip=_gather_heads(gdsk_l), ssd_norm_w=g_ssd_norm,
        attn_sinks=gs[0:1, :Q_HEADS], w_out=g_w_out, norm_post_w=g_norm_post)
    return loss_blk[0, 0], dh[ROW0:], grads


ANY = pl.BlockSpec(memory_space=pl.ANY)
MESH = pl.DeviceIdType.MESH
GATHER_CHUNKS = 4
PAIR_CHUNKS = 16
JOIN_CHUNKS = 8


def _rcopy(src, dst, ssem, rsem, dev):
    return pltpu.make_async_remote_copy(src_ref=src, dst_ref=dst, send_sem=ssem, recv_sem=rsem, device_id=dev,
                                        device_id_type=MESH)


def _place():
    x, y, c = lax.axis_index("x"), lax.axis_index("y"), lax.axis_index("c")
    chips = [(1 - x, y), (x, 1 - y), (1 - x, 1 - y)]
    return x, y, c, chips


def _gather_plan(x_ref, out_ref, send_sems, recv_sems, local_sems, hr, kc):
    ch = hr // kc
    assert ch * kc == hr and ch % 16 == 0
    x, y, c, chips = _place()
    me = 2 * x + y
    sibling = (x, y, 1 - c)

    def piece(chip, hc, k):
        return out_ref.at[chip, pl.ds(hc * hr + k * ch, ch), :]

    def local():
        return [pltpu.make_async_copy(x_ref.at[pl.ds(k * ch, ch), :], out_ref.at[me, pl.ds(k * ch, ch), :],
                                      local_sems.at[k]) for k in range(2 * kc)]

    def first():
        return [_rcopy(x_ref.at[pl.ds(c * hr + k * ch, ch), :], piece(me, c, k), send_sems.at[j * kc + k],
                       recv_sems.at[j * kc + k], (*chip, c)) for j, chip in enumerate(chips) for k in range(kc)]

    def passed(hc):
        return [_rcopy(piece(2 * chip[0] + chip[1], hc, k), piece(2 * chip[0] + chip[1], hc, k),
                       send_sems.at[(3 + j) * kc + k], recv_sems.at[(3 + j) * kc + k], sibling)
                for j, chip in enumerate(chips) for k in range(kc)]

    def arrivals():
        return [_rcopy(piece(2 * chip[0] + chip[1], c, k), piece(2 * chip[0] + chip[1], c, k), send_sems.at[j * kc + k],
                       recv_sems.at[j * kc + k], (*chip, c)) for j, chip in enumerate(chips) for k in range(kc)]

    def start():
        for cp in local() + first():
            cp.start()

    def forward():
        for arrived, fw in zip(arrivals(), passed(c)):
            arrived.wait_recv()
            fw.start()

    def finish():
        for cp in passed(1 - c):
            cp.wait_recv()
        for cp in first() + passed(c):
            cp.wait_send()
        for cp in local():
            cp.wait()

    return start, forward, finish


def _gather_shards(shard, name, kc):
    r, n = shard.shape

    def body(x_ref, out_ref, send_sems, recv_sems, local_sems):
        for phase in _gather_plan(x_ref, out_ref, send_sems, recv_sems, local_sems, r // 2, kc):
            phase()

    return _pallas(
        body, name=name, in_specs=[ANY], out_specs=ANY,
        out_shape=jax.ShapeDtypeStruct((N_SHARD, r, n), shard.dtype),
        scratch_shapes=[pltpu.SemaphoreType.DMA((6 * kc,)), pltpu.SemaphoreType.DMA((6 * kc,)),
                        pltpu.SemaphoreType.DMA((2 * kc,))])(shard)


def _pair_send(g4, name):
    _, r, n = g4.shape
    hr = r // 2
    kc = PAIR_CHUNKS
    ch = hr // kc
    assert ch * kc == hr and ch % 8 == 0

    def body(g_ref, got_ref, send_sems, recv_sems):
        x, y, c, _ = _place()
        cps = [_rcopy(g_ref.at[:, pl.ds((1 - c) * hr + k * ch, ch), :], got_ref.at[:, pl.ds(k * ch, ch), :],
                      send_sems.at[k], recv_sems.at[k], (x, y, 1 - c)) for k in range(kc)]
        for cp in cps:
            cp.start()
        for cp in cps:
            cp.wait()

    return _pallas(
        body, name=name, in_specs=[ANY], out_specs=ANY, out_shape=jax.ShapeDtypeStruct((N_SHARD, hr, n), F32),
        scratch_shapes=[pltpu.SemaphoreType.DMA((kc,)), pltpu.SemaphoreType.DMA((kc,))])(g4)


def _pair_add(g4, got, core, name):
    k, r, n = g4.shape
    hr = r // 2
    tr = _tile(hr, 256)
    nt = hr // tr

    def body(core_ref, a_ref, b_ref, o_ref):
        o_ref[...] = _bf(a_ref[...] + b_ref[...])

    spec = pl.BlockSpec((1, tr, n), lambda j, i, core_ref: (j, i, 0))
    return _pallas(
        body, name=name,
        grid_spec=pltpu.PrefetchScalarGridSpec(
            num_scalar_prefetch=1, grid=(k, nt),
            in_specs=[pl.BlockSpec((1, tr, n), lambda j, i, core_ref: (j, core_ref[0] * nt + i, 0)), spec],
            out_specs=spec),
        out_shape=jax.ShapeDtypeStruct((k, hr, n), BF16), compiler_params=_cp("parallel", "parallel"))(core, g4, got)


def _exchange_copies(g_ref, got_ref, send_sems, recv_sems):
    hr = g_ref.shape[1]
    kc = GATHER_CHUNKS
    ch = hr // kc
    assert ch * kc == hr and ch % 16 == 0
    x, y, c, chips = _place()
    return [_rcopy(g_ref.at[2 * chip[0] + chip[1], pl.ds(k * ch, ch), :], got_ref.at[j, pl.ds(k * ch, ch), :],
                   send_sems.at[j * kc + k], recv_sems.at[j * kc + k], (*chip, c))
            for j, chip in enumerate(chips) for k in range(kc)]


def _exchange_scratch():
    return [pltpu.SemaphoreType.DMA((3 * GATHER_CHUNKS,)), pltpu.SemaphoreType.DMA((3 * GATHER_CHUNKS,))]


def _exchange_shape(ga):
    return jax.ShapeDtypeStruct((3,) + ga.shape[1:], ga.dtype)


def _chip_sum(ga, got, place, name):
    _, hr, n = ga.shape
    tr = _tile(hr, 256)
    nt = hr // tr

    def body(place_ref, own_ref, got_ref, o_ref):
        acc = own_ref[0].astype(F32)
        for j in range(3):
            acc = acc + got_ref[j].astype(F32)
        o_ref[...] = acc

    return _pallas(
        body, name=name,
        grid_spec=pltpu.PrefetchScalarGridSpec(
            num_scalar_prefetch=1, grid=(nt,),
            in_specs=[pl.BlockSpec((1, tr, n), lambda i, place_ref: (place_ref[0], i, 0)),
                      pl.BlockSpec((3, tr, n), lambda i, place_ref: (0, i, 0))],
            out_specs=pl.BlockSpec((tr, n), lambda i, place_ref: (place_ref[1] * nt + i, 0))),
        out_shape=jax.ShapeDtypeStruct((2 * hr, n), F32), compiler_params=_cp("parallel"))(place, ga, got)


def _pair_join(buf, name):
    r, n = buf.shape
    hr = r // 2
    kc = JOIN_CHUNKS
    ch = hr // kc
    assert ch * kc == hr and ch % 8 == 0

    def body(in_ref, out_ref, send_sems, recv_sems):
        x, y, c, _ = _place()
        cps = [_rcopy(out_ref.at[pl.ds(c * hr + k * ch, ch), :], out_ref.at[pl.ds(c * hr + k * ch, ch), :],
                      send_sems.at[k], recv_sems.at[k], (x, y, 1 - c)) for k in range(kc)]
        for cp in cps:
            cp.start()
        for k in range(kc):
            rows = out_ref.at[pl.ds((1 - c) * hr + k * ch, ch), :]
            _rcopy(rows, rows, send_sems.at[k], recv_sems.at[k], (x, y, 1 - c)).wait_recv()
        for cp in cps:
            cp.wait_send()

    return _pallas(
        body, name=name, in_specs=[ANY], out_specs=ANY, out_shape=jax.ShapeDtypeStruct((r, n), F32),
        input_output_aliases={0: 0},
        scratch_shapes=[pltpu.SemaphoreType.DMA((kc,)), pltpu.SemaphoreType.DMA((kc,))])(buf)


def _reduce_pair(g4, place, tag):
    got = _pair_send(g4, tag + "_pair_send")
    return _pair_add(g4, got, place[1:2], tag + "_pair_add")


def _reduce_finish(ga, slabs, place, tag):
    return _pair_join(_chip_sum(ga, slabs, place, tag + "_chip_sum"), tag + "_pair_join")


def _allreduce_small(p, name):
    rows, n = p.shape
    ndev = 8

    def body(p_ref, out_ref, slots, send_sems, recv_sems):
        x, y, c, _ = _place()
        my = 4 * x + 2 * y + c
        slots[my] = p_ref[...]
        cps = []
        for k in range(1, ndev):
            kx, ky, kc = (k >> 2) & 1, (k >> 1) & 1, k & 1
            peer = (x ^ kx, y ^ ky, c ^ kc)
            cp = _rcopy(p_ref, slots.at[my], send_sems.at[k - 1], recv_sems.at[k - 1], peer)
            cp.start()
            cps.append(cp)
        for k in range(1, ndev):
            _rcopy(p_ref, slots.at[my ^ k], send_sems.at[k - 1], recv_sems.at[k - 1], (x, y, c)).wait_recv()
        for cp in cps:
            cp.wait_send()
        acc = slots[0]
        for j in range(1, ndev):
            acc = acc + slots[j]
        out_ref[...] = acc

    vm = pl.BlockSpec(memory_space=pltpu.VMEM)
    return _pallas(
        body, name=name, in_specs=[vm], out_specs=vm, out_shape=jax.ShapeDtypeStruct((rows, n), F32),
        scratch_shapes=[pltpu.VMEM((ndev, rows, n), F32), pltpu.SemaphoreType.DMA((ndev - 1,)),
                        pltpu.SemaphoreType.DMA((ndev - 1,))])(p)


def _adamw(w, g, m, v, name):
    r, n = w.shape
    tr = _tile(r, 256, 8)
    c1 = 1.0 / (1.0 - ADAM_B1 ** ADAM_STEP)
    c2 = 1.0 / (1.0 - ADAM_B2 ** ADAM_STEP)

    def body(w_ref, g_ref, m_ref, v_ref, d_ref, mo_ref, vo_ref):
        gv = g_ref[...]
        mn = ADAM_B1 * m_ref[...] + (1.0 - ADAM_B1) * gv
        vn = ADAM_B2 * v_ref[...] + (1.0 - ADAM_B2) * (gv * gv)
        d_ref[...] = -ADAM_LR * ((mn * c1) / (jnp.sqrt(vn * c2) + ADAM_EPS) + ADAM_WD * w_ref[...])
        mo_ref[...] = mn
        vo_ref[...] = vn

    spec = pl.BlockSpec((tr, n), lambda i: (i, 0))
    shp = jax.ShapeDtypeStruct((r, n), F32)
    return _pallas(body, name=name, grid=(r // tr,), in_specs=[spec] * 4, out_specs=[spec] * 3, out_shape=[shp] * 3,
                   compiler_params=_cp("parallel"))(w, g, m, v)


PACK_W = 1024
SMALL_REPL = ("norm_pre_w", "conv_b", "ssd_norm_w", "norm_post_w")
SMALL_HEAD = ("dt_bias", "a_log", "d_skip", "attn_sinks")


def _rows(a):
    return a.reshape(-1, PACK_W)


def _head_row(vals, extra=None):
    parts = [vals[n].reshape(1, -1) for n in SMALL_HEAD]
    if extra is not None:
        parts.append(extra.reshape(1, 1))
    row = jnp.concatenate(parts, axis=1)
    return jnp.pad(row, ((0, 0), (0, PACK_W - row.shape[1])))


def _pad_rows(a, rows):
    return jnp.pad(a, ((0, rows - a.shape[0]), (0, 0)))


def _pack_repl(vals, extra=None):
    body = jnp.concatenate([_rows(vals[n]) for n in SMALL_REPL] + [_head_row(vals, extra)], axis=0)
    return _pad_rows(body, 16)


def _unpack_repl(buf):
    out, r = {}, 0
    for n, k in zip(SMALL_REPL, (2, 4, 2, 2)):
        out[n] = buf[r:r + k].reshape(1, k * PACK_W)
        r += k
    col = 0
    for n, k in zip(SMALL_HEAD, (32, 32, 32, 16)):
        out[n] = buf[r:r + 1, col:col + k]
        col += k
    return out, buf[r, col]


def kernel(x, meta_tokens, norm_pre_w, w_in, conv_w, conv_b, dt_bias, a_log, d_skip, ssd_norm_w, attn_sinks, w_out, norm_post_w, loss_target, m_meta_tokens, m_norm_pre_w, m_w_in, m_conv_w, m_conv_b, m_dt_bias, m_a_log, m_d_skip, m_ssd_norm_w, m_attn_sinks, m_w_out, m_norm_post_w, v_meta_tokens, v_norm_pre_w, v_w_in, v_conv_w, v_conv_b, v_dt_bias, v_a_log, v_d_skip, v_ssd_norm_w, v_attn_sinks, v_w_out, v_norm_post_w):
    names = ("meta_tokens", "norm_pre_w", "w_in", "conv_w", "conv_b", "dt_bias", "a_log", "d_skip", "ssd_norm_w",
             "attn_sinks", "w_out", "norm_post_w")
    w = dict(zip(names, (meta_tokens, norm_pre_w, w_in, conv_w, conv_b, dt_bias, a_log, d_skip, ssd_norm_w, attn_sinks,
                         w_out, norm_post_w)))
    m = dict(zip(names, (m_meta_tokens, m_norm_pre_w, m_w_in, m_conv_w, m_conv_b, m_dt_bias, m_a_log, m_d_skip,
                         m_ssd_norm_w, m_attn_sinks, m_w_out, m_norm_post_w)))
    v = dict(zip(names, (v_meta_tokens, v_norm_pre_w, v_w_in, v_conv_w, v_conv_b, v_dt_bias, v_a_log, v_d_skip,
                         v_ssd_norm_w, v_attn_sinks, v_w_out, v_norm_post_w)))
    cx, cy, cc = lax.axis_index("x"), lax.axis_index("y"), lax.axis_index("c")
    chip = 2 * cx + cy
    meta_cols = D_MODEL // N_SHARD
    conv_cols = D_CONV // N_SHARD

    place = jnp.stack([chip, cc]).astype(jnp.int32)
    w_re = _shards_to_re(_gather_shards(_bf(w_in[0]), "gather_w_in", 2 * GATHER_CHUNKS))
    conv_z = lax.dynamic_update_slice(jnp.zeros((CONV_WIDTH, D_CONV), F32), conv_w[0], (0, chip * conv_cols))
    meta_z = lax.dynamic_update_slice(jnp.zeros((N_META, D_MODEL), F32), meta_tokens, (0, chip * meta_cols))
    small = jnp.concatenate([_rows(conv_z), _rows(meta_z)], axis=0)
    small = _allreduce_small(jnp.where(cc == 0, small, 0.0), "gather_small")
    conv_full = small[0:16].reshape(CONV_WIDTH, D_CONV)
    meta_full = small[16:48].reshape(N_META, D_MODEL)

    loss_dev, grad_x, g = _local_step(x[0], loss_target[0], meta_full, norm_pre_w, w_re, conv_full, conv_b, dt_bias,
                                      a_log, d_skip, ssd_norm_w, attn_sinks, _bf(w_out[0]), norm_post_w, place)
    g_w_in, g_w_out = g["w_in"], g["w_out"]

    packed = jnp.concatenate([_rows(g["conv_w"]), _rows(g["meta_tokens"]), _pack_repl(g, loss_dev)], axis=0)
    red = _allreduce_small(packed, "reduce_small")
    g_conv_full = red[0:16].reshape(CONV_WIDTH, D_CONV)
    g_meta_full = red[16:48].reshape(N_META, D_MODEL)
    g_small, loss = _unpack_repl(red[48:64])
    grads = dict(g_small)
    grads["w_in"] = g_w_in
    grads["w_out"] = g_w_out
    grads["conv_w"] = lax.dynamic_slice(g_conv_full, (0, chip * conv_cols), (CONV_WIDTH, conv_cols))
    grads["meta_tokens"] = lax.dynamic_slice(g_meta_full, (0, chip * meta_cols), (N_META, meta_cols))

    upd = {}
    upd["w_in"] = _adamw(w_in[0], g_w_in, m_w_in[0], v_w_in[0], "adamw_w_in")
    upd["w_out"] = _adamw(w_out[0], g_w_out, m_w_out[0], v_w_out[0], "adamw_w_out")

    def pack_small(vals, conv, meta):
        return jnp.concatenate([_pad_rows(conv.reshape(CONV_WIDTH, conv_cols), 8), _rows(meta), _pack_repl(vals)], axis=0)

    sm = _adamw(pack_small(w, w["conv_w"], w["meta_tokens"]), pack_small(grads, grads["conv_w"], grads["meta_tokens"]),
                pack_small(m, m["conv_w"], m["meta_tokens"]), pack_small(v, v["conv_w"], v["meta_tokens"]),
                "adamw_small")
    for n in names:
        if n not in ("w_in", "w_out"):
            upd[n] = [None, None, None]
    for k, buf in enumerate(sm):
        upd["conv_w"][k] = buf[0:CONV_WIDTH]
        upd["meta_tokens"][k] = buf[8:16].reshape(N_META, meta_cols)
        rest, _ = _unpack_repl(buf[16:32])
        for n in SMALL_REPL + SMALL_HEAD:
            upd[n][k] = rest[n]

    def shaped(n, a):
        return a.reshape(w[n].shape)

    outs = [loss, grad_x[None]]
    outs += [shaped(n, grads[n]) for n in names]
    for k in range(3):
        outs += [shaped(n, upd[n][k]) for n in names]
    return tuple(outs)
```

```python
import functools

import jax
import jax.numpy as jnp
from jax import lax
from jax.experimental import pallas as pl
from jax.experimental.pallas import tpu as pltpu

F32 = jnp.float32
BF16 = jnp.bfloat16

D_MODEL = 2048
CHUNK = 64
N_META = 16
PAD_LEAD = CHUNK - N_META
ROW0 = PAD_LEAD + N_META
EPS = 1e-6
SSD_HEADS = 32
HEAD_DIM = 64
GROUPS = 8
HPG = SSD_HEADS // GROUPS
D_STATE = 128
D_SSD = 2048
GROUP_W = D_SSD // GROUPS
CONV_WIDTH = 4
D_CONV = 4096
Q_HEADS = 16
KV_HEADS = 4
REP = Q_HEADS // KV_HEADS
D_ATT = 1024
D_KV = 256
BAND_CHUNKS = 3
ROPE_THETA = 10000.0
D_MIX = D_SSD + D_ATT
D_IN = 8736
N_SHARD = 4
W_IN_SHARD = D_IN // N_SHARD
W_OUT_SHARD = D_MIX // N_SHARD

OZ, OXS, OB, OC, OQ, OG, OK, OV, ODT = 0, 2048, 4096, 5120, 6144, 7168, 8192, 8448, 8704
DT_SLAB = 512
N_RE = ODT + DT_SLAB
LANES = 128

ADAM_LR, ADAM_B1, ADAM_B2, ADAM_EPS, ADAM_WD, ADAM_STEP = 0.001, 0.9, 0.999, 1e-08, 0.01, 10

SSD_GROUPS_PER_STEP = 4
VMEM_LIMIT = 52 * 1024 * 1024
NEG = -1e30
HI = lax.Precision.HIGHEST


def _pallas(body, **kw):
    return pl.pallas_call(body, **kw)


def _cp(*sem):
    return pltpu.CompilerParams(dimension_semantics=sem, vmem_limit_bytes=VMEM_LIMIT)


def _tile(n, cap, mult=16):
    best = None
    for d in range(mult, min(n, cap) + 1, mult):
        if n % d == 0:
            best = d
    assert best is not None, (n, cap)
    return best


def _nt(a, b):
    return lax.dot_general(a, b, (((1,), (1,)), ((), ())), preferred_element_type=F32)


def _tn(a, b):
    return lax.dot_general(a, b, (((0,), (0,)), ((), ())), preferred_element_type=F32)


def _mm(a, b):
    return jnp.dot(a, b, preferred_element_type=F32)


def _sigmoid(x):
    return 1.0 / (1.0 + jnp.exp(-x))


def _bf(x):
    return x.astype(BF16)


def _inproj(hpad, norm_w, w_re, w_out_shard):
    t, d = hpad.shape
    n = w_re.shape[1]
    tm, tn = _tile(t, 832), 512
    ni, nj = t // tm, n // tn
    r_out, n_out = w_out_shard.shape
    kc = GATHER_CHUNKS

    def body(h_ref, nw_ref, w_ref, ws_ref, proj_ref, hn_ref, wall_ref, hn_s, send_sems, recv_sems, local_sems):
        i, j = pl.program_id(0), pl.program_id(1)
        start, forward, finish = _gather_plan(ws_ref, wall_ref, send_sems, recv_sems, local_sems, r_out // 2, kc)
        pl.when((i == 0) & (j == 0))(start)
        pl.when((i == ni // 2) & (j == 0))(forward)

        @pl.when(j == 0)
        def _():
            h = h_ref[...]
            ms = jnp.mean(h * h, axis=-1, keepdims=True)
            hn = _bf(h * lax.rsqrt(ms + EPS) * nw_ref[...])
            hn_s[...] = hn
            hn_ref[...] = hn
        proj_ref[...] = _mm(hn_s[...], w_ref[...])
        pl.when((i == ni - 1) & (j == nj - 1))(finish)

    return _pallas(
        body, name="inproj", grid=(ni, nj),
        in_specs=[pl.BlockSpec((tm, d), lambda i, j: (i, 0)), pl.BlockSpec((1, d), lambda i, j: (0, 0)),
                  pl.BlockSpec((d, tn), lambda i, j: (0, j)), ANY],
        out_specs=[pl.BlockSpec((tm, tn), lambda i, j: (i, j)), pl.BlockSpec((tm, d), lambda i, j: (i, 0)), ANY],
        out_shape=[jax.ShapeDtypeStruct((t, n), F32), jax.ShapeDtypeStruct((t, d), BF16),
                   jax.ShapeDtypeStruct((N_SHARD, r_out, n_out), w_out_shard.dtype)],
        scratch_shapes=[pltpu.VMEM((tm, d), BF16), pltpu.SemaphoreType.DMA((6 * kc,)), pltpu.SemaphoreType.DMA((6 * kc,)),
                        pltpu.SemaphoreType.DMA((2 * kc,))],
        compiler_params=_cp("arbitrary", "arbitrary"))(hpad, norm_w, w_re, w_out_shard)


def _conv_fwd(proj, conv_w, conv_b):
    t = proj.shape[0]
    tc = 256
    off = OXS // tc

    def body(x_ref, w_ref, b_ref, o_ref, xp):
        xp[0:8, :] = jnp.zeros((8, tc), F32)
        xp[8:t + 8, :] = x_ref[...]
        w = w_ref[...]
        u = (b_ref[...] + w[3:4, :] * xp[8:t + 8, :] + w[2:3, :] * xp[7:t + 7, :]
             + w[1:2, :] * xp[6:t + 6, :] + w[0:1, :] * xp[5:t + 5, :])
        o_ref[...] = u * _sigmoid(u)

    return _pallas(
        body, name="conv_fwd", grid=(D_CONV // tc,),
        in_specs=[pl.BlockSpec((t, tc), lambda j: (0, j + off)), pl.BlockSpec((CONV_WIDTH, tc), lambda j: (0, j)),
                  pl.BlockSpec((1, tc), lambda j: (0, j))],
        out_specs=pl.BlockSpec((t, tc), lambda j: (0, j)),
        out_shape=jax.ShapeDtypeStruct((t, D_CONV), F32),
        scratch_shapes=[pltpu.VMEM((t + 8, tc), F32)],
        compiler_params=_cp("parallel"))(proj, conv_w, conv_b)


def _softplus(u):
    e = jnp.exp(-jnp.abs(u))
    w = 1.0 + e
    l1p = jnp.where(w == 1.0, e, jnp.log(w) * (e / jnp.where(w == 1.0, 1.0, w - 1.0)))
    return jnp.maximum(u, 0.0) + l1p


def _dt_prep(proj, dt_bias_l, a_log_l):
    t = proj.shape[0]
    nc = t // CHUNK
    q = CHUNK

    def body(raw_ref, bias_ref, alog_ref, dt_ref, acs_ref, acst_ref):
        c = pl.program_id(0)
        sp = _softplus(raw_ref[...] + bias_ref[...])
        row = c * q + lax.broadcasted_iota(jnp.int32, (q, LANES), 0)
        dt = jnp.where(row >= PAD_LEAD, sp, 0.0)
        da = dt * (-jnp.exp(alog_ref[...]))
        ri = lax.broadcasted_iota(jnp.int32, (q, q), 0)
        ci = lax.broadcasted_iota(jnp.int32, (q, q), 1)
        tri = (ri >= ci).astype(F32)
        acs = jnp.dot(tri, da, preferred_element_type=F32, precision=HI)
        dt_ref[...] = dt
        acs_ref[...] = acs
        acst_ref[0] = acs.T

    return _pallas(
        body, name="dt_prep", grid=(nc,),
        in_specs=[pl.BlockSpec((q, LANES), lambda c: (c, ODT // LANES)), pl.BlockSpec((1, LANES), lambda c: (0, 0)),
                  pl.BlockSpec((1, LANES), lambda c: (0, 0))],
        out_specs=[pl.BlockSpec((q, LANES), lambda c: (c, 0)), pl.BlockSpec((q, LANES), lambda c: (c, 0)),
                   pl.BlockSpec((1, LANES, q), lambda c: (c, 0, 0))],
        out_shape=[jax.ShapeDtypeStruct((t, LANES), F32), jax.ShapeDtypeStruct((t, LANES), F32),
                   jax.ShapeDtypeStruct((nc, LANES, q), F32)],
        compiler_params=_cp("parallel"))(proj, dt_bias_l, a_log_l)


def _head_cols(blk, idx):
    lane = lax.broadcasted_iota(jnp.int32, blk.shape, 1)
    return jnp.sum(jnp.where(lane == idx, blk, 0.0), axis=1, keepdims=True)


class _HeadVals:
    pass


def _lane_head(shape):
    return lax.broadcasted_iota(jnp.int32, shape, len(shape) - 1) >> 6


def _group_heads(g, gi, dtb, acsb, acst_ref, dskb):
    q = dtb.shape[0]
    hv = _HeadVals()
    lh = _lane_head((1, GROUP_W))
    hv.dt = jnp.zeros((q, GROUP_W), F32)
    hv.acs = jnp.zeros((q, GROUP_W), F32)
    hv.acs_last = jnp.zeros((1, GROUP_W), F32)
    hv.dsk = jnp.zeros((1, GROUP_W), F32)
    rows = []
    for r in range(HPG):
        idx = GROUPS * g + r
        sel = lh == r
        acs_r = acst_ref[0, GROUPS * gi + r:GROUPS * gi + r + 1, :]
        rows.append(acs_r)
        hv.dt = jnp.where(sel, _head_cols(dtb, idx), hv.dt)
        hv.acs = jnp.where(sel, _head_cols(acsb, idx), hv.acs)
        hv.acs_last = jnp.where(sel, acs_r[:, q - 1:q], hv.acs_last)
        hv.dsk = jnp.where(sel, _head_cols(dskb, idx), hv.dsk)
    hv.acs_row = jnp.concatenate(rows, axis=1)
    return hv


def _head_tri(q, lower):
    ri = lax.broadcasted_iota(jnp.int32, (q, GROUP_W), 0)
    li = lax.broadcasted_iota(jnp.int32, (q, GROUP_W), 1) & (HEAD_DIM - 1)
    return ri >= li if lower else ri <= li


def _block_diag(v):
    rb = lax.broadcasted_iota(jnp.int32, (GROUP_W, GROUP_W), 0) >> 6
    cb = lax.broadcasted_iota(jnp.int32, (GROUP_W, GROUP_W), 1) >> 6
    return jnp.where(rb == cb, jnp.concatenate([v] * HPG, axis=0), jnp.zeros((), v.dtype))


def _head_sums(v, r):
    return jnp.sum(jnp.where(_lane_head((1, GROUP_W)) == r, v, 0.0), axis=1, keepdims=True)


def _ssd_fwd(xbc, proj, dt, acs, acst, d_skip_l, ssd_norm_w):
    t = xbc.shape[0]
    q = CHUNK
    nc = t // q

    gps = SSD_GROUPS_PER_STEP
    gw, sw = gps * GROUP_W, gps * D_STATE

    def body(xs_ref, b_ref, c_ref, dt_ref, acs_ref, acst_ref, z_ref, dsk_ref, nw_ref,
             y_ref, ymix_ref, st_ref, state):
        @pl.when(pl.program_id(1) == 0)
        def _():
            state[...] = jnp.zeros_like(state)

        for gi in range(gps):
            g = gps * pl.program_id(0) + gi
            cols = slice(GROUP_W * gi, GROUP_W * (gi + 1))
            x = xs_ref[:, cols]
            bmb = _bf(b_ref[:, D_STATE * gi:D_STATE * (gi + 1)])
            cmb = _bf(c_ref[:, D_STATE * gi:D_STATE * (gi + 1)])
            hv = _group_heads(g, gi, dt_ref[...], acs_ref[...], acst_ref, dsk_ref[...])
            decay = jnp.exp(jnp.where(_head_tri(q, True), hv.acs - hv.acs_row, NEG))
            m_all = _bf(_nt(cmb, jnp.concatenate([bmb] * HPG, axis=0)) * decay)
            xdt = x * hv.dt
            s_prev = state[gi]
            st_ref[0, gi] = s_prev
            y = (_mm(m_all, _block_diag(_bf(xdt))) + _mm(cmb, _bf(s_prev)) * jnp.exp(hv.acs) + hv.dsk * x)
            state[gi] = jnp.exp(hv.acs_last) * s_prev + _tn(bmb, _bf(xdt * jnp.exp(hv.acs_last - hv.acs)))
            y_ref[:, cols] = y
            z = z_ref[:, cols]
            yg = y * (z * _sigmoid(z))
            ms = jnp.mean(yg * yg, axis=-1, keepdims=True)
            ymix_ref[:, cols] = _bf(yg * lax.rsqrt(ms + EPS) * nw_ref[:, cols])

    return _pallas(
        body, name="ssd_fwd", grid=(GROUPS // gps, nc),
        in_specs=[pl.BlockSpec((q, gw), lambda g, c: (c, g)),
                  pl.BlockSpec((q, sw), lambda g, c: (c, D_SSD // sw + g)),
                  pl.BlockSpec((q, sw), lambda g, c: (c, (D_SSD + GROUPS * D_STATE) // sw + g)),
                  pl.BlockSpec((q, LANES), lambda g, c: (c, 0)), pl.BlockSpec((q, LANES), lambda g, c: (c, 0)),
                  pl.BlockSpec((1, gps * GROUPS, q), lambda g, c: (c, g, 0)),
                  pl.BlockSpec((q, gw), lambda g, c: (c, g)),
                  pl.BlockSpec((1, LANES), lambda g, c: (0, 0)), pl.BlockSpec((1, gw), lambda g, c: (0, g))],
        out_specs=[pl.BlockSpec((q, gw), lambda g, c: (c, g)), pl.BlockSpec((q, gw), lambda g, c: (c, g)),
                   pl.BlockSpec((1, gps, D_STATE, GROUP_W), lambda g, c: (c, g, 0, 0))],
        out_shape=[jax.ShapeDtypeStruct((t, D_SSD), F32), jax.ShapeDtypeStruct((t, D_SSD), BF16),
                   jax.ShapeDtypeStruct((nc, GROUPS, D_STATE, GROUP_W), F32)],
        scratch_shapes=[pltpu.VMEM((gps, D_STATE, GROUP_W), F32)],
        compiler_params=_cp("parallel", "arbitrary"))(xbc, xbc, xbc, dt, acs, acst, proj, d_skip_l, ssd_norm_w)


def _swap_halves(v):
    lane = lax.broadcasted_iota(jnp.int32, v.shape, 1)
    return jnp.where((lane & (HEAD_DIM - 1)) < HEAD_DIM // 2, pltpu.roll(v, LANES - HEAD_DIM // 2, 1),
                     pltpu.roll(v, HEAD_DIM // 2, 1))


def _rope(qsrc, q_off, ksrc, k_off, cos_t, sin_t):
    t = qsrc.shape[0]
    tr = _tile(t, 832)

    def body(q_ref, k_ref, cos_ref, sin_ref, qo_ref, ko_ref):
        cs = cos_ref[...]
        sn = sin_ref[...]
        for src, dst, width in ((q_ref, qo_ref, D_ATT), (k_ref, ko_ref, D_KV)):
            for s in range(width // LANES):
                v = src[:, LANES * s:LANES * (s + 1)].astype(F32)
                dst[:, LANES * s:LANES * (s + 1)] = _bf(v * cs + _swap_halves(v) * sn)

    return _pallas(
        body, name="rope", grid=(t // tr,),
        in_specs=[pl.BlockSpec((tr, D_ATT), lambda i: (i, q_off // D_ATT)),
                  pl.BlockSpec((tr, D_KV), lambda i: (i, k_off // D_KV)),
                  pl.BlockSpec((tr, LANES), lambda i: (i, 0)), pl.BlockSpec((tr, LANES), lambda i: (i, 0))],
        out_specs=[pl.BlockSpec((tr, D_ATT), lambda i: (i, 0)), pl.BlockSpec((tr, D_KV), lambda i: (i, 0))],
        out_shape=[jax.ShapeDtypeStruct((t, D_ATT), BF16), jax.ShapeDtypeStruct((t, D_KV), BF16)],
        compiler_params=_cp("parallel"))(qsrc, ksrc, cos_t, sin_t)


def _band_specs(width, col_block):
    return [pl.BlockSpec((CHUNK, width), functools.partial(lambda c, j: (jnp.maximum(c - j, 0), col_block), j=j))
            for j in (2, 1, 0)]


def _attn_probs(qh, kb, sink_col, valid):
    s = _nt(qh, kb) * (HEAD_DIM ** -0.5)
    s = jnp.where(valid, s, NEG)
    m = jnp.maximum(jnp.max(s, axis=1, keepdims=True), sink_col)
    p = jnp.exp(s - m)
    psink = jnp.exp(sink_col - m)
    inv = 1.0 / (jnp.sum(p, axis=1, keepdims=True) + psink)
    return p * inv, psink * inv


def _attn_operands(c, q_ref, k_refs, v_refs, sink_ref, h):
    q = q_ref[...]
    qh = jnp.concatenate([q[:, HEAD_DIM * (REP * h + r):HEAD_DIM * (REP * h + r + 1)] for r in range(REP)], axis=0)
    kb = jnp.concatenate([k[:, HEAD_DIM * h:HEAD_DIM * (h + 1)] for k in k_refs], axis=0)
    vb = jnp.concatenate([_bf(v[:, HEAD_DIM * h:HEAD_DIM * (h + 1)]) for v in v_refs], axis=0)
    rows = lax.broadcasted_iota(jnp.int32, (REP * CHUNK, 1), 0) >> 6
    sink_col = jnp.zeros((REP * CHUNK, 1), F32)
    for r in range(REP):
        sink_col = jnp.where(rows == r, sink_ref[REP * h + r], sink_col)
    key_abs = (c - (BAND_CHUNKS - 1)) * CHUNK + lax.broadcasted_iota(jnp.int32, (1, BAND_CHUNKS * CHUNK), 1)
    return qh, kb, vb, sink_col, key_abs >= PAD_LEAD


def _attn_fwd(qr, kr, proj, sinks):
    t = qr.shape[0]
    nc = t // CHUNK

    def body(q_ref, k2, k1, k0, v2, v1, v0, g_ref, sink_ref, o_ref):
        c = pl.program_id(0)
        ks = [k2[...], k1[...], k0[...]]
        vs = [v2[...], v1[...], v0[...]]
        outs = []
        for h in range(KV_HEADS):
            qh, kb, vb, sink_col, valid = _attn_operands(c, q_ref, ks, vs, sink_ref, h)
            p, _ = _attn_probs(qh, kb, sink_col, valid)
            o = _mm(_bf(p), vb)
            outs += [o[CHUNK * r:CHUNK * (r + 1)] for r in range(REP)]
        att = jnp.concatenate(outs, axis=1)
        gate = g_ref[...]
        o_ref[...] = _bf(att * (gate * _sigmoid(gate)))

    return _pallas(
        body, name="attn_fwd", grid=(nc,),
        in_specs=[pl.BlockSpec((CHUNK, D_ATT), lambda c: (c, 0))] + _band_specs(D_KV, 0)
        + _band_specs(D_KV, OV // D_KV) + [pl.BlockSpec((CHUNK, D_ATT), lambda c: (c, OG // D_ATT)),
                                           pl.BlockSpec(memory_space=pltpu.SMEM)],
        out_specs=pl.BlockSpec((CHUNK, D_ATT), lambda c: (c, 0)),
        out_shape=jax.ShapeDtypeStruct((t, D_ATT), BF16),
        compiler_params=_cp("parallel"))(qr, kr, kr, kr, proj, proj, proj, proj, sinks)


def _outproj(ymix, amix, w_out):
    t = ymix.shape[0]
    tm, tn = _tile(t, 832), 512

    def body(y_ref, a_ref, wy_ref, wa_ref, o_ref):
        o_ref[...] = _mm(y_ref[...], wy_ref[...]) + _mm(a_ref[...], wa_ref[...])

    return _pallas(
        body, name="outproj", grid=(t // tm, D_MODEL // tn),
        in_specs=[pl.BlockSpec((tm, D_SSD), lambda i, j: (i, 0)), pl.BlockSpec((tm, D_ATT), lambda i, j: (i, 0)),
                  pl.BlockSpec((D_SSD, tn), lambda i, j: (0, j)),
                  pl.BlockSpec((D_ATT, tn), lambda i, j: (D_SSD // D_ATT, j))],
        out_specs=pl.BlockSpec((tm, tn), lambda i, j: (i, j)),
        out_shape=jax.ShapeDtypeStruct((t, D_MODEL), F32),
        compiler_params=_cp("parallel", "parallel"))(ymix, amix, w_out, w_out)


def _post_loss(out, x, target, norm_post_w):
    t = out.shape[0]
    nc = t // CHUNK

    def body(o_ref, x_ref, tg_ref, nw_ref, dout_ref, dy_ref, loss_ref, gnw_ref):
        i = pl.program_id(0)

        @pl.when(i == 0)
        def _():
            dout_ref[...] = jnp.zeros_like(dout_ref)
            dy_ref[...] = jnp.zeros_like(dy_ref)
            loss_ref[...] = jnp.zeros_like(loss_ref)
            gnw_ref[...] = jnp.zeros_like(gnw_ref)

        @pl.when(i > 0)
        def _():
            o = o_ref[...]
            nw = nw_ref[...]
            rstd = lax.rsqrt(jnp.mean(o * o, axis=-1, keepdims=True) + EPS)
            n = o * rstd
            err = x_ref[...] + n * nw - tg_ref[...]
            loss_ref[...] += jnp.sum(err * err) * (0.5 / D_MODEL)
            dy = err * (1.0 / D_MODEL)
            dy_ref[...] = dy
            gnw_ref[...] += jnp.sum(dy * n, axis=0, keepdims=True)
            dn = dy * nw
            dout_ref[...] = _bf(rstd * (dn - n * jnp.mean(dn * n, axis=-1, keepdims=True)))

    prev = lambda i: (jnp.maximum(i - 1, 0), 0)
    return _pallas(
        body, name="post_loss", grid=(nc,),
        in_specs=[pl.BlockSpec((CHUNK, D_MODEL), lambda i: (i, 0)), pl.BlockSpec((CHUNK, D_MODEL), prev),
                  pl.BlockSpec((CHUNK, D_MODEL), prev), pl.BlockSpec((1, D_MODEL), lambda i: (0, 0))],
        out_specs=[pl.BlockSpec((CHUNK, D_MODEL), lambda i: (i, 0)), pl.BlockSpec((CHUNK, D_MODEL), lambda i: (i, 0)),
                   pl.BlockSpec((8, LANES), lambda i: (0, 0)), pl.BlockSpec((1, D_MODEL), lambda i: (0, 0))],
        out_shape=[jax.ShapeDtypeStruct((t, D_MODEL), BF16), jax.ShapeDtypeStruct((t, D_MODEL), F32),
                   jax.ShapeDtypeStruct((8, LANES), F32), jax.ShapeDtypeStruct((1, D_MODEL), F32)],
        compiler_params=_cp("arbitrary"))(out, x, target, norm_post_w)


def _nt_matmul(a, b, name):
    t, k = a.shape
    n = b.shape[0]
    tm, tn = _tile(t, 832), 512

    def body(a_ref, b_ref, o_ref):
        o_ref[...] = _nt(a_ref[...], b_ref[...])

    return _pallas(
        body, name=name, grid=(t // tm, n // tn),
        in_specs=[pl.BlockSpec((tm, k), lambda i, j: (i, 0)), pl.BlockSpec((tn, k), lambda i, j: (j, 0))],
        out_specs=pl.BlockSpec((tm, tn), lambda i, j: (i, j)),
        out_shape=jax.ShapeDtypeStruct((t, n), F32),
        compiler_params=_cp("parallel", "parallel"))(a, b)


def _tn_matmul(a, b, name):
    t, m = a.shape
    n = b.shape[1]
    tk, tm, tn = _tile(t, 832), min(m, 1024), min(n, 1024)
    nk = t // tk

    def body(a_ref, b_ref, o_ref):
        @pl.when(pl.program_id(2) == 0)
        def _():
            o_ref[...] = jnp.zeros_like(o_ref)
        o_ref[...] += _tn(a_ref[...], b_ref[...])

    return _pallas(
        body, name=name, grid=(m // tm, n // tn, nk),
        in_specs=[pl.BlockSpec((tk, tm), lambda i, j, k: (k, i)), pl.BlockSpec((tk, tn), lambda i, j, k: (k, j))],
        out_specs=pl.BlockSpec((tm, tn), lambda i, j, k: (i, j)),
        out_shape=jax.ShapeDtypeStruct((m, n), F32),
        compiler_params=_cp("parallel", "parallel", "arbitrary"))(a, b)


def _attn_bwd(qr, kr, proj, dmix, sinks, ga):
    t = qr.shape[0]
    nc = t // CHUNK
    scale = HEAD_DIM ** -0.5

    def body(q_ref, k2, k1, k0, v2, v1, v0, g_ref, da_ref, sink_ref, ga_ref, dq_ref, dg_ref, dk_ref, dv_ref, gs_ref,
             got_ref, send_sems, recv_sems):
        c = pl.program_id(0)

        @pl.when(c == 0)
        def _():
            for cp in _exchange_copies(ga_ref, got_ref, send_sems, recv_sems):
                cp.start()
            dk_ref[...] = jnp.zeros_like(dk_ref)
            dv_ref[...] = jnp.zeros_like(dv_ref)
            gs_ref[...] = jnp.zeros_like(gs_ref)

        ks = [k2[...], k1[...], k0[...]]
        vs = [v2[...], v1[...], v0[...]]
        gate = g_ref[...]
        sg = _sigmoid(gate)
        da = da_ref[...]
        datt = da * (gate * sg)
        lane = lax.broadcasted_iota(jnp.int32, (1, LANES), 1)
        rows = lax.broadcasted_iota(jnp.int32, (REP * CHUNK, 1), 0) >> 6
        dqs, atts, dks, dvs = [], [], [], []
        gs = jnp.zeros((1, LANES), F32)
        for h in range(KV_HEADS):
            qh, kb, vb, sink_col, valid = _attn_operands(c, q_ref, ks, vs, sink_ref, h)
            p, psink = _attn_probs(qh, kb, sink_col, valid)
            pb = _bf(p)
            o = _mm(pb, vb)
            do = jnp.concatenate([datt[:, HEAD_DIM * (REP * h + r):HEAD_DIM * (REP * h + r + 1)] for r in range(REP)],
                                 axis=0)
            dob = _bf(do)
            delta = jnp.sum(do * o, axis=1, keepdims=True)
            ds = _bf(p * (_nt(dob, vb) - delta) * scale)
            gsink = -psink * delta
            for r in range(REP):
                gs = gs + jnp.where(lane == REP * h + r, jnp.sum(jnp.where(rows == r, gsink, 0.0)), 0.0)
            dqh = _mm(ds, kb)
            dqs += [dqh[CHUNK * r:CHUNK * (r + 1)] for r in range(REP)]
            atts += [o[CHUNK * r:CHUNK * (r + 1)] for r in range(REP)]
            dks.append(_tn(ds, qh))
            dvs.append(_tn(pb, dob))
        dq_ref[...] = jnp.concatenate(dqs, axis=1)
        att = jnp.concatenate(atts, axis=1)
        dg_ref[...] = _bf(da * att * (sg * (1.0 + gate * (1.0 - sg))))
        gs_ref[0:1, :] += gs
        dkf = jnp.concatenate(dks, axis=1)
        dvf = jnp.concatenate(dvs, axis=1)
        for j in range(BAND_CHUNKS):
            r0 = pl.multiple_of(jnp.maximum(c - (BAND_CHUNKS - 1) + j, 0) * CHUNK, CHUNK)
            dk_ref[pl.ds(r0, CHUNK), :] += dkf[CHUNK * j:CHUNK * (j + 1)]
            dv_ref[pl.ds(r0, CHUNK), :] += dvf[CHUNK * j:CHUNK * (j + 1)]

        @pl.when(c == nc - 1)
        def _():
            for cp in _exchange_copies(ga_ref, got_ref, send_sems, recv_sems):
                cp.wait()

    return _pallas(
        body, name="attn_bwd", grid=(nc,),
        in_specs=[pl.BlockSpec((CHUNK, D_ATT), lambda c: (c, 0))] + _band_specs(D_KV, 0)
        + _band_specs(D_KV, OV // D_KV) + [pl.BlockSpec((CHUNK, D_ATT), lambda c: (c, OG // D_ATT)),
                                           pl.BlockSpec((CHUNK, D_ATT), lambda c: (c, D_SSD // D_ATT)),
                                           pl.BlockSpec(memory_space=pltpu.SMEM), ANY],
        out_specs=[pl.BlockSpec((CHUNK, D_ATT), lambda c: (c, 0)), pl.BlockSpec((CHUNK, D_ATT), lambda c: (c, 0)),
                   pl.BlockSpec((t, D_KV), lambda c: (0, 0)), pl.BlockSpec((t, D_KV), lambda c: (0, 0)),
                   pl.BlockSpec((8, LANES), lambda c: (0, 0)), ANY],
        out_shape=[jax.ShapeDtypeStruct((t, D_ATT), F32), jax.ShapeDtypeStruct((t, D_ATT), BF16),
                   jax.ShapeDtypeStruct((t, D_KV), F32), jax.ShapeDtypeStruct((t, D_KV), F32),
                   jax.ShapeDtypeStruct((8, LANES), F32), _exchange_shape(ga)],
        scratch_shapes=_exchange_scratch(),
        compiler_params=_cp("arbitrary"))(qr, kr, kr, kr, proj, proj, proj, proj, dmix, sinks, ga)


def _ssd_bwd(dmix, y_ssd, xbc, proj, dt, acs, acst, states, d_skip_l, ssd_norm_w):
    t = xbc.shape[0]
    q = CHUNK
    nc = t // q
    gps = SSD_GROUPS_PER_STEP
    gw, sw = gps * GROUP_W, gps * D_STATE

    def body(dmix_ref, y_ref, z_ref, nw_ref, xs_ref, b_ref, c_ref, dt_ref, acs_ref, acst_ref, st_ref, dsk_ref,
             dz_ref, dxs_ref, db_ref, dc_ref, dacs_ref, ddt_ref, gnw_ref, gdsk_ref, dstate):
        @pl.when(pl.program_id(1) == 0)
        def _():
            dstate[...] = jnp.zeros_like(dstate)
            gnw_ref[...] = jnp.zeros_like(gnw_ref)
            gdsk_ref[...] = jnp.zeros_like(gdsk_ref)

        last_row = lax.broadcasted_iota(jnp.int32, (q, 1), 0) == q - 1
        lane = lax.broadcasted_iota(jnp.int32, (q, LANES), 1)
        lane1 = lax.broadcasted_iota(jnp.int32, (8, LANES), 1)
        for gi in range(gps):
            g = gps * pl.program_id(0) + gi
            cols = slice(GROUP_W * gi, GROUP_W * (gi + 1))
            scols = slice(D_STATE * gi, D_STATE * (gi + 1))
            y = y_ref[:, cols]
            z = z_ref[:, cols]
            sz = _sigmoid(z)
            silu_z = z * sz
            yg = y * silu_z
            rstd = lax.rsqrt(jnp.mean(yg * yg, axis=-1, keepdims=True) + EPS)
            n = yg * rstd
            dout = dmix_ref[:, cols]
            gnw_ref[:, cols] += jnp.sum(dout * n, axis=0, keepdims=True)
            dn = dout * nw_ref[:, cols]
            dyg = rstd * (dn - n * jnp.mean(dn * n, axis=-1, keepdims=True))
            dy = dyg * silu_z
            dz_ref[:, cols] = _bf(dyg * y * (sz * (1.0 + z * (1.0 - sz))))

            x = xs_ref[:, cols]
            bmb, cmb = _bf(b_ref[:, scols]), _bf(c_ref[:, scols])
            hv = _group_heads(g, gi, dt_ref[...], acs_ref[...], acst_ref, dsk_ref[...])
            dec = jnp.exp(jnp.where(_head_tri(q, True), hv.acs - hv.acs_row, NEG))
            dect = jnp.exp(jnp.where(_head_tri(q, False), hv.acs_row - hv.acs, NEG))
            b4 = jnp.concatenate([bmb] * HPG, axis=0)
            c4 = jnp.concatenate([cmb] * HPG, axis=0)
            m_all = _nt(cmb, b4) * dec
            mt_all = _nt(bmb, c4) * dect
            xdt = x * hv.dt
            xdt_b, dyb = _bf(xdt), _bf(dy)
            x_bd, dy_bd = _block_diag(xdt_b), _block_diag(dyb)
            s_prev = st_ref[0, gi]
            spb = _bf(s_prev)
            ds_new = dstate[gi]
            dsb = _bf(ds_new)
            e = jnp.exp(hv.acs)
            elast = jnp.exp(hv.acs_last)
            dte = jnp.exp(hv.acs_last - hv.acs)
            bds = _mm(bmb, dsb)
            dxdt = _mm(_bf(mt_all), dy_bd) + bds * dte
            dm = _nt(dyb, x_bd)
            dmt = _nt(xdt_b, dy_bd)
            dye = _bf(dy * e)
            dc_ref[:, scols] = _mm(_bf(dm * dec), b4) + _nt(dye, spb)
            db_ref[:, scols] = _mm(_bf(dmt * dect), c4) + _nt(_bf(xdt * dte), dsb)
            dstate[gi] = elast * ds_new + _tn(cmb, dye)
            dxs_ref[:, cols] = dxdt * hv.dt + hv.dsk * dy
            ddte_dte = bds * xdt * dte
            dacs_l = dm * m_all - dmt * mt_all + dy * _mm(cmb, spb) * e - ddte_dte
            dlast_l = (jnp.sum(ddte_dte, axis=0, keepdims=True)
                       + jnp.sum(s_prev * ds_new, axis=0, keepdims=True) * elast)
            ddt_l = dxdt * x
            gdsk_l = jnp.sum(dy * x, axis=0, keepdims=True)
            dacs_out = jnp.zeros((q, LANES), F32)
            ddt_out = jnp.zeros((q, LANES), F32)
            gdsk = jnp.zeros((8, LANES), F32)
            for r in range(HPG):
                dacs = _head_sums(dacs_l, r) + jnp.where(last_row, _head_sums(dlast_l, r), 0.0)
                dacs_out = jnp.where(lane == r, dacs, dacs_out)
                ddt_out = jnp.where(lane == r, _head_sums(ddt_l, r), ddt_out)
                gdsk = gdsk + jnp.where(lane1 == r, _head_sums(gdsk_l, r), 0.0)
            dacs_ref[:, LANES * gi:LANES * (gi + 1)] = dacs_out
            ddt_ref[:, LANES * gi:LANES * (gi + 1)] = ddt_out
            gdsk_ref[gi] += gdsk

    rev = lambda c: nc - 1 - c
    wide = pl.BlockSpec((q, gw), lambda g, c: (rev(c), g))
    return _pallas(
        body, name="ssd_bwd", grid=(GROUPS // gps, nc),
        in_specs=[wide, wide, wide, pl.BlockSpec((1, gw), lambda g, c: (0, g)), wide,
                  pl.BlockSpec((q, sw), lambda g, c: (rev(c), D_SSD // sw + g)),
                  pl.BlockSpec((q, sw), lambda g, c: (rev(c), (D_SSD + GROUPS * D_STATE) // sw + g)),
                  pl.BlockSpec((q, LANES), lambda g, c: (rev(c), 0)), pl.BlockSpec((q, LANES), lambda g, c: (rev(c), 0)),
                  pl.BlockSpec((1, gps * GROUPS, q), lambda g, c: (rev(c), g, 0)),
                  pl.BlockSpec((1, gps, D_STATE, GROUP_W), lambda g, c: (rev(c), g, 0, 0)),
                  pl.BlockSpec((1, LANES), lambda g, c: (0, 0))],
        out_specs=[wide, wide,
                   pl.BlockSpec((q, sw), lambda g, c: (rev(c), g)), pl.BlockSpec((q, sw), lambda g, c: (rev(c), g)),
                   pl.BlockSpec((q, gps * LANES), lambda g, c: (rev(c), g)),
                   pl.BlockSpec((q, gps * LANES), lambda g, c: (rev(c), g)),
                   pl.BlockSpec((1, gw), lambda g, c: (0, g)), pl.BlockSpec((gps, 8, LANES), lambda g, c: (g, 0, 0))],
        out_shape=[jax.ShapeDtypeStruct((t, D_SSD), BF16), jax.ShapeDtypeStruct((t, D_SSD), F32),
                   jax.ShapeDtypeStruct((t, GROUPS * D_STATE), F32), jax.ShapeDtypeStruct((t, GROUPS * D_STATE), F32),
                   jax.ShapeDtypeStruct((t, GROUPS * LANES), F32), jax.ShapeDtypeStruct((t, GROUPS * LANES), F32),
                   jax.ShapeDtypeStruct((1, D_SSD), F32), jax.ShapeDtypeStruct((GROUPS, 8, LANES), F32)],
        scratch_shapes=[pltpu.VMEM((gps, D_STATE, GROUP_W), F32)],
        compiler_params=_cp("parallel", "arbitrary"))(dmix, y_ssd, proj, ssd_norm_w, xbc, xbc, xbc, dt, acs, acst,
                                                      states, d_skip_l)


def _dt_bwd(dacs_g, ddt_g, dt, proj, dt_bias_l, a_log_l):
    t = dt.shape[0]
    q = CHUNK
    nc = t // q

    def body(dacs_ref, ddt_ref, dt_ref, raw_ref, bias_ref, alog_ref, draw_ref, ga_ref, gb_ref):
        c = pl.program_id(0)

        @pl.when(c == 0)
        def _():
            ga_ref[...] = jnp.zeros_like(ga_ref)
            gb_ref[...] = jnp.zeros_like(gb_ref)

        lane = lax.broadcasted_iota(jnp.int32, (q, LANES), 1)
        dacs = jnp.zeros((q, LANES), F32)
        ddt = jnp.zeros((q, LANES), F32)
        for g in range(GROUPS):
            mask = (lane >= GROUPS * g) & (lane < GROUPS * g + HPG)
            sl = slice(LANES * g, LANES * (g + 1))
            if g == 0:
                dacs = jnp.where(mask, dacs_ref[:, sl], dacs)
                ddt = jnp.where(mask, ddt_ref[:, sl], ddt)
            else:
                dacs = jnp.where(mask, pltpu.roll(dacs_ref[:, sl], GROUPS * g, 1), dacs)
                ddt = jnp.where(mask, pltpu.roll(ddt_ref[:, sl], GROUPS * g, 1), ddt)
        ri = lax.broadcasted_iota(jnp.int32, (q, q), 0)
        ci = lax.broadcasted_iota(jnp.int32, (q, q), 1)
        triu = (ri <= ci).astype(F32)
        dda = jnp.dot(triu, dacs, preferred_element_type=F32, precision=HI)
        a = -jnp.exp(alog_ref[...])
        dtv = dt_ref[...]
        row = c * q + lax.broadcasted_iota(jnp.int32, (q, LANES), 0)
        used = (lane & (GROUPS - 1)) < HPG
        dsp = jnp.where((row >= PAD_LEAD) & used, dda * a + ddt, 0.0)
        draw = dsp * _sigmoid(raw_ref[...] + bias_ref[...])
        draw_ref[...] = _bf(draw)
        gb_ref[0:1, :] += jnp.sum(draw, axis=0, keepdims=True)
        ga_ref[0:1, :] += jnp.sum(jnp.where(used, dda * dtv, 0.0), axis=0, keepdims=True) * a

    return _pallas(
        body, name="dt_bwd", grid=(nc,),
        in_specs=[pl.BlockSpec((q, GROUPS * LANES), lambda c: (c, 0)), pl.BlockSpec((q, GROUPS * LANES), lambda c: (c, 0)),
                  pl.BlockSpec((q, LANES), lambda c: (c, 0)), pl.BlockSpec((q, LANES), lambda c: (c, ODT // LANES)),
                  pl.BlockSpec((1, LANES), lambda c: (0, 0)), pl.BlockSpec((1, LANES), lambda c: (0, 0))],
        out_specs=[pl.BlockSpec((q, LANES), lambda c: (c, 0)), pl.BlockSpec((8, LANES), lambda c: (0, 0)),
                   pl.BlockSpec((8, LANES), lambda c: (0, 0))],
        out_shape=[jax.ShapeDtypeStruct((t, LANES), BF16), jax.ShapeDtypeStruct((8, LANES), F32),
                   jax.ShapeDtypeStruct((8, LANES), F32)],
        compiler_params=_cp("arbitrary"))(dacs_g, ddt_g, dt, proj, dt_bias_l, a_log_l)


def _conv_bwd(dseg, proj, conv_w, conv_b, col_off, name):
    t, width = dseg.shape
    tc = 128
    off_p = (OXS + col_off) // tc
    off_w = col_off // tc

    def body(d_ref, x_ref, w_ref, b_ref, dx_ref, gw_ref, gb_ref, xp, dup):
        xp[0:8, :] = jnp.zeros((8, tc), F32)
        xp[8:t + 8, :] = x_ref[...]
        w = w_ref[...]
        u = (b_ref[...] + w[3:4, :] * xp[8:t + 8, :] + w[2:3, :] * xp[7:t + 7, :]
             + w[1:2, :] * xp[6:t + 6, :] + w[0:1, :] * xp[5:t + 5, :])
        su = _sigmoid(u)
        du = d_ref[...] * (su * (1.0 + u * (1.0 - su)))
        dup[0:t, :] = du
        dup[t:t + 8, :] = jnp.zeros((8, tc), F32)
        dx_ref[...] = _bf(w[3:4, :] * du + w[2:3, :] * dup[1:t + 1, :] + w[1:2, :] * dup[2:t + 2, :]
                          + w[0:1, :] * dup[3:t + 3, :])
        gb_ref[...] = jnp.sum(du, axis=0, keepdims=True)
        gw_ref[...] = jnp.concatenate(
            [jnp.sum(du * xp[5 + k:t + 5 + k, :], axis=0, keepdims=True) for k in range(CONV_WIDTH)], axis=0)

    return _pallas(
        body, name=name, grid=(width // tc,),
        in_specs=[pl.BlockSpec((t, tc), lambda j: (0, j)), pl.BlockSpec((t, tc), lambda j: (0, j + off_p)),
                  pl.BlockSpec((CONV_WIDTH, tc), lambda j: (0, j + off_w)), pl.BlockSpec((1, tc), lambda j: (0, j + off_w))],
        out_specs=[pl.BlockSpec((t, tc), lambda j: (0, j)), pl.BlockSpec((CONV_WIDTH, tc), lambda j: (0, j)),
                   pl.BlockSpec((1, tc), lambda j: (0, j))],
        out_shape=[jax.ShapeDtypeStruct((t, width), BF16), jax.ShapeDtypeStruct((CONV_WIDTH, width), F32),
                   jax.ShapeDtypeStruct((1, width), F32)],
        scratch_shapes=[pltpu.VMEM((t + 8, tc), F32), pltpu.VMEM((t + 8, tc), F32)],
        compiler_params=_cp("parallel"))(dseg, proj, conv_w, conv_b)


def _dinproj(dproj, w_re, hpad, norm_w, dy_t, ga):
    t, n = dproj.shape
    d = hpad.shape[1]
    tm, tk = _tile(t, 416), 1024
    nk = n // tk
    ni = t // tm

    def body(dp_ref, w_ref, h_ref, nw_ref, dy_ref, ga_ref, dh_ref, gnw_ref, got_ref, acc, send_sems, recv_sems):
        i, k = pl.program_id(0), pl.program_id(1)

        @pl.when((i == 0) & (k == 0))
        def _():
            for cp in _exchange_copies(ga_ref, got_ref, send_sems, recv_sems):
                cp.start()
            gnw_ref[...] = jnp.zeros_like(gnw_ref)

        @pl.when(k == 0)
        def _():
            acc[...] = jnp.zeros_like(acc)

        acc[...] += _nt(dp_ref[...], w_ref[...])

        @pl.when(k == nk - 1)
        def _():
            h = h_ref[...]
            rstd = lax.rsqrt(jnp.mean(h * h, axis=-1, keepdims=True) + EPS)
            nrm = h * rstd
            dhn = acc[...]
            gnw_ref[...] += jnp.sum(dhn * nrm, axis=0, keepdims=True)
            dn = dhn * nw_ref[...]
            dh_ref[...] = rstd * (dn - nrm * jnp.mean(dn * nrm, axis=-1, keepdims=True)) + dy_ref[...]

        @pl.when((i == ni - 1) & (k == nk - 1))
        def _():
            for cp in _exchange_copies(ga_ref, got_ref, send_sems, recv_sems):
                cp.wait()

    return _pallas(
        body, name="dinproj", grid=(ni, nk),
        in_specs=[pl.BlockSpec((tm, tk), lambda i, k: (i, k)), pl.BlockSpec((d, tk), lambda i, k: (0, k)),
                  pl.BlockSpec((tm, d), lambda i, k: (i, 0)), pl.BlockSpec((1, d), lambda i, k: (0, 0)),
                  pl.BlockSpec((tm, d), lambda i, k: (i, 0)), ANY],
        out_specs=[pl.BlockSpec((tm, d), lambda i, k: (i, 0)), pl.BlockSpec((1, d), lambda i, k: (0, 0)), ANY],
        out_shape=[jax.ShapeDtypeStruct((t, d), F32), jax.ShapeDtypeStruct((1, d), F32), _exchange_shape(ga)],
        scratch_shapes=[pltpu.VMEM((tm, d), F32)] + _exchange_scratch(),
        compiler_params=_cp("arbitrary", "arbitrary"))(dproj, w_re, hpad, norm_w, dy_t, ga)


def _spread_heads(v):
    v = jnp.pad(v.reshape(GROUPS, HPG), ((0, 0), (0, GROUPS - HPG))).reshape(1, GROUPS * GROUPS)
    return jnp.pad(v, ((0, 0), (0, LANES - GROUPS * GROUPS)))


def _gather_heads(v):
    return v[0:1, :GROUPS * GROUPS].reshape(GROUPS, GROUPS)[:, :HPG].reshape(1, SSD_HEADS)


def _rope_tables(t):
    half = HEAD_DIM // 2
    inv = ROPE_THETA ** (-jnp.arange(half, dtype=F32) / half)
    pos = (jnp.arange(t) - PAD_LEAD).astype(F32)
    ang = pos[:, None] * inv[None, :]
    cos, sin = jnp.cos(ang), jnp.sin(ang)
    cos_t = jnp.concatenate([cos, cos, cos, cos], axis=1)
    sin_t = jnp.concatenate([-sin, sin, -sin, sin], axis=1)
    return cos_t, sin_t


def _column_pieces():
    runs = [(0, OB + 2 * GROUPS * D_STATE, 0)]
    o = OB + 2 * GROUPS * D_STATE
    runs += [(o + HPG * g, HPG, ODT + GROUPS * g) for g in range(GROUPS)]
    o += SSD_HEADS
    for width, dst in ((D_ATT, OQ), (D_KV, OK), (D_KV, OV), (D_ATT, OG)):
        runs.append((o, width, dst))
        o += width
    assert o == D_IN
    pieces = []
    for o0, width, dst in runs:
        for j in range(N_SHARD):
            lo, hi = max(o0, W_IN_SHARD * j), min(o0 + width, W_IN_SHARD * (j + 1))
            if lo < hi:
                pieces.append((j, lo - W_IN_SHARD * j, hi - W_IN_SHARD * j, dst + lo - o0))
    return pieces


def _shards_to_re(w_all):
    _, k, _ = w_all.shape
    tr = 256

    def body(x_ref, o_ref):
        o_ref[:, ODT:ODT + DT_SLAB] = jnp.zeros((tr, DT_SLAB), o_ref.dtype)
        for j, c0, c1, d0 in _column_pieces():
            o_ref[:, d0:d0 + c1 - c0] = x_ref[j, :, c0:c1]

    return _pallas(body, name="shards_to_re", grid=(k // tr,),
                   in_specs=[pl.BlockSpec((N_SHARD, tr, W_IN_SHARD), lambda i: (0, i, 0))],
                   out_specs=pl.BlockSpec((tr, N_RE), lambda i: (i, 0)),
                   out_shape=jax.ShapeDtypeStruct((k, N_RE), w_all.dtype), compiler_params=_cp("parallel"))(w_all)


def _re_rows_to_shards(gt):
    k = gt.shape[1]
    tc = 128

    def body(g_ref, o_ref):
        for j, c0, c1, d0 in _column_pieces():
            o_ref[j, c0:c1, :] = g_ref[d0:d0 + c1 - c0, :]

    return _pallas(body, name="re_rows_to_shards", grid=(k // tc,),
                   in_specs=[pl.BlockSpec((N_RE, tc), lambda i: (0, i))],
                   out_specs=pl.BlockSpec((N_SHARD, W_IN_SHARD, tc), lambda i: (0, 0, i)),
                   out_shape=jax.ShapeDtypeStruct((N_SHARD, W_IN_SHARD, k), gt.dtype), compiler_params=_cp("parallel"))(gt)


def _local_step(x, target, meta, norm_pre_w, w_re, conv_w, conv_b, dt_bias, a_log, d_skip, ssd_norm_w, sinks,
                w_out_shard, norm_post_w, place):
    seq = x.shape[0]
    t = PAD_LEAD + N_META + seq
    hpad = jnp.concatenate([jnp.zeros((PAD_LEAD, D_MODEL), F32), meta, x], axis=0)
    dt_bias_l, a_log_l, d_skip_l = _spread_heads(dt_bias), _spread_heads(a_log), _spread_heads(d_skip)
    cos_t, sin_t = _rope_tables(t)
    sink_v = sinks.reshape(Q_HEADS)

    proj, hn, w_out_all = _inproj(hpad, norm_pre_w, w_re, w_out_shard)
    w_out = w_out_all.reshape(D_MIX, D_MODEL)
    xbc = _conv_fwd(proj, conv_w, conv_b)
    dt, acs, acst = _dt_prep(proj, dt_bias_l, a_log_l)
    y_ssd, ymix, states = _ssd_fwd(xbc, proj, dt, acs, acst, d_skip_l, ssd_norm_w)
    qr, kr = _rope(proj, OQ, proj, OK, cos_t, sin_t)
    amix = _attn_fwd(qr, kr, proj, sink_v)
    out = _outproj(ymix, amix, w_out)
    dout, dy_t, loss_blk, g_norm_post = _post_loss(out, x, target, norm_post_w)

    dmix = _nt_matmul(dout, w_out, "dmix")
    g_w_out = jnp.concatenate([_tn_matmul(ymix, dout, "gw_out_y"), _tn_matmul(amix, dout, "gw_out_a")], axis=0)
    ga_out = _reduce_pair(g_w_out.reshape(N_SHARD, W_OUT_SHARD, D_MODEL), place, "gw_out")
    dq_r, dg, dk_r, dv, gs, slabs_out = _attn_bwd(qr, kr, proj, dmix, sink_v, ga_out)
    g_w_out = _reduce_finish(ga_out, slabs_out, place, "gw_out")
    dq, dk = _rope(dq_r, 0, dk_r, 0, cos_t, -sin_t)
    dz, dxs, db, dc, dacs_g, ddt_g, g_ssd_norm, gdsk = _ssd_bwd(dmix, y_ssd, xbc, proj, dt, acs, acst, states,
                                                                d_skip_l, ssd_norm_w)
    draw, ga, gb = _dt_bwd(dacs_g, ddt_g, dt, proj, dt_bias_l, a_log_l)
    dxs_p, gcw0, gcb0 = _conv_bwd(dxs, proj, conv_w, conv_b, 0, "conv_bwd_x")
    db_p, gcw1, gcb1 = _conv_bwd(db, proj, conv_w, conv_b, D_SSD, "conv_bwd_b")
    dc_p, gcw2, gcb2 = _conv_bwd(dc, proj, conv_w, conv_b, D_SSD + GROUPS * D_STATE, "conv_bwd_c")
    dproj = jnp.concatenate([dz, dxs_p, db_p, dc_p, dq, dg, dk, _bf(dv), draw,
                             jnp.zeros((t, DT_SLAB - LANES), BF16)], axis=1)
    ga_in = _reduce_pair(_re_rows_to_shards(_tn_matmul(dproj, hn, "gw_in")), place, "gw_in")
    dh, g_norm_pre, slabs_in = _dinproj(dproj, w_re, hpad, norm_pre_w, dy_t, ga_in)
    g_w_in = _reduce_finish(ga_in, slabs_in, place, "gw_in")

    gdsk_l = jnp.concatenate([gdsk[g, 0:1, 0:GROUPS] for g in range(GROUPS)], axis=1)
    gdsk_l = jnp.pad(gdsk_l, ((0, 0), (0, LANES - GROUPS * GROUPS)))
    grads = dict(
        meta_tokens=dh[PAD_LEAD:ROW0], norm_pre_w=g_norm_pre, w_in=g_w_in,
        conv_w=jnp.concatenate([gcw0, gcw1, gcw2], axis=1), conv_b=jnp.concatenate([gcb0, gcb1, gcb2], axis=1),
        dt_bias=_gather_heads(gb), a_log=_gather_heads(ga), d_skip=_gather_heads(gdsk_l), ssd_norm_w=g_ssd_norm,
        attn_sinks=gs[0:1, :Q_HEADS], w_out=g_w_out, norm_post_w=g_norm_post)
    return loss_blk[0, 0], dh[ROW0:], grads


ANY = pl.BlockSpec(memory_space=pl.ANY)
MESH = pl.DeviceIdType.MESH
GATHER_CHUNKS = 4
PAIR_CHUNKS = 16
JOIN_CHUNKS = 8


def _rcopy(src, dst, ssem, rsem, dev):
    return pltpu.make_async_remote_copy(src_ref=src, dst_ref=dst, send_sem=ssem, recv_sem=rsem, device_id=dev,
                                        device_id_type=MESH)


def _place():
    x, y, c = lax.axis_index("x"), lax.axis_index("y"), lax.axis_index("c")
    chips = [(1 - x, y), (x, 1 - y), (1 - x, 1 - y)]
    return x, y, c, chips


def _gather_plan(x_ref, out_ref, send_sems, recv_sems, local_sems, hr, kc):
    ch = hr // kc
    assert ch * kc == hr and ch % 16 == 0
    x, y, c, chips = _place()
    me = 2 * x + y
    sibling = (x, y, 1 - c)

    def piece(chip, hc, k):
        return out_ref.at[chip, pl.ds(hc * hr + k * ch, ch), :]

    def local():
        return [pltpu.make_async_copy(x_ref.at[pl.ds(k * ch, ch), :], out_ref.at[me, pl.ds(k * ch, ch), :],
                                      local_sems.at[k]) for k in range(2 * kc)]

    def first():
        return [_rcopy(x_ref.at[pl.ds(c * hr + k * ch, ch), :], piece(me, c, k), send_sems.at[j * kc + k],
                       recv_sems.at[j * kc + k], (*chip, c)) for j, chip in enumerate(chips) for k in range(kc)]

    def passed(hc):
        return [_rcopy(piece(2 * chip[0] + chip[1], hc, k), piece(2 * chip[0] + chip[1], hc, k),
                       send_sems.at[(3 + j) * kc + k], recv_sems.at[(3 + j) * kc + k], sibling)
                for j, chip in enumerate(chips) for k in range(kc)]

    def arrivals():
        return [_rcopy(piece(2 * chip[0] + chip[1], c, k), piece(2 * chip[0] + chip[1], c, k), send_sems.at[j * kc + k],
                       recv_sems.at[j * kc + k], (*chip, c)) for j, chip in enumerate(chips) for k in range(kc)]

    def start():
        for cp in local() + first():
            cp.start()

    def forward():
        for arrived, fw in zip(arrivals(), passed(c)):
            arrived.wait_recv()
            fw.start()

    def finish():
        for cp in passed(1 - c):
            cp.wait_recv()
        for cp in first() + passed(c):
            cp.wait_send()
        for cp in local():
            cp.wait()

    return start, forward, finish


def _gather_shards(shard, name, kc):
    r, n = shard.shape

    def body(x_ref, out_ref, send_sems, recv_sems, local_sems):
        for phase in _gather_plan(x_ref, out_ref, send_sems, recv_sems, local_sems, r // 2, kc):
            phase()

    return _pallas(
        body, name=name, in_specs=[ANY], out_specs=ANY,
        out_shape=jax.ShapeDtypeStruct((N_SHARD, r, n), shard.dtype),
        scratch_shapes=[pltpu.SemaphoreType.DMA((6 * kc,)), pltpu.SemaphoreType.DMA((6 * kc,)),
                        pltpu.SemaphoreType.DMA((2 * kc,))])(shard)


def _pair_send(g4, name):
    _, r, n = g4.shape
    hn = n // 2
    kc = PAIR_CHUNKS // N_SHARD
    cw = hn // kc
    assert cw * kc == hn and cw % LANES == 0

    def body(g_ref, got_ref, send_sems, recv_sems):
        x, y, c, _ = _place()
        cps = [_rcopy(g_ref.at[j, :, pl.ds((1 - c) * hn + k * cw, cw)], got_ref.at[j, :, pl.ds(k * cw, cw)],
                      send_sems.at[j * kc + k], recv_sems.at[j * kc + k], (x, y, 1 - c))
               for j in range(N_SHARD) for k in range(kc)]
        for cp in cps:
            cp.start()
        for cp in cps:
            cp.wait()

    return _pallas(
        body, name=name, in_specs=[ANY], out_specs=ANY, out_shape=jax.ShapeDtypeStruct((N_SHARD, r, hn), F32),
        scratch_shapes=[pltpu.SemaphoreType.DMA((N_SHARD * kc,)), pltpu.SemaphoreType.DMA((N_SHARD * kc,))])(g4)


REDUCE_TILE = 256


def _pair_add(g4, got, core, name):
    k, r, n = g4.shape
    hn = n // 2
    tc = REDUCE_TILE
    nt = hn // tc

    def body(core_ref, a_ref, b_ref, o_ref):
        o_ref[...] = _bf(a_ref[...] + b_ref[...])

    spec = pl.BlockSpec((1, r, tc), lambda j, i, core_ref: (j, 0, i))
    return _pallas(
        body, name=name,
        grid_spec=pltpu.PrefetchScalarGridSpec(
            num_scalar_prefetch=1, grid=(k, nt),
            in_specs=[pl.BlockSpec((1, r, tc), lambda j, i, core_ref: (j, 0, core_ref[0] * nt + i)), spec],
            out_specs=spec),
        out_shape=jax.ShapeDtypeStruct((k, r, hn), BF16), compiler_params=_cp("parallel", "parallel"))(core, g4, got)


def _exchange_copies(g_ref, got_ref, send_sems, recv_sems):
    hn = g_ref.shape[2]
    kc = GATHER_CHUNKS
    cw = hn // kc
    assert cw * kc == hn and cw % LANES == 0
    x, y, c, chips = _place()
    return [_rcopy(g_ref.at[2 * chip[0] + chip[1], :, pl.ds(k * cw, cw)], got_ref.at[j, :, pl.ds(k * cw, cw)],
                   send_sems.at[j * kc + k], recv_sems.at[j * kc + k], (*chip, c))
            for j, chip in enumerate(chips) for k in range(kc)]


def _exchange_scratch():
    return [pltpu.SemaphoreType.DMA((3 * GATHER_CHUNKS,)), pltpu.SemaphoreType.DMA((3 * GATHER_CHUNKS,))]


def _exchange_shape(ga):
    return jax.ShapeDtypeStruct((3,) + ga.shape[1:], ga.dtype)


def _chip_sum(ga, got, place, name):
    _, r, hn = ga.shape
    tc = REDUCE_TILE
    nt = hn // tc

    def body(place_ref, own_ref, got_ref, o_ref):
        acc = own_ref[0].astype(F32)
        for j in range(3):
            acc = acc + got_ref[j].astype(F32)
        o_ref[...] = acc

    return _pallas(
        body, name=name,
        grid_spec=pltpu.PrefetchScalarGridSpec(
            num_scalar_prefetch=1, grid=(nt,),
            in_specs=[pl.BlockSpec((1, r, tc), lambda i, place_ref: (place_ref[0], 0, i)),
                      pl.BlockSpec((3, r, tc), lambda i, place_ref: (0, 0, i))],
            out_specs=pl.BlockSpec((r, tc), lambda i, place_ref: (0, place_ref[1] * nt + i))),
        out_shape=jax.ShapeDtypeStruct((r, 2 * hn), F32), compiler_params=_cp("parallel"))(place, ga, got)


def _pair_join(buf, name):
    r, n = buf.shape
    hn = n // 2
    kc = JOIN_CHUNKS
    cw = hn // kc
    assert cw * kc == hn and cw % LANES == 0

    def body(in_ref, out_ref, send_sems, recv_sems):
        x, y, c, _ = _place()
        cps = [_rcopy(out_ref.at[:, pl.ds(c * hn + k * cw, cw)], out_ref.at[:, pl.ds(c * hn + k * cw, cw)],
                      send_sems.at[k], recv_sems.at[k], (x, y, 1 - c)) for k in range(kc)]
        for cp in cps:
            cp.start()
        for k in range(kc):
            cols = out_ref.at[:, pl.ds((1 - c) * hn + k * cw, cw)]
            _rcopy(cols, cols, send_sems.at[k], recv_sems.at[k], (x, y, 1 - c)).wait_recv()
        for cp in cps:
            cp.wait_send()

    return _pallas(
        body, name=name, in_specs=[ANY], out_specs=ANY, out_shape=jax.ShapeDtypeStruct((r, n), F32),
        input_output_aliases={0: 0},
        scratch_shapes=[pltpu.SemaphoreType.DMA((kc,)), pltpu.SemaphoreType.DMA((kc,))])(buf)


def _reduce_pair(g4, place, tag):
    got = _pair_send(g4, tag + "_pair_send")
    return _pair_add(g4, got, place[1:2], tag + "_pair_add")


def _reduce_finish(ga, slabs, place, tag):
    return _pair_join(_chip_sum(ga, slabs, place, tag + "_chip_sum"), tag + "_pair_join")


def _allreduce_small(p, name):
    rows, n = p.shape
    ndev = 8

    def body(p_ref, out_ref, slots, send_sems, recv_sems):
        x, y, c, _ = _place()
        my = 4 * x + 2 * y + c
        slots[my] = p_ref[...]
        cps = []
        for k in range(1, ndev):
            kx, ky, kc = (k >> 2) & 1, (k >> 1) & 1, k & 1
            peer = (x ^ kx, y ^ ky, c ^ kc)
            cp = _rcopy(p_ref, slots.at[my], send_sems.at[k - 1], recv_sems.at[k - 1], peer)
            cp.start()
            cps.append(cp)
        for k in range(1, ndev):
            _rcopy(p_ref, slots.at[my ^ k], send_sems.at[k - 1], recv_sems.at[k - 1], (x, y, c)).wait_recv()
        for cp in cps:
            cp.wait_send()
        acc = slots[0]
        for j in range(1, ndev):
            acc = acc + slots[j]
        out_ref[...] = acc

    vm = pl.BlockSpec(memory_space=pltpu.VMEM)
    return _pallas(
        body, name=name, in_specs=[vm], out_specs=vm, out_shape=jax.ShapeDtypeStruct((rows, n), F32),
        scratch_shapes=[pltpu.VMEM((ndev, rows, n), F32), pltpu.SemaphoreType.DMA((ndev - 1,)),
                        pltpu.SemaphoreType.DMA((ndev - 1,))])(p)


def _adamw(w, g, m, v, name):
    r, n = w.shape
    tr = _tile(r, 256, 8)
    c1 = 1.0 / (1.0 - ADAM_B1 ** ADAM_STEP)
    c2 = 1.0 / (1.0 - ADAM_B2 ** ADAM_STEP)

    def body(w_ref, g_ref, m_ref, v_ref, d_ref, mo_ref, vo_ref):
        gv = g_ref[...]
        mn = ADAM_B1 * m_ref[...] + (1.0 - ADAM_B1) * gv
        vn = ADAM_B2 * v_ref[...] + (1.0 - ADAM_B2) * (gv * gv)
        d_ref[...] = -ADAM_LR * ((mn * c1) / (jnp.sqrt(vn * c2) + ADAM_EPS) + ADAM_WD * w_ref[...])
        mo_ref[...] = mn
        vo_ref[...] = vn

    spec = pl.BlockSpec((tr, n), lambda i: (i, 0))
    shp = jax.ShapeDtypeStruct((r, n), F32)
    return _pallas(body, name=name, grid=(r // tr,), in_specs=[spec] * 4, out_specs=[spec] * 3, out_shape=[shp] * 3,
                   compiler_params=_cp("parallel"))(w, g, m, v)


PACK_W = 1024
SMALL_REPL = ("norm_pre_w", "conv_b", "ssd_norm_w", "norm_post_w")
SMALL_HEAD = ("dt_bias", "a_log", "d_skip", "attn_sinks")


def _rows(a):
    return a.reshape(-1, PACK_W)


def _head_row(vals, extra=None):
    parts = [vals[n].reshape(1, -1) for n in SMALL_HEAD]
    if extra is not None:
        parts.append(extra.reshape(1, 1))
    row = jnp.concatenate(parts, axis=1)
    return jnp.pad(row, ((0, 0), (0, PACK_W - row.shape[1])))


def _pad_rows(a, rows):
    return jnp.pad(a, ((0, rows - a.shape[0]), (0, 0)))


def _pack_repl(vals, extra=None):
    body = jnp.concatenate([_rows(vals[n]) for n in SMALL_REPL] + [_head_row(vals, extra)], axis=0)
    return _pad_rows(body, 16)


def _unpack_repl(buf):
    out, r = {}, 0
    for n, k in zip(SMALL_REPL, (2, 4, 2, 2)):
        out[n] = buf[r:r + k].reshape(1, k * PACK_W)
        r += k
    col = 0
    for n, k in zip(SMALL_HEAD, (32, 32, 32, 16)):
        out[n] = buf[r:r + 1, col:col + k]
        col += k
    return out, buf[r, col]


def kernel(x, meta_tokens, norm_pre_w, w_in, conv_w, conv_b, dt_bias, a_log, d_skip, ssd_norm_w, attn_sinks, w_out, norm_post_w, loss_target, m_meta_tokens, m_norm_pre_w, m_w_in, m_conv_w, m_conv_b, m_dt_bias, m_a_log, m_d_skip, m_ssd_norm_w, m_attn_sinks, m_w_out, m_norm_post_w, v_meta_tokens, v_norm_pre_w, v_w_in, v_conv_w, v_conv_b, v_dt_bias, v_a_log, v_d_skip, v_ssd_norm_w, v_attn_sinks, v_w_out, v_norm_post_w):
    names = ("meta_tokens", "norm_pre_w", "w_in", "conv_w", "conv_b", "dt_bias", "a_log", "d_skip", "ssd_norm_w",
             "attn_sinks", "w_out", "norm_post_w")
    w = dict(zip(names, (meta_tokens, norm_pre_w, w_in, conv_w, conv_b, dt_bias, a_log, d_skip, ssd_norm_w, attn_sinks,
                         w_out, norm_post_w)))
    m = dict(zip(names, (m_meta_tokens, m_norm_pre_w, m_w_in, m_conv_w, m_conv_b, m_dt_bias, m_a_log, m_d_skip,
                         m_ssd_norm_w, m_attn_sinks, m_w_out, m_norm_post_w)))
    v = dict(zip(names, (v_meta_tokens, v_norm_pre_w, v_w_in, v_conv_w, v_conv_b, v_dt_bias, v_a_log, v_d_skip,
                         v_ssd_norm_w, v_attn_sinks, v_w_out, v_norm_post_w)))
    cx, cy, cc = lax.axis_index("x"), lax.axis_index("y"), lax.axis_index("c")
    chip = 2 * cx + cy
    meta_cols = D_MODEL // N_SHARD
    conv_cols = D_CONV // N_SHARD

    place = jnp.stack([chip, cc]).astype(jnp.int32)
    w_re = _shards_to_re(_gather_shards(_bf(w_in[0]), "gather_w_in", 2 * GATHER_CHUNKS))
    conv_z = lax.dynamic_update_slice(jnp.zeros((CONV_WIDTH, D_CONV), F32), conv_w[0], (0, chip * conv_cols))
    meta_z = lax.dynamic_update_slice(jnp.zeros((N_META, D_MODEL), F32), meta_tokens, (0, chip * meta_cols))
    small = jnp.concatenate([_rows(conv_z), _rows(meta_z)], axis=0)
    small = _allreduce_small(jnp.where(cc == 0, small, 0.0), "gather_small")
    conv_full = small[0:16].reshape(CONV_WIDTH, D_CONV)
    meta_full = small[16:48].reshape(N_META, D_MODEL)

    loss_dev, grad_x, g = _local_step(x[0], loss_target[0], meta_full, norm_pre_w, w_re, conv_full, conv_b, dt_bias,
                                      a_log, d_skip, ssd_norm_w, attn_sinks, _bf(w_out[0]), norm_post_w, place)
    g_w_in, g_w_out = g["w_in"], g["w_out"]

    packed = jnp.concatenate([_rows(g["conv_w"]), _rows(g["meta_tokens"]), _pack_repl(g, loss_dev)], axis=0)
    red = _allreduce_small(packed, "reduce_small")
    g_conv_full = red[0:16].reshape(CONV_WIDTH, D_CONV)
    g_meta_full = red[16:48].reshape(N_META, D_MODEL)
    g_small, loss = _unpack_repl(red[48:64])
    grads = dict(g_small)
    grads["w_in"] = g_w_in
    grads["w_out"] = g_w_out
    grads["conv_w"] = lax.dynamic_slice(g_conv_full, (0, chip * conv_cols), (CONV_WIDTH, conv_cols))
    grads["meta_tokens"] = lax.dynamic_slice(g_meta_full, (0, chip * meta_cols), (N_META, meta_cols))

    upd = {}
    upd["w_in"] = [jnp.swapaxes(a, 0, 1) for a in _adamw(jnp.swapaxes(w_in[0], 0, 1), g_w_in, jnp.swapaxes(m_w_in[0], 0, 1),
                                                         jnp.swapaxes(v_w_in[0], 0, 1), "adamw_w_in")]
    grads["w_in"] = jnp.swapaxes(g_w_in, 0, 1)
    upd["w_out"] = _adamw(w_out[0], g_w_out, m_w_out[0], v_w_out[0], "adamw_w_out")

    def pack_small(vals, conv, meta):
        return jnp.concatenate([_pad_rows(conv.reshape(CONV_WIDTH, conv_cols), 8), _rows(meta), _pack_repl(vals)], axis=0)

    sm = _adamw(pack_small(w, w["conv_w"], w["meta_tokens"]), pack_small(grads, grads["conv_w"], grads["meta_tokens"]),
                pack_small(m, m["conv_w"], m["meta_tokens"]), pack_small(v, v["conv_w"], v["meta_tokens"]),
                "adamw_small")
    for n in names:
        if n not in ("w_in", "w_out"):
            upd[n] = [None, None, None]
    for k, buf in enumerate(sm):
        upd["conv_w"][k] = buf[0:CONV_WIDTH]
        upd["meta_tokens"][k] = buf[8:16].reshape(N_META, meta_cols)
        rest, _ = _unpack_repl(buf[16:32])
        for n in SMALL_REPL + SMALL_HEAD:
            upd[n][k] = rest[n]

    def shaped(n, a):
        return a.reshape(w[n].shape)

    outs = [loss, grad_x[None]]
    outs += [shaped(n, grads[n]) for n in names]
    for k in range(3):
        outs += [shaped(n, upd[n][k]) for n in names]
    return tuple(outs)
```

```python
import functools

import jax
import jax.numpy as jnp
from jax import lax
from jax.experimental import pallas as pl
from jax.experimental.pallas import tpu as pltpu

F32 = jnp.float32
BF16 = jnp.bfloat16

D_MODEL = 2048
CHUNK = 64
N_META = 16
PAD_LEAD = CHUNK - N_META
ROW0 = PAD_LEAD + N_META
EPS = 1e-6
SSD_HEADS = 32
HEAD_DIM = 64
GROUPS = 8
HPG = SSD_HEADS // GROUPS
D_STATE = 128
D_SSD = 2048
GROUP_W = D_SSD // GROUPS
CONV_WIDTH = 4
D_CONV = 4096
Q_HEADS = 16
KV_HEADS = 4
REP = Q_HEADS // KV_HEADS
D_ATT = 1024
D_KV = 256
BAND_CHUNKS = 3
ROPE_THETA = 10000.0
D_MIX = D_SSD + D_ATT
D_IN = 8736
N_SHARD = 4
W_IN_SHARD = D_IN // N_SHARD
W_OUT_SHARD = D_MIX // N_SHARD

OZ, OXS, OB, OC, OQ, OG, OK, OV, ODT = 0, 2048, 4096, 5120, 6144, 7168, 8192, 8448, 8704
DT_SLAB = 512
N_RE = ODT + DT_SLAB
LANES = 128

ADAM_LR, ADAM_B1, ADAM_B2, ADAM_EPS, ADAM_WD, ADAM_STEP = 0.001, 0.9, 0.999, 1e-08, 0.01, 10

SSD_GROUPS_PER_STEP = 4
SEG_TILE = 1024
VMEM_LIMIT = 52 * 1024 * 1024
NEG = -1e30
HI = lax.Precision.HIGHEST


def _pallas(body, **kw):
    return pl.pallas_call(body, **kw)


def _cp(*sem):
    return pltpu.CompilerParams(dimension_semantics=sem, vmem_limit_bytes=VMEM_LIMIT)


def _tile(n, cap, mult=16):
    best = None
    for d in range(mult, min(n, cap) + 1, mult):
        if n % d == 0:
            best = d
    assert best is not None, (n, cap)
    return best


def _nt(a, b):
    return lax.dot_general(a, b, (((1,), (1,)), ((), ())), preferred_element_type=F32)


def _tn(a, b):
    return lax.dot_general(a, b, (((0,), (0,)), ((), ())), preferred_element_type=F32)


def _mm(a, b):
    return jnp.dot(a, b, preferred_element_type=F32)


def _sigmoid(x):
    return 1.0 / (1.0 + jnp.exp(-x))


def _bf(x):
    return x.astype(BF16)


def _inproj(hpad, norm_w, w_re, w_out_shard):
    t, d = hpad.shape
    n = w_re.shape[1]
    tm, tn = _tile(t, 832), 1024
    ni, nj = t // tm, n // tn
    r_out, n_out = w_out_shard.shape
    kc = GATHER_CHUNKS

    def body(h_ref, nw_ref, w_ref, ws_ref, proj_ref, hn_ref, wall_ref, hn_s, send_sems, recv_sems, local_sems):
        i, j = pl.program_id(0), pl.program_id(1)
        start, forward, finish = _gather_plan(ws_ref, wall_ref, send_sems, recv_sems, local_sems, r_out // 2, kc)
        pl.when((i == 0) & (j == 0))(start)
        pl.when((i == ni // 2) & (j == 0))(forward)

        @pl.when(j == 0)
        def _():
            h = h_ref[...]
            ms = jnp.mean(h * h, axis=-1, keepdims=True)
            hn = _bf(h * lax.rsqrt(ms + EPS) * nw_ref[...])
            hn_s[...] = hn
            hn_ref[...] = hn
        proj_ref[...] = _mm(hn_s[...], w_ref[...])
        pl.when((i == ni - 1) & (j == nj - 1))(finish)

    return _pallas(
        body, name="inproj", grid=(ni, nj),
        in_specs=[pl.BlockSpec((tm, d), lambda i, j: (i, 0)), pl.BlockSpec((1, d), lambda i, j: (0, 0)),
                  pl.BlockSpec((d, tn), lambda i, j: (0, j)), ANY],
        out_specs=[pl.BlockSpec((tm, tn), lambda i, j: (i, j)), pl.BlockSpec((tm, d), lambda i, j: (i, 0)), ANY],
        out_shape=[jax.ShapeDtypeStruct((t, n), F32), jax.ShapeDtypeStruct((t, d), BF16),
                   jax.ShapeDtypeStruct((N_SHARD, r_out, n_out), w_out_shard.dtype)],
        scratch_shapes=[pltpu.VMEM((tm, d), BF16), pltpu.SemaphoreType.DMA((6 * kc,)), pltpu.SemaphoreType.DMA((6 * kc,)),
                        pltpu.SemaphoreType.DMA((2 * kc,))],
        compiler_params=_cp("arbitrary", "arbitrary"))(hpad, norm_w, w_re, w_out_shard)


def _conv_fwd(proj, conv_w, conv_b):
    t = proj.shape[0]
    tc = 256
    off = OXS // tc

    def body(x_ref, w_ref, b_ref, o_ref, xp):
        xp[0:8, :] = jnp.zeros((8, tc), F32)
        xp[8:t + 8, :] = x_ref[...]
        w = w_ref[...]
        u = (b_ref[...] + w[3:4, :] * xp[8:t + 8, :] + w[2:3, :] * xp[7:t + 7, :]
             + w[1:2, :] * xp[6:t + 6, :] + w[0:1, :] * xp[5:t + 5, :])
        o_ref[...] = u * _sigmoid(u)

    return _pallas(
        body, name="conv_fwd", grid=(D_CONV // tc,),
        in_specs=[pl.BlockSpec((t, tc), lambda j: (0, j + off)), pl.BlockSpec((CONV_WIDTH, tc), lambda j: (0, j)),
                  pl.BlockSpec((1, tc), lambda j: (0, j))],
        out_specs=pl.BlockSpec((t, tc), lambda j: (0, j)),
        out_shape=jax.ShapeDtypeStruct((t, D_CONV), F32),
        scratch_shapes=[pltpu.VMEM((t + 8, tc), F32)],
        compiler_params=_cp("parallel"))(proj, conv_w, conv_b)


def _softplus(u):
    e = jnp.exp(-jnp.abs(u))
    w = 1.0 + e
    l1p = jnp.where(w == 1.0, e, jnp.log(w) * (e / jnp.where(w == 1.0, 1.0, w - 1.0)))
    return jnp.maximum(u, 0.0) + l1p


def _chunks_per_step(nc):
    return max(d for d in range(1, 14) if nc % d == 0)


def _dt_prep(proj, dt_bias_l, a_log_l):
    t = proj.shape[0]
    nc = t // CHUNK
    q = CHUNK
    cps = _chunks_per_step(nc)
    rows = cps * q

    def body(raw_ref, bias_ref, alog_ref, dt_ref, acs_ref, acst_ref):
        ri = lax.broadcasted_iota(jnp.int32, (q, q), 0)
        ci = lax.broadcasted_iota(jnp.int32, (q, q), 1)
        tri = (ri >= ci).astype(F32)
        neg_a = -jnp.exp(alog_ref[...])
        for k in range(cps):
            rk = slice(q * k, q * (k + 1))
            sp = _softplus(raw_ref[rk, :] + bias_ref[...])
            row = pl.program_id(0) * rows + q * k + lax.broadcasted_iota(jnp.int32, (q, LANES), 0)
            dt = jnp.where(row >= PAD_LEAD, sp, 0.0)
            acs = jnp.dot(tri, dt * neg_a, preferred_element_type=F32, precision=HI)
            dt_ref[rk, :] = dt
            acs_ref[rk, :] = acs
            acst_ref[k] = acs.T

    return _pallas(
        body, name="dt_prep", grid=(nc // cps,),
        in_specs=[pl.BlockSpec((rows, LANES), lambda c: (c, ODT // LANES)), pl.BlockSpec((1, LANES), lambda c: (0, 0)),
                  pl.BlockSpec((1, LANES), lambda c: (0, 0))],
        out_specs=[pl.BlockSpec((rows, LANES), lambda c: (c, 0)), pl.BlockSpec((rows, LANES), lambda c: (c, 0)),
                   pl.BlockSpec((cps, LANES, q), lambda c: (c, 0, 0))],
        out_shape=[jax.ShapeDtypeStruct((t, LANES), F32), jax.ShapeDtypeStruct((t, LANES), F32),
                   jax.ShapeDtypeStruct((nc, LANES, q), F32)],
        compiler_params=_cp("parallel"))(proj, dt_bias_l, a_log_l)


def _head_cols(blk, idx):
    lane = lax.broadcasted_iota(jnp.int32, blk.shape, 1)
    return jnp.sum(jnp.where(lane == idx, blk, 0.0), axis=1, keepdims=True)


class _HeadVals:
    pass


def _lane_head(shape):
    return lax.broadcasted_iota(jnp.int32, shape, len(shape) - 1) >> 6


def _group_heads(g, gi, dtb, acsb, acst_ref, dskb):
    q = dtb.shape[0]
    hv = _HeadVals()
    lh = _lane_head((1, GROUP_W))
    hv.dt = jnp.zeros((q, GROUP_W), F32)
    hv.acs = jnp.zeros((q, GROUP_W), F32)
    hv.acs_last = jnp.zeros((1, GROUP_W), F32)
    hv.dsk = jnp.zeros((1, GROUP_W), F32)
    rows = []
    for r in range(HPG):
        idx = GROUPS * g + r
        sel = lh == r
        acs_r = acst_ref[0, GROUPS * gi + r:GROUPS * gi + r + 1, :]
        rows.append(acs_r)
        hv.dt = jnp.where(sel, _head_cols(dtb, idx), hv.dt)
        hv.acs = jnp.where(sel, _head_cols(acsb, idx), hv.acs)
        hv.acs_last = jnp.where(sel, acs_r[:, q - 1:q], hv.acs_last)
        hv.dsk = jnp.where(sel, _head_cols(dskb, idx), hv.dsk)
    hv.acs_row = jnp.concatenate(rows, axis=1)
    return hv


def _head_tri(q, lower):
    ri = lax.broadcasted_iota(jnp.int32, (q, GROUP_W), 0)
    li = lax.broadcasted_iota(jnp.int32, (q, GROUP_W), 1) & (HEAD_DIM - 1)
    return ri >= li if lower else ri <= li


def _block_diag(v):
    rb = lax.broadcasted_iota(jnp.int32, (GROUP_W, GROUP_W), 0) >> 6
    cb = lax.broadcasted_iota(jnp.int32, (GROUP_W, GROUP_W), 1) >> 6
    return jnp.where(rb == cb, jnp.concatenate([v] * HPG, axis=0), jnp.zeros((), v.dtype))


def _head_sums(v, r):
    return jnp.sum(jnp.where(_lane_head((1, GROUP_W)) == r, v, 0.0), axis=1, keepdims=True)


def _ssd_fwd(xbc, proj, dt, acs, acst, d_skip_l, ssd_norm_w):
    t = xbc.shape[0]
    q = CHUNK
    nc = t // q

    gps = SSD_GROUPS_PER_STEP
    gw, sw = gps * GROUP_W, gps * D_STATE

    def body(xs_ref, b_ref, c_ref, dt_ref, acs_ref, acst_ref, z_ref, dsk_ref, nw_ref,
             y_ref, ymix_ref, st_ref, state):
        @pl.when(pl.program_id(1) == 0)
        def _():
            state[...] = jnp.zeros_like(state)

        for gi in range(gps):
            g = gps * pl.program_id(0) + gi
            cols = slice(GROUP_W * gi, GROUP_W * (gi + 1))
            x = xs_ref[:, cols]
            bmb = _bf(b_ref[:, D_STATE * gi:D_STATE * (gi + 1)])
            cmb = _bf(c_ref[:, D_STATE * gi:D_STATE * (gi + 1)])
            hv = _group_heads(g, gi, dt_ref[...], acs_ref[...], acst_ref, dsk_ref[...])
            decay = jnp.exp(jnp.where(_head_tri(q, True), hv.acs - hv.acs_row, NEG))
            m_all = _bf(_nt(cmb, jnp.concatenate([bmb] * HPG, axis=0)) * decay)
            xdt = x * hv.dt
            s_prev = state[gi]
            st_ref[0, gi] = s_prev
            y = (_mm(m_all, _block_diag(_bf(xdt))) + _mm(cmb, _bf(s_prev)) * jnp.exp(hv.acs) + hv.dsk * x)
            state[gi] = jnp.exp(hv.acs_last) * s_prev + _tn(bmb, _bf(xdt * jnp.exp(hv.acs_last - hv.acs)))
            y_ref[:, cols] = y
            z = z_ref[:, cols]
            yg = y * (z * _sigmoid(z))
            ms = jnp.mean(yg * yg, axis=-1, keepdims=True)
            ymix_ref[:, cols] = _bf(yg * lax.rsqrt(ms + EPS) * nw_ref[:, cols])

    return _pallas(
        body, name="ssd_fwd", grid=(GROUPS // gps, nc),
        in_specs=[pl.BlockSpec((q, gw), lambda g, c: (c, g)),
                  pl.BlockSpec((q, sw), lambda g, c: (c, D_SSD // sw + g)),
                  pl.BlockSpec((q, sw), lambda g, c: (c, (D_SSD + GROUPS * D_STATE) // sw + g)),
                  pl.BlockSpec((q, LANES), lambda g, c: (c, 0)), pl.BlockSpec((q, LANES), lambda g, c: (c, 0)),
                  pl.BlockSpec((1, gps * GROUPS, q), lambda g, c: (c, g, 0)),
                  pl.BlockSpec((q, gw), lambda g, c: (c, g)),
                  pl.BlockSpec((1, LANES), lambda g, c: (0, 0)), pl.BlockSpec((1, gw), lambda g, c: (0, g))],
        out_specs=[pl.BlockSpec((q, gw), lambda g, c: (c, g)), pl.BlockSpec((q, gw), lambda g, c: (c, g)),
                   pl.BlockSpec((1, gps, D_STATE, GROUP_W), lambda g, c: (c, g, 0, 0))],
        out_shape=[jax.ShapeDtypeStruct((t, D_SSD), F32), jax.ShapeDtypeStruct((t, D_SSD), BF16),
                   jax.ShapeDtypeStruct((nc, GROUPS, D_STATE, GROUP_W), F32)],
        scratch_shapes=[pltpu.VMEM((gps, D_STATE, GROUP_W), F32)],
        compiler_params=_cp("parallel", "arbitrary"))(xbc, xbc, xbc, dt, acs, acst, proj, d_skip_l, ssd_norm_w)


def _swap_halves(v):
    lane = lax.broadcasted_iota(jnp.int32, v.shape, 1)
    return jnp.where((lane & (HEAD_DIM - 1)) < HEAD_DIM // 2, pltpu.roll(v, LANES - HEAD_DIM // 2, 1),
                     pltpu.roll(v, HEAD_DIM // 2, 1))


def _rope(qsrc, q_off, ksrc, k_off, cos_t, sin_t):
    t = qsrc.shape[0]
    tr = _tile(t, 832)

    def body(q_ref, k_ref, cos_ref, sin_ref, qo_ref, ko_ref):
        cs = cos_ref[...]
        sn = sin_ref[...]
        for src, dst, width in ((q_ref, qo_ref, D_ATT), (k_ref, ko_ref, D_KV)):
            for s in range(width // LANES):
                v = src[:, LANES * s:LANES * (s + 1)].astype(F32)
                dst[:, LANES * s:LANES * (s + 1)] = _bf(v * cs + _swap_halves(v) * sn)

    return _pallas(
        body, name="rope", grid=(t // tr,),
        in_specs=[pl.BlockSpec((tr, D_ATT), lambda i: (i, q_off // D_ATT)),
                  pl.BlockSpec((tr, D_KV), lambda i: (i, k_off // D_KV)),
                  pl.BlockSpec((tr, LANES), lambda i: (i, 0)), pl.BlockSpec((tr, LANES), lambda i: (i, 0))],
        out_specs=[pl.BlockSpec((tr, D_ATT), lambda i: (i, 0)), pl.BlockSpec((tr, D_KV), lambda i: (i, 0))],
        out_shape=[jax.ShapeDtypeStruct((t, D_ATT), BF16), jax.ShapeDtypeStruct((t, D_KV), BF16)],
        compiler_params=_cp("parallel"))(qsrc, ksrc, cos_t, sin_t)


def _band_specs(width, col_block):
    return [pl.BlockSpec((CHUNK, width), functools.partial(lambda c, j: (jnp.maximum(c - j, 0), col_block), j=j))
            for j in (2, 1, 0)]


def _attn_probs(qh, kb, sink_col, valid):
    s = _nt(qh, kb) * (HEAD_DIM ** -0.5)
    s = jnp.where(valid, s, NEG)
    m = jnp.maximum(jnp.max(s, axis=1, keepdims=True), sink_col)
    p = jnp.exp(s - m)
    psink = jnp.exp(sink_col - m)
    inv = 1.0 / (jnp.sum(p, axis=1, keepdims=True) + psink)
    return p * inv, psink * inv


def _attn_operands(c, q_ref, k_refs, v_refs, sink_ref, h):
    q = q_ref[...]
    qh = jnp.concatenate([q[:, HEAD_DIM * (REP * h + r):HEAD_DIM * (REP * h + r + 1)] for r in range(REP)], axis=0)
    kb = jnp.concatenate([k[:, HEAD_DIM * h:HEAD_DIM * (h + 1)] for k in k_refs], axis=0)
    vb = jnp.concatenate([_bf(v[:, HEAD_DIM * h:HEAD_DIM * (h + 1)]) for v in v_refs], axis=0)
    rows = lax.broadcasted_iota(jnp.int32, (REP * CHUNK, 1), 0) >> 6
    sink_col = jnp.zeros((REP * CHUNK, 1), F32)
    for r in range(REP):
        sink_col = jnp.where(rows == r, sink_ref[REP * h + r], sink_col)
    key_abs = (c - (BAND_CHUNKS - 1)) * CHUNK + lax.broadcasted_iota(jnp.int32, (1, BAND_CHUNKS * CHUNK), 1)
    return qh, kb, vb, sink_col, key_abs >= PAD_LEAD


def _attn_fwd(qr, kr, proj, sinks):
    t = qr.shape[0]
    nc = t // CHUNK

    def body(q_ref, k2, k1, k0, v2, v1, v0, g_ref, sink_ref, o_ref):
        c = pl.program_id(0)
        ks = [k2[...], k1[...], k0[...]]
        vs = [v2[...], v1[...], v0[...]]
        outs = []
        for h in range(KV_HEADS):
            qh, kb, vb, sink_col, valid = _attn_operands(c, q_ref, ks, vs, sink_ref, h)
            p, _ = _attn_probs(qh, kb, sink_col, valid)
            o = _mm(_bf(p), vb)
            outs += [o[CHUNK * r:CHUNK * (r + 1)] for r in range(REP)]
        att = jnp.concatenate(outs, axis=1)
        gate = g_ref[...]
        o_ref[...] = _bf(att * (gate * _sigmoid(gate)))

    return _pallas(
        body, name="attn_fwd", grid=(nc,),
        in_specs=[pl.BlockSpec((CHUNK, D_ATT), lambda c: (c, 0))] + _band_specs(D_KV, 0)
        + _band_specs(D_KV, OV // D_KV) + [pl.BlockSpec((CHUNK, D_ATT), lambda c: (c, OG // D_ATT)),
                                           pl.BlockSpec(memory_space=pltpu.SMEM)],
        out_specs=pl.BlockSpec((CHUNK, D_ATT), lambda c: (c, 0)),
        out_shape=jax.ShapeDtypeStruct((t, D_ATT), BF16),
        compiler_params=_cp("parallel"))(qr, kr, kr, kr, proj, proj, proj, proj, sinks)


def _outproj(ymix, amix, w_out):
    t = ymix.shape[0]
    tm, tn = _tile(t, 832), 1024

    def body(y_ref, a_ref, wy_ref, wa_ref, o_ref):
        o_ref[...] = _mm(y_ref[...], wy_ref[...]) + _mm(a_ref[...], wa_ref[...])

    return _pallas(
        body, name="outproj", grid=(t // tm, D_MODEL // tn),
        in_specs=[pl.BlockSpec((tm, D_SSD), lambda i, j: (i, 0)), pl.BlockSpec((tm, D_ATT), lambda i, j: (i, 0)),
                  pl.BlockSpec((D_SSD, tn), lambda i, j: (0, j)),
                  pl.BlockSpec((D_ATT, tn), lambda i, j: (D_SSD // D_ATT, j))],
        out_specs=pl.BlockSpec((tm, tn), lambda i, j: (i, j)),
        out_shape=jax.ShapeDtypeStruct((t, D_MODEL), F32),
        compiler_params=_cp("parallel", "parallel"))(ymix, amix, w_out, w_out)


def _post_loss(out, x, target, norm_post_w):
    t = out.shape[0]
    nc = t // CHUNK

    def body(o_ref, x_ref, tg_ref, nw_ref, dout_ref, dy_ref, loss_ref, gnw_ref):
        i = pl.program_id(0)

        @pl.when(i == 0)
        def _():
            dout_ref[...] = jnp.zeros_like(dout_ref)
            dy_ref[...] = jnp.zeros_like(dy_ref)
            loss_ref[...] = jnp.zeros_like(loss_ref)
            gnw_ref[...] = jnp.zeros_like(gnw_ref)

        @pl.when(i > 0)
        def _():
            o = o_ref[...]
            nw = nw_ref[...]
            rstd = lax.rsqrt(jnp.mean(o * o, axis=-1, keepdims=True) + EPS)
            n = o * rstd
            err = x_ref[...] + n * nw - tg_ref[...]
            loss_ref[...] += jnp.sum(err * err) * (0.5 / D_MODEL)
            dy = err * (1.0 / D_MODEL)
            dy_ref[...] = dy
            gnw_ref[...] += jnp.sum(dy * n, axis=0, keepdims=True)
            dn = dy * nw
            dout_ref[...] = _bf(rstd * (dn - n * jnp.mean(dn * n, axis=-1, keepdims=True)))

    prev = lambda i: (jnp.maximum(i - 1, 0), 0)
    return _pallas(
        body, name="post_loss", grid=(nc,),
        in_specs=[pl.BlockSpec((CHUNK, D_MODEL), lambda i: (i, 0)), pl.BlockSpec((CHUNK, D_MODEL), prev),
                  pl.BlockSpec((CHUNK, D_MODEL), prev), pl.BlockSpec((1, D_MODEL), lambda i: (0, 0))],
        out_specs=[pl.BlockSpec((CHUNK, D_MODEL), lambda i: (i, 0)), pl.BlockSpec((CHUNK, D_MODEL), lambda i: (i, 0)),
                   pl.BlockSpec((8, LANES), lambda i: (0, 0)), pl.BlockSpec((1, D_MODEL), lambda i: (0, 0))],
        out_shape=[jax.ShapeDtypeStruct((t, D_MODEL), BF16), jax.ShapeDtypeStruct((t, D_MODEL), F32),
                   jax.ShapeDtypeStruct((8, LANES), F32), jax.ShapeDtypeStruct((1, D_MODEL), F32)],
        compiler_params=_cp("arbitrary"))(out, x, target, norm_post_w)


def _nt_matmul(a, b, name):
    t, k = a.shape
    n = b.shape[0]
    tm, tn = _tile(t, 832), 1024

    def body(a_ref, b_ref, o_ref):
        o_ref[...] = _nt(a_ref[...], b_ref[...])

    return _pallas(
        body, name=name, grid=(t // tm, n // tn),
        in_specs=[pl.BlockSpec((tm, k), lambda i, j: (i, 0)), pl.BlockSpec((tn, k), lambda i, j: (j, 0))],
        out_specs=pl.BlockSpec((tm, tn), lambda i, j: (i, j)),
        out_shape=jax.ShapeDtypeStruct((t, n), F32),
        compiler_params=_cp("parallel", "parallel"))(a, b)


def _tn_matmul(a, b, name):
    t, m = a.shape
    n = b.shape[1]
    tk, tm, tn = _tile(t, 832), min(m, 1024), min(n, 2048)
    nk = t // tk

    def body(a_ref, b_ref, o_ref):
        @pl.when(pl.program_id(2) == 0)
        def _():
            o_ref[...] = jnp.zeros_like(o_ref)
        o_ref[...] += _tn(a_ref[...], b_ref[...])

    return _pallas(
        body, name=name, grid=(m // tm, n // tn, nk),
        in_specs=[pl.BlockSpec((tk, tm), lambda i, j, k: (k, i)), pl.BlockSpec((tk, tn), lambda i, j, k: (k, j))],
        out_specs=pl.BlockSpec((tm, tn), lambda i, j, k: (i, j)),
        out_shape=jax.ShapeDtypeStruct((m, n), F32),
        compiler_params=_cp("parallel", "parallel", "arbitrary"))(a, b)


def _attn_bwd(qr, kr, proj, dmix, sinks, ga):
    t = qr.shape[0]
    nc = t // CHUNK
    scale = HEAD_DIM ** -0.5

    def body(q_ref, k2, k1, k0, v2, v1, v0, g_ref, da_ref, sink_ref, ga_ref, dq_ref, dg_ref, dk_ref, dv_ref, gs_ref,
             got_ref, send_sems, recv_sems):
        c = pl.program_id(0)

        @pl.when(c == 0)
        def _():
            for cp in _exchange_copies(ga_ref, got_ref, send_sems, recv_sems):
                cp.start()
            dk_ref[...] = jnp.zeros_like(dk_ref)
            dv_ref[...] = jnp.zeros_like(dv_ref)
            gs_ref[...] = jnp.zeros_like(gs_ref)

        ks = [k2[...], k1[...], k0[...]]
        vs = [v2[...], v1[...], v0[...]]
        gate = g_ref[...]
        sg = _sigmoid(gate)
        da = da_ref[...]
        datt = da * (gate * sg)
        lane = lax.broadcasted_iota(jnp.int32, (1, LANES), 1)
        rows = lax.broadcasted_iota(jnp.int32, (REP * CHUNK, 1), 0) >> 6
        dqs, atts, dks, dvs = [], [], [], []
        gs = jnp.zeros((1, LANES), F32)
        for h in range(KV_HEADS):
            qh, kb, vb, sink_col, valid = _attn_operands(c, q_ref, ks, vs, sink_ref, h)
            p, psink = _attn_probs(qh, kb, sink_col, valid)
            pb = _bf(p)
            o = _mm(pb, vb)
            do = jnp.concatenate([datt[:, HEAD_DIM * (REP * h + r):HEAD_DIM * (REP * h + r + 1)] for r in range(REP)],
                                 axis=0)
            dob = _bf(do)
            delta = jnp.sum(do * o, axis=1, keepdims=True)
            ds = _bf(p * (_nt(dob, vb) - delta) * scale)
            gsink = -psink * delta
            for r in range(REP):
                gs = gs + jnp.where(lane == REP * h + r, jnp.sum(jnp.where(rows == r, gsink, 0.0)), 0.0)
            dqh = _mm(ds, kb)
            dqs += [dqh[CHUNK * r:CHUNK * (r + 1)] for r in range(REP)]
            atts += [o[CHUNK * r:CHUNK * (r + 1)] for r in range(REP)]
            dks.append(_tn(ds, qh))
            dvs.append(_tn(pb, dob))
        dq_ref[...] = jnp.concatenate(dqs, axis=1)
        att = jnp.concatenate(atts, axis=1)
        dg_ref[...] = _bf(da * att * (sg * (1.0 + gate * (1.0 - sg))))
        gs_ref[0:1, :] += gs
        dkf = jnp.concatenate(dks, axis=1)
        dvf = jnp.concatenate(dvs, axis=1)
        for j in range(BAND_CHUNKS):
            r0 = pl.multiple_of(jnp.maximum(c - (BAND_CHUNKS - 1) + j, 0) * CHUNK, CHUNK)
            dk_ref[pl.ds(r0, CHUNK), :] += dkf[CHUNK * j:CHUNK * (j + 1)]
            dv_ref[pl.ds(r0, CHUNK), :] += dvf[CHUNK * j:CHUNK * (j + 1)]

        @pl.when(c == nc - 1)
        def _():
            for cp in _exchange_copies(ga_ref, got_ref, send_sems, recv_sems):
                cp.wait()

    return _pallas(
        body, name="attn_bwd", grid=(nc,),
        in_specs=[pl.BlockSpec((CHUNK, D_ATT), lambda c: (c, 0))] + _band_specs(D_KV, 0)
        + _band_specs(D_KV, OV // D_KV) + [pl.BlockSpec((CHUNK, D_ATT), lambda c: (c, OG // D_ATT)),
                                           pl.BlockSpec((CHUNK, D_ATT), lambda c: (c, D_SSD // D_ATT)),
                                           pl.BlockSpec(memory_space=pltpu.SMEM), ANY],
        out_specs=[pl.BlockSpec((CHUNK, D_ATT), lambda c: (c, 0)), pl.BlockSpec((CHUNK, D_ATT), lambda c: (c, 0)),
                   pl.BlockSpec((t, D_KV), lambda c: (0, 0)), pl.BlockSpec((t, D_KV), lambda c: (0, 0)),
                   pl.BlockSpec((8, LANES), lambda c: (0, 0)), ANY],
        out_shape=[jax.ShapeDtypeStruct((t, D_ATT), F32), jax.ShapeDtypeStruct((t, D_ATT), BF16),
                   jax.ShapeDtypeStruct((t, D_KV), F32), jax.ShapeDtypeStruct((t, D_KV), F32),
                   jax.ShapeDtypeStruct((8, LANES), F32), _exchange_shape(ga)],
        scratch_shapes=_exchange_scratch(),
        compiler_params=_cp("arbitrary"))(qr, kr, kr, kr, proj, proj, proj, proj, dmix, sinks, ga)


def _ssd_bwd(dmix, y_ssd, xbc, proj, dt, acs, acst, states, d_skip_l, ssd_norm_w):
    t = xbc.shape[0]
    q = CHUNK
    nc = t // q
    gps = SSD_GROUPS_PER_STEP
    gw, sw = gps * GROUP_W, gps * D_STATE

    def body(dmix_ref, y_ref, z_ref, nw_ref, xs_ref, b_ref, c_ref, dt_ref, acs_ref, acst_ref, st_ref, dsk_ref,
             dz_ref, dxs_ref, db_ref, dc_ref, dacs_ref, ddt_ref, gnw_ref, gdsk_ref, dstate):
        @pl.when(pl.program_id(1) == 0)
        def _():
            dstate[...] = jnp.zeros_like(dstate)
            gnw_ref[...] = jnp.zeros_like(gnw_ref)
            gdsk_ref[...] = jnp.zeros_like(gdsk_ref)

        last_row = lax.broadcasted_iota(jnp.int32, (q, 1), 0) == q - 1
        lane = lax.broadcasted_iota(jnp.int32, (q, LANES), 1)
        lane1 = lax.broadcasted_iota(jnp.int32, (8, LANES), 1)
        for gi in range(gps):
            g = gps * pl.program_id(0) + gi
            cols = slice(GROUP_W * gi, GROUP_W * (gi + 1))
            scols = slice(D_STATE * gi, D_STATE * (gi + 1))
            y = y_ref[:, cols]
            z = z_ref[:, cols]
            sz = _sigmoid(z)
            silu_z = z * sz
            yg = y * silu_z
            rstd = lax.rsqrt(jnp.mean(yg * yg, axis=-1, keepdims=True) + EPS)
            n = yg * rstd
            dout = dmix_ref[:, cols]
            gnw_ref[:, cols] += jnp.sum(dout * n, axis=0, keepdims=True)
            dn = dout * nw_ref[:, cols]
            dyg = rstd * (dn - n * jnp.mean(dn * n, axis=-1, keepdims=True))
            dy = dyg * silu_z
            dz_ref[:, cols] = _bf(dyg * y * (sz * (1.0 + z * (1.0 - sz))))

            x = xs_ref[:, cols]
            bmb, cmb = _bf(b_ref[:, scols]), _bf(c_ref[:, scols])
            hv = _group_heads(g, gi, dt_ref[...], acs_ref[...], acst_ref, dsk_ref[...])
            dec = jnp.exp(jnp.where(_head_tri(q, True), hv.acs - hv.acs_row, NEG))
            dect = jnp.exp(jnp.where(_head_tri(q, False), hv.acs_row - hv.acs, NEG))
            b4 = jnp.concatenate([bmb] * HPG, axis=0)
            c4 = jnp.concatenate([cmb] * HPG, axis=0)
            m_all = _nt(cmb, b4) * dec
            mt_all = _nt(bmb, c4) * dect
            xdt = x * hv.dt
            xdt_b, dyb = _bf(xdt), _bf(dy)
            x_bd, dy_bd = _block_diag(xdt_b), _block_diag(dyb)
            s_prev = st_ref[0, gi]
            spb = _bf(s_prev)
            ds_new = dstate[gi]
            dsb = _bf(ds_new)
            e = jnp.exp(hv.acs)
            elast = jnp.exp(hv.acs_last)
            dte = jnp.exp(hv.acs_last - hv.acs)
            bds = _mm(bmb, dsb)
            dxdt = _mm(_bf(mt_all), dy_bd) + bds * dte
            dm = _nt(dyb, x_bd)
            dmt = _nt(xdt_b, dy_bd)
            dye = _bf(dy * e)
            dc_ref[:, scols] = _mm(_bf(dm * dec), b4) + _nt(dye, spb)
            db_ref[:, scols] = _mm(_bf(dmt * dect), c4) + _nt(_bf(xdt * dte), dsb)
            dstate[gi] = elast * ds_new + _tn(cmb, dye)
            dxs_ref[:, cols] = dxdt * hv.dt + hv.dsk * dy
            ddte_dte = bds * xdt * dte
            dacs_l = dm * m_all - dmt * mt_all + dy * _mm(cmb, spb) * e - ddte_dte
            dlast_l = (jnp.sum(ddte_dte, axis=0, keepdims=True)
                       + jnp.sum(s_prev * ds_new, axis=0, keepdims=True) * elast)
            ddt_l = dxdt * x
            gdsk_l = jnp.sum(dy * x, axis=0, keepdims=True)
            dacs_out = jnp.zeros((q, LANES), F32)
            ddt_out = jnp.zeros((q, LANES), F32)
            gdsk = jnp.zeros((8, LANES), F32)
            for r in range(HPG):
                dacs = _head_sums(dacs_l, r) + jnp.where(last_row, _head_sums(dlast_l, r), 0.0)
                dacs_out = jnp.where(lane == r, dacs, dacs_out)
                ddt_out = jnp.where(lane == r, _head_sums(ddt_l, r), ddt_out)
                gdsk = gdsk + jnp.where(lane1 == r, _head_sums(gdsk_l, r), 0.0)
            dacs_ref[:, LANES * gi:LANES * (gi + 1)] = dacs_out
            ddt_ref[:, LANES * gi:LANES * (gi + 1)] = ddt_out
            gdsk_ref[gi] += gdsk

    rev = lambda c: nc - 1 - c
    wide = pl.BlockSpec((q, gw), lambda g, c: (rev(c), g))
    return _pallas(
        body, name="ssd_bwd", grid=(GROUPS // gps, nc),
        in_specs=[wide, wide, wide, pl.BlockSpec((1, gw), lambda g, c: (0, g)), wide,
                  pl.BlockSpec((q, sw), lambda g, c: (rev(c), D_SSD // sw + g)),
                  pl.BlockSpec((q, sw), lambda g, c: (rev(c), (D_SSD + GROUPS * D_STATE) // sw + g)),
                  pl.BlockSpec((q, LANES), lambda g, c: (rev(c), 0)), pl.BlockSpec((q, LANES), lambda g, c: (rev(c), 0)),
                  pl.BlockSpec((1, gps * GROUPS, q), lambda g, c: (rev(c), g, 0)),
                  pl.BlockSpec((1, gps, D_STATE, GROUP_W), lambda g, c: (rev(c), g, 0, 0)),
                  pl.BlockSpec((1, LANES), lambda g, c: (0, 0))],
        out_specs=[wide, wide,
                   pl.BlockSpec((q, sw), lambda g, c: (rev(c), g)), pl.BlockSpec((q, sw), lambda g, c: (rev(c), g)),
                   pl.BlockSpec((q, gps * LANES), lambda g, c: (rev(c), g)),
                   pl.BlockSpec((q, gps * LANES), lambda g, c: (rev(c), g)),
                   pl.BlockSpec((1, gw), lambda g, c: (0, g)), pl.BlockSpec((gps, 8, LANES), lambda g, c: (g, 0, 0))],
        out_shape=[jax.ShapeDtypeStruct((t, D_SSD), BF16), jax.ShapeDtypeStruct((t, D_SSD), F32),
                   jax.ShapeDtypeStruct((t, GROUPS * D_STATE), F32), jax.ShapeDtypeStruct((t, GROUPS * D_STATE), F32),
                   jax.ShapeDtypeStruct((t, GROUPS * LANES), F32), jax.ShapeDtypeStruct((t, GROUPS * LANES), F32),
                   jax.ShapeDtypeStruct((1, D_SSD), F32), jax.ShapeDtypeStruct((GROUPS, 8, LANES), F32)],
        scratch_shapes=[pltpu.VMEM((gps, D_STATE, GROUP_W), F32)],
        compiler_params=_cp("parallel", "arbitrary"))(dmix, y_ssd, proj, ssd_norm_w, xbc, xbc, xbc, dt, acs, acst,
                                                      states, d_skip_l)


def _dt_bwd(dacs_g, ddt_g, dt, proj, dt_bias_l, a_log_l):
    t = dt.shape[0]
    q = CHUNK
    nc = t // q
    cps = _chunks_per_step(nc)
    rows = cps * q

    def body(dacs_ref, ddt_ref, dt_ref, raw_ref, bias_ref, alog_ref, draw_ref, ga_ref, gb_ref):
        @pl.when(pl.program_id(0) == 0)
        def _():
            ga_ref[...] = jnp.zeros_like(ga_ref)
            gb_ref[...] = jnp.zeros_like(gb_ref)

        lane = lax.broadcasted_iota(jnp.int32, (q, LANES), 1)
        ri = lax.broadcasted_iota(jnp.int32, (q, q), 0)
        ci = lax.broadcasted_iota(jnp.int32, (q, q), 1)
        triu = (ri <= ci).astype(F32)
        a = -jnp.exp(alog_ref[...])
        used = (lane & (GROUPS - 1)) < HPG
        ga = jnp.zeros((1, LANES), F32)
        gb = jnp.zeros((1, LANES), F32)
        for k in range(cps):
            rk = slice(q * k, q * (k + 1))
            dacs = jnp.zeros((q, LANES), F32)
            ddt = jnp.zeros((q, LANES), F32)
            for g in range(GROUPS):
                mask = (lane >= GROUPS * g) & (lane < GROUPS * g + HPG)
                sl = slice(LANES * g, LANES * (g + 1))
                if g == 0:
                    dacs = jnp.where(mask, dacs_ref[rk, sl], dacs)
                    ddt = jnp.where(mask, ddt_ref[rk, sl], ddt)
                else:
                    dacs = jnp.where(mask, pltpu.roll(dacs_ref[rk, sl], GROUPS * g, 1), dacs)
                    ddt = jnp.where(mask, pltpu.roll(ddt_ref[rk, sl], GROUPS * g, 1), ddt)
            dda = jnp.dot(triu, dacs, preferred_element_type=F32, precision=HI)
            row = pl.program_id(0) * rows + q * k + lax.broadcasted_iota(jnp.int32, (q, LANES), 0)
            dsp = jnp.where((row >= PAD_LEAD) & used, dda * a + ddt, 0.0)
            draw = dsp * _sigmoid(raw_ref[rk, :] + bias_ref[...])
            draw_ref[rk, :] = _bf(draw)
            gb = gb + jnp.sum(draw, axis=0, keepdims=True)
            ga = ga + jnp.sum(jnp.where(used, dda * dt_ref[rk, :], 0.0), axis=0, keepdims=True)
        gb_ref[0:1, :] += gb
        ga_ref[0:1, :] += ga * a

    return _pallas(
        body, name="dt_bwd", grid=(nc // cps,),
        in_specs=[pl.BlockSpec((rows, GROUPS * LANES), lambda c: (c, 0)),
                  pl.BlockSpec((rows, GROUPS * LANES), lambda c: (c, 0)),
                  pl.BlockSpec((rows, LANES), lambda c: (c, 0)), pl.BlockSpec((rows, LANES), lambda c: (c, ODT // LANES)),
                  pl.BlockSpec((1, LANES), lambda c: (0, 0)), pl.BlockSpec((1, LANES), lambda c: (0, 0))],
        out_specs=[pl.BlockSpec((rows, LANES), lambda c: (c, 0)), pl.BlockSpec((8, LANES), lambda c: (0, 0)),
                   pl.BlockSpec((8, LANES), lambda c: (0, 0))],
        out_shape=[jax.ShapeDtypeStruct((t, LANES), BF16), jax.ShapeDtypeStruct((8, LANES), F32),
                   jax.ShapeDtypeStruct((8, LANES), F32)],
        compiler_params=_cp("arbitrary"))(dacs_g, ddt_g, dt, proj, dt_bias_l, a_log_l)


def _conv_bwd(dseg, proj, conv_w, conv_b, col_off, name):
    t, width = dseg.shape
    tc = 128
    off_p = (OXS + col_off) // tc
    off_w = col_off // tc

    def body(d_ref, x_ref, w_ref, b_ref, dx_ref, gw_ref, gb_ref, xp, dup):
        xp[0:8, :] = jnp.zeros((8, tc), F32)
        xp[8:t + 8, :] = x_ref[...]
        w = w_ref[...]
        u = (b_ref[...] + w[3:4, :] * xp[8:t + 8, :] + w[2:3, :] * xp[7:t + 7, :]
             + w[1:2, :] * xp[6:t + 6, :] + w[0:1, :] * xp[5:t + 5, :])
        su = _sigmoid(u)
        du = d_ref[...] * (su * (1.0 + u * (1.0 - su)))
        dup[0:t, :] = du
        dup[t:t + 8, :] = jnp.zeros((8, tc), F32)
        dx_ref[...] = _bf(w[3:4, :] * du + w[2:3, :] * dup[1:t + 1, :] + w[1:2, :] * dup[2:t + 2, :]
                          + w[0:1, :] * dup[3:t + 3, :])
        gb_ref[...] = jnp.sum(du, axis=0, keepdims=True)
        gw_ref[...] = jnp.concatenate(
            [jnp.sum(du * xp[5 + k:t + 5 + k, :], axis=0, keepdims=True) for k in range(CONV_WIDTH)], axis=0)

    return _pallas(
        body, name=name, grid=(width // tc,),
        in_specs=[pl.BlockSpec((t, tc), lambda j: (0, j)), pl.BlockSpec((t, tc), lambda j: (0, j + off_p)),
                  pl.BlockSpec((CONV_WIDTH, tc), lambda j: (0, j + off_w)), pl.BlockSpec((1, tc), lambda j: (0, j + off_w))],
        out_specs=[pl.BlockSpec((t, tc), lambda j: (0, j)), pl.BlockSpec((CONV_WIDTH, tc), lambda j: (0, j)),
                   pl.BlockSpec((1, tc), lambda j: (0, j))],
        out_shape=[jax.ShapeDtypeStruct((t, width), BF16), jax.ShapeDtypeStruct((CONV_WIDTH, width), F32),
                   jax.ShapeDtypeStruct((1, width), F32)],
        scratch_shapes=[pltpu.VMEM((t + 8, tc), F32), pltpu.VMEM((t + 8, tc), F32)],
        compiler_params=_cp("parallel"))(dseg, proj, conv_w, conv_b)


def _dinproj(segs, w_re, hpad, norm_w, dy_t, ga):
    t = segs[0].shape[0]
    d = hpad.shape[1]
    tm, tk = _tile(t, 416), SEG_TILE
    counts = [s.shape[1] // tk for s in segs]
    firsts = [sum(counts[:s]) for s in range(len(segs))]
    nk = sum(counts)
    assert nk * tk == w_re.shape[1]
    ni = t // tm
    ns = len(segs)

    def body(*refs):
        seg_refs = refs[:ns]
        w_ref, h_ref, nw_ref, dy_ref, ga_ref, dh_ref, gnw_ref, got_ref, acc, send_sems, recv_sems = refs[ns:]
        i, k = pl.program_id(0), pl.program_id(1)

        @pl.when((i == 0) & (k == 0))
        def _():
            for cp in _exchange_copies(ga_ref, got_ref, send_sems, recv_sems):
                cp.start()
            gnw_ref[...] = jnp.zeros_like(gnw_ref)

        @pl.when(k == 0)
        def _():
            acc[...] = jnp.zeros_like(acc)

        for s in range(ns):
            @pl.when((k >= firsts[s]) & (k < firsts[s] + counts[s]))
            def _(s=s):
                acc[...] += _nt(seg_refs[s][...], w_ref[...])

        @pl.when(k == nk - 1)
        def _():
            h = h_ref[...]
            rstd = lax.rsqrt(jnp.mean(h * h, axis=-1, keepdims=True) + EPS)
            nrm = h * rstd
            dhn = acc[...]
            gnw_ref[...] += jnp.sum(dhn * nrm, axis=0, keepdims=True)
            dn = dhn * nw_ref[...]
            dh_ref[...] = rstd * (dn - nrm * jnp.mean(dn * nrm, axis=-1, keepdims=True)) + dy_ref[...]

        @pl.when((i == ni - 1) & (k == nk - 1))
        def _():
            for cp in _exchange_copies(ga_ref, got_ref, send_sems, recv_sems):
                cp.wait()

    seg_specs = [pl.BlockSpec((tm, tk), functools.partial(lambda i, k, f0, n0: (i, jnp.clip(k - f0, 0, n0 - 1)),
                                                          f0=firsts[s], n0=counts[s])) for s in range(ns)]
    return _pallas(
        body, name="dinproj", grid=(ni, nk),
        in_specs=seg_specs + [pl.BlockSpec((d, tk), lambda i, k: (0, k)),
                              pl.BlockSpec((tm, d), lambda i, k: (i, 0)), pl.BlockSpec((1, d), lambda i, k: (0, 0)),
                              pl.BlockSpec((tm, d), lambda i, k: (i, 0)), ANY],
        out_specs=[pl.BlockSpec((tm, d), lambda i, k: (i, 0)), pl.BlockSpec((1, d), lambda i, k: (0, 0)), ANY],
        out_shape=[jax.ShapeDtypeStruct((t, d), F32), jax.ShapeDtypeStruct((1, d), F32), _exchange_shape(ga)],
        scratch_shapes=[pltpu.VMEM((tm, d), F32)] + _exchange_scratch(),
        compiler_params=_cp("arbitrary", "arbitrary"))(*segs, w_re, hpad, norm_w, dy_t, ga)


def _spread_heads(v):
    v = jnp.pad(v.reshape(GROUPS, HPG), ((0, 0), (0, GROUPS - HPG))).reshape(1, GROUPS * GROUPS)
    return jnp.pad(v, ((0, 0), (0, LANES - GROUPS * GROUPS)))


def _gather_heads(v):
    return v[0:1, :GROUPS * GROUPS].reshape(GROUPS, GROUPS)[:, :HPG].reshape(1, SSD_HEADS)


def _rope_tables(t):
    half = HEAD_DIM // 2
    inv = ROPE_THETA ** (-jnp.arange(half, dtype=F32) / half)
    pos = (jnp.arange(t) - PAD_LEAD).astype(F32)
    ang = pos[:, None] * inv[None, :]
    cos, sin = jnp.cos(ang), jnp.sin(ang)
    cos_t = jnp.concatenate([cos, cos, cos, cos], axis=1)
    sin_t = jnp.concatenate([-sin, sin, -sin, sin], axis=1)
    return cos_t, sin_t


def _column_pieces():
    runs = [(0, OB + 2 * GROUPS * D_STATE, 0)]
    o = OB + 2 * GROUPS * D_STATE
    runs += [(o + HPG * g, HPG, ODT + GROUPS * g) for g in range(GROUPS)]
    o += SSD_HEADS
    for width, dst in ((D_ATT, OQ), (D_KV, OK), (D_KV, OV), (D_ATT, OG)):
        runs.append((o, width, dst))
        o += width
    assert o == D_IN
    pieces = []
    for o0, width, dst in runs:
        for j in range(N_SHARD):
            lo, hi = max(o0, W_IN_SHARD * j), min(o0 + width, W_IN_SHARD * (j + 1))
            if lo < hi:
                pieces.append((j, lo - W_IN_SHARD * j, hi - W_IN_SHARD * j, dst + lo - o0))
    return pieces


def _shards_to_re(w_all):
    _, k, _ = w_all.shape
    tr = 256

    def body(x_ref, o_ref):
        o_ref[:, ODT:ODT + DT_SLAB] = jnp.zeros((tr, DT_SLAB), o_ref.dtype)
        for j, c0, c1, d0 in _column_pieces():
            o_ref[:, d0:d0 + c1 - c0] = x_ref[j, :, c0:c1]

    return _pallas(body, name="shards_to_re", grid=(k // tr,),
                   in_specs=[pl.BlockSpec((N_SHARD, tr, W_IN_SHARD), lambda i: (0, i, 0))],
                   out_specs=pl.BlockSpec((tr, N_RE), lambda i: (i, 0)),
                   out_shape=jax.ShapeDtypeStruct((k, N_RE), w_all.dtype), compiler_params=_cp("parallel"))(w_all)


def _re_rows_to_shards(parts):
    k = parts[0].shape[1]
    tc = 128
    starts = [sum(p.shape[0] for p in parts[:s]) for s in range(len(parts))]
    assert starts[-1] + parts[-1].shape[0] == N_RE
    moves = []
    for j, c0, c1, d0 in _column_pieces():
        for s, p in enumerate(parts):
            lo, hi = max(d0, starts[s]), min(d0 + c1 - c0, starts[s] + p.shape[0])
            if lo < hi:
                moves.append((s, lo - starts[s], j, c0 + lo - d0, hi - lo))

    def body(*refs):
        o_ref = refs[-1]
        for s, r0, j, c0, n in moves:
            o_ref[j, c0:c0 + n, :] = refs[s][r0:r0 + n, :]

    return _pallas(body, name="re_rows_to_shards", grid=(k // tc,),
                   in_specs=[pl.BlockSpec((p.shape[0], tc), lambda i: (0, i)) for p in parts],
                   out_specs=pl.BlockSpec((N_SHARD, W_IN_SHARD, tc), lambda i: (0, 0, i)),
                   out_shape=jax.ShapeDtypeStruct((N_SHARD, W_IN_SHARD, k), parts[0].dtype),
                   compiler_params=_cp("parallel"))(*parts)


def _local_step(x, target, meta, norm_pre_w, w_re, conv_w, conv_b, dt_bias, a_log, d_skip, ssd_norm_w, sinks,
                w_out_shard, norm_post_w, place):
    seq = x.shape[0]
    t = PAD_LEAD + N_META + seq
    hpad = jnp.concatenate([jnp.zeros((PAD_LEAD, D_MODEL), F32), meta, x], axis=0)
    dt_bias_l, a_log_l, d_skip_l = _spread_heads(dt_bias), _spread_heads(a_log), _spread_heads(d_skip)
    cos_t, sin_t = _rope_tables(t)
    sink_v = sinks.reshape(Q_HEADS)

    proj, hn, w_out_all = _inproj(hpad, norm_pre_w, w_re, w_out_shard)
    w_out = w_out_all.reshape(D_MIX, D_MODEL)
    xbc = _conv_fwd(proj, conv_w, conv_b)
    dt, acs, acst = _dt_prep(proj, dt_bias_l, a_log_l)
    y_ssd, ymix, states = _ssd_fwd(xbc, proj, dt, acs, acst, d_skip_l, ssd_norm_w)
    qr, kr = _rope(proj, OQ, proj, OK, cos_t, sin_t)
    amix = _attn_fwd(qr, kr, proj, sink_v)
    out = _outproj(ymix, amix, w_out)
    dout, dy_t, loss_blk, g_norm_post = _post_loss(out, x, target, norm_post_w)

    dmix = _nt_matmul(dout, w_out, "dmix")
    g_w_out = jnp.concatenate([_tn_matmul(ymix, dout, "gw_out_y"), _tn_matmul(amix, dout, "gw_out_a")], axis=0)
    ga_out = _reduce_pair(g_w_out.reshape(N_SHARD, W_OUT_SHARD, D_MODEL), place, "gw_out")
    dq_r, dg, dk_r, dv, gs, slabs_out = _attn_bwd(qr, kr, proj, dmix, sink_v, ga_out)
    g_w_out = _reduce_finish(ga_out, slabs_out, place, "gw_out")
    dq, dk = _rope(dq_r, 0, dk_r, 0, cos_t, -sin_t)
    dz, dxs, db, dc, dacs_g, ddt_g, g_ssd_norm, gdsk = _ssd_bwd(dmix, y_ssd, xbc, proj, dt, acs, acst, states,
                                                                d_skip_l, ssd_norm_w)
    draw, ga, gb = _dt_bwd(dacs_g, ddt_g, dt, proj, dt_bias_l, a_log_l)
    dxs_p, gcw0, gcb0 = _conv_bwd(dxs, proj, conv_w, conv_b, 0, "conv_bwd_x")
    db_p, gcw1, gcb1 = _conv_bwd(db, proj, conv_w, conv_b, D_SSD, "conv_bwd_b")
    dc_p, gcw2, gcb2 = _conv_bwd(dc, proj, conv_w, conv_b, D_SSD + GROUPS * D_STATE, "conv_bwd_c")
    tail = jnp.concatenate([dk, _bf(dv), draw, jnp.zeros((t, DT_SLAB - LANES), BF16)], axis=1)
    segs = [dz, dxs_p, db_p, dc_p, dq, dg, tail]
    g_parts = [_tn_matmul(seg, hn, "gw_in_%d" % s) for s, seg in enumerate(segs)]
    ga_in = _reduce_pair(_re_rows_to_shards(g_parts), place, "gw_in")
    dh, g_norm_pre, slabs_in = _dinproj(segs, w_re, hpad, norm_pre_w, dy_t, ga_in)
    g_w_in = _reduce_finish(ga_in, slabs_in, place, "gw_in")

    gdsk_l = jnp.concatenate([gdsk[g, 0:1, 0:GROUPS] for g in range(GROUPS)], axis=1)
    gdsk_l = jnp.pad(gdsk_l, ((0, 0), (0, LANES - GROUPS * GROUPS)))
    grads = dict(
        meta_tokens=dh[PAD_LEAD:ROW0], norm_pre_w=g_norm_pre, w_in=g_w_in,
        conv_w=jnp.concatenate([gcw0, gcw1, gcw2], axis=1), conv_b=jnp.concatenate([gcb0, gcb1, gcb2], axis=1),
        dt_bias=_gather_heads(gb), a_log=_gather_heads(ga), d_skip=_gather_heads(gdsk_l), ssd_norm_w=g_ssd_norm,
        attn_sinks=gs[0:1, :Q_HEADS], w_out=g_w_out, norm_post_w=g_norm_post)
    return loss_blk[0, 0], dh[ROW0:], grads


ANY = pl.BlockSpec(memory_space=pl.ANY)
MESH = pl.DeviceIdType.MESH
GATHER_CHUNKS = 4
PAIR_CHUNKS = 16
JOIN_CHUNKS = 8


def _rcopy(src, dst, ssem, rsem, dev):
    return pltpu.make_async_remote_copy(src_ref=src, dst_ref=dst, send_sem=ssem, recv_sem=rsem, device_id=dev,
                                        device_id_type=MESH)


def _place():
    x, y, c = lax.axis_index("x"), lax.axis_index("y"), lax.axis_index("c")
    chips = [(1 - x, y), (x, 1 - y), (1 - x, 1 - y)]
    return x, y, c, chips


def _gather_plan(x_ref, out_ref, send_sems, recv_sems, local_sems, hr, kc):
    ch = hr // kc
    assert ch * kc == hr and ch % 16 == 0
    x, y, c, chips = _place()
    me = 2 * x + y
    sibling = (x, y, 1 - c)

    def piece(chip, hc, k):
        return out_ref.at[chip, pl.ds(hc * hr + k * ch, ch), :]

    def local():
        return [pltpu.make_async_copy(x_ref.at[pl.ds(k * ch, ch), :], out_ref.at[me, pl.ds(k * ch, ch), :],
                                      local_sems.at[k]) for k in range(2 * kc)]

    def first():
        return [_rcopy(x_ref.at[pl.ds(c * hr + k * ch, ch), :], piece(me, c, k), send_sems.at[j * kc + k],
                       recv_sems.at[j * kc + k], (*chip, c)) for j, chip in enumerate(chips) for k in range(kc)]

    def passed(hc):
        return [_rcopy(piece(2 * chip[0] + chip[1], hc, k), piece(2 * chip[0] + chip[1], hc, k),
                       send_sems.at[(3 + j) * kc + k], recv_sems.at[(3 + j) * kc + k], sibling)
                for j, chip in enumerate(chips) for k in range(kc)]

    def arrivals():
        return [_rcopy(piece(2 * chip[0] + chip[1], c, k), piece(2 * chip[0] + chip[1], c, k), send_sems.at[j * kc + k],
                       recv_sems.at[j * kc + k], (*chip, c)) for j, chip in enumerate(chips) for k in range(kc)]

    def start():
        for cp in local() + first():
            cp.start()

    def forward():
        for arrived, fw in zip(arrivals(), passed(c)):
            arrived.wait_recv()
            fw.start()

    def finish():
        for cp in passed(1 - c):
            cp.wait_recv()
        for cp in first() + passed(c):
            cp.wait_send()
        for cp in local():
            cp.wait()

    return start, forward, finish


def _gather_shards(shard, name, kc):
    r, n = shard.shape

    def body(x_ref, out_ref, send_sems, recv_sems, local_sems):
        for phase in _gather_plan(x_ref, out_ref, send_sems, recv_sems, local_sems, r // 2, kc):
            phase()

    return _pallas(
        body, name=name, in_specs=[ANY], out_specs=ANY,
        out_shape=jax.ShapeDtypeStruct((N_SHARD, r, n), shard.dtype),
        scratch_shapes=[pltpu.SemaphoreType.DMA((6 * kc,)), pltpu.SemaphoreType.DMA((6 * kc,)),
                        pltpu.SemaphoreType.DMA((2 * kc,))])(shard)


def _pair_send(g4, name):
    _, r, n = g4.shape
    hn = n // 2
    kc = PAIR_CHUNKS // N_SHARD
    cw = hn // kc
    assert cw * kc == hn and cw % LANES == 0

    def body(g_ref, got_ref, send_sems, recv_sems):
        x, y, c, _ = _place()
        cps = [_rcopy(g_ref.at[j, :, pl.ds((1 - c) * hn + k * cw, cw)], got_ref.at[j, :, pl.ds(k * cw, cw)],
                      send_sems.at[j * kc + k], recv_sems.at[j * kc + k], (x, y, 1 - c))
               for j in range(N_SHARD) for k in range(kc)]
        for cp in cps:
            cp.start()
        for cp in cps:
            cp.wait()

    return _pallas(
        body, name=name, in_specs=[ANY], out_specs=ANY, out_shape=jax.ShapeDtypeStruct((N_SHARD, r, hn), F32),
        scratch_shapes=[pltpu.SemaphoreType.DMA((N_SHARD * kc,)), pltpu.SemaphoreType.DMA((N_SHARD * kc,))])(g4)


REDUCE_TILE = 256


def _pair_add(g4, got, core, name):
    k, r, n = g4.shape
    hn = n // 2
    tc = REDUCE_TILE
    nt = hn // tc

    def body(core_ref, a_ref, b_ref, o_ref):
        o_ref[...] = _bf(a_ref[...] + b_ref[...])

    spec = pl.BlockSpec((1, r, tc), lambda j, i, core_ref: (j, 0, i))
    return _pallas(
        body, name=name,
        grid_spec=pltpu.PrefetchScalarGridSpec(
            num_scalar_prefetch=1, grid=(k, nt),
            in_specs=[pl.BlockSpec((1, r, tc), lambda j, i, core_ref: (j, 0, core_ref[0] * nt + i)), spec],
            out_specs=spec),
        out_shape=jax.ShapeDtypeStruct((k, r, hn), BF16), compiler_params=_cp("parallel", "parallel"))(core, g4, got)


def _exchange_copies(g_ref, got_ref, send_sems, recv_sems):
    hn = g_ref.shape[2]
    kc = GATHER_CHUNKS
    cw = hn // kc
    assert cw * kc == hn and cw % LANES == 0
    x, y, c, chips = _place()
    return [_rcopy(g_ref.at[2 * chip[0] + chip[1], :, pl.ds(k * cw, cw)], got_ref.at[j, :, pl.ds(k * cw, cw)],
                   send_sems.at[j * kc + k], recv_sems.at[j * kc + k], (*chip, c))
            for j, chip in enumerate(chips) for k in range(kc)]


def _exchange_scratch():
    return [pltpu.SemaphoreType.DMA((3 * GATHER_CHUNKS,)), pltpu.SemaphoreType.DMA((3 * GATHER_CHUNKS,))]


def _exchange_shape(ga):
    return jax.ShapeDtypeStruct((3,) + ga.shape[1:], ga.dtype)


def _chip_sum(ga, got, place, name):
    _, r, hn = ga.shape
    tc = REDUCE_TILE
    nt = hn // tc

    def body(place_ref, own_ref, got_ref, o_ref):
        acc = own_ref[0].astype(F32)
        for j in range(3):
            acc = acc + got_ref[j].astype(F32)
        o_ref[...] = acc

    return _pallas(
        body, name=name,
        grid_spec=pltpu.PrefetchScalarGridSpec(
            num_scalar_prefetch=1, grid=(nt,),
            in_specs=[pl.BlockSpec((1, r, tc), lambda i, place_ref: (place_ref[0], 0, i)),
                      pl.BlockSpec((3, r, tc), lambda i, place_ref: (0, 0, i))],
            out_specs=pl.BlockSpec((r, tc), lambda i, place_ref: (0, place_ref[1] * nt + i))),
        out_shape=jax.ShapeDtypeStruct((r, 2 * hn), F32), compiler_params=_cp("parallel"))(place, ga, got)


def _pair_join(buf, name):
    r, n = buf.shape
    hn = n // 2
    kc = JOIN_CHUNKS
    cw = hn // kc
    assert cw * kc == hn and cw % LANES == 0

    def body(in_ref, out_ref, send_sems, recv_sems):
        x, y, c, _ = _place()
        cps = [_rcopy(out_ref.at[:, pl.ds(c * hn + k * cw, cw)], out_ref.at[:, pl.ds(c * hn + k * cw, cw)],
                      send_sems.at[k], recv_sems.at[k], (x, y, 1 - c)) for k in range(kc)]
        for cp in cps:
            cp.start()
        for k in range(kc):
            cols = out_ref.at[:, pl.ds((1 - c) * hn + k * cw, cw)]
            _rcopy(cols, cols, send_sems.at[k], recv_sems.at[k], (x, y, 1 - c)).wait_recv()
        for cp in cps:
            cp.wait_send()

    return _pallas(
        body, name=name, in_specs=[ANY], out_specs=ANY, out_shape=jax.ShapeDtypeStruct((r, n), F32),
        input_output_aliases={0: 0},
        scratch_shapes=[pltpu.SemaphoreType.DMA((kc,)), pltpu.SemaphoreType.DMA((kc,))])(buf)


def _reduce_pair(g4, place, tag):
    got = _pair_send(g4, tag + "_pair_send")
    return _pair_add(g4, got, place[1:2], tag + "_pair_add")


def _reduce_finish(ga, slabs, place, tag):
    return _pair_join(_chip_sum(ga, slabs, place, tag + "_chip_sum"), tag + "_pair_join")


def _allreduce_small(p, name):
    rows, n = p.shape
    ndev = 8

    def body(p_ref, out_ref, slots, send_sems, recv_sems):
        x, y, c, _ = _place()
        my = 4 * x + 2 * y + c
        slots[my] = p_ref[...]
        cps = []
        for k in range(1, ndev):
            kx, ky, kc = (k >> 2) & 1, (k >> 1) & 1, k & 1
            peer = (x ^ kx, y ^ ky, c ^ kc)
            cp = _rcopy(p_ref, slots.at[my], send_sems.at[k - 1], recv_sems.at[k - 1], peer)
            cp.start()
            cps.append(cp)
        for k in range(1, ndev):
            _rcopy(p_ref, slots.at[my ^ k], send_sems.at[k - 1], recv_sems.at[k - 1], (x, y, c)).wait_recv()
        for cp in cps:
            cp.wait_send()
        acc = slots[0]
        for j in range(1, ndev):
            acc = acc + slots[j]
        out_ref[...] = acc

    vm = pl.BlockSpec(memory_space=pltpu.VMEM)
    return _pallas(
        body, name=name, in_specs=[vm], out_specs=vm, out_shape=jax.ShapeDtypeStruct((rows, n), F32),
        scratch_shapes=[pltpu.VMEM((ndev, rows, n), F32), pltpu.SemaphoreType.DMA((ndev - 1,)),
                        pltpu.SemaphoreType.DMA((ndev - 1,))])(p)


def _adamw(w, g, m, v, name):
    r, n = w.shape
    tr = _tile(r, 256, 8)
    c1 = 1.0 / (1.0 - ADAM_B1 ** ADAM_STEP)
    c2 = 1.0 / (1.0 - ADAM_B2 ** ADAM_STEP)

    def body(w_ref, g_ref, m_ref, v_ref, d_ref, mo_ref, vo_ref):
        gv = g_ref[...]
        mn = ADAM_B1 * m_ref[...] + (1.0 - ADAM_B1) * gv
        vn = ADAM_B2 * v_ref[...] + (1.0 - ADAM_B2) * (gv * gv)
        d_ref[...] = -ADAM_LR * ((mn * c1) / (jnp.sqrt(vn * c2) + ADAM_EPS) + ADAM_WD * w_ref[...])
        mo_ref[...] = mn
        vo_ref[...] = vn

    spec = pl.BlockSpec((tr, n), lambda i: (i, 0))
    shp = jax.ShapeDtypeStruct((r, n), F32)
    return _pallas(body, name=name, grid=(r // tr,), in_specs=[spec] * 4, out_specs=[spec] * 3, out_shape=[shp] * 3,
                   compiler_params=_cp("parallel"))(w, g, m, v)


PACK_W = 1024
SMALL_REPL = ("norm_pre_w", "conv_b", "ssd_norm_w", "norm_post_w")
SMALL_HEAD = ("dt_bias", "a_log", "d_skip", "attn_sinks")


def _rows(a):
    return a.reshape(-1, PACK_W)


def _head_row(vals, extra=None):
    parts = [vals[n].reshape(1, -1) for n in SMALL_HEAD]
    if extra is not None:
        parts.append(extra.reshape(1, 1))
    row = jnp.concatenate(parts, axis=1)
    return jnp.pad(row, ((0, 0), (0, PACK_W - row.shape[1])))


def _pad_rows(a, rows):
    return jnp.pad(a, ((0, rows - a.shape[0]), (0, 0)))


def _pack_repl(vals, extra=None):
    body = jnp.concatenate([_rows(vals[n]) for n in SMALL_REPL] + [_head_row(vals, extra)], axis=0)
    return _pad_rows(body, 16)


def _unpack_repl(buf):
    out, r = {}, 0
    for n, k in zip(SMALL_REPL, (2, 4, 2, 2)):
        out[n] = buf[r:r + k].reshape(1, k * PACK_W)
        r += k
    col = 0
    for n, k in zip(SMALL_HEAD, (32, 32, 32, 16)):
        out[n] = buf[r:r + 1, col:col + k]
        col += k
    return out, buf[r, col]


def kernel(x, meta_tokens, norm_pre_w, w_in, conv_w, conv_b, dt_bias, a_log, d_skip, ssd_norm_w, attn_sinks, w_out, norm_post_w, loss_target, m_meta_tokens, m_norm_pre_w, m_w_in, m_conv_w, m_conv_b, m_dt_bias, m_a_log, m_d_skip, m_ssd_norm_w, m_attn_sinks, m_w_out, m_norm_post_w, v_meta_tokens, v_norm_pre_w, v_w_in, v_conv_w, v_conv_b, v_dt_bias, v_a_log, v_d_skip, v_ssd_norm_w, v_attn_sinks, v_w_out, v_norm_post_w):
    names = ("meta_tokens", "norm_pre_w", "w_in", "conv_w", "conv_b", "dt_bias", "a_log", "d_skip", "ssd_norm_w",
             "attn_sinks", "w_out", "norm_post_w")
    w = dict(zip(names, (meta_tokens, norm_pre_w, w_in, conv_w, conv_b, dt_bias, a_log, d_skip, ssd_norm_w, attn_sinks,
                         w_out, norm_post_w)))
    m = dict(zip(names, (m_meta_tokens, m_norm_pre_w, m_w_in, m_conv_w, m_conv_b, m_dt_bias, m_a_log, m_d_skip,
                         m_ssd_norm_w, m_attn_sinks, m_w_out, m_norm_post_w)))
    v = dict(zip(names, (v_meta_tokens, v_norm_pre_w, v_w_in, v_conv_w, v_conv_b, v_dt_bias, v_a_log, v_d_skip,
                         v_ssd_norm_w, v_attn_sinks, v_w_out, v_norm_post_w)))
    cx, cy, cc = lax.axis_index("x"), lax.axis_index("y"), lax.axis_index("c")
    chip = 2 * cx + cy
    meta_cols = D_MODEL // N_SHARD
    conv_cols = D_CONV // N_SHARD

    place = jnp.stack([chip, cc]).astype(jnp.int32)
    w_re = _shards_to_re(_gather_shards(_bf(w_in[0]), "gather_w_in", 2 * GATHER_CHUNKS))
    conv_z = lax.dynamic_update_slice(jnp.zeros((CONV_WIDTH, D_CONV), F32), conv_w[0], (0, chip * conv_cols))
    meta_z = lax.dynamic_update_slice(jnp.zeros((N_META, D_MODEL), F32), meta_tokens, (0, chip * meta_cols))
    small = jnp.concatenate([_rows(conv_z), _rows(meta_z)], axis=0)
    small = _allreduce_small(jnp.where(cc == 0, small, 0.0), "gather_small")
    conv_full = small[0:16].reshape(CONV_WIDTH, D_CONV)
    meta_full = small[16:48].reshape(N_META, D_MODEL)

    loss_dev, grad_x, g = _local_step(x[0], loss_target[0], meta_full, norm_pre_w, w_re, conv_full, conv_b, dt_bias,
                                      a_log, d_skip, ssd_norm_w, attn_sinks, _bf(w_out[0]), norm_post_w, place)
    g_w_in, g_w_out = g["w_in"], g["w_out"]

    packed = jnp.concatenate([_rows(g["conv_w"]), _rows(g["meta_tokens"]), _pack_repl(g, loss_dev)], axis=0)
    red = _allreduce_small(packed, "reduce_small")
    g_conv_full = red[0:16].reshape(CONV_WIDTH, D_CONV)
    g_meta_full = red[16:48].reshape(N_META, D_MODEL)
    g_small, loss = _unpack_repl(red[48:64])
    grads = dict(g_small)
    grads["w_in"] = g_w_in
    grads["w_out"] = g_w_out
    grads["conv_w"] = lax.dynamic_slice(g_conv_full, (0, chip * conv_cols), (CONV_WIDTH, conv_cols))
    grads["meta_tokens"] = lax.dynamic_slice(g_meta_full, (0, chip * meta_cols), (N_META, meta_cols))

    upd = {}
    upd["w_in"] = [jnp.swapaxes(a, 0, 1) for a in _adamw(jnp.swapaxes(w_in[0], 0, 1), g_w_in, jnp.swapaxes(m_w_in[0], 0, 1),
                                                         jnp.swapaxes(v_w_in[0], 0, 1), "adamw_w_in")]
    grads["w_in"] = jnp.swapaxes(g_w_in, 0, 1)
    upd["w_out"] = _adamw(w_out[0], g_w_out, m_w_out[0], v_w_out[0], "adamw_w_out")

    def pack_small(vals, conv, meta):
        return jnp.concatenate([_pad_rows(conv.reshape(CONV_WIDTH, conv_cols), 8), _rows(meta), _pack_repl(vals)], axis=0)

    sm = _adamw(pack_small(w, w["conv_w"], w["meta_tokens"]), pack_small(grads, grads["conv_w"], grads["meta_tokens"]),
                pack_small(m, m["conv_w"], m["meta_tokens"]), pack_small(v, v["conv_w"], v["meta_tokens"]),
                "adamw_small")
    for n in names:
        if n not in ("w_in", "w_out"):
            upd[n] = [None, None, None]
    for k, buf in enumerate(sm):
        upd["conv_w"][k] = buf[0:CONV_WIDTH]
        upd["meta_tokens"][k] = buf[8:16].reshape(N_META, meta_cols)
        rest, _ = _unpack_repl(buf[16:32])
        for n in SMALL_REPL + SMALL_HEAD:
            upd[n][k] = rest[n]

    def shaped(n, a):
        return a.reshape(w[n].shape)

    outs = [loss, grad_x[None]]
    outs += [shaped(n, grads[n]) for n in names]
    for k in range(3):
        outs += [shaped(n, upd[n][k]) for n in names]
    return tuple(outs)
```

```python
import functools

import jax
import jax.numpy as jnp
from jax import lax
from jax.experimental import pallas as pl
from jax.experimental.pallas import tpu as pltpu

F32 = jnp.float32
BF16 = jnp.bfloat16

D_MODEL = 2048
CHUNK = 64
N_META = 16
PAD_LEAD = CHUNK - N_META
ROW0 = PAD_LEAD + N_META
EPS = 1e-6
SSD_HEADS = 32
HEAD_DIM = 64
GROUPS = 8
HPG = SSD_HEADS // GROUPS
D_STATE = 128
D_SSD = 2048
GROUP_W = D_SSD // GROUPS
CONV_WIDTH = 4
D_CONV = 4096
Q_HEADS = 16
KV_HEADS = 4
REP = Q_HEADS // KV_HEADS
D_ATT = 1024
D_KV = 256
BAND_CHUNKS = 3
ROPE_THETA = 10000.0
D_MIX = D_SSD + D_ATT
D_IN = 8736
N_SHARD = 4
W_IN_SHARD = D_IN // N_SHARD
W_OUT_SHARD = D_MIX // N_SHARD

OZ, OXS, OB, OC, OQ, OG, OK, OV, ODT = 0, 2048, 4096, 5120, 6144, 7168, 8192, 8448, 8704
DT_SLAB = 512
N_RE = ODT + DT_SLAB
LANES = 128

ADAM_LR, ADAM_B1, ADAM_B2, ADAM_EPS, ADAM_WD, ADAM_STEP = 0.001, 0.9, 0.999, 1e-08, 0.01, 10

SSD_GROUPS_PER_STEP = 8
SEG_TILE = 1024
VMEM_LIMIT = 52 * 1024 * 1024
NEG = -1e30
HI = lax.Precision.HIGHEST


def _pallas(body, **kw):
    return pl.pallas_call(body, **kw)


def _cp(*sem):
    return pltpu.CompilerParams(dimension_semantics=sem, vmem_limit_bytes=VMEM_LIMIT)


def _tile(n, cap, mult=16):
    best = None
    for d in range(mult, min(n, cap) + 1, mult):
        if n % d == 0:
            best = d
    assert best is not None, (n, cap)
    return best


def _nt(a, b):
    return lax.dot_general(a, b, (((1,), (1,)), ((), ())), preferred_element_type=F32)


def _tn(a, b):
    return lax.dot_general(a, b, (((0,), (0,)), ((), ())), preferred_element_type=F32)


def _mm(a, b):
    return jnp.dot(a, b, preferred_element_type=F32)


def _sigmoid(x):
    return 1.0 / (1.0 + jnp.exp(-x))


def _bf(x):
    return x.astype(BF16)


def _inproj(hpad, norm_w, w_re, w_out_shard):
    t, d = hpad.shape
    n = w_re.shape[1]
    tm, tn = _tile(t, 832), 1024
    ni, nj = t // tm, n // tn
    r_out, n_out = w_out_shard.shape
    kc = GATHER_CHUNKS

    def body(h_ref, nw_ref, w_ref, ws_ref, proj_ref, hn_ref, wall_ref, hn_s, send_sems, recv_sems, local_sems):
        i, j = pl.program_id(0), pl.program_id(1)
        start, forward, finish = _gather_plan(ws_ref, wall_ref, send_sems, recv_sems, local_sems, r_out // 2, kc)
        pl.when((i == 0) & (j == 0))(start)
        pl.when((i == ni // 2) & (j == 0))(forward)

        @pl.when(j == 0)
        def _():
            h = h_ref[...]
            ms = jnp.mean(h * h, axis=-1, keepdims=True)
            hn = _bf(h * lax.rsqrt(ms + EPS) * nw_ref[...])
            hn_s[...] = hn
            hn_ref[...] = hn
        proj_ref[...] = _mm(hn_s[...], w_ref[...])
        pl.when((i == ni - 1) & (j == nj - 1))(finish)

    return _pallas(
        body, name="inproj", grid=(ni, nj),
        in_specs=[pl.BlockSpec((tm, d), lambda i, j: (i, 0)), pl.BlockSpec((1, d), lambda i, j: (0, 0)),
                  pl.BlockSpec((d, tn), lambda i, j: (0, j)), ANY],
        out_specs=[pl.BlockSpec((tm, tn), lambda i, j: (i, j)), pl.BlockSpec((tm, d), lambda i, j: (i, 0)), ANY],
        out_shape=[jax.ShapeDtypeStruct((t, n), F32), jax.ShapeDtypeStruct((t, d), BF16),
                   jax.ShapeDtypeStruct((N_SHARD, r_out, n_out), w_out_shard.dtype)],
        scratch_shapes=[pltpu.VMEM((tm, d), BF16), pltpu.SemaphoreType.DMA((6 * kc,)), pltpu.SemaphoreType.DMA((6 * kc,)),
                        pltpu.SemaphoreType.DMA((2 * kc,))],
        compiler_params=_cp("arbitrary", "arbitrary"))(hpad, norm_w, w_re, w_out_shard)


def _conv_fwd(proj, conv_w, conv_b):
    t = proj.shape[0]
    tc = 256
    off = OXS // tc

    def body(x_ref, w_ref, b_ref, o_ref):
        x = x_ref[...]
        w = w_ref[...]
        row = lax.broadcasted_iota(jnp.int32, (t, tc), 0)
        u = b_ref[...] + w[3:4, :] * x
        for k in range(1, CONV_WIDTH):
            u = u + w[3 - k:4 - k, :] * jnp.where(row >= k, pltpu.roll(x, k, 0), 0.0)
        h = 0.5 * u
        o_ref[...] = h + h * jnp.tanh(h)

    return _pallas(
        body, name="conv_fwd", grid=(D_CONV // tc,),
        in_specs=[pl.BlockSpec((t, tc), lambda j: (0, j + off)), pl.BlockSpec((CONV_WIDTH, tc), lambda j: (0, j)),
                  pl.BlockSpec((1, tc), lambda j: (0, j))],
        out_specs=pl.BlockSpec((t, tc), lambda j: (0, j)),
        out_shape=jax.ShapeDtypeStruct((t, D_CONV), F32),
        compiler_params=_cp("parallel"))(proj, conv_w, conv_b)


def _softplus(u):
    e = jnp.exp(-jnp.abs(u))
    w = 1.0 + e
    l1p = jnp.where(w == 1.0, e, jnp.log(w) * (e / jnp.where(w == 1.0, 1.0, w - 1.0)))
    return jnp.maximum(u, 0.0) + l1p


def _chunks_per_step(nc):
    return max(d for d in range(1, 14) if nc % d == 0)


def _dt_prep(proj, dt_bias_l, a_log_l):
    t = proj.shape[0]
    nc = t // CHUNK
    q = CHUNK
    cps = _chunks_per_step(nc)
    rows = cps * q

    def body(raw_ref, bias_ref, alog_ref, dt_ref, acs_ref, acst_ref):
        ri = lax.broadcasted_iota(jnp.int32, (q, q), 0)
        ci = lax.broadcasted_iota(jnp.int32, (q, q), 1)
        tri = (ri >= ci).astype(F32)
        neg_a = -jnp.exp(alog_ref[...])
        for k in range(cps):
            rk = slice(q * k, q * (k + 1))
            sp = _softplus(raw_ref[rk, :] + bias_ref[...])
            row = pl.program_id(0) * rows + q * k + lax.broadcasted_iota(jnp.int32, (q, LANES), 0)
            dt = jnp.where(row >= PAD_LEAD, sp, 0.0)
            acs = jnp.dot(tri, dt * neg_a, preferred_element_type=F32, precision=HI)
            dt_ref[rk, :] = dt
            acs_ref[rk, :] = acs
            acst_ref[k] = acs.T

    return _pallas(
        body, name="dt_prep", grid=(nc // cps,),
        in_specs=[pl.BlockSpec((rows, LANES), lambda c: (c, ODT // LANES)), pl.BlockSpec((1, LANES), lambda c: (0, 0)),
                  pl.BlockSpec((1, LANES), lambda c: (0, 0))],
        out_specs=[pl.BlockSpec((rows, LANES), lambda c: (c, 0)), pl.BlockSpec((rows, LANES), lambda c: (c, 0)),
                   pl.BlockSpec((cps, LANES, q), lambda c: (c, 0, 0))],
        out_shape=[jax.ShapeDtypeStruct((t, LANES), F32), jax.ShapeDtypeStruct((t, LANES), F32),
                   jax.ShapeDtypeStruct((nc, LANES, q), F32)],
        compiler_params=_cp("parallel"))(proj, dt_bias_l, a_log_l)


def _head_cols(blk, idx):
    lane = lax.broadcasted_iota(jnp.int32, blk.shape, 1)
    return jnp.sum(jnp.where(lane == idx, blk, 0.0), axis=1, keepdims=True)


class _HeadVals:
    pass


def _lane_head(shape):
    return lax.broadcasted_iota(jnp.int32, shape, len(shape) - 1) >> 6


def _group_heads(g, gi, dtb, acsb, acst_ref, dskb):
    q = dtb.shape[0]
    hv = _HeadVals()
    lh = _lane_head((1, GROUP_W))
    hv.dt = jnp.zeros((q, GROUP_W), F32)
    hv.acs = jnp.zeros((q, GROUP_W), F32)
    hv.acs_last = jnp.zeros((1, GROUP_W), F32)
    hv.dsk = jnp.zeros((1, GROUP_W), F32)
    rows = []
    for r in range(HPG):
        idx = GROUPS * g + r
        sel = lh == r
        acs_r = acst_ref[0, GROUPS * gi + r:GROUPS * gi + r + 1, :]
        rows.append(acs_r)
        hv.dt = jnp.where(sel, _head_cols(dtb, idx), hv.dt)
        hv.acs = jnp.where(sel, _head_cols(acsb, idx), hv.acs)
        hv.acs_last = jnp.where(sel, acs_r[:, q - 1:q], hv.acs_last)
        hv.dsk = jnp.where(sel, _head_cols(dskb, idx), hv.dsk)
    hv.acs_row = jnp.concatenate(rows, axis=1)
    return hv


def _head_tri(q, lower):
    ri = lax.broadcasted_iota(jnp.int32, (q, GROUP_W), 0)
    li = lax.broadcasted_iota(jnp.int32, (q, GROUP_W), 1) & (HEAD_DIM - 1)
    return ri >= li if lower else ri <= li


def _block_diag(v):
    rb = lax.broadcasted_iota(jnp.int32, (GROUP_W, GROUP_W), 0) >> 6
    cb = lax.broadcasted_iota(jnp.int32, (GROUP_W, GROUP_W), 1) >> 6
    return jnp.where(rb == cb, jnp.concatenate([v] * HPG, axis=0), jnp.zeros((), v.dtype))


def _head_sums(v, r):
    return jnp.sum(jnp.where(_lane_head((1, GROUP_W)) == r, v, 0.0), axis=1, keepdims=True)


def _ssd_fwd(xbc, proj, dt, acs, acst, d_skip_l, ssd_norm_w):
    t = xbc.shape[0]
    q = CHUNK
    nc = t // q

    gps = SSD_GROUPS_PER_STEP
    gw, sw = gps * GROUP_W, gps * D_STATE

    def body(xs_ref, b_ref, c_ref, dt_ref, acs_ref, acst_ref, z_ref, dsk_ref, nw_ref,
             y_ref, ymix_ref, st_ref, state):
        @pl.when(pl.program_id(1) == 0)
        def _():
            state[...] = jnp.zeros_like(state)

        for gi in range(gps):
            g = gps * pl.program_id(0) + gi
            cols = slice(GROUP_W * gi, GROUP_W * (gi + 1))
            x = xs_ref[:, cols]
            bmb = _bf(b_ref[:, D_STATE * gi:D_STATE * (gi + 1)])
            cmb = _bf(c_ref[:, D_STATE * gi:D_STATE * (gi + 1)])
            hv = _group_heads(g, gi, dt_ref[...], acs_ref[...], acst_ref, dsk_ref[...])
            decay = jnp.exp(jnp.where(_head_tri(q, True), hv.acs - hv.acs_row, NEG))
            m_all = _bf(_nt(cmb, jnp.concatenate([bmb] * HPG, axis=0)) * decay)
            xdt = x * hv.dt
            s_prev = state[gi]
            st_ref[0, gi] = s_prev
            y = (_mm(m_all, _block_diag(_bf(xdt))) + _mm(cmb, _bf(s_prev)) * jnp.exp(hv.acs) + hv.dsk * x)
            state[gi] = jnp.exp(hv.acs_last) * s_prev + _tn(bmb, _bf(xdt * jnp.exp(hv.acs_last - hv.acs)))
            y_ref[:, cols] = y
            z = z_ref[:, cols]
            yg = y * (z * _sigmoid(z))
            ms = jnp.mean(yg * yg, axis=-1, keepdims=True)
            ymix_ref[:, cols] = _bf(yg * lax.rsqrt(ms + EPS) * nw_ref[:, cols])

    return _pallas(
        body, name="ssd_fwd", grid=(GROUPS // gps, nc),
        in_specs=[pl.BlockSpec((q, gw), lambda g, c: (c, g)),
                  pl.BlockSpec((q, sw), lambda g, c: (c, D_SSD // sw + g)),
                  pl.BlockSpec((q, sw), lambda g, c: (c, (D_SSD + GROUPS * D_STATE) // sw + g)),
                  pl.BlockSpec((q, LANES), lambda g, c: (c, 0)), pl.BlockSpec((q, LANES), lambda g, c: (c, 0)),
                  pl.BlockSpec((1, gps * GROUPS, q), lambda g, c: (c, g, 0)),
                  pl.BlockSpec((q, gw), lambda g, c: (c, g)),
                  pl.BlockSpec((1, LANES), lambda g, c: (0, 0)), pl.BlockSpec((1, gw), lambda g, c: (0, g))],
        out_specs=[pl.BlockSpec((q, gw), lambda g, c: (c, g)), pl.BlockSpec((q, gw), lambda g, c: (c, g)),
                   pl.BlockSpec((1, gps, D_STATE, GROUP_W), lambda g, c: (c, g, 0, 0))],
        out_shape=[jax.ShapeDtypeStruct((t, D_SSD), F32), jax.ShapeDtypeStruct((t, D_SSD), BF16),
                   jax.ShapeDtypeStruct((nc, GROUPS, D_STATE, GROUP_W), F32)],
        scratch_shapes=[pltpu.VMEM((gps, D_STATE, GROUP_W), F32)],
        compiler_params=_cp("parallel", "arbitrary"))(xbc, xbc, xbc, dt, acs, acst, proj, d_skip_l, ssd_norm_w)


def _swap_halves(v):
    lane = lax.broadcasted_iota(jnp.int32, v.shape, 1)
    return jnp.where((lane & (HEAD_DIM - 1)) < HEAD_DIM // 2, pltpu.roll(v, LANES - HEAD_DIM // 2, 1),
                     pltpu.roll(v, HEAD_DIM // 2, 1))


def _rope(qsrc, q_off, ksrc, k_off, cos_t, sin_t):
    t = qsrc.shape[0]
    tr = _tile(t, 832)

    def body(q_ref, k_ref, cos_ref, sin_ref, qo_ref, ko_ref):
        cs = cos_ref[...]
        sn = sin_ref[...]
        for src, dst, width in ((q_ref, qo_ref, D_ATT), (k_ref, ko_ref, D_KV)):
            for s in range(width // LANES):
                v = src[:, LANES * s:LANES * (s + 1)].astype(F32)
                dst[:, LANES * s:LANES * (s + 1)] = _bf(v * cs + _swap_halves(v) * sn)

    return _pallas(
        body, name="rope", grid=(t // tr,),
        in_specs=[pl.BlockSpec((tr, D_ATT), lambda i: (i, q_off // D_ATT)),
                  pl.BlockSpec((tr, D_KV), lambda i: (i, k_off // D_KV)),
                  pl.BlockSpec((tr, LANES), lambda i: (i, 0)), pl.BlockSpec((tr, LANES), lambda i: (i, 0))],
        out_specs=[pl.BlockSpec((tr, D_ATT), lambda i: (i, 0)), pl.BlockSpec((tr, D_KV), lambda i: (i, 0))],
        out_shape=[jax.ShapeDtypeStruct((t, D_ATT), BF16), jax.ShapeDtypeStruct((t, D_KV), BF16)],
        compiler_params=_cp("parallel"))(qsrc, ksrc, cos_t, sin_t)


def _band_specs(width, col_block):
    return [pl.BlockSpec((CHUNK, width), functools.partial(lambda c, j: (jnp.maximum(c - j, 0), col_block), j=j))
            for j in (2, 1, 0)]


def _attn_probs(qh, kb, sink_col, valid):
    s = _nt(qh, kb) * (HEAD_DIM ** -0.5)
    s = jnp.where(valid, s, NEG)
    m = jnp.maximum(jnp.max(s, axis=1, keepdims=True), sink_col)
    p = jnp.exp(s - m)
    psink = jnp.exp(sink_col - m)
    inv = 1.0 / (jnp.sum(p, axis=1, keepdims=True) + psink)
    return p * inv, psink * inv


def _attn_operands(c, q_ref, k_refs, v_refs, sink_ref, h):
    q = q_ref[...]
    qh = jnp.concatenate([q[:, HEAD_DIM * (REP * h + r):HEAD_DIM * (REP * h + r + 1)] for r in range(REP)], axis=0)
    kb = jnp.concatenate([k[:, HEAD_DIM * h:HEAD_DIM * (h + 1)] for k in k_refs], axis=0)
    vb = jnp.concatenate([_bf(v[:, HEAD_DIM * h:HEAD_DIM * (h + 1)]) for v in v_refs], axis=0)
    rows = lax.broadcasted_iota(jnp.int32, (REP * CHUNK, 1), 0) >> 6
    sink_col = jnp.zeros((REP * CHUNK, 1), F32)
    for r in range(REP):
        sink_col = jnp.where(rows == r, sink_ref[REP * h + r], sink_col)
    key_abs = (c - (BAND_CHUNKS - 1)) * CHUNK + lax.broadcasted_iota(jnp.int32, (1, BAND_CHUNKS * CHUNK), 1)
    return qh, kb, vb, sink_col, key_abs >= PAD_LEAD


def _attn_fwd(qr, kr, proj, sinks):
    t = qr.shape[0]
    nc = t // CHUNK

    def body(q_ref, k2, k1, k0, v2, v1, v0, g_ref, sink_ref, o_ref):
        c = pl.program_id(0)
        ks = [k2[...], k1[...], k0[...]]
        vs = [v2[...], v1[...], v0[...]]
        outs = []
        for h in range(KV_HEADS):
            qh, kb, vb, sink_col, valid = _attn_operands(c, q_ref, ks, vs, sink_ref, h)
            p, _ = _attn_probs(qh, kb, sink_col, valid)
            o = _mm(_bf(p), vb)
            outs += [o[CHUNK * r:CHUNK * (r + 1)] for r in range(REP)]
        att = jnp.concatenate(outs, axis=1)
        gate = g_ref[...]
        o_ref[...] = _bf(att * (gate * _sigmoid(gate)))

    return _pallas(
        body, name="attn_fwd", grid=(nc,),
        in_specs=[pl.BlockSpec((CHUNK, D_ATT), lambda c: (c, 0))] + _band_specs(D_KV, 0)
        + _band_specs(D_KV, OV // D_KV) + [pl.BlockSpec((CHUNK, D_ATT), lambda c: (c, OG // D_ATT)),
                                           pl.BlockSpec(memory_space=pltpu.SMEM)],
        out_specs=pl.BlockSpec((CHUNK, D_ATT), lambda c: (c, 0)),
        out_shape=jax.ShapeDtypeStruct((t, D_ATT), BF16),
        compiler_params=_cp("parallel"))(qr, kr, kr, kr, proj, proj, proj, proj, sinks)


def _outproj(ymix, amix, w_out):
    t = ymix.shape[0]
    tm, tn = _tile(t, 832), 1024

    def body(y_ref, a_ref, wy_ref, wa_ref, o_ref):
        o_ref[...] = _mm(y_ref[...], wy_ref[...]) + _mm(a_ref[...], wa_ref[...])

    return _pallas(
        body, name="outproj", grid=(t // tm, D_MODEL // tn),
        in_specs=[pl.BlockSpec((tm, D_SSD), lambda i, j: (i, 0)), pl.BlockSpec((tm, D_ATT), lambda i, j: (i, 0)),
                  pl.BlockSpec((D_SSD, tn), lambda i, j: (0, j)),
                  pl.BlockSpec((D_ATT, tn), lambda i, j: (D_SSD // D_ATT, j))],
        out_specs=pl.BlockSpec((tm, tn), lambda i, j: (i, j)),
        out_shape=jax.ShapeDtypeStruct((t, D_MODEL), F32),
        compiler_params=_cp("parallel", "parallel"))(ymix, amix, w_out, w_out)


def _post_loss(out, x, target, norm_post_w):
    t = out.shape[0]
    nc = t // CHUNK

    def body(o_ref, x_ref, tg_ref, nw_ref, dout_ref, dy_ref, loss_ref, gnw_ref):
        i = pl.program_id(0)

        @pl.when(i == 0)
        def _():
            dout_ref[...] = jnp.zeros_like(dout_ref)
            dy_ref[...] = jnp.zeros_like(dy_ref)
            loss_ref[...] = jnp.zeros_like(loss_ref)
            gnw_ref[...] = jnp.zeros_like(gnw_ref)

        @pl.when(i > 0)
        def _():
            o = o_ref[...]
            nw = nw_ref[...]
            rstd = lax.rsqrt(jnp.mean(o * o, axis=-1, keepdims=True) + EPS)
            n = o * rstd
            err = x_ref[...] + n * nw - tg_ref[...]
            loss_ref[...] += jnp.sum(err * err) * (0.5 / D_MODEL)
            dy = err * (1.0 / D_MODEL)
            dy_ref[...] = dy
            gnw_ref[...] += jnp.sum(dy * n, axis=0, keepdims=True)
            dn = dy * nw
            dout_ref[...] = _bf(rstd * (dn - n * jnp.mean(dn * n, axis=-1, keepdims=True)))

    prev = lambda i: (jnp.maximum(i - 1, 0), 0)
    return _pallas(
        body, name="post_loss", grid=(nc,),
        in_specs=[pl.BlockSpec((CHUNK, D_MODEL), lambda i: (i, 0)), pl.BlockSpec((CHUNK, D_MODEL), prev),
                  pl.BlockSpec((CHUNK, D_MODEL), prev), pl.BlockSpec((1, D_MODEL), lambda i: (0, 0))],
        out_specs=[pl.BlockSpec((CHUNK, D_MODEL), lambda i: (i, 0)), pl.BlockSpec((CHUNK, D_MODEL), lambda i: (i, 0)),
                   pl.BlockSpec((8, LANES), lambda i: (0, 0)), pl.BlockSpec((1, D_MODEL), lambda i: (0, 0))],
        out_shape=[jax.ShapeDtypeStruct((t, D_MODEL), BF16), jax.ShapeDtypeStruct((t, D_MODEL), F32),
                   jax.ShapeDtypeStruct((8, LANES), F32), jax.ShapeDtypeStruct((1, D_MODEL), F32)],
        compiler_params=_cp("arbitrary"))(out, x, target, norm_post_w)


def _nt_matmul(a, b, name):
    t, k = a.shape
    n = b.shape[0]
    tm, tn = _tile(t, 832), 1024

    def body(a_ref, b_ref, o_ref):
        o_ref[...] = _nt(a_ref[...], b_ref[...])

    return _pallas(
        body, name=name, grid=(t // tm, n // tn),
        in_specs=[pl.BlockSpec((tm, k), lambda i, j: (i, 0)), pl.BlockSpec((tn, k), lambda i, j: (j, 0))],
        out_specs=pl.BlockSpec((tm, tn), lambda i, j: (i, j)),
        out_shape=jax.ShapeDtypeStruct((t, n), F32),
        compiler_params=_cp("parallel", "parallel"))(a, b)


def _tn_matmul(a, b, name):
    t, m = a.shape
    n = b.shape[1]
    tk, tm, tn = _tile(t, 832), min(m, 1024), min(n, 2048)
    nk = t // tk

    def body(a_ref, b_ref, o_ref):
        @pl.when(pl.program_id(2) == 0)
        def _():
            o_ref[...] = jnp.zeros_like(o_ref)
        o_ref[...] += _tn(a_ref[...], b_ref[...])

    return _pallas(
        body, name=name, grid=(m // tm, n // tn, nk),
        in_specs=[pl.BlockSpec((tk, tm), lambda i, j, k: (k, i)), pl.BlockSpec((tk, tn), lambda i, j, k: (k, j))],
        out_specs=pl.BlockSpec((tm, tn), lambda i, j, k: (i, j)),
        out_shape=jax.ShapeDtypeStruct((m, n), F32),
        compiler_params=_cp("parallel", "parallel", "arbitrary"))(a, b)


def _attn_bwd(qr, kr, proj, dmix, sinks, ga):
    t = qr.shape[0]
    nc = t // CHUNK
    scale = HEAD_DIM ** -0.5

    def body(q_ref, k2, k1, k0, v2, v1, v0, g_ref, da_ref, sink_ref, ga_ref, dq_ref, dg_ref, dk_ref, dv_ref, gs_ref,
             got_ref, send_sems, recv_sems):
        c = pl.program_id(0)

        @pl.when(c == 0)
        def _():
            for cp in _exchange_copies(ga_ref, got_ref, send_sems, recv_sems):
                cp.start()
            dk_ref[...] = jnp.zeros_like(dk_ref)
            dv_ref[...] = jnp.zeros_like(dv_ref)
            gs_ref[...] = jnp.zeros_like(gs_ref)

        ks = [k2[...], k1[...], k0[...]]
        vs = [v2[...], v1[...], v0[...]]
        gate = g_ref[...]
        sg = _sigmoid(gate)
        da = da_ref[...]
        datt = da * (gate * sg)
        lane = lax.broadcasted_iota(jnp.int32, (1, LANES), 1)
        rows = lax.broadcasted_iota(jnp.int32, (REP * CHUNK, 1), 0) >> 6
        dqs, atts, dks, dvs = [], [], [], []
        gs = jnp.zeros((1, LANES), F32)
        for h in range(KV_HEADS):
            qh, kb, vb, sink_col, valid = _attn_operands(c, q_ref, ks, vs, sink_ref, h)
            p, psink = _attn_probs(qh, kb, sink_col, valid)
            pb = _bf(p)
            o = _mm(pb, vb)
            do = jnp.concatenate([datt[:, HEAD_DIM * (REP * h + r):HEAD_DIM * (REP * h + r + 1)] for r in range(REP)],
                                 axis=0)
            dob = _bf(do)
            delta = jnp.sum(do * o, axis=1, keepdims=True)
            ds = _bf(p * (_nt(dob, vb) - delta) * scale)
            gsink = -psink * delta
            for r in range(REP):
                gs = gs + jnp.where(lane == REP * h + r, jnp.sum(jnp.where(rows == r, gsink, 0.0)), 0.0)
            dqh = _mm(ds, kb)
            dqs += [dqh[CHUNK * r:CHUNK * (r + 1)] for r in range(REP)]
            atts += [o[CHUNK * r:CHUNK * (r + 1)] for r in range(REP)]
            dks.append(_tn(ds, qh))
            dvs.append(_tn(pb, dob))
        dq_ref[...] = jnp.concatenate(dqs, axis=1)
        att = jnp.concatenate(atts, axis=1)
        dg_ref[...] = _bf(da * att * (sg * (1.0 + gate * (1.0 - sg))))
        gs_ref[0:1, :] += gs
        dkf = jnp.concatenate(dks, axis=1)
        dvf = jnp.concatenate(dvs, axis=1)
        for j in range(BAND_CHUNKS):
            r0 = pl.multiple_of(jnp.maximum(c - (BAND_CHUNKS - 1) + j, 0) * CHUNK, CHUNK)
            dk_ref[pl.ds(r0, CHUNK), :] += dkf[CHUNK * j:CHUNK * (j + 1)]
            dv_ref[pl.ds(r0, CHUNK), :] += dvf[CHUNK * j:CHUNK * (j + 1)]

        @pl.when(c == nc - 1)
        def _():
            for cp in _exchange_copies(ga_ref, got_ref, send_sems, recv_sems):
                cp.wait()

    return _pallas(
        body, name="attn_bwd", grid=(nc,),
        in_specs=[pl.BlockSpec((CHUNK, D_ATT), lambda c: (c, 0))] + _band_specs(D_KV, 0)
        + _band_specs(D_KV, OV // D_KV) + [pl.BlockSpec((CHUNK, D_ATT), lambda c: (c, OG // D_ATT)),
                                           pl.BlockSpec((CHUNK, D_ATT), lambda c: (c, D_SSD // D_ATT)),
                                           pl.BlockSpec(memory_space=pltpu.SMEM), ANY],
        out_specs=[pl.BlockSpec((CHUNK, D_ATT), lambda c: (c, 0)), pl.BlockSpec((CHUNK, D_ATT), lambda c: (c, 0)),
                   pl.BlockSpec((t, D_KV), lambda c: (0, 0)), pl.BlockSpec((t, D_KV), lambda c: (0, 0)),
                   pl.BlockSpec((8, LANES), lambda c: (0, 0)), ANY],
        out_shape=[jax.ShapeDtypeStruct((t, D_ATT), F32), jax.ShapeDtypeStruct((t, D_ATT), BF16),
                   jax.ShapeDtypeStruct((t, D_KV), F32), jax.ShapeDtypeStruct((t, D_KV), F32),
                   jax.ShapeDtypeStruct((8, LANES), F32), _exchange_shape(ga)],
        scratch_shapes=_exchange_scratch(),
        compiler_params=_cp("arbitrary"))(qr, kr, kr, kr, proj, proj, proj, proj, dmix, sinks, ga)


def _ssd_bwd(dmix, y_ssd, xbc, proj, dt, acs, acst, states, d_skip_l, ssd_norm_w):
    t = xbc.shape[0]
    q = CHUNK
    nc = t // q
    gps = SSD_GROUPS_PER_STEP
    gw, sw = gps * GROUP_W, gps * D_STATE

    def body(dmix_ref, y_ref, z_ref, nw_ref, xs_ref, b_ref, c_ref, dt_ref, acs_ref, acst_ref, st_ref, dsk_ref,
             dz_ref, dxs_ref, db_ref, dc_ref, dacs_ref, ddt_ref, gnw_ref, gdsk_ref, dstate):
        @pl.when(pl.program_id(1) == 0)
        def _():
            dstate[...] = jnp.zeros_like(dstate)
            gnw_ref[...] = jnp.zeros_like(gnw_ref)
            gdsk_ref[...] = jnp.zeros_like(gdsk_ref)

        last_row = lax.broadcasted_iota(jnp.int32, (q, 1), 0) == q - 1
        lane = lax.broadcasted_iota(jnp.int32, (q, LANES), 1)
        lane1 = lax.broadcasted_iota(jnp.int32, (8, LANES), 1)
        for gi in range(gps):
            g = gps * pl.program_id(0) + gi
            cols = slice(GROUP_W * gi, GROUP_W * (gi + 1))
            scols = slice(D_STATE * gi, D_STATE * (gi + 1))
            y = y_ref[:, cols]
            z = z_ref[:, cols]
            sz = _sigmoid(z)
            silu_z = z * sz
            yg = y * silu_z
            rstd = lax.rsqrt(jnp.mean(yg * yg, axis=-1, keepdims=True) + EPS)
            n = yg * rstd
            dout = dmix_ref[:, cols]
            gnw_ref[:, cols] += jnp.sum(dout * n, axis=0, keepdims=True)
            dn = dout * nw_ref[:, cols]
            dyg = rstd * (dn - n * jnp.mean(dn * n, axis=-1, keepdims=True))
            dy = dyg * silu_z
            dz_ref[:, cols] = _bf(dyg * y * (sz * (1.0 + z * (1.0 - sz))))

            x = xs_ref[:, cols]
            bmb, cmb = _bf(b_ref[:, scols]), _bf(c_ref[:, scols])
            hv = _group_heads(g, gi, dt_ref[...], acs_ref[...], acst_ref, dsk_ref[...])
            dec = jnp.exp(jnp.where(_head_tri(q, True), hv.acs - hv.acs_row, NEG))
            dect = jnp.exp(jnp.where(_head_tri(q, False), hv.acs_row - hv.acs, NEG))
            b4 = jnp.concatenate([bmb] * HPG, axis=0)
            c4 = jnp.concatenate([cmb] * HPG, axis=0)
            m_all = _nt(cmb, b4) * dec
            mt_all = _nt(bmb, c4) * dect
            xdt = x * hv.dt
            xdt_b, dyb = _bf(xdt), _bf(dy)
            x_bd, dy_bd = _block_diag(xdt_b), _block_diag(dyb)
            s_prev = st_ref[0, gi]
            spb = _bf(s_prev)
            ds_new = dstate[gi]
            dsb = _bf(ds_new)
            e = jnp.exp(hv.acs)
            elast = jnp.exp(hv.acs_last)
            dte = jnp.exp(hv.acs_last - hv.acs)
            bds = _mm(bmb, dsb)
            dxdt = _mm(_bf(mt_all), dy_bd) + bds * dte
            dm = _nt(dyb, x_bd)
            dmt = _nt(xdt_b, dy_bd)
            dye = _bf(dy * e)
            dc_ref[:, scols] = _mm(_bf(dm * dec), b4) + _nt(dye, spb)
            db_ref[:, scols] = _mm(_bf(dmt * dect), c4) + _nt(_bf(xdt * dte), dsb)
            dstate[gi] = elast * ds_new + _tn(cmb, dye)
            dxs_ref[:, cols] = dxdt * hv.dt + hv.dsk * dy
            ddte_dte = bds * xdt * dte
            dacs_l = dm * m_all - dmt * mt_all + dy * _mm(cmb, spb) * e - ddte_dte
            dlast_l = (jnp.sum(ddte_dte, axis=0, keepdims=True)
                       + jnp.sum(s_prev * ds_new, axis=0, keepdims=True) * elast)
            ddt_l = dxdt * x
            gdsk_l = jnp.sum(dy * x, axis=0, keepdims=True)
            dacs_out = jnp.zeros((q, LANES), F32)
            ddt_out = jnp.zeros((q, LANES), F32)
            gdsk = jnp.zeros((8, LANES), F32)
            for r in range(HPG):
                dacs = _head_sums(dacs_l, r) + jnp.where(last_row, _head_sums(dlast_l, r), 0.0)
                dacs_out = jnp.where(lane == r, dacs, dacs_out)
                ddt_out = jnp.where(lane == r, _head_sums(ddt_l, r), ddt_out)
                gdsk = gdsk + jnp.where(lane1 == r, _head_sums(gdsk_l, r), 0.0)
            dacs_ref[:, LANES * gi:LANES * (gi + 1)] = dacs_out
            ddt_ref[:, LANES * gi:LANES * (gi + 1)] = ddt_out
            gdsk_ref[gi] += gdsk

    rev = lambda c: nc - 1 - c
    wide = pl.BlockSpec((q, gw), lambda g, c: (rev(c), g))
    return _pallas(
        body, name="ssd_bwd", grid=(GROUPS // gps, nc),
        in_specs=[wide, wide, wide, pl.BlockSpec((1, gw), lambda g, c: (0, g)), wide,
                  pl.BlockSpec((q, sw), lambda g, c: (rev(c), D_SSD // sw + g)),
                  pl.BlockSpec((q, sw), lambda g, c: (rev(c), (D_SSD + GROUPS * D_STATE) // sw + g)),
                  pl.BlockSpec((q, LANES), lambda g, c: (rev(c), 0)), pl.BlockSpec((q, LANES), lambda g, c: (rev(c), 0)),
                  pl.BlockSpec((1, gps * GROUPS, q), lambda g, c: (rev(c), g, 0)),
                  pl.BlockSpec((1, gps, D_STATE, GROUP_W), lambda g, c: (rev(c), g, 0, 0)),
                  pl.BlockSpec((1, LANES), lambda g, c: (0, 0))],
        out_specs=[wide, wide,
                   pl.BlockSpec((q, sw), lambda g, c: (rev(c), g)), pl.BlockSpec((q, sw), lambda g, c: (rev(c), g)),
                   pl.BlockSpec((q, gps * LANES), lambda g, c: (rev(c), g)),
                   pl.BlockSpec((q, gps * LANES), lambda g, c: (rev(c), g)),
                   pl.BlockSpec((1, gw), lambda g, c: (0, g)), pl.BlockSpec((gps, 8, LANES), lambda g, c: (g, 0, 0))],
        out_shape=[jax.ShapeDtypeStruct((t, D_SSD), BF16), jax.ShapeDtypeStruct((t, D_SSD), F32),
                   jax.ShapeDtypeStruct((t, GROUPS * D_STATE), F32), jax.ShapeDtypeStruct((t, GROUPS * D_STATE), F32),
                   jax.ShapeDtypeStruct((t, GROUPS * LANES), F32), jax.ShapeDtypeStruct((t, GROUPS * LANES), F32),
                   jax.ShapeDtypeStruct((1, D_SSD), F32), jax.ShapeDtypeStruct((GROUPS, 8, LANES), F32)],
        scratch_shapes=[pltpu.VMEM((gps, D_STATE, GROUP_W), F32)],
        compiler_params=_cp("parallel", "arbitrary"))(dmix, y_ssd, proj, ssd_norm_w, xbc, xbc, xbc, dt, acs, acst,
                                                      states, d_skip_l)


def _dt_bwd(dacs_g, ddt_g, dt, proj, dt_bias_l, a_log_l):
    t = dt.shape[0]
    q = CHUNK
    nc = t // q
    cps = _chunks_per_step(nc)
    rows = cps * q

    def body(dacs_ref, ddt_ref, dt_ref, raw_ref, bias_ref, alog_ref, draw_ref, ga_ref, gb_ref):
        @pl.when(pl.program_id(0) == 0)
        def _():
            ga_ref[...] = jnp.zeros_like(ga_ref)
            gb_ref[...] = jnp.zeros_like(gb_ref)

        lane = lax.broadcasted_iota(jnp.int32, (q, LANES), 1)
        ri = lax.broadcasted_iota(jnp.int32, (q, q), 0)
        ci = lax.broadcasted_iota(jnp.int32, (q, q), 1)
        triu = (ri <= ci).astype(F32)
        a = -jnp.exp(alog_ref[...])
        used = (lane & (GROUPS - 1)) < HPG
        ga = jnp.zeros((1, LANES), F32)
        gb = jnp.zeros((1, LANES), F32)
        for k in range(cps):
            rk = slice(q * k, q * (k + 1))
            dacs = jnp.zeros((q, LANES), F32)
            ddt = jnp.zeros((q, LANES), F32)
            for g in range(GROUPS):
                mask = (lane >= GROUPS * g) & (lane < GROUPS * g + HPG)
                sl = slice(LANES * g, LANES * (g + 1))
                if g == 0:
                    dacs = jnp.where(mask, dacs_ref[rk, sl], dacs)
                    ddt = jnp.where(mask, ddt_ref[rk, sl], ddt)
                else:
                    dacs = jnp.where(mask, pltpu.roll(dacs_ref[rk, sl], GROUPS * g, 1), dacs)
                    ddt = jnp.where(mask, pltpu.roll(ddt_ref[rk, sl], GROUPS * g, 1), ddt)
            dda = jnp.dot(triu, dacs, preferred_element_type=F32, precision=HI)
            row = pl.program_id(0) * rows + q * k + lax.broadcasted_iota(jnp.int32, (q, LANES), 0)
            dsp = jnp.where((row >= PAD_LEAD) & used, dda * a + ddt, 0.0)
            draw = dsp * _sigmoid(raw_ref[rk, :] + bias_ref[...])
            draw_ref[rk, :] = _bf(draw)
            gb = gb + jnp.sum(draw, axis=0, keepdims=True)
            ga = ga + jnp.sum(jnp.where(used, dda * dt_ref[rk, :], 0.0), axis=0, keepdims=True)
        gb_ref[0:1, :] += gb
        ga_ref[0:1, :] += ga * a

    return _pallas(
        body, name="dt_bwd", grid=(nc // cps,),
        in_specs=[pl.BlockSpec((rows, GROUPS * LANES), lambda c: (c, 0)),
                  pl.BlockSpec((rows, GROUPS * LANES), lambda c: (c, 0)),
                  pl.BlockSpec((rows, LANES), lambda c: (c, 0)), pl.BlockSpec((rows, LANES), lambda c: (c, ODT // LANES)),
                  pl.BlockSpec((1, LANES), lambda c: (0, 0)), pl.BlockSpec((1, LANES), lambda c: (0, 0))],
        out_specs=[pl.BlockSpec((rows, LANES), lambda c: (c, 0)), pl.BlockSpec((8, LANES), lambda c: (0, 0)),
                   pl.BlockSpec((8, LANES), lambda c: (0, 0))],
        out_shape=[jax.ShapeDtypeStruct((t, LANES), BF16), jax.ShapeDtypeStruct((8, LANES), F32),
                   jax.ShapeDtypeStruct((8, LANES), F32)],
        compiler_params=_cp("arbitrary"))(dacs_g, ddt_g, dt, proj, dt_bias_l, a_log_l)


def _conv_bwd(dseg, proj, conv_w, conv_b, col_off, name):
    t, width = dseg.shape
    tc = 128
    off_p = (OXS + col_off) // tc
    off_w = col_off // tc

    def body(d_ref, x_ref, w_ref, b_ref, dx_ref, gw_ref, gb_ref, xp, dup):
        xp[0:8, :] = jnp.zeros((8, tc), F32)
        xp[8:t + 8, :] = x_ref[...]
        w = w_ref[...]
        u = (b_ref[...] + w[3:4, :] * xp[8:t + 8, :] + w[2:3, :] * xp[7:t + 7, :]
             + w[1:2, :] * xp[6:t + 6, :] + w[0:1, :] * xp[5:t + 5, :])
        su = _sigmoid(u)
        du = d_ref[...] * (su * (1.0 + u * (1.0 - su)))
        dup[0:t, :] = du
        dup[t:t + 8, :] = jnp.zeros((8, tc), F32)
        dx_ref[...] = _bf(w[3:4, :] * du + w[2:3, :] * dup[1:t + 1, :] + w[1:2, :] * dup[2:t + 2, :]
                          + w[0:1, :] * dup[3:t + 3, :])
        gb_ref[...] = jnp.sum(du, axis=0, keepdims=True)
        gw_ref[...] = jnp.concatenate(
            [jnp.sum(du * xp[5 + k:t + 5 + k, :], axis=0, keepdims=True) for k in range(CONV_WIDTH)], axis=0)

    return _pallas(
        body, name=name, grid=(width // tc,),
        in_specs=[pl.BlockSpec((t, tc), lambda j: (0, j)), pl.BlockSpec((t, tc), lambda j: (0, j + off_p)),
                  pl.BlockSpec((CONV_WIDTH, tc), lambda j: (0, j + off_w)), pl.BlockSpec((1, tc), lambda j: (0, j + off_w))],
        out_specs=[pl.BlockSpec((t, tc), lambda j: (0, j)), pl.BlockSpec((CONV_WIDTH, tc), lambda j: (0, j)),
                   pl.BlockSpec((1, tc), lambda j: (0, j))],
        out_shape=[jax.ShapeDtypeStruct((t, width), BF16), jax.ShapeDtypeStruct((CONV_WIDTH, width), F32),
                   jax.ShapeDtypeStruct((1, width), F32)],
        scratch_shapes=[pltpu.VMEM((t + 8, tc), F32), pltpu.VMEM((t + 8, tc), F32)],
        compiler_params=_cp("parallel"))(dseg, proj, conv_w, conv_b)


def _dinproj(segs, w_re, hpad, norm_w, dy_t, ga):
    t = segs[0].shape[0]
    d = hpad.shape[1]
    tm, tk = _tile(t, 416), SEG_TILE
    counts = [s.shape[1] // tk for s in segs]
    firsts = [sum(counts[:s]) for s in range(len(segs))]
    nk = sum(counts)
    assert nk * tk == w_re.shape[1]
    ni = t // tm
    ns = len(segs)

    def body(*refs):
        seg_refs = refs[:ns]
        w_ref, h_ref, nw_ref, dy_ref, ga_ref, dh_ref, gnw_ref, got_ref, acc, send_sems, recv_sems = refs[ns:]
        i, k = pl.program_id(0), pl.program_id(1)

        @pl.when((i == 0) & (k == 0))
        def _():
            for cp in _exchange_copies(ga_ref, got_ref, send_sems, recv_sems):
                cp.start()
            gnw_ref[...] = jnp.zeros_like(gnw_ref)

        @pl.when(k == 0)
        def _():
            acc[...] = jnp.zeros_like(acc)

        for s in range(ns):
            @pl.when((k >= firsts[s]) & (k < firsts[s] + counts[s]))
            def _(s=s):
                acc[...] += _nt(seg_refs[s][...], w_ref[...])

        @pl.when(k == nk - 1)
        def _():
            h = h_ref[...]
            rstd = lax.rsqrt(jnp.mean(h * h, axis=-1, keepdims=True) + EPS)
            nrm = h * rstd
            dhn = acc[...]
            gnw_ref[...] += jnp.sum(dhn * nrm, axis=0, keepdims=True)
            dn = dhn * nw_ref[...]
            dh_ref[...] = rstd * (dn - nrm * jnp.mean(dn * nrm, axis=-1, keepdims=True)) + dy_ref[...]

        @pl.when((i == ni - 1) & (k == nk - 1))
        def _():
            for cp in _exchange_copies(ga_ref, got_ref, send_sems, recv_sems):
                cp.wait()

    seg_specs = [pl.BlockSpec((tm, tk), functools.partial(lambda i, k, f0, n0: (i, jnp.clip(k - f0, 0, n0 - 1)),
                                                          f0=firsts[s], n0=counts[s])) for s in range(ns)]
    return _pallas(
        body, name="dinproj", grid=(ni, nk),
        in_specs=seg_specs + [pl.BlockSpec((d, tk), lambda i, k: (0, k)),
                              pl.BlockSpec((tm, d), lambda i, k: (i, 0)), pl.BlockSpec((1, d), lambda i, k: (0, 0)),
                              pl.BlockSpec((tm, d), lambda i, k: (i, 0)), ANY],
        out_specs=[pl.BlockSpec((tm, d), lambda i, k: (i, 0)), pl.BlockSpec((1, d), lambda i, k: (0, 0)), ANY],
        out_shape=[jax.ShapeDtypeStruct((t, d), F32), jax.ShapeDtypeStruct((1, d), F32), _exchange_shape(ga)],
        scratch_shapes=[pltpu.VMEM((tm, d), F32)] + _exchange_scratch(),
        compiler_params=_cp("arbitrary", "arbitrary"))(*segs, w_re, hpad, norm_w, dy_t, ga)


def _spread_heads(v):
    v = jnp.pad(v.reshape(GROUPS, HPG), ((0, 0), (0, GROUPS - HPG))).reshape(1, GROUPS * GROUPS)
    return jnp.pad(v, ((0, 0), (0, LANES - GROUPS * GROUPS)))


def _gather_heads(v):
    return v[0:1, :GROUPS * GROUPS].reshape(GROUPS, GROUPS)[:, :HPG].reshape(1, SSD_HEADS)


def _rope_tables(t):
    half = HEAD_DIM // 2
    inv = ROPE_THETA ** (-jnp.arange(half, dtype=F32) / half)
    pos = (jnp.arange(t) - PAD_LEAD).astype(F32)
    ang = pos[:, None] * inv[None, :]
    cos, sin = jnp.cos(ang), jnp.sin(ang)
    cos_t = jnp.concatenate([cos, cos, cos, cos], axis=1)
    sin_t = jnp.concatenate([-sin, sin, -sin, sin], axis=1)
    return cos_t, sin_t


def _column_pieces():
    runs = [(0, OB + 2 * GROUPS * D_STATE, 0)]
    o = OB + 2 * GROUPS * D_STATE
    runs += [(o + HPG * g, HPG, ODT + GROUPS * g) for g in range(GROUPS)]
    o += SSD_HEADS
    for width, dst in ((D_ATT, OQ), (D_KV, OK), (D_KV, OV), (D_ATT, OG)):
        runs.append((o, width, dst))
        o += width
    assert o == D_IN
    pieces = []
    for o0, width, dst in runs:
        for j in range(N_SHARD):
            lo, hi = max(o0, W_IN_SHARD * j), min(o0 + width, W_IN_SHARD * (j + 1))
            if lo < hi:
                pieces.append((j, lo - W_IN_SHARD * j, hi - W_IN_SHARD * j, dst + lo - o0))
    return pieces


def _shards_to_re(w_all):
    _, k, _ = w_all.shape
    tr = 256

    def body(x_ref, o_ref):
        o_ref[:, ODT:ODT + DT_SLAB] = jnp.zeros((tr, DT_SLAB), o_ref.dtype)
        for j, c0, c1, d0 in _column_pieces():
            o_ref[:, d0:d0 + c1 - c0] = x_ref[j, :, c0:c1]

    return _pallas(body, name="shards_to_re", grid=(k // tr,),
                   in_specs=[pl.BlockSpec((N_SHARD, tr, W_IN_SHARD), lambda i: (0, i, 0))],
                   out_specs=pl.BlockSpec((tr, N_RE), lambda i: (i, 0)),
                   out_shape=jax.ShapeDtypeStruct((k, N_RE), w_all.dtype), compiler_params=_cp("parallel"))(w_all)


def _pair_add_to_shards(parts, got, pieces, shard_rows, core, name):
    n = parts[0].shape[1]
    hn = n // 2
    tc = 128
    nt = hn // tc
    ns = len(parts)
    starts = [sum(p.shape[0] for p in parts[:s]) for s in range(ns)]
    moves = []
    for j, c0, c1, d0 in pieces:
        for s, p in enumerate(parts):
            lo, hi = max(d0, starts[s]), min(d0 + c1 - c0, starts[s] + p.shape[0])
            if lo < hi:
                moves.append((s, lo - starts[s], j, c0 + lo - d0, hi - lo))
    assert sum(m[4] for m in moves) == N_SHARD * shard_rows

    def body(core_ref, *refs):
        own, theirs, o_ref, acc = refs[:ns], refs[ns:2 * ns], refs[2 * ns], refs[2 * ns + 1]
        for s, r0, j, c0, rows in moves:
            acc[j, c0:c0 + rows, :] = own[s][r0:r0 + rows, :] + theirs[s][r0:r0 + rows, :]
        o_ref[...] = _bf(acc[...])

    return _pallas(
        body, name=name,
        grid_spec=pltpu.PrefetchScalarGridSpec(
            num_scalar_prefetch=1, grid=(nt,),
            in_specs=[pl.BlockSpec((p.shape[0], tc), lambda i, core_ref: (0, core_ref[0] * nt + i)) for p in parts]
            + [pl.BlockSpec((p.shape[0], tc), lambda i, core_ref: (0, i)) for p in parts],
            out_specs=pl.BlockSpec((N_SHARD, shard_rows, tc), lambda i, core_ref: (0, 0, i)),
            scratch_shapes=[pltpu.VMEM((N_SHARD, shard_rows, tc), F32)]),
        out_shape=jax.ShapeDtypeStruct((N_SHARD, shard_rows, hn), BF16),
        compiler_params=_cp("parallel"))(core, *parts, *got)


def _local_step(x, target, meta, norm_pre_w, w_re, conv_w, conv_b, dt_bias, a_log, d_skip, ssd_norm_w, sinks,
                w_out_shard, norm_post_w, place):
    seq = x.shape[0]
    t = PAD_LEAD + N_META + seq
    hpad = jnp.concatenate([jnp.zeros((PAD_LEAD, D_MODEL), F32), meta, x], axis=0)
    dt_bias_l, a_log_l, d_skip_l = _spread_heads(dt_bias), _spread_heads(a_log), _spread_heads(d_skip)
    cos_t, sin_t = _rope_tables(t)
    sink_v = sinks.reshape(Q_HEADS)

    proj, hn, w_out_all = _inproj(hpad, norm_pre_w, w_re, w_out_shard)
    w_out = w_out_all.reshape(D_MIX, D_MODEL)
    xbc = _conv_fwd(proj, conv_w, conv_b)
    dt, acs, acst = _dt_prep(proj, dt_bias_l, a_log_l)
    y_ssd, ymix, states = _ssd_fwd(xbc, proj, dt, acs, acst, d_skip_l, ssd_norm_w)
    qr, kr = _rope(proj, OQ, proj, OK, cos_t, sin_t)
    amix = _attn_fwd(qr, kr, proj, sink_v)
    out = _outproj(ymix, amix, w_out)
    dout, dy_t, loss_blk, g_norm_post = _post_loss(out, x, target, norm_post_w)

    dmix = _nt_matmul(dout, w_out, "dmix")
    g_out_parts = [_tn_matmul(ymix, dout, "gw_out_y"), _tn_matmul(amix, dout, "gw_out_a")]
    ga_out = _reduce_pair(g_out_parts, [(j, 0, W_OUT_SHARD, W_OUT_SHARD * j) for j in range(N_SHARD)], W_OUT_SHARD,
                          place, "gw_out")
    dq_r, dg, dk_r, dv, gs, slabs_out = _attn_bwd(qr, kr, proj, dmix, sink_v, ga_out)
    g_w_out = _reduce_finish(ga_out, slabs_out, place, "gw_out")
    dq, dk = _rope(dq_r, 0, dk_r, 0, cos_t, -sin_t)
    dz, dxs, db, dc, dacs_g, ddt_g, g_ssd_norm, gdsk = _ssd_bwd(dmix, y_ssd, xbc, proj, dt, acs, acst, states,
                                                                d_skip_l, ssd_norm_w)
    draw, ga, gb = _dt_bwd(dacs_g, ddt_g, dt, proj, dt_bias_l, a_log_l)
    dxs_p, gcw0, gcb0 = _conv_bwd(dxs, proj, conv_w, conv_b, 0, "conv_bwd_x")
    db_p, gcw1, gcb1 = _conv_bwd(db, proj, conv_w, conv_b, D_SSD, "conv_bwd_b")
    dc_p, gcw2, gcb2 = _conv_bwd(dc, proj, conv_w, conv_b, D_SSD + GROUPS * D_STATE, "conv_bwd_c")
    tail = jnp.concatenate([dk, _bf(dv), draw, jnp.zeros((t, DT_SLAB - LANES), BF16)], axis=1)
    segs = [dz, dxs_p, db_p, dc_p, dq, dg, tail]
    g_parts = [_tn_matmul(seg, hn, "gw_in_%d" % s) for s, seg in enumerate(segs)]
    ga_in = _reduce_pair(g_parts, _column_pieces(), W_IN_SHARD, place, "gw_in")
    dh, g_norm_pre, slabs_in = _dinproj(segs, w_re, hpad, norm_pre_w, dy_t, ga_in)
    g_w_in = _reduce_finish(ga_in, slabs_in, place, "gw_in")

    gdsk_l = jnp.concatenate([gdsk[g, 0:1, 0:GROUPS] for g in range(GROUPS)], axis=1)
    gdsk_l = jnp.pad(gdsk_l, ((0, 0), (0, LANES - GROUPS * GROUPS)))
    grads = dict(
        meta_tokens=dh[PAD_LEAD:ROW0], norm_pre_w=g_norm_pre, w_in=g_w_in,
        conv_w=jnp.concatenate([gcw0, gcw1, gcw2], axis=1), conv_b=jnp.concatenate([gcb0, gcb1, gcb2], axis=1),
        dt_bias=_gather_heads(gb), a_log=_gather_heads(ga), d_skip=_gather_heads(gdsk_l), ssd_norm_w=g_ssd_norm,
        attn_sinks=gs[0:1, :Q_HEADS], w_out=g_w_out, norm_post_w=g_norm_post)
    return loss_blk[0, 0], dh[ROW0:], grads


ANY = pl.BlockSpec(memory_space=pl.ANY)
MESH = pl.DeviceIdType.MESH
GATHER_CHUNKS = 4
PAIR_CHUNKS = 8
JOIN_CHUNKS = 8


def _rcopy(src, dst, ssem, rsem, dev):
    return pltpu.make_async_remote_copy(src_ref=src, dst_ref=dst, send_sem=ssem, recv_sem=rsem, device_id=dev,
                                        device_id_type=MESH)


def _place():
    x, y, c = lax.axis_index("x"), lax.axis_index("y"), lax.axis_index("c")
    chips = [(1 - x, y), (x, 1 - y), (1 - x, 1 - y)]
    return x, y, c, chips


def _gather_plan(x_ref, out_ref, send_sems, recv_sems, local_sems, hr, kc):
    ch = hr // kc
    assert ch * kc == hr and ch % 16 == 0
    x, y, c, chips = _place()
    me = 2 * x + y
    sibling = (x, y, 1 - c)

    def piece(chip, hc, k):
        return out_ref.at[chip, pl.ds(hc * hr + k * ch, ch), :]

    def local():
        return [pltpu.make_async_copy(x_ref.at[pl.ds(k * ch, ch), :], out_ref.at[me, pl.ds(k * ch, ch), :],
                                      local_sems.at[k]) for k in range(2 * kc)]

    def first():
        return [_rcopy(x_ref.at[pl.ds(c * hr + k * ch, ch), :], piece(me, c, k), send_sems.at[j * kc + k],
                       recv_sems.at[j * kc + k], (*chip, c)) for j, chip in enumerate(chips) for k in range(kc)]

    def passed(hc):
        return [_rcopy(piece(2 * chip[0] + chip[1], hc, k), piece(2 * chip[0] + chip[1], hc, k),
                       send_sems.at[(3 + j) * kc + k], recv_sems.at[(3 + j) * kc + k], sibling)
                for j, chip in enumerate(chips) for k in range(kc)]

    def arrivals():
        return [_rcopy(piece(2 * chip[0] + chip[1], c, k), piece(2 * chip[0] + chip[1], c, k), send_sems.at[j * kc + k],
                       recv_sems.at[j * kc + k], (*chip, c)) for j, chip in enumerate(chips) for k in range(kc)]

    def start():
        for cp in local() + first():
            cp.start()

    def forward():
        for arrived in arrivals():
            arrived.wait_recv()
        for fw in passed(c):
            fw.start()

    def finish():
        for cp in passed(1 - c):
            cp.wait_recv()
        for cp in first() + passed(c):
            cp.wait_send()
        for cp in local():
            cp.wait()

    return start, forward, finish


def _gather_shards(shard, name, kc):
    r, n = shard.shape

    def body(x_ref, out_ref, send_sems, recv_sems, local_sems):
        for phase in _gather_plan(x_ref, out_ref, send_sems, recv_sems, local_sems, r // 2, kc):
            phase()

    return _pallas(
        body, name=name, in_specs=[ANY], out_specs=ANY,
        out_shape=jax.ShapeDtypeStruct((N_SHARD, r, n), shard.dtype),
        scratch_shapes=[pltpu.SemaphoreType.DMA((6 * kc,)), pltpu.SemaphoreType.DMA((6 * kc,)),
                        pltpu.SemaphoreType.DMA((2 * kc,))])(shard)


def _pair_send(parts, name):
    n = parts[0].shape[1]
    hn = n // 2
    kc = PAIR_CHUNKS
    cw = hn // kc
    assert cw * kc == hn and cw % LANES == 0
    ns = len(parts)

    def body(*refs):
        srcs, dsts, send_sems, recv_sems = refs[:ns], refs[ns:2 * ns], refs[2 * ns], refs[2 * ns + 1]
        x, y, c, _ = _place()
        cps = [_rcopy(srcs[s].at[:, pl.ds((1 - c) * hn + k * cw, cw)], dsts[s].at[:, pl.ds(k * cw, cw)],
                      send_sems.at[s * kc + k], recv_sems.at[s * kc + k], (x, y, 1 - c))
               for s in range(ns) for k in range(kc)]
        for cp in cps:
            cp.start()
        for cp in cps:
            cp.wait()

    return _pallas(
        body, name=name, in_specs=[ANY] * ns, out_specs=[ANY] * ns,
        out_shape=[jax.ShapeDtypeStruct((p.shape[0], hn), F32) for p in parts],
        scratch_shapes=[pltpu.SemaphoreType.DMA((ns * kc,)), pltpu.SemaphoreType.DMA((ns * kc,))])(*parts)


REDUCE_TILE = 256


def _exchange_copies(g_ref, got_ref, send_sems, recv_sems):
    hn = g_ref.shape[2]
    kc = GATHER_CHUNKS
    cw = hn // kc
    assert cw * kc == hn and cw % LANES == 0
    x, y, c, chips = _place()
    return [_rcopy(g_ref.at[2 * chip[0] + chip[1], :, pl.ds(k * cw, cw)], got_ref.at[j, :, pl.ds(k * cw, cw)],
                   send_sems.at[j * kc + k], recv_sems.at[j * kc + k], (*chip, c))
            for j, chip in enumerate(chips) for k in range(kc)]


def _exchange_scratch():
    return [pltpu.SemaphoreType.DMA((3 * GATHER_CHUNKS,)), pltpu.SemaphoreType.DMA((3 * GATHER_CHUNKS,))]


def _exchange_shape(ga):
    return jax.ShapeDtypeStruct((3,) + ga.shape[1:], ga.dtype)


def _chip_sum(ga, got, place, name):
    _, r, hn = ga.shape
    tc = REDUCE_TILE
    nt = hn // tc

    def body(place_ref, own_ref, got_ref, o_ref):
        acc = own_ref[0].astype(F32)
        for j in range(3):
            acc = acc + got_ref[j].astype(F32)
        o_ref[...] = acc

    return _pallas(
        body, name=name,
        grid_spec=pltpu.PrefetchScalarGridSpec(
            num_scalar_prefetch=1, grid=(nt,),
            in_specs=[pl.BlockSpec((1, r, tc), lambda i, place_ref: (place_ref[0], 0, i)),
                      pl.BlockSpec((3, r, tc), lambda i, place_ref: (0, 0, i))],
            out_specs=pl.BlockSpec((r, tc), lambda i, place_ref: (0, place_ref[1] * nt + i))),
        out_shape=jax.ShapeDtypeStruct((r, 2 * hn), F32), compiler_params=_cp("parallel"))(place, ga, got)


def _pair_join(buf, name):
    r, n = buf.shape
    hn = n // 2
    kc = JOIN_CHUNKS
    cw = hn // kc
    assert cw * kc == hn and cw % LANES == 0

    def body(in_ref, out_ref, send_sems, recv_sems):
        x, y, c, _ = _place()
        cps = [_rcopy(out_ref.at[:, pl.ds(c * hn + k * cw, cw)], out_ref.at[:, pl.ds(c * hn + k * cw, cw)],
                      send_sems.at[k], recv_sems.at[k], (x, y, 1 - c)) for k in range(kc)]
        for cp in cps:
            cp.start()
        for k in range(kc):
            cols = out_ref.at[:, pl.ds((1 - c) * hn + k * cw, cw)]
            _rcopy(cols, cols, send_sems.at[k], recv_sems.at[k], (x, y, 1 - c)).wait_recv()
        for cp in cps:
            cp.wait_send()

    return _pallas(
        body, name=name, in_specs=[ANY], out_specs=ANY, out_shape=jax.ShapeDtypeStruct((r, n), F32),
        input_output_aliases={0: 0},
        scratch_shapes=[pltpu.SemaphoreType.DMA((kc,)), pltpu.SemaphoreType.DMA((kc,))])(buf)


def _reduce_pair(parts, pieces, shard_rows, place, tag):
    got = _pair_send(parts, tag + "_pair_send")
    return _pair_add_to_shards(parts, got, pieces, shard_rows, place[1:2], tag + "_pair_add")


def _reduce_finish(ga, slabs, place, tag):
    return _pair_join(_chip_sum(ga, slabs, place, tag + "_chip_sum"), tag + "_pair_join")


def _allreduce_small(p, name):
    rows, n = p.shape
    ndev = 8

    def body(p_ref, out_ref, slots, send_sems, recv_sems):
        x, y, c, _ = _place()
        my = 4 * x + 2 * y + c
        slots[my] = p_ref[...]
        cps = []
        for k in range(1, ndev):
            kx, ky, kc = (k >> 2) & 1, (k >> 1) & 1, k & 1
            peer = (x ^ kx, y ^ ky, c ^ kc)
            cp = _rcopy(p_ref, slots.at[my], send_sems.at[k - 1], recv_sems.at[k - 1], peer)
            cp.start()
            cps.append(cp)
        for k in range(1, ndev):
            _rcopy(p_ref, slots.at[my ^ k], send_sems.at[k - 1], recv_sems.at[k - 1], (x, y, c)).wait_recv()
        for cp in cps:
            cp.wait_send()
        acc = slots[0]
        for j in range(1, ndev):
            acc = acc + slots[j]
        out_ref[...] = acc

    vm = pl.BlockSpec(memory_space=pltpu.VMEM)
    return _pallas(
        body, name=name, in_specs=[vm], out_specs=vm, out_shape=jax.ShapeDtypeStruct((rows, n), F32),
        scratch_shapes=[pltpu.VMEM((ndev, rows, n), F32), pltpu.SemaphoreType.DMA((ndev - 1,)),
                        pltpu.SemaphoreType.DMA((ndev - 1,))])(p)


def _adamw(w, g, m, v, name):
    r, n = w.shape
    tr = _tile(r, 256, 8)
    c1 = 1.0 / (1.0 - ADAM_B1 ** ADAM_STEP)
    c2 = 1.0 / (1.0 - ADAM_B2 ** ADAM_STEP)

    def body(w_ref, g_ref, m_ref, v_ref, d_ref, mo_ref, vo_ref):
        gv = g_ref[...]
        mn = ADAM_B1 * m_ref[...] + (1.0 - ADAM_B1) * gv
        vn = ADAM_B2 * v_ref[...] + (1.0 - ADAM_B2) * (gv * gv)
        d_ref[...] = -ADAM_LR * ((mn * c1) / (jnp.sqrt(vn * c2) + ADAM_EPS) + ADAM_WD * w_ref[...])
        mo_ref[...] = mn
        vo_ref[...] = vn

    spec = pl.BlockSpec((tr, n), lambda i: (i, 0))
    shp = jax.ShapeDtypeStruct((r, n), F32)
    return _pallas(body, name=name, grid=(r // tr,), in_specs=[spec] * 4, out_specs=[spec] * 3, out_shape=[shp] * 3,
                   compiler_params=_cp("parallel"))(w, g, m, v)


PACK_W = 1024
SMALL_REPL = ("norm_pre_w", "conv_b", "ssd_norm_w", "norm_post_w")
SMALL_HEAD = ("dt_bias", "a_log", "d_skip", "attn_sinks")


def _rows(a):
    return a.reshape(-1, PACK_W)


def _head_row(vals, extra=None):
    parts = [vals[n].reshape(1, -1) for n in SMALL_HEAD]
    if extra is not None:
        parts.append(extra.reshape(1, 1))
    row = jnp.concatenate(parts, axis=1)
    return jnp.pad(row, ((0, 0), (0, PACK_W - row.shape[1])))


def _pad_rows(a, rows):
    return jnp.pad(a, ((0, rows - a.shape[0]), (0, 0)))


def _pack_repl(vals, extra=None):
    body = jnp.concatenate([_rows(vals[n]) for n in SMALL_REPL] + [_head_row(vals, extra)], axis=0)
    return _pad_rows(body, 16)


def _unpack_repl(buf):
    out, r = {}, 0
    for n, k in zip(SMALL_REPL, (2, 4, 2, 2)):
        out[n] = buf[r:r + k].reshape(1, k * PACK_W)
        r += k
    col = 0
    for n, k in zip(SMALL_HEAD, (32, 32, 32, 16)):
        out[n] = buf[r:r + 1, col:col + k]
        col += k
    return out, buf[r, col]


def kernel(x, meta_tokens, norm_pre_w, w_in, conv_w, conv_b, dt_bias, a_log, d_skip, ssd_norm_w, attn_sinks, w_out, norm_post_w, loss_target, m_meta_tokens, m_norm_pre_w, m_w_in, m_conv_w, m_conv_b, m_dt_bias, m_a_log, m_d_skip, m_ssd_norm_w, m_attn_sinks, m_w_out, m_norm_post_w, v_meta_tokens, v_norm_pre_w, v_w_in, v_conv_w, v_conv_b, v_dt_bias, v_a_log, v_d_skip, v_ssd_norm_w, v_attn_sinks, v_w_out, v_norm_post_w):
    names = ("meta_tokens", "norm_pre_w", "w_in", "conv_w", "conv_b", "dt_bias", "a_log", "d_skip", "ssd_norm_w",
             "attn_sinks", "w_out", "norm_post_w")
    w = dict(zip(names, (meta_tokens, norm_pre_w, w_in, conv_w, conv_b, dt_bias, a_log, d_skip, ssd_norm_w, attn_sinks,
                         w_out, norm_post_w)))
    m = dict(zip(names, (m_meta_tokens, m_norm_pre_w, m_w_in, m_conv_w, m_conv_b, m_dt_bias, m_a_log, m_d_skip,
                         m_ssd_norm_w, m_attn_sinks, m_w_out, m_norm_post_w)))
    v = dict(zip(names, (v_meta_tokens, v_norm_pre_w, v_w_in, v_conv_w, v_conv_b, v_dt_bias, v_a_log, v_d_skip,
                         v_ssd_norm_w, v_attn_sinks, v_w_out, v_norm_post_w)))
    cx, cy, cc = lax.axis_index("x"), lax.axis_index("y"), lax.axis_index("c")
    chip = 2 * cx + cy
    meta_cols = D_MODEL // N_SHARD
    conv_cols = D_CONV // N_SHARD

    place = jnp.stack([chip, cc]).astype(jnp.int32)
    w_re = _shards_to_re(_gather_shards(_bf(w_in[0]), "gather_w_in", 2 * GATHER_CHUNKS))
    conv_z = lax.dynamic_update_slice(jnp.zeros((CONV_WIDTH, D_CONV), F32), conv_w[0], (0, chip * conv_cols))
    meta_z = lax.dynamic_update_slice(jnp.zeros((N_META, D_MODEL), F32), meta_tokens, (0, chip * meta_cols))
    small = jnp.concatenate([_rows(conv_z), _rows(meta_z)], axis=0)
    small = _allreduce_small(jnp.where(cc == 0, small, 0.0), "gather_small")
    conv_full = small[0:16].reshape(CONV_WIDTH, D_CONV)
    meta_full = small[16:48].reshape(N_META, D_MODEL)

    loss_dev, grad_x, g = _local_step(x[0], loss_target[0], meta_full, norm_pre_w, w_re, conv_full, conv_b, dt_bias,
                                      a_log, d_skip, ssd_norm_w, attn_sinks, _bf(w_out[0]), norm_post_w, place)
    g_w_in, g_w_out = g["w_in"], g["w_out"]

    packed = jnp.concatenate([_rows(g["conv_w"]), _rows(g["meta_tokens"]), _pack_repl(g, loss_dev)], axis=0)
    red = _allreduce_small(packed, "reduce_small")
    g_conv_full = red[0:16].reshape(CONV_WIDTH, D_CONV)
    g_meta_full = red[16:48].reshape(N_META, D_MODEL)
    g_small, loss = _unpack_repl(red[48:64])
    grads = dict(g_small)
    grads["w_in"] = g_w_in
    grads["w_out"] = g_w_out
    grads["conv_w"] = lax.dynamic_slice(g_conv_full, (0, chip * conv_cols), (CONV_WIDTH, conv_cols))
    grads["meta_tokens"] = lax.dynamic_slice(g_meta_full, (0, chip * meta_cols), (N_META, meta_cols))

    upd = {}
    upd["w_in"] = [jnp.swapaxes(a, 0, 1) for a in _adamw(jnp.swapaxes(w_in[0], 0, 1), g_w_in, jnp.swapaxes(m_w_in[0], 0, 1),
                                                         jnp.swapaxes(v_w_in[0], 0, 1), "adamw_w_in")]
    grads["w_in"] = jnp.swapaxes(g_w_in, 0, 1)
    upd["w_out"] = _adamw(w_out[0], g_w_out, m_w_out[0], v_w_out[0], "adamw_w_out")

    def pack_small(vals, conv, meta):
        return jnp.concatenate([_pad_rows(conv.reshape(CONV_WIDTH, conv_cols), 8), _rows(meta), _pack_repl(vals)], axis=0)

    sm = _adamw(pack_small(w, w["conv_w"], w["meta_tokens"]), pack_small(grads, grads["conv_w"], grads["meta_tokens"]),
                pack_small(m, m["conv_w"], m["meta_tokens"]), pack_small(v, v["conv_w"], v["meta_tokens"]),
                "adamw_small")
    for n in names:
        if n not in ("w_in", "w_out"):
            upd[n] = [None, None, None]
    for k, buf in enumerate(sm):
        upd["conv_w"][k] = buf[0:CONV_WIDTH]
        upd["meta_tokens"][k] = buf[8:16].reshape(N_META, meta_cols)
        rest, _ = _unpack_repl(buf[16:32])
        for n in SMALL_REPL + SMALL_HEAD:
            upd[n][k] = rest[n]

    def shaped(n, a):
        return a.reshape(w[n].shape)

    outs = [loss, grad_x[None]]
    outs += [shaped(n, grads[n]) for n in names]
    for k in range(3):
        outs += [shaped(n, upd[n][k]) for n in names]
    return tuple(outs)
```

```python
import functools

import jax
import jax.numpy as jnp
from jax import lax
from jax.experimental import pallas as pl
from jax.experimental.pallas import tpu as pltpu

F32 = jnp.float32
BF16 = jnp.bfloat16

D_MODEL = 2048
CHUNK = 64
N_META = 16
PAD_LEAD = CHUNK - N_META
ROW0 = PAD_LEAD + N_META
EPS = 1e-6
SSD_HEADS = 32
HEAD_DIM = 64
GROUPS = 8
HPG = SSD_HEADS // GROUPS
D_STATE = 128
D_SSD = 2048
GROUP_W = D_SSD // GROUPS
CONV_WIDTH = 4
D_CONV = 4096
Q_HEADS = 16
KV_HEADS = 4
REP = Q_HEADS // KV_HEADS
D_ATT = 1024
D_KV = 256
BAND_CHUNKS = 3
ROPE_THETA = 10000.0
D_MIX = D_SSD + D_ATT
D_IN = 8736
N_SHARD = 4
W_IN_SHARD = D_IN // N_SHARD
W_OUT_SHARD = D_MIX // N_SHARD

OZ, OXS, OB, OC, OQ, OG, OK, OV, ODT = 0, 2048, 4096, 5120, 6144, 7168, 8192, 8448, 8704
DT_SLAB = 512
N_RE = ODT + DT_SLAB
LANES = 128

ADAM_LR, ADAM_B1, ADAM_B2, ADAM_EPS, ADAM_WD, ADAM_STEP = 0.001, 0.9, 0.999, 1e-08, 0.01, 10

SSD_FWD_GROUPS_PER_STEP = 4
SSD_BWD_GROUPS_PER_STEP = 8
SEG_TILE = 1024
VMEM_LIMIT = 52 * 1024 * 1024
NEG = -1e30
HI = lax.Precision.HIGHEST


def _pallas(body, **kw):
    return pl.pallas_call(body, **kw)


def _cp(*sem):
    return pltpu.CompilerParams(dimension_semantics=sem, vmem_limit_bytes=VMEM_LIMIT)


def _tile(n, cap, mult=16):
    best = None
    for d in range(mult, min(n, cap) + 1, mult):
        if n % d == 0:
            best = d
    assert best is not None, (n, cap)
    return best


def _nt(a, b):
    return lax.dot_general(a, b, (((1,), (1,)), ((), ())), preferred_element_type=F32)


def _tn(a, b):
    return lax.dot_general(a, b, (((0,), (0,)), ((), ())), preferred_element_type=F32)


def _mm(a, b):
    return jnp.dot(a, b, preferred_element_type=F32)


def _sigmoid(x):
    return 1.0 / (1.0 + jnp.exp(-x))


def _bf(x):
    return x.astype(BF16)


def _inproj(hpad, norm_w, w_re, w_out_shard):
    t, d = hpad.shape
    n = w_re.shape[1]
    tm, tn = _tile(t, 832), 1024
    ni, nj = t // tm, n // tn
    r_out, n_out = w_out_shard.shape
    kc = GATHER_CHUNKS

    def body(h_ref, nw_ref, w_ref, ws_ref, proj_ref, hn_ref, wall_ref, hn_s, send_sems, recv_sems, local_sems):
        i, j = pl.program_id(0), pl.program_id(1)
        start, forward, finish = _gather_plan(ws_ref, wall_ref, send_sems, recv_sems, local_sems, r_out // 2, kc)
        pl.when((i == 0) & (j == 0))(start)
        pl.when((i == ni // 2) & (j == 0))(forward)

        @pl.when(j == 0)
        def _():
            h = h_ref[...]
            ms = jnp.mean(h * h, axis=-1, keepdims=True)
            hn = _bf(h * lax.rsqrt(ms + EPS) * nw_ref[...])
            hn_s[...] = hn
            hn_ref[...] = hn
        proj_ref[...] = _mm(hn_s[...], w_ref[...])
        pl.when((i == ni - 1) & (j == nj - 1))(finish)

    return _pallas(
        body, name="inproj", grid=(ni, nj),
        in_specs=[pl.BlockSpec((tm, d), lambda i, j: (i, 0)), pl.BlockSpec((1, d), lambda i, j: (0, 0)),
                  pl.BlockSpec((d, tn), lambda i, j: (0, j)), ANY],
        out_specs=[pl.BlockSpec((tm, tn), lambda i, j: (i, j)), pl.BlockSpec((tm, d), lambda i, j: (i, 0)), ANY],
        out_shape=[jax.ShapeDtypeStruct((t, n), F32), jax.ShapeDtypeStruct((t, d), BF16),
                   jax.ShapeDtypeStruct((N_SHARD, r_out, n_out), w_out_shard.dtype)],
        scratch_shapes=[pltpu.VMEM((tm, d), BF16), pltpu.SemaphoreType.DMA((6 * kc,)), pltpu.SemaphoreType.DMA((6 * kc,)),
                        pltpu.SemaphoreType.DMA((2 * kc,))],
        compiler_params=_cp("arbitrary", "arbitrary"))(hpad, norm_w, w_re, w_out_shard)


def _conv_fwd(proj, conv_w, conv_b):
    t = proj.shape[0]
    tc = 256
    off = OXS // tc

    def body(x_ref, w_ref, b_ref, o_ref):
        x = x_ref[...]
        w = w_ref[...]
        row = lax.broadcasted_iota(jnp.int32, (t, tc), 0)
        u = b_ref[...] + w[3:4, :] * x
        for k in range(1, CONV_WIDTH):
            u = u + w[3 - k:4 - k, :] * jnp.where(row >= k, pltpu.roll(x, k, 0), 0.0)
        h = 0.5 * u
        o_ref[...] = h + h * jnp.tanh(h)

    return _pallas(
        body, name="conv_fwd", grid=(D_CONV // tc,),
        in_specs=[pl.BlockSpec((t, tc), lambda j: (0, j + off)), pl.BlockSpec((CONV_WIDTH, tc), lambda j: (0, j)),
                  pl.BlockSpec((1, tc), lambda j: (0, j))],
        out_specs=pl.BlockSpec((t, tc), lambda j: (0, j)),
        out_shape=jax.ShapeDtypeStruct((t, D_CONV), F32),
        compiler_params=_cp("parallel"))(proj, conv_w, conv_b)


def _softplus(u):
    e = jnp.exp(-jnp.abs(u))
    w = 1.0 + e
    l1p = jnp.where(w == 1.0, e, jnp.log(w) * (e / jnp.where(w == 1.0, 1.0, w - 1.0)))
    return jnp.maximum(u, 0.0) + l1p


def _chunks_per_step(nc):
    return max(d for d in range(1, 14) if nc % d == 0)


def _dt_prep(proj, dt_bias_l, a_log_l):
    t = proj.shape[0]
    nc = t // CHUNK
    q = CHUNK
    cps = _chunks_per_step(nc)
    rows = cps * q

    def body(raw_ref, bias_ref, alog_ref, dt_ref, acs_ref, acst_ref):
        ri = lax.broadcasted_iota(jnp.int32, (q, q), 0)
        ci = lax.broadcasted_iota(jnp.int32, (q, q), 1)
        tri = (ri >= ci).astype(F32)
        neg_a = -jnp.exp(alog_ref[...])
        for k in range(cps):
            rk = slice(q * k, q * (k + 1))
            sp = _softplus(raw_ref[rk, :] + bias_ref[...])
            row = pl.program_id(0) * rows + q * k + lax.broadcasted_iota(jnp.int32, (q, LANES), 0)
            dt = jnp.where(row >= PAD_LEAD, sp, 0.0)
            acs = jnp.dot(tri, dt * neg_a, preferred_element_type=F32, precision=HI)
            dt_ref[rk, :] = dt
            acs_ref[rk, :] = acs
            acst_ref[k] = acs.T

    return _pallas(
        body, name="dt_prep", grid=(nc // cps,),
        in_specs=[pl.BlockSpec((rows, LANES), lambda c: (c, ODT // LANES)), pl.BlockSpec((1, LANES), lambda c: (0, 0)),
                  pl.BlockSpec((1, LANES), lambda c: (0, 0))],
        out_specs=[pl.BlockSpec((rows, LANES), lambda c: (c, 0)), pl.BlockSpec((rows, LANES), lambda c: (c, 0)),
                   pl.BlockSpec((cps, LANES, q), lambda c: (c, 0, 0))],
        out_shape=[jax.ShapeDtypeStruct((t, LANES), F32), jax.ShapeDtypeStruct((t, LANES), F32),
                   jax.ShapeDtypeStruct((nc, LANES, q), F32)],
        compiler_params=_cp("parallel"))(proj, dt_bias_l, a_log_l)


def _head_cols(blk, idx):
    lane = lax.broadcasted_iota(jnp.int32, blk.shape, 1)
    return jnp.sum(jnp.where(lane == idx, blk, 0.0), axis=1, keepdims=True)


class _HeadVals:
    pass


def _lane_head(shape):
    return lax.broadcasted_iota(jnp.int32, shape, len(shape) - 1) >> 6


def _group_heads(g, gi, dtb, acsb, acst_ref, dskb):
    q = dtb.shape[0]
    hv = _HeadVals()
    lh = _lane_head((1, GROUP_W))
    hv.dt = jnp.zeros((q, GROUP_W), F32)
    hv.acs = jnp.zeros((q, GROUP_W), F32)
    hv.acs_last = jnp.zeros((1, GROUP_W), F32)
    hv.dsk = jnp.zeros((1, GROUP_W), F32)
    rows = []
    for r in range(HPG):
        idx = GROUPS * g + r
        sel = lh == r
        acs_r = acst_ref[0, GROUPS * gi + r:GROUPS * gi + r + 1, :]
        rows.append(acs_r)
        hv.dt = jnp.where(sel, _head_cols(dtb, idx), hv.dt)
        hv.acs = jnp.where(sel, _head_cols(acsb, idx), hv.acs)
        hv.acs_last = jnp.where(sel, acs_r[:, q - 1:q], hv.acs_last)
        hv.dsk = jnp.where(sel, _head_cols(dskb, idx), hv.dsk)
    hv.acs_row = jnp.concatenate(rows, axis=1)
    return hv


def _head_tri(q, lower):
    ri = lax.broadcasted_iota(jnp.int32, (q, GROUP_W), 0)
    li = lax.broadcasted_iota(jnp.int32, (q, GROUP_W), 1) & (HEAD_DIM - 1)
    return ri >= li if lower else ri <= li


def _block_diag(v):
    rb = lax.broadcasted_iota(jnp.int32, (GROUP_W, GROUP_W), 0) >> 6
    cb = lax.broadcasted_iota(jnp.int32, (GROUP_W, GROUP_W), 1) >> 6
    return jnp.where(rb == cb, jnp.concatenate([v] * HPG, axis=0), jnp.zeros((), v.dtype))


def _head_sums(v, r):
    return jnp.sum(jnp.where(_lane_head((1, GROUP_W)) == r, v, 0.0), axis=1, keepdims=True)


def _ssd_fwd(xbc, proj, dt, acs, acst, d_skip_l, ssd_norm_w):
    t = xbc.shape[0]
    q = CHUNK
    nc = t // q

    gps = SSD_FWD_GROUPS_PER_STEP
    gw, sw = gps * GROUP_W, gps * D_STATE

    def body(xs_ref, b_ref, c_ref, dt_ref, acs_ref, acst_ref, z_ref, dsk_ref, nw_ref,
             y_ref, ymix_ref, st_ref, state):
        @pl.when(pl.program_id(1) == 0)
        def _():
            state[...] = jnp.zeros_like(state)

        for gi in range(gps):
            g = gps * pl.program_id(0) + gi
            cols = slice(GROUP_W * gi, GROUP_W * (gi + 1))
            x = xs_ref[:, cols]
            bmb = _bf(b_ref[:, D_STATE * gi:D_STATE * (gi + 1)])
            cmb = _bf(c_ref[:, D_STATE * gi:D_STATE * (gi + 1)])
            hv = _group_heads(g, gi, dt_ref[...], acs_ref[...], acst_ref, dsk_ref[...])
            decay = jnp.exp(jnp.where(_head_tri(q, True), hv.acs - hv.acs_row, NEG))
            m_all = _bf(_nt(cmb, jnp.concatenate([bmb] * HPG, axis=0)) * decay)
            xdt = x * hv.dt
            s_prev = state[gi]
            st_ref[0, gi] = s_prev
            y = (_mm(m_all, _block_diag(_bf(xdt))) + _mm(cmb, _bf(s_prev)) * jnp.exp(hv.acs) + hv.dsk * x)
            state[gi] = jnp.exp(hv.acs_last) * s_prev + _tn(bmb, _bf(xdt * jnp.exp(hv.acs_last - hv.acs)))
            y_ref[:, cols] = y
            z = z_ref[:, cols]
            yg = y * (z * _sigmoid(z))
            ms = jnp.mean(yg * yg, axis=-1, keepdims=True)
            ymix_ref[:, cols] = _bf(yg * lax.rsqrt(ms + EPS) * nw_ref[:, cols])

    return _pallas(
        body, name="ssd_fwd", grid=(GROUPS // gps, nc),
        in_specs=[pl.BlockSpec((q, gw), lambda g, c: (c, g)),
                  pl.BlockSpec((q, sw), lambda g, c: (c, D_SSD // sw + g)),
                  pl.BlockSpec((q, sw), lambda g, c: (c, (D_SSD + GROUPS * D_STATE) // sw + g)),
                  pl.BlockSpec((q, LANES), lambda g, c: (c, 0)), pl.BlockSpec((q, LANES), lambda g, c: (c, 0)),
                  pl.BlockSpec((1, gps * GROUPS, q), lambda g, c: (c, g, 0)),
                  pl.BlockSpec((q, gw), lambda g, c: (c, g)),
                  pl.BlockSpec((1, LANES), lambda g, c: (0, 0)), pl.BlockSpec((1, gw), lambda g, c: (0, g))],
        out_specs=[pl.BlockSpec((q, gw), lambda g, c: (c, g)), pl.BlockSpec((q, gw), lambda g, c: (c, g)),
                   pl.BlockSpec((1, gps, D_STATE, GROUP_W), lambda g, c: (c, g, 0, 0))],
        out_shape=[jax.ShapeDtypeStruct((t, D_SSD), F32), jax.ShapeDtypeStruct((t, D_SSD), BF16),
                   jax.ShapeDtypeStruct((nc, GROUPS, D_STATE, GROUP_W), F32)],
        scratch_shapes=[pltpu.VMEM((gps, D_STATE, GROUP_W), F32)],
        compiler_params=_cp("parallel", "arbitrary"))(xbc, xbc, xbc, dt, acs, acst, proj, d_skip_l, ssd_norm_w)


def _swap_halves(v):
    lane = lax.broadcasted_iota(jnp.int32, v.shape, 1)
    return jnp.where((lane & (HEAD_DIM - 1)) < HEAD_DIM // 2, pltpu.roll(v, LANES - HEAD_DIM // 2, 1),
                     pltpu.roll(v, HEAD_DIM // 2, 1))


def _rope(qsrc, q_off, ksrc, k_off, cos_t, sin_t):
    t = qsrc.shape[0]
    tr = _tile(t, 832)

    def body(q_ref, k_ref, cos_ref, sin_ref, qo_ref, ko_ref):
        cs = cos_ref[...]
        sn = sin_ref[...]
        for src, dst, width in ((q_ref, qo_ref, D_ATT), (k_ref, ko_ref, D_KV)):
            for s in range(width // LANES):
                v = src[:, LANES * s:LANES * (s + 1)].astype(F32)
                dst[:, LANES * s:LANES * (s + 1)] = _bf(v * cs + _swap_halves(v) * sn)

    return _pallas(
        body, name="rope", grid=(t // tr,),
        in_specs=[pl.BlockSpec((tr, D_ATT), lambda i: (i, q_off // D_ATT)),
                  pl.BlockSpec((tr, D_KV), lambda i: (i, k_off // D_KV)),
                  pl.BlockSpec((tr, LANES), lambda i: (i, 0)), pl.BlockSpec((tr, LANES), lambda i: (i, 0))],
        out_specs=[pl.BlockSpec((tr, D_ATT), lambda i: (i, 0)), pl.BlockSpec((tr, D_KV), lambda i: (i, 0))],
        out_shape=[jax.ShapeDtypeStruct((t, D_ATT), BF16), jax.ShapeDtypeStruct((t, D_KV), BF16)],
        compiler_params=_cp("parallel"))(qsrc, ksrc, cos_t, sin_t)


def _attn_chunks_per_step(nc):
    return max(d for d in range(1, 6) if nc % d == 0)


def _band(ref, c):
    return [ref[pl.ds(pl.multiple_of(jnp.maximum(c - j, 0) * CHUNK, CHUNK), CHUNK), :] for j in (2, 1, 0)]


def _attn_probs(qh, kb, sink_col, valid):
    s = _nt(qh, kb) * (HEAD_DIM ** -0.5)
    s = jnp.where(valid, s, NEG)
    m = jnp.maximum(jnp.max(s, axis=1, keepdims=True), sink_col)
    p = jnp.exp(s - m)
    psink = jnp.exp(sink_col - m)
    inv = 1.0 / (jnp.sum(p, axis=1, keepdims=True) + psink)
    return p * inv, psink * inv


def _attn_operands(c, q, k_refs, v_refs, sink_ref, h):
    qh = jnp.concatenate([q[:, HEAD_DIM * (REP * h + r):HEAD_DIM * (REP * h + r + 1)] for r in range(REP)], axis=0)
    kb = jnp.concatenate([k[:, HEAD_DIM * h:HEAD_DIM * (h + 1)] for k in k_refs], axis=0)
    vb = jnp.concatenate([_bf(v[:, HEAD_DIM * h:HEAD_DIM * (h + 1)]) for v in v_refs], axis=0)
    rows = lax.broadcasted_iota(jnp.int32, (REP * CHUNK, 1), 0) >> 6
    sink_col = jnp.zeros((REP * CHUNK, 1), F32)
    for r in range(REP):
        sink_col = jnp.where(rows == r, sink_ref[REP * h + r], sink_col)
    key_abs = (c - (BAND_CHUNKS - 1)) * CHUNK + lax.broadcasted_iota(jnp.int32, (1, BAND_CHUNKS * CHUNK), 1)
    return qh, kb, vb, sink_col, key_abs >= PAD_LEAD


def _attn_fwd(qr, kr, proj, sinks):
    t = qr.shape[0]
    nc = t // CHUNK
    cps = _attn_chunks_per_step(nc)
    rows = cps * CHUNK

    def body(q_ref, k_ref, v_ref, g_ref, sink_ref, o_ref):
        for j in range(cps):
            c = pl.program_id(0) * cps + j
            rj = slice(CHUNK * j, CHUNK * (j + 1))
            ks, vs = _band(k_ref, c), _band(v_ref, c)
            q = q_ref[rj, :]
            outs = []
            for h in range(KV_HEADS):
                qh, kb, vb, sink_col, valid = _attn_operands(c, q, ks, vs, sink_ref, h)
                p, _ = _attn_probs(qh, kb, sink_col, valid)
                o = _mm(_bf(p), vb)
                outs += [o[CHUNK * r:CHUNK * (r + 1)] for r in range(REP)]
            att = jnp.concatenate(outs, axis=1)
            gate = g_ref[rj, :]
            o_ref[rj, :] = _bf(att * (gate * _sigmoid(gate)))

    return _pallas(
        body, name="attn_fwd", grid=(nc // cps,),
        in_specs=[pl.BlockSpec((rows, D_ATT), lambda i: (i, 0)), pl.BlockSpec((t, D_KV), lambda i: (0, 0)),
                  pl.BlockSpec((t, D_KV), lambda i: (0, OV // D_KV)),
                  pl.BlockSpec((rows, D_ATT), lambda i: (i, OG // D_ATT)), pl.BlockSpec(memory_space=pltpu.SMEM)],
        out_specs=pl.BlockSpec((rows, D_ATT), lambda i: (i, 0)),
        out_shape=jax.ShapeDtypeStruct((t, D_ATT), BF16),
        compiler_params=_cp("parallel"))(qr, kr, proj, proj, sinks)


def _outproj(ymix, amix, w_out):
    t = ymix.shape[0]
    tm, tn = _tile(t, 832), 1024

    def body(y_ref, a_ref, wy_ref, wa_ref, o_ref):
        o_ref[...] = _mm(y_ref[...], wy_ref[...]) + _mm(a_ref[...], wa_ref[...])

    return _pallas(
        body, name="outproj", grid=(t // tm, D_MODEL // tn),
        in_specs=[pl.BlockSpec((tm, D_SSD), lambda i, j: (i, 0)), pl.BlockSpec((tm, D_ATT), lambda i, j: (i, 0)),
                  pl.BlockSpec((D_SSD, tn), lambda i, j: (0, j)),
                  pl.BlockSpec((D_ATT, tn), lambda i, j: (D_SSD // D_ATT, j))],
        out_specs=pl.BlockSpec((tm, tn), lambda i, j: (i, j)),
        out_shape=jax.ShapeDtypeStruct((t, D_MODEL), F32),
        compiler_params=_cp("parallel", "parallel"))(ymix, amix, w_out, w_out)


def _post_loss(out, x, target, norm_post_w):
    t = out.shape[0]
    nc = t // CHUNK

    def body(o_ref, x_ref, tg_ref, nw_ref, dout_ref, dy_ref, loss_ref, gnw_ref):
        i = pl.program_id(0)

        @pl.when(i == 0)
        def _():
            dout_ref[...] = jnp.zeros_like(dout_ref)
            dy_ref[...] = jnp.zeros_like(dy_ref)
            loss_ref[...] = jnp.zeros_like(loss_ref)
            gnw_ref[...] = jnp.zeros_like(gnw_ref)

        @pl.when(i > 0)
        def _():
            o = o_ref[...]
            nw = nw_ref[...]
            rstd = lax.rsqrt(jnp.mean(o * o, axis=-1, keepdims=True) + EPS)
            n = o * rstd
            err = x_ref[...] + n * nw - tg_ref[...]
            loss_ref[...] += jnp.sum(err * err) * (0.5 / D_MODEL)
            dy = err * (1.0 / D_MODEL)
            dy_ref[...] = dy
            gnw_ref[...] += jnp.sum(dy * n, axis=0, keepdims=True)
            dn = dy * nw
            dout_ref[...] = _bf(rstd * (dn - n * jnp.mean(dn * n, axis=-1, keepdims=True)))

    prev = lambda i: (jnp.maximum(i - 1, 0), 0)
    return _pallas(
        body, name="post_loss", grid=(nc,),
        in_specs=[pl.BlockSpec((CHUNK, D_MODEL), lambda i: (i, 0)), pl.BlockSpec((CHUNK, D_MODEL), prev),
                  pl.BlockSpec((CHUNK, D_MODEL), prev), pl.BlockSpec((1, D_MODEL), lambda i: (0, 0))],
        out_specs=[pl.BlockSpec((CHUNK, D_MODEL), lambda i: (i, 0)), pl.BlockSpec((CHUNK, D_MODEL), lambda i: (i, 0)),
                   pl.BlockSpec((8, LANES), lambda i: (0, 0)), pl.BlockSpec((1, D_MODEL), lambda i: (0, 0))],
        out_shape=[jax.ShapeDtypeStruct((t, D_MODEL), BF16), jax.ShapeDtypeStruct((t, D_MODEL), F32),
                   jax.ShapeDtypeStruct((8, LANES), F32), jax.ShapeDtypeStruct((1, D_MODEL), F32)],
        compiler_params=_cp("arbitrary"))(out, x, target, norm_post_w)


def _nt_matmul(a, b, name):
    t, k = a.shape
    n = b.shape[0]
    tm, tn = _tile(t, 832), 1024

    def body(a_ref, b_ref, o_ref):
        o_ref[...] = _nt(a_ref[...], b_ref[...])

    return _pallas(
        body, name=name, grid=(t // tm, n // tn),
        in_specs=[pl.BlockSpec((tm, k), lambda i, j: (i, 0)), pl.BlockSpec((tn, k), lambda i, j: (j, 0))],
        out_specs=pl.BlockSpec((tm, tn), lambda i, j: (i, j)),
        out_shape=jax.ShapeDtypeStruct((t, n), F32),
        compiler_params=_cp("parallel", "parallel"))(a, b)


def _tn_matmul(a, b, name):
    t, m = a.shape
    n = b.shape[1]
    tk, tm, tn = _tile(t, 832), min(m, 1024), min(n, 2048)
    nk = t // tk

    def body(a_ref, b_ref, o_ref):
        @pl.when(pl.program_id(2) == 0)
        def _():
            o_ref[...] = jnp.zeros_like(o_ref)
        o_ref[...] += _tn(a_ref[...], b_ref[...])

    return _pallas(
        body, name=name, grid=(m // tm, n // tn, nk),
        in_specs=[pl.BlockSpec((tk, tm), lambda i, j, k: (k, i)), pl.BlockSpec((tk, tn), lambda i, j, k: (k, j))],
        out_specs=pl.BlockSpec((tm, tn), lambda i, j, k: (i, j)),
        out_shape=jax.ShapeDtypeStruct((m, n), F32),
        compiler_params=_cp("parallel", "parallel", "arbitrary"))(a, b)


def _attn_bwd(qr, kr, proj, dmix, sinks, ga):
    t = qr.shape[0]
    nc = t // CHUNK
    cps = _attn_chunks_per_step(nc)
    nsteps = nc // cps
    rows_step = cps * CHUNK
    scale = HEAD_DIM ** -0.5

    def body(q_ref, k_ref, v_ref, g_ref, da_ref, sink_ref, ga_ref, dq_ref, dg_ref, dk_ref, dv_ref, gs_ref,
             got_ref, send_sems, recv_sems):
        step = pl.program_id(0)

        @pl.when(step == 0)
        def _():
            for cp in _exchange_copies(ga_ref, got_ref, send_sems, recv_sems):
                cp.start()
            dk_ref[...] = jnp.zeros_like(dk_ref)
            dv_ref[...] = jnp.zeros_like(dv_ref)
            gs_ref[...] = jnp.zeros_like(gs_ref)

        lane = lax.broadcasted_iota(jnp.int32, (1, LANES), 1)
        rows = lax.broadcasted_iota(jnp.int32, (REP * CHUNK, 1), 0) >> 6
        gs = jnp.zeros((1, LANES), F32)
        dk_parts = [[] for _ in range(cps + BAND_CHUNKS - 1)]
        dv_parts = [[] for _ in range(cps + BAND_CHUNKS - 1)]
        for j in range(cps):
            c = step * cps + j
            rj = slice(CHUNK * j, CHUNK * (j + 1))
            ks, vs = _band(k_ref, c), _band(v_ref, c)
            q = q_ref[rj, :]
            gate = g_ref[rj, :]
            sg = _sigmoid(gate)
            da = da_ref[rj, :]
            datt = da * (gate * sg)
            dqs, atts, dks, dvs = [], [], [], []
            for h in range(KV_HEADS):
                qh, kb, vb, sink_col, valid = _attn_operands(c, q, ks, vs, sink_ref, h)
                p, psink = _attn_probs(qh, kb, sink_col, valid)
                pb = _bf(p)
                o = _mm(pb, vb)
                do = jnp.concatenate([datt[:, HEAD_DIM * (REP * h + r):HEAD_DIM * (REP * h + r + 1)]
                                      for r in range(REP)], axis=0)
                dob = _bf(do)
                delta = jnp.sum(do * o, axis=1, keepdims=True)
                ds = _bf(p * (_nt(dob, vb) - delta) * scale)
                gsink = -psink * delta
                for r in range(REP):
                    gs = gs + jnp.where(lane == REP * h + r, jnp.sum(jnp.where(rows == r, gsink, 0.0)), 0.0)
                dqh = _mm(ds, kb)
                dqs += [dqh[CHUNK * r:CHUNK * (r + 1)] for r in range(REP)]
                atts += [o[CHUNK * r:CHUNK * (r + 1)] for r in range(REP)]
                dks.append(_tn(ds, qh))
                dvs.append(_tn(pb, dob))
            dq_ref[rj, :] = jnp.concatenate(dqs, axis=1)
            att = jnp.concatenate(atts, axis=1)
            dg_ref[rj, :] = _bf(da * att * (sg * (1.0 + gate * (1.0 - sg))))
            dkf = jnp.concatenate(dks, axis=1)
            dvf = jnp.concatenate(dvs, axis=1)
            for b in range(BAND_CHUNKS):
                dk_parts[j + b].append(dkf[CHUNK * b:CHUNK * (b + 1)])
                dv_parts[j + b].append(dvf[CHUNK * b:CHUNK * (b + 1)])
        gs_ref[0:1, :] += gs
        for rel in range(cps + BAND_CHUNKS - 1):
            r0 = pl.multiple_of(jnp.maximum(step * cps - (BAND_CHUNKS - 1) + rel, 0) * CHUNK, CHUNK)
            dk_ref[pl.ds(r0, CHUNK), :] += sum(dk_parts[rel][1:], dk_parts[rel][0])
            dv_ref[pl.ds(r0, CHUNK), :] += sum(dv_parts[rel][1:], dv_parts[rel][0])

        @pl.when(step == nsteps - 1)
        def _():
            for cp in _exchange_copies(ga_ref, got_ref, send_sems, recv_sems):
                cp.wait()

    return _pallas(
        body, name="attn_bwd", grid=(nsteps,),
        in_specs=[pl.BlockSpec((rows_step, D_ATT), lambda i: (i, 0)), pl.BlockSpec((t, D_KV), lambda i: (0, 0)),
                  pl.BlockSpec((t, D_KV), lambda i: (0, OV // D_KV)),
                  pl.BlockSpec((rows_step, D_ATT), lambda i: (i, OG // D_ATT)),
                  pl.BlockSpec((rows_step, D_ATT), lambda i: (i, D_SSD // D_ATT)),
                  pl.BlockSpec(memory_space=pltpu.SMEM), ANY],
        out_specs=[pl.BlockSpec((rows_step, D_ATT), lambda i: (i, 0)), pl.BlockSpec((rows_step, D_ATT), lambda i: (i, 0)),
                   pl.BlockSpec((t, D_KV), lambda i: (0, 0)), pl.BlockSpec((t, D_KV), lambda i: (0, 0)),
                   pl.BlockSpec((8, LANES), lambda i: (0, 0)), ANY],
        out_shape=[jax.ShapeDtypeStruct((t, D_ATT), F32), jax.ShapeDtypeStruct((t, D_ATT), BF16),
                   jax.ShapeDtypeStruct((t, D_KV), F32), jax.ShapeDtypeStruct((t, D_KV), F32),
                   jax.ShapeDtypeStruct((8, LANES), F32), _exchange_shape(ga)],
        scratch_shapes=_exchange_scratch(),
        compiler_params=_cp("arbitrary"))(qr, kr, proj, proj, dmix, sinks, ga)


def _ssd_bwd(dmix, y_ssd, xbc, proj, dt, acs, acst, states, d_skip_l, ssd_norm_w):
    t = xbc.shape[0]
    q = CHUNK
    nc = t // q
    gps = SSD_BWD_GROUPS_PER_STEP
    gw, sw = gps * GROUP_W, gps * D_STATE

    def body(dmix_ref, y_ref, z_ref, nw_ref, xs_ref, b_ref, c_ref, dt_ref, acs_ref, acst_ref, st_ref, dsk_ref,
             dz_ref, dxs_ref, db_ref, dc_ref, dacs_ref, ddt_ref, gnw_ref, gdsk_ref, dstate):
        @pl.when(pl.program_id(1) == 0)
        def _():
            dstate[...] = jnp.zeros_like(dstate)
            gnw_ref[...] = jnp.zeros_like(gnw_ref)
            gdsk_ref[...] = jnp.zeros_like(gdsk_ref)

        last_row = lax.broadcasted_iota(jnp.int32, (q, 1), 0) == q - 1
        lane = lax.broadcasted_iota(jnp.int32, (q, LANES), 1)
        lane1 = lax.broadcasted_iota(jnp.int32, (8, LANES), 1)
        for gi in range(gps):
            g = gps * pl.program_id(0) + gi
            cols = slice(GROUP_W * gi, GROUP_W * (gi + 1))
            scols = slice(D_STATE * gi, D_STATE * (gi + 1))
            y = y_ref[:, cols]
            z = z_ref[:, cols]
            sz = _sigmoid(z)
            silu_z = z * sz
            yg = y * silu_z
            rstd = lax.rsqrt(jnp.mean(yg * yg, axis=-1, keepdims=True) + EPS)
            n = yg * rstd
            dout = dmix_ref[:, cols]
            gnw_ref[:, cols] += jnp.sum(dout * n, axis=0, keepdims=True)
            dn = dout * nw_ref[:, cols]
            dyg = rstd * (dn - n * jnp.mean(dn * n, axis=-1, keepdims=True))
            dy = dyg * silu_z
            dz_ref[:, cols] = _bf(dyg * y * (sz * (1.0 + z * (1.0 - sz))))

            x = xs_ref[:, cols]
            bmb, cmb = _bf(b_ref[:, scols]), _bf(c_ref[:, scols])
            hv = _group_heads(g, gi, dt_ref[...], acs_ref[...], acst_ref, dsk_ref[...])
            dec = jnp.exp(jnp.where(_head_tri(q, True), hv.acs - hv.acs_row, NEG))
            dect = jnp.exp(jnp.where(_head_tri(q, False), hv.acs_row - hv.acs, NEG))
            b4 = jnp.concatenate([bmb] * HPG, axis=0)
            c4 = jnp.concatenate([cmb] * HPG, axis=0)
            m_all = _nt(cmb, b4) * dec
            mt_all = _nt(bmb, c4) * dect
            xdt = x * hv.dt
            xdt_b, dyb = _bf(xdt), _bf(dy)
            x_bd, dy_bd = _block_diag(xdt_b), _block_diag(dyb)
            s_prev = st_ref[0, gi]
            spb = _bf(s_prev)
            ds_new = dstate[gi]
            dsb = _bf(ds_new)
            e = jnp.exp(hv.acs)
            elast = jnp.exp(hv.acs_last)
            dte = jnp.exp(hv.acs_last - hv.acs)
            bds = _mm(bmb, dsb)
            dxdt = _mm(_bf(mt_all), dy_bd) + bds * dte
            dm = _nt(dyb, x_bd)
            dmt = _nt(xdt_b, dy_bd)
            dye = _bf(dy * e)
            dc_ref[:, scols] = _mm(_bf(dm * dec), b4) + _nt(dye, spb)
            db_ref[:, scols] = _mm(_bf(dmt * dect), c4) + _nt(_bf(xdt * dte), dsb)
            dstate[gi] = elast * ds_new + _tn(cmb, dye)
            dxs_ref[:, cols] = dxdt * hv.dt + hv.dsk * dy
            ddte_dte = bds * xdt * dte
            dacs_l = dm * m_all - dmt * mt_all + dy * _mm(cmb, spb) * e - ddte_dte
            dlast_l = (jnp.sum(ddte_dte, axis=0, keepdims=True)
                       + jnp.sum(s_prev * ds_new, axis=0, keepdims=True) * elast)
            ddt_l = dxdt * x
            gdsk_l = jnp.sum(dy * x, axis=0, keepdims=True)
            dacs_out = jnp.zeros((q, LANES), F32)
            ddt_out = jnp.zeros((q, LANES), F32)
            gdsk = jnp.zeros((8, LANES), F32)
            for r in range(HPG):
                dacs = _head_sums(dacs_l, r) + jnp.where(last_row, _head_sums(dlast_l, r), 0.0)
                dacs_out = jnp.where(lane == r, dacs, dacs_out)
                ddt_out = jnp.where(lane == r, _head_sums(ddt_l, r), ddt_out)
                gdsk = gdsk + jnp.where(lane1 == r, _head_sums(gdsk_l, r), 0.0)
            dacs_ref[:, LANES * gi:LANES * (gi + 1)] = dacs_out
            ddt_ref[:, LANES * gi:LANES * (gi + 1)] = ddt_out
            gdsk_ref[gi] += gdsk

    rev = lambda c: nc - 1 - c
    wide = pl.BlockSpec((q, gw), lambda g, c: (rev(c), g))
    return _pallas(
        body, name="ssd_bwd", grid=(GROUPS // gps, nc),
        in_specs=[wide, wide, wide, pl.BlockSpec((1, gw), lambda g, c: (0, g)), wide,
                  pl.BlockSpec((q, sw), lambda g, c: (rev(c), D_SSD // sw + g)),
                  pl.BlockSpec((q, sw), lambda g, c: (rev(c), (D_SSD + GROUPS * D_STATE) // sw + g)),
                  pl.BlockSpec((q, LANES), lambda g, c: (rev(c), 0)), pl.BlockSpec((q, LANES), lambda g, c: (rev(c), 0)),
                  pl.BlockSpec((1, gps * GROUPS, q), lambda g, c: (rev(c), g, 0)),
                  pl.BlockSpec((1, gps, D_STATE, GROUP_W), lambda g, c: (rev(c), g, 0, 0)),
                  pl.BlockSpec((1, LANES), lambda g, c: (0, 0))],
        out_specs=[wide, wide,
                   pl.BlockSpec((q, sw), lambda g, c: (rev(c), g)), pl.BlockSpec((q, sw), lambda g, c: (rev(c), g)),
                   pl.BlockSpec((q, gps * LANES), lambda g, c: (rev(c), g)),
                   pl.BlockSpec((q, gps * LANES), lambda g, c: (rev(c), g)),
                   pl.BlockSpec((1, gw), lambda g, c: (0, g)), pl.BlockSpec((gps, 8, LANES), lambda g, c: (g, 0, 0))],
        out_shape=[jax.ShapeDtypeStruct((t, D_SSD), BF16), jax.ShapeDtypeStruct((t, D_SSD), F32),
                   jax.ShapeDtypeStruct((t, GROUPS * D_STATE), F32), jax.ShapeDtypeStruct((t, GROUPS * D_STATE), F32),
                   jax.ShapeDtypeStruct((t, GROUPS * LANES), F32), jax.ShapeDtypeStruct((t, GROUPS * LANES), F32),
                   jax.ShapeDtypeStruct((1, D_SSD), F32), jax.ShapeDtypeStruct((GROUPS, 8, LANES), F32)],
        scratch_shapes=[pltpu.VMEM((gps, D_STATE, GROUP_W), F32)],
        compiler_params=_cp("parallel", "arbitrary"))(dmix, y_ssd, proj, ssd_norm_w, xbc, xbc, xbc, dt, acs, acst,
                                                      states, d_skip_l)


def _dt_bwd(dacs_g, ddt_g, dt, proj, dt_bias_l, a_log_l):
    t = dt.shape[0]
    q = CHUNK
    nc = t // q
    cps = _chunks_per_step(nc)
    rows = cps * q

    def body(dacs_ref, ddt_ref, dt_ref, raw_ref, bias_ref, alog_ref, draw_ref, ga_ref, gb_ref):
        @pl.when(pl.program_id(0) == 0)
        def _():
            ga_ref[...] = jnp.zeros_like(ga_ref)
            gb_ref[...] = jnp.zeros_like(gb_ref)

        lane = lax.broadcasted_iota(jnp.int32, (q, LANES), 1)
        ri = lax.broadcasted_iota(jnp.int32, (q, q), 0)
        ci = lax.broadcasted_iota(jnp.int32, (q, q), 1)
        triu = (ri <= ci).astype(F32)
        a = -jnp.exp(alog_ref[...])
        used = (lane & (GROUPS - 1)) < HPG
        ga = jnp.zeros((1, LANES), F32)
        gb = jnp.zeros((1, LANES), F32)
        for k in range(cps):
            rk = slice(q * k, q * (k + 1))
            dacs = jnp.zeros((q, LANES), F32)
            ddt = jnp.zeros((q, LANES), F32)
            for g in range(GROUPS):
                mask = (lane >= GROUPS * g) & (lane < GROUPS * g + HPG)
                sl = slice(LANES * g, LANES * (g + 1))
                if g == 0:
                    dacs = jnp.where(mask, dacs_ref[rk, sl], dacs)
                    ddt = jnp.where(mask, ddt_ref[rk, sl], ddt)
                else:
                    dacs = jnp.where(mask, pltpu.roll(dacs_ref[rk, sl], GROUPS * g, 1), dacs)
                    ddt = jnp.where(mask, pltpu.roll(ddt_ref[rk, sl], GROUPS * g, 1), ddt)
            dda = jnp.dot(triu, dacs, preferred_element_type=F32, precision=HI)
            row = pl.program_id(0) * rows + q * k + lax.broadcasted_iota(jnp.int32, (q, LANES), 0)
            dsp = jnp.where((row >= PAD_LEAD) & used, dda * a + ddt, 0.0)
            draw = dsp * _sigmoid(raw_ref[rk, :] + bias_ref[...])
            draw_ref[rk, :] = _bf(draw)
            gb = gb + jnp.sum(draw, axis=0, keepdims=True)
            ga = ga + jnp.sum(jnp.where(used, dda * dt_ref[rk, :], 0.0), axis=0, keepdims=True)
        gb_ref[0:1, :] += gb
        ga_ref[0:1, :] += ga * a

    return _pallas(
        body, name="dt_bwd", grid=(nc // cps,),
        in_specs=[pl.BlockSpec((rows, GROUPS * LANES), lambda c: (c, 0)),
                  pl.BlockSpec((rows, GROUPS * LANES), lambda c: (c, 0)),
                  pl.BlockSpec((rows, LANES), lambda c: (c, 0)), pl.BlockSpec((rows, LANES), lambda c: (c, ODT // LANES)),
                  pl.BlockSpec((1, LANES), lambda c: (0, 0)), pl.BlockSpec((1, LANES), lambda c: (0, 0))],
        out_specs=[pl.BlockSpec((rows, LANES), lambda c: (c, 0)), pl.BlockSpec((8, LANES), lambda c: (0, 0)),
                   pl.BlockSpec((8, LANES), lambda c: (0, 0))],
        out_shape=[jax.ShapeDtypeStruct((t, LANES), BF16), jax.ShapeDtypeStruct((8, LANES), F32),
                   jax.ShapeDtypeStruct((8, LANES), F32)],
        compiler_params=_cp("arbitrary"))(dacs_g, ddt_g, dt, proj, dt_bias_l, a_log_l)


def _conv_bwd(dseg, proj, conv_w, conv_b, col_off, name):
    t, width = dseg.shape
    tc = 128
    off_p = (OXS + col_off) // tc
    off_w = col_off // tc

    def body(d_ref, x_ref, w_ref, b_ref, dx_ref, gw_ref, gb_ref, xp, dup):
        xp[0:8, :] = jnp.zeros((8, tc), F32)
        xp[8:t + 8, :] = x_ref[...]
        w = w_ref[...]
        u = (b_ref[...] + w[3:4, :] * xp[8:t + 8, :] + w[2:3, :] * xp[7:t + 7, :]
             + w[1:2, :] * xp[6:t + 6, :] + w[0:1, :] * xp[5:t + 5, :])
        su = _sigmoid(u)
        du = d_ref[...] * (su * (1.0 + u * (1.0 - su)))
        dup[0:t, :] = du
        dup[t:t + 8, :] = jnp.zeros((8, tc), F32)
        dx_ref[...] = _bf(w[3:4, :] * du + w[2:3, :] * dup[1:t + 1, :] + w[1:2, :] * dup[2:t + 2, :]
                          + w[0:1, :] * dup[3:t + 3, :])
        gb_ref[...] = jnp.sum(du, axis=0, keepdims=True)
        gw_ref[...] = jnp.concatenate(
            [jnp.sum(du * xp[5 + k:t + 5 + k, :], axis=0, keepdims=True) for k in range(CONV_WIDTH)], axis=0)

    return _pallas(
        body, name=name, grid=(width // tc,),
        in_specs=[pl.BlockSpec((t, tc), lambda j: (0, j)), pl.BlockSpec((t, tc), lambda j: (0, j + off_p)),
                  pl.BlockSpec((CONV_WIDTH, tc), lambda j: (0, j + off_w)), pl.BlockSpec((1, tc), lambda j: (0, j + off_w))],
        out_specs=[pl.BlockSpec((t, tc), lambda j: (0, j)), pl.BlockSpec((CONV_WIDTH, tc), lambda j: (0, j)),
                   pl.BlockSpec((1, tc), lambda j: (0, j))],
        out_shape=[jax.ShapeDtypeStruct((t, width), BF16), jax.ShapeDtypeStruct((CONV_WIDTH, width), F32),
                   jax.ShapeDtypeStruct((1, width), F32)],
        scratch_shapes=[pltpu.VMEM((t + 8, tc), F32), pltpu.VMEM((t + 8, tc), F32)],
        compiler_params=_cp("parallel"))(dseg, proj, conv_w, conv_b)


def _dinproj(segs, w_re, hpad, norm_w, dy_t, ga):
    t = segs[0].shape[0]
    d = hpad.shape[1]
    tm, tk = _tile(t, 416), SEG_TILE
    counts = [s.shape[1] // tk for s in segs]
    firsts = [sum(counts[:s]) for s in range(len(segs))]
    nk = sum(counts)
    assert nk * tk == w_re.shape[1]
    ni = t // tm
    ns = len(segs)

    def body(*refs):
        seg_refs = refs[:ns]
        w_ref, h_ref, nw_ref, dy_ref, ga_ref, dh_ref, gnw_ref, got_ref, acc, send_sems, recv_sems = refs[ns:]
        i, k = pl.program_id(0), pl.program_id(1)

        @pl.when((i == 0) & (k == 0))
        def _():
            for cp in _exchange_copies(ga_ref, got_ref, send_sems, recv_sems):
                cp.start()
            gnw_ref[...] = jnp.zeros_like(gnw_ref)

        @pl.when(k == 0)
        def _():
            acc[...] = jnp.zeros_like(acc)

        for s in range(ns):
            @pl.when((k >= firsts[s]) & (k < firsts[s] + counts[s]))
            def _(s=s):
                acc[...] += _nt(seg_refs[s][...], w_ref[...])

        @pl.when(k == nk - 1)
        def _():
            h = h_ref[...]
            rstd = lax.rsqrt(jnp.mean(h * h, axis=-1, keepdims=True) + EPS)
            nrm = h * rstd
            dhn = acc[...]
            gnw_ref[...] += jnp.sum(dhn * nrm, axis=0, keepdims=True)
            dn = dhn * nw_ref[...]
            dh_ref[...] = rstd * (dn - nrm * jnp.mean(dn * nrm, axis=-1, keepdims=True)) + dy_ref[...]

        @pl.when((i == ni - 1) & (k == nk - 1))
        def _():
            for cp in _exchange_copies(ga_ref, got_ref, send_sems, recv_sems):
                cp.wait()

    seg_specs = [pl.BlockSpec((tm, tk), functools.partial(lambda i, k, f0, n0: (i, jnp.clip(k - f0, 0, n0 - 1)),
                                                          f0=firsts[s], n0=counts[s])) for s in range(ns)]
    return _pallas(
        body, name="dinproj", grid=(ni, nk),
        in_specs=seg_specs + [pl.BlockSpec((d, tk), lambda i, k: (0, k)),
                              pl.BlockSpec((tm, d), lambda i, k: (i, 0)), pl.BlockSpec((1, d), lambda i, k: (0, 0)),
                              pl.BlockSpec((tm, d), lambda i, k: (i, 0)), ANY],
        out_specs=[pl.BlockSpec((tm, d), lambda i, k: (i, 0)), pl.BlockSpec((1, d), lambda i, k: (0, 0)), ANY],
        out_shape=[jax.ShapeDtypeStruct((t, d), F32), jax.ShapeDtypeStruct((1, d), F32), _exchange_shape(ga)],
        scratch_shapes=[pltpu.VMEM((tm, d), F32)] + _exchange_scratch(),
        compiler_params=_cp("arbitrary", "arbitrary"))(*segs, w_re, hpad, norm_w, dy_t, ga)


def _spread_heads(v):
    v = jnp.pad(v.reshape(GROUPS, HPG), ((0, 0), (0, GROUPS - HPG))).reshape(1, GROUPS * GROUPS)
    return jnp.pad(v, ((0, 0), (0, LANES - GROUPS * GROUPS)))


def _gather_heads(v):
    return v[0:1, :GROUPS * GROUPS].reshape(GROUPS, GROUPS)[:, :HPG].reshape(1, SSD_HEADS)


def _rope_tables(t):
    half = HEAD_DIM // 2
    inv = ROPE_THETA ** (-jnp.arange(half, dtype=F32) / half)
    pos = (jnp.arange(t) - PAD_LEAD).astype(F32)
    ang = pos[:, None] * inv[None, :]
    cos, sin = jnp.cos(ang), jnp.sin(ang)
    cos_t = jnp.concatenate([cos, cos, cos, cos], axis=1)
    sin_t = jnp.concatenate([-sin, sin, -sin, sin], axis=1)
    return cos_t, sin_t


def _column_pieces():
    runs = [(0, OB + 2 * GROUPS * D_STATE, 0)]
    o = OB + 2 * GROUPS * D_STATE
    runs += [(o + HPG * g, HPG, ODT + GROUPS * g) for g in range(GROUPS)]
    o += SSD_HEADS
    for width, dst in ((D_ATT, OQ), (D_KV, OK), (D_KV, OV), (D_ATT, OG)):
        runs.append((o, width, dst))
        o += width
    assert o == D_IN
    pieces = []
    for o0, width, dst in runs:
        for j in range(N_SHARD):
            lo, hi = max(o0, W_IN_SHARD * j), min(o0 + width, W_IN_SHARD * (j + 1))
            if lo < hi:
                pieces.append((j, lo - W_IN_SHARD * j, hi - W_IN_SHARD * j, dst + lo - o0))
    return pieces


def _shards_to_re(w_all):
    _, k, _ = w_all.shape
    tr = 256

    def body(x_ref, o_ref):
        o_ref[:, ODT:ODT + DT_SLAB] = jnp.zeros((tr, DT_SLAB), o_ref.dtype)
        for j, c0, c1, d0 in _column_pieces():
            o_ref[:, d0:d0 + c1 - c0] = x_ref[j, :, c0:c1]

    return _pallas(body, name="shards_to_re", grid=(k // tr,),
                   in_specs=[pl.BlockSpec((N_SHARD, tr, W_IN_SHARD), lambda i: (0, i, 0))],
                   out_specs=pl.BlockSpec((tr, N_RE), lambda i: (i, 0)),
                   out_shape=jax.ShapeDtypeStruct((k, N_RE), w_all.dtype), compiler_params=_cp("parallel"))(w_all)


def _pair_add_to_shards(parts, got, pieces, shard_rows, core, name):
    n = parts[0].shape[1]
    hn = n // 2
    tc = 128
    nt = hn // tc
    ns = len(parts)
    starts = [sum(p.shape[0] for p in parts[:s]) for s in range(ns)]
    moves = []
    for j, c0, c1, d0 in pieces:
        for s, p in enumerate(parts):
            lo, hi = max(d0, starts[s]), min(d0 + c1 - c0, starts[s] + p.shape[0])
            if lo < hi:
                moves.append((s, lo - starts[s], j, c0 + lo - d0, hi - lo))
    assert sum(m[4] for m in moves) == N_SHARD * shard_rows

    def body(core_ref, *refs):
        own, theirs, o_ref, acc = refs[:ns], refs[ns:2 * ns], refs[2 * ns], refs[2 * ns + 1]
        for s, r0, j, c0, rows in moves:
            acc[j, c0:c0 + rows, :] = own[s][r0:r0 + rows, :] + theirs[s][r0:r0 + rows, :]
        o_ref[...] = _bf(acc[...])

    return _pallas(
        body, name=name,
        grid_spec=pltpu.PrefetchScalarGridSpec(
            num_scalar_prefetch=1, grid=(nt,),
            in_specs=[pl.BlockSpec((p.shape[0], tc), lambda i, core_ref: (0, core_ref[0] * nt + i)) for p in parts]
            + [pl.BlockSpec((p.shape[0], tc), lambda i, core_ref: (0, i)) for p in parts],
            out_specs=pl.BlockSpec((N_SHARD, shard_rows, tc), lambda i, core_ref: (0, 0, i)),
            scratch_shapes=[pltpu.VMEM((N_SHARD, shard_rows, tc), F32)]),
        out_shape=jax.ShapeDtypeStruct((N_SHARD, shard_rows, hn), BF16),
        compiler_params=_cp("parallel"))(core, *parts, *got)


def _local_step(x, target, meta, norm_pre_w, w_re, conv_w, conv_b, dt_bias, a_log, d_skip, ssd_norm_w, sinks,
                w_out_shard, norm_post_w, place):
    seq = x.shape[0]
    t = PAD_LEAD + N_META + seq
    hpad = jnp.concatenate([jnp.zeros((PAD_LEAD, D_MODEL), F32), meta, x], axis=0)
    dt_bias_l, a_log_l, d_skip_l = _spread_heads(dt_bias), _spread_heads(a_log), _spread_heads(d_skip)
    cos_t, sin_t = _rope_tables(t)
    sink_v = sinks.reshape(Q_HEADS)

    proj, hn, w_out_all = _inproj(hpad, norm_pre_w, w_re, w_out_shard)
    w_out = w_out_all.reshape(D_MIX, D_MODEL)
    xbc = _conv_fwd(proj, conv_w, conv_b)
    dt, acs, acst = _dt_prep(proj, dt_bias_l, a_log_l)
    y_ssd, ymix, states = _ssd_fwd(xbc, proj, dt, acs, acst, d_skip_l, ssd_norm_w)
    qr, kr = _rope(proj, OQ, proj, OK, cos_t, sin_t)
    amix = _attn_fwd(qr, kr, proj, sink_v)
    out = _outproj(ymix, amix, w_out)
    dout, dy_t, loss_blk, g_norm_post = _post_loss(out, x, target, norm_post_w)

    dmix = _nt_matmul(dout, w_out, "dmix")
    g_out_parts = [_tn_matmul(ymix, dout, "gw_out_y"), _tn_matmul(amix, dout, "gw_out_a")]
    ga_out = _reduce_pair(g_out_parts, [(j, 0, W_OUT_SHARD, W_OUT_SHARD * j) for j in range(N_SHARD)], W_OUT_SHARD,
                          place, "gw_out")
    dq_r, dg, dk_r, dv, gs, slabs_out = _attn_bwd(qr, kr, proj, dmix, sink_v, ga_out)
    g_w_out = _reduce_finish(ga_out, slabs_out, place, "gw_out")
    dq, dk = _rope(dq_r, 0, dk_r, 0, cos_t, -sin_t)
    dz, dxs, db, dc, dacs_g, ddt_g, g_ssd_norm, gdsk = _ssd_bwd(dmix, y_ssd, xbc, proj, dt, acs, acst, states,
                                                                d_skip_l, ssd_norm_w)
    draw, ga, gb = _dt_bwd(dacs_g, ddt_g, dt, proj, dt_bias_l, a_log_l)
    dxs_p, gcw0, gcb0 = _conv_bwd(dxs, proj, conv_w, conv_b, 0, "conv_bwd_x")
    db_p, gcw1, gcb1 = _conv_bwd(db, proj, conv_w, conv_b, D_SSD, "conv_bwd_b")
    dc_p, gcw2, gcb2 = _conv_bwd(dc, proj, conv_w, conv_b, D_SSD + GROUPS * D_STATE, "conv_bwd_c")
    tail = jnp.concatenate([dk, _bf(dv), draw, jnp.zeros((t, DT_SLAB - LANES), BF16)], axis=1)
    segs = [dz, dxs_p, db_p, dc_p, dq, dg, tail]
    g_parts = [_tn_matmul(seg, hn, "gw_in_%d" % s) for s, seg in enumerate(segs)]
    ga_in = _reduce_pair(g_parts, _column_pieces(), W_IN_SHARD, place, "gw_in")
    dh, g_norm_pre, slabs_in = _dinproj(segs, w_re, hpad, norm_pre_w, dy_t, ga_in)
    g_w_in = _reduce_finish(ga_in, slabs_in, place, "gw_in")

    gdsk_l = jnp.concatenate([gdsk[g, 0:1, 0:GROUPS] for g in range(GROUPS)], axis=1)
    gdsk_l = jnp.pad(gdsk_l, ((0, 0), (0, LANES - GROUPS * GROUPS)))
    grads = dict(
        meta_tokens=dh[PAD_LEAD:ROW0], norm_pre_w=g_norm_pre, w_in=g_w_in,
        conv_w=jnp.concatenate([gcw0, gcw1, gcw2], axis=1), conv_b=jnp.concatenate([gcb0, gcb1, gcb2], axis=1),
        dt_bias=_gather_heads(gb), a_log=_gather_heads(ga), d_skip=_gather_heads(gdsk_l), ssd_norm_w=g_ssd_norm,
        attn_sinks=gs[0:1, :Q_HEADS], w_out=g_w_out, norm_post_w=g_norm_post)
    return loss_blk[0, 0], dh[ROW0:], grads


ANY = pl.BlockSpec(memory_space=pl.ANY)
MESH = pl.DeviceIdType.MESH
GATHER_CHUNKS = 4
PAIR_CHUNKS = 8
JOIN_CHUNKS = 8


def _rcopy(src, dst, ssem, rsem, dev):
    return pltpu.make_async_remote_copy(src_ref=src, dst_ref=dst, send_sem=ssem, recv_sem=rsem, device_id=dev,
                                        device_id_type=MESH)


def _place():
    x, y, c = lax.axis_index("x"), lax.axis_index("y"), lax.axis_index("c")
    chips = [(1 - x, y), (x, 1 - y), (1 - x, 1 - y)]
    return x, y, c, chips


def _gather_plan(x_ref, out_ref, send_sems, recv_sems, local_sems, hr, kc):
    ch = hr // kc
    assert ch * kc == hr and ch % 16 == 0
    x, y, c, chips = _place()
    me = 2 * x + y
    sibling = (x, y, 1 - c)

    def piece(chip, hc, k):
        return out_ref.at[chip, pl.ds(hc * hr + k * ch, ch), :]

    def local():
        return [pltpu.make_async_copy(x_ref.at[pl.ds(k * ch, ch), :], out_ref.at[me, pl.ds(k * ch, ch), :],
                                      local_sems.at[k]) for k in range(2 * kc)]

    def first():
        return [_rcopy(x_ref.at[pl.ds(c * hr + k * ch, ch), :], piece(me, c, k), send_sems.at[j * kc + k],
                       recv_sems.at[j * kc + k], (*chip, c)) for j, chip in enumerate(chips) for k in range(kc)]

    def passed(hc):
        return [_rcopy(piece(2 * chip[0] + chip[1], hc, k), piece(2 * chip[0] + chip[1], hc, k),
                       send_sems.at[(3 + j) * kc + k], recv_sems.at[(3 + j) * kc + k], sibling)
                for j, chip in enumerate(chips) for k in range(kc)]

    def arrivals():
        return [_rcopy(piece(2 * chip[0] + chip[1], c, k), piece(2 * chip[0] + chip[1], c, k), send_sems.at[j * kc + k],
                       recv_sems.at[j * kc + k], (*chip, c)) for j, chip in enumerate(chips) for k in range(kc)]

    def start():
        for cp in local() + first():
            cp.start()

    def forward():
        for arrived in arrivals():
            arrived.wait_recv()
        for fw in passed(c):
            fw.start()

    def finish():
        for cp in passed(1 - c):
            cp.wait_recv()
        for cp in first() + passed(c):
            cp.wait_send()
        for cp in local():
            cp.wait()

    return start, forward, finish


def _gather_shards(shard, name, kc):
    r, n = shard.shape
    hr = r // 2
    qr = hr // 2
    ch = qr // kc
    assert ch * kc == qr and ch % 16 == 0
    nflow = 12

    def body(x_ref, out_ref, send_sems, recv_sems, local_sems):
        x, y, c, _ = _place()
        me, cxn, cyn, cdg = 2 * x + y, 2 * (1 - x) + y, 2 * x + 1 - y, 2 * (1 - x) + 1 - y
        xn, yn, sibling = (1 - x, y, c), (x, 1 - y, c), (x, y, 1 - c)

        def piece(chip, hc, part, k):
            return out_ref.at[chip, pl.ds(hc * hr + part * qr + k * ch, ch), :]

        def own(part, k):
            return x_ref.at[pl.ds(c * hr + part * qr + k * ch, ch), :]

        def sems(flow, k):
            return send_sems.at[flow * kc + k], recv_sems.at[flow * kc + k]

        def arrival(flow, chip, hc, part, k):
            return _rcopy(piece(chip, hc, part, k), piece(chip, hc, part, k), *sems(flow, k), sibling)

        local = [pltpu.make_async_copy(x_ref.at[pl.ds(k * ch, ch), :], out_ref.at[me, pl.ds(k * ch, ch), :],
                                       local_sems.at[k]) for k in range(4 * kc)]
        sends = []
        for flow, part, peer in ((0, 0, xn), (1, 1, yn), (2, 0, yn), (3, 1, xn)):
            sends += [_rcopy(own(part, k), piece(me, c, part, k), *sems(flow, k), peer) for k in range(kc)]
        for cp in local + sends:
            cp.start()
        landing = ((0, cxn, 0), (1, cyn, 1), (2, cyn, 0), (3, cxn, 1), (4, cdg, 0), (5, cdg, 1))
        for i, (flow, chip, part) in enumerate(landing):
            for k in range(kc):
                arrival(flow, chip, c, part, k).wait_recv()
                if flow < 2:
                    on = _rcopy(piece(chip, c, part, k), piece(chip, c, part, k), *sems(4 + flow, k), yn if flow == 0 else xn)
                    on.start()
                    sends.append(on)
                fw = _rcopy(piece(chip, c, part, k), piece(chip, c, part, k), *sems(6 + i, k), sibling)
                fw.start()
                sends.append(fw)
        for i, (flow, chip, part) in enumerate(landing):
            for k in range(kc):
                arrival(6 + i, chip, 1 - c, part, k).wait_recv()
        for cp in sends:
            cp.wait_send()
        for cp in local:
            cp.wait()

    return _pallas(
        body, name=name, in_specs=[ANY], out_specs=ANY,
        out_shape=jax.ShapeDtypeStruct((N_SHARD, r, n), shard.dtype),
        scratch_shapes=[pltpu.SemaphoreType.DMA((nflow * kc,)), pltpu.SemaphoreType.DMA((nflow * kc,)),
                        pltpu.SemaphoreType.DMA((4 * kc,))])(shard)


def _pair_send(parts, name):
    n = parts[0].shape[1]
    hn = n // 2
    kc = PAIR_CHUNKS
    cw = hn // kc
    assert cw * kc == hn and cw % LANES == 0
    ns = len(parts)

    def body(*refs):
        srcs, dsts, send_sems, recv_sems = refs[:ns], refs[ns:2 * ns], refs[2 * ns], refs[2 * ns + 1]
        x, y, c, _ = _place()
        cps = [_rcopy(srcs[s].at[:, pl.ds((1 - c) * hn + k * cw, cw)], dsts[s].at[:, pl.ds(k * cw, cw)],
                      send_sems.at[s * kc + k], recv_sems.at[s * kc + k], (x, y, 1 - c))
               for s in range(ns) for k in range(kc)]
        for cp in cps:
            cp.start()
        for cp in cps:
            cp.wait()

    return _pallas(
        body, name=name, in_specs=[ANY] * ns, out_specs=[ANY] * ns,
        out_shape=[jax.ShapeDtypeStruct((p.shape[0], hn), F32) for p in parts],
        scratch_shapes=[pltpu.SemaphoreType.DMA((ns * kc,)), pltpu.SemaphoreType.DMA((ns * kc,))])(*parts)


REDUCE_TILE = 256


def _exchange_copies(g_ref, got_ref, send_sems, recv_sems):
    hn = g_ref.shape[2]
    kc = GATHER_CHUNKS
    cw = hn // kc
    assert cw * kc == hn and cw % LANES == 0
    x, y, c, chips = _place()
    return [_rcopy(g_ref.at[2 * chip[0] + chip[1], :, pl.ds(k * cw, cw)], got_ref.at[j, :, pl.ds(k * cw, cw)],
                   send_sems.at[j * kc + k], recv_sems.at[j * kc + k], (*chip, c))
            for j, chip in enumerate(chips) for k in range(kc)]


def _exchange_scratch():
    return [pltpu.SemaphoreType.DMA((3 * GATHER_CHUNKS,)), pltpu.SemaphoreType.DMA((3 * GATHER_CHUNKS,))]


def _exchange_shape(ga):
    return jax.ShapeDtypeStruct((3,) + ga.shape[1:], ga.dtype)


def _chip_sum(ga, got, place, name):
    _, r, hn = ga.shape
    tc = REDUCE_TILE
    nt = hn // tc

    def body(place_ref, own_ref, got_ref, o_ref):
        acc = own_ref[0].astype(F32)
        for j in range(3):
            acc = acc + got_ref[j].astype(F32)
        o_ref[...] = acc

    return _pallas(
        body, name=name,
        grid_spec=pltpu.PrefetchScalarGridSpec(
            num_scalar_prefetch=1, grid=(nt,),
            in_specs=[pl.BlockSpec((1, r, tc), lambda i, place_ref: (place_ref[0], 0, i)),
                      pl.BlockSpec((3, r, tc), lambda i, place_ref: (0, 0, i))],
            out_specs=pl.BlockSpec((r, tc), lambda i, place_ref: (0, place_ref[1] * nt + i))),
        out_shape=jax.ShapeDtypeStruct((r, 2 * hn), F32), compiler_params=_cp("parallel"))(place, ga, got)


def _pair_join(buf, name):
    r, n = buf.shape
    hn = n // 2
    kc = JOIN_CHUNKS
    cw = hn // kc
    assert cw * kc == hn and cw % LANES == 0

    def body(in_ref, out_ref, send_sems, recv_sems):
        x, y, c, _ = _place()
        cps = [_rcopy(out_ref.at[:, pl.ds(c * hn + k * cw, cw)], out_ref.at[:, pl.ds(c * hn + k * cw, cw)],
                      send_sems.at[k], recv_sems.at[k], (x, y, 1 - c)) for k in range(kc)]
        for cp in cps:
            cp.start()
        for k in range(kc):
            cols = out_ref.at[:, pl.ds((1 - c) * hn + k * cw, cw)]
            _rcopy(cols, cols, send_sems.at[k], recv_sems.at[k], (x, y, 1 - c)).wait_recv()
        for cp in cps:
            cp.wait_send()

    return _pallas(
        body, name=name, in_specs=[ANY], out_specs=ANY, out_shape=jax.ShapeDtypeStruct((r, n), F32),
        input_output_aliases={0: 0},
        scratch_shapes=[pltpu.SemaphoreType.DMA((kc,)), pltpu.SemaphoreType.DMA((kc,))])(buf)


def _reduce_pair(parts, pieces, shard_rows, place, tag):
    got = _pair_send(parts, tag + "_pair_send")
    return _pair_add_to_shards(parts, got, pieces, shard_rows, place[1:2], tag + "_pair_add")


def _reduce_finish(ga, slabs, place, tag):
    return _pair_join(_chip_sum(ga, slabs, place, tag + "_chip_sum"), tag + "_pair_join")


def _allreduce_small(p, name):
    rows, n = p.shape
    ndev = 8

    def body(p_ref, out_ref, slots, send_sems, recv_sems):
        x, y, c, _ = _place()
        my = 4 * x + 2 * y + c
        slots[my] = p_ref[...]
        cps = []
        for k in range(1, ndev):
            kx, ky, kc = (k >> 2) & 1, (k >> 1) & 1, k & 1
            peer = (x ^ kx, y ^ ky, c ^ kc)
            cp = _rcopy(p_ref, slots.at[my], send_sems.at[k - 1], recv_sems.at[k - 1], peer)
            cp.start()
            cps.append(cp)
        for k in range(1, ndev):
            _rcopy(p_ref, slots.at[my ^ k], send_sems.at[k - 1], recv_sems.at[k - 1], (x, y, c)).wait_recv()
        for cp in cps:
            cp.wait_send()
        acc = slots[0]
        for j in range(1, ndev):
            acc = acc + slots[j]
        out_ref[...] = acc

    vm = pl.BlockSpec(memory_space=pltpu.VMEM)
    return _pallas(
        body, name=name, in_specs=[vm], out_specs=vm, out_shape=jax.ShapeDtypeStruct((rows, n), F32),
        scratch_shapes=[pltpu.VMEM((ndev, rows, n), F32), pltpu.SemaphoreType.DMA((ndev - 1,)),
                        pltpu.SemaphoreType.DMA((ndev - 1,))])(p)


def _adamw(w, g, m, v, name):
    r, n = w.shape
    tr = _tile(r, 256, 8)
    c1 = 1.0 / (1.0 - ADAM_B1 ** ADAM_STEP)
    c2 = 1.0 / (1.0 - ADAM_B2 ** ADAM_STEP)

    def body(w_ref, g_ref, m_ref, v_ref, d_ref, mo_ref, vo_ref):
        gv = g_ref[...]
        mn = ADAM_B1 * m_ref[...] + (1.0 - ADAM_B1) * gv
        vn = ADAM_B2 * v_ref[...] + (1.0 - ADAM_B2) * (gv * gv)
        d_ref[...] = -ADAM_LR * ((mn * c1) / (jnp.sqrt(vn * c2) + ADAM_EPS) + ADAM_WD * w_ref[...])
        mo_ref[...] = mn
        vo_ref[...] = vn

    spec = pl.BlockSpec((tr, n), lambda i: (i, 0))
    shp = jax.ShapeDtypeStruct((r, n), F32)
    return _pallas(body, name=name, grid=(r // tr,), in_specs=[spec] * 4, out_specs=[spec] * 3, out_shape=[shp] * 3,
                   compiler_params=_cp("parallel"))(w, g, m, v)


PACK_W = 1024
SMALL_REPL = ("norm_pre_w", "conv_b", "ssd_norm_w", "norm_post_w")
SMALL_HEAD = ("dt_bias", "a_log", "d_skip", "attn_sinks")


def _rows(a):
    return a.reshape(-1, PACK_W)


def _head_row(vals, extra=None):
    parts = [vals[n].reshape(1, -1) for n in SMALL_HEAD]
    if extra is not None:
        parts.append(extra.reshape(1, 1))
    row = jnp.concatenate(parts, axis=1)
    return jnp.pad(row, ((0, 0), (0, PACK_W - row.shape[1])))


def _pad_rows(a, rows):
    return jnp.pad(a, ((0, rows - a.shape[0]), (0, 0)))


def _pack_repl(vals, extra=None):
    body = jnp.concatenate([_rows(vals[n]) for n in SMALL_REPL] + [_head_row(vals, extra)], axis=0)
    return _pad_rows(body, 16)


def _unpack_repl(buf):
    out, r = {}, 0
    for n, k in zip(SMALL_REPL, (2, 4, 2, 2)):
        out[n] = buf[r:r + k].reshape(1, k * PACK_W)
        r += k
    col = 0
    for n, k in zip(SMALL_HEAD, (32, 32, 32, 16)):
        out[n] = buf[r:r + 1, col:col + k]
        col += k
    return out, buf[r, col]


def kernel(x, meta_tokens, norm_pre_w, w_in, conv_w, conv_b, dt_bias, a_log, d_skip, ssd_norm_w, attn_sinks, w_out, norm_post_w, loss_target, m_meta_tokens, m_norm_pre_w, m_w_in, m_conv_w, m_conv_b, m_dt_bias, m_a_log, m_d_skip, m_ssd_norm_w, m_attn_sinks, m_w_out, m_norm_post_w, v_meta_tokens, v_norm_pre_w, v_w_in, v_conv_w, v_conv_b, v_dt_bias, v_a_log, v_d_skip, v_ssd_norm_w, v_attn_sinks, v_w_out, v_norm_post_w):
    names = ("meta_tokens", "norm_pre_w", "w_in", "conv_w", "conv_b", "dt_bias", "a_log", "d_skip", "ssd_norm_w",
             "attn_sinks", "w_out", "norm_post_w")
    w = dict(zip(names, (meta_tokens, norm_pre_w, w_in, conv_w, conv_b, dt_bias, a_log, d_skip, ssd_norm_w, attn_sinks,
                         w_out, norm_post_w)))
    m = dict(zip(names, (m_meta_tokens, m_norm_pre_w, m_w_in, m_conv_w, m_conv_b, m_dt_bias, m_a_log, m_d_skip,
                         m_ssd_norm_w, m_attn_sinks, m_w_out, m_norm_post_w)))
    v = dict(zip(names, (v_meta_tokens, v_norm_pre_w, v_w_in, v_conv_w, v_conv_b, v_dt_bias, v_a_log, v_d_skip,
                         v_ssd_norm_w, v_attn_sinks, v_w_out, v_norm_post_w)))
    cx, cy, cc = lax.axis_index("x"), lax.axis_index("y"), lax.axis_index("c")
    chip = 2 * cx + cy
    meta_cols = D_MODEL // N_SHARD
    conv_cols = D_CONV // N_SHARD

    place = jnp.stack([chip, cc]).astype(jnp.int32)
    w_re = _shards_to_re(_gather_shards(_bf(w_in[0]), "gather_w_in", GATHER_CHUNKS))
    conv_z = lax.dynamic_update_slice(jnp.zeros((CONV_WIDTH, D_CONV), F32), conv_w[0], (0, chip * conv_cols))
    meta_z = lax.dynamic_update_slice(jnp.zeros((N_META, D_MODEL), F32), meta_tokens, (0, chip * meta_cols))
    small = jnp.concatenate([_rows(conv_z), _rows(meta_z)], axis=0)
    small = _allreduce_small(jnp.where(cc == 0, small, 0.0), "gather_small")
    conv_full = small[0:16].reshape(CONV_WIDTH, D_CONV)
    meta_full = small[16:48].reshape(N_META, D_MODEL)

    loss_dev, grad_x, g = _local_step(x[0], loss_target[0], meta_full, norm_pre_w, w_re, conv_full, conv_b, dt_bias,
                                      a_log, d_skip, ssd_norm_w, attn_sinks, _bf(w_out[0]), norm_post_w, place)
    g_w_in, g_w_out = g["w_in"], g["w_out"]

    packed = jnp.concatenate([_rows(g["conv_w"]), _rows(g["meta_tokens"]), _pack_repl(g, loss_dev)], axis=0)
    red = _allreduce_small(packed, "reduce_small")
    g_conv_full = red[0:16].reshape(CONV_WIDTH, D_CONV)
    g_meta_full = red[16:48].reshape(N_META, D_MODEL)
    g_small, loss = _unpack_repl(red[48:64])
    grads = dict(g_small)
    grads["w_in"] = g_w_in
    grads["w_out"] = g_w_out
    grads["conv_w"] = lax.dynamic_slice(g_conv_full, (0, chip * conv_cols), (CONV_WIDTH, conv_cols))
    grads["meta_tokens"] = lax.dynamic_slice(g_meta_full, (0, chip * meta_cols), (N_META, meta_cols))

    upd = {}
    upd["w_in"] = [jnp.swapaxes(a, 0, 1) for a in _adamw(jnp.swapaxes(w_in[0], 0, 1), g_w_in, jnp.swapaxes(m_w_in[0], 0, 1),
                                                         jnp.swapaxes(v_w_in[0], 0, 1), "adamw_w_in")]
    grads["w_in"] = jnp.swapaxes(g_w_in, 0, 1)
    upd["w_out"] = _adamw(w_out[0], g_w_out, m_w_out[0], v_w_out[0], "adamw_w_out")

    def pack_small(vals, conv, meta):
        return jnp.concatenate([_pad_rows(conv.reshape(CONV_WIDTH, conv_cols), 8), _rows(meta), _pack_repl(vals)], axis=0)

    sm = _adamw(pack_small(w, w["conv_w"], w["meta_tokens"]), pack_small(grads, grads["conv_w"], grads["meta_tokens"]),
                pack_small(m, m["conv_w"], m["meta_tokens"]), pack_small(v, v["conv_w"], v["meta_tokens"]),
                "adamw_small")
    for n in names:
        if n not in ("w_in", "w_out"):
            upd[n] = [None, None, None]
    for k, buf in enumerate(sm):
        upd["conv_w"][k] = buf[0:CONV_WIDTH]
        upd["meta_tokens"][k] = buf[8:16].reshape(N_META, meta_cols)
        rest, _ = _unpack_repl(buf[16:32])
        for n in SMALL_REPL + SMALL_HEAD:
            upd[n][k] = rest[n]

    def shaped(n, a):
        return a.reshape(w[n].shape)

    outs = [loss, grad_x[None]]
    outs += [shaped(n, grads[n]) for n in names]
    for k in range(3):
        outs += [shaped(n, upd[n][k]) for n in names]
    return tuple(outs)
```

```python
import functools

import jax
import jax.numpy as jnp
from jax import lax
from jax.experimental import pallas as pl
from jax.experimental.pallas import tpu as pltpu

F32 = jnp.float32
BF16 = jnp.bfloat16

D_MODEL = 2048
CHUNK = 64
N_META = 16
PAD_LEAD = CHUNK - N_META
ROW0 = PAD_LEAD + N_META
EPS = 1e-6
SSD_HEADS = 32
HEAD_DIM = 64
GROUPS = 8
HPG = SSD_HEADS // GROUPS
D_STATE = 128
D_SSD = 2048
GROUP_W = D_SSD // GROUPS
CONV_WIDTH = 4
D_CONV = 4096
Q_HEADS = 16
KV_HEADS = 4
REP = Q_HEADS // KV_HEADS
D_ATT = 1024
D_KV = 256
BAND_CHUNKS = 3
ROPE_THETA = 10000.0
D_MIX = D_SSD + D_ATT
D_IN = 8736
N_SHARD = 4
W_IN_SHARD = D_IN // N_SHARD
W_OUT_SHARD = D_MIX // N_SHARD

OZ, OXS, OB, OC, OQ, OG, OK, OV, ODT = 0, 2048, 4096, 5120, 6144, 7168, 8192, 8448, 8704
DT_SLAB = 512
N_RE = ODT + DT_SLAB
LANES = 128

ADAM_LR, ADAM_B1, ADAM_B2, ADAM_EPS, ADAM_WD, ADAM_STEP = 0.001, 0.9, 0.999, 1e-08, 0.01, 10

SSD_FWD_GROUPS_PER_STEP = 4
SSD_BWD_GROUPS_PER_STEP = 8
SEG_TILE = 1024
VMEM_LIMIT = 52 * 1024 * 1024
NEG = -1e30
HI = lax.Precision.HIGHEST


def _pallas(body, **kw):
    return pl.pallas_call(body, **kw)


def _cp(*sem):
    return pltpu.CompilerParams(dimension_semantics=sem, vmem_limit_bytes=VMEM_LIMIT)


def _tile(n, cap, mult=16):
    best = None
    for d in range(mult, min(n, cap) + 1, mult):
        if n % d == 0:
            best = d
    assert best is not None, (n, cap)
    return best


def _nt(a, b):
    return lax.dot_general(a, b, (((1,), (1,)), ((), ())), preferred_element_type=F32)


def _tn(a, b):
    return lax.dot_general(a, b, (((0,), (0,)), ((), ())), preferred_element_type=F32)


def _mm(a, b):
    return jnp.dot(a, b, preferred_element_type=F32)


def _sigmoid(x):
    return 1.0 / (1.0 + jnp.exp(-x))


def _bf(x):
    return x.astype(BF16)


def _inproj(hpad, norm_w, w_re, w_out_shard):
    t, d = hpad.shape
    n = w_re.shape[1]
    tm, tn = _tile(t, 832), 1024
    ni, nj = t // tm, n // tn
    r_out, n_out = w_out_shard.shape
    kc = GATHER_CHUNKS

    def body(h_ref, nw_ref, w_ref, ws_ref, proj_ref, hn_ref, wall_ref, hn_s, send_sems, recv_sems, local_sems):
        i, j = pl.program_id(0), pl.program_id(1)
        start, forward, finish = _gather_plan(ws_ref, wall_ref, send_sems, recv_sems, local_sems, r_out // 2, kc)
        pl.when((i == 0) & (j == 0))(start)
        pl.when((i == ni // 2) & (j == 0))(forward)

        @pl.when(j == 0)
        def _():
            h = h_ref[...]
            ms = jnp.mean(h * h, axis=-1, keepdims=True)
            hn = _bf(h * lax.rsqrt(ms + EPS) * nw_ref[...])
            hn_s[...] = hn
            hn_ref[...] = hn
        proj_ref[...] = _mm(hn_s[...], w_ref[...])
        pl.when((i == ni - 1) & (j == nj - 1))(finish)

    return _pallas(
        body, name="inproj", grid=(ni, nj),
        in_specs=[pl.BlockSpec((tm, d), lambda i, j: (i, 0)), pl.BlockSpec((1, d), lambda i, j: (0, 0)),
                  pl.BlockSpec((d, tn), lambda i, j: (0, j)), ANY],
        out_specs=[pl.BlockSpec((tm, tn), lambda i, j: (i, j)), pl.BlockSpec((tm, d), lambda i, j: (i, 0)), ANY],
        out_shape=[jax.ShapeDtypeStruct((t, n), F32), jax.ShapeDtypeStruct((t, d), BF16),
                   jax.ShapeDtypeStruct((N_SHARD, r_out, n_out), w_out_shard.dtype)],
        scratch_shapes=[pltpu.VMEM((tm, d), BF16), pltpu.SemaphoreType.DMA((6 * kc,)), pltpu.SemaphoreType.DMA((6 * kc,)),
                        pltpu.SemaphoreType.DMA((2 * kc,))],
        compiler_params=_cp("arbitrary", "arbitrary"))(hpad, norm_w, w_re, w_out_shard)


def _conv_fwd(proj, conv_w, conv_b):
    t = proj.shape[0]
    tc = 256
    off = OXS // tc

    def body(x_ref, w_ref, b_ref, o_ref):
        x = x_ref[...]
        w = w_ref[...]
        row = lax.broadcasted_iota(jnp.int32, (t, tc), 0)
        u = b_ref[...] + w[3:4, :] * x
        for k in range(1, CONV_WIDTH):
            u = u + w[3 - k:4 - k, :] * jnp.where(row >= k, pltpu.roll(x, k, 0), 0.0)
        h = 0.5 * u
        o_ref[...] = h + h * jnp.tanh(h)

    return _pallas(
        body, name="conv_fwd", grid=(D_CONV // tc,),
        in_specs=[pl.BlockSpec((t, tc), lambda j: (0, j + off)), pl.BlockSpec((CONV_WIDTH, tc), lambda j: (0, j)),
                  pl.BlockSpec((1, tc), lambda j: (0, j))],
        out_specs=pl.BlockSpec((t, tc), lambda j: (0, j)),
        out_shape=jax.ShapeDtypeStruct((t, D_CONV), F32),
        compiler_params=_cp("parallel"))(proj, conv_w, conv_b)


def _softplus(u):
    e = jnp.exp(-jnp.abs(u))
    w = 1.0 + e
    l1p = jnp.where(w == 1.0, e, jnp.log(w) * (e / jnp.where(w == 1.0, 1.0, w - 1.0)))
    return jnp.maximum(u, 0.0) + l1p


def _chunks_per_step(nc):
    return max(d for d in range(1, 14) if nc % d == 0)


def _dt_prep(proj, dt_bias_l, a_log_l):
    t = proj.shape[0]
    nc = t // CHUNK
    q = CHUNK
    cps = _chunks_per_step(nc)
    rows = cps * q

    def body(raw_ref, bias_ref, alog_ref, dt_ref, acs_ref, acst_ref):
        ri = lax.broadcasted_iota(jnp.int32, (q, q), 0)
        ci = lax.broadcasted_iota(jnp.int32, (q, q), 1)
        tri = (ri >= ci).astype(F32)
        neg_a = -jnp.exp(alog_ref[...])
        for k in range(cps):
            rk = slice(q * k, q * (k + 1))
            sp = _softplus(raw_ref[rk, :] + bias_ref[...])
            row = pl.program_id(0) * rows + q * k + lax.broadcasted_iota(jnp.int32, (q, LANES), 0)
            dt = jnp.where(row >= PAD_LEAD, sp, 0.0)
            acs = jnp.dot(tri, dt * neg_a, preferred_element_type=F32, precision=HI)
            dt_ref[rk, :] = dt
            acs_ref[rk, :] = acs
            acst_ref[k] = acs.T

    return _pallas(
        body, name="dt_prep", grid=(nc // cps,),
        in_specs=[pl.BlockSpec((rows, LANES), lambda c: (c, ODT // LANES)), pl.BlockSpec((1, LANES), lambda c: (0, 0)),
                  pl.BlockSpec((1, LANES), lambda c: (0, 0))],
        out_specs=[pl.BlockSpec((rows, LANES), lambda c: (c, 0)), pl.BlockSpec((rows, LANES), lambda c: (c, 0)),
                   pl.BlockSpec((cps, LANES, q), lambda c: (c, 0, 0))],
        out_shape=[jax.ShapeDtypeStruct((t, LANES), F32), jax.ShapeDtypeStruct((t, LANES), F32),
                   jax.ShapeDtypeStruct((nc, LANES, q), F32)],
        compiler_params=_cp("parallel"))(proj, dt_bias_l, a_log_l)


def _head_cols(blk, idx):
    lane = lax.broadcasted_iota(jnp.int32, blk.shape, 1)
    return jnp.sum(jnp.where(lane == idx, blk, 0.0), axis=1, keepdims=True)


class _HeadVals:
    pass


def _lane_head(shape):
    return lax.broadcasted_iota(jnp.int32, shape, len(shape) - 1) >> 6


def _group_heads(g, gi, dtb, acsb, acst_ref, dskb):
    q = dtb.shape[0]
    hv = _HeadVals()
    lh = _lane_head((1, GROUP_W))
    hv.dt = jnp.zeros((q, GROUP_W), F32)
    hv.acs = jnp.zeros((q, GROUP_W), F32)
    hv.acs_last = jnp.zeros((1, GROUP_W), F32)
    hv.dsk = jnp.zeros((1, GROUP_W), F32)
    rows = []
    for r in range(HPG):
        idx = GROUPS * g + r
        sel = lh == r
        acs_r = acst_ref[0, GROUPS * gi + r:GROUPS * gi + r + 1, :]
        rows.append(acs_r)
        hv.dt = jnp.where(sel, _head_cols(dtb, idx), hv.dt)
        hv.acs = jnp.where(sel, _head_cols(acsb, idx), hv.acs)
        hv.acs_last = jnp.where(sel, acs_r[:, q - 1:q], hv.acs_last)
        hv.dsk = jnp.where(sel, _head_cols(dskb, idx), hv.dsk)
    hv.acs_row = jnp.concatenate(rows, axis=1)
    return hv


def _head_tri(q, lower):
    ri = lax.broadcasted_iota(jnp.int32, (q, GROUP_W), 0)
    li = lax.broadcasted_iota(jnp.int32, (q, GROUP_W), 1) & (HEAD_DIM - 1)
    return ri >= li if lower else ri <= li


def _block_diag(v):
    rb = lax.broadcasted_iota(jnp.int32, (GROUP_W, GROUP_W), 0) >> 6
    cb = lax.broadcasted_iota(jnp.int32, (GROUP_W, GROUP_W), 1) >> 6
    return jnp.where(rb == cb, jnp.concatenate([v] * HPG, axis=0), jnp.zeros((), v.dtype))


def _head_sums(v, r):
    return jnp.sum(jnp.where(_lane_head((1, GROUP_W)) == r, v, 0.0), axis=1, keepdims=True)


def _ssd_fwd(xbc, proj, dt, acs, acst, d_skip_l, ssd_norm_w):
    t = xbc.shape[0]
    q = CHUNK
    nc = t // q

    gps = SSD_FWD_GROUPS_PER_STEP
    gw, sw = gps * GROUP_W, gps * D_STATE

    def body(xs_ref, b_ref, c_ref, dt_ref, acs_ref, acst_ref, z_ref, dsk_ref, nw_ref,
             y_ref, ymix_ref, st_ref, state):
        @pl.when(pl.program_id(1) == 0)
        def _():
            state[...] = jnp.zeros_like(state)

        for gi in range(gps):
            g = gps * pl.program_id(0) + gi
            cols = slice(GROUP_W * gi, GROUP_W * (gi + 1))
            x = xs_ref[:, cols]
            bmb = _bf(b_ref[:, D_STATE * gi:D_STATE * (gi + 1)])
            cmb = _bf(c_ref[:, D_STATE * gi:D_STATE * (gi + 1)])
            hv = _group_heads(g, gi, dt_ref[...], acs_ref[...], acst_ref, dsk_ref[...])
            decay = jnp.exp(jnp.where(_head_tri(q, True), hv.acs - hv.acs_row, NEG))
            m_all = _bf(_nt(cmb, jnp.concatenate([bmb] * HPG, axis=0)) * decay)
            xdt = x * hv.dt
            s_prev = state[gi]
            st_ref[0, gi] = s_prev
            y = (_mm(m_all, _block_diag(_bf(xdt))) + _mm(cmb, _bf(s_prev)) * jnp.exp(hv.acs) + hv.dsk * x)
            state[gi] = jnp.exp(hv.acs_last) * s_prev + _tn(bmb, _bf(xdt * jnp.exp(hv.acs_last - hv.acs)))
            y_ref[:, cols] = y
            z = z_ref[:, cols]
            yg = y * (z * _sigmoid(z))
            ms = jnp.mean(yg * yg, axis=-1, keepdims=True)
            ymix_ref[:, cols] = _bf(yg * lax.rsqrt(ms + EPS) * nw_ref[:, cols])

    return _pallas(
        body, name="ssd_fwd", grid=(GROUPS // gps, nc),
        in_specs=[pl.BlockSpec((q, gw), lambda g, c: (c, g)),
                  pl.BlockSpec((q, sw), lambda g, c: (c, D_SSD // sw + g)),
                  pl.BlockSpec((q, sw), lambda g, c: (c, (D_SSD + GROUPS * D_STATE) // sw + g)),
                  pl.BlockSpec((q, LANES), lambda g, c: (c, 0)), pl.BlockSpec((q, LANES), lambda g, c: (c, 0)),
                  pl.BlockSpec((1, gps * GROUPS, q), lambda g, c: (c, g, 0)),
                  pl.BlockSpec((q, gw), lambda g, c: (c, g)),
                  pl.BlockSpec((1, LANES), lambda g, c: (0, 0)), pl.BlockSpec((1, gw), lambda g, c: (0, g))],
        out_specs=[pl.BlockSpec((q, gw), lambda g, c: (c, g)), pl.BlockSpec((q, gw), lambda g, c: (c, g)),
                   pl.BlockSpec((1, gps, D_STATE, GROUP_W), lambda g, c: (c, g, 0, 0))],
        out_shape=[jax.ShapeDtypeStruct((t, D_SSD), F32), jax.ShapeDtypeStruct((t, D_SSD), BF16),
                   jax.ShapeDtypeStruct((nc, GROUPS, D_STATE, GROUP_W), F32)],
        scratch_shapes=[pltpu.VMEM((gps, D_STATE, GROUP_W), F32)],
        compiler_params=_cp("parallel", "arbitrary"))(xbc, xbc, xbc, dt, acs, acst, proj, d_skip_l, ssd_norm_w)


def _swap_halves(v):
    lane = lax.broadcasted_iota(jnp.int32, v.shape, 1)
    return jnp.where((lane & (HEAD_DIM - 1)) < HEAD_DIM // 2, pltpu.roll(v, LANES - HEAD_DIM // 2, 1),
                     pltpu.roll(v, HEAD_DIM // 2, 1))


def _rope(qsrc, q_off, ksrc, k_off, cos_t, sin_t):
    t = qsrc.shape[0]
    tr = _tile(t, 832)
    q_scale = HEAD_DIM ** -0.5

    def body(q_ref, k_ref, cos_ref, sin_ref, qo_ref, ko_ref):
        cs = cos_ref[...]
        sn = sin_ref[...]
        for src, dst, width, scale in ((q_ref, qo_ref, D_ATT, q_scale), (k_ref, ko_ref, D_KV, 1.0)):
            for s in range(width // LANES):
                v = src[:, LANES * s:LANES * (s + 1)].astype(F32)
                dst[:, LANES * s:LANES * (s + 1)] = _bf((v * cs + _swap_halves(v) * sn) * scale)

    return _pallas(
        body, name="rope", grid=(t // tr,),
        in_specs=[pl.BlockSpec((tr, D_ATT), lambda i: (i, q_off // D_ATT)),
                  pl.BlockSpec((tr, D_KV), lambda i: (i, k_off // D_KV)),
                  pl.BlockSpec((tr, LANES), lambda i: (i, 0)), pl.BlockSpec((tr, LANES), lambda i: (i, 0))],
        out_specs=[pl.BlockSpec((tr, D_ATT), lambda i: (i, 0)), pl.BlockSpec((tr, D_KV), lambda i: (i, 0))],
        out_shape=[jax.ShapeDtypeStruct((t, D_ATT), BF16), jax.ShapeDtypeStruct((t, D_KV), BF16)],
        compiler_params=_cp("parallel"))(qsrc, ksrc, cos_t, sin_t)


def _attn_chunks_per_step(nc):
    return max(d for d in range(1, 6) if nc % d == 0)


def _band(ref, c):
    return [ref[pl.ds(pl.multiple_of(jnp.maximum(c - j, 0) * CHUNK, CHUNK), CHUNK), :] for j in (2, 1, 0)]


def _attn_probs(qh, kb, sink_col, valid):
    s = jnp.where(valid, _nt(qh, kb), NEG)
    m = jnp.maximum(jnp.max(s, axis=1, keepdims=True), sink_col)
    p = jnp.exp(s - m)
    psink = jnp.exp(sink_col - m)
    return p, psink, 1.0 / (jnp.sum(p, axis=1, keepdims=True) + psink)


def _attn_operands(c, q, k_refs, v_refs, sink_ref, h):
    qh = jnp.concatenate([q[:, HEAD_DIM * (REP * h + r):HEAD_DIM * (REP * h + r + 1)] for r in range(REP)], axis=0)
    kb = jnp.concatenate([k[:, HEAD_DIM * h:HEAD_DIM * (h + 1)] for k in k_refs], axis=0)
    vb = jnp.concatenate([_bf(v[:, HEAD_DIM * h:HEAD_DIM * (h + 1)]) for v in v_refs], axis=0)
    rows = lax.broadcasted_iota(jnp.int32, (REP * CHUNK, 1), 0) >> 6
    sink_col = jnp.zeros((REP * CHUNK, 1), F32)
    for r in range(REP):
        sink_col = jnp.where(rows == r, sink_ref[REP * h + r], sink_col)
    key_abs = (c - (BAND_CHUNKS - 1)) * CHUNK + lax.broadcasted_iota(jnp.int32, (1, BAND_CHUNKS * CHUNK), 1)
    return qh, kb, vb, sink_col, key_abs >= PAD_LEAD


def _attn_fwd(qr, kr, proj, sinks):
    t = qr.shape[0]
    nc = t // CHUNK
    cps = _attn_chunks_per_step(nc)
    rows = cps * CHUNK

    def body(q_ref, k_ref, v_ref, g_ref, sink_ref, o_ref):
        for j in range(cps):
            c = pl.program_id(0) * cps + j
            rj = slice(CHUNK * j, CHUNK * (j + 1))
            ks, vs = _band(k_ref, c), _band(v_ref, c)
            q = q_ref[rj, :]
            outs = []
            for h in range(KV_HEADS):
                qh, kb, vb, sink_col, valid = _attn_operands(c, q, ks, vs, sink_ref, h)
                p, _, inv = _attn_probs(qh, kb, sink_col, valid)
                o = _mm(_bf(p), vb) * inv
                outs += [o[CHUNK * r:CHUNK * (r + 1)] for r in range(REP)]
            att = jnp.concatenate(outs, axis=1)
            gate = g_ref[rj, :]
            o_ref[rj, :] = _bf(att * (gate * _sigmoid(gate)))

    return _pallas(
        body, name="attn_fwd", grid=(nc // cps,),
        in_specs=[pl.BlockSpec((rows, D_ATT), lambda i: (i, 0)), pl.BlockSpec((t, D_KV), lambda i: (0, 0)),
                  pl.BlockSpec((t, D_KV), lambda i: (0, OV // D_KV)),
                  pl.BlockSpec((rows, D_ATT), lambda i: (i, OG // D_ATT)), pl.BlockSpec(memory_space=pltpu.SMEM)],
        out_specs=pl.BlockSpec((rows, D_ATT), lambda i: (i, 0)),
        out_shape=jax.ShapeDtypeStruct((t, D_ATT), BF16),
        compiler_params=_cp("parallel"))(qr, kr, proj, proj, sinks)


def _outproj(ymix, amix, w_out):
    t = ymix.shape[0]
    tm, tn = _tile(t, 832), 1024

    def body(y_ref, a_ref, wy_ref, wa_ref, o_ref):
        o_ref[...] = _mm(y_ref[...], wy_ref[...]) + _mm(a_ref[...], wa_ref[...])

    return _pallas(
        body, name="outproj", grid=(t // tm, D_MODEL // tn),
        in_specs=[pl.BlockSpec((tm, D_SSD), lambda i, j: (i, 0)), pl.BlockSpec((tm, D_ATT), lambda i, j: (i, 0)),
                  pl.BlockSpec((D_SSD, tn), lambda i, j: (0, j)),
                  pl.BlockSpec((D_ATT, tn), lambda i, j: (D_SSD // D_ATT, j))],
        out_specs=pl.BlockSpec((tm, tn), lambda i, j: (i, j)),
        out_shape=jax.ShapeDtypeStruct((t, D_MODEL), F32),
        compiler_params=_cp("parallel", "parallel"))(ymix, amix, w_out, w_out)


def _post_loss(out, x, target, norm_post_w):
    t = out.shape[0]
    nc = t // CHUNK

    def body(o_ref, x_ref, tg_ref, nw_ref, dout_ref, dy_ref, loss_ref, gnw_ref):
        i = pl.program_id(0)

        @pl.when(i == 0)
        def _():
            dout_ref[...] = jnp.zeros_like(dout_ref)
            dy_ref[...] = jnp.zeros_like(dy_ref)
            loss_ref[...] = jnp.zeros_like(loss_ref)
            gnw_ref[...] = jnp.zeros_like(gnw_ref)

        @pl.when(i > 0)
        def _():
            o = o_ref[...]
            nw = nw_ref[...]
            rstd = lax.rsqrt(jnp.mean(o * o, axis=-1, keepdims=True) + EPS)
            n = o * rstd
            err = x_ref[...] + n * nw - tg_ref[...]
            loss_ref[...] += jnp.sum(err * err) * (0.5 / D_MODEL)
            dy = err * (1.0 / D_MODEL)
            dy_ref[...] = dy
            gnw_ref[...] += jnp.sum(dy * n, axis=0, keepdims=True)
            dn = dy * nw
            dout_ref[...] = _bf(rstd * (dn - n * jnp.mean(dn * n, axis=-1, keepdims=True)))

    prev = lambda i: (jnp.maximum(i - 1, 0), 0)
    return _pallas(
        body, name="post_loss", grid=(nc,),
        in_specs=[pl.BlockSpec((CHUNK, D_MODEL), lambda i: (i, 0)), pl.BlockSpec((CHUNK, D_MODEL), prev),
                  pl.BlockSpec((CHUNK, D_MODEL), prev), pl.BlockSpec((1, D_MODEL), lambda i: (0, 0))],
        out_specs=[pl.BlockSpec((CHUNK, D_MODEL), lambda i: (i, 0)), pl.BlockSpec((CHUNK, D_MODEL), lambda i: (i, 0)),
                   pl.BlockSpec((8, LANES), lambda i: (0, 0)), pl.BlockSpec((1, D_MODEL), lambda i: (0, 0))],
        out_shape=[jax.ShapeDtypeStruct((t, D_MODEL), BF16), jax.ShapeDtypeStruct((t, D_MODEL), F32),
                   jax.ShapeDtypeStruct((8, LANES), F32), jax.ShapeDtypeStruct((1, D_MODEL), F32)],
        compiler_params=_cp("arbitrary"))(out, x, target, norm_post_w)


def _nt_matmul(a, b, name):
    t, k = a.shape
    n = b.shape[0]
    tm, tn = _tile(t, 832), 1024

    def body(a_ref, b_ref, o_ref):
        o_ref[...] = _nt(a_ref[...], b_ref[...])

    return _pallas(
        body, name=name, grid=(t // tm, n // tn),
        in_specs=[pl.BlockSpec((tm, k), lambda i, j: (i, 0)), pl.BlockSpec((tn, k), lambda i, j: (j, 0))],
        out_specs=pl.BlockSpec((tm, tn), lambda i, j: (i, j)),
        out_shape=jax.ShapeDtypeStruct((t, n), F32),
        compiler_params=_cp("parallel", "parallel"))(a, b)


def _tn_matmul(a, b, name):
    t, m = a.shape
    n = b.shape[1]
    tk, tm, tn = _tile(t, 832), min(m, 1024), min(n, 2048)
    nk = t // tk

    def body(a_ref, b_ref, o_ref):
        @pl.when(pl.program_id(2) == 0)
        def _():
            o_ref[...] = jnp.zeros_like(o_ref)
        o_ref[...] += _tn(a_ref[...], b_ref[...])

    return _pallas(
        body, name=name, grid=(m // tm, n // tn, nk),
        in_specs=[pl.BlockSpec((tk, tm), lambda i, j, k: (k, i)), pl.BlockSpec((tk, tn), lambda i, j, k: (k, j))],
        out_specs=pl.BlockSpec((tm, tn), lambda i, j, k: (i, j)),
        out_shape=jax.ShapeDtypeStruct((m, n), F32),
        compiler_params=_cp("parallel", "parallel", "arbitrary"))(a, b)


def _attn_bwd(qr, kr, proj, dmix, sinks, ga):
    t = qr.shape[0]
    nc = t // CHUNK
    cps = _attn_chunks_per_step(nc)
    nsteps = nc // cps
    rows_step = cps * CHUNK

    def body(q_ref, k_ref, v_ref, g_ref, da_ref, sink_ref, ga_ref, dq_ref, dg_ref, dk_ref, dv_ref, gs_ref,
             got_ref, send_sems, recv_sems):
        step = pl.program_id(0)

        @pl.when(step == 0)
        def _():
            for cp in _exchange_copies(ga_ref, got_ref, send_sems, recv_sems):
                cp.start()
            dk_ref[...] = jnp.zeros_like(dk_ref)
            dv_ref[...] = jnp.zeros_like(dv_ref)
            gs_ref[...] = jnp.zeros_like(gs_ref)

        lane = lax.broadcasted_iota(jnp.int32, (1, LANES), 1)
        rows = lax.broadcasted_iota(jnp.int32, (REP * CHUNK, 1), 0) >> 6
        gs = jnp.zeros((1, LANES), F32)
        dk_parts = [[] for _ in range(cps + BAND_CHUNKS - 1)]
        dv_parts = [[] for _ in range(cps + BAND_CHUNKS - 1)]
        for j in range(cps):
            c = step * cps + j
            rj = slice(CHUNK * j, CHUNK * (j + 1))
            ks, vs = _band(k_ref, c), _band(v_ref, c)
            q = q_ref[rj, :]
            gate = g_ref[rj, :]
            sg = _sigmoid(gate)
            da = da_ref[rj, :]
            datt = da * (gate * sg)
            dqs, atts, dks, dvs = [], [], [], []
            for h in range(KV_HEADS):
                qh, kb, vb, sink_col, valid = _attn_operands(c, q, ks, vs, sink_ref, h)
                p, psink, inv = _attn_probs(qh, kb, sink_col, valid)
                pb = _bf(p)
                o = _mm(pb, vb) * inv
                do = jnp.concatenate([datt[:, HEAD_DIM * (REP * h + r):HEAD_DIM * (REP * h + r + 1)]
                                      for r in range(REP)], axis=0)
                dob = _bf(do * inv)
                delta = jnp.sum(do * o, axis=1, keepdims=True) * inv
                ds = _bf(p * (_nt(dob, vb) - delta))
                gsink = -psink * delta
                for r in range(REP):
                    gs = gs + jnp.where(lane == REP * h + r, jnp.sum(jnp.where(rows == r, gsink, 0.0)), 0.0)
                dqh = _mm(ds, kb)
                dqs += [dqh[CHUNK * r:CHUNK * (r + 1)] for r in range(REP)]
                atts += [o[CHUNK * r:CHUNK * (r + 1)] for r in range(REP)]
                dks.append(_tn(ds, qh))
                dvs.append(_tn(pb, dob))
            dq_ref[rj, :] = jnp.concatenate(dqs, axis=1)
            att = jnp.concatenate(atts, axis=1)
            dg_ref[rj, :] = _bf(da * att * (sg * (1.0 + gate * (1.0 - sg))))
            dkf = jnp.concatenate(dks, axis=1)
            dvf = jnp.concatenate(dvs, axis=1)
            for b in range(BAND_CHUNKS):
                dk_parts[j + b].append(dkf[CHUNK * b:CHUNK * (b + 1)])
                dv_parts[j + b].append(dvf[CHUNK * b:CHUNK * (b + 1)])
        gs_ref[0:1, :] += gs
        for rel in range(cps + BAND_CHUNKS - 1):
            r0 = pl.multiple_of(jnp.maximum(step * cps - (BAND_CHUNKS - 1) + rel, 0) * CHUNK, CHUNK)
            dk_ref[pl.ds(r0, CHUNK), :] += sum(dk_parts[rel][1:], dk_parts[rel][0])
            dv_ref[pl.ds(r0, CHUNK), :] += sum(dv_parts[rel][1:], dv_parts[rel][0])

        @pl.when(step == nsteps - 1)
        def _():
            for cp in _exchange_copies(ga_ref, got_ref, send_sems, recv_sems):
                cp.wait()

    return _pallas(
        body, name="attn_bwd", grid=(nsteps,),
        in_specs=[pl.BlockSpec((rows_step, D_ATT), lambda i: (i, 0)), pl.BlockSpec((t, D_KV), lambda i: (0, 0)),
                  pl.BlockSpec((t, D_KV), lambda i: (0, OV // D_KV)),
                  pl.BlockSpec((rows_step, D_ATT), lambda i: (i, OG // D_ATT)),
                  pl.BlockSpec((rows_step, D_ATT), lambda i: (i, D_SSD // D_ATT)),
                  pl.BlockSpec(memory_space=pltpu.SMEM), ANY],
        out_specs=[pl.BlockSpec((rows_step, D_ATT), lambda i: (i, 0)), pl.BlockSpec((rows_step, D_ATT), lambda i: (i, 0)),
                   pl.BlockSpec((t, D_KV), lambda i: (0, 0)), pl.BlockSpec((t, D_KV), lambda i: (0, 0)),
                   pl.BlockSpec((8, LANES), lambda i: (0, 0)), ANY],
        out_shape=[jax.ShapeDtypeStruct((t, D_ATT), F32), jax.ShapeDtypeStruct((t, D_ATT), BF16),
                   jax.ShapeDtypeStruct((t, D_KV), F32), jax.ShapeDtypeStruct((t, D_KV), F32),
                   jax.ShapeDtypeStruct((8, LANES), F32), _exchange_shape(ga)],
        scratch_shapes=_exchange_scratch(),
        compiler_params=_cp("arbitrary"))(qr, kr, proj, proj, dmix, sinks, ga)


def _ssd_bwd(dmix, y_ssd, xbc, proj, dt, acs, acst, states, d_skip_l, ssd_norm_w):
    t = xbc.shape[0]
    q = CHUNK
    nc = t // q
    gps = SSD_BWD_GROUPS_PER_STEP
    gw, sw = gps * GROUP_W, gps * D_STATE

    def body(dmix_ref, y_ref, z_ref, nw_ref, xs_ref, b_ref, c_ref, dt_ref, acs_ref, acst_ref, st_ref, dsk_ref,
             dz_ref, dxs_ref, db_ref, dc_ref, dacs_ref, ddt_ref, gnw_ref, gdsk_ref, dstate):
        @pl.when(pl.program_id(1) == 0)
        def _():
            dstate[...] = jnp.zeros_like(dstate)
            gnw_ref[...] = jnp.zeros_like(gnw_ref)
            gdsk_ref[...] = jnp.zeros_like(gdsk_ref)

        last_row = lax.broadcasted_iota(jnp.int32, (q, 1), 0) == q - 1
        lane = lax.broadcasted_iota(jnp.int32, (q, LANES), 1)
        lane1 = lax.broadcasted_iota(jnp.int32, (8, LANES), 1)
        for gi in range(gps):
            g = gps * pl.program_id(0) + gi
            cols = slice(GROUP_W * gi, GROUP_W * (gi + 1))
            scols = slice(D_STATE * gi, D_STATE * (gi + 1))
            y = y_ref[:, cols]
            z = z_ref[:, cols]
            sz = _sigmoid(z)
            silu_z = z * sz
            yg = y * silu_z
            rstd = lax.rsqrt(jnp.mean(yg * yg, axis=-1, keepdims=True) + EPS)
            n = yg * rstd
            dout = dmix_ref[:, cols]
            gnw_ref[:, cols] += jnp.sum(dout * n, axis=0, keepdims=True)
            dn = dout * nw_ref[:, cols]
            dyg = rstd * (dn - n * jnp.mean(dn * n, axis=-1, keepdims=True))
            dy = dyg * silu_z
            dz_ref[:, cols] = _bf(dyg * y * (sz * (1.0 + z * (1.0 - sz))))

            x = xs_ref[:, cols]
            bmb, cmb = _bf(b_ref[:, scols]), _bf(c_ref[:, scols])
            hv = _group_heads(g, gi, dt_ref[...], acs_ref[...], acst_ref, dsk_ref[...])
            dec = jnp.exp(jnp.where(_head_tri(q, True), hv.acs - hv.acs_row, NEG))
            dect = jnp.exp(jnp.where(_head_tri(q, False), hv.acs_row - hv.acs, NEG))
            b4 = jnp.concatenate([bmb] * HPG, axis=0)
            c4 = jnp.concatenate([cmb] * HPG, axis=0)
            m_all = _nt(cmb, b4) * dec
            mt_all = _nt(bmb, c4) * dect
            xdt = x * hv.dt
            xdt_b, dyb = _bf(xdt), _bf(dy)
            x_bd, dy_bd = _block_diag(xdt_b), _block_diag(dyb)
            s_prev = st_ref[0, gi]
            spb = _bf(s_prev)
            ds_new = dstate[gi]
            dsb = _bf(ds_new)
            e = jnp.exp(hv.acs)
            elast = jnp.exp(hv.acs_last)
            dte = jnp.exp(hv.acs_last - hv.acs)
            bds = _mm(bmb, dsb)
            dxdt = _mm(_bf(mt_all), dy_bd) + bds * dte
            dm = _nt(dyb, x_bd)
            dmt = _nt(xdt_b, dy_bd)
            dye = _bf(dy * e)
            dc_ref[:, scols] = _mm(_bf(dm * dec), b4) + _nt(dye, spb)
            db_ref[:, scols] = _mm(_bf(dmt * dect), c4) + _nt(_bf(xdt * dte), dsb)
            dstate[gi] = elast * ds_new + _tn(cmb, dye)
            dxs_ref[:, cols] = dxdt * hv.dt + hv.dsk * dy
            ddte_dte = bds * xdt * dte
            dacs_l = dm * m_all - dmt * mt_all + dy * _mm(cmb, spb) * e - ddte_dte
            dlast_l = (jnp.sum(ddte_dte, axis=0, keepdims=True)
                       + jnp.sum(s_prev * ds_new, axis=0, keepdims=True) * elast)
            ddt_l = dxdt * x
            gdsk_l = jnp.sum(dy * x, axis=0, keepdims=True)
            dacs_out = jnp.zeros((q, LANES), F32)
            ddt_out = jnp.zeros((q, LANES), F32)
            gdsk = jnp.zeros((8, LANES), F32)
            for r in range(HPG):
                dacs = _head_sums(dacs_l, r) + jnp.where(last_row, _head_sums(dlast_l, r), 0.0)
                dacs_out = jnp.where(lane == r, dacs, dacs_out)
                ddt_out = jnp.where(lane == r, _head_sums(ddt_l, r), ddt_out)
                gdsk = gdsk + jnp.where(lane1 == r, _head_sums(gdsk_l, r), 0.0)
            dacs_ref[:, LANES * gi:LANES * (gi + 1)] = dacs_out
            ddt_ref[:, LANES * gi:LANES * (gi + 1)] = ddt_out
            gdsk_ref[gi] += gdsk

    rev = lambda c: nc - 1 - c
    wide = pl.BlockSpec((q, gw), lambda g, c: (rev(c), g))
    return _pallas(
        body, name="ssd_bwd", grid=(GROUPS // gps, nc),
        in_specs=[wide, wide, wide, pl.BlockSpec((1, gw), lambda g, c: (0, g)), wide,
                  pl.BlockSpec((q, sw), lambda g, c: (rev(c), D_SSD // sw + g)),
                  pl.BlockSpec((q, sw), lambda g, c: (rev(c), (D_SSD + GROUPS * D_STATE) // sw + g)),
                  pl.BlockSpec((q, LANES), lambda g, c: (rev(c), 0)), pl.BlockSpec((q, LANES), lambda g, c: (rev(c), 0)),
                  pl.BlockSpec((1, gps * GROUPS, q), lambda g, c: (rev(c), g, 0)),
                  pl.BlockSpec((1, gps, D_STATE, GROUP_W), lambda g, c: (rev(c), g, 0, 0)),
                  pl.BlockSpec((1, LANES), lambda g, c: (0, 0))],
        out_specs=[wide, wide,
                   pl.BlockSpec((q, sw), lambda g, c: (rev(c), g)), pl.BlockSpec((q, sw), lambda g, c: (rev(c), g)),
                   pl.BlockSpec((q, gps * LANES), lambda g, c: (rev(c), g)),
                   pl.BlockSpec((q, gps * LANES), lambda g, c: (rev(c), g)),
                   pl.BlockSpec((1, gw), lambda g, c: (0, g)), pl.BlockSpec((gps, 8, LANES), lambda g, c: (g, 0, 0))],
        out_shape=[jax.ShapeDtypeStruct((t, D_SSD), BF16), jax.ShapeDtypeStruct((t, D_SSD), F32),
                   jax.ShapeDtypeStruct((t, GROUPS * D_STATE), F32), jax.ShapeDtypeStruct((t, GROUPS * D_STATE), F32),
                   jax.ShapeDtypeStruct((t, GROUPS * LANES), F32), jax.ShapeDtypeStruct((t, GROUPS * LANES), F32),
                   jax.ShapeDtypeStruct((1, D_SSD), F32), jax.ShapeDtypeStruct((GROUPS, 8, LANES), F32)],
        scratch_shapes=[pltpu.VMEM((gps, D_STATE, GROUP_W), F32)],
        compiler_params=_cp("parallel", "arbitrary"))(dmix, y_ssd, proj, ssd_norm_w, xbc, xbc, xbc, dt, acs, acst,
                                                      states, d_skip_l)


def _dt_bwd(dacs_g, ddt_g, dt, proj, dt_bias_l, a_log_l):
    t = dt.shape[0]
    q = CHUNK
    nc = t // q
    cps = _chunks_per_step(nc)
    rows = cps * q

    def body(dacs_ref, ddt_ref, dt_ref, raw_ref, bias_ref, alog_ref, draw_ref, ga_ref, gb_ref):
        @pl.when(pl.program_id(0) == 0)
        def _():
            ga_ref[...] = jnp.zeros_like(ga_ref)
            gb_ref[...] = jnp.zeros_like(gb_ref)

        lane = lax.broadcasted_iota(jnp.int32, (q, LANES), 1)
        ri = lax.broadcasted_iota(jnp.int32, (q, q), 0)
        ci = lax.broadcasted_iota(jnp.int32, (q, q), 1)
        triu = (ri <= ci).astype(F32)
        a = -jnp.exp(alog_ref[...])
        used = (lane & (GROUPS - 1)) < HPG
        ga = jnp.zeros((1, LANES), F32)
        gb = jnp.zeros((1, LANES), F32)
        for k in range(cps):
            rk = slice(q * k, q * (k + 1))
            dacs = jnp.zeros((q, LANES), F32)
            ddt = jnp.zeros((q, LANES), F32)
            for g in range(GROUPS):
                mask = (lane >= GROUPS * g) & (lane < GROUPS * g + HPG)
                sl = slice(LANES * g, LANES * (g + 1))
                if g == 0:
                    dacs = jnp.where(mask, dacs_ref[rk, sl], dacs)
                    ddt = jnp.where(mask, ddt_ref[rk, sl], ddt)
                else:
                    dacs = jnp.where(mask, pltpu.roll(dacs_ref[rk, sl], GROUPS * g, 1), dacs)
                    ddt = jnp.where(mask, pltpu.roll(ddt_ref[rk, sl], GROUPS * g, 1), ddt)
            dda = jnp.dot(triu, dacs, preferred_element_type=F32, precision=HI)
            row = pl.program_id(0) * rows + q * k + lax.broadcasted_iota(jnp.int32, (q, LANES), 0)
            dsp = jnp.where((row >= PAD_LEAD) & used, dda * a + ddt, 0.0)
            draw = dsp * _sigmoid(raw_ref[rk, :] + bias_ref[...])
            draw_ref[rk, :] = _bf(draw)
            gb = gb + jnp.sum(draw, axis=0, keepdims=True)
            ga = ga + jnp.sum(jnp.where(used, dda * dt_ref[rk, :], 0.0), axis=0, keepdims=True)
        gb_ref[0:1, :] += gb
        ga_ref[0:1, :] += ga * a

    return _pallas(
        body, name="dt_bwd", grid=(nc // cps,),
        in_specs=[pl.BlockSpec((rows, GROUPS * LANES), lambda c: (c, 0)),
                  pl.BlockSpec((rows, GROUPS * LANES), lambda c: (c, 0)),
                  pl.BlockSpec((rows, LANES), lambda c: (c, 0)), pl.BlockSpec((rows, LANES), lambda c: (c, ODT // LANES)),
                  pl.BlockSpec((1, LANES), lambda c: (0, 0)), pl.BlockSpec((1, LANES), lambda c: (0, 0))],
        out_specs=[pl.BlockSpec((rows, LANES), lambda c: (c, 0)), pl.BlockSpec((8, LANES), lambda c: (0, 0)),
                   pl.BlockSpec((8, LANES), lambda c: (0, 0))],
        out_shape=[jax.ShapeDtypeStruct((t, LANES), BF16), jax.ShapeDtypeStruct((8, LANES), F32),
                   jax.ShapeDtypeStruct((8, LANES), F32)],
        compiler_params=_cp("arbitrary"))(dacs_g, ddt_g, dt, proj, dt_bias_l, a_log_l)


def _conv_bwd(dseg, proj, conv_w, conv_b, col_off, name):
    t, width = dseg.shape
    tc = 128
    off_p = (OXS + col_off) // tc
    off_w = col_off // tc

    def body(d_ref, x_ref, w_ref, b_ref, dx_ref, gw_ref, gb_ref, xp, dup):
        xp[0:8, :] = jnp.zeros((8, tc), F32)
        xp[8:t + 8, :] = x_ref[...]
        w = w_ref[...]
        u = (b_ref[...] + w[3:4, :] * xp[8:t + 8, :] + w[2:3, :] * xp[7:t + 7, :]
             + w[1:2, :] * xp[6:t + 6, :] + w[0:1, :] * xp[5:t + 5, :])
        su = _sigmoid(u)
        du = d_ref[...] * (su * (1.0 + u * (1.0 - su)))
        dup[0:t, :] = du
        dup[t:t + 8, :] = jnp.zeros((8, tc), F32)
        dx_ref[...] = _bf(w[3:4, :] * du + w[2:3, :] * dup[1:t + 1, :] + w[1:2, :] * dup[2:t + 2, :]
                          + w[0:1, :] * dup[3:t + 3, :])
        gb_ref[...] = jnp.sum(du, axis=0, keepdims=True)
        gw_ref[...] = jnp.concatenate(
            [jnp.sum(du * xp[5 + k:t + 5 + k, :], axis=0, keepdims=True) for k in range(CONV_WIDTH)], axis=0)

    return _pallas(
        body, name=name, grid=(width // tc,),
        in_specs=[pl.BlockSpec((t, tc), lambda j: (0, j)), pl.BlockSpec((t, tc), lambda j: (0, j + off_p)),
                  pl.BlockSpec((CONV_WIDTH, tc), lambda j: (0, j + off_w)), pl.BlockSpec((1, tc), lambda j: (0, j + off_w))],
        out_specs=[pl.BlockSpec((t, tc), lambda j: (0, j)), pl.BlockSpec((CONV_WIDTH, tc), lambda j: (0, j)),
                   pl.BlockSpec((1, tc), lambda j: (0, j))],
        out_shape=[jax.ShapeDtypeStruct((t, width), BF16), jax.ShapeDtypeStruct((CONV_WIDTH, width), F32),
                   jax.ShapeDtypeStruct((1, width), F32)],
        scratch_shapes=[pltpu.VMEM((t + 8, tc), F32), pltpu.VMEM((t + 8, tc), F32)],
        compiler_params=_cp("parallel"))(dseg, proj, conv_w, conv_b)


def _dinproj(segs, w_re, hpad, norm_w, dy_t, ga):
    t = segs[0].shape[0]
    d = hpad.shape[1]
    tm, tk = _tile(t, 416), SEG_TILE
    counts = [s.shape[1] // tk for s in segs]
    firsts = [sum(counts[:s]) for s in range(len(segs))]
    nk = sum(counts)
    assert nk * tk == w_re.shape[1]
    ni = t // tm
    ns = len(segs)

    def body(*refs):
        seg_refs = refs[:ns]
        w_ref, h_ref, nw_ref, dy_ref, ga_ref, dh_ref, gnw_ref, got_ref, acc, send_sems, recv_sems = refs[ns:]
        i, k = pl.program_id(0), pl.program_id(1)

        @pl.when((i == 0) & (k == 0))
        def _():
            for cp in _exchange_copies(ga_ref, got_ref, send_sems, recv_sems):
                cp.start()
            gnw_ref[...] = jnp.zeros_like(gnw_ref)

        @pl.when(k == 0)
        def _():
            acc[...] = jnp.zeros_like(acc)

        for s in range(ns):
            @pl.when((k >= firsts[s]) & (k < firsts[s] + counts[s]))
            def _(s=s):
                acc[...] += _nt(seg_refs[s][...], w_ref[...])

        @pl.when(k == nk - 1)
        def _():
            h = h_ref[...]
            rstd = lax.rsqrt(jnp.mean(h * h, axis=-1, keepdims=True) + EPS)
            nrm = h * rstd
            dhn = acc[...]
            gnw_ref[...] += jnp.sum(dhn * nrm, axis=0, keepdims=True)
            dn = dhn * nw_ref[...]
            dh_ref[...] = rstd * (dn - nrm * jnp.mean(dn * nrm, axis=-1, keepdims=True)) + dy_ref[...]

        @pl.when((i == ni - 1) & (k == nk - 1))
        def _():
            for cp in _exchange_copies(ga_ref, got_ref, send_sems, recv_sems):
                cp.wait()

    seg_specs = [pl.BlockSpec((tm, tk), functools.partial(lambda i, k, f0, n0: (i, jnp.clip(k - f0, 0, n0 - 1)),
                                                          f0=firsts[s], n0=counts[s])) for s in range(ns)]
    return _pallas(
        body, name="dinproj", grid=(ni, nk),
        in_specs=seg_specs + [pl.BlockSpec((d, tk), lambda i, k: (0, k)),
                              pl.BlockSpec((tm, d), lambda i, k: (i, 0)), pl.BlockSpec((1, d), lambda i, k: (0, 0)),
                              pl.BlockSpec((tm, d), lambda i, k: (i, 0)), ANY],
        out_specs=[pl.BlockSpec((tm, d), lambda i, k: (i, 0)), pl.BlockSpec((1, d), lambda i, k: (0, 0)), ANY],
        out_shape=[jax.ShapeDtypeStruct((t, d), F32), jax.ShapeDtypeStruct((1, d), F32), _exchange_shape(ga)],
        scratch_shapes=[pltpu.VMEM((tm, d), F32)] + _exchange_scratch(),
        compiler_params=_cp("arbitrary", "arbitrary"))(*segs, w_re, hpad, norm_w, dy_t, ga)


def _spread_heads(v):
    v = jnp.pad(v.reshape(GROUPS, HPG), ((0, 0), (0, GROUPS - HPG))).reshape(1, GROUPS * GROUPS)
    return jnp.pad(v, ((0, 0), (0, LANES - GROUPS * GROUPS)))


def _gather_heads(v):
    return v[0:1, :GROUPS * GROUPS].reshape(GROUPS, GROUPS)[:, :HPG].reshape(1, SSD_HEADS)


def _rope_tables(t):
    half = HEAD_DIM // 2
    inv = ROPE_THETA ** (-jnp.arange(half, dtype=F32) / half)
    pos = (jnp.arange(t) - PAD_LEAD).astype(F32)
    ang = pos[:, None] * inv[None, :]
    cos, sin = jnp.cos(ang), jnp.sin(ang)
    cos_t = jnp.concatenate([cos, cos, cos, cos], axis=1)
    sin_t = jnp.concatenate([-sin, sin, -sin, sin], axis=1)
    return cos_t, sin_t


def _column_pieces():
    runs = [(0, OB + 2 * GROUPS * D_STATE, 0)]
    o = OB + 2 * GROUPS * D_STATE
    runs += [(o + HPG * g, HPG, ODT + GROUPS * g) for g in range(GROUPS)]
    o += SSD_HEADS
    for width, dst in ((D_ATT, OQ), (D_KV, OK), (D_KV, OV), (D_ATT, OG)):
        runs.append((o, width, dst))
        o += width
    assert o == D_IN
    pieces = []
    for o0, width, dst in runs:
        for j in range(N_SHARD):
            lo, hi = max(o0, W_IN_SHARD * j), min(o0 + width, W_IN_SHARD * (j + 1))
            if lo < hi:
                pieces.append((j, lo - W_IN_SHARD * j, hi - W_IN_SHARD * j, dst + lo - o0))
    return pieces


def _shards_to_re(w_all):
    _, k, _ = w_all.shape
    tr = 256

    def body(x_ref, o_ref):
        o_ref[:, ODT:ODT + DT_SLAB] = jnp.zeros((tr, DT_SLAB), o_ref.dtype)
        for j, c0, c1, d0 in _column_pieces():
            o_ref[:, d0:d0 + c1 - c0] = x_ref[j, :, c0:c1]

    return _pallas(body, name="shards_to_re", grid=(k // tr,),
                   in_specs=[pl.BlockSpec((N_SHARD, tr, W_IN_SHARD), lambda i: (0, i, 0))],
                   out_specs=pl.BlockSpec((tr, N_RE), lambda i: (i, 0)),
                   out_shape=jax.ShapeDtypeStruct((k, N_RE), w_all.dtype), compiler_params=_cp("parallel"))(w_all)


def _pair_add_to_shards(parts, got, pieces, shard_rows, core, name):
    n = parts[0].shape[1]
    hn = n // 2
    tc = 128
    nt = hn // tc
    ns = len(parts)
    starts = [sum(p.shape[0] for p in parts[:s]) for s in range(ns)]
    moves = []
    for j, c0, c1, d0 in pieces:
        for s, p in enumerate(parts):
            lo, hi = max(d0, starts[s]), min(d0 + c1 - c0, starts[s] + p.shape[0])
            if lo < hi:
                moves.append((s, lo - starts[s], j, c0 + lo - d0, hi - lo))
    assert sum(m[4] for m in moves) == N_SHARD * shard_rows

    def body(core_ref, *refs):
        own, theirs, o_ref, acc = refs[:ns], refs[ns:2 * ns], refs[2 * ns], refs[2 * ns + 1]
        for s, r0, j, c0, rows in moves:
            acc[j, c0:c0 + rows, :] = own[s][r0:r0 + rows, :] + theirs[s][r0:r0 + rows, :]
        o_ref[...] = _bf(acc[...])

    return _pallas(
        body, name=name,
        grid_spec=pltpu.PrefetchScalarGridSpec(
            num_scalar_prefetch=1, grid=(nt,),
            in_specs=[pl.BlockSpec((p.shape[0], tc), lambda i, core_ref: (0, core_ref[0] * nt + i)) for p in parts]
            + [pl.BlockSpec((p.shape[0], tc), lambda i, core_ref: (0, i)) for p in parts],
            out_specs=pl.BlockSpec((N_SHARD, shard_rows, tc), lambda i, core_ref: (0, 0, i)),
            scratch_shapes=[pltpu.VMEM((N_SHARD, shard_rows, tc), F32)]),
        out_shape=jax.ShapeDtypeStruct((N_SHARD, shard_rows, hn), BF16),
        compiler_params=_cp("parallel"))(core, *parts, *got)


def _local_step(x, target, meta, norm_pre_w, w_re, conv_w, conv_b, dt_bias, a_log, d_skip, ssd_norm_w, sinks,
                w_out_shard, norm_post_w, place):
    seq = x.shape[0]
    t = PAD_LEAD + N_META + seq
    hpad = jnp.concatenate([jnp.zeros((PAD_LEAD, D_MODEL), F32), meta, x], axis=0)
    dt_bias_l, a_log_l, d_skip_l = _spread_heads(dt_bias), _spread_heads(a_log), _spread_heads(d_skip)
    cos_t, sin_t = _rope_tables(t)
    sink_v = sinks.reshape(Q_HEADS)

    proj, hn, w_out_all = _inproj(hpad, norm_pre_w, w_re, w_out_shard)
    w_out = w_out_all.reshape(D_MIX, D_MODEL)
    xbc = _conv_fwd(proj, conv_w, conv_b)
    dt, acs, acst = _dt_prep(proj, dt_bias_l, a_log_l)
    y_ssd, ymix, states = _ssd_fwd(xbc, proj, dt, acs, acst, d_skip_l, ssd_norm_w)
    qr, kr = _rope(proj, OQ, proj, OK, cos_t, sin_t)
    amix = _attn_fwd(qr, kr, proj, sink_v)
    out = _outproj(ymix, amix, w_out)
    dout, dy_t, loss_blk, g_norm_post = _post_loss(out, x, target, norm_post_w)

    dmix = _nt_matmul(dout, w_out, "dmix")
    g_out_parts = [_tn_matmul(ymix, dout, "gw_out_y"), _tn_matmul(amix, dout, "gw_out_a")]
    ga_out = _reduce_pair(g_out_parts, [(j, 0, W_OUT_SHARD, W_OUT_SHARD * j) for j in range(N_SHARD)], W_OUT_SHARD,
                          place, "gw_out")
    dq_r, dg, dk_r, dv, gs, slabs_out = _attn_bwd(qr, kr, proj, dmix, sink_v, ga_out)
    g_w_out = _reduce_finish(ga_out, slabs_out, place, "gw_out")
    dq, dk = _rope(dq_r, 0, dk_r, 0, cos_t, -sin_t)
    dz, dxs, db, dc, dacs_g, ddt_g, g_ssd_norm, gdsk = _ssd_bwd(dmix, y_ssd, xbc, proj, dt, acs, acst, states,
                                                                d_skip_l, ssd_norm_w)
    draw, ga, gb = _dt_bwd(dacs_g, ddt_g, dt, proj, dt_bias_l, a_log_l)
    dxs_p, gcw0, gcb0 = _conv_bwd(dxs, proj, conv_w, conv_b, 0, "conv_bwd_x")
    db_p, gcw1, gcb1 = _conv_bwd(db, proj, conv_w, conv_b, D_SSD, "conv_bwd_b")
    dc_p, gcw2, gcb2 = _conv_bwd(dc, proj, conv_w, conv_b, D_SSD + GROUPS * D_STATE, "conv_bwd_c")
    tail = jnp.concatenate([dk, _bf(dv), draw, jnp.zeros((t, DT_SLAB - LANES), BF16)], axis=1)
    segs = [dz, dxs_p, db_p, dc_p, dq, dg, tail]
    g_parts = [_tn_matmul(seg, hn, "gw_in_%d" % s) for s, seg in enumerate(segs)]
    ga_in = _reduce_pair(g_parts, _column_pieces(), W_IN_SHARD, place, "gw_in")
    dh, g_norm_pre, slabs_in = _dinproj(segs, w_re, hpad, norm_pre_w, dy_t, ga_in)
    g_w_in = _reduce_finish(ga_in, slabs_in, place, "gw_in")

    gdsk_l = jnp.concatenate([gdsk[g, 0:1, 0:GROUPS] for g in range(GROUPS)], axis=1)
    gdsk_l = jnp.pad(gdsk_l, ((0, 0), (0, LANES - GROUPS * GROUPS)))
    grads = dict(
        meta_tokens=dh[PAD_LEAD:ROW0], norm_pre_w=g_norm_pre, w_in=g_w_in,
        conv_w=jnp.concatenate([gcw0, gcw1, gcw2], axis=1), conv_b=jnp.concatenate([gcb0, gcb1, gcb2], axis=1),
        dt_bias=_gather_heads(gb), a_log=_gather_heads(ga), d_skip=_gather_heads(gdsk_l), ssd_norm_w=g_ssd_norm,
        attn_sinks=gs[0:1, :Q_HEADS], w_out=g_w_out, norm_post_w=g_norm_post)
    return loss_blk[0, 0], dh[ROW0:], grads


ANY = pl.BlockSpec(memory_space=pl.ANY)
MESH = pl.DeviceIdType.MESH
GATHER_CHUNKS = 4
PAIR_CHUNKS = 8
JOIN_CHUNKS = 8


def _rcopy(src, dst, ssem, rsem, dev):
    return pltpu.make_async_remote_copy(src_ref=src, dst_ref=dst, send_sem=ssem, recv_sem=rsem, device_id=dev,
                                        device_id_type=MESH)


def _place():
    x, y, c = lax.axis_index("x"), lax.axis_index("y"), lax.axis_index("c")
    chips = [(1 - x, y), (x, 1 - y), (1 - x, 1 - y)]
    return x, y, c, chips


def _gather_plan(x_ref, out_ref, send_sems, recv_sems, local_sems, hr, kc):
    ch = hr // kc
    assert ch * kc == hr and ch % 16 == 0
    x, y, c, chips = _place()
    me = 2 * x + y
    sibling = (x, y, 1 - c)

    def piece(chip, hc, k):
        return out_ref.at[chip, pl.ds(hc * hr + k * ch, ch), :]

    def local():
        return [pltpu.make_async_copy(x_ref.at[pl.ds(k * ch, ch), :], out_ref.at[me, pl.ds(k * ch, ch), :],
                                      local_sems.at[k]) for k in range(2 * kc)]

    def first():
        return [_rcopy(x_ref.at[pl.ds(c * hr + k * ch, ch), :], piece(me, c, k), send_sems.at[j * kc + k],
                       recv_sems.at[j * kc + k], (*chip, c)) for j, chip in enumerate(chips) for k in range(kc)]

    def passed(hc):
        return [_rcopy(piece(2 * chip[0] + chip[1], hc, k), piece(2 * chip[0] + chip[1], hc, k),
                       send_sems.at[(3 + j) * kc + k], recv_sems.at[(3 + j) * kc + k], sibling)
                for j, chip in enumerate(chips) for k in range(kc)]

    def arrivals():
        return [_rcopy(piece(2 * chip[0] + chip[1], c, k), piece(2 * chip[0] + chip[1], c, k), send_sems.at[j * kc + k],
                       recv_sems.at[j * kc + k], (*chip, c)) for j, chip in enumerate(chips) for k in range(kc)]

    def start():
        for cp in local() + first():
            cp.start()

    def forward():
        for arrived in arrivals():
            arrived.wait_recv()
        for fw in passed(c):
            fw.start()

    def finish():
        for cp in passed(1 - c):
            cp.wait_recv()
        for cp in first() + passed(c):
            cp.wait_send()
        for cp in local():
            cp.wait()

    return start, forward, finish


def _gather_shards(shard, name, kc):
    r, n = shard.shape
    hr = r // 2
    qr = hr // 2
    ch = qr // kc
    assert ch * kc == qr and ch % 16 == 0
    nflow = 6

    def pieces_of(out_ref, c):
        def piece(chip, hc, part, k):
            return out_ref.at[chip, pl.ds(hc * hr + part * qr + k * ch, ch), :]
        return piece

    def between_chips(x_ref, out_ref, send_sems, recv_sems, local_sems):
        x, y, c, _ = _place()
        me, cxn, cyn, cdg = 2 * x + y, 2 * (1 - x) + y, 2 * x + 1 - y, 2 * (1 - x) + 1 - y
        xn, yn = (1 - x, y, c), (x, 1 - y, c)
        piece = pieces_of(out_ref, c)

        def own(part, k):
            return x_ref.at[pl.ds(c * hr + part * qr + k * ch, ch), :]

        def sems(flow, k):
            return send_sems.at[flow * kc + k], recv_sems.at[flow * kc + k]

        local = [pltpu.make_async_copy(x_ref.at[pl.ds(k * ch, ch), :], out_ref.at[me, pl.ds(k * ch, ch), :],
                                       local_sems.at[k]) for k in range(4 * kc)]
        sends = []
        for flow, part, peer in ((0, 0, xn), (1, 1, yn), (2, 0, yn), (3, 1, xn)):
            sends += [_rcopy(own(part, k), piece(me, c, part, k), *sems(flow, k), peer) for k in range(kc)]
        for cp in local + sends:
            cp.start()
        for flow, chip, part in ((0, cxn, 0), (1, cyn, 1), (2, cyn, 0), (3, cxn, 1), (4, cdg, 0), (5, cdg, 1)):
            for k in range(kc):
                _rcopy(piece(chip, c, part, k), piece(chip, c, part, k), *sems(flow, k), xn).wait_recv()
                if flow < 2:
                    on = _rcopy(piece(chip, c, part, k), piece(chip, c, part, k), *sems(4 + flow, k),
                                yn if flow == 0 else xn)
                    on.start()
                    sends.append(on)
        for cp in sends:
            cp.wait_send()
        for cp in local:
            cp.wait()

    def within_chip(in_ref, out_ref, send_sems, recv_sems):
        x, y, c, _ = _place()
        others = (2 * (1 - x) + y, 2 * x + 1 - y, 2 * (1 - x) + 1 - y)
        piece = pieces_of(out_ref, c)
        sibling = (x, y, 1 - c)

        def copies(hc):
            return [_rcopy(piece(chip, hc, part, k), piece(chip, hc, part, k), send_sems.at[(2 * i + part) * kc + k],
                           recv_sems.at[(2 * i + part) * kc + k], sibling)
                    for i, chip in enumerate(others) for part in range(2) for k in range(kc)]

        for cp in copies(c):
            cp.start()
        for cp in copies(1 - c):
            cp.wait_recv()
        for cp in copies(c):
            cp.wait_send()

    full = jax.ShapeDtypeStruct((N_SHARD, r, n), shard.dtype)
    halves = _pallas(
        between_chips, name=name + "_chips", in_specs=[ANY], out_specs=ANY, out_shape=full,
        scratch_shapes=[pltpu.SemaphoreType.DMA((nflow * kc,)), pltpu.SemaphoreType.DMA((nflow * kc,)),
                        pltpu.SemaphoreType.DMA((4 * kc,))])(shard)
    return _pallas(
        within_chip, name=name + "_pair", in_specs=[ANY], out_specs=ANY, out_shape=full, input_output_aliases={0: 0},
        scratch_shapes=[pltpu.SemaphoreType.DMA((6 * kc,)), pltpu.SemaphoreType.DMA((6 * kc,))])(halves)


def _pair_send(parts, name):
    n = parts[0].shape[1]
    hn = n // 2
    kc = PAIR_CHUNKS
    cw = hn // kc
    assert cw * kc == hn and cw % LANES == 0
    ns = len(parts)

    def body(*refs):
        srcs, dsts, send_sems, recv_sems = refs[:ns], refs[ns:2 * ns], refs[2 * ns], refs[2 * ns + 1]
        x, y, c, _ = _place()
        cps = [_rcopy(srcs[s].at[:, pl.ds((1 - c) * hn + k * cw, cw)], dsts[s].at[:, pl.ds(k * cw, cw)],
                      send_sems.at[s * kc + k], recv_sems.at[s * kc + k], (x, y, 1 - c))
               for s in range(ns) for k in range(kc)]
        for cp in cps:
            cp.start()
        for cp in cps:
            cp.wait()

    return _pallas(
        body, name=name, in_specs=[ANY] * ns, out_specs=[ANY] * ns,
        out_shape=[jax.ShapeDtypeStruct((p.shape[0], hn), F32) for p in parts],
        scratch_shapes=[pltpu.SemaphoreType.DMA((ns * kc,)), pltpu.SemaphoreType.DMA((ns * kc,))])(*parts)


REDUCE_TILE = 256


def _exchange_copies(g_ref, got_ref, send_sems, recv_sems):
    hn = g_ref.shape[2]
    kc = GATHER_CHUNKS
    cw = hn // kc
    assert cw * kc == hn and cw % LANES == 0
    x, y, c, chips = _place()
    return [_rcopy(g_ref.at[2 * chip[0] + chip[1], :, pl.ds(k * cw, cw)], got_ref.at[j, :, pl.ds(k * cw, cw)],
                   send_sems.at[j * kc + k], recv_sems.at[j * kc + k], (*chip, c))
            for j, chip in enumerate(chips) for k in range(kc)]


def _exchange_scratch():
    return [pltpu.SemaphoreType.DMA((3 * GATHER_CHUNKS,)), pltpu.SemaphoreType.DMA((3 * GATHER_CHUNKS,))]


def _exchange_shape(ga):
    return jax.ShapeDtypeStruct((3,) + ga.shape[1:], ga.dtype)


def _chip_sum(ga, got, place, name):
    _, r, hn = ga.shape
    tc = REDUCE_TILE
    nt = hn // tc

    def body(place_ref, own_ref, got_ref, o_ref):
        acc = own_ref[0].astype(F32)
        for j in range(3):
            acc = acc + got_ref[j].astype(F32)
        o_ref[...] = acc

    return _pallas(
        body, name=name,
        grid_spec=pltpu.PrefetchScalarGridSpec(
            num_scalar_prefetch=1, grid=(nt,),
            in_specs=[pl.BlockSpec((1, r, tc), lambda i, place_ref: (place_ref[0], 0, i)),
                      pl.BlockSpec((3, r, tc), lambda i, place_ref: (0, 0, i))],
            out_specs=pl.BlockSpec((r, tc), lambda i, place_ref: (0, place_ref[1] * nt + i))),
        out_shape=jax.ShapeDtypeStruct((r, 2 * hn), F32), compiler_params=_cp("parallel"))(place, ga, got)


def _pair_join(buf, name):
    r, n = buf.shape
    hn = n // 2
    kc = JOIN_CHUNKS
    cw = hn // kc
    assert cw * kc == hn and cw % LANES == 0

    def body(in_ref, out_ref, send_sems, recv_sems):
        x, y, c, _ = _place()
        cps = [_rcopy(out_ref.at[:, pl.ds(c * hn + k * cw, cw)], out_ref.at[:, pl.ds(c * hn + k * cw, cw)],
                      send_sems.at[k], recv_sems.at[k], (x, y, 1 - c)) for k in range(kc)]
        for cp in cps:
            cp.start()
        for k in range(kc):
            cols = out_ref.at[:, pl.ds((1 - c) * hn + k * cw, cw)]
            _rcopy(cols, cols, send_sems.at[k], recv_sems.at[k], (x, y, 1 - c)).wait_recv()
        for cp in cps:
            cp.wait_send()

    return _pallas(
        body, name=name, in_specs=[ANY], out_specs=ANY, out_shape=jax.ShapeDtypeStruct((r, n), F32),
        input_output_aliases={0: 0},
        scratch_shapes=[pltpu.SemaphoreType.DMA((kc,)), pltpu.SemaphoreType.DMA((kc,))])(buf)


def _reduce_pair(parts, pieces, shard_rows, place, tag):
    got = _pair_send(parts, tag + "_pair_send")
    return _pair_add_to_shards(parts, got, pieces, shard_rows, place[1:2], tag + "_pair_add")


def _reduce_finish(ga, slabs, place, tag):
    return _pair_join(_chip_sum(ga, slabs, place, tag + "_chip_sum"), tag + "_pair_join")


def _allreduce_small(p, name):
    rows, n = p.shape
    ndev = 8

    def body(p_ref, out_ref, slots, send_sems, recv_sems):
        x, y, c, _ = _place()
        my = 4 * x + 2 * y + c
        slots[my] = p_ref[...]
        cps = []
        for k in range(1, ndev):
            kx, ky, kc = (k >> 2) & 1, (k >> 1) & 1, k & 1
            peer = (x ^ kx, y ^ ky, c ^ kc)
            cp = _rcopy(p_ref, slots.at[my], send_sems.at[k - 1], recv_sems.at[k - 1], peer)
            cp.start()
            cps.append(cp)
        for k in range(1, ndev):
            _rcopy(p_ref, slots.at[my ^ k], send_sems.at[k - 1], recv_sems.at[k - 1], (x, y, c)).wait_recv()
        for cp in cps:
            cp.wait_send()
        acc = slots[0]
        for j in range(1, ndev):
            acc = acc + slots[j]
        out_ref[...] = acc

    vm = pl.BlockSpec(memory_space=pltpu.VMEM)
    return _pallas(
        body, name=name, in_specs=[vm], out_specs=vm, out_shape=jax.ShapeDtypeStruct((rows, n), F32),
        scratch_shapes=[pltpu.VMEM((ndev, rows, n), F32), pltpu.SemaphoreType.DMA((ndev - 1,)),
                        pltpu.SemaphoreType.DMA((ndev - 1,))])(p)


def _adamw(w, g, m, v, name):
    r, n = w.shape
    tr = _tile(r, 256, 8)
    c1 = 1.0 / (1.0 - ADAM_B1 ** ADAM_STEP)
    c2 = 1.0 / (1.0 - ADAM_B2 ** ADAM_STEP)

    def body(w_ref, g_ref, m_ref, v_ref, d_ref, mo_ref, vo_ref):
        gv = g_ref[...]
        mn = ADAM_B1 * m_ref[...] + (1.0 - ADAM_B1) * gv
        vn = ADAM_B2 * v_ref[...] + (1.0 - ADAM_B2) * (gv * gv)
        d_ref[...] = -ADAM_LR * ((mn * c1) / (jnp.sqrt(vn * c2) + ADAM_EPS) + ADAM_WD * w_ref[...])
        mo_ref[...] = mn
        vo_ref[...] = vn

    spec = pl.BlockSpec((tr, n), lambda i: (i, 0))
    shp = jax.ShapeDtypeStruct((r, n), F32)
    return _pallas(body, name=name, grid=(r // tr,), in_specs=[spec] * 4, out_specs=[spec] * 3, out_shape=[shp] * 3,
                   compiler_params=_cp("parallel"))(w, g, m, v)


PACK_W = 1024
SMALL_REPL = ("norm_pre_w", "conv_b", "ssd_norm_w", "norm_post_w")
SMALL_HEAD = ("dt_bias", "a_log", "d_skip", "attn_sinks")


def _rows(a):
    return a.reshape(-1, PACK_W)


def _head_row(vals, extra=None):
    parts = [vals[n].reshape(1, -1) for n in SMALL_HEAD]
    if extra is not None:
        parts.append(extra.reshape(1, 1))
    row = jnp.concatenate(parts, axis=1)
    return jnp.pad(row, ((0, 0), (0, PACK_W - row.shape[1])))


def _pad_rows(a, rows):
    return jnp.pad(a, ((0, rows - a.shape[0]), (0, 0)))


def _pack_repl(vals, extra=None):
    body = jnp.concatenate([_rows(vals[n]) for n in SMALL_REPL] + [_head_row(vals, extra)], axis=0)
    return _pad_rows(body, 16)


def _unpack_repl(buf):
    out, r = {}, 0
    for n, k in zip(SMALL_REPL, (2, 4, 2, 2)):
        out[n] = buf[r:r + k].reshape(1, k * PACK_W)
        r += k
    col = 0
    for n, k in zip(SMALL_HEAD, (32, 32, 32, 16)):
        out[n] = buf[r:r + 1, col:col + k]
        col += k
    return out, buf[r, col]


def kernel(x, meta_tokens, norm_pre_w, w_in, conv_w, conv_b, dt_bias, a_log, d_skip, ssd_norm_w, attn_sinks, w_out, norm_post_w, loss_target, m_meta_tokens, m_norm_pre_w, m_w_in, m_conv_w, m_conv_b, m_dt_bias, m_a_log, m_d_skip, m_ssd_norm_w, m_attn_sinks, m_w_out, m_norm_post_w, v_meta_tokens, v_norm_pre_w, v_w_in, v_conv_w, v_conv_b, v_dt_bias, v_a_log, v_d_skip, v_ssd_norm_w, v_attn_sinks, v_w_out, v_norm_post_w):
    names = ("meta_tokens", "norm_pre_w", "w_in", "conv_w", "conv_b", "dt_bias", "a_log", "d_skip", "ssd_norm_w",
             "attn_sinks", "w_out", "norm_post_w")
    w = dict(zip(names, (meta_tokens, norm_pre_w, w_in, conv_w, conv_b, dt_bias, a_log, d_skip, ssd_norm_w, attn_sinks,
                         w_out, norm_post_w)))
    m = dict(zip(names, (m_meta_tokens, m_norm_pre_w, m_w_in, m_conv_w, m_conv_b, m_dt_bias, m_a_log, m_d_skip,
                         m_ssd_norm_w, m_attn_sinks, m_w_out, m_norm_post_w)))
    v = dict(zip(names, (v_meta_tokens, v_norm_pre_w, v_w_in, v_conv_w, v_conv_b, v_dt_bias, v_a_log, v_d_skip,
                         v_ssd_norm_w, v_attn_sinks, v_w_out, v_norm_post_w)))
    cx, cy, cc = lax.axis_index("x"), lax.axis_index("y"), lax.axis_index("c")
    chip = 2 * cx + cy
    meta_cols = D_MODEL // N_SHARD
    conv_cols = D_CONV // N_SHARD

    place = jnp.stack([chip, cc]).astype(jnp.int32)
    w_re = _shards_to_re(_gather_shards(_bf(w_in[0]), "gather_w_in", GATHER_CHUNKS))
    conv_z = lax.dynamic_update_slice(jnp.zeros((CONV_WIDTH, D_CONV), F32), conv_w[0], (0, chip * conv_cols))
    meta_z = lax.dynamic_update_slice(jnp.zeros((N_META, D_MODEL), F32), meta_tokens, (0, chip * meta_cols))
    small = jnp.concatenate([_rows(conv_z), _rows(meta_z)], axis=0)
    small = _allreduce_small(jnp.where(cc == 0, small, 0.0), "gather_small")
    conv_full = small[0:16].reshape(CONV_WIDTH, D_CONV)
    meta_full = small[16:48].reshape(N_META, D_MODEL)

    loss_dev, grad_x, g = _local_step(x[0], loss_target[0], meta_full, norm_pre_w, w_re, conv_full, conv_b, dt_bias,
                                      a_log, d_skip, ssd_norm_w, attn_sinks, _bf(w_out[0]), norm_post_w, place)
    g_w_in, g_w_out = g["w_in"], g["w_out"]

    packed = jnp.concatenate([_rows(g["conv_w"]), _rows(g["meta_tokens"]), _pack_repl(g, loss_dev)], axis=0)
    red = _allreduce_small(packed, "reduce_small")
    g_conv_full = red[0:16].reshape(CONV_WIDTH, D_CONV)
    g_meta_full = red[16:48].reshape(N_META, D_MODEL)
    g_small, loss = _unpack_repl(red[48:64])
    grads = dict(g_small)
    grads["w_in"] = g_w_in
    grads["w_out"] = g_w_out
    grads["conv_w"] = lax.dynamic_slice(g_conv_full, (0, chip * conv_cols), (CONV_WIDTH, conv_cols))
    grads["meta_tokens"] = lax.dynamic_slice(g_meta_full, (0, chip * meta_cols), (N_META, meta_cols))

    upd = {}
    upd["w_in"] = [jnp.swapaxes(a, 0, 1) for a in _adamw(jnp.swapaxes(w_in[0], 0, 1), g_w_in, jnp.swapaxes(m_w_in[0], 0, 1),
                                                         jnp.swapaxes(v_w_in[0], 0, 1), "adamw_w_in")]
    grads["w_in"] = jnp.swapaxes(g_w_in, 0, 1)
    upd["w_out"] = _adamw(w_out[0], g_w_out, m_w_out[0], v_w_out[0], "adamw_w_out")

    def pack_small(vals, conv, meta):
        return jnp.concatenate([_pad_rows(conv.reshape(CONV_WIDTH, conv_cols), 8), _rows(meta), _pack_repl(vals)], axis=0)

    sm = _adamw(pack_small(w, w["conv_w"], w["meta_tokens"]), pack_small(grads, grads["conv_w"], grads["meta_tokens"]),
                pack_small(m, m["conv_w"], m["meta_tokens"]), pack_small(v, v["conv_w"], v["meta_tokens"]),
                "adamw_small")
    for n in names:
        if n not in ("w_in", "w_out"):
            upd[n] = [None, None, None]
    for k, buf in enumerate(sm):
        upd["conv_w"][k] = buf[0:CONV_WIDTH]
        upd["meta_tokens"][k] = buf[8:16].reshape(N_META, meta_cols)
        rest, _ = _unpack_repl(buf[16:32])
        for n in SMALL_REPL + SMALL_HEAD:
            upd[n][k] = rest[n]

    def shaped(n, a):
        return a.reshape(w[n].shape)

    outs = [loss, grad_x[None]]
    outs += [shaped(n, grads[n]) for n in names]
    for k in range(3):
        outs += [shaped(n, upd[n][k]) for n in names]
    return tuple(outs)
```

```python
import functools

import jax
import jax.numpy as jnp
from jax import lax
from jax.experimental import pallas as pl
from jax.experimental.pallas import tpu as pltpu

F32 = jnp.float32
BF16 = jnp.bfloat16

D_MODEL = 2048
CHUNK = 64
N_META = 16
PAD_LEAD = CHUNK - N_META
ROW0 = PAD_LEAD + N_META
EPS = 1e-6
SSD_HEADS = 32
HEAD_DIM = 64
GROUPS = 8
HPG = SSD_HEADS // GROUPS
D_STATE = 128
D_SSD = 2048
GROUP_W = D_SSD // GROUPS
CONV_WIDTH = 4
D_CONV = 4096
Q_HEADS = 16
KV_HEADS = 4
REP = Q_HEADS // KV_HEADS
D_ATT = 1024
D_KV = 256
BAND_CHUNKS = 3
ROPE_THETA = 10000.0
D_MIX = D_SSD + D_ATT
D_IN = 8736
N_SHARD = 4
W_IN_SHARD = D_IN // N_SHARD
W_OUT_SHARD = D_MIX // N_SHARD

OZ, OXS, OB, OC, OQ, OG, OK, OV, ODT = 0, 2048, 4096, 5120, 6144, 7168, 8192, 8448, 8704
DT_SLAB = 512
N_RE = ODT + DT_SLAB
LANES = 128

ADAM_LR, ADAM_B1, ADAM_B2, ADAM_EPS, ADAM_WD, ADAM_STEP = 0.001, 0.9, 0.999, 1e-08, 0.01, 10

SSD_FWD_GROUPS_PER_STEP = 4
SSD_BWD_GROUPS_PER_STEP = 8
SEG_TILE = 1024
VMEM_LIMIT = 52 * 1024 * 1024
NEG = -1e30
HI = lax.Precision.HIGHEST


def _pallas(body, **kw):
    return pl.pallas_call(body, **kw)


def _cp(*sem):
    return pltpu.CompilerParams(dimension_semantics=sem, vmem_limit_bytes=VMEM_LIMIT)


def _tile(n, cap, mult=16):
    best = None
    for d in range(mult, min(n, cap) + 1, mult):
        if n % d == 0:
            best = d
    assert best is not None, (n, cap)
    return best


def _nt(a, b):
    return lax.dot_general(a, b, (((1,), (1,)), ((), ())), preferred_element_type=F32)


def _tn(a, b):
    return lax.dot_general(a, b, (((0,), (0,)), ((), ())), preferred_element_type=F32)


def _mm(a, b):
    return jnp.dot(a, b, preferred_element_type=F32)


def _sigmoid(x):
    return 1.0 / (1.0 + jnp.exp(-x))


def _bf(x):
    return x.astype(BF16)


def _inproj(hpad, norm_w, w_re, w_out_shard):
    t, d = hpad.shape
    n = w_re.shape[1]
    tm, tn = _tile(t, 832), 1024
    ni, nj = t // tm, n // tn
    r_out, n_out = w_out_shard.shape
    kc = GATHER_CHUNKS

    def body(h_ref, nw_ref, w_ref, ws_ref, proj_ref, hn_ref, wall_ref, hn_s, send_sems, recv_sems, local_sems):
        i, j = pl.program_id(0), pl.program_id(1)
        start, forward, finish = _gather_plan(ws_ref, wall_ref, send_sems, recv_sems, local_sems, r_out // 2, kc)
        pl.when((i == 0) & (j == 0))(start)
        pl.when((i == ni // 2) & (j == 0))(forward)

        @pl.when(j == 0)
        def _():
            h = h_ref[...]
            ms = jnp.mean(h * h, axis=-1, keepdims=True)
            hn = _bf(h * lax.rsqrt(ms + EPS) * nw_ref[...])
            hn_s[...] = hn
            hn_ref[...] = hn
        proj_ref[...] = _mm(hn_s[...], w_ref[...])
        pl.when((i == ni - 1) & (j == nj - 1))(finish)

    return _pallas(
        body, name="inproj", grid=(ni, nj),
        in_specs=[pl.BlockSpec((tm, d), lambda i, j: (i, 0)), pl.BlockSpec((1, d), lambda i, j: (0, 0)),
                  pl.BlockSpec((d, tn), lambda i, j: (0, j)), ANY],
        out_specs=[pl.BlockSpec((tm, tn), lambda i, j: (i, j)), pl.BlockSpec((tm, d), lambda i, j: (i, 0)), ANY],
        out_shape=[jax.ShapeDtypeStruct((t, n), F32), jax.ShapeDtypeStruct((t, d), BF16),
                   jax.ShapeDtypeStruct((N_SHARD, r_out, n_out), w_out_shard.dtype)],
        scratch_shapes=[pltpu.VMEM((tm, d), BF16), pltpu.SemaphoreType.DMA((6 * kc,)), pltpu.SemaphoreType.DMA((6 * kc,)),
                        pltpu.SemaphoreType.DMA((2 * kc,))],
        compiler_params=_cp("arbitrary", "arbitrary"))(hpad, norm_w, w_re, w_out_shard)


def _conv_fwd(proj, conv_w, conv_b):
    t = proj.shape[0]
    tc = 256
    off = OXS // tc

    def body(x_ref, w_ref, b_ref, o_ref):
        x = x_ref[...]
        w = w_ref[...]
        row = lax.broadcasted_iota(jnp.int32, (t, tc), 0)
        u = b_ref[...] + w[3:4, :] * x
        for k in range(1, CONV_WIDTH):
            u = u + w[3 - k:4 - k, :] * jnp.where(row >= k, pltpu.roll(x, k, 0), 0.0)
        h = 0.5 * u
        o_ref[...] = h + h * jnp.tanh(h)

    return _pallas(
        body, name="conv_fwd", grid=(D_CONV // tc,),
        in_specs=[pl.BlockSpec((t, tc), lambda j: (0, j + off)), pl.BlockSpec((CONV_WIDTH, tc), lambda j: (0, j)),
                  pl.BlockSpec((1, tc), lambda j: (0, j))],
        out_specs=pl.BlockSpec((t, tc), lambda j: (0, j)),
        out_shape=jax.ShapeDtypeStruct((t, D_CONV), F32),
        compiler_params=_cp("parallel"))(proj, conv_w, conv_b)


def _softplus(u):
    e = jnp.exp(-jnp.abs(u))
    w = 1.0 + e
    l1p = jnp.where(w == 1.0, e, jnp.log(w) * (e / jnp.where(w == 1.0, 1.0, w - 1.0)))
    return jnp.maximum(u, 0.0) + l1p


def _chunks_per_step(nc):
    return max(d for d in range(1, 14) if nc % d == 0)


def _dt_prep(proj, dt_bias_l, a_log_l):
    t = proj.shape[0]
    nc = t // CHUNK
    q = CHUNK
    cps = _chunks_per_step(nc)
    rows = cps * q

    def body(raw_ref, bias_ref, alog_ref, dt_ref, acs_ref, acst_ref):
        ri = lax.broadcasted_iota(jnp.int32, (q, q), 0)
        ci = lax.broadcasted_iota(jnp.int32, (q, q), 1)
        tri = (ri >= ci).astype(F32)
        neg_a = -jnp.exp(alog_ref[...])
        for k in range(cps):
            rk = slice(q * k, q * (k + 1))
            sp = _softplus(raw_ref[rk, :] + bias_ref[...])
            row = pl.program_id(0) * rows + q * k + lax.broadcasted_iota(jnp.int32, (q, LANES), 0)
            dt = jnp.where(row >= PAD_LEAD, sp, 0.0)
            acs = jnp.dot(tri, dt * neg_a, preferred_element_type=F32, precision=HI)
            dt_ref[rk, :] = dt
            acs_ref[rk, :] = acs
            acst_ref[k] = acs.T

    return _pallas(
        body, name="dt_prep", grid=(nc // cps,),
        in_specs=[pl.BlockSpec((rows, LANES), lambda c: (c, ODT // LANES)), pl.BlockSpec((1, LANES), lambda c: (0, 0)),
                  pl.BlockSpec((1, LANES), lambda c: (0, 0))],
        out_specs=[pl.BlockSpec((rows, LANES), lambda c: (c, 0)), pl.BlockSpec((rows, LANES), lambda c: (c, 0)),
                   pl.BlockSpec((cps, LANES, q), lambda c: (c, 0, 0))],
        out_shape=[jax.ShapeDtypeStruct((t, LANES), F32), jax.ShapeDtypeStruct((t, LANES), F32),
                   jax.ShapeDtypeStruct((nc, LANES, q), F32)],
        compiler_params=_cp("parallel"))(proj, dt_bias_l, a_log_l)


def _head_cols(blk, idx):
    lane = lax.broadcasted_iota(jnp.int32, blk.shape, 1)
    return jnp.sum(jnp.where(lane == idx, blk, 0.0), axis=1, keepdims=True)


class _HeadVals:
    pass


def _lane_head(shape):
    return lax.broadcasted_iota(jnp.int32, shape, len(shape) - 1) >> 6


def _group_heads(g, gi, dtb, acsb, acst_ref, dskb):
    q = dtb.shape[0]
    hv = _HeadVals()
    lh = _lane_head((1, GROUP_W))
    hv.dt = jnp.zeros((q, GROUP_W), F32)
    hv.acs = jnp.zeros((q, GROUP_W), F32)
    hv.acs_last = jnp.zeros((1, GROUP_W), F32)
    hv.dsk = jnp.zeros((1, GROUP_W), F32)
    rows = []
    for r in range(HPG):
        idx = GROUPS * g + r
        sel = lh == r
        acs_r = acst_ref[0, GROUPS * gi + r:GROUPS * gi + r + 1, :]
        rows.append(acs_r)
        hv.dt = jnp.where(sel, _head_cols(dtb, idx), hv.dt)
        hv.acs = jnp.where(sel, _head_cols(acsb, idx), hv.acs)
        hv.acs_last = jnp.where(sel, acs_r[:, q - 1:q], hv.acs_last)
        hv.dsk = jnp.where(sel, _head_cols(dskb, idx), hv.dsk)
    hv.acs_row = jnp.concatenate(rows, axis=1)
    return hv


def _head_tri(q, lower):
    ri = lax.broadcasted_iota(jnp.int32, (q, GROUP_W), 0)
    li = lax.broadcasted_iota(jnp.int32, (q, GROUP_W), 1) & (HEAD_DIM - 1)
    return ri >= li if lower else ri <= li


def _block_diag(v):
    rb = lax.broadcasted_iota(jnp.int32, (GROUP_W, GROUP_W), 0) >> 6
    cb = lax.broadcasted_iota(jnp.int32, (GROUP_W, GROUP_W), 1) >> 6
    return jnp.where(rb == cb, jnp.concatenate([v] * HPG, axis=0), jnp.zeros((), v.dtype))


def _head_sums(v, r):
    return jnp.sum(jnp.where(_lane_head((1, GROUP_W)) == r, v, 0.0), axis=1, keepdims=True)


def _ssd_fwd(xbc, proj, dt, acs, acst, d_skip_l, ssd_norm_w):
    t = xbc.shape[0]
    q = CHUNK
    nc = t // q

    gps = SSD_FWD_GROUPS_PER_STEP
    gw, sw = gps * GROUP_W, gps * D_STATE

    def body(xs_ref, b_ref, c_ref, dt_ref, acs_ref, acst_ref, z_ref, dsk_ref, nw_ref,
             y_ref, ymix_ref, st_ref, state):
        @pl.when(pl.program_id(1) == 0)
        def _():
            state[...] = jnp.zeros_like(state)

        for gi in range(gps):
            g = gps * pl.program_id(0) + gi
            cols = slice(GROUP_W * gi, GROUP_W * (gi + 1))
            x = xs_ref[:, cols]
            bmb = _bf(b_ref[:, D_STATE * gi:D_STATE * (gi + 1)])
            cmb = _bf(c_ref[:, D_STATE * gi:D_STATE * (gi + 1)])
            hv = _group_heads(g, gi, dt_ref[...], acs_ref[...], acst_ref, dsk_ref[...])
            decay = jnp.exp(jnp.where(_head_tri(q, True), hv.acs - hv.acs_row, NEG))
            m_all = _bf(_nt(cmb, jnp.concatenate([bmb] * HPG, axis=0)) * decay)
            xdt = x * hv.dt
            s_prev = state[gi]
            st_ref[0, gi] = s_prev
            y = (_mm(m_all, _block_diag(_bf(xdt))) + _mm(cmb, _bf(s_prev)) * jnp.exp(hv.acs) + hv.dsk * x)
            state[gi] = jnp.exp(hv.acs_last) * s_prev + _tn(bmb, _bf(xdt * jnp.exp(hv.acs_last - hv.acs)))
            y_ref[:, cols] = y
            z = z_ref[:, cols]
            yg = y * (z * _sigmoid(z))
            ms = jnp.mean(yg * yg, axis=-1, keepdims=True)
            ymix_ref[:, cols] = _bf(yg * lax.rsqrt(ms + EPS) * nw_ref[:, cols])

    return _pallas(
        body, name="ssd_fwd", grid=(GROUPS // gps, nc),
        in_specs=[pl.BlockSpec((q, gw), lambda g, c: (c, g)),
                  pl.BlockSpec((q, sw), lambda g, c: (c, D_SSD // sw + g)),
                  pl.BlockSpec((q, sw), lambda g, c: (c, (D_SSD + GROUPS * D_STATE) // sw + g)),
                  pl.BlockSpec((q, LANES), lambda g, c: (c, 0)), pl.BlockSpec((q, LANES), lambda g, c: (c, 0)),
                  pl.BlockSpec((1, gps * GROUPS, q), lambda g, c: (c, g, 0)),
                  pl.BlockSpec((q, gw), lambda g, c: (c, g)),
                  pl.BlockSpec((1, LANES), lambda g, c: (0, 0)), pl.BlockSpec((1, gw), lambda g, c: (0, g))],
        out_specs=[pl.BlockSpec((q, gw), lambda g, c: (c, g)), pl.BlockSpec((q, gw), lambda g, c: (c, g)),
                   pl.BlockSpec((1, gps, D_STATE, GROUP_W), lambda g, c: (c, g, 0, 0))],
        out_shape=[jax.ShapeDtypeStruct((t, D_SSD), F32), jax.ShapeDtypeStruct((t, D_SSD), BF16),
                   jax.ShapeDtypeStruct((nc, GROUPS, D_STATE, GROUP_W), F32)],
        scratch_shapes=[pltpu.VMEM((gps, D_STATE, GROUP_W), F32)],
        compiler_params=_cp("parallel", "arbitrary"))(xbc, xbc, xbc, dt, acs, acst, proj, d_skip_l, ssd_norm_w)


def _swap_halves(v):
    lane = lax.broadcasted_iota(jnp.int32, v.shape, 1)
    return jnp.where((lane & (HEAD_DIM - 1)) < HEAD_DIM // 2, pltpu.roll(v, LANES - HEAD_DIM // 2, 1),
                     pltpu.roll(v, HEAD_DIM // 2, 1))


def _rope(qsrc, q_off, ksrc, k_off, cos_t, sin_t):
    t = qsrc.shape[0]
    tr = _tile(t, 832)
    q_scale = HEAD_DIM ** -0.5

    def body(q_ref, k_ref, cos_ref, sin_ref, qo_ref, ko_ref):
        cs = cos_ref[...]
        sn = sin_ref[...]
        for src, dst, width, scale in ((q_ref, qo_ref, D_ATT, q_scale), (k_ref, ko_ref, D_KV, 1.0)):
            for s in range(width // LANES):
                v = src[:, LANES * s:LANES * (s + 1)].astype(F32)
                dst[:, LANES * s:LANES * (s + 1)] = _bf((v * cs + _swap_halves(v) * sn) * scale)

    return _pallas(
        body, name="rope", grid=(t // tr,),
        in_specs=[pl.BlockSpec((tr, D_ATT), lambda i: (i, q_off // D_ATT)),
                  pl.BlockSpec((tr, D_KV), lambda i: (i, k_off // D_KV)),
                  pl.BlockSpec((tr, LANES), lambda i: (i, 0)), pl.BlockSpec((tr, LANES), lambda i: (i, 0))],
        out_specs=[pl.BlockSpec((tr, D_ATT), lambda i: (i, 0)), pl.BlockSpec((tr, D_KV), lambda i: (i, 0))],
        out_shape=[jax.ShapeDtypeStruct((t, D_ATT), BF16), jax.ShapeDtypeStruct((t, D_KV), BF16)],
        compiler_params=_cp("parallel"))(qsrc, ksrc, cos_t, sin_t)


def _attn_chunks_per_step(nc):
    return max(d for d in range(1, 6) if nc % d == 0)


def _band(ref, c):
    return [ref[pl.ds(pl.multiple_of(jnp.maximum(c - j, 0) * CHUNK, CHUNK), CHUNK), :] for j in (2, 1, 0)]


def _attn_probs(qh, kb, sink_col, valid):
    s = jnp.where(valid, _nt(qh, kb), NEG)
    m = jnp.maximum(jnp.max(s, axis=1, keepdims=True), sink_col)
    p = jnp.exp(s - m)
    psink = jnp.exp(sink_col - m)
    return p, psink, 1.0 / (jnp.sum(p, axis=1, keepdims=True) + psink)


def _attn_operands(c, q, k_refs, v_refs, sink_ref, h):
    qh = jnp.concatenate([q[:, HEAD_DIM * (REP * h + r):HEAD_DIM * (REP * h + r + 1)] for r in range(REP)], axis=0)
    kb = jnp.concatenate([k[:, HEAD_DIM * h:HEAD_DIM * (h + 1)] for k in k_refs], axis=0)
    vb = jnp.concatenate([_bf(v[:, HEAD_DIM * h:HEAD_DIM * (h + 1)]) for v in v_refs], axis=0)
    rows = lax.broadcasted_iota(jnp.int32, (REP * CHUNK, 1), 0) >> 6
    sink_col = jnp.zeros((REP * CHUNK, 1), F32)
    for r in range(REP):
        sink_col = jnp.where(rows == r, sink_ref[REP * h + r], sink_col)
    key_abs = (c - (BAND_CHUNKS - 1)) * CHUNK + lax.broadcasted_iota(jnp.int32, (1, BAND_CHUNKS * CHUNK), 1)
    return qh, kb, vb, sink_col, key_abs >= PAD_LEAD


def _attn_fwd(qr, kr, proj, sinks):
    t = qr.shape[0]
    nc = t // CHUNK
    cps = _attn_chunks_per_step(nc)
    rows = cps * CHUNK

    def body(q_ref, k_ref, v_ref, g_ref, sink_ref, o_ref):
        for j in range(cps):
            c = pl.program_id(0) * cps + j
            rj = slice(CHUNK * j, CHUNK * (j + 1))
            ks, vs = _band(k_ref, c), _band(v_ref, c)
            q = q_ref[rj, :]
            outs = []
            for h in range(KV_HEADS):
                qh, kb, vb, sink_col, valid = _attn_operands(c, q, ks, vs, sink_ref, h)
                p, _, inv = _attn_probs(qh, kb, sink_col, valid)
                o = _mm(_bf(p), vb) * inv
                outs += [o[CHUNK * r:CHUNK * (r + 1)] for r in range(REP)]
            att = jnp.concatenate(outs, axis=1)
            gate = g_ref[rj, :]
            o_ref[rj, :] = _bf(att * (gate * _sigmoid(gate)))

    return _pallas(
        body, name="attn_fwd", grid=(nc // cps,),
        in_specs=[pl.BlockSpec((rows, D_ATT), lambda i: (i, 0)), pl.BlockSpec((t, D_KV), lambda i: (0, 0)),
                  pl.BlockSpec((t, D_KV), lambda i: (0, OV // D_KV)),
                  pl.BlockSpec((rows, D_ATT), lambda i: (i, OG // D_ATT)), pl.BlockSpec(memory_space=pltpu.SMEM)],
        out_specs=pl.BlockSpec((rows, D_ATT), lambda i: (i, 0)),
        out_shape=jax.ShapeDtypeStruct((t, D_ATT), BF16),
        compiler_params=_cp("parallel"))(qr, kr, proj, proj, sinks)


def _outproj(ymix, amix, w_out):
    t = ymix.shape[0]
    tm, tn = _tile(t, 832), 1024

    def body(y_ref, a_ref, wy_ref, wa_ref, o_ref):
        o_ref[...] = _mm(y_ref[...], wy_ref[...]) + _mm(a_ref[...], wa_ref[...])

    return _pallas(
        body, name="outproj", grid=(t // tm, D_MODEL // tn),
        in_specs=[pl.BlockSpec((tm, D_SSD), lambda i, j: (i, 0)), pl.BlockSpec((tm, D_ATT), lambda i, j: (i, 0)),
                  pl.BlockSpec((D_SSD, tn), lambda i, j: (0, j)),
                  pl.BlockSpec((D_ATT, tn), lambda i, j: (D_SSD // D_ATT, j))],
        out_specs=pl.BlockSpec((tm, tn), lambda i, j: (i, j)),
        out_shape=jax.ShapeDtypeStruct((t, D_MODEL), F32),
        compiler_params=_cp("parallel", "parallel"))(ymix, amix, w_out, w_out)


def _post_loss(out, x, target, norm_post_w):
    t = out.shape[0]
    nc = t // CHUNK

    def body(o_ref, x_ref, tg_ref, nw_ref, dout_ref, dy_ref, loss_ref, gnw_ref):
        i = pl.program_id(0)

        @pl.when(i == 0)
        def _():
            dout_ref[...] = jnp.zeros_like(dout_ref)
            dy_ref[...] = jnp.zeros_like(dy_ref)
            loss_ref[...] = jnp.zeros_like(loss_ref)
            gnw_ref[...] = jnp.zeros_like(gnw_ref)

        @pl.when(i > 0)
        def _():
            o = o_ref[...]
            nw = nw_ref[...]
            rstd = lax.rsqrt(jnp.mean(o * o, axis=-1, keepdims=True) + EPS)
            n = o * rstd
            err = x_ref[...] + n * nw - tg_ref[...]
            loss_ref[...] += jnp.sum(err * err) * (0.5 / D_MODEL)
            dy = err * (1.0 / D_MODEL)
            dy_ref[...] = dy
            gnw_ref[...] += jnp.sum(dy * n, axis=0, keepdims=True)
            dn = dy * nw
            dout_ref[...] = _bf(rstd * (dn - n * jnp.mean(dn * n, axis=-1, keepdims=True)))

    prev = lambda i: (jnp.maximum(i - 1, 0), 0)
    return _pallas(
        body, name="post_loss", grid=(nc,),
        in_specs=[pl.BlockSpec((CHUNK, D_MODEL), lambda i: (i, 0)), pl.BlockSpec((CHUNK, D_MODEL), prev),
                  pl.BlockSpec((CHUNK, D_MODEL), prev), pl.BlockSpec((1, D_MODEL), lambda i: (0, 0))],
        out_specs=[pl.BlockSpec((CHUNK, D_MODEL), lambda i: (i, 0)), pl.BlockSpec((CHUNK, D_MODEL), lambda i: (i, 0)),
                   pl.BlockSpec((8, LANES), lambda i: (0, 0)), pl.BlockSpec((1, D_MODEL), lambda i: (0, 0))],
        out_shape=[jax.ShapeDtypeStruct((t, D_MODEL), BF16), jax.ShapeDtypeStruct((t, D_MODEL), F32),
                   jax.ShapeDtypeStruct((8, LANES), F32), jax.ShapeDtypeStruct((1, D_MODEL), F32)],
        compiler_params=_cp("arbitrary"))(out, x, target, norm_post_w)


def _nt_matmul(a, b, name):
    t, k = a.shape
    n = b.shape[0]
    tm, tn = _tile(t, 832), 1024

    def body(a_ref, b_ref, o_ref):
        o_ref[...] = _nt(a_ref[...], b_ref[...])

    return _pallas(
        body, name=name, grid=(t // tm, n // tn),
        in_specs=[pl.BlockSpec((tm, k), lambda i, j: (i, 0)), pl.BlockSpec((tn, k), lambda i, j: (j, 0))],
        out_specs=pl.BlockSpec((tm, tn), lambda i, j: (i, j)),
        out_shape=jax.ShapeDtypeStruct((t, n), F32),
        compiler_params=_cp("parallel", "parallel"))(a, b)


def _tn_matmul(a, b, name):
    t, m = a.shape
    n = b.shape[1]
    tk, tm, tn = _tile(t, 832), min(m, 1024), min(n, 2048)
    nk = t // tk

    def body(a_ref, b_ref, o_ref):
        @pl.when(pl.program_id(2) == 0)
        def _():
            o_ref[...] = jnp.zeros_like(o_ref)
        o_ref[...] += _tn(a_ref[...], b_ref[...])

    return _pallas(
        body, name=name, grid=(m // tm, n // tn, nk),
        in_specs=[pl.BlockSpec((tk, tm), lambda i, j, k: (k, i)), pl.BlockSpec((tk, tn), lambda i, j, k: (k, j))],
        out_specs=pl.BlockSpec((tm, tn), lambda i, j, k: (i, j)),
        out_shape=jax.ShapeDtypeStruct((m, n), F32),
        compiler_params=_cp("parallel", "parallel", "arbitrary"))(a, b)


def _attn_bwd(qr, kr, proj, dmix, sinks, ga):
    t = qr.shape[0]
    nc = t // CHUNK
    cps = _attn_chunks_per_step(nc)
    nsteps = nc // cps
    rows_step = cps * CHUNK

    def body(q_ref, k_ref, v_ref, g_ref, da_ref, sink_ref, ga_ref, dq_ref, dg_ref, dk_ref, dv_ref, gs_ref,
             got_ref, send_sems, recv_sems):
        step = pl.program_id(0)

        @pl.when(step == 0)
        def _():
            for cp in _exchange_copies(ga_ref, got_ref, send_sems, recv_sems):
                cp.start()
            dk_ref[...] = jnp.zeros_like(dk_ref)
            dv_ref[...] = jnp.zeros_like(dv_ref)
            gs_ref[...] = jnp.zeros_like(gs_ref)

        lane = lax.broadcasted_iota(jnp.int32, (1, LANES), 1)
        rows = lax.broadcasted_iota(jnp.int32, (REP * CHUNK, 1), 0) >> 6
        gs = jnp.zeros((1, LANES), F32)
        dk_parts = [[] for _ in range(cps + BAND_CHUNKS - 1)]
        dv_parts = [[] for _ in range(cps + BAND_CHUNKS - 1)]
        for j in range(cps):
            c = step * cps + j
            rj = slice(CHUNK * j, CHUNK * (j + 1))
            ks, vs = _band(k_ref, c), _band(v_ref, c)
            q = q_ref[rj, :]
            gate = g_ref[rj, :]
            sg = _sigmoid(gate)
            da = da_ref[rj, :]
            datt = da * (gate * sg)
            dqs, atts, dks, dvs = [], [], [], []
            for h in range(KV_HEADS):
                qh, kb, vb, sink_col, valid = _attn_operands(c, q, ks, vs, sink_ref, h)
                p, psink, inv = _attn_probs(qh, kb, sink_col, valid)
                pb = _bf(p)
                o = _mm(pb, vb) * inv
                do = jnp.concatenate([datt[:, HEAD_DIM * (REP * h + r):HEAD_DIM * (REP * h + r + 1)]
                                      for r in range(REP)], axis=0)
                dob = _bf(do * inv)
                delta = jnp.sum(do * o, axis=1, keepdims=True) * inv
                ds = _bf(p * (_nt(dob, vb) - delta))
                gsink = -psink * delta
                for r in range(REP):
                    gs = gs + jnp.where(lane == REP * h + r, jnp.sum(jnp.where(rows == r, gsink, 0.0)), 0.0)
                dqh = _mm(ds, kb)
                dqs += [dqh[CHUNK * r:CHUNK * (r + 1)] for r in range(REP)]
                atts += [o[CHUNK * r:CHUNK * (r + 1)] for r in range(REP)]
                dks.append(_tn(ds, qh))
                dvs.append(_tn(pb, dob))
            dq_ref[rj, :] = jnp.concatenate(dqs, axis=1)
            att = jnp.concatenate(atts, axis=1)
            dg_ref[rj, :] = _bf(da * att * (sg * (1.0 + gate * (1.0 - sg))))
            dkf = jnp.concatenate(dks, axis=1)
            dvf = jnp.concatenate(dvs, axis=1)
            for b in range(BAND_CHUNKS):
                dk_parts[j + b].append(dkf[CHUNK * b:CHUNK * (b + 1)])
                dv_parts[j + b].append(dvf[CHUNK * b:CHUNK * (b + 1)])
        gs_ref[0:1, :] += gs
        for rel in range(cps + BAND_CHUNKS - 1):
            r0 = pl.multiple_of(jnp.maximum(step * cps - (BAND_CHUNKS - 1) + rel, 0) * CHUNK, CHUNK)
            dk_ref[pl.ds(r0, CHUNK), :] += sum(dk_parts[rel][1:], dk_parts[rel][0])
            dv_ref[pl.ds(r0, CHUNK), :] += sum(dv_parts[rel][1:], dv_parts[rel][0])

        @pl.when(step == nsteps - 1)
        def _():
            for cp in _exchange_copies(ga_ref, got_ref, send_sems, recv_sems):
                cp.wait()

    return _pallas(
        body, name="attn_bwd", grid=(nsteps,),
        in_specs=[pl.BlockSpec((rows_step, D_ATT), lambda i: (i, 0)), pl.BlockSpec((t, D_KV), lambda i: (0, 0)),
                  pl.BlockSpec((t, D_KV), lambda i: (0, OV // D_KV)),
                  pl.BlockSpec((rows_step, D_ATT), lambda i: (i, OG // D_ATT)),
                  pl.BlockSpec((rows_step, D_ATT), lambda i: (i, D_SSD // D_ATT)),
                  pl.BlockSpec(memory_space=pltpu.SMEM), ANY],
        out_specs=[pl.BlockSpec((rows_step, D_ATT), lambda i: (i, 0)), pl.BlockSpec((rows_step, D_ATT), lambda i: (i, 0)),
                   pl.BlockSpec((t, D_KV), lambda i: (0, 0)), pl.BlockSpec((t, D_KV), lambda i: (0, 0)),
                   pl.BlockSpec((8, LANES), lambda i: (0, 0)), ANY],
        out_shape=[jax.ShapeDtypeStruct((t, D_ATT), F32), jax.ShapeDtypeStruct((t, D_ATT), BF16),
                   jax.ShapeDtypeStruct((t, D_KV), F32), jax.ShapeDtypeStruct((t, D_KV), F32),
                   jax.ShapeDtypeStruct((8, LANES), F32), _exchange_shape(ga)],
        scratch_shapes=_exchange_scratch(),
        compiler_params=_cp("arbitrary"))(qr, kr, proj, proj, dmix, sinks, ga)


def _ssd_bwd(dmix, y_ssd, xbc, proj, dt, acs, acst, states, d_skip_l, ssd_norm_w):
    t = xbc.shape[0]
    q = CHUNK
    nc = t // q
    gps = SSD_BWD_GROUPS_PER_STEP
    gw, sw = gps * GROUP_W, gps * D_STATE

    def body(dmix_ref, y_ref, z_ref, nw_ref, xs_ref, b_ref, c_ref, dt_ref, acs_ref, acst_ref, st_ref, dsk_ref,
             dz_ref, dxs_ref, db_ref, dc_ref, dacs_ref, ddt_ref, gnw_ref, gdsk_ref, dstate):
        @pl.when(pl.program_id(1) == 0)
        def _():
            dstate[...] = jnp.zeros_like(dstate)
            gnw_ref[...] = jnp.zeros_like(gnw_ref)
            gdsk_ref[...] = jnp.zeros_like(gdsk_ref)

        last_row = lax.broadcasted_iota(jnp.int32, (q, 1), 0) == q - 1
        lane = lax.broadcasted_iota(jnp.int32, (q, LANES), 1)
        lane1 = lax.broadcasted_iota(jnp.int32, (8, LANES), 1)
        for gi in range(gps):
            g = gps * pl.program_id(0) + gi
            cols = slice(GROUP_W * gi, GROUP_W * (gi + 1))
            scols = slice(D_STATE * gi, D_STATE * (gi + 1))
            y = y_ref[:, cols]
            z = z_ref[:, cols]
            sz = _sigmoid(z)
            silu_z = z * sz
            yg = y * silu_z
            rstd = lax.rsqrt(jnp.mean(yg * yg, axis=-1, keepdims=True) + EPS)
            n = yg * rstd
            dout = dmix_ref[:, cols]
            gnw_ref[:, cols] += jnp.sum(dout * n, axis=0, keepdims=True)
            dn = dout * nw_ref[:, cols]
            dyg = rstd * (dn - n * jnp.mean(dn * n, axis=-1, keepdims=True))
            dy = dyg * silu_z
            dz_ref[:, cols] = _bf(dyg * y * (sz * (1.0 + z * (1.0 - sz))))

            x = xs_ref[:, cols]
            bmb, cmb = _bf(b_ref[:, scols]), _bf(c_ref[:, scols])
            hv = _group_heads(g, gi, dt_ref[...], acs_ref[...], acst_ref, dsk_ref[...])
            dec = jnp.exp(jnp.where(_head_tri(q, True), hv.acs - hv.acs_row, NEG))
            dect = jnp.exp(jnp.where(_head_tri(q, False), hv.acs_row - hv.acs, NEG))
            b4 = jnp.concatenate([bmb] * HPG, axis=0)
            c4 = jnp.concatenate([cmb] * HPG, axis=0)
            m_all = _nt(cmb, b4) * dec
            mt_all = _nt(bmb, c4) * dect
            xdt = x * hv.dt
            xdt_b, dyb = _bf(xdt), _bf(dy)
            x_bd, dy_bd = _block_diag(xdt_b), _block_diag(dyb)
            s_prev = st_ref[0, gi]
            spb = _bf(s_prev)
            ds_new = dstate[gi]
            dsb = _bf(ds_new)
            e = jnp.exp(hv.acs)
            elast = jnp.exp(hv.acs_last)
            dte = jnp.exp(hv.acs_last - hv.acs)
            bds = _mm(bmb, dsb)
            dxdt = _mm(_bf(mt_all), dy_bd) + bds * dte
            dm = _nt(dyb, x_bd)
            dmt = _nt(xdt_b, dy_bd)
            dye = _bf(dy * e)
            dc_ref[:, scols] = _mm(_bf(dm * dec), b4) + _nt(dye, spb)
            db_ref[:, scols] = _mm(_bf(dmt * dect), c4) + _nt(_bf(xdt * dte), dsb)
            dstate[gi] = elast * ds_new + _tn(cmb, dye)
            dxs_ref[:, cols] = dxdt * hv.dt + hv.dsk * dy
            ddte_dte = bds * xdt * dte
            dacs_l = dm * m_all - dmt * mt_all + dy * _mm(cmb, spb) * e - ddte_dte
            dlast_l = (jnp.sum(ddte_dte, axis=0, keepdims=True)
                       + jnp.sum(s_prev * ds_new, axis=0, keepdims=True) * elast)
            ddt_l = dxdt * x
            gdsk_l = jnp.sum(dy * x, axis=0, keepdims=True)
            dacs_out = jnp.zeros((q, LANES), F32)
            ddt_out = jnp.zeros((q, LANES), F32)
            gdsk = jnp.zeros((8, LANES), F32)
            for r in range(HPG):
                dacs = _head_sums(dacs_l, r) + jnp.where(last_row, _head_sums(dlast_l, r), 0.0)
                dacs_out = jnp.where(lane == r, dacs, dacs_out)
                ddt_out = jnp.where(lane == r, _head_sums(ddt_l, r), ddt_out)
                gdsk = gdsk + jnp.where(lane1 == r, _head_sums(gdsk_l, r), 0.0)
            dacs_ref[:, LANES * gi:LANES * (gi + 1)] = dacs_out
            ddt_ref[:, LANES * gi:LANES * (gi + 1)] = ddt_out
            gdsk_ref[gi] += gdsk

    rev = lambda c: nc - 1 - c
    wide = pl.BlockSpec((q, gw), lambda g, c: (rev(c), g))
    return _pallas(
        body, name="ssd_bwd", grid=(GROUPS // gps, nc),
        in_specs=[wide, wide, wide, pl.BlockSpec((1, gw), lambda g, c: (0, g)), wide,
                  pl.BlockSpec((q, sw), lambda g, c: (rev(c), D_SSD // sw + g)),
                  pl.BlockSpec((q, sw), lambda g, c: (rev(c), (D_SSD + GROUPS * D_STATE) // sw + g)),
                  pl.BlockSpec((q, LANES), lambda g, c: (rev(c), 0)), pl.BlockSpec((q, LANES), lambda g, c: (rev(c), 0)),
                  pl.BlockSpec((1, gps * GROUPS, q), lambda g, c: (rev(c), g, 0)),
                  pl.BlockSpec((1, gps, D_STATE, GROUP_W), lambda g, c: (rev(c), g, 0, 0)),
                  pl.BlockSpec((1, LANES), lambda g, c: (0, 0))],
        out_specs=[wide, wide,
                   pl.BlockSpec((q, sw), lambda g, c: (rev(c), g)), pl.BlockSpec((q, sw), lambda g, c: (rev(c), g)),
                   pl.BlockSpec((q, gps * LANES), lambda g, c: (rev(c), g)),
                   pl.BlockSpec((q, gps * LANES), lambda g, c: (rev(c), g)),
                   pl.BlockSpec((1, gw), lambda g, c: (0, g)), pl.BlockSpec((gps, 8, LANES), lambda g, c: (g, 0, 0))],
        out_shape=[jax.ShapeDtypeStruct((t, D_SSD), BF16), jax.ShapeDtypeStruct((t, D_SSD), F32),
                   jax.ShapeDtypeStruct((t, GROUPS * D_STATE), F32), jax.ShapeDtypeStruct((t, GROUPS * D_STATE), F32),
                   jax.ShapeDtypeStruct((t, GROUPS * LANES), F32), jax.ShapeDtypeStruct((t, GROUPS * LANES), F32),
                   jax.ShapeDtypeStruct((1, D_SSD), F32), jax.ShapeDtypeStruct((GROUPS, 8, LANES), F32)],
        scratch_shapes=[pltpu.VMEM((gps, D_STATE, GROUP_W), F32)],
        compiler_params=_cp("parallel", "arbitrary"))(dmix, y_ssd, proj, ssd_norm_w, xbc, xbc, xbc, dt, acs, acst,
                                                      states, d_skip_l)


def _dt_bwd(dacs_g, ddt_g, dt, proj, dt_bias_l, a_log_l):
    t = dt.shape[0]
    q = CHUNK
    nc = t // q
    cps = _chunks_per_step(nc)
    rows = cps * q

    def body(dacs_ref, ddt_ref, dt_ref, raw_ref, bias_ref, alog_ref, draw_ref, ga_ref, gb_ref):
        @pl.when(pl.program_id(0) == 0)
        def _():
            ga_ref[...] = jnp.zeros_like(ga_ref)
            gb_ref[...] = jnp.zeros_like(gb_ref)

        lane = lax.broadcasted_iota(jnp.int32, (q, LANES), 1)
        ri = lax.broadcasted_iota(jnp.int32, (q, q), 0)
        ci = lax.broadcasted_iota(jnp.int32, (q, q), 1)
        triu = (ri <= ci).astype(F32)
        a = -jnp.exp(alog_ref[...])
        used = (lane & (GROUPS - 1)) < HPG
        ga = jnp.zeros((1, LANES), F32)
        gb = jnp.zeros((1, LANES), F32)
        for k in range(cps):
            rk = slice(q * k, q * (k + 1))
            dacs = jnp.zeros((q, LANES), F32)
            ddt = jnp.zeros((q, LANES), F32)
            for g in range(GROUPS):
                mask = (lane >= GROUPS * g) & (lane < GROUPS * g + HPG)
                sl = slice(LANES * g, LANES * (g + 1))
                if g == 0:
                    dacs = jnp.where(mask, dacs_ref[rk, sl], dacs)
                    ddt = jnp.where(mask, ddt_ref[rk, sl], ddt)
                else:
                    dacs = jnp.where(mask, pltpu.roll(dacs_ref[rk, sl], GROUPS * g, 1), dacs)
                    ddt = jnp.where(mask, pltpu.roll(ddt_ref[rk, sl], GROUPS * g, 1), ddt)
            dda = jnp.dot(triu, dacs, preferred_element_type=F32, precision=HI)
            row = pl.program_id(0) * rows + q * k + lax.broadcasted_iota(jnp.int32, (q, LANES), 0)
            dsp = jnp.where((row >= PAD_LEAD) & used, dda * a + ddt, 0.0)
            draw = dsp * _sigmoid(raw_ref[rk, :] + bias_ref[...])
            draw_ref[rk, :] = _bf(draw)
            gb = gb + jnp.sum(draw, axis=0, keepdims=True)
            ga = ga + jnp.sum(jnp.where(used, dda * dt_ref[rk, :], 0.0), axis=0, keepdims=True)
        gb_ref[0:1, :] += gb
        ga_ref[0:1, :] += ga * a

    return _pallas(
        body, name="dt_bwd", grid=(nc // cps,),
        in_specs=[pl.BlockSpec((rows, GROUPS * LANES), lambda c: (c, 0)),
                  pl.BlockSpec((rows, GROUPS * LANES), lambda c: (c, 0)),
                  pl.BlockSpec((rows, LANES), lambda c: (c, 0)), pl.BlockSpec((rows, LANES), lambda c: (c, ODT // LANES)),
                  pl.BlockSpec((1, LANES), lambda c: (0, 0)), pl.BlockSpec((1, LANES), lambda c: (0, 0))],
        out_specs=[pl.BlockSpec((rows, LANES), lambda c: (c, 0)), pl.BlockSpec((8, LANES), lambda c: (0, 0)),
                   pl.BlockSpec((8, LANES), lambda c: (0, 0))],
        out_shape=[jax.ShapeDtypeStruct((t, LANES), BF16), jax.ShapeDtypeStruct((8, LANES), F32),
                   jax.ShapeDtypeStruct((8, LANES), F32)],
        compiler_params=_cp("arbitrary"))(dacs_g, ddt_g, dt, proj, dt_bias_l, a_log_l)


def _conv_bwd(dseg, proj, conv_w, conv_b, col_off, name):
    t, width = dseg.shape
    tc = 128
    off_p = (OXS + col_off) // tc
    off_w = col_off // tc

    def body(d_ref, x_ref, w_ref, b_ref, dx_ref, gw_ref, gb_ref, xp, dup):
        xp[0:8, :] = jnp.zeros((8, tc), F32)
        xp[8:t + 8, :] = x_ref[...]
        w = w_ref[...]
        u = (b_ref[...] + w[3:4, :] * xp[8:t + 8, :] + w[2:3, :] * xp[7:t + 7, :]
             + w[1:2, :] * xp[6:t + 6, :] + w[0:1, :] * xp[5:t + 5, :])
        su = _sigmoid(u)
        du = d_ref[...] * (su * (1.0 + u * (1.0 - su)))
        dup[0:t, :] = du
        dup[t:t + 8, :] = jnp.zeros((8, tc), F32)
        dx_ref[...] = _bf(w[3:4, :] * du + w[2:3, :] * dup[1:t + 1, :] + w[1:2, :] * dup[2:t + 2, :]
                          + w[0:1, :] * dup[3:t + 3, :])
        gb_ref[...] = jnp.sum(du, axis=0, keepdims=True)
        gw_ref[...] = jnp.concatenate(
            [jnp.sum(du * xp[5 + k:t + 5 + k, :], axis=0, keepdims=True) for k in range(CONV_WIDTH)], axis=0)

    return _pallas(
        body, name=name, grid=(width // tc,),
        in_specs=[pl.BlockSpec((t, tc), lambda j: (0, j)), pl.BlockSpec((t, tc), lambda j: (0, j + off_p)),
                  pl.BlockSpec((CONV_WIDTH, tc), lambda j: (0, j + off_w)), pl.BlockSpec((1, tc), lambda j: (0, j + off_w))],
        out_specs=[pl.BlockSpec((t, tc), lambda j: (0, j)), pl.BlockSpec((CONV_WIDTH, tc), lambda j: (0, j)),
                   pl.BlockSpec((1, tc), lambda j: (0, j))],
        out_shape=[jax.ShapeDtypeStruct((t, width), BF16), jax.ShapeDtypeStruct((CONV_WIDTH, width), F32),
                   jax.ShapeDtypeStruct((1, width), F32)],
        scratch_shapes=[pltpu.VMEM((t + 8, tc), F32), pltpu.VMEM((t + 8, tc), F32)],
        compiler_params=_cp("parallel"))(dseg, proj, conv_w, conv_b)


def _dinproj(segs, w_re, hpad, norm_w, dy_t, ga):
    t = segs[0].shape[0]
    d = hpad.shape[1]
    tm, tk = _tile(t, 416), SEG_TILE
    counts = [s.shape[1] // tk for s in segs]
    firsts = [sum(counts[:s]) for s in range(len(segs))]
    nk = sum(counts)
    assert nk * tk == w_re.shape[1]
    ni = t // tm
    ns = len(segs)

    def body(*refs):
        seg_refs = refs[:ns]
        w_ref, h_ref, nw_ref, dy_ref, ga_ref, dh_ref, gnw_ref, got_ref, acc, send_sems, recv_sems = refs[ns:]
        i, k = pl.program_id(0), pl.program_id(1)

        @pl.when((i == 0) & (k == 0))
        def _():
            for cp in _exchange_copies(ga_ref, got_ref, send_sems, recv_sems):
                cp.start()
            gnw_ref[...] = jnp.zeros_like(gnw_ref)

        @pl.when(k == 0)
        def _():
            acc[...] = jnp.zeros_like(acc)

        for s in range(ns):
            @pl.when((k >= firsts[s]) & (k < firsts[s] + counts[s]))
            def _(s=s):
                acc[...] += _nt(seg_refs[s][...], w_ref[...])

        @pl.when(k == nk - 1)
        def _():
            h = h_ref[...]
            rstd = lax.rsqrt(jnp.mean(h * h, axis=-1, keepdims=True) + EPS)
            nrm = h * rstd
            dhn = acc[...]
            gnw_ref[...] += jnp.sum(dhn * nrm, axis=0, keepdims=True)
            dn = dhn * nw_ref[...]
            dh_ref[...] = rstd * (dn - nrm * jnp.mean(dn * nrm, axis=-1, keepdims=True)) + dy_ref[...]

        @pl.when((i == ni - 1) & (k == nk - 1))
        def _():
            for cp in _exchange_copies(ga_ref, got_ref, send_sems, recv_sems):
                cp.wait()

    seg_specs = [pl.BlockSpec((tm, tk), functools.partial(lambda i, k, f0, n0: (i, jnp.clip(k - f0, 0, n0 - 1)),
                                                          f0=firsts[s], n0=counts[s])) for s in range(ns)]
    return _pallas(
        body, name="dinproj", grid=(ni, nk),
        in_specs=seg_specs + [pl.BlockSpec((d, tk), lambda i, k: (0, k)),
                              pl.BlockSpec((tm, d), lambda i, k: (i, 0)), pl.BlockSpec((1, d), lambda i, k: (0, 0)),
                              pl.BlockSpec((tm, d), lambda i, k: (i, 0)), ANY],
        out_specs=[pl.BlockSpec((tm, d), lambda i, k: (i, 0)), pl.BlockSpec((1, d), lambda i, k: (0, 0)), ANY],
        out_shape=[jax.ShapeDtypeStruct((t, d), F32), jax.ShapeDtypeStruct((1, d), F32), _exchange_shape(ga)],
        scratch_shapes=[pltpu.VMEM((tm, d), F32)] + _exchange_scratch(),
        compiler_params=_cp("arbitrary", "arbitrary"))(*segs, w_re, hpad, norm_w, dy_t, ga)


def _spread_heads(v):
    v = jnp.pad(v.reshape(GROUPS, HPG), ((0, 0), (0, GROUPS - HPG))).reshape(1, GROUPS * GROUPS)
    return jnp.pad(v, ((0, 0), (0, LANES - GROUPS * GROUPS)))


def _gather_heads(v):
    return v[0:1, :GROUPS * GROUPS].reshape(GROUPS, GROUPS)[:, :HPG].reshape(1, SSD_HEADS)


def _rope_tables(t):
    half = HEAD_DIM // 2
    inv = ROPE_THETA ** (-jnp.arange(half, dtype=F32) / half)
    pos = (jnp.arange(t) - PAD_LEAD).astype(F32)
    ang = pos[:, None] * inv[None, :]
    cos, sin = jnp.cos(ang), jnp.sin(ang)
    cos_t = jnp.concatenate([cos, cos, cos, cos], axis=1)
    sin_t = jnp.concatenate([-sin, sin, -sin, sin], axis=1)
    return cos_t, sin_t


def _column_pieces():
    runs = [(0, OB + 2 * GROUPS * D_STATE, 0)]
    o = OB + 2 * GROUPS * D_STATE
    runs += [(o + HPG * g, HPG, ODT + GROUPS * g) for g in range(GROUPS)]
    o += SSD_HEADS
    for width, dst in ((D_ATT, OQ), (D_KV, OK), (D_KV, OV), (D_ATT, OG)):
        runs.append((o, width, dst))
        o += width
    assert o == D_IN
    pieces = []
    for o0, width, dst in runs:
        for j in range(N_SHARD):
            lo, hi = max(o0, W_IN_SHARD * j), min(o0 + width, W_IN_SHARD * (j + 1))
            if lo < hi:
                pieces.append((j, lo - W_IN_SHARD * j, hi - W_IN_SHARD * j, dst + lo - o0))
    return pieces


def _shards_to_re(w_all):
    _, k, _ = w_all.shape
    tr = 256

    def body(x_ref, o_ref):
        o_ref[:, ODT:ODT + DT_SLAB] = jnp.zeros((tr, DT_SLAB), o_ref.dtype)
        for j, c0, c1, d0 in _column_pieces():
            o_ref[:, d0:d0 + c1 - c0] = x_ref[j, :, c0:c1]

    return _pallas(body, name="shards_to_re", grid=(k // tr,),
                   in_specs=[pl.BlockSpec((N_SHARD, tr, W_IN_SHARD), lambda i: (0, i, 0))],
                   out_specs=pl.BlockSpec((tr, N_RE), lambda i: (i, 0)),
                   out_shape=jax.ShapeDtypeStruct((k, N_RE), w_all.dtype), compiler_params=_cp("parallel"))(w_all)


def _pair_add_to_shards(parts, got, pieces, shard_rows, core, name):
    n = parts[0].shape[1]
    hn = n // 2
    tc = 128
    nt = hn // tc
    ns = len(parts)
    starts = [sum(p.shape[0] for p in parts[:s]) for s in range(ns)]
    moves = []
    for j, c0, c1, d0 in pieces:
        for s, p in enumerate(parts):
            lo, hi = max(d0, starts[s]), min(d0 + c1 - c0, starts[s] + p.shape[0])
            if lo < hi:
                moves.append((s, lo - starts[s], j, c0 + lo - d0, hi - lo))
    assert sum(m[4] for m in moves) == N_SHARD * shard_rows

    def body(core_ref, *refs):
        own, theirs, o_ref, acc = refs[:ns], refs[ns:2 * ns], refs[2 * ns], refs[2 * ns + 1]
        for s, r0, j, c0, rows in moves:
            acc[j, c0:c0 + rows, :] = own[s][r0:r0 + rows, :] + theirs[s][r0:r0 + rows, :]
        o_ref[...] = _bf(acc[...])

    return _pallas(
        body, name=name,
        grid_spec=pltpu.PrefetchScalarGridSpec(
            num_scalar_prefetch=1, grid=(nt,),
            in_specs=[pl.BlockSpec((p.shape[0], tc), lambda i, core_ref: (0, core_ref[0] * nt + i)) for p in parts]
            + [pl.BlockSpec((p.shape[0], tc), lambda i, core_ref: (0, i)) for p in parts],
            out_specs=pl.BlockSpec((N_SHARD, shard_rows, tc), lambda i, core_ref: (0, 0, i)),
            scratch_shapes=[pltpu.VMEM((N_SHARD, shard_rows, tc), F32)]),
        out_shape=jax.ShapeDtypeStruct((N_SHARD, shard_rows, hn), BF16),
        compiler_params=_cp("parallel"))(core, *parts, *got)


def _local_step(x, target, meta, norm_pre_w, w_re, conv_w, conv_b, dt_bias, a_log, d_skip, ssd_norm_w, sinks,
                w_out_shard, norm_post_w, place):
    seq = x.shape[0]
    t = PAD_LEAD + N_META + seq
    hpad = jnp.concatenate([jnp.zeros((PAD_LEAD, D_MODEL), F32), meta, x], axis=0)
    dt_bias_l, a_log_l, d_skip_l = _spread_heads(dt_bias), _spread_heads(a_log), _spread_heads(d_skip)
    cos_t, sin_t = _rope_tables(t)
    sink_v = sinks.reshape(Q_HEADS)

    proj, hn, w_out_all = _inproj(hpad, norm_pre_w, w_re, w_out_shard)
    w_out = w_out_all.reshape(D_MIX, D_MODEL)
    xbc = _conv_fwd(proj, conv_w, conv_b)
    dt, acs, acst = _dt_prep(proj, dt_bias_l, a_log_l)
    y_ssd, ymix, states = _ssd_fwd(xbc, proj, dt, acs, acst, d_skip_l, ssd_norm_w)
    qr, kr = _rope(proj, OQ, proj, OK, cos_t, sin_t)
    amix = _attn_fwd(qr, kr, proj, sink_v)
    out = _outproj(ymix, amix, w_out)
    dout, dy_t, loss_blk, g_norm_post = _post_loss(out, x, target, norm_post_w)

    dmix = _nt_matmul(dout, w_out, "dmix")
    g_out_parts = [_tn_matmul(ymix, dout, "gw_out_y"), _tn_matmul(amix, dout, "gw_out_a")]
    ga_out = _reduce_pair(g_out_parts, [(j, 0, W_OUT_SHARD, W_OUT_SHARD * j) for j in range(N_SHARD)], W_OUT_SHARD,
                          place, "gw_out")
    dq_r, dg, dk_r, dv, gs, slabs_out = _attn_bwd(qr, kr, proj, dmix, sink_v, ga_out)
    g_w_out = _reduce_finish(ga_out, slabs_out, place, "gw_out")
    dq, dk = _rope(dq_r, 0, dk_r, 0, cos_t, -sin_t)
    dz, dxs, db, dc, dacs_g, ddt_g, g_ssd_norm, gdsk = _ssd_bwd(dmix, y_ssd, xbc, proj, dt, acs, acst, states,
                                                                d_skip_l, ssd_norm_w)
    draw, ga, gb = _dt_bwd(dacs_g, ddt_g, dt, proj, dt_bias_l, a_log_l)
    dxs_p, gcw0, gcb0 = _conv_bwd(dxs, proj, conv_w, conv_b, 0, "conv_bwd_x")
    db_p, gcw1, gcb1 = _conv_bwd(db, proj, conv_w, conv_b, D_SSD, "conv_bwd_b")
    dc_p, gcw2, gcb2 = _conv_bwd(dc, proj, conv_w, conv_b, D_SSD + GROUPS * D_STATE, "conv_bwd_c")
    tail = jnp.concatenate([dk, _bf(dv), draw, jnp.zeros((t, DT_SLAB - LANES), BF16)], axis=1)
    segs = [dz, dxs_p, db_p, dc_p, dq, dg, tail]
    g_parts = [_tn_matmul(seg, hn, "gw_in_%d" % s) for s, seg in enumerate(segs)]
    ga_in = _reduce_pair(g_parts, _column_pieces(), W_IN_SHARD, place, "gw_in")
    dh, g_norm_pre, slabs_in = _dinproj(segs, w_re, hpad, norm_pre_w, dy_t, ga_in)
    g_w_in = _reduce_finish(ga_in, slabs_in, place, "gw_in")

    gdsk_l = jnp.concatenate([gdsk[g, 0:1, 0:GROUPS] for g in range(GROUPS)], axis=1)
    gdsk_l = jnp.pad(gdsk_l, ((0, 0), (0, LANES - GROUPS * GROUPS)))
    grads = dict(
        meta_tokens=dh[PAD_LEAD:ROW0], norm_pre_w=g_norm_pre, w_in=g_w_in,
        conv_w=jnp.concatenate([gcw0, gcw1, gcw2], axis=1), conv_b=jnp.concatenate([gcb0, gcb1, gcb2], axis=1),
        dt_bias=_gather_heads(gb), a_log=_gather_heads(ga), d_skip=_gather_heads(gdsk_l), ssd_norm_w=g_ssd_norm,
        attn_sinks=gs[0:1, :Q_HEADS], w_out=g_w_out, norm_post_w=g_norm_post)
    return loss_blk[0, 0], dh[ROW0:], grads


ANY = pl.BlockSpec(memory_space=pl.ANY)
MESH = pl.DeviceIdType.MESH
GATHER_CHUNKS = 4
PAIR_CHUNKS = 8
JOIN_CHUNKS = 8


def _rcopy(src, dst, ssem, rsem, dev):
    return pltpu.make_async_remote_copy(src_ref=src, dst_ref=dst, send_sem=ssem, recv_sem=rsem, device_id=dev,
                                        device_id_type=MESH)


def _place():
    x, y, c = lax.axis_index("x"), lax.axis_index("y"), lax.axis_index("c")
    chips = [(1 - x, y), (x, 1 - y), (1 - x, 1 - y)]
    return x, y, c, chips


def _gather_plan(x_ref, out_ref, send_sems, recv_sems, local_sems, hr, kc):
    ch = hr // kc
    assert ch * kc == hr and ch % 16 == 0
    x, y, c, chips = _place()
    me = 2 * x + y
    sibling = (x, y, 1 - c)

    def piece(chip, hc, k):
        return out_ref.at[chip, pl.ds(hc * hr + k * ch, ch), :]

    def local():
        return [pltpu.make_async_copy(x_ref.at[pl.ds(k * ch, ch), :], out_ref.at[me, pl.ds(k * ch, ch), :],
                                      local_sems.at[k]) for k in range(2 * kc)]

    def first():
        return [_rcopy(x_ref.at[pl.ds(c * hr + k * ch, ch), :], piece(me, c, k), send_sems.at[j * kc + k],
                       recv_sems.at[j * kc + k], (*chip, c)) for j, chip in enumerate(chips) for k in range(kc)]

    def passed(hc):
        return [_rcopy(piece(2 * chip[0] + chip[1], hc, k), piece(2 * chip[0] + chip[1], hc, k),
                       send_sems.at[(3 + j) * kc + k], recv_sems.at[(3 + j) * kc + k], sibling)
                for j, chip in enumerate(chips) for k in range(kc)]

    def arrivals():
        return [_rcopy(piece(2 * chip[0] + chip[1], c, k), piece(2 * chip[0] + chip[1], c, k), send_sems.at[j * kc + k],
                       recv_sems.at[j * kc + k], (*chip, c)) for j, chip in enumerate(chips) for k in range(kc)]

    def start():
        for cp in local() + first():
            cp.start()

    def forward():
        for arrived in arrivals():
            arrived.wait_recv()
        for fw in passed(c):
            fw.start()

    def finish():
        for cp in passed(1 - c):
            cp.wait_recv()
        for cp in first() + passed(c):
            cp.wait_send()
        for cp in local():
            cp.wait()

    return start, forward, finish


def _gather_shards(shard, name, kc, chip):
    r, n = shard.shape
    hr = r // 2
    qr = hr // 2
    ch = qr // kc
    assert ch * kc == qr and ch % 16 == 0
    nflow = 12
    tr = 256

    def body(x_ref, out_ref, send_sems, recv_sems):
        x, y, c, _ = _place()
        me, cxn, cyn, cdg = 2 * x + y, 2 * (1 - x) + y, 2 * x + 1 - y, 2 * (1 - x) + 1 - y
        xn, yn, sibling = (1 - x, y, c), (x, 1 - y, c), (x, y, 1 - c)

        def piece(chip, hc, part, k):
            return out_ref.at[chip, pl.ds(hc * hr + part * qr + k * ch, ch), :]

        def own(part, k):
            return x_ref.at[pl.ds(c * hr + part * qr + k * ch, ch), :]

        def sems(flow, k):
            return send_sems.at[flow * kc + k], recv_sems.at[flow * kc + k]

        def arrival(flow, chip, hc, part, k):
            return _rcopy(piece(chip, hc, part, k), piece(chip, hc, part, k), *sems(flow, k), sibling)

        sends = []
        for flow, part, peer in ((0, 0, xn), (1, 1, yn), (2, 0, yn), (3, 1, xn)):
            sends += [_rcopy(own(part, k), piece(me, c, part, k), *sems(flow, k), peer) for k in range(kc)]
        for cp in sends:
            cp.start()
        landing = ((0, cxn, 0), (1, cyn, 1), (2, cyn, 0), (3, cxn, 1), (4, cdg, 0), (5, cdg, 1))
        for i, (flow, chip, part) in enumerate(landing):
            for k in range(kc):
                arrival(flow, chip, c, part, k).wait_recv()
                if flow < 2:
                    on = _rcopy(piece(chip, c, part, k), piece(chip, c, part, k), *sems(4 + flow, k),
                                yn if flow == 0 else xn)
                    on.start()
                    sends.append(on)
                fw = _rcopy(piece(chip, c, part, k), piece(chip, c, part, k), *sems(6 + i, k), sibling)
                fw.start()
                sends.append(fw)
        for i, (flow, chip, part) in enumerate(landing):
            for k in range(kc):
                arrival(6 + i, chip, 1 - c, part, k).wait_recv()
        for cp in sends:
            cp.wait_send()

    full = jax.ShapeDtypeStruct((N_SHARD, r, n), shard.dtype)
    others = _pallas(
        body, name=name, in_specs=[ANY], out_specs=ANY, out_shape=full,
        scratch_shapes=[pltpu.SemaphoreType.DMA((nflow * kc,)), pltpu.SemaphoreType.DMA((nflow * kc,))])(shard)

    def place(chip_ref, own_ref, all_ref, o_ref):
        o_ref[0] = own_ref[...]

    return _pallas(
        place, name=name + "_own",
        grid_spec=pltpu.PrefetchScalarGridSpec(
            num_scalar_prefetch=1, grid=(r // tr,),
            in_specs=[pl.BlockSpec((tr, n), lambda i, chip_ref: (i, 0)), ANY],
            out_specs=pl.BlockSpec((1, tr, n), lambda i, chip_ref: (chip_ref[0], i, 0))),
        out_shape=full, input_output_aliases={2: 0}, compiler_params=_cp("parallel"))(chip, shard, others)


def _pair_send(parts, name):
    n = parts[0].shape[1]
    hn = n // 2
    kc = PAIR_CHUNKS
    cw = hn // kc
    assert cw * kc == hn and cw % LANES == 0
    ns = len(parts)

    def body(*refs):
        srcs, dsts, send_sems, recv_sems = refs[:ns], refs[ns:2 * ns], refs[2 * ns], refs[2 * ns + 1]
        x, y, c, _ = _place()
        cps = [_rcopy(srcs[s].at[:, pl.ds((1 - c) * hn + k * cw, cw)], dsts[s].at[:, pl.ds(k * cw, cw)],
                      send_sems.at[s * kc + k], recv_sems.at[s * kc + k], (x, y, 1 - c))
               for s in range(ns) for k in range(kc)]
        for cp in cps:
            cp.start()
        for cp in cps:
            cp.wait()

    return _pallas(
        body, name=name, in_specs=[ANY] * ns, out_specs=[ANY] * ns,
        out_shape=[jax.ShapeDtypeStruct((p.shape[0], hn), F32) for p in parts],
        scratch_shapes=[pltpu.SemaphoreType.DMA((ns * kc,)), pltpu.SemaphoreType.DMA((ns * kc,))])(*parts)


REDUCE_TILE = 256


def _exchange_copies(g_ref, got_ref, send_sems, recv_sems):
    hn = g_ref.shape[2]
    kc = GATHER_CHUNKS
    cw = hn // kc
    assert cw * kc == hn and cw % LANES == 0
    x, y, c, chips = _place()
    return [_rcopy(g_ref.at[2 * chip[0] + chip[1], :, pl.ds(k * cw, cw)], got_ref.at[j, :, pl.ds(k * cw, cw)],
                   send_sems.at[j * kc + k], recv_sems.at[j * kc + k], (*chip, c))
            for j, chip in enumerate(chips) for k in range(kc)]


def _exchange_scratch():
    return [pltpu.SemaphoreType.DMA((3 * GATHER_CHUNKS,)), pltpu.SemaphoreType.DMA((3 * GATHER_CHUNKS,))]


def _exchange_shape(ga):
    return jax.ShapeDtypeStruct((3,) + ga.shape[1:], ga.dtype)


def _chip_sum(ga, got, place, name):
    _, r, hn = ga.shape
    tc = REDUCE_TILE
    nt = hn // tc

    def body(place_ref, own_ref, got_ref, o_ref):
        acc = own_ref[0].astype(F32)
        for j in range(3):
            acc = acc + got_ref[j].astype(F32)
        o_ref[...] = acc

    return _pallas(
        body, name=name,
        grid_spec=pltpu.PrefetchScalarGridSpec(
            num_scalar_prefetch=1, grid=(nt,),
            in_specs=[pl.BlockSpec((1, r, tc), lambda i, place_ref: (place_ref[0], 0, i)),
                      pl.BlockSpec((3, r, tc), lambda i, place_ref: (0, 0, i))],
            out_specs=pl.BlockSpec((r, tc), lambda i, place_ref: (0, place_ref[1] * nt + i))),
        out_shape=jax.ShapeDtypeStruct((r, 2 * hn), F32), compiler_params=_cp("parallel"))(place, ga, got)


def _pair_join(buf, name):
    r, n = buf.shape
    hn = n // 2
    kc = JOIN_CHUNKS
    cw = hn // kc
    assert cw * kc == hn and cw % LANES == 0

    def body(in_ref, out_ref, send_sems, recv_sems):
        x, y, c, _ = _place()
        cps = [_rcopy(out_ref.at[:, pl.ds(c * hn + k * cw, cw)], out_ref.at[:, pl.ds(c * hn + k * cw, cw)],
                      send_sems.at[k], recv_sems.at[k], (x, y, 1 - c)) for k in range(kc)]
        for cp in cps:
            cp.start()
        for k in range(kc):
            cols = out_ref.at[:, pl.ds((1 - c) * hn + k * cw, cw)]
            _rcopy(cols, cols, send_sems.at[k], recv_sems.at[k], (x, y, 1 - c)).wait_recv()
        for cp in cps:
            cp.wait_send()

    return _pallas(
        body, name=name, in_specs=[ANY], out_specs=ANY, out_shape=jax.ShapeDtypeStruct((r, n), F32),
        input_output_aliases={0: 0},
        scratch_shapes=[pltpu.SemaphoreType.DMA((kc,)), pltpu.SemaphoreType.DMA((kc,))])(buf)


def _reduce_pair(parts, pieces, shard_rows, place, tag):
    got = _pair_send(parts, tag + "_pair_send")
    return _pair_add_to_shards(parts, got, pieces, shard_rows, place[1:2], tag + "_pair_add")


def _reduce_finish(ga, slabs, place, tag):
    return _pair_join(_chip_sum(ga, slabs, place, tag + "_chip_sum"), tag + "_pair_join")


def _allreduce_small(p, name):
    rows, n = p.shape
    ndev = 8

    def body(p_ref, out_ref, slots, send_sems, recv_sems):
        x, y, c, _ = _place()
        my = 4 * x + 2 * y + c
        slots[my] = p_ref[...]
        cps = []
        for k in range(1, ndev):
            kx, ky, kc = (k >> 2) & 1, (k >> 1) & 1, k & 1
            peer = (x ^ kx, y ^ ky, c ^ kc)
            cp = _rcopy(p_ref, slots.at[my], send_sems.at[k - 1], recv_sems.at[k - 1], peer)
            cp.start()
            cps.append(cp)
        for k in range(1, ndev):
            _rcopy(p_ref, slots.at[my ^ k], send_sems.at[k - 1], recv_sems.at[k - 1], (x, y, c)).wait_recv()
        for cp in cps:
            cp.wait_send()
        acc = slots[0]
        for j in range(1, ndev):
            acc = acc + slots[j]
        out_ref[...] = acc

    vm = pl.BlockSpec(memory_space=pltpu.VMEM)
    return _pallas(
        body, name=name, in_specs=[vm], out_specs=vm, out_shape=jax.ShapeDtypeStruct((rows, n), F32),
        scratch_shapes=[pltpu.VMEM((ndev, rows, n), F32), pltpu.SemaphoreType.DMA((ndev - 1,)),
                        pltpu.SemaphoreType.DMA((ndev - 1,))])(p)


def _adamw(w, g, m, v, name):
    r, n = w.shape
    tr = _tile(r, 256, 8)
    c1 = 1.0 / (1.0 - ADAM_B1 ** ADAM_STEP)
    c2 = 1.0 / (1.0 - ADAM_B2 ** ADAM_STEP)

    def body(w_ref, g_ref, m_ref, v_ref, d_ref, mo_ref, vo_ref):
        gv = g_ref[...]
        mn = ADAM_B1 * m_ref[...] + (1.0 - ADAM_B1) * gv
        vn = ADAM_B2 * v_ref[...] + (1.0 - ADAM_B2) * (gv * gv)
        d_ref[...] = -ADAM_LR * ((mn * c1) / (jnp.sqrt(vn * c2) + ADAM_EPS) + ADAM_WD * w_ref[...])
        mo_ref[...] = mn
        vo_ref[...] = vn

    spec = pl.BlockSpec((tr, n), lambda i: (i, 0))
    shp = jax.ShapeDtypeStruct((r, n), F32)
    return _pallas(body, name=name, grid=(r // tr,), in_specs=[spec] * 4, out_specs=[spec] * 3, out_shape=[shp] * 3,
                   compiler_params=_cp("parallel"))(w, g, m, v)


PACK_W = 1024
SMALL_REPL = ("norm_pre_w", "conv_b", "ssd_norm_w", "norm_post_w")
SMALL_HEAD = ("dt_bias", "a_log", "d_skip", "attn_sinks")


def _rows(a):
    return a.reshape(-1, PACK_W)


def _head_row(vals, extra=None):
    parts = [vals[n].reshape(1, -1) for n in SMALL_HEAD]
    if extra is not None:
        parts.append(extra.reshape(1, 1))
    row = jnp.concatenate(parts, axis=1)
    return jnp.pad(row, ((0, 0), (0, PACK_W - row.shape[1])))


def _pad_rows(a, rows):
    return jnp.pad(a, ((0, rows - a.shape[0]), (0, 0)))


def _pack_repl(vals, extra=None):
    body = jnp.concatenate([_rows(vals[n]) for n in SMALL_REPL] + [_head_row(vals, extra)], axis=0)
    return _pad_rows(body, 16)


def _unpack_repl(buf):
    out, r = {}, 0
    for n, k in zip(SMALL_REPL, (2, 4, 2, 2)):
        out[n] = buf[r:r + k].reshape(1, k * PACK_W)
        r += k
    col = 0
    for n, k in zip(SMALL_HEAD, (32, 32, 32, 16)):
        out[n] = buf[r:r + 1, col:col + k]
        col += k
    return out, buf[r, col]


def kernel(x, meta_tokens, norm_pre_w, w_in, conv_w, conv_b, dt_bias, a_log, d_skip, ssd_norm_w, attn_sinks, w_out, norm_post_w, loss_target, m_meta_tokens, m_norm_pre_w, m_w_in, m_conv_w, m_conv_b, m_dt_bias, m_a_log, m_d_skip, m_ssd_norm_w, m_attn_sinks, m_w_out, m_norm_post_w, v_meta_tokens, v_norm_pre_w, v_w_in, v_conv_w, v_conv_b, v_dt_bias, v_a_log, v_d_skip, v_ssd_norm_w, v_attn_sinks, v_w_out, v_norm_post_w):
    names = ("meta_tokens", "norm_pre_w", "w_in", "conv_w", "conv_b", "dt_bias", "a_log", "d_skip", "ssd_norm_w",
             "attn_sinks", "w_out", "norm_post_w")
    w = dict(zip(names, (meta_tokens, norm_pre_w, w_in, conv_w, conv_b, dt_bias, a_log, d_skip, ssd_norm_w, attn_sinks,
                         w_out, norm_post_w)))
    m = dict(zip(names, (m_meta_tokens, m_norm_pre_w, m_w_in, m_conv_w, m_conv_b, m_dt_bias, m_a_log, m_d_skip,
                         m_ssd_norm_w, m_attn_sinks, m_w_out, m_norm_post_w)))
    v = dict(zip(names, (v_meta_tokens, v_norm_pre_w, v_w_in, v_conv_w, v_conv_b, v_dt_bias, v_a_log, v_d_skip,
                         v_ssd_norm_w, v_attn_sinks, v_w_out, v_norm_post_w)))
    cx, cy, cc = lax.axis_index("x"), lax.axis_index("y"), lax.axis_index("c")
    chip = 2 * cx + cy
    meta_cols = D_MODEL // N_SHARD
    conv_cols = D_CONV // N_SHARD

    place = jnp.stack([chip, cc]).astype(jnp.int32)
    w_re = _shards_to_re(_gather_shards(_bf(w_in[0]), "gather_w_in", GATHER_CHUNKS, place[0:1]))
    conv_z = lax.dynamic_update_slice(jnp.zeros((CONV_WIDTH, D_CONV), F32), conv_w[0], (0, chip * conv_cols))
    meta_z = lax.dynamic_update_slice(jnp.zeros((N_META, D_MODEL), F32), meta_tokens, (0, chip * meta_cols))
    small = jnp.concatenate([_rows(conv_z), _rows(meta_z)], axis=0)
    small = _allreduce_small(jnp.where(cc == 0, small, 0.0), "gather_small")
    conv_full = small[0:16].reshape(CONV_WIDTH, D_CONV)
    meta_full = small[16:48].reshape(N_META, D_MODEL)

    loss_dev, grad_x, g = _local_step(x[0], loss_target[0], meta_full, norm_pre_w, w_re, conv_full, conv_b, dt_bias,
                                      a_log, d_skip, ssd_norm_w, attn_sinks, _bf(w_out[0]), norm_post_w, place)
    g_w_in, g_w_out = g["w_in"], g["w_out"]

    packed = jnp.concatenate([_rows(g["conv_w"]), _rows(g["meta_tokens"]), _pack_repl(g, loss_dev)], axis=0)
    red = _allreduce_small(packed, "reduce_small")
    g_conv_full = red[0:16].reshape(CONV_WIDTH, D_CONV)
    g_meta_full = red[16:48].reshape(N_META, D_MODEL)
    g_small, loss = _unpack_repl(red[48:64])
    grads = dict(g_small)
    grads["w_in"] = g_w_in
    grads["w_out"] = g_w_out
    grads["conv_w"] = lax.dynamic_slice(g_conv_full, (0, chip * conv_cols), (CONV_WIDTH, conv_cols))
    grads["meta_tokens"] = lax.dynamic_slice(g_meta_full, (0, chip * meta_cols), (N_META, meta_cols))

    upd = {}
    upd["w_in"] = [jnp.swapaxes(a, 0, 1) for a in _adamw(jnp.swapaxes(w_in[0], 0, 1), g_w_in, jnp.swapaxes(m_w_in[0], 0, 1),
                                                         jnp.swapaxes(v_w_in[0], 0, 1), "adamw_w_in")]
    grads["w_in"] = jnp.swapaxes(g_w_in, 0, 1)
    upd["w_out"] = _adamw(w_out[0], g_w_out, m_w_out[0], v_w_out[0], "adamw_w_out")

    def pack_small(vals, conv, meta):
        return jnp.concatenate([_pad_rows(conv.reshape(CONV_WIDTH, conv_cols), 8), _rows(meta), _pack_repl(vals)], axis=0)

    sm = _adamw(pack_small(w, w["conv_w"], w["meta_tokens"]), pack_small(grads, grads["conv_w"], grads["meta_tokens"]),
                pack_small(m, m["conv_w"], m["meta_tokens"]), pack_small(v, v["conv_w"], v["meta_tokens"]),
                "adamw_small")
    for n in names:
        if n not in ("w_in", "w_out"):
            upd[n] = [None, None, None]
    for k, buf in enumerate(sm):
        upd["conv_w"][k] = buf[0:CONV_WIDTH]
        upd["meta_tokens"][k] = buf[8:16].reshape(N_META, meta_cols)
        rest, _ = _unpack_repl(buf[16:32])
        for n in SMALL_REPL + SMALL_HEAD:
            upd[n][k] = rest[n]

    def shaped(n, a):
        return a.reshape(w[n].shape)

    outs = [loss, grad_x[None]]
    outs += [shaped(n, grads[n]) for n in names]
    for k in range(3):
        outs += [shaped(n, upd[n][k]) for n in names]
    return tuple(outs)
```

```python
import functools

import jax
import jax.numpy as jnp
from jax import lax
from jax.experimental import pallas as pl
from jax.experimental.pallas import tpu as pltpu

F32 = jnp.float32
BF16 = jnp.bfloat16

D_MODEL = 2048
CHUNK = 64
N_META = 16
PAD_LEAD = CHUNK - N_META
ROW0 = PAD_LEAD + N_META
EPS = 1e-6
SSD_HEADS = 32
HEAD_DIM = 64
GROUPS = 8
HPG = SSD_HEADS // GROUPS
D_STATE = 128
D_SSD = 2048
GROUP_W = D_SSD // GROUPS
CONV_WIDTH = 4
D_CONV = 4096
Q_HEADS = 16
KV_HEADS = 4
REP = Q_HEADS // KV_HEADS
D_ATT = 1024
D_KV = 256
BAND_CHUNKS = 3
ROPE_THETA = 10000.0
D_MIX = D_SSD + D_ATT
D_IN = 8736
N_SHARD = 4
W_IN_SHARD = D_IN // N_SHARD
W_OUT_SHARD = D_MIX // N_SHARD

OZ, OXS, OB, OC, OQ, OG, OK, OV, ODT = 0, 2048, 4096, 5120, 6144, 7168, 8192, 8448, 8704
DT_SLAB = 512
N_RE = ODT + DT_SLAB
LANES = 128

ADAM_LR, ADAM_B1, ADAM_B2, ADAM_EPS, ADAM_WD, ADAM_STEP = 0.001, 0.9, 0.999, 1e-08, 0.01, 10

SSD_FWD_GROUPS_PER_STEP = 4
SSD_BWD_GROUPS_PER_STEP = 8
SEG_TILE = 1024
VMEM_LIMIT = 52 * 1024 * 1024
NEG = -1e30
HI = lax.Precision.HIGHEST


def _pallas(body, **kw):
    return pl.pallas_call(body, **kw)


def _cp(*sem):
    return pltpu.CompilerParams(dimension_semantics=sem, vmem_limit_bytes=VMEM_LIMIT)


def _tile(n, cap, mult=16):
    best = None
    for d in range(mult, min(n, cap) + 1, mult):
        if n % d == 0:
            best = d
    assert best is not None, (n, cap)
    return best


def _nt(a, b):
    return lax.dot_general(a, b, (((1,), (1,)), ((), ())), preferred_element_type=F32)


def _tn(a, b):
    return lax.dot_general(a, b, (((0,), (0,)), ((), ())), preferred_element_type=F32)


def _mm(a, b):
    return jnp.dot(a, b, preferred_element_type=F32)


def _sigmoid(x):
    return 1.0 / (1.0 + jnp.exp(-x))


def _bf(x):
    return x.astype(BF16)


def _inproj(hpad, norm_w, w_re, w_out_shard):
    t, d = hpad.shape
    n = w_re.shape[1]
    tm, tn = _tile(t, 832), 1024
    ni, nj = t // tm, n // tn
    r_out, n_out = w_out_shard.shape
    kc = GATHER_CHUNKS

    def body(h_ref, nw_ref, w_ref, ws_ref, proj_ref, hn_ref, wall_ref, hn_s, send_sems, recv_sems, local_sems):
        i, j = pl.program_id(0), pl.program_id(1)
        start, forward, finish = _gather_plan(ws_ref, wall_ref, send_sems, recv_sems, local_sems, r_out // 2, kc)
        pl.when((i == 0) & (j == 0))(start)
        pl.when((i == ni // 2) & (j == 0))(forward)

        @pl.when(j == 0)
        def _():
            h = h_ref[...]
            ms = jnp.mean(h * h, axis=-1, keepdims=True)
            hn = _bf(h * lax.rsqrt(ms + EPS) * nw_ref[...])
            hn_s[...] = hn
            hn_ref[...] = hn
        proj_ref[...] = _mm(hn_s[...], w_ref[...])
        pl.when((i == ni - 1) & (j == nj - 1))(finish)

    return _pallas(
        body, name="inproj", grid=(ni, nj),
        in_specs=[pl.BlockSpec((tm, d), lambda i, j: (i, 0)), pl.BlockSpec((1, d), lambda i, j: (0, 0)),
                  pl.BlockSpec((d, tn), lambda i, j: (0, j)), ANY],
        out_specs=[pl.BlockSpec((tm, tn), lambda i, j: (i, j)), pl.BlockSpec((tm, d), lambda i, j: (i, 0)), ANY],
        out_shape=[jax.ShapeDtypeStruct((t, n), F32), jax.ShapeDtypeStruct((t, d), BF16),
                   jax.ShapeDtypeStruct((N_SHARD, r_out, n_out), w_out_shard.dtype)],
        scratch_shapes=[pltpu.VMEM((tm, d), BF16), pltpu.SemaphoreType.DMA((6 * kc,)), pltpu.SemaphoreType.DMA((6 * kc,)),
                        pltpu.SemaphoreType.DMA((2 * kc,))],
        compiler_params=_cp("arbitrary", "arbitrary"))(hpad, norm_w, w_re, w_out_shard)


def _conv_fwd(proj, conv_w, conv_b):
    t = proj.shape[0]
    tc = 256
    off = OXS // tc

    def body(x_ref, w_ref, b_ref, o_ref):
        x = x_ref[...]
        w = w_ref[...]
        row = lax.broadcasted_iota(jnp.int32, (t, tc), 0)
        u = b_ref[...] + w[3:4, :] * x
        for k in range(1, CONV_WIDTH):
            u = u + w[3 - k:4 - k, :] * jnp.where(row >= k, pltpu.roll(x, k, 0), 0.0)
        h = 0.5 * u
        o_ref[...] = h + h * jnp.tanh(h)

    return _pallas(
        body, name="conv_fwd", grid=(D_CONV // tc,),
        in_specs=[pl.BlockSpec((t, tc), lambda j: (0, j + off)), pl.BlockSpec((CONV_WIDTH, tc), lambda j: (0, j)),
                  pl.BlockSpec((1, tc), lambda j: (0, j))],
        out_specs=pl.BlockSpec((t, tc), lambda j: (0, j)),
        out_shape=jax.ShapeDtypeStruct((t, D_CONV), F32),
        compiler_params=_cp("parallel"))(proj, conv_w, conv_b)


def _softplus(u):
    e = jnp.exp(-jnp.abs(u))
    w = 1.0 + e
    l1p = jnp.where(w == 1.0, e, jnp.log(w) * (e / jnp.where(w == 1.0, 1.0, w - 1.0)))
    return jnp.maximum(u, 0.0) + l1p


def _chunks_per_step(nc):
    return max(d for d in range(1, 14) if nc % d == 0)


def _dt_prep(proj, dt_bias_l, a_log_l):
    t = proj.shape[0]
    nc = t // CHUNK
    q = CHUNK
    cps = _chunks_per_step(nc)
    rows = cps * q

    def body(raw_ref, bias_ref, alog_ref, dt_ref, acs_ref, acst_ref):
        ri = lax.broadcasted_iota(jnp.int32, (q, q), 0)
        ci = lax.broadcasted_iota(jnp.int32, (q, q), 1)
        tri = (ri >= ci).astype(F32)
        neg_a = -jnp.exp(alog_ref[...])
        for k in range(cps):
            rk = slice(q * k, q * (k + 1))
            sp = _softplus(raw_ref[rk, :] + bias_ref[...])
            row = pl.program_id(0) * rows + q * k + lax.broadcasted_iota(jnp.int32, (q, LANES), 0)
            dt = jnp.where(row >= PAD_LEAD, sp, 0.0)
            acs = jnp.dot(tri, dt * neg_a, preferred_element_type=F32, precision=HI)
            dt_ref[rk, :] = dt
            acs_ref[rk, :] = acs
            acst_ref[k] = acs.T

    return _pallas(
        body, name="dt_prep", grid=(nc // cps,),
        in_specs=[pl.BlockSpec((rows, LANES), lambda c: (c, ODT // LANES)), pl.BlockSpec((1, LANES), lambda c: (0, 0)),
                  pl.BlockSpec((1, LANES), lambda c: (0, 0))],
        out_specs=[pl.BlockSpec((rows, LANES), lambda c: (c, 0)), pl.BlockSpec((rows, LANES), lambda c: (c, 0)),
                   pl.BlockSpec((cps, LANES, q), lambda c: (c, 0, 0))],
        out_shape=[jax.ShapeDtypeStruct((t, LANES), F32), jax.ShapeDtypeStruct((t, LANES), F32),
                   jax.ShapeDtypeStruct((nc, LANES, q), F32)],
        compiler_params=_cp("parallel"))(proj, dt_bias_l, a_log_l)


def _head_cols(blk, idx):
    lane = lax.broadcasted_iota(jnp.int32, blk.shape, 1)
    return jnp.sum(jnp.where(lane == idx, blk, 0.0), axis=1, keepdims=True)


class _HeadVals:
    pass


def _lane_head(shape):
    return lax.broadcasted_iota(jnp.int32, shape, len(shape) - 1) >> 6


def _group_heads(g, gi, dtb, acsb, acst_ref, dskb):
    q = dtb.shape[0]
    hv = _HeadVals()
    lh = _lane_head((1, GROUP_W))
    hv.dt = jnp.zeros((q, GROUP_W), F32)
    hv.acs = jnp.zeros((q, GROUP_W), F32)
    hv.acs_last = jnp.zeros((1, GROUP_W), F32)
    hv.dsk = jnp.zeros((1, GROUP_W), F32)
    rows = []
    for r in range(HPG):
        idx = GROUPS * g + r
        sel = lh == r
        acs_r = acst_ref[0, GROUPS * gi + r:GROUPS * gi + r + 1, :]
        rows.append(acs_r)
        hv.dt = jnp.where(sel, _head_cols(dtb, idx), hv.dt)
        hv.acs = jnp.where(sel, _head_cols(acsb, idx), hv.acs)
        hv.acs_last = jnp.where(sel, acs_r[:, q - 1:q], hv.acs_last)
        hv.dsk = jnp.where(sel, _head_cols(dskb, idx), hv.dsk)
    hv.acs_row = jnp.concatenate(rows, axis=1)
    return hv


def _head_tri(q, lower):
    ri = lax.broadcasted_iota(jnp.int32, (q, GROUP_W), 0)
    li = lax.broadcasted_iota(jnp.int32, (q, GROUP_W), 1) & (HEAD_DIM - 1)
    return ri >= li if lower else ri <= li


def _block_diag(v):
    rb = lax.broadcasted_iota(jnp.int32, (GROUP_W, GROUP_W), 0) >> 6
    cb = lax.broadcasted_iota(jnp.int32, (GROUP_W, GROUP_W), 1) >> 6
    return jnp.where(rb == cb, jnp.concatenate([v] * HPG, axis=0), jnp.zeros((), v.dtype))


def _head_sums(v, r):
    return jnp.sum(jnp.where(_lane_head((1, GROUP_W)) == r, v, 0.0), axis=1, keepdims=True)


def _ssd_fwd(xbc, proj, dt, acs, acst, d_skip_l, ssd_norm_w):
    t = xbc.shape[0]
    q = CHUNK
    nc = t // q

    gps = SSD_FWD_GROUPS_PER_STEP
    gw, sw = gps * GROUP_W, gps * D_STATE

    def body(xs_ref, b_ref, c_ref, dt_ref, acs_ref, acst_ref, z_ref, dsk_ref, nw_ref,
             y_ref, ymix_ref, st_ref, state):
        @pl.when(pl.program_id(1) == 0)
        def _():
            state[...] = jnp.zeros_like(state)

        for gi in range(gps):
            g = gps * pl.program_id(0) + gi
            cols = slice(GROUP_W * gi, GROUP_W * (gi + 1))
            x = xs_ref[:, cols]
            bmb = _bf(b_ref[:, D_STATE * gi:D_STATE * (gi + 1)])
            cmb = _bf(c_ref[:, D_STATE * gi:D_STATE * (gi + 1)])
            hv = _group_heads(g, gi, dt_ref[...], acs_ref[...], acst_ref, dsk_ref[...])
            decay = jnp.exp(jnp.where(_head_tri(q, True), hv.acs - hv.acs_row, NEG))
            m_all = _bf(_nt(cmb, jnp.concatenate([bmb] * HPG, axis=0)) * decay)
            xdt = x * hv.dt
            s_prev = state[gi]
            st_ref[0, gi] = s_prev
            y = (_mm(m_all, _block_diag(_bf(xdt))) + _mm(cmb, _bf(s_prev)) * jnp.exp(hv.acs) + hv.dsk * x)
            state[gi] = jnp.exp(hv.acs_last) * s_prev + _tn(bmb, _bf(xdt * jnp.exp(hv.acs_last - hv.acs)))
            y_ref[:, cols] = y
            z = z_ref[:, cols]
            yg = y * (z * _sigmoid(z))
            ms = jnp.mean(yg * yg, axis=-1, keepdims=True)
            ymix_ref[:, cols] = _bf(yg * lax.rsqrt(ms + EPS) * nw_ref[:, cols])

    return _pallas(
        body, name="ssd_fwd", grid=(GROUPS // gps, nc),
        in_specs=[pl.BlockSpec((q, gw), lambda g, c: (c, g)),
                  pl.BlockSpec((q, sw), lambda g, c: (c, D_SSD // sw + g)),
                  pl.BlockSpec((q, sw), lambda g, c: (c, (D_SSD + GROUPS * D_STATE) // sw + g)),
                  pl.BlockSpec((q, LANES), lambda g, c: (c, 0)), pl.BlockSpec((q, LANES), lambda g, c: (c, 0)),
                  pl.BlockSpec((1, gps * GROUPS, q), lambda g, c: (c, g, 0)),
                  pl.BlockSpec((q, gw), lambda g, c: (c, g)),
                  pl.BlockSpec((1, LANES), lambda g, c: (0, 0)), pl.BlockSpec((1, gw), lambda g, c: (0, g))],
        out_specs=[pl.BlockSpec((q, gw), lambda g, c: (c, g)), pl.BlockSpec((q, gw), lambda g, c: (c, g)),
                   pl.BlockSpec((1, gps, D_STATE, GROUP_W), lambda g, c: (c, g, 0, 0))],
        out_shape=[jax.ShapeDtypeStruct((t, D_SSD), F32), jax.ShapeDtypeStruct((t, D_SSD), BF16),
                   jax.ShapeDtypeStruct((nc, GROUPS, D_STATE, GROUP_W), F32)],
        scratch_shapes=[pltpu.VMEM((gps, D_STATE, GROUP_W), F32)],
        compiler_params=_cp("parallel", "arbitrary"))(xbc, xbc, xbc, dt, acs, acst, proj, d_skip_l, ssd_norm_w)


def _swap_halves(v):
    lane = lax.broadcasted_iota(jnp.int32, v.shape, 1)
    return jnp.where((lane & (HEAD_DIM - 1)) < HEAD_DIM // 2, pltpu.roll(v, LANES - HEAD_DIM // 2, 1),
                     pltpu.roll(v, HEAD_DIM // 2, 1))


def _rope(qsrc, q_off, ksrc, k_off, cos_t, sin_t):
    t = qsrc.shape[0]
    tr = _tile(t, 832)
    q_scale = HEAD_DIM ** -0.5

    def body(q_ref, k_ref, cos_ref, sin_ref, qo_ref, ko_ref):
        cs = cos_ref[...]
        sn = sin_ref[...]
        for src, dst, width, scale in ((q_ref, qo_ref, D_ATT, q_scale), (k_ref, ko_ref, D_KV, 1.0)):
            for s in range(width // LANES):
                v = src[:, LANES * s:LANES * (s + 1)].astype(F32)
                dst[:, LANES * s:LANES * (s + 1)] = _bf((v * cs + _swap_halves(v) * sn) * scale)

    return _pallas(
        body, name="rope", grid=(t // tr,),
        in_specs=[pl.BlockSpec((tr, D_ATT), lambda i: (i, q_off // D_ATT)),
                  pl.BlockSpec((tr, D_KV), lambda i: (i, k_off // D_KV)),
                  pl.BlockSpec((tr, LANES), lambda i: (i, 0)), pl.BlockSpec((tr, LANES), lambda i: (i, 0))],
        out_specs=[pl.BlockSpec((tr, D_ATT), lambda i: (i, 0)), pl.BlockSpec((tr, D_KV), lambda i: (i, 0))],
        out_shape=[jax.ShapeDtypeStruct((t, D_ATT), BF16), jax.ShapeDtypeStruct((t, D_KV), BF16)],
        compiler_params=_cp("parallel"))(qsrc, ksrc, cos_t, sin_t)


def _attn_chunks_per_step(nc):
    return max(d for d in range(1, 6) if nc % d == 0)


def _band(ref, c):
    return [ref[pl.ds(pl.multiple_of(jnp.maximum(c - j, 0) * CHUNK, CHUNK), CHUNK), :] for j in (2, 1, 0)]


def _attn_probs(qh, kb, sink_col, valid):
    s = jnp.where(valid, _nt(qh, kb), NEG)
    m = jnp.maximum(jnp.max(s, axis=1, keepdims=True), sink_col)
    p = jnp.exp(s - m)
    psink = jnp.exp(sink_col - m)
    return p, psink, 1.0 / (jnp.sum(p, axis=1, keepdims=True) + psink)


def _attn_operands(c, q, k_refs, v_refs, sink_ref, h):
    qh = jnp.concatenate([q[:, HEAD_DIM * (REP * h + r):HEAD_DIM * (REP * h + r + 1)] for r in range(REP)], axis=0)
    kb = jnp.concatenate([k[:, HEAD_DIM * h:HEAD_DIM * (h + 1)] for k in k_refs], axis=0)
    vb = jnp.concatenate([_bf(v[:, HEAD_DIM * h:HEAD_DIM * (h + 1)]) for v in v_refs], axis=0)
    rows = lax.broadcasted_iota(jnp.int32, (REP * CHUNK, 1), 0) >> 6
    sink_col = jnp.zeros((REP * CHUNK, 1), F32)
    for r in range(REP):
        sink_col = jnp.where(rows == r, sink_ref[REP * h + r], sink_col)
    key_abs = (c - (BAND_CHUNKS - 1)) * CHUNK + lax.broadcasted_iota(jnp.int32, (1, BAND_CHUNKS * CHUNK), 1)
    return qh, kb, vb, sink_col, key_abs >= PAD_LEAD


def _attn_fwd(qr, kr, proj, sinks):
    t = qr.shape[0]
    nc = t // CHUNK
    cps = _attn_chunks_per_step(nc)
    rows = cps * CHUNK

    def body(q_ref, k_ref, v_ref, g_ref, sink_ref, o_ref):
        for j in range(cps):
            c = pl.program_id(0) * cps + j
            rj = slice(CHUNK * j, CHUNK * (j + 1))
            ks, vs = _band(k_ref, c), _band(v_ref, c)
            q = q_ref[rj, :]
            outs = []
            for h in range(KV_HEADS):
                qh, kb, vb, sink_col, valid = _attn_operands(c, q, ks, vs, sink_ref, h)
                p, _, inv = _attn_probs(qh, kb, sink_col, valid)
                o = _mm(_bf(p), vb) * inv
                outs += [o[CHUNK * r:CHUNK * (r + 1)] for r in range(REP)]
            att = jnp.concatenate(outs, axis=1)
            gate = g_ref[rj, :]
            o_ref[rj, :] = _bf(att * (gate * _sigmoid(gate)))

    return _pallas(
        body, name="attn_fwd", grid=(nc // cps,),
        in_specs=[pl.BlockSpec((rows, D_ATT), lambda i: (i, 0)), pl.BlockSpec((t, D_KV), lambda i: (0, 0)),
                  pl.BlockSpec((t, D_KV), lambda i: (0, OV // D_KV)),
                  pl.BlockSpec((rows, D_ATT), lambda i: (i, OG // D_ATT)), pl.BlockSpec(memory_space=pltpu.SMEM)],
        out_specs=pl.BlockSpec((rows, D_ATT), lambda i: (i, 0)),
        out_shape=jax.ShapeDtypeStruct((t, D_ATT), BF16),
        compiler_params=_cp("parallel"))(qr, kr, proj, proj, sinks)


def _outproj(ymix, amix, w_out):
    t = ymix.shape[0]
    tm, tn = _tile(t, 832), 1024

    def body(y_ref, a_ref, wy_ref, wa_ref, o_ref):
        o_ref[...] = _mm(y_ref[...], wy_ref[...]) + _mm(a_ref[...], wa_ref[...])

    return _pallas(
        body, name="outproj", grid=(t // tm, D_MODEL // tn),
        in_specs=[pl.BlockSpec((tm, D_SSD), lambda i, j: (i, 0)), pl.BlockSpec((tm, D_ATT), lambda i, j: (i, 0)),
                  pl.BlockSpec((D_SSD, tn), lambda i, j: (0, j)),
                  pl.BlockSpec((D_ATT, tn), lambda i, j: (D_SSD // D_ATT, j))],
        out_specs=pl.BlockSpec((tm, tn), lambda i, j: (i, j)),
        out_shape=jax.ShapeDtypeStruct((t, D_MODEL), F32),
        compiler_params=_cp("parallel", "parallel"))(ymix, amix, w_out, w_out)


def _post_loss(out, x, target, norm_post_w):
    t = out.shape[0]
    nc = t // CHUNK
    cps = _attn_chunks_per_step(nc)
    rows = cps * CHUNK

    def body(o_ref, *refs):
        x_refs, tg_refs = refs[:cps], refs[cps:2 * cps]
        nw_ref, dout_ref, dy_ref, loss_ref, gnw_ref = refs[2 * cps:]
        i = pl.program_id(0)

        @pl.when(i == 0)
        def _():
            loss_ref[...] = jnp.zeros_like(loss_ref)
            gnw_ref[...] = jnp.zeros_like(gnw_ref)

        nw = nw_ref[...]
        loss = jnp.zeros((), F32)
        gnw = jnp.zeros((1, D_MODEL), F32)
        for k in range(cps):
            rk = slice(CHUNK * k, CHUNK * (k + 1))
            frames = i * cps + k > 0
            o = o_ref[rk, :]
            rstd = lax.rsqrt(jnp.mean(o * o, axis=-1, keepdims=True) + EPS)
            n = o * rstd
            err = jnp.where(frames, x_refs[k][...] + n * nw - tg_refs[k][...], 0.0)
            loss = loss + jnp.sum(err * err)
            dy = err * (1.0 / D_MODEL)
            dy_ref[rk, :] = dy
            gnw = gnw + jnp.sum(dy * n, axis=0, keepdims=True)
            dn = dy * nw
            dout_ref[rk, :] = _bf(rstd * (dn - n * jnp.mean(dn * n, axis=-1, keepdims=True)))
        loss_ref[...] += loss * (0.5 / D_MODEL)
        gnw_ref[...] += gnw

    lower = [pl.BlockSpec((CHUNK, D_MODEL), functools.partial(lambda i, k: (jnp.maximum(i * cps + k - 1, 0), 0), k=k))
             for k in range(cps)]
    return _pallas(
        body, name="post_loss", grid=(nc // cps,),
        in_specs=[pl.BlockSpec((rows, D_MODEL), lambda i: (i, 0))] + lower + lower
        + [pl.BlockSpec((1, D_MODEL), lambda i: (0, 0))],
        out_specs=[pl.BlockSpec((rows, D_MODEL), lambda i: (i, 0)), pl.BlockSpec((rows, D_MODEL), lambda i: (i, 0)),
                   pl.BlockSpec((8, LANES), lambda i: (0, 0)), pl.BlockSpec((1, D_MODEL), lambda i: (0, 0))],
        out_shape=[jax.ShapeDtypeStruct((t, D_MODEL), BF16), jax.ShapeDtypeStruct((t, D_MODEL), F32),
                   jax.ShapeDtypeStruct((8, LANES), F32), jax.ShapeDtypeStruct((1, D_MODEL), F32)],
        compiler_params=_cp("arbitrary"))(out, *([x] * cps), *([target] * cps), norm_post_w)


def _carried(grid, carry):
    if carry is None:
        return [], [], [], [], lambda refs: None, lambda refs: None
    hn = carry.shape[1] // 2

    def at(ids, which):
        cond = None
        for d, size in enumerate(grid):
            here = pl.program_id(d) == (0 if which == "first" else size - 1)
            cond = here if cond is None else cond & here
        return cond

    def start(refs):
        @pl.when(at(grid, "first"))
        def _():
            for cp in _pair_copies(*refs):
                cp.start()

    def finish(refs):
        @pl.when(at(grid, "last"))
        def _():
            for cp in _pair_copies(*refs):
                cp.wait()

    return ([ANY], [ANY], [jax.ShapeDtypeStruct((carry.shape[0], hn), F32)],
            [pltpu.SemaphoreType.DMA((PAIR_CHUNKS,)), pltpu.SemaphoreType.DMA((PAIR_CHUNKS,))], start, finish)


def _nt_matmul(a, b, name, carry=None):
    t, k = a.shape
    n = b.shape[0]
    tm, tn = _tile(t, 832), 1024
    grid = (t // tm, n // tn)
    cin, cout, cshape, cscratch, start, finish = _carried(grid, carry)

    def body(a_ref, b_ref, *refs):
        o_ref = refs[len(cin)]
        comm = (refs[0], refs[2], refs[3], refs[4]) if carry is not None else None
        start(comm)
        o_ref[...] = _nt(a_ref[...], b_ref[...])
        finish(comm)

    res = _pallas(
        body, name=name, grid=grid,
        in_specs=[pl.BlockSpec((tm, k), lambda i, j: (i, 0)), pl.BlockSpec((tn, k), lambda i, j: (j, 0))] + cin,
        out_specs=[pl.BlockSpec((tm, tn), lambda i, j: (i, j))] + cout,
        out_shape=[jax.ShapeDtypeStruct((t, n), F32)] + cshape, scratch_shapes=cscratch,
        compiler_params=_cp("arbitrary", "arbitrary"))(a, b, *([carry] if carry is not None else []))
    return res if carry is not None else res[0]


def _tn_matmul(a, b, name, carry=None):
    t, m = a.shape
    n = b.shape[1]
    tk, tm, tn = _tile(t, 832), min(m, 2048), min(n, 2048)
    nk = t // tk
    grid = (m // tm, n // tn, nk)
    cin, cout, cshape, cscratch, start, finish = _carried(grid, carry)

    def body(a_ref, b_ref, *refs):
        o_ref = refs[len(cin)]
        comm = (refs[0], refs[2], refs[3], refs[4]) if carry is not None else None
        start(comm)

        @pl.when(pl.program_id(2) == 0)
        def _():
            o_ref[...] = jnp.zeros_like(o_ref)
        o_ref[...] += _tn(a_ref[...], b_ref[...])
        finish(comm)

    res = _pallas(
        body, name=name, grid=grid,
        in_specs=[pl.BlockSpec((tk, tm), lambda i, j, k: (k, i)), pl.BlockSpec((tk, tn), lambda i, j, k: (k, j))] + cin,
        out_specs=[pl.BlockSpec((tm, tn), lambda i, j, k: (i, j))] + cout,
        out_shape=[jax.ShapeDtypeStruct((m, n), F32)] + cshape, scratch_shapes=cscratch,
        compiler_params=_cp("arbitrary", "arbitrary", "arbitrary"))(a, b, *([carry] if carry is not None else []))
    return res if carry is not None else res[0]


def _attn_bwd(qr, kr, proj, dmix, sinks, ga):
    t = qr.shape[0]
    nc = t // CHUNK
    cps = _attn_chunks_per_step(nc)
    nsteps = nc // cps
    rows_step = cps * CHUNK

    def body(q_ref, k_ref, v_ref, g_ref, da_ref, sink_ref, ga_ref, dq_ref, dg_ref, dk_ref, dv_ref, gs_ref,
             got_ref, send_sems, recv_sems):
        step = pl.program_id(0)

        @pl.when(step == 0)
        def _():
            for cp in _exchange_copies(ga_ref, got_ref, send_sems, recv_sems):
                cp.start()
            dk_ref[...] = jnp.zeros_like(dk_ref)
            dv_ref[...] = jnp.zeros_like(dv_ref)
            gs_ref[...] = jnp.zeros_like(gs_ref)

        lane = lax.broadcasted_iota(jnp.int32, (1, LANES), 1)
        rows = lax.broadcasted_iota(jnp.int32, (REP * CHUNK, 1), 0) >> 6
        gs = jnp.zeros((1, LANES), F32)
        dk_parts = [[] for _ in range(cps + BAND_CHUNKS - 1)]
        dv_parts = [[] for _ in range(cps + BAND_CHUNKS - 1)]
        for j in range(cps):
            c = step * cps + j
            rj = slice(CHUNK * j, CHUNK * (j + 1))
            ks, vs = _band(k_ref, c), _band(v_ref, c)
            q = q_ref[rj, :]
            gate = g_ref[rj, :]
            sg = _sigmoid(gate)
            da = da_ref[rj, :]
            datt = da * (gate * sg)
            dqs, atts, dks, dvs = [], [], [], []
            for h in range(KV_HEADS):
                qh, kb, vb, sink_col, valid = _attn_operands(c, q, ks, vs, sink_ref, h)
                p, psink, inv = _attn_probs(qh, kb, sink_col, valid)
                pb = _bf(p)
                o = _mm(pb, vb) * inv
                do = jnp.concatenate([datt[:, HEAD_DIM * (REP * h + r):HEAD_DIM * (REP * h + r + 1)]
                                      for r in range(REP)], axis=0)
                dob = _bf(do * inv)
                delta = jnp.sum(do * o, axis=1, keepdims=True) * inv
                ds = _bf(p * (_nt(dob, vb) - delta))
                gsink = -psink * delta
                for r in range(REP):
                    gs = gs + jnp.where(lane == REP * h + r, jnp.sum(jnp.where(rows == r, gsink, 0.0)), 0.0)
                dqh = _mm(ds, kb)
                dqs += [dqh[CHUNK * r:CHUNK * (r + 1)] for r in range(REP)]
                atts += [o[CHUNK * r:CHUNK * (r + 1)] for r in range(REP)]
                dks.append(_tn(ds, qh))
                dvs.append(_tn(pb, dob))
            dq_ref[rj, :] = jnp.concatenate(dqs, axis=1)
            att = jnp.concatenate(atts, axis=1)
            dg_ref[rj, :] = _bf(da * att * (sg * (1.0 + gate * (1.0 - sg))))
            dkf = jnp.concatenate(dks, axis=1)
            dvf = jnp.concatenate(dvs, axis=1)
            for b in range(BAND_CHUNKS):
                dk_parts[j + b].append(dkf[CHUNK * b:CHUNK * (b + 1)])
                dv_parts[j + b].append(dvf[CHUNK * b:CHUNK * (b + 1)])
        gs_ref[0:1, :] += gs
        for rel in range(cps + BAND_CHUNKS - 1):
            r0 = pl.multiple_of(jnp.maximum(step * cps - (BAND_CHUNKS - 1) + rel, 0) * CHUNK, CHUNK)
            dk_ref[pl.ds(r0, CHUNK), :] += sum(dk_parts[rel][1:], dk_parts[rel][0])
            dv_ref[pl.ds(r0, CHUNK), :] += sum(dv_parts[rel][1:], dv_parts[rel][0])

        @pl.when(step == nsteps - 1)
        def _():
            for cp in _exchange_copies(ga_ref, got_ref, send_sems, recv_sems):
                cp.wait()

    return _pallas(
        body, name="attn_bwd", grid=(nsteps,),
        in_specs=[pl.BlockSpec((rows_step, D_ATT), lambda i: (i, 0)), pl.BlockSpec((t, D_KV), lambda i: (0, 0)),
                  pl.BlockSpec((t, D_KV), lambda i: (0, OV // D_KV)),
                  pl.BlockSpec((rows_step, D_ATT), lambda i: (i, OG // D_ATT)),
                  pl.BlockSpec((rows_step, D_ATT), lambda i: (i, D_SSD // D_ATT)),
                  pl.BlockSpec(memory_space=pltpu.SMEM), ANY],
        out_specs=[pl.BlockSpec((rows_step, D_ATT), lambda i: (i, 0)), pl.BlockSpec((rows_step, D_ATT), lambda i: (i, 0)),
                   pl.BlockSpec((t, D_KV), lambda i: (0, 0)), pl.BlockSpec((t, D_KV), lambda i: (0, 0)),
                   pl.BlockSpec((8, LANES), lambda i: (0, 0)), ANY],
        out_shape=[jax.ShapeDtypeStruct((t, D_ATT), F32), jax.ShapeDtypeStruct((t, D_ATT), BF16),
                   jax.ShapeDtypeStruct((t, D_KV), F32), jax.ShapeDtypeStruct((t, D_KV), F32),
                   jax.ShapeDtypeStruct((8, LANES), F32), _exchange_shape(ga)],
        scratch_shapes=_exchange_scratch(),
        compiler_params=_cp("arbitrary"))(qr, kr, proj, proj, dmix, sinks, ga)


def _ssd_bwd(dmix, y_ssd, xbc, proj, dt, acs, acst, states, d_skip_l, ssd_norm_w):
    t = xbc.shape[0]
    q = CHUNK
    nc = t // q
    gps = SSD_BWD_GROUPS_PER_STEP
    gw, sw = gps * GROUP_W, gps * D_STATE

    def body(dmix_ref, y_ref, z_ref, nw_ref, xs_ref, b_ref, c_ref, dt_ref, acs_ref, acst_ref, st_ref, dsk_ref,
             dz_ref, dxs_ref, db_ref, dc_ref, dacs_ref, ddt_ref, gnw_ref, gdsk_ref, dstate):
        @pl.when(pl.program_id(1) == 0)
        def _():
            dstate[...] = jnp.zeros_like(dstate)
            gnw_ref[...] = jnp.zeros_like(gnw_ref)
            gdsk_ref[...] = jnp.zeros_like(gdsk_ref)

        last_row = lax.broadcasted_iota(jnp.int32, (q, 1), 0) == q - 1
        lane = lax.broadcasted_iota(jnp.int32, (q, LANES), 1)
        lane1 = lax.broadcasted_iota(jnp.int32, (8, LANES), 1)
        for gi in range(gps):
            g = gps * pl.program_id(0) + gi
            cols = slice(GROUP_W * gi, GROUP_W * (gi + 1))
            scols = slice(D_STATE * gi, D_STATE * (gi + 1))
            y = y_ref[:, cols]
            z = z_ref[:, cols]
            sz = _sigmoid(z)
            silu_z = z * sz
            yg = y * silu_z
            rstd = lax.rsqrt(jnp.mean(yg * yg, axis=-1, keepdims=True) + EPS)
            n = yg * rstd
            dout = dmix_ref[:, cols]
            gnw_ref[:, cols] += jnp.sum(dout * n, axis=0, keepdims=True)
            dn = dout * nw_ref[:, cols]
            dyg = rstd * (dn - n * jnp.mean(dn * n, axis=-1, keepdims=True))
            dy = dyg * silu_z
            dz_ref[:, cols] = _bf(dyg * y * (sz * (1.0 + z * (1.0 - sz))))

            x = xs_ref[:, cols]
            bmb, cmb = _bf(b_ref[:, scols]), _bf(c_ref[:, scols])
            hv = _group_heads(g, gi, dt_ref[...], acs_ref[...], acst_ref, dsk_ref[...])
            dec = jnp.exp(jnp.where(_head_tri(q, True), hv.acs - hv.acs_row, NEG))
            dect = jnp.exp(jnp.where(_head_tri(q, False), hv.acs_row - hv.acs, NEG))
            b4 = jnp.concatenate([bmb] * HPG, axis=0)
            c4 = jnp.concatenate([cmb] * HPG, axis=0)
            m_all = _nt(cmb, b4) * dec
            mt_all = _nt(bmb, c4) * dect
            xdt = x * hv.dt
            xdt_b, dyb = _bf(xdt), _bf(dy)
            x_bd, dy_bd = _block_diag(xdt_b), _block_diag(dyb)
            s_prev = st_ref[0, gi]
            spb = _bf(s_prev)
            ds_new = dstate[gi]
            dsb = _bf(ds_new)
            e = jnp.exp(hv.acs)
            elast = jnp.exp(hv.acs_last)
            dte = jnp.exp(hv.acs_last - hv.acs)
            bds = _mm(bmb, dsb)
            dxdt = _mm(_bf(mt_all), dy_bd) + bds * dte
            dm = _nt(dyb, x_bd)
            dmt = _nt(xdt_b, dy_bd)
            dye = _bf(dy * e)
            dc_ref[:, scols] = _mm(_bf(dm * dec), b4) + _nt(dye, spb)
            db_ref[:, scols] = _mm(_bf(dmt * dect), c4) + _nt(_bf(xdt * dte), dsb)
            dstate[gi] = elast * ds_new + _tn(cmb, dye)
            dxs_ref[:, cols] = dxdt * hv.dt + hv.dsk * dy
            ddte_dte = bds * xdt * dte
            dacs_l = dm * m_all - dmt * mt_all + dy * _mm(cmb, spb) * e - ddte_dte
            dlast_l = (jnp.sum(ddte_dte, axis=0, keepdims=True)
                       + jnp.sum(s_prev * ds_new, axis=0, keepdims=True) * elast)
            ddt_l = dxdt * x
            gdsk_l = jnp.sum(dy * x, axis=0, keepdims=True)
            dacs_out = jnp.zeros((q, LANES), F32)
            ddt_out = jnp.zeros((q, LANES), F32)
            gdsk = jnp.zeros((8, LANES), F32)
            for r in range(HPG):
                dacs = _head_sums(dacs_l, r) + jnp.where(last_row, _head_sums(dlast_l, r), 0.0)
                dacs_out = jnp.where(lane == r, dacs, dacs_out)
                ddt_out = jnp.where(lane == r, _head_sums(ddt_l, r), ddt_out)
                gdsk = gdsk + jnp.where(lane1 == r, _head_sums(gdsk_l, r), 0.0)
            dacs_ref[:, LANES * gi:LANES * (gi + 1)] = dacs_out
            ddt_ref[:, LANES * gi:LANES * (gi + 1)] = ddt_out
            gdsk_ref[gi] += gdsk

    rev = lambda c: nc - 1 - c
    wide = pl.BlockSpec((q, gw), lambda g, c: (rev(c), g))
    return _pallas(
        body, name="ssd_bwd", grid=(GROUPS // gps, nc),
        in_specs=[wide, wide, wide, pl.BlockSpec((1, gw), lambda g, c: (0, g)), wide,
                  pl.BlockSpec((q, sw), lambda g, c: (rev(c), D_SSD // sw + g)),
                  pl.BlockSpec((q, sw), lambda g, c: (rev(c), (D_SSD + GROUPS * D_STATE) // sw + g)),
                  pl.BlockSpec((q, LANES), lambda g, c: (rev(c), 0)), pl.BlockSpec((q, LANES), lambda g, c: (rev(c), 0)),
                  pl.BlockSpec((1, gps * GROUPS, q), lambda g, c: (rev(c), g, 0)),
                  pl.BlockSpec((1, gps, D_STATE, GROUP_W), lambda g, c: (rev(c), g, 0, 0)),
                  pl.BlockSpec((1, LANES), lambda g, c: (0, 0))],
        out_specs=[wide, wide,
                   pl.BlockSpec((q, sw), lambda g, c: (rev(c), g)), pl.BlockSpec((q, sw), lambda g, c: (rev(c), g)),
                   pl.BlockSpec((q, gps * LANES), lambda g, c: (rev(c), g)),
                   pl.BlockSpec((q, gps * LANES), lambda g, c: (rev(c), g)),
                   pl.BlockSpec((1, gw), lambda g, c: (0, g)), pl.BlockSpec((gps, 8, LANES), lambda g, c: (g, 0, 0))],
        out_shape=[jax.ShapeDtypeStruct((t, D_SSD), BF16), jax.ShapeDtypeStruct((t, D_SSD), F32),
                   jax.ShapeDtypeStruct((t, GROUPS * D_STATE), F32), jax.ShapeDtypeStruct((t, GROUPS * D_STATE), F32),
                   jax.ShapeDtypeStruct((t, GROUPS * LANES), F32), jax.ShapeDtypeStruct((t, GROUPS * LANES), F32),
                   jax.ShapeDtypeStruct((1, D_SSD), F32), jax.ShapeDtypeStruct((GROUPS, 8, LANES), F32)],
        scratch_shapes=[pltpu.VMEM((gps, D_STATE, GROUP_W), F32)],
        compiler_params=_cp("parallel", "arbitrary"))(dmix, y_ssd, proj, ssd_norm_w, xbc, xbc, xbc, dt, acs, acst,
                                                      states, d_skip_l)


def _dt_bwd(dacs_g, ddt_g, dt, proj, dt_bias_l, a_log_l):
    t = dt.shape[0]
    q = CHUNK
    nc = t // q
    cps = _chunks_per_step(nc)
    rows = cps * q

    def body(dacs_ref, ddt_ref, dt_ref, raw_ref, bias_ref, alog_ref, draw_ref, ga_ref, gb_ref):
        @pl.when(pl.program_id(0) == 0)
        def _():
            ga_ref[...] = jnp.zeros_like(ga_ref)
            gb_ref[...] = jnp.zeros_like(gb_ref)

        lane = lax.broadcasted_iota(jnp.int32, (q, LANES), 1)
        ri = lax.broadcasted_iota(jnp.int32, (q, q), 0)
        ci = lax.broadcasted_iota(jnp.int32, (q, q), 1)
        triu = (ri <= ci).astype(F32)
        a = -jnp.exp(alog_ref[...])
        used = (lane & (GROUPS - 1)) < HPG
        ga = jnp.zeros((1, LANES), F32)
        gb = jnp.zeros((1, LANES), F32)
        for k in range(cps):
            rk = slice(q * k, q * (k + 1))
            dacs = jnp.zeros((q, LANES), F32)
            ddt = jnp.zeros((q, LANES), F32)
            for g in range(GROUPS):
                mask = (lane >= GROUPS * g) & (lane < GROUPS * g + HPG)
                sl = slice(LANES * g, LANES * (g + 1))
                if g == 0:
                    dacs = jnp.where(mask, dacs_ref[rk, sl], dacs)
                    ddt = jnp.where(mask, ddt_ref[rk, sl], ddt)
                else:
                    dacs = jnp.where(mask, pltpu.roll(dacs_ref[rk, sl], GROUPS * g, 1), dacs)
                    ddt = jnp.where(mask, pltpu.roll(ddt_ref[rk, sl], GROUPS * g, 1), ddt)
            dda = jnp.dot(triu, dacs, preferred_element_type=F32, precision=HI)
            row = pl.program_id(0) * rows + q * k + lax.broadcasted_iota(jnp.int32, (q, LANES), 0)
            dsp = jnp.where((row >= PAD_LEAD) & used, dda * a + ddt, 0.0)
            draw = dsp * _sigmoid(raw_ref[rk, :] + bias_ref[...])
            draw_ref[rk, :] = _bf(draw)
            gb = gb + jnp.sum(draw, axis=0, keepdims=True)
            ga = ga + jnp.sum(jnp.where(used, dda * dt_ref[rk, :], 0.0), axis=0, keepdims=True)
        gb_ref[0:1, :] += gb
        ga_ref[0:1, :] += ga * a

    return _pallas(
        body, name="dt_bwd", grid=(nc // cps,),
        in_specs=[pl.BlockSpec((rows, GROUPS * LANES), lambda c: (c, 0)),
                  pl.BlockSpec((rows, GROUPS * LANES), lambda c: (c, 0)),
                  pl.BlockSpec((rows, LANES), lambda c: (c, 0)), pl.BlockSpec((rows, LANES), lambda c: (c, ODT // LANES)),
                  pl.BlockSpec((1, LANES), lambda c: (0, 0)), pl.BlockSpec((1, LANES), lambda c: (0, 0))],
        out_specs=[pl.BlockSpec((rows, LANES), lambda c: (c, 0)), pl.BlockSpec((8, LANES), lambda c: (0, 0)),
                   pl.BlockSpec((8, LANES), lambda c: (0, 0))],
        out_shape=[jax.ShapeDtypeStruct((t, LANES), BF16), jax.ShapeDtypeStruct((8, LANES), F32),
                   jax.ShapeDtypeStruct((8, LANES), F32)],
        compiler_params=_cp("arbitrary"))(dacs_g, ddt_g, dt, proj, dt_bias_l, a_log_l)


def _conv_bwd(dseg, proj, conv_w, conv_b, col_off, name):
    t, width = dseg.shape
    tc = 128
    off_p = (OXS + col_off) // tc
    off_w = col_off // tc

    def body(d_ref, x_ref, w_ref, b_ref, dx_ref, gw_ref, gb_ref, xp, dup):
        xp[0:8, :] = jnp.zeros((8, tc), F32)
        xp[8:t + 8, :] = x_ref[...]
        w = w_ref[...]
        u = (b_ref[...] + w[3:4, :] * xp[8:t + 8, :] + w[2:3, :] * xp[7:t + 7, :]
             + w[1:2, :] * xp[6:t + 6, :] + w[0:1, :] * xp[5:t + 5, :])
        su = _sigmoid(u)
        du = d_ref[...] * (su * (1.0 + u * (1.0 - su)))
        dup[0:t, :] = du
        dup[t:t + 8, :] = jnp.zeros((8, tc), F32)
        dx_ref[...] = _bf(w[3:4, :] * du + w[2:3, :] * dup[1:t + 1, :] + w[1:2, :] * dup[2:t + 2, :]
                          + w[0:1, :] * dup[3:t + 3, :])
        gb_ref[...] = jnp.sum(du, axis=0, keepdims=True)
        gw_ref[...] = jnp.concatenate(
            [jnp.sum(du * xp[5 + k:t + 5 + k, :], axis=0, keepdims=True) for k in range(CONV_WIDTH)], axis=0)

    return _pallas(
        body, name=name, grid=(width // tc,),
        in_specs=[pl.BlockSpec((t, tc), lambda j: (0, j)), pl.BlockSpec((t, tc), lambda j: (0, j + off_p)),
                  pl.BlockSpec((CONV_WIDTH, tc), lambda j: (0, j + off_w)), pl.BlockSpec((1, tc), lambda j: (0, j + off_w))],
        out_specs=[pl.BlockSpec((t, tc), lambda j: (0, j)), pl.BlockSpec((CONV_WIDTH, tc), lambda j: (0, j)),
                   pl.BlockSpec((1, tc), lambda j: (0, j))],
        out_shape=[jax.ShapeDtypeStruct((t, width), BF16), jax.ShapeDtypeStruct((CONV_WIDTH, width), F32),
                   jax.ShapeDtypeStruct((1, width), F32)],
        scratch_shapes=[pltpu.VMEM((t + 8, tc), F32), pltpu.VMEM((t + 8, tc), F32)],
        compiler_params=_cp("parallel"))(dseg, proj, conv_w, conv_b)


def _dinproj(segs, w_re, hpad, norm_w, dy_t, ga):
    t = segs[0].shape[0]
    d = hpad.shape[1]
    tm, tk = _tile(t, 416), SEG_TILE
    counts = [s.shape[1] // tk for s in segs]
    firsts = [sum(counts[:s]) for s in range(len(segs))]
    nk = sum(counts)
    assert nk * tk == w_re.shape[1]
    ni = t // tm
    ns = len(segs)

    def body(*refs):
        seg_refs = refs[:ns]
        w_ref, h_ref, nw_ref, dy_ref, ga_ref, dh_ref, gnw_ref, got_ref, acc, send_sems, recv_sems = refs[ns:]
        i, k = pl.program_id(0), pl.program_id(1)

        @pl.when((i == 0) & (k == 0))
        def _():
            for cp in _exchange_copies(ga_ref, got_ref, send_sems, recv_sems):
                cp.start()
            gnw_ref[...] = jnp.zeros_like(gnw_ref)

        @pl.when(k == 0)
        def _():
            acc[...] = jnp.zeros_like(acc)

        for s in range(ns):
            @pl.when((k >= firsts[s]) & (k < firsts[s] + counts[s]))
            def _(s=s):
                acc[...] += _nt(seg_refs[s][...], w_ref[...])

        @pl.when(k == nk - 1)
        def _():
            h = h_ref[...]
            rstd = lax.rsqrt(jnp.mean(h * h, axis=-1, keepdims=True) + EPS)
            nrm = h * rstd
            dhn = acc[...]
            gnw_ref[...] += jnp.sum(dhn * nrm, axis=0, keepdims=True)
            dn = dhn * nw_ref[...]
            dh_ref[...] = rstd * (dn - nrm * jnp.mean(dn * nrm, axis=-1, keepdims=True)) + dy_ref[...]

        @pl.when((i == ni - 1) & (k == nk - 1))
        def _():
            for cp in _exchange_copies(ga_ref, got_ref, send_sems, recv_sems):
                cp.wait()

    seg_specs = [pl.BlockSpec((tm, tk), functools.partial(lambda i, k, f0, n0: (i, jnp.clip(k - f0, 0, n0 - 1)),
                                                          f0=firsts[s], n0=counts[s])) for s in range(ns)]
    return _pallas(
        body, name="dinproj", grid=(ni, nk),
        in_specs=seg_specs + [pl.BlockSpec((d, tk), lambda i, k: (0, k)),
                              pl.BlockSpec((tm, d), lambda i, k: (i, 0)), pl.BlockSpec((1, d), lambda i, k: (0, 0)),
                              pl.BlockSpec((tm, d), lambda i, k: (i, 0)), ANY],
        out_specs=[pl.BlockSpec((tm, d), lambda i, k: (i, 0)), pl.BlockSpec((1, d), lambda i, k: (0, 0)), ANY],
        out_shape=[jax.ShapeDtypeStruct((t, d), F32), jax.ShapeDtypeStruct((1, d), F32), _exchange_shape(ga)],
        scratch_shapes=[pltpu.VMEM((tm, d), F32)] + _exchange_scratch(),
        compiler_params=_cp("arbitrary", "arbitrary"))(*segs, w_re, hpad, norm_w, dy_t, ga)


def _spread_heads(v):
    v = jnp.pad(v.reshape(GROUPS, HPG), ((0, 0), (0, GROUPS - HPG))).reshape(1, GROUPS * GROUPS)
    return jnp.pad(v, ((0, 0), (0, LANES - GROUPS * GROUPS)))


def _gather_heads(v):
    return v[0:1, :GROUPS * GROUPS].reshape(GROUPS, GROUPS)[:, :HPG].reshape(1, SSD_HEADS)


def _rope_tables(t):
    half = HEAD_DIM // 2
    inv = ROPE_THETA ** (-jnp.arange(half, dtype=F32) / half)
    pos = (jnp.arange(t) - PAD_LEAD).astype(F32)
    ang = pos[:, None] * inv[None, :]
    cos, sin = jnp.cos(ang), jnp.sin(ang)
    cos_t = jnp.concatenate([cos, cos, cos, cos], axis=1)
    sin_t = jnp.concatenate([-sin, sin, -sin, sin], axis=1)
    return cos_t, sin_t


def _column_pieces():
    runs = [(0, OB + 2 * GROUPS * D_STATE, 0)]
    o = OB + 2 * GROUPS * D_STATE
    runs += [(o + HPG * g, HPG, ODT + GROUPS * g) for g in range(GROUPS)]
    o += SSD_HEADS
    for width, dst in ((D_ATT, OQ), (D_KV, OK), (D_KV, OV), (D_ATT, OG)):
        runs.append((o, width, dst))
        o += width
    assert o == D_IN
    pieces = []
    for o0, width, dst in runs:
        for j in range(N_SHARD):
            lo, hi = max(o0, W_IN_SHARD * j), min(o0 + width, W_IN_SHARD * (j + 1))
            if lo < hi:
                pieces.append((j, lo - W_IN_SHARD * j, hi - W_IN_SHARD * j, dst + lo - o0))
    return pieces


def _shards_to_re(w_all):
    _, k, _ = w_all.shape
    tr = 256

    def body(x_ref, o_ref):
        o_ref[:, ODT:ODT + DT_SLAB] = jnp.zeros((tr, DT_SLAB), o_ref.dtype)
        for j, c0, c1, d0 in _column_pieces():
            o_ref[:, d0:d0 + c1 - c0] = x_ref[j, :, c0:c1]

    return _pallas(body, name="shards_to_re", grid=(k // tr,),
                   in_specs=[pl.BlockSpec((N_SHARD, tr, W_IN_SHARD), lambda i: (0, i, 0))],
                   out_specs=pl.BlockSpec((tr, N_RE), lambda i: (i, 0)),
                   out_shape=jax.ShapeDtypeStruct((k, N_RE), w_all.dtype), compiler_params=_cp("parallel"))(w_all)


def _pair_add_to_shards(parts, got, pieces, shard_rows, core, name):
    n = parts[0].shape[1]
    hn = n // 2
    tc = 128
    nt = hn // tc
    ns = len(parts)
    starts = [sum(p.shape[0] for p in parts[:s]) for s in range(ns)]
    moves = []
    for j, c0, c1, d0 in pieces:
        for s, p in enumerate(parts):
            lo, hi = max(d0, starts[s]), min(d0 + c1 - c0, starts[s] + p.shape[0])
            if lo < hi:
                moves.append((s, lo - starts[s], j, c0 + lo - d0, hi - lo))
    assert sum(m[4] for m in moves) == N_SHARD * shard_rows

    def body(core_ref, *refs):
        own, theirs, o_ref, acc = refs[:ns], refs[ns:2 * ns], refs[2 * ns], refs[2 * ns + 1]
        for s, r0, j, c0, rows in moves:
            acc[j, c0:c0 + rows, :] = own[s][r0:r0 + rows, :] + theirs[s][r0:r0 + rows, :]
        o_ref[...] = _bf(acc[...])

    return _pallas(
        body, name=name,
        grid_spec=pltpu.PrefetchScalarGridSpec(
            num_scalar_prefetch=1, grid=(nt,),
            in_specs=[pl.BlockSpec((p.shape[0], tc), lambda i, core_ref: (0, core_ref[0] * nt + i)) for p in parts]
            + [pl.BlockSpec((p.shape[0], tc), lambda i, core_ref: (0, i)) for p in parts],
            out_specs=pl.BlockSpec((N_SHARD, shard_rows, tc), lambda i, core_ref: (0, 0, i)),
            scratch_shapes=[pltpu.VMEM((N_SHARD, shard_rows, tc), F32)]),
        out_shape=jax.ShapeDtypeStruct((N_SHARD, shard_rows, hn), BF16),
        compiler_params=_cp("parallel"))(core, *parts, *got)


def _local_step(x, target, meta, norm_pre_w, w_re, conv_w, conv_b, dt_bias, a_log, d_skip, ssd_norm_w, sinks,
                w_out_shard, norm_post_w, place):
    seq = x.shape[0]
    t = PAD_LEAD + N_META + seq
    hpad = jnp.concatenate([jnp.zeros((PAD_LEAD, D_MODEL), F32), meta, x], axis=0)
    dt_bias_l, a_log_l, d_skip_l = _spread_heads(dt_bias), _spread_heads(a_log), _spread_heads(d_skip)
    cos_t, sin_t = _rope_tables(t)
    sink_v = sinks.reshape(Q_HEADS)

    proj, hn, w_out_all = _inproj(hpad, norm_pre_w, w_re, w_out_shard)
    w_out = w_out_all.reshape(D_MIX, D_MODEL)
    xbc = _conv_fwd(proj, conv_w, conv_b)
    dt, acs, acst = _dt_prep(proj, dt_bias_l, a_log_l)
    y_ssd, ymix, states = _ssd_fwd(xbc, proj, dt, acs, acst, d_skip_l, ssd_norm_w)
    qr, kr = _rope(proj, OQ, proj, OK, cos_t, sin_t)
    amix = _attn_fwd(qr, kr, proj, sink_v)
    out = _outproj(ymix, amix, w_out)
    dout, dy_t, loss_blk, g_norm_post = _post_loss(out, x, target, norm_post_w)

    g_out_y = _tn_matmul(ymix, dout, "gw_out_y")
    g_out_a, got_y = _tn_matmul(amix, dout, "gw_out_a", carry=g_out_y)
    dmix, got_a = _nt_matmul(dout, w_out, "dmix", carry=g_out_a)
    ga_out = _reduce_pair([g_out_y, g_out_a], [got_y, got_a], [(j, 0, W_OUT_SHARD, W_OUT_SHARD * j) for j in range(N_SHARD)],
                          W_OUT_SHARD, place, "gw_out")
    dq_r, dg, dk_r, dv, gs, slabs_out = _attn_bwd(qr, kr, proj, dmix, sink_v, ga_out)
    g_w_out = _reduce_finish(ga_out, slabs_out, place, "gw_out")
    dq, dk = _rope(dq_r, 0, dk_r, 0, cos_t, -sin_t)
    dz, dxs, db, dc, dacs_g, ddt_g, g_ssd_norm, gdsk = _ssd_bwd(dmix, y_ssd, xbc, proj, dt, acs, acst, states,
                                                                d_skip_l, ssd_norm_w)
    draw, ga, gb = _dt_bwd(dacs_g, ddt_g, dt, proj, dt_bias_l, a_log_l)
    dxs_p, gcw0, gcb0 = _conv_bwd(dxs, proj, conv_w, conv_b, 0, "conv_bwd_x")
    db_p, gcw1, gcb1 = _conv_bwd(db, proj, conv_w, conv_b, D_SSD, "conv_bwd_b")
    dc_p, gcw2, gcb2 = _conv_bwd(dc, proj, conv_w, conv_b, D_SSD + GROUPS * D_STATE, "conv_bwd_c")
    tail = jnp.concatenate([dk, _bf(dv), draw, jnp.zeros((t, DT_SLAB - LANES), BF16)], axis=1)
    segs = [dz, dxs_p, db_p, dc_p, dq, dg, tail]
    g_parts, got_parts = [_tn_matmul(segs[0], hn, "gw_in_0")], []
    for s in range(1, len(segs)):
        part, got = _tn_matmul(segs[s], hn, "gw_in_%d" % s, carry=g_parts[-1])
        g_parts.append(part)
        got_parts.append(got)
    ga_in = _reduce_pair(g_parts, got_parts, _column_pieces(), W_IN_SHARD, place, "gw_in")
    dh, g_norm_pre, slabs_in = _dinproj(segs, w_re, hpad, norm_pre_w, dy_t, ga_in)
    g_w_in = _reduce_finish(ga_in, slabs_in, place, "gw_in")

    gdsk_l = jnp.concatenate([gdsk[g, 0:1, 0:GROUPS] for g in range(GROUPS)], axis=1)
    gdsk_l = jnp.pad(gdsk_l, ((0, 0), (0, LANES - GROUPS * GROUPS)))
    grads = dict(
        meta_tokens=dh[PAD_LEAD:ROW0], norm_pre_w=g_norm_pre, w_in=g_w_in,
        conv_w=jnp.concatenate([gcw0, gcw1, gcw2], axis=1), conv_b=jnp.concatenate([gcb0, gcb1, gcb2], axis=1),
        dt_bias=_gather_heads(gb), a_log=_gather_heads(ga), d_skip=_gather_heads(gdsk_l), ssd_norm_w=g_ssd_norm,
        attn_sinks=gs[0:1, :Q_HEADS], w_out=g_w_out, norm_post_w=g_norm_post)
    return loss_blk[0, 0], dh[ROW0:], grads


ANY = pl.BlockSpec(memory_space=pl.ANY)
MESH = pl.DeviceIdType.MESH
GATHER_CHUNKS = 4
PAIR_CHUNKS = 8
JOIN_CHUNKS = 8


def _rcopy(src, dst, ssem, rsem, dev):
    return pltpu.make_async_remote_copy(src_ref=src, dst_ref=dst, send_sem=ssem, recv_sem=rsem, device_id=dev,
                                        device_id_type=MESH)


def _place():
    x, y, c = lax.axis_index("x"), lax.axis_index("y"), lax.axis_index("c")
    chips = [(1 - x, y), (x, 1 - y), (1 - x, 1 - y)]
    return x, y, c, chips


def _gather_plan(x_ref, out_ref, send_sems, recv_sems, local_sems, hr, kc):
    ch = hr // kc
    assert ch * kc == hr and ch % 16 == 0
    x, y, c, chips = _place()
    me = 2 * x + y
    sibling = (x, y, 1 - c)

    def piece(chip, hc, k):
        return out_ref.at[chip, pl.ds(hc * hr + k * ch, ch), :]

    def local():
        return [pltpu.make_async_copy(x_ref.at[pl.ds(k * ch, ch), :], out_ref.at[me, pl.ds(k * ch, ch), :],
                                      local_sems.at[k]) for k in range(2 * kc)]

    def first():
        return [_rcopy(x_ref.at[pl.ds(c * hr + k * ch, ch), :], piece(me, c, k), send_sems.at[j * kc + k],
                       recv_sems.at[j * kc + k], (*chip, c)) for j, chip in enumerate(chips) for k in range(kc)]

    def passed(hc):
        return [_rcopy(piece(2 * chip[0] + chip[1], hc, k), piece(2 * chip[0] + chip[1], hc, k),
                       send_sems.at[(3 + j) * kc + k], recv_sems.at[(3 + j) * kc + k], sibling)
                for j, chip in enumerate(chips) for k in range(kc)]

    def arrivals():
        return [_rcopy(piece(2 * chip[0] + chip[1], c, k), piece(2 * chip[0] + chip[1], c, k), send_sems.at[j * kc + k],
                       recv_sems.at[j * kc + k], (*chip, c)) for j, chip in enumerate(chips) for k in range(kc)]

    def start():
        for cp in local() + first():
            cp.start()

    def forward():
        for arrived in arrivals():
            arrived.wait_recv()
        for fw in passed(c):
            fw.start()

    def finish():
        for cp in passed(1 - c):
            cp.wait_recv()
        for cp in first() + passed(c):
            cp.wait_send()
        for cp in local():
            cp.wait()

    return start, forward, finish


def _gather_shards(shard, name, kc, chip):
    r, n = shard.shape
    hr = r // 2
    qr = hr // 2
    ch = qr // kc
    assert ch * kc == qr and ch % 16 == 0
    nflow = 12
    tr = 256

    def body(x_ref, out_ref, send_sems, recv_sems):
        x, y, c, _ = _place()
        me, cxn, cyn, cdg = 2 * x + y, 2 * (1 - x) + y, 2 * x + 1 - y, 2 * (1 - x) + 1 - y
        xn, yn, sibling = (1 - x, y, c), (x, 1 - y, c), (x, y, 1 - c)

        def piece(chip, hc, part, k):
            return out_ref.at[chip, pl.ds(hc * hr + part * qr + k * ch, ch), :]

        def own(part, k):
            return x_ref.at[pl.ds(c * hr + part * qr + k * ch, ch), :]

        def sems(flow, k):
            return send_sems.at[flow * kc + k], recv_sems.at[flow * kc + k]

        def arrival(flow, chip, hc, part, k):
            return _rcopy(piece(chip, hc, part, k), piece(chip, hc, part, k), *sems(flow, k), sibling)

        sends = []
        for flow, part, peer in ((0, 0, xn), (1, 1, yn), (2, 0, yn), (3, 1, xn)):
            sends += [_rcopy(own(part, k), piece(me, c, part, k), *sems(flow, k), peer) for k in range(kc)]
        for cp in sends:
            cp.start()
        landing = ((0, cxn, 0), (1, cyn, 1), (2, cyn, 0), (3, cxn, 1), (4, cdg, 0), (5, cdg, 1))
        for i, (flow, chip, part) in enumerate(landing):
            for k in range(kc):
                arrival(flow, chip, c, part, k).wait_recv()
                if flow < 2:
                    on = _rcopy(piece(chip, c, part, k), piece(chip, c, part, k), *sems(4 + flow, k),
                                yn if flow == 0 else xn)
                    on.start()
                    sends.append(on)
                fw = _rcopy(piece(chip, c, part, k), piece(chip, c, part, k), *sems(6 + i, k), sibling)
                fw.start()
                sends.append(fw)
        for i, (flow, chip, part) in enumerate(landing):
            for k in range(kc):
                arrival(6 + i, chip, 1 - c, part, k).wait_recv()
        for cp in sends:
            cp.wait_send()

    full = jax.ShapeDtypeStruct((N_SHARD, r, n), shard.dtype)
    others = _pallas(
        body, name=name, in_specs=[ANY], out_specs=ANY, out_shape=full,
        scratch_shapes=[pltpu.SemaphoreType.DMA((nflow * kc,)), pltpu.SemaphoreType.DMA((nflow * kc,))])(shard)

    def place(chip_ref, own_ref, all_ref, o_ref):
        o_ref[0] = own_ref[...]

    return _pallas(
        place, name=name + "_own",
        grid_spec=pltpu.PrefetchScalarGridSpec(
            num_scalar_prefetch=1, grid=(r // tr,),
            in_specs=[pl.BlockSpec((tr, n), lambda i, chip_ref: (i, 0)), ANY],
            out_specs=pl.BlockSpec((1, tr, n), lambda i, chip_ref: (chip_ref[0], i, 0))),
        out_shape=full, input_output_aliases={2: 0}, compiler_params=_cp("parallel"))(chip, shard, others)


def _pair_copies(src_ref, dst_ref, send_sems, recv_sems):
    hn = src_ref.shape[1] // 2
    cw = hn // PAIR_CHUNKS
    assert cw * PAIR_CHUNKS == hn and cw % LANES == 0
    x, y, c, _ = _place()
    return [_rcopy(src_ref.at[:, pl.ds((1 - c) * hn + k * cw, cw)], dst_ref.at[:, pl.ds(k * cw, cw)],
                   send_sems.at[k], recv_sems.at[k], (x, y, 1 - c)) for k in range(PAIR_CHUNKS)]


def _pair_send(parts, name):
    n = parts[0].shape[1]
    hn = n // 2
    kc = PAIR_CHUNKS
    cw = hn // kc
    assert cw * kc == hn and cw % LANES == 0
    ns = len(parts)

    def body(*refs):
        srcs, dsts, send_sems, recv_sems = refs[:ns], refs[ns:2 * ns], refs[2 * ns], refs[2 * ns + 1]
        x, y, c, _ = _place()
        cps = [_rcopy(srcs[s].at[:, pl.ds((1 - c) * hn + k * cw, cw)], dsts[s].at[:, pl.ds(k * cw, cw)],
                      send_sems.at[s * kc + k], recv_sems.at[s * kc + k], (x, y, 1 - c))
               for s in range(ns) for k in range(kc)]
        for cp in cps:
            cp.start()
        for cp in cps:
            cp.wait()

    return _pallas(
        body, name=name, in_specs=[ANY] * ns, out_specs=[ANY] * ns,
        out_shape=[jax.ShapeDtypeStruct((p.shape[0], hn), F32) for p in parts],
        scratch_shapes=[pltpu.SemaphoreType.DMA((ns * kc,)), pltpu.SemaphoreType.DMA((ns * kc,))])(*parts)


REDUCE_TILE = 256


def _exchange_copies(g_ref, got_ref, send_sems, recv_sems):
    hn = g_ref.shape[2]
    kc = GATHER_CHUNKS
    cw = hn // kc
    assert cw * kc == hn and cw % LANES == 0
    x, y, c, chips = _place()
    return [_rcopy(g_ref.at[2 * chip[0] + chip[1], :, pl.ds(k * cw, cw)], got_ref.at[j, :, pl.ds(k * cw, cw)],
                   send_sems.at[j * kc + k], recv_sems.at[j * kc + k], (*chip, c))
            for j, chip in enumerate(chips) for k in range(kc)]


def _exchange_scratch():
    return [pltpu.SemaphoreType.DMA((3 * GATHER_CHUNKS,)), pltpu.SemaphoreType.DMA((3 * GATHER_CHUNKS,))]


def _exchange_shape(ga):
    return jax.ShapeDtypeStruct((3,) + ga.shape[1:], ga.dtype)


def _chip_sum(ga, got, place, name):
    _, r, hn = ga.shape
    tc = REDUCE_TILE
    nt = hn // tc

    def body(place_ref, own_ref, got_ref, o_ref):
        acc = own_ref[0].astype(F32)
        for j in range(3):
            acc = acc + got_ref[j].astype(F32)
        o_ref[...] = acc

    return _pallas(
        body, name=name,
        grid_spec=pltpu.PrefetchScalarGridSpec(
            num_scalar_prefetch=1, grid=(nt,),
            in_specs=[pl.BlockSpec((1, r, tc), lambda i, place_ref: (place_ref[0], 0, i)),
                      pl.BlockSpec((3, r, tc), lambda i, place_ref: (0, 0, i))],
            out_specs=pl.BlockSpec((r, tc), lambda i, place_ref: (0, place_ref[1] * nt + i))),
        out_shape=jax.ShapeDtypeStruct((r, 2 * hn), F32), compiler_params=_cp("parallel"))(place, ga, got)


def _pair_join(buf, name):
    r, n = buf.shape
    hn = n // 2
    kc = JOIN_CHUNKS
    cw = hn // kc
    assert cw * kc == hn and cw % LANES == 0

    def body(in_ref, out_ref, send_sems, recv_sems):
        x, y, c, _ = _place()
        cps = [_rcopy(out_ref.at[:, pl.ds(c * hn + k * cw, cw)], out_ref.at[:, pl.ds(c * hn + k * cw, cw)],
                      send_sems.at[k], recv_sems.at[k], (x, y, 1 - c)) for k in range(kc)]
        for cp in cps:
            cp.start()
        for k in range(kc):
            cols = out_ref.at[:, pl.ds((1 - c) * hn + k * cw, cw)]
            _rcopy(cols, cols, send_sems.at[k], recv_sems.at[k], (x, y, 1 - c)).wait_recv()
        for cp in cps:
            cp.wait_send()

    return _pallas(
        body, name=name, in_specs=[ANY], out_specs=ANY, out_shape=jax.ShapeDtypeStruct((r, n), F32),
        input_output_aliases={0: 0},
        scratch_shapes=[pltpu.SemaphoreType.DMA((kc,)), pltpu.SemaphoreType.DMA((kc,))])(buf)


def _reduce_pair(parts, got, pieces, shard_rows, place, tag):
    if len(got) < len(parts):
        got = list(got) + list(_pair_send(parts[len(got):], tag + "_pair_send"))
    return _pair_add_to_shards(parts, got, pieces, shard_rows, place[1:2], tag + "_pair_add")


def _reduce_finish(ga, slabs, place, tag):
    return _pair_join(_chip_sum(ga, slabs, place, tag + "_chip_sum"), tag + "_pair_join")


def _allreduce_small(p, name):
    rows, n = p.shape
    ndev = 8

    def body(p_ref, out_ref, slots, send_sems, recv_sems):
        x, y, c, _ = _place()
        my = 4 * x + 2 * y + c
        slots[my] = p_ref[...]
        cps = []
        for k in range(1, ndev):
            kx, ky, kc = (k >> 2) & 1, (k >> 1) & 1, k & 1
            peer = (x ^ kx, y ^ ky, c ^ kc)
            cp = _rcopy(p_ref, slots.at[my], send_sems.at[k - 1], recv_sems.at[k - 1], peer)
            cp.start()
            cps.append(cp)
        for k in range(1, ndev):
            _rcopy(p_ref, slots.at[my ^ k], send_sems.at[k - 1], recv_sems.at[k - 1], (x, y, c)).wait_recv()
        for cp in cps:
            cp.wait_send()
        acc = slots[0]
        for j in range(1, ndev):
            acc = acc + slots[j]
        out_ref[...] = acc

    vm = pl.BlockSpec(memory_space=pltpu.VMEM)
    return _pallas(
        body, name=name, in_specs=[vm], out_specs=vm, out_shape=jax.ShapeDtypeStruct((rows, n), F32),
        scratch_shapes=[pltpu.VMEM((ndev, rows, n), F32), pltpu.SemaphoreType.DMA((ndev - 1,)),
                        pltpu.SemaphoreType.DMA((ndev - 1,))])(p)


def _adamw(w, g, m, v, name):
    r, n = w.shape
    tr = _tile(r, 256, 8)
    c1 = 1.0 / (1.0 - ADAM_B1 ** ADAM_STEP)
    c2 = 1.0 / (1.0 - ADAM_B2 ** ADAM_STEP)

    def body(w_ref, g_ref, m_ref, v_ref, d_ref, mo_ref, vo_ref):
        gv = g_ref[...]
        mn = ADAM_B1 * m_ref[...] + (1.0 - ADAM_B1) * gv
        vn = ADAM_B2 * v_ref[...] + (1.0 - ADAM_B2) * (gv * gv)
        d_ref[...] = -ADAM_LR * ((mn * c1) / (jnp.sqrt(vn * c2) + ADAM_EPS) + ADAM_WD * w_ref[...])
        mo_ref[...] = mn
        vo_ref[...] = vn

    spec = pl.BlockSpec((tr, n), lambda i: (i, 0))
    shp = jax.ShapeDtypeStruct((r, n), F32)
    return _pallas(body, name=name, grid=(r // tr,), in_specs=[spec] * 4, out_specs=[spec] * 3, out_shape=[shp] * 3,
                   compiler_params=_cp("parallel"))(w, g, m, v)


PACK_W = 1024
SMALL_REPL = ("norm_pre_w", "conv_b", "ssd_norm_w", "norm_post_w")
SMALL_HEAD = ("dt_bias", "a_log", "d_skip", "attn_sinks")


def _rows(a):
    return a.reshape(-1, PACK_W)


def _head_row(vals, extra=None):
    parts = [vals[n].reshape(1, -1) for n in SMALL_HEAD]
    if extra is not None:
        parts.append(extra.reshape(1, 1))
    row = jnp.concatenate(parts, axis=1)
    return jnp.pad(row, ((0, 0), (0, PACK_W - row.shape[1])))


def _pad_rows(a, rows):
    return jnp.pad(a, ((0, rows - a.shape[0]), (0, 0)))


def _pack_repl(vals, extra=None):
    body = jnp.concatenate([_rows(vals[n]) for n in SMALL_REPL] + [_head_row(vals, extra)], axis=0)
    return _pad_rows(body, 16)


def _unpack_repl(buf):
    out, r = {}, 0
    for n, k in zip(SMALL_REPL, (2, 4, 2, 2)):
        out[n] = buf[r:r + k].reshape(1, k * PACK_W)
        r += k
    col = 0
    for n, k in zip(SMALL_HEAD, (32, 32, 32, 16)):
        out[n] = buf[r:r + 1, col:col + k]
        col += k
    return out, buf[r, col]


def kernel(x, meta_tokens, norm_pre_w, w_in, conv_w, conv_b, dt_bias, a_log, d_skip, ssd_norm_w, attn_sinks, w_out, norm_post_w, loss_target, m_meta_tokens, m_norm_pre_w, m_w_in, m_conv_w, m_conv_b, m_dt_bias, m_a_log, m_d_skip, m_ssd_norm_w, m_attn_sinks, m_w_out, m_norm_post_w, v_meta_tokens, v_norm_pre_w, v_w_in, v_conv_w, v_conv_b, v_dt_bias, v_a_log, v_d_skip, v_ssd_norm_w, v_attn_sinks, v_w_out, v_norm_post_w):
    names = ("meta_tokens", "norm_pre_w", "w_in", "conv_w", "conv_b", "dt_bias", "a_log", "d_skip", "ssd_norm_w",
             "attn_sinks", "w_out", "norm_post_w")
    w = dict(zip(names, (meta_tokens, norm_pre_w, w_in, conv_w, conv_b, dt_bias, a_log, d_skip, ssd_norm_w, attn_sinks,
                         w_out, norm_post_w)))
    m = dict(zip(names, (m_meta_tokens, m_norm_pre_w, m_w_in, m_conv_w, m_conv_b, m_dt_bias, m_a_log, m_d_skip,
                         m_ssd_norm_w, m_attn_sinks, m_w_out, m_norm_post_w)))
    v = dict(zip(names, (v_meta_tokens, v_norm_pre_w, v_w_in, v_conv_w, v_conv_b, v_dt_bias, v_a_log, v_d_skip,
                         v_ssd_norm_w, v_attn_sinks, v_w_out, v_norm_post_w)))
    cx, cy, cc = lax.axis_index("x"), lax.axis_index("y"), lax.axis_index("c")
    chip = 2 * cx + cy
    meta_cols = D_MODEL // N_SHARD
    conv_cols = D_CONV // N_SHARD

    place = jnp.stack([chip, cc]).astype(jnp.int32)
    w_re = _shards_to_re(_gather_shards(_bf(w_in[0]), "gather_w_in", GATHER_CHUNKS, place[0:1]))
    conv_z = lax.dynamic_update_slice(jnp.zeros((CONV_WIDTH, D_CONV), F32), conv_w[0], (0, chip * conv_cols))
    meta_z = lax.dynamic_update_slice(jnp.zeros((N_META, D_MODEL), F32), meta_tokens, (0, chip * meta_cols))
    small = jnp.concatenate([_rows(conv_z), _rows(meta_z)], axis=0)
    small = _allreduce_small(jnp.where(cc == 0, small, 0.0), "gather_small")
    conv_full = small[0:16].reshape(CONV_WIDTH, D_CONV)
    meta_full = small[16:48].reshape(N_META, D_MODEL)

    loss_dev, grad_x, g = _local_step(x[0], loss_target[0], meta_full, norm_pre_w, w_re, conv_full, conv_b, dt_bias,
                                      a_log, d_skip, ssd_norm_w, attn_sinks, _bf(w_out[0]), norm_post_w, place)
    g_w_in, g_w_out = g["w_in"], g["w_out"]

    packed = jnp.concatenate([_rows(g["conv_w"]), _rows(g["meta_tokens"]), _pack_repl(g, loss_dev)], axis=0)
    red = _allreduce_small(packed, "reduce_small")
    g_conv_full = red[0:16].reshape(CONV_WIDTH, D_CONV)
    g_meta_full = red[16:48].reshape(N_META, D_MODEL)
    g_small, loss = _unpack_repl(red[48:64])
    grads = dict(g_small)
    grads["w_in"] = g_w_in
    grads["w_out"] = g_w_out
    grads["conv_w"] = lax.dynamic_slice(g_conv_full, (0, chip * conv_cols), (CONV_WIDTH, conv_cols))
    grads["meta_tokens"] = lax.dynamic_slice(g_meta_full, (0, chip * meta_cols), (N_META, meta_cols))

    upd = {}
    upd["w_in"] = [jnp.swapaxes(a, 0, 1) for a in _adamw(jnp.swapaxes(w_in[0], 0, 1), g_w_in, jnp.swapaxes(m_w_in[0], 0, 1),
                                                         jnp.swapaxes(v_w_in[0], 0, 1), "adamw_w_in")]
    grads["w_in"] = jnp.swapaxes(g_w_in, 0, 1)
    upd["w_out"] = _adamw(w_out[0], g_w_out, m_w_out[0], v_w_out[0], "adamw_w_out")

    def pack_small(vals, conv, meta):
        return jnp.concatenate([_pad_rows(conv.reshape(CONV_WIDTH, conv_cols), 8), _rows(meta), _pack_repl(vals)], axis=0)

    sm = _adamw(pack_small(w, w["conv_w"], w["meta_tokens"]), pack_small(grads, grads["conv_w"], grads["meta_tokens"]),
                pack_small(m, m["conv_w"], m["meta_tokens"]), pack_small(v, v["conv_w"], v["meta_tokens"]),
                "adamw_small")
    for n in names:
        if n not in ("w_in", "w_out"):
            upd[n] = [None, None, None]
    for k, buf in enumerate(sm):
        upd["conv_w"][k] = buf[0:CONV_WIDTH]
        upd["meta_tokens"][k] = buf[8:16].reshape(N_META, meta_cols)
        rest, _ = _unpack_repl(buf[16:32])
        for n in SMALL_REPL + SMALL_HEAD:
            upd[n][k] = rest[n]

    def shaped(n, a):
        return a.reshape(w[n].shape)

    outs = [loss, grad_x[None]]
    outs += [shaped(n, grads[n]) for n in names]
    for k in range(3):
        outs += [shaped(n, upd[n][k]) for n in names]
    return tuple(outs)
```

```python
import functools

import jax
import jax.numpy as jnp
from jax import lax
from jax.experimental import pallas as pl
from jax.experimental.pallas import tpu as pltpu

F32 = jnp.float32
BF16 = jnp.bfloat16

D_MODEL = 2048
CHUNK = 64
N_META = 16
PAD_LEAD = CHUNK - N_META
ROW0 = PAD_LEAD + N_META
EPS = 1e-6
SSD_HEADS = 32
HEAD_DIM = 64
GROUPS = 8
HPG = SSD_HEADS // GROUPS
D_STATE = 128
D_SSD = 2048
GROUP_W = D_SSD // GROUPS
CONV_WIDTH = 4
D_CONV = 4096
Q_HEADS = 16
KV_HEADS = 4
REP = Q_HEADS // KV_HEADS
D_ATT = 1024
D_KV = 256
BAND_CHUNKS = 3
ROPE_THETA = 10000.0
D_MIX = D_SSD + D_ATT
D_IN = 8736
N_SHARD = 4
W_IN_SHARD = D_IN // N_SHARD
W_OUT_SHARD = D_MIX // N_SHARD

OZ, OXS, OB, OC, OQ, OG, OK, OV, ODT = 0, 2048, 4096, 5120, 6144, 7168, 8192, 8448, 8704
DT_SLAB = 512
N_RE = ODT + DT_SLAB
LANES = 128

ADAM_LR, ADAM_B1, ADAM_B2, ADAM_EPS, ADAM_WD, ADAM_STEP = 0.001, 0.9, 0.999, 1e-08, 0.01, 10

SSD_FWD_GROUPS_PER_STEP = 4
SSD_BWD_GROUPS_PER_STEP = 8
SEG_TILE = 1024
VMEM_LIMIT = 52 * 1024 * 1024
NEG = -1e30
HI = lax.Precision.HIGHEST


def _pallas(body, **kw):
    return pl.pallas_call(body, **kw)


def _cp(*sem):
    return pltpu.CompilerParams(dimension_semantics=sem, vmem_limit_bytes=VMEM_LIMIT)


def _tile(n, cap, mult=16):
    best = None
    for d in range(mult, min(n, cap) + 1, mult):
        if n % d == 0:
            best = d
    assert best is not None, (n, cap)
    return best


def _nt(a, b):
    return lax.dot_general(a, b, (((1,), (1,)), ((), ())), preferred_element_type=F32)


def _tn(a, b):
    return lax.dot_general(a, b, (((0,), (0,)), ((), ())), preferred_element_type=F32)


def _mm(a, b):
    return jnp.dot(a, b, preferred_element_type=F32)


def _sigmoid(x):
    return 1.0 / (1.0 + jnp.exp(-x))


def _bf(x):
    return x.astype(BF16)


def _inproj(hpad, norm_w, w_re, w_out_shard):
    t, d = hpad.shape
    n = w_re.shape[1]
    tm, tn = _tile(t, 1040), 1024
    ni, nj = t // tm, n // tn
    r_out, n_out = w_out_shard.shape
    kc = GATHER_CHUNKS

    def body(h_ref, nw_ref, w_ref, ws_ref, proj_ref, hn_ref, wall_ref, hn_s, send_sems, recv_sems, local_sems):
        i, j = pl.program_id(0), pl.program_id(1)
        start, forward, finish = _gather_plan(ws_ref, wall_ref, send_sems, recv_sems, local_sems, r_out // 2, kc)
        pl.when((i == 0) & (j == 0))(start)
        pl.when((i == ni // 2) & (j == 0))(forward)

        @pl.when(j == 0)
        def _():
            h = h_ref[...]
            ms = jnp.mean(h * h, axis=-1, keepdims=True)
            hn = _bf(h * lax.rsqrt(ms + EPS) * nw_ref[...])
            hn_s[...] = hn
            hn_ref[...] = hn
        proj_ref[...] = _mm(hn_s[...], w_ref[...])
        pl.when((i == ni - 1) & (j == nj - 1))(finish)

    return _pallas(
        body, name="inproj", grid=(ni, nj),
        in_specs=[pl.BlockSpec((tm, d), lambda i, j: (i, 0)), pl.BlockSpec((1, d), lambda i, j: (0, 0)),
                  pl.BlockSpec((d, tn), lambda i, j: (0, j)), ANY],
        out_specs=[pl.BlockSpec((tm, tn), lambda i, j: (i, j)), pl.BlockSpec((tm, d), lambda i, j: (i, 0)), ANY],
        out_shape=[jax.ShapeDtypeStruct((t, n), F32), jax.ShapeDtypeStruct((t, d), BF16),
                   jax.ShapeDtypeStruct((N_SHARD, r_out, n_out), w_out_shard.dtype)],
        scratch_shapes=[pltpu.VMEM((tm, d), BF16), pltpu.SemaphoreType.DMA((6 * kc,)), pltpu.SemaphoreType.DMA((6 * kc,)),
                        pltpu.SemaphoreType.DMA((2 * kc,))],
        compiler_params=_cp("arbitrary", "arbitrary"))(hpad, norm_w, w_re, w_out_shard)


def _conv_fwd(proj, conv_w, conv_b):
    t = proj.shape[0]
    tc = 256
    off = OXS // tc

    def body(x_ref, w_ref, b_ref, o_ref):
        x = x_ref[...]
        w = w_ref[...]
        row = lax.broadcasted_iota(jnp.int32, (t, tc), 0)
        u = b_ref[...] + w[3:4, :] * x
        for k in range(1, CONV_WIDTH):
            u = u + w[3 - k:4 - k, :] * jnp.where(row >= k, pltpu.roll(x, k, 0), 0.0)
        h = 0.5 * u
        o_ref[...] = h + h * jnp.tanh(h)

    return _pallas(
        body, name="conv_fwd", grid=(D_CONV // tc,),
        in_specs=[pl.BlockSpec((t, tc), lambda j: (0, j + off)), pl.BlockSpec((CONV_WIDTH, tc), lambda j: (0, j)),
                  pl.BlockSpec((1, tc), lambda j: (0, j))],
        out_specs=pl.BlockSpec((t, tc), lambda j: (0, j)),
        out_shape=jax.ShapeDtypeStruct((t, D_CONV), F32),
        compiler_params=_cp("parallel"))(proj, conv_w, conv_b)


def _softplus(u):
    e = jnp.exp(-jnp.abs(u))
    w = 1.0 + e
    l1p = jnp.where(w == 1.0, e, jnp.log(w) * (e / jnp.where(w == 1.0, 1.0, w - 1.0)))
    return jnp.maximum(u, 0.0) + l1p


def _chunks_per_step(nc):
    return max(d for d in range(1, 14) if nc % d == 0)


def _dt_prep(proj, dt_bias_l, a_log_l):
    t = proj.shape[0]
    nc = t // CHUNK
    q = CHUNK
    cps = _chunks_per_step(nc)
    rows = cps * q

    def body(raw_ref, bias_ref, alog_ref, dt_ref, acs_ref, acst_ref):
        ri = lax.broadcasted_iota(jnp.int32, (q, q), 0)
        ci = lax.broadcasted_iota(jnp.int32, (q, q), 1)
        tri = (ri >= ci).astype(F32)
        neg_a = -jnp.exp(alog_ref[...])
        for k in range(cps):
            rk = slice(q * k, q * (k + 1))
            sp = _softplus(raw_ref[rk, :] + bias_ref[...])
            row = pl.program_id(0) * rows + q * k + lax.broadcasted_iota(jnp.int32, (q, LANES), 0)
            dt = jnp.where(row >= PAD_LEAD, sp, 0.0)
            acs = jnp.dot(tri, dt * neg_a, preferred_element_type=F32, precision=HI)
            dt_ref[rk, :] = dt
            acs_ref[rk, :] = acs
            acst_ref[k] = acs.T

    return _pallas(
        body, name="dt_prep", grid=(nc // cps,),
        in_specs=[pl.BlockSpec((rows, LANES), lambda c: (c, ODT // LANES)), pl.BlockSpec((1, LANES), lambda c: (0, 0)),
                  pl.BlockSpec((1, LANES), lambda c: (0, 0))],
        out_specs=[pl.BlockSpec((rows, LANES), lambda c: (c, 0)), pl.BlockSpec((rows, LANES), lambda c: (c, 0)),
                   pl.BlockSpec((cps, LANES, q), lambda c: (c, 0, 0))],
        out_shape=[jax.ShapeDtypeStruct((t, LANES), F32), jax.ShapeDtypeStruct((t, LANES), F32),
                   jax.ShapeDtypeStruct((nc, LANES, q), F32)],
        compiler_params=_cp("parallel"))(proj, dt_bias_l, a_log_l)


def _head_cols(blk, idx):
    lane = lax.broadcasted_iota(jnp.int32, blk.shape, 1)
    return jnp.sum(jnp.where(lane == idx, blk, 0.0), axis=1, keepdims=True)


class _HeadVals:
    pass


def _lane_head(shape):
    return lax.broadcasted_iota(jnp.int32, shape, len(shape) - 1) >> 6


def _group_heads(g, gi, dtb, acsb, acst_ref, dskb):
    q = dtb.shape[0]
    hv = _HeadVals()
    lh = _lane_head((1, GROUP_W))
    hv.dt = jnp.zeros((q, GROUP_W), F32)
    hv.acs = jnp.zeros((q, GROUP_W), F32)
    hv.acs_last = jnp.zeros((1, GROUP_W), F32)
    hv.dsk = jnp.zeros((1, GROUP_W), F32)
    rows = []
    for r in range(HPG):
        idx = GROUPS * g + r
        sel = lh == r
        acs_r = acst_ref[0, GROUPS * gi + r:GROUPS * gi + r + 1, :]
        rows.append(acs_r)
        hv.dt = jnp.where(sel, _head_cols(dtb, idx), hv.dt)
        hv.acs = jnp.where(sel, _head_cols(acsb, idx), hv.acs)
        hv.acs_last = jnp.where(sel, acs_r[:, q - 1:q], hv.acs_last)
        hv.dsk = jnp.where(sel, _head_cols(dskb, idx), hv.dsk)
    hv.acs_row = jnp.concatenate(rows, axis=1)
    return hv


def _head_tri(q, lower):
    ri = lax.broadcasted_iota(jnp.int32, (q, GROUP_W), 0)
    li = lax.broadcasted_iota(jnp.int32, (q, GROUP_W), 1) & (HEAD_DIM - 1)
    return ri >= li if lower else ri <= li


def _block_diag_mask():
    rb = lax.broadcasted_iota(jnp.int32, (GROUP_W, GROUP_W), 0) >> 6
    cb = lax.broadcasted_iota(jnp.int32, (GROUP_W, GROUP_W), 1) >> 6
    return rb == cb


def _block_diag(v, mask):
    return jnp.where(mask, jnp.concatenate([v] * HPG, axis=0), jnp.zeros((), v.dtype))


def _head_sums(v, r):
    return jnp.sum(jnp.where(_lane_head((1, GROUP_W)) == r, v, 0.0), axis=1, keepdims=True)


def _ssd_fwd(xbc, proj, dt, acs, acst, d_skip_l, ssd_norm_w):
    t = xbc.shape[0]
    q = CHUNK
    nc = t // q

    gps = SSD_FWD_GROUPS_PER_STEP
    gw, sw = gps * GROUP_W, gps * D_STATE

    def body(xs_ref, b_ref, c_ref, dt_ref, acs_ref, acst_ref, z_ref, dsk_ref, nw_ref,
             y_ref, ymix_ref, st_ref, state):
        @pl.when(pl.program_id(1) == 0)
        def _():
            state[...] = jnp.zeros_like(state)

        lower = _head_tri(q, True)
        bd_mask = _block_diag_mask()
        for gi in range(gps):
            g = gps * pl.program_id(0) + gi
            cols = slice(GROUP_W * gi, GROUP_W * (gi + 1))
            x = xs_ref[:, cols]
            bmb = _bf(b_ref[:, D_STATE * gi:D_STATE * (gi + 1)])
            cmb = _bf(c_ref[:, D_STATE * gi:D_STATE * (gi + 1)])
            hv = _group_heads(g, gi, dt_ref[...], acs_ref[...], acst_ref, dsk_ref[...])
            decay = jnp.exp(jnp.where(lower, hv.acs - hv.acs_row, NEG))
            m_all = _bf(_nt(cmb, jnp.concatenate([bmb] * HPG, axis=0)) * decay)
            xdt = x * hv.dt
            s_prev = state[gi]
            st_ref[0, gi] = s_prev
            y = (_mm(m_all, _block_diag(_bf(xdt), bd_mask)) + _mm(cmb, _bf(s_prev)) * jnp.exp(hv.acs) + hv.dsk * x)
            state[gi] = jnp.exp(hv.acs_last) * s_prev + _tn(bmb, _bf(xdt * jnp.exp(hv.acs_last - hv.acs)))
            y_ref[:, cols] = y
            z = z_ref[:, cols]
            yg = y * (z * _sigmoid(z))
            ms = jnp.mean(yg * yg, axis=-1, keepdims=True)
            ymix_ref[:, cols] = _bf(yg * lax.rsqrt(ms + EPS) * nw_ref[:, cols])

    return _pallas(
        body, name="ssd_fwd", grid=(GROUPS // gps, nc),
        in_specs=[pl.BlockSpec((q, gw), lambda g, c: (c, g)),
                  pl.BlockSpec((q, sw), lambda g, c: (c, D_SSD // sw + g)),
                  pl.BlockSpec((q, sw), lambda g, c: (c, (D_SSD + GROUPS * D_STATE) // sw + g)),
                  pl.BlockSpec((q, LANES), lambda g, c: (c, 0)), pl.BlockSpec((q, LANES), lambda g, c: (c, 0)),
                  pl.BlockSpec((1, gps * GROUPS, q), lambda g, c: (c, g, 0)),
                  pl.BlockSpec((q, gw), lambda g, c: (c, g)),
                  pl.BlockSpec((1, LANES), lambda g, c: (0, 0)), pl.BlockSpec((1, gw), lambda g, c: (0, g))],
        out_specs=[pl.BlockSpec((q, gw), lambda g, c: (c, g)), pl.BlockSpec((q, gw), lambda g, c: (c, g)),
                   pl.BlockSpec((1, gps, D_STATE, GROUP_W), lambda g, c: (c, g, 0, 0))],
        out_shape=[jax.ShapeDtypeStruct((t, D_SSD), F32), jax.ShapeDtypeStruct((t, D_SSD), BF16),
                   jax.ShapeDtypeStruct((nc, GROUPS, D_STATE, GROUP_W), F32)],
        scratch_shapes=[pltpu.VMEM((gps, D_STATE, GROUP_W), F32)],
        compiler_params=_cp("parallel", "arbitrary"))(xbc, xbc, xbc, dt, acs, acst, proj, d_skip_l, ssd_norm_w)


def _swap_halves(v):
    lane = lax.broadcasted_iota(jnp.int32, v.shape, 1)
    return jnp.where((lane & (HEAD_DIM - 1)) < HEAD_DIM // 2, pltpu.roll(v, LANES - HEAD_DIM // 2, 1),
                     pltpu.roll(v, HEAD_DIM // 2, 1))


def _rope(qsrc, q_off, ksrc, k_off, cos_t, sin_t):
    t = qsrc.shape[0]
    tr = _tile(t, 832)
    q_scale = HEAD_DIM ** -0.5

    def body(q_ref, k_ref, cos_ref, sin_ref, qo_ref, ko_ref):
        cs = cos_ref[...]
        sn = sin_ref[...]
        for src, dst, width, scale in ((q_ref, qo_ref, D_ATT, q_scale), (k_ref, ko_ref, D_KV, 1.0)):
            for s in range(width // LANES):
                v = src[:, LANES * s:LANES * (s + 1)].astype(F32)
                dst[:, LANES * s:LANES * (s + 1)] = _bf((v * cs + _swap_halves(v) * sn) * scale)

    return _pallas(
        body, name="rope", grid=(t // tr,),
        in_specs=[pl.BlockSpec((tr, D_ATT), lambda i: (i, q_off // D_ATT)),
                  pl.BlockSpec((tr, D_KV), lambda i: (i, k_off // D_KV)),
                  pl.BlockSpec((tr, LANES), lambda i: (i, 0)), pl.BlockSpec((tr, LANES), lambda i: (i, 0))],
        out_specs=[pl.BlockSpec((tr, D_ATT), lambda i: (i, 0)), pl.BlockSpec((tr, D_KV), lambda i: (i, 0))],
        out_shape=[jax.ShapeDtypeStruct((t, D_ATT), BF16), jax.ShapeDtypeStruct((t, D_KV), BF16)],
        compiler_params=_cp("parallel"))(qsrc, ksrc, cos_t, sin_t)


def _attn_chunks_per_step(nc):
    return max(d for d in range(1, 6) if nc % d == 0)


def _band(ref, c):
    return [ref[pl.ds(pl.multiple_of(jnp.maximum(c - j, 0) * CHUNK, CHUNK), CHUNK), :] for j in (2, 1, 0)]


def _attn_probs(qh, kb, sink_col, valid):
    s = jnp.where(valid, _nt(qh, kb), NEG)
    m = jnp.maximum(jnp.max(s, axis=1, keepdims=True), sink_col)
    p = jnp.exp(s - m)
    psink = jnp.exp(sink_col - m)
    return p, psink, 1.0 / (jnp.sum(p, axis=1, keepdims=True) + psink)


def _attn_operands(c, q, k_refs, v_refs, sink_ref, h):
    qh = jnp.concatenate([q[:, HEAD_DIM * (REP * h + r):HEAD_DIM * (REP * h + r + 1)] for r in range(REP)], axis=0)
    kb = jnp.concatenate([k[:, HEAD_DIM * h:HEAD_DIM * (h + 1)] for k in k_refs], axis=0)
    vb = jnp.concatenate([_bf(v[:, HEAD_DIM * h:HEAD_DIM * (h + 1)]) for v in v_refs], axis=0)
    rows = lax.broadcasted_iota(jnp.int32, (REP * CHUNK, 1), 0) >> 6
    sink_col = jnp.zeros((REP * CHUNK, 1), F32)
    for r in range(REP):
        sink_col = jnp.where(rows == r, sink_ref[REP * h + r], sink_col)
    key_abs = (c - (BAND_CHUNKS - 1)) * CHUNK + lax.broadcasted_iota(jnp.int32, (1, BAND_CHUNKS * CHUNK), 1)
    return qh, kb, vb, sink_col, key_abs >= PAD_LEAD


def _attn_fwd(qr, kr, proj, sinks):
    t = qr.shape[0]
    nc = t // CHUNK
    cps = _attn_chunks_per_step(nc)
    rows = cps * CHUNK

    def body(q_ref, k_ref, v_ref, g_ref, sink_ref, o_ref):
        for j in range(cps):
            c = pl.program_id(0) * cps + j
            rj = slice(CHUNK * j, CHUNK * (j + 1))
            ks, vs = _band(k_ref, c), _band(v_ref, c)
            q = q_ref[rj, :]
            outs = []
            for h in range(KV_HEADS):
                qh, kb, vb, sink_col, valid = _attn_operands(c, q, ks, vs, sink_ref, h)
                p, _, inv = _attn_probs(qh, kb, sink_col, valid)
                o = _mm(_bf(p), vb) * inv
                outs += [o[CHUNK * r:CHUNK * (r + 1)] for r in range(REP)]
            att = jnp.concatenate(outs, axis=1)
            gate = g_ref[rj, :]
            o_ref[rj, :] = _bf(att * (gate * _sigmoid(gate)))

    return _pallas(
        body, name="attn_fwd", grid=(nc // cps,),
        in_specs=[pl.BlockSpec((rows, D_ATT), lambda i: (i, 0)), pl.BlockSpec((t, D_KV), lambda i: (0, 0)),
                  pl.BlockSpec((t, D_KV), lambda i: (0, OV // D_KV)),
                  pl.BlockSpec((rows, D_ATT), lambda i: (i, OG // D_ATT)), pl.BlockSpec(memory_space=pltpu.SMEM)],
        out_specs=pl.BlockSpec((rows, D_ATT), lambda i: (i, 0)),
        out_shape=jax.ShapeDtypeStruct((t, D_ATT), BF16),
        compiler_params=_cp("parallel"))(qr, kr, proj, proj, sinks)


def _outproj(ymix, amix, w_out):
    t = ymix.shape[0]
    tm, tn = _tile(t, 832), 1024

    def body(y_ref, a_ref, wy_ref, wa_ref, o_ref):
        o_ref[...] = _mm(y_ref[...], wy_ref[...]) + _mm(a_ref[...], wa_ref[...])

    return _pallas(
        body, name="outproj", grid=(t // tm, D_MODEL // tn),
        in_specs=[pl.BlockSpec((tm, D_SSD), lambda i, j: (i, 0)), pl.BlockSpec((tm, D_ATT), lambda i, j: (i, 0)),
                  pl.BlockSpec((D_SSD, tn), lambda i, j: (0, j)),
                  pl.BlockSpec((D_ATT, tn), lambda i, j: (D_SSD // D_ATT, j))],
        out_specs=pl.BlockSpec((tm, tn), lambda i, j: (i, j)),
        out_shape=jax.ShapeDtypeStruct((t, D_MODEL), F32),
        compiler_params=_cp("parallel", "parallel"))(ymix, amix, w_out, w_out)


def _post_loss(out, x, target, norm_post_w):
    t = out.shape[0]
    nc = t // CHUNK
    cps = _attn_chunks_per_step(nc)
    rows = cps * CHUNK

    def body(o_ref, *refs):
        x_refs, tg_refs = refs[:cps], refs[cps:2 * cps]
        nw_ref, dout_ref, dy_ref, loss_ref, gnw_ref = refs[2 * cps:]
        i = pl.program_id(0)

        @pl.when(i == 0)
        def _():
            loss_ref[...] = jnp.zeros_like(loss_ref)
            gnw_ref[...] = jnp.zeros_like(gnw_ref)

        nw = nw_ref[...]
        loss = jnp.zeros((), F32)
        gnw = jnp.zeros((1, D_MODEL), F32)
        for k in range(cps):
            rk = slice(CHUNK * k, CHUNK * (k + 1))
            frames = i * cps + k > 0
            o = o_ref[rk, :]
            rstd = lax.rsqrt(jnp.mean(o * o, axis=-1, keepdims=True) + EPS)
            n = o * rstd
            err = jnp.where(frames, x_refs[k][...] + n * nw - tg_refs[k][...], 0.0)
            loss = loss + jnp.sum(err * err)
            dy = err * (1.0 / D_MODEL)
            dy_ref[rk, :] = dy
            gnw = gnw + jnp.sum(dy * n, axis=0, keepdims=True)
            dn = dy * nw
            dout_ref[rk, :] = _bf(rstd * (dn - n * jnp.mean(dn * n, axis=-1, keepdims=True)))
        loss_ref[...] += loss * (0.5 / D_MODEL)
        gnw_ref[...] += gnw

    lower = [pl.BlockSpec((CHUNK, D_MODEL), functools.partial(lambda i, k: (jnp.maximum(i * cps + k - 1, 0), 0), k=k))
             for k in range(cps)]
    return _pallas(
        body, name="post_loss", grid=(nc // cps,),
        in_specs=[pl.BlockSpec((rows, D_MODEL), lambda i: (i, 0))] + lower + lower
        + [pl.BlockSpec((1, D_MODEL), lambda i: (0, 0))],
        out_specs=[pl.BlockSpec((rows, D_MODEL), lambda i: (i, 0)), pl.BlockSpec((rows, D_MODEL), lambda i: (i, 0)),
                   pl.BlockSpec((8, LANES), lambda i: (0, 0)), pl.BlockSpec((1, D_MODEL), lambda i: (0, 0))],
        out_shape=[jax.ShapeDtypeStruct((t, D_MODEL), BF16), jax.ShapeDtypeStruct((t, D_MODEL), F32),
                   jax.ShapeDtypeStruct((8, LANES), F32), jax.ShapeDtypeStruct((1, D_MODEL), F32)],
        compiler_params=_cp("arbitrary"))(out, *([x] * cps), *([target] * cps), norm_post_w)


def _carried(grid, carry):
    if carry is None:
        return [], [], [], [], lambda refs: None, lambda refs: None
    hn = carry.shape[1] // 2

    def at(ids, which):
        cond = None
        for d, size in enumerate(grid):
            here = pl.program_id(d) == (0 if which == "first" else size - 1)
            cond = here if cond is None else cond & here
        return cond

    def start(refs):
        @pl.when(at(grid, "first"))
        def _():
            for cp in _pair_copies(*refs):
                cp.start()

    def finish(refs):
        @pl.when(at(grid, "last"))
        def _():
            for cp in _pair_copies(*refs):
                cp.wait()

    return ([ANY], [ANY], [jax.ShapeDtypeStruct((carry.shape[0], hn), F32)],
            [pltpu.SemaphoreType.DMA((PAIR_CHUNKS,)), pltpu.SemaphoreType.DMA((PAIR_CHUNKS,))], start, finish)


def _nt_matmul(a, b, name, carry=None):
    t, k = a.shape
    n = b.shape[0]
    tm, tn = _tile(t, 832), 1024
    grid = (t // tm, n // tn)
    cin, cout, cshape, cscratch, start, finish = _carried(grid, carry)

    def body(a_ref, b_ref, *refs):
        o_ref = refs[len(cin)]
        comm = (refs[0], refs[2], refs[3], refs[4]) if carry is not None else None
        start(comm)
        o_ref[...] = _nt(a_ref[...], b_ref[...])
        finish(comm)

    res = _pallas(
        body, name=name, grid=grid,
        in_specs=[pl.BlockSpec((tm, k), lambda i, j: (i, 0)), pl.BlockSpec((tn, k), lambda i, j: (j, 0))] + cin,
        out_specs=[pl.BlockSpec((tm, tn), lambda i, j: (i, j))] + cout,
        out_shape=[jax.ShapeDtypeStruct((t, n), F32)] + cshape, scratch_shapes=cscratch,
        compiler_params=_cp("arbitrary", "arbitrary"))(a, b, *([carry] if carry is not None else []))
    return res if carry is not None else res[0]


def _tn_matmul(a, b, name, carry=None):
    t, m = a.shape
    n = b.shape[1]
    tk, tm, tn = _tile(t, 832), min(m, 2048), min(n, 2048)
    nk = t // tk
    grid = (m // tm, n // tn, nk)
    cin, cout, cshape, cscratch, start, finish = _carried(grid, carry)

    def body(a_ref, b_ref, *refs):
        o_ref = refs[len(cin)]
        comm = (refs[0], refs[2], refs[3], refs[4]) if carry is not None else None
        start(comm)

        @pl.when(pl.program_id(2) == 0)
        def _():
            o_ref[...] = jnp.zeros_like(o_ref)
        o_ref[...] += _tn(a_ref[...], b_ref[...])
        finish(comm)

    res = _pallas(
        body, name=name, grid=grid,
        in_specs=[pl.BlockSpec((tk, tm), lambda i, j, k: (k, i)), pl.BlockSpec((tk, tn), lambda i, j, k: (k, j))] + cin,
        out_specs=[pl.BlockSpec((tm, tn), lambda i, j, k: (i, j))] + cout,
        out_shape=[jax.ShapeDtypeStruct((m, n), F32)] + cshape, scratch_shapes=cscratch,
        compiler_params=_cp("arbitrary", "arbitrary", "arbitrary"))(a, b, *([carry] if carry is not None else []))
    return res if carry is not None else res[0]


def _attn_bwd(qr, kr, proj, dmix, sinks, ga):
    t = qr.shape[0]
    nc = t // CHUNK
    cps = _attn_chunks_per_step(nc)
    nsteps = nc // cps
    rows_step = cps * CHUNK

    def body(q_ref, k_ref, v_ref, g_ref, da_ref, sink_ref, ga_ref, dq_ref, dg_ref, dk_ref, dv_ref, gs_ref,
             got_ref, send_sems, recv_sems):
        step = pl.program_id(0)

        @pl.when(step == 0)
        def _():
            for cp in _exchange_copies(ga_ref, got_ref, send_sems, recv_sems):
                cp.start()
            dk_ref[...] = jnp.zeros_like(dk_ref)
            dv_ref[...] = jnp.zeros_like(dv_ref)
            gs_ref[...] = jnp.zeros_like(gs_ref)

        lane = lax.broadcasted_iota(jnp.int32, (1, LANES), 1)
        rows = lax.broadcasted_iota(jnp.int32, (REP * CHUNK, 1), 0) >> 6
        gs = jnp.zeros((1, LANES), F32)
        dk_parts = [[] for _ in range(cps + BAND_CHUNKS - 1)]
        dv_parts = [[] for _ in range(cps + BAND_CHUNKS - 1)]
        for j in range(cps):
            c = step * cps + j
            rj = slice(CHUNK * j, CHUNK * (j + 1))
            ks, vs = _band(k_ref, c), _band(v_ref, c)
            q = q_ref[rj, :]
            gate = g_ref[rj, :]
            sg = _sigmoid(gate)
            da = da_ref[rj, :]
            datt = da * (gate * sg)
            dqs, atts, dks, dvs = [], [], [], []
            for h in range(KV_HEADS):
                qh, kb, vb, sink_col, valid = _attn_operands(c, q, ks, vs, sink_ref, h)
                p, psink, inv = _attn_probs(qh, kb, sink_col, valid)
                pb = _bf(p)
                o = _mm(pb, vb) * inv
                do = jnp.concatenate([datt[:, HEAD_DIM * (REP * h + r):HEAD_DIM * (REP * h + r + 1)]
                                      for r in range(REP)], axis=0)
                dob = _bf(do * inv)
                delta = jnp.sum(do * o, axis=1, keepdims=True) * inv
                ds = _bf(p * (_nt(dob, vb) - delta))
                gsink = -psink * delta
                for r in range(REP):
                    gs = gs + jnp.where(lane == REP * h + r, jnp.sum(jnp.where(rows == r, gsink, 0.0)), 0.0)
                dqh = _mm(ds, kb)
                dqs += [dqh[CHUNK * r:CHUNK * (r + 1)] for r in range(REP)]
                atts += [o[CHUNK * r:CHUNK * (r + 1)] for r in range(REP)]
                dks.append(_tn(ds, qh))
                dvs.append(_tn(pb, dob))
            dq_ref[rj, :] = jnp.concatenate(dqs, axis=1)
            att = jnp.concatenate(atts, axis=1)
            dg_ref[rj, :] = _bf(da * att * (sg * (1.0 + gate * (1.0 - sg))))
            dkf = jnp.concatenate(dks, axis=1)
            dvf = jnp.concatenate(dvs, axis=1)
            for b in range(BAND_CHUNKS):
                dk_parts[j + b].append(dkf[CHUNK * b:CHUNK * (b + 1)])
                dv_parts[j + b].append(dvf[CHUNK * b:CHUNK * (b + 1)])
        gs_ref[0:1, :] += gs
        for rel in range(cps + BAND_CHUNKS - 1):
            r0 = pl.multiple_of(jnp.maximum(step * cps - (BAND_CHUNKS - 1) + rel, 0) * CHUNK, CHUNK)
            dk_ref[pl.ds(r0, CHUNK), :] += sum(dk_parts[rel][1:], dk_parts[rel][0])
            dv_ref[pl.ds(r0, CHUNK), :] += sum(dv_parts[rel][1:], dv_parts[rel][0])

        @pl.when(step == nsteps - 1)
        def _():
            for cp in _exchange_copies(ga_ref, got_ref, send_sems, recv_sems):
                cp.wait()

    return _pallas(
        body, name="attn_bwd", grid=(nsteps,),
        in_specs=[pl.BlockSpec((rows_step, D_ATT), lambda i: (i, 0)), pl.BlockSpec((t, D_KV), lambda i: (0, 0)),
                  pl.BlockSpec((t, D_KV), lambda i: (0, OV // D_KV)),
                  pl.BlockSpec((rows_step, D_ATT), lambda i: (i, OG // D_ATT)),
                  pl.BlockSpec((rows_step, D_ATT), lambda i: (i, D_SSD // D_ATT)),
                  pl.BlockSpec(memory_space=pltpu.SMEM), ANY],
        out_specs=[pl.BlockSpec((rows_step, D_ATT), lambda i: (i, 0)), pl.BlockSpec((rows_step, D_ATT), lambda i: (i, 0)),
                   pl.BlockSpec((t, D_KV), lambda i: (0, 0)), pl.BlockSpec((t, D_KV), lambda i: (0, 0)),
                   pl.BlockSpec((8, LANES), lambda i: (0, 0)), ANY],
        out_shape=[jax.ShapeDtypeStruct((t, D_ATT), F32), jax.ShapeDtypeStruct((t, D_ATT), BF16),
                   jax.ShapeDtypeStruct((t, D_KV), F32), jax.ShapeDtypeStruct((t, D_KV), F32),
                   jax.ShapeDtypeStruct((8, LANES), F32), _exchange_shape(ga)],
        scratch_shapes=_exchange_scratch(),
        compiler_params=_cp("arbitrary"))(qr, kr, proj, proj, dmix, sinks, ga)


def _ssd_bwd(dmix, y_ssd, xbc, proj, dt, acs, acst, states, d_skip_l, ssd_norm_w):
    t = xbc.shape[0]
    q = CHUNK
    nc = t // q
    gps = SSD_BWD_GROUPS_PER_STEP
    gw, sw = gps * GROUP_W, gps * D_STATE

    def body(dmix_ref, y_ref, z_ref, nw_ref, xs_ref, b_ref, c_ref, dt_ref, acs_ref, acst_ref, st_ref, dsk_ref,
             dz_ref, dxs_ref, db_ref, dc_ref, dacs_ref, ddt_ref, gnw_ref, gdsk_ref, dstate):
        @pl.when(pl.program_id(1) == 0)
        def _():
            dstate[...] = jnp.zeros_like(dstate)
            gnw_ref[...] = jnp.zeros_like(gnw_ref)
            gdsk_ref[...] = jnp.zeros_like(gdsk_ref)

        last_row = lax.broadcasted_iota(jnp.int32, (q, 1), 0) == q - 1
        lane = lax.broadcasted_iota(jnp.int32, (q, LANES), 1)
        lane1 = lax.broadcasted_iota(jnp.int32, (8, LANES), 1)
        lower, upper = _head_tri(q, True), _head_tri(q, False)
        bd_mask = _block_diag_mask()
        for gi in range(gps):
            g = gps * pl.program_id(0) + gi
            cols = slice(GROUP_W * gi, GROUP_W * (gi + 1))
            scols = slice(D_STATE * gi, D_STATE * (gi + 1))
            y = y_ref[:, cols]
            z = z_ref[:, cols]
            sz = _sigmoid(z)
            silu_z = z * sz
            yg = y * silu_z
            rstd = lax.rsqrt(jnp.mean(yg * yg, axis=-1, keepdims=True) + EPS)
            n = yg * rstd
            dout = dmix_ref[:, cols]
            gnw_ref[:, cols] += jnp.sum(dout * n, axis=0, keepdims=True)
            dn = dout * nw_ref[:, cols]
            dyg = rstd * (dn - n * jnp.mean(dn * n, axis=-1, keepdims=True))
            dy = dyg * silu_z
            dz_ref[:, cols] = _bf(dyg * y * (sz * (1.0 + z * (1.0 - sz))))

            x = xs_ref[:, cols]
            bmb, cmb = _bf(b_ref[:, scols]), _bf(c_ref[:, scols])
            hv = _group_heads(g, gi, dt_ref[...], acs_ref[...], acst_ref, dsk_ref[...])
            dec = jnp.exp(jnp.where(lower, hv.acs - hv.acs_row, NEG))
            dect = jnp.exp(jnp.where(upper, hv.acs_row - hv.acs, NEG))
            b4 = jnp.concatenate([bmb] * HPG, axis=0)
            c4 = jnp.concatenate([cmb] * HPG, axis=0)
            m_all = _nt(cmb, b4) * dec
            mt_all = _nt(bmb, c4) * dect
            xdt = x * hv.dt
            xdt_b, dyb = _bf(xdt), _bf(dy)
            x_bd, dy_bd = _block_diag(xdt_b, bd_mask), _block_diag(dyb, bd_mask)
            s_prev = st_ref[0, gi]
            spb = _bf(s_prev)
            ds_new = dstate[gi]
            dsb = _bf(ds_new)
            e = jnp.exp(hv.acs)
            elast = jnp.exp(hv.acs_last)
            dte = jnp.exp(hv.acs_last - hv.acs)
            bds = _mm(bmb, dsb)
            dxdt = _mm(_bf(mt_all), dy_bd) + bds * dte
            dm = _nt(dyb, x_bd)
            dmt = _nt(xdt_b, dy_bd)
            dye = _bf(dy * e)
            dc_ref[:, scols] = _mm(_bf(dm * dec), b4) + _nt(dye, spb)
            db_ref[:, scols] = _mm(_bf(dmt * dect), c4) + _nt(_bf(xdt * dte), dsb)
            dstate[gi] = elast * ds_new + _tn(cmb, dye)
            dxs_ref[:, cols] = dxdt * hv.dt + hv.dsk * dy
            ddte_dte = bds * xdt * dte
            dacs_l = dm * m_all - dmt * mt_all + dy * _mm(cmb, spb) * e - ddte_dte
            dlast_l = (jnp.sum(ddte_dte, axis=0, keepdims=True)
                       + jnp.sum(s_prev * ds_new, axis=0, keepdims=True) * elast)
            ddt_l = dxdt * x
            gdsk_l = jnp.sum(dy * x, axis=0, keepdims=True)
            dacs_out = jnp.zeros((q, LANES), F32)
            ddt_out = jnp.zeros((q, LANES), F32)
            gdsk = jnp.zeros((8, LANES), F32)
            for r in range(HPG):
                dacs = _head_sums(dacs_l, r) + jnp.where(last_row, _head_sums(dlast_l, r), 0.0)
                dacs_out = jnp.where(lane == r, dacs, dacs_out)
                ddt_out = jnp.where(lane == r, _head_sums(ddt_l, r), ddt_out)
                gdsk = gdsk + jnp.where(lane1 == r, _head_sums(gdsk_l, r), 0.0)
            dacs_ref[:, LANES * gi:LANES * (gi + 1)] = dacs_out
            ddt_ref[:, LANES * gi:LANES * (gi + 1)] = ddt_out
            gdsk_ref[gi] += gdsk

    rev = lambda c: nc - 1 - c
    wide = pl.BlockSpec((q, gw), lambda g, c: (rev(c), g))
    return _pallas(
        body, name="ssd_bwd", grid=(GROUPS // gps, nc),
        in_specs=[wide, wide, wide, pl.BlockSpec((1, gw), lambda g, c: (0, g)), wide,
                  pl.BlockSpec((q, sw), lambda g, c: (rev(c), D_SSD // sw + g)),
                  pl.BlockSpec((q, sw), lambda g, c: (rev(c), (D_SSD + GROUPS * D_STATE) // sw + g)),
                  pl.BlockSpec((q, LANES), lambda g, c: (rev(c), 0)), pl.BlockSpec((q, LANES), lambda g, c: (rev(c), 0)),
                  pl.BlockSpec((1, gps * GROUPS, q), lambda g, c: (rev(c), g, 0)),
                  pl.BlockSpec((1, gps, D_STATE, GROUP_W), lambda g, c: (rev(c), g, 0, 0)),
                  pl.BlockSpec((1, LANES), lambda g, c: (0, 0))],
        out_specs=[wide, wide,
                   pl.BlockSpec((q, sw), lambda g, c: (rev(c), g)), pl.BlockSpec((q, sw), lambda g, c: (rev(c), g)),
                   pl.BlockSpec((q, gps * LANES), lambda g, c: (rev(c), g)),
                   pl.BlockSpec((q, gps * LANES), lambda g, c: (rev(c), g)),
                   pl.BlockSpec((1, gw), lambda g, c: (0, g)), pl.BlockSpec((gps, 8, LANES), lambda g, c: (g, 0, 0))],
        out_shape=[jax.ShapeDtypeStruct((t, D_SSD), BF16), jax.ShapeDtypeStruct((t, D_SSD), F32),
                   jax.ShapeDtypeStruct((t, GROUPS * D_STATE), F32), jax.ShapeDtypeStruct((t, GROUPS * D_STATE), F32),
                   jax.ShapeDtypeStruct((t, GROUPS * LANES), F32), jax.ShapeDtypeStruct((t, GROUPS * LANES), F32),
                   jax.ShapeDtypeStruct((1, D_SSD), F32), jax.ShapeDtypeStruct((GROUPS, 8, LANES), F32)],
        scratch_shapes=[pltpu.VMEM((gps, D_STATE, GROUP_W), F32)],
        compiler_params=_cp("parallel", "arbitrary"))(dmix, y_ssd, proj, ssd_norm_w, xbc, xbc, xbc, dt, acs, acst,
                                                      states, d_skip_l)


def _dt_bwd(dacs_g, ddt_g, dt, proj, dt_bias_l, a_log_l):
    t = dt.shape[0]
    q = CHUNK
    nc = t // q
    cps = _chunks_per_step(nc)
    rows = cps * q

    def body(dacs_ref, ddt_ref, dt_ref, raw_ref, bias_ref, alog_ref, draw_ref, ga_ref, gb_ref):
        @pl.when(pl.program_id(0) == 0)
        def _():
            ga_ref[...] = jnp.zeros_like(ga_ref)
            gb_ref[...] = jnp.zeros_like(gb_ref)

        lane = lax.broadcasted_iota(jnp.int32, (q, LANES), 1)
        ri = lax.broadcasted_iota(jnp.int32, (q, q), 0)
        ci = lax.broadcasted_iota(jnp.int32, (q, q), 1)
        triu = (ri <= ci).astype(F32)
        a = -jnp.exp(alog_ref[...])
        used = (lane & (GROUPS - 1)) < HPG
        ga = jnp.zeros((1, LANES), F32)
        gb = jnp.zeros((1, LANES), F32)
        for k in range(cps):
            rk = slice(q * k, q * (k + 1))
            dacs = jnp.zeros((q, LANES), F32)
            ddt = jnp.zeros((q, LANES), F32)
            for g in range(GROUPS):
                mask = (lane >= GROUPS * g) & (lane < GROUPS * g + HPG)
                sl = slice(LANES * g, LANES * (g + 1))
                if g == 0:
                    dacs = jnp.where(mask, dacs_ref[rk, sl], dacs)
                    ddt = jnp.where(mask, ddt_ref[rk, sl], ddt)
                else:
                    dacs = jnp.where(mask, pltpu.roll(dacs_ref[rk, sl], GROUPS * g, 1), dacs)
                    ddt = jnp.where(mask, pltpu.roll(ddt_ref[rk, sl], GROUPS * g, 1), ddt)
            dda = jnp.dot(triu, dacs, preferred_element_type=F32, precision=HI)
            row = pl.program_id(0) * rows + q * k + lax.broadcasted_iota(jnp.int32, (q, LANES), 0)
            dsp = jnp.where((row >= PAD_LEAD) & used, dda * a + ddt, 0.0)
            draw = dsp * _sigmoid(raw_ref[rk, :] + bias_ref[...])
            draw_ref[rk, :] = _bf(draw)
            gb = gb + jnp.sum(draw, axis=0, keepdims=True)
            ga = ga + jnp.sum(jnp.where(used, dda * dt_ref[rk, :], 0.0), axis=0, keepdims=True)
        gb_ref[0:1, :] += gb
        ga_ref[0:1, :] += ga * a

    return _pallas(
        body, name="dt_bwd", grid=(nc // cps,),
        in_specs=[pl.BlockSpec((rows, GROUPS * LANES), lambda c: (c, 0)),
                  pl.BlockSpec((rows, GROUPS * LANES), lambda c: (c, 0)),
                  pl.BlockSpec((rows, LANES), lambda c: (c, 0)), pl.BlockSpec((rows, LANES), lambda c: (c, ODT // LANES)),
                  pl.BlockSpec((1, LANES), lambda c: (0, 0)), pl.BlockSpec((1, LANES), lambda c: (0, 0))],
        out_specs=[pl.BlockSpec((rows, LANES), lambda c: (c, 0)), pl.BlockSpec((8, LANES), lambda c: (0, 0)),
                   pl.BlockSpec((8, LANES), lambda c: (0, 0))],
        out_shape=[jax.ShapeDtypeStruct((t, LANES), BF16), jax.ShapeDtypeStruct((8, LANES), F32),
                   jax.ShapeDtypeStruct((8, LANES), F32)],
        compiler_params=_cp("arbitrary"))(dacs_g, ddt_g, dt, proj, dt_bias_l, a_log_l)


def _conv_bwd(dseg, proj, conv_w, conv_b, col_off, name):
    t, width = dseg.shape
    tc = 128
    off_p = (OXS + col_off) // tc
    off_w = col_off // tc

    def body(d_ref, x_ref, w_ref, b_ref, dx_ref, gw_ref, gb_ref, xp, dup):
        xp[0:8, :] = jnp.zeros((8, tc), F32)
        xp[8:t + 8, :] = x_ref[...]
        w = w_ref[...]
        u = (b_ref[...] + w[3:4, :] * xp[8:t + 8, :] + w[2:3, :] * xp[7:t + 7, :]
             + w[1:2, :] * xp[6:t + 6, :] + w[0:1, :] * xp[5:t + 5, :])
        su = _sigmoid(u)
        du = d_ref[...] * (su * (1.0 + u * (1.0 - su)))
        dup[0:t, :] = du
        dup[t:t + 8, :] = jnp.zeros((8, tc), F32)
        dx_ref[...] = _bf(w[3:4, :] * du + w[2:3, :] * dup[1:t + 1, :] + w[1:2, :] * dup[2:t + 2, :]
                          + w[0:1, :] * dup[3:t + 3, :])
        gb_ref[...] = jnp.sum(du, axis=0, keepdims=True)
        gw_ref[...] = jnp.concatenate(
            [jnp.sum(du * xp[5 + k:t + 5 + k, :], axis=0, keepdims=True) for k in range(CONV_WIDTH)], axis=0)

    return _pallas(
        body, name=name, grid=(width // tc,),
        in_specs=[pl.BlockSpec((t, tc), lambda j: (0, j)), pl.BlockSpec((t, tc), lambda j: (0, j + off_p)),
                  pl.BlockSpec((CONV_WIDTH, tc), lambda j: (0, j + off_w)), pl.BlockSpec((1, tc), lambda j: (0, j + off_w))],
        out_specs=[pl.BlockSpec((t, tc), lambda j: (0, j)), pl.BlockSpec((CONV_WIDTH, tc), lambda j: (0, j)),
                   pl.BlockSpec((1, tc), lambda j: (0, j))],
        out_shape=[jax.ShapeDtypeStruct((t, width), BF16), jax.ShapeDtypeStruct((CONV_WIDTH, width), F32),
                   jax.ShapeDtypeStruct((1, width), F32)],
        scratch_shapes=[pltpu.VMEM((t + 8, tc), F32), pltpu.VMEM((t + 8, tc), F32)],
        compiler_params=_cp("parallel"))(dseg, proj, conv_w, conv_b)


def _dinproj(segs, w_re, hpad, norm_w, dy_t, ga):
    t = segs[0].shape[0]
    d = hpad.shape[1]
    tm, tk = _tile(t, 416), SEG_TILE
    counts = [s.shape[1] // tk for s in segs]
    firsts = [sum(counts[:s]) for s in range(len(segs))]
    nk = sum(counts)
    assert nk * tk == w_re.shape[1]
    ni = t // tm
    ns = len(segs)

    def body(*refs):
        seg_refs = refs[:ns]
        w_ref, h_ref, nw_ref, dy_ref, ga_ref, dh_ref, gnw_ref, got_ref, acc, send_sems, recv_sems = refs[ns:]
        i, k = pl.program_id(0), pl.program_id(1)

        @pl.when((i == 0) & (k == 0))
        def _():
            for cp in _exchange_copies(ga_ref, got_ref, send_sems, recv_sems):
                cp.start()
            gnw_ref[...] = jnp.zeros_like(gnw_ref)

        @pl.when(k == 0)
        def _():
            acc[...] = jnp.zeros_like(acc)

        for s in range(ns):
            @pl.when((k >= firsts[s]) & (k < firsts[s] + counts[s]))
            def _(s=s):
                acc[...] += _nt(seg_refs[s][...], w_ref[...])

        @pl.when(k == nk - 1)
        def _():
            h = h_ref[...]
            rstd = lax.rsqrt(jnp.mean(h * h, axis=-1, keepdims=True) + EPS)
            nrm = h * rstd
            dhn = acc[...]
            gnw_ref[...] += jnp.sum(dhn * nrm, axis=0, keepdims=True)
            dn = dhn * nw_ref[...]
            dh_ref[...] = rstd * (dn - nrm * jnp.mean(dn * nrm, axis=-1, keepdims=True)) + dy_ref[...]

        @pl.when((i == ni - 1) & (k == nk - 1))
        def _():
            for cp in _exchange_copies(ga_ref, got_ref, send_sems, recv_sems):
                cp.wait()

    seg_specs = [pl.BlockSpec((tm, tk), functools.partial(lambda i, k, f0, n0: (i, jnp.clip(k - f0, 0, n0 - 1)),
                                                          f0=firsts[s], n0=counts[s])) for s in range(ns)]
    return _pallas(
        body, name="dinproj", grid=(ni, nk),
        in_specs=seg_specs + [pl.BlockSpec((d, tk), lambda i, k: (0, k)),
                              pl.BlockSpec((tm, d), lambda i, k: (i, 0)), pl.BlockSpec((1, d), lambda i, k: (0, 0)),
                              pl.BlockSpec((tm, d), lambda i, k: (i, 0)), ANY],
        out_specs=[pl.BlockSpec((tm, d), lambda i, k: (i, 0)), pl.BlockSpec((1, d), lambda i, k: (0, 0)), ANY],
        out_shape=[jax.ShapeDtypeStruct((t, d), F32), jax.ShapeDtypeStruct((1, d), F32), _exchange_shape(ga)],
        scratch_shapes=[pltpu.VMEM((tm, d), F32)] + _exchange_scratch(),
        compiler_params=_cp("arbitrary", "arbitrary"))(*segs, w_re, hpad, norm_w, dy_t, ga)


def _spread_heads(v):
    v = jnp.pad(v.reshape(GROUPS, HPG), ((0, 0), (0, GROUPS - HPG))).reshape(1, GROUPS * GROUPS)
    return jnp.pad(v, ((0, 0), (0, LANES - GROUPS * GROUPS)))


def _gather_heads(v):
    return v[0:1, :GROUPS * GROUPS].reshape(GROUPS, GROUPS)[:, :HPG].reshape(1, SSD_HEADS)


def _rope_tables(t):
    half = HEAD_DIM // 2
    inv = ROPE_THETA ** (-jnp.arange(half, dtype=F32) / half)
    pos = (jnp.arange(t) - PAD_LEAD).astype(F32)
    ang = pos[:, None] * inv[None, :]
    cos, sin = jnp.cos(ang), jnp.sin(ang)
    cos_t = jnp.concatenate([cos, cos, cos, cos], axis=1)
    sin_t = jnp.concatenate([-sin, sin, -sin, sin], axis=1)
    return cos_t, sin_t


def _column_pieces():
    runs = [(0, OB + 2 * GROUPS * D_STATE, 0)]
    o = OB + 2 * GROUPS * D_STATE
    runs += [(o + HPG * g, HPG, ODT + GROUPS * g) for g in range(GROUPS)]
    o += SSD_HEADS
    for width, dst in ((D_ATT, OQ), (D_KV, OK), (D_KV, OV), (D_ATT, OG)):
        runs.append((o, width, dst))
        o += width
    assert o == D_IN
    pieces = []
    for o0, width, dst in runs:
        for j in range(N_SHARD):
            lo, hi = max(o0, W_IN_SHARD * j), min(o0 + width, W_IN_SHARD * (j + 1))
            if lo < hi:
                pieces.append((j, lo - W_IN_SHARD * j, hi - W_IN_SHARD * j, dst + lo - o0))
    return pieces


def _shards_to_re(w_all):
    _, k, _ = w_all.shape
    tr = 256

    def body(x_ref, o_ref):
        o_ref[:, ODT:ODT + DT_SLAB] = jnp.zeros((tr, DT_SLAB), o_ref.dtype)
        for j, c0, c1, d0 in _column_pieces():
            o_ref[:, d0:d0 + c1 - c0] = x_ref[j, :, c0:c1]

    return _pallas(body, name="shards_to_re", grid=(k // tr,),
                   in_specs=[pl.BlockSpec((N_SHARD, tr, W_IN_SHARD), lambda i: (0, i, 0))],
                   out_specs=pl.BlockSpec((tr, N_RE), lambda i: (i, 0)),
                   out_shape=jax.ShapeDtypeStruct((k, N_RE), w_all.dtype), compiler_params=_cp("parallel"))(w_all)


def _pair_add_to_shards(parts, got, pieces, shard_rows, core, name):
    n = parts[0].shape[1]
    hn = n // 2
    tc = 128
    nt = hn // tc
    ns = len(parts)
    starts = [sum(p.shape[0] for p in parts[:s]) for s in range(ns)]
    moves = []
    for j, c0, c1, d0 in pieces:
        for s, p in enumerate(parts):
            lo, hi = max(d0, starts[s]), min(d0 + c1 - c0, starts[s] + p.shape[0])
            if lo < hi:
                moves.append((s, lo - starts[s], j, c0 + lo - d0, hi - lo))
    assert sum(m[4] for m in moves) == N_SHARD * shard_rows

    def body(core_ref, *refs):
        own, theirs, o_ref, acc = refs[:ns], refs[ns:2 * ns], refs[2 * ns], refs[2 * ns + 1]
        for s, r0, j, c0, rows in moves:
            acc[j, c0:c0 + rows, :] = own[s][r0:r0 + rows, :] + theirs[s][r0:r0 + rows, :]
        o_ref[...] = _bf(acc[...])

    return _pallas(
        body, name=name,
        grid_spec=pltpu.PrefetchScalarGridSpec(
            num_scalar_prefetch=1, grid=(nt,),
            in_specs=[pl.BlockSpec((p.shape[0], tc), lambda i, core_ref: (0, core_ref[0] * nt + i)) for p in parts]
            + [pl.BlockSpec((p.shape[0], tc), lambda i, core_ref: (0, i)) for p in parts],
            out_specs=pl.BlockSpec((N_SHARD, shard_rows, tc), lambda i, core_ref: (0, 0, i)),
            scratch_shapes=[pltpu.VMEM((N_SHARD, shard_rows, tc), F32)]),
        out_shape=jax.ShapeDtypeStruct((N_SHARD, shard_rows, hn), BF16),
        compiler_params=_cp("parallel"))(core, *parts, *got)


def _local_step(x, target, meta, norm_pre_w, w_re, conv_w, conv_b, dt_bias, a_log, d_skip, ssd_norm_w, sinks,
                w_out_shard, norm_post_w, place):
    seq = x.shape[0]
    t = PAD_LEAD + N_META + seq
    hpad = jnp.concatenate([jnp.zeros((PAD_LEAD, D_MODEL), F32), meta, x], axis=0)
    dt_bias_l, a_log_l, d_skip_l = _spread_heads(dt_bias), _spread_heads(a_log), _spread_heads(d_skip)
    cos_t, sin_t = _rope_tables(t)
    sink_v = sinks.reshape(Q_HEADS)

    proj, hn, w_out_all = _inproj(hpad, norm_pre_w, w_re, w_out_shard)
    w_out = w_out_all.reshape(D_MIX, D_MODEL)
    xbc = _conv_fwd(proj, conv_w, conv_b)
    dt, acs, acst = _dt_prep(proj, dt_bias_l, a_log_l)
    y_ssd, ymix, states = _ssd_fwd(xbc, proj, dt, acs, acst, d_skip_l, ssd_norm_w)
    qr, kr = _rope(proj, OQ, proj, OK, cos_t, sin_t)
    amix = _attn_fwd(qr, kr, proj, sink_v)
    out = _outproj(ymix, amix, w_out)
    dout, dy_t, loss_blk, g_norm_post = _post_loss(out, x, target, norm_post_w)

    g_out_y = _tn_matmul(ymix, dout, "gw_out_y")
    g_out_a, got_y = _tn_matmul(amix, dout, "gw_out_a", carry=g_out_y)
    dmix, got_a = _nt_matmul(dout, w_out, "dmix", carry=g_out_a)
    ga_out = _reduce_pair([g_out_y, g_out_a], [got_y, got_a], [(j, 0, W_OUT_SHARD, W_OUT_SHARD * j) for j in range(N_SHARD)],
                          W_OUT_SHARD, place, "gw_out")
    dq_r, dg, dk_r, dv, gs, slabs_out = _attn_bwd(qr, kr, proj, dmix, sink_v, ga_out)
    g_w_out = _reduce_finish(ga_out, slabs_out, place, "gw_out")
    dq, dk = _rope(dq_r, 0, dk_r, 0, cos_t, -sin_t)
    dz, dxs, db, dc, dacs_g, ddt_g, g_ssd_norm, gdsk = _ssd_bwd(dmix, y_ssd, xbc, proj, dt, acs, acst, states,
                                                                d_skip_l, ssd_norm_w)
    draw, ga, gb = _dt_bwd(dacs_g, ddt_g, dt, proj, dt_bias_l, a_log_l)
    dxs_p, gcw0, gcb0 = _conv_bwd(dxs, proj, conv_w, conv_b, 0, "conv_bwd_x")
    db_p, gcw1, gcb1 = _conv_bwd(db, proj, conv_w, conv_b, D_SSD, "conv_bwd_b")
    dc_p, gcw2, gcb2 = _conv_bwd(dc, proj, conv_w, conv_b, D_SSD + GROUPS * D_STATE, "conv_bwd_c")
    tail = jnp.concatenate([dk, _bf(dv), draw, jnp.zeros((t, DT_SLAB - LANES), BF16)], axis=1)
    segs = [dz, dxs_p, db_p, dc_p, dq, dg, tail]
    g_parts, got_parts = [_tn_matmul(segs[0], hn, "gw_in_0")], []
    for s in range(1, len(segs)):
        part, got = _tn_matmul(segs[s], hn, "gw_in_%d" % s, carry=g_parts[-1])
        g_parts.append(part)
        got_parts.append(got)
    ga_in = _reduce_pair(g_parts, got_parts, _column_pieces(), W_IN_SHARD, place, "gw_in")
    dh, g_norm_pre, slabs_in = _dinproj(segs, w_re, hpad, norm_pre_w, dy_t, ga_in)
    g_w_in_half = _chip_sum(ga_in, slabs_in, place, "gw_in_chip_sum")

    gdsk_l = jnp.concatenate([gdsk[g, 0:1, 0:GROUPS] for g in range(GROUPS)], axis=1)
    gdsk_l = jnp.pad(gdsk_l, ((0, 0), (0, LANES - GROUPS * GROUPS)))
    grads = dict(
        meta_tokens=dh[PAD_LEAD:ROW0], norm_pre_w=g_norm_pre, w_in_half=g_w_in_half,
        conv_w=jnp.concatenate([gcw0, gcw1, gcw2], axis=1), conv_b=jnp.concatenate([gcb0, gcb1, gcb2], axis=1),
        dt_bias=_gather_heads(gb), a_log=_gather_heads(ga), d_skip=_gather_heads(gdsk_l), ssd_norm_w=g_ssd_norm,
        attn_sinks=gs[0:1, :Q_HEADS], w_out=g_w_out, norm_post_w=g_norm_post)
    return loss_blk[0, 0], dh[ROW0:], grads


ANY = pl.BlockSpec(memory_space=pl.ANY)
MESH = pl.DeviceIdType.MESH
GATHER_CHUNKS = 4
PAIR_CHUNKS = 8
JOIN_CHUNKS = 8


def _rcopy(src, dst, ssem, rsem, dev):
    return pltpu.make_async_remote_copy(src_ref=src, dst_ref=dst, send_sem=ssem, recv_sem=rsem, device_id=dev,
                                        device_id_type=MESH)


def _place():
    x, y, c = lax.axis_index("x"), lax.axis_index("y"), lax.axis_index("c")
    chips = [(1 - x, y), (x, 1 - y), (1 - x, 1 - y)]
    return x, y, c, chips


def _gather_plan(x_ref, out_ref, send_sems, recv_sems, local_sems, hr, kc):
    ch = hr // kc
    assert ch * kc == hr and ch % 16 == 0
    x, y, c, chips = _place()
    me = 2 * x + y
    sibling = (x, y, 1 - c)

    def piece(chip, hc, k):
        return out_ref.at[chip, pl.ds(hc * hr + k * ch, ch), :]

    def local():
        return [pltpu.make_async_copy(x_ref.at[pl.ds(k * ch, ch), :], out_ref.at[me, pl.ds(k * ch, ch), :],
                                      local_sems.at[k]) for k in range(2 * kc)]

    def first():
        return [_rcopy(x_ref.at[pl.ds(c * hr + k * ch, ch), :], piece(me, c, k), send_sems.at[j * kc + k],
                       recv_sems.at[j * kc + k], (*chip, c)) for j, chip in enumerate(chips) for k in range(kc)]

    def passed(hc):
        return [_rcopy(piece(2 * chip[0] + chip[1], hc, k), piece(2 * chip[0] + chip[1], hc, k),
                       send_sems.at[(3 + j) * kc + k], recv_sems.at[(3 + j) * kc + k], sibling)
                for j, chip in enumerate(chips) for k in range(kc)]

    def arrivals():
        return [_rcopy(piece(2 * chip[0] + chip[1], c, k), piece(2 * chip[0] + chip[1], c, k), send_sems.at[j * kc + k],
                       recv_sems.at[j * kc + k], (*chip, c)) for j, chip in enumerate(chips) for k in range(kc)]

    def start():
        for cp in local() + first():
            cp.start()

    def forward():
        for arrived in arrivals():
            arrived.wait_recv()
        for fw in passed(c):
            fw.start()

    def finish():
        for cp in passed(1 - c):
            cp.wait_recv()
        for cp in first() + passed(c):
            cp.wait_send()
        for cp in local():
            cp.wait()

    return start, forward, finish


def _gather_shards(shard, name, kc, chip, small):
    r, n = shard.shape
    hr = r // 2
    qr = hr // 2
    ch = qr // kc
    assert ch * kc == qr and ch % 16 == 0
    nflow = 12
    tr = 256

    def body(x_ref, p_ref, out_ref, slots_ref, send_sems, recv_sems, *small_sems):
        start_small, wait_small = _small_exchange(p_ref, slots_ref, *small_sems)
        start_small()
        x, y, c, _ = _place()
        me, cxn, cyn, cdg = 2 * x + y, 2 * (1 - x) + y, 2 * x + 1 - y, 2 * (1 - x) + 1 - y
        xn, yn, sibling = (1 - x, y, c), (x, 1 - y, c), (x, y, 1 - c)

        def piece(chip, hc, part, k):
            return out_ref.at[chip, pl.ds(hc * hr + part * qr + k * ch, ch), :]

        def own(part, k):
            return x_ref.at[pl.ds(c * hr + part * qr + k * ch, ch), :]

        def sems(flow, k):
            return send_sems.at[flow * kc + k], recv_sems.at[flow * kc + k]

        def arrival(flow, chip, hc, part, k):
            return _rcopy(piece(chip, hc, part, k), piece(chip, hc, part, k), *sems(flow, k), sibling)

        sends = []
        for flow, part, peer in ((0, 0, xn), (1, 1, yn), (2, 0, yn), (3, 1, xn)):
            sends += [_rcopy(own(part, k), piece(me, c, part, k), *sems(flow, k), peer) for k in range(kc)]
        for cp in sends:
            cp.start()
        landing = ((0, cxn, 0), (1, cyn, 1), (2, cyn, 0), (3, cxn, 1), (4, cdg, 0), (5, cdg, 1))
        for i, (flow, chip, part) in enumerate(landing):
            for k in range(kc):
                arrival(flow, chip, c, part, k).wait_recv()
                if flow < 2:
                    on = _rcopy(piece(chip, c, part, k), piece(chip, c, part, k), *sems(4 + flow, k),
                                yn if flow == 0 else xn)
                    on.start()
                    sends.append(on)
                fw = _rcopy(piece(chip, c, part, k), piece(chip, c, part, k), *sems(6 + i, k), sibling)
                fw.start()
                sends.append(fw)
        for i, (flow, chip, part) in enumerate(landing):
            for k in range(kc):
                arrival(6 + i, chip, 1 - c, part, k).wait_recv()
        for cp in sends:
            cp.wait_send()
        wait_small()

    full = jax.ShapeDtypeStruct((N_SHARD, r, n), shard.dtype)
    others, slots = _pallas(
        body, name=name, in_specs=[ANY, ANY], out_specs=[ANY, ANY],
        out_shape=[full, jax.ShapeDtypeStruct((N_DEV,) + small.shape, F32)],
        scratch_shapes=[pltpu.SemaphoreType.DMA((nflow * kc,)), pltpu.SemaphoreType.DMA((nflow * kc,))]
        + _small_scratch())(shard, small)

    def place(chip_ref, own_ref, all_ref, o_ref):
        o_ref[0] = own_ref[...]

    gathered = _pallas(
        place, name=name + "_own",
        grid_spec=pltpu.PrefetchScalarGridSpec(
            num_scalar_prefetch=1, grid=(r // tr,),
            in_specs=[pl.BlockSpec((tr, n), lambda i, chip_ref: (i, 0)), ANY],
            out_specs=pl.BlockSpec((1, tr, n), lambda i, chip_ref: (chip_ref[0], i, 0))),
        out_shape=full, input_output_aliases={2: 0}, compiler_params=_cp("parallel"))(chip, shard, others)
    return gathered, _sum_slots(slots, name + "_small")


def _pair_copies(src_ref, dst_ref, send_sems, recv_sems):
    hn = src_ref.shape[1] // 2
    cw = hn // PAIR_CHUNKS
    assert cw * PAIR_CHUNKS == hn and cw % LANES == 0
    x, y, c, _ = _place()
    return [_rcopy(src_ref.at[:, pl.ds((1 - c) * hn + k * cw, cw)], dst_ref.at[:, pl.ds(k * cw, cw)],
                   send_sems.at[k], recv_sems.at[k], (x, y, 1 - c)) for k in range(PAIR_CHUNKS)]


def _pair_send(parts, name):
    n = parts[0].shape[1]
    hn = n // 2
    kc = PAIR_CHUNKS
    cw = hn // kc
    assert cw * kc == hn and cw % LANES == 0
    ns = len(parts)

    def body(*refs):
        srcs, dsts, send_sems, recv_sems = refs[:ns], refs[ns:2 * ns], refs[2 * ns], refs[2 * ns + 1]
        x, y, c, _ = _place()
        cps = [_rcopy(srcs[s].at[:, pl.ds((1 - c) * hn + k * cw, cw)], dsts[s].at[:, pl.ds(k * cw, cw)],
                      send_sems.at[s * kc + k], recv_sems.at[s * kc + k], (x, y, 1 - c))
               for s in range(ns) for k in range(kc)]
        for cp in cps:
            cp.start()
        for cp in cps:
            cp.wait()

    return _pallas(
        body, name=name, in_specs=[ANY] * ns, out_specs=[ANY] * ns,
        out_shape=[jax.ShapeDtypeStruct((p.shape[0], hn), F32) for p in parts],
        scratch_shapes=[pltpu.SemaphoreType.DMA((ns * kc,)), pltpu.SemaphoreType.DMA((ns * kc,))])(*parts)


REDUCE_TILE = 256


def _exchange_copies(g_ref, got_ref, send_sems, recv_sems):
    hn = g_ref.shape[2]
    kc = GATHER_CHUNKS
    cw = hn // kc
    assert cw * kc == hn and cw % LANES == 0
    x, y, c, chips = _place()
    return [_rcopy(g_ref.at[2 * chip[0] + chip[1], :, pl.ds(k * cw, cw)], got_ref.at[j, :, pl.ds(k * cw, cw)],
                   send_sems.at[j * kc + k], recv_sems.at[j * kc + k], (*chip, c))
            for j, chip in enumerate(chips) for k in range(kc)]


def _exchange_scratch():
    return [pltpu.SemaphoreType.DMA((3 * GATHER_CHUNKS,)), pltpu.SemaphoreType.DMA((3 * GATHER_CHUNKS,))]


def _exchange_shape(ga):
    return jax.ShapeDtypeStruct((3,) + ga.shape[1:], ga.dtype)


def _chip_sum(ga, got, place, name):
    _, r, hn = ga.shape
    tc = REDUCE_TILE
    nt = hn // tc

    def body(place_ref, own_ref, got_ref, o_ref):
        acc = own_ref[0].astype(F32)
        for j in range(3):
            acc = acc + got_ref[j].astype(F32)
        o_ref[...] = acc

    return _pallas(
        body, name=name,
        grid_spec=pltpu.PrefetchScalarGridSpec(
            num_scalar_prefetch=1, grid=(nt,),
            in_specs=[pl.BlockSpec((1, r, tc), lambda i, place_ref: (place_ref[0], 0, i)),
                      pl.BlockSpec((3, r, tc), lambda i, place_ref: (0, 0, i))],
            out_specs=pl.BlockSpec((r, tc), lambda i, place_ref: (0, place_ref[1] * nt + i))),
        out_shape=jax.ShapeDtypeStruct((r, 2 * hn), F32), compiler_params=_cp("parallel"))(place, ga, got)


def _pair_join(buf, name, small=None):
    r, n = buf.shape
    hn = n // 2
    kc = JOIN_CHUNKS
    cw = hn // kc
    assert cw * kc == hn and cw % LANES == 0

    def body(in_ref, *refs):
        if small is None:
            out_ref, send_sems, recv_sems = refs
        else:
            p_ref, out_ref, slots_ref, send_sems, recv_sems = refs[:5]
            start_small, wait_small = _small_exchange(p_ref, slots_ref, *refs[5:])
            start_small()
        x, y, c, _ = _place()
        cps = [_rcopy(out_ref.at[:, pl.ds(c * hn + k * cw, cw)], out_ref.at[:, pl.ds(c * hn + k * cw, cw)],
                      send_sems.at[k], recv_sems.at[k], (x, y, 1 - c)) for k in range(kc)]
        for cp in cps:
            cp.start()
        for k in range(kc):
            cols = out_ref.at[:, pl.ds((1 - c) * hn + k * cw, cw)]
            _rcopy(cols, cols, send_sems.at[k], recv_sems.at[k], (x, y, 1 - c)).wait_recv()
        for cp in cps:
            cp.wait_send()
        if small is not None:
            wait_small()

    sems = [pltpu.SemaphoreType.DMA((kc,)), pltpu.SemaphoreType.DMA((kc,))]
    if small is None:
        return _pallas(body, name=name, in_specs=[ANY], out_specs=ANY, out_shape=jax.ShapeDtypeStruct((r, n), F32),
                       input_output_aliases={0: 0}, scratch_shapes=sems)(buf)
    return _pallas(
        body, name=name, in_specs=[ANY, ANY], out_specs=[ANY, ANY],
        out_shape=[jax.ShapeDtypeStruct((r, n), F32), jax.ShapeDtypeStruct((N_DEV,) + small.shape, F32)],
        input_output_aliases={0: 0}, scratch_shapes=sems + _small_scratch())(buf, small)


def _reduce_pair(parts, got, pieces, shard_rows, place, tag):
    if len(got) < len(parts):
        got = list(got) + list(_pair_send(parts[len(got):], tag + "_pair_send"))
    return _pair_add_to_shards(parts, got, pieces, shard_rows, place[1:2], tag + "_pair_add")


def _reduce_finish(ga, slabs, place, tag):
    return _pair_join(_chip_sum(ga, slabs, place, tag + "_chip_sum"), tag + "_pair_join")


N_DEV = 8


def _small_exchange(p_ref, slots_ref, send_sems, recv_sems, local_sem):
    x, y, c, _ = _place()
    my = 4 * x + 2 * y + c

    def sends():
        return [_rcopy(p_ref, slots_ref.at[my], send_sems.at[k - 1], recv_sems.at[k - 1],
                       (x ^ ((k >> 2) & 1), y ^ ((k >> 1) & 1), c ^ (k & 1))) for k in range(1, N_DEV)]

    def local():
        return pltpu.make_async_copy(p_ref, slots_ref.at[my], local_sem)

    def start():
        local().start()
        for cp in sends():
            cp.start()

    def wait():
        for k in range(1, N_DEV):
            _rcopy(p_ref, slots_ref.at[my ^ k], send_sems.at[k - 1], recv_sems.at[k - 1], (x, y, c)).wait_recv()
        for cp in sends():
            cp.wait_send()
        local().wait()

    return start, wait


def _small_scratch():
    return [pltpu.SemaphoreType.DMA((N_DEV - 1,)), pltpu.SemaphoreType.DMA((N_DEV - 1,)), pltpu.SemaphoreType.DMA]


def _sum_slots(slots, name):
    _, rows, n = slots.shape

    def body(s_ref, o_ref):
        acc = s_ref[0]
        for j in range(1, N_DEV):
            acc = acc + s_ref[j]
        o_ref[...] = acc

    vm = pl.BlockSpec(memory_space=pltpu.VMEM)
    return _pallas(body, name=name, in_specs=[vm], out_specs=vm, out_shape=jax.ShapeDtypeStruct((rows, n), F32))(slots)


def _adamw(w, g, m, v, name):
    r, n = w.shape
    tr = _tile(r, 256, 8)
    c1 = 1.0 / (1.0 - ADAM_B1 ** ADAM_STEP)
    c2 = 1.0 / (1.0 - ADAM_B2 ** ADAM_STEP)

    def body(w_ref, g_ref, m_ref, v_ref, d_ref, mo_ref, vo_ref):
        gv = g_ref[...]
        mn = ADAM_B1 * m_ref[...] + (1.0 - ADAM_B1) * gv
        vn = ADAM_B2 * v_ref[...] + (1.0 - ADAM_B2) * (gv * gv)
        d_ref[...] = -ADAM_LR * ((mn * c1) / (jnp.sqrt(vn * c2) + ADAM_EPS) + ADAM_WD * w_ref[...])
        mo_ref[...] = mn
        vo_ref[...] = vn

    spec = pl.BlockSpec((tr, n), lambda i: (i, 0))
    shp = jax.ShapeDtypeStruct((r, n), F32)
    return _pallas(body, name=name, grid=(r // tr,), in_specs=[spec] * 4, out_specs=[spec] * 3, out_shape=[shp] * 3,
                   compiler_params=_cp("parallel"))(w, g, m, v)


PACK_W = 1024
SMALL_REPL = ("norm_pre_w", "conv_b", "ssd_norm_w", "norm_post_w")
SMALL_HEAD = ("dt_bias", "a_log", "d_skip", "attn_sinks")


def _rows(a):
    return a.reshape(-1, PACK_W)


def _head_row(vals, extra=None):
    parts = [vals[n].reshape(1, -1) for n in SMALL_HEAD]
    if extra is not None:
        parts.append(extra.reshape(1, 1))
    row = jnp.concatenate(parts, axis=1)
    return jnp.pad(row, ((0, 0), (0, PACK_W - row.shape[1])))


def _pad_rows(a, rows):
    return jnp.pad(a, ((0, rows - a.shape[0]), (0, 0)))


def _pack_repl(vals, extra=None):
    body = jnp.concatenate([_rows(vals[n]) for n in SMALL_REPL] + [_head_row(vals, extra)], axis=0)
    return _pad_rows(body, 16)


def _unpack_repl(buf):
    out, r = {}, 0
    for n, k in zip(SMALL_REPL, (2, 4, 2, 2)):
        out[n] = buf[r:r + k].reshape(1, k * PACK_W)
        r += k
    col = 0
    for n, k in zip(SMALL_HEAD, (32, 32, 32, 16)):
        out[n] = buf[r:r + 1, col:col + k]
        col += k
    return out, buf[r, col]


def kernel(x, meta_tokens, norm_pre_w, w_in, conv_w, conv_b, dt_bias, a_log, d_skip, ssd_norm_w, attn_sinks, w_out, norm_post_w, loss_target, m_meta_tokens, m_norm_pre_w, m_w_in, m_conv_w, m_conv_b, m_dt_bias, m_a_log, m_d_skip, m_ssd_norm_w, m_attn_sinks, m_w_out, m_norm_post_w, v_meta_tokens, v_norm_pre_w, v_w_in, v_conv_w, v_conv_b, v_dt_bias, v_a_log, v_d_skip, v_ssd_norm_w, v_attn_sinks, v_w_out, v_norm_post_w):
    names = ("meta_tokens", "norm_pre_w", "w_in", "conv_w", "conv_b", "dt_bias", "a_log", "d_skip", "ssd_norm_w",
             "attn_sinks", "w_out", "norm_post_w")
    w = dict(zip(names, (meta_tokens, norm_pre_w, w_in, conv_w, conv_b, dt_bias, a_log, d_skip, ssd_norm_w, attn_sinks,
                         w_out, norm_post_w)))
    m = dict(zip(names, (m_meta_tokens, m_norm_pre_w, m_w_in, m_conv_w, m_conv_b, m_dt_bias, m_a_log, m_d_skip,
                         m_ssd_norm_w, m_attn_sinks, m_w_out, m_norm_post_w)))
    v = dict(zip(names, (v_meta_tokens, v_norm_pre_w, v_w_in, v_conv_w, v_conv_b, v_dt_bias, v_a_log, v_d_skip,
                         v_ssd_norm_w, v_attn_sinks, v_w_out, v_norm_post_w)))
    cx, cy, cc = lax.axis_index("x"), lax.axis_index("y"), lax.axis_index("c")
    chip = 2 * cx + cy
    meta_cols = D_MODEL // N_SHARD
    conv_cols = D_CONV // N_SHARD

    place = jnp.stack([chip, cc]).astype(jnp.int32)
    conv_z = lax.dynamic_update_slice(jnp.zeros((CONV_WIDTH, D_CONV), F32), conv_w[0], (0, chip * conv_cols))
    meta_z = lax.dynamic_update_slice(jnp.zeros((N_META, D_MODEL), F32), meta_tokens, (0, chip * meta_cols))
    small = jnp.concatenate([_rows(conv_z), _rows(meta_z)], axis=0)
    w_in_all, small = _gather_shards(_bf(w_in[0]), "gather_w_in", GATHER_CHUNKS, place[0:1],
                                     jnp.where(cc == 0, small, 0.0))
    w_re = _shards_to_re(w_in_all)
    conv_full = small[0:16].reshape(CONV_WIDTH, D_CONV)
    meta_full = small[16:48].reshape(N_META, D_MODEL)

    loss_dev, grad_x, g = _local_step(x[0], loss_target[0], meta_full, norm_pre_w, w_re, conv_full, conv_b, dt_bias,
                                      a_log, d_skip, ssd_norm_w, attn_sinks, _bf(w_out[0]), norm_post_w, place)
    g_w_out = g["w_out"]

    packed = jnp.concatenate([_rows(g["conv_w"]), _rows(g["meta_tokens"]), _pack_repl(g, loss_dev)], axis=0)
    g_w_in, slots = _pair_join(g["w_in_half"], "gw_in_pair_join", small=packed)
    red = _sum_slots(slots, "reduce_small")
    g_conv_full = red[0:16].reshape(CONV_WIDTH, D_CONV)
    g_meta_full = red[16:48].reshape(N_META, D_MODEL)
    g_small, loss = _unpack_repl(red[48:64])
    grads = dict(g_small)
    grads["w_in"] = g_w_in
    grads["w_out"] = g_w_out
    grads["conv_w"] = lax.dynamic_slice(g_conv_full, (0, chip * conv_cols), (CONV_WIDTH, conv_cols))
    grads["meta_tokens"] = lax.dynamic_slice(g_meta_full, (0, chip * meta_cols), (N_META, meta_cols))

    upd = {}
    upd["w_in"] = [jnp.swapaxes(a, 0, 1) for a in _adamw(jnp.swapaxes(w_in[0], 0, 1), g_w_in, jnp.swapaxes(m_w_in[0], 0, 1),
                                                         jnp.swapaxes(v_w_in[0], 0, 1), "adamw_w_in")]
    grads["w_in"] = jnp.swapaxes(g_w_in, 0, 1)
    upd["w_out"] = _adamw(w_out[0], g_w_out, m_w_out[0], v_w_out[0], "adamw_w_out")

    def pack_small(vals, conv, meta):
        return jnp.concatenate([_pad_rows(conv.reshape(CONV_WIDTH, conv_cols), 8), _rows(meta), _pack_repl(vals)], axis=0)

    sm = _adamw(pack_small(w, w["conv_w"], w["meta_tokens"]), pack_small(grads, grads["conv_w"], grads["meta_tokens"]),
                pack_small(m, m["conv_w"], m["meta_tokens"]), pack_small(v, v["conv_w"], v["meta_tokens"]),
                "adamw_small")
    for n in names:
        if n not in ("w_in", "w_out"):
            upd[n] = [None, None, None]
    for k, buf in enumerate(sm):
        upd["conv_w"][k] = buf[0:CONV_WIDTH]
        upd["meta_tokens"][k] = buf[8:16].reshape(N_META, meta_cols)
        rest, _ = _unpack_repl(buf[16:32])
        for n in SMALL_REPL + SMALL_HEAD:
            upd[n][k] = rest[n]

    def shaped(n, a):
        return a.reshape(w[n].shape)

    outs = [loss, grad_x[None]]
    outs += [shaped(n, grads[n]) for n in names]
    for k in range(3):
        outs += [shaped(n, upd[n][k]) for n in names]
    return tuple(outs)
```

```python
import functools

import jax
import jax.numpy as jnp
from jax import lax
from jax.experimental import pallas as pl
from jax.experimental.pallas import tpu as pltpu

F32 = jnp.float32
BF16 = jnp.bfloat16

D_MODEL = 2048
CHUNK = 64
N_META = 16
PAD_LEAD = CHUNK - N_META
ROW0 = PAD_LEAD + N_META
EPS = 1e-6
SSD_HEADS = 32
HEAD_DIM = 64
GROUPS = 8
HPG = SSD_HEADS // GROUPS
D_STATE = 128
D_SSD = 2048
GROUP_W = D_SSD // GROUPS
CONV_WIDTH = 4
D_CONV = 4096
Q_HEADS = 16
KV_HEADS = 4
REP = Q_HEADS // KV_HEADS
D_ATT = 1024
D_KV = 256
BAND_CHUNKS = 3
ROPE_THETA = 10000.0
D_MIX = D_SSD + D_ATT
D_IN = 8736
N_SHARD = 4
W_IN_SHARD = D_IN // N_SHARD
W_OUT_SHARD = D_MIX // N_SHARD

OZ, OXS, OB, OC, OQ, OG, OK, OV, ODT = 0, 2048, 4096, 5120, 6144, 7168, 8192, 8448, 8704
DT_SLAB = 512
N_RE = ODT + DT_SLAB
LANES = 128

ADAM_LR, ADAM_B1, ADAM_B2, ADAM_EPS, ADAM_WD, ADAM_STEP = 0.001, 0.9, 0.999, 1e-08, 0.01, 10

SSD_FWD_GROUPS_PER_STEP = 4
SSD_BWD_GROUPS_PER_STEP = 8
SEG_TILE = 1024
VMEM_LIMIT = 52 * 1024 * 1024
NEG = -1e30
HI = lax.Precision.HIGHEST


def _pallas(body, **kw):
    return pl.pallas_call(body, **kw)


def _cp(*sem):
    return pltpu.CompilerParams(dimension_semantics=sem, vmem_limit_bytes=VMEM_LIMIT)


def _tile(n, cap, mult=16):
    best = None
    for d in range(mult, min(n, cap) + 1, mult):
        if n % d == 0:
            best = d
    assert best is not None, (n, cap)
    return best


def _nt(a, b):
    return lax.dot_general(a, b, (((1,), (1,)), ((), ())), preferred_element_type=F32)


def _tn(a, b):
    return lax.dot_general(a, b, (((0,), (0,)), ((), ())), preferred_element_type=F32)


def _mm(a, b):
    return jnp.dot(a, b, preferred_element_type=F32)


def _sigmoid(x):
    return 1.0 / (1.0 + jnp.exp(-x))


def _bf(x):
    return x.astype(BF16)


def _inproj(hpad, norm_w, w_re, w_out_shard):
    t, d = hpad.shape
    n = w_re.shape[1]
    tm, tn = _tile(t, 1040), 1024
    ni, nj = t // tm, n // tn
    r_out, n_out = w_out_shard.shape
    kc = GATHER_CHUNKS

    def body(h_ref, nw_ref, w_ref, ws_ref, proj_ref, hn_ref, wall_ref, hn_s, send_sems, recv_sems, local_sems):
        i, j = pl.program_id(0), pl.program_id(1)
        start, forward, finish = _gather_plan(ws_ref, wall_ref, send_sems, recv_sems, local_sems, r_out // 2, kc)
        pl.when((i == 0) & (j == 0))(start)
        pl.when((i == ni // 2) & (j == 0))(forward)

        @pl.when(j == 0)
        def _():
            h = h_ref[...]
            ms = jnp.mean(h * h, axis=-1, keepdims=True)
            hn = _bf(h * lax.rsqrt(ms + EPS) * nw_ref[...])
            hn_s[...] = hn
            hn_ref[...] = hn
        proj_ref[...] = _mm(hn_s[...], w_ref[...])
        pl.when((i == ni - 1) & (j == nj - 1))(finish)

    return _pallas(
        body, name="inproj", grid=(ni, nj),
        in_specs=[pl.BlockSpec((tm, d), lambda i, j: (i, 0)), pl.BlockSpec((1, d), lambda i, j: (0, 0)),
                  pl.BlockSpec((d, tn), lambda i, j: (0, j)), ANY],
        out_specs=[pl.BlockSpec((tm, tn), lambda i, j: (i, j)), pl.BlockSpec((tm, d), lambda i, j: (i, 0)), ANY],
        out_shape=[jax.ShapeDtypeStruct((t, n), F32), jax.ShapeDtypeStruct((t, d), BF16),
                   jax.ShapeDtypeStruct((N_SHARD, r_out, n_out), w_out_shard.dtype)],
        scratch_shapes=[pltpu.VMEM((tm, d), BF16), pltpu.SemaphoreType.DMA((6 * kc,)), pltpu.SemaphoreType.DMA((6 * kc,)),
                        pltpu.SemaphoreType.DMA((2 * kc,))],
        compiler_params=_cp("arbitrary", "arbitrary"))(hpad, norm_w, w_re, w_out_shard)


def _conv_fwd(proj, conv_w, conv_b):
    t = proj.shape[0]
    tc = 256
    off = OXS // tc

    def body(x_ref, w_ref, b_ref, o_ref):
        x = x_ref[...]
        w = w_ref[...]
        row = lax.broadcasted_iota(jnp.int32, (t, tc), 0)
        u = b_ref[...] + w[3:4, :] * x
        for k in range(1, CONV_WIDTH):
            u = u + w[3 - k:4 - k, :] * jnp.where(row >= k, pltpu.roll(x, k, 0), 0.0)
        h = 0.5 * u
        o_ref[...] = h + h * jnp.tanh(h)

    return _pallas(
        body, name="conv_fwd", grid=(D_CONV // tc,),
        in_specs=[pl.BlockSpec((t, tc), lambda j: (0, j + off)), pl.BlockSpec((CONV_WIDTH, tc), lambda j: (0, j)),
                  pl.BlockSpec((1, tc), lambda j: (0, j))],
        out_specs=pl.BlockSpec((t, tc), lambda j: (0, j)),
        out_shape=jax.ShapeDtypeStruct((t, D_CONV), F32),
        compiler_params=_cp("parallel"))(proj, conv_w, conv_b)


def _softplus(u):
    e = jnp.exp(-jnp.abs(u))
    w = 1.0 + e
    l1p = jnp.where(w == 1.0, e, jnp.log(w) * (e / jnp.where(w == 1.0, 1.0, w - 1.0)))
    return jnp.maximum(u, 0.0) + l1p


def _chunks_per_step(nc):
    return max(d for d in range(1, 14) if nc % d == 0)


def _dt_prep(proj, dt_bias_l, a_log_l):
    t = proj.shape[0]
    nc = t // CHUNK
    q = CHUNK
    cps = _chunks_per_step(nc)
    rows = cps * q

    def body(raw_ref, bias_ref, alog_ref, dt_ref, acs_ref, acst_ref):
        ri = lax.broadcasted_iota(jnp.int32, (q, q), 0)
        ci = lax.broadcasted_iota(jnp.int32, (q, q), 1)
        tri = (ri >= ci).astype(F32)
        neg_a = -jnp.exp(alog_ref[...])
        for k in range(cps):
            rk = slice(q * k, q * (k + 1))
            sp = _softplus(raw_ref[rk, :] + bias_ref[...])
            row = pl.program_id(0) * rows + q * k + lax.broadcasted_iota(jnp.int32, (q, LANES), 0)
            dt = jnp.where(row >= PAD_LEAD, sp, 0.0)
            acs = jnp.dot(tri, dt * neg_a, preferred_element_type=F32, precision=HI)
            dt_ref[rk, :] = dt
            acs_ref[rk, :] = acs
            acst_ref[k] = acs.T

    return _pallas(
        body, name="dt_prep", grid=(nc // cps,),
        in_specs=[pl.BlockSpec((rows, LANES), lambda c: (c, ODT // LANES)), pl.BlockSpec((1, LANES), lambda c: (0, 0)),
                  pl.BlockSpec((1, LANES), lambda c: (0, 0))],
        out_specs=[pl.BlockSpec((rows, LANES), lambda c: (c, 0)), pl.BlockSpec((rows, LANES), lambda c: (c, 0)),
                   pl.BlockSpec((cps, LANES, q), lambda c: (c, 0, 0))],
        out_shape=[jax.ShapeDtypeStruct((t, LANES), F32), jax.ShapeDtypeStruct((t, LANES), F32),
                   jax.ShapeDtypeStruct((nc, LANES, q), F32)],
        compiler_params=_cp("parallel"))(proj, dt_bias_l, a_log_l)


def _head_cols(blk, idx):
    lane = lax.broadcasted_iota(jnp.int32, blk.shape, 1)
    return jnp.sum(jnp.where(lane == idx, blk, 0.0), axis=1, keepdims=True)


class _HeadVals:
    pass


def _lane_head(shape):
    return lax.broadcasted_iota(jnp.int32, shape, len(shape) - 1) >> 6


def _group_heads(g, gi, dtb, acsb, acst_ref, dskb):
    q = dtb.shape[0]
    hv = _HeadVals()
    lh = _lane_head((1, GROUP_W))
    hv.dt = jnp.zeros((q, GROUP_W), F32)
    hv.acs = jnp.zeros((q, GROUP_W), F32)
    hv.acs_last = jnp.zeros((1, GROUP_W), F32)
    hv.dsk = jnp.zeros((1, GROUP_W), F32)
    rows = []
    for r in range(HPG):
        idx = GROUPS * g + r
        sel = lh == r
        acs_r = acst_ref[0, GROUPS * gi + r:GROUPS * gi + r + 1, :]
        rows.append(acs_r)
        hv.dt = jnp.where(sel, _head_cols(dtb, idx), hv.dt)
        hv.acs = jnp.where(sel, _head_cols(acsb, idx), hv.acs)
        hv.acs_last = jnp.where(sel, acs_r[:, q - 1:q], hv.acs_last)
        hv.dsk = jnp.where(sel, _head_cols(dskb, idx), hv.dsk)
    hv.acs_row = jnp.concatenate(rows, axis=1)
    return hv


def _head_tri(q, lower):
    ri = lax.broadcasted_iota(jnp.int32, (q, GROUP_W), 0)
    li = lax.broadcasted_iota(jnp.int32, (q, GROUP_W), 1) & (HEAD_DIM - 1)
    return ri >= li if lower else ri <= li


def _block_diag_mask():
    rb = lax.broadcasted_iota(jnp.int32, (GROUP_W, GROUP_W), 0) >> 6
    cb = lax.broadcasted_iota(jnp.int32, (GROUP_W, GROUP_W), 1) >> 6
    return rb == cb


def _block_diag(v, mask):
    return jnp.where(mask, jnp.concatenate([v] * HPG, axis=0), jnp.zeros((), v.dtype))


def _head_sums(v, r):
    return jnp.sum(jnp.where(_lane_head((1, GROUP_W)) == r, v, 0.0), axis=1, keepdims=True)


def _ssd_fwd(xbc, proj, dt, acs, acst, d_skip_l, ssd_norm_w):
    t = xbc.shape[0]
    q = CHUNK
    nc = t // q

    gps = SSD_FWD_GROUPS_PER_STEP
    gw, sw = gps * GROUP_W, gps * D_STATE

    def body(xs_ref, b_ref, c_ref, dt_ref, acs_ref, acst_ref, z_ref, dsk_ref, nw_ref,
             y_ref, ymix_ref, st_ref, state):
        @pl.when(pl.program_id(1) == 0)
        def _():
            state[...] = jnp.zeros_like(state)

        lower = _head_tri(q, True)
        bd_mask = _block_diag_mask()
        for gi in range(gps):
            g = gps * pl.program_id(0) + gi
            cols = slice(GROUP_W * gi, GROUP_W * (gi + 1))
            x = xs_ref[:, cols]
            bmb = _bf(b_ref[:, D_STATE * gi:D_STATE * (gi + 1)])
            cmb = _bf(c_ref[:, D_STATE * gi:D_STATE * (gi + 1)])
            hv = _group_heads(g, gi, dt_ref[...], acs_ref[...], acst_ref, dsk_ref[...])
            decay = jnp.exp(jnp.where(lower, hv.acs - hv.acs_row, NEG))
            m_all = _bf(_nt(cmb, jnp.concatenate([bmb] * HPG, axis=0)) * decay)
            xdt = x * hv.dt
            s_prev = state[gi]
            st_ref[0, gi] = s_prev
            y = (_mm(m_all, _block_diag(_bf(xdt), bd_mask)) + _mm(cmb, _bf(s_prev)) * jnp.exp(hv.acs) + hv.dsk * x)
            state[gi] = jnp.exp(hv.acs_last) * s_prev + _tn(bmb, _bf(xdt * jnp.exp(hv.acs_last - hv.acs)))
            y_ref[:, cols] = y
            z = z_ref[:, cols]
            yg = y * (z * _sigmoid(z))
            ms = jnp.mean(yg * yg, axis=-1, keepdims=True)
            ymix_ref[:, cols] = _bf(yg * lax.rsqrt(ms + EPS) * nw_ref[:, cols])

    return _pallas(
        body, name="ssd_fwd", grid=(GROUPS // gps, nc),
        in_specs=[pl.BlockSpec((q, gw), lambda g, c: (c, g)),
                  pl.BlockSpec((q, sw), lambda g, c: (c, D_SSD // sw + g)),
                  pl.BlockSpec((q, sw), lambda g, c: (c, (D_SSD + GROUPS * D_STATE) // sw + g)),
                  pl.BlockSpec((q, LANES), lambda g, c: (c, 0)), pl.BlockSpec((q, LANES), lambda g, c: (c, 0)),
                  pl.BlockSpec((1, gps * GROUPS, q), lambda g, c: (c, g, 0)),
                  pl.BlockSpec((q, gw), lambda g, c: (c, g)),
                  pl.BlockSpec((1, LANES), lambda g, c: (0, 0)), pl.BlockSpec((1, gw), lambda g, c: (0, g))],
        out_specs=[pl.BlockSpec((q, gw), lambda g, c: (c, g)), pl.BlockSpec((q, gw), lambda g, c: (c, g)),
                   pl.BlockSpec((1, gps, D_STATE, GROUP_W), lambda g, c: (c, g, 0, 0))],
        out_shape=[jax.ShapeDtypeStruct((t, D_SSD), F32), jax.ShapeDtypeStruct((t, D_SSD), BF16),
                   jax.ShapeDtypeStruct((nc, GROUPS, D_STATE, GROUP_W), F32)],
        scratch_shapes=[pltpu.VMEM((gps, D_STATE, GROUP_W), F32)],
        compiler_params=_cp("parallel", "arbitrary"))(xbc, xbc, xbc, dt, acs, acst, proj, d_skip_l, ssd_norm_w)


def _swap_halves(v):
    lane = lax.broadcasted_iota(jnp.int32, v.shape, 1)
    return jnp.where((lane & (HEAD_DIM - 1)) < HEAD_DIM // 2, pltpu.roll(v, LANES - HEAD_DIM // 2, 1),
                     pltpu.roll(v, HEAD_DIM // 2, 1))


def _rope(qsrc, q_off, ksrc, k_off, cos_t, sin_t):
    t = qsrc.shape[0]
    tr = _tile(t, 832)
    q_scale = HEAD_DIM ** -0.5

    def body(q_ref, k_ref, cos_ref, sin_ref, qo_ref, ko_ref):
        cs = cos_ref[...]
        sn = sin_ref[...]
        for src, dst, width, scale in ((q_ref, qo_ref, D_ATT, q_scale), (k_ref, ko_ref, D_KV, 1.0)):
            for s in range(width // LANES):
                v = src[:, LANES * s:LANES * (s + 1)].astype(F32)
                dst[:, LANES * s:LANES * (s + 1)] = _bf((v * cs + _swap_halves(v) * sn) * scale)

    return _pallas(
        body, name="rope", grid=(t // tr,),
        in_specs=[pl.BlockSpec((tr, D_ATT), lambda i: (i, q_off // D_ATT)),
                  pl.BlockSpec((tr, D_KV), lambda i: (i, k_off // D_KV)),
                  pl.BlockSpec((tr, LANES), lambda i: (i, 0)), pl.BlockSpec((tr, LANES), lambda i: (i, 0))],
        out_specs=[pl.BlockSpec((tr, D_ATT), lambda i: (i, 0)), pl.BlockSpec((tr, D_KV), lambda i: (i, 0))],
        out_shape=[jax.ShapeDtypeStruct((t, D_ATT), BF16), jax.ShapeDtypeStruct((t, D_KV), BF16)],
        compiler_params=_cp("parallel"))(qsrc, ksrc, cos_t, sin_t)


def _attn_chunks_per_step(nc):
    return max(d for d in range(1, 6) if nc % d == 0)


def _band(ref, c):
    return [ref[pl.ds(pl.multiple_of(jnp.maximum(c - j, 0) * CHUNK, CHUNK), CHUNK), :] for j in (2, 1, 0)]


def _attn_probs(qh, kb, sink_col, valid):
    s = jnp.where(valid, _nt(qh, kb), NEG)
    m = jnp.maximum(jnp.max(s, axis=1, keepdims=True), sink_col)
    p = jnp.exp(s - m)
    psink = jnp.exp(sink_col - m)
    return p, psink, 1.0 / (jnp.sum(p, axis=1, keepdims=True) + psink)


def _attn_operands(c, q, k_refs, v_refs, sink_ref, h):
    qh = jnp.concatenate([q[:, HEAD_DIM * (REP * h + r):HEAD_DIM * (REP * h + r + 1)] for r in range(REP)], axis=0)
    kb = jnp.concatenate([k[:, HEAD_DIM * h:HEAD_DIM * (h + 1)] for k in k_refs], axis=0)
    vb = jnp.concatenate([_bf(v[:, HEAD_DIM * h:HEAD_DIM * (h + 1)]) for v in v_refs], axis=0)
    rows = lax.broadcasted_iota(jnp.int32, (REP * CHUNK, 1), 0) >> 6
    sink_col = jnp.zeros((REP * CHUNK, 1), F32)
    for r in range(REP):
        sink_col = jnp.where(rows == r, sink_ref[REP * h + r], sink_col)
    key_abs = (c - (BAND_CHUNKS - 1)) * CHUNK + lax.broadcasted_iota(jnp.int32, (1, BAND_CHUNKS * CHUNK), 1)
    return qh, kb, vb, sink_col, key_abs >= PAD_LEAD


def _attn_fwd(qr, kr, proj, sinks):
    t = qr.shape[0]
    nc = t // CHUNK
    cps = _attn_chunks_per_step(nc)
    rows = cps * CHUNK

    def body(q_ref, k_ref, v_ref, g_ref, sink_ref, o_ref):
        for j in range(cps):
            c = pl.program_id(0) * cps + j
            rj = slice(CHUNK * j, CHUNK * (j + 1))
            ks, vs = _band(k_ref, c), _band(v_ref, c)
            q = q_ref[rj, :]
            outs = []
            for h in range(KV_HEADS):
                qh, kb, vb, sink_col, valid = _attn_operands(c, q, ks, vs, sink_ref, h)
                p, _, inv = _attn_probs(qh, kb, sink_col, valid)
                o = _mm(_bf(p), vb) * inv
                outs += [o[CHUNK * r:CHUNK * (r + 1)] for r in range(REP)]
            att = jnp.concatenate(outs, axis=1)
            gate = g_ref[rj, :]
            o_ref[rj, :] = _bf(att * (gate * _sigmoid(gate)))

    return _pallas(
        body, name="attn_fwd", grid=(nc // cps,),
        in_specs=[pl.BlockSpec((rows, D_ATT), lambda i: (i, 0)), pl.BlockSpec((t, D_KV), lambda i: (0, 0)),
                  pl.BlockSpec((t, D_KV), lambda i: (0, OV // D_KV)),
                  pl.BlockSpec((rows, D_ATT), lambda i: (i, OG // D_ATT)), pl.BlockSpec(memory_space=pltpu.SMEM)],
        out_specs=pl.BlockSpec((rows, D_ATT), lambda i: (i, 0)),
        out_shape=jax.ShapeDtypeStruct((t, D_ATT), BF16),
        compiler_params=_cp("parallel"))(qr, kr, proj, proj, sinks)


def _outproj(ymix, amix, w_out):
    t = ymix.shape[0]
    tm, tn = _tile(t, 832), 1024

    def body(y_ref, a_ref, wy_ref, wa_ref, o_ref):
        o_ref[...] = _mm(y_ref[...], wy_ref[...]) + _mm(a_ref[...], wa_ref[...])

    return _pallas(
        body, name="outproj", grid=(t // tm, D_MODEL // tn),
        in_specs=[pl.BlockSpec((tm, D_SSD), lambda i, j: (i, 0)), pl.BlockSpec((tm, D_ATT), lambda i, j: (i, 0)),
                  pl.BlockSpec((D_SSD, tn), lambda i, j: (0, j)),
                  pl.BlockSpec((D_ATT, tn), lambda i, j: (D_SSD // D_ATT, j))],
        out_specs=pl.BlockSpec((tm, tn), lambda i, j: (i, j)),
        out_shape=jax.ShapeDtypeStruct((t, D_MODEL), F32),
        compiler_params=_cp("parallel", "parallel"))(ymix, amix, w_out, w_out)


def _post_loss(out, x, target, norm_post_w):
    t = out.shape[0]
    nc = t // CHUNK
    cps = _attn_chunks_per_step(nc)
    rows = cps * CHUNK

    def body(o_ref, *refs):
        x_refs, tg_refs = refs[:cps], refs[cps:2 * cps]
        nw_ref, dout_ref, dy_ref, loss_ref, gnw_ref = refs[2 * cps:]
        i = pl.program_id(0)

        @pl.when(i == 0)
        def _():
            loss_ref[...] = jnp.zeros_like(loss_ref)
            gnw_ref[...] = jnp.zeros_like(gnw_ref)

        nw = nw_ref[...]
        loss = jnp.zeros((), F32)
        gnw = jnp.zeros((1, D_MODEL), F32)
        for k in range(cps):
            rk = slice(CHUNK * k, CHUNK * (k + 1))
            frames = i * cps + k > 0
            o = o_ref[rk, :]
            rstd = lax.rsqrt(jnp.mean(o * o, axis=-1, keepdims=True) + EPS)
            n = o * rstd
            err = jnp.where(frames, x_refs[k][...] + n * nw - tg_refs[k][...], 0.0)
            loss = loss + jnp.sum(err * err)
            dy = err * (1.0 / D_MODEL)
            dy_ref[rk, :] = dy
            gnw = gnw + jnp.sum(dy * n, axis=0, keepdims=True)
            dn = dy * nw
            dout_ref[rk, :] = _bf(rstd * (dn - n * jnp.mean(dn * n, axis=-1, keepdims=True)))
        loss_ref[...] += loss * (0.5 / D_MODEL)
        gnw_ref[...] += gnw

    lower = [pl.BlockSpec((CHUNK, D_MODEL), functools.partial(lambda i, k: (jnp.maximum(i * cps + k - 1, 0), 0), k=k))
             for k in range(cps)]
    return _pallas(
        body, name="post_loss", grid=(nc // cps,),
        in_specs=[pl.BlockSpec((rows, D_MODEL), lambda i: (i, 0))] + lower + lower
        + [pl.BlockSpec((1, D_MODEL), lambda i: (0, 0))],
        out_specs=[pl.BlockSpec((rows, D_MODEL), lambda i: (i, 0)), pl.BlockSpec((rows, D_MODEL), lambda i: (i, 0)),
                   pl.BlockSpec((8, LANES), lambda i: (0, 0)), pl.BlockSpec((1, D_MODEL), lambda i: (0, 0))],
        out_shape=[jax.ShapeDtypeStruct((t, D_MODEL), BF16), jax.ShapeDtypeStruct((t, D_MODEL), F32),
                   jax.ShapeDtypeStruct((8, LANES), F32), jax.ShapeDtypeStruct((1, D_MODEL), F32)],
        compiler_params=_cp("arbitrary"))(out, *([x] * cps), *([target] * cps), norm_post_w)


def _carried(grid, carry):
    if carry is None:
        return [], [], [], [], lambda refs: None, lambda refs: None
    hn = carry.shape[1] // 2

    def at(ids, which):
        cond = None
        for d, size in enumerate(grid):
            here = pl.program_id(d) == (0 if which == "first" else size - 1)
            cond = here if cond is None else cond & here
        return cond

    def start(refs):
        @pl.when(at(grid, "first"))
        def _():
            for cp in _pair_copies(*refs):
                cp.start()

    def finish(refs):
        @pl.when(at(grid, "last"))
        def _():
            for cp in _pair_copies(*refs):
                cp.wait()

    return ([ANY], [ANY], [jax.ShapeDtypeStruct((carry.shape[0], hn), F32)],
            [pltpu.SemaphoreType.DMA((PAIR_CHUNKS,)), pltpu.SemaphoreType.DMA((PAIR_CHUNKS,))], start, finish)


def _nt_matmul(a, b, name, carry=None):
    t, k = a.shape
    n = b.shape[0]
    tm, tn = _tile(t, 832), 1024
    grid = (t // tm, n // tn)
    cin, cout, cshape, cscratch, start, finish = _carried(grid, carry)

    def body(a_ref, b_ref, *refs):
        o_ref = refs[len(cin)]
        comm = (refs[0], refs[2], refs[3], refs[4]) if carry is not None else None
        start(comm)
        o_ref[...] = _nt(a_ref[...], b_ref[...])
        finish(comm)

    res = _pallas(
        body, name=name, grid=grid,
        in_specs=[pl.BlockSpec((tm, k), lambda i, j: (i, 0)), pl.BlockSpec((tn, k), lambda i, j: (j, 0))] + cin,
        out_specs=[pl.BlockSpec((tm, tn), lambda i, j: (i, j))] + cout,
        out_shape=[jax.ShapeDtypeStruct((t, n), F32)] + cshape, scratch_shapes=cscratch,
        compiler_params=_cp("arbitrary", "arbitrary"))(a, b, *([carry] if carry is not None else []))
    return res if carry is not None else res[0]


def _tn_matmul(a, b, name, carry=None):
    t, m = a.shape
    n = b.shape[1]
    tk, tm, tn = _tile(t, 832), min(m, 2048), min(n, 2048)
    nk = t // tk
    grid = (m // tm, n // tn, nk)
    cin, cout, cshape, cscratch, start, finish = _carried(grid, carry)

    def body(a_ref, b_ref, *refs):
        o_ref = refs[len(cin)]
        comm = (refs[0], refs[2], refs[3], refs[4]) if carry is not None else None
        start(comm)

        @pl.when(pl.program_id(2) == 0)
        def _():
            o_ref[...] = jnp.zeros_like(o_ref)
        o_ref[...] += _tn(a_ref[...], b_ref[...])
        finish(comm)

    res = _pallas(
        body, name=name, grid=grid,
        in_specs=[pl.BlockSpec((tk, tm), lambda i, j, k: (k, i)), pl.BlockSpec((tk, tn), lambda i, j, k: (k, j))] + cin,
        out_specs=[pl.BlockSpec((tm, tn), lambda i, j, k: (i, j))] + cout,
        out_shape=[jax.ShapeDtypeStruct((m, n), F32)] + cshape, scratch_shapes=cscratch,
        compiler_params=_cp("arbitrary", "arbitrary", "arbitrary"))(a, b, *([carry] if carry is not None else []))
    return res if carry is not None else res[0]


def _attn_bwd(qr, kr, proj, dmix, sinks, ga):
    t = qr.shape[0]
    nc = t // CHUNK
    cps = _attn_chunks_per_step(nc)
    nsteps = nc // cps
    rows_step = cps * CHUNK

    def body(q_ref, k_ref, v_ref, g_ref, da_ref, sink_ref, ga_ref, dq_ref, dg_ref, dk_ref, dv_ref, gs_ref,
             got_ref, send_sems, recv_sems):
        step = pl.program_id(0)

        @pl.when(step == 0)
        def _():
            for cp in _exchange_copies(ga_ref, got_ref, send_sems, recv_sems):
                cp.start()
            dk_ref[...] = jnp.zeros_like(dk_ref)
            dv_ref[...] = jnp.zeros_like(dv_ref)
            gs_ref[...] = jnp.zeros_like(gs_ref)

        lane = lax.broadcasted_iota(jnp.int32, (1, LANES), 1)
        rows = lax.broadcasted_iota(jnp.int32, (REP * CHUNK, 1), 0) >> 6
        gs = jnp.zeros((1, LANES), F32)
        dk_parts = [[] for _ in range(cps + BAND_CHUNKS - 1)]
        dv_parts = [[] for _ in range(cps + BAND_CHUNKS - 1)]
        for j in range(cps):
            c = step * cps + j
            rj = slice(CHUNK * j, CHUNK * (j + 1))
            ks, vs = _band(k_ref, c), _band(v_ref, c)
            q = q_ref[rj, :]
            gate = g_ref[rj, :]
            sg = _sigmoid(gate)
            da = da_ref[rj, :]
            datt = da * (gate * sg)
            dqs, atts, dks, dvs = [], [], [], []
            for h in range(KV_HEADS):
                qh, kb, vb, sink_col, valid = _attn_operands(c, q, ks, vs, sink_ref, h)
                p, psink, inv = _attn_probs(qh, kb, sink_col, valid)
                pb = _bf(p)
                o = _mm(pb, vb) * inv
                do = jnp.concatenate([datt[:, HEAD_DIM * (REP * h + r):HEAD_DIM * (REP * h + r + 1)]
                                      for r in range(REP)], axis=0)
                dob = _bf(do * inv)
                delta = jnp.sum(do * o, axis=1, keepdims=True) * inv
                ds = _bf(p * (_nt(dob, vb) - delta))
                gsink = -psink * delta
                for r in range(REP):
                    gs = gs + jnp.where(lane == REP * h + r, jnp.sum(jnp.where(rows == r, gsink, 0.0)), 0.0)
                dqh = _mm(ds, kb)
                dqs += [dqh[CHUNK * r:CHUNK * (r + 1)] for r in range(REP)]
                atts += [o[CHUNK * r:CHUNK * (r + 1)] for r in range(REP)]
                dks.append(_tn(ds, qh))
                dvs.append(_tn(pb, dob))
            dq_ref[rj, :] = jnp.concatenate(dqs, axis=1)
            att = jnp.concatenate(atts, axis=1)
            dg_ref[rj, :] = _bf(da * att * (sg * (1.0 + gate * (1.0 - sg))))
            dkf = jnp.concatenate(dks, axis=1)
            dvf = jnp.concatenate(dvs, axis=1)
            for b in range(BAND_CHUNKS):
                dk_parts[j + b].append(dkf[CHUNK * b:CHUNK * (b + 1)])
                dv_parts[j + b].append(dvf[CHUNK * b:CHUNK * (b + 1)])
        gs_ref[0:1, :] += gs
        for rel in range(cps + BAND_CHUNKS - 1):
            r0 = pl.multiple_of(jnp.maximum(step * cps - (BAND_CHUNKS - 1) + rel, 0) * CHUNK, CHUNK)
            dk_ref[pl.ds(r0, CHUNK), :] += sum(dk_parts[rel][1:], dk_parts[rel][0])
            dv_ref[pl.ds(r0, CHUNK), :] += sum(dv_parts[rel][1:], dv_parts[rel][0])

        @pl.when(step == nsteps - 1)
        def _():
            for cp in _exchange_copies(ga_ref, got_ref, send_sems, recv_sems):
                cp.wait()

    return _pallas(
        body, name="attn_bwd", grid=(nsteps,),
        in_specs=[pl.BlockSpec((rows_step, D_ATT), lambda i: (i, 0)), pl.BlockSpec((t, D_KV), lambda i: (0, 0)),
                  pl.BlockSpec((t, D_KV), lambda i: (0, OV // D_KV)),
                  pl.BlockSpec((rows_step, D_ATT), lambda i: (i, OG // D_ATT)),
                  pl.BlockSpec((rows_step, D_ATT), lambda i: (i, D_SSD // D_ATT)),
                  pl.BlockSpec(memory_space=pltpu.SMEM), ANY],
        out_specs=[pl.BlockSpec((rows_step, D_ATT), lambda i: (i, 0)), pl.BlockSpec((rows_step, D_ATT), lambda i: (i, 0)),
                   pl.BlockSpec((t, D_KV), lambda i: (0, 0)), pl.BlockSpec((t, D_KV), lambda i: (0, 0)),
                   pl.BlockSpec((8, LANES), lambda i: (0, 0)), ANY],
        out_shape=[jax.ShapeDtypeStruct((t, D_ATT), F32), jax.ShapeDtypeStruct((t, D_ATT), BF16),
                   jax.ShapeDtypeStruct((t, D_KV), F32), jax.ShapeDtypeStruct((t, D_KV), F32),
                   jax.ShapeDtypeStruct((8, LANES), F32), _exchange_shape(ga)],
        scratch_shapes=_exchange_scratch(),
        compiler_params=_cp("arbitrary"))(qr, kr, proj, proj, dmix, sinks, ga)


def _ssd_bwd(dmix, y_ssd, xbc, proj, dt, acs, acst, states, d_skip_l, ssd_norm_w):
    t = xbc.shape[0]
    q = CHUNK
    nc = t // q
    gps = SSD_BWD_GROUPS_PER_STEP
    gw, sw = gps * GROUP_W, gps * D_STATE

    def body(dmix_ref, y_ref, z_ref, nw_ref, xs_ref, b_ref, c_ref, dt_ref, acs_ref, acst_ref, st_ref, dsk_ref,
             dz_ref, dxs_ref, db_ref, dc_ref, dacs_ref, ddt_ref, gnw_ref, gdsk_ref, dstate):
        @pl.when(pl.program_id(1) == 0)
        def _():
            dstate[...] = jnp.zeros_like(dstate)
            gnw_ref[...] = jnp.zeros_like(gnw_ref)
            gdsk_ref[...] = jnp.zeros_like(gdsk_ref)

        last_row = lax.broadcasted_iota(jnp.int32, (q, 1), 0) == q - 1
        lane = lax.broadcasted_iota(jnp.int32, (q, LANES), 1)
        lane1 = lax.broadcasted_iota(jnp.int32, (8, LANES), 1)
        lower, upper = _head_tri(q, True), _head_tri(q, False)
        bd_mask = _block_diag_mask()
        for gi in range(gps):
            g = gps * pl.program_id(0) + gi
            cols = slice(GROUP_W * gi, GROUP_W * (gi + 1))
            scols = slice(D_STATE * gi, D_STATE * (gi + 1))
            y = y_ref[:, cols]
            z = z_ref[:, cols]
            sz = _sigmoid(z)
            silu_z = z * sz
            yg = y * silu_z
            rstd = lax.rsqrt(jnp.mean(yg * yg, axis=-1, keepdims=True) + EPS)
            n = yg * rstd
            dout = dmix_ref[:, cols]
            gnw_ref[:, cols] += jnp.sum(dout * n, axis=0, keepdims=True)
            dn = dout * nw_ref[:, cols]
            dyg = rstd * (dn - n * jnp.mean(dn * n, axis=-1, keepdims=True))
            dy = dyg * silu_z
            dz_ref[:, cols] = _bf(dyg * y * (sz * (1.0 + z * (1.0 - sz))))

            x = xs_ref[:, cols]
            bmb, cmb = _bf(b_ref[:, scols]), _bf(c_ref[:, scols])
            hv = _group_heads(g, gi, dt_ref[...], acs_ref[...], acst_ref, dsk_ref[...])
            dec = jnp.exp(jnp.where(lower, hv.acs - hv.acs_row, NEG))
            dect = jnp.exp(jnp.where(upper, hv.acs_row - hv.acs, NEG))
            b4 = jnp.concatenate([bmb] * HPG, axis=0)
            c4 = jnp.concatenate([cmb] * HPG, axis=0)
            m_all = _nt(cmb, b4) * dec
            mt_all = _nt(bmb, c4) * dect
            xdt = x * hv.dt
            xdt_b, dyb = _bf(xdt), _bf(dy)
            x_bd, dy_bd = _block_diag(xdt_b, bd_mask), _block_diag(dyb, bd_mask)
            s_prev = st_ref[0, gi]
            spb = _bf(s_prev)
            ds_new = dstate[gi]
            dsb = _bf(ds_new)
            e = jnp.exp(hv.acs)
            elast = jnp.exp(hv.acs_last)
            dte = jnp.exp(hv.acs_last - hv.acs)
            bds = _mm(bmb, dsb)
            dxdt = _mm(_bf(mt_all), dy_bd) + bds * dte
            dm = _nt(dyb, x_bd)
            dmt = _nt(xdt_b, dy_bd)
            dye = _bf(dy * e)
            dc_ref[:, scols] = _mm(_bf(dm * dec), b4) + _nt(dye, spb)
            db_ref[:, scols] = _mm(_bf(dmt * dect), c4) + _nt(_bf(xdt * dte), dsb)
            dstate[gi] = elast * ds_new + _tn(cmb, dye)
            dxs_ref[:, cols] = dxdt * hv.dt + hv.dsk * dy
            ddte_dte = bds * xdt * dte
            dacs_l = dm * m_all - dmt * mt_all + dy * _mm(cmb, spb) * e - ddte_dte
            dlast_l = (jnp.sum(ddte_dte, axis=0, keepdims=True)
                       + jnp.sum(s_prev * ds_new, axis=0, keepdims=True) * elast)
            ddt_l = dxdt * x
            gdsk_l = jnp.sum(dy * x, axis=0, keepdims=True)
            dacs_out = jnp.zeros((q, LANES), F32)
            ddt_out = jnp.zeros((q, LANES), F32)
            gdsk = jnp.zeros((8, LANES), F32)
            for r in range(HPG):
                dacs = _head_sums(dacs_l, r) + jnp.where(last_row, _head_sums(dlast_l, r), 0.0)
                dacs_out = jnp.where(lane == r, dacs, dacs_out)
                ddt_out = jnp.where(lane == r, _head_sums(ddt_l, r), ddt_out)
                gdsk = gdsk + jnp.where(lane1 == r, _head_sums(gdsk_l, r), 0.0)
            dacs_ref[:, LANES * gi:LANES * (gi + 1)] = dacs_out
            ddt_ref[:, LANES * gi:LANES * (gi + 1)] = ddt_out
            gdsk_ref[gi] += gdsk

    rev = lambda c: nc - 1 - c
    wide = pl.BlockSpec((q, gw), lambda g, c: (rev(c), g))
    return _pallas(
        body, name="ssd_bwd", grid=(GROUPS // gps, nc),
        in_specs=[wide, wide, wide, pl.BlockSpec((1, gw), lambda g, c: (0, g)), wide,
                  pl.BlockSpec((q, sw), lambda g, c: (rev(c), D_SSD // sw + g)),
                  pl.BlockSpec((q, sw), lambda g, c: (rev(c), (D_SSD + GROUPS * D_STATE) // sw + g)),
                  pl.BlockSpec((q, LANES), lambda g, c: (rev(c), 0)), pl.BlockSpec((q, LANES), lambda g, c: (rev(c), 0)),
                  pl.BlockSpec((1, gps * GROUPS, q), lambda g, c: (rev(c), g, 0)),
                  pl.BlockSpec((1, gps, D_STATE, GROUP_W), lambda g, c: (rev(c), g, 0, 0)),
                  pl.BlockSpec((1, LANES), lambda g, c: (0, 0))],
        out_specs=[wide, wide,
                   pl.BlockSpec((q, sw), lambda g, c: (rev(c), g)), pl.BlockSpec((q, sw), lambda g, c: (rev(c), g)),
                   pl.BlockSpec((q, gps * LANES), lambda g, c: (rev(c), g)),
                   pl.BlockSpec((q, gps * LANES), lambda g, c: (rev(c), g)),
                   pl.BlockSpec((1, gw), lambda g, c: (0, g)), pl.BlockSpec((gps, 8, LANES), lambda g, c: (g, 0, 0))],
        out_shape=[jax.ShapeDtypeStruct((t, D_SSD), BF16), jax.ShapeDtypeStruct((t, D_SSD), F32),
                   jax.ShapeDtypeStruct((t, GROUPS * D_STATE), F32), jax.ShapeDtypeStruct((t, GROUPS * D_STATE), F32),
                   jax.ShapeDtypeStruct((t, GROUPS * LANES), F32), jax.ShapeDtypeStruct((t, GROUPS * LANES), F32),
                   jax.ShapeDtypeStruct((1, D_SSD), F32), jax.ShapeDtypeStruct((GROUPS, 8, LANES), F32)],
        scratch_shapes=[pltpu.VMEM((gps, D_STATE, GROUP_W), F32)],
        compiler_params=_cp("parallel", "arbitrary"))(dmix, y_ssd, proj, ssd_norm_w, xbc, xbc, xbc, dt, acs, acst,
                                                      states, d_skip_l)


def _dt_bwd(dacs_g, ddt_g, dt, proj, dt_bias_l, a_log_l):
    t = dt.shape[0]
    q = CHUNK
    nc = t // q
    cps = _chunks_per_step(nc)
    rows = cps * q

    def body(dacs_ref, ddt_ref, dt_ref, raw_ref, bias_ref, alog_ref, draw_ref, ga_ref, gb_ref):
        @pl.when(pl.program_id(0) == 0)
        def _():
            ga_ref[...] = jnp.zeros_like(ga_ref)
            gb_ref[...] = jnp.zeros_like(gb_ref)

        lane = lax.broadcasted_iota(jnp.int32, (q, LANES), 1)
        ri = lax.broadcasted_iota(jnp.int32, (q, q), 0)
        ci = lax.broadcasted_iota(jnp.int32, (q, q), 1)
        triu = (ri <= ci).astype(F32)
        a = -jnp.exp(alog_ref[...])
        used = (lane & (GROUPS - 1)) < HPG
        ga = jnp.zeros((1, LANES), F32)
        gb = jnp.zeros((1, LANES), F32)
        for k in range(cps):
            rk = slice(q * k, q * (k + 1))
            dacs = jnp.zeros((q, LANES), F32)
            ddt = jnp.zeros((q, LANES), F32)
            for g in range(GROUPS):
                mask = (lane >= GROUPS * g) & (lane < GROUPS * g + HPG)
                sl = slice(LANES * g, LANES * (g + 1))
                if g == 0:
                    dacs = jnp.where(mask, dacs_ref[rk, sl], dacs)
                    ddt = jnp.where(mask, ddt_ref[rk, sl], ddt)
                else:
                    dacs = jnp.where(mask, pltpu.roll(dacs_ref[rk, sl], GROUPS * g, 1), dacs)
                    ddt = jnp.where(mask, pltpu.roll(ddt_ref[rk, sl], GROUPS * g, 1), ddt)
            dda = jnp.dot(triu, dacs, preferred_element_type=F32, precision=HI)
            row = pl.program_id(0) * rows + q * k + lax.broadcasted_iota(jnp.int32, (q, LANES), 0)
            dsp = jnp.where((row >= PAD_LEAD) & used, dda * a + ddt, 0.0)
            draw = dsp * _sigmoid(raw_ref[rk, :] + bias_ref[...])
            draw_ref[rk, :] = _bf(draw)
            gb = gb + jnp.sum(draw, axis=0, keepdims=True)
            ga = ga + jnp.sum(jnp.where(used, dda * dt_ref[rk, :], 0.0), axis=0, keepdims=True)
        gb_ref[0:1, :] += gb
        ga_ref[0:1, :] += ga * a

    return _pallas(
        body, name="dt_bwd", grid=(nc // cps,),
        in_specs=[pl.BlockSpec((rows, GROUPS * LANES), lambda c: (c, 0)),
                  pl.BlockSpec((rows, GROUPS * LANES), lambda c: (c, 0)),
                  pl.BlockSpec((rows, LANES), lambda c: (c, 0)), pl.BlockSpec((rows, LANES), lambda c: (c, ODT // LANES)),
                  pl.BlockSpec((1, LANES), lambda c: (0, 0)), pl.BlockSpec((1, LANES), lambda c: (0, 0))],
        out_specs=[pl.BlockSpec((rows, LANES), lambda c: (c, 0)), pl.BlockSpec((8, LANES), lambda c: (0, 0)),
                   pl.BlockSpec((8, LANES), lambda c: (0, 0))],
        out_shape=[jax.ShapeDtypeStruct((t, LANES), BF16), jax.ShapeDtypeStruct((8, LANES), F32),
                   jax.ShapeDtypeStruct((8, LANES), F32)],
        compiler_params=_cp("arbitrary"))(dacs_g, ddt_g, dt, proj, dt_bias_l, a_log_l)


def _conv_bwd(dseg, proj, conv_w, conv_b, col_off, name):
    t, width = dseg.shape
    tc = 128
    rt = _tile(t, 320)
    off_p = (OXS + col_off) // tc
    off_w = col_off // tc

    def body(d_ref, x_ref, w_ref, b_ref, dx_ref, gw_ref, gb_ref, xp, dup):
        xp[0:8, :] = jnp.zeros((8, tc), F32)
        xp[8:t + 8, :] = x_ref[...]
        dup[t:t + 8, :] = jnp.zeros((8, tc), F32)
        w = w_ref[...]
        bias = b_ref[...]

        def first(i, acc):
            r0 = pl.multiple_of(i * rt, 8)
            xs = [xp[pl.ds(r0 + 5 + k, rt), :] for k in range(CONV_WIDTH)]
            u = bias + w[3:4, :] * xs[3] + w[2:3, :] * xs[2] + w[1:2, :] * xs[1] + w[0:1, :] * xs[0]
            su = 0.5 + 0.5 * jnp.tanh(0.5 * u)
            du = d_ref[pl.ds(r0, rt), :] * (su * (1.0 + u * (1.0 - su)))
            dup[pl.ds(r0, rt), :] = du
            return tuple(acc[k] + jnp.sum(du * xs[k], axis=0, keepdims=True) for k in range(CONV_WIDTH)) + (
                acc[CONV_WIDTH] + jnp.sum(du, axis=0, keepdims=True),)

        zero = jnp.zeros((1, tc), F32)
        acc = lax.fori_loop(0, t // rt, first, (zero,) * (CONV_WIDTH + 1))
        gw_ref[...] = jnp.concatenate(acc[:CONV_WIDTH], axis=0)
        gb_ref[...] = acc[CONV_WIDTH]

        def second(i, carry):
            r0 = pl.multiple_of(i * rt, 16)
            dx_ref[pl.ds(r0, rt), :] = _bf(w[3:4, :] * dup[pl.ds(r0, rt), :] + w[2:3, :] * dup[pl.ds(r0 + 1, rt), :]
                                          + w[1:2, :] * dup[pl.ds(r0 + 2, rt), :] + w[0:1, :] * dup[pl.ds(r0 + 3, rt), :])
            return carry

        lax.fori_loop(0, t // rt, second, 0)

    return _pallas(
        body, name=name, grid=(width // tc,),
        in_specs=[pl.BlockSpec((t, tc), lambda j: (0, j)), pl.BlockSpec((t, tc), lambda j: (0, j + off_p)),
                  pl.BlockSpec((CONV_WIDTH, tc), lambda j: (0, j + off_w)), pl.BlockSpec((1, tc), lambda j: (0, j + off_w))],
        out_specs=[pl.BlockSpec((t, tc), lambda j: (0, j)), pl.BlockSpec((CONV_WIDTH, tc), lambda j: (0, j)),
                   pl.BlockSpec((1, tc), lambda j: (0, j))],
        out_shape=[jax.ShapeDtypeStruct((t, width), BF16), jax.ShapeDtypeStruct((CONV_WIDTH, width), F32),
                   jax.ShapeDtypeStruct((1, width), F32)],
        scratch_shapes=[pltpu.VMEM((t + 8, tc), F32), pltpu.VMEM((t + 8, tc), F32)],
        compiler_params=_cp("parallel"))(dseg, proj, conv_w, conv_b)


def _dinproj(segs, w_re, hpad, norm_w, dy_t, ga):
    t = segs[0].shape[0]
    d = hpad.shape[1]
    tm, tk = _tile(t, 416), SEG_TILE
    counts = [s.shape[1] // tk for s in segs]
    firsts = [sum(counts[:s]) for s in range(len(segs))]
    nk = sum(counts)
    assert nk * tk == w_re.shape[1]
    ni = t // tm
    ns = len(segs)

    def body(*refs):
        seg_refs = refs[:ns]
        w_ref, h_ref, nw_ref, dy_ref, ga_ref, dh_ref, gnw_ref, got_ref, acc, send_sems, recv_sems = refs[ns:]
        i, k = pl.program_id(0), pl.program_id(1)

        @pl.when((i == 0) & (k == 0))
        def _():
            for cp in _exchange_copies(ga_ref, got_ref, send_sems, recv_sems):
                cp.start()
            gnw_ref[...] = jnp.zeros_like(gnw_ref)

        @pl.when(k == 0)
        def _():
            acc[...] = jnp.zeros_like(acc)

        for s in range(ns):
            @pl.when((k >= firsts[s]) & (k < firsts[s] + counts[s]))
            def _(s=s):
                acc[...] += _nt(seg_refs[s][...], w_ref[...])

        @pl.when(k == nk - 1)
        def _():
            h = h_ref[...]
            rstd = lax.rsqrt(jnp.mean(h * h, axis=-1, keepdims=True) + EPS)
            nrm = h * rstd
            dhn = acc[...]
            gnw_ref[...] += jnp.sum(dhn * nrm, axis=0, keepdims=True)
            dn = dhn * nw_ref[...]
            dh_ref[...] = rstd * (dn - nrm * jnp.mean(dn * nrm, axis=-1, keepdims=True)) + dy_ref[...]

        @pl.when((i == ni - 1) & (k == nk - 1))
        def _():
            for cp in _exchange_copies(ga_ref, got_ref, send_sems, recv_sems):
                cp.wait()

    seg_specs = [pl.BlockSpec((tm, tk), functools.partial(lambda i, k, f0, n0: (i, jnp.clip(k - f0, 0, n0 - 1)),
                                                          f0=firsts[s], n0=counts[s])) for s in range(ns)]
    return _pallas(
        body, name="dinproj", grid=(ni, nk),
        in_specs=seg_specs + [pl.BlockSpec((d, tk), lambda i, k: (0, k)),
                              pl.BlockSpec((tm, d), lambda i, k: (i, 0)), pl.BlockSpec((1, d), lambda i, k: (0, 0)),
                              pl.BlockSpec((tm, d), lambda i, k: (i, 0)), ANY],
        out_specs=[pl.BlockSpec((tm, d), lambda i, k: (i, 0)), pl.BlockSpec((1, d), lambda i, k: (0, 0)), ANY],
        out_shape=[jax.ShapeDtypeStruct((t, d), F32), jax.ShapeDtypeStruct((1, d), F32), _exchange_shape(ga)],
        scratch_shapes=[pltpu.VMEM((tm, d), F32)] + _exchange_scratch(),
        compiler_params=_cp("arbitrary", "arbitrary"))(*segs, w_re, hpad, norm_w, dy_t, ga)


def _spread_heads(v):
    v = jnp.pad(v.reshape(GROUPS, HPG), ((0, 0), (0, GROUPS - HPG))).reshape(1, GROUPS * GROUPS)
    return jnp.pad(v, ((0, 0), (0, LANES - GROUPS * GROUPS)))


def _gather_heads(v):
    return v[0:1, :GROUPS * GROUPS].reshape(GROUPS, GROUPS)[:, :HPG].reshape(1, SSD_HEADS)


def _rope_tables(t):
    half = HEAD_DIM // 2
    inv = ROPE_THETA ** (-jnp.arange(half, dtype=F32) / half)
    pos = (jnp.arange(t) - PAD_LEAD).astype(F32)
    ang = pos[:, None] * inv[None, :]
    cos, sin = jnp.cos(ang), jnp.sin(ang)
    cos_t = jnp.concatenate([cos, cos, cos, cos], axis=1)
    sin_t = jnp.concatenate([-sin, sin, -sin, sin], axis=1)
    return cos_t, sin_t


def _column_pieces():
    runs = [(0, OB + 2 * GROUPS * D_STATE, 0)]
    o = OB + 2 * GROUPS * D_STATE
    runs += [(o + HPG * g, HPG, ODT + GROUPS * g) for g in range(GROUPS)]
    o += SSD_HEADS
    for width, dst in ((D_ATT, OQ), (D_KV, OK), (D_KV, OV), (D_ATT, OG)):
        runs.append((o, width, dst))
        o += width
    assert o == D_IN
    pieces = []
    for o0, width, dst in runs:
        for j in range(N_SHARD):
            lo, hi = max(o0, W_IN_SHARD * j), min(o0 + width, W_IN_SHARD * (j + 1))
            if lo < hi:
                pieces.append((j, lo - W_IN_SHARD * j, hi - W_IN_SHARD * j, dst + lo - o0))
    return pieces


def _shards_to_re(w_all):
    _, k, _ = w_all.shape
    tr = 256

    def body(x_ref, o_ref):
        o_ref[:, ODT:ODT + DT_SLAB] = jnp.zeros((tr, DT_SLAB), o_ref.dtype)
        for j, c0, c1, d0 in _column_pieces():
            o_ref[:, d0:d0 + c1 - c0] = x_ref[j, :, c0:c1]

    return _pallas(body, name="shards_to_re", grid=(k // tr,),
                   in_specs=[pl.BlockSpec((N_SHARD, tr, W_IN_SHARD), lambda i: (0, i, 0))],
                   out_specs=pl.BlockSpec((tr, N_RE), lambda i: (i, 0)),
                   out_shape=jax.ShapeDtypeStruct((k, N_RE), w_all.dtype), compiler_params=_cp("parallel"))(w_all)


def _pair_add_to_shards(parts, got, pieces, shard_rows, core, name):
    n = parts[0].shape[1]
    hn = n // 2
    tc = 128
    nt = hn // tc
    ns = len(parts)
    starts = [sum(p.shape[0] for p in parts[:s]) for s in range(ns)]
    moves = []
    for j, c0, c1, d0 in pieces:
        for s, p in enumerate(parts):
            lo, hi = max(d0, starts[s]), min(d0 + c1 - c0, starts[s] + p.shape[0])
            if lo < hi:
                moves.append((s, lo - starts[s], j, c0 + lo - d0, hi - lo))
    assert sum(m[4] for m in moves) == N_SHARD * shard_rows

    def body(core_ref, *refs):
        own, theirs, o_ref, acc = refs[:ns], refs[ns:2 * ns], refs[2 * ns], refs[2 * ns + 1]
        for s, r0, j, c0, rows in moves:
            acc[j, c0:c0 + rows, :] = own[s][r0:r0 + rows, :] + theirs[s][r0:r0 + rows, :]
        o_ref[...] = _bf(acc[...])

    return _pallas(
        body, name=name,
        grid_spec=pltpu.PrefetchScalarGridSpec(
            num_scalar_prefetch=1, grid=(nt,),
            in_specs=[pl.BlockSpec((p.shape[0], tc), lambda i, core_ref: (0, core_ref[0] * nt + i)) for p in parts]
            + [pl.BlockSpec((p.shape[0], tc), lambda i, core_ref: (0, i)) for p in parts],
            out_specs=pl.BlockSpec((N_SHARD, shard_rows, tc), lambda i, core_ref: (0, 0, i)),
            scratch_shapes=[pltpu.VMEM((N_SHARD, shard_rows, tc), F32)]),
        out_shape=jax.ShapeDtypeStruct((N_SHARD, shard_rows, hn), BF16),
        compiler_params=_cp("parallel"))(core, *parts, *got)


def _local_step(x, target, meta, norm_pre_w, w_re, conv_w, conv_b, dt_bias, a_log, d_skip, ssd_norm_w, sinks,
                w_out_shard, norm_post_w, place):
    seq = x.shape[0]
    t = PAD_LEAD + N_META + seq
    hpad = jnp.concatenate([jnp.zeros((PAD_LEAD, D_MODEL), F32), meta, x], axis=0)
    dt_bias_l, a_log_l, d_skip_l = _spread_heads(dt_bias), _spread_heads(a_log), _spread_heads(d_skip)
    cos_t, sin_t = _rope_tables(t)
    sink_v = sinks.reshape(Q_HEADS)

    proj, hn, w_out_all = _inproj(hpad, norm_pre_w, w_re, w_out_shard)
    w_out = w_out_all.reshape(D_MIX, D_MODEL)
    xbc = _conv_fwd(proj, conv_w, conv_b)
    dt, acs, acst = _dt_prep(proj, dt_bias_l, a_log_l)
    y_ssd, ymix, states = _ssd_fwd(xbc, proj, dt, acs, acst, d_skip_l, ssd_norm_w)
    qr, kr = _rope(proj, OQ, proj, OK, cos_t, sin_t)
    amix = _attn_fwd(qr, kr, proj, sink_v)
    out = _outproj(ymix, amix, w_out)
    dout, dy_t, loss_blk, g_norm_post = _post_loss(out, x, target, norm_post_w)

    g_out_y = _tn_matmul(ymix, dout, "gw_out_y")
    g_out_a, got_y = _tn_matmul(amix, dout, "gw_out_a", carry=g_out_y)
    dmix, got_a = _nt_matmul(dout, w_out, "dmix", carry=g_out_a)
    ga_out = _reduce_pair([g_out_y, g_out_a], [got_y, got_a], [(j, 0, W_OUT_SHARD, W_OUT_SHARD * j) for j in range(N_SHARD)],
                          W_OUT_SHARD, place, "gw_out")
    dq_r, dg, dk_r, dv, gs, slabs_out = _attn_bwd(qr, kr, proj, dmix, sink_v, ga_out)
    g_w_out = _reduce_finish(ga_out, slabs_out, place, "gw_out")
    dq, dk = _rope(dq_r, 0, dk_r, 0, cos_t, -sin_t)
    dz, dxs, db, dc, dacs_g, ddt_g, g_ssd_norm, gdsk = _ssd_bwd(dmix, y_ssd, xbc, proj, dt, acs, acst, states,
                                                                d_skip_l, ssd_norm_w)
    draw, ga, gb = _dt_bwd(dacs_g, ddt_g, dt, proj, dt_bias_l, a_log_l)
    dxs_p, gcw0, gcb0 = _conv_bwd(dxs, proj, conv_w, conv_b, 0, "conv_bwd_x")
    db_p, gcw1, gcb1 = _conv_bwd(db, proj, conv_w, conv_b, D_SSD, "conv_bwd_b")
    dc_p, gcw2, gcb2 = _conv_bwd(dc, proj, conv_w, conv_b, D_SSD + GROUPS * D_STATE, "conv_bwd_c")
    tail = jnp.concatenate([dk, _bf(dv), draw, jnp.zeros((t, DT_SLAB - LANES), BF16)], axis=1)
    segs = [dz, dxs_p, db_p, dc_p, dq, dg, tail]
    g_parts, got_parts = [_tn_matmul(segs[0], hn, "gw_in_0")], []
    for s in range(1, len(segs)):
        part, got = _tn_matmul(segs[s], hn, "gw_in_%d" % s, carry=g_parts[-1])
        g_parts.append(part)
        got_parts.append(got)
    ga_in = _reduce_pair(g_parts, got_parts, _column_pieces(), W_IN_SHARD, place, "gw_in")
    dh, g_norm_pre, slabs_in = _dinproj(segs, w_re, hpad, norm_pre_w, dy_t, ga_in)
    g_w_in_half = _chip_sum(ga_in, slabs_in, place, "gw_in_chip_sum")

    gdsk_l = jnp.concatenate([gdsk[g, 0:1, 0:GROUPS] for g in range(GROUPS)], axis=1)
    gdsk_l = jnp.pad(gdsk_l, ((0, 0), (0, LANES - GROUPS * GROUPS)))
    grads = dict(
        meta_tokens=dh[PAD_LEAD:ROW0], norm_pre_w=g_norm_pre, w_in_half=g_w_in_half,
        conv_w=jnp.concatenate([gcw0, gcw1, gcw2], axis=1), conv_b=jnp.concatenate([gcb0, gcb1, gcb2], axis=1),
        dt_bias=_gather_heads(gb), a_log=_gather_heads(ga), d_skip=_gather_heads(gdsk_l), ssd_norm_w=g_ssd_norm,
        attn_sinks=gs[0:1, :Q_HEADS], w_out=g_w_out, norm_post_w=g_norm_post)
    return loss_blk[0, 0], dh[ROW0:], grads


ANY = pl.BlockSpec(memory_space=pl.ANY)
MESH = pl.DeviceIdType.MESH
GATHER_CHUNKS = 4
PAIR_CHUNKS = 8
JOIN_CHUNKS = 8


def _rcopy(src, dst, ssem, rsem, dev):
    return pltpu.make_async_remote_copy(src_ref=src, dst_ref=dst, send_sem=ssem, recv_sem=rsem, device_id=dev,
                                        device_id_type=MESH)


def _place():
    x, y, c = lax.axis_index("x"), lax.axis_index("y"), lax.axis_index("c")
    chips = [(1 - x, y), (x, 1 - y), (1 - x, 1 - y)]
    return x, y, c, chips


def _gather_plan(x_ref, out_ref, send_sems, recv_sems, local_sems, hr, kc):
    ch = hr // kc
    assert ch * kc == hr and ch % 16 == 0
    x, y, c, chips = _place()
    me = 2 * x + y
    sibling = (x, y, 1 - c)

    def piece(chip, hc, k):
        return out_ref.at[chip, pl.ds(hc * hr + k * ch, ch), :]

    def local():
        return [pltpu.make_async_copy(x_ref.at[pl.ds(k * ch, ch), :], out_ref.at[me, pl.ds(k * ch, ch), :],
                                      local_sems.at[k]) for k in range(2 * kc)]

    def first():
        return [_rcopy(x_ref.at[pl.ds(c * hr + k * ch, ch), :], piece(me, c, k), send_sems.at[j * kc + k],
                       recv_sems.at[j * kc + k], (*chip, c)) for j, chip in enumerate(chips) for k in range(kc)]

    def passed(hc):
        return [_rcopy(piece(2 * chip[0] + chip[1], hc, k), piece(2 * chip[0] + chip[1], hc, k),
                       send_sems.at[(3 + j) * kc + k], recv_sems.at[(3 + j) * kc + k], sibling)
                for j, chip in enumerate(chips) for k in range(kc)]

    def arrivals():
        return [_rcopy(piece(2 * chip[0] + chip[1], c, k), piece(2 * chip[0] + chip[1], c, k), send_sems.at[j * kc + k],
                       recv_sems.at[j * kc + k], (*chip, c)) for j, chip in enumerate(chips) for k in range(kc)]

    def start():
        for cp in local() + first():
            cp.start()

    def forward():
        for arrived in arrivals():
            arrived.wait_recv()
        for fw in passed(c):
            fw.start()

    def finish():
        for cp in passed(1 - c):
            cp.wait_recv()
        for cp in first() + passed(c):
            cp.wait_send()
        for cp in local():
            cp.wait()

    return start, forward, finish


def _gather_shards(shard, name, kc, chip, small):
    r, n = shard.shape
    hr = r // 2
    qr = hr // 2
    ch = qr // kc
    assert ch * kc == qr and ch % 16 == 0
    nflow = 12
    tr = 256

    def body(x_ref, p_ref, out_ref, slots_ref, send_sems, recv_sems, *small_sems):
        start_small, wait_small = _small_exchange(p_ref, slots_ref, *small_sems)
        start_small()
        x, y, c, _ = _place()
        me, cxn, cyn, cdg = 2 * x + y, 2 * (1 - x) + y, 2 * x + 1 - y, 2 * (1 - x) + 1 - y
        xn, yn, sibling = (1 - x, y, c), (x, 1 - y, c), (x, y, 1 - c)

        def piece(chip, hc, part, k):
            return out_ref.at[chip, pl.ds(hc * hr + part * qr + k * ch, ch), :]

        def own(part, k):
            return x_ref.at[pl.ds(c * hr + part * qr + k * ch, ch), :]

        def sems(flow, k):
            return send_sems.at[flow * kc + k], recv_sems.at[flow * kc + k]

        def arrival(flow, chip, hc, part, k):
            return _rcopy(piece(chip, hc, part, k), piece(chip, hc, part, k), *sems(flow, k), sibling)

        sends = []
        for flow, part, peer in ((0, 0, xn), (1, 1, yn), (2, 0, yn), (3, 1, xn)):
            sends += [_rcopy(own(part, k), piece(me, c, part, k), *sems(flow, k), peer) for k in range(kc)]
        for cp in sends:
            cp.start()
        landing = ((0, cxn, 0), (1, cyn, 1), (2, cyn, 0), (3, cxn, 1), (4, cdg, 0), (5, cdg, 1))
        for i, (flow, chip, part) in enumerate(landing):
            for k in range(kc):
                arrival(flow, chip, c, part, k).wait_recv()
                if flow < 2:
                    on = _rcopy(piece(chip, c, part, k), piece(chip, c, part, k), *sems(4 + flow, k),
                                yn if flow == 0 else xn)
                    on.start()
                    sends.append(on)
                fw = _rcopy(piece(chip, c, part, k), piece(chip, c, part, k), *sems(6 + i, k), sibling)
                fw.start()
                sends.append(fw)
        for i, (flow, chip, part) in enumerate(landing):
            for k in range(kc):
                arrival(6 + i, chip, 1 - c, part, k).wait_recv()
        for cp in sends:
            cp.wait_send()
        wait_small()

    full = jax.ShapeDtypeStruct((N_SHARD, r, n), shard.dtype)
    others, slots = _pallas(
        body, name=name, in_specs=[ANY, ANY], out_specs=[ANY, ANY],
        out_shape=[full, jax.ShapeDtypeStruct((N_DEV,) + small.shape, F32)],
        scratch_shapes=[pltpu.SemaphoreType.DMA((nflow * kc,)), pltpu.SemaphoreType.DMA((nflow * kc,))]
        + _small_scratch())(shard, small)

    def place(chip_ref, own_ref, all_ref, o_ref):
        o_ref[0] = own_ref[...]

    gathered = _pallas(
        place, name=name + "_own",
        grid_spec=pltpu.PrefetchScalarGridSpec(
            num_scalar_prefetch=1, grid=(r // tr,),
            in_specs=[pl.BlockSpec((tr, n), lambda i, chip_ref: (i, 0)), ANY],
            out_specs=pl.BlockSpec((1, tr, n), lambda i, chip_ref: (chip_ref[0], i, 0))),
        out_shape=full, input_output_aliases={2: 0}, compiler_params=_cp("parallel"))(chip, shard, others)
    return gathered, _sum_slots(slots, name + "_small")


def _pair_copies(src_ref, dst_ref, send_sems, recv_sems):
    hn = src_ref.shape[1] // 2
    cw = hn // PAIR_CHUNKS
    assert cw * PAIR_CHUNKS == hn and cw % LANES == 0
    x, y, c, _ = _place()
    return [_rcopy(src_ref.at[:, pl.ds((1 - c) * hn + k * cw, cw)], dst_ref.at[:, pl.ds(k * cw, cw)],
                   send_sems.at[k], recv_sems.at[k], (x, y, 1 - c)) for k in range(PAIR_CHUNKS)]


def _pair_send(parts, name):
    n = parts[0].shape[1]
    hn = n // 2
    kc = PAIR_CHUNKS
    cw = hn // kc
    assert cw * kc == hn and cw % LANES == 0
    ns = len(parts)

    def body(*refs):
        srcs, dsts, send_sems, recv_sems = refs[:ns], refs[ns:2 * ns], refs[2 * ns], refs[2 * ns + 1]
        x, y, c, _ = _place()
        cps = [_rcopy(srcs[s].at[:, pl.ds((1 - c) * hn + k * cw, cw)], dsts[s].at[:, pl.ds(k * cw, cw)],
                      send_sems.at[s * kc + k], recv_sems.at[s * kc + k], (x, y, 1 - c))
               for s in range(ns) for k in range(kc)]
        for cp in cps:
            cp.start()
        for cp in cps:
            cp.wait()

    return _pallas(
        body, name=name, in_specs=[ANY] * ns, out_specs=[ANY] * ns,
        out_shape=[jax.ShapeDtypeStruct((p.shape[0], hn), F32) for p in parts],
        scratch_shapes=[pltpu.SemaphoreType.DMA((ns * kc,)), pltpu.SemaphoreType.DMA((ns * kc,))])(*parts)


REDUCE_TILE = 256


def _exchange_copies(g_ref, got_ref, send_sems, recv_sems):
    hn = g_ref.shape[2]
    kc = GATHER_CHUNKS
    cw = hn // kc
    assert cw * kc == hn and cw % LANES == 0
    x, y, c, chips = _place()
    return [_rcopy(g_ref.at[2 * chip[0] + chip[1], :, pl.ds(k * cw, cw)], got_ref.at[j, :, pl.ds(k * cw, cw)],
                   send_sems.at[j * kc + k], recv_sems.at[j * kc + k], (*chip, c))
            for j, chip in enumerate(chips) for k in range(kc)]


def _exchange_scratch():
    return [pltpu.SemaphoreType.DMA((3 * GATHER_CHUNKS,)), pltpu.SemaphoreType.DMA((3 * GATHER_CHUNKS,))]


def _exchange_shape(ga):
    return jax.ShapeDtypeStruct((3,) + ga.shape[1:], ga.dtype)


def _chip_sum(ga, got, place, name):
    _, r, hn = ga.shape
    tc = REDUCE_TILE
    nt = hn // tc

    def body(place_ref, own_ref, got_ref, o_ref):
        acc = own_ref[0].astype(F32)
        for j in range(3):
            acc = acc + got_ref[j].astype(F32)
        o_ref[...] = acc

    return _pallas(
        body, name=name,
        grid_spec=pltpu.PrefetchScalarGridSpec(
            num_scalar_prefetch=1, grid=(nt,),
            in_specs=[pl.BlockSpec((1, r, tc), lambda i, place_ref: (place_ref[0], 0, i)),
                      pl.BlockSpec((3, r, tc), lambda i, place_ref: (0, 0, i))],
            out_specs=pl.BlockSpec((r, tc), lambda i, place_ref: (0, place_ref[1] * nt + i))),
        out_shape=jax.ShapeDtypeStruct((r, 2 * hn), F32), compiler_params=_cp("parallel"))(place, ga, got)


def _pair_join(buf, name, small=None):
    r, n = buf.shape
    hn = n // 2
    kc = JOIN_CHUNKS
    cw = hn // kc
    assert cw * kc == hn and cw % LANES == 0

    def body(in_ref, *refs):
        if small is None:
            out_ref, send_sems, recv_sems = refs
        else:
            p_ref, out_ref, slots_ref, send_sems, recv_sems = refs[:5]
            start_small, wait_small = _small_exchange(p_ref, slots_ref, *refs[5:])
            start_small()
        x, y, c, _ = _place()
        cps = [_rcopy(out_ref.at[:, pl.ds(c * hn + k * cw, cw)], out_ref.at[:, pl.ds(c * hn + k * cw, cw)],
                      send_sems.at[k], recv_sems.at[k], (x, y, 1 - c)) for k in range(kc)]
        for cp in cps:
            cp.start()
        for k in range(kc):
            cols = out_ref.at[:, pl.ds((1 - c) * hn + k * cw, cw)]
            _rcopy(cols, cols, send_sems.at[k], recv_sems.at[k], (x, y, 1 - c)).wait_recv()
        for cp in cps:
            cp.wait_send()
        if small is not None:
            wait_small()

    sems = [pltpu.SemaphoreType.DMA((kc,)), pltpu.SemaphoreType.DMA((kc,))]
    if small is None:
        return _pallas(body, name=name, in_specs=[ANY], out_specs=ANY, out_shape=jax.ShapeDtypeStruct((r, n), F32),
                       input_output_aliases={0: 0}, scratch_shapes=sems)(buf)
    return _pallas(
        body, name=name, in_specs=[ANY, ANY], out_specs=[ANY, ANY],
        out_shape=[jax.ShapeDtypeStruct((r, n), F32), jax.ShapeDtypeStruct((N_DEV,) + small.shape, F32)],
        input_output_aliases={0: 0}, scratch_shapes=sems + _small_scratch())(buf, small)


def _reduce_pair(parts, got, pieces, shard_rows, place, tag):
    if len(got) < len(parts):
        got = list(got) + list(_pair_send(parts[len(got):], tag + "_pair_send"))
    return _pair_add_to_shards(parts, got, pieces, shard_rows, place[1:2], tag + "_pair_add")


def _reduce_finish(ga, slabs, place, tag):
    return _pair_join(_chip_sum(ga, slabs, place, tag + "_chip_sum"), tag + "_pair_join")


N_DEV = 8


def _small_exchange(p_ref, slots_ref, send_sems, recv_sems, local_sem):
    x, y, c, _ = _place()
    my = 4 * x + 2 * y + c

    def sends():
        return [_rcopy(p_ref, slots_ref.at[my], send_sems.at[k - 1], recv_sems.at[k - 1],
                       (x ^ ((k >> 2) & 1), y ^ ((k >> 1) & 1), c ^ (k & 1))) for k in range(1, N_DEV)]

    def local():
        return pltpu.make_async_copy(p_ref, slots_ref.at[my], local_sem)

    def start():
        local().start()
        for cp in sends():
            cp.start()

    def wait():
        for k in range(1, N_DEV):
            _rcopy(p_ref, slots_ref.at[my ^ k], send_sems.at[k - 1], recv_sems.at[k - 1], (x, y, c)).wait_recv()
        for cp in sends():
            cp.wait_send()
        local().wait()

    return start, wait


def _small_scratch():
    return [pltpu.SemaphoreType.DMA((N_DEV - 1,)), pltpu.SemaphoreType.DMA((N_DEV - 1,)), pltpu.SemaphoreType.DMA]


def _sum_slots(slots, name):
    _, rows, n = slots.shape

    def body(s_ref, o_ref):
        acc = s_ref[0]
        for j in range(1, N_DEV):
            acc = acc + s_ref[j]
        o_ref[...] = acc

    vm = pl.BlockSpec(memory_space=pltpu.VMEM)
    return _pallas(body, name=name, in_specs=[vm], out_specs=vm, out_shape=jax.ShapeDtypeStruct((rows, n), F32))(slots)


def _adamw(w, g, m, v, name):
    r, n = w.shape
    tr = _tile(r, 256, 8)
    c1 = 1.0 / (1.0 - ADAM_B1 ** ADAM_STEP)
    c2 = 1.0 / (1.0 - ADAM_B2 ** ADAM_STEP)

    def body(w_ref, g_ref, m_ref, v_ref, d_ref, mo_ref, vo_ref):
        gv = g_ref[...]
        mn = ADAM_B1 * m_ref[...] + (1.0 - ADAM_B1) * gv
        vn = ADAM_B2 * v_ref[...] + (1.0 - ADAM_B2) * (gv * gv)
        d_ref[...] = -ADAM_LR * ((mn * c1) / (jnp.sqrt(vn * c2) + ADAM_EPS) + ADAM_WD * w_ref[...])
        mo_ref[...] = mn
        vo_ref[...] = vn

    spec = pl.BlockSpec((tr, n), lambda i: (i, 0))
    shp = jax.ShapeDtypeStruct((r, n), F32)
    return _pallas(body, name=name, grid=(r // tr,), in_specs=[spec] * 4, out_specs=[spec] * 3, out_shape=[shp] * 3,
                   compiler_params=_cp("parallel"))(w, g, m, v)


PACK_W = 1024
SMALL_REPL = ("norm_pre_w", "conv_b", "ssd_norm_w", "norm_post_w")
SMALL_HEAD = ("dt_bias", "a_log", "d_skip", "attn_sinks")


def _rows(a):
    return a.reshape(-1, PACK_W)


def _head_row(vals, extra=None):
    parts = [vals[n].reshape(1, -1) for n in SMALL_HEAD]
    if extra is not None:
        parts.append(extra.reshape(1, 1))
    row = jnp.concatenate(parts, axis=1)
    return jnp.pad(row, ((0, 0), (0, PACK_W - row.shape[1])))


def _pad_rows(a, rows):
    return jnp.pad(a, ((0, rows - a.shape[0]), (0, 0)))


def _pack_repl(vals, extra=None):
    body = jnp.concatenate([_rows(vals[n]) for n in SMALL_REPL] + [_head_row(vals, extra)], axis=0)
    return _pad_rows(body, 16)


def _unpack_repl(buf):
    out, r = {}, 0
    for n, k in zip(SMALL_REPL, (2, 4, 2, 2)):
        out[n] = buf[r:r + k].reshape(1, k * PACK_W)
        r += k
    col = 0
    for n, k in zip(SMALL_HEAD, (32, 32, 32, 16)):
        out[n] = buf[r:r + 1, col:col + k]
        col += k
    return out, buf[r, col]


def kernel(x, meta_tokens, norm_pre_w, w_in, conv_w, conv_b, dt_bias, a_log, d_skip, ssd_norm_w, attn_sinks, w_out, norm_post_w, loss_target, m_meta_tokens, m_norm_pre_w, m_w_in, m_conv_w, m_conv_b, m_dt_bias, m_a_log, m_d_skip, m_ssd_norm_w, m_attn_sinks, m_w_out, m_norm_post_w, v_meta_tokens, v_norm_pre_w, v_w_in, v_conv_w, v_conv_b, v_dt_bias, v_a_log, v_d_skip, v_ssd_norm_w, v_attn_sinks, v_w_out, v_norm_post_w):
    names = ("meta_tokens", "norm_pre_w", "w_in", "conv_w", "conv_b", "dt_bias", "a_log", "d_skip", "ssd_norm_w",
             "attn_sinks", "w_out", "norm_post_w")
    w = dict(zip(names, (meta_tokens, norm_pre_w, w_in, conv_w, conv_b, dt_bias, a_log, d_skip, ssd_norm_w, attn_sinks,
                         w_out, norm_post_w)))
    m = dict(zip(names, (m_meta_tokens, m_norm_pre_w, m_w_in, m_conv_w, m_conv_b, m_dt_bias, m_a_log, m_d_skip,
                         m_ssd_norm_w, m_attn_sinks, m_w_out, m_norm_post_w)))
    v = dict(zip(names, (v_meta_tokens, v_norm_pre_w, v_w_in, v_conv_w, v_conv_b, v_dt_bias, v_a_log, v_d_skip,
                         v_ssd_norm_w, v_attn_sinks, v_w_out, v_norm_post_w)))
    cx, cy, cc = lax.axis_index("x"), lax.axis_index("y"), lax.axis_index("c")
    chip = 2 * cx + cy
    meta_cols = D_MODEL // N_SHARD
    conv_cols = D_CONV // N_SHARD

    place = jnp.stack([chip, cc]).astype(jnp.int32)
    conv_z = lax.dynamic_update_slice(jnp.zeros((CONV_WIDTH, D_CONV), F32), conv_w[0], (0, chip * conv_cols))
    meta_z = lax.dynamic_update_slice(jnp.zeros((N_META, D_MODEL), F32), meta_tokens, (0, chip * meta_cols))
    small = jnp.concatenate([_rows(conv_z), _rows(meta_z)], axis=0)
    w_in_all, small = _gather_shards(_bf(w_in[0]), "gather_w_in", GATHER_CHUNKS, place[0:1],
                                     jnp.where(cc == 0, small, 0.0))
    w_re = _shards_to_re(w_in_all)
    conv_full = small[0:16].reshape(CONV_WIDTH, D_CONV)
    meta_full = small[16:48].reshape(N_META, D_MODEL)

    loss_dev, grad_x, g = _local_step(x[0], loss_target[0], meta_full, norm_pre_w, w_re, conv_full, conv_b, dt_bias,
                                      a_log, d_skip, ssd_norm_w, attn_sinks, _bf(w_out[0]), norm_post_w, place)
    g_w_out = g["w_out"]

    packed = jnp.concatenate([_rows(g["conv_w"]), _rows(g["meta_tokens"]), _pack_repl(g, loss_dev)], axis=0)
    g_w_in, slots = _pair_join(g["w_in_half"], "gw_in_pair_join", small=packed)
    red = _sum_slots(slots, "reduce_small")
    g_conv_full = red[0:16].reshape(CONV_WIDTH, D_CONV)
    g_meta_full = red[16:48].reshape(N_META, D_MODEL)
    g_small, loss = _unpack_repl(red[48:64])
    grads = dict(g_small)
    grads["w_in"] = g_w_in
    grads["w_out"] = g_w_out
    grads["conv_w"] = lax.dynamic_slice(g_conv_full, (0, chip * conv_cols), (CONV_WIDTH, conv_cols))
    grads["meta_tokens"] = lax.dynamic_slice(g_meta_full, (0, chip * meta_cols), (N_META, meta_cols))

    upd = {}
    upd["w_in"] = [jnp.swapaxes(a, 0, 1) for a in _adamw(jnp.swapaxes(w_in[0], 0, 1), g_w_in, jnp.swapaxes(m_w_in[0], 0, 1),
                                                         jnp.swapaxes(v_w_in[0], 0, 1), "adamw_w_in")]
    grads["w_in"] = jnp.swapaxes(g_w_in, 0, 1)
    upd["w_out"] = _adamw(w_out[0], g_w_out, m_w_out[0], v_w_out[0], "adamw_w_out")

    def pack_small(vals, conv, meta):
        return jnp.concatenate([_pad_rows(conv.reshape(CONV_WIDTH, conv_cols), 8), _rows(meta), _pack_repl(vals)], axis=0)

    sm = _adamw(pack_small(w, w["conv_w"], w["meta_tokens"]), pack_small(grads, grads["conv_w"], grads["meta_tokens"]),
                pack_small(m, m["conv_w"], m["meta_tokens"]), pack_small(v, v["conv_w"], v["meta_tokens"]),
                "adamw_small")
    for n in names:
        if n not in ("w_in", "w_out"):
            upd[n] = [None, None, None]
    for k, buf in enumerate(sm):
        upd["conv_w"][k] = buf[0:CONV_WIDTH]
        upd["meta_tokens"][k] = buf[8:16].reshape(N_META, meta_cols)
        rest, _ = _unpack_repl(buf[16:32])
        for n in SMALL_REPL + SMALL_HEAD:
            upd[n][k] = rest[n]

    def shaped(n, a):
        return a.reshape(w[n].shape)

    outs = [loss, grad_x[None]]
    outs += [shaped(n, grads[n]) for n in names]
    for k in range(3):
        outs += [shaped(n, upd[n][k]) for n in names]
    return tuple(outs)
```

```python
import functools

import jax
import jax.numpy as jnp
from jax import lax
from jax.experimental import pallas as pl
from jax.experimental.pallas import tpu as pltpu

F32 = jnp.float32
BF16 = jnp.bfloat16

D_MODEL = 2048
CHUNK = 64
N_META = 16
PAD_LEAD = CHUNK - N_META
ROW0 = PAD_LEAD + N_META
EPS = 1e-6
SSD_HEADS = 32
HEAD_DIM = 64
GROUPS = 8
HPG = SSD_HEADS // GROUPS
D_STATE = 128
D_SSD = 2048
GROUP_W = D_SSD // GROUPS
CONV_WIDTH = 4
D_CONV = 4096
Q_HEADS = 16
KV_HEADS = 4
REP = Q_HEADS // KV_HEADS
D_ATT = 1024
D_KV = 256
BAND_CHUNKS = 3
ROPE_THETA = 10000.0
D_MIX = D_SSD + D_ATT
D_IN = 8736
N_SHARD = 4
W_IN_SHARD = D_IN // N_SHARD
W_OUT_SHARD = D_MIX // N_SHARD

OZ, OXS, OB, OC, OQ, OG, OK, OV, ODT = 0, 2048, 4096, 5120, 6144, 7168, 8192, 8448, 8704
DT_SLAB = 512
N_RE = ODT + DT_SLAB
LANES = 128

ADAM_LR, ADAM_B1, ADAM_B2, ADAM_EPS, ADAM_WD, ADAM_STEP = 0.001, 0.9, 0.999, 1e-08, 0.01, 10

SSD_FWD_GROUPS_PER_STEP = 4
SSD_BWD_GROUPS_PER_STEP = 8
SEG_TILE = 1024
VMEM_LIMIT = 52 * 1024 * 1024
NEG = -1e30
HI = lax.Precision.HIGHEST


def _pallas(body, **kw):
    return pl.pallas_call(body, **kw)


def _cp(*sem):
    return pltpu.CompilerParams(dimension_semantics=sem, vmem_limit_bytes=VMEM_LIMIT)


def _tile(n, cap, mult=16):
    best = None
    for d in range(mult, min(n, cap) + 1, mult):
        if n % d == 0:
            best = d
    assert best is not None, (n, cap)
    return best


def _nt(a, b):
    return lax.dot_general(a, b, (((1,), (1,)), ((), ())), preferred_element_type=F32)


def _tn(a, b):
    return lax.dot_general(a, b, (((0,), (0,)), ((), ())), preferred_element_type=F32)


def _mm(a, b):
    return jnp.dot(a, b, preferred_element_type=F32)


def _sigmoid(x):
    return 1.0 / (1.0 + jnp.exp(-x))


def _bf(x):
    return x.astype(BF16)


def _inproj(hpad, norm_w, w_re, w_out_shard):
    t, d = hpad.shape
    n = w_re.shape[1]
    tm, tn = _tile(t, 1040), 1024
    ni, nj = t // tm, n // tn
    r_out, n_out = w_out_shard.shape
    kc = GATHER_CHUNKS

    def body(h_ref, nw_ref, w_ref, ws_ref, proj_ref, hn_ref, wall_ref, hn_s, send_sems, recv_sems, local_sems):
        i, j = pl.program_id(0), pl.program_id(1)
        start, forward, finish = _gather_plan(ws_ref, wall_ref, send_sems, recv_sems, local_sems, r_out // 2, kc)
        pl.when((i == 0) & (j == 0))(start)
        pl.when((i == ni // 2) & (j == 0))(forward)

        @pl.when(j == 0)
        def _():
            h = h_ref[...]
            ms = jnp.mean(h * h, axis=-1, keepdims=True)
            hn = _bf(h * lax.rsqrt(ms + EPS) * nw_ref[...])
            hn_s[...] = hn
            hn_ref[...] = hn
        proj_ref[...] = _mm(hn_s[...], w_ref[...])
        pl.when((i == ni - 1) & (j == nj - 1))(finish)

    return _pallas(
        body, name="inproj", grid=(ni, nj),
        in_specs=[pl.BlockSpec((tm, d), lambda i, j: (i, 0)), pl.BlockSpec((1, d), lambda i, j: (0, 0)),
                  pl.BlockSpec((d, tn), lambda i, j: (0, j)), ANY],
        out_specs=[pl.BlockSpec((tm, tn), lambda i, j: (i, j)), pl.BlockSpec((tm, d), lambda i, j: (i, 0)), ANY],
        out_shape=[jax.ShapeDtypeStruct((t, n), F32), jax.ShapeDtypeStruct((t, d), BF16),
                   jax.ShapeDtypeStruct((N_SHARD, r_out, n_out), w_out_shard.dtype)],
        scratch_shapes=[pltpu.VMEM((tm, d), BF16), pltpu.SemaphoreType.DMA((6 * kc,)), pltpu.SemaphoreType.DMA((6 * kc,)),
                        pltpu.SemaphoreType.DMA((2 * kc,))],
        compiler_params=_cp("arbitrary", "arbitrary"))(hpad, norm_w, w_re, w_out_shard)


def _conv_fwd(proj, conv_w, conv_b):
    t = proj.shape[0]
    tc = 128
    rt = _tile(t, 320)
    off = OXS // tc

    def body(x_ref, w_ref, b_ref, o_ref, xp):
        xp[0:8, :] = jnp.zeros((8, tc), F32)
        xp[8:t + 8, :] = x_ref[...]
        w = w_ref[...]
        bias = b_ref[...]

        def tile(i, carry):
            r0 = pl.multiple_of(i * rt, 8)
            u = bias
            for k in range(CONV_WIDTH):
                u = u + w[k:k + 1, :] * xp[pl.ds(r0 + 5 + k, rt), :]
            h = 0.5 * u
            o_ref[pl.ds(r0, rt), :] = h + h * jnp.tanh(h)
            return carry

        lax.fori_loop(0, t // rt, tile, 0)

    return _pallas(
        body, name="conv_fwd", grid=(D_CONV // tc,),
        in_specs=[pl.BlockSpec((t, tc), lambda j: (0, j + off)), pl.BlockSpec((CONV_WIDTH, tc), lambda j: (0, j)),
                  pl.BlockSpec((1, tc), lambda j: (0, j))],
        out_specs=pl.BlockSpec((t, tc), lambda j: (0, j)),
        out_shape=jax.ShapeDtypeStruct((t, D_CONV), F32),
        scratch_shapes=[pltpu.VMEM((t + 8, tc), F32)],
        compiler_params=_cp("parallel"))(proj, conv_w, conv_b)


def _softplus(u):
    e = jnp.exp(-jnp.abs(u))
    w = 1.0 + e
    l1p = jnp.where(w == 1.0, e, jnp.log(w) * (e / jnp.where(w == 1.0, 1.0, w - 1.0)))
    return jnp.maximum(u, 0.0) + l1p


def _chunks_per_step(nc):
    return max(d for d in range(1, 14) if nc % d == 0)


def _dt_prep(proj, dt_bias_l, a_log_l):
    t = proj.shape[0]
    nc = t // CHUNK
    q = CHUNK
    cps = _chunks_per_step(nc)
    rows = cps * q

    def body(raw_ref, bias_ref, alog_ref, dt_ref, acs_ref, acst_ref):
        ri = lax.broadcasted_iota(jnp.int32, (q, q), 0)
        ci = lax.broadcasted_iota(jnp.int32, (q, q), 1)
        tri = (ri >= ci).astype(F32)
        neg_a = -jnp.exp(alog_ref[...])
        for k in range(cps):
            rk = slice(q * k, q * (k + 1))
            sp = _softplus(raw_ref[rk, :] + bias_ref[...])
            row = pl.program_id(0) * rows + q * k + lax.broadcasted_iota(jnp.int32, (q, LANES), 0)
            dt = jnp.where(row >= PAD_LEAD, sp, 0.0)
            acs = jnp.dot(tri, dt * neg_a, preferred_element_type=F32, precision=HI)
            dt_ref[rk, :] = dt
            acs_ref[rk, :] = acs
            acst_ref[k] = acs.T

    return _pallas(
        body, name="dt_prep", grid=(nc // cps,),
        in_specs=[pl.BlockSpec((rows, LANES), lambda c: (c, ODT // LANES)), pl.BlockSpec((1, LANES), lambda c: (0, 0)),
                  pl.BlockSpec((1, LANES), lambda c: (0, 0))],
        out_specs=[pl.BlockSpec((rows, LANES), lambda c: (c, 0)), pl.BlockSpec((rows, LANES), lambda c: (c, 0)),
                   pl.BlockSpec((cps, LANES, q), lambda c: (c, 0, 0))],
        out_shape=[jax.ShapeDtypeStruct((t, LANES), F32), jax.ShapeDtypeStruct((t, LANES), F32),
                   jax.ShapeDtypeStruct((nc, LANES, q), F32)],
        compiler_params=_cp("parallel"))(proj, dt_bias_l, a_log_l)


def _head_cols(blk, idx):
    lane = lax.broadcasted_iota(jnp.int32, blk.shape, 1)
    return jnp.sum(jnp.where(lane == idx, blk, 0.0), axis=1, keepdims=True)


class _HeadVals:
    pass


def _lane_head(shape):
    return lax.broadcasted_iota(jnp.int32, shape, len(shape) - 1) >> 6


def _group_heads(g, gi, dtb, acsb, acst_ref, dskb):
    q = dtb.shape[0]
    hv = _HeadVals()
    lh = _lane_head((1, GROUP_W))
    hv.dt = jnp.zeros((q, GROUP_W), F32)
    hv.acs = jnp.zeros((q, GROUP_W), F32)
    hv.acs_last = jnp.zeros((1, GROUP_W), F32)
    hv.dsk = jnp.zeros((1, GROUP_W), F32)
    rows = []
    for r in range(HPG):
        idx = GROUPS * g + r
        sel = lh == r
        acs_r = acst_ref[0, GROUPS * gi + r:GROUPS * gi + r + 1, :]
        rows.append(acs_r)
        hv.dt = jnp.where(sel, _head_cols(dtb, idx), hv.dt)
        hv.acs = jnp.where(sel, _head_cols(acsb, idx), hv.acs)
        hv.acs_last = jnp.where(sel, acs_r[:, q - 1:q], hv.acs_last)
        hv.dsk = jnp.where(sel, _head_cols(dskb, idx), hv.dsk)
    hv.acs_row = jnp.concatenate(rows, axis=1)
    return hv


def _head_tri(q, lower):
    ri = lax.broadcasted_iota(jnp.int32, (q, GROUP_W), 0)
    li = lax.broadcasted_iota(jnp.int32, (q, GROUP_W), 1) & (HEAD_DIM - 1)
    return ri >= li if lower else ri <= li


def _block_diag_mask():
    rb = lax.broadcasted_iota(jnp.int32, (GROUP_W, GROUP_W), 0) >> 6
    cb = lax.broadcasted_iota(jnp.int32, (GROUP_W, GROUP_W), 1) >> 6
    return rb == cb


def _block_diag(v, mask):
    return jnp.where(mask, jnp.concatenate([v] * HPG, axis=0), jnp.zeros((), v.dtype))


def _head_sums(v, r):
    return jnp.sum(jnp.where(_lane_head((1, GROUP_W)) == r, v, 0.0), axis=1, keepdims=True)


def _ssd_fwd(xbc, proj, dt, acs, acst, d_skip_l, ssd_norm_w):
    t = xbc.shape[0]
    q = CHUNK
    nc = t // q

    gps = SSD_FWD_GROUPS_PER_STEP
    gw, sw = gps * GROUP_W, gps * D_STATE

    def body(xs_ref, b_ref, c_ref, dt_ref, acs_ref, acst_ref, z_ref, dsk_ref, nw_ref,
             y_ref, ymix_ref, st_ref, state):
        @pl.when(pl.program_id(1) == 0)
        def _():
            state[...] = jnp.zeros_like(state)

        lower = _head_tri(q, True)
        bd_mask = _block_diag_mask()
        for gi in range(gps):
            g = gps * pl.program_id(0) + gi
            cols = slice(GROUP_W * gi, GROUP_W * (gi + 1))
            x = xs_ref[:, cols]
            bmb = _bf(b_ref[:, D_STATE * gi:D_STATE * (gi + 1)])
            cmb = _bf(c_ref[:, D_STATE * gi:D_STATE * (gi + 1)])
            hv = _group_heads(g, gi, dt_ref[...], acs_ref[...], acst_ref, dsk_ref[...])
            decay = jnp.exp(jnp.where(lower, hv.acs - hv.acs_row, NEG))
            m_all = _bf(_nt(cmb, jnp.concatenate([bmb] * HPG, axis=0)) * decay)
            xdt = x * hv.dt
            s_prev = state[gi]
            st_ref[0, gi] = s_prev
            y = (_mm(m_all, _block_diag(_bf(xdt), bd_mask)) + _mm(cmb, _bf(s_prev)) * jnp.exp(hv.acs) + hv.dsk * x)
            state[gi] = jnp.exp(hv.acs_last) * s_prev + _tn(bmb, _bf(xdt * jnp.exp(hv.acs_last - hv.acs)))
            y_ref[:, cols] = y
            z = z_ref[:, cols]
            yg = y * (z * _sigmoid(z))
            ms = jnp.mean(yg * yg, axis=-1, keepdims=True)
            ymix_ref[:, cols] = _bf(yg * lax.rsqrt(ms + EPS) * nw_ref[:, cols])

    return _pallas(
        body, name="ssd_fwd", grid=(GROUPS // gps, nc),
        in_specs=[pl.BlockSpec((q, gw), lambda g, c: (c, g)),
                  pl.BlockSpec((q, sw), lambda g, c: (c, D_SSD // sw + g)),
                  pl.BlockSpec((q, sw), lambda g, c: (c, (D_SSD + GROUPS * D_STATE) // sw + g)),
                  pl.BlockSpec((q, LANES), lambda g, c: (c, 0)), pl.BlockSpec((q, LANES), lambda g, c: (c, 0)),
                  pl.BlockSpec((1, gps * GROUPS, q), lambda g, c: (c, g, 0)),
                  pl.BlockSpec((q, gw), lambda g, c: (c, g)),
                  pl.BlockSpec((1, LANES), lambda g, c: (0, 0)), pl.BlockSpec((1, gw), lambda g, c: (0, g))],
        out_specs=[pl.BlockSpec((q, gw), lambda g, c: (c, g)), pl.BlockSpec((q, gw), lambda g, c: (c, g)),
                   pl.BlockSpec((1, gps, D_STATE, GROUP_W), lambda g, c: (c, g, 0, 0))],
        out_shape=[jax.ShapeDtypeStruct((t, D_SSD), F32), jax.ShapeDtypeStruct((t, D_SSD), BF16),
                   jax.ShapeDtypeStruct((nc, GROUPS, D_STATE, GROUP_W), F32)],
        scratch_shapes=[pltpu.VMEM((gps, D_STATE, GROUP_W), F32)],
        compiler_params=_cp("parallel", "arbitrary"))(xbc, xbc, xbc, dt, acs, acst, proj, d_skip_l, ssd_norm_w)


def _swap_halves(v):
    lane = lax.broadcasted_iota(jnp.int32, v.shape, 1)
    return jnp.where((lane & (HEAD_DIM - 1)) < HEAD_DIM // 2, pltpu.roll(v, LANES - HEAD_DIM // 2, 1),
                     pltpu.roll(v, HEAD_DIM // 2, 1))


def _rope(qsrc, q_off, ksrc, k_off, cos_t, sin_t):
    t = qsrc.shape[0]
    tr = _tile(t, 832)
    q_scale = HEAD_DIM ** -0.5

    def body(q_ref, k_ref, cos_ref, sin_ref, qo_ref, ko_ref):
        cs = cos_ref[...]
        sn = sin_ref[...]
        for src, dst, width, scale in ((q_ref, qo_ref, D_ATT, q_scale), (k_ref, ko_ref, D_KV, 1.0)):
            for s in range(width // LANES):
                v = src[:, LANES * s:LANES * (s + 1)].astype(F32)
                dst[:, LANES * s:LANES * (s + 1)] = _bf((v * cs + _swap_halves(v) * sn) * scale)

    return _pallas(
        body, name="rope", grid=(t // tr,),
        in_specs=[pl.BlockSpec((tr, D_ATT), lambda i: (i, q_off // D_ATT)),
                  pl.BlockSpec((tr, D_KV), lambda i: (i, k_off // D_KV)),
                  pl.BlockSpec((tr, LANES), lambda i: (i, 0)), pl.BlockSpec((tr, LANES), lambda i: (i, 0))],
        out_specs=[pl.BlockSpec((tr, D_ATT), lambda i: (i, 0)), pl.BlockSpec((tr, D_KV), lambda i: (i, 0))],
        out_shape=[jax.ShapeDtypeStruct((t, D_ATT), BF16), jax.ShapeDtypeStruct((t, D_KV), BF16)],
        compiler_params=_cp("parallel"))(qsrc, ksrc, cos_t, sin_t)


def _attn_chunks_per_step(nc):
    return max(d for d in range(1, 6) if nc % d == 0)


def _band(ref, c):
    return [ref[pl.ds(pl.multiple_of(jnp.maximum(c - j, 0) * CHUNK, CHUNK), CHUNK), :] for j in (2, 1, 0)]


def _attn_probs(qh, kb, sink_col, valid):
    s = jnp.where(valid, _nt(qh, kb), NEG)
    m = jnp.maximum(jnp.max(s, axis=1, keepdims=True), sink_col)
    p = jnp.exp(s - m)
    psink = jnp.exp(sink_col - m)
    return p, psink, 1.0 / (jnp.sum(p, axis=1, keepdims=True) + psink)


def _attn_operands(c, q, k_refs, v_refs, sink_ref, h):
    qh = jnp.concatenate([q[:, HEAD_DIM * (REP * h + r):HEAD_DIM * (REP * h + r + 1)] for r in range(REP)], axis=0)
    kb = jnp.concatenate([k[:, HEAD_DIM * h:HEAD_DIM * (h + 1)] for k in k_refs], axis=0)
    vb = jnp.concatenate([_bf(v[:, HEAD_DIM * h:HEAD_DIM * (h + 1)]) for v in v_refs], axis=0)
    rows = lax.broadcasted_iota(jnp.int32, (REP * CHUNK, 1), 0) >> 6
    sink_col = jnp.zeros((REP * CHUNK, 1), F32)
    for r in range(REP):
        sink_col = jnp.where(rows == r, sink_ref[REP * h + r], sink_col)
    key_abs = (c - (BAND_CHUNKS - 1)) * CHUNK + lax.broadcasted_iota(jnp.int32, (1, BAND_CHUNKS * CHUNK), 1)
    return qh, kb, vb, sink_col, key_abs >= PAD_LEAD


def _attn_fwd(qr, kr, proj, sinks):
    t = qr.shape[0]
    nc = t // CHUNK
    cps = _attn_chunks_per_step(nc)
    rows = cps * CHUNK

    def body(q_ref, k_ref, v_ref, g_ref, sink_ref, o_ref):
        for j in range(cps):
            c = pl.program_id(0) * cps + j
            rj = slice(CHUNK * j, CHUNK * (j + 1))
            ks, vs = _band(k_ref, c), _band(v_ref, c)
            q = q_ref[rj, :]
            outs = []
            for h in range(KV_HEADS):
                qh, kb, vb, sink_col, valid = _attn_operands(c, q, ks, vs, sink_ref, h)
                p, _, inv = _attn_probs(qh, kb, sink_col, valid)
                o = _mm(_bf(p), vb) * inv
                outs += [o[CHUNK * r:CHUNK * (r + 1)] for r in range(REP)]
            att = jnp.concatenate(outs, axis=1)
            gate = g_ref[rj, :]
            o_ref[rj, :] = _bf(att * (gate * _sigmoid(gate)))

    return _pallas(
        body, name="attn_fwd", grid=(nc // cps,),
        in_specs=[pl.BlockSpec((rows, D_ATT), lambda i: (i, 0)), pl.BlockSpec((t, D_KV), lambda i: (0, 0)),
                  pl.BlockSpec((t, D_KV), lambda i: (0, OV // D_KV)),
                  pl.BlockSpec((rows, D_ATT), lambda i: (i, OG // D_ATT)), pl.BlockSpec(memory_space=pltpu.SMEM)],
        out_specs=pl.BlockSpec((rows, D_ATT), lambda i: (i, 0)),
        out_shape=jax.ShapeDtypeStruct((t, D_ATT), BF16),
        compiler_params=_cp("parallel"))(qr, kr, proj, proj, sinks)


def _outproj(ymix, amix, w_out):
    t = ymix.shape[0]
    tm, tn = _tile(t, 832), 1024

    def body(y_ref, a_ref, wy_ref, wa_ref, o_ref):
        o_ref[...] = _mm(y_ref[...], wy_ref[...]) + _mm(a_ref[...], wa_ref[...])

    return _pallas(
        body, name="outproj", grid=(t // tm, D_MODEL // tn),
        in_specs=[pl.BlockSpec((tm, D_SSD), lambda i, j: (i, 0)), pl.BlockSpec((tm, D_ATT), lambda i, j: (i, 0)),
                  pl.BlockSpec((D_SSD, tn), lambda i, j: (0, j)),
                  pl.BlockSpec((D_ATT, tn), lambda i, j: (D_SSD // D_ATT, j))],
        out_specs=pl.BlockSpec((tm, tn), lambda i, j: (i, j)),
        out_shape=jax.ShapeDtypeStruct((t, D_MODEL), F32),
        compiler_params=_cp("parallel", "parallel"))(ymix, amix, w_out, w_out)


def _post_loss(out, x, target, norm_post_w):
    t = out.shape[0]
    nc = t // CHUNK
    cps = _attn_chunks_per_step(nc)
    rows = cps * CHUNK

    def body(o_ref, *refs):
        x_refs, tg_refs = refs[:cps], refs[cps:2 * cps]
        nw_ref, dout_ref, dy_ref, loss_ref, gnw_ref = refs[2 * cps:]
        i = pl.program_id(0)

        @pl.when(i == 0)
        def _():
            loss_ref[...] = jnp.zeros_like(loss_ref)
            gnw_ref[...] = jnp.zeros_like(gnw_ref)

        nw = nw_ref[...]
        loss = jnp.zeros((), F32)
        gnw = jnp.zeros((1, D_MODEL), F32)
        for k in range(cps):
            rk = slice(CHUNK * k, CHUNK * (k + 1))
            frames = i * cps + k > 0
            o = o_ref[rk, :]
            rstd = lax.rsqrt(jnp.mean(o * o, axis=-1, keepdims=True) + EPS)
            n = o * rstd
            err = jnp.where(frames, x_refs[k][...] + n * nw - tg_refs[k][...], 0.0)
            loss = loss + jnp.sum(err * err)
            dy = err * (1.0 / D_MODEL)
            dy_ref[rk, :] = dy
            gnw = gnw + jnp.sum(dy * n, axis=0, keepdims=True)
            dn = dy * nw
            dout_ref[rk, :] = _bf(rstd * (dn - n * jnp.mean(dn * n, axis=-1, keepdims=True)))
        loss_ref[...] += loss * (0.5 / D_MODEL)
        gnw_ref[...] += gnw

    lower = [pl.BlockSpec((CHUNK, D_MODEL), functools.partial(lambda i, k: (jnp.maximum(i * cps + k - 1, 0), 0), k=k))
             for k in range(cps)]
    return _pallas(
        body, name="post_loss", grid=(nc // cps,),
        in_specs=[pl.BlockSpec((rows, D_MODEL), lambda i: (i, 0))] + lower + lower
        + [pl.BlockSpec((1, D_MODEL), lambda i: (0, 0))],
        out_specs=[pl.BlockSpec((rows, D_MODEL), lambda i: (i, 0)), pl.BlockSpec((rows, D_MODEL), lambda i: (i, 0)),
                   pl.BlockSpec((8, LANES), lambda i: (0, 0)), pl.BlockSpec((1, D_MODEL), lambda i: (0, 0))],
        out_shape=[jax.ShapeDtypeStruct((t, D_MODEL), BF16), jax.ShapeDtypeStruct((t, D_MODEL), F32),
                   jax.ShapeDtypeStruct((8, LANES), F32), jax.ShapeDtypeStruct((1, D_MODEL), F32)],
        compiler_params=_cp("arbitrary"))(out, *([x] * cps), *([target] * cps), norm_post_w)


def _carried(grid, carry):
    if carry is None:
        return [], [], [], [], lambda refs: None, lambda refs: None
    hn = carry.shape[1] // 2

    def at(ids, which):
        cond = None
        for d, size in enumerate(grid):
            here = pl.program_id(d) == (0 if which == "first" else size - 1)
            cond = here if cond is None else cond & here
        return cond

    def start(refs):
        @pl.when(at(grid, "first"))
        def _():
            for cp in _pair_copies(*refs):
                cp.start()

    def finish(refs):
        @pl.when(at(grid, "last"))
        def _():
            for cp in _pair_copies(*refs):
                cp.wait()

    return ([ANY], [ANY], [jax.ShapeDtypeStruct((carry.shape[0], hn), F32)],
            [pltpu.SemaphoreType.DMA((PAIR_CHUNKS,)), pltpu.SemaphoreType.DMA((PAIR_CHUNKS,))], start, finish)


def _nt_matmul(a, b, name, carry=None):
    t, k = a.shape
    n = b.shape[0]
    tm, tn = _tile(t, 832), 1024
    grid = (t // tm, n // tn)
    cin, cout, cshape, cscratch, start, finish = _carried(grid, carry)

    def body(a_ref, b_ref, *refs):
        o_ref = refs[len(cin)]
        comm = (refs[0], refs[2], refs[3], refs[4]) if carry is not None else None
        start(comm)
        o_ref[...] = _nt(a_ref[...], b_ref[...])
        finish(comm)

    res = _pallas(
        body, name=name, grid=grid,
        in_specs=[pl.BlockSpec((tm, k), lambda i, j: (i, 0)), pl.BlockSpec((tn, k), lambda i, j: (j, 0))] + cin,
        out_specs=[pl.BlockSpec((tm, tn), lambda i, j: (i, j))] + cout,
        out_shape=[jax.ShapeDtypeStruct((t, n), F32)] + cshape, scratch_shapes=cscratch,
        compiler_params=_cp("arbitrary", "arbitrary"))(a, b, *([carry] if carry is not None else []))
    return res if carry is not None else res[0]


def _tn_matmul(a, b, name, carry=None):
    t, m = a.shape
    n = b.shape[1]
    tk, tm, tn = _tile(t, 832), min(m, 2048), min(n, 2048)
    nk = t // tk
    grid = (m // tm, n // tn, nk)
    cin, cout, cshape, cscratch, start, finish = _carried(grid, carry)

    def body(a_ref, b_ref, *refs):
        o_ref = refs[len(cin)]
        comm = (refs[0], refs[2], refs[3], refs[4]) if carry is not None else None
        start(comm)

        @pl.when(pl.program_id(2) == 0)
        def _():
            o_ref[...] = jnp.zeros_like(o_ref)
        o_ref[...] += _tn(a_ref[...], b_ref[...])
        finish(comm)

    res = _pallas(
        body, name=name, grid=grid,
        in_specs=[pl.BlockSpec((tk, tm), lambda i, j, k: (k, i)), pl.BlockSpec((tk, tn), lambda i, j, k: (k, j))] + cin,
        out_specs=[pl.BlockSpec((tm, tn), lambda i, j, k: (i, j))] + cout,
        out_shape=[jax.ShapeDtypeStruct((m, n), F32)] + cshape, scratch_shapes=cscratch,
        compiler_params=_cp("arbitrary", "arbitrary", "arbitrary"))(a, b, *([carry] if carry is not None else []))
    return res if carry is not None else res[0]


def _attn_bwd(qr, kr, proj, dmix, sinks, ga):
    t = qr.shape[0]
    nc = t // CHUNK
    cps = _attn_chunks_per_step(nc)
    nsteps = nc // cps
    rows_step = cps * CHUNK

    def body(q_ref, k_ref, v_ref, g_ref, da_ref, sink_ref, ga_ref, dq_ref, dg_ref, dk_ref, dv_ref, gs_ref,
             got_ref, send_sems, recv_sems):
        step = pl.program_id(0)

        @pl.when(step == 0)
        def _():
            for cp in _exchange_copies(ga_ref, got_ref, send_sems, recv_sems):
                cp.start()
            dk_ref[...] = jnp.zeros_like(dk_ref)
            dv_ref[...] = jnp.zeros_like(dv_ref)
            gs_ref[...] = jnp.zeros_like(gs_ref)

        lane = lax.broadcasted_iota(jnp.int32, (1, LANES), 1)
        rows = lax.broadcasted_iota(jnp.int32, (REP * CHUNK, 1), 0) >> 6
        gs = jnp.zeros((1, LANES), F32)
        dk_parts = [[] for _ in range(cps + BAND_CHUNKS - 1)]
        dv_parts = [[] for _ in range(cps + BAND_CHUNKS - 1)]
        for j in range(cps):
            c = step * cps + j
            rj = slice(CHUNK * j, CHUNK * (j + 1))
            ks, vs = _band(k_ref, c), _band(v_ref, c)
            q = q_ref[rj, :]
            gate = g_ref[rj, :]
            sg = _sigmoid(gate)
            da = da_ref[rj, :]
            datt = da * (gate * sg)
            dqs, atts, dks, dvs = [], [], [], []
            for h in range(KV_HEADS):
                qh, kb, vb, sink_col, valid = _attn_operands(c, q, ks, vs, sink_ref, h)
                p, psink, inv = _attn_probs(qh, kb, sink_col, valid)
                pb = _bf(p)
                o = _mm(pb, vb) * inv
                do = jnp.concatenate([datt[:, HEAD_DIM * (REP * h + r):HEAD_DIM * (REP * h + r + 1)]
                                      for r in range(REP)], axis=0)
                dob = _bf(do * inv)
                delta = jnp.sum(do * o, axis=1, keepdims=True) * inv
                ds = _bf(p * (_nt(dob, vb) - delta))
                gsink = -psink * delta
                for r in range(REP):
                    gs = gs + jnp.where(lane == REP * h + r, jnp.sum(jnp.where(rows == r, gsink, 0.0)), 0.0)
                dqh = _mm(ds, kb)
                dqs += [dqh[CHUNK * r:CHUNK * (r + 1)] for r in range(REP)]
                atts += [o[CHUNK * r:CHUNK * (r + 1)] for r in range(REP)]
                dks.append(_tn(ds, qh))
                dvs.append(_tn(pb, dob))
            dq_ref[rj, :] = jnp.concatenate(dqs, axis=1)
            att = jnp.concatenate(atts, axis=1)
            dg_ref[rj, :] = _bf(da * att * (sg * (1.0 + gate * (1.0 - sg))))
            dkf = jnp.concatenate(dks, axis=1)
            dvf = jnp.concatenate(dvs, axis=1)
            for b in range(BAND_CHUNKS):
                dk_parts[j + b].append(dkf[CHUNK * b:CHUNK * (b + 1)])
                dv_parts[j + b].append(dvf[CHUNK * b:CHUNK * (b + 1)])
        gs_ref[0:1, :] += gs
        for rel in range(cps + BAND_CHUNKS - 1):
            r0 = pl.multiple_of(jnp.maximum(step * cps - (BAND_CHUNKS - 1) + rel, 0) * CHUNK, CHUNK)
            dk_ref[pl.ds(r0, CHUNK), :] += sum(dk_parts[rel][1:], dk_parts[rel][0])
            dv_ref[pl.ds(r0, CHUNK), :] += sum(dv_parts[rel][1:], dv_parts[rel][0])

        @pl.when(step == nsteps - 1)
        def _():
            for cp in _exchange_copies(ga_ref, got_ref, send_sems, recv_sems):
                cp.wait()

    return _pallas(
        body, name="attn_bwd", grid=(nsteps,),
        in_specs=[pl.BlockSpec((rows_step, D_ATT), lambda i: (i, 0)), pl.BlockSpec((t, D_KV), lambda i: (0, 0)),
                  pl.BlockSpec((t, D_KV), lambda i: (0, OV // D_KV)),
                  pl.BlockSpec((rows_step, D_ATT), lambda i: (i, OG // D_ATT)),
                  pl.BlockSpec((rows_step, D_ATT), lambda i: (i, D_SSD // D_ATT)),
                  pl.BlockSpec(memory_space=pltpu.SMEM), ANY],
        out_specs=[pl.BlockSpec((rows_step, D_ATT), lambda i: (i, 0)), pl.BlockSpec((rows_step, D_ATT), lambda i: (i, 0)),
                   pl.BlockSpec((t, D_KV), lambda i: (0, 0)), pl.BlockSpec((t, D_KV), lambda i: (0, 0)),
                   pl.BlockSpec((8, LANES), lambda i: (0, 0)), ANY],
        out_shape=[jax.ShapeDtypeStruct((t, D_ATT), F32), jax.ShapeDtypeStruct((t, D_ATT), BF16),
                   jax.ShapeDtypeStruct((t, D_KV), F32), jax.ShapeDtypeStruct((t, D_KV), F32),
                   jax.ShapeDtypeStruct((8, LANES), F32), _exchange_shape(ga)],
        scratch_shapes=_exchange_scratch(),
        compiler_params=_cp("arbitrary"))(qr, kr, proj, proj, dmix, sinks, ga)


def _ssd_bwd(dmix, y_ssd, xbc, proj, dt, acs, acst, states, d_skip_l, ssd_norm_w):
    t = xbc.shape[0]
    q = CHUNK
    nc = t // q
    gps = SSD_BWD_GROUPS_PER_STEP
    gw, sw = gps * GROUP_W, gps * D_STATE

    def body(dmix_ref, y_ref, z_ref, nw_ref, xs_ref, b_ref, c_ref, dt_ref, acs_ref, acst_ref, st_ref, dsk_ref,
             dz_ref, dxs_ref, db_ref, dc_ref, dacs_ref, ddt_ref, gnw_ref, gdsk_ref, dstate):
        @pl.when(pl.program_id(1) == 0)
        def _():
            dstate[...] = jnp.zeros_like(dstate)
            gnw_ref[...] = jnp.zeros_like(gnw_ref)
            gdsk_ref[...] = jnp.zeros_like(gdsk_ref)

        last_row = lax.broadcasted_iota(jnp.int32, (q, 1), 0) == q - 1
        lane = lax.broadcasted_iota(jnp.int32, (q, LANES), 1)
        lane1 = lax.broadcasted_iota(jnp.int32, (8, LANES), 1)
        lower, upper = _head_tri(q, True), _head_tri(q, False)
        bd_mask = _block_diag_mask()
        for gi in range(gps):
            g = gps * pl.program_id(0) + gi
            cols = slice(GROUP_W * gi, GROUP_W * (gi + 1))
            scols = slice(D_STATE * gi, D_STATE * (gi + 1))
            y = y_ref[:, cols]
            z = z_ref[:, cols]
            sz = _sigmoid(z)
            silu_z = z * sz
            yg = y * silu_z
            rstd = lax.rsqrt(jnp.mean(yg * yg, axis=-1, keepdims=True) + EPS)
            n = yg * rstd
            dout = dmix_ref[:, cols]
            gnw_ref[:, cols] += jnp.sum(dout * n, axis=0, keepdims=True)
            dn = dout * nw_ref[:, cols]
            dyg = rstd * (dn - n * jnp.mean(dn * n, axis=-1, keepdims=True))
            dy = dyg * silu_z
            dz_ref[:, cols] = _bf(dyg * y * (sz * (1.0 + z * (1.0 - sz))))

            x = xs_ref[:, cols]
            bmb, cmb = _bf(b_ref[:, scols]), _bf(c_ref[:, scols])
            hv = _group_heads(g, gi, dt_ref[...], acs_ref[...], acst_ref, dsk_ref[...])
            dec = jnp.exp(jnp.where(lower, hv.acs - hv.acs_row, NEG))
            dect = jnp.exp(jnp.where(upper, hv.acs_row - hv.acs, NEG))
            b4 = jnp.concatenate([bmb] * HPG, axis=0)
            c4 = jnp.concatenate([cmb] * HPG, axis=0)
            m_all = _nt(cmb, b4) * dec
            mt_all = _nt(bmb, c4) * dect
            xdt = x * hv.dt
            xdt_b, dyb = _bf(xdt), _bf(dy)
            x_bd, dy_bd = _block_diag(xdt_b, bd_mask), _block_diag(dyb, bd_mask)
            s_prev = st_ref[0, gi]
            spb = _bf(s_prev)
            ds_new = dstate[gi]
            dsb = _bf(ds_new)
            e = jnp.exp(hv.acs)
            elast = jnp.exp(hv.acs_last)
            dte = jnp.exp(hv.acs_last - hv.acs)
            bds = _mm(bmb, dsb)
            dxdt = _mm(_bf(mt_all), dy_bd) + bds * dte
            dm = _nt(dyb, x_bd)
            dmt = _nt(xdt_b, dy_bd)
            dye = _bf(dy * e)
            dc_ref[:, scols] = _mm(_bf(dm * dec), b4) + _nt(dye, spb)
            db_ref[:, scols] = _mm(_bf(dmt * dect), c4) + _nt(_bf(xdt * dte), dsb)
            dstate[gi] = elast * ds_new + _tn(cmb, dye)
            dxs_ref[:, cols] = dxdt * hv.dt + hv.dsk * dy
            ddte_dte = bds * xdt * dte
            dacs_l = dm * m_all - dmt * mt_all + dy * _mm(cmb, spb) * e - ddte_dte
            dlast_l = (jnp.sum(ddte_dte, axis=0, keepdims=True)
                       + jnp.sum(s_prev * ds_new, axis=0, keepdims=True) * elast)
            ddt_l = dxdt * x
            gdsk_l = jnp.sum(dy * x, axis=0, keepdims=True)
            dacs_out = jnp.zeros((q, LANES), F32)
            ddt_out = jnp.zeros((q, LANES), F32)
            gdsk = jnp.zeros((8, LANES), F32)
            for r in range(HPG):
                dacs = _head_sums(dacs_l, r) + jnp.where(last_row, _head_sums(dlast_l, r), 0.0)
                dacs_out = jnp.where(lane == r, dacs, dacs_out)
                ddt_out = jnp.where(lane == r, _head_sums(ddt_l, r), ddt_out)
                gdsk = gdsk + jnp.where(lane1 == r, _head_sums(gdsk_l, r), 0.0)
            dacs_ref[:, LANES * gi:LANES * (gi + 1)] = dacs_out
            ddt_ref[:, LANES * gi:LANES * (gi + 1)] = ddt_out
            gdsk_ref[gi] += gdsk

    rev = lambda c: nc - 1 - c
    wide = pl.BlockSpec((q, gw), lambda g, c: (rev(c), g))
    return _pallas(
        body, name="ssd_bwd", grid=(GROUPS // gps, nc),
        in_specs=[wide, wide, wide, pl.BlockSpec((1, gw), lambda g, c: (0, g)), wide,
                  pl.BlockSpec((q, sw), lambda g, c: (rev(c), D_SSD // sw + g)),
                  pl.BlockSpec((q, sw), lambda g, c: (rev(c), (D_SSD + GROUPS * D_STATE) // sw + g)),
                  pl.BlockSpec((q, LANES), lambda g, c: (rev(c), 0)), pl.BlockSpec((q, LANES), lambda g, c: (rev(c), 0)),
                  pl.BlockSpec((1, gps * GROUPS, q), lambda g, c: (rev(c), g, 0)),
                  pl.BlockSpec((1, gps, D_STATE, GROUP_W), lambda g, c: (rev(c), g, 0, 0)),
                  pl.BlockSpec((1, LANES), lambda g, c: (0, 0))],
        out_specs=[wide, wide,
                   pl.BlockSpec((q, sw), lambda g, c: (rev(c), g)), pl.BlockSpec((q, sw), lambda g, c: (rev(c), g)),
                   pl.BlockSpec((q, gps * LANES), lambda g, c: (rev(c), g)),
                   pl.BlockSpec((q, gps * LANES), lambda g, c: (rev(c), g)),
                   pl.BlockSpec((1, gw), lambda g, c: (0, g)), pl.BlockSpec((gps, 8, LANES), lambda g, c: (g, 0, 0))],
        out_shape=[jax.ShapeDtypeStruct((t, D_SSD), BF16), jax.ShapeDtypeStruct((t, D_SSD), F32),
                   jax.ShapeDtypeStruct((t, GROUPS * D_STATE), F32), jax.ShapeDtypeStruct((t, GROUPS * D_STATE), F32),
                   jax.ShapeDtypeStruct((t, GROUPS * LANES), F32), jax.ShapeDtypeStruct((t, GROUPS * LANES), F32),
                   jax.ShapeDtypeStruct((1, D_SSD), F32), jax.ShapeDtypeStruct((GROUPS, 8, LANES), F32)],
        scratch_shapes=[pltpu.VMEM((gps, D_STATE, GROUP_W), F32)],
        compiler_params=_cp("parallel", "arbitrary"))(dmix, y_ssd, proj, ssd_norm_w, xbc, xbc, xbc, dt, acs, acst,
                                                      states, d_skip_l)


def _dt_bwd(dacs_g, ddt_g, dt, proj, dt_bias_l, a_log_l):
    t = dt.shape[0]
    q = CHUNK
    nc = t // q
    cps = _chunks_per_step(nc)
    rows = cps * q

    def body(dacs_ref, ddt_ref, dt_ref, raw_ref, bias_ref, alog_ref, draw_ref, ga_ref, gb_ref):
        @pl.when(pl.program_id(0) == 0)
        def _():
            ga_ref[...] = jnp.zeros_like(ga_ref)
            gb_ref[...] = jnp.zeros_like(gb_ref)

        lane = lax.broadcasted_iota(jnp.int32, (q, LANES), 1)
        ri = lax.broadcasted_iota(jnp.int32, (q, q), 0)
        ci = lax.broadcasted_iota(jnp.int32, (q, q), 1)
        triu = (ri <= ci).astype(F32)
        a = -jnp.exp(alog_ref[...])
        used = (lane & (GROUPS - 1)) < HPG
        ga = jnp.zeros((1, LANES), F32)
        gb = jnp.zeros((1, LANES), F32)
        for k in range(cps):
            rk = slice(q * k, q * (k + 1))
            dacs = jnp.zeros((q, LANES), F32)
            ddt = jnp.zeros((q, LANES), F32)
            for g in range(GROUPS):
                mask = (lane >= GROUPS * g) & (lane < GROUPS * g + HPG)
                sl = slice(LANES * g, LANES * (g + 1))
                if g == 0:
                    dacs = jnp.where(mask, dacs_ref[rk, sl], dacs)
                    ddt = jnp.where(mask, ddt_ref[rk, sl], ddt)
                else:
                    dacs = jnp.where(mask, pltpu.roll(dacs_ref[rk, sl], GROUPS * g, 1), dacs)
                    ddt = jnp.where(mask, pltpu.roll(ddt_ref[rk, sl], GROUPS * g, 1), ddt)
            dda = jnp.dot(triu, dacs, preferred_element_type=F32, precision=HI)
            row = pl.program_id(0) * rows + q * k + lax.broadcasted_iota(jnp.int32, (q, LANES), 0)
            dsp = jnp.where((row >= PAD_LEAD) & used, dda * a + ddt, 0.0)
            draw = dsp * _sigmoid(raw_ref[rk, :] + bias_ref[...])
            draw_ref[rk, :] = _bf(draw)
            gb = gb + jnp.sum(draw, axis=0, keepdims=True)
            ga = ga + jnp.sum(jnp.where(used, dda * dt_ref[rk, :], 0.0), axis=0, keepdims=True)
        gb_ref[0:1, :] += gb
        ga_ref[0:1, :] += ga * a

    return _pallas(
        body, name="dt_bwd", grid=(nc // cps,),
        in_specs=[pl.BlockSpec((rows, GROUPS * LANES), lambda c: (c, 0)),
                  pl.BlockSpec((rows, GROUPS * LANES), lambda c: (c, 0)),
                  pl.BlockSpec((rows, LANES), lambda c: (c, 0)), pl.BlockSpec((rows, LANES), lambda c: (c, ODT // LANES)),
                  pl.BlockSpec((1, LANES), lambda c: (0, 0)), pl.BlockSpec((1, LANES), lambda c: (0, 0))],
        out_specs=[pl.BlockSpec((rows, LANES), lambda c: (c, 0)), pl.BlockSpec((8, LANES), lambda c: (0, 0)),
                   pl.BlockSpec((8, LANES), lambda c: (0, 0))],
        out_shape=[jax.ShapeDtypeStruct((t, LANES), BF16), jax.ShapeDtypeStruct((8, LANES), F32),
                   jax.ShapeDtypeStruct((8, LANES), F32)],
        compiler_params=_cp("arbitrary"))(dacs_g, ddt_g, dt, proj, dt_bias_l, a_log_l)


def _conv_bwd(dseg, proj, conv_w, conv_b, col_off, name):
    t, width = dseg.shape
    tc = 128
    rt = _tile(t, 320)
    off_p = (OXS + col_off) // tc
    off_w = col_off // tc

    def body(d_ref, x_ref, w_ref, b_ref, dx_ref, gw_ref, gb_ref, xp, dup):
        xp[0:8, :] = jnp.zeros((8, tc), F32)
        xp[8:t + 8, :] = x_ref[...]
        dup[t:t + 8, :] = jnp.zeros((8, tc), F32)
        w = w_ref[...]
        bias = b_ref[...]

        def first(i, acc):
            r0 = pl.multiple_of(i * rt, 8)
            xs = [xp[pl.ds(r0 + 5 + k, rt), :] for k in range(CONV_WIDTH)]
            u = bias + w[3:4, :] * xs[3] + w[2:3, :] * xs[2] + w[1:2, :] * xs[1] + w[0:1, :] * xs[0]
            su = 0.5 + 0.5 * jnp.tanh(0.5 * u)
            du = d_ref[pl.ds(r0, rt), :] * (su * (1.0 + u * (1.0 - su)))
            dup[pl.ds(r0, rt), :] = du
            return tuple(acc[k] + jnp.sum(du * xs[k], axis=0, keepdims=True) for k in range(CONV_WIDTH)) + (
                acc[CONV_WIDTH] + jnp.sum(du, axis=0, keepdims=True),)

        zero = jnp.zeros((1, tc), F32)
        acc = lax.fori_loop(0, t // rt, first, (zero,) * (CONV_WIDTH + 1))
        gw_ref[...] = jnp.concatenate(acc[:CONV_WIDTH], axis=0)
        gb_ref[...] = acc[CONV_WIDTH]

        def second(i, carry):
            r0 = pl.multiple_of(i * rt, 16)
            dx_ref[pl.ds(r0, rt), :] = _bf(w[3:4, :] * dup[pl.ds(r0, rt), :] + w[2:3, :] * dup[pl.ds(r0 + 1, rt), :]
                                          + w[1:2, :] * dup[pl.ds(r0 + 2, rt), :] + w[0:1, :] * dup[pl.ds(r0 + 3, rt), :])
            return carry

        lax.fori_loop(0, t // rt, second, 0)

    return _pallas(
        body, name=name, grid=(width // tc,),
        in_specs=[pl.BlockSpec((t, tc), lambda j: (0, j)), pl.BlockSpec((t, tc), lambda j: (0, j + off_p)),
                  pl.BlockSpec((CONV_WIDTH, tc), lambda j: (0, j + off_w)), pl.BlockSpec((1, tc), lambda j: (0, j + off_w))],
        out_specs=[pl.BlockSpec((t, tc), lambda j: (0, j)), pl.BlockSpec((CONV_WIDTH, tc), lambda j: (0, j)),
                   pl.BlockSpec((1, tc), lambda j: (0, j))],
        out_shape=[jax.ShapeDtypeStruct((t, width), BF16), jax.ShapeDtypeStruct((CONV_WIDTH, width), F32),
                   jax.ShapeDtypeStruct((1, width), F32)],
        scratch_shapes=[pltpu.VMEM((t + 8, tc), F32), pltpu.VMEM((t + 8, tc), F32)],
        compiler_params=_cp("parallel"))(dseg, proj, conv_w, conv_b)


def _dinproj(segs, w_re, hpad, norm_w, dy_t, ga):
    t = segs[0].shape[0]
    d = hpad.shape[1]
    tm, tk = _tile(t, 416), SEG_TILE
    counts = [s.shape[1] // tk for s in segs]
    firsts = [sum(counts[:s]) for s in range(len(segs))]
    nk = sum(counts)
    assert nk * tk == w_re.shape[1]
    ni = t // tm
    ns = len(segs)

    def body(*refs):
        seg_refs = refs[:ns]
        w_ref, h_ref, nw_ref, dy_ref, ga_ref, dh_ref, gnw_ref, got_ref, acc, send_sems, recv_sems = refs[ns:]
        i, k = pl.program_id(0), pl.program_id(1)

        @pl.when((i == 0) & (k == 0))
        def _():
            for cp in _exchange_copies(ga_ref, got_ref, send_sems, recv_sems):
                cp.start()
            gnw_ref[...] = jnp.zeros_like(gnw_ref)

        @pl.when(k == 0)
        def _():
            acc[...] = jnp.zeros_like(acc)

        for s in range(ns):
            @pl.when((k >= firsts[s]) & (k < firsts[s] + counts[s]))
            def _(s=s):
                acc[...] += _nt(seg_refs[s][...], w_ref[...])

        @pl.when(k == nk - 1)
        def _():
            h = h_ref[...]
            rstd = lax.rsqrt(jnp.mean(h * h, axis=-1, keepdims=True) + EPS)
            nrm = h * rstd
            dhn = acc[...]
            gnw_ref[...] += jnp.sum(dhn * nrm, axis=0, keepdims=True)
            dn = dhn * nw_ref[...]
            dh_ref[...] = rstd * (dn - nrm * jnp.mean(dn * nrm, axis=-1, keepdims=True)) + dy_ref[...]

        @pl.when((i == ni - 1) & (k == nk - 1))
        def _():
            for cp in _exchange_copies(ga_ref, got_ref, send_sems, recv_sems):
                cp.wait()

    seg_specs = [pl.BlockSpec((tm, tk), functools.partial(lambda i, k, f0, n0: (i, jnp.clip(k - f0, 0, n0 - 1)),
                                                          f0=firsts[s], n0=counts[s])) for s in range(ns)]
    return _pallas(
        body, name="dinproj", grid=(ni, nk),
        in_specs=seg_specs + [pl.BlockSpec((d, tk), lambda i, k: (0, k)),
                              pl.BlockSpec((tm, d), lambda i, k: (i, 0)), pl.BlockSpec((1, d), lambda i, k: (0, 0)),
                              pl.BlockSpec((tm, d), lambda i, k: (i, 0)), ANY],
        out_specs=[pl.BlockSpec((tm, d), lambda i, k: (i, 0)), pl.BlockSpec((1, d), lambda i, k: (0, 0)), ANY],
        out_shape=[jax.ShapeDtypeStruct((t, d), F32), jax.ShapeDtypeStruct((1, d), F32), _exchange_shape(ga)],
        scratch_shapes=[pltpu.VMEM((tm, d), F32)] + _exchange_scratch(),
        compiler_params=_cp("arbitrary", "arbitrary"))(*segs, w_re, hpad, norm_w, dy_t, ga)


def _spread_heads(v):
    v = jnp.pad(v.reshape(GROUPS, HPG), ((0, 0), (0, GROUPS - HPG))).reshape(1, GROUPS * GROUPS)
    return jnp.pad(v, ((0, 0), (0, LANES - GROUPS * GROUPS)))


def _gather_heads(v):
    return v[0:1, :GROUPS * GROUPS].reshape(GROUPS, GROUPS)[:, :HPG].reshape(1, SSD_HEADS)


def _rope_tables(t):
    half = HEAD_DIM // 2
    inv = ROPE_THETA ** (-jnp.arange(half, dtype=F32) / half)
    pos = (jnp.arange(t) - PAD_LEAD).astype(F32)
    ang = pos[:, None] * inv[None, :]
    cos, sin = jnp.cos(ang), jnp.sin(ang)
    cos_t = jnp.concatenate([cos, cos, cos, cos], axis=1)
    sin_t = jnp.concatenate([-sin, sin, -sin, sin], axis=1)
    return cos_t, sin_t


def _column_pieces():
    runs = [(0, OB + 2 * GROUPS * D_STATE, 0)]
    o = OB + 2 * GROUPS * D_STATE
    runs += [(o + HPG * g, HPG, ODT + GROUPS * g) for g in range(GROUPS)]
    o += SSD_HEADS
    for width, dst in ((D_ATT, OQ), (D_KV, OK), (D_KV, OV), (D_ATT, OG)):
        runs.append((o, width, dst))
        o += width
    assert o == D_IN
    pieces = []
    for o0, width, dst in runs:
        for j in range(N_SHARD):
            lo, hi = max(o0, W_IN_SHARD * j), min(o0 + width, W_IN_SHARD * (j + 1))
            if lo < hi:
                pieces.append((j, lo - W_IN_SHARD * j, hi - W_IN_SHARD * j, dst + lo - o0))
    return pieces


def _shards_to_re(w_all):
    _, k, _ = w_all.shape
    tr = 256

    def body(x_ref, o_ref):
        o_ref[:, ODT:ODT + DT_SLAB] = jnp.zeros((tr, DT_SLAB), o_ref.dtype)
        for j, c0, c1, d0 in _column_pieces():
            o_ref[:, d0:d0 + c1 - c0] = x_ref[j, :, c0:c1]

    return _pallas(body, name="shards_to_re", grid=(k // tr,),
                   in_specs=[pl.BlockSpec((N_SHARD, tr, W_IN_SHARD), lambda i: (0, i, 0))],
                   out_specs=pl.BlockSpec((tr, N_RE), lambda i: (i, 0)),
                   out_shape=jax.ShapeDtypeStruct((k, N_RE), w_all.dtype), compiler_params=_cp("parallel"))(w_all)


def _pair_add_to_shards(parts, got, pieces, shard_rows, core, name):
    n = parts[0].shape[1]
    hn = n // 2
    tc = 128
    nt = hn // tc
    ns = len(parts)
    starts = [sum(p.shape[0] for p in parts[:s]) for s in range(ns)]
    moves = []
    for j, c0, c1, d0 in pieces:
        for s, p in enumerate(parts):
            lo, hi = max(d0, starts[s]), min(d0 + c1 - c0, starts[s] + p.shape[0])
            if lo < hi:
                moves.append((s, lo - starts[s], j, c0 + lo - d0, hi - lo))
    assert sum(m[4] for m in moves) == N_SHARD * shard_rows

    def body(core_ref, *refs):
        own, theirs, o_ref, acc = refs[:ns], refs[ns:2 * ns], refs[2 * ns], refs[2 * ns + 1]
        for s, r0, j, c0, rows in moves:
            acc[j, c0:c0 + rows, :] = own[s][r0:r0 + rows, :] + theirs[s][r0:r0 + rows, :]
        o_ref[...] = _bf(acc[...])

    return _pallas(
        body, name=name,
        grid_spec=pltpu.PrefetchScalarGridSpec(
            num_scalar_prefetch=1, grid=(nt,),
            in_specs=[pl.BlockSpec((p.shape[0], tc), lambda i, core_ref: (0, core_ref[0] * nt + i)) for p in parts]
            + [pl.BlockSpec((p.shape[0], tc), lambda i, core_ref: (0, i)) for p in parts],
            out_specs=pl.BlockSpec((N_SHARD, shard_rows, tc), lambda i, core_ref: (0, 0, i)),
            scratch_shapes=[pltpu.VMEM((N_SHARD, shard_rows, tc), F32)]),
        out_shape=jax.ShapeDtypeStruct((N_SHARD, shard_rows, hn), BF16),
        compiler_params=_cp("parallel"))(core, *parts, *got)


def _local_step(x, target, meta, norm_pre_w, w_re, conv_w, conv_b, dt_bias, a_log, d_skip, ssd_norm_w, sinks,
                w_out_shard, norm_post_w, place):
    seq = x.shape[0]
    t = PAD_LEAD + N_META + seq
    hpad = jnp.concatenate([jnp.zeros((PAD_LEAD, D_MODEL), F32), meta, x], axis=0)
    dt_bias_l, a_log_l, d_skip_l = _spread_heads(dt_bias), _spread_heads(a_log), _spread_heads(d_skip)
    cos_t, sin_t = _rope_tables(t)
    sink_v = sinks.reshape(Q_HEADS)

    proj, hn, w_out_all = _inproj(hpad, norm_pre_w, w_re, w_out_shard)
    w_out = w_out_all.reshape(D_MIX, D_MODEL)
    xbc = _conv_fwd(proj, conv_w, conv_b)
    dt, acs, acst = _dt_prep(proj, dt_bias_l, a_log_l)
    y_ssd, ymix, states = _ssd_fwd(xbc, proj, dt, acs, acst, d_skip_l, ssd_norm_w)
    qr, kr = _rope(proj, OQ, proj, OK, cos_t, sin_t)
    amix = _attn_fwd(qr, kr, proj, sink_v)
    out = _outproj(ymix, amix, w_out)
    dout, dy_t, loss_blk, g_norm_post = _post_loss(out, x, target, norm_post_w)

    g_out_y = _tn_matmul(ymix, dout, "gw_out_y")
    g_out_a, got_y = _tn_matmul(amix, dout, "gw_out_a", carry=g_out_y)
    dmix, got_a = _nt_matmul(dout, w_out, "dmix", carry=g_out_a)
    ga_out = _reduce_pair([g_out_y, g_out_a], [got_y, got_a], [(j, 0, W_OUT_SHARD, W_OUT_SHARD * j) for j in range(N_SHARD)],
                          W_OUT_SHARD, place, "gw_out")
    dq_r, dg, dk_r, dv, gs, slabs_out = _attn_bwd(qr, kr, proj, dmix, sink_v, ga_out)
    g_w_out = _reduce_finish(ga_out, slabs_out, place, "gw_out")
    dq, dk = _rope(dq_r, 0, dk_r, 0, cos_t, -sin_t)
    dz, dxs, db, dc, dacs_g, ddt_g, g_ssd_norm, gdsk = _ssd_bwd(dmix, y_ssd, xbc, proj, dt, acs, acst, states,
                                                                d_skip_l, ssd_norm_w)
    draw, ga, gb = _dt_bwd(dacs_g, ddt_g, dt, proj, dt_bias_l, a_log_l)
    dxs_p, gcw0, gcb0 = _conv_bwd(dxs, proj, conv_w, conv_b, 0, "conv_bwd_x")
    db_p, gcw1, gcb1 = _conv_bwd(db, proj, conv_w, conv_b, D_SSD, "conv_bwd_b")
    dc_p, gcw2, gcb2 = _conv_bwd(dc, proj, conv_w, conv_b, D_SSD + GROUPS * D_STATE, "conv_bwd_c")
    tail = jnp.concatenate([dk, _bf(dv), draw, jnp.zeros((t, DT_SLAB - LANES), BF16)], axis=1)
    segs = [dz, dxs_p, db_p, dc_p, dq, dg, tail]
    g_parts, got_parts = [_tn_matmul(segs[0], hn, "gw_in_0")], []
    for s in range(1, len(segs)):
        part, got = _tn_matmul(segs[s], hn, "gw_in_%d" % s, carry=g_parts[-1])
        g_parts.append(part)
        got_parts.append(got)
    ga_in = _reduce_pair(g_parts, got_parts, _column_pieces(), W_IN_SHARD, place, "gw_in")
    dh, g_norm_pre, slabs_in = _dinproj(segs, w_re, hpad, norm_pre_w, dy_t, ga_in)
    g_w_in_half = _chip_sum(ga_in, slabs_in, place, "gw_in_chip_sum")

    gdsk_l = jnp.concatenate([gdsk[g, 0:1, 0:GROUPS] for g in range(GROUPS)], axis=1)
    gdsk_l = jnp.pad(gdsk_l, ((0, 0), (0, LANES - GROUPS * GROUPS)))
    grads = dict(
        meta_tokens=dh[PAD_LEAD:ROW0], norm_pre_w=g_norm_pre, w_in_half=g_w_in_half,
        conv_w=jnp.concatenate([gcw0, gcw1, gcw2], axis=1), conv_b=jnp.concatenate([gcb0, gcb1, gcb2], axis=1),
        dt_bias=_gather_heads(gb), a_log=_gather_heads(ga), d_skip=_gather_heads(gdsk_l), ssd_norm_w=g_ssd_norm,
        attn_sinks=gs[0:1, :Q_HEADS], w_out=g_w_out, norm_post_w=g_norm_post)
    return loss_blk[0, 0], dh[ROW0:], grads


ANY = pl.BlockSpec(memory_space=pl.ANY)
MESH = pl.DeviceIdType.MESH
GATHER_CHUNKS = 4
PAIR_CHUNKS = 8
JOIN_CHUNKS = 8


def _rcopy(src, dst, ssem, rsem, dev):
    return pltpu.make_async_remote_copy(src_ref=src, dst_ref=dst, send_sem=ssem, recv_sem=rsem, device_id=dev,
                                        device_id_type=MESH)


def _place():
    x, y, c = lax.axis_index("x"), lax.axis_index("y"), lax.axis_index("c")
    chips = [(1 - x, y), (x, 1 - y), (1 - x, 1 - y)]
    return x, y, c, chips


def _gather_plan(x_ref, out_ref, send_sems, recv_sems, local_sems, hr, kc):
    ch = hr // kc
    assert ch * kc == hr and ch % 16 == 0
    x, y, c, chips = _place()
    me = 2 * x + y
    sibling = (x, y, 1 - c)

    def piece(chip, hc, k):
        return out_ref.at[chip, pl.ds(hc * hr + k * ch, ch), :]

    def local():
        return [pltpu.make_async_copy(x_ref.at[pl.ds(k * ch, ch), :], out_ref.at[me, pl.ds(k * ch, ch), :],
                                      local_sems.at[k]) for k in range(2 * kc)]

    def first():
        return [_rcopy(x_ref.at[pl.ds(c * hr + k * ch, ch), :], piece(me, c, k), send_sems.at[j * kc + k],
                       recv_sems.at[j * kc + k], (*chip, c)) for j, chip in enumerate(chips) for k in range(kc)]

    def passed(hc):
        return [_rcopy(piece(2 * chip[0] + chip[1], hc, k), piece(2 * chip[0] + chip[1], hc, k),
                       send_sems.at[(3 + j) * kc + k], recv_sems.at[(3 + j) * kc + k], sibling)
                for j, chip in enumerate(chips) for k in range(kc)]

    def arrivals():
        return [_rcopy(piece(2 * chip[0] + chip[1], c, k), piece(2 * chip[0] + chip[1], c, k), send_sems.at[j * kc + k],
                       recv_sems.at[j * kc + k], (*chip, c)) for j, chip in enumerate(chips) for k in range(kc)]

    def start():
        for cp in local() + first():
            cp.start()

    def forward():
        for arrived in arrivals():
            arrived.wait_recv()
        for fw in passed(c):
            fw.start()

    def finish():
        for cp in passed(1 - c):
            cp.wait_recv()
        for cp in first() + passed(c):
            cp.wait_send()
        for cp in local():
            cp.wait()

    return start, forward, finish


def _gather_shards(shard, name, kc, chip, small):
    r, n = shard.shape
    hr = r // 2
    qr = hr // 2
    ch = qr // kc
    assert ch * kc == qr and ch % 16 == 0
    nflow = 12
    tr = 256

    def body(x_ref, p_ref, out_ref, slots_ref, send_sems, recv_sems, *small_sems):
        start_small, wait_small = _chip_small_exchange(p_ref, slots_ref, *small_sems)
        start_small()
        x, y, c, _ = _place()
        me, cxn, cyn, cdg = 2 * x + y, 2 * (1 - x) + y, 2 * x + 1 - y, 2 * (1 - x) + 1 - y
        xn, yn, sibling = (1 - x, y, c), (x, 1 - y, c), (x, y, 1 - c)

        def piece(chip, hc, part, k):
            return out_ref.at[chip, pl.ds(hc * hr + part * qr + k * ch, ch), :]

        def own(part, k):
            return x_ref.at[pl.ds(c * hr + part * qr + k * ch, ch), :]

        def sems(flow, k):
            return send_sems.at[flow * kc + k], recv_sems.at[flow * kc + k]

        def arrival(flow, chip, hc, part, k):
            return _rcopy(piece(chip, hc, part, k), piece(chip, hc, part, k), *sems(flow, k), sibling)

        sends = []
        for flow, part, peer in ((0, 0, xn), (1, 1, yn), (2, 0, yn), (3, 1, xn)):
            sends += [_rcopy(own(part, k), piece(me, c, part, k), *sems(flow, k), peer) for k in range(kc)]
        for cp in sends:
            cp.start()
        landing = ((0, cxn, 0), (1, cyn, 1), (2, cyn, 0), (3, cxn, 1), (4, cdg, 0), (5, cdg, 1))
        for i, (flow, chip, part) in enumerate(landing):
            for k in range(kc):
                arrival(flow, chip, c, part, k).wait_recv()
                if flow < 2:
                    on = _rcopy(piece(chip, c, part, k), piece(chip, c, part, k), *sems(4 + flow, k),
                                yn if flow == 0 else xn)
                    on.start()
                    sends.append(on)
                fw = _rcopy(piece(chip, c, part, k), piece(chip, c, part, k), *sems(6 + i, k), sibling)
                fw.start()
                sends.append(fw)
        for i, (flow, chip, part) in enumerate(landing):
            for k in range(kc):
                arrival(6 + i, chip, 1 - c, part, k).wait_recv()
        for cp in sends:
            cp.wait_send()
        wait_small()

    full = jax.ShapeDtypeStruct((N_SHARD, r, n), shard.dtype)
    others, slots = _pallas(
        body, name=name, in_specs=[ANY, ANY], out_specs=[ANY, ANY],
        out_shape=[full, jax.ShapeDtypeStruct((N_SHARD,) + small.shape, F32)],
        scratch_shapes=[pltpu.SemaphoreType.DMA((nflow * kc,)), pltpu.SemaphoreType.DMA((nflow * kc,)),
                        pltpu.SemaphoreType.DMA((3,)), pltpu.SemaphoreType.DMA((3,)), pltpu.SemaphoreType.DMA])(
                            shard, small)

    def place(chip_ref, own_ref, all_ref, o_ref):
        o_ref[0] = own_ref[...]

    gathered = _pallas(
        place, name=name + "_own",
        grid_spec=pltpu.PrefetchScalarGridSpec(
            num_scalar_prefetch=1, grid=(r // tr,),
            in_specs=[pl.BlockSpec((tr, n), lambda i, chip_ref: (i, 0)), ANY],
            out_specs=pl.BlockSpec((1, tr, n), lambda i, chip_ref: (chip_ref[0], i, 0))),
        out_shape=full, input_output_aliases={2: 0}, compiler_params=_cp("parallel"))(chip, shard, others)
    return gathered, slots


def _pair_copies(src_ref, dst_ref, send_sems, recv_sems):
    hn = src_ref.shape[1] // 2
    cw = hn // PAIR_CHUNKS
    assert cw * PAIR_CHUNKS == hn and cw % LANES == 0
    x, y, c, _ = _place()
    return [_rcopy(src_ref.at[:, pl.ds((1 - c) * hn + k * cw, cw)], dst_ref.at[:, pl.ds(k * cw, cw)],
                   send_sems.at[k], recv_sems.at[k], (x, y, 1 - c)) for k in range(PAIR_CHUNKS)]


def _pair_send(parts, name):
    n = parts[0].shape[1]
    hn = n // 2
    kc = PAIR_CHUNKS
    cw = hn // kc
    assert cw * kc == hn and cw % LANES == 0
    ns = len(parts)

    def body(*refs):
        srcs, dsts, send_sems, recv_sems = refs[:ns], refs[ns:2 * ns], refs[2 * ns], refs[2 * ns + 1]
        x, y, c, _ = _place()
        cps = [_rcopy(srcs[s].at[:, pl.ds((1 - c) * hn + k * cw, cw)], dsts[s].at[:, pl.ds(k * cw, cw)],
                      send_sems.at[s * kc + k], recv_sems.at[s * kc + k], (x, y, 1 - c))
               for s in range(ns) for k in range(kc)]
        for cp in cps:
            cp.start()
        for cp in cps:
            cp.wait()

    return _pallas(
        body, name=name, in_specs=[ANY] * ns, out_specs=[ANY] * ns,
        out_shape=[jax.ShapeDtypeStruct((p.shape[0], hn), F32) for p in parts],
        scratch_shapes=[pltpu.SemaphoreType.DMA((ns * kc,)), pltpu.SemaphoreType.DMA((ns * kc,))])(*parts)


REDUCE_TILE = 256


def _exchange_copies(g_ref, got_ref, send_sems, recv_sems):
    hn = g_ref.shape[2]
    kc = GATHER_CHUNKS
    cw = hn // kc
    assert cw * kc == hn and cw % LANES == 0
    x, y, c, chips = _place()
    return [_rcopy(g_ref.at[2 * chip[0] + chip[1], :, pl.ds(k * cw, cw)], got_ref.at[j, :, pl.ds(k * cw, cw)],
                   send_sems.at[j * kc + k], recv_sems.at[j * kc + k], (*chip, c))
            for j, chip in enumerate(chips) for k in range(kc)]


def _exchange_scratch():
    return [pltpu.SemaphoreType.DMA((3 * GATHER_CHUNKS,)), pltpu.SemaphoreType.DMA((3 * GATHER_CHUNKS,))]


def _exchange_shape(ga):
    return jax.ShapeDtypeStruct((3,) + ga.shape[1:], ga.dtype)


def _chip_sum(ga, got, place, name):
    _, r, hn = ga.shape
    tc = REDUCE_TILE
    nt = hn // tc

    def body(place_ref, own_ref, got_ref, o_ref):
        acc = own_ref[0].astype(F32)
        for j in range(3):
            acc = acc + got_ref[j].astype(F32)
        o_ref[...] = acc

    return _pallas(
        body, name=name,
        grid_spec=pltpu.PrefetchScalarGridSpec(
            num_scalar_prefetch=1, grid=(nt,),
            in_specs=[pl.BlockSpec((1, r, tc), lambda i, place_ref: (place_ref[0], 0, i)),
                      pl.BlockSpec((3, r, tc), lambda i, place_ref: (0, 0, i))],
            out_specs=pl.BlockSpec((r, tc), lambda i, place_ref: (0, place_ref[1] * nt + i))),
        out_shape=jax.ShapeDtypeStruct((r, 2 * hn), F32), compiler_params=_cp("parallel"))(place, ga, got)


def _pair_join(buf, name, small=None):
    r, n = buf.shape
    hn = n // 2
    kc = JOIN_CHUNKS
    cw = hn // kc
    assert cw * kc == hn and cw % LANES == 0

    def body(in_ref, *refs):
        if small is None:
            out_ref, send_sems, recv_sems = refs
        else:
            p_ref, out_ref, slots_ref, send_sems, recv_sems = refs[:5]
            start_small, wait_small = _small_exchange(p_ref, slots_ref, *refs[5:])
            start_small()
        x, y, c, _ = _place()
        cps = [_rcopy(out_ref.at[:, pl.ds(c * hn + k * cw, cw)], out_ref.at[:, pl.ds(c * hn + k * cw, cw)],
                      send_sems.at[k], recv_sems.at[k], (x, y, 1 - c)) for k in range(kc)]
        for cp in cps:
            cp.start()
        for k in range(kc):
            cols = out_ref.at[:, pl.ds((1 - c) * hn + k * cw, cw)]
            _rcopy(cols, cols, send_sems.at[k], recv_sems.at[k], (x, y, 1 - c)).wait_recv()
        for cp in cps:
            cp.wait_send()
        if small is not None:
            wait_small()

    sems = [pltpu.SemaphoreType.DMA((kc,)), pltpu.SemaphoreType.DMA((kc,))]
    if small is None:
        return _pallas(body, name=name, in_specs=[ANY], out_specs=ANY, out_shape=jax.ShapeDtypeStruct((r, n), F32),
                       input_output_aliases={0: 0}, scratch_shapes=sems)(buf)
    return _pallas(
        body, name=name, in_specs=[ANY, ANY], out_specs=[ANY, ANY],
        out_shape=[jax.ShapeDtypeStruct((r, n), F32), jax.ShapeDtypeStruct((N_DEV,) + small.shape, F32)],
        input_output_aliases={0: 0}, scratch_shapes=sems + _small_scratch())(buf, small)


def _reduce_pair(parts, got, pieces, shard_rows, place, tag):
    if len(got) < len(parts):
        got = list(got) + list(_pair_send(parts[len(got):], tag + "_pair_send"))
    return _pair_add_to_shards(parts, got, pieces, shard_rows, place[1:2], tag + "_pair_add")


def _reduce_finish(ga, slabs, place, tag):
    return _pair_join(_chip_sum(ga, slabs, place, tag + "_chip_sum"), tag + "_pair_join")


N_DEV = 8


def _small_exchange(p_ref, slots_ref, send_sems, recv_sems, local_sem):
    x, y, c, _ = _place()
    my = 4 * x + 2 * y + c

    def sends():
        return [_rcopy(p_ref, slots_ref.at[my], send_sems.at[k - 1], recv_sems.at[k - 1],
                       (x ^ ((k >> 2) & 1), y ^ ((k >> 1) & 1), c ^ (k & 1))) for k in range(1, N_DEV)]

    def local():
        return pltpu.make_async_copy(p_ref, slots_ref.at[my], local_sem)

    def start():
        local().start()
        for cp in sends():
            cp.start()

    def wait():
        for k in range(1, N_DEV):
            _rcopy(p_ref, slots_ref.at[my ^ k], send_sems.at[k - 1], recv_sems.at[k - 1], (x, y, c)).wait_recv()
        for cp in sends():
            cp.wait_send()
        local().wait()

    return start, wait


def _chip_small_exchange(p_ref, slots_ref, send_sems, recv_sems, local_sem):
    x, y, c, chips = _place()
    me = 2 * x + y

    def sends():
        return [_rcopy(p_ref, slots_ref.at[me], send_sems.at[j], recv_sems.at[j], (*chip, c))
                for j, chip in enumerate(chips)]

    def local():
        return pltpu.make_async_copy(p_ref, slots_ref.at[me], local_sem)

    def start():
        local().start()
        for cp in sends():
            cp.start()

    def wait():
        for j, chip in enumerate(chips):
            slot = slots_ref.at[2 * chip[0] + chip[1]]
            _rcopy(slot, slot, send_sems.at[j], recv_sems.at[j], (*chip, c)).wait_recv()
        for cp in sends():
            cp.wait_send()
        local().wait()

    return start, wait


def _small_scratch():
    return [pltpu.SemaphoreType.DMA((N_DEV - 1,)), pltpu.SemaphoreType.DMA((N_DEV - 1,)), pltpu.SemaphoreType.DMA]


def _sum_slots(slots, name):
    _, rows, n = slots.shape

    def body(s_ref, o_ref):
        acc = s_ref[0]
        for j in range(1, N_DEV):
            acc = acc + s_ref[j]
        o_ref[...] = acc

    vm = pl.BlockSpec(memory_space=pltpu.VMEM)
    return _pallas(body, name=name, in_specs=[vm], out_specs=vm, out_shape=jax.ShapeDtypeStruct((rows, n), F32))(slots)


def _adamw(w, g, m, v, name):
    r, n = w.shape
    tr = _tile(r, 256, 8)
    c1 = 1.0 / (1.0 - ADAM_B1 ** ADAM_STEP)
    c2 = 1.0 / (1.0 - ADAM_B2 ** ADAM_STEP)

    def body(w_ref, g_ref, m_ref, v_ref, d_ref, mo_ref, vo_ref):
        gv = g_ref[...]
        mn = ADAM_B1 * m_ref[...] + (1.0 - ADAM_B1) * gv
        vn = ADAM_B2 * v_ref[...] + (1.0 - ADAM_B2) * (gv * gv)
        d_ref[...] = -ADAM_LR * ((mn * c1) / (jnp.sqrt(vn * c2) + ADAM_EPS) + ADAM_WD * w_ref[...])
        mo_ref[...] = mn
        vo_ref[...] = vn

    spec = pl.BlockSpec((tr, n), lambda i: (i, 0))
    shp = jax.ShapeDtypeStruct((r, n), F32)
    return _pallas(body, name=name, grid=(r // tr,), in_specs=[spec] * 4, out_specs=[spec] * 3, out_shape=[shp] * 3,
                   compiler_params=_cp("parallel"))(w, g, m, v)


PACK_W = 1024
SMALL_REPL = ("norm_pre_w", "conv_b", "ssd_norm_w", "norm_post_w")
SMALL_HEAD = ("dt_bias", "a_log", "d_skip", "attn_sinks")


def _rows(a):
    return a.reshape(-1, PACK_W)


def _head_row(vals, extra=None):
    parts = [vals[n].reshape(1, -1) for n in SMALL_HEAD]
    if extra is not None:
        parts.append(extra.reshape(1, 1))
    row = jnp.concatenate(parts, axis=1)
    return jnp.pad(row, ((0, 0), (0, PACK_W - row.shape[1])))


def _pad_rows(a, rows):
    return jnp.pad(a, ((0, rows - a.shape[0]), (0, 0)))


def _pack_repl(vals, extra=None):
    body = jnp.concatenate([_rows(vals[n]) for n in SMALL_REPL] + [_head_row(vals, extra)], axis=0)
    return _pad_rows(body, 16)


def _unpack_repl(buf):
    out, r = {}, 0
    for n, k in zip(SMALL_REPL, (2, 4, 2, 2)):
        out[n] = buf[r:r + k].reshape(1, k * PACK_W)
        r += k
    col = 0
    for n, k in zip(SMALL_HEAD, (32, 32, 32, 16)):
        out[n] = buf[r:r + 1, col:col + k]
        col += k
    return out, buf[r, col]


def kernel(x, meta_tokens, norm_pre_w, w_in, conv_w, conv_b, dt_bias, a_log, d_skip, ssd_norm_w, attn_sinks, w_out, norm_post_w, loss_target, m_meta_tokens, m_norm_pre_w, m_w_in, m_conv_w, m_conv_b, m_dt_bias, m_a_log, m_d_skip, m_ssd_norm_w, m_attn_sinks, m_w_out, m_norm_post_w, v_meta_tokens, v_norm_pre_w, v_w_in, v_conv_w, v_conv_b, v_dt_bias, v_a_log, v_d_skip, v_ssd_norm_w, v_attn_sinks, v_w_out, v_norm_post_w):
    names = ("meta_tokens", "norm_pre_w", "w_in", "conv_w", "conv_b", "dt_bias", "a_log", "d_skip", "ssd_norm_w",
             "attn_sinks", "w_out", "norm_post_w")
    w = dict(zip(names, (meta_tokens, norm_pre_w, w_in, conv_w, conv_b, dt_bias, a_log, d_skip, ssd_norm_w, attn_sinks,
                         w_out, norm_post_w)))
    m = dict(zip(names, (m_meta_tokens, m_norm_pre_w, m_w_in, m_conv_w, m_conv_b, m_dt_bias, m_a_log, m_d_skip,
                         m_ssd_norm_w, m_attn_sinks, m_w_out, m_norm_post_w)))
    v = dict(zip(names, (v_meta_tokens, v_norm_pre_w, v_w_in, v_conv_w, v_conv_b, v_dt_bias, v_a_log, v_d_skip,
                         v_ssd_norm_w, v_attn_sinks, v_w_out, v_norm_post_w)))
    cx, cy, cc = lax.axis_index("x"), lax.axis_index("y"), lax.axis_index("c")
    chip = 2 * cx + cy
    meta_cols = D_MODEL // N_SHARD
    conv_cols = D_CONV // N_SHARD

    place = jnp.stack([chip, cc]).astype(jnp.int32)
    small = jnp.concatenate([_pad_rows(conv_w[0], 8), _rows(meta_tokens)], axis=0)
    w_in_all, small_all = _gather_shards(_bf(w_in[0]), "gather_w_in", GATHER_CHUNKS, place[0:1], small)
    w_re = _shards_to_re(w_in_all)
    conv_full = jnp.transpose(small_all[:, 0:CONV_WIDTH], (1, 0, 2)).reshape(CONV_WIDTH, D_CONV)
    meta_full = jnp.transpose(small_all[:, 8:16].reshape(N_SHARD, N_META, meta_cols), (1, 0, 2)).reshape(N_META, D_MODEL)

    loss_dev, grad_x, g = _local_step(x[0], loss_target[0], meta_full, norm_pre_w, w_re, conv_full, conv_b, dt_bias,
                                      a_log, d_skip, ssd_norm_w, attn_sinks, _bf(w_out[0]), norm_post_w, place)
    g_w_out = g["w_out"]

    packed = jnp.concatenate([_rows(g["conv_w"]), _rows(g["meta_tokens"]), _pack_repl(g, loss_dev)], axis=0)
    g_w_in, slots = _pair_join(g["w_in_half"], "gw_in_pair_join", small=packed)
    red = _sum_slots(slots, "reduce_small")
    g_conv_full = red[0:16].reshape(CONV_WIDTH, D_CONV)
    g_meta_full = red[16:48].reshape(N_META, D_MODEL)
    g_small, loss = _unpack_repl(red[48:64])
    grads = dict(g_small)
    grads["w_in"] = g_w_in
    grads["w_out"] = g_w_out
    grads["conv_w"] = lax.dynamic_slice(g_conv_full, (0, chip * conv_cols), (CONV_WIDTH, conv_cols))
    grads["meta_tokens"] = lax.dynamic_slice(g_meta_full, (0, chip * meta_cols), (N_META, meta_cols))

    upd = {}
    upd["w_in"] = [jnp.swapaxes(a, 0, 1) for a in _adamw(jnp.swapaxes(w_in[0], 0, 1), g_w_in, jnp.swapaxes(m_w_in[0], 0, 1),
                                                         jnp.swapaxes(v_w_in[0], 0, 1), "adamw_w_in")]
    grads["w_in"] = jnp.swapaxes(g_w_in, 0, 1)
    upd["w_out"] = _adamw(w_out[0], g_w_out, m_w_out[0], v_w_out[0], "adamw_w_out")

    def pack_small(vals, conv, meta):
        return jnp.concatenate([_pad_rows(conv.reshape(CONV_WIDTH, conv_cols), 8), _rows(meta), _pack_repl(vals)], axis=0)

    sm = _adamw(pack_small(w, w["conv_w"], w["meta_tokens"]), pack_small(grads, grads["conv_w"], grads["meta_tokens"]),
                pack_small(m, m["conv_w"], m["meta_tokens"]), pack_small(v, v["conv_w"], v["meta_tokens"]),
                "adamw_small")
    for n in names:
        if n not in ("w_in", "w_out"):
            upd[n] = [None, None, None]
    for k, buf in enumerate(sm):
        upd["conv_w"][k] = buf[0:CONV_WIDTH]
        upd["meta_tokens"][k] = buf[8:16].reshape(N_META, meta_cols)
        rest, _ = _unpack_repl(buf[16:32])
        for n in SMALL_REPL + SMALL_HEAD:
            upd[n][k] = rest[n]

    def shaped(n, a):
        return a.reshape(w[n].shape)

    outs = [loss, grad_x[None]]
    outs += [shaped(n, grads[n]) for n in names]
    for k in range(3):
        outs += [shaped(n, upd[n][k]) for n in names]
    return tuple(outs)
```

```python
import functools

import jax
import jax.numpy as jnp
from jax import lax
from jax.experimental import pallas as pl
from jax.experimental.pallas import tpu as pltpu

F32 = jnp.float32
BF16 = jnp.bfloat16

D_MODEL = 2048
CHUNK = 64
N_META = 16
PAD_LEAD = CHUNK - N_META
ROW0 = PAD_LEAD + N_META
EPS = 1e-6
SSD_HEADS = 32
HEAD_DIM = 64
GROUPS = 8
HPG = SSD_HEADS // GROUPS
D_STATE = 128
D_SSD = 2048
GROUP_W = D_SSD // GROUPS
CONV_WIDTH = 4
D_CONV = 4096
Q_HEADS = 16
KV_HEADS = 4
REP = Q_HEADS // KV_HEADS
D_ATT = 1024
D_KV = 256
BAND_CHUNKS = 3
ROPE_THETA = 10000.0
D_MIX = D_SSD + D_ATT
D_IN = 8736
N_SHARD = 4
W_IN_SHARD = D_IN // N_SHARD
W_OUT_SHARD = D_MIX // N_SHARD

OZ, OXS, OB, OC, OQ, OG, OK, OV, ODT = 0, 2048, 4096, 5120, 6144, 7168, 8192, 8448, 8704
DT_SLAB = 512
N_RE = ODT + DT_SLAB
LANES = 128

ADAM_LR, ADAM_B1, ADAM_B2, ADAM_EPS, ADAM_WD, ADAM_STEP = 0.001, 0.9, 0.999, 1e-08, 0.01, 10

SSD_FWD_GROUPS_PER_STEP = 4
SSD_BWD_GROUPS_PER_STEP = 8
SEG_TILE = 1024
VMEM_LIMIT = 52 * 1024 * 1024
NEG = -1e30
HI = lax.Precision.HIGHEST


def _pallas(body, **kw):
    return pl.pallas_call(body, **kw)


def _cp(*sem):
    return pltpu.CompilerParams(dimension_semantics=sem, vmem_limit_bytes=VMEM_LIMIT)


def _tile(n, cap, mult=16):
    best = None
    for d in range(mult, min(n, cap) + 1, mult):
        if n % d == 0:
            best = d
    assert best is not None, (n, cap)
    return best


def _nt(a, b):
    return lax.dot_general(a, b, (((1,), (1,)), ((), ())), preferred_element_type=F32)


def _tn(a, b):
    return lax.dot_general(a, b, (((0,), (0,)), ((), ())), preferred_element_type=F32)


def _mm(a, b):
    return jnp.dot(a, b, preferred_element_type=F32)


def _sigmoid(x):
    return 1.0 / (1.0 + jnp.exp(-x))


def _bf(x):
    return x.astype(BF16)


def _inproj(hpad, norm_w, w_re, w_out_shard):
    t, d = hpad.shape
    n = w_re.shape[1]
    tm, tn = _tile(t, 1040), 1024
    ni, nj = t // tm, n // tn
    r_out, n_out = w_out_shard.shape
    kc = GATHER_CHUNKS

    def body(h_ref, nw_ref, w_ref, ws_ref, proj_ref, hn_ref, wall_ref, hn_s, send_sems, recv_sems, local_sems):
        i, j = pl.program_id(0), pl.program_id(1)
        start, forward, finish = _gather_plan(ws_ref, wall_ref, send_sems, recv_sems, local_sems, r_out // 2, kc)
        pl.when((i == 0) & (j == 0))(start)
        pl.when((i == ni // 2) & (j == 0))(forward)

        @pl.when(j == 0)
        def _():
            h = h_ref[...]
            ms = jnp.mean(h * h, axis=-1, keepdims=True)
            hn = _bf(h * lax.rsqrt(ms + EPS) * nw_ref[...])
            hn_s[...] = hn
            hn_ref[...] = hn
        proj_ref[...] = _mm(hn_s[...], w_ref[...])
        pl.when((i == ni - 1) & (j == nj - 1))(finish)

    return _pallas(
        body, name="inproj", grid=(ni, nj),
        in_specs=[pl.BlockSpec((tm, d), lambda i, j: (i, 0)), pl.BlockSpec((1, d), lambda i, j: (0, 0)),
                  pl.BlockSpec((d, tn), lambda i, j: (0, j)), ANY],
        out_specs=[pl.BlockSpec((tm, tn), lambda i, j: (i, j)), pl.BlockSpec((tm, d), lambda i, j: (i, 0)), ANY],
        out_shape=[jax.ShapeDtypeStruct((t, n), F32), jax.ShapeDtypeStruct((t, d), BF16),
                   jax.ShapeDtypeStruct((N_SHARD, r_out, n_out), w_out_shard.dtype)],
        scratch_shapes=[pltpu.VMEM((tm, d), BF16), pltpu.SemaphoreType.DMA((6 * kc,)), pltpu.SemaphoreType.DMA((6 * kc,)),
                        pltpu.SemaphoreType.DMA((2 * kc,))],
        compiler_params=_cp("arbitrary", "arbitrary"))(hpad, norm_w, w_re, w_out_shard)


def _conv_fwd(proj, conv_w, conv_b):
    t = proj.shape[0]
    tc = 256
    off = OXS // tc

    def body(x_ref, w_ref, b_ref, o_ref):
        x = x_ref[...]
        w = w_ref[...]
        row = lax.broadcasted_iota(jnp.int32, (t, tc), 0)
        u = b_ref[...] + w[3:4, :] * x
        for k in range(1, CONV_WIDTH):
            u = u + w[3 - k:4 - k, :] * jnp.where(row >= k, pltpu.roll(x, k, 0), 0.0)
        h = 0.5 * u
        o_ref[...] = h + h * jnp.tanh(h)

    return _pallas(
        body, name="conv_fwd", grid=(D_CONV // tc,),
        in_specs=[pl.BlockSpec((t, tc), lambda j: (0, j + off)), pl.BlockSpec((CONV_WIDTH, tc), lambda j: (0, j)),
                  pl.BlockSpec((1, tc), lambda j: (0, j))],
        out_specs=pl.BlockSpec((t, tc), lambda j: (0, j)),
        out_shape=jax.ShapeDtypeStruct((t, D_CONV), F32),
        compiler_params=_cp("parallel"))(proj, conv_w, conv_b)


def _softplus(u):
    e = jnp.exp(-jnp.abs(u))
    w = 1.0 + e
    l1p = jnp.where(w == 1.0, e, jnp.log(w) * (e / jnp.where(w == 1.0, 1.0, w - 1.0)))
    return jnp.maximum(u, 0.0) + l1p


def _chunks_per_step(nc):
    return max(d for d in range(1, 14) if nc % d == 0)


def _dt_prep(proj, dt_bias_l, a_log_l):
    t = proj.shape[0]
    nc = t // CHUNK
    q = CHUNK
    cps = _chunks_per_step(nc)
    rows = cps * q

    def body(raw_ref, bias_ref, alog_ref, dt_ref, acs_ref, acst_ref):
        ri = lax.broadcasted_iota(jnp.int32, (q, q), 0)
        ci = lax.broadcasted_iota(jnp.int32, (q, q), 1)
        tri = (ri >= ci).astype(F32)
        neg_a = -jnp.exp(alog_ref[...])
        for k in range(cps):
            rk = slice(q * k, q * (k + 1))
            sp = _softplus(raw_ref[rk, :] + bias_ref[...])
            row = pl.program_id(0) * rows + q * k + lax.broadcasted_iota(jnp.int32, (q, LANES), 0)
            dt = jnp.where(row >= PAD_LEAD, sp, 0.0)
            acs = jnp.dot(tri, dt * neg_a, preferred_element_type=F32, precision=HI)
            dt_ref[rk, :] = dt
            acs_ref[rk, :] = acs
            acst_ref[k] = acs.T

    return _pallas(
        body, name="dt_prep", grid=(nc // cps,),
        in_specs=[pl.BlockSpec((rows, LANES), lambda c: (c, ODT // LANES)), pl.BlockSpec((1, LANES), lambda c: (0, 0)),
                  pl.BlockSpec((1, LANES), lambda c: (0, 0))],
        out_specs=[pl.BlockSpec((rows, LANES), lambda c: (c, 0)), pl.BlockSpec((rows, LANES), lambda c: (c, 0)),
                   pl.BlockSpec((cps, LANES, q), lambda c: (c, 0, 0))],
        out_shape=[jax.ShapeDtypeStruct((t, LANES), F32), jax.ShapeDtypeStruct((t, LANES), F32),
                   jax.ShapeDtypeStruct((nc, LANES, q), F32)],
        compiler_params=_cp("parallel"))(proj, dt_bias_l, a_log_l)


def _head_cols(blk, idx):
    lane = lax.broadcasted_iota(jnp.int32, blk.shape, 1)
    return jnp.sum(jnp.where(lane == idx, blk, 0.0), axis=1, keepdims=True)


class _HeadVals:
    pass


def _lane_head(shape):
    return lax.broadcasted_iota(jnp.int32, shape, len(shape) - 1) >> 6


def _group_heads(g, gi, dtb, acsb, acst_ref, dskb):
    q = dtb.shape[0]
    hv = _HeadVals()
    lh = _lane_head((1, GROUP_W))
    hv.dt = jnp.zeros((q, GROUP_W), F32)
    hv.acs = jnp.zeros((q, GROUP_W), F32)
    hv.acs_last = jnp.zeros((1, GROUP_W), F32)
    hv.dsk = jnp.zeros((1, GROUP_W), F32)
    rows = []
    for r in range(HPG):
        idx = GROUPS * g + r
        sel = lh == r
        acs_r = acst_ref[0, GROUPS * gi + r:GROUPS * gi + r + 1, :]
        rows.append(acs_r)
        hv.dt = jnp.where(sel, _head_cols(dtb, idx), hv.dt)
        hv.acs = jnp.where(sel, _head_cols(acsb, idx), hv.acs)
        hv.acs_last = jnp.where(sel, acs_r[:, q - 1:q], hv.acs_last)
        hv.dsk = jnp.where(sel, _head_cols(dskb, idx), hv.dsk)
    hv.acs_row = jnp.concatenate(rows, axis=1)
    return hv


def _head_tri(q, lower):
    ri = lax.broadcasted_iota(jnp.int32, (q, GROUP_W), 0)
    li = lax.broadcasted_iota(jnp.int32, (q, GROUP_W), 1) & (HEAD_DIM - 1)
    return ri >= li if lower else ri <= li


def _block_diag_mask():
    rb = lax.broadcasted_iota(jnp.int32, (GROUP_W, GROUP_W), 0) >> 6
    cb = lax.broadcasted_iota(jnp.int32, (GROUP_W, GROUP_W), 1) >> 6
    return rb == cb


def _block_diag(v, mask):
    return jnp.where(mask, jnp.concatenate([v] * HPG, axis=0), jnp.zeros((), v.dtype))


def _head_sums(v, r):
    return jnp.sum(jnp.where(_lane_head((1, GROUP_W)) == r, v, 0.0), axis=1, keepdims=True)


def _ssd_fwd(xbc, proj, dt, acs, acst, d_skip_l, ssd_norm_w):
    t = xbc.shape[0]
    q = CHUNK
    nc = t // q

    gps = SSD_FWD_GROUPS_PER_STEP
    gw, sw = gps * GROUP_W, gps * D_STATE

    def body(xs_ref, b_ref, c_ref, dt_ref, acs_ref, acst_ref, z_ref, dsk_ref, nw_ref,
             y_ref, ymix_ref, st_ref, state):
        @pl.when(pl.program_id(1) == 0)
        def _():
            state[...] = jnp.zeros_like(state)

        lower = _head_tri(q, True)
        bd_mask = _block_diag_mask()
        for gi in range(gps):
            g = gps * pl.program_id(0) + gi
            cols = slice(GROUP_W * gi, GROUP_W * (gi + 1))
            x = xs_ref[:, cols]
            bmb = _bf(b_ref[:, D_STATE * gi:D_STATE * (gi + 1)])
            cmb = _bf(c_ref[:, D_STATE * gi:D_STATE * (gi + 1)])
            hv = _group_heads(g, gi, dt_ref[...], acs_ref[...], acst_ref, dsk_ref[...])
            decay = jnp.exp(jnp.where(lower, hv.acs - hv.acs_row, NEG))
            m_all = _bf(_nt(cmb, jnp.concatenate([bmb] * HPG, axis=0)) * decay)
            xdt = x * hv.dt
            s_prev = state[gi]
            st_ref[0, gi] = s_prev
            y = (_mm(m_all, _block_diag(_bf(xdt), bd_mask)) + _mm(cmb, _bf(s_prev)) * jnp.exp(hv.acs) + hv.dsk * x)
            state[gi] = jnp.exp(hv.acs_last) * s_prev + _tn(bmb, _bf(xdt * jnp.exp(hv.acs_last - hv.acs)))
            y_ref[:, cols] = y
            z = z_ref[:, cols]
            yg = y * (z * _sigmoid(z))
            ms = jnp.mean(yg * yg, axis=-1, keepdims=True)
            ymix_ref[:, cols] = _bf(yg * lax.rsqrt(ms + EPS) * nw_ref[:, cols])

    return _pallas(
        body, name="ssd_fwd", grid=(GROUPS // gps, nc),
        in_specs=[pl.BlockSpec((q, gw), lambda g, c: (c, g)),
                  pl.BlockSpec((q, sw), lambda g, c: (c, D_SSD // sw + g)),
                  pl.BlockSpec((q, sw), lambda g, c: (c, (D_SSD + GROUPS * D_STATE) // sw + g)),
                  pl.BlockSpec((q, LANES), lambda g, c: (c, 0)), pl.BlockSpec((q, LANES), lambda g, c: (c, 0)),
                  pl.BlockSpec((1, gps * GROUPS, q), lambda g, c: (c, g, 0)),
                  pl.BlockSpec((q, gw), lambda g, c: (c, g)),
                  pl.BlockSpec((1, LANES), lambda g, c: (0, 0)), pl.BlockSpec((1, gw), lambda g, c: (0, g))],
        out_specs=[pl.BlockSpec((q, gw), lambda g, c: (c, g)), pl.BlockSpec((q, gw), lambda g, c: (c, g)),
                   pl.BlockSpec((1, gps, D_STATE, GROUP_W), lambda g, c: (c, g, 0, 0))],
        out_shape=[jax.ShapeDtypeStruct((t, D_SSD), F32), jax.ShapeDtypeStruct((t, D_SSD), BF16),
                   jax.ShapeDtypeStruct((nc, GROUPS, D_STATE, GROUP_W), F32)],
        scratch_shapes=[pltpu.VMEM((gps, D_STATE, GROUP_W), F32)],
        compiler_params=_cp("parallel", "arbitrary"))(xbc, xbc, xbc, dt, acs, acst, proj, d_skip_l, ssd_norm_w)


def _swap_halves(v):
    lane = lax.broadcasted_iota(jnp.int32, v.shape, 1)
    return jnp.where((lane & (HEAD_DIM - 1)) < HEAD_DIM // 2, pltpu.roll(v, LANES - HEAD_DIM // 2, 1),
                     pltpu.roll(v, HEAD_DIM // 2, 1))


def _rope(qsrc, q_off, ksrc, k_off, cos_t, sin_t):
    t = qsrc.shape[0]
    tr = _tile(t, 832)
    q_scale = HEAD_DIM ** -0.5

    def body(q_ref, k_ref, cos_ref, sin_ref, qo_ref, ko_ref):
        cs = cos_ref[...]
        sn = sin_ref[...]
        for src, dst, width, scale in ((q_ref, qo_ref, D_ATT, q_scale), (k_ref, ko_ref, D_KV, 1.0)):
            for s in range(width // LANES):
                v = src[:, LANES * s:LANES * (s + 1)].astype(F32)
                dst[:, LANES * s:LANES * (s + 1)] = _bf((v * cs + _swap_halves(v) * sn) * scale)

    return _pallas(
        body, name="rope", grid=(t // tr,),
        in_specs=[pl.BlockSpec((tr, D_ATT), lambda i: (i, q_off // D_ATT)),
                  pl.BlockSpec((tr, D_KV), lambda i: (i, k_off // D_KV)),
                  pl.BlockSpec((tr, LANES), lambda i: (i, 0)), pl.BlockSpec((tr, LANES), lambda i: (i, 0))],
        out_specs=[pl.BlockSpec((tr, D_ATT), lambda i: (i, 0)), pl.BlockSpec((tr, D_KV), lambda i: (i, 0))],
        out_shape=[jax.ShapeDtypeStruct((t, D_ATT), BF16), jax.ShapeDtypeStruct((t, D_KV), BF16)],
        compiler_params=_cp("parallel"))(qsrc, ksrc, cos_t, sin_t)


def _attn_chunks_per_step(nc):
    return max(d for d in range(1, 6) if nc % d == 0)


def _band(ref, c):
    return [ref[pl.ds(pl.multiple_of(jnp.maximum(c - j, 0) * CHUNK, CHUNK), CHUNK), :] for j in (2, 1, 0)]


def _attn_probs(qh, kb, sink_col, valid):
    s = jnp.where(valid, _nt(qh, kb), NEG)
    m = jnp.maximum(jnp.max(s, axis=1, keepdims=True), sink_col)
    p = jnp.exp(s - m)
    psink = jnp.exp(sink_col - m)
    return p, psink, 1.0 / (jnp.sum(p, axis=1, keepdims=True) + psink)


def _attn_operands(c, q, k_refs, v_refs, sink_ref, h):
    qh = jnp.concatenate([q[:, HEAD_DIM * (REP * h + r):HEAD_DIM * (REP * h + r + 1)] for r in range(REP)], axis=0)
    kb = jnp.concatenate([k[:, HEAD_DIM * h:HEAD_DIM * (h + 1)] for k in k_refs], axis=0)
    vb = jnp.concatenate([_bf(v[:, HEAD_DIM * h:HEAD_DIM * (h + 1)]) for v in v_refs], axis=0)
    rows = lax.broadcasted_iota(jnp.int32, (REP * CHUNK, 1), 0) >> 6
    sink_col = jnp.zeros((REP * CHUNK, 1), F32)
    for r in range(REP):
        sink_col = jnp.where(rows == r, sink_ref[REP * h + r], sink_col)
    key_abs = (c - (BAND_CHUNKS - 1)) * CHUNK + lax.broadcasted_iota(jnp.int32, (1, BAND_CHUNKS * CHUNK), 1)
    return qh, kb, vb, sink_col, key_abs >= PAD_LEAD


def _attn_fwd(qr, kr, proj, sinks):
    t = qr.shape[0]
    nc = t // CHUNK
    cps = _attn_chunks_per_step(nc)
    rows = cps * CHUNK

    def body(q_ref, k_ref, v_ref, g_ref, sink_ref, o_ref):
        for j in range(cps):
            c = pl.program_id(0) * cps + j
            rj = slice(CHUNK * j, CHUNK * (j + 1))
            ks, vs = _band(k_ref, c), _band(v_ref, c)
            q = q_ref[rj, :]
            outs = []
            for h in range(KV_HEADS):
                qh, kb, vb, sink_col, valid = _attn_operands(c, q, ks, vs, sink_ref, h)
                p, _, inv = _attn_probs(qh, kb, sink_col, valid)
                o = _mm(_bf(p), vb) * inv
                outs += [o[CHUNK * r:CHUNK * (r + 1)] for r in range(REP)]
            att = jnp.concatenate(outs, axis=1)
            gate = g_ref[rj, :]
            o_ref[rj, :] = _bf(att * (gate * _sigmoid(gate)))

    return _pallas(
        body, name="attn_fwd", grid=(nc // cps,),
        in_specs=[pl.BlockSpec((rows, D_ATT), lambda i: (i, 0)), pl.BlockSpec((t, D_KV), lambda i: (0, 0)),
                  pl.BlockSpec((t, D_KV), lambda i: (0, OV // D_KV)),
                  pl.BlockSpec((rows, D_ATT), lambda i: (i, OG // D_ATT)), pl.BlockSpec(memory_space=pltpu.SMEM)],
        out_specs=pl.BlockSpec((rows, D_ATT), lambda i: (i, 0)),
        out_shape=jax.ShapeDtypeStruct((t, D_ATT), BF16),
        compiler_params=_cp("parallel"))(qr, kr, proj, proj, sinks)


def _outproj(ymix, amix, w_out):
    t = ymix.shape[0]
    tm, tn = _tile(t, 832), 1024

    def body(y_ref, a_ref, wy_ref, wa_ref, o_ref):
        o_ref[...] = _mm(y_ref[...], wy_ref[...]) + _mm(a_ref[...], wa_ref[...])

    return _pallas(
        body, name="outproj", grid=(t // tm, D_MODEL // tn),
        in_specs=[pl.BlockSpec((tm, D_SSD), lambda i, j: (i, 0)), pl.BlockSpec((tm, D_ATT), lambda i, j: (i, 0)),
                  pl.BlockSpec((D_SSD, tn), lambda i, j: (0, j)),
                  pl.BlockSpec((D_ATT, tn), lambda i, j: (D_SSD // D_ATT, j))],
        out_specs=pl.BlockSpec((tm, tn), lambda i, j: (i, j)),
        out_shape=jax.ShapeDtypeStruct((t, D_MODEL), F32),
        compiler_params=_cp("parallel", "parallel"))(ymix, amix, w_out, w_out)


def _post_loss(out, x, target, norm_post_w):
    t = out.shape[0]
    nc = t // CHUNK
    cps = _attn_chunks_per_step(nc)
    rows = cps * CHUNK

    def body(o_ref, *refs):
        x_refs, tg_refs = refs[:cps], refs[cps:2 * cps]
        nw_ref, dout_ref, dy_ref, loss_ref, gnw_ref = refs[2 * cps:]
        i = pl.program_id(0)

        @pl.when(i == 0)
        def _():
            loss_ref[...] = jnp.zeros_like(loss_ref)
            gnw_ref[...] = jnp.zeros_like(gnw_ref)

        nw = nw_ref[...]
        loss = jnp.zeros((), F32)
        gnw = jnp.zeros((1, D_MODEL), F32)
        for k in range(cps):
            rk = slice(CHUNK * k, CHUNK * (k + 1))
            frames = i * cps + k > 0
            o = o_ref[rk, :]
            rstd = lax.rsqrt(jnp.mean(o * o, axis=-1, keepdims=True) + EPS)
            n = o * rstd
            err = jnp.where(frames, x_refs[k][...] + n * nw - tg_refs[k][...], 0.0)
            loss = loss + jnp.sum(err * err)
            dy = err * (1.0 / D_MODEL)
            dy_ref[rk, :] = dy
            gnw = gnw + jnp.sum(dy * n, axis=0, keepdims=True)
            dn = dy * nw
            dout_ref[rk, :] = _bf(rstd * (dn - n * jnp.mean(dn * n, axis=-1, keepdims=True)))
        loss_ref[...] += loss * (0.5 / D_MODEL)
        gnw_ref[...] += gnw

    lower = [pl.BlockSpec((CHUNK, D_MODEL), functools.partial(lambda i, k: (jnp.maximum(i * cps + k - 1, 0), 0), k=k))
             for k in range(cps)]
    return _pallas(
        body, name="post_loss", grid=(nc // cps,),
        in_specs=[pl.BlockSpec((rows, D_MODEL), lambda i: (i, 0))] + lower + lower
        + [pl.BlockSpec((1, D_MODEL), lambda i: (0, 0))],
        out_specs=[pl.BlockSpec((rows, D_MODEL), lambda i: (i, 0)), pl.BlockSpec((rows, D_MODEL), lambda i: (i, 0)),
                   pl.BlockSpec((8, LANES), lambda i: (0, 0)), pl.BlockSpec((1, D_MODEL), lambda i: (0, 0))],
        out_shape=[jax.ShapeDtypeStruct((t, D_MODEL), BF16), jax.ShapeDtypeStruct((t, D_MODEL), F32),
                   jax.ShapeDtypeStruct((8, LANES), F32), jax.ShapeDtypeStruct((1, D_MODEL), F32)],
        compiler_params=_cp("arbitrary"))(out, *([x] * cps), *([target] * cps), norm_post_w)


def _carried(grid, carry):
    if carry is None:
        return [], [], [], [], lambda refs: None, lambda refs: None
    hn = carry.shape[1] // 2

    def at(ids, which):
        cond = None
        for d, size in enumerate(grid):
            here = pl.program_id(d) == (0 if which == "first" else size - 1)
            cond = here if cond is None else cond & here
        return cond

    def start(refs):
        @pl.when(at(grid, "first"))
        def _():
            for cp in _pair_copies(*refs):
                cp.start()

    def finish(refs):
        @pl.when(at(grid, "last"))
        def _():
            for cp in _pair_copies(*refs):
                cp.wait()

    return ([ANY], [ANY], [jax.ShapeDtypeStruct((carry.shape[0], hn), F32)],
            [pltpu.SemaphoreType.DMA((PAIR_CHUNKS,)), pltpu.SemaphoreType.DMA((PAIR_CHUNKS,))], start, finish)


def _nt_matmul(a, b, name, carry=None):
    t, k = a.shape
    n = b.shape[0]
    tm, tn = _tile(t, 832), 1024
    grid = (t // tm, n // tn)
    cin, cout, cshape, cscratch, start, finish = _carried(grid, carry)

    def body(a_ref, b_ref, *refs):
        o_ref = refs[len(cin)]
        comm = (refs[0], refs[2], refs[3], refs[4]) if carry is not None else None
        start(comm)
        o_ref[...] = _nt(a_ref[...], b_ref[...])
        finish(comm)

    res = _pallas(
        body, name=name, grid=grid,
        in_specs=[pl.BlockSpec((tm, k), lambda i, j: (i, 0)), pl.BlockSpec((tn, k), lambda i, j: (j, 0))] + cin,
        out_specs=[pl.BlockSpec((tm, tn), lambda i, j: (i, j))] + cout,
        out_shape=[jax.ShapeDtypeStruct((t, n), F32)] + cshape, scratch_shapes=cscratch,
        compiler_params=_cp("arbitrary", "arbitrary"))(a, b, *([carry] if carry is not None else []))
    return res if carry is not None else res[0]


def _tn_matmul(a, b, name, carry=None):
    t, m = a.shape
    n = b.shape[1]
    tk, tm, tn = _tile(t, 832), min(m, 2048), min(n, 2048)
    nk = t // tk
    grid = (m // tm, n // tn, nk)
    cin, cout, cshape, cscratch, start, finish = _carried(grid, carry)

    def body(a_ref, b_ref, *refs):
        o_ref = refs[len(cin)]
        comm = (refs[0], refs[2], refs[3], refs[4]) if carry is not None else None
        start(comm)

        @pl.when(pl.program_id(2) == 0)
        def _():
            o_ref[...] = jnp.zeros_like(o_ref)
        o_ref[...] += _tn(a_ref[...], b_ref[...])
        finish(comm)

    res = _pallas(
        body, name=name, grid=grid,
        in_specs=[pl.BlockSpec((tk, tm), lambda i, j, k: (k, i)), pl.BlockSpec((tk, tn), lambda i, j, k: (k, j))] + cin,
        out_specs=[pl.BlockSpec((tm, tn), lambda i, j, k: (i, j))] + cout,
        out_shape=[jax.ShapeDtypeStruct((m, n), F32)] + cshape, scratch_shapes=cscratch,
        compiler_params=_cp("arbitrary", "arbitrary", "arbitrary"))(a, b, *([carry] if carry is not None else []))
    return res if carry is not None else res[0]


def _attn_bwd(qr, kr, proj, dmix, sinks, ga):
    t = qr.shape[0]
    nc = t // CHUNK
    cps = _attn_chunks_per_step(nc)
    nsteps = nc // cps
    rows_step = cps * CHUNK

    def body(q_ref, k_ref, v_ref, g_ref, da_ref, sink_ref, ga_ref, dq_ref, dg_ref, dk_ref, dv_ref, gs_ref,
             got_ref, send_sems, recv_sems):
        step = pl.program_id(0)

        @pl.when(step == 0)
        def _():
            for cp in _exchange_copies(ga_ref, got_ref, send_sems, recv_sems):
                cp.start()
            dk_ref[...] = jnp.zeros_like(dk_ref)
            dv_ref[...] = jnp.zeros_like(dv_ref)
            gs_ref[...] = jnp.zeros_like(gs_ref)

        lane = lax.broadcasted_iota(jnp.int32, (1, LANES), 1)
        rows = lax.broadcasted_iota(jnp.int32, (REP * CHUNK, 1), 0) >> 6
        gs = jnp.zeros((1, LANES), F32)
        dk_parts = [[] for _ in range(cps + BAND_CHUNKS - 1)]
        dv_parts = [[] for _ in range(cps + BAND_CHUNKS - 1)]
        for j in range(cps):
            c = step * cps + j
            rj = slice(CHUNK * j, CHUNK * (j + 1))
            ks, vs = _band(k_ref, c), _band(v_ref, c)
            q = q_ref[rj, :]
            gate = g_ref[rj, :]
            sg = _sigmoid(gate)
            da = da_ref[rj, :]
            datt = da * (gate * sg)
            dqs, atts, dks, dvs = [], [], [], []
            for h in range(KV_HEADS):
                qh, kb, vb, sink_col, valid = _attn_operands(c, q, ks, vs, sink_ref, h)
                p, psink, inv = _attn_probs(qh, kb, sink_col, valid)
                pb = _bf(p)
                o = _mm(pb, vb) * inv
                do = jnp.concatenate([datt[:, HEAD_DIM * (REP * h + r):HEAD_DIM * (REP * h + r + 1)]
                                      for r in range(REP)], axis=0)
                dob = _bf(do * inv)
                delta = jnp.sum(do * o, axis=1, keepdims=True) * inv
                ds = _bf(p * (_nt(dob, vb) - delta))
                gsink = -psink * delta
                for r in range(REP):
                    gs = gs + jnp.where(lane == REP * h + r, jnp.sum(jnp.where(rows == r, gsink, 0.0)), 0.0)
                dqh = _mm(ds, kb)
                dqs += [dqh[CHUNK * r:CHUNK * (r + 1)] for r in range(REP)]
                atts += [o[CHUNK * r:CHUNK * (r + 1)] for r in range(REP)]
                dks.append(_tn(ds, qh))
                dvs.append(_tn(pb, dob))
            dq_ref[rj, :] = jnp.concatenate(dqs, axis=1)
            att = jnp.concatenate(atts, axis=1)
            dg_ref[rj, :] = _bf(da * att * (sg * (1.0 + gate * (1.0 - sg))))
            dkf = jnp.concatenate(dks, axis=1)
            dvf = jnp.concatenate(dvs, axis=1)
            for b in range(BAND_CHUNKS):
                dk_parts[j + b].append(dkf[CHUNK * b:CHUNK * (b + 1)])
                dv_parts[j + b].append(dvf[CHUNK * b:CHUNK * (b + 1)])
        gs_ref[0:1, :] += gs
        for rel in range(cps + BAND_CHUNKS - 1):
            r0 = pl.multiple_of(jnp.maximum(step * cps - (BAND_CHUNKS - 1) + rel, 0) * CHUNK, CHUNK)
            dk_ref[pl.ds(r0, CHUNK), :] += sum(dk_parts[rel][1:], dk_parts[rel][0])
            dv_ref[pl.ds(r0, CHUNK), :] += sum(dv_parts[rel][1:], dv_parts[rel][0])

        @pl.when(step == nsteps - 1)
        def _():
            for cp in _exchange_copies(ga_ref, got_ref, send_sems, recv_sems):
                cp.wait()

    return _pallas(
        body, name="attn_bwd", grid=(nsteps,),
        in_specs=[pl.BlockSpec((rows_step, D_ATT), lambda i: (i, 0)), pl.BlockSpec((t, D_KV), lambda i: (0, 0)),
                  pl.BlockSpec((t, D_KV), lambda i: (0, OV // D_KV)),
                  pl.BlockSpec((rows_step, D_ATT), lambda i: (i, OG // D_ATT)),
                  pl.BlockSpec((rows_step, D_ATT), lambda i: (i, D_SSD // D_ATT)),
                  pl.BlockSpec(memory_space=pltpu.SMEM), ANY],
        out_specs=[pl.BlockSpec((rows_step, D_ATT), lambda i: (i, 0)), pl.BlockSpec((rows_step, D_ATT), lambda i: (i, 0)),
                   pl.BlockSpec((t, D_KV), lambda i: (0, 0)), pl.BlockSpec((t, D_KV), lambda i: (0, 0)),
                   pl.BlockSpec((8, LANES), lambda i: (0, 0)), ANY],
        out_shape=[jax.ShapeDtypeStruct((t, D_ATT), F32), jax.ShapeDtypeStruct((t, D_ATT), BF16),
                   jax.ShapeDtypeStruct((t, D_KV), F32), jax.ShapeDtypeStruct((t, D_KV), F32),
                   jax.ShapeDtypeStruct((8, LANES), F32), _exchange_shape(ga)],
        scratch_shapes=_exchange_scratch(),
        compiler_params=_cp("arbitrary"))(qr, kr, proj, proj, dmix, sinks, ga)


def _ssd_bwd(dmix, y_ssd, xbc, proj, dt, acs, acst, states, d_skip_l, ssd_norm_w):
    t = xbc.shape[0]
    q = CHUNK
    nc = t // q
    gps = SSD_BWD_GROUPS_PER_STEP
    gw, sw = gps * GROUP_W, gps * D_STATE

    def body(dmix_ref, y_ref, z_ref, nw_ref, xs_ref, b_ref, c_ref, dt_ref, acs_ref, acst_ref, st_ref, dsk_ref,
             dz_ref, dxs_ref, db_ref, dc_ref, dacs_ref, ddt_ref, gnw_ref, gdsk_ref, dstate):
        @pl.when(pl.program_id(1) == 0)
        def _():
            dstate[...] = jnp.zeros_like(dstate)
            gnw_ref[...] = jnp.zeros_like(gnw_ref)
            gdsk_ref[...] = jnp.zeros_like(gdsk_ref)

        last_row = lax.broadcasted_iota(jnp.int32, (q, 1), 0) == q - 1
        lane = lax.broadcasted_iota(jnp.int32, (q, LANES), 1)
        lane1 = lax.broadcasted_iota(jnp.int32, (8, LANES), 1)
        lower, upper = _head_tri(q, True), _head_tri(q, False)
        bd_mask = _block_diag_mask()
        for gi in range(gps):
            g = gps * pl.program_id(0) + gi
            cols = slice(GROUP_W * gi, GROUP_W * (gi + 1))
            scols = slice(D_STATE * gi, D_STATE * (gi + 1))
            y = y_ref[:, cols]
            z = z_ref[:, cols]
            sz = _sigmoid(z)
            silu_z = z * sz
            yg = y * silu_z
            rstd = lax.rsqrt(jnp.mean(yg * yg, axis=-1, keepdims=True) + EPS)
            n = yg * rstd
            dout = dmix_ref[:, cols]
            gnw_ref[:, cols] += jnp.sum(dout * n, axis=0, keepdims=True)
            dn = dout * nw_ref[:, cols]
            dyg = rstd * (dn - n * jnp.mean(dn * n, axis=-1, keepdims=True))
            dy = dyg * silu_z
            dz_ref[:, cols] = _bf(dyg * y * (sz * (1.0 + z * (1.0 - sz))))

            x = xs_ref[:, cols]
            bmb, cmb = _bf(b_ref[:, scols]), _bf(c_ref[:, scols])
            hv = _group_heads(g, gi, dt_ref[...], acs_ref[...], acst_ref, dsk_ref[...])
            dec = jnp.exp(jnp.where(lower, hv.acs - hv.acs_row, NEG))
            dect = jnp.exp(jnp.where(upper, hv.acs_row - hv.acs, NEG))
            b4 = jnp.concatenate([bmb] * HPG, axis=0)
            c4 = jnp.concatenate([cmb] * HPG, axis=0)
            m_all = _nt(cmb, b4) * dec
            mt_all = _nt(bmb, c4) * dect
            xdt = x * hv.dt
            xdt_b, dyb = _bf(xdt), _bf(dy)
            x_bd, dy_bd = _block_diag(xdt_b, bd_mask), _block_diag(dyb, bd_mask)
            s_prev = st_ref[0, gi]
            spb = _bf(s_prev)
            ds_new = dstate[gi]
            dsb = _bf(ds_new)
            e = jnp.exp(hv.acs)
            elast = jnp.exp(hv.acs_last)
            dte = jnp.exp(hv.acs_last - hv.acs)
            bds = _mm(bmb, dsb)
            dxdt = _mm(_bf(mt_all), dy_bd) + bds * dte
            dm = _nt(dyb, x_bd)
            dmt = _nt(xdt_b, dy_bd)
            dye = _bf(dy * e)
            dc_ref[:, scols] = _mm(_bf(dm * dec), b4) + _nt(dye, spb)
            db_ref[:, scols] = _mm(_bf(dmt * dect), c4) + _nt(_bf(xdt * dte), dsb)
            dstate[gi] = elast * ds_new + _tn(cmb, dye)
            dxs_ref[:, cols] = dxdt * hv.dt + hv.dsk * dy
            ddte_dte = bds * xdt * dte
            dacs_l = dm * m_all - dmt * mt_all + dy * _mm(cmb, spb) * e - ddte_dte
            dlast_l = (jnp.sum(ddte_dte, axis=0, keepdims=True)
                       + jnp.sum(s_prev * ds_new, axis=0, keepdims=True) * elast)
            ddt_l = dxdt * x
            gdsk_l = jnp.sum(dy * x, axis=0, keepdims=True)
            dacs_out = jnp.zeros((q, LANES), F32)
            ddt_out = jnp.zeros((q, LANES), F32)
            gdsk = jnp.zeros((8, LANES), F32)
            for r in range(HPG):
                dacs = _head_sums(dacs_l, r) + jnp.where(last_row, _head_sums(dlast_l, r), 0.0)
                dacs_out = jnp.where(lane == r, dacs, dacs_out)
                ddt_out = jnp.where(lane == r, _head_sums(ddt_l, r), ddt_out)
                gdsk = gdsk + jnp.where(lane1 == r, _head_sums(gdsk_l, r), 0.0)
            dacs_ref[:, LANES * gi:LANES * (gi + 1)] = dacs_out
            ddt_ref[:, LANES * gi:LANES * (gi + 1)] = ddt_out
            gdsk_ref[gi] += gdsk

    rev = lambda c: nc - 1 - c
    wide = pl.BlockSpec((q, gw), lambda g, c: (rev(c), g))
    return _pallas(
        body, name="ssd_bwd", grid=(GROUPS // gps, nc),
        in_specs=[wide, wide, wide, pl.BlockSpec((1, gw), lambda g, c: (0, g)), wide,
                  pl.BlockSpec((q, sw), lambda g, c: (rev(c), D_SSD // sw + g)),
                  pl.BlockSpec((q, sw), lambda g, c: (rev(c), (D_SSD + GROUPS * D_STATE) // sw + g)),
                  pl.BlockSpec((q, LANES), lambda g, c: (rev(c), 0)), pl.BlockSpec((q, LANES), lambda g, c: (rev(c), 0)),
                  pl.BlockSpec((1, gps * GROUPS, q), lambda g, c: (rev(c), g, 0)),
                  pl.BlockSpec((1, gps, D_STATE, GROUP_W), lambda g, c: (rev(c), g, 0, 0)),
                  pl.BlockSpec((1, LANES), lambda g, c: (0, 0))],
        out_specs=[wide, wide,
                   pl.BlockSpec((q, sw), lambda g, c: (rev(c), g)), pl.BlockSpec((q, sw), lambda g, c: (rev(c), g)),
                   pl.BlockSpec((q, gps * LANES), lambda g, c: (rev(c), g)),
                   pl.BlockSpec((q, gps * LANES), lambda g, c: (rev(c), g)),
                   pl.BlockSpec((1, gw), lambda g, c: (0, g)), pl.BlockSpec((gps, 8, LANES), lambda g, c: (g, 0, 0))],
        out_shape=[jax.ShapeDtypeStruct((t, D_SSD), BF16), jax.ShapeDtypeStruct((t, D_SSD), F32),
                   jax.ShapeDtypeStruct((t, GROUPS * D_STATE), F32), jax.ShapeDtypeStruct((t, GROUPS * D_STATE), F32),
                   jax.ShapeDtypeStruct((t, GROUPS * LANES), F32), jax.ShapeDtypeStruct((t, GROUPS * LANES), F32),
                   jax.ShapeDtypeStruct((1, D_SSD), F32), jax.ShapeDtypeStruct((GROUPS, 8, LANES), F32)],
        scratch_shapes=[pltpu.VMEM((gps, D_STATE, GROUP_W), F32)],
        compiler_params=_cp("parallel", "arbitrary"))(dmix, y_ssd, proj, ssd_norm_w, xbc, xbc, xbc, dt, acs, acst,
                                                      states, d_skip_l)


def _dt_bwd(dacs_g, ddt_g, dt, proj, dt_bias_l, a_log_l):
    t = dt.shape[0]
    q = CHUNK
    nc = t // q
    cps = _chunks_per_step(nc)
    rows = cps * q

    def body(dacs_ref, ddt_ref, dt_ref, raw_ref, bias_ref, alog_ref, draw_ref, ga_ref, gb_ref):
        @pl.when(pl.program_id(0) == 0)
        def _():
            ga_ref[...] = jnp.zeros_like(ga_ref)
            gb_ref[...] = jnp.zeros_like(gb_ref)

        lane = lax.broadcasted_iota(jnp.int32, (q, LANES), 1)
        ri = lax.broadcasted_iota(jnp.int32, (q, q), 0)
        ci = lax.broadcasted_iota(jnp.int32, (q, q), 1)
        triu = (ri <= ci).astype(F32)
        a = -jnp.exp(alog_ref[...])
        used = (lane & (GROUPS - 1)) < HPG
        ga = jnp.zeros((1, LANES), F32)
        gb = jnp.zeros((1, LANES), F32)
        for k in range(cps):
            rk = slice(q * k, q * (k + 1))
            dacs = jnp.zeros((q, LANES), F32)
            ddt = jnp.zeros((q, LANES), F32)
            for g in range(GROUPS):
                mask = (lane >= GROUPS * g) & (lane < GROUPS * g + HPG)
                sl = slice(LANES * g, LANES * (g + 1))
                if g == 0:
                    dacs = jnp.where(mask, dacs_ref[rk, sl], dacs)
                    ddt = jnp.where(mask, ddt_ref[rk, sl], ddt)
                else:
                    dacs = jnp.where(mask, pltpu.roll(dacs_ref[rk, sl], GROUPS * g, 1), dacs)
                    ddt = jnp.where(mask, pltpu.roll(ddt_ref[rk, sl], GROUPS * g, 1), ddt)
            dda = jnp.dot(triu, dacs, preferred_element_type=F32, precision=HI)
            row = pl.program_id(0) * rows + q * k + lax.broadcasted_iota(jnp.int32, (q, LANES), 0)
            dsp = jnp.where((row >= PAD_LEAD) & used, dda * a + ddt, 0.0)
            draw = dsp * _sigmoid(raw_ref[rk, :] + bias_ref[...])
            draw_ref[rk, :] = _bf(draw)
            gb = gb + jnp.sum(draw, axis=0, keepdims=True)
            ga = ga + jnp.sum(jnp.where(used, dda * dt_ref[rk, :], 0.0), axis=0, keepdims=True)
        gb_ref[0:1, :] += gb
        ga_ref[0:1, :] += ga * a

    return _pallas(
        body, name="dt_bwd", grid=(nc // cps,),
        in_specs=[pl.BlockSpec((rows, GROUPS * LANES), lambda c: (c, 0)),
                  pl.BlockSpec((rows, GROUPS * LANES), lambda c: (c, 0)),
                  pl.BlockSpec((rows, LANES), lambda c: (c, 0)), pl.BlockSpec((rows, LANES), lambda c: (c, ODT // LANES)),
                  pl.BlockSpec((1, LANES), lambda c: (0, 0)), pl.BlockSpec((1, LANES), lambda c: (0, 0))],
        out_specs=[pl.BlockSpec((rows, LANES), lambda c: (c, 0)), pl.BlockSpec((8, LANES), lambda c: (0, 0)),
                   pl.BlockSpec((8, LANES), lambda c: (0, 0))],
        out_shape=[jax.ShapeDtypeStruct((t, LANES), BF16), jax.ShapeDtypeStruct((8, LANES), F32),
                   jax.ShapeDtypeStruct((8, LANES), F32)],
        compiler_params=_cp("arbitrary"))(dacs_g, ddt_g, dt, proj, dt_bias_l, a_log_l)


def _conv_bwd(dseg, proj, conv_w, conv_b, col_off, name):
    t, width = dseg.shape
    tc = 128
    rt = _tile(t, 320)
    off_p = (OXS + col_off) // tc
    off_w = col_off // tc

    def body(d_ref, x_ref, w_ref, b_ref, dx_ref, gw_ref, gb_ref, xp, dup):
        xp[0:8, :] = jnp.zeros((8, tc), F32)
        xp[8:t + 8, :] = x_ref[...]
        dup[t:t + 8, :] = jnp.zeros((8, tc), F32)
        w = w_ref[...]
        bias = b_ref[...]

        def first(i, acc):
            r0 = pl.multiple_of(i * rt, 8)
            xs = [xp[pl.ds(r0 + 5 + k, rt), :] for k in range(CONV_WIDTH)]
            u = bias + w[3:4, :] * xs[3] + w[2:3, :] * xs[2] + w[1:2, :] * xs[1] + w[0:1, :] * xs[0]
            su = 0.5 + 0.5 * jnp.tanh(0.5 * u)
            du = d_ref[pl.ds(r0, rt), :] * (su * (1.0 + u * (1.0 - su)))
            dup[pl.ds(r0, rt), :] = du
            return tuple(acc[k] + jnp.sum(du * xs[k], axis=0, keepdims=True) for k in range(CONV_WIDTH)) + (
                acc[CONV_WIDTH] + jnp.sum(du, axis=0, keepdims=True),)

        zero = jnp.zeros((1, tc), F32)
        acc = lax.fori_loop(0, t // rt, first, (zero,) * (CONV_WIDTH + 1))
        gw_ref[...] = jnp.concatenate(acc[:CONV_WIDTH], axis=0)
        gb_ref[...] = acc[CONV_WIDTH]

        def second(i, carry):
            r0 = pl.multiple_of(i * rt, 16)
            dx_ref[pl.ds(r0, rt), :] = _bf(w[3:4, :] * dup[pl.ds(r0, rt), :] + w[2:3, :] * dup[pl.ds(r0 + 1, rt), :]
                                          + w[1:2, :] * dup[pl.ds(r0 + 2, rt), :] + w[0:1, :] * dup[pl.ds(r0 + 3, rt), :])
            return carry

        lax.fori_loop(0, t // rt, second, 0)

    return _pallas(
        body, name=name, grid=(width // tc,),
        in_specs=[pl.BlockSpec((t, tc), lambda j: (0, j)), pl.BlockSpec((t, tc), lambda j: (0, j + off_p)),
                  pl.BlockSpec((CONV_WIDTH, tc), lambda j: (0, j + off_w)), pl.BlockSpec((1, tc), lambda j: (0, j + off_w))],
        out_specs=[pl.BlockSpec((t, tc), lambda j: (0, j)), pl.BlockSpec((CONV_WIDTH, tc), lambda j: (0, j)),
                   pl.BlockSpec((1, tc), lambda j: (0, j))],
        out_shape=[jax.ShapeDtypeStruct((t, width), BF16), jax.ShapeDtypeStruct((CONV_WIDTH, width), F32),
                   jax.ShapeDtypeStruct((1, width), F32)],
        scratch_shapes=[pltpu.VMEM((t + 8, tc), F32), pltpu.VMEM((t + 8, tc), F32)],
        compiler_params=_cp("parallel"))(dseg, proj, conv_w, conv_b)


def _dinproj(segs, w_re, hpad, norm_w, dy_t, ga):
    t = segs[0].shape[0]
    d = hpad.shape[1]
    tm, tk = _tile(t, 416), SEG_TILE
    counts = [s.shape[1] // tk for s in segs]
    firsts = [sum(counts[:s]) for s in range(len(segs))]
    nk = sum(counts)
    assert nk * tk == w_re.shape[1]
    ni = t // tm
    ns = len(segs)

    def body(*refs):
        seg_refs = refs[:ns]
        w_ref, h_ref, nw_ref, dy_ref, ga_ref, dh_ref, gnw_ref, got_ref, acc, send_sems, recv_sems = refs[ns:]
        i, k = pl.program_id(0), pl.program_id(1)

        @pl.when((i == 0) & (k == 0))
        def _():
            for cp in _exchange_copies(ga_ref, got_ref, send_sems, recv_sems):
                cp.start()
            gnw_ref[...] = jnp.zeros_like(gnw_ref)

        @pl.when(k == 0)
        def _():
            acc[...] = jnp.zeros_like(acc)

        for s in range(ns):
            @pl.when((k >= firsts[s]) & (k < firsts[s] + counts[s]))
            def _(s=s):
                acc[...] += _nt(seg_refs[s][...], w_ref[...])

        @pl.when(k == nk - 1)
        def _():
            h = h_ref[...]
            rstd = lax.rsqrt(jnp.mean(h * h, axis=-1, keepdims=True) + EPS)
            nrm = h * rstd
            dhn = acc[...]
            gnw_ref[...] += jnp.sum(dhn * nrm, axis=0, keepdims=True)
            dn = dhn * nw_ref[...]
            dh_ref[...] = rstd * (dn - nrm * jnp.mean(dn * nrm, axis=-1, keepdims=True)) + dy_ref[...]

        @pl.when((i == ni - 1) & (k == nk - 1))
        def _():
            for cp in _exchange_copies(ga_ref, got_ref, send_sems, recv_sems):
                cp.wait()

    seg_specs = [pl.BlockSpec((tm, tk), functools.partial(lambda i, k, f0, n0: (i, jnp.clip(k - f0, 0, n0 - 1)),
                                                          f0=firsts[s], n0=counts[s])) for s in range(ns)]
    return _pallas(
        body, name="dinproj", grid=(ni, nk),
        in_specs=seg_specs + [pl.BlockSpec((d, tk), lambda i, k: (0, k)),
                              pl.BlockSpec((tm, d), lambda i, k: (i, 0)), pl.BlockSpec((1, d), lambda i, k: (0, 0)),
                              pl.BlockSpec((tm, d), lambda i, k: (i, 0)), ANY],
        out_specs=[pl.BlockSpec((tm, d), lambda i, k: (i, 0)), pl.BlockSpec((1, d), lambda i, k: (0, 0)), ANY],
        out_shape=[jax.ShapeDtypeStruct((t, d), F32), jax.ShapeDtypeStruct((1, d), F32), _exchange_shape(ga)],
        scratch_shapes=[pltpu.VMEM((tm, d), F32)] + _exchange_scratch(),
        compiler_params=_cp("arbitrary", "arbitrary"))(*segs, w_re, hpad, norm_w, dy_t, ga)


def _spread_heads(v):
    v = jnp.pad(v.reshape(GROUPS, HPG), ((0, 0), (0, GROUPS - HPG))).reshape(1, GROUPS * GROUPS)
    return jnp.pad(v, ((0, 0), (0, LANES - GROUPS * GROUPS)))


def _gather_heads(v):
    return v[0:1, :GROUPS * GROUPS].reshape(GROUPS, GROUPS)[:, :HPG].reshape(1, SSD_HEADS)


def _rope_tables(t):
    half = HEAD_DIM // 2
    inv = ROPE_THETA ** (-jnp.arange(half, dtype=F32) / half)
    pos = (jnp.arange(t) - PAD_LEAD).astype(F32)
    ang = pos[:, None] * inv[None, :]
    cos, sin = jnp.cos(ang), jnp.sin(ang)
    cos_t = jnp.concatenate([cos, cos, cos, cos], axis=1)
    sin_t = jnp.concatenate([-sin, sin, -sin, sin], axis=1)
    return cos_t, sin_t


def _column_pieces():
    runs = [(0, OB + 2 * GROUPS * D_STATE, 0)]
    o = OB + 2 * GROUPS * D_STATE
    runs += [(o + HPG * g, HPG, ODT + GROUPS * g) for g in range(GROUPS)]
    o += SSD_HEADS
    for width, dst in ((D_ATT, OQ), (D_KV, OK), (D_KV, OV), (D_ATT, OG)):
        runs.append((o, width, dst))
        o += width
    assert o == D_IN
    pieces = []
    for o0, width, dst in runs:
        for j in range(N_SHARD):
            lo, hi = max(o0, W_IN_SHARD * j), min(o0 + width, W_IN_SHARD * (j + 1))
            if lo < hi:
                pieces.append((j, lo - W_IN_SHARD * j, hi - W_IN_SHARD * j, dst + lo - o0))
    return pieces


def _shards_to_re(w_all):
    _, k, _ = w_all.shape
    tr = 256

    def body(x_ref, o_ref):
        o_ref[:, ODT:ODT + DT_SLAB] = jnp.zeros((tr, DT_SLAB), o_ref.dtype)
        for j, c0, c1, d0 in _column_pieces():
            o_ref[:, d0:d0 + c1 - c0] = x_ref[j, :, c0:c1]

    return _pallas(body, name="shards_to_re", grid=(k // tr,),
                   in_specs=[pl.BlockSpec((N_SHARD, tr, W_IN_SHARD), lambda i: (0, i, 0))],
                   out_specs=pl.BlockSpec((tr, N_RE), lambda i: (i, 0)),
                   out_shape=jax.ShapeDtypeStruct((k, N_RE), w_all.dtype), compiler_params=_cp("parallel"))(w_all)


def _pair_add_to_shards(parts, got, pieces, shard_rows, core, name):
    n = parts[0].shape[1]
    hn = n // 2
    tc = 128
    nt = hn // tc
    ns = len(parts)
    starts = [sum(p.shape[0] for p in parts[:s]) for s in range(ns)]
    moves = []
    for j, c0, c1, d0 in pieces:
        for s, p in enumerate(parts):
            lo, hi = max(d0, starts[s]), min(d0 + c1 - c0, starts[s] + p.shape[0])
            if lo < hi:
                moves.append((s, lo - starts[s], j, c0 + lo - d0, hi - lo))
    assert sum(m[4] for m in moves) == N_SHARD * shard_rows

    def body(core_ref, *refs):
        own, theirs, o_ref, acc = refs[:ns], refs[ns:2 * ns], refs[2 * ns], refs[2 * ns + 1]
        for s, r0, j, c0, rows in moves:
            acc[j, c0:c0 + rows, :] = own[s][r0:r0 + rows, :] + theirs[s][r0:r0 + rows, :]
        o_ref[...] = _bf(acc[...])

    return _pallas(
        body, name=name,
        grid_spec=pltpu.PrefetchScalarGridSpec(
            num_scalar_prefetch=1, grid=(nt,),
            in_specs=[pl.BlockSpec((p.shape[0], tc), lambda i, core_ref: (0, core_ref[0] * nt + i)) for p in parts]
            + [pl.BlockSpec((p.shape[0], tc), lambda i, core_ref: (0, i)) for p in parts],
            out_specs=pl.BlockSpec((N_SHARD, shard_rows, tc), lambda i, core_ref: (0, 0, i)),
            scratch_shapes=[pltpu.VMEM((N_SHARD, shard_rows, tc), F32)]),
        out_shape=jax.ShapeDtypeStruct((N_SHARD, shard_rows, hn), BF16),
        compiler_params=_cp("parallel"))(core, *parts, *got)


def _local_step(x, target, meta, norm_pre_w, w_re, conv_w, conv_b, dt_bias, a_log, d_skip, ssd_norm_w, sinks,
                w_out_shard, norm_post_w, place):
    seq = x.shape[0]
    t = PAD_LEAD + N_META + seq
    hpad = jnp.concatenate([jnp.zeros((PAD_LEAD, D_MODEL), F32), meta, x], axis=0)
    dt_bias_l, a_log_l, d_skip_l = _spread_heads(dt_bias), _spread_heads(a_log), _spread_heads(d_skip)
    cos_t, sin_t = _rope_tables(t)
    sink_v = sinks.reshape(Q_HEADS)

    proj, hn, w_out_all = _inproj(hpad, norm_pre_w, w_re, w_out_shard)
    w_out = w_out_all.reshape(D_MIX, D_MODEL)
    xbc = _conv_fwd(proj, conv_w, conv_b)
    dt, acs, acst = _dt_prep(proj, dt_bias_l, a_log_l)
    y_ssd, ymix, states = _ssd_fwd(xbc, proj, dt, acs, acst, d_skip_l, ssd_norm_w)
    qr, kr = _rope(proj, OQ, proj, OK, cos_t, sin_t)
    amix = _attn_fwd(qr, kr, proj, sink_v)
    out = _outproj(ymix, amix, w_out)
    dout, dy_t, loss_blk, g_norm_post = _post_loss(out, x, target, norm_post_w)

    g_out_y = _tn_matmul(ymix, dout, "gw_out_y")
    g_out_a, got_y = _tn_matmul(amix, dout, "gw_out_a", carry=g_out_y)
    dmix, got_a = _nt_matmul(dout, w_out, "dmix", carry=g_out_a)
    ga_out = _reduce_pair([g_out_y, g_out_a], [got_y, got_a], [(j, 0, W_OUT_SHARD, W_OUT_SHARD * j) for j in range(N_SHARD)],
                          W_OUT_SHARD, place, "gw_out")
    dq_r, dg, dk_r, dv, gs, slabs_out = _attn_bwd(qr, kr, proj, dmix, sink_v, ga_out)
    g_w_out = _reduce_finish(ga_out, slabs_out, place, "gw_out")
    dq, dk = _rope(dq_r, 0, dk_r, 0, cos_t, -sin_t)
    dz, dxs, db, dc, dacs_g, ddt_g, g_ssd_norm, gdsk = _ssd_bwd(dmix, y_ssd, xbc, proj, dt, acs, acst, states,
                                                                d_skip_l, ssd_norm_w)
    draw, ga, gb = _dt_bwd(dacs_g, ddt_g, dt, proj, dt_bias_l, a_log_l)
    dxs_p, gcw0, gcb0 = _conv_bwd(dxs, proj, conv_w, conv_b, 0, "conv_bwd_x")
    db_p, gcw1, gcb1 = _conv_bwd(db, proj, conv_w, conv_b, D_SSD, "conv_bwd_b")
    dc_p, gcw2, gcb2 = _conv_bwd(dc, proj, conv_w, conv_b, D_SSD + GROUPS * D_STATE, "conv_bwd_c")
    tail = jnp.concatenate([dk, _bf(dv), draw, jnp.zeros((t, DT_SLAB - LANES), BF16)], axis=1)
    segs = [dz, dxs_p, db_p, dc_p, dq, dg, tail]
    g_parts, got_parts = [_tn_matmul(segs[0], hn, "gw_in_0")], []
    for s in range(1, len(segs)):
        part, got = _tn_matmul(segs[s], hn, "gw_in_%d" % s, carry=g_parts[-1])
        g_parts.append(part)
        got_parts.append(got)
    ga_in = _reduce_pair(g_parts, got_parts, _column_pieces(), W_IN_SHARD, place, "gw_in")
    dh, g_norm_pre, slabs_in = _dinproj(segs, w_re, hpad, norm_pre_w, dy_t, ga_in)
    g_w_in_half = _chip_sum(ga_in, slabs_in, place, "gw_in_chip_sum")

    gdsk_l = jnp.concatenate([gdsk[g, 0:1, 0:GROUPS] for g in range(GROUPS)], axis=1)
    gdsk_l = jnp.pad(gdsk_l, ((0, 0), (0, LANES - GROUPS * GROUPS)))
    grads = dict(
        meta_tokens=dh[PAD_LEAD:ROW0], norm_pre_w=g_norm_pre, w_in_half=g_w_in_half,
        conv_w=jnp.concatenate([gcw0, gcw1, gcw2], axis=1), conv_b=jnp.concatenate([gcb0, gcb1, gcb2], axis=1),
        dt_bias=_gather_heads(gb), a_log=_gather_heads(ga), d_skip=_gather_heads(gdsk_l), ssd_norm_w=g_ssd_norm,
        attn_sinks=gs[0:1, :Q_HEADS], w_out=g_w_out, norm_post_w=g_norm_post)
    return loss_blk[0, 0], dh[ROW0:], grads


ANY = pl.BlockSpec(memory_space=pl.ANY)
MESH = pl.DeviceIdType.MESH
GATHER_CHUNKS = 4
PAIR_CHUNKS = 8
JOIN_CHUNKS = 8


def _rcopy(src, dst, ssem, rsem, dev):
    return pltpu.make_async_remote_copy(src_ref=src, dst_ref=dst, send_sem=ssem, recv_sem=rsem, device_id=dev,
                                        device_id_type=MESH)


def _place():
    x, y, c = lax.axis_index("x"), lax.axis_index("y"), lax.axis_index("c")
    chips = [(1 - x, y), (x, 1 - y), (1 - x, 1 - y)]
    return x, y, c, chips


def _gather_plan(x_ref, out_ref, send_sems, recv_sems, local_sems, hr, kc):
    ch = hr // kc
    assert ch * kc == hr and ch % 16 == 0
    x, y, c, chips = _place()
    me = 2 * x + y
    sibling = (x, y, 1 - c)

    def piece(chip, hc, k):
        return out_ref.at[chip, pl.ds(hc * hr + k * ch, ch), :]

    def local():
        return [pltpu.make_async_copy(x_ref.at[pl.ds(k * ch, ch), :], out_ref.at[me, pl.ds(k * ch, ch), :],
                                      local_sems.at[k]) for k in range(2 * kc)]

    def first():
        return [_rcopy(x_ref.at[pl.ds(c * hr + k * ch, ch), :], piece(me, c, k), send_sems.at[j * kc + k],
                       recv_sems.at[j * kc + k], (*chip, c)) for j, chip in enumerate(chips) for k in range(kc)]

    def passed(hc):
        return [_rcopy(piece(2 * chip[0] + chip[1], hc, k), piece(2 * chip[0] + chip[1], hc, k),
                       send_sems.at[(3 + j) * kc + k], recv_sems.at[(3 + j) * kc + k], sibling)
                for j, chip in enumerate(chips) for k in range(kc)]

    def arrivals():
        return [_rcopy(piece(2 * chip[0] + chip[1], c, k), piece(2 * chip[0] + chip[1], c, k), send_sems.at[j * kc + k],
                       recv_sems.at[j * kc + k], (*chip, c)) for j, chip in enumerate(chips) for k in range(kc)]

    def start():
        for cp in local() + first():
            cp.start()

    def forward():
        for arrived in arrivals():
            arrived.wait_recv()
        for fw in passed(c):
            fw.start()

    def finish():
        for cp in passed(1 - c):
            cp.wait_recv()
        for cp in first() + passed(c):
            cp.wait_send()
        for cp in local():
            cp.wait()

    return start, forward, finish


def _gather_shards(shard, name, kc, chip, small):
    r, n = shard.shape
    hr = r // 2
    qr = hr // 2
    ch = qr // kc
    assert ch * kc == qr and ch % 16 == 0
    nflow = 12
    tr = 256

    def body(x_ref, p_ref, out_ref, slots_ref, send_sems, recv_sems, *small_sems):
        start_small, wait_small = _chip_small_exchange(p_ref, slots_ref, *small_sems)
        start_small()
        x, y, c, _ = _place()
        me, cxn, cyn, cdg = 2 * x + y, 2 * (1 - x) + y, 2 * x + 1 - y, 2 * (1 - x) + 1 - y
        xn, yn, sibling = (1 - x, y, c), (x, 1 - y, c), (x, y, 1 - c)

        def piece(chip, hc, part, k):
            return out_ref.at[chip, pl.ds(hc * hr + part * qr + k * ch, ch), :]

        def own(part, k):
            return x_ref.at[pl.ds(c * hr + part * qr + k * ch, ch), :]

        def sems(flow, k):
            return send_sems.at[flow * kc + k], recv_sems.at[flow * kc + k]

        def arrival(flow, chip, hc, part, k):
            return _rcopy(piece(chip, hc, part, k), piece(chip, hc, part, k), *sems(flow, k), sibling)

        sends = []
        for flow, part, peer in ((0, 0, xn), (1, 1, yn), (2, 0, yn), (3, 1, xn)):
            sends += [_rcopy(own(part, k), piece(me, c, part, k), *sems(flow, k), peer) for k in range(kc)]
        for cp in sends:
            cp.start()
        landing = ((0, cxn, 0), (1, cyn, 1), (2, cyn, 0), (3, cxn, 1), (4, cdg, 0), (5, cdg, 1))
        for i, (flow, chip, part) in enumerate(landing):
            for k in range(kc):
                arrival(flow, chip, c, part, k).wait_recv()
                if flow < 2:
                    on = _rcopy(piece(chip, c, part, k), piece(chip, c, part, k), *sems(4 + flow, k),
                                yn if flow == 0 else xn)
                    on.start()
                    sends.append(on)
                fw = _rcopy(piece(chip, c, part, k), piece(chip, c, part, k), *sems(6 + i, k), sibling)
                fw.start()
                sends.append(fw)
        for i, (flow, chip, part) in enumerate(landing):
            for k in range(kc):
                arrival(6 + i, chip, 1 - c, part, k).wait_recv()
        for cp in sends:
            cp.wait_send()
        wait_small()

    full = jax.ShapeDtypeStruct((N_SHARD, r, n), shard.dtype)
    others, slots = _pallas(
        body, name=name, in_specs=[ANY, ANY], out_specs=[ANY, ANY],
        out_shape=[full, jax.ShapeDtypeStruct((N_SHARD,) + small.shape, F32)],
        scratch_shapes=[pltpu.SemaphoreType.DMA((nflow * kc,)), pltpu.SemaphoreType.DMA((nflow * kc,)),
                        pltpu.SemaphoreType.DMA((3,)), pltpu.SemaphoreType.DMA((3,)), pltpu.SemaphoreType.DMA])(
                            shard, small)

    def place(chip_ref, own_ref, all_ref, o_ref):
        o_ref[0] = own_ref[...]

    gathered = _pallas(
        place, name=name + "_own",
        grid_spec=pltpu.PrefetchScalarGridSpec(
            num_scalar_prefetch=1, grid=(r // tr,),
            in_specs=[pl.BlockSpec((tr, n), lambda i, chip_ref: (i, 0)), ANY],
            out_specs=pl.BlockSpec((1, tr, n), lambda i, chip_ref: (chip_ref[0], i, 0))),
        out_shape=full, input_output_aliases={2: 0}, compiler_params=_cp("parallel"))(chip, shard, others)
    return gathered, slots


def _pair_copies(src_ref, dst_ref, send_sems, recv_sems):
    hn = src_ref.shape[1] // 2
    cw = hn // PAIR_CHUNKS
    assert cw * PAIR_CHUNKS == hn and cw % LANES == 0
    x, y, c, _ = _place()
    return [_rcopy(src_ref.at[:, pl.ds((1 - c) * hn + k * cw, cw)], dst_ref.at[:, pl.ds(k * cw, cw)],
                   send_sems.at[k], recv_sems.at[k], (x, y, 1 - c)) for k in range(PAIR_CHUNKS)]


def _pair_send(parts, name):
    n = parts[0].shape[1]
    hn = n // 2
    kc = PAIR_CHUNKS
    cw = hn // kc
    assert cw * kc == hn and cw % LANES == 0
    ns = len(parts)

    def body(*refs):
        srcs, dsts, send_sems, recv_sems = refs[:ns], refs[ns:2 * ns], refs[2 * ns], refs[2 * ns + 1]
        x, y, c, _ = _place()
        cps = [_rcopy(srcs[s].at[:, pl.ds((1 - c) * hn + k * cw, cw)], dsts[s].at[:, pl.ds(k * cw, cw)],
                      send_sems.at[s * kc + k], recv_sems.at[s * kc + k], (x, y, 1 - c))
               for s in range(ns) for k in range(kc)]
        for cp in cps:
            cp.start()
        for cp in cps:
            cp.wait()

    return _pallas(
        body, name=name, in_specs=[ANY] * ns, out_specs=[ANY] * ns,
        out_shape=[jax.ShapeDtypeStruct((p.shape[0], hn), F32) for p in parts],
        scratch_shapes=[pltpu.SemaphoreType.DMA((ns * kc,)), pltpu.SemaphoreType.DMA((ns * kc,))])(*parts)


REDUCE_TILE = 256


def _exchange_copies(g_ref, got_ref, send_sems, recv_sems):
    hn = g_ref.shape[2]
    kc = GATHER_CHUNKS
    cw = hn // kc
    assert cw * kc == hn and cw % LANES == 0
    x, y, c, chips = _place()
    return [_rcopy(g_ref.at[2 * chip[0] + chip[1], :, pl.ds(k * cw, cw)], got_ref.at[j, :, pl.ds(k * cw, cw)],
                   send_sems.at[j * kc + k], recv_sems.at[j * kc + k], (*chip, c))
            for j, chip in enumerate(chips) for k in range(kc)]


def _exchange_scratch():
    return [pltpu.SemaphoreType.DMA((3 * GATHER_CHUNKS,)), pltpu.SemaphoreType.DMA((3 * GATHER_CHUNKS,))]


def _exchange_shape(ga):
    return jax.ShapeDtypeStruct((3,) + ga.shape[1:], ga.dtype)


def _chip_sum(ga, got, place, name):
    _, r, hn = ga.shape
    tc = REDUCE_TILE
    nt = hn // tc

    def body(place_ref, own_ref, got_ref, o_ref):
        acc = own_ref[0].astype(F32)
        for j in range(3):
            acc = acc + got_ref[j].astype(F32)
        o_ref[...] = acc

    return _pallas(
        body, name=name,
        grid_spec=pltpu.PrefetchScalarGridSpec(
            num_scalar_prefetch=1, grid=(nt,),
            in_specs=[pl.BlockSpec((1, r, tc), lambda i, place_ref: (place_ref[0], 0, i)),
                      pl.BlockSpec((3, r, tc), lambda i, place_ref: (0, 0, i))],
            out_specs=pl.BlockSpec((r, tc), lambda i, place_ref: (0, place_ref[1] * nt + i))),
        out_shape=jax.ShapeDtypeStruct((r, 2 * hn), F32), compiler_params=_cp("parallel"))(place, ga, got)


def _pair_join(buf, name, small=None):
    r, n = buf.shape
    hn = n // 2
    kc = JOIN_CHUNKS
    cw = hn // kc
    assert cw * kc == hn and cw % LANES == 0

    def body(in_ref, *refs):
        if small is None:
            out_ref, send_sems, recv_sems = refs
        else:
            p_ref, out_ref, slots_ref, send_sems, recv_sems = refs[:5]
            start_small, wait_small = _small_exchange(p_ref, slots_ref, *refs[5:])
            start_small()
        x, y, c, _ = _place()
        cps = [_rcopy(out_ref.at[:, pl.ds(c * hn + k * cw, cw)], out_ref.at[:, pl.ds(c * hn + k * cw, cw)],
                      send_sems.at[k], recv_sems.at[k], (x, y, 1 - c)) for k in range(kc)]
        for cp in cps:
            cp.start()
        for k in range(kc):
            cols = out_ref.at[:, pl.ds((1 - c) * hn + k * cw, cw)]
            _rcopy(cols, cols, send_sems.at[k], recv_sems.at[k], (x, y, 1 - c)).wait_recv()
        for cp in cps:
            cp.wait_send()
        if small is not None:
            wait_small()

    sems = [pltpu.SemaphoreType.DMA((kc,)), pltpu.SemaphoreType.DMA((kc,))]
    if small is None:
        return _pallas(body, name=name, in_specs=[ANY], out_specs=ANY, out_shape=jax.ShapeDtypeStruct((r, n), F32),
                       input_output_aliases={0: 0}, scratch_shapes=sems)(buf)
    return _pallas(
        body, name=name, in_specs=[ANY, ANY], out_specs=[ANY, ANY],
        out_shape=[jax.ShapeDtypeStruct((r, n), F32), jax.ShapeDtypeStruct((N_DEV,) + small.shape, F32)],
        input_output_aliases={0: 0}, scratch_shapes=sems + _small_scratch())(buf, small)


def _reduce_pair(parts, got, pieces, shard_rows, place, tag):
    if len(got) < len(parts):
        got = list(got) + list(_pair_send(parts[len(got):], tag + "_pair_send"))
    return _pair_add_to_shards(parts, got, pieces, shard_rows, place[1:2], tag + "_pair_add")


def _reduce_finish(ga, slabs, place, tag):
    return _pair_join(_chip_sum(ga, slabs, place, tag + "_chip_sum"), tag + "_pair_join")


N_DEV = 8


def _small_exchange(p_ref, slots_ref, send_sems, recv_sems, local_sem):
    x, y, c, _ = _place()
    my = 4 * x + 2 * y + c

    def sends():
        return [_rcopy(p_ref, slots_ref.at[my], send_sems.at[k - 1], recv_sems.at[k - 1],
                       (x ^ ((k >> 2) & 1), y ^ ((k >> 1) & 1), c ^ (k & 1))) for k in range(1, N_DEV)]

    def local():
        return pltpu.make_async_copy(p_ref, slots_ref.at[my], local_sem)

    def start():
        local().start()
        for cp in sends():
            cp.start()

    def wait():
        for k in range(1, N_DEV):
            _rcopy(p_ref, slots_ref.at[my ^ k], send_sems.at[k - 1], recv_sems.at[k - 1], (x, y, c)).wait_recv()
        for cp in sends():
            cp.wait_send()
        local().wait()

    return start, wait


def _chip_small_exchange(p_ref, slots_ref, send_sems, recv_sems, local_sem):
    x, y, c, chips = _place()
    me = 2 * x + y

    def sends():
        return [_rcopy(p_ref, slots_ref.at[me], send_sems.at[j], recv_sems.at[j], (*chip, c))
                for j, chip in enumerate(chips)]

    def local():
        return pltpu.make_async_copy(p_ref, slots_ref.at[me], local_sem)

    def start():
        local().start()
        for cp in sends():
            cp.start()

    def wait():
        for j, chip in enumerate(chips):
            slot = slots_ref.at[2 * chip[0] + chip[1]]
            _rcopy(slot, slot, send_sems.at[j], recv_sems.at[j], (*chip, c)).wait_recv()
        for cp in sends():
            cp.wait_send()
        local().wait()

    return start, wait


def _small_scratch():
    return [pltpu.SemaphoreType.DMA((N_DEV - 1,)), pltpu.SemaphoreType.DMA((N_DEV - 1,)), pltpu.SemaphoreType.DMA]


def _sum_slots(slots, name):
    _, rows, n = slots.shape

    def body(s_ref, o_ref):
        acc = s_ref[0]
        for j in range(1, N_DEV):
            acc = acc + s_ref[j]
        o_ref[...] = acc

    vm = pl.BlockSpec(memory_space=pltpu.VMEM)
    return _pallas(body, name=name, in_specs=[vm], out_specs=vm, out_shape=jax.ShapeDtypeStruct((rows, n), F32))(slots)


def _adamw(w, g, m, v, name):
    r, n = w.shape
    tr = _tile(r, 256, 8)
    c1 = 1.0 / (1.0 - ADAM_B1 ** ADAM_STEP)
    c2 = 1.0 / (1.0 - ADAM_B2 ** ADAM_STEP)

    def body(w_ref, g_ref, m_ref, v_ref, d_ref, mo_ref, vo_ref, go_ref):
        gv = g_ref[...]
        mn = ADAM_B1 * m_ref[...] + (1.0 - ADAM_B1) * gv
        vn = ADAM_B2 * v_ref[...] + (1.0 - ADAM_B2) * (gv * gv)
        d_ref[...] = -ADAM_LR * ((mn * c1) / (jnp.sqrt(vn * c2) + ADAM_EPS) + ADAM_WD * w_ref[...])
        mo_ref[...] = mn
        vo_ref[...] = vn
        go_ref[...] = gv

    spec = pl.BlockSpec((tr, n), lambda i: (i, 0))
    shp = jax.ShapeDtypeStruct((r, n), F32)
    return _pallas(body, name=name, grid=(r // tr,), in_specs=[spec] * 4, out_specs=[spec] * 4, out_shape=[shp] * 4,
                   compiler_params=_cp("parallel"))(w, g, m, v)


PACK_W = 1024
SMALL_REPL = ("norm_pre_w", "conv_b", "ssd_norm_w", "norm_post_w")
SMALL_HEAD = ("dt_bias", "a_log", "d_skip", "attn_sinks")


def _rows(a):
    return a.reshape(-1, PACK_W)


def _head_row(vals, extra=None):
    parts = [vals[n].reshape(1, -1) for n in SMALL_HEAD]
    if extra is not None:
        parts.append(extra.reshape(1, 1))
    row = jnp.concatenate(parts, axis=1)
    return jnp.pad(row, ((0, 0), (0, PACK_W - row.shape[1])))


def _pad_rows(a, rows):
    return jnp.pad(a, ((0, rows - a.shape[0]), (0, 0)))


def _pack_repl(vals, extra=None):
    body = jnp.concatenate([_rows(vals[n]) for n in SMALL_REPL] + [_head_row(vals, extra)], axis=0)
    return _pad_rows(body, 16)


def _unpack_repl(buf):
    out, r = {}, 0
    for n, k in zip(SMALL_REPL, (2, 4, 2, 2)):
        out[n] = buf[r:r + k].reshape(1, k * PACK_W)
        r += k
    col = 0
    for n, k in zip(SMALL_HEAD, (32, 32, 32, 16)):
        out[n] = buf[r:r + 1, col:col + k]
        col += k
    return out, buf[r, col]


def kernel(x, meta_tokens, norm_pre_w, w_in, conv_w, conv_b, dt_bias, a_log, d_skip, ssd_norm_w, attn_sinks, w_out, norm_post_w, loss_target, m_meta_tokens, m_norm_pre_w, m_w_in, m_conv_w, m_conv_b, m_dt_bias, m_a_log, m_d_skip, m_ssd_norm_w, m_attn_sinks, m_w_out, m_norm_post_w, v_meta_tokens, v_norm_pre_w, v_w_in, v_conv_w, v_conv_b, v_dt_bias, v_a_log, v_d_skip, v_ssd_norm_w, v_attn_sinks, v_w_out, v_norm_post_w):
    names = ("meta_tokens", "norm_pre_w", "w_in", "conv_w", "conv_b", "dt_bias", "a_log", "d_skip", "ssd_norm_w",
             "attn_sinks", "w_out", "norm_post_w")
    w = dict(zip(names, (meta_tokens, norm_pre_w, w_in, conv_w, conv_b, dt_bias, a_log, d_skip, ssd_norm_w, attn_sinks,
                         w_out, norm_post_w)))
    m = dict(zip(names, (m_meta_tokens, m_norm_pre_w, m_w_in, m_conv_w, m_conv_b, m_dt_bias, m_a_log, m_d_skip,
                         m_ssd_norm_w, m_attn_sinks, m_w_out, m_norm_post_w)))
    v = dict(zip(names, (v_meta_tokens, v_norm_pre_w, v_w_in, v_conv_w, v_conv_b, v_dt_bias, v_a_log, v_d_skip,
                         v_ssd_norm_w, v_attn_sinks, v_w_out, v_norm_post_w)))
    cx, cy, cc = lax.axis_index("x"), lax.axis_index("y"), lax.axis_index("c")
    chip = 2 * cx + cy
    meta_cols = D_MODEL // N_SHARD
    conv_cols = D_CONV // N_SHARD

    place = jnp.stack([chip, cc]).astype(jnp.int32)
    small = jnp.concatenate([_pad_rows(conv_w[0], 8), _rows(meta_tokens)], axis=0)
    w_in_all, small_all = _gather_shards(_bf(w_in[0]), "gather_w_in", GATHER_CHUNKS, place[0:1], small)
    w_re = _shards_to_re(w_in_all)
    conv_full = jnp.transpose(small_all[:, 0:CONV_WIDTH], (1, 0, 2)).reshape(CONV_WIDTH, D_CONV)
    meta_full = jnp.transpose(small_all[:, 8:16].reshape(N_SHARD, N_META, meta_cols), (1, 0, 2)).reshape(N_META, D_MODEL)

    loss_dev, grad_x, g = _local_step(x[0], loss_target[0], meta_full, norm_pre_w, w_re, conv_full, conv_b, dt_bias,
                                      a_log, d_skip, ssd_norm_w, attn_sinks, _bf(w_out[0]), norm_post_w, place)
    g_w_out = g["w_out"]

    packed = jnp.concatenate([_rows(g["conv_w"]), _rows(g["meta_tokens"]), _pack_repl(g, loss_dev)], axis=0)
    g_w_in, slots = _pair_join(g["w_in_half"], "gw_in_pair_join", small=packed)
    red = _sum_slots(slots, "reduce_small")
    g_conv_full = red[0:16].reshape(CONV_WIDTH, D_CONV)
    g_meta_full = red[16:48].reshape(N_META, D_MODEL)
    g_small, loss = _unpack_repl(red[48:64])
    grads = dict(g_small)
    grads["w_in"] = g_w_in
    grads["w_out"] = g_w_out
    grads["conv_w"] = lax.dynamic_slice(g_conv_full, (0, chip * conv_cols), (CONV_WIDTH, conv_cols))
    grads["meta_tokens"] = lax.dynamic_slice(g_meta_full, (0, chip * meta_cols), (N_META, meta_cols))

    upd = {}
    upd["w_in"] = [jnp.swapaxes(a, 0, 1) for a in _adamw(jnp.swapaxes(w_in[0], 0, 1), g_w_in, jnp.swapaxes(m_w_in[0], 0, 1),
                                                         jnp.swapaxes(v_w_in[0], 0, 1), "adamw_w_in")]
    grads["w_in"] = upd["w_in"][3]
    upd["w_out"] = _adamw(w_out[0], g_w_out, m_w_out[0], v_w_out[0], "adamw_w_out")
    grads["w_out"] = upd["w_out"][3]

    def pack_small(vals, conv, meta):
        return jnp.concatenate([_pad_rows(conv.reshape(CONV_WIDTH, conv_cols), 8), _rows(meta), _pack_repl(vals)], axis=0)

    sm = _adamw(pack_small(w, w["conv_w"], w["meta_tokens"]), pack_small(grads, grads["conv_w"], grads["meta_tokens"]),
                pack_small(m, m["conv_w"], m["meta_tokens"]), pack_small(v, v["conv_w"], v["meta_tokens"]),
                "adamw_small")
    for n in names:
        if n not in ("w_in", "w_out"):
            upd[n] = [None, None, None]
    for k, buf in enumerate(sm[:3]):
        upd["conv_w"][k] = buf[0:CONV_WIDTH]
        upd["meta_tokens"][k] = buf[8:16].reshape(N_META, meta_cols)
        rest, _ = _unpack_repl(buf[16:32])
        for n in SMALL_REPL + SMALL_HEAD:
            upd[n][k] = rest[n]

    def shaped(n, a):
        return a.reshape(w[n].shape)

    outs = [loss, grad_x[None]]
    outs += [shaped(n, grads[n]) for n in names]
    for k in range(3):
        outs += [shaped(n, upd[n][k]) for n in names]
    return tuple(outs)
```

```python
import functools

import jax
import jax.numpy as jnp
from jax import lax
from jax.experimental import pallas as pl
from jax.experimental.pallas import tpu as pltpu

F32 = jnp.float32
BF16 = jnp.bfloat16

D_MODEL = 2048
CHUNK = 64
N_META = 16
PAD_LEAD = CHUNK - N_META
ROW0 = PAD_LEAD + N_META
EPS = 1e-6
SSD_HEADS = 32
HEAD_DIM = 64
GROUPS = 8
HPG = SSD_HEADS // GROUPS
D_STATE = 128
D_SSD = 2048
GROUP_W = D_SSD // GROUPS
CONV_WIDTH = 4
D_CONV = 4096
Q_HEADS = 16
KV_HEADS = 4
REP = Q_HEADS // KV_HEADS
D_ATT = 1024
D_KV = 256
BAND_CHUNKS = 3
ROPE_THETA = 10000.0
D_MIX = D_SSD + D_ATT
D_IN = 8736
N_SHARD = 4
W_IN_SHARD = D_IN // N_SHARD
W_OUT_SHARD = D_MIX // N_SHARD

OZ, OXS, OB, OC, OQ, OG, OK, OV, ODT = 0, 2048, 4096, 5120, 6144, 7168, 8192, 8448, 8704
DT_SLAB = 512
N_RE = ODT + DT_SLAB
LANES = 128

ADAM_LR, ADAM_B1, ADAM_B2, ADAM_EPS, ADAM_WD, ADAM_STEP = 0.001, 0.9, 0.999, 1e-08, 0.01, 10

SSD_FWD_GROUPS_PER_STEP = 4
SSD_BWD_GROUPS_PER_STEP = 8
SEG_TILE = 1024
VMEM_LIMIT = 52 * 1024 * 1024
NEG = -1e30
HI = lax.Precision.HIGHEST


def _pallas(body, **kw):
    return pl.pallas_call(body, **kw)


def _cp(*sem):
    return pltpu.CompilerParams(dimension_semantics=sem, vmem_limit_bytes=VMEM_LIMIT)


def _tile(n, cap, mult=16):
    best = None
    for d in range(mult, min(n, cap) + 1, mult):
        if n % d == 0:
            best = d
    assert best is not None, (n, cap)
    return best


def _nt(a, b):
    return lax.dot_general(a, b, (((1,), (1,)), ((), ())), preferred_element_type=F32)


def _tn(a, b):
    return lax.dot_general(a, b, (((0,), (0,)), ((), ())), preferred_element_type=F32)


def _mm(a, b):
    return jnp.dot(a, b, preferred_element_type=F32)


def _sigmoid(x):
    return 1.0 / (1.0 + jnp.exp(-x))


def _bf(x):
    return x.astype(BF16)


def _inproj(hpad, norm_w, w_re, w_out_shard):
    t, d = hpad.shape
    n = w_re.shape[1]
    tm, tn = _tile(t, 1040), 1024
    ni, nj = t // tm, n // tn
    r_out, n_out = w_out_shard.shape
    kc = GATHER_CHUNKS

    def body(h_ref, nw_ref, w_ref, ws_ref, proj_ref, hn_ref, wall_ref, hn_s, send_sems, recv_sems, local_sems):
        i, j = pl.program_id(0), pl.program_id(1)
        start, forward, finish = _gather_plan(ws_ref, wall_ref, send_sems, recv_sems, local_sems, r_out // 2, kc)
        pl.when((i == 0) & (j == 0))(start)
        pl.when((i == ni // 2) & (j == 0))(forward)

        @pl.when(j == 0)
        def _():
            h = h_ref[...]
            ms = jnp.mean(h * h, axis=-1, keepdims=True)
            hn = _bf(h * lax.rsqrt(ms + EPS) * nw_ref[...])
            hn_s[...] = hn
            hn_ref[...] = hn
        proj_ref[...] = _mm(hn_s[...], w_ref[...])
        pl.when((i == ni - 1) & (j == nj - 1))(finish)

    return _pallas(
        body, name="inproj", grid=(ni, nj),
        in_specs=[pl.BlockSpec((tm, d), lambda i, j: (i, 0)), pl.BlockSpec((1, d), lambda i, j: (0, 0)),
                  pl.BlockSpec((d, tn), lambda i, j: (0, j)), ANY],
        out_specs=[pl.BlockSpec((tm, tn), lambda i, j: (i, j)), pl.BlockSpec((tm, d), lambda i, j: (i, 0)), ANY],
        out_shape=[jax.ShapeDtypeStruct((t, n), F32), jax.ShapeDtypeStruct((t, d), BF16),
                   jax.ShapeDtypeStruct((N_SHARD, r_out, n_out), w_out_shard.dtype)],
        scratch_shapes=[pltpu.VMEM((tm, d), BF16), pltpu.SemaphoreType.DMA((6 * kc,)), pltpu.SemaphoreType.DMA((6 * kc,)),
                        pltpu.SemaphoreType.DMA((2 * kc,))],
        compiler_params=_cp("arbitrary", "arbitrary"))(hpad, norm_w, w_re, w_out_shard)


def _conv_fwd(proj, conv_w, conv_b):
    t = proj.shape[0]
    tc = 256
    off = OXS // tc

    def body(x_ref, w_ref, b_ref, o_ref):
        x = x_ref[...]
        w = w_ref[...]
        row = lax.broadcasted_iota(jnp.int32, (t, tc), 0)
        u = b_ref[...] + w[3:4, :] * x
        for k in range(1, CONV_WIDTH):
            u = u + w[3 - k:4 - k, :] * jnp.where(row >= k, pltpu.roll(x, k, 0), 0.0)
        h = 0.5 * u
        o_ref[...] = h + h * jnp.tanh(h)

    return _pallas(
        body, name="conv_fwd", grid=(D_CONV // tc,),
        in_specs=[pl.BlockSpec((t, tc), lambda j: (0, j + off)), pl.BlockSpec((CONV_WIDTH, tc), lambda j: (0, j)),
                  pl.BlockSpec((1, tc), lambda j: (0, j))],
        out_specs=pl.BlockSpec((t, tc), lambda j: (0, j)),
        out_shape=jax.ShapeDtypeStruct((t, D_CONV), F32),
        compiler_params=_cp("parallel"))(proj, conv_w, conv_b)


def _softplus(u):
    e = jnp.exp(-jnp.abs(u))
    w = 1.0 + e
    l1p = jnp.where(w == 1.0, e, jnp.log(w) * (e / jnp.where(w == 1.0, 1.0, w - 1.0)))
    return jnp.maximum(u, 0.0) + l1p


def _chunks_per_step(nc):
    return max(d for d in range(1, 14) if nc % d == 0)


def _dt_prep(proj, dt_bias_l, a_log_l):
    t = proj.shape[0]
    nc = t // CHUNK
    q = CHUNK
    cps = _chunks_per_step(nc)
    rows = cps * q

    def body(raw_ref, bias_ref, alog_ref, dt_ref, acs_ref, acst_ref):
        ri = lax.broadcasted_iota(jnp.int32, (q, q), 0)
        ci = lax.broadcasted_iota(jnp.int32, (q, q), 1)
        tri = (ri >= ci).astype(F32)
        neg_a = -jnp.exp(alog_ref[...])
        for k in range(cps):
            rk = slice(q * k, q * (k + 1))
            sp = _softplus(raw_ref[rk, :] + bias_ref[...])
            row = pl.program_id(0) * rows + q * k + lax.broadcasted_iota(jnp.int32, (q, LANES), 0)
            dt = jnp.where(row >= PAD_LEAD, sp, 0.0)
            acs = jnp.dot(tri, dt * neg_a, preferred_element_type=F32, precision=HI)
            dt_ref[rk, :] = dt
            acs_ref[rk, :] = acs
            acst_ref[k] = acs.T

    return _pallas(
        body, name="dt_prep", grid=(nc // cps,),
        in_specs=[pl.BlockSpec((rows, LANES), lambda c: (c, ODT // LANES)), pl.BlockSpec((1, LANES), lambda c: (0, 0)),
                  pl.BlockSpec((1, LANES), lambda c: (0, 0))],
        out_specs=[pl.BlockSpec((rows, LANES), lambda c: (c, 0)), pl.BlockSpec((rows, LANES), lambda c: (c, 0)),
                   pl.BlockSpec((cps, LANES, q), lambda c: (c, 0, 0))],
        out_shape=[jax.ShapeDtypeStruct((t, LANES), F32), jax.ShapeDtypeStruct((t, LANES), F32),
                   jax.ShapeDtypeStruct((nc, LANES, q), F32)],
        compiler_params=_cp("parallel"))(proj, dt_bias_l, a_log_l)


class _HeadVals:
    pass


def _lane_head(shape):
    return lax.broadcasted_iota(jnp.int32, shape, len(shape) - 1) >> 6


def _group_heads(g, gi, dtx_ref, acsx_ref, arow_ref, dskx_ref):
    cols = slice(GROUP_W * gi, GROUP_W * (gi + 1))
    hv = _HeadVals()
    hv.dt = dtx_ref[:, cols]
    hv.acs = acsx_ref[:, cols]
    hv.acs_row = arow_ref[0, pl.ds(g, 1), :]
    hv.acs_last = hv.acs[CHUNK - 1:CHUNK, :]
    hv.dsk = dskx_ref[:, cols]
    return hv


def _expand_heads(v):
    rows = v.shape[0]
    heads = v[:, :GROUPS * GROUPS].reshape(rows, GROUPS, GROUPS)[:, :, :HPG].reshape(rows, SSD_HEADS)
    return jnp.repeat(heads, HEAD_DIM, axis=1)


def _chunk_rows(acst):
    nc = acst.shape[0]
    return acst[:, :GROUPS * GROUPS].reshape(nc, GROUPS, GROUPS, CHUNK)[:, :, :HPG].reshape(nc, GROUPS, GROUP_W)


def _head_tri(q, lower):
    ri = lax.broadcasted_iota(jnp.int32, (q, GROUP_W), 0)
    li = lax.broadcasted_iota(jnp.int32, (q, GROUP_W), 1) & (HEAD_DIM - 1)
    return ri >= li if lower else ri <= li


def _block_diag_mask():
    rb = lax.broadcasted_iota(jnp.int32, (GROUP_W, GROUP_W), 0) >> 6
    cb = lax.broadcasted_iota(jnp.int32, (GROUP_W, GROUP_W), 1) >> 6
    return rb == cb


def _block_diag(v, mask):
    return jnp.where(mask, jnp.concatenate([v] * HPG, axis=0), jnp.zeros((), v.dtype))


def _head_sums(v, r):
    return jnp.sum(jnp.where(_lane_head((1, GROUP_W)) == r, v, 0.0), axis=1, keepdims=True)


def _ssd_fwd(xbc, proj, dt_x, acs_x, acs_rows, dsk_x, ssd_norm_w):
    t = xbc.shape[0]
    q = CHUNK
    nc = t // q

    gps = SSD_FWD_GROUPS_PER_STEP
    gw, sw = gps * GROUP_W, gps * D_STATE

    def body(xs_ref, b_ref, c_ref, dt_ref, acs_ref, acst_ref, z_ref, dsk_ref, nw_ref,
             y_ref, ymix_ref, st_ref, state):
        @pl.when(pl.program_id(1) == 0)
        def _():
            state[...] = jnp.zeros_like(state)

        lower = _head_tri(q, True)
        bd_mask = _block_diag_mask()
        for gi in range(gps):
            g = gps * pl.program_id(0) + gi
            cols = slice(GROUP_W * gi, GROUP_W * (gi + 1))
            x = xs_ref[:, cols]
            bmb = _bf(b_ref[:, D_STATE * gi:D_STATE * (gi + 1)])
            cmb = _bf(c_ref[:, D_STATE * gi:D_STATE * (gi + 1)])
            hv = _group_heads(g, gi, dt_ref, acs_ref, acst_ref, dsk_ref)
            decay = jnp.exp(jnp.where(lower, hv.acs - hv.acs_row, NEG))
            m_all = _bf(_nt(cmb, jnp.concatenate([bmb] * HPG, axis=0)) * decay)
            xdt = x * hv.dt
            s_prev = state[gi]
            st_ref[0, gi] = s_prev
            y = (_mm(m_all, _block_diag(_bf(xdt), bd_mask)) + _mm(cmb, _bf(s_prev)) * jnp.exp(hv.acs) + hv.dsk * x)
            state[gi] = jnp.exp(hv.acs_last) * s_prev + _tn(bmb, _bf(xdt * jnp.exp(hv.acs_last - hv.acs)))
            y_ref[:, cols] = y
            z = z_ref[:, cols]
            yg = y * (z * _sigmoid(z))
            ms = jnp.mean(yg * yg, axis=-1, keepdims=True)
            ymix_ref[:, cols] = _bf(yg * lax.rsqrt(ms + EPS) * nw_ref[:, cols])

    return _pallas(
        body, name="ssd_fwd", grid=(GROUPS // gps, nc),
        in_specs=[pl.BlockSpec((q, gw), lambda g, c: (c, g)),
                  pl.BlockSpec((q, sw), lambda g, c: (c, D_SSD // sw + g)),
                  pl.BlockSpec((q, sw), lambda g, c: (c, (D_SSD + GROUPS * D_STATE) // sw + g)),
                  pl.BlockSpec((q, gw), lambda g, c: (c, g)), pl.BlockSpec((q, gw), lambda g, c: (c, g)),
                  pl.BlockSpec((1, GROUPS, GROUP_W), lambda g, c: (c, 0, 0)),
                  pl.BlockSpec((q, gw), lambda g, c: (c, g)),
                  pl.BlockSpec((1, gw), lambda g, c: (0, g)), pl.BlockSpec((1, gw), lambda g, c: (0, g))],
        out_specs=[pl.BlockSpec((q, gw), lambda g, c: (c, g)), pl.BlockSpec((q, gw), lambda g, c: (c, g)),
                   pl.BlockSpec((1, gps, D_STATE, GROUP_W), lambda g, c: (c, g, 0, 0))],
        out_shape=[jax.ShapeDtypeStruct((t, D_SSD), F32), jax.ShapeDtypeStruct((t, D_SSD), BF16),
                   jax.ShapeDtypeStruct((nc, GROUPS, D_STATE, GROUP_W), F32)],
        scratch_shapes=[pltpu.VMEM((gps, D_STATE, GROUP_W), F32)],
        compiler_params=_cp("parallel", "arbitrary"))(xbc, xbc, xbc, dt_x, acs_x, acs_rows, proj, dsk_x, ssd_norm_w)


def _swap_halves(v):
    lane = lax.broadcasted_iota(jnp.int32, v.shape, 1)
    return jnp.where((lane & (HEAD_DIM - 1)) < HEAD_DIM // 2, pltpu.roll(v, LANES - HEAD_DIM // 2, 1),
                     pltpu.roll(v, HEAD_DIM // 2, 1))


def _rope(qsrc, q_off, ksrc, k_off, cos_t, sin_t):
    t = qsrc.shape[0]
    tr = _tile(t, 832)
    q_scale = HEAD_DIM ** -0.5

    def body(q_ref, k_ref, cos_ref, sin_ref, qo_ref, ko_ref):
        cs = cos_ref[...]
        sn = sin_ref[...]
        for src, dst, width, scale in ((q_ref, qo_ref, D_ATT, q_scale), (k_ref, ko_ref, D_KV, 1.0)):
            for s in range(width // LANES):
                v = src[:, LANES * s:LANES * (s + 1)].astype(F32)
                dst[:, LANES * s:LANES * (s + 1)] = _bf((v * cs + _swap_halves(v) * sn) * scale)

    return _pallas(
        body, name="rope", grid=(t // tr,),
        in_specs=[pl.BlockSpec((tr, D_ATT), lambda i: (i, q_off // D_ATT)),
                  pl.BlockSpec((tr, D_KV), lambda i: (i, k_off // D_KV)),
                  pl.BlockSpec((tr, LANES), lambda i: (i, 0)), pl.BlockSpec((tr, LANES), lambda i: (i, 0))],
        out_specs=[pl.BlockSpec((tr, D_ATT), lambda i: (i, 0)), pl.BlockSpec((tr, D_KV), lambda i: (i, 0))],
        out_shape=[jax.ShapeDtypeStruct((t, D_ATT), BF16), jax.ShapeDtypeStruct((t, D_KV), BF16)],
        compiler_params=_cp("parallel"))(qsrc, ksrc, cos_t, sin_t)


def _attn_chunks_per_step(nc):
    return max(d for d in range(1, 6) if nc % d == 0)


def _band(ref, c):
    return [ref[pl.ds(pl.multiple_of(jnp.maximum(c - j, 0) * CHUNK, CHUNK), CHUNK), :] for j in (2, 1, 0)]


def _attn_probs(qh, kb, sink_col, valid):
    s = jnp.where(valid, _nt(qh, kb), NEG)
    m = jnp.maximum(jnp.max(s, axis=1, keepdims=True), sink_col)
    p = jnp.exp(s - m)
    psink = jnp.exp(sink_col - m)
    return p, psink, 1.0 / (jnp.sum(p, axis=1, keepdims=True) + psink)


def _attn_operands(c, q, k_refs, v_refs, sink_ref, h):
    qh = jnp.concatenate([q[:, HEAD_DIM * (REP * h + r):HEAD_DIM * (REP * h + r + 1)] for r in range(REP)], axis=0)
    kb = jnp.concatenate([k[:, HEAD_DIM * h:HEAD_DIM * (h + 1)] for k in k_refs], axis=0)
    vb = jnp.concatenate([_bf(v[:, HEAD_DIM * h:HEAD_DIM * (h + 1)]) for v in v_refs], axis=0)
    rows = lax.broadcasted_iota(jnp.int32, (REP * CHUNK, 1), 0) >> 6
    sink_col = jnp.zeros((REP * CHUNK, 1), F32)
    for r in range(REP):
        sink_col = jnp.where(rows == r, sink_ref[REP * h + r], sink_col)
    key_abs = (c - (BAND_CHUNKS - 1)) * CHUNK + lax.broadcasted_iota(jnp.int32, (1, BAND_CHUNKS * CHUNK), 1)
    return qh, kb, vb, sink_col, key_abs >= PAD_LEAD


def _attn_fwd(qr, kr, proj, sinks):
    t = qr.shape[0]
    nc = t // CHUNK
    cps = _attn_chunks_per_step(nc)
    rows = cps * CHUNK

    def body(q_ref, k_ref, v_ref, g_ref, sink_ref, o_ref):
        for j in range(cps):
            c = pl.program_id(0) * cps + j
            rj = slice(CHUNK * j, CHUNK * (j + 1))
            ks, vs = _band(k_ref, c), _band(v_ref, c)
            q = q_ref[rj, :]
            outs = []
            for h in range(KV_HEADS):
                qh, kb, vb, sink_col, valid = _attn_operands(c, q, ks, vs, sink_ref, h)
                p, _, inv = _attn_probs(qh, kb, sink_col, valid)
                o = _mm(_bf(p), vb) * inv
                outs += [o[CHUNK * r:CHUNK * (r + 1)] for r in range(REP)]
            att = jnp.concatenate(outs, axis=1)
            gate = g_ref[rj, :]
            o_ref[rj, :] = _bf(att * (gate * _sigmoid(gate)))

    return _pallas(
        body, name="attn_fwd", grid=(nc // cps,),
        in_specs=[pl.BlockSpec((rows, D_ATT), lambda i: (i, 0)), pl.BlockSpec((t, D_KV), lambda i: (0, 0)),
                  pl.BlockSpec((t, D_KV), lambda i: (0, OV // D_KV)),
                  pl.BlockSpec((rows, D_ATT), lambda i: (i, OG // D_ATT)), pl.BlockSpec(memory_space=pltpu.SMEM)],
        out_specs=pl.BlockSpec((rows, D_ATT), lambda i: (i, 0)),
        out_shape=jax.ShapeDtypeStruct((t, D_ATT), BF16),
        compiler_params=_cp("parallel"))(qr, kr, proj, proj, sinks)


def _outproj(ymix, amix, w_out):
    t = ymix.shape[0]
    tm, tn = _tile(t, 1040), 1024

    def body(y_ref, a_ref, wy_ref, wa_ref, o_ref):
        o_ref[...] = _mm(y_ref[...], wy_ref[...]) + _mm(a_ref[...], wa_ref[...])

    return _pallas(
        body, name="outproj", grid=(t // tm, D_MODEL // tn),
        in_specs=[pl.BlockSpec((tm, D_SSD), lambda i, j: (i, 0)), pl.BlockSpec((tm, D_ATT), lambda i, j: (i, 0)),
                  pl.BlockSpec((D_SSD, tn), lambda i, j: (0, j)),
                  pl.BlockSpec((D_ATT, tn), lambda i, j: (D_SSD // D_ATT, j))],
        out_specs=pl.BlockSpec((tm, tn), lambda i, j: (i, j)),
        out_shape=jax.ShapeDtypeStruct((t, D_MODEL), F32),
        compiler_params=_cp("parallel", "parallel"))(ymix, amix, w_out, w_out)


def _post_loss(out, x, target, norm_post_w):
    t = out.shape[0]
    nc = t // CHUNK
    cps = _attn_chunks_per_step(nc)
    rows = cps * CHUNK

    def body(o_ref, *refs):
        x_refs, tg_refs = refs[:cps], refs[cps:2 * cps]
        nw_ref, dout_ref, dy_ref, loss_ref, gnw_ref = refs[2 * cps:]
        i = pl.program_id(0)

        @pl.when(i == 0)
        def _():
            loss_ref[...] = jnp.zeros_like(loss_ref)
            gnw_ref[...] = jnp.zeros_like(gnw_ref)

        nw = nw_ref[...]
        loss = jnp.zeros((), F32)
        gnw = jnp.zeros((1, D_MODEL), F32)
        for k in range(cps):
            rk = slice(CHUNK * k, CHUNK * (k + 1))
            frames = i * cps + k > 0
            o = o_ref[rk, :]
            rstd = lax.rsqrt(jnp.mean(o * o, axis=-1, keepdims=True) + EPS)
            n = o * rstd
            err = jnp.where(frames, x_refs[k][...] + n * nw - tg_refs[k][...], 0.0)
            loss = loss + jnp.sum(err * err)
            dy = err * (1.0 / D_MODEL)
            dy_ref[rk, :] = dy
            gnw = gnw + jnp.sum(dy * n, axis=0, keepdims=True)
            dn = dy * nw
            dout_ref[rk, :] = _bf(rstd * (dn - n * jnp.mean(dn * n, axis=-1, keepdims=True)))
        loss_ref[...] += loss * (0.5 / D_MODEL)
        gnw_ref[...] += gnw

    lower = [pl.BlockSpec((CHUNK, D_MODEL), functools.partial(lambda i, k: (jnp.maximum(i * cps + k - 1, 0), 0), k=k))
             for k in range(cps)]
    return _pallas(
        body, name="post_loss", grid=(nc // cps,),
        in_specs=[pl.BlockSpec((rows, D_MODEL), lambda i: (i, 0))] + lower + lower
        + [pl.BlockSpec((1, D_MODEL), lambda i: (0, 0))],
        out_specs=[pl.BlockSpec((rows, D_MODEL), lambda i: (i, 0)), pl.BlockSpec((rows, D_MODEL), lambda i: (i, 0)),
                   pl.BlockSpec((8, LANES), lambda i: (0, 0)), pl.BlockSpec((1, D_MODEL), lambda i: (0, 0))],
        out_shape=[jax.ShapeDtypeStruct((t, D_MODEL), BF16), jax.ShapeDtypeStruct((t, D_MODEL), F32),
                   jax.ShapeDtypeStruct((8, LANES), F32), jax.ShapeDtypeStruct((1, D_MODEL), F32)],
        compiler_params=_cp("arbitrary"))(out, *([x] * cps), *([target] * cps), norm_post_w)


def _carried(grid, carry):
    if carry is None:
        return [], [], [], [], lambda refs: None, lambda refs: None
    hn = carry.shape[1] // 2

    def at(ids, which):
        cond = None
        for d, size in enumerate(grid):
            here = pl.program_id(d) == (0 if which == "first" else size - 1)
            cond = here if cond is None else cond & here
        return cond

    def start(refs):
        @pl.when(at(grid, "first"))
        def _():
            for cp in _pair_copies(*refs):
                cp.start()

    def finish(refs):
        @pl.when(at(grid, "last"))
        def _():
            for cp in _pair_copies(*refs):
                cp.wait()

    return ([ANY], [ANY], [jax.ShapeDtypeStruct((carry.shape[0], hn), F32)],
            [pltpu.SemaphoreType.DMA((PAIR_CHUNKS,)), pltpu.SemaphoreType.DMA((PAIR_CHUNKS,))], start, finish)


def _nt_matmul(a, b, name, carry=None):
    t, k = a.shape
    n = b.shape[0]
    tm, tn = _tile(t, 1040), 1024
    grid = (t // tm, n // tn)
    cin, cout, cshape, cscratch, start, finish = _carried(grid, carry)

    def body(a_ref, b_ref, *refs):
        o_ref = refs[len(cin)]
        comm = (refs[0], refs[2], refs[3], refs[4]) if carry is not None else None
        start(comm)
        o_ref[...] = _nt(a_ref[...], b_ref[...])
        finish(comm)

    res = _pallas(
        body, name=name, grid=grid,
        in_specs=[pl.BlockSpec((tm, k), lambda i, j: (i, 0)), pl.BlockSpec((tn, k), lambda i, j: (j, 0))] + cin,
        out_specs=[pl.BlockSpec((tm, tn), lambda i, j: (i, j))] + cout,
        out_shape=[jax.ShapeDtypeStruct((t, n), F32)] + cshape, scratch_shapes=cscratch,
        compiler_params=_cp("arbitrary", "arbitrary"))(a, b, *([carry] if carry is not None else []))
    return res if carry is not None else res[0]


def _tn_matmul(a, b, name, carry=None):
    t, m = a.shape
    n = b.shape[1]
    tk, tm, tn = _tile(t, 832), min(m, 2048), min(n, 2048)
    nk = t // tk
    grid = (m // tm, n // tn, nk)
    cin, cout, cshape, cscratch, start, finish = _carried(grid, carry)

    def body(a_ref, b_ref, *refs):
        o_ref = refs[len(cin)]
        comm = (refs[0], refs[2], refs[3], refs[4]) if carry is not None else None
        start(comm)

        @pl.when(pl.program_id(2) == 0)
        def _():
            o_ref[...] = jnp.zeros_like(o_ref)
        o_ref[...] += _tn(a_ref[...], b_ref[...])
        finish(comm)

    res = _pallas(
        body, name=name, grid=grid,
        in_specs=[pl.BlockSpec((tk, tm), lambda i, j, k: (k, i)), pl.BlockSpec((tk, tn), lambda i, j, k: (k, j))] + cin,
        out_specs=[pl.BlockSpec((tm, tn), lambda i, j, k: (i, j))] + cout,
        out_shape=[jax.ShapeDtypeStruct((m, n), F32)] + cshape, scratch_shapes=cscratch,
        compiler_params=_cp("arbitrary", "arbitrary", "arbitrary"))(a, b, *([carry] if carry is not None else []))
    return res if carry is not None else res[0]


def _attn_bwd(qr, kr, proj, dmix, sinks, ga):
    t = qr.shape[0]
    nc = t // CHUNK
    cps = _attn_chunks_per_step(nc)
    nsteps = nc // cps
    rows_step = cps * CHUNK

    def body(q_ref, k_ref, v_ref, g_ref, da_ref, sink_ref, ga_ref, dq_ref, dg_ref, dk_ref, dv_ref, gs_ref,
             got_ref, send_sems, recv_sems):
        step = pl.program_id(0)

        @pl.when(step == 0)
        def _():
            for cp in _exchange_copies(ga_ref, got_ref, send_sems, recv_sems):
                cp.start()
            dk_ref[...] = jnp.zeros_like(dk_ref)
            dv_ref[...] = jnp.zeros_like(dv_ref)
            gs_ref[...] = jnp.zeros_like(gs_ref)

        lane = lax.broadcasted_iota(jnp.int32, (1, LANES), 1)
        rows = lax.broadcasted_iota(jnp.int32, (REP * CHUNK, 1), 0) >> 6
        gs = jnp.zeros((1, LANES), F32)
        dk_parts = [[] for _ in range(cps + BAND_CHUNKS - 1)]
        dv_parts = [[] for _ in range(cps + BAND_CHUNKS - 1)]
        for j in range(cps):
            c = step * cps + j
            rj = slice(CHUNK * j, CHUNK * (j + 1))
            ks, vs = _band(k_ref, c), _band(v_ref, c)
            q = q_ref[rj, :]
            gate = g_ref[rj, :]
            sg = _sigmoid(gate)
            da = da_ref[rj, :]
            datt = da * (gate * sg)
            dqs, atts, dks, dvs = [], [], [], []
            for h in range(KV_HEADS):
                qh, kb, vb, sink_col, valid = _attn_operands(c, q, ks, vs, sink_ref, h)
                p, psink, inv = _attn_probs(qh, kb, sink_col, valid)
                pb = _bf(p)
                o = _mm(pb, vb) * inv
                do = jnp.concatenate([datt[:, HEAD_DIM * (REP * h + r):HEAD_DIM * (REP * h + r + 1)]
                                      for r in range(REP)], axis=0)
                dob = _bf(do * inv)
                delta = jnp.sum(do * o, axis=1, keepdims=True) * inv
                ds = _bf(p * (_nt(dob, vb) - delta))
                gsink = -psink * delta
                for r in range(REP):
                    gs = gs + jnp.where(lane == REP * h + r, jnp.sum(jnp.where(rows == r, gsink, 0.0)), 0.0)
                dqh = _mm(ds, kb)
                dqs += [dqh[CHUNK * r:CHUNK * (r + 1)] for r in range(REP)]
                atts += [o[CHUNK * r:CHUNK * (r + 1)] for r in range(REP)]
                dks.append(_tn(ds, qh))
                dvs.append(_tn(pb, dob))
            dq_ref[rj, :] = jnp.concatenate(dqs, axis=1)
            att = jnp.concatenate(atts, axis=1)
            dg_ref[rj, :] = _bf(da * att * (sg * (1.0 + gate * (1.0 - sg))))
            dkf = jnp.concatenate(dks, axis=1)
            dvf = jnp.concatenate(dvs, axis=1)
            for b in range(BAND_CHUNKS):
                dk_parts[j + b].append(dkf[CHUNK * b:CHUNK * (b + 1)])
                dv_parts[j + b].append(dvf[CHUNK * b:CHUNK * (b + 1)])
        gs_ref[0:1, :] += gs
        for rel in range(cps + BAND_CHUNKS - 1):
            r0 = pl.multiple_of(jnp.maximum(step * cps - (BAND_CHUNKS - 1) + rel, 0) * CHUNK, CHUNK)
            dk_ref[pl.ds(r0, CHUNK), :] += sum(dk_parts[rel][1:], dk_parts[rel][0])
            dv_ref[pl.ds(r0, CHUNK), :] += sum(dv_parts[rel][1:], dv_parts[rel][0])

        @pl.when(step == nsteps - 1)
        def _():
            for cp in _exchange_copies(ga_ref, got_ref, send_sems, recv_sems):
                cp.wait()

    return _pallas(
        body, name="attn_bwd", grid=(nsteps,),
        in_specs=[pl.BlockSpec((rows_step, D_ATT), lambda i: (i, 0)), pl.BlockSpec((t, D_KV), lambda i: (0, 0)),
                  pl.BlockSpec((t, D_KV), lambda i: (0, OV // D_KV)),
                  pl.BlockSpec((rows_step, D_ATT), lambda i: (i, OG // D_ATT)),
                  pl.BlockSpec((rows_step, D_ATT), lambda i: (i, D_SSD // D_ATT)),
                  pl.BlockSpec(memory_space=pltpu.SMEM), ANY],
        out_specs=[pl.BlockSpec((rows_step, D_ATT), lambda i: (i, 0)), pl.BlockSpec((rows_step, D_ATT), lambda i: (i, 0)),
                   pl.BlockSpec((t, D_KV), lambda i: (0, 0)), pl.BlockSpec((t, D_KV), lambda i: (0, 0)),
                   pl.BlockSpec((8, LANES), lambda i: (0, 0)), ANY],
        out_shape=[jax.ShapeDtypeStruct((t, D_ATT), F32), jax.ShapeDtypeStruct((t, D_ATT), BF16),
                   jax.ShapeDtypeStruct((t, D_KV), F32), jax.ShapeDtypeStruct((t, D_KV), F32),
                   jax.ShapeDtypeStruct((8, LANES), F32), _exchange_shape(ga)],
        scratch_shapes=_exchange_scratch(),
        compiler_params=_cp("arbitrary"))(qr, kr, proj, proj, dmix, sinks, ga)


def _ssd_bwd(dmix, y_ssd, xbc, proj, dt_x, acs_x, acs_rows, states, dsk_x, ssd_norm_w):
    t = xbc.shape[0]
    q = CHUNK
    nc = t // q
    gps = SSD_BWD_GROUPS_PER_STEP
    gw, sw = gps * GROUP_W, gps * D_STATE

    def body(dmix_ref, y_ref, z_ref, nw_ref, xs_ref, b_ref, c_ref, dt_ref, acs_ref, acst_ref, st_ref, dsk_ref,
             dz_ref, dxs_ref, db_ref, dc_ref, dacs_ref, ddt_ref, gnw_ref, gdsk_ref, dstate):
        @pl.when(pl.program_id(1) == 0)
        def _():
            dstate[...] = jnp.zeros_like(dstate)
            gnw_ref[...] = jnp.zeros_like(gnw_ref)
            gdsk_ref[...] = jnp.zeros_like(gdsk_ref)

        last_row = lax.broadcasted_iota(jnp.int32, (q, 1), 0) == q - 1
        lane = lax.broadcasted_iota(jnp.int32, (q, LANES), 1)
        lane1 = lax.broadcasted_iota(jnp.int32, (8, LANES), 1)
        lower, upper = _head_tri(q, True), _head_tri(q, False)
        bd_mask = _block_diag_mask()
        for gi in range(gps):
            g = gps * pl.program_id(0) + gi
            cols = slice(GROUP_W * gi, GROUP_W * (gi + 1))
            scols = slice(D_STATE * gi, D_STATE * (gi + 1))
            y = y_ref[:, cols]
            z = z_ref[:, cols]
            sz = _sigmoid(z)
            silu_z = z * sz
            yg = y * silu_z
            rstd = lax.rsqrt(jnp.mean(yg * yg, axis=-1, keepdims=True) + EPS)
            n = yg * rstd
            dout = dmix_ref[:, cols]
            gnw_ref[:, cols] += jnp.sum(dout * n, axis=0, keepdims=True)
            dn = dout * nw_ref[:, cols]
            dyg = rstd * (dn - n * jnp.mean(dn * n, axis=-1, keepdims=True))
            dy = dyg * silu_z
            dz_ref[:, cols] = _bf(dyg * y * (sz * (1.0 + z * (1.0 - sz))))

            x = xs_ref[:, cols]
            bmb, cmb = _bf(b_ref[:, scols]), _bf(c_ref[:, scols])
            hv = _group_heads(g, gi, dt_ref, acs_ref, acst_ref, dsk_ref)
            dec = jnp.exp(jnp.where(lower, hv.acs - hv.acs_row, NEG))
            dect = jnp.exp(jnp.where(upper, hv.acs_row - hv.acs, NEG))
            b4 = jnp.concatenate([bmb] * HPG, axis=0)
            c4 = jnp.concatenate([cmb] * HPG, axis=0)
            m_all = _nt(cmb, b4) * dec
            mt_all = _nt(bmb, c4) * dect
            xdt = x * hv.dt
            xdt_b, dyb = _bf(xdt), _bf(dy)
            x_bd, dy_bd = _block_diag(xdt_b, bd_mask), _block_diag(dyb, bd_mask)
            s_prev = st_ref[0, gi]
            spb = _bf(s_prev)
            ds_new = dstate[gi]
            dsb = _bf(ds_new)
            e = jnp.exp(hv.acs)
            elast = jnp.exp(hv.acs_last)
            dte = jnp.exp(hv.acs_last - hv.acs)
            bds = _mm(bmb, dsb)
            dxdt = _mm(_bf(mt_all), dy_bd) + bds * dte
            dm = _nt(dyb, x_bd)
            dmt = _nt(xdt_b, dy_bd)
            dye = _bf(dy * e)
            dc_ref[:, scols] = _mm(_bf(dm * dec), b4) + _nt(dye, spb)
            db_ref[:, scols] = _mm(_bf(dmt * dect), c4) + _nt(_bf(xdt * dte), dsb)
            dstate[gi] = elast * ds_new + _tn(cmb, dye)
            dxs_ref[:, cols] = dxdt * hv.dt + hv.dsk * dy
            ddte_dte = bds * xdt * dte
            dacs_l = dm * m_all - dmt * mt_all + dy * _mm(cmb, spb) * e - ddte_dte
            dlast_l = (jnp.sum(ddte_dte, axis=0, keepdims=True)
                       + jnp.sum(s_prev * ds_new, axis=0, keepdims=True) * elast)
            ddt_l = dxdt * x
            gdsk_l = jnp.sum(dy * x, axis=0, keepdims=True)
            dacs_out = jnp.zeros((q, LANES), F32)
            ddt_out = jnp.zeros((q, LANES), F32)
            gdsk = jnp.zeros((8, LANES), F32)
            for r in range(HPG):
                dacs = _head_sums(dacs_l, r) + jnp.where(last_row, _head_sums(dlast_l, r), 0.0)
                dacs_out = jnp.where(lane == r, dacs, dacs_out)
                ddt_out = jnp.where(lane == r, _head_sums(ddt_l, r), ddt_out)
                gdsk = gdsk + jnp.where(lane1 == r, _head_sums(gdsk_l, r), 0.0)
            dacs_ref[:, LANES * gi:LANES * (gi + 1)] = dacs_out
            ddt_ref[:, LANES * gi:LANES * (gi + 1)] = ddt_out
            gdsk_ref[gi] += gdsk

    rev = lambda c: nc - 1 - c
    wide = pl.BlockSpec((q, gw), lambda g, c: (rev(c), g))
    return _pallas(
        body, name="ssd_bwd", grid=(GROUPS // gps, nc),
        in_specs=[wide, wide, wide, pl.BlockSpec((1, gw), lambda g, c: (0, g)), wide,
                  pl.BlockSpec((q, sw), lambda g, c: (rev(c), D_SSD // sw + g)),
                  pl.BlockSpec((q, sw), lambda g, c: (rev(c), (D_SSD + GROUPS * D_STATE) // sw + g)),
                  wide, wide, pl.BlockSpec((1, GROUPS, GROUP_W), lambda g, c: (rev(c), 0, 0)),
                  pl.BlockSpec((1, gps, D_STATE, GROUP_W), lambda g, c: (rev(c), g, 0, 0)),
                  pl.BlockSpec((1, gw), lambda g, c: (0, g))],
        out_specs=[wide, wide,
                   pl.BlockSpec((q, sw), lambda g, c: (rev(c), g)), pl.BlockSpec((q, sw), lambda g, c: (rev(c), g)),
                   pl.BlockSpec((q, gps * LANES), lambda g, c: (rev(c), g)),
                   pl.BlockSpec((q, gps * LANES), lambda g, c: (rev(c), g)),
                   pl.BlockSpec((1, gw), lambda g, c: (0, g)), pl.BlockSpec((gps, 8, LANES), lambda g, c: (g, 0, 0))],
        out_shape=[jax.ShapeDtypeStruct((t, D_SSD), BF16), jax.ShapeDtypeStruct((t, D_SSD), F32),
                   jax.ShapeDtypeStruct((t, GROUPS * D_STATE), F32), jax.ShapeDtypeStruct((t, GROUPS * D_STATE), F32),
                   jax.ShapeDtypeStruct((t, GROUPS * LANES), F32), jax.ShapeDtypeStruct((t, GROUPS * LANES), F32),
                   jax.ShapeDtypeStruct((1, D_SSD), F32), jax.ShapeDtypeStruct((GROUPS, 8, LANES), F32)],
        scratch_shapes=[pltpu.VMEM((gps, D_STATE, GROUP_W), F32)],
        compiler_params=_cp("parallel", "arbitrary"))(dmix, y_ssd, proj, ssd_norm_w, xbc, xbc, xbc, dt_x, acs_x, acs_rows,
                                                      states, dsk_x)


def _dt_bwd(dacs_g, ddt_g, dt, proj, dt_bias_l, a_log_l):
    t = dt.shape[0]
    q = CHUNK
    nc = t // q
    cps = _chunks_per_step(nc)
    rows = cps * q

    def body(dacs_ref, ddt_ref, dt_ref, raw_ref, bias_ref, alog_ref, draw_ref, ga_ref, gb_ref):
        @pl.when(pl.program_id(0) == 0)
        def _():
            ga_ref[...] = jnp.zeros_like(ga_ref)
            gb_ref[...] = jnp.zeros_like(gb_ref)

        lane = lax.broadcasted_iota(jnp.int32, (q, LANES), 1)
        ri = lax.broadcasted_iota(jnp.int32, (q, q), 0)
        ci = lax.broadcasted_iota(jnp.int32, (q, q), 1)
        triu = (ri <= ci).astype(F32)
        a = -jnp.exp(alog_ref[...])
        used = (lane & (GROUPS - 1)) < HPG
        ga = jnp.zeros((1, LANES), F32)
        gb = jnp.zeros((1, LANES), F32)
        for k in range(cps):
            rk = slice(q * k, q * (k + 1))
            dacs = jnp.zeros((q, LANES), F32)
            ddt = jnp.zeros((q, LANES), F32)
            for g in range(GROUPS):
                mask = (lane >= GROUPS * g) & (lane < GROUPS * g + HPG)
                sl = slice(LANES * g, LANES * (g + 1))
                if g == 0:
                    dacs = jnp.where(mask, dacs_ref[rk, sl], dacs)
                    ddt = jnp.where(mask, ddt_ref[rk, sl], ddt)
                else:
                    dacs = jnp.where(mask, pltpu.roll(dacs_ref[rk, sl], GROUPS * g, 1), dacs)
                    ddt = jnp.where(mask, pltpu.roll(ddt_ref[rk, sl], GROUPS * g, 1), ddt)
            dda = jnp.dot(triu, dacs, preferred_element_type=F32, precision=HI)
            row = pl.program_id(0) * rows + q * k + lax.broadcasted_iota(jnp.int32, (q, LANES), 0)
            dsp = jnp.where((row >= PAD_LEAD) & used, dda * a + ddt, 0.0)
            draw = dsp * _sigmoid(raw_ref[rk, :] + bias_ref[...])
            draw_ref[rk, :] = _bf(draw)
            gb = gb + jnp.sum(draw, axis=0, keepdims=True)
            ga = ga + jnp.sum(jnp.where(used, dda * dt_ref[rk, :], 0.0), axis=0, keepdims=True)
        gb_ref[0:1, :] += gb
        ga_ref[0:1, :] += ga * a

    return _pallas(
        body, name="dt_bwd", grid=(nc // cps,),
        in_specs=[pl.BlockSpec((rows, GROUPS * LANES), lambda c: (c, 0)),
                  pl.BlockSpec((rows, GROUPS * LANES), lambda c: (c, 0)),
                  pl.BlockSpec((rows, LANES), lambda c: (c, 0)), pl.BlockSpec((rows, LANES), lambda c: (c, ODT // LANES)),
                  pl.BlockSpec((1, LANES), lambda c: (0, 0)), pl.BlockSpec((1, LANES), lambda c: (0, 0))],
        out_specs=[pl.BlockSpec((rows, LANES), lambda c: (c, 0)), pl.BlockSpec((8, LANES), lambda c: (0, 0)),
                   pl.BlockSpec((8, LANES), lambda c: (0, 0))],
        out_shape=[jax.ShapeDtypeStruct((t, LANES), BF16), jax.ShapeDtypeStruct((8, LANES), F32),
                   jax.ShapeDtypeStruct((8, LANES), F32)],
        compiler_params=_cp("arbitrary"))(dacs_g, ddt_g, dt, proj, dt_bias_l, a_log_l)


def _conv_bwd(dseg, proj, conv_w, conv_b, col_off, name):
    t, width = dseg.shape
    tc = 128
    rt = _tile(t, 320)
    off_p = (OXS + col_off) // tc
    off_w = col_off // tc

    def body(d_ref, x_ref, w_ref, b_ref, dx_ref, gw_ref, gb_ref, xp, dup):
        xp[0:8, :] = jnp.zeros((8, tc), F32)
        xp[8:t + 8, :] = x_ref[...]
        dup[t:t + 8, :] = jnp.zeros((8, tc), F32)
        w = w_ref[...]
        bias = b_ref[...]

        def first(i, acc):
            r0 = pl.multiple_of(i * rt, 8)
            xs = [xp[pl.ds(r0 + 5 + k, rt), :] for k in range(CONV_WIDTH)]
            u = bias + w[3:4, :] * xs[3] + w[2:3, :] * xs[2] + w[1:2, :] * xs[1] + w[0:1, :] * xs[0]
            su = 0.5 + 0.5 * jnp.tanh(0.5 * u)
            du = d_ref[pl.ds(r0, rt), :] * (su * (1.0 + u * (1.0 - su)))
            dup[pl.ds(r0, rt), :] = du
            return tuple(acc[k] + jnp.sum(du * xs[k], axis=0, keepdims=True) for k in range(CONV_WIDTH)) + (
                acc[CONV_WIDTH] + jnp.sum(du, axis=0, keepdims=True),)

        zero = jnp.zeros((1, tc), F32)
        acc = lax.fori_loop(0, t // rt, first, (zero,) * (CONV_WIDTH + 1))
        gw_ref[...] = jnp.concatenate(acc[:CONV_WIDTH], axis=0)
        gb_ref[...] = acc[CONV_WIDTH]

        def second(i, carry):
            r0 = pl.multiple_of(i * rt, 16)
            dx_ref[pl.ds(r0, rt), :] = _bf(w[3:4, :] * dup[pl.ds(r0, rt), :] + w[2:3, :] * dup[pl.ds(r0 + 1, rt), :]
                                          + w[1:2, :] * dup[pl.ds(r0 + 2, rt), :] + w[0:1, :] * dup[pl.ds(r0 + 3, rt), :])
            return carry

        lax.fori_loop(0, t // rt, second, 0)

    return _pallas(
        body, name=name, grid=(width // tc,),
        in_specs=[pl.BlockSpec((t, tc), lambda j: (0, j)), pl.BlockSpec((t, tc), lambda j: (0, j + off_p)),
                  pl.BlockSpec((CONV_WIDTH, tc), lambda j: (0, j + off_w)), pl.BlockSpec((1, tc), lambda j: (0, j + off_w))],
        out_specs=[pl.BlockSpec((t, tc), lambda j: (0, j)), pl.BlockSpec((CONV_WIDTH, tc), lambda j: (0, j)),
                   pl.BlockSpec((1, tc), lambda j: (0, j))],
        out_shape=[jax.ShapeDtypeStruct((t, width), BF16), jax.ShapeDtypeStruct((CONV_WIDTH, width), F32),
                   jax.ShapeDtypeStruct((1, width), F32)],
        scratch_shapes=[pltpu.VMEM((t + 8, tc), F32), pltpu.VMEM((t + 8, tc), F32)],
        compiler_params=_cp("parallel"))(dseg, proj, conv_w, conv_b)


def _dinproj(segs, w_re, hpad, norm_w, dy_t, ga):
    t = segs[0].shape[0]
    d = hpad.shape[1]
    tm, tk = _tile(t, 416), SEG_TILE
    counts = [s.shape[1] // tk for s in segs]
    firsts = [sum(counts[:s]) for s in range(len(segs))]
    nk = sum(counts)
    assert nk * tk == w_re.shape[1]
    ni = t // tm
    ns = len(segs)

    def body(*refs):
        seg_refs = refs[:ns]
        w_ref, h_ref, nw_ref, dy_ref, ga_ref, dh_ref, gnw_ref, got_ref, acc, send_sems, recv_sems = refs[ns:]
        i, k = pl.program_id(0), pl.program_id(1)

        @pl.when((i == 0) & (k == 0))
        def _():
            for cp in _exchange_copies(ga_ref, got_ref, send_sems, recv_sems):
                cp.start()
            gnw_ref[...] = jnp.zeros_like(gnw_ref)

        @pl.when(k == 0)
        def _():
            acc[...] = jnp.zeros_like(acc)

        for s in range(ns):
            @pl.when((k >= firsts[s]) & (k < firsts[s] + counts[s]))
            def _(s=s):
                acc[...] += _nt(seg_refs[s][...], w_ref[...])

        @pl.when(k == nk - 1)
        def _():
            h = h_ref[...]
            rstd = lax.rsqrt(jnp.mean(h * h, axis=-1, keepdims=True) + EPS)
            nrm = h * rstd
            dhn = acc[...]
            gnw_ref[...] += jnp.sum(dhn * nrm, axis=0, keepdims=True)
            dn = dhn * nw_ref[...]
            dh_ref[...] = rstd * (dn - nrm * jnp.mean(dn * nrm, axis=-1, keepdims=True)) + dy_ref[...]

        @pl.when((i == ni - 1) & (k == nk - 1))
        def _():
            for cp in _exchange_copies(ga_ref, got_ref, send_sems, recv_sems):
                cp.wait()

    seg_specs = [pl.BlockSpec((tm, tk), functools.partial(lambda i, k, f0, n0: (i, jnp.clip(k - f0, 0, n0 - 1)),
                                                          f0=firsts[s], n0=counts[s])) for s in range(ns)]
    return _pallas(
        body, name="dinproj", grid=(ni, nk),
        in_specs=seg_specs + [pl.BlockSpec((d, tk), lambda i, k: (0, k)),
                              pl.BlockSpec((tm, d), lambda i, k: (i, 0)), pl.BlockSpec((1, d), lambda i, k: (0, 0)),
                              pl.BlockSpec((tm, d), lambda i, k: (i, 0)), ANY],
        out_specs=[pl.BlockSpec((tm, d), lambda i, k: (i, 0)), pl.BlockSpec((1, d), lambda i, k: (0, 0)), ANY],
        out_shape=[jax.ShapeDtypeStruct((t, d), F32), jax.ShapeDtypeStruct((1, d), F32), _exchange_shape(ga)],
        scratch_shapes=[pltpu.VMEM((tm, d), F32)] + _exchange_scratch(),
        compiler_params=_cp("arbitrary", "arbitrary"))(*segs, w_re, hpad, norm_w, dy_t, ga)


def _spread_heads(v):
    v = jnp.pad(v.reshape(GROUPS, HPG), ((0, 0), (0, GROUPS - HPG))).reshape(1, GROUPS * GROUPS)
    return jnp.pad(v, ((0, 0), (0, LANES - GROUPS * GROUPS)))


def _gather_heads(v):
    return v[0:1, :GROUPS * GROUPS].reshape(GROUPS, GROUPS)[:, :HPG].reshape(1, SSD_HEADS)


def _rope_tables(t):
    half = HEAD_DIM // 2
    inv = ROPE_THETA ** (-jnp.arange(half, dtype=F32) / half)
    pos = (jnp.arange(t) - PAD_LEAD).astype(F32)
    ang = pos[:, None] * inv[None, :]
    cos, sin = jnp.cos(ang), jnp.sin(ang)
    cos_t = jnp.concatenate([cos, cos, cos, cos], axis=1)
    sin_t = jnp.concatenate([-sin, sin, -sin, sin], axis=1)
    return cos_t, sin_t


def _column_pieces():
    runs = [(0, OB + 2 * GROUPS * D_STATE, 0)]
    o = OB + 2 * GROUPS * D_STATE
    runs += [(o + HPG * g, HPG, ODT + GROUPS * g) for g in range(GROUPS)]
    o += SSD_HEADS
    for width, dst in ((D_ATT, OQ), (D_KV, OK), (D_KV, OV), (D_ATT, OG)):
        runs.append((o, width, dst))
        o += width
    assert o == D_IN
    pieces = []
    for o0, width, dst in runs:
        for j in range(N_SHARD):
            lo, hi = max(o0, W_IN_SHARD * j), min(o0 + width, W_IN_SHARD * (j + 1))
            if lo < hi:
                pieces.append((j, lo - W_IN_SHARD * j, hi - W_IN_SHARD * j, dst + lo - o0))
    return pieces


def _shards_to_re(w_all):
    _, k, _ = w_all.shape
    tr = 256

    def body(x_ref, o_ref):
        o_ref[:, ODT:ODT + DT_SLAB] = jnp.zeros((tr, DT_SLAB), o_ref.dtype)
        for j, c0, c1, d0 in _column_pieces():
            o_ref[:, d0:d0 + c1 - c0] = x_ref[j, :, c0:c1]

    return _pallas(body, name="shards_to_re", grid=(k // tr,),
                   in_specs=[pl.BlockSpec((N_SHARD, tr, W_IN_SHARD), lambda i: (0, i, 0))],
                   out_specs=pl.BlockSpec((tr, N_RE), lambda i: (i, 0)),
                   out_shape=jax.ShapeDtypeStruct((k, N_RE), w_all.dtype), compiler_params=_cp("parallel"))(w_all)


def _pair_add_to_shards(parts, got, pieces, shard_rows, core, name):
    n = parts[0].shape[1]
    hn = n // 2
    tc = 128
    nt = hn // tc
    ns = len(parts)
    starts = [sum(p.shape[0] for p in parts[:s]) for s in range(ns)]
    moves = []
    for j, c0, c1, d0 in pieces:
        for s, p in enumerate(parts):
            lo, hi = max(d0, starts[s]), min(d0 + c1 - c0, starts[s] + p.shape[0])
            if lo < hi:
                moves.append((s, lo - starts[s], j, c0 + lo - d0, hi - lo))
    assert sum(m[4] for m in moves) == N_SHARD * shard_rows

    def body(core_ref, *refs):
        own, theirs, o_ref, acc = refs[:ns], refs[ns:2 * ns], refs[2 * ns], refs[2 * ns + 1]
        for s, r0, j, c0, rows in moves:
            acc[j, c0:c0 + rows, :] = own[s][r0:r0 + rows, :] + theirs[s][r0:r0 + rows, :]
        o_ref[...] = _bf(acc[...])

    return _pallas(
        body, name=name,
        grid_spec=pltpu.PrefetchScalarGridSpec(
            num_scalar_prefetch=1, grid=(nt,),
            in_specs=[pl.BlockSpec((p.shape[0], tc), lambda i, core_ref: (0, core_ref[0] * nt + i)) for p in parts]
            + [pl.BlockSpec((p.shape[0], tc), lambda i, core_ref: (0, i)) for p in parts],
            out_specs=pl.BlockSpec((N_SHARD, shard_rows, tc), lambda i, core_ref: (0, 0, i)),
            scratch_shapes=[pltpu.VMEM((N_SHARD, shard_rows, tc), F32)]),
        out_shape=jax.ShapeDtypeStruct((N_SHARD, shard_rows, hn), BF16),
        compiler_params=_cp("parallel"))(core, *parts, *got)


def _local_step(x, target, meta, norm_pre_w, w_re, conv_w, conv_b, dt_bias, a_log, d_skip, ssd_norm_w, sinks,
                w_out_shard, norm_post_w, place):
    seq = x.shape[0]
    t = PAD_LEAD + N_META + seq
    hpad = jnp.concatenate([jnp.zeros((PAD_LEAD, D_MODEL), F32), meta, x], axis=0)
    dt_bias_l, a_log_l, d_skip_l = _spread_heads(dt_bias), _spread_heads(a_log), _spread_heads(d_skip)
    cos_t, sin_t = _rope_tables(t)
    sink_v = sinks.reshape(Q_HEADS)

    proj, hn, w_out_all = _inproj(hpad, norm_pre_w, w_re, w_out_shard)
    w_out = w_out_all.reshape(D_MIX, D_MODEL)
    xbc = _conv_fwd(proj, conv_w, conv_b)
    dt, acs, acst = _dt_prep(proj, dt_bias_l, a_log_l)
    dt_x, acs_x, acs_rows, dsk_x = _expand_heads(dt), _expand_heads(acs), _chunk_rows(acst), _expand_heads(d_skip_l)
    y_ssd, ymix, states = _ssd_fwd(xbc, proj, dt_x, acs_x, acs_rows, dsk_x, ssd_norm_w)
    qr, kr = _rope(proj, OQ, proj, OK, cos_t, sin_t)
    amix = _attn_fwd(qr, kr, proj, sink_v)
    out = _outproj(ymix, amix, w_out)
    dout, dy_t, loss_blk, g_norm_post = _post_loss(out, x, target, norm_post_w)

    g_out_y = _tn_matmul(ymix, dout, "gw_out_y")
    g_out_a, got_y = _tn_matmul(amix, dout, "gw_out_a", carry=g_out_y)
    dmix, got_a = _nt_matmul(dout, w_out, "dmix", carry=g_out_a)
    ga_out = _reduce_pair([g_out_y, g_out_a], [got_y, got_a], [(j, 0, W_OUT_SHARD, W_OUT_SHARD * j) for j in range(N_SHARD)],
                          W_OUT_SHARD, place, "gw_out")
    dq_r, dg, dk_r, dv, gs, slabs_out = _attn_bwd(qr, kr, proj, dmix, sink_v, ga_out)
    g_w_out = _reduce_finish(ga_out, slabs_out, place, "gw_out")
    dq, dk = _rope(dq_r, 0, dk_r, 0, cos_t, -sin_t)
    dz, dxs, db, dc, dacs_g, ddt_g, g_ssd_norm, gdsk = _ssd_bwd(dmix, y_ssd, xbc, proj, dt_x, acs_x, acs_rows, states,
                                                                dsk_x, ssd_norm_w)
    draw, ga, gb = _dt_bwd(dacs_g, ddt_g, dt, proj, dt_bias_l, a_log_l)
    dxs_p, gcw0, gcb0 = _conv_bwd(dxs, proj, conv_w, conv_b, 0, "conv_bwd_x")
    db_p, gcw1, gcb1 = _conv_bwd(db, proj, conv_w, conv_b, D_SSD, "conv_bwd_b")
    dc_p, gcw2, gcb2 = _conv_bwd(dc, proj, conv_w, conv_b, D_SSD + GROUPS * D_STATE, "conv_bwd_c")
    tail = jnp.concatenate([dk, _bf(dv), draw, jnp.zeros((t, DT_SLAB - LANES), BF16)], axis=1)
    segs = [dz, dxs_p, db_p, dc_p, dq, dg, tail]
    g_parts, got_parts = [_tn_matmul(segs[0], hn, "gw_in_0")], []
    for s in range(1, len(segs)):
        part, got = _tn_matmul(segs[s], hn, "gw_in_%d" % s, carry=g_parts[-1])
        g_parts.append(part)
        got_parts.append(got)
    ga_in = _reduce_pair(g_parts, got_parts, _column_pieces(), W_IN_SHARD, place, "gw_in")
    dh, g_norm_pre, slabs_in = _dinproj(segs, w_re, hpad, norm_pre_w, dy_t, ga_in)
    g_w_in_half = _chip_sum(ga_in, slabs_in, place, "gw_in_chip_sum")

    gdsk_l = jnp.concatenate([gdsk[g, 0:1, 0:GROUPS] for g in range(GROUPS)], axis=1)
    gdsk_l = jnp.pad(gdsk_l, ((0, 0), (0, LANES - GROUPS * GROUPS)))
    grads = dict(
        meta_tokens=dh[PAD_LEAD:ROW0], norm_pre_w=g_norm_pre, w_in_half=g_w_in_half,
        conv_w=jnp.concatenate([gcw0, gcw1, gcw2], axis=1), conv_b=jnp.concatenate([gcb0, gcb1, gcb2], axis=1),
        dt_bias=_gather_heads(gb), a_log=_gather_heads(ga), d_skip=_gather_heads(gdsk_l), ssd_norm_w=g_ssd_norm,
        attn_sinks=gs[0:1, :Q_HEADS], w_out=g_w_out, norm_post_w=g_norm_post)
    return loss_blk[0, 0], dh[ROW0:], grads


ANY = pl.BlockSpec(memory_space=pl.ANY)
MESH = pl.DeviceIdType.MESH
GATHER_CHUNKS = 4
PAIR_CHUNKS = 8
JOIN_CHUNKS = 8


def _rcopy(src, dst, ssem, rsem, dev):
    return pltpu.make_async_remote_copy(src_ref=src, dst_ref=dst, send_sem=ssem, recv_sem=rsem, device_id=dev,
                                        device_id_type=MESH)


def _place():
    x, y, c = lax.axis_index("x"), lax.axis_index("y"), lax.axis_index("c")
    chips = [(1 - x, y), (x, 1 - y), (1 - x, 1 - y)]
    return x, y, c, chips


def _gather_plan(x_ref, out_ref, send_sems, recv_sems, local_sems, hr, kc):
    ch = hr // kc
    assert ch * kc == hr and ch % 16 == 0
    x, y, c, chips = _place()
    me = 2 * x + y
    sibling = (x, y, 1 - c)

    def piece(chip, hc, k):
        return out_ref.at[chip, pl.ds(hc * hr + k * ch, ch), :]

    def local():
        return [pltpu.make_async_copy(x_ref.at[pl.ds(k * ch, ch), :], out_ref.at[me, pl.ds(k * ch, ch), :],
                                      local_sems.at[k]) for k in range(2 * kc)]

    def first():
        return [_rcopy(x_ref.at[pl.ds(c * hr + k * ch, ch), :], piece(me, c, k), send_sems.at[j * kc + k],
                       recv_sems.at[j * kc + k], (*chip, c)) for j, chip in enumerate(chips) for k in range(kc)]

    def passed(hc):
        return [_rcopy(piece(2 * chip[0] + chip[1], hc, k), piece(2 * chip[0] + chip[1], hc, k),
                       send_sems.at[(3 + j) * kc + k], recv_sems.at[(3 + j) * kc + k], sibling)
                for j, chip in enumerate(chips) for k in range(kc)]

    def arrivals():
        return [_rcopy(piece(2 * chip[0] + chip[1], c, k), piece(2 * chip[0] + chip[1], c, k), send_sems.at[j * kc + k],
                       recv_sems.at[j * kc + k], (*chip, c)) for j, chip in enumerate(chips) for k in range(kc)]

    def start():
        for cp in local() + first():
            cp.start()

    def forward():
        for arrived in arrivals():
            arrived.wait_recv()
        for fw in passed(c):
            fw.start()

    def finish():
        for cp in passed(1 - c):
            cp.wait_recv()
        for cp in first() + passed(c):
            cp.wait_send()
        for cp in local():
            cp.wait()

    return start, forward, finish


def _gather_shards(shard, name, kc, chip, small):
    r, n = shard.shape
    hr = r // 2
    qr = hr // 2
    ch = qr // kc
    assert ch * kc == qr and ch % 16 == 0
    nflow = 12
    tr = 256

    def body(x_ref, p_ref, out_ref, slots_ref, send_sems, recv_sems, *small_sems):
        start_small, wait_small = _chip_small_exchange(p_ref, slots_ref, *small_sems)
        start_small()
        x, y, c, _ = _place()
        me, cxn, cyn, cdg = 2 * x + y, 2 * (1 - x) + y, 2 * x + 1 - y, 2 * (1 - x) + 1 - y
        xn, yn, sibling = (1 - x, y, c), (x, 1 - y, c), (x, y, 1 - c)

        def piece(chip, hc, part, k):
            return out_ref.at[chip, pl.ds(hc * hr + part * qr + k * ch, ch), :]

        def own(part, k):
            return x_ref.at[pl.ds(c * hr + part * qr + k * ch, ch), :]

        def sems(flow, k):
            return send_sems.at[flow * kc + k], recv_sems.at[flow * kc + k]

        def arrival(flow, chip, hc, part, k):
            return _rcopy(piece(chip, hc, part, k), piece(chip, hc, part, k), *sems(flow, k), sibling)

        sends = []
        for flow, part, peer in ((0, 0, xn), (1, 1, yn), (2, 0, yn), (3, 1, xn)):
            sends += [_rcopy(own(part, k), piece(me, c, part, k), *sems(flow, k), peer) for k in range(kc)]
        for cp in sends:
            cp.start()
        landing = ((0, cxn, 0), (1, cyn, 1), (2, cyn, 0), (3, cxn, 1), (4, cdg, 0), (5, cdg, 1))
        for i, (flow, chip, part) in enumerate(landing):
            for k in range(kc):
                arrival(flow, chip, c, part, k).wait_recv()
                if flow < 2:
                    on = _rcopy(piece(chip, c, part, k), piece(chip, c, part, k), *sems(4 + flow, k),
                                yn if flow == 0 else xn)
                    on.start()
                    sends.append(on)
                fw = _rcopy(piece(chip, c, part, k), piece(chip, c, part, k), *sems(6 + i, k), sibling)
                fw.start()
                sends.append(fw)
        for i, (flow, chip, part) in enumerate(landing):
            for k in range(kc):
                arrival(6 + i, chip, 1 - c, part, k).wait_recv()
        for cp in sends:
            cp.wait_send()
        wait_small()

    full = jax.ShapeDtypeStruct((N_SHARD, r, n), shard.dtype)
    others, slots = _pallas(
        body, name=name, in_specs=[ANY, ANY], out_specs=[ANY, ANY],
        out_shape=[full, jax.ShapeDtypeStruct((N_SHARD,) + small.shape, F32)],
        scratch_shapes=[pltpu.SemaphoreType.DMA((nflow * kc,)), pltpu.SemaphoreType.DMA((nflow * kc,)),
                        pltpu.SemaphoreType.DMA((3,)), pltpu.SemaphoreType.DMA((3,)), pltpu.SemaphoreType.DMA])(
                            shard, small)

    def place(chip_ref, own_ref, all_ref, o_ref):
        o_ref[0] = own_ref[...]

    gathered = _pallas(
        place, name=name + "_own",
        grid_spec=pltpu.PrefetchScalarGridSpec(
            num_scalar_prefetch=1, grid=(r // tr,),
            in_specs=[pl.BlockSpec((tr, n), lambda i, chip_ref: (i, 0)), ANY],
            out_specs=pl.BlockSpec((1, tr, n), lambda i, chip_ref: (chip_ref[0], i, 0))),
        out_shape=full, input_output_aliases={2: 0}, compiler_params=_cp("parallel"))(chip, shard, others)
    return gathered, slots


def _pair_copies(src_ref, dst_ref, send_sems, recv_sems):
    hn = src_ref.shape[1] // 2
    cw = hn // PAIR_CHUNKS
    assert cw * PAIR_CHUNKS == hn and cw % LANES == 0
    x, y, c, _ = _place()
    return [_rcopy(src_ref.at[:, pl.ds((1 - c) * hn + k * cw, cw)], dst_ref.at[:, pl.ds(k * cw, cw)],
                   send_sems.at[k], recv_sems.at[k], (x, y, 1 - c)) for k in range(PAIR_CHUNKS)]


def _pair_send(parts, name):
    n = parts[0].shape[1]
    hn = n // 2
    kc = PAIR_CHUNKS
    cw = hn // kc
    assert cw * kc == hn and cw % LANES == 0
    ns = len(parts)

    def body(*refs):
        srcs, dsts, send_sems, recv_sems = refs[:ns], refs[ns:2 * ns], refs[2 * ns], refs[2 * ns + 1]
        x, y, c, _ = _place()
        cps = [_rcopy(srcs[s].at[:, pl.ds((1 - c) * hn + k * cw, cw)], dsts[s].at[:, pl.ds(k * cw, cw)],
                      send_sems.at[s * kc + k], recv_sems.at[s * kc + k], (x, y, 1 - c))
               for s in range(ns) for k in range(kc)]
        for cp in cps:
            cp.start()
        for cp in cps:
            cp.wait()

    return _pallas(
        body, name=name, in_specs=[ANY] * ns, out_specs=[ANY] * ns,
        out_shape=[jax.ShapeDtypeStruct((p.shape[0], hn), F32) for p in parts],
        scratch_shapes=[pltpu.SemaphoreType.DMA((ns * kc,)), pltpu.SemaphoreType.DMA((ns * kc,))])(*parts)


REDUCE_TILE = 256


def _exchange_copies(g_ref, got_ref, send_sems, recv_sems):
    hn = g_ref.shape[2]
    kc = GATHER_CHUNKS
    cw = hn // kc
    assert cw * kc == hn and cw % LANES == 0
    x, y, c, chips = _place()
    return [_rcopy(g_ref.at[2 * chip[0] + chip[1], :, pl.ds(k * cw, cw)], got_ref.at[j, :, pl.ds(k * cw, cw)],
                   send_sems.at[j * kc + k], recv_sems.at[j * kc + k], (*chip, c))
            for j, chip in enumerate(chips) for k in range(kc)]


def _exchange_scratch():
    return [pltpu.SemaphoreType.DMA((3 * GATHER_CHUNKS,)), pltpu.SemaphoreType.DMA((3 * GATHER_CHUNKS,))]


def _exchange_shape(ga):
    return jax.ShapeDtypeStruct((3,) + ga.shape[1:], ga.dtype)


def _chip_sum(ga, got, place, name):
    _, r, hn = ga.shape
    tc = REDUCE_TILE
    nt = hn // tc

    def body(place_ref, own_ref, got_ref, o_ref):
        acc = own_ref[0].astype(F32)
        for j in range(3):
            acc = acc + got_ref[j].astype(F32)
        o_ref[...] = acc

    return _pallas(
        body, name=name,
        grid_spec=pltpu.PrefetchScalarGridSpec(
            num_scalar_prefetch=1, grid=(nt,),
            in_specs=[pl.BlockSpec((1, r, tc), lambda i, place_ref: (place_ref[0], 0, i)),
                      pl.BlockSpec((3, r, tc), lambda i, place_ref: (0, 0, i))],
            out_specs=pl.BlockSpec((r, tc), lambda i, place_ref: (0, place_ref[1] * nt + i))),
        out_shape=jax.ShapeDtypeStruct((r, 2 * hn), F32), compiler_params=_cp("parallel"))(place, ga, got)


def _pair_join(buf, name, small=None):
    r, n = buf.shape
    hn = n // 2
    kc = JOIN_CHUNKS
    cw = hn // kc
    assert cw * kc == hn and cw % LANES == 0

    def body(in_ref, *refs):
        if small is None:
            out_ref, send_sems, recv_sems = refs
        else:
            p_ref, out_ref, slots_ref, send_sems, recv_sems = refs[:5]
            start_small, wait_small = _small_exchange(p_ref, slots_ref, *refs[5:])
            start_small()
        x, y, c, _ = _place()
        cps = [_rcopy(out_ref.at[:, pl.ds(c * hn + k * cw, cw)], out_ref.at[:, pl.ds(c * hn + k * cw, cw)],
                      send_sems.at[k], recv_sems.at[k], (x, y, 1 - c)) for k in range(kc)]
        for cp in cps:
            cp.start()
        for k in range(kc):
            cols = out_ref.at[:, pl.ds((1 - c) * hn + k * cw, cw)]
            _rcopy(cols, cols, send_sems.at[k], recv_sems.at[k], (x, y, 1 - c)).wait_recv()
        for cp in cps:
            cp.wait_send()
        if small is not None:
            wait_small()

    sems = [pltpu.SemaphoreType.DMA((kc,)), pltpu.SemaphoreType.DMA((kc,))]
    if small is None:
        return _pallas(body, name=name, in_specs=[ANY], out_specs=ANY, out_shape=jax.ShapeDtypeStruct((r, n), F32),
                       input_output_aliases={0: 0}, scratch_shapes=sems)(buf)
    return _pallas(
        body, name=name, in_specs=[ANY, ANY], out_specs=[ANY, ANY],
        out_shape=[jax.ShapeDtypeStruct((r, n), F32), jax.ShapeDtypeStruct((N_DEV,) + small.shape, F32)],
        input_output_aliases={0: 0}, scratch_shapes=sems + _small_scratch())(buf, small)


def _reduce_pair(parts, got, pieces, shard_rows, place, tag):
    if len(got) < len(parts):
        got = list(got) + list(_pair_send(parts[len(got):], tag + "_pair_send"))
    return _pair_add_to_shards(parts, got, pieces, shard_rows, place[1:2], tag + "_pair_add")


def _reduce_finish(ga, slabs, place, tag):
    return _pair_join(_chip_sum(ga, slabs, place, tag + "_chip_sum"), tag + "_pair_join")


N_DEV = 8


def _small_exchange(p_ref, slots_ref, send_sems, recv_sems, local_sem):
    x, y, c, _ = _place()
    my = 4 * x + 2 * y + c

    def sends():
        return [_rcopy(p_ref, slots_ref.at[my], send_sems.at[k - 1], recv_sems.at[k - 1],
                       (x ^ ((k >> 2) & 1), y ^ ((k >> 1) & 1), c ^ (k & 1))) for k in range(1, N_DEV)]

    def local():
        return pltpu.make_async_copy(p_ref, slots_ref.at[my], local_sem)

    def start():
        local().start()
        for cp in sends():
            cp.start()

    def wait():
        for k in range(1, N_DEV):
            _rcopy(p_ref, slots_ref.at[my ^ k], send_sems.at[k - 1], recv_sems.at[k - 1], (x, y, c)).wait_recv()
        for cp in sends():
            cp.wait_send()
        local().wait()

    return start, wait


def _chip_small_exchange(p_ref, slots_ref, send_sems, recv_sems, local_sem):
    x, y, c, chips = _place()
    me = 2 * x + y

    def sends():
        return [_rcopy(p_ref, slots_ref.at[me], send_sems.at[j], recv_sems.at[j], (*chip, c))
                for j, chip in enumerate(chips)]

    def local():
        return pltpu.make_async_copy(p_ref, slots_ref.at[me], local_sem)

    def start():
        local().start()
        for cp in sends():
            cp.start()

    def wait():
        for j, chip in enumerate(chips):
            slot = slots_ref.at[2 * chip[0] + chip[1]]
            _rcopy(slot, slot, send_sems.at[j], recv_sems.at[j], (*chip, c)).wait_recv()
        for cp in sends():
            cp.wait_send()
        local().wait()

    return start, wait


def _small_scratch():
    return [pltpu.SemaphoreType.DMA((N_DEV - 1,)), pltpu.SemaphoreType.DMA((N_DEV - 1,)), pltpu.SemaphoreType.DMA]


def _sum_slots(slots, name):
    _, rows, n = slots.shape

    def body(s_ref, o_ref):
        acc = s_ref[0]
        for j in range(1, N_DEV):
            acc = acc + s_ref[j]
        o_ref[...] = acc

    vm = pl.BlockSpec(memory_space=pltpu.VMEM)
    return _pallas(body, name=name, in_specs=[vm], out_specs=vm, out_shape=jax.ShapeDtypeStruct((rows, n), F32))(slots)


def _adamw(w, g, m, v, name):
    r, n = w.shape
    tr = _tile(r, 256, 8)
    c1 = 1.0 / (1.0 - ADAM_B1 ** ADAM_STEP)
    c2 = 1.0 / (1.0 - ADAM_B2 ** ADAM_STEP)

    def body(w_ref, g_ref, m_ref, v_ref, d_ref, mo_ref, vo_ref, go_ref):
        gv = g_ref[...]
        mn = ADAM_B1 * m_ref[...] + (1.0 - ADAM_B1) * gv
        vn = ADAM_B2 * v_ref[...] + (1.0 - ADAM_B2) * (gv * gv)
        d_ref[...] = -ADAM_LR * ((mn * c1) / (jnp.sqrt(vn * c2) + ADAM_EPS) + ADAM_WD * w_ref[...])
        mo_ref[...] = mn
        vo_ref[...] = vn
        go_ref[...] = gv

    spec = pl.BlockSpec((tr, n), lambda i: (i, 0))
    shp = jax.ShapeDtypeStruct((r, n), F32)
    return _pallas(body, name=name, grid=(r // tr,), in_specs=[spec] * 4, out_specs=[spec] * 4, out_shape=[shp] * 4,
                   compiler_params=_cp("parallel"))(w, g, m, v)


PACK_W = 1024
SMALL_REPL = ("norm_pre_w", "conv_b", "ssd_norm_w", "norm_post_w")
SMALL_HEAD = ("dt_bias", "a_log", "d_skip", "attn_sinks")


def _rows(a):
    return a.reshape(-1, PACK_W)


def _head_row(vals, extra=None):
    parts = [vals[n].reshape(1, -1) for n in SMALL_HEAD]
    if extra is not None:
        parts.append(extra.reshape(1, 1))
    row = jnp.concatenate(parts, axis=1)
    return jnp.pad(row, ((0, 0), (0, PACK_W - row.shape[1])))


def _pad_rows(a, rows):
    return jnp.pad(a, ((0, rows - a.shape[0]), (0, 0)))


def _pack_repl(vals, extra=None):
    body = jnp.concatenate([_rows(vals[n]) for n in SMALL_REPL] + [_head_row(vals, extra)], axis=0)
    return _pad_rows(body, 16)


def _unpack_repl(buf):
    out, r = {}, 0
    for n, k in zip(SMALL_REPL, (2, 4, 2, 2)):
        out[n] = buf[r:r + k].reshape(1, k * PACK_W)
        r += k
    col = 0
    for n, k in zip(SMALL_HEAD, (32, 32, 32, 16)):
        out[n] = buf[r:r + 1, col:col + k]
        col += k
    return out, buf[r, col]


def kernel(x, meta_tokens, norm_pre_w, w_in, conv_w, conv_b, dt_bias, a_log, d_skip, ssd_norm_w, attn_sinks, w_out, norm_post_w, loss_target, m_meta_tokens, m_norm_pre_w, m_w_in, m_conv_w, m_conv_b, m_dt_bias, m_a_log, m_d_skip, m_ssd_norm_w, m_attn_sinks, m_w_out, m_norm_post_w, v_meta_tokens, v_norm_pre_w, v_w_in, v_conv_w, v_conv_b, v_dt_bias, v_a_log, v_d_skip, v_ssd_norm_w, v_attn_sinks, v_w_out, v_norm_post_w):
    names = ("meta_tokens", "norm_pre_w", "w_in", "conv_w", "conv_b", "dt_bias", "a_log", "d_skip", "ssd_norm_w",
             "attn_sinks", "w_out", "norm_post_w")
    w = dict(zip(names, (meta_tokens, norm_pre_w, w_in, conv_w, conv_b, dt_bias, a_log, d_skip, ssd_norm_w, attn_sinks,
                         w_out, norm_post_w)))
    m = dict(zip(names, (m_meta_tokens, m_norm_pre_w, m_w_in, m_conv_w, m_conv_b, m_dt_bias, m_a_log, m_d_skip,
                         m_ssd_norm_w, m_attn_sinks, m_w_out, m_norm_post_w)))
    v = dict(zip(names, (v_meta_tokens, v_norm_pre_w, v_w_in, v_conv_w, v_conv_b, v_dt_bias, v_a_log, v_d_skip,
                         v_ssd_norm_w, v_attn_sinks, v_w_out, v_norm_post_w)))
    cx, cy, cc = lax.axis_index("x"), lax.axis_index("y"), lax.axis_index("c")
    chip = 2 * cx + cy
    meta_cols = D_MODEL // N_SHARD
    conv_cols = D_CONV // N_SHARD

    place = jnp.stack([chip, cc]).astype(jnp.int32)
    small = jnp.concatenate([_pad_rows(conv_w[0], 8), _rows(meta_tokens)], axis=0)
    w_in_all, small_all = _gather_shards(_bf(w_in[0]), "gather_w_in", GATHER_CHUNKS, place[0:1], small)
    w_re = _shards_to_re(w_in_all)
    conv_full = jnp.transpose(small_all[:, 0:CONV_WIDTH], (1, 0, 2)).reshape(CONV_WIDTH, D_CONV)
    meta_full = jnp.transpose(small_all[:, 8:16].reshape(N_SHARD, N_META, meta_cols), (1, 0, 2)).reshape(N_META, D_MODEL)

    loss_dev, grad_x, g = _local_step(x[0], loss_target[0], meta_full, norm_pre_w, w_re, conv_full, conv_b, dt_bias,
                                      a_log, d_skip, ssd_norm_w, attn_sinks, _bf(w_out[0]), norm_post_w, place)
    g_w_out = g["w_out"]

    packed = jnp.concatenate([_rows(g["conv_w"]), _rows(g["meta_tokens"]), _pack_repl(g, loss_dev)], axis=0)
    g_w_in, slots = _pair_join(g["w_in_half"], "gw_in_pair_join", small=packed)
    red = _sum_slots(slots, "reduce_small")
    g_conv_full = red[0:16].reshape(CONV_WIDTH, D_CONV)
    g_meta_full = red[16:48].reshape(N_META, D_MODEL)
    g_small, loss = _unpack_repl(red[48:64])
    grads = dict(g_small)
    grads["w_in"] = g_w_in
    grads["w_out"] = g_w_out
    grads["conv_w"] = lax.dynamic_slice(g_conv_full, (0, chip * conv_cols), (CONV_WIDTH, conv_cols))
    grads["meta_tokens"] = lax.dynamic_slice(g_meta_full, (0, chip * meta_cols), (N_META, meta_cols))

    upd = {}
    upd["w_in"] = [jnp.swapaxes(a, 0, 1) for a in _adamw(jnp.swapaxes(w_in[0], 0, 1), g_w_in, jnp.swapaxes(m_w_in[0], 0, 1),
                                                         jnp.swapaxes(v_w_in[0], 0, 1), "adamw_w_in")]
    grads["w_in"] = upd["w_in"][3]
    upd["w_out"] = _adamw(w_out[0], g_w_out, m_w_out[0], v_w_out[0], "adamw_w_out")
    grads["w_out"] = upd["w_out"][3]

    def pack_small(vals, conv, meta):
        return jnp.concatenate([_pad_rows(conv.reshape(CONV_WIDTH, conv_cols), 8), _rows(meta), _pack_repl(vals)], axis=0)

    sm = _adamw(pack_small(w, w["conv_w"], w["meta_tokens"]), pack_small(grads, grads["conv_w"], grads["meta_tokens"]),
                pack_small(m, m["conv_w"], m["meta_tokens"]), pack_small(v, v["conv_w"], v["meta_tokens"]),
                "adamw_small")
    for n in names:
        if n not in ("w_in", "w_out"):
            upd[n] = [None, None, None]
    for k, buf in enumerate(sm[:3]):
        upd["conv_w"][k] = buf[0:CONV_WIDTH]
        upd["meta_tokens"][k] = buf[8:16].reshape(N_META, meta_cols)
        rest, _ = _unpack_repl(buf[16:32])
        for n in SMALL_REPL + SMALL_HEAD:
            upd[n][k] = rest[n]

    def shaped(n, a):
        return a.reshape(w[n].shape)

    outs = [loss, grad_x[None]]
    outs += [shaped(n, grads[n]) for n in names]
    for k in range(3):
        outs += [shaped(n, upd[n][k]) for n in names]
    return tuple(outs)
```

```python
import functools

import jax
import jax.numpy as jnp
from jax import lax
from jax.experimental import pallas as pl
from jax.experimental.pallas import tpu as pltpu

F32 = jnp.float32
BF16 = jnp.bfloat16

D_MODEL = 2048
CHUNK = 64
N_META = 16
PAD_LEAD = CHUNK - N_META
ROW0 = PAD_LEAD + N_META
EPS = 1e-6
SSD_HEADS = 32
HEAD_DIM = 64
GROUPS = 8
HPG = SSD_HEADS // GROUPS
D_STATE = 128
D_SSD = 2048
GROUP_W = D_SSD // GROUPS
CONV_WIDTH = 4
D_CONV = 4096
Q_HEADS = 16
KV_HEADS = 4
REP = Q_HEADS // KV_HEADS
D_ATT = 1024
D_KV = 256
BAND_CHUNKS = 3
ROPE_THETA = 10000.0
D_MIX = D_SSD + D_ATT
D_IN = 8736
N_SHARD = 4
W_IN_SHARD = D_IN // N_SHARD
W_OUT_SHARD = D_MIX // N_SHARD

OZ, OXS, OB, OC, OQ, OG, OK, OV, ODT = 0, 2048, 4096, 5120, 6144, 7168, 8192, 8448, 8704
DT_SLAB = 512
N_RE = ODT + DT_SLAB
LANES = 128

ADAM_LR, ADAM_B1, ADAM_B2, ADAM_EPS, ADAM_WD, ADAM_STEP = 0.001, 0.9, 0.999, 1e-08, 0.01, 10

SSD_FWD_GROUPS_PER_STEP = 4
SSD_BWD_GROUPS_PER_STEP = 8
SEG_TILE = 1024
VMEM_LIMIT = 52 * 1024 * 1024
NEG = -1e30
HI = lax.Precision.HIGHEST


def _pallas(body, **kw):
    return pl.pallas_call(body, **kw)


def _cp(*sem):
    return pltpu.CompilerParams(dimension_semantics=sem, vmem_limit_bytes=VMEM_LIMIT)


def _tile(n, cap, mult=16):
    best = None
    for d in range(mult, min(n, cap) + 1, mult):
        if n % d == 0:
            best = d
    assert best is not None, (n, cap)
    return best


def _nt(a, b):
    return lax.dot_general(a, b, (((1,), (1,)), ((), ())), preferred_element_type=F32)


def _tn(a, b):
    return lax.dot_general(a, b, (((0,), (0,)), ((), ())), preferred_element_type=F32)


def _mm(a, b):
    return jnp.dot(a, b, preferred_element_type=F32)


def _sigmoid(x):
    return 1.0 / (1.0 + jnp.exp(-x))


def _bf(x):
    return x.astype(BF16)


def _inproj(hpad, norm_w, w_re, w_out_shard):
    t, d = hpad.shape
    n = w_re.shape[1]
    tm, tn = _tile(t, 1040), 1024
    ni, nj = t // tm, n // tn
    r_out, n_out = w_out_shard.shape
    kc = GATHER_CHUNKS

    def body(h_ref, nw_ref, w_ref, ws_ref, proj_ref, hn_ref, wall_ref, hn_s, send_sems, recv_sems, local_sems):
        i, j = pl.program_id(0), pl.program_id(1)
        start, forward, finish = _gather_plan(ws_ref, wall_ref, send_sems, recv_sems, local_sems, r_out // 2, kc)
        pl.when((i == 0) & (j == 0))(start)
        pl.when((i == ni // 2) & (j == 0))(forward)

        @pl.when(j == 0)
        def _():
            h = h_ref[...]
            ms = jnp.mean(h * h, axis=-1, keepdims=True)
            hn = _bf(h * lax.rsqrt(ms + EPS) * nw_ref[...])
            hn_s[...] = hn
            hn_ref[...] = hn
        proj_ref[...] = _mm(hn_s[...], w_ref[...])
        pl.when((i == ni - 1) & (j == nj - 1))(finish)

    return _pallas(
        body, name="inproj", grid=(ni, nj),
        in_specs=[pl.BlockSpec((tm, d), lambda i, j: (i, 0)), pl.BlockSpec((1, d), lambda i, j: (0, 0)),
                  pl.BlockSpec((d, tn), lambda i, j: (0, j)), ANY],
        out_specs=[pl.BlockSpec((tm, tn), lambda i, j: (i, j)), pl.BlockSpec((tm, d), lambda i, j: (i, 0)), ANY],
        out_shape=[jax.ShapeDtypeStruct((t, n), F32), jax.ShapeDtypeStruct((t, d), BF16),
                   jax.ShapeDtypeStruct((N_SHARD, r_out, n_out), w_out_shard.dtype)],
        scratch_shapes=[pltpu.VMEM((tm, d), BF16), pltpu.SemaphoreType.DMA((6 * kc,)), pltpu.SemaphoreType.DMA((6 * kc,)),
                        pltpu.SemaphoreType.DMA((2 * kc,))],
        compiler_params=_cp("arbitrary", "arbitrary"))(hpad, norm_w, w_re, w_out_shard)


def _conv_fwd(proj, conv_w, conv_b):
    t = proj.shape[0]
    tc = 256
    off = OXS // tc

    def body(x_ref, w_ref, b_ref, o_ref):
        x = x_ref[...]
        w = w_ref[...]
        row = lax.broadcasted_iota(jnp.int32, (t, tc), 0)
        u = b_ref[...] + w[3:4, :] * x
        for k in range(1, CONV_WIDTH):
            u = u + w[3 - k:4 - k, :] * jnp.where(row >= k, pltpu.roll(x, k, 0), 0.0)
        h = 0.5 * u
        o_ref[...] = h + h * jnp.tanh(h)

    return _pallas(
        body, name="conv_fwd", grid=(D_CONV // tc,),
        in_specs=[pl.BlockSpec((t, tc), lambda j: (0, j + off)), pl.BlockSpec((CONV_WIDTH, tc), lambda j: (0, j)),
                  pl.BlockSpec((1, tc), lambda j: (0, j))],
        out_specs=pl.BlockSpec((t, tc), lambda j: (0, j)),
        out_shape=jax.ShapeDtypeStruct((t, D_CONV), F32),
        compiler_params=_cp("parallel"))(proj, conv_w, conv_b)


def _softplus(u):
    e = jnp.exp(-jnp.abs(u))
    w = 1.0 + e
    l1p = jnp.where(w == 1.0, e, jnp.log(w) * (e / jnp.where(w == 1.0, 1.0, w - 1.0)))
    return jnp.maximum(u, 0.0) + l1p


def _chunks_per_step(nc):
    return max(d for d in range(1, 14) if nc % d == 0)


def _dt_prep(proj, dt_bias_l, a_log_l):
    t = proj.shape[0]
    nc = t // CHUNK
    q = CHUNK
    cps = _chunks_per_step(nc)
    rows = cps * q

    def body(raw_ref, bias_ref, alog_ref, dt_ref, acs_ref, acst_ref):
        ri = lax.broadcasted_iota(jnp.int32, (q, q), 0)
        ci = lax.broadcasted_iota(jnp.int32, (q, q), 1)
        tri = (ri >= ci).astype(F32)
        neg_a = -jnp.exp(alog_ref[...])
        for k in range(cps):
            rk = slice(q * k, q * (k + 1))
            sp = _softplus(raw_ref[rk, :] + bias_ref[...])
            row = pl.program_id(0) * rows + q * k + lax.broadcasted_iota(jnp.int32, (q, LANES), 0)
            dt = jnp.where(row >= PAD_LEAD, sp, 0.0)
            acs = jnp.dot(tri, dt * neg_a, preferred_element_type=F32, precision=HI)
            dt_ref[rk, :] = dt
            acs_ref[rk, :] = acs
            acst_ref[k] = acs.T

    return _pallas(
        body, name="dt_prep", grid=(nc // cps,),
        in_specs=[pl.BlockSpec((rows, LANES), lambda c: (c, ODT // LANES)), pl.BlockSpec((1, LANES), lambda c: (0, 0)),
                  pl.BlockSpec((1, LANES), lambda c: (0, 0))],
        out_specs=[pl.BlockSpec((rows, LANES), lambda c: (c, 0)), pl.BlockSpec((rows, LANES), lambda c: (c, 0)),
                   pl.BlockSpec((cps, LANES, q), lambda c: (c, 0, 0))],
        out_shape=[jax.ShapeDtypeStruct((t, LANES), F32), jax.ShapeDtypeStruct((t, LANES), F32),
                   jax.ShapeDtypeStruct((nc, LANES, q), F32)],
        compiler_params=_cp("parallel"))(proj, dt_bias_l, a_log_l)


def _head_cols(blk, idx):
    lane = lax.broadcasted_iota(jnp.int32, blk.shape, 1)
    return jnp.sum(jnp.where(lane == idx, blk, 0.0), axis=1, keepdims=True)


class _HeadVals:
    pass


def _lane_head(shape):
    return lax.broadcasted_iota(jnp.int32, shape, len(shape) - 1) >> 6


def _group_heads(g, gi, dtb, acsb, acst_ref, dskb):
    q = dtb.shape[0]
    hv = _HeadVals()
    lh = _lane_head((1, GROUP_W))
    hv.dt = jnp.zeros((q, GROUP_W), F32)
    hv.acs = jnp.zeros((q, GROUP_W), F32)
    hv.acs_last = jnp.zeros((1, GROUP_W), F32)
    hv.dsk = jnp.zeros((1, GROUP_W), F32)
    rows = []
    for r in range(HPG):
        idx = GROUPS * g + r
        sel = lh == r
        acs_r = acst_ref[0, GROUPS * gi + r:GROUPS * gi + r + 1, :]
        rows.append(acs_r)
        hv.dt = jnp.where(sel, _head_cols(dtb, idx), hv.dt)
        hv.acs = jnp.where(sel, _head_cols(acsb, idx), hv.acs)
        hv.acs_last = jnp.where(sel, acs_r[:, q - 1:q], hv.acs_last)
        hv.dsk = jnp.where(sel, _head_cols(dskb, idx), hv.dsk)
    hv.acs_row = jnp.concatenate(rows, axis=1)
    return hv


def _head_tri(q, lower):
    ri = lax.broadcasted_iota(jnp.int32, (q, GROUP_W), 0)
    li = lax.broadcasted_iota(jnp.int32, (q, GROUP_W), 1) & (HEAD_DIM - 1)
    return ri >= li if lower else ri <= li


def _block_diag_mask():
    rb = lax.broadcasted_iota(jnp.int32, (GROUP_W, GROUP_W), 0) >> 6
    cb = lax.broadcasted_iota(jnp.int32, (GROUP_W, GROUP_W), 1) >> 6
    return rb == cb


def _block_diag(v, mask):
    return jnp.where(mask, jnp.concatenate([v] * HPG, axis=0), jnp.zeros((), v.dtype))


def _head_sums(v, r):
    return jnp.sum(jnp.where(_lane_head((1, GROUP_W)) == r, v, 0.0), axis=1, keepdims=True)


def _ssd_fwd(xbc, proj, dt, acs, acst, d_skip_l, ssd_norm_w):
    t = xbc.shape[0]
    q = CHUNK
    nc = t // q

    gps = SSD_FWD_GROUPS_PER_STEP
    gw, sw = gps * GROUP_W, gps * D_STATE

    def body(xs_ref, b_ref, c_ref, dt_ref, acs_ref, acst_ref, z_ref, dsk_ref, nw_ref,
             y_ref, ymix_ref, st_ref, state):
        @pl.when(pl.program_id(1) == 0)
        def _():
            state[...] = jnp.zeros_like(state)

        lower = _head_tri(q, True)
        bd_mask = _block_diag_mask()
        for gi in range(gps):
            g = gps * pl.program_id(0) + gi
            cols = slice(GROUP_W * gi, GROUP_W * (gi + 1))
            x = xs_ref[:, cols]
            bmb = _bf(b_ref[:, D_STATE * gi:D_STATE * (gi + 1)])
            cmb = _bf(c_ref[:, D_STATE * gi:D_STATE * (gi + 1)])
            hv = _group_heads(g, gi, dt_ref[...], acs_ref[...], acst_ref, dsk_ref[...])
            decay = jnp.exp(jnp.where(lower, hv.acs - hv.acs_row, NEG))
            m_all = _bf(_nt(cmb, jnp.concatenate([bmb] * HPG, axis=0)) * decay)
            xdt = x * hv.dt
            s_prev = state[gi]
            st_ref[0, gi] = s_prev
            y = (_mm(m_all, _block_diag(_bf(xdt), bd_mask)) + _mm(cmb, _bf(s_prev)) * jnp.exp(hv.acs) + hv.dsk * x)
            state[gi] = jnp.exp(hv.acs_last) * s_prev + _tn(bmb, _bf(xdt * jnp.exp(hv.acs_last - hv.acs)))
            y_ref[:, cols] = y
            z = z_ref[:, cols]
            yg = y * (z * _sigmoid(z))
            ms = jnp.mean(yg * yg, axis=-1, keepdims=True)
            ymix_ref[:, cols] = _bf(yg * lax.rsqrt(ms + EPS) * nw_ref[:, cols])

    return _pallas(
        body, name="ssd_fwd", grid=(GROUPS // gps, nc),
        in_specs=[pl.BlockSpec((q, gw), lambda g, c: (c, g)),
                  pl.BlockSpec((q, sw), lambda g, c: (c, D_SSD // sw + g)),
                  pl.BlockSpec((q, sw), lambda g, c: (c, (D_SSD + GROUPS * D_STATE) // sw + g)),
                  pl.BlockSpec((q, LANES), lambda g, c: (c, 0)), pl.BlockSpec((q, LANES), lambda g, c: (c, 0)),
                  pl.BlockSpec((1, gps * GROUPS, q), lambda g, c: (c, g, 0)),
                  pl.BlockSpec((q, gw), lambda g, c: (c, g)),
                  pl.BlockSpec((1, LANES), lambda g, c: (0, 0)), pl.BlockSpec((1, gw), lambda g, c: (0, g))],
        out_specs=[pl.BlockSpec((q, gw), lambda g, c: (c, g)), pl.BlockSpec((q, gw), lambda g, c: (c, g)),
                   pl.BlockSpec((1, gps, D_STATE, GROUP_W), lambda g, c: (c, g, 0, 0))],
        out_shape=[jax.ShapeDtypeStruct((t, D_SSD), F32), jax.ShapeDtypeStruct((t, D_SSD), BF16),
                   jax.ShapeDtypeStruct((nc, GROUPS, D_STATE, GROUP_W), F32)],
        scratch_shapes=[pltpu.VMEM((gps, D_STATE, GROUP_W), F32)],
        compiler_params=_cp("parallel", "arbitrary"))(xbc, xbc, xbc, dt, acs, acst, proj, d_skip_l, ssd_norm_w)


def _swap_halves(v):
    lane = lax.broadcasted_iota(jnp.int32, v.shape, 1)
    return jnp.where((lane & (HEAD_DIM - 1)) < HEAD_DIM // 2, pltpu.roll(v, LANES - HEAD_DIM // 2, 1),
                     pltpu.roll(v, HEAD_DIM // 2, 1))


def _rope(qsrc, q_off, ksrc, k_off, cos_t, sin_t):
    t = qsrc.shape[0]
    tr = _tile(t, 832)
    q_scale = HEAD_DIM ** -0.5

    def body(q_ref, k_ref, cos_ref, sin_ref, qo_ref, ko_ref):
        cs = cos_ref[...]
        sn = sin_ref[...]
        for src, dst, width, scale in ((q_ref, qo_ref, D_ATT, q_scale), (k_ref, ko_ref, D_KV, 1.0)):
            for s in range(width // LANES):
                v = src[:, LANES * s:LANES * (s + 1)].astype(F32)
                dst[:, LANES * s:LANES * (s + 1)] = _bf((v * cs + _swap_halves(v) * sn) * scale)

    return _pallas(
        body, name="rope", grid=(t // tr,),
        in_specs=[pl.BlockSpec((tr, D_ATT), lambda i: (i, q_off // D_ATT)),
                  pl.BlockSpec((tr, D_KV), lambda i: (i, k_off // D_KV)),
                  pl.BlockSpec((tr, LANES), lambda i: (i, 0)), pl.BlockSpec((tr, LANES), lambda i: (i, 0))],
        out_specs=[pl.BlockSpec((tr, D_ATT), lambda i: (i, 0)), pl.BlockSpec((tr, D_KV), lambda i: (i, 0))],
        out_shape=[jax.ShapeDtypeStruct((t, D_ATT), BF16), jax.ShapeDtypeStruct((t, D_KV), BF16)],
        compiler_params=_cp("parallel"))(qsrc, ksrc, cos_t, sin_t)


def _attn_chunks_per_step(nc):
    return max(d for d in range(1, 6) if nc % d == 0)


def _band(ref, c):
    return [ref[pl.ds(pl.multiple_of(jnp.maximum(c - j, 0) * CHUNK, CHUNK), CHUNK), :] for j in (2, 1, 0)]


def _attn_probs(qh, kb, sink_col, valid):
    s = _nt(qh, kb)
    if valid is not None:
        s = jnp.where(valid, s, NEG)
    m = jnp.maximum(jnp.max(s, axis=1, keepdims=True), sink_col)
    p = jnp.exp(s - m)
    psink = jnp.exp(sink_col - m)
    return p, psink, 1.0 / (jnp.sum(p, axis=1, keepdims=True) + psink)


def _attn_operands(c, q, k_refs, v_refs, sink_ref, h):
    qh = jnp.concatenate([q[:, HEAD_DIM * (REP * h + r):HEAD_DIM * (REP * h + r + 1)] for r in range(REP)], axis=0)
    kb = jnp.concatenate([k[:, HEAD_DIM * h:HEAD_DIM * (h + 1)] for k in k_refs], axis=0)
    vb = jnp.concatenate([_bf(v[:, HEAD_DIM * h:HEAD_DIM * (h + 1)]) for v in v_refs], axis=0)
    rows = lax.broadcasted_iota(jnp.int32, (REP * CHUNK, 1), 0) >> 6
    sink_col = jnp.zeros((REP * CHUNK, 1), F32)
    for r in range(REP):
        sink_col = jnp.where(rows == r, sink_ref[REP * h + r], sink_col)
    key_abs = (c - (BAND_CHUNKS - 1)) * CHUNK + lax.broadcasted_iota(jnp.int32, (1, BAND_CHUNKS * CHUNK), 1)
    return qh, kb, vb, sink_col, key_abs >= PAD_LEAD


def _attn_fwd(qr, kr, proj, sinks):
    t = qr.shape[0]
    nc = t // CHUNK
    cps = _attn_chunks_per_step(nc)
    rows = cps * CHUNK

    def body(q_ref, k_ref, v_ref, g_ref, sink_ref, o_ref):
        def chunks(masked):
            for j in range(cps):
                c = pl.program_id(0) * cps + j
                rj = slice(CHUNK * j, CHUNK * (j + 1))
                ks, vs = _band(k_ref, c), _band(v_ref, c)
                q = q_ref[rj, :]
                outs = []
                for h in range(KV_HEADS):
                    qh, kb, vb, sink_col, valid = _attn_operands(c, q, ks, vs, sink_ref, h)
                    p, _, inv = _attn_probs(qh, kb, sink_col, valid if masked else None)
                    o = _mm(_bf(p), vb) * inv
                    outs += [o[CHUNK * r:CHUNK * (r + 1)] for r in range(REP)]
                att = jnp.concatenate(outs, axis=1)
                gate = g_ref[rj, :]
                o_ref[rj, :] = _bf(att * (gate * _sigmoid(gate)))

        early = pl.program_id(0) * cps < BAND_CHUNKS
        pl.when(early)(functools.partial(chunks, True))
        pl.when(jnp.logical_not(early))(functools.partial(chunks, False))

    return _pallas(
        body, name="attn_fwd", grid=(nc // cps,),
        in_specs=[pl.BlockSpec((rows, D_ATT), lambda i: (i, 0)), pl.BlockSpec((t, D_KV), lambda i: (0, 0)),
                  pl.BlockSpec((t, D_KV), lambda i: (0, OV // D_KV)),
                  pl.BlockSpec((rows, D_ATT), lambda i: (i, OG // D_ATT)), pl.BlockSpec(memory_space=pltpu.SMEM)],
        out_specs=pl.BlockSpec((rows, D_ATT), lambda i: (i, 0)),
        out_shape=jax.ShapeDtypeStruct((t, D_ATT), BF16),
        compiler_params=_cp("parallel"))(qr, kr, proj, proj, sinks)


def _outproj(ymix, amix, w_out):
    t = ymix.shape[0]
    tm, tn = _tile(t, 1040), 1024

    def body(y_ref, a_ref, wy_ref, wa_ref, o_ref):
        o_ref[...] = _mm(y_ref[...], wy_ref[...]) + _mm(a_ref[...], wa_ref[...])

    return _pallas(
        body, name="outproj", grid=(t // tm, D_MODEL // tn),
        in_specs=[pl.BlockSpec((tm, D_SSD), lambda i, j: (i, 0)), pl.BlockSpec((tm, D_ATT), lambda i, j: (i, 0)),
                  pl.BlockSpec((D_SSD, tn), lambda i, j: (0, j)),
                  pl.BlockSpec((D_ATT, tn), lambda i, j: (D_SSD // D_ATT, j))],
        out_specs=pl.BlockSpec((tm, tn), lambda i, j: (i, j)),
        out_shape=jax.ShapeDtypeStruct((t, D_MODEL), F32),
        compiler_params=_cp("parallel", "parallel"))(ymix, amix, w_out, w_out)


def _post_loss(out, x, target, norm_post_w):
    t = out.shape[0]
    nc = t // CHUNK
    cps = _attn_chunks_per_step(nc)
    rows = cps * CHUNK

    def body(o_ref, *refs):
        x_refs, tg_refs = refs[:cps], refs[cps:2 * cps]
        nw_ref, dout_ref, dy_ref, loss_ref, gnw_ref = refs[2 * cps:]
        i = pl.program_id(0)

        @pl.when(i == 0)
        def _():
            loss_ref[...] = jnp.zeros_like(loss_ref)
            gnw_ref[...] = jnp.zeros_like(gnw_ref)

        nw = nw_ref[...]
        loss = jnp.zeros((), F32)
        gnw = jnp.zeros((1, D_MODEL), F32)
        for k in range(cps):
            rk = slice(CHUNK * k, CHUNK * (k + 1))
            frames = i * cps + k > 0
            o = o_ref[rk, :]
            rstd = lax.rsqrt(jnp.mean(o * o, axis=-1, keepdims=True) + EPS)
            n = o * rstd
            err = jnp.where(frames, x_refs[k][...] + n * nw - tg_refs[k][...], 0.0)
            loss = loss + jnp.sum(err * err)
            dy = err * (1.0 / D_MODEL)
            dy_ref[rk, :] = dy
            gnw = gnw + jnp.sum(dy * n, axis=0, keepdims=True)
            dn = dy * nw
            dout_ref[rk, :] = _bf(rstd * (dn - n * jnp.mean(dn * n, axis=-1, keepdims=True)))
        loss_ref[...] += loss * (0.5 / D_MODEL)
        gnw_ref[...] += gnw

    lower = [pl.BlockSpec((CHUNK, D_MODEL), functools.partial(lambda i, k: (jnp.maximum(i * cps + k - 1, 0), 0), k=k))
             for k in range(cps)]
    return _pallas(
        body, name="post_loss", grid=(nc // cps,),
        in_specs=[pl.BlockSpec((rows, D_MODEL), lambda i: (i, 0))] + lower + lower
        + [pl.BlockSpec((1, D_MODEL), lambda i: (0, 0))],
        out_specs=[pl.BlockSpec((rows, D_MODEL), lambda i: (i, 0)), pl.BlockSpec((rows, D_MODEL), lambda i: (i, 0)),
                   pl.BlockSpec((8, LANES), lambda i: (0, 0)), pl.BlockSpec((1, D_MODEL), lambda i: (0, 0))],
        out_shape=[jax.ShapeDtypeStruct((t, D_MODEL), BF16), jax.ShapeDtypeStruct((t, D_MODEL), F32),
                   jax.ShapeDtypeStruct((8, LANES), F32), jax.ShapeDtypeStruct((1, D_MODEL), F32)],
        compiler_params=_cp("arbitrary"))(out, *([x] * cps), *([target] * cps), norm_post_w)


def _carried(grid, carry):
    if carry is None:
        return [], [], [], [], lambda refs: None, lambda refs: None
    hn = carry.shape[1] // 2

    def at(ids, which):
        cond = None
        for d, size in enumerate(grid):
            here = pl.program_id(d) == (0 if which == "first" else size - 1)
            cond = here if cond is None else cond & here
        return cond

    def start(refs):
        @pl.when(at(grid, "first"))
        def _():
            for cp in _pair_copies(*refs):
                cp.start()

    def finish(refs):
        @pl.when(at(grid, "last"))
        def _():
            for cp in _pair_copies(*refs):
                cp.wait()

    return ([ANY], [ANY], [jax.ShapeDtypeStruct((carry.shape[0], hn), F32)],
            [pltpu.SemaphoreType.DMA((PAIR_CHUNKS,)), pltpu.SemaphoreType.DMA((PAIR_CHUNKS,))], start, finish)


def _nt_matmul(a, b, name, carry=None):
    t, k = a.shape
    n = b.shape[0]
    tm, tn = _tile(t, 1040), 1024
    grid = (t // tm, n // tn)
    cin, cout, cshape, cscratch, start, finish = _carried(grid, carry)

    def body(a_ref, b_ref, *refs):
        o_ref = refs[len(cin)]
        comm = (refs[0], refs[2], refs[3], refs[4]) if carry is not None else None
        start(comm)
        o_ref[...] = _nt(a_ref[...], b_ref[...])
        finish(comm)

    res = _pallas(
        body, name=name, grid=grid,
        in_specs=[pl.BlockSpec((tm, k), lambda i, j: (i, 0)), pl.BlockSpec((tn, k), lambda i, j: (j, 0))] + cin,
        out_specs=[pl.BlockSpec((tm, tn), lambda i, j: (i, j))] + cout,
        out_shape=[jax.ShapeDtypeStruct((t, n), F32)] + cshape, scratch_shapes=cscratch,
        compiler_params=_cp("arbitrary", "arbitrary"))(a, b, *([carry] if carry is not None else []))
    return res if carry is not None else res[0]


def _tn_matmul(a, b, name, carry=None):
    t, m = a.shape
    n = b.shape[1]
    tk, tm, tn = _tile(t, 832), min(m, 2048), min(n, 2048)
    nk = t // tk
    grid = (m // tm, n // tn, nk)
    cin, cout, cshape, cscratch, start, finish = _carried(grid, carry)

    def body(a_ref, b_ref, *refs):
        o_ref = refs[len(cin)]
        comm = (refs[0], refs[2], refs[3], refs[4]) if carry is not None else None
        start(comm)

        @pl.when(pl.program_id(2) == 0)
        def _():
            o_ref[...] = jnp.zeros_like(o_ref)
        o_ref[...] += _tn(a_ref[...], b_ref[...])
        finish(comm)

    res = _pallas(
        body, name=name, grid=grid,
        in_specs=[pl.BlockSpec((tk, tm), lambda i, j, k: (k, i)), pl.BlockSpec((tk, tn), lambda i, j, k: (k, j))] + cin,
        out_specs=[pl.BlockSpec((tm, tn), lambda i, j, k: (i, j))] + cout,
        out_shape=[jax.ShapeDtypeStruct((m, n), F32)] + cshape, scratch_shapes=cscratch,
        compiler_params=_cp("arbitrary", "arbitrary", "arbitrary"))(a, b, *([carry] if carry is not None else []))
    return res if carry is not None else res[0]


def _attn_bwd(qr, kr, proj, dmix, sinks, ga):
    t = qr.shape[0]
    nc = t // CHUNK
    cps = _attn_chunks_per_step(nc)
    nsteps = nc // cps
    rows_step = cps * CHUNK

    def body(q_ref, k_ref, v_ref, g_ref, da_ref, sink_ref, ga_ref, dq_ref, dg_ref, dk_ref, dv_ref, gs_ref,
             got_ref, send_sems, recv_sems):
        step = pl.program_id(0)

        @pl.when(step == 0)
        def _():
            for cp in _exchange_copies(ga_ref, got_ref, send_sems, recv_sems):
                cp.start()
            dk_ref[...] = jnp.zeros_like(dk_ref)
            dv_ref[...] = jnp.zeros_like(dv_ref)
            gs_ref[...] = jnp.zeros_like(gs_ref)

        def chunks(masked):
            lane = lax.broadcasted_iota(jnp.int32, (1, LANES), 1)
            rows = lax.broadcasted_iota(jnp.int32, (REP * CHUNK, 1), 0) >> 6
            gs = jnp.zeros((1, LANES), F32)
            dk_parts = [[] for _ in range(cps + BAND_CHUNKS - 1)]
            dv_parts = [[] for _ in range(cps + BAND_CHUNKS - 1)]
            for j in range(cps):
                c = step * cps + j
                rj = slice(CHUNK * j, CHUNK * (j + 1))
                ks, vs = _band(k_ref, c), _band(v_ref, c)
                q = q_ref[rj, :]
                gate = g_ref[rj, :]
                sg = _sigmoid(gate)
                da = da_ref[rj, :]
                datt = da * (gate * sg)
                dqs, atts, dks, dvs = [], [], [], []
                for h in range(KV_HEADS):
                    qh, kb, vb, sink_col, valid = _attn_operands(c, q, ks, vs, sink_ref, h)
                    p, psink, inv = _attn_probs(qh, kb, sink_col, valid if masked else None)
                    pb = _bf(p)
                    o = _mm(pb, vb) * inv
                    do = jnp.concatenate([datt[:, HEAD_DIM * (REP * h + r):HEAD_DIM * (REP * h + r + 1)]
                                          for r in range(REP)], axis=0)
                    dob = _bf(do * inv)
                    delta = jnp.sum(do * o, axis=1, keepdims=True) * inv
                    ds = _bf(p * (_nt(dob, vb) - delta))
                    gsink = -psink * delta
                    for r in range(REP):
                        gs = gs + jnp.where(lane == REP * h + r, jnp.sum(jnp.where(rows == r, gsink, 0.0)), 0.0)
                    dqh = _mm(ds, kb)
                    dqs += [dqh[CHUNK * r:CHUNK * (r + 1)] for r in range(REP)]
                    atts += [o[CHUNK * r:CHUNK * (r + 1)] for r in range(REP)]
                    dks.append(_tn(ds, qh))
                    dvs.append(_tn(pb, dob))
                dq_ref[rj, :] = jnp.concatenate(dqs, axis=1)
                att = jnp.concatenate(atts, axis=1)
                dg_ref[rj, :] = _bf(da * att * (sg * (1.0 + gate * (1.0 - sg))))
                dkf = jnp.concatenate(dks, axis=1)
                dvf = jnp.concatenate(dvs, axis=1)
                for b in range(BAND_CHUNKS):
                    dk_parts[j + b].append(dkf[CHUNK * b:CHUNK * (b + 1)])
                    dv_parts[j + b].append(dvf[CHUNK * b:CHUNK * (b + 1)])
            gs_ref[0:1, :] += gs
            for rel in range(cps + BAND_CHUNKS - 1):
                r0 = pl.multiple_of(jnp.maximum(step * cps - (BAND_CHUNKS - 1) + rel, 0) * CHUNK, CHUNK)
                dk_ref[pl.ds(r0, CHUNK), :] += sum(dk_parts[rel][1:], dk_parts[rel][0])
                dv_ref[pl.ds(r0, CHUNK), :] += sum(dv_parts[rel][1:], dv_parts[rel][0])

        early = step * cps < BAND_CHUNKS
        pl.when(early)(functools.partial(chunks, True))
        pl.when(jnp.logical_not(early))(functools.partial(chunks, False))

        @pl.when(step == nsteps - 1)
        def _():
            for cp in _exchange_copies(ga_ref, got_ref, send_sems, recv_sems):
                cp.wait()

    return _pallas(
        body, name="attn_bwd", grid=(nsteps,),
        in_specs=[pl.BlockSpec((rows_step, D_ATT), lambda i: (i, 0)), pl.BlockSpec((t, D_KV), lambda i: (0, 0)),
                  pl.BlockSpec((t, D_KV), lambda i: (0, OV // D_KV)),
                  pl.BlockSpec((rows_step, D_ATT), lambda i: (i, OG // D_ATT)),
                  pl.BlockSpec((rows_step, D_ATT), lambda i: (i, D_SSD // D_ATT)),
                  pl.BlockSpec(memory_space=pltpu.SMEM), ANY],
        out_specs=[pl.BlockSpec((rows_step, D_ATT), lambda i: (i, 0)), pl.BlockSpec((rows_step, D_ATT), lambda i: (i, 0)),
                   pl.BlockSpec((t, D_KV), lambda i: (0, 0)), pl.BlockSpec((t, D_KV), lambda i: (0, 0)),
                   pl.BlockSpec((8, LANES), lambda i: (0, 0)), ANY],
        out_shape=[jax.ShapeDtypeStruct((t, D_ATT), F32), jax.ShapeDtypeStruct((t, D_ATT), BF16),
                   jax.ShapeDtypeStruct((t, D_KV), F32), jax.ShapeDtypeStruct((t, D_KV), F32),
                   jax.ShapeDtypeStruct((8, LANES), F32), _exchange_shape(ga)],
        scratch_shapes=_exchange_scratch(),
        compiler_params=_cp("arbitrary"))(qr, kr, proj, proj, dmix, sinks, ga)


def _ssd_bwd(dmix, y_ssd, xbc, proj, dt, acs, acst, states, d_skip_l, ssd_norm_w):
    t = xbc.shape[0]
    q = CHUNK
    nc = t // q
    gps = SSD_BWD_GROUPS_PER_STEP
    gw, sw = gps * GROUP_W, gps * D_STATE

    def body(dmix_ref, y_ref, z_ref, nw_ref, xs_ref, b_ref, c_ref, dt_ref, acs_ref, acst_ref, st_ref, dsk_ref,
             dz_ref, dxs_ref, db_ref, dc_ref, dacs_ref, ddt_ref, gnw_ref, gdsk_ref, dstate):
        @pl.when(pl.program_id(1) == 0)
        def _():
            dstate[...] = jnp.zeros_like(dstate)
            gnw_ref[...] = jnp.zeros_like(gnw_ref)
            gdsk_ref[...] = jnp.zeros_like(gdsk_ref)

        last_row = lax.broadcasted_iota(jnp.int32, (q, 1), 0) == q - 1
        lane = lax.broadcasted_iota(jnp.int32, (q, LANES), 1)
        lane1 = lax.broadcasted_iota(jnp.int32, (8, LANES), 1)
        lower, upper = _head_tri(q, True), _head_tri(q, False)
        bd_mask = _block_diag_mask()
        for gi in range(gps):
            g = gps * pl.program_id(0) + gi
            cols = slice(GROUP_W * gi, GROUP_W * (gi + 1))
            scols = slice(D_STATE * gi, D_STATE * (gi + 1))
            y = y_ref[:, cols]
            z = z_ref[:, cols]
            sz = _sigmoid(z)
            silu_z = z * sz
            yg = y * silu_z
            rstd = lax.rsqrt(jnp.mean(yg * yg, axis=-1, keepdims=True) + EPS)
            n = yg * rstd
            dout = dmix_ref[:, cols]
            gnw_ref[:, cols] += jnp.sum(dout * n, axis=0, keepdims=True)
            dn = dout * nw_ref[:, cols]
            dyg = rstd * (dn - n * jnp.mean(dn * n, axis=-1, keepdims=True))
            dy = dyg * silu_z
            dz_ref[:, cols] = _bf(dyg * y * (sz * (1.0 + z * (1.0 - sz))))

            x = xs_ref[:, cols]
            bmb, cmb = _bf(b_ref[:, scols]), _bf(c_ref[:, scols])
            hv = _group_heads(g, gi, dt_ref[...], acs_ref[...], acst_ref, dsk_ref[...])
            dec = jnp.exp(jnp.where(lower, hv.acs - hv.acs_row, NEG))
            dect = jnp.exp(jnp.where(upper, hv.acs_row - hv.acs, NEG))
            b4 = jnp.concatenate([bmb] * HPG, axis=0)
            c4 = jnp.concatenate([cmb] * HPG, axis=0)
            m_all = _nt(cmb, b4) * dec
            mt_all = _nt(bmb, c4) * dect
            xdt = x * hv.dt
            xdt_b, dyb = _bf(xdt), _bf(dy)
            x_bd, dy_bd = _block_diag(xdt_b, bd_mask), _block_diag(dyb, bd_mask)
            s_prev = st_ref[0, gi]
            spb = _bf(s_prev)
            ds_new = dstate[gi]
            dsb = _bf(ds_new)
            e = jnp.exp(hv.acs)
            elast = jnp.exp(hv.acs_last)
            dte = jnp.exp(hv.acs_last - hv.acs)
            bds = _mm(bmb, dsb)
            dxdt = _mm(_bf(mt_all), dy_bd) + bds * dte
            dm = _nt(dyb, x_bd)
            dmt = _nt(xdt_b, dy_bd)
            dye = _bf(dy * e)
            dc_ref[:, scols] = _mm(_bf(dm * dec), b4) + _nt(dye, spb)
            db_ref[:, scols] = _mm(_bf(dmt * dect), c4) + _nt(_bf(xdt * dte), dsb)
            dstate[gi] = elast * ds_new + _tn(cmb, dye)
            dxs_ref[:, cols] = dxdt * hv.dt + hv.dsk * dy
            ddte_dte = bds * xdt * dte
            dacs_l = dm * m_all - dmt * mt_all + dy * _mm(cmb, spb) * e - ddte_dte
            dlast_l = (jnp.sum(ddte_dte, axis=0, keepdims=True)
                       + jnp.sum(s_prev * ds_new, axis=0, keepdims=True) * elast)
            ddt_l = dxdt * x
            gdsk_l = jnp.sum(dy * x, axis=0, keepdims=True)
            dacs_out = jnp.zeros((q, LANES), F32)
            ddt_out = jnp.zeros((q, LANES), F32)
            gdsk = jnp.zeros((8, LANES), F32)
            for r in range(HPG):
                dacs = _head_sums(dacs_l, r) + jnp.where(last_row, _head_sums(dlast_l, r), 0.0)
                dacs_out = jnp.where(lane == r, dacs, dacs_out)
                ddt_out = jnp.where(lane == r, _head_sums(ddt_l, r), ddt_out)
                gdsk = gdsk + jnp.where(lane1 == r, _head_sums(gdsk_l, r), 0.0)
            dacs_ref[:, LANES * gi:LANES * (gi + 1)] = dacs_out
            ddt_ref[:, LANES * gi:LANES * (gi + 1)] = ddt_out
            gdsk_ref[gi] += gdsk

    rev = lambda c: nc - 1 - c
    wide = pl.BlockSpec((q, gw), lambda g, c: (rev(c), g))
    return _pallas(
        body, name="ssd_bwd", grid=(GROUPS // gps, nc),
        in_specs=[wide, wide, wide, pl.BlockSpec((1, gw), lambda g, c: (0, g)), wide,
                  pl.BlockSpec((q, sw), lambda g, c: (rev(c), D_SSD // sw + g)),
                  pl.BlockSpec((q, sw), lambda g, c: (rev(c), (D_SSD + GROUPS * D_STATE) // sw + g)),
                  pl.BlockSpec((q, LANES), lambda g, c: (rev(c), 0)), pl.BlockSpec((q, LANES), lambda g, c: (rev(c), 0)),
                  pl.BlockSpec((1, gps * GROUPS, q), lambda g, c: (rev(c), g, 0)),
                  pl.BlockSpec((1, gps, D_STATE, GROUP_W), lambda g, c: (rev(c), g, 0, 0)),
                  pl.BlockSpec((1, LANES), lambda g, c: (0, 0))],
        out_specs=[wide, wide,
                   pl.BlockSpec((q, sw), lambda g, c: (rev(c), g)), pl.BlockSpec((q, sw), lambda g, c: (rev(c), g)),
                   pl.BlockSpec((q, gps * LANES), lambda g, c: (rev(c), g)),
                   pl.BlockSpec((q, gps * LANES), lambda g, c: (rev(c), g)),
                   pl.BlockSpec((1, gw), lambda g, c: (0, g)), pl.BlockSpec((gps, 8, LANES), lambda g, c: (g, 0, 0))],
        out_shape=[jax.ShapeDtypeStruct((t, D_SSD), BF16), jax.ShapeDtypeStruct((t, D_SSD), F32),
                   jax.ShapeDtypeStruct((t, GROUPS * D_STATE), F32), jax.ShapeDtypeStruct((t, GROUPS * D_STATE), F32),
                   jax.ShapeDtypeStruct((t, GROUPS * LANES), F32), jax.ShapeDtypeStruct((t, GROUPS * LANES), F32),
                   jax.ShapeDtypeStruct((1, D_SSD), F32), jax.ShapeDtypeStruct((GROUPS, 8, LANES), F32)],
        scratch_shapes=[pltpu.VMEM((gps, D_STATE, GROUP_W), F32)],
        compiler_params=_cp("parallel", "arbitrary"))(dmix, y_ssd, proj, ssd_norm_w, xbc, xbc, xbc, dt, acs, acst,
                                                      states, d_skip_l)


def _dt_bwd(dacs_g, ddt_g, dt, proj, dt_bias_l, a_log_l):
    t = dt.shape[0]
    q = CHUNK
    nc = t // q
    cps = _chunks_per_step(nc)
    rows = cps * q

    def body(dacs_ref, ddt_ref, dt_ref, raw_ref, bias_ref, alog_ref, draw_ref, ga_ref, gb_ref):
        @pl.when(pl.program_id(0) == 0)
        def _():
            ga_ref[...] = jnp.zeros_like(ga_ref)
            gb_ref[...] = jnp.zeros_like(gb_ref)

        lane = lax.broadcasted_iota(jnp.int32, (q, LANES), 1)
        ri = lax.broadcasted_iota(jnp.int32, (q, q), 0)
        ci = lax.broadcasted_iota(jnp.int32, (q, q), 1)
        triu = (ri <= ci).astype(F32)
        a = -jnp.exp(alog_ref[...])
        used = (lane & (GROUPS - 1)) < HPG
        ga = jnp.zeros((1, LANES), F32)
        gb = jnp.zeros((1, LANES), F32)
        for k in range(cps):
            rk = slice(q * k, q * (k + 1))
            dacs = jnp.zeros((q, LANES), F32)
            ddt = jnp.zeros((q, LANES), F32)
            for g in range(GROUPS):
                mask = (lane >= GROUPS * g) & (lane < GROUPS * g + HPG)
                sl = slice(LANES * g, LANES * (g + 1))
                if g == 0:
                    dacs = jnp.where(mask, dacs_ref[rk, sl], dacs)
                    ddt = jnp.where(mask, ddt_ref[rk, sl], ddt)
                else:
                    dacs = jnp.where(mask, pltpu.roll(dacs_ref[rk, sl], GROUPS * g, 1), dacs)
                    ddt = jnp.where(mask, pltpu.roll(ddt_ref[rk, sl], GROUPS * g, 1), ddt)
            dda = jnp.dot(triu, dacs, preferred_element_type=F32, precision=HI)
            row = pl.program_id(0) * rows + q * k + lax.broadcasted_iota(jnp.int32, (q, LANES), 0)
            dsp = jnp.where((row >= PAD_LEAD) & used, dda * a + ddt, 0.0)
            draw = dsp * _sigmoid(raw_ref[rk, :] + bias_ref[...])
            draw_ref[rk, :] = _bf(draw)
            gb = gb + jnp.sum(draw, axis=0, keepdims=True)
            ga = ga + jnp.sum(jnp.where(used, dda * dt_ref[rk, :], 0.0), axis=0, keepdims=True)
        gb_ref[0:1, :] += gb
        ga_ref[0:1, :] += ga * a

    return _pallas(
        body, name="dt_bwd", grid=(nc // cps,),
        in_specs=[pl.BlockSpec((rows, GROUPS * LANES), lambda c: (c, 0)),
                  pl.BlockSpec((rows, GROUPS * LANES), lambda c: (c, 0)),
                  pl.BlockSpec((rows, LANES), lambda c: (c, 0)), pl.BlockSpec((rows, LANES), lambda c: (c, ODT // LANES)),
                  pl.BlockSpec((1, LANES), lambda c: (0, 0)), pl.BlockSpec((1, LANES), lambda c: (0, 0))],
        out_specs=[pl.BlockSpec((rows, LANES), lambda c: (c, 0)), pl.BlockSpec((8, LANES), lambda c: (0, 0)),
                   pl.BlockSpec((8, LANES), lambda c: (0, 0))],
        out_shape=[jax.ShapeDtypeStruct((t, LANES), BF16), jax.ShapeDtypeStruct((8, LANES), F32),
                   jax.ShapeDtypeStruct((8, LANES), F32)],
        compiler_params=_cp("arbitrary"))(dacs_g, ddt_g, dt, proj, dt_bias_l, a_log_l)


def _conv_bwd(dseg, proj, conv_w, conv_b, col_off, name):
    t, width = dseg.shape
    tc = 128
    rt = _tile(t, 320)
    off_p = (OXS + col_off) // tc
    off_w = col_off // tc

    def body(d_ref, x_ref, w_ref, b_ref, dx_ref, gw_ref, gb_ref, xp, dup):
        xp[0:8, :] = jnp.zeros((8, tc), F32)
        xp[8:t + 8, :] = x_ref[...]
        dup[t:t + 8, :] = jnp.zeros((8, tc), F32)
        w = w_ref[...]
        bias = b_ref[...]

        def first(i, acc):
            r0 = pl.multiple_of(i * rt, 8)
            xs = [xp[pl.ds(r0 + 5 + k, rt), :] for k in range(CONV_WIDTH)]
            u = bias + w[3:4, :] * xs[3] + w[2:3, :] * xs[2] + w[1:2, :] * xs[1] + w[0:1, :] * xs[0]
            su = 0.5 + 0.5 * jnp.tanh(0.5 * u)
            du = d_ref[pl.ds(r0, rt), :] * (su * (1.0 + u * (1.0 - su)))
            dup[pl.ds(r0, rt), :] = du
            return tuple(acc[k] + jnp.sum(du * xs[k], axis=0, keepdims=True) for k in range(CONV_WIDTH)) + (
                acc[CONV_WIDTH] + jnp.sum(du, axis=0, keepdims=True),)

        zero = jnp.zeros((1, tc), F32)
        acc = lax.fori_loop(0, t // rt, first, (zero,) * (CONV_WIDTH + 1))
        gw_ref[...] = jnp.concatenate(acc[:CONV_WIDTH], axis=0)
        gb_ref[...] = acc[CONV_WIDTH]

        def second(i, carry):
            r0 = pl.multiple_of(i * rt, 16)
            dx_ref[pl.ds(r0, rt), :] = _bf(w[3:4, :] * dup[pl.ds(r0, rt), :] + w[2:3, :] * dup[pl.ds(r0 + 1, rt), :]
                                          + w[1:2, :] * dup[pl.ds(r0 + 2, rt), :] + w[0:1, :] * dup[pl.ds(r0 + 3, rt), :])
            return carry

        lax.fori_loop(0, t // rt, second, 0)

    return _pallas(
        body, name=name, grid=(width // tc,),
        in_specs=[pl.BlockSpec((t, tc), lambda j: (0, j)), pl.BlockSpec((t, tc), lambda j: (0, j + off_p)),
                  pl.BlockSpec((CONV_WIDTH, tc), lambda j: (0, j + off_w)), pl.BlockSpec((1, tc), lambda j: (0, j + off_w))],
        out_specs=[pl.BlockSpec((t, tc), lambda j: (0, j)), pl.BlockSpec((CONV_WIDTH, tc), lambda j: (0, j)),
                   pl.BlockSpec((1, tc), lambda j: (0, j))],
        out_shape=[jax.ShapeDtypeStruct((t, width), BF16), jax.ShapeDtypeStruct((CONV_WIDTH, width), F32),
                   jax.ShapeDtypeStruct((1, width), F32)],
        scratch_shapes=[pltpu.VMEM((t + 8, tc), F32), pltpu.VMEM((t + 8, tc), F32)],
        compiler_params=_cp("parallel"))(dseg, proj, conv_w, conv_b)


def _dinproj(segs, w_re, hpad, norm_w, dy_t, ga):
    t = segs[0].shape[0]
    d = hpad.shape[1]
    tm, tk = _tile(t, 416), SEG_TILE
    counts = [s.shape[1] // tk for s in segs]
    firsts = [sum(counts[:s]) for s in range(len(segs))]
    nk = sum(counts)
    assert nk * tk == w_re.shape[1]
    ni = t // tm
    ns = len(segs)

    def body(*refs):
        seg_refs = refs[:ns]
        w_ref, h_ref, nw_ref, dy_ref, ga_ref, dh_ref, gnw_ref, got_ref, acc, send_sems, recv_sems = refs[ns:]
        i, k = pl.program_id(0), pl.program_id(1)

        @pl.when((i == 0) & (k == 0))
        def _():
            for cp in _exchange_copies(ga_ref, got_ref, send_sems, recv_sems):
                cp.start()
            gnw_ref[...] = jnp.zeros_like(gnw_ref)

        @pl.when(k == 0)
        def _():
            acc[...] = jnp.zeros_like(acc)

        for s in range(ns):
            @pl.when((k >= firsts[s]) & (k < firsts[s] + counts[s]))
            def _(s=s):
                acc[...] += _nt(seg_refs[s][...], w_ref[...])

        @pl.when(k == nk - 1)
        def _():
            h = h_ref[...]
            rstd = lax.rsqrt(jnp.mean(h * h, axis=-1, keepdims=True) + EPS)
            nrm = h * rstd
            dhn = acc[...]
            gnw_ref[...] += jnp.sum(dhn * nrm, axis=0, keepdims=True)
            dn = dhn * nw_ref[...]
            dh_ref[...] = rstd * (dn - nrm * jnp.mean(dn * nrm, axis=-1, keepdims=True)) + dy_ref[...]

        @pl.when((i == ni - 1) & (k == nk - 1))
        def _():
            for cp in _exchange_copies(ga_ref, got_ref, send_sems, recv_sems):
                cp.wait()

    seg_specs = [pl.BlockSpec((tm, tk), functools.partial(lambda i, k, f0, n0: (i, jnp.clip(k - f0, 0, n0 - 1)),
                                                          f0=firsts[s], n0=counts[s])) for s in range(ns)]
    return _pallas(
        body, name="dinproj", grid=(ni, nk),
        in_specs=seg_specs + [pl.BlockSpec((d, tk), lambda i, k: (0, k)),
                              pl.BlockSpec((tm, d), lambda i, k: (i, 0)), pl.BlockSpec((1, d), lambda i, k: (0, 0)),
                              pl.BlockSpec((tm, d), lambda i, k: (i, 0)), ANY],
        out_specs=[pl.BlockSpec((tm, d), lambda i, k: (i, 0)), pl.BlockSpec((1, d), lambda i, k: (0, 0)), ANY],
        out_shape=[jax.ShapeDtypeStruct((t, d), F32), jax.ShapeDtypeStruct((1, d), F32), _exchange_shape(ga)],
        scratch_shapes=[pltpu.VMEM((tm, d), F32)] + _exchange_scratch(),
        compiler_params=_cp("arbitrary", "arbitrary"))(*segs, w_re, hpad, norm_w, dy_t, ga)


def _spread_heads(v):
    v = jnp.pad(v.reshape(GROUPS, HPG), ((0, 0), (0, GROUPS - HPG))).reshape(1, GROUPS * GROUPS)
    return jnp.pad(v, ((0, 0), (0, LANES - GROUPS * GROUPS)))


def _gather_heads(v):
    return v[0:1, :GROUPS * GROUPS].reshape(GROUPS, GROUPS)[:, :HPG].reshape(1, SSD_HEADS)


def _rope_tables(t):
    half = HEAD_DIM // 2
    inv = ROPE_THETA ** (-jnp.arange(half, dtype=F32) / half)
    pos = (jnp.arange(t) - PAD_LEAD).astype(F32)
    ang = pos[:, None] * inv[None, :]
    cos, sin = jnp.cos(ang), jnp.sin(ang)
    cos_t = jnp.concatenate([cos, cos, cos, cos], axis=1)
    sin_t = jnp.concatenate([-sin, sin, -sin, sin], axis=1)
    return cos_t, sin_t


def _column_pieces():
    runs = [(0, OB + 2 * GROUPS * D_STATE, 0)]
    o = OB + 2 * GROUPS * D_STATE
    runs += [(o + HPG * g, HPG, ODT + GROUPS * g) for g in range(GROUPS)]
    o += SSD_HEADS
    for width, dst in ((D_ATT, OQ), (D_KV, OK), (D_KV, OV), (D_ATT, OG)):
        runs.append((o, width, dst))
        o += width
    assert o == D_IN
    pieces = []
    for o0, width, dst in runs:
        for j in range(N_SHARD):
            lo, hi = max(o0, W_IN_SHARD * j), min(o0 + width, W_IN_SHARD * (j + 1))
            if lo < hi:
                pieces.append((j, lo - W_IN_SHARD * j, hi - W_IN_SHARD * j, dst + lo - o0))
    return pieces


def _shards_to_re(w_all):
    _, k, _ = w_all.shape
    tr = 256

    def body(x_ref, o_ref):
        o_ref[:, ODT:ODT + DT_SLAB] = jnp.zeros((tr, DT_SLAB), o_ref.dtype)
        for j, c0, c1, d0 in _column_pieces():
            o_ref[:, d0:d0 + c1 - c0] = x_ref[j, :, c0:c1]

    return _pallas(body, name="shards_to_re", grid=(k // tr,),
                   in_specs=[pl.BlockSpec((N_SHARD, tr, W_IN_SHARD), lambda i: (0, i, 0))],
                   out_specs=pl.BlockSpec((tr, N_RE), lambda i: (i, 0)),
                   out_shape=jax.ShapeDtypeStruct((k, N_RE), w_all.dtype), compiler_params=_cp("parallel"))(w_all)


def _pair_add_to_shards(parts, got, pieces, shard_rows, core, name):
    n = parts[0].shape[1]
    hn = n // 2
    tc = 128
    nt = hn // tc
    ns = len(parts)
    starts = [sum(p.shape[0] for p in parts[:s]) for s in range(ns)]
    moves = []
    for j, c0, c1, d0 in pieces:
        for s, p in enumerate(parts):
            lo, hi = max(d0, starts[s]), min(d0 + c1 - c0, starts[s] + p.shape[0])
            if lo < hi:
                moves.append((s, lo - starts[s], j, c0 + lo - d0, hi - lo))
    assert sum(m[4] for m in moves) == N_SHARD * shard_rows

    def body(core_ref, *refs):
        own, theirs, o_ref, acc = refs[:ns], refs[ns:2 * ns], refs[2 * ns], refs[2 * ns + 1]
        for s, r0, j, c0, rows in moves:
            acc[j, c0:c0 + rows, :] = own[s][r0:r0 + rows, :] + theirs[s][r0:r0 + rows, :]
        o_ref[...] = _bf(acc[...])

    return _pallas(
        body, name=name,
        grid_spec=pltpu.PrefetchScalarGridSpec(
            num_scalar_prefetch=1, grid=(nt,),
            in_specs=[pl.BlockSpec((p.shape[0], tc), lambda i, core_ref: (0, core_ref[0] * nt + i)) for p in parts]
            + [pl.BlockSpec((p.shape[0], tc), lambda i, core_ref: (0, i)) for p in parts],
            out_specs=pl.BlockSpec((N_SHARD, shard_rows, tc), lambda i, core_ref: (0, 0, i)),
            scratch_shapes=[pltpu.VMEM((N_SHARD, shard_rows, tc), F32)]),
        out_shape=jax.ShapeDtypeStruct((N_SHARD, shard_rows, hn), BF16),
        compiler_params=_cp("parallel"))(core, *parts, *got)


def _local_step(x, target, meta, norm_pre_w, w_re, conv_w, conv_b, dt_bias, a_log, d_skip, ssd_norm_w, sinks,
                w_out_shard, norm_post_w, place):
    seq = x.shape[0]
    t = PAD_LEAD + N_META + seq
    hpad = jnp.concatenate([jnp.zeros((PAD_LEAD, D_MODEL), F32), meta, x], axis=0)
    dt_bias_l, a_log_l, d_skip_l = _spread_heads(dt_bias), _spread_heads(a_log), _spread_heads(d_skip)
    cos_t, sin_t = _rope_tables(t)
    sink_v = sinks.reshape(Q_HEADS)

    proj, hn, w_out_all = _inproj(hpad, norm_pre_w, w_re, w_out_shard)
    w_out = w_out_all.reshape(D_MIX, D_MODEL)
    xbc = _conv_fwd(proj, conv_w, conv_b)
    dt, acs, acst = _dt_prep(proj, dt_bias_l, a_log_l)
    y_ssd, ymix, states = _ssd_fwd(xbc, proj, dt, acs, acst, d_skip_l, ssd_norm_w)
    qr, kr = _rope(proj, OQ, proj, OK, cos_t, sin_t)
    amix = _attn_fwd(qr, kr, proj, sink_v)
    out = _outproj(ymix, amix, w_out)
    dout, dy_t, loss_blk, g_norm_post = _post_loss(out, x, target, norm_post_w)

    g_out_y = _tn_matmul(ymix, dout, "gw_out_y")
    g_out_a, got_y = _tn_matmul(amix, dout, "gw_out_a", carry=g_out_y)
    dmix, got_a = _nt_matmul(dout, w_out, "dmix", carry=g_out_a)
    ga_out = _reduce_pair([g_out_y, g_out_a], [got_y, got_a], [(j, 0, W_OUT_SHARD, W_OUT_SHARD * j) for j in range(N_SHARD)],
                          W_OUT_SHARD, place, "gw_out")
    dq_r, dg, dk_r, dv, gs, slabs_out = _attn_bwd(qr, kr, proj, dmix, sink_v, ga_out)
    g_w_out = _reduce_finish(ga_out, slabs_out, place, "gw_out")
    dq, dk = _rope(dq_r, 0, dk_r, 0, cos_t, -sin_t)
    dz, dxs, db, dc, dacs_g, ddt_g, g_ssd_norm, gdsk = _ssd_bwd(dmix, y_ssd, xbc, proj, dt, acs, acst, states,
                                                                d_skip_l, ssd_norm_w)
    draw, ga, gb = _dt_bwd(dacs_g, ddt_g, dt, proj, dt_bias_l, a_log_l)
    dxs_p, gcw0, gcb0 = _conv_bwd(dxs, proj, conv_w, conv_b, 0, "conv_bwd_x")
    db_p, gcw1, gcb1 = _conv_bwd(db, proj, conv_w, conv_b, D_SSD, "conv_bwd_b")
    dc_p, gcw2, gcb2 = _conv_bwd(dc, proj, conv_w, conv_b, D_SSD + GROUPS * D_STATE, "conv_bwd_c")
    tail = jnp.concatenate([dk, _bf(dv), draw, jnp.zeros((t, DT_SLAB - LANES), BF16)], axis=1)
    segs = [dz, dxs_p, db_p, dc_p, dq, dg, tail]
    g_parts, got_parts = [_tn_matmul(segs[0], hn, "gw_in_0")], []
    for s in range(1, len(segs)):
        part, got = _tn_matmul(segs[s], hn, "gw_in_%d" % s, carry=g_parts[-1])
        g_parts.append(part)
        got_parts.append(got)
    ga_in = _reduce_pair(g_parts, got_parts, _column_pieces(), W_IN_SHARD, place, "gw_in")
    dh, g_norm_pre, slabs_in = _dinproj(segs, w_re, hpad, norm_pre_w, dy_t, ga_in)
    g_w_in_half = _chip_sum(ga_in, slabs_in, place, "gw_in_chip_sum")

    gdsk_l = jnp.concatenate([gdsk[g, 0:1, 0:GROUPS] for g in range(GROUPS)], axis=1)
    gdsk_l = jnp.pad(gdsk_l, ((0, 0), (0, LANES - GROUPS * GROUPS)))
    grads = dict(
        meta_tokens=dh[PAD_LEAD:ROW0], norm_pre_w=g_norm_pre, w_in_half=g_w_in_half,
        conv_w=jnp.concatenate([gcw0, gcw1, gcw2], axis=1), conv_b=jnp.concatenate([gcb0, gcb1, gcb2], axis=1),
        dt_bias=_gather_heads(gb), a_log=_gather_heads(ga), d_skip=_gather_heads(gdsk_l), ssd_norm_w=g_ssd_norm,
        attn_sinks=gs[0:1, :Q_HEADS], w_out=g_w_out, norm_post_w=g_norm_post)
    return loss_blk[0, 0], dh[ROW0:], grads


ANY = pl.BlockSpec(memory_space=pl.ANY)
MESH = pl.DeviceIdType.MESH
GATHER_CHUNKS = 4
PAIR_CHUNKS = 8
JOIN_CHUNKS = 8


def _rcopy(src, dst, ssem, rsem, dev):
    return pltpu.make_async_remote_copy(src_ref=src, dst_ref=dst, send_sem=ssem, recv_sem=rsem, device_id=dev,
                                        device_id_type=MESH)


def _place():
    x, y, c = lax.axis_index("x"), lax.axis_index("y"), lax.axis_index("c")
    chips = [(1 - x, y), (x, 1 - y), (1 - x, 1 - y)]
    return x, y, c, chips


def _gather_plan(x_ref, out_ref, send_sems, recv_sems, local_sems, hr, kc):
    ch = hr // kc
    assert ch * kc == hr and ch % 16 == 0
    x, y, c, chips = _place()
    me = 2 * x + y
    sibling = (x, y, 1 - c)

    def piece(chip, hc, k):
        return out_ref.at[chip, pl.ds(hc * hr + k * ch, ch), :]

    def local():
        return [pltpu.make_async_copy(x_ref.at[pl.ds(k * ch, ch), :], out_ref.at[me, pl.ds(k * ch, ch), :],
                                      local_sems.at[k]) for k in range(2 * kc)]

    def first():
        return [_rcopy(x_ref.at[pl.ds(c * hr + k * ch, ch), :], piece(me, c, k), send_sems.at[j * kc + k],
                       recv_sems.at[j * kc + k], (*chip, c)) for j, chip in enumerate(chips) for k in range(kc)]

    def passed(hc):
        return [_rcopy(piece(2 * chip[0] + chip[1], hc, k), piece(2 * chip[0] + chip[1], hc, k),
                       send_sems.at[(3 + j) * kc + k], recv_sems.at[(3 + j) * kc + k], sibling)
                for j, chip in enumerate(chips) for k in range(kc)]

    def arrivals():
        return [_rcopy(piece(2 * chip[0] + chip[1], c, k), piece(2 * chip[0] + chip[1], c, k), send_sems.at[j * kc + k],
                       recv_sems.at[j * kc + k], (*chip, c)) for j, chip in enumerate(chips) for k in range(kc)]

    def start():
        for cp in local() + first():
            cp.start()

    def forward():
        for arrived in arrivals():
            arrived.wait_recv()
        for fw in passed(c):
            fw.start()

    def finish():
        for cp in passed(1 - c):
            cp.wait_recv()
        for cp in first() + passed(c):
            cp.wait_send()
        for cp in local():
            cp.wait()

    return start, forward, finish


def _gather_shards(shard, name, kc, chip, small):
    r, n = shard.shape
    hr = r // 2
    qr = hr // 2
    ch = qr // kc
    assert ch * kc == qr and ch % 16 == 0
    nflow = 12
    tr = 256

    def body(x_ref, p_ref, out_ref, slots_ref, send_sems, recv_sems, *small_sems):
        start_small, wait_small = _chip_small_exchange(p_ref, slots_ref, *small_sems)
        start_small()
        x, y, c, _ = _place()
        me, cxn, cyn, cdg = 2 * x + y, 2 * (1 - x) + y, 2 * x + 1 - y, 2 * (1 - x) + 1 - y
        xn, yn, sibling = (1 - x, y, c), (x, 1 - y, c), (x, y, 1 - c)

        def piece(chip, hc, part, k):
            return out_ref.at[chip, pl.ds(hc * hr + part * qr + k * ch, ch), :]

        def own(part, k):
            return x_ref.at[pl.ds(c * hr + part * qr + k * ch, ch), :]

        def sems(flow, k):
            return send_sems.at[flow * kc + k], recv_sems.at[flow * kc + k]

        def arrival(flow, chip, hc, part, k):
            return _rcopy(piece(chip, hc, part, k), piece(chip, hc, part, k), *sems(flow, k), sibling)

        sends = []
        for flow, part, peer in ((0, 0, xn), (1, 1, yn), (2, 0, yn), (3, 1, xn)):
            sends += [_rcopy(own(part, k), piece(me, c, part, k), *sems(flow, k), peer) for k in range(kc)]
        for cp in sends:
            cp.start()
        landing = ((0, cxn, 0), (1, cyn, 1), (2, cyn, 0), (3, cxn, 1), (4, cdg, 0), (5, cdg, 1))
        for i, (flow, chip, part) in enumerate(landing):
            for k in range(kc):
                arrival(flow, chip, c, part, k).wait_recv()
                if flow < 2:
                    on = _rcopy(piece(chip, c, part, k), piece(chip, c, part, k), *sems(4 + flow, k),
                                yn if flow == 0 else xn)
                    on.start()
                    sends.append(on)
                fw = _rcopy(piece(chip, c, part, k), piece(chip, c, part, k), *sems(6 + i, k), sibling)
                fw.start()
                sends.append(fw)
        for i, (flow, chip, part) in enumerate(landing):
            for k in range(kc):
                arrival(6 + i, chip, 1 - c, part, k).wait_recv()
        for cp in sends:
            cp.wait_send()
        wait_small()

    full = jax.ShapeDtypeStruct((N_SHARD, r, n), shard.dtype)
    others, slots = _pallas(
        body, name=name, in_specs=[ANY, ANY], out_specs=[ANY, ANY],
        out_shape=[full, jax.ShapeDtypeStruct((N_SHARD,) + small.shape, F32)],
        scratch_shapes=[pltpu.SemaphoreType.DMA((nflow * kc,)), pltpu.SemaphoreType.DMA((nflow * kc,)),
                        pltpu.SemaphoreType.DMA((3,)), pltpu.SemaphoreType.DMA((3,)), pltpu.SemaphoreType.DMA])(
                            shard, small)

    def place(chip_ref, own_ref, all_ref, o_ref):
        o_ref[0] = own_ref[...]

    gathered = _pallas(
        place, name=name + "_own",
        grid_spec=pltpu.PrefetchScalarGridSpec(
            num_scalar_prefetch=1, grid=(r // tr,),
            in_specs=[pl.BlockSpec((tr, n), lambda i, chip_ref: (i, 0)), ANY],
            out_specs=pl.BlockSpec((1, tr, n), lambda i, chip_ref: (chip_ref[0], i, 0))),
        out_shape=full, input_output_aliases={2: 0}, compiler_params=_cp("parallel"))(chip, shard, others)
    return gathered, slots


def _pair_copies(src_ref, dst_ref, send_sems, recv_sems):
    hn = src_ref.shape[1] // 2
    cw = hn // PAIR_CHUNKS
    assert cw * PAIR_CHUNKS == hn and cw % LANES == 0
    x, y, c, _ = _place()
    return [_rcopy(src_ref.at[:, pl.ds((1 - c) * hn + k * cw, cw)], dst_ref.at[:, pl.ds(k * cw, cw)],
                   send_sems.at[k], recv_sems.at[k], (x, y, 1 - c)) for k in range(PAIR_CHUNKS)]


def _pair_send(parts, name):
    n = parts[0].shape[1]
    hn = n // 2
    kc = PAIR_CHUNKS
    cw = hn // kc
    assert cw * kc == hn and cw % LANES == 0
    ns = len(parts)

    def body(*refs):
        srcs, dsts, send_sems, recv_sems = refs[:ns], refs[ns:2 * ns], refs[2 * ns], refs[2 * ns + 1]
        x, y, c, _ = _place()
        cps = [_rcopy(srcs[s].at[:, pl.ds((1 - c) * hn + k * cw, cw)], dsts[s].at[:, pl.ds(k * cw, cw)],
                      send_sems.at[s * kc + k], recv_sems.at[s * kc + k], (x, y, 1 - c))
               for s in range(ns) for k in range(kc)]
        for cp in cps:
            cp.start()
        for cp in cps:
            cp.wait()

    return _pallas(
        body, name=name, in_specs=[ANY] * ns, out_specs=[ANY] * ns,
        out_shape=[jax.ShapeDtypeStruct((p.shape[0], hn), F32) for p in parts],
        scratch_shapes=[pltpu.SemaphoreType.DMA((ns * kc,)), pltpu.SemaphoreType.DMA((ns * kc,))])(*parts)


REDUCE_TILE = 256


def _exchange_copies(g_ref, got_ref, send_sems, recv_sems):
    hn = g_ref.shape[2]
    kc = GATHER_CHUNKS
    cw = hn // kc
    assert cw * kc == hn and cw % LANES == 0
    x, y, c, chips = _place()
    return [_rcopy(g_ref.at[2 * chip[0] + chip[1], :, pl.ds(k * cw, cw)], got_ref.at[j, :, pl.ds(k * cw, cw)],
                   send_sems.at[j * kc + k], recv_sems.at[j * kc + k], (*chip, c))
            for j, chip in enumerate(chips) for k in range(kc)]


def _exchange_scratch():
    return [pltpu.SemaphoreType.DMA((3 * GATHER_CHUNKS,)), pltpu.SemaphoreType.DMA((3 * GATHER_CHUNKS,))]


def _exchange_shape(ga):
    return jax.ShapeDtypeStruct((3,) + ga.shape[1:], ga.dtype)


def _chip_sum(ga, got, place, name):
    _, r, hn = ga.shape
    tc = REDUCE_TILE
    nt = hn // tc

    def body(place_ref, own_ref, got_ref, o_ref):
        acc = own_ref[0].astype(F32)
        for j in range(3):
            acc = acc + got_ref[j].astype(F32)
        o_ref[...] = acc

    return _pallas(
        body, name=name,
        grid_spec=pltpu.PrefetchScalarGridSpec(
            num_scalar_prefetch=1, grid=(nt,),
            in_specs=[pl.BlockSpec((1, r, tc), lambda i, place_ref: (place_ref[0], 0, i)),
                      pl.BlockSpec((3, r, tc), lambda i, place_ref: (0, 0, i))],
            out_specs=pl.BlockSpec((r, tc), lambda i, place_ref: (0, place_ref[1] * nt + i))),
        out_shape=jax.ShapeDtypeStruct((r, 2 * hn), F32), compiler_params=_cp("parallel"))(place, ga, got)


def _pair_join(buf, name, small=None):
    r, n = buf.shape
    hn = n // 2
    kc = JOIN_CHUNKS
    cw = hn // kc
    assert cw * kc == hn and cw % LANES == 0

    def body(in_ref, *refs):
        if small is None:
            out_ref, send_sems, recv_sems = refs
        else:
            p_ref, out_ref, slots_ref, send_sems, recv_sems = refs[:5]
            start_small, wait_small = _small_exchange(p_ref, slots_ref, *refs[5:])
            start_small()
        x, y, c, _ = _place()
        cps = [_rcopy(out_ref.at[:, pl.ds(c * hn + k * cw, cw)], out_ref.at[:, pl.ds(c * hn + k * cw, cw)],
                      send_sems.at[k], recv_sems.at[k], (x, y, 1 - c)) for k in range(kc)]
        for cp in cps:
            cp.start()
        for k in range(kc):
            cols = out_ref.at[:, pl.ds((1 - c) * hn + k * cw, cw)]
            _rcopy(cols, cols, send_sems.at[k], recv_sems.at[k], (x, y, 1 - c)).wait_recv()
        for cp in cps:
            cp.wait_send()
        if small is not None:
            wait_small()

    sems = [pltpu.SemaphoreType.DMA((kc,)), pltpu.SemaphoreType.DMA((kc,))]
    if small is None:
        return _pallas(body, name=name, in_specs=[ANY], out_specs=ANY, out_shape=jax.ShapeDtypeStruct((r, n), F32),
                       input_output_aliases={0: 0}, scratch_shapes=sems)(buf)
    return _pallas(
        body, name=name, in_specs=[ANY, ANY], out_specs=[ANY, ANY],
        out_shape=[jax.ShapeDtypeStruct((r, n), F32), jax.ShapeDtypeStruct((N_DEV,) + small.shape, F32)],
        input_output_aliases={0: 0}, scratch_shapes=sems + _small_scratch())(buf, small)


def _reduce_pair(parts, got, pieces, shard_rows, place, tag):
    if len(got) < len(parts):
        got = list(got) + list(_pair_send(parts[len(got):], tag + "_pair_send"))
    return _pair_add_to_shards(parts, got, pieces, shard_rows, place[1:2], tag + "_pair_add")


def _reduce_finish(ga, slabs, place, tag):
    return _pair_join(_chip_sum(ga, slabs, place, tag + "_chip_sum"), tag + "_pair_join")


N_DEV = 8


def _small_exchange(p_ref, slots_ref, send_sems, recv_sems, local_sem):
    x, y, c, _ = _place()
    my = 4 * x + 2 * y + c

    def sends():
        return [_rcopy(p_ref, slots_ref.at[my], send_sems.at[k - 1], recv_sems.at[k - 1],
                       (x ^ ((k >> 2) & 1), y ^ ((k >> 1) & 1), c ^ (k & 1))) for k in range(1, N_DEV)]

    def local():
        return pltpu.make_async_copy(p_ref, slots_ref.at[my], local_sem)

    def start():
        local().start()
        for cp in sends():
            cp.start()

    def wait():
        for k in range(1, N_DEV):
            _rcopy(p_ref, slots_ref.at[my ^ k], send_sems.at[k - 1], recv_sems.at[k - 1], (x, y, c)).wait_recv()
        for cp in sends():
            cp.wait_send()
        local().wait()

    return start, wait


def _chip_small_exchange(p_ref, slots_ref, send_sems, recv_sems, local_sem):
    x, y, c, chips = _place()
    me = 2 * x + y

    def sends():
        return [_rcopy(p_ref, slots_ref.at[me], send_sems.at[j], recv_sems.at[j], (*chip, c))
                for j, chip in enumerate(chips)]

    def local():
        return pltpu.make_async_copy(p_ref, slots_ref.at[me], local_sem)

    def start():
        local().start()
        for cp in sends():
            cp.start()

    def wait():
        for j, chip in enumerate(chips):
            slot = slots_ref.at[2 * chip[0] + chip[1]]
            _rcopy(slot, slot, send_sems.at[j], recv_sems.at[j], (*chip, c)).wait_recv()
        for cp in sends():
            cp.wait_send()
        local().wait()

    return start, wait


def _small_scratch():
    return [pltpu.SemaphoreType.DMA((N_DEV - 1,)), pltpu.SemaphoreType.DMA((N_DEV - 1,)), pltpu.SemaphoreType.DMA]


def _sum_slots(slots, name):
    _, rows, n = slots.shape

    def body(s_ref, o_ref):
        acc = s_ref[0]
        for j in range(1, N_DEV):
            acc = acc + s_ref[j]
        o_ref[...] = acc

    vm = pl.BlockSpec(memory_space=pltpu.VMEM)
    return _pallas(body, name=name, in_specs=[vm], out_specs=vm, out_shape=jax.ShapeDtypeStruct((rows, n), F32))(slots)


def _adamw(w, g, m, v, name):
    r, n = w.shape
    tr = _tile(r, 256, 8)
    c1 = 1.0 / (1.0 - ADAM_B1 ** ADAM_STEP)
    c2 = 1.0 / (1.0 - ADAM_B2 ** ADAM_STEP)

    def body(w_ref, g_ref, m_ref, v_ref, d_ref, mo_ref, vo_ref, go_ref):
        gv = g_ref[...]
        mn = ADAM_B1 * m_ref[...] + (1.0 - ADAM_B1) * gv
        vn = ADAM_B2 * v_ref[...] + (1.0 - ADAM_B2) * (gv * gv)
        d_ref[...] = -ADAM_LR * ((mn * c1) / (jnp.sqrt(vn * c2) + ADAM_EPS) + ADAM_WD * w_ref[...])
        mo_ref[...] = mn
        vo_ref[...] = vn
        go_ref[...] = gv

    spec = pl.BlockSpec((tr, n), lambda i: (i, 0))
    shp = jax.ShapeDtypeStruct((r, n), F32)
    return _pallas(body, name=name, grid=(r // tr,), in_specs=[spec] * 4, out_specs=[spec] * 4, out_shape=[shp] * 4,
                   compiler_params=_cp("parallel"))(w, g, m, v)


PACK_W = 1024
SMALL_REPL = ("norm_pre_w", "conv_b", "ssd_norm_w", "norm_post_w")
SMALL_HEAD = ("dt_bias", "a_log", "d_skip", "attn_sinks")


def _rows(a):
    return a.reshape(-1, PACK_W)


def _head_row(vals, extra=None):
    parts = [vals[n].reshape(1, -1) for n in SMALL_HEAD]
    if extra is not None:
        parts.append(extra.reshape(1, 1))
    row = jnp.concatenate(parts, axis=1)
    return jnp.pad(row, ((0, 0), (0, PACK_W - row.shape[1])))


def _pad_rows(a, rows):
    return jnp.pad(a, ((0, rows - a.shape[0]), (0, 0)))


def _pack_repl(vals, extra=None):
    body = jnp.concatenate([_rows(vals[n]) for n in SMALL_REPL] + [_head_row(vals, extra)], axis=0)
    return _pad_rows(body, 16)


def _unpack_repl(buf):
    out, r = {}, 0
    for n, k in zip(SMALL_REPL, (2, 4, 2, 2)):
        out[n] = buf[r:r + k].reshape(1, k * PACK_W)
        r += k
    col = 0
    for n, k in zip(SMALL_HEAD, (32, 32, 32, 16)):
        out[n] = buf[r:r + 1, col:col + k]
        col += k
    return out, buf[r, col]


def kernel(x, meta_tokens, norm_pre_w, w_in, conv_w, conv_b, dt_bias, a_log, d_skip, ssd_norm_w, attn_sinks, w_out, norm_post_w, loss_target, m_meta_tokens, m_norm_pre_w, m_w_in, m_conv_w, m_conv_b, m_dt_bias, m_a_log, m_d_skip, m_ssd_norm_w, m_attn_sinks, m_w_out, m_norm_post_w, v_meta_tokens, v_norm_pre_w, v_w_in, v_conv_w, v_conv_b, v_dt_bias, v_a_log, v_d_skip, v_ssd_norm_w, v_attn_sinks, v_w_out, v_norm_post_w):
    names = ("meta_tokens", "norm_pre_w", "w_in", "conv_w", "conv_b", "dt_bias", "a_log", "d_skip", "ssd_norm_w",
             "attn_sinks", "w_out", "norm_post_w")
    w = dict(zip(names, (meta_tokens, norm_pre_w, w_in, conv_w, conv_b, dt_bias, a_log, d_skip, ssd_norm_w, attn_sinks,
                         w_out, norm_post_w)))
    m = dict(zip(names, (m_meta_tokens, m_norm_pre_w, m_w_in, m_conv_w, m_conv_b, m_dt_bias, m_a_log, m_d_skip,
                         m_ssd_norm_w, m_attn_sinks, m_w_out, m_norm_post_w)))
    v = dict(zip(names, (v_meta_tokens, v_norm_pre_w, v_w_in, v_conv_w, v_conv_b, v_dt_bias, v_a_log, v_d_skip,
                         v_ssd_norm_w, v_attn_sinks, v_w_out, v_norm_post_w)))
    cx, cy, cc = lax.axis_index("x"), lax.axis_index("y"), lax.axis_index("c")
    chip = 2 * cx + cy
    meta_cols = D_MODEL // N_SHARD
    conv_cols = D_CONV // N_SHARD

    place = jnp.stack([chip, cc]).astype(jnp.int32)
    small = jnp.concatenate([_pad_rows(conv_w[0], 8), _rows(meta_tokens)], axis=0)
    w_in_all, small_all = _gather_shards(_bf(w_in[0]), "gather_w_in", GATHER_CHUNKS, place[0:1], small)
    w_re = _shards_to_re(w_in_all)
    conv_full = jnp.transpose(small_all[:, 0:CONV_WIDTH], (1, 0, 2)).reshape(CONV_WIDTH, D_CONV)
    meta_full = jnp.transpose(small_all[:, 8:16].reshape(N_SHARD, N_META, meta_cols), (1, 0, 2)).reshape(N_META, D_MODEL)

    loss_dev, grad_x, g = _local_step(x[0], loss_target[0], meta_full, norm_pre_w, w_re, conv_full, conv_b, dt_bias,
                                      a_log, d_skip, ssd_norm_w, attn_sinks, _bf(w_out[0]), norm_post_w, place)
    g_w_out = g["w_out"]

    packed = jnp.concatenate([_rows(g["conv_w"]), _rows(g["meta_tokens"]), _pack_repl(g, loss_dev)], axis=0)
    g_w_in, slots = _pair_join(g["w_in_half"], "gw_in_pair_join", small=packed)
    red = _sum_slots(slots, "reduce_small")
    g_conv_full = red[0:16].reshape(CONV_WIDTH, D_CONV)
    g_meta_full = red[16:48].reshape(N_META, D_MODEL)
    g_small, loss = _unpack_repl(red[48:64])
    grads = dict(g_small)
    grads["w_in"] = g_w_in
    grads["w_out"] = g_w_out
    grads["conv_w"] = lax.dynamic_slice(g_conv_full, (0, chip * conv_cols), (CONV_WIDTH, conv_cols))
    grads["meta_tokens"] = lax.dynamic_slice(g_meta_full, (0, chip * meta_cols), (N_META, meta_cols))

    upd = {}
    upd["w_in"] = [jnp.swapaxes(a, 0, 1) for a in _adamw(jnp.swapaxes(w_in[0], 0, 1), g_w_in, jnp.swapaxes(m_w_in[0], 0, 1),
                                                         jnp.swapaxes(v_w_in[0], 0, 1), "adamw_w_in")]
    grads["w_in"] = upd["w_in"][3]
    upd["w_out"] = _adamw(w_out[0], g_w_out, m_w_out[0], v_w_out[0], "adamw_w_out")
    grads["w_out"] = upd["w_out"][3]

    def pack_small(vals, conv, meta):
        return jnp.concatenate([_pad_rows(conv.reshape(CONV_WIDTH, conv_cols), 8), _rows(meta), _pack_repl(vals)], axis=0)

    sm = _adamw(pack_small(w, w["conv_w"], w["meta_tokens"]), pack_small(grads, grads["conv_w"], grads["meta_tokens"]),
                pack_small(m, m["conv_w"], m["meta_tokens"]), pack_small(v, v["conv_w"], v["meta_tokens"]),
                "adamw_small")
    for n in names:
        if n not in ("w_in", "w_out"):
            upd[n] = [None, None, None]
    for k, buf in enumerate(sm[:3]):
        upd["conv_w"][k] = buf[0:CONV_WIDTH]
        upd["meta_tokens"][k] = buf[8:16].reshape(N_META, meta_cols)
        rest, _ = _unpack_repl(buf[16:32])
        for n in SMALL_REPL + SMALL_HEAD:
            upd[n][k] = rest[n]

    def shaped(n, a):
        return a.reshape(w[n].shape)

    outs = [loss, grad_x[None]]
    outs += [shaped(n, grads[n]) for n in names]
    for k in range(3):
        outs += [shaped(n, upd[n][k]) for n in names]
    return tuple(outs)
```

```python
import functools

import jax
import jax.numpy as jnp
from jax import lax
from jax.experimental import pallas as pl
from jax.experimental.pallas import tpu as pltpu

F32 = jnp.float32
BF16 = jnp.bfloat16

D_MODEL = 2048
CHUNK = 64
N_META = 16
PAD_LEAD = CHUNK - N_META
ROW0 = PAD_LEAD + N_META
EPS = 1e-6
SSD_HEADS = 32
HEAD_DIM = 64
GROUPS = 8
HPG = SSD_HEADS // GROUPS
D_STATE = 128
D_SSD = 2048
GROUP_W = D_SSD // GROUPS
CONV_WIDTH = 4
D_CONV = 4096
Q_HEADS = 16
KV_HEADS = 4
REP = Q_HEADS // KV_HEADS
D_ATT = 1024
D_KV = 256
BAND_CHUNKS = 3
ROPE_THETA = 10000.0
D_MIX = D_SSD + D_ATT
D_IN = 8736
N_SHARD = 4
W_IN_SHARD = D_IN // N_SHARD
W_OUT_SHARD = D_MIX // N_SHARD

OZ, OXS, OB, OC, OQ, OG, OK, OV, ODT = 0, 2048, 4096, 5120, 6144, 7168, 8192, 8448, 8704
DT_SLAB = 512
N_RE = ODT + DT_SLAB
LANES = 128

ADAM_LR, ADAM_B1, ADAM_B2, ADAM_EPS, ADAM_WD, ADAM_STEP = 0.001, 0.9, 0.999, 1e-08, 0.01, 10

SSD_FWD_GROUPS_PER_STEP = 4
SSD_BWD_GROUPS_PER_STEP = 8
SEG_TILE = 1024
VMEM_LIMIT = 52 * 1024 * 1024
NEG = -1e30
HI = lax.Precision.HIGHEST


def _pallas(body, **kw):
    return pl.pallas_call(body, **kw)


def _cp(*sem):
    return pltpu.CompilerParams(dimension_semantics=sem, vmem_limit_bytes=VMEM_LIMIT)


def _tile(n, cap, mult=16):
    best = None
    for d in range(mult, min(n, cap) + 1, mult):
        if n % d == 0:
            best = d
    assert best is not None, (n, cap)
    return best


def _nt(a, b):
    return lax.dot_general(a, b, (((1,), (1,)), ((), ())), preferred_element_type=F32)


def _tn(a, b):
    return lax.dot_general(a, b, (((0,), (0,)), ((), ())), preferred_element_type=F32)


def _mm(a, b):
    return jnp.dot(a, b, preferred_element_type=F32)


def _sigmoid(x):
    return 1.0 / (1.0 + jnp.exp(-x))


def _bf(x):
    return x.astype(BF16)


def _inproj(hpad, norm_w, w_re, w_out_shard):
    t, d = hpad.shape
    n = w_re.shape[1]
    tm, tn = _tile(t, 1040), 1024
    ni, nj = t // tm, n // tn
    r_out, n_out = w_out_shard.shape
    kc = GATHER_CHUNKS

    def body(h_ref, nw_ref, w_ref, ws_ref, proj_ref, hn_ref, wall_ref, hn_s, send_sems, recv_sems, local_sems):
        i, j = pl.program_id(0), pl.program_id(1)
        start, forward, finish = _gather_plan(ws_ref, wall_ref, send_sems, recv_sems, local_sems, r_out // 2, kc)
        pl.when((i == 0) & (j == 0))(start)
        pl.when((i == ni // 2) & (j == 0))(forward)

        @pl.when(j == 0)
        def _():
            h = h_ref[...]
            ms = jnp.mean(h * h, axis=-1, keepdims=True)
            hn = _bf(h * lax.rsqrt(ms + EPS) * nw_ref[...])
            hn_s[...] = hn
            hn_ref[...] = hn
        proj_ref[...] = _mm(hn_s[...], w_ref[...])
        pl.when((i == ni - 1) & (j == nj - 1))(finish)

    return _pallas(
        body, name="inproj", grid=(ni, nj),
        in_specs=[pl.BlockSpec((tm, d), lambda i, j: (i, 0)), pl.BlockSpec((1, d), lambda i, j: (0, 0)),
                  pl.BlockSpec((d, tn), lambda i, j: (0, j)), ANY],
        out_specs=[pl.BlockSpec((tm, tn), lambda i, j: (i, j)), pl.BlockSpec((tm, d), lambda i, j: (i, 0)), ANY],
        out_shape=[jax.ShapeDtypeStruct((t, n), F32), jax.ShapeDtypeStruct((t, d), BF16),
                   jax.ShapeDtypeStruct((N_SHARD, r_out, n_out), w_out_shard.dtype)],
        scratch_shapes=[pltpu.VMEM((tm, d), BF16), pltpu.SemaphoreType.DMA((6 * kc,)), pltpu.SemaphoreType.DMA((6 * kc,)),
                        pltpu.SemaphoreType.DMA((2 * kc,))],
        compiler_params=_cp("arbitrary", "arbitrary"))(hpad, norm_w, w_re, w_out_shard)


def _conv_fwd(proj, conv_w, conv_b):
    t = proj.shape[0]
    tc = 256
    off = OXS // tc

    def body(x_ref, w_ref, b_ref, o_ref):
        x = x_ref[...]
        w = w_ref[...]
        row = lax.broadcasted_iota(jnp.int32, (t, tc), 0)
        u = b_ref[...] + w[3:4, :] * x
        for k in range(1, CONV_WIDTH):
            u = u + w[3 - k:4 - k, :] * jnp.where(row >= k, pltpu.roll(x, k, 0), 0.0)
        h = 0.5 * u
        o_ref[...] = h + h * jnp.tanh(h)

    return _pallas(
        body, name="conv_fwd", grid=(D_CONV // tc,),
        in_specs=[pl.BlockSpec((t, tc), lambda j: (0, j + off)), pl.BlockSpec((CONV_WIDTH, tc), lambda j: (0, j)),
                  pl.BlockSpec((1, tc), lambda j: (0, j))],
        out_specs=pl.BlockSpec((t, tc), lambda j: (0, j)),
        out_shape=jax.ShapeDtypeStruct((t, D_CONV), F32),
        compiler_params=_cp("parallel"))(proj, conv_w, conv_b)


def _softplus(u):
    e = jnp.exp(-jnp.abs(u))
    w = 1.0 + e
    l1p = jnp.where(w == 1.0, e, jnp.log(w) * (e / jnp.where(w == 1.0, 1.0, w - 1.0)))
    return jnp.maximum(u, 0.0) + l1p


def _chunks_per_step(nc):
    return max(d for d in range(1, 14) if nc % d == 0)


def _dt_prep(proj, dt_bias_l, a_log_l):
    t = proj.shape[0]
    nc = t // CHUNK
    q = CHUNK
    cps = _chunks_per_step(nc)
    rows = cps * q

    def body(raw_ref, bias_ref, alog_ref, dt_ref, acs_ref, acst_ref):
        ri = lax.broadcasted_iota(jnp.int32, (q, q), 0)
        ci = lax.broadcasted_iota(jnp.int32, (q, q), 1)
        tri = (ri >= ci).astype(F32)
        neg_a = -jnp.exp(alog_ref[...])
        for k in range(cps):
            rk = slice(q * k, q * (k + 1))
            sp = _softplus(raw_ref[rk, :] + bias_ref[...])
            row = pl.program_id(0) * rows + q * k + lax.broadcasted_iota(jnp.int32, (q, LANES), 0)
            dt = jnp.where(row >= PAD_LEAD, sp, 0.0)
            acs = jnp.dot(tri, dt * neg_a, preferred_element_type=F32, precision=HI)
            dt_ref[rk, :] = dt
            acs_ref[rk, :] = acs
            acst_ref[k] = acs.T

    return _pallas(
        body, name="dt_prep", grid=(nc // cps,),
        in_specs=[pl.BlockSpec((rows, LANES), lambda c: (c, ODT // LANES)), pl.BlockSpec((1, LANES), lambda c: (0, 0)),
                  pl.BlockSpec((1, LANES), lambda c: (0, 0))],
        out_specs=[pl.BlockSpec((rows, LANES), lambda c: (c, 0)), pl.BlockSpec((rows, LANES), lambda c: (c, 0)),
                   pl.BlockSpec((cps, LANES, q), lambda c: (c, 0, 0))],
        out_shape=[jax.ShapeDtypeStruct((t, LANES), F32), jax.ShapeDtypeStruct((t, LANES), F32),
                   jax.ShapeDtypeStruct((nc, LANES, q), F32)],
        compiler_params=_cp("parallel"))(proj, dt_bias_l, a_log_l)


def _head_cols(blk, idx):
    lane = lax.broadcasted_iota(jnp.int32, blk.shape, 1)
    return jnp.sum(jnp.where(lane == idx, blk, 0.0), axis=1, keepdims=True)


class _HeadVals:
    pass


def _lane_head(shape):
    return lax.broadcasted_iota(jnp.int32, shape, len(shape) - 1) >> 6


def _group_heads(g, gi, dtb, acsb, acst_ref, dskb):
    q = dtb.shape[0]
    hv = _HeadVals()
    lh = _lane_head((1, GROUP_W))
    hv.dt = jnp.zeros((q, GROUP_W), F32)
    hv.acs = jnp.zeros((q, GROUP_W), F32)
    hv.acs_last = jnp.zeros((1, GROUP_W), F32)
    hv.dsk = jnp.zeros((1, GROUP_W), F32)
    rows = []
    for r in range(HPG):
        idx = GROUPS * g + r
        sel = lh == r
        acs_r = acst_ref[0, GROUPS * gi + r:GROUPS * gi + r + 1, :]
        rows.append(acs_r)
        hv.dt = jnp.where(sel, _head_cols(dtb, idx), hv.dt)
        hv.acs = jnp.where(sel, _head_cols(acsb, idx), hv.acs)
        hv.acs_last = jnp.where(sel, acs_r[:, q - 1:q], hv.acs_last)
        hv.dsk = jnp.where(sel, _head_cols(dskb, idx), hv.dsk)
    hv.acs_row = jnp.concatenate(rows, axis=1)
    return hv


def _head_tri(q, lower):
    ri = lax.broadcasted_iota(jnp.int32, (q, GROUP_W), 0)
    li = lax.broadcasted_iota(jnp.int32, (q, GROUP_W), 1) & (HEAD_DIM - 1)
    return ri >= li if lower else ri <= li


def _block_diag_mask():
    rb = lax.broadcasted_iota(jnp.int32, (GROUP_W, GROUP_W), 0) >> 6
    cb = lax.broadcasted_iota(jnp.int32, (GROUP_W, GROUP_W), 1) >> 6
    return rb == cb


def _block_diag(v, mask):
    return jnp.where(mask, jnp.concatenate([v] * HPG, axis=0), jnp.zeros((), v.dtype))


def _head_sums(v, r):
    return jnp.sum(jnp.where(_lane_head((1, GROUP_W)) == r, v, 0.0), axis=1, keepdims=True)


def _ssd_fwd(xbc, proj, dt, acs, acst, d_skip_l, ssd_norm_w):
    t = xbc.shape[0]
    q = CHUNK
    nc = t // q

    gps = SSD_FWD_GROUPS_PER_STEP
    gw, sw = gps * GROUP_W, gps * D_STATE

    def body(xs_ref, b_ref, c_ref, dt_ref, acs_ref, acst_ref, z_ref, dsk_ref, nw_ref,
             y_ref, ymix_ref, st_ref, state):
        @pl.when(pl.program_id(1) == 0)
        def _():
            state[...] = jnp.zeros_like(state)

        lower = _head_tri(q, True)
        bd_mask = _block_diag_mask()
        for gi in range(gps):
            g = gps * pl.program_id(0) + gi
            cols = slice(GROUP_W * gi, GROUP_W * (gi + 1))
            x = xs_ref[:, cols]
            bmb = _bf(b_ref[:, D_STATE * gi:D_STATE * (gi + 1)])
            cmb = _bf(c_ref[:, D_STATE * gi:D_STATE * (gi + 1)])
            hv = _group_heads(g, gi, dt_ref[...], acs_ref[...], acst_ref, dsk_ref[...])
            decay = jnp.exp(jnp.where(lower, hv.acs - hv.acs_row, NEG))
            m_all = _bf(_nt(cmb, jnp.concatenate([bmb] * HPG, axis=0)) * decay)
            xdt = x * hv.dt
            s_prev = state[gi]
            st_ref[0, gi] = s_prev
            y = (_mm(m_all, _block_diag(_bf(xdt), bd_mask)) + _mm(cmb, _bf(s_prev)) * jnp.exp(hv.acs) + hv.dsk * x)
            state[gi] = jnp.exp(hv.acs_last) * s_prev + _tn(bmb, _bf(xdt * jnp.exp(hv.acs_last - hv.acs)))
            y_ref[:, cols] = y
            z = z_ref[:, cols]
            yg = y * (z * _sigmoid(z))
            ms = jnp.mean(yg * yg, axis=-1, keepdims=True)
            ymix_ref[:, cols] = _bf(yg * lax.rsqrt(ms + EPS) * nw_ref[:, cols])

    return _pallas(
        body, name="ssd_fwd", grid=(GROUPS // gps, nc),
        in_specs=[pl.BlockSpec((q, gw), lambda g, c: (c, g)),
                  pl.BlockSpec((q, sw), lambda g, c: (c, D_SSD // sw + g)),
                  pl.BlockSpec((q, sw), lambda g, c: (c, (D_SSD + GROUPS * D_STATE) // sw + g)),
                  pl.BlockSpec((q, LANES), lambda g, c: (c, 0)), pl.BlockSpec((q, LANES), lambda g, c: (c, 0)),
                  pl.BlockSpec((1, gps * GROUPS, q), lambda g, c: (c, g, 0)),
                  pl.BlockSpec((q, gw), lambda g, c: (c, g)),
                  pl.BlockSpec((1, LANES), lambda g, c: (0, 0)), pl.BlockSpec((1, gw), lambda g, c: (0, g))],
        out_specs=[pl.BlockSpec((q, gw), lambda g, c: (c, g)), pl.BlockSpec((q, gw), lambda g, c: (c, g)),
                   pl.BlockSpec((1, gps, D_STATE, GROUP_W), lambda g, c: (c, g, 0, 0))],
        out_shape=[jax.ShapeDtypeStruct((t, D_SSD), F32), jax.ShapeDtypeStruct((t, D_SSD), BF16),
                   jax.ShapeDtypeStruct((nc, GROUPS, D_STATE, GROUP_W), F32)],
        scratch_shapes=[pltpu.VMEM((gps, D_STATE, GROUP_W), F32)],
        compiler_params=_cp("parallel", "arbitrary"))(xbc, xbc, xbc, dt, acs, acst, proj, d_skip_l, ssd_norm_w)


def _swap_halves(v):
    lane = lax.broadcasted_iota(jnp.int32, v.shape, 1)
    return jnp.where((lane & (HEAD_DIM - 1)) < HEAD_DIM // 2, pltpu.roll(v, LANES - HEAD_DIM // 2, 1),
                     pltpu.roll(v, HEAD_DIM // 2, 1))


def _rope(qsrc, q_off, ksrc, k_off, cos_t, sin_t):
    t = qsrc.shape[0]
    tr = _tile(t, 832)
    q_scale = HEAD_DIM ** -0.5

    def body(q_ref, k_ref, cos_ref, sin_ref, qo_ref, ko_ref):
        cs = cos_ref[...]
        sn = sin_ref[...]
        for src, dst, width, scale in ((q_ref, qo_ref, D_ATT, q_scale), (k_ref, ko_ref, D_KV, 1.0)):
            for s in range(width // LANES):
                v = src[:, LANES * s:LANES * (s + 1)].astype(F32)
                dst[:, LANES * s:LANES * (s + 1)] = _bf((v * cs + _swap_halves(v) * sn) * scale)

    return _pallas(
        body, name="rope", grid=(t // tr,),
        in_specs=[pl.BlockSpec((tr, D_ATT), lambda i: (i, q_off // D_ATT)),
                  pl.BlockSpec((tr, D_KV), lambda i: (i, k_off // D_KV)),
                  pl.BlockSpec((tr, LANES), lambda i: (i, 0)), pl.BlockSpec((tr, LANES), lambda i: (i, 0))],
        out_specs=[pl.BlockSpec((tr, D_ATT), lambda i: (i, 0)), pl.BlockSpec((tr, D_KV), lambda i: (i, 0))],
        out_shape=[jax.ShapeDtypeStruct((t, D_ATT), BF16), jax.ShapeDtypeStruct((t, D_KV), BF16)],
        compiler_params=_cp("parallel"))(qsrc, ksrc, cos_t, sin_t)


def _attn_chunks_per_step(nc):
    return max(d for d in range(1, 6) if nc % d == 0)


def _band(ref, c):
    return [ref[pl.ds(pl.multiple_of(jnp.maximum(c - j, 0) * CHUNK, CHUNK), CHUNK), :] for j in (2, 1, 0)]


def _attn_probs(qh, kb, sink_col, valid):
    s = jnp.where(valid, _nt(qh, kb), NEG)
    m = jnp.maximum(jnp.max(s, axis=1, keepdims=True), sink_col)
    p = jnp.exp(s - m)
    psink = jnp.exp(sink_col - m)
    return p, psink, 1.0 / (jnp.sum(p, axis=1, keepdims=True) + psink)


def _attn_operands(c, q, k_refs, v_refs, sink_ref, h):
    qh = jnp.concatenate([q[:, HEAD_DIM * (REP * h + r):HEAD_DIM * (REP * h + r + 1)] for r in range(REP)], axis=0)
    kb = jnp.concatenate([k[:, HEAD_DIM * h:HEAD_DIM * (h + 1)] for k in k_refs], axis=0)
    vb = jnp.concatenate([_bf(v[:, HEAD_DIM * h:HEAD_DIM * (h + 1)]) for v in v_refs], axis=0)
    rows = lax.broadcasted_iota(jnp.int32, (REP * CHUNK, 1), 0) >> 6
    sink_col = jnp.zeros((REP * CHUNK, 1), F32)
    for r in range(REP):
        sink_col = jnp.where(rows == r, sink_ref[REP * h + r], sink_col)
    key_abs = (c - (BAND_CHUNKS - 1)) * CHUNK + lax.broadcasted_iota(jnp.int32, (1, BAND_CHUNKS * CHUNK), 1)
    return qh, kb, vb, sink_col, key_abs >= PAD_LEAD


def _attn_fwd(qr, kr, proj, sinks):
    t = qr.shape[0]
    nc = t // CHUNK
    cps = _attn_chunks_per_step(nc)
    rows = cps * CHUNK

    def body(q_ref, k_ref, v_ref, g_ref, sink_ref, o_ref):
        for j in range(cps):
            c = pl.program_id(0) * cps + j
            rj = slice(CHUNK * j, CHUNK * (j + 1))
            ks, vs = _band(k_ref, c), _band(v_ref, c)
            q = q_ref[rj, :]
            outs = []
            for h in range(KV_HEADS):
                qh, kb, vb, sink_col, valid = _attn_operands(c, q, ks, vs, sink_ref, h)
                p, _, inv = _attn_probs(qh, kb, sink_col, valid)
                o = _mm(_bf(p), vb) * inv
                outs += [o[CHUNK * r:CHUNK * (r + 1)] for r in range(REP)]
            att = jnp.concatenate(outs, axis=1)
            gate = g_ref[rj, :]
            o_ref[rj, :] = _bf(att * (gate * _sigmoid(gate)))

    return _pallas(
        body, name="attn_fwd", grid=(nc // cps,),
        in_specs=[pl.BlockSpec((rows, D_ATT), lambda i: (i, 0)), pl.BlockSpec((t, D_KV), lambda i: (0, 0)),
                  pl.BlockSpec((t, D_KV), lambda i: (0, OV // D_KV)),
                  pl.BlockSpec((rows, D_ATT), lambda i: (i, OG // D_ATT)), pl.BlockSpec(memory_space=pltpu.SMEM)],
        out_specs=pl.BlockSpec((rows, D_ATT), lambda i: (i, 0)),
        out_shape=jax.ShapeDtypeStruct((t, D_ATT), BF16),
        compiler_params=_cp("parallel"))(qr, kr, proj, proj, sinks)


def _outproj(ymix, amix, w_out):
    t = ymix.shape[0]
    tm, tn = _tile(t, 1040), 1024

    def body(y_ref, a_ref, wy_ref, wa_ref, o_ref):
        o_ref[...] = _mm(y_ref[...], wy_ref[...]) + _mm(a_ref[...], wa_ref[...])

    return _pallas(
        body, name="outproj", grid=(t // tm, D_MODEL // tn),
        in_specs=[pl.BlockSpec((tm, D_SSD), lambda i, j: (i, 0)), pl.BlockSpec((tm, D_ATT), lambda i, j: (i, 0)),
                  pl.BlockSpec((D_SSD, tn), lambda i, j: (0, j)),
                  pl.BlockSpec((D_ATT, tn), lambda i, j: (D_SSD // D_ATT, j))],
        out_specs=pl.BlockSpec((tm, tn), lambda i, j: (i, j)),
        out_shape=jax.ShapeDtypeStruct((t, D_MODEL), F32),
        compiler_params=_cp("parallel", "parallel"))(ymix, amix, w_out, w_out)


def _post_loss(out, x, target, norm_post_w):
    t = out.shape[0]
    nc = t // CHUNK
    cps = _attn_chunks_per_step(nc)
    rows = cps * CHUNK

    def body(o_ref, *refs):
        x_refs, tg_refs = refs[:cps], refs[cps:2 * cps]
        nw_ref, dout_ref, dy_ref, loss_ref, gnw_ref = refs[2 * cps:]
        i = pl.program_id(0)

        @pl.when(i == 0)
        def _():
            loss_ref[...] = jnp.zeros_like(loss_ref)
            gnw_ref[...] = jnp.zeros_like(gnw_ref)

        nw = nw_ref[...]
        loss = jnp.zeros((), F32)
        gnw = jnp.zeros((1, D_MODEL), F32)
        for k in range(cps):
            rk = slice(CHUNK * k, CHUNK * (k + 1))
            frames = i * cps + k > 0
            o = o_ref[rk, :]
            rstd = lax.rsqrt(jnp.mean(o * o, axis=-1, keepdims=True) + EPS)
            n = o * rstd
            err = jnp.where(frames, x_refs[k][...] + n * nw - tg_refs[k][...], 0.0)
            loss = loss + jnp.sum(err * err)
            dy = err * (1.0 / D_MODEL)
            dy_ref[rk, :] = dy
            gnw = gnw + jnp.sum(dy * n, axis=0, keepdims=True)
            dn = dy * nw
            dout_ref[rk, :] = _bf(rstd * (dn - n * jnp.mean(dn * n, axis=-1, keepdims=True)))
        loss_ref[...] += loss * (0.5 / D_MODEL)
        gnw_ref[...] += gnw

    lower = [pl.BlockSpec((CHUNK, D_MODEL), functools.partial(lambda i, k: (jnp.maximum(i * cps + k - 1, 0), 0), k=k))
             for k in range(cps)]
    return _pallas(
        body, name="post_loss", grid=(nc // cps,),
        in_specs=[pl.BlockSpec((rows, D_MODEL), lambda i: (i, 0))] + lower + lower
        + [pl.BlockSpec((1, D_MODEL), lambda i: (0, 0))],
        out_specs=[pl.BlockSpec((rows, D_MODEL), lambda i: (i, 0)), pl.BlockSpec((rows, D_MODEL), lambda i: (i, 0)),
                   pl.BlockSpec((8, LANES), lambda i: (0, 0)), pl.BlockSpec((1, D_MODEL), lambda i: (0, 0))],
        out_shape=[jax.ShapeDtypeStruct((t, D_MODEL), BF16), jax.ShapeDtypeStruct((t, D_MODEL), F32),
                   jax.ShapeDtypeStruct((8, LANES), F32), jax.ShapeDtypeStruct((1, D_MODEL), F32)],
        compiler_params=_cp("arbitrary"))(out, *([x] * cps), *([target] * cps), norm_post_w)


def _carried(grid, carry):
    if carry is None:
        return [], [], [], [], lambda refs: None, lambda refs: None
    hn = carry.shape[1] // 2

    def at(ids, which):
        cond = None
        for d, size in enumerate(grid):
            here = pl.program_id(d) == (0 if which == "first" else size - 1)
            cond = here if cond is None else cond & here
        return cond

    def start(refs):
        @pl.when(at(grid, "first"))
        def _():
            for cp in _pair_copies(*refs):
                cp.start()

    def finish(refs):
        @pl.when(at(grid, "last"))
        def _():
            for cp in _pair_copies(*refs):
                cp.wait()

    return ([ANY], [ANY], [jax.ShapeDtypeStruct((carry.shape[0], hn), F32)],
            [pltpu.SemaphoreType.DMA((PAIR_CHUNKS,)), pltpu.SemaphoreType.DMA((PAIR_CHUNKS,))], start, finish)


def _nt_matmul(a, b, name, carry=None):
    t, k = a.shape
    n = b.shape[0]
    tm, tn = _tile(t, 1040), 1024
    grid = (t // tm, n // tn)
    cin, cout, cshape, cscratch, start, finish = _carried(grid, carry)

    def body(a_ref, b_ref, *refs):
        o_ref = refs[len(cin)]
        comm = (refs[0], refs[2], refs[3], refs[4]) if carry is not None else None
        start(comm)
        o_ref[...] = _nt(a_ref[...], b_ref[...])
        finish(comm)

    res = _pallas(
        body, name=name, grid=grid,
        in_specs=[pl.BlockSpec((tm, k), lambda i, j: (i, 0)), pl.BlockSpec((tn, k), lambda i, j: (j, 0))] + cin,
        out_specs=[pl.BlockSpec((tm, tn), lambda i, j: (i, j))] + cout,
        out_shape=[jax.ShapeDtypeStruct((t, n), F32)] + cshape, scratch_shapes=cscratch,
        compiler_params=_cp("arbitrary", "arbitrary"))(a, b, *([carry] if carry is not None else []))
    return res if carry is not None else res[0]


def _tn_matmul(a, b, name, carry=None):
    t, m = a.shape
    n = b.shape[1]
    tk, tm, tn = _tile(t, 832), min(m, 2048), min(n, 2048)
    nk = t // tk
    grid = (m // tm, n // tn, nk)
    cin, cout, cshape, cscratch, start, finish = _carried(grid, carry)

    def body(a_ref, b_ref, *refs):
        o_ref = refs[len(cin)]
        comm = (refs[0], refs[2], refs[3], refs[4]) if carry is not None else None
        start(comm)

        @pl.when(pl.program_id(2) == 0)
        def _():
            o_ref[...] = jnp.zeros_like(o_ref)
        o_ref[...] += _tn(a_ref[...], b_ref[...])
        finish(comm)

    res = _pallas(
        body, name=name, grid=grid,
        in_specs=[pl.BlockSpec((tk, tm), lambda i, j, k: (k, i)), pl.BlockSpec((tk, tn), lambda i, j, k: (k, j))] + cin,
        out_specs=[pl.BlockSpec((tm, tn), lambda i, j, k: (i, j))] + cout,
        out_shape=[jax.ShapeDtypeStruct((m, n), F32)] + cshape, scratch_shapes=cscratch,
        compiler_params=_cp("arbitrary", "arbitrary", "arbitrary"))(a, b, *([carry] if carry is not None else []))
    return res if carry is not None else res[0]


def _attn_bwd(qr, kr, proj, dmix, sinks, ga):
    t = qr.shape[0]
    nc = t // CHUNK
    cps = _attn_chunks_per_step(nc)
    nsteps = nc // cps
    rows_step = cps * CHUNK

    def body(q_ref, k_ref, v_ref, g_ref, da_ref, sink_ref, ga_ref, dq_ref, dg_ref, dk_ref, dv_ref, gs_ref,
             got_ref, send_sems, recv_sems):
        step = pl.program_id(0)

        @pl.when(step == 0)
        def _():
            for cp in _exchange_copies(ga_ref, got_ref, send_sems, recv_sems):
                cp.start()
            dk_ref[...] = jnp.zeros_like(dk_ref)
            dv_ref[...] = jnp.zeros_like(dv_ref)
            gs_ref[...] = jnp.zeros_like(gs_ref)

        lane = lax.broadcasted_iota(jnp.int32, (1, LANES), 1)
        rows = lax.broadcasted_iota(jnp.int32, (REP * CHUNK, 1), 0) >> 6
        gs = jnp.zeros((1, LANES), F32)
        dk_parts = [[] for _ in range(cps + BAND_CHUNKS - 1)]
        dv_parts = [[] for _ in range(cps + BAND_CHUNKS - 1)]
        for j in range(cps):
            c = step * cps + j
            rj = slice(CHUNK * j, CHUNK * (j + 1))
            ks, vs = _band(k_ref, c), _band(v_ref, c)
            q = q_ref[rj, :]
            gate = g_ref[rj, :]
            sg = _sigmoid(gate)
            da = da_ref[rj, :]
            datt = da * (gate * sg)
            dqs, atts, dks, dvs = [], [], [], []
            for h in range(KV_HEADS):
                qh, kb, vb, sink_col, valid = _attn_operands(c, q, ks, vs, sink_ref, h)
                p, psink, inv = _attn_probs(qh, kb, sink_col, valid)
                pb = _bf(p)
                o = _mm(pb, vb) * inv
                do = jnp.concatenate([datt[:, HEAD_DIM * (REP * h + r):HEAD_DIM * (REP * h + r + 1)]
                                      for r in range(REP)], axis=0)
                dob = _bf(do * inv)
                delta = jnp.sum(do * o, axis=1, keepdims=True) * inv
                ds = _bf(p * (_nt(dob, vb) - delta))
                gsink = -psink * delta
                for r in range(REP):
                    gs = gs + jnp.where(lane == REP * h + r, jnp.sum(jnp.where(rows == r, gsink, 0.0)), 0.0)
                dqh = _mm(ds, kb)
                dqs += [dqh[CHUNK * r:CHUNK * (r + 1)] for r in range(REP)]
                atts += [o[CHUNK * r:CHUNK * (r + 1)] for r in range(REP)]
                dks.append(_tn(ds, qh))
                dvs.append(_tn(pb, dob))
            dq_ref[rj, :] = jnp.concatenate(dqs, axis=1)
            att = jnp.concatenate(atts, axis=1)
            dg_ref[rj, :] = _bf(da * att * (sg * (1.0 + gate * (1.0 - sg))))
            dkf = jnp.concatenate(dks, axis=1)
            dvf = jnp.concatenate(dvs, axis=1)
            for b in range(BAND_CHUNKS):
                dk_parts[j + b].append(dkf[CHUNK * b:CHUNK * (b + 1)])
                dv_parts[j + b].append(dvf[CHUNK * b:CHUNK * (b + 1)])
        gs_ref[0:1, :] += gs
        for rel in range(cps + BAND_CHUNKS - 1):
            r0 = pl.multiple_of(jnp.maximum(step * cps - (BAND_CHUNKS - 1) + rel, 0) * CHUNK, CHUNK)
            dk_ref[pl.ds(r0, CHUNK), :] += sum(dk_parts[rel][1:], dk_parts[rel][0])
            dv_ref[pl.ds(r0, CHUNK), :] += sum(dv_parts[rel][1:], dv_parts[rel][0])

        @pl.when(step == nsteps - 1)
        def _():
            for cp in _exchange_copies(ga_ref, got_ref, send_sems, recv_sems):
                cp.wait()

    return _pallas(
        body, name="attn_bwd", grid=(nsteps,),
        in_specs=[pl.BlockSpec((rows_step, D_ATT), lambda i: (i, 0)), pl.BlockSpec((t, D_KV), lambda i: (0, 0)),
                  pl.BlockSpec((t, D_KV), lambda i: (0, OV // D_KV)),
                  pl.BlockSpec((rows_step, D_ATT), lambda i: (i, OG // D_ATT)),
                  pl.BlockSpec((rows_step, D_ATT), lambda i: (i, D_SSD // D_ATT)),
                  pl.BlockSpec(memory_space=pltpu.SMEM), ANY],
        out_specs=[pl.BlockSpec((rows_step, D_ATT), lambda i: (i, 0)), pl.BlockSpec((rows_step, D_ATT), lambda i: (i, 0)),
                   pl.BlockSpec((t, D_KV), lambda i: (0, 0)), pl.BlockSpec((t, D_KV), lambda i: (0, 0)),
                   pl.BlockSpec((8, LANES), lambda i: (0, 0)), ANY],
        out_shape=[jax.ShapeDtypeStruct((t, D_ATT), F32), jax.ShapeDtypeStruct((t, D_ATT), BF16),
                   jax.ShapeDtypeStruct((t, D_KV), F32), jax.ShapeDtypeStruct((t, D_KV), F32),
                   jax.ShapeDtypeStruct((8, LANES), F32), _exchange_shape(ga)],
        scratch_shapes=_exchange_scratch(),
        compiler_params=_cp("arbitrary"))(qr, kr, proj, proj, dmix, sinks, ga)


def _ssd_bwd(dmix, y_ssd, xbc, proj, dt, acs, acst, states, d_skip_l, ssd_norm_w):
    t = xbc.shape[0]
    q = CHUNK
    nc = t // q
    gps = SSD_BWD_GROUPS_PER_STEP
    gw, sw = gps * GROUP_W, gps * D_STATE

    def body(dmix_ref, y_ref, z_ref, nw_ref, xs_ref, b_ref, c_ref, dt_ref, acs_ref, acst_ref, st_ref, dsk_ref,
             dz_ref, dxs_ref, db_ref, dc_ref, dacs_ref, ddt_ref, gnw_ref, gdsk_ref, dstate):
        @pl.when(pl.program_id(1) == 0)
        def _():
            dstate[...] = jnp.zeros_like(dstate)
            gnw_ref[...] = jnp.zeros_like(gnw_ref)
            gdsk_ref[...] = jnp.zeros_like(gdsk_ref)

        last_row = lax.broadcasted_iota(jnp.int32, (q, 1), 0) == q - 1
        lane = lax.broadcasted_iota(jnp.int32, (q, LANES), 1)
        lane1 = lax.broadcasted_iota(jnp.int32, (8, LANES), 1)
        lower, upper = _head_tri(q, True), _head_tri(q, False)
        bd_mask = _block_diag_mask()
        for gi in range(gps):
            g = gps * pl.program_id(0) + gi
            cols = slice(GROUP_W * gi, GROUP_W * (gi + 1))
            scols = slice(D_STATE * gi, D_STATE * (gi + 1))
            y = y_ref[:, cols]
            z = z_ref[:, cols]
            sz = _sigmoid(z)
            silu_z = z * sz
            yg = y * silu_z
            rstd = lax.rsqrt(jnp.mean(yg * yg, axis=-1, keepdims=True) + EPS)
            n = yg * rstd
            dout = dmix_ref[:, cols]
            gnw_ref[:, cols] += jnp.sum(dout * n, axis=0, keepdims=True)
            dn = dout * nw_ref[:, cols]
            dyg = rstd * (dn - n * jnp.mean(dn * n, axis=-1, keepdims=True))
            dy = dyg * silu_z
            dz_ref[:, cols] = _bf(dyg * y * (sz * (1.0 + z * (1.0 - sz))))

            x = xs_ref[:, cols]
            bmb, cmb = _bf(b_ref[:, scols]), _bf(c_ref[:, scols])
            hv = _group_heads(g, gi, dt_ref[...], acs_ref[...], acst_ref, dsk_ref[...])
            dec = jnp.exp(jnp.where(lower, hv.acs - hv.acs_row, NEG))
            dect = jnp.exp(jnp.where(upper, hv.acs_row - hv.acs, NEG))
            b4 = jnp.concatenate([bmb] * HPG, axis=0)
            c4 = jnp.concatenate([cmb] * HPG, axis=0)
            m_all = _nt(cmb, b4) * dec
            mt_all = _nt(bmb, c4) * dect
            xdt = x * hv.dt
            xdt_b, dyb = _bf(xdt), _bf(dy)
            x_bd, dy_bd = _block_diag(xdt_b, bd_mask), _block_diag(dyb, bd_mask)
            s_prev = st_ref[0, gi]
            spb = _bf(s_prev)
            ds_new = dstate[gi]
            dsb = _bf(ds_new)
            e = jnp.exp(hv.acs)
            elast = jnp.exp(hv.acs_last)
            dte = jnp.exp(hv.acs_last - hv.acs)
            bds = _mm(bmb, dsb)
            dxdt = _mm(_bf(mt_all), dy_bd) + bds * dte
            dm = _nt(dyb, x_bd)
            dmt = _nt(xdt_b, dy_bd)
            dye = _bf(dy * e)
            dc_ref[:, scols] = _mm(_bf(dm * dec), b4) + _nt(dye, spb)
            db_ref[:, scols] = _mm(_bf(dmt * dect), c4) + _nt(_bf(xdt * dte), dsb)
            dstate[gi] = elast * ds_new + _tn(cmb, dye)
            dxs_ref[:, cols] = dxdt * hv.dt + hv.dsk * dy
            ddte_dte = bds * xdt * dte
            dacs_l = dm * m_all - dmt * mt_all + dy * _mm(cmb, spb) * e - ddte_dte
            dlast_l = (jnp.sum(ddte_dte, axis=0, keepdims=True)
                       + jnp.sum(s_prev * ds_new, axis=0, keepdims=True) * elast)
            ddt_l = dxdt * x
            gdsk_l = jnp.sum(dy * x, axis=0, keepdims=True)
            dacs_out = jnp.zeros((q, LANES), F32)
            ddt_out = jnp.zeros((q, LANES), F32)
            gdsk = jnp.zeros((8, LANES), F32)
            for r in range(HPG):
                dacs = _head_sums(dacs_l, r) + jnp.where(last_row, _head_sums(dlast_l, r), 0.0)
                dacs_out = jnp.where(lane == r, dacs, dacs_out)
                ddt_out = jnp.where(lane == r, _head_sums(ddt_l, r), ddt_out)
                gdsk = gdsk + jnp.where(lane1 == r, _head_sums(gdsk_l, r), 0.0)
            dacs_ref[:, LANES * gi:LANES * (gi + 1)] = dacs_out
            ddt_ref[:, LANES * gi:LANES * (gi + 1)] = ddt_out
            gdsk_ref[gi] += gdsk

    rev = lambda c: nc - 1 - c
    wide = pl.BlockSpec((q, gw), lambda g, c: (rev(c), g))
    return _pallas(
        body, name="ssd_bwd", grid=(GROUPS // gps, nc),
        in_specs=[wide, wide, wide, pl.BlockSpec((1, gw), lambda g, c: (0, g)), wide,
                  pl.BlockSpec((q, sw), lambda g, c: (rev(c), D_SSD // sw + g)),
                  pl.BlockSpec((q, sw), lambda g, c: (rev(c), (D_SSD + GROUPS * D_STATE) // sw + g)),
                  pl.BlockSpec((q, LANES), lambda g, c: (rev(c), 0)), pl.BlockSpec((q, LANES), lambda g, c: (rev(c), 0)),
                  pl.BlockSpec((1, gps * GROUPS, q), lambda g, c: (rev(c), g, 0)),
                  pl.BlockSpec((1, gps, D_STATE, GROUP_W), lambda g, c: (rev(c), g, 0, 0)),
                  pl.BlockSpec((1, LANES), lambda g, c: (0, 0))],
        out_specs=[wide, wide,
                   pl.BlockSpec((q, sw), lambda g, c: (rev(c), g)), pl.BlockSpec((q, sw), lambda g, c: (rev(c), g)),
                   pl.BlockSpec((q, gps * LANES), lambda g, c: (rev(c), g)),
                   pl.BlockSpec((q, gps * LANES), lambda g, c: (rev(c), g)),
                   pl.BlockSpec((1, gw), lambda g, c: (0, g)), pl.BlockSpec((gps, 8, LANES), lambda g, c: (g, 0, 0))],
        out_shape=[jax.ShapeDtypeStruct((t, D_SSD), BF16), jax.ShapeDtypeStruct((t, D_SSD), F32),
                   jax.ShapeDtypeStruct((t, GROUPS * D_STATE), F32), jax.ShapeDtypeStruct((t, GROUPS * D_STATE), F32),
                   jax.ShapeDtypeStruct((t, GROUPS * LANES), F32), jax.ShapeDtypeStruct((t, GROUPS * LANES), F32),
                   jax.ShapeDtypeStruct((1, D_SSD), F32), jax.ShapeDtypeStruct((GROUPS, 8, LANES), F32)],
        scratch_shapes=[pltpu.VMEM((gps, D_STATE, GROUP_W), F32)],
        compiler_params=_cp("parallel", "arbitrary"))(dmix, y_ssd, proj, ssd_norm_w, xbc, xbc, xbc, dt, acs, acst,
                                                      states, d_skip_l)


def _dt_bwd(dacs_g, ddt_g, dt, proj, dt_bias_l, a_log_l):
    t = dt.shape[0]
    q = CHUNK
    nc = t // q
    cps = _chunks_per_step(nc)
    rows = cps * q

    def body(dacs_ref, ddt_ref, dt_ref, raw_ref, bias_ref, alog_ref, draw_ref, ga_ref, gb_ref):
        @pl.when(pl.program_id(0) == 0)
        def _():
            ga_ref[...] = jnp.zeros_like(ga_ref)
            gb_ref[...] = jnp.zeros_like(gb_ref)

        lane = lax.broadcasted_iota(jnp.int32, (q, LANES), 1)
        ri = lax.broadcasted_iota(jnp.int32, (q, q), 0)
        ci = lax.broadcasted_iota(jnp.int32, (q, q), 1)
        triu = (ri <= ci).astype(F32)
        a = -jnp.exp(alog_ref[...])
        used = (lane & (GROUPS - 1)) < HPG
        ga = jnp.zeros((1, LANES), F32)
        gb = jnp.zeros((1, LANES), F32)
        for k in range(cps):
            rk = slice(q * k, q * (k + 1))
            dacs = jnp.zeros((q, LANES), F32)
            ddt = jnp.zeros((q, LANES), F32)
            for g in range(GROUPS):
                mask = (lane >= GROUPS * g) & (lane < GROUPS * g + HPG)
                sl = slice(LANES * g, LANES * (g + 1))
                if g == 0:
                    dacs = jnp.where(mask, dacs_ref[rk, sl], dacs)
                    ddt = jnp.where(mask, ddt_ref[rk, sl], ddt)
                else:
                    dacs = jnp.where(mask, pltpu.roll(dacs_ref[rk, sl], GROUPS * g, 1), dacs)
                    ddt = jnp.where(mask, pltpu.roll(ddt_ref[rk, sl], GROUPS * g, 1), ddt)
            dda = jnp.dot(triu, dacs, preferred_element_type=F32, precision=HI)
            row = pl.program_id(0) * rows + q * k + lax.broadcasted_iota(jnp.int32, (q, LANES), 0)
            dsp = jnp.where((row >= PAD_LEAD) & used, dda * a + ddt, 0.0)
            draw = dsp * _sigmoid(raw_ref[rk, :] + bias_ref[...])
            draw_ref[rk, :] = _bf(draw)
            gb = gb + jnp.sum(draw, axis=0, keepdims=True)
            ga = ga + jnp.sum(jnp.where(used, dda * dt_ref[rk, :], 0.0), axis=0, keepdims=True)
        gb_ref[0:1, :] += gb
        ga_ref[0:1, :] += ga * a

    return _pallas(
        body, name="dt_bwd", grid=(nc // cps,),
        in_specs=[pl.BlockSpec((rows, GROUPS * LANES), lambda c: (c, 0)),
                  pl.BlockSpec((rows, GROUPS * LANES), lambda c: (c, 0)),
                  pl.BlockSpec((rows, LANES), lambda c: (c, 0)), pl.BlockSpec((rows, LANES), lambda c: (c, ODT // LANES)),
                  pl.BlockSpec((1, LANES), lambda c: (0, 0)), pl.BlockSpec((1, LANES), lambda c: (0, 0))],
        out_specs=[pl.BlockSpec((rows, LANES), lambda c: (c, 0)), pl.BlockSpec((8, LANES), lambda c: (0, 0)),
                   pl.BlockSpec((8, LANES), lambda c: (0, 0))],
        out_shape=[jax.ShapeDtypeStruct((t, LANES), BF16), jax.ShapeDtypeStruct((8, LANES), F32),
                   jax.ShapeDtypeStruct((8, LANES), F32)],
        compiler_params=_cp("arbitrary"))(dacs_g, ddt_g, dt, proj, dt_bias_l, a_log_l)


def _conv_bwd(dseg, proj, conv_w, conv_b, col_off, name):
    t, width = dseg.shape
    tc = 128
    rt = _tile(t, 320)
    off_p = (OXS + col_off) // tc
    off_w = col_off // tc

    def body(d_ref, x_ref, w_ref, b_ref, dx_ref, gw_ref, gb_ref, xp, dup):
        xp[0:8, :] = jnp.zeros((8, tc), F32)
        xp[8:t + 8, :] = x_ref[...]
        dup[t:t + 8, :] = jnp.zeros((8, tc), F32)
        w = w_ref[...]
        bias = b_ref[...]

        def first(i, acc):
            r0 = pl.multiple_of(i * rt, 8)
            xs = [xp[pl.ds(r0 + 5 + k, rt), :] for k in range(CONV_WIDTH)]
            u = bias + w[3:4, :] * xs[3] + w[2:3, :] * xs[2] + w[1:2, :] * xs[1] + w[0:1, :] * xs[0]
            su = 0.5 + 0.5 * jnp.tanh(0.5 * u)
            du = d_ref[pl.ds(r0, rt), :] * (su * (1.0 + u * (1.0 - su)))
            dup[pl.ds(r0, rt), :] = du
            return tuple(acc[k] + jnp.sum(du * xs[k], axis=0, keepdims=True) for k in range(CONV_WIDTH)) + (
                acc[CONV_WIDTH] + jnp.sum(du, axis=0, keepdims=True),)

        zero = jnp.zeros((1, tc), F32)
        acc = lax.fori_loop(0, t // rt, first, (zero,) * (CONV_WIDTH + 1))
        gw_ref[...] = jnp.concatenate(acc[:CONV_WIDTH], axis=0)
        gb_ref[...] = acc[CONV_WIDTH]

        def second(i, carry):
            r0 = pl.multiple_of(i * rt, 16)
            dx_ref[pl.ds(r0, rt), :] = _bf(w[3:4, :] * dup[pl.ds(r0, rt), :] + w[2:3, :] * dup[pl.ds(r0 + 1, rt), :]
                                          + w[1:2, :] * dup[pl.ds(r0 + 2, rt), :] + w[0:1, :] * dup[pl.ds(r0 + 3, rt), :])
            return carry

        lax.fori_loop(0, t // rt, second, 0)

    return _pallas(
        body, name=name, grid=(width // tc,),
        in_specs=[pl.BlockSpec((t, tc), lambda j: (0, j)), pl.BlockSpec((t, tc), lambda j: (0, j + off_p)),
                  pl.BlockSpec((CONV_WIDTH, tc), lambda j: (0, j + off_w)), pl.BlockSpec((1, tc), lambda j: (0, j + off_w))],
        out_specs=[pl.BlockSpec((t, tc), lambda j: (0, j)), pl.BlockSpec((CONV_WIDTH, tc), lambda j: (0, j)),
                   pl.BlockSpec((1, tc), lambda j: (0, j))],
        out_shape=[jax.ShapeDtypeStruct((t, width), BF16), jax.ShapeDtypeStruct((CONV_WIDTH, width), F32),
                   jax.ShapeDtypeStruct((1, width), F32)],
        scratch_shapes=[pltpu.VMEM((t + 8, tc), F32), pltpu.VMEM((t + 8, tc), F32)],
        compiler_params=_cp("parallel"))(dseg, proj, conv_w, conv_b)


def _dinproj(segs, w_re, hpad, norm_w, dy_t, ga):
    t = segs[0].shape[0]
    d = hpad.shape[1]
    tm, tk = _tile(t, 416), SEG_TILE
    counts = [s.shape[1] // tk for s in segs]
    firsts = [sum(counts[:s]) for s in range(len(segs))]
    nk = sum(counts)
    assert nk * tk == w_re.shape[1]
    ni = t // tm
    ns = len(segs)

    def body(*refs):
        seg_refs = refs[:ns]
        w_ref, h_ref, nw_ref, dy_ref, ga_ref, dh_ref, gnw_ref, got_ref, acc, send_sems, recv_sems = refs[ns:]
        i, k = pl.program_id(0), pl.program_id(1)

        @pl.when((i == 0) & (k == 0))
        def _():
            for cp in _exchange_copies(ga_ref, got_ref, send_sems, recv_sems):
                cp.start()
            gnw_ref[...] = jnp.zeros_like(gnw_ref)

        @pl.when(k == 0)
        def _():
            acc[...] = jnp.zeros_like(acc)

        for s in range(ns):
            @pl.when((k >= firsts[s]) & (k < firsts[s] + counts[s]))
            def _(s=s):
                acc[...] += _nt(seg_refs[s][...], w_ref[...])

        @pl.when(k == nk - 1)
        def _():
            h = h_ref[...]
            rstd = lax.rsqrt(jnp.mean(h * h, axis=-1, keepdims=True) + EPS)
            nrm = h * rstd
            dhn = acc[...]
            gnw_ref[...] += jnp.sum(dhn * nrm, axis=0, keepdims=True)
            dn = dhn * nw_ref[...]
            dh_ref[...] = rstd * (dn - nrm * jnp.mean(dn * nrm, axis=-1, keepdims=True)) + dy_ref[...]

        @pl.when((i == ni - 1) & (k == nk - 1))
        def _():
            for cp in _exchange_copies(ga_ref, got_ref, send_sems, recv_sems):
                cp.wait()

    seg_specs = [pl.BlockSpec((tm, tk), functools.partial(lambda i, k, f0, n0: (i, jnp.clip(k - f0, 0, n0 - 1)),
                                                          f0=firsts[s], n0=counts[s])) for s in range(ns)]
    return _pallas(
        body, name="dinproj", grid=(ni, nk),
        in_specs=seg_specs + [pl.BlockSpec((d, tk), lambda i, k: (0, k)),
                              pl.BlockSpec((tm, d), lambda i, k: (i, 0)), pl.BlockSpec((1, d), lambda i, k: (0, 0)),
                              pl.BlockSpec((tm, d), lambda i, k: (i, 0)), ANY],
        out_specs=[pl.BlockSpec((tm, d), lambda i, k: (i, 0)), pl.BlockSpec((1, d), lambda i, k: (0, 0)), ANY],
        out_shape=[jax.ShapeDtypeStruct((t, d), F32), jax.ShapeDtypeStruct((1, d), F32), _exchange_shape(ga)],
        scratch_shapes=[pltpu.VMEM((tm, d), F32)] + _exchange_scratch(),
        compiler_params=_cp("arbitrary", "arbitrary"))(*segs, w_re, hpad, norm_w, dy_t, ga)


def _spread_heads(v):
    v = jnp.pad(v.reshape(GROUPS, HPG), ((0, 0), (0, GROUPS - HPG))).reshape(1, GROUPS * GROUPS)
    return jnp.pad(v, ((0, 0), (0, LANES - GROUPS * GROUPS)))


def _gather_heads(v):
    return v[0:1, :GROUPS * GROUPS].reshape(GROUPS, GROUPS)[:, :HPG].reshape(1, SSD_HEADS)


def _rope_tables(t):
    half = HEAD_DIM // 2
    inv = ROPE_THETA ** (-jnp.arange(half, dtype=F32) / half)
    pos = (jnp.arange(t) - PAD_LEAD).astype(F32)
    ang = pos[:, None] * inv[None, :]
    cos, sin = jnp.cos(ang), jnp.sin(ang)
    cos_t = jnp.concatenate([cos, cos, cos, cos], axis=1)
    sin_t = jnp.concatenate([-sin, sin, -sin, sin], axis=1)
    return cos_t, sin_t


def _column_pieces():
    runs = [(0, OB + 2 * GROUPS * D_STATE, 0)]
    o = OB + 2 * GROUPS * D_STATE
    runs += [(o + HPG * g, HPG, ODT + GROUPS * g) for g in range(GROUPS)]
    o += SSD_HEADS
    for width, dst in ((D_ATT, OQ), (D_KV, OK), (D_KV, OV), (D_ATT, OG)):
        runs.append((o, width, dst))
        o += width
    assert o == D_IN
    pieces = []
    for o0, width, dst in runs:
        for j in range(N_SHARD):
            lo, hi = max(o0, W_IN_SHARD * j), min(o0 + width, W_IN_SHARD * (j + 1))
            if lo < hi:
                pieces.append((j, lo - W_IN_SHARD * j, hi - W_IN_SHARD * j, dst + lo - o0))
    return pieces


def _shards_to_re(w_all):
    _, k, _ = w_all.shape
    tr = 256

    def body(x_ref, o_ref):
        o_ref[:, ODT:ODT + DT_SLAB] = jnp.zeros((tr, DT_SLAB), o_ref.dtype)
        for j, c0, c1, d0 in _column_pieces():
            o_ref[:, d0:d0 + c1 - c0] = x_ref[j, :, c0:c1]

    return _pallas(body, name="shards_to_re", grid=(k // tr,),
                   in_specs=[pl.BlockSpec((N_SHARD, tr, W_IN_SHARD), lambda i: (0, i, 0))],
                   out_specs=pl.BlockSpec((tr, N_RE), lambda i: (i, 0)),
                   out_shape=jax.ShapeDtypeStruct((k, N_RE), w_all.dtype), compiler_params=_cp("parallel"))(w_all)


def _pair_add_to_shards(parts, got, pieces, shard_rows, core, name):
    n = parts[0].shape[1]
    hn = n // 2
    tc = 128
    nt = hn // tc
    ns = len(parts)
    starts = [sum(p.shape[0] for p in parts[:s]) for s in range(ns)]
    moves = []
    for j, c0, c1, d0 in pieces:
        for s, p in enumerate(parts):
            lo, hi = max(d0, starts[s]), min(d0 + c1 - c0, starts[s] + p.shape[0])
            if lo < hi:
                moves.append((s, lo - starts[s], j, c0 + lo - d0, hi - lo))
    assert sum(m[4] for m in moves) == N_SHARD * shard_rows

    def body(core_ref, *refs):
        own, theirs, o_ref, acc = refs[:ns], refs[ns:2 * ns], refs[2 * ns], refs[2 * ns + 1]
        for s, r0, j, c0, rows in moves:
            acc[j, c0:c0 + rows, :] = own[s][r0:r0 + rows, :] + theirs[s][r0:r0 + rows, :]
        o_ref[...] = _bf(acc[...])

    return _pallas(
        body, name=name,
        grid_spec=pltpu.PrefetchScalarGridSpec(
            num_scalar_prefetch=1, grid=(nt,),
            in_specs=[pl.BlockSpec((p.shape[0], tc), lambda i, core_ref: (0, core_ref[0] * nt + i)) for p in parts]
            + [pl.BlockSpec((p.shape[0], tc), lambda i, core_ref: (0, i)) for p in parts],
            out_specs=pl.BlockSpec((N_SHARD, shard_rows, tc), lambda i, core_ref: (0, 0, i)),
            scratch_shapes=[pltpu.VMEM((N_SHARD, shard_rows, tc), F32)]),
        out_shape=jax.ShapeDtypeStruct((N_SHARD, shard_rows, hn), BF16),
        compiler_params=_cp("parallel"))(core, *parts, *got)


def _local_step(x, target, meta, norm_pre_w, w_re, conv_w, conv_b, dt_bias, a_log, d_skip, ssd_norm_w, sinks,
                w_out_shard, norm_post_w, place):
    seq = x.shape[0]
    t = PAD_LEAD + N_META + seq
    hpad = jnp.concatenate([jnp.zeros((PAD_LEAD, D_MODEL), F32), meta, x], axis=0)
    dt_bias_l, a_log_l, d_skip_l = _spread_heads(dt_bias), _spread_heads(a_log), _spread_heads(d_skip)
    cos_t, sin_t = _rope_tables(t)
    sink_v = sinks.reshape(Q_HEADS)

    proj, hn, w_out_all = _inproj(hpad, norm_pre_w, w_re, w_out_shard)
    w_out = w_out_all.reshape(D_MIX, D_MODEL)
    xbc = _conv_fwd(proj, conv_w, conv_b)
    dt, acs, acst = _dt_prep(proj, dt_bias_l, a_log_l)
    y_ssd, ymix, states = _ssd_fwd(xbc, proj, dt, acs, acst, d_skip_l, ssd_norm_w)
    qr, kr = _rope(proj, OQ, proj, OK, cos_t, sin_t)
    amix = _attn_fwd(qr, kr, proj, sink_v)
    out = _outproj(ymix, amix, w_out)
    dout, dy_t, loss_blk, g_norm_post = _post_loss(out, x, target, norm_post_w)

    g_out_y = _tn_matmul(ymix, dout, "gw_out_y")
    g_out_a, got_y = _tn_matmul(amix, dout, "gw_out_a", carry=g_out_y)
    dmix, got_a = _nt_matmul(dout, w_out, "dmix", carry=g_out_a)
    ga_out = _reduce_pair([g_out_y, g_out_a], [got_y, got_a], [(j, 0, W_OUT_SHARD, W_OUT_SHARD * j) for j in range(N_SHARD)],
                          W_OUT_SHARD, place, "gw_out")
    dq_r, dg, dk_r, dv, gs, slabs_out = _attn_bwd(qr, kr, proj, dmix, sink_v, ga_out)
    g_w_out = _reduce_finish(ga_out, slabs_out, place, "gw_out")
    dq, dk = _rope(dq_r, 0, dk_r, 0, cos_t, -sin_t)
    dz, dxs, db, dc, dacs_g, ddt_g, g_ssd_norm, gdsk = _ssd_bwd(dmix, y_ssd, xbc, proj, dt, acs, acst, states,
                                                                d_skip_l, ssd_norm_w)
    draw, ga, gb = _dt_bwd(dacs_g, ddt_g, dt, proj, dt_bias_l, a_log_l)
    dxs_p, gcw0, gcb0 = _conv_bwd(dxs, proj, conv_w, conv_b, 0, "conv_bwd_x")
    db_p, gcw1, gcb1 = _conv_bwd(db, proj, conv_w, conv_b, D_SSD, "conv_bwd_b")
    dc_p, gcw2, gcb2 = _conv_bwd(dc, proj, conv_w, conv_b, D_SSD + GROUPS * D_STATE, "conv_bwd_c")
    tail = jnp.concatenate([dk, _bf(dv), draw, jnp.zeros((t, DT_SLAB - LANES), BF16)], axis=1)
    segs = [dz, dxs_p, db_p, dc_p, dq, dg, tail]
    g_parts, got_parts = [_tn_matmul(segs[0], hn, "gw_in_0")], []
    for s in range(1, len(segs)):
        part, got = _tn_matmul(segs[s], hn, "gw_in_%d" % s, carry=g_parts[-1])
        g_parts.append(part)
        got_parts.append(got)
    ga_in = _reduce_pair(g_parts, got_parts, _column_pieces(), W_IN_SHARD, place, "gw_in")
    dh, g_norm_pre, slabs_in = _dinproj(segs, w_re, hpad, norm_pre_w, dy_t, ga_in)
    g_w_in_half = _chip_sum(ga_in, slabs_in, place, "gw_in_chip_sum")

    gdsk_l = jnp.concatenate([gdsk[g, 0:1, 0:GROUPS] for g in range(GROUPS)], axis=1)
    gdsk_l = jnp.pad(gdsk_l, ((0, 0), (0, LANES - GROUPS * GROUPS)))
    grads = dict(
        meta_tokens=dh[PAD_LEAD:ROW0], norm_pre_w=g_norm_pre, w_in_half=g_w_in_half,
        conv_w=jnp.concatenate([gcw0, gcw1, gcw2], axis=1), conv_b=jnp.concatenate([gcb0, gcb1, gcb2], axis=1),
        dt_bias=_gather_heads(gb), a_log=_gather_heads(ga), d_skip=_gather_heads(gdsk_l), ssd_norm_w=g_ssd_norm,
        attn_sinks=gs[0:1, :Q_HEADS], w_out=g_w_out, norm_post_w=g_norm_post)
    return loss_blk[0, 0], dh[ROW0:], grads


ANY = pl.BlockSpec(memory_space=pl.ANY)
MESH = pl.DeviceIdType.MESH
GATHER_CHUNKS = 8
PAIR_CHUNKS = 8
JOIN_CHUNKS = 8


def _rcopy(src, dst, ssem, rsem, dev):
    return pltpu.make_async_remote_copy(src_ref=src, dst_ref=dst, send_sem=ssem, recv_sem=rsem, device_id=dev,
                                        device_id_type=MESH)


def _place():
    x, y, c = lax.axis_index("x"), lax.axis_index("y"), lax.axis_index("c")
    chips = [(1 - x, y), (x, 1 - y), (1 - x, 1 - y)]
    return x, y, c, chips


def _gather_plan(x_ref, out_ref, send_sems, recv_sems, local_sems, hr, kc):
    ch = hr // kc
    assert ch * kc == hr and ch % 16 == 0
    x, y, c, chips = _place()
    me = 2 * x + y
    sibling = (x, y, 1 - c)

    def piece(chip, hc, k):
        return out_ref.at[chip, pl.ds(hc * hr + k * ch, ch), :]

    def local():
        return [pltpu.make_async_copy(x_ref.at[pl.ds(k * ch, ch), :], out_ref.at[me, pl.ds(k * ch, ch), :],
                                      local_sems.at[k]) for k in range(2 * kc)]

    def first():
        return [_rcopy(x_ref.at[pl.ds(c * hr + k * ch, ch), :], piece(me, c, k), send_sems.at[j * kc + k],
                       recv_sems.at[j * kc + k], (*chip, c)) for j, chip in enumerate(chips) for k in range(kc)]

    def passed(hc):
        return [_rcopy(piece(2 * chip[0] + chip[1], hc, k), piece(2 * chip[0] + chip[1], hc, k),
                       send_sems.at[(3 + j) * kc + k], recv_sems.at[(3 + j) * kc + k], sibling)
                for j, chip in enumerate(chips) for k in range(kc)]

    def arrivals():
        return [_rcopy(piece(2 * chip[0] + chip[1], c, k), piece(2 * chip[0] + chip[1], c, k), send_sems.at[j * kc + k],
                       recv_sems.at[j * kc + k], (*chip, c)) for j, chip in enumerate(chips) for k in range(kc)]

    def start():
        for cp in local() + first():
            cp.start()

    def forward():
        for arrived in arrivals():
            arrived.wait_recv()
        for fw in passed(c):
            fw.start()

    def finish():
        for cp in passed(1 - c):
            cp.wait_recv()
        for cp in first() + passed(c):
            cp.wait_send()
        for cp in local():
            cp.wait()

    return start, forward, finish


def _gather_shards(shard, name, kc, chip, small):
    r, n = shard.shape
    hr = r // 2
    qr = hr // 2
    ch = qr // kc
    assert ch * kc == qr and ch % 16 == 0
    nflow = 12
    tr = 256

    def body(x_ref, p_ref, out_ref, slots_ref, send_sems, recv_sems, *small_sems):
        start_small, wait_small = _chip_small_exchange(p_ref, slots_ref, *small_sems)
        start_small()
        x, y, c, _ = _place()
        me, cxn, cyn, cdg = 2 * x + y, 2 * (1 - x) + y, 2 * x + 1 - y, 2 * (1 - x) + 1 - y
        xn, yn, sibling = (1 - x, y, c), (x, 1 - y, c), (x, y, 1 - c)

        def piece(chip, hc, part, k):
            return out_ref.at[chip, pl.ds(hc * hr + part * qr + k * ch, ch), :]

        def own(part, k):
            return x_ref.at[pl.ds(c * hr + part * qr + k * ch, ch), :]

        def sems(flow, k):
            return send_sems.at[flow * kc + k], recv_sems.at[flow * kc + k]

        def arrival(flow, chip, hc, part, k):
            return _rcopy(piece(chip, hc, part, k), piece(chip, hc, part, k), *sems(flow, k), sibling)

        sends = []
        for flow, part, peer in ((0, 0, xn), (1, 1, yn), (2, 0, yn), (3, 1, xn)):
            sends += [_rcopy(own(part, k), piece(me, c, part, k), *sems(flow, k), peer) for k in range(kc)]
        for cp in sends:
            cp.start()
        landing = ((0, cxn, 0), (1, cyn, 1), (2, cyn, 0), (3, cxn, 1), (4, cdg, 0), (5, cdg, 1))
        for i, (flow, chip, part) in enumerate(landing):
            for k in range(kc):
                arrival(flow, chip, c, part, k).wait_recv()
                if flow < 2:
                    on = _rcopy(piece(chip, c, part, k), piece(chip, c, part, k), *sems(4 + flow, k),
                                yn if flow == 0 else xn)
                    on.start()
                    sends.append(on)
                fw = _rcopy(piece(chip, c, part, k), piece(chip, c, part, k), *sems(6 + i, k), sibling)
                fw.start()
                sends.append(fw)
        for i, (flow, chip, part) in enumerate(landing):
            for k in range(kc):
                arrival(6 + i, chip, 1 - c, part, k).wait_recv()
        for cp in sends:
            cp.wait_send()
        wait_small()

    full = jax.ShapeDtypeStruct((N_SHARD, r, n), shard.dtype)
    others, slots = _pallas(
        body, name=name, in_specs=[ANY, ANY], out_specs=[ANY, ANY],
        out_shape=[full, jax.ShapeDtypeStruct((N_SHARD,) + small.shape, F32)],
        scratch_shapes=[pltpu.SemaphoreType.DMA((nflow * kc,)), pltpu.SemaphoreType.DMA((nflow * kc,)),
                        pltpu.SemaphoreType.DMA((3,)), pltpu.SemaphoreType.DMA((3,)), pltpu.SemaphoreType.DMA])(
                            shard, small)

    def place(chip_ref, own_ref, all_ref, o_ref):
        o_ref[0] = own_ref[...]

    gathered = _pallas(
        place, name=name + "_own",
        grid_spec=pltpu.PrefetchScalarGridSpec(
            num_scalar_prefetch=1, grid=(r // tr,),
            in_specs=[pl.BlockSpec((tr, n), lambda i, chip_ref: (i, 0)), ANY],
            out_specs=pl.BlockSpec((1, tr, n), lambda i, chip_ref: (chip_ref[0], i, 0))),
        out_shape=full, input_output_aliases={2: 0}, compiler_params=_cp("parallel"))(chip, shard, others)
    return gathered, slots


def _pair_copies(src_ref, dst_ref, send_sems, recv_sems):
    hn = src_ref.shape[1] // 2
    cw = hn // PAIR_CHUNKS
    assert cw * PAIR_CHUNKS == hn and cw % LANES == 0
    x, y, c, _ = _place()
    return [_rcopy(src_ref.at[:, pl.ds((1 - c) * hn + k * cw, cw)], dst_ref.at[:, pl.ds(k * cw, cw)],
                   send_sems.at[k], recv_sems.at[k], (x, y, 1 - c)) for k in range(PAIR_CHUNKS)]


def _pair_send(parts, name):
    n = parts[0].shape[1]
    hn = n // 2
    kc = PAIR_CHUNKS
    cw = hn // kc
    assert cw * kc == hn and cw % LANES == 0
    ns = len(parts)

    def body(*refs):
        srcs, dsts, send_sems, recv_sems = refs[:ns], refs[ns:2 * ns], refs[2 * ns], refs[2 * ns + 1]
        x, y, c, _ = _place()
        cps = [_rcopy(srcs[s].at[:, pl.ds((1 - c) * hn + k * cw, cw)], dsts[s].at[:, pl.ds(k * cw, cw)],
                      send_sems.at[s * kc + k], recv_sems.at[s * kc + k], (x, y, 1 - c))
               for s in range(ns) for k in range(kc)]
        for cp in cps:
            cp.start()
        for cp in cps:
            cp.wait()

    return _pallas(
        body, name=name, in_specs=[ANY] * ns, out_specs=[ANY] * ns,
        out_shape=[jax.ShapeDtypeStruct((p.shape[0], hn), F32) for p in parts],
        scratch_shapes=[pltpu.SemaphoreType.DMA((ns * kc,)), pltpu.SemaphoreType.DMA((ns * kc,))])(*parts)


REDUCE_TILE = 256


def _exchange_copies(g_ref, got_ref, send_sems, recv_sems):
    hn = g_ref.shape[2]
    kc = GATHER_CHUNKS
    cw = hn // kc
    assert cw * kc == hn and cw % LANES == 0
    x, y, c, chips = _place()
    return [_rcopy(g_ref.at[2 * chip[0] + chip[1], :, pl.ds(k * cw, cw)], got_ref.at[j, :, pl.ds(k * cw, cw)],
                   send_sems.at[j * kc + k], recv_sems.at[j * kc + k], (*chip, c))
            for j, chip in enumerate(chips) for k in range(kc)]


def _exchange_scratch():
    return [pltpu.SemaphoreType.DMA((3 * GATHER_CHUNKS,)), pltpu.SemaphoreType.DMA((3 * GATHER_CHUNKS,))]


def _exchange_shape(ga):
    return jax.ShapeDtypeStruct((3,) + ga.shape[1:], ga.dtype)


def _chip_sum(ga, got, place, name):
    _, r, hn = ga.shape
    tc = REDUCE_TILE
    nt = hn // tc

    def body(place_ref, own_ref, got_ref, o_ref):
        acc = own_ref[0].astype(F32)
        for j in range(3):
            acc = acc + got_ref[j].astype(F32)
        o_ref[...] = acc

    return _pallas(
        body, name=name,
        grid_spec=pltpu.PrefetchScalarGridSpec(
            num_scalar_prefetch=1, grid=(nt,),
            in_specs=[pl.BlockSpec((1, r, tc), lambda i, place_ref: (place_ref[0], 0, i)),
                      pl.BlockSpec((3, r, tc), lambda i, place_ref: (0, 0, i))],
            out_specs=pl.BlockSpec((r, tc), lambda i, place_ref: (0, place_ref[1] * nt + i))),
        out_shape=jax.ShapeDtypeStruct((r, 2 * hn), F32), compiler_params=_cp("parallel"))(place, ga, got)


def _pair_join(buf, name, small=None):
    r, n = buf.shape
    hn = n // 2
    kc = JOIN_CHUNKS
    cw = hn // kc
    assert cw * kc == hn and cw % LANES == 0

    def body(in_ref, *refs):
        if small is None:
            out_ref, send_sems, recv_sems = refs
        else:
            p_ref, out_ref, slots_ref, send_sems, recv_sems = refs[:5]
            start_small, wait_small = _small_exchange(p_ref, slots_ref, *refs[5:])
            start_small()
        x, y, c, _ = _place()
        cps = [_rcopy(out_ref.at[:, pl.ds(c * hn + k * cw, cw)], out_ref.at[:, pl.ds(c * hn + k * cw, cw)],
                      send_sems.at[k], recv_sems.at[k], (x, y, 1 - c)) for k in range(kc)]
        for cp in cps:
            cp.start()
        for k in range(kc):
            cols = out_ref.at[:, pl.ds((1 - c) * hn + k * cw, cw)]
            _rcopy(cols, cols, send_sems.at[k], recv_sems.at[k], (x, y, 1 - c)).wait_recv()
        for cp in cps:
            cp.wait_send()
        if small is not None:
            wait_small()

    sems = [pltpu.SemaphoreType.DMA((kc,)), pltpu.SemaphoreType.DMA((kc,))]
    if small is None:
        return _pallas(body, name=name, in_specs=[ANY], out_specs=ANY, out_shape=jax.ShapeDtypeStruct((r, n), F32),
                       input_output_aliases={0: 0}, scratch_shapes=sems)(buf)
    return _pallas(
        body, name=name, in_specs=[ANY, ANY], out_specs=[ANY, ANY],
        out_shape=[jax.ShapeDtypeStruct((r, n), F32), jax.ShapeDtypeStruct((N_DEV,) + small.shape, F32)],
        input_output_aliases={0: 0}, scratch_shapes=sems + _small_scratch())(buf, small)


def _reduce_pair(parts, got, pieces, shard_rows, place, tag):
    if len(got) < len(parts):
        got = list(got) + list(_pair_send(parts[len(got):], tag + "_pair_send"))
    return _pair_add_to_shards(parts, got, pieces, shard_rows, place[1:2], tag + "_pair_add")


def _reduce_finish(ga, slabs, place, tag):
    return _pair_join(_chip_sum(ga, slabs, place, tag + "_chip_sum"), tag + "_pair_join")


N_DEV = 8


def _small_exchange(p_ref, slots_ref, send_sems, recv_sems, local_sem):
    x, y, c, _ = _place()
    my = 4 * x + 2 * y + c

    def sends():
        return [_rcopy(p_ref, slots_ref.at[my], send_sems.at[k - 1], recv_sems.at[k - 1],
                       (x ^ ((k >> 2) & 1), y ^ ((k >> 1) & 1), c ^ (k & 1))) for k in range(1, N_DEV)]

    def local():
        return pltpu.make_async_copy(p_ref, slots_ref.at[my], local_sem)

    def start():
        local().start()
        for cp in sends():
            cp.start()

    def wait():
        for k in range(1, N_DEV):
            _rcopy(p_ref, slots_ref.at[my ^ k], send_sems.at[k - 1], recv_sems.at[k - 1], (x, y, c)).wait_recv()
        for cp in sends():
            cp.wait_send()
        local().wait()

    return start, wait


def _chip_small_exchange(p_ref, slots_ref, send_sems, recv_sems, local_sem):
    x, y, c, chips = _place()
    me = 2 * x + y

    def sends():
        return [_rcopy(p_ref, slots_ref.at[me], send_sems.at[j], recv_sems.at[j], (*chip, c))
                for j, chip in enumerate(chips)]

    def local():
        return pltpu.make_async_copy(p_ref, slots_ref.at[me], local_sem)

    def start():
        local().start()
        for cp in sends():
            cp.start()

    def wait():
        for j, chip in enumerate(chips):
            slot = slots_ref.at[2 * chip[0] + chip[1]]
            _rcopy(slot, slot, send_sems.at[j], recv_sems.at[j], (*chip, c)).wait_recv()
        for cp in sends():
            cp.wait_send()
        local().wait()

    return start, wait


def _small_scratch():
    return [pltpu.SemaphoreType.DMA((N_DEV - 1,)), pltpu.SemaphoreType.DMA((N_DEV - 1,)), pltpu.SemaphoreType.DMA]


def _sum_slots(slots, name):
    _, rows, n = slots.shape

    def body(s_ref, o_ref):
        acc = s_ref[0]
        for j in range(1, N_DEV):
            acc = acc + s_ref[j]
        o_ref[...] = acc

    vm = pl.BlockSpec(memory_space=pltpu.VMEM)
    return _pallas(body, name=name, in_specs=[vm], out_specs=vm, out_shape=jax.ShapeDtypeStruct((rows, n), F32))(slots)


def _adamw(w, g, m, v, name):
    r, n = w.shape
    tr = _tile(r, 256, 8)
    c1 = 1.0 / (1.0 - ADAM_B1 ** ADAM_STEP)
    c2 = 1.0 / (1.0 - ADAM_B2 ** ADAM_STEP)

    def body(w_ref, g_ref, m_ref, v_ref, d_ref, mo_ref, vo_ref, go_ref):
        gv = g_ref[...]
        mn = ADAM_B1 * m_ref[...] + (1.0 - ADAM_B1) * gv
        vn = ADAM_B2 * v_ref[...] + (1.0 - ADAM_B2) * (gv * gv)
        d_ref[...] = -ADAM_LR * ((mn * c1) / (jnp.sqrt(vn * c2) + ADAM_EPS) + ADAM_WD * w_ref[...])
        mo_ref[...] = mn
        vo_ref[...] = vn
        go_ref[...] = gv

    spec = pl.BlockSpec((tr, n), lambda i: (i, 0))
    shp = jax.ShapeDtypeStruct((r, n), F32)
    return _pallas(body, name=name, grid=(r // tr,), in_specs=[spec] * 4, out_specs=[spec] * 4, out_shape=[shp] * 4,
                   compiler_params=_cp("parallel"))(w, g, m, v)


PACK_W = 1024
SMALL_REPL = ("norm_pre_w", "conv_b", "ssd_norm_w", "norm_post_w")
SMALL_HEAD = ("dt_bias", "a_log", "d_skip", "attn_sinks")


def _rows(a):
    return a.reshape(-1, PACK_W)


def _head_row(vals, extra=None):
    parts = [vals[n].reshape(1, -1) for n in SMALL_HEAD]
    if extra is not None:
        parts.append(extra.reshape(1, 1))
    row = jnp.concatenate(parts, axis=1)
    return jnp.pad(row, ((0, 0), (0, PACK_W - row.shape[1])))


def _pad_rows(a, rows):
    return jnp.pad(a, ((0, rows - a.shape[0]), (0, 0)))


def _pack_repl(vals, extra=None):
    body = jnp.concatenate([_rows(vals[n]) for n in SMALL_REPL] + [_head_row(vals, extra)], axis=0)
    return _pad_rows(body, 16)


def _unpack_repl(buf):
    out, r = {}, 0
    for n, k in zip(SMALL_REPL, (2, 4, 2, 2)):
        out[n] = buf[r:r + k].reshape(1, k * PACK_W)
        r += k
    col = 0
    for n, k in zip(SMALL_HEAD, (32, 32, 32, 16)):
        out[n] = buf[r:r + 1, col:col + k]
        col += k
    return out, buf[r, col]


def kernel(x, meta_tokens, norm_pre_w, w_in, conv_w, conv_b, dt_bias, a_log, d_skip, ssd_norm_w, attn_sinks, w_out, norm_post_w, loss_target, m_meta_tokens, m_norm_pre_w, m_w_in, m_conv_w, m_conv_b, m_dt_bias, m_a_log, m_d_skip, m_ssd_norm_w, m_attn_sinks, m_w_out, m_norm_post_w, v_meta_tokens, v_norm_pre_w, v_w_in, v_conv_w, v_conv_b, v_dt_bias, v_a_log, v_d_skip, v_ssd_norm_w, v_attn_sinks, v_w_out, v_norm_post_w):
    names = ("meta_tokens", "norm_pre_w", "w_in", "conv_w", "conv_b", "dt_bias", "a_log", "d_skip", "ssd_norm_w",
             "attn_sinks", "w_out", "norm_post_w")
    w = dict(zip(names, (meta_tokens, norm_pre_w, w_in, conv_w, conv_b, dt_bias, a_log, d_skip, ssd_norm_w, attn_sinks,
                         w_out, norm_post_w)))
    m = dict(zip(names, (m_meta_tokens, m_norm_pre_w, m_w_in, m_conv_w, m_conv_b, m_dt_bias, m_a_log, m_d_skip,
                         m_ssd_norm_w, m_attn_sinks, m_w_out, m_norm_post_w)))
    v = dict(zip(names, (v_meta_tokens, v_norm_pre_w, v_w_in, v_conv_w, v_conv_b, v_dt_bias, v_a_log, v_d_skip,
                         v_ssd_norm_w, v_attn_sinks, v_w_out, v_norm_post_w)))
    cx, cy, cc = lax.axis_index("x"), lax.axis_index("y"), lax.axis_index("c")
    chip = 2 * cx + cy
    meta_cols = D_MODEL // N_SHARD
    conv_cols = D_CONV // N_SHARD

    place = jnp.stack([chip, cc]).astype(jnp.int32)
    small = jnp.concatenate([_pad_rows(conv_w[0], 8), _rows(meta_tokens)], axis=0)
    w_in_all, small_all = _gather_shards(_bf(w_in[0]), "gather_w_in", GATHER_CHUNKS, place[0:1], small)
    w_re = _shards_to_re(w_in_all)
    conv_full = jnp.transpose(small_all[:, 0:CONV_WIDTH], (1, 0, 2)).reshape(CONV_WIDTH, D_CONV)
    meta_full = jnp.transpose(small_all[:, 8:16].reshape(N_SHARD, N_META, meta_cols), (1, 0, 2)).reshape(N_META, D_MODEL)

    loss_dev, grad_x, g = _local_step(x[0], loss_target[0], meta_full, norm_pre_w, w_re, conv_full, conv_b, dt_bias,
                                      a_log, d_skip, ssd_norm_w, attn_sinks, _bf(w_out[0]), norm_post_w, place)
    g_w_out = g["w_out"]

    packed = jnp.concatenate([_rows(g["conv_w"]), _rows(g["meta_tokens"]), _pack_repl(g, loss_dev)], axis=0)
    g_w_in, slots = _pair_join(g["w_in_half"], "gw_in_pair_join", small=packed)
    red = _sum_slots(slots, "reduce_small")
    g_conv_full = red[0:16].reshape(CONV_WIDTH, D_CONV)
    g_meta_full = red[16:48].reshape(N_META, D_MODEL)
    g_small, loss = _unpack_repl(red[48:64])
    grads = dict(g_small)
    grads["w_in"] = g_w_in
    grads["w_out"] = g_w_out
    grads["conv_w"] = lax.dynamic_slice(g_conv_full, (0, chip * conv_cols), (CONV_WIDTH, conv_cols))
    grads["meta_tokens"] = lax.dynamic_slice(g_meta_full, (0, chip * meta_cols), (N_META, meta_cols))

    upd = {}
    upd["w_in"] = [jnp.swapaxes(a, 0, 1) for a in _adamw(jnp.swapaxes(w_in[0], 0, 1), g_w_in, jnp.swapaxes(m_w_in[0], 0, 1),
                                                         jnp.swapaxes(v_w_in[0], 0, 1), "adamw_w_in")]
    grads["w_in"] = upd["w_in"][3]
    upd["w_out"] = _adamw(w_out[0], g_w_out, m_w_out[0], v_w_out[0], "adamw_w_out")
    grads["w_out"] = upd["w_out"][3]

    def pack_small(vals, conv, meta):
        return jnp.concatenate([_pad_rows(conv.reshape(CONV_WIDTH, conv_cols), 8), _rows(meta), _pack_repl(vals)], axis=0)

    sm = _adamw(pack_small(w, w["conv_w"], w["meta_tokens"]), pack_small(grads, grads["conv_w"], grads["meta_tokens"]),
                pack_small(m, m["conv_w"], m["meta_tokens"]), pack_small(v, v["conv_w"], v["meta_tokens"]),
                "adamw_small")
    for n in names:
        if n not in ("w_in", "w_out"):
            upd[n] = [None, None, None]
    for k, buf in enumerate(sm[:3]):
        upd["conv_w"][k] = buf[0:CONV_WIDTH]
        upd["meta_tokens"][k] = buf[8:16].reshape(N_META, meta_cols)
        rest, _ = _unpack_repl(buf[16:32])
        for n in SMALL_REPL + SMALL_HEAD:
            upd[n][k] = rest[n]

    def shaped(n, a):
        return a.reshape(w[n].shape)

    outs = [loss, grad_x[None]]
    outs += [shaped(n, grads[n]) for n in names]
    for k in range(3):
        outs += [shaped(n, upd[n][k]) for n in names]
    return tuple(outs)
```

```python
import functools

import jax
import jax.numpy as jnp
from jax import lax
from jax.experimental import pallas as pl
from jax.experimental.pallas import tpu as pltpu

F32 = jnp.float32
BF16 = jnp.bfloat16

D_MODEL = 2048
CHUNK = 64
N_META = 16
PAD_LEAD = CHUNK - N_META
ROW0 = PAD_LEAD + N_META
EPS = 1e-6
SSD_HEADS = 32
HEAD_DIM = 64
GROUPS = 8
HPG = SSD_HEADS // GROUPS
D_STATE = 128
D_SSD = 2048
GROUP_W = D_SSD // GROUPS
CONV_WIDTH = 4
D_CONV = 4096
Q_HEADS = 16
KV_HEADS = 4
REP = Q_HEADS // KV_HEADS
D_ATT = 1024
D_KV = 256
BAND_CHUNKS = 3
ROPE_THETA = 10000.0
D_MIX = D_SSD + D_ATT
D_IN = 8736
N_SHARD = 4
W_IN_SHARD = D_IN // N_SHARD
W_OUT_SHARD = D_MIX // N_SHARD

OZ, OXS, OB, OC, OQ, OG, OK, OV, ODT = 0, 2048, 4096, 5120, 6144, 7168, 8192, 8448, 8704
DT_SLAB = 512
N_RE = ODT + DT_SLAB
LANES = 128

ADAM_LR, ADAM_B1, ADAM_B2, ADAM_EPS, ADAM_WD, ADAM_STEP = 0.001, 0.9, 0.999, 1e-08, 0.01, 10

SSD_FWD_GROUPS_PER_STEP = 4
SSD_BWD_GROUPS_PER_STEP = 8
SEG_TILE = 1024
VMEM_LIMIT = 52 * 1024 * 1024
NEG = -1e30
HI = lax.Precision.HIGHEST


def _pallas(body, **kw):
    return pl.pallas_call(body, **kw)


def _cp(*sem):
    return pltpu.CompilerParams(dimension_semantics=sem, vmem_limit_bytes=VMEM_LIMIT)


def _tile(n, cap, mult=16):
    best = None
    for d in range(mult, min(n, cap) + 1, mult):
        if n % d == 0:
            best = d
    assert best is not None, (n, cap)
    return best


def _nt(a, b):
    return lax.dot_general(a, b, (((1,), (1,)), ((), ())), preferred_element_type=F32)


def _tn(a, b):
    return lax.dot_general(a, b, (((0,), (0,)), ((), ())), preferred_element_type=F32)


def _mm(a, b):
    return jnp.dot(a, b, preferred_element_type=F32)


def _sigmoid(x):
    return 0.5 + 0.5 * jnp.tanh(0.5 * x)


def _bf(x):
    return x.astype(BF16)


def _inproj(hpad, norm_w, w_re, w_out_shard):
    t, d = hpad.shape
    n = w_re.shape[1]
    tm, tn = _tile(t, 1040), 1024
    ni, nj = t // tm, n // tn
    r_out, n_out = w_out_shard.shape
    kc = GATHER_CHUNKS

    def body(h_ref, nw_ref, w_ref, ws_ref, proj_ref, hn_ref, wall_ref, hn_s, send_sems, recv_sems, local_sems):
        i, j = pl.program_id(0), pl.program_id(1)
        start, forward, finish = _gather_plan(ws_ref, wall_ref, send_sems, recv_sems, local_sems, r_out // 2, kc)
        pl.when((i == 0) & (j == 0))(start)
        pl.when((i == ni // 2) & (j == 0))(forward)

        @pl.when(j == 0)
        def _():
            h = h_ref[...]
            ms = jnp.mean(h * h, axis=-1, keepdims=True)
            hn = _bf(h * lax.rsqrt(ms + EPS) * nw_ref[...])
            hn_s[...] = hn
            hn_ref[...] = hn
        proj_ref[...] = _mm(hn_s[...], w_ref[...])
        pl.when((i == ni - 1) & (j == nj - 1))(finish)

    return _pallas(
        body, name="inproj", grid=(ni, nj),
        in_specs=[pl.BlockSpec((tm, d), lambda i, j: (i, 0)), pl.BlockSpec((1, d), lambda i, j: (0, 0)),
                  pl.BlockSpec((d, tn), lambda i, j: (0, j)), ANY],
        out_specs=[pl.BlockSpec((tm, tn), lambda i, j: (i, j)), pl.BlockSpec((tm, d), lambda i, j: (i, 0)), ANY],
        out_shape=[jax.ShapeDtypeStruct((t, n), F32), jax.ShapeDtypeStruct((t, d), BF16),
                   jax.ShapeDtypeStruct((N_SHARD, r_out, n_out), w_out_shard.dtype)],
        scratch_shapes=[pltpu.VMEM((tm, d), BF16), pltpu.SemaphoreType.DMA((6 * kc,)), pltpu.SemaphoreType.DMA((6 * kc,)),
                        pltpu.SemaphoreType.DMA((2 * kc,))],
        compiler_params=_cp("arbitrary", "arbitrary"))(hpad, norm_w, w_re, w_out_shard)


def _conv_fwd(proj, conv_w, conv_b):
    t = proj.shape[0]
    tc = 256
    off = OXS // tc

    def body(x_ref, w_ref, b_ref, o_ref):
        x = x_ref[...]
        w = w_ref[...]
        row = lax.broadcasted_iota(jnp.int32, (t, tc), 0)
        u = b_ref[...] + w[3:4, :] * x
        for k in range(1, CONV_WIDTH):
            u = u + w[3 - k:4 - k, :] * jnp.where(row >= k, pltpu.roll(x, k, 0), 0.0)
        h = 0.5 * u
        o_ref[...] = h + h * jnp.tanh(h)

    return _pallas(
        body, name="conv_fwd", grid=(D_CONV // tc,),
        in_specs=[pl.BlockSpec((t, tc), lambda j: (0, j + off)), pl.BlockSpec((CONV_WIDTH, tc), lambda j: (0, j)),
                  pl.BlockSpec((1, tc), lambda j: (0, j))],
        out_specs=pl.BlockSpec((t, tc), lambda j: (0, j)),
        out_shape=jax.ShapeDtypeStruct((t, D_CONV), F32),
        compiler_params=_cp("parallel"))(proj, conv_w, conv_b)


def _softplus(u):
    e = jnp.exp(-jnp.abs(u))
    w = 1.0 + e
    l1p = jnp.where(w == 1.0, e, jnp.log(w) * (e / jnp.where(w == 1.0, 1.0, w - 1.0)))
    return jnp.maximum(u, 0.0) + l1p


def _chunks_per_step(nc):
    return max(d for d in range(1, 14) if nc % d == 0)


def _dt_prep(proj, dt_bias_l, a_log_l):
    t = proj.shape[0]
    nc = t // CHUNK
    q = CHUNK
    cps = _chunks_per_step(nc)
    rows = cps * q

    def body(raw_ref, bias_ref, alog_ref, dt_ref, acs_ref, acst_ref):
        ri = lax.broadcasted_iota(jnp.int32, (q, q), 0)
        ci = lax.broadcasted_iota(jnp.int32, (q, q), 1)
        tri = (ri >= ci).astype(F32)
        neg_a = -jnp.exp(alog_ref[...])
        for k in range(cps):
            rk = slice(q * k, q * (k + 1))
            sp = _softplus(raw_ref[rk, :] + bias_ref[...])
            row = pl.program_id(0) * rows + q * k + lax.broadcasted_iota(jnp.int32, (q, LANES), 0)
            dt = jnp.where(row >= PAD_LEAD, sp, 0.0)
            acs = jnp.dot(tri, dt * neg_a, preferred_element_type=F32, precision=HI)
            dt_ref[rk, :] = dt
            acs_ref[rk, :] = acs
            acst_ref[k] = acs.T

    return _pallas(
        body, name="dt_prep", grid=(nc // cps,),
        in_specs=[pl.BlockSpec((rows, LANES), lambda c: (c, ODT // LANES)), pl.BlockSpec((1, LANES), lambda c: (0, 0)),
                  pl.BlockSpec((1, LANES), lambda c: (0, 0))],
        out_specs=[pl.BlockSpec((rows, LANES), lambda c: (c, 0)), pl.BlockSpec((rows, LANES), lambda c: (c, 0)),
                   pl.BlockSpec((cps, LANES, q), lambda c: (c, 0, 0))],
        out_shape=[jax.ShapeDtypeStruct((t, LANES), F32), jax.ShapeDtypeStruct((t, LANES), F32),
                   jax.ShapeDtypeStruct((nc, LANES, q), F32)],
        compiler_params=_cp("parallel"))(proj, dt_bias_l, a_log_l)


def _head_cols(blk, idx):
    lane = lax.broadcasted_iota(jnp.int32, blk.shape, 1)
    return jnp.sum(jnp.where(lane == idx, blk, 0.0), axis=1, keepdims=True)


class _HeadVals:
    pass


def _lane_head(shape):
    return lax.broadcasted_iota(jnp.int32, shape, len(shape) - 1) >> 6


def _group_heads(g, gi, dtb, acsb, acst_ref, dskb):
    q = dtb.shape[0]
    hv = _HeadVals()
    lh = _lane_head((1, GROUP_W))
    hv.dt = jnp.zeros((q, GROUP_W), F32)
    hv.acs = jnp.zeros((q, GROUP_W), F32)
    hv.acs_last = jnp.zeros((1, GROUP_W), F32)
    hv.dsk = jnp.zeros((1, GROUP_W), F32)
    rows = []
    for r in range(HPG):
        idx = GROUPS * g + r
        sel = lh == r
        acs_r = acst_ref[0, GROUPS * gi + r:GROUPS * gi + r + 1, :]
        rows.append(acs_r)
        hv.dt = jnp.where(sel, _head_cols(dtb, idx), hv.dt)
        hv.acs = jnp.where(sel, _head_cols(acsb, idx), hv.acs)
        hv.acs_last = jnp.where(sel, acs_r[:, q - 1:q], hv.acs_last)
        hv.dsk = jnp.where(sel, _head_cols(dskb, idx), hv.dsk)
    hv.acs_row = jnp.concatenate(rows, axis=1)
    return hv


def _head_tri(q, lower):
    ri = lax.broadcasted_iota(jnp.int32, (q, GROUP_W), 0)
    li = lax.broadcasted_iota(jnp.int32, (q, GROUP_W), 1) & (HEAD_DIM - 1)
    return ri >= li if lower else ri <= li


def _block_diag_mask():
    rb = lax.broadcasted_iota(jnp.int32, (GROUP_W, GROUP_W), 0) >> 6
    cb = lax.broadcasted_iota(jnp.int32, (GROUP_W, GROUP_W), 1) >> 6
    return rb == cb


def _block_diag(v, mask):
    return jnp.where(mask, jnp.concatenate([v] * HPG, axis=0), jnp.zeros((), v.dtype))


def _head_sums(v, r):
    return jnp.sum(jnp.where(_lane_head((1, GROUP_W)) == r, v, 0.0), axis=1, keepdims=True)


def _ssd_fwd(xbc, proj, dt, acs, acst, d_skip_l, ssd_norm_w):
    t = xbc.shape[0]
    q = CHUNK
    nc = t // q

    gps = SSD_FWD_GROUPS_PER_STEP
    gw, sw = gps * GROUP_W, gps * D_STATE

    def body(xs_ref, b_ref, c_ref, dt_ref, acs_ref, acst_ref, z_ref, dsk_ref, nw_ref,
             y_ref, ymix_ref, st_ref, state):
        @pl.when(pl.program_id(1) == 0)
        def _():
            state[...] = jnp.zeros_like(state)

        lower = _head_tri(q, True)
        bd_mask = _block_diag_mask()
        for gi in range(gps):
            g = gps * pl.program_id(0) + gi
            cols = slice(GROUP_W * gi, GROUP_W * (gi + 1))
            x = xs_ref[:, cols]
            bmb = _bf(b_ref[:, D_STATE * gi:D_STATE * (gi + 1)])
            cmb = _bf(c_ref[:, D_STATE * gi:D_STATE * (gi + 1)])
            hv = _group_heads(g, gi, dt_ref[...], acs_ref[...], acst_ref, dsk_ref[...])
            decay = jnp.exp(jnp.where(lower, hv.acs - hv.acs_row, NEG))
            m_all = _bf(_nt(cmb, jnp.concatenate([bmb] * HPG, axis=0)) * decay)
            xdt = x * hv.dt
            s_prev = state[gi]
            st_ref[0, gi] = s_prev
            y = (_mm(m_all, _block_diag(_bf(xdt), bd_mask)) + _mm(cmb, _bf(s_prev)) * jnp.exp(hv.acs) + hv.dsk * x)
            state[gi] = jnp.exp(hv.acs_last) * s_prev + _tn(bmb, _bf(xdt * jnp.exp(hv.acs_last - hv.acs)))
            y_ref[:, cols] = y
            z = z_ref[:, cols]
            yg = y * (z * _sigmoid(z))
            ms = jnp.mean(yg * yg, axis=-1, keepdims=True)
            ymix_ref[:, cols] = _bf(yg * lax.rsqrt(ms + EPS) * nw_ref[:, cols])

    return _pallas(
        body, name="ssd_fwd", grid=(GROUPS // gps, nc),
        in_specs=[pl.BlockSpec((q, gw), lambda g, c: (c, g)),
                  pl.BlockSpec((q, sw), lambda g, c: (c, D_SSD // sw + g)),
                  pl.BlockSpec((q, sw), lambda g, c: (c, (D_SSD + GROUPS * D_STATE) // sw + g)),
                  pl.BlockSpec((q, LANES), lambda g, c: (c, 0)), pl.BlockSpec((q, LANES), lambda g, c: (c, 0)),
                  pl.BlockSpec((1, gps * GROUPS, q), lambda g, c: (c, g, 0)),
                  pl.BlockSpec((q, gw), lambda g, c: (c, g)),
                  pl.BlockSpec((1, LANES), lambda g, c: (0, 0)), pl.BlockSpec((1, gw), lambda g, c: (0, g))],
        out_specs=[pl.BlockSpec((q, gw), lambda g, c: (c, g)), pl.BlockSpec((q, gw), lambda g, c: (c, g)),
                   pl.BlockSpec((1, gps, D_STATE, GROUP_W), lambda g, c: (c, g, 0, 0))],
        out_shape=[jax.ShapeDtypeStruct((t, D_SSD), F32), jax.ShapeDtypeStruct((t, D_SSD), BF16),
                   jax.ShapeDtypeStruct((nc, GROUPS, D_STATE, GROUP_W), F32)],
        scratch_shapes=[pltpu.VMEM((gps, D_STATE, GROUP_W), F32)],
        compiler_params=_cp("parallel", "arbitrary"))(xbc, xbc, xbc, dt, acs, acst, proj, d_skip_l, ssd_norm_w)


def _swap_halves(v):
    lane = lax.broadcasted_iota(jnp.int32, v.shape, 1)
    return jnp.where((lane & (HEAD_DIM - 1)) < HEAD_DIM // 2, pltpu.roll(v, LANES - HEAD_DIM // 2, 1),
                     pltpu.roll(v, HEAD_DIM // 2, 1))


def _rope(qsrc, q_off, ksrc, k_off, cos_t, sin_t):
    t = qsrc.shape[0]
    tr = _tile(t, 832)
    q_scale = HEAD_DIM ** -0.5

    def body(q_ref, k_ref, cos_ref, sin_ref, qo_ref, ko_ref):
        cs = cos_ref[...]
        sn = sin_ref[...]
        for src, dst, width, scale in ((q_ref, qo_ref, D_ATT, q_scale), (k_ref, ko_ref, D_KV, 1.0)):
            for s in range(width // LANES):
                v = src[:, LANES * s:LANES * (s + 1)].astype(F32)
                dst[:, LANES * s:LANES * (s + 1)] = _bf((v * cs + _swap_halves(v) * sn) * scale)

    return _pallas(
        body, name="rope", grid=(t // tr,),
        in_specs=[pl.BlockSpec((tr, D_ATT), lambda i: (i, q_off // D_ATT)),
                  pl.BlockSpec((tr, D_KV), lambda i: (i, k_off // D_KV)),
                  pl.BlockSpec((tr, LANES), lambda i: (i, 0)), pl.BlockSpec((tr, LANES), lambda i: (i, 0))],
        out_specs=[pl.BlockSpec((tr, D_ATT), lambda i: (i, 0)), pl.BlockSpec((tr, D_KV), lambda i: (i, 0))],
        out_shape=[jax.ShapeDtypeStruct((t, D_ATT), BF16), jax.ShapeDtypeStruct((t, D_KV), BF16)],
        compiler_params=_cp("parallel"))(qsrc, ksrc, cos_t, sin_t)


def _attn_chunks_per_step(nc):
    return max(d for d in range(1, 6) if nc % d == 0)


def _band(ref, c):
    return [ref[pl.ds(pl.multiple_of(jnp.maximum(c - j, 0) * CHUNK, CHUNK), CHUNK), :] for j in (2, 1, 0)]


def _attn_probs(qh, kb, sink_col, valid):
    s = jnp.where(valid, _nt(qh, kb), NEG)
    m = jnp.maximum(jnp.max(s, axis=1, keepdims=True), sink_col)
    p = jnp.exp(s - m)
    psink = jnp.exp(sink_col - m)
    return p, psink, 1.0 / (jnp.sum(p, axis=1, keepdims=True) + psink)


def _attn_operands(c, q, k_refs, v_refs, sink_ref, h):
    qh = jnp.concatenate([q[:, HEAD_DIM * (REP * h + r):HEAD_DIM * (REP * h + r + 1)] for r in range(REP)], axis=0)
    kb = jnp.concatenate([k[:, HEAD_DIM * h:HEAD_DIM * (h + 1)] for k in k_refs], axis=0)
    vb = jnp.concatenate([_bf(v[:, HEAD_DIM * h:HEAD_DIM * (h + 1)]) for v in v_refs], axis=0)
    rows = lax.broadcasted_iota(jnp.int32, (REP * CHUNK, 1), 0) >> 6
    sink_col = jnp.zeros((REP * CHUNK, 1), F32)
    for r in range(REP):
        sink_col = jnp.where(rows == r, sink_ref[REP * h + r], sink_col)
    key_abs = (c - (BAND_CHUNKS - 1)) * CHUNK + lax.broadcasted_iota(jnp.int32, (1, BAND_CHUNKS * CHUNK), 1)
    return qh, kb, vb, sink_col, key_abs >= PAD_LEAD


def _attn_fwd(qr, kr, proj, sinks):
    t = qr.shape[0]
    nc = t // CHUNK
    cps = _attn_chunks_per_step(nc)
    rows = cps * CHUNK

    def body(q_ref, k_ref, v_ref, g_ref, sink_ref, o_ref):
        for j in range(cps):
            c = pl.program_id(0) * cps + j
            rj = slice(CHUNK * j, CHUNK * (j + 1))
            ks, vs = _band(k_ref, c), _band(v_ref, c)
            q = q_ref[rj, :]
            outs = []
            for h in range(KV_HEADS):
                qh, kb, vb, sink_col, valid = _attn_operands(c, q, ks, vs, sink_ref, h)
                p, _, inv = _attn_probs(qh, kb, sink_col, valid)
                o = _mm(_bf(p), vb) * inv
                outs += [o[CHUNK * r:CHUNK * (r + 1)] for r in range(REP)]
            att = jnp.concatenate(outs, axis=1)
            gate = g_ref[rj, :]
            o_ref[rj, :] = _bf(att * (gate * _sigmoid(gate)))

    return _pallas(
        body, name="attn_fwd", grid=(nc // cps,),
        in_specs=[pl.BlockSpec((rows, D_ATT), lambda i: (i, 0)), pl.BlockSpec((t, D_KV), lambda i: (0, 0)),
                  pl.BlockSpec((t, D_KV), lambda i: (0, OV // D_KV)),
                  pl.BlockSpec((rows, D_ATT), lambda i: (i, OG // D_ATT)), pl.BlockSpec(memory_space=pltpu.SMEM)],
        out_specs=pl.BlockSpec((rows, D_ATT), lambda i: (i, 0)),
        out_shape=jax.ShapeDtypeStruct((t, D_ATT), BF16),
        compiler_params=_cp("parallel"))(qr, kr, proj, proj, sinks)


def _outproj(ymix, amix, w_out):
    t = ymix.shape[0]
    tm, tn = _tile(t, 832), 1024

    def body(y_ref, a_ref, wy_ref, wa_ref, o_ref):
        o_ref[...] = _mm(y_ref[...], wy_ref[...]) + _mm(a_ref[...], wa_ref[...])

    return _pallas(
        body, name="outproj", grid=(t // tm, D_MODEL // tn),
        in_specs=[pl.BlockSpec((tm, D_SSD), lambda i, j: (i, 0)), pl.BlockSpec((tm, D_ATT), lambda i, j: (i, 0)),
                  pl.BlockSpec((D_SSD, tn), lambda i, j: (0, j)),
                  pl.BlockSpec((D_ATT, tn), lambda i, j: (D_SSD // D_ATT, j))],
        out_specs=pl.BlockSpec((tm, tn), lambda i, j: (i, j)),
        out_shape=jax.ShapeDtypeStruct((t, D_MODEL), F32),
        compiler_params=_cp("parallel", "parallel"))(ymix, amix, w_out, w_out)


def _post_loss(out, x, target, norm_post_w):
    t = out.shape[0]
    nc = t // CHUNK
    cps = _attn_chunks_per_step(nc)
    rows = cps * CHUNK

    def body(o_ref, *refs):
        x_refs, tg_refs = refs[:cps], refs[cps:2 * cps]
        nw_ref, dout_ref, dy_ref, loss_ref, gnw_ref = refs[2 * cps:]
        i = pl.program_id(0)

        @pl.when(i == 0)
        def _():
            loss_ref[...] = jnp.zeros_like(loss_ref)
            gnw_ref[...] = jnp.zeros_like(gnw_ref)

        nw = nw_ref[...]
        loss = jnp.zeros((), F32)
        gnw = jnp.zeros((1, D_MODEL), F32)
        for k in range(cps):
            rk = slice(CHUNK * k, CHUNK * (k + 1))
            frames = i * cps + k > 0
            o = o_ref[rk, :]
            rstd = lax.rsqrt(jnp.mean(o * o, axis=-1, keepdims=True) + EPS)
            n = o * rstd
            err = jnp.where(frames, x_refs[k][...] + n * nw - tg_refs[k][...], 0.0)
            loss = loss + jnp.sum(err * err)
            dy = err * (1.0 / D_MODEL)
            dy_ref[rk, :] = dy
            gnw = gnw + jnp.sum(dy * n, axis=0, keepdims=True)
            dn = dy * nw
            dout_ref[rk, :] = _bf(rstd * (dn - n * jnp.mean(dn * n, axis=-1, keepdims=True)))
        loss_ref[...] += loss * (0.5 / D_MODEL)
        gnw_ref[...] += gnw

    lower = [pl.BlockSpec((CHUNK, D_MODEL), functools.partial(lambda i, k: (jnp.maximum(i * cps + k - 1, 0), 0), k=k))
             for k in range(cps)]
    return _pallas(
        body, name="post_loss", grid=(nc // cps,),
        in_specs=[pl.BlockSpec((rows, D_MODEL), lambda i: (i, 0))] + lower + lower
        + [pl.BlockSpec((1, D_MODEL), lambda i: (0, 0))],
        out_specs=[pl.BlockSpec((rows, D_MODEL), lambda i: (i, 0)), pl.BlockSpec((rows, D_MODEL), lambda i: (i, 0)),
                   pl.BlockSpec((8, LANES), lambda i: (0, 0)), pl.BlockSpec((1, D_MODEL), lambda i: (0, 0))],
        out_shape=[jax.ShapeDtypeStruct((t, D_MODEL), BF16), jax.ShapeDtypeStruct((t, D_MODEL), F32),
                   jax.ShapeDtypeStruct((8, LANES), F32), jax.ShapeDtypeStruct((1, D_MODEL), F32)],
        compiler_params=_cp("arbitrary"))(out, *([x] * cps), *([target] * cps), norm_post_w)


def _carried(grid, carry):
    if carry is None:
        return [], [], [], [], lambda refs: None, lambda refs: None
    hn = carry.shape[1] // 2

    def at(ids, which):
        cond = None
        for d, size in enumerate(grid):
            here = pl.program_id(d) == (0 if which == "first" else size - 1)
            cond = here if cond is None else cond & here
        return cond

    def start(refs):
        @pl.when(at(grid, "first"))
        def _():
            for cp in _pair_copies(*refs):
                cp.start()

    def finish(refs):
        @pl.when(at(grid, "last"))
        def _():
            for cp in _pair_copies(*refs):
                cp.wait()

    return ([ANY], [ANY], [jax.ShapeDtypeStruct((carry.shape[0], hn), F32)],
            [pltpu.SemaphoreType.DMA((PAIR_CHUNKS,)), pltpu.SemaphoreType.DMA((PAIR_CHUNKS,))], start, finish)


def _nt_matmul(a, b, name, carry=None):
    t, k = a.shape
    n = b.shape[0]
    tm, tn = _tile(t, 832), 1024
    grid = (t // tm, n // tn)
    cin, cout, cshape, cscratch, start, finish = _carried(grid, carry)

    def body(a_ref, b_ref, *refs):
        o_ref = refs[len(cin)]
        comm = (refs[0], refs[2], refs[3], refs[4]) if carry is not None else None
        start(comm)
        o_ref[...] = _nt(a_ref[...], b_ref[...])
        finish(comm)

    res = _pallas(
        body, name=name, grid=grid,
        in_specs=[pl.BlockSpec((tm, k), lambda i, j: (i, 0)), pl.BlockSpec((tn, k), lambda i, j: (j, 0))] + cin,
        out_specs=[pl.BlockSpec((tm, tn), lambda i, j: (i, j))] + cout,
        out_shape=[jax.ShapeDtypeStruct((t, n), F32)] + cshape, scratch_shapes=cscratch,
        compiler_params=_cp("arbitrary", "arbitrary"))(a, b, *([carry] if carry is not None else []))
    return res if carry is not None else res[0]


def _tn_matmul(a, b, name, carry=None):
    t, m = a.shape
    n = b.shape[1]
    tk, tm, tn = _tile(t, 832), min(m, 2048), min(n, 2048)
    nk = t // tk
    grid = (m // tm, n // tn, nk)
    cin, cout, cshape, cscratch, start, finish = _carried(grid, carry)

    def body(a_ref, b_ref, *refs):
        o_ref = refs[len(cin)]
        comm = (refs[0], refs[2], refs[3], refs[4]) if carry is not None else None
        start(comm)

        @pl.when(pl.program_id(2) == 0)
        def _():
            o_ref[...] = jnp.zeros_like(o_ref)
        o_ref[...] += _tn(a_ref[...], b_ref[...])
        finish(comm)

    res = _pallas(
        body, name=name, grid=grid,
        in_specs=[pl.BlockSpec((tk, tm), lambda i, j, k: (k, i)), pl.BlockSpec((tk, tn), lambda i, j, k: (k, j))] + cin,
        out_specs=[pl.BlockSpec((tm, tn), lambda i, j, k: (i, j))] + cout,
        out_shape=[jax.ShapeDtypeStruct((m, n), F32)] + cshape, scratch_shapes=cscratch,
        compiler_params=_cp("arbitrary", "arbitrary", "arbitrary"))(a, b, *([carry] if carry is not None else []))
    return res if carry is not None else res[0]


def _attn_bwd(qr, kr, proj, dmix, sinks, ga):
    t = qr.shape[0]
    nc = t // CHUNK
    cps = _attn_chunks_per_step(nc)
    nsteps = nc // cps
    rows_step = cps * CHUNK

    def body(q_ref, k_ref, v_ref, g_ref, da_ref, sink_ref, ga_ref, dq_ref, dg_ref, dk_ref, dv_ref, gs_ref,
             got_ref, send_sems, recv_sems):
        step = pl.program_id(0)

        @pl.when(step == 0)
        def _():
            for cp in _exchange_copies(ga_ref, got_ref, send_sems, recv_sems):
                cp.start()
            dk_ref[...] = jnp.zeros_like(dk_ref)
            dv_ref[...] = jnp.zeros_like(dv_ref)
            gs_ref[...] = jnp.zeros_like(gs_ref)

        lane = lax.broadcasted_iota(jnp.int32, (1, LANES), 1)
        rows = lax.broadcasted_iota(jnp.int32, (REP * CHUNK, 1), 0) >> 6
        gs = jnp.zeros((1, LANES), F32)
        dk_parts = [[] for _ in range(cps + BAND_CHUNKS - 1)]
        dv_parts = [[] for _ in range(cps + BAND_CHUNKS - 1)]
        for j in range(cps):
            c = step * cps + j
            rj = slice(CHUNK * j, CHUNK * (j + 1))
            ks, vs = _band(k_ref, c), _band(v_ref, c)
            q = q_ref[rj, :]
            gate = g_ref[rj, :]
            sg = _sigmoid(gate)
            da = da_ref[rj, :]
            datt = da * (gate * sg)
            dqs, atts, dks, dvs = [], [], [], []
            for h in range(KV_HEADS):
                qh, kb, vb, sink_col, valid = _attn_operands(c, q, ks, vs, sink_ref, h)
                p, psink, inv = _attn_probs(qh, kb, sink_col, valid)
                pb = _bf(p)
                o = _mm(pb, vb) * inv
                do = jnp.concatenate([datt[:, HEAD_DIM * (REP * h + r):HEAD_DIM * (REP * h + r + 1)]
                                      for r in range(REP)], axis=0)
                dob = _bf(do * inv)
                delta = jnp.sum(do * o, axis=1, keepdims=True) * inv
                ds = _bf(p * (_nt(dob, vb) - delta))
                gsink = -psink * delta
                for r in range(REP):
                    gs = gs + jnp.where(lane == REP * h + r, jnp.sum(jnp.where(rows == r, gsink, 0.0)), 0.0)
                dqh = _mm(ds, kb)
                dqs += [dqh[CHUNK * r:CHUNK * (r + 1)] for r in range(REP)]
                atts += [o[CHUNK * r:CHUNK * (r + 1)] for r in range(REP)]
                dks.append(_tn(ds, qh))
                dvs.append(_tn(pb, dob))
            dq_ref[rj, :] = jnp.concatenate(dqs, axis=1)
            att = jnp.concatenate(atts, axis=1)
            dg_ref[rj, :] = _bf(da * att * (sg * (1.0 + gate * (1.0 - sg))))
            dkf = jnp.concatenate(dks, axis=1)
            dvf = jnp.concatenate(dvs, axis=1)
            for b in range(BAND_CHUNKS):
                dk_parts[j + b].append(dkf[CHUNK * b:CHUNK * (b + 1)])
                dv_parts[j + b].append(dvf[CHUNK * b:CHUNK * (b + 1)])
        gs_ref[0:1, :] += gs
        for rel in range(cps + BAND_CHUNKS - 1):
            r0 = pl.multiple_of(jnp.maximum(step * cps - (BAND_CHUNKS - 1) + rel, 0) * CHUNK, CHUNK)
            dk_ref[pl.ds(r0, CHUNK), :] += sum(dk_parts[rel][1:], dk_parts[rel][0])
            dv_ref[pl.ds(r0, CHUNK), :] += sum(dv_parts[rel][1:], dv_parts[rel][0])

        @pl.when(step == nsteps - 1)
        def _():
            for cp in _exchange_copies(ga_ref, got_ref, send_sems, recv_sems):
                cp.wait()

    return _pallas(
        body, name="attn_bwd", grid=(nsteps,),
        in_specs=[pl.BlockSpec((rows_step, D_ATT), lambda i: (i, 0)), pl.BlockSpec((t, D_KV), lambda i: (0, 0)),
                  pl.BlockSpec((t, D_KV), lambda i: (0, OV // D_KV)),
                  pl.BlockSpec((rows_step, D_ATT), lambda i: (i, OG // D_ATT)),
                  pl.BlockSpec((rows_step, D_ATT), lambda i: (i, D_SSD // D_ATT)),
                  pl.BlockSpec(memory_space=pltpu.SMEM), ANY],
        out_specs=[pl.BlockSpec((rows_step, D_ATT), lambda i: (i, 0)), pl.BlockSpec((rows_step, D_ATT), lambda i: (i, 0)),
                   pl.BlockSpec((t, D_KV), lambda i: (0, 0)), pl.BlockSpec((t, D_KV), lambda i: (0, 0)),
                   pl.BlockSpec((8, LANES), lambda i: (0, 0)), ANY],
        out_shape=[jax.ShapeDtypeStruct((t, D_ATT), F32), jax.ShapeDtypeStruct((t, D_ATT), BF16),
                   jax.ShapeDtypeStruct((t, D_KV), F32), jax.ShapeDtypeStruct((t, D_KV), F32),
                   jax.ShapeDtypeStruct((8, LANES), F32), _exchange_shape(ga)],
        scratch_shapes=_exchange_scratch(),
        compiler_params=_cp("arbitrary"))(qr, kr, proj, proj, dmix, sinks, ga)


def _ssd_bwd(dmix, y_ssd, xbc, proj, dt, acs, acst, states, d_skip_l, ssd_norm_w):
    t = xbc.shape[0]
    q = CHUNK
    nc = t // q
    gps = SSD_BWD_GROUPS_PER_STEP
    gw, sw = gps * GROUP_W, gps * D_STATE

    def body(dmix_ref, y_ref, z_ref, nw_ref, xs_ref, b_ref, c_ref, dt_ref, acs_ref, acst_ref, st_ref, dsk_ref,
             dz_ref, dxs_ref, db_ref, dc_ref, dacs_ref, ddt_ref, gnw_ref, gdsk_ref, dstate):
        @pl.when(pl.program_id(1) == 0)
        def _():
            dstate[...] = jnp.zeros_like(dstate)
            gnw_ref[...] = jnp.zeros_like(gnw_ref)
            gdsk_ref[...] = jnp.zeros_like(gdsk_ref)

        last_row = lax.broadcasted_iota(jnp.int32, (q, 1), 0) == q - 1
        lane = lax.broadcasted_iota(jnp.int32, (q, LANES), 1)
        lane1 = lax.broadcasted_iota(jnp.int32, (8, LANES), 1)
        lower, upper = _head_tri(q, True), _head_tri(q, False)
        bd_mask = _block_diag_mask()
        for gi in range(gps):
            g = gps * pl.program_id(0) + gi
            cols = slice(GROUP_W * gi, GROUP_W * (gi + 1))
            scols = slice(D_STATE * gi, D_STATE * (gi + 1))
            y = y_ref[:, cols]
            z = z_ref[:, cols]
            sz = _sigmoid(z)
            silu_z = z * sz
            yg = y * silu_z
            rstd = lax.rsqrt(jnp.mean(yg * yg, axis=-1, keepdims=True) + EPS)
            n = yg * rstd
            dout = dmix_ref[:, cols]
            gnw_ref[:, cols] += jnp.sum(dout * n, axis=0, keepdims=True)
            dn = dout * nw_ref[:, cols]
            dyg = rstd * (dn - n * jnp.mean(dn * n, axis=-1, keepdims=True))
            dy = dyg * silu_z
            dz_ref[:, cols] = _bf(dyg * y * (sz * (1.0 + z * (1.0 - sz))))

            x = xs_ref[:, cols]
            bmb, cmb = _bf(b_ref[:, scols]), _bf(c_ref[:, scols])
            hv = _group_heads(g, gi, dt_ref[...], acs_ref[...], acst_ref, dsk_ref[...])
            dec = jnp.exp(jnp.where(lower, hv.acs - hv.acs_row, NEG))
            dect = jnp.exp(jnp.where(upper, hv.acs_row - hv.acs, NEG))
            b4 = jnp.concatenate([bmb] * HPG, axis=0)
            c4 = jnp.concatenate([cmb] * HPG, axis=0)
            m_all = _nt(cmb, b4) * dec
            mt_all = _nt(bmb, c4) * dect
            xdt = x * hv.dt
            xdt_b, dyb = _bf(xdt), _bf(dy)
            x_bd, dy_bd = _block_diag(xdt_b, bd_mask), _block_diag(dyb, bd_mask)
            s_prev = st_ref[0, gi]
            spb = _bf(s_prev)
            ds_new = dstate[gi]
            dsb = _bf(ds_new)
            e = jnp.exp(hv.acs)
            elast = jnp.exp(hv.acs_last)
            dte = jnp.exp(hv.acs_last - hv.acs)
            bds = _mm(bmb, dsb)
            dxdt = _mm(_bf(mt_all), dy_bd) + bds * dte
            dm = _nt(dyb, x_bd)
            dmt = _nt(xdt_b, dy_bd)
            dye = _bf(dy * e)
            dc_ref[:, scols] = _mm(_bf(dm * dec), b4) + _nt(dye, spb)
            db_ref[:, scols] = _mm(_bf(dmt * dect), c4) + _nt(_bf(xdt * dte), dsb)
            dstate[gi] = elast * ds_new + _tn(cmb, dye)
            dxs_ref[:, cols] = dxdt * hv.dt + hv.dsk * dy
            ddte_dte = bds * xdt * dte
            dacs_l = dm * m_all - dmt * mt_all + dy * _mm(cmb, spb) * e - ddte_dte
            dlast_l = (jnp.sum(ddte_dte, axis=0, keepdims=True)
                       + jnp.sum(s_prev * ds_new, axis=0, keepdims=True) * elast)
            ddt_l = dxdt * x
            gdsk_l = jnp.sum(dy * x, axis=0, keepdims=True)
            dacs_out = jnp.zeros((q, LANES), F32)
            ddt_out = jnp.zeros((q, LANES), F32)
            gdsk = jnp.zeros((8, LANES), F32)
            for r in range(HPG):
                dacs = _head_sums(dacs_l, r) + jnp.where(last_row, _head_sums(dlast_l, r), 0.0)
                dacs_out = jnp.where(lane == r, dacs, dacs_out)
                ddt_out = jnp.where(lane == r, _head_sums(ddt_l, r), ddt_out)
                gdsk = gdsk + jnp.where(lane1 == r, _head_sums(gdsk_l, r), 0.0)
            dacs_ref[:, LANES * gi:LANES * (gi + 1)] = dacs_out
            ddt_ref[:, LANES * gi:LANES * (gi + 1)] = ddt_out
            gdsk_ref[gi] += gdsk

    rev = lambda c: nc - 1 - c
    wide = pl.BlockSpec((q, gw), lambda g, c: (rev(c), g))
    return _pallas(
        body, name="ssd_bwd", grid=(GROUPS // gps, nc),
        in_specs=[wide, wide, wide, pl.BlockSpec((1, gw), lambda g, c: (0, g)), wide,
                  pl.BlockSpec((q, sw), lambda g, c: (rev(c), D_SSD // sw + g)),
                  pl.BlockSpec((q, sw), lambda g, c: (rev(c), (D_SSD + GROUPS * D_STATE) // sw + g)),
                  pl.BlockSpec((q, LANES), lambda g, c: (rev(c), 0)), pl.BlockSpec((q, LANES), lambda g, c: (rev(c), 0)),
                  pl.BlockSpec((1, gps * GROUPS, q), lambda g, c: (rev(c), g, 0)),
                  pl.BlockSpec((1, gps, D_STATE, GROUP_W), lambda g, c: (rev(c), g, 0, 0)),
                  pl.BlockSpec((1, LANES), lambda g, c: (0, 0))],
        out_specs=[wide, wide,
                   pl.BlockSpec((q, sw), lambda g, c: (rev(c), g)), pl.BlockSpec((q, sw), lambda g, c: (rev(c), g)),
                   pl.BlockSpec((q, gps * LANES), lambda g, c: (rev(c), g)),
                   pl.BlockSpec((q, gps * LANES), lambda g, c: (rev(c), g)),
                   pl.BlockSpec((1, gw), lambda g, c: (0, g)), pl.BlockSpec((gps, 8, LANES), lambda g, c: (g, 0, 0))],
        out_shape=[jax.ShapeDtypeStruct((t, D_SSD), BF16), jax.ShapeDtypeStruct((t, D_SSD), F32),
                   jax.ShapeDtypeStruct((t, GROUPS * D_STATE), F32), jax.ShapeDtypeStruct((t, GROUPS * D_STATE), F32),
                   jax.ShapeDtypeStruct((t, GROUPS * LANES), F32), jax.ShapeDtypeStruct((t, GROUPS * LANES), F32),
                   jax.ShapeDtypeStruct((1, D_SSD), F32), jax.ShapeDtypeStruct((GROUPS, 8, LANES), F32)],
        scratch_shapes=[pltpu.VMEM((gps, D_STATE, GROUP_W), F32)],
        compiler_params=_cp("parallel", "arbitrary"))(dmix, y_ssd, proj, ssd_norm_w, xbc, xbc, xbc, dt, acs, acst,
                                                      states, d_skip_l)


def _dt_bwd(dacs_g, ddt_g, dt, proj, dt_bias_l, a_log_l):
    t = dt.shape[0]
    q = CHUNK
    nc = t // q
    cps = _chunks_per_step(nc)
    rows = cps * q

    def body(dacs_ref, ddt_ref, dt_ref, raw_ref, bias_ref, alog_ref, draw_ref, ga_ref, gb_ref):
        @pl.when(pl.program_id(0) == 0)
        def _():
            ga_ref[...] = jnp.zeros_like(ga_ref)
            gb_ref[...] = jnp.zeros_like(gb_ref)

        lane = lax.broadcasted_iota(jnp.int32, (q, LANES), 1)
        ri = lax.broadcasted_iota(jnp.int32, (q, q), 0)
        ci = lax.broadcasted_iota(jnp.int32, (q, q), 1)
        triu = (ri <= ci).astype(F32)
        a = -jnp.exp(alog_ref[...])
        used = (lane & (GROUPS - 1)) < HPG
        ga = jnp.zeros((1, LANES), F32)
        gb = jnp.zeros((1, LANES), F32)
        for k in range(cps):
            rk = slice(q * k, q * (k + 1))
            dacs = jnp.zeros((q, LANES), F32)
            ddt = jnp.zeros((q, LANES), F32)
            for g in range(GROUPS):
                mask = (lane >= GROUPS * g) & (lane < GROUPS * g + HPG)
                sl = slice(LANES * g, LANES * (g + 1))
                if g == 0:
                    dacs = jnp.where(mask, dacs_ref[rk, sl], dacs)
                    ddt = jnp.where(mask, ddt_ref[rk, sl], ddt)
                else:
                    dacs = jnp.where(mask, pltpu.roll(dacs_ref[rk, sl], GROUPS * g, 1), dacs)
                    ddt = jnp.where(mask, pltpu.roll(ddt_ref[rk, sl], GROUPS * g, 1), ddt)
            dda = jnp.dot(triu, dacs, preferred_element_type=F32, precision=HI)
            row = pl.program_id(0) * rows + q * k + lax.broadcasted_iota(jnp.int32, (q, LANES), 0)
            dsp = jnp.where((row >= PAD_LEAD) & used, dda * a + ddt, 0.0)
            draw = dsp * _sigmoid(raw_ref[rk, :] + bias_ref[...])
            draw_ref[rk, :] = _bf(draw)
            gb = gb + jnp.sum(draw, axis=0, keepdims=True)
            ga = ga + jnp.sum(jnp.where(used, dda * dt_ref[rk, :], 0.0), axis=0, keepdims=True)
        gb_ref[0:1, :] += gb
        ga_ref[0:1, :] += ga * a

    return _pallas(
        body, name="dt_bwd", grid=(nc // cps,),
        in_specs=[pl.BlockSpec((rows, GROUPS * LANES), lambda c: (c, 0)),
                  pl.BlockSpec((rows, GROUPS * LANES), lambda c: (c, 0)),
                  pl.BlockSpec((rows, LANES), lambda c: (c, 0)), pl.BlockSpec((rows, LANES), lambda c: (c, ODT // LANES)),
                  pl.BlockSpec((1, LANES), lambda c: (0, 0)), pl.BlockSpec((1, LANES), lambda c: (0, 0))],
        out_specs=[pl.BlockSpec((rows, LANES), lambda c: (c, 0)), pl.BlockSpec((8, LANES), lambda c: (0, 0)),
                   pl.BlockSpec((8, LANES), lambda c: (0, 0))],
        out_shape=[jax.ShapeDtypeStruct((t, LANES), BF16), jax.ShapeDtypeStruct((8, LANES), F32),
                   jax.ShapeDtypeStruct((8, LANES), F32)],
        compiler_params=_cp("arbitrary"))(dacs_g, ddt_g, dt, proj, dt_bias_l, a_log_l)


def _conv_bwd(dseg, proj, conv_w, conv_b, col_off, name):
    t, width = dseg.shape
    tc = 128
    rt = _tile(t, 320)
    off_p = (OXS + col_off) // tc
    off_w = col_off // tc

    def body(d_ref, x_ref, w_ref, b_ref, dx_ref, gw_ref, gb_ref, xp, dup):
        xp[0:8, :] = jnp.zeros((8, tc), F32)
        xp[8:t + 8, :] = x_ref[...]
        dup[t:t + 8, :] = jnp.zeros((8, tc), F32)
        w = w_ref[...]
        bias = b_ref[...]

        def first(i, acc):
            r0 = pl.multiple_of(i * rt, 8)
            xs = [xp[pl.ds(r0 + 5 + k, rt), :] for k in range(CONV_WIDTH)]
            u = bias + w[3:4, :] * xs[3] + w[2:3, :] * xs[2] + w[1:2, :] * xs[1] + w[0:1, :] * xs[0]
            su = 0.5 + 0.5 * jnp.tanh(0.5 * u)
            du = d_ref[pl.ds(r0, rt), :] * (su * (1.0 + u * (1.0 - su)))
            dup[pl.ds(r0, rt), :] = du
            return tuple(acc[k] + jnp.sum(du * xs[k], axis=0, keepdims=True) for k in range(CONV_WIDTH)) + (
                acc[CONV_WIDTH] + jnp.sum(du, axis=0, keepdims=True),)

        zero = jnp.zeros((1, tc), F32)
        acc = lax.fori_loop(0, t // rt, first, (zero,) * (CONV_WIDTH + 1))
        gw_ref[...] = jnp.concatenate(acc[:CONV_WIDTH], axis=0)
        gb_ref[...] = acc[CONV_WIDTH]

        def second(i, carry):
            r0 = pl.multiple_of(i * rt, 16)
            dx_ref[pl.ds(r0, rt), :] = _bf(w[3:4, :] * dup[pl.ds(r0, rt), :] + w[2:3, :] * dup[pl.ds(r0 + 1, rt), :]
                                          + w[1:2, :] * dup[pl.ds(r0 + 2, rt), :] + w[0:1, :] * dup[pl.ds(r0 + 3, rt), :])
            return carry

        lax.fori_loop(0, t // rt, second, 0)

    return _pallas(
        body, name=name, grid=(width // tc,),
        in_specs=[pl.BlockSpec((t, tc), lambda j: (0, j)), pl.BlockSpec((t, tc), lambda j: (0, j + off_p)),
                  pl.BlockSpec((CONV_WIDTH, tc), lambda j: (0, j + off_w)), pl.BlockSpec((1, tc), lambda j: (0, j + off_w))],
        out_specs=[pl.BlockSpec((t, tc), lambda j: (0, j)), pl.BlockSpec((CONV_WIDTH, tc), lambda j: (0, j)),
                   pl.BlockSpec((1, tc), lambda j: (0, j))],
        out_shape=[jax.ShapeDtypeStruct((t, width), BF16), jax.ShapeDtypeStruct((CONV_WIDTH, width), F32),
                   jax.ShapeDtypeStruct((1, width), F32)],
        scratch_shapes=[pltpu.VMEM((t + 8, tc), F32), pltpu.VMEM((t + 8, tc), F32)],
        compiler_params=_cp("parallel"))(dseg, proj, conv_w, conv_b)


def _dinproj(segs, w_re, hpad, norm_w, dy_t, ga):
    t = segs[0].shape[0]
    d = hpad.shape[1]
    tm, tk = _tile(t, 416), SEG_TILE
    counts = [s.shape[1] // tk for s in segs]
    firsts = [sum(counts[:s]) for s in range(len(segs))]
    nk = sum(counts)
    assert nk * tk == w_re.shape[1]
    ni = t // tm
    ns = len(segs)

    def body(*refs):
        seg_refs = refs[:ns]
        w_ref, h_ref, nw_ref, dy_ref, ga_ref, dh_ref, gnw_ref, got_ref, acc, send_sems, recv_sems = refs[ns:]
        i, k = pl.program_id(0), pl.program_id(1)

        @pl.when((i == 0) & (k == 0))
        def _():
            for cp in _exchange_copies(ga_ref, got_ref, send_sems, recv_sems):
                cp.start()
            gnw_ref[...] = jnp.zeros_like(gnw_ref)

        @pl.when(k == 0)
        def _():
            acc[...] = jnp.zeros_like(acc)

        for s in range(ns):
            @pl.when((k >= firsts[s]) & (k < firsts[s] + counts[s]))
            def _(s=s):
                acc[...] += _nt(seg_refs[s][...], w_ref[...])

        @pl.when(k == nk - 1)
        def _():
            h = h_ref[...]
            rstd = lax.rsqrt(jnp.mean(h * h, axis=-1, keepdims=True) + EPS)
            nrm = h * rstd
            dhn = acc[...]
            gnw_ref[...] += jnp.sum(dhn * nrm, axis=0, keepdims=True)
            dn = dhn * nw_ref[...]
            dh_ref[...] = rstd * (dn - nrm * jnp.mean(dn * nrm, axis=-1, keepdims=True)) + dy_ref[...]

        @pl.when((i == ni - 1) & (k == nk - 1))
        def _():
            for cp in _exchange_copies(ga_ref, got_ref, send_sems, recv_sems):
                cp.wait()

    seg_specs = [pl.BlockSpec((tm, tk), functools.partial(lambda i, k, f0, n0: (i, jnp.clip(k - f0, 0, n0 - 1)),
                                                          f0=firsts[s], n0=counts[s])) for s in range(ns)]
    return _pallas(
        body, name="dinproj", grid=(ni, nk),
        in_specs=seg_specs + [pl.BlockSpec((d, tk), lambda i, k: (0, k)),
                              pl.BlockSpec((tm, d), lambda i, k: (i, 0)), pl.BlockSpec((1, d), lambda i, k: (0, 0)),
                              pl.BlockSpec((tm, d), lambda i, k: (i, 0)), ANY],
        out_specs=[pl.BlockSpec((tm, d), lambda i, k: (i, 0)), pl.BlockSpec((1, d), lambda i, k: (0, 0)), ANY],
        out_shape=[jax.ShapeDtypeStruct((t, d), F32), jax.ShapeDtypeStruct((1, d), F32), _exchange_shape(ga)],
        scratch_shapes=[pltpu.VMEM((tm, d), F32)] + _exchange_scratch(),
        compiler_params=_cp("arbitrary", "arbitrary"))(*segs, w_re, hpad, norm_w, dy_t, ga)


def _spread_heads(v):
    v = jnp.pad(v.reshape(GROUPS, HPG), ((0, 0), (0, GROUPS - HPG))).reshape(1, GROUPS * GROUPS)
    return jnp.pad(v, ((0, 0), (0, LANES - GROUPS * GROUPS)))


def _gather_heads(v):
    return v[0:1, :GROUPS * GROUPS].reshape(GROUPS, GROUPS)[:, :HPG].reshape(1, SSD_HEADS)


def _rope_tables(t):
    half = HEAD_DIM // 2
    inv = ROPE_THETA ** (-jnp.arange(half, dtype=F32) / half)
    pos = (jnp.arange(t) - PAD_LEAD).astype(F32)
    ang = pos[:, None] * inv[None, :]
    cos, sin = jnp.cos(ang), jnp.sin(ang)
    cos_t = jnp.concatenate([cos, cos, cos, cos], axis=1)
    sin_t = jnp.concatenate([-sin, sin, -sin, sin], axis=1)
    return cos_t, sin_t


def _column_pieces():
    runs = [(0, OB + 2 * GROUPS * D_STATE, 0)]
    o = OB + 2 * GROUPS * D_STATE
    runs += [(o + HPG * g, HPG, ODT + GROUPS * g) for g in range(GROUPS)]
    o += SSD_HEADS
    for width, dst in ((D_ATT, OQ), (D_KV, OK), (D_KV, OV), (D_ATT, OG)):
        runs.append((o, width, dst))
        o += width
    assert o == D_IN
    pieces = []
    for o0, width, dst in runs:
        for j in range(N_SHARD):
            lo, hi = max(o0, W_IN_SHARD * j), min(o0 + width, W_IN_SHARD * (j + 1))
            if lo < hi:
                pieces.append((j, lo - W_IN_SHARD * j, hi - W_IN_SHARD * j, dst + lo - o0))
    return pieces


def _shards_to_re(w_all):
    _, k, _ = w_all.shape
    tr = 256

    def body(x_ref, o_ref):
        o_ref[:, ODT:ODT + DT_SLAB] = jnp.zeros((tr, DT_SLAB), o_ref.dtype)
        for j, c0, c1, d0 in _column_pieces():
            o_ref[:, d0:d0 + c1 - c0] = x_ref[j, :, c0:c1]

    return _pallas(body, name="shards_to_re", grid=(k // tr,),
                   in_specs=[pl.BlockSpec((N_SHARD, tr, W_IN_SHARD), lambda i: (0, i, 0))],
                   out_specs=pl.BlockSpec((tr, N_RE), lambda i: (i, 0)),
                   out_shape=jax.ShapeDtypeStruct((k, N_RE), w_all.dtype), compiler_params=_cp("parallel"))(w_all)


def _pair_add_to_shards(parts, got, pieces, shard_rows, core, name):
    n = parts[0].shape[1]
    hn = n // 2
    tc = 128
    nt = hn // tc
    ns = len(parts)
    starts = [sum(p.shape[0] for p in parts[:s]) for s in range(ns)]
    moves = []
    for j, c0, c1, d0 in pieces:
        for s, p in enumerate(parts):
            lo, hi = max(d0, starts[s]), min(d0 + c1 - c0, starts[s] + p.shape[0])
            if lo < hi:
                moves.append((s, lo - starts[s], j, c0 + lo - d0, hi - lo))
    assert sum(m[4] for m in moves) == N_SHARD * shard_rows

    def body(core_ref, *refs):
        own, theirs, o_ref, acc = refs[:ns], refs[ns:2 * ns], refs[2 * ns], refs[2 * ns + 1]
        for s, r0, j, c0, rows in moves:
            acc[j, c0:c0 + rows, :] = own[s][r0:r0 + rows, :] + theirs[s][r0:r0 + rows, :]
        o_ref[...] = _bf(acc[...])

    return _pallas(
        body, name=name,
        grid_spec=pltpu.PrefetchScalarGridSpec(
            num_scalar_prefetch=1, grid=(nt,),
            in_specs=[pl.BlockSpec((p.shape[0], tc), lambda i, core_ref: (0, core_ref[0] * nt + i)) for p in parts]
            + [pl.BlockSpec((p.shape[0], tc), lambda i, core_ref: (0, i)) for p in parts],
            out_specs=pl.BlockSpec((N_SHARD, shard_rows, tc), lambda i, core_ref: (0, 0, i)),
            scratch_shapes=[pltpu.VMEM((N_SHARD, shard_rows, tc), F32)]),
        out_shape=jax.ShapeDtypeStruct((N_SHARD, shard_rows, hn), BF16),
        compiler_params=_cp("parallel"))(core, *parts, *got)


def _local_step(x, target, meta, norm_pre_w, w_re, conv_w, conv_b, dt_bias, a_log, d_skip, ssd_norm_w, sinks,
                w_out_shard, norm_post_w, place):
    seq = x.shape[0]
    t = PAD_LEAD + N_META + seq
    hpad = jnp.concatenate([jnp.zeros((PAD_LEAD, D_MODEL), F32), meta, x], axis=0)
    dt_bias_l, a_log_l, d_skip_l = _spread_heads(dt_bias), _spread_heads(a_log), _spread_heads(d_skip)
    cos_t, sin_t = _rope_tables(t)
    sink_v = sinks.reshape(Q_HEADS)

    proj, hn, w_out_all = _inproj(hpad, norm_pre_w, w_re, w_out_shard)
    w_out = w_out_all.reshape(D_MIX, D_MODEL)
    xbc = _conv_fwd(proj, conv_w, conv_b)
    dt, acs, acst = _dt_prep(proj, dt_bias_l, a_log_l)
    y_ssd, ymix, states = _ssd_fwd(xbc, proj, dt, acs, acst, d_skip_l, ssd_norm_w)
    qr, kr = _rope(proj, OQ, proj, OK, cos_t, sin_t)
    amix = _attn_fwd(qr, kr, proj, sink_v)
    out = _outproj(ymix, amix, w_out)
    dout, dy_t, loss_blk, g_norm_post = _post_loss(out, x, target, norm_post_w)

    g_out_y = _tn_matmul(ymix, dout, "gw_out_y")
    g_out_a, got_y = _tn_matmul(amix, dout, "gw_out_a", carry=g_out_y)
    dmix, got_a = _nt_matmul(dout, w_out, "dmix", carry=g_out_a)
    ga_out = _reduce_pair([g_out_y, g_out_a], [got_y, got_a], [(j, 0, W_OUT_SHARD, W_OUT_SHARD * j) for j in range(N_SHARD)],
                          W_OUT_SHARD, place, "gw_out")
    dq_r, dg, dk_r, dv, gs, slabs_out = _attn_bwd(qr, kr, proj, dmix, sink_v, ga_out)
    g_w_out = _reduce_finish(ga_out, slabs_out, place, "gw_out")
    dq, dk = _rope(dq_r, 0, dk_r, 0, cos_t, -sin_t)
    dz, dxs, db, dc, dacs_g, ddt_g, g_ssd_norm, gdsk = _ssd_bwd(dmix, y_ssd, xbc, proj, dt, acs, acst, states,
                                                                d_skip_l, ssd_norm_w)
    draw, ga, gb = _dt_bwd(dacs_g, ddt_g, dt, proj, dt_bias_l, a_log_l)
    dxs_p, gcw0, gcb0 = _conv_bwd(dxs, proj, conv_w, conv_b, 0, "conv_bwd_x")
    db_p, gcw1, gcb1 = _conv_bwd(db, proj, conv_w, conv_b, D_SSD, "conv_bwd_b")
    dc_p, gcw2, gcb2 = _conv_bwd(dc, proj, conv_w, conv_b, D_SSD + GROUPS * D_STATE, "conv_bwd_c")
    tail = jnp.concatenate([dk, _bf(dv), draw, jnp.zeros((t, DT_SLAB - LANES), BF16)], axis=1)
    segs = [dz, dxs_p, db_p, dc_p, dq, dg, tail]
    g_parts, got_parts = [_tn_matmul(segs[0], hn, "gw_in_0")], []
    for s in range(1, len(segs)):
        part, got = _tn_matmul(segs[s], hn, "gw_in_%d" % s, carry=g_parts[-1])
        g_parts.append(part)
        got_parts.append(got)
    ga_in = _reduce_pair(g_parts, got_parts, _column_pieces(), W_IN_SHARD, place, "gw_in")
    dh, g_norm_pre, slabs_in = _dinproj(segs, w_re, hpad, norm_pre_w, dy_t, ga_in)
    g_w_in_half = _chip_sum(ga_in, slabs_in, place, "gw_in_chip_sum")

    gdsk_l = jnp.concatenate([gdsk[g, 0:1, 0:GROUPS] for g in range(GROUPS)], axis=1)
    gdsk_l = jnp.pad(gdsk_l, ((0, 0), (0, LANES - GROUPS * GROUPS)))
    grads = dict(
        meta_tokens=dh[PAD_LEAD:ROW0], norm_pre_w=g_norm_pre, w_in_half=g_w_in_half,
        conv_w=jnp.concatenate([gcw0, gcw1, gcw2], axis=1), conv_b=jnp.concatenate([gcb0, gcb1, gcb2], axis=1),
        dt_bias=_gather_heads(gb), a_log=_gather_heads(ga), d_skip=_gather_heads(gdsk_l), ssd_norm_w=g_ssd_norm,
        attn_sinks=gs[0:1, :Q_HEADS], w_out=g_w_out, norm_post_w=g_norm_post)
    return loss_blk[0, 0], dh[ROW0:], grads


ANY = pl.BlockSpec(memory_space=pl.ANY)
MESH = pl.DeviceIdType.MESH
GATHER_CHUNKS = 4
PAIR_CHUNKS = 8
JOIN_CHUNKS = 8


def _rcopy(src, dst, ssem, rsem, dev):
    return pltpu.make_async_remote_copy(src_ref=src, dst_ref=dst, send_sem=ssem, recv_sem=rsem, device_id=dev,
                                        device_id_type=MESH)


def _place():
    x, y, c = lax.axis_index("x"), lax.axis_index("y"), lax.axis_index("c")
    chips = [(1 - x, y), (x, 1 - y), (1 - x, 1 - y)]
    return x, y, c, chips


def _gather_plan(x_ref, out_ref, send_sems, recv_sems, local_sems, hr, kc):
    ch = hr // kc
    assert ch * kc == hr and ch % 16 == 0
    x, y, c, chips = _place()
    me = 2 * x + y
    sibling = (x, y, 1 - c)

    def piece(chip, hc, k):
        return out_ref.at[chip, pl.ds(hc * hr + k * ch, ch), :]

    def local():
        return [pltpu.make_async_copy(x_ref.at[pl.ds(k * ch, ch), :], out_ref.at[me, pl.ds(k * ch, ch), :],
                                      local_sems.at[k]) for k in range(2 * kc)]

    def first():
        return [_rcopy(x_ref.at[pl.ds(c * hr + k * ch, ch), :], piece(me, c, k), send_sems.at[j * kc + k],
                       recv_sems.at[j * kc + k], (*chip, c)) for j, chip in enumerate(chips) for k in range(kc)]

    def passed(hc):
        return [_rcopy(piece(2 * chip[0] + chip[1], hc, k), piece(2 * chip[0] + chip[1], hc, k),
                       send_sems.at[(3 + j) * kc + k], recv_sems.at[(3 + j) * kc + k], sibling)
                for j, chip in enumerate(chips) for k in range(kc)]

    def arrivals():
        return [_rcopy(piece(2 * chip[0] + chip[1], c, k), piece(2 * chip[0] + chip[1], c, k), send_sems.at[j * kc + k],
                       recv_sems.at[j * kc + k], (*chip, c)) for j, chip in enumerate(chips) for k in range(kc)]

    def start():
        for cp in local() + first():
            cp.start()

    def forward():
        for arrived in arrivals():
            arrived.wait_recv()
        for fw in passed(c):
            fw.start()

    def finish():
        for cp in passed(1 - c):
            cp.wait_recv()
        for cp in first() + passed(c):
            cp.wait_send()
        for cp in local():
            cp.wait()

    return start, forward, finish


def _gather_shards(shard, name, kc, chip, small):
    r, n = shard.shape
    hr = r // 2
    qr = hr // 2
    ch = qr // kc
    assert ch * kc == qr and ch % 16 == 0
    nflow = 12
    tr = 256

    def body(x_ref, p_ref, out_ref, slots_ref, send_sems, recv_sems, *small_sems):
        start_small, wait_small = _chip_small_exchange(p_ref, slots_ref, *small_sems)
        start_small()
        x, y, c, _ = _place()
        me, cxn, cyn, cdg = 2 * x + y, 2 * (1 - x) + y, 2 * x + 1 - y, 2 * (1 - x) + 1 - y
        xn, yn, sibling = (1 - x, y, c), (x, 1 - y, c), (x, y, 1 - c)

        def piece(chip, hc, part, k):
            return out_ref.at[chip, pl.ds(hc * hr + part * qr + k * ch, ch), :]

        def own(part, k):
            return x_ref.at[pl.ds(c * hr + part * qr + k * ch, ch), :]

        def sems(flow, k):
            return send_sems.at[flow * kc + k], recv_sems.at[flow * kc + k]

        def arrival(flow, chip, hc, part, k):
            return _rcopy(piece(chip, hc, part, k), piece(chip, hc, part, k), *sems(flow, k), sibling)

        sends = []
        for flow, part, peer in ((0, 0, xn), (1, 1, yn), (2, 0, yn), (3, 1, xn)):
            sends += [_rcopy(own(part, k), piece(me, c, part, k), *sems(flow, k), peer) for k in range(kc)]
        for cp in sends:
            cp.start()
        landing = ((0, cxn, 0), (1, cyn, 1), (2, cyn, 0), (3, cxn, 1), (4, cdg, 0), (5, cdg, 1))
        for i, (flow, chip, part) in enumerate(landing):
            for k in range(kc):
                arrival(flow, chip, c, part, k).wait_recv()
                if flow < 2:
                    on = _rcopy(piece(chip, c, part, k), piece(chip, c, part, k), *sems(4 + flow, k),
                                yn if flow == 0 else xn)
                    on.start()
                    sends.append(on)
                fw = _rcopy(piece(chip, c, part, k), piece(chip, c, part, k), *sems(6 + i, k), sibling)
                fw.start()
                sends.append(fw)
        for i, (flow, chip, part) in enumerate(landing):
            for k in range(kc):
                arrival(6 + i, chip, 1 - c, part, k).wait_recv()
        for cp in sends:
            cp.wait_send()
        wait_small()

    full = jax.ShapeDtypeStruct((N_SHARD, r, n), shard.dtype)
    others, slots = _pallas(
        body, name=name, in_specs=[ANY, ANY], out_specs=[ANY, ANY],
        out_shape=[full, jax.ShapeDtypeStruct((N_SHARD,) + small.shape, F32)],
        scratch_shapes=[pltpu.SemaphoreType.DMA((nflow * kc,)), pltpu.SemaphoreType.DMA((nflow * kc,)),
                        pltpu.SemaphoreType.DMA((3,)), pltpu.SemaphoreType.DMA((3,)), pltpu.SemaphoreType.DMA])(
                            shard, small)

    def place(chip_ref, own_ref, all_ref, o_ref):
        o_ref[0] = own_ref[...]

    gathered = _pallas(
        place, name=name + "_own",
        grid_spec=pltpu.PrefetchScalarGridSpec(
            num_scalar_prefetch=1, grid=(r // tr,),
            in_specs=[pl.BlockSpec((tr, n), lambda i, chip_ref: (i, 0)), ANY],
            out_specs=pl.BlockSpec((1, tr, n), lambda i, chip_ref: (chip_ref[0], i, 0))),
        out_shape=full, input_output_aliases={2: 0}, compiler_params=_cp("parallel"))(chip, shard, others)
    return gathered, slots


def _pair_copies(src_ref, dst_ref, send_sems, recv_sems):
    hn = src_ref.shape[1] // 2
    cw = hn // PAIR_CHUNKS
    assert cw * PAIR_CHUNKS == hn and cw % LANES == 0
    x, y, c, _ = _place()
    return [_rcopy(src_ref.at[:, pl.ds((1 - c) * hn + k * cw, cw)], dst_ref.at[:, pl.ds(k * cw, cw)],
                   send_sems.at[k], recv_sems.at[k], (x, y, 1 - c)) for k in range(PAIR_CHUNKS)]


def _pair_send(parts, name):
    n = parts[0].shape[1]
    hn = n // 2
    kc = PAIR_CHUNKS
    cw = hn // kc
    assert cw * kc == hn and cw % LANES == 0
    ns = len(parts)

    def body(*refs):
        srcs, dsts, send_sems, recv_sems = refs[:ns], refs[ns:2 * ns], refs[2 * ns], refs[2 * ns + 1]
        x, y, c, _ = _place()
        cps = [_rcopy(srcs[s].at[:, pl.ds((1 - c) * hn + k * cw, cw)], dsts[s].at[:, pl.ds(k * cw, cw)],
                      send_sems.at[s * kc + k], recv_sems.at[s * kc + k], (x, y, 1 - c))
               for s in range(ns) for k in range(kc)]
        for cp in cps:
            cp.start()
        for cp in cps:
            cp.wait()

    return _pallas(
        body, name=name, in_specs=[ANY] * ns, out_specs=[ANY] * ns,
        out_shape=[jax.ShapeDtypeStruct((p.shape[0], hn), F32) for p in parts],
        scratch_shapes=[pltpu.SemaphoreType.DMA((ns * kc,)), pltpu.SemaphoreType.DMA((ns * kc,))])(*parts)


REDUCE_TILE = 256


def _exchange_copies(g_ref, got_ref, send_sems, recv_sems):
    hn = g_ref.shape[2]
    kc = GATHER_CHUNKS
    cw = hn // kc
    assert cw * kc == hn and cw % LANES == 0
    x, y, c, chips = _place()
    return [_rcopy(g_ref.at[2 * chip[0] + chip[1], :, pl.ds(k * cw, cw)], got_ref.at[j, :, pl.ds(k * cw, cw)],
                   send_sems.at[j * kc + k], recv_sems.at[j * kc + k], (*chip, c))
            for j, chip in enumerate(chips) for k in range(kc)]


def _exchange_scratch():
    return [pltpu.SemaphoreType.DMA((3 * GATHER_CHUNKS,)), pltpu.SemaphoreType.DMA((3 * GATHER_CHUNKS,))]


def _exchange_shape(ga):
    return jax.ShapeDtypeStruct((3,) + ga.shape[1:], ga.dtype)


def _chip_sum(ga, got, place, name):
    _, r, hn = ga.shape
    tc = REDUCE_TILE
    nt = hn // tc

    def body(place_ref, own_ref, got_ref, o_ref):
        acc = own_ref[0].astype(F32)
        for j in range(3):
            acc = acc + got_ref[j].astype(F32)
        o_ref[...] = acc

    return _pallas(
        body, name=name,
        grid_spec=pltpu.PrefetchScalarGridSpec(
            num_scalar_prefetch=1, grid=(nt,),
            in_specs=[pl.BlockSpec((1, r, tc), lambda i, place_ref: (place_ref[0], 0, i)),
                      pl.BlockSpec((3, r, tc), lambda i, place_ref: (0, 0, i))],
            out_specs=pl.BlockSpec((r, tc), lambda i, place_ref: (0, place_ref[1] * nt + i))),
        out_shape=jax.ShapeDtypeStruct((r, 2 * hn), F32), compiler_params=_cp("parallel"))(place, ga, got)


def _pair_join(buf, name, small=None):
    r, n = buf.shape
    hn = n // 2
    kc = JOIN_CHUNKS
    cw = hn // kc
    assert cw * kc == hn and cw % LANES == 0

    def body(in_ref, *refs):
        if small is None:
            out_ref, send_sems, recv_sems = refs
        else:
            p_ref, out_ref, slots_ref, send_sems, recv_sems = refs[:5]
            start_small, wait_small = _small_exchange(p_ref, slots_ref, *refs[5:])
            start_small()
        x, y, c, _ = _place()
        cps = [_rcopy(out_ref.at[:, pl.ds(c * hn + k * cw, cw)], out_ref.at[:, pl.ds(c * hn + k * cw, cw)],
                      send_sems.at[k], recv_sems.at[k], (x, y, 1 - c)) for k in range(kc)]
        for cp in cps:
            cp.start()
        for k in range(kc):
            cols = out_ref.at[:, pl.ds((1 - c) * hn + k * cw, cw)]
            _rcopy(cols, cols, send_sems.at[k], recv_sems.at[k], (x, y, 1 - c)).wait_recv()
        for cp in cps:
            cp.wait_send()
        if small is not None:
            wait_small()

    sems = [pltpu.SemaphoreType.DMA((kc,)), pltpu.SemaphoreType.DMA((kc,))]
    if small is None:
        return _pallas(body, name=name, in_specs=[ANY], out_specs=ANY, out_shape=jax.ShapeDtypeStruct((r, n), F32),
                       input_output_aliases={0: 0}, scratch_shapes=sems)(buf)
    return _pallas(
        body, name=name, in_specs=[ANY, ANY], out_specs=[ANY, ANY],
        out_shape=[jax.ShapeDtypeStruct((r, n), F32), jax.ShapeDtypeStruct((N_DEV,) + small.shape, F32)],
        input_output_aliases={0: 0}, scratch_shapes=sems + _small_scratch())(buf, small)


def _reduce_pair(parts, got, pieces, shard_rows, place, tag):
    if len(got) < len(parts):
        got = list(got) + list(_pair_send(parts[len(got):], tag + "_pair_send"))
    return _pair_add_to_shards(parts, got, pieces, shard_rows, place[1:2], tag + "_pair_add")


def _reduce_finish(ga, slabs, place, tag):
    return _pair_join(_chip_sum(ga, slabs, place, tag + "_chip_sum"), tag + "_pair_join")


N_DEV = 8


def _small_exchange(p_ref, slots_ref, send_sems, recv_sems, local_sem):
    x, y, c, _ = _place()
    my = 4 * x + 2 * y + c

    def sends():
        return [_rcopy(p_ref, slots_ref.at[my], send_sems.at[k - 1], recv_sems.at[k - 1],
                       (x ^ ((k >> 2) & 1), y ^ ((k >> 1) & 1), c ^ (k & 1))) for k in range(1, N_DEV)]

    def local():
        return pltpu.make_async_copy(p_ref, slots_ref.at[my], local_sem)

    def start():
        local().start()
        for cp in sends():
            cp.start()

    def wait():
        for k in range(1, N_DEV):
            _rcopy(p_ref, slots_ref.at[my ^ k], send_sems.at[k - 1], recv_sems.at[k - 1], (x, y, c)).wait_recv()
        for cp in sends():
            cp.wait_send()
        local().wait()

    return start, wait


def _chip_small_exchange(p_ref, slots_ref, send_sems, recv_sems, local_sem):
    x, y, c, chips = _place()
    me = 2 * x + y

    def sends():
        return [_rcopy(p_ref, slots_ref.at[me], send_sems.at[j], recv_sems.at[j], (*chip, c))
                for j, chip in enumerate(chips)]

    def local():
        return pltpu.make_async_copy(p_ref, slots_ref.at[me], local_sem)

    def start():
        local().start()
        for cp in sends():
            cp.start()

    def wait():
        for j, chip in enumerate(chips):
            slot = slots_ref.at[2 * chip[0] + chip[1]]
            _rcopy(slot, slot, send_sems.at[j], recv_sems.at[j], (*chip, c)).wait_recv()
        for cp in sends():
            cp.wait_send()
        local().wait()

    return start, wait


def _small_scratch():
    return [pltpu.SemaphoreType.DMA((N_DEV - 1,)), pltpu.SemaphoreType.DMA((N_DEV - 1,)), pltpu.SemaphoreType.DMA]


def _sum_slots(slots, name):
    _, rows, n = slots.shape

    def body(s_ref, o_ref):
        acc = s_ref[0]
        for j in range(1, N_DEV):
            acc = acc + s_ref[j]
        o_ref[...] = acc

    vm = pl.BlockSpec(memory_space=pltpu.VMEM)
    return _pallas(body, name=name, in_specs=[vm], out_specs=vm, out_shape=jax.ShapeDtypeStruct((rows, n), F32))(slots)


def _adamw(w, g, m, v, name):
    r, n = w.shape
    tr = _tile(r, 256, 8)
    c1 = 1.0 / (1.0 - ADAM_B1 ** ADAM_STEP)
    c2 = 1.0 / (1.0 - ADAM_B2 ** ADAM_STEP)

    def body(w_ref, g_ref, m_ref, v_ref, d_ref, mo_ref, vo_ref, go_ref):
        gv = g_ref[...]
        mn = ADAM_B1 * m_ref[...] + (1.0 - ADAM_B1) * gv
        vn = ADAM_B2 * v_ref[...] + (1.0 - ADAM_B2) * (gv * gv)
        d_ref[...] = -ADAM_LR * ((mn * c1) / (jnp.sqrt(vn * c2) + ADAM_EPS) + ADAM_WD * w_ref[...])
        mo_ref[...] = mn
        vo_ref[...] = vn
        go_ref[...] = gv

    spec = pl.BlockSpec((tr, n), lambda i: (i, 0))
    shp = jax.ShapeDtypeStruct((r, n), F32)
    return _pallas(body, name=name, grid=(r // tr,), in_specs=[spec] * 4, out_specs=[spec] * 4, out_shape=[shp] * 4,
                   compiler_params=_cp("parallel"))(w, g, m, v)


PACK_W = 1024
SMALL_REPL = ("norm_pre_w", "conv_b", "ssd_norm_w", "norm_post_w")
SMALL_HEAD = ("dt_bias", "a_log", "d_skip", "attn_sinks")


def _rows(a):
    return a.reshape(-1, PACK_W)


def _head_row(vals, extra=None):
    parts = [vals[n].reshape(1, -1) for n in SMALL_HEAD]
    if extra is not None:
        parts.append(extra.reshape(1, 1))
    row = jnp.concatenate(parts, axis=1)
    return jnp.pad(row, ((0, 0), (0, PACK_W - row.shape[1])))


def _pad_rows(a, rows):
    return jnp.pad(a, ((0, rows - a.shape[0]), (0, 0)))


def _pack_repl(vals, extra=None):
    body = jnp.concatenate([_rows(vals[n]) for n in SMALL_REPL] + [_head_row(vals, extra)], axis=0)
    return _pad_rows(body, 16)


def _unpack_repl(buf):
    out, r = {}, 0
    for n, k in zip(SMALL_REPL, (2, 4, 2, 2)):
        out[n] = buf[r:r + k].reshape(1, k * PACK_W)
        r += k
    col = 0
    for n, k in zip(SMALL_HEAD, (32, 32, 32, 16)):
        out[n] = buf[r:r + 1, col:col + k]
        col += k
    return out, buf[r, col]


def kernel(x, meta_tokens, norm_pre_w, w_in, conv_w, conv_b, dt_bias, a_log, d_skip, ssd_norm_w, attn_sinks, w_out, norm_post_w, loss_target, m_meta_tokens, m_norm_pre_w, m_w_in, m_conv_w, m_conv_b, m_dt_bias, m_a_log, m_d_skip, m_ssd_norm_w, m_attn_sinks, m_w_out, m_norm_post_w, v_meta_tokens, v_norm_pre_w, v_w_in, v_conv_w, v_conv_b, v_dt_bias, v_a_log, v_d_skip, v_ssd_norm_w, v_attn_sinks, v_w_out, v_norm_post_w):
    names = ("meta_tokens", "norm_pre_w", "w_in", "conv_w", "conv_b", "dt_bias", "a_log", "d_skip", "ssd_norm_w",
             "attn_sinks", "w_out", "norm_post_w")
    w = dict(zip(names, (meta_tokens, norm_pre_w, w_in, conv_w, conv_b, dt_bias, a_log, d_skip, ssd_norm_w, attn_sinks,
                         w_out, norm_post_w)))
    m = dict(zip(names, (m_meta_tokens, m_norm_pre_w, m_w_in, m_conv_w, m_conv_b, m_dt_bias, m_a_log, m_d_skip,
                         m_ssd_norm_w, m_attn_sinks, m_w_out, m_norm_post_w)))
    v = dict(zip(names, (v_meta_tokens, v_norm_pre_w, v_w_in, v_conv_w, v_conv_b, v_dt_bias, v_a_log, v_d_skip,
                         v_ssd_norm_w, v_attn_sinks, v_w_out, v_norm_post_w)))
    cx, cy, cc = lax.axis_index("x"), lax.axis_index("y"), lax.axis_index("c")
    chip = 2 * cx + cy
    meta_cols = D_MODEL // N_SHARD
    conv_cols = D_CONV // N_SHARD

    place = jnp.stack([chip, cc]).astype(jnp.int32)
    small = jnp.concatenate([_pad_rows(conv_w[0], 8), _rows(meta_tokens)], axis=0)
    w_in_all, small_all = _gather_shards(_bf(w_in[0]), "gather_w_in", GATHER_CHUNKS, place[0:1], small)
    w_re = _shards_to_re(w_in_all)
    conv_full = jnp.transpose(small_all[:, 0:CONV_WIDTH], (1, 0, 2)).reshape(CONV_WIDTH, D_CONV)
    meta_full = jnp.transpose(small_all[:, 8:16].reshape(N_SHARD, N_META, meta_cols), (1, 0, 2)).reshape(N_META, D_MODEL)

    loss_dev, grad_x, g = _local_step(x[0], loss_target[0], meta_full, norm_pre_w, w_re, conv_full, conv_b, dt_bias,
                                      a_log, d_skip, ssd_norm_w, attn_sinks, _bf(w_out[0]), norm_post_w, place)
    g_w_out = g["w_out"]

    packed = jnp.concatenate([_rows(g["conv_w"]), _rows(g["meta_tokens"]), _pack_repl(g, loss_dev)], axis=0)
    g_w_in, slots = _pair_join(g["w_in_half"], "gw_in_pair_join", small=packed)
    red = _sum_slots(slots, "reduce_small")
    g_conv_full = red[0:16].reshape(CONV_WIDTH, D_CONV)
    g_meta_full = red[16:48].reshape(N_META, D_MODEL)
    g_small, loss = _unpack_repl(red[48:64])
    grads = dict(g_small)
    grads["w_in"] = g_w_in
    grads["w_out"] = g_w_out
    grads["conv_w"] = lax.dynamic_slice(g_conv_full, (0, chip * conv_cols), (CONV_WIDTH, conv_cols))
    grads["meta_tokens"] = lax.dynamic_slice(g_meta_full, (0, chip * meta_cols), (N_META, meta_cols))

    upd = {}
    upd["w_in"] = [jnp.swapaxes(a, 0, 1) for a in _adamw(jnp.swapaxes(w_in[0], 0, 1), g_w_in, jnp.swapaxes(m_w_in[0], 0, 1),
                                                         jnp.swapaxes(v_w_in[0], 0, 1), "adamw_w_in")]
    grads["w_in"] = upd["w_in"][3]
    upd["w_out"] = _adamw(w_out[0], g_w_out, m_w_out[0], v_w_out[0], "adamw_w_out")
    grads["w_out"] = upd["w_out"][3]

    def pack_small(vals, conv, meta):
        return jnp.concatenate([_pad_rows(conv.reshape(CONV_WIDTH, conv_cols), 8), _rows(meta), _pack_repl(vals)], axis=0)

    sm = _adamw(pack_small(w, w["conv_w"], w["meta_tokens"]), pack_small(grads, grads["conv_w"], grads["meta_tokens"]),
                pack_small(m, m["conv_w"], m["meta_tokens"]), pack_small(v, v["conv_w"], v["meta_tokens"]),
                "adamw_small")
    for n in names:
        if n not in ("w_in", "w_out"):
            upd[n] = [None, None, None]
    for k, buf in enumerate(sm[:3]):
        upd["conv_w"][k] = buf[0:CONV_WIDTH]
        upd["meta_tokens"][k] = buf[8:16].reshape(N_META, meta_cols)
        rest, _ = _unpack_repl(buf[16:32])
        for n in SMALL_REPL + SMALL_HEAD:
            upd[n][k] = rest[n]

    def shaped(n, a):
        return a.reshape(w[n].shape)

    outs = [loss, grad_x[None]]
    outs += [shaped(n, grads[n]) for n in names]
    for k in range(3):
        outs += [shaped(n, upd[n][k]) for n in names]
    return tuple(outs)
```

```python
import functools

import jax
import jax.numpy as jnp
from jax import lax
from jax.experimental import pallas as pl
from jax.experimental.pallas import tpu as pltpu

F32 = jnp.float32
BF16 = jnp.bfloat16

D_MODEL = 2048
CHUNK = 64
N_META = 16
PAD_LEAD = CHUNK - N_META
ROW0 = PAD_LEAD + N_META
EPS = 1e-6
SSD_HEADS = 32
HEAD_DIM = 64
GROUPS = 8
HPG = SSD_HEADS // GROUPS
D_STATE = 128
D_SSD = 2048
GROUP_W = D_SSD // GROUPS
CONV_WIDTH = 4
D_CONV = 4096
Q_HEADS = 16
KV_HEADS = 4
REP = Q_HEADS // KV_HEADS
D_ATT = 1024
D_KV = 256
BAND_CHUNKS = 3
ROPE_THETA = 10000.0
D_MIX = D_SSD + D_ATT
D_IN = 8736
N_SHARD = 4
W_IN_SHARD = D_IN // N_SHARD
W_OUT_SHARD = D_MIX // N_SHARD

OZ, OXS, OB, OC, OQ, OG, OK, OV, ODT = 0, 2048, 4096, 5120, 6144, 7168, 8192, 8448, 8704
DT_SLAB = 512
N_RE = ODT + DT_SLAB
LANES = 128

ADAM_LR, ADAM_B1, ADAM_B2, ADAM_EPS, ADAM_WD, ADAM_STEP = 0.001, 0.9, 0.999, 1e-08, 0.01, 10

SSD_FWD_GROUPS_PER_STEP = 4
SSD_BWD_GROUPS_PER_STEP = 8
SEG_TILE = 1024
VMEM_LIMIT = 52 * 1024 * 1024
NEG = -1e30
HI = lax.Precision.HIGHEST


def _pallas(body, **kw):
    return pl.pallas_call(body, **kw)


def _cp(*sem):
    return pltpu.CompilerParams(dimension_semantics=sem, vmem_limit_bytes=VMEM_LIMIT)


def _tile(n, cap, mult=16):
    best = None
    for d in range(mult, min(n, cap) + 1, mult):
        if n % d == 0:
            best = d
    assert best is not None, (n, cap)
    return best


def _nt(a, b):
    return lax.dot_general(a, b, (((1,), (1,)), ((), ())), preferred_element_type=F32)


def _tn(a, b):
    return lax.dot_general(a, b, (((0,), (0,)), ((), ())), preferred_element_type=F32)


def _mm(a, b):
    return jnp.dot(a, b, preferred_element_type=F32)


def _sigmoid(x):
    return 1.0 / (1.0 + jnp.exp(-x))


def _bf(x):
    return x.astype(BF16)


def _inproj(hpad, norm_w, w_re, w_out_shard):
    t, d = hpad.shape
    n = w_re.shape[1]
    tm, tn = _tile(t, 1040), 1024
    ni, nj = t // tm, n // tn
    r_out, n_out = w_out_shard.shape
    kc = GATHER_CHUNKS

    def body(h_ref, nw_ref, w_ref, ws_ref, proj_ref, hn_ref, wall_ref, hn_s, send_sems, recv_sems, local_sems):
        i, j = pl.program_id(0), pl.program_id(1)
        start, forward, finish = _gather_plan(ws_ref, wall_ref, send_sems, recv_sems, local_sems, r_out // 2, kc)
        pl.when((i == 0) & (j == 0))(start)
        pl.when((i == ni // 2) & (j == 0))(forward)

        @pl.when(j == 0)
        def _():
            h = h_ref[...]
            ms = jnp.mean(h * h, axis=-1, keepdims=True)
            hn = _bf(h * lax.rsqrt(ms + EPS) * nw_ref[...])
            hn_s[...] = hn
            hn_ref[...] = hn
        proj_ref[...] = _mm(hn_s[...], w_ref[...])
        pl.when((i == ni - 1) & (j == nj - 1))(finish)

    return _pallas(
        body, name="inproj", grid=(ni, nj),
        in_specs=[pl.BlockSpec((tm, d), lambda i, j: (i, 0)), pl.BlockSpec((1, d), lambda i, j: (0, 0)),
                  pl.BlockSpec((d, tn), lambda i, j: (0, j)), ANY],
        out_specs=[pl.BlockSpec((tm, tn), lambda i, j: (i, j)), pl.BlockSpec((tm, d), lambda i, j: (i, 0)), ANY],
        out_shape=[jax.ShapeDtypeStruct((t, n), F32), jax.ShapeDtypeStruct((t, d), BF16),
                   jax.ShapeDtypeStruct((N_SHARD, r_out, n_out), w_out_shard.dtype)],
        scratch_shapes=[pltpu.VMEM((tm, d), BF16), pltpu.SemaphoreType.DMA((6 * kc,)), pltpu.SemaphoreType.DMA((6 * kc,)),
                        pltpu.SemaphoreType.DMA((2 * kc,))],
        compiler_params=_cp("arbitrary", "arbitrary"))(hpad, norm_w, w_re, w_out_shard)


def _conv_fwd(proj, conv_w, conv_b):
    t = proj.shape[0]
    tc = 256
    off = OXS // tc

    def body(x_ref, w_ref, b_ref, o_ref):
        x = x_ref[...]
        w = w_ref[...]
        row = lax.broadcasted_iota(jnp.int32, (t, tc), 0)
        u = b_ref[...] + w[3:4, :] * x
        for k in range(1, CONV_WIDTH):
            u = u + w[3 - k:4 - k, :] * jnp.where(row >= k, pltpu.roll(x, k, 0), 0.0)
        h = 0.5 * u
        o_ref[...] = h + h * jnp.tanh(h)

    return _pallas(
        body, name="conv_fwd", grid=(D_CONV // tc,),
        in_specs=[pl.BlockSpec((t, tc), lambda j: (0, j + off)), pl.BlockSpec((CONV_WIDTH, tc), lambda j: (0, j)),
                  pl.BlockSpec((1, tc), lambda j: (0, j))],
        out_specs=pl.BlockSpec((t, tc), lambda j: (0, j)),
        out_shape=jax.ShapeDtypeStruct((t, D_CONV), F32),
        compiler_params=_cp("parallel"))(proj, conv_w, conv_b)


def _softplus(u):
    e = jnp.exp(-jnp.abs(u))
    w = 1.0 + e
    l1p = jnp.where(w == 1.0, e, jnp.log(w) * (e / jnp.where(w == 1.0, 1.0, w - 1.0)))
    return jnp.maximum(u, 0.0) + l1p


def _chunks_per_step(nc):
    return max(d for d in range(1, 14) if nc % d == 0)


def _dt_prep(proj, dt_bias_l, a_log_l):
    t = proj.shape[0]
    nc = t // CHUNK
    q = CHUNK
    cps = _chunks_per_step(nc)
    rows = cps * q

    def body(raw_ref, bias_ref, alog_ref, dt_ref, acs_ref, acst_ref):
        ri = lax.broadcasted_iota(jnp.int32, (q, q), 0)
        ci = lax.broadcasted_iota(jnp.int32, (q, q), 1)
        tri = (ri >= ci).astype(F32)
        neg_a = -jnp.exp(alog_ref[...])
        for k in range(cps):
            rk = slice(q * k, q * (k + 1))
            sp = _softplus(raw_ref[rk, :] + bias_ref[...])
            row = pl.program_id(0) * rows + q * k + lax.broadcasted_iota(jnp.int32, (q, LANES), 0)
            dt = jnp.where(row >= PAD_LEAD, sp, 0.0)
            acs = jnp.dot(tri, dt * neg_a, preferred_element_type=F32, precision=HI)
            dt_ref[rk, :] = dt
            acs_ref[rk, :] = acs
            acst_ref[k] = acs.T

    return _pallas(
        body, name="dt_prep", grid=(nc // cps,),
        in_specs=[pl.BlockSpec((rows, LANES), lambda c: (c, ODT // LANES)), pl.BlockSpec((1, LANES), lambda c: (0, 0)),
                  pl.BlockSpec((1, LANES), lambda c: (0, 0))],
        out_specs=[pl.BlockSpec((rows, LANES), lambda c: (c, 0)), pl.BlockSpec((rows, LANES), lambda c: (c, 0)),
                   pl.BlockSpec((cps, LANES, q), lambda c: (c, 0, 0))],
        out_shape=[jax.ShapeDtypeStruct((t, LANES), F32), jax.ShapeDtypeStruct((t, LANES), F32),
                   jax.ShapeDtypeStruct((nc, LANES, q), F32)],
        compiler_params=_cp("parallel"))(proj, dt_bias_l, a_log_l)


def _head_cols(blk, idx):
    lane = lax.broadcasted_iota(jnp.int32, blk.shape, 1)
    return jnp.sum(jnp.where(lane == idx, blk, 0.0), axis=1, keepdims=True)


class _HeadVals:
    pass


def _lane_head(shape):
    return lax.broadcasted_iota(jnp.int32, shape, len(shape) - 1) >> 6


def _group_heads(g, gi, dtb, acsb, acst_ref, dskb):
    q = dtb.shape[0]
    hv = _HeadVals()
    lh = _lane_head((1, GROUP_W))
    hv.dt = jnp.zeros((q, GROUP_W), F32)
    hv.acs = jnp.zeros((q, GROUP_W), F32)
    hv.acs_last = jnp.zeros((1, GROUP_W), F32)
    hv.dsk = jnp.zeros((1, GROUP_W), F32)
    rows = []
    for r in range(HPG):
        idx = GROUPS * g + r
        sel = lh == r
        acs_r = acst_ref[0, GROUPS * gi + r:GROUPS * gi + r + 1, :]
        rows.append(acs_r)
        hv.dt = jnp.where(sel, _head_cols(dtb, idx), hv.dt)
        hv.acs = jnp.where(sel, _head_cols(acsb, idx), hv.acs)
        hv.acs_last = jnp.where(sel, acs_r[:, q - 1:q], hv.acs_last)
        hv.dsk = jnp.where(sel, _head_cols(dskb, idx), hv.dsk)
    hv.acs_row = jnp.concatenate(rows, axis=1)
    return hv


def _head_tri(q, lower):
    ri = lax.broadcasted_iota(jnp.int32, (q, GROUP_W), 0)
    li = lax.broadcasted_iota(jnp.int32, (q, GROUP_W), 1) & (HEAD_DIM - 1)
    return ri >= li if lower else ri <= li


def _block_diag_mask():
    rb = lax.broadcasted_iota(jnp.int32, (GROUP_W, GROUP_W), 0) >> 6
    cb = lax.broadcasted_iota(jnp.int32, (GROUP_W, GROUP_W), 1) >> 6
    return rb == cb


def _block_diag(v, mask):
    return jnp.where(mask, jnp.concatenate([v] * HPG, axis=0), jnp.zeros((), v.dtype))


def _head_sums(v, r):
    return jnp.sum(jnp.where(_lane_head((1, GROUP_W)) == r, v, 0.0), axis=1, keepdims=True)


def _ssd_fwd(xbc, proj, dt, acs, acst, d_skip_l, ssd_norm_w):
    t = xbc.shape[0]
    q = CHUNK
    nc = t // q

    gps = SSD_FWD_GROUPS_PER_STEP
    gw, sw = gps * GROUP_W, gps * D_STATE

    def body(xs_ref, b_ref, c_ref, dt_ref, acs_ref, acst_ref, z_ref, dsk_ref, nw_ref,
             y_ref, ymix_ref, st_ref, state):
        @pl.when(pl.program_id(1) == 0)
        def _():
            state[...] = jnp.zeros_like(state)

        lower = _head_tri(q, True)
        bd_mask = _block_diag_mask()
        for gi in range(gps):
            g = gps * pl.program_id(0) + gi
            cols = slice(GROUP_W * gi, GROUP_W * (gi + 1))
            x = xs_ref[:, cols]
            bmb = _bf(b_ref[:, D_STATE * gi:D_STATE * (gi + 1)])
            cmb = _bf(c_ref[:, D_STATE * gi:D_STATE * (gi + 1)])
            hv = _group_heads(g, gi, dt_ref[...], acs_ref[...], acst_ref, dsk_ref[...])
            decay = jnp.exp(jnp.where(lower, hv.acs - hv.acs_row, NEG))
            m_all = _bf(_nt(cmb, jnp.concatenate([bmb] * HPG, axis=0)) * decay)
            xdt = x * hv.dt
            s_prev = state[gi]
            st_ref[0, gi] = s_prev
            y = (_mm(m_all, _block_diag(_bf(xdt), bd_mask)) + _mm(cmb, _bf(s_prev)) * jnp.exp(hv.acs) + hv.dsk * x)
            state[gi] = jnp.exp(hv.acs_last) * s_prev + _tn(bmb, _bf(xdt * jnp.exp(hv.acs_last - hv.acs)))
            y_ref[:, cols] = y
            z = z_ref[:, cols]
            yg = y * (z * _sigmoid(z))
            ms = jnp.mean(yg * yg, axis=-1, keepdims=True)
            ymix_ref[:, cols] = _bf(yg * lax.rsqrt(ms + EPS) * nw_ref[:, cols])

    return _pallas(
        body, name="ssd_fwd", grid=(GROUPS // gps, nc),
        in_specs=[pl.BlockSpec((q, gw), lambda g, c: (c, g)),
                  pl.BlockSpec((q, sw), lambda g, c: (c, D_SSD // sw + g)),
                  pl.BlockSpec((q, sw), lambda g, c: (c, (D_SSD + GROUPS * D_STATE) // sw + g)),
                  pl.BlockSpec((q, LANES), lambda g, c: (c, 0)), pl.BlockSpec((q, LANES), lambda g, c: (c, 0)),
                  pl.BlockSpec((1, gps * GROUPS, q), lambda g, c: (c, g, 0)),
                  pl.BlockSpec((q, gw), lambda g, c: (c, g)),
                  pl.BlockSpec((1, LANES), lambda g, c: (0, 0)), pl.BlockSpec((1, gw), lambda g, c: (0, g))],
        out_specs=[pl.BlockSpec((q, gw), lambda g, c: (c, g)), pl.BlockSpec((q, gw), lambda g, c: (c, g)),
                   pl.BlockSpec((1, gps, D_STATE, GROUP_W), lambda g, c: (c, g, 0, 0))],
        out_shape=[jax.ShapeDtypeStruct((t, D_SSD), F32), jax.ShapeDtypeStruct((t, D_SSD), BF16),
                   jax.ShapeDtypeStruct((nc, GROUPS, D_STATE, GROUP_W), F32)],
        scratch_shapes=[pltpu.VMEM((gps, D_STATE, GROUP_W), F32)],
        compiler_params=_cp("parallel", "arbitrary"))(xbc, xbc, xbc, dt, acs, acst, proj, d_skip_l, ssd_norm_w)


def _swap_halves(v):
    lane = lax.broadcasted_iota(jnp.int32, v.shape, 1)
    return jnp.where((lane & (HEAD_DIM - 1)) < HEAD_DIM // 2, pltpu.roll(v, LANES - HEAD_DIM // 2, 1),
                     pltpu.roll(v, HEAD_DIM // 2, 1))


def _rope(qsrc, q_off, ksrc, k_off, cos_t, sin_t):
    t = qsrc.shape[0]
    tr = _tile(t, 832)
    q_scale = HEAD_DIM ** -0.5

    def body(q_ref, k_ref, cos_ref, sin_ref, qo_ref, ko_ref):
        cs = cos_ref[...]
        sn = sin_ref[...]
        for src, dst, width, scale in ((q_ref, qo_ref, D_ATT, q_scale), (k_ref, ko_ref, D_KV, 1.0)):
            for s in range(width // LANES):
                v = src[:, LANES * s:LANES * (s + 1)].astype(F32)
                dst[:, LANES * s:LANES * (s + 1)] = _bf((v * cs + _swap_halves(v) * sn) * scale)

    return _pallas(
        body, name="rope", grid=(t // tr,),
        in_specs=[pl.BlockSpec((tr, D_ATT), lambda i: (i, q_off // D_ATT)),
                  pl.BlockSpec((tr, D_KV), lambda i: (i, k_off // D_KV)),
                  pl.BlockSpec((tr, LANES), lambda i: (i, 0)), pl.BlockSpec((tr, LANES), lambda i: (i, 0))],
        out_specs=[pl.BlockSpec((tr, D_ATT), lambda i: (i, 0)), pl.BlockSpec((tr, D_KV), lambda i: (i, 0))],
        out_shape=[jax.ShapeDtypeStruct((t, D_ATT), BF16), jax.ShapeDtypeStruct((t, D_KV), BF16)],
        compiler_params=_cp("parallel"))(qsrc, ksrc, cos_t, sin_t)


def _attn_chunks_per_step(nc):
    return max(d for d in range(1, 6) if nc % d == 0)


def _band(ref, c):
    return [ref[pl.ds(pl.multiple_of(jnp.maximum(c - j, 0) * CHUNK, CHUNK), CHUNK), :] for j in (2, 1, 0)]


def _attn_probs(qh, kb, sink_col, valid):
    s = jnp.where(valid, _nt(qh, kb), NEG)
    m = jnp.maximum(jnp.max(s, axis=1, keepdims=True), sink_col)
    p = jnp.exp(s - m)
    psink = jnp.exp(sink_col - m)
    return p, psink, 1.0 / (jnp.sum(p, axis=1, keepdims=True) + psink)


def _attn_operands(c, q, k_refs, v_refs, sink_ref, h):
    qh = jnp.concatenate([q[:, HEAD_DIM * (REP * h + r):HEAD_DIM * (REP * h + r + 1)] for r in range(REP)], axis=0)
    kb = jnp.concatenate([k[:, HEAD_DIM * h:HEAD_DIM * (h + 1)] for k in k_refs], axis=0)
    vb = jnp.concatenate([_bf(v[:, HEAD_DIM * h:HEAD_DIM * (h + 1)]) for v in v_refs], axis=0)
    rows = lax.broadcasted_iota(jnp.int32, (REP * CHUNK, 1), 0) >> 6
    sink_col = jnp.zeros((REP * CHUNK, 1), F32)
    for r in range(REP):
        sink_col = jnp.where(rows == r, sink_ref[REP * h + r], sink_col)
    key_abs = (c - (BAND_CHUNKS - 1)) * CHUNK + lax.broadcasted_iota(jnp.int32, (1, BAND_CHUNKS * CHUNK), 1)
    return qh, kb, vb, sink_col, key_abs >= PAD_LEAD


def _attn_fwd(qr, kr, proj, sinks):
    t = qr.shape[0]
    nc = t // CHUNK
    cps = _attn_chunks_per_step(nc)
    rows = cps * CHUNK

    def body(q_ref, k_ref, v_ref, g_ref, sink_ref, o_ref):
        for j in range(cps):
            c = pl.program_id(0) * cps + j
            rj = slice(CHUNK * j, CHUNK * (j + 1))
            ks, vs = _band(k_ref, c), _band(v_ref, c)
            q = q_ref[rj, :]
            outs = []
            for h in range(KV_HEADS):
                qh, kb, vb, sink_col, valid = _attn_operands(c, q, ks, vs, sink_ref, h)
                p, _, inv = _attn_probs(qh, kb, sink_col, valid)
                o = _mm(_bf(p), vb) * inv
                outs += [o[CHUNK * r:CHUNK * (r + 1)] for r in range(REP)]
            att = jnp.concatenate(outs, axis=1)
            gate = g_ref[rj, :]
            o_ref[rj, :] = _bf(att * (gate * _sigmoid(gate)))

    return _pallas(
        body, name="attn_fwd", grid=(nc // cps,),
        in_specs=[pl.BlockSpec((rows, D_ATT), lambda i: (i, 0)), pl.BlockSpec((t, D_KV), lambda i: (0, 0)),
                  pl.BlockSpec((t, D_KV), lambda i: (0, OV // D_KV)),
                  pl.BlockSpec((rows, D_ATT), lambda i: (i, OG // D_ATT)), pl.BlockSpec(memory_space=pltpu.SMEM)],
        out_specs=pl.BlockSpec((rows, D_ATT), lambda i: (i, 0)),
        out_shape=jax.ShapeDtypeStruct((t, D_ATT), BF16),
        compiler_params=_cp("parallel"))(qr, kr, proj, proj, sinks)


def _outproj(ymix, amix, w_out):
    t = ymix.shape[0]
    tm, tn = _tile(t, 832), 1024

    def body(y_ref, a_ref, wy_ref, wa_ref, o_ref):
        o_ref[...] = _mm(y_ref[...], wy_ref[...]) + _mm(a_ref[...], wa_ref[...])

    return _pallas(
        body, name="outproj", grid=(t // tm, D_MODEL // tn),
        in_specs=[pl.BlockSpec((tm, D_SSD), lambda i, j: (i, 0)), pl.BlockSpec((tm, D_ATT), lambda i, j: (i, 0)),
                  pl.BlockSpec((D_SSD, tn), lambda i, j: (0, j)),
                  pl.BlockSpec((D_ATT, tn), lambda i, j: (D_SSD // D_ATT, j))],
        out_specs=pl.BlockSpec((tm, tn), lambda i, j: (i, j)),
        out_shape=jax.ShapeDtypeStruct((t, D_MODEL), F32),
        compiler_params=_cp("parallel", "parallel"))(ymix, amix, w_out, w_out)


def _post_loss(out, x, target, norm_post_w):
    t = out.shape[0]
    nc = t // CHUNK
    cps = _attn_chunks_per_step(nc)
    rows = cps * CHUNK

    def body(o_ref, *refs):
        x_refs, tg_refs = refs[:cps], refs[cps:2 * cps]
        nw_ref, dout_ref, dy_ref, loss_ref, gnw_ref = refs[2 * cps:]
        i = pl.program_id(0)

        @pl.when(i == 0)
        def _():
            loss_ref[...] = jnp.zeros_like(loss_ref)
            gnw_ref[...] = jnp.zeros_like(gnw_ref)

        nw = nw_ref[...]
        loss = jnp.zeros((), F32)
        gnw = jnp.zeros((1, D_MODEL), F32)
        for k in range(cps):
            rk = slice(CHUNK * k, CHUNK * (k + 1))
            frames = i * cps + k > 0
            o = o_ref[rk, :]
            rstd = lax.rsqrt(jnp.mean(o * o, axis=-1, keepdims=True) + EPS)
            n = o * rstd
            err = jnp.where(frames, x_refs[k][...] + n * nw - tg_refs[k][...], 0.0)
            loss = loss + jnp.sum(err * err)
            dy = err * (1.0 / D_MODEL)
            dy_ref[rk, :] = dy
            gnw = gnw + jnp.sum(dy * n, axis=0, keepdims=True)
            dn = dy * nw
            dout_ref[rk, :] = _bf(rstd * (dn - n * jnp.mean(dn * n, axis=-1, keepdims=True)))
        loss_ref[...] += loss * (0.5 / D_MODEL)
        gnw_ref[...] += gnw

    lower = [pl.BlockSpec((CHUNK, D_MODEL), functools.partial(lambda i, k: (jnp.maximum(i * cps + k - 1, 0), 0), k=k))
             for k in range(cps)]
    return _pallas(
        body, name="post_loss", grid=(nc // cps,),
        in_specs=[pl.BlockSpec((rows, D_MODEL), lambda i: (i, 0))] + lower + lower
        + [pl.BlockSpec((1, D_MODEL), lambda i: (0, 0))],
        out_specs=[pl.BlockSpec((rows, D_MODEL), lambda i: (i, 0)), pl.BlockSpec((rows, D_MODEL), lambda i: (i, 0)),
                   pl.BlockSpec((8, LANES), lambda i: (0, 0)), pl.BlockSpec((1, D_MODEL), lambda i: (0, 0))],
        out_shape=[jax.ShapeDtypeStruct((t, D_MODEL), BF16), jax.ShapeDtypeStruct((t, D_MODEL), F32),
                   jax.ShapeDtypeStruct((8, LANES), F32), jax.ShapeDtypeStruct((1, D_MODEL), F32)],
        compiler_params=_cp("arbitrary"))(out, *([x] * cps), *([target] * cps), norm_post_w)


def _carried(grid, carry):
    if carry is None:
        return [], [], [], [], lambda refs: None, lambda refs: None
    hn = carry.shape[1] // 2

    def at(ids, which):
        cond = None
        for d, size in enumerate(grid):
            here = pl.program_id(d) == (0 if which == "first" else size - 1)
            cond = here if cond is None else cond & here
        return cond

    def start(refs):
        @pl.when(at(grid, "first"))
        def _():
            for cp in _pair_copies(*refs):
                cp.start()

    def finish(refs):
        @pl.when(at(grid, "last"))
        def _():
            for cp in _pair_copies(*refs):
                cp.wait()

    return ([ANY], [ANY], [jax.ShapeDtypeStruct((carry.shape[0], hn), F32)],
            [pltpu.SemaphoreType.DMA((PAIR_CHUNKS,)), pltpu.SemaphoreType.DMA((PAIR_CHUNKS,))], start, finish)


def _nt_matmul(a, b, name, carry=None):
    t, k = a.shape
    n = b.shape[0]
    tm, tn = _tile(t, 832), 1024
    grid = (t // tm, n // tn)
    cin, cout, cshape, cscratch, start, finish = _carried(grid, carry)

    def body(a_ref, b_ref, *refs):
        o_ref = refs[len(cin)]
        comm = (refs[0], refs[2], refs[3], refs[4]) if carry is not None else None
        start(comm)
        o_ref[...] = _nt(a_ref[...], b_ref[...])
        finish(comm)

    res = _pallas(
        body, name=name, grid=grid,
        in_specs=[pl.BlockSpec((tm, k), lambda i, j: (i, 0)), pl.BlockSpec((tn, k), lambda i, j: (j, 0))] + cin,
        out_specs=[pl.BlockSpec((tm, tn), lambda i, j: (i, j))] + cout,
        out_shape=[jax.ShapeDtypeStruct((t, n), F32)] + cshape, scratch_shapes=cscratch,
        compiler_params=_cp("arbitrary", "arbitrary"))(a, b, *([carry] if carry is not None else []))
    return res if carry is not None else res[0]


def _tn_matmul(a, b, name, carry=None):
    t, m = a.shape
    n = b.shape[1]
    tk, tm, tn = _tile(t, 832), min(m, 2048), min(n, 2048)
    nk = t // tk
    grid = (m // tm, n // tn, nk)
    cin, cout, cshape, cscratch, start, finish = _carried(grid, carry)

    def body(a_ref, b_ref, *refs):
        o_ref = refs[len(cin)]
        comm = (refs[0], refs[2], refs[3], refs[4]) if carry is not None else None
        start(comm)

        @pl.when(pl.program_id(2) == 0)
        def _():
            o_ref[...] = jnp.zeros_like(o_ref)
        o_ref[...] += _tn(a_ref[...], b_ref[...])
        finish(comm)

    res = _pallas(
        body, name=name, grid=grid,
        in_specs=[pl.BlockSpec((tk, tm), lambda i, j, k: (k, i)), pl.BlockSpec((tk, tn), lambda i, j, k: (k, j))] + cin,
        out_specs=[pl.BlockSpec((tm, tn), lambda i, j, k: (i, j))] + cout,
        out_shape=[jax.ShapeDtypeStruct((m, n), F32)] + cshape, scratch_shapes=cscratch,
        compiler_params=_cp("arbitrary", "arbitrary", "arbitrary"))(a, b, *([carry] if carry is not None else []))
    return res if carry is not None else res[0]


def _attn_bwd(qr, kr, proj, dmix, sinks, ga):
    t = qr.shape[0]
    nc = t // CHUNK
    cps = _attn_chunks_per_step(nc)
    nsteps = nc // cps
    rows_step = cps * CHUNK

    def body(q_ref, k_ref, v_ref, g_ref, da_ref, sink_ref, ga_ref, dq_ref, dg_ref, dk_ref, dv_ref, gs_ref,
             got_ref, send_sems, recv_sems):
        step = pl.program_id(0)

        @pl.when(step == 0)
        def _():
            for cp in _exchange_copies(ga_ref, got_ref, send_sems, recv_sems):
                cp.start()
            dk_ref[...] = jnp.zeros_like(dk_ref)
            dv_ref[...] = jnp.zeros_like(dv_ref)
            gs_ref[...] = jnp.zeros_like(gs_ref)

        lane = lax.broadcasted_iota(jnp.int32, (1, LANES), 1)
        rows = lax.broadcasted_iota(jnp.int32, (REP * CHUNK, 1), 0) >> 6
        gs = jnp.zeros((1, LANES), F32)
        dk_parts = [[] for _ in range(cps + BAND_CHUNKS - 1)]
        dv_parts = [[] for _ in range(cps + BAND_CHUNKS - 1)]
        for j in range(cps):
            c = step * cps + j
            rj = slice(CHUNK * j, CHUNK * (j + 1))
            ks, vs = _band(k_ref, c), _band(v_ref, c)
            q = q_ref[rj, :]
            gate = g_ref[rj, :]
            sg = _sigmoid(gate)
            da = da_ref[rj, :]
            datt = da * (gate * sg)
            dqs, atts, dks, dvs = [], [], [], []
            for h in range(KV_HEADS):
                qh, kb, vb, sink_col, valid = _attn_operands(c, q, ks, vs, sink_ref, h)
                p, psink, inv = _attn_probs(qh, kb, sink_col, valid)
                pb = _bf(p)
                o = _mm(pb, vb) * inv
                do = jnp.concatenate([datt[:, HEAD_DIM * (REP * h + r):HEAD_DIM * (REP * h + r + 1)]
                                      for r in range(REP)], axis=0)
                dob = _bf(do * inv)
                delta = jnp.sum(do * o, axis=1, keepdims=True) * inv
                ds = _bf(p * (_nt(dob, vb) - delta))
                gsink = -psink * delta
                for r in range(REP):
                    gs = gs + jnp.where(lane == REP * h + r, jnp.sum(jnp.where(rows == r, gsink, 0.0)), 0.0)
                dqh = _mm(ds, kb)
                dqs += [dqh[CHUNK * r:CHUNK * (r + 1)] for r in range(REP)]
                atts += [o[CHUNK * r:CHUNK * (r + 1)] for r in range(REP)]
                dks.append(_tn(ds, qh))
                dvs.append(_tn(pb, dob))
            dq_ref[rj, :] = jnp.concatenate(dqs, axis=1)
            att = jnp.concatenate(atts, axis=1)
            dg_ref[rj, :] = _bf(da * att * (sg * (1.0 + gate * (1.0 - sg))))
            dkf = jnp.concatenate(dks, axis=1)
            dvf = jnp.concatenate(dvs, axis=1)
            for b in range(BAND_CHUNKS):
                dk_parts[j + b].append(dkf[CHUNK * b:CHUNK * (b + 1)])
                dv_parts[j + b].append(dvf[CHUNK * b:CHUNK * (b + 1)])
        gs_ref[0:1, :] += gs
        for rel in range(cps + BAND_CHUNKS - 1):
            r0 = pl.multiple_of(jnp.maximum(step * cps - (BAND_CHUNKS - 1) + rel, 0) * CHUNK, CHUNK)
            dk_ref[pl.ds(r0, CHUNK), :] += sum(dk_parts[rel][1:], dk_parts[rel][0])
            dv_ref[pl.ds(r0, CHUNK), :] += sum(dv_parts[rel][1:], dv_parts[rel][0])

        @pl.when(step == nsteps - 1)
        def _():
            for cp in _exchange_copies(ga_ref, got_ref, send_sems, recv_sems):
                cp.wait()

    return _pallas(
        body, name="attn_bwd", grid=(nsteps,),
        in_specs=[pl.BlockSpec((rows_step, D_ATT), lambda i: (i, 0)), pl.BlockSpec((t, D_KV), lambda i: (0, 0)),
                  pl.BlockSpec((t, D_KV), lambda i: (0, OV // D_KV)),
                  pl.BlockSpec((rows_step, D_ATT), lambda i: (i, OG // D_ATT)),
                  pl.BlockSpec((rows_step, D_ATT), lambda i: (i, D_SSD // D_ATT)),
                  pl.BlockSpec(memory_space=pltpu.SMEM), ANY],
        out_specs=[pl.BlockSpec((rows_step, D_ATT), lambda i: (i, 0)), pl.BlockSpec((rows_step, D_ATT), lambda i: (i, 0)),
                   pl.BlockSpec((t, D_KV), lambda i: (0, 0)), pl.BlockSpec((t, D_KV), lambda i: (0, 0)),
                   pl.BlockSpec((8, LANES), lambda i: (0, 0)), ANY],
        out_shape=[jax.ShapeDtypeStruct((t, D_ATT), F32), jax.ShapeDtypeStruct((t, D_ATT), BF16),
                   jax.ShapeDtypeStruct((t, D_KV), F32), jax.ShapeDtypeStruct((t, D_KV), F32),
                   jax.ShapeDtypeStruct((8, LANES), F32), _exchange_shape(ga)],
        scratch_shapes=_exchange_scratch(),
        compiler_params=_cp("arbitrary"))(qr, kr, proj, proj, dmix, sinks, ga)


def _ssd_bwd(dmix, y_ssd, xbc, proj, dt, acs, acst, states, d_skip_l, ssd_norm_w):
    t = xbc.shape[0]
    q = CHUNK
    nc = t // q
    gps = SSD_BWD_GROUPS_PER_STEP
    gw, sw = gps * GROUP_W, gps * D_STATE

    def body(dmix_ref, y_ref, z_ref, nw_ref, xs_ref, b_ref, c_ref, dt_ref, acs_ref, acst_ref, st_ref, dsk_ref,
             dz_ref, dxs_ref, db_ref, dc_ref, dacs_ref, ddt_ref, gnw_ref, gdsk_ref, dstate):
        @pl.when(pl.program_id(1) == 0)
        def _():
            dstate[...] = jnp.zeros_like(dstate)
            gnw_ref[...] = jnp.zeros_like(gnw_ref)
            gdsk_ref[...] = jnp.zeros_like(gdsk_ref)

        last_row = lax.broadcasted_iota(jnp.int32, (q, 1), 0) == q - 1
        lane = lax.broadcasted_iota(jnp.int32, (q, LANES), 1)
        lane1 = lax.broadcasted_iota(jnp.int32, (8, LANES), 1)
        lower, upper = _head_tri(q, True), _head_tri(q, False)
        bd_mask = _block_diag_mask()
        for gi in range(gps):
            g = gps * pl.program_id(0) + gi
            cols = slice(GROUP_W * gi, GROUP_W * (gi + 1))
            scols = slice(D_STATE * gi, D_STATE * (gi + 1))
            y = y_ref[:, cols]
            z = z_ref[:, cols]
            sz = _sigmoid(z)
            silu_z = z * sz
            yg = y * silu_z
            rstd = lax.rsqrt(jnp.mean(yg * yg, axis=-1, keepdims=True) + EPS)
            n = yg * rstd
            dout = dmix_ref[:, cols]
            gnw_ref[:, cols] += jnp.sum(dout * n, axis=0, keepdims=True)
            dn = dout * nw_ref[:, cols]
            dyg = rstd * (dn - n * jnp.mean(dn * n, axis=-1, keepdims=True))
            dy = dyg * silu_z
            dz_ref[:, cols] = _bf(dyg * y * (sz * (1.0 + z * (1.0 - sz))))

            x = xs_ref[:, cols]
            bmb, cmb = _bf(b_ref[:, scols]), _bf(c_ref[:, scols])
            hv = _group_heads(g, gi, dt_ref[...], acs_ref[...], acst_ref, dsk_ref[...])
            dec = jnp.exp(jnp.where(lower, hv.acs - hv.acs_row, NEG))
            dect = jnp.exp(jnp.where(upper, hv.acs_row - hv.acs, NEG))
            b4 = jnp.concatenate([bmb] * HPG, axis=0)
            c4 = jnp.concatenate([cmb] * HPG, axis=0)
            m_all = _nt(cmb, b4) * dec
            mt_all = _nt(bmb, c4) * dect
            xdt = x * hv.dt
            xdt_b, dyb = _bf(xdt), _bf(dy)
            x_bd, dy_bd = _block_diag(xdt_b, bd_mask), _block_diag(dyb, bd_mask)
            s_prev = st_ref[0, gi]
            spb = _bf(s_prev)
            ds_new = dstate[gi]
            dsb = _bf(ds_new)
            e = jnp.exp(hv.acs)
            elast = jnp.exp(hv.acs_last)
            dte = jnp.exp(hv.acs_last - hv.acs)
            bds = _mm(bmb, dsb)
            dxdt = _mm(_bf(mt_all), dy_bd) + bds * dte
            dm = _nt(dyb, x_bd)
            dmt = _nt(xdt_b, dy_bd)
            dye = _bf(dy * e)
            dc_ref[:, scols] = _mm(_bf(dm * dec), b4) + _nt(dye, spb)
            db_ref[:, scols] = _mm(_bf(dmt * dect), c4) + _nt(_bf(xdt * dte), dsb)
            dstate[gi] = elast * ds_new + _tn(cmb, dye)
            dxs_ref[:, cols] = dxdt * hv.dt + hv.dsk * dy
            ddte_dte = bds * xdt * dte
            dacs_l = dm * m_all - dmt * mt_all + dy * _mm(cmb, spb) * e - ddte_dte
            dlast_l = (jnp.sum(ddte_dte, axis=0, keepdims=True)
                       + jnp.sum(s_prev * ds_new, axis=0, keepdims=True) * elast)
            ddt_l = dxdt * x
            gdsk_l = jnp.sum(dy * x, axis=0, keepdims=True)
            dacs_out = jnp.zeros((q, LANES), F32)
            ddt_out = jnp.zeros((q, LANES), F32)
            gdsk = jnp.zeros((8, LANES), F32)
            for r in range(HPG):
                dacs = _head_sums(dacs_l, r) + jnp.where(last_row, _head_sums(dlast_l, r), 0.0)
                dacs_out = jnp.where(lane == r, dacs, dacs_out)
                ddt_out = jnp.where(lane == r, _head_sums(ddt_l, r), ddt_out)
                gdsk = gdsk + jnp.where(lane1 == r, _head_sums(gdsk_l, r), 0.0)
            dacs_ref[:, LANES * gi:LANES * (gi + 1)] = dacs_out
            ddt_ref[:, LANES * gi:LANES * (gi + 1)] = ddt_out
            gdsk_ref[gi] += gdsk

    rev = lambda c: nc - 1 - c
    wide = pl.BlockSpec((q, gw), lambda g, c: (rev(c), g))
    return _pallas(
        body, name="ssd_bwd", grid=(GROUPS // gps, nc),
        in_specs=[wide, wide, wide, pl.BlockSpec((1, gw), lambda g, c: (0, g)), wide,
                  pl.BlockSpec((q, sw), lambda g, c: (rev(c), D_SSD // sw + g)),
                  pl.BlockSpec((q, sw), lambda g, c: (rev(c), (D_SSD + GROUPS * D_STATE) // sw + g)),
                  pl.BlockSpec((q, LANES), lambda g, c: (rev(c), 0)), pl.BlockSpec((q, LANES), lambda g, c: (rev(c), 0)),
                  pl.BlockSpec((1, gps * GROUPS, q), lambda g, c: (rev(c), g, 0)),
                  pl.BlockSpec((1, gps, D_STATE, GROUP_W), lambda g, c: (rev(c), g, 0, 0)),
                  pl.BlockSpec((1, LANES), lambda g, c: (0, 0))],
        out_specs=[wide, wide,
                   pl.BlockSpec((q, sw), lambda g, c: (rev(c), g)), pl.BlockSpec((q, sw), lambda g, c: (rev(c), g)),
                   pl.BlockSpec((q, gps * LANES), lambda g, c: (rev(c), g)),
                   pl.BlockSpec((q, gps * LANES), lambda g, c: (rev(c), g)),
                   pl.BlockSpec((1, gw), lambda g, c: (0, g)), pl.BlockSpec((gps, 8, LANES), lambda g, c: (g, 0, 0))],
        out_shape=[jax.ShapeDtypeStruct((t, D_SSD), BF16), jax.ShapeDtypeStruct((t, D_SSD), F32),
                   jax.ShapeDtypeStruct((t, GROUPS * D_STATE), F32), jax.ShapeDtypeStruct((t, GROUPS * D_STATE), F32),
                   jax.ShapeDtypeStruct((t, GROUPS * LANES), F32), jax.ShapeDtypeStruct((t, GROUPS * LANES), F32),
                   jax.ShapeDtypeStruct((1, D_SSD), F32), jax.ShapeDtypeStruct((GROUPS, 8, LANES), F32)],
        scratch_shapes=[pltpu.VMEM((gps, D_STATE, GROUP_W), F32)],
        compiler_params=_cp("parallel", "arbitrary"))(dmix, y_ssd, proj, ssd_norm_w, xbc, xbc, xbc, dt, acs, acst,
                                                      states, d_skip_l)


def _dt_bwd(dacs_g, ddt_g, dt, proj, dt_bias_l, a_log_l):
    t = dt.shape[0]
    q = CHUNK
    nc = t // q
    cps = _chunks_per_step(nc)
    rows = cps * q

    def body(dacs_ref, ddt_ref, dt_ref, raw_ref, bias_ref, alog_ref, draw_ref, ga_ref, gb_ref):
        @pl.when(pl.program_id(0) == 0)
        def _():
            ga_ref[...] = jnp.zeros_like(ga_ref)
            gb_ref[...] = jnp.zeros_like(gb_ref)

        lane = lax.broadcasted_iota(jnp.int32, (q, LANES), 1)
        ri = lax.broadcasted_iota(jnp.int32, (q, q), 0)
        ci = lax.broadcasted_iota(jnp.int32, (q, q), 1)
        triu = (ri <= ci).astype(F32)
        a = -jnp.exp(alog_ref[...])
        used = (lane & (GROUPS - 1)) < HPG
        ga = jnp.zeros((1, LANES), F32)
        gb = jnp.zeros((1, LANES), F32)
        for k in range(cps):
            rk = slice(q * k, q * (k + 1))
            dacs = jnp.zeros((q, LANES), F32)
            ddt = jnp.zeros((q, LANES), F32)
            for g in range(GROUPS):
                mask = (lane >= GROUPS * g) & (lane < GROUPS * g + HPG)
                sl = slice(LANES * g, LANES * (g + 1))
                if g == 0:
                    dacs = jnp.where(mask, dacs_ref[rk, sl], dacs)
                    ddt = jnp.where(mask, ddt_ref[rk, sl], ddt)
                else:
                    dacs = jnp.where(mask, pltpu.roll(dacs_ref[rk, sl], GROUPS * g, 1), dacs)
                    ddt = jnp.where(mask, pltpu.roll(ddt_ref[rk, sl], GROUPS * g, 1), ddt)
            dda = jnp.dot(triu, dacs, preferred_element_type=F32, precision=HI)
            row = pl.program_id(0) * rows + q * k + lax.broadcasted_iota(jnp.int32, (q, LANES), 0)
            dsp = jnp.where((row >= PAD_LEAD) & used, dda * a + ddt, 0.0)
            draw = dsp * _sigmoid(raw_ref[rk, :] + bias_ref[...])
            draw_ref[rk, :] = _bf(draw)
            gb = gb + jnp.sum(draw, axis=0, keepdims=True)
            ga = ga + jnp.sum(jnp.where(used, dda * dt_ref[rk, :], 0.0), axis=0, keepdims=True)
        gb_ref[0:1, :] += gb
        ga_ref[0:1, :] += ga * a

    return _pallas(
        body, name="dt_bwd", grid=(nc // cps,),
        in_specs=[pl.BlockSpec((rows, GROUPS * LANES), lambda c: (c, 0)),
                  pl.BlockSpec((rows, GROUPS * LANES), lambda c: (c, 0)),
                  pl.BlockSpec((rows, LANES), lambda c: (c, 0)), pl.BlockSpec((rows, LANES), lambda c: (c, ODT // LANES)),
                  pl.BlockSpec((1, LANES), lambda c: (0, 0)), pl.BlockSpec((1, LANES), lambda c: (0, 0))],
        out_specs=[pl.BlockSpec((rows, LANES), lambda c: (c, 0)), pl.BlockSpec((8, LANES), lambda c: (0, 0)),
                   pl.BlockSpec((8, LANES), lambda c: (0, 0))],
        out_shape=[jax.ShapeDtypeStruct((t, LANES), BF16), jax.ShapeDtypeStruct((8, LANES), F32),
                   jax.ShapeDtypeStruct((8, LANES), F32)],
        compiler_params=_cp("arbitrary"))(dacs_g, ddt_g, dt, proj, dt_bias_l, a_log_l)


def _conv_bwd(dsegs, proj, conv_w, conv_b, col_off, name):
    t, seg_w = dsegs[0].shape
    nseg = len(dsegs)
    width = nseg * seg_w
    tc = 128
    per = seg_w // tc
    rt = _tile(t, 320)
    off_p = (OXS + col_off) // tc
    off_w = col_off // tc

    def body(*refs):
        d_refs = refs[:nseg]
        x_ref, w_ref, b_ref, dx_ref, gw_ref, gb_ref, xp, dup = refs[nseg:]
        assert per & (per - 1) == 0
        seg = pl.program_id(0) >> (per.bit_length() - 1)

        def d_rows(r0):
            d = d_refs[nseg - 1][pl.ds(r0, rt), :]
            for s in range(nseg - 2, -1, -1):
                d = jnp.where(seg == s, d_refs[s][pl.ds(r0, rt), :], d)
            return d

        xp[0:8, :] = jnp.zeros((8, tc), F32)
        xp[8:t + 8, :] = x_ref[...]
        dup[t:t + 8, :] = jnp.zeros((8, tc), F32)
        w = w_ref[...]
        bias = b_ref[...]

        def first(i, acc):
            r0 = pl.multiple_of(i * rt, 8)
            xs = [xp[pl.ds(r0 + 5 + k, rt), :] for k in range(CONV_WIDTH)]
            u = bias + w[3:4, :] * xs[3] + w[2:3, :] * xs[2] + w[1:2, :] * xs[1] + w[0:1, :] * xs[0]
            su = 0.5 + 0.5 * jnp.tanh(0.5 * u)
            du = d_rows(r0) * (su * (1.0 + u * (1.0 - su)))
            dup[pl.ds(r0, rt), :] = du
            return tuple(acc[k] + jnp.sum(du * xs[k], axis=0, keepdims=True) for k in range(CONV_WIDTH)) + (
                acc[CONV_WIDTH] + jnp.sum(du, axis=0, keepdims=True),)

        zero = jnp.zeros((1, tc), F32)
        acc = lax.fori_loop(0, t // rt, first, (zero,) * (CONV_WIDTH + 1))
        gw_ref[...] = jnp.concatenate(acc[:CONV_WIDTH], axis=0)
        gb_ref[...] = acc[CONV_WIDTH]

        def second(i, carry):
            r0 = pl.multiple_of(i * rt, 16)
            dx_ref[pl.ds(r0, rt), :] = _bf(w[3:4, :] * dup[pl.ds(r0, rt), :] + w[2:3, :] * dup[pl.ds(r0 + 1, rt), :]
                                          + w[1:2, :] * dup[pl.ds(r0 + 2, rt), :] + w[0:1, :] * dup[pl.ds(r0 + 3, rt), :])
            return carry

        lax.fori_loop(0, t // rt, second, 0)

    d_specs = [pl.BlockSpec((t, tc), functools.partial(lambda j, s: (0, jnp.clip(j - s * per, 0, per - 1)), s=s))
               for s in range(nseg)]
    return _pallas(
        body, name=name, grid=(width // tc,),
        in_specs=d_specs + [pl.BlockSpec((t, tc), lambda j: (0, j + off_p)),
                            pl.BlockSpec((CONV_WIDTH, tc), lambda j: (0, j + off_w)),
                            pl.BlockSpec((1, tc), lambda j: (0, j + off_w))],
        out_specs=[pl.BlockSpec((t, tc), lambda j: (0, j)), pl.BlockSpec((CONV_WIDTH, tc), lambda j: (0, j)),
                   pl.BlockSpec((1, tc), lambda j: (0, j))],
        out_shape=[jax.ShapeDtypeStruct((t, width), BF16), jax.ShapeDtypeStruct((CONV_WIDTH, width), F32),
                   jax.ShapeDtypeStruct((1, width), F32)],
        scratch_shapes=[pltpu.VMEM((t + 8, tc), F32), pltpu.VMEM((t + 8, tc), F32)],
        compiler_params=_cp("parallel"))(*dsegs, proj, conv_w, conv_b)


def _dinproj(segs, w_re, hpad, norm_w, dy_t, ga):
    t = segs[0].shape[0]
    d = hpad.shape[1]
    tm, tk = _tile(t, 416), SEG_TILE
    counts = [s.shape[1] // tk for s in segs]
    firsts = [sum(counts[:s]) for s in range(len(segs))]
    nk = sum(counts)
    assert nk * tk == w_re.shape[1]
    ni = t // tm
    ns = len(segs)

    def body(*refs):
        seg_refs = refs[:ns]
        w_ref, h_ref, nw_ref, dy_ref, ga_ref, dh_ref, gnw_ref, got_ref, acc, send_sems, recv_sems = refs[ns:]
        i, k = pl.program_id(0), pl.program_id(1)

        @pl.when((i == 0) & (k == 0))
        def _():
            for cp in _exchange_copies(ga_ref, got_ref, send_sems, recv_sems):
                cp.start()
            gnw_ref[...] = jnp.zeros_like(gnw_ref)

        @pl.when(k == 0)
        def _():
            acc[...] = jnp.zeros_like(acc)

        for s in range(ns):
            @pl.when((k >= firsts[s]) & (k < firsts[s] + counts[s]))
            def _(s=s):
                acc[...] += _nt(seg_refs[s][...], w_ref[...])

        @pl.when(k == nk - 1)
        def _():
            h = h_ref[...]
            rstd = lax.rsqrt(jnp.mean(h * h, axis=-1, keepdims=True) + EPS)
            nrm = h * rstd
            dhn = acc[...]
            gnw_ref[...] += jnp.sum(dhn * nrm, axis=0, keepdims=True)
            dn = dhn * nw_ref[...]
            dh_ref[...] = rstd * (dn - nrm * jnp.mean(dn * nrm, axis=-1, keepdims=True)) + dy_ref[...]

        @pl.when((i == ni - 1) & (k == nk - 1))
        def _():
            for cp in _exchange_copies(ga_ref, got_ref, send_sems, recv_sems):
                cp.wait()

    seg_specs = [pl.BlockSpec((tm, tk), functools.partial(lambda i, k, f0, n0: (i, jnp.clip(k - f0, 0, n0 - 1)),
                                                          f0=firsts[s], n0=counts[s])) for s in range(ns)]
    return _pallas(
        body, name="dinproj", grid=(ni, nk),
        in_specs=seg_specs + [pl.BlockSpec((d, tk), lambda i, k: (0, k)),
                              pl.BlockSpec((tm, d), lambda i, k: (i, 0)), pl.BlockSpec((1, d), lambda i, k: (0, 0)),
                              pl.BlockSpec((tm, d), lambda i, k: (i, 0)), ANY],
        out_specs=[pl.BlockSpec((tm, d), lambda i, k: (i, 0)), pl.BlockSpec((1, d), lambda i, k: (0, 0)), ANY],
        out_shape=[jax.ShapeDtypeStruct((t, d), F32), jax.ShapeDtypeStruct((1, d), F32), _exchange_shape(ga)],
        scratch_shapes=[pltpu.VMEM((tm, d), F32)] + _exchange_scratch(),
        compiler_params=_cp("arbitrary", "arbitrary"))(*segs, w_re, hpad, norm_w, dy_t, ga)


def _spread_heads(v):
    v = jnp.pad(v.reshape(GROUPS, HPG), ((0, 0), (0, GROUPS - HPG))).reshape(1, GROUPS * GROUPS)
    return jnp.pad(v, ((0, 0), (0, LANES - GROUPS * GROUPS)))


def _gather_heads(v):
    return v[0:1, :GROUPS * GROUPS].reshape(GROUPS, GROUPS)[:, :HPG].reshape(1, SSD_HEADS)


def _rope_tables(t):
    half = HEAD_DIM // 2
    inv = ROPE_THETA ** (-jnp.arange(half, dtype=F32) / half)
    pos = (jnp.arange(t) - PAD_LEAD).astype(F32)
    ang = pos[:, None] * inv[None, :]
    cos, sin = jnp.cos(ang), jnp.sin(ang)
    cos_t = jnp.concatenate([cos, cos, cos, cos], axis=1)
    sin_t = jnp.concatenate([-sin, sin, -sin, sin], axis=1)
    return cos_t, sin_t


def _column_pieces():
    runs = [(0, OB + 2 * GROUPS * D_STATE, 0)]
    o = OB + 2 * GROUPS * D_STATE
    runs += [(o + HPG * g, HPG, ODT + GROUPS * g) for g in range(GROUPS)]
    o += SSD_HEADS
    for width, dst in ((D_ATT, OQ), (D_KV, OK), (D_KV, OV), (D_ATT, OG)):
        runs.append((o, width, dst))
        o += width
    assert o == D_IN
    pieces = []
    for o0, width, dst in runs:
        for j in range(N_SHARD):
            lo, hi = max(o0, W_IN_SHARD * j), min(o0 + width, W_IN_SHARD * (j + 1))
            if lo < hi:
                pieces.append((j, lo - W_IN_SHARD * j, hi - W_IN_SHARD * j, dst + lo - o0))
    return pieces


def _shards_to_re(w_all):
    _, k, _ = w_all.shape
    tr = 256

    def body(x_ref, o_ref):
        o_ref[:, ODT:ODT + DT_SLAB] = jnp.zeros((tr, DT_SLAB), o_ref.dtype)
        for j, c0, c1, d0 in _column_pieces():
            o_ref[:, d0:d0 + c1 - c0] = x_ref[j, :, c0:c1]

    return _pallas(body, name="shards_to_re", grid=(k // tr,),
                   in_specs=[pl.BlockSpec((N_SHARD, tr, W_IN_SHARD), lambda i: (0, i, 0))],
                   out_specs=pl.BlockSpec((tr, N_RE), lambda i: (i, 0)),
                   out_shape=jax.ShapeDtypeStruct((k, N_RE), w_all.dtype), compiler_params=_cp("parallel"))(w_all)


def _pair_add_to_shards(parts, got, pieces, shard_rows, core, name):
    n = parts[0].shape[1]
    hn = n // 2
    tc = 128
    nt = hn // tc
    ns = len(parts)
    starts = [sum(p.shape[0] for p in parts[:s]) for s in range(ns)]
    moves = []
    for j, c0, c1, d0 in pieces:
        for s, p in enumerate(parts):
            lo, hi = max(d0, starts[s]), min(d0 + c1 - c0, starts[s] + p.shape[0])
            if lo < hi:
                moves.append((s, lo - starts[s], j, c0 + lo - d0, hi - lo))
    assert sum(m[4] for m in moves) == N_SHARD * shard_rows

    def body(core_ref, *refs):
        own, theirs, o_ref, acc = refs[:ns], refs[ns:2 * ns], refs[2 * ns], refs[2 * ns + 1]
        for s, r0, j, c0, rows in moves:
            acc[j, c0:c0 + rows, :] = own[s][r0:r0 + rows, :] + theirs[s][r0:r0 + rows, :]
        o_ref[...] = _bf(acc[...])

    return _pallas(
        body, name=name,
        grid_spec=pltpu.PrefetchScalarGridSpec(
            num_scalar_prefetch=1, grid=(nt,),
            in_specs=[pl.BlockSpec((p.shape[0], tc), lambda i, core_ref: (0, core_ref[0] * nt + i)) for p in parts]
            + [pl.BlockSpec((p.shape[0], tc), lambda i, core_ref: (0, i)) for p in parts],
            out_specs=pl.BlockSpec((N_SHARD, shard_rows, tc), lambda i, core_ref: (0, 0, i)),
            scratch_shapes=[pltpu.VMEM((N_SHARD, shard_rows, tc), F32)]),
        out_shape=jax.ShapeDtypeStruct((N_SHARD, shard_rows, hn), BF16),
        compiler_params=_cp("parallel"))(core, *parts, *got)


def _local_step(x, target, meta, norm_pre_w, w_re, conv_w, conv_b, dt_bias, a_log, d_skip, ssd_norm_w, sinks,
                w_out_shard, norm_post_w, place):
    seq = x.shape[0]
    t = PAD_LEAD + N_META + seq
    hpad = jnp.concatenate([jnp.zeros((PAD_LEAD, D_MODEL), F32), meta, x], axis=0)
    dt_bias_l, a_log_l, d_skip_l = _spread_heads(dt_bias), _spread_heads(a_log), _spread_heads(d_skip)
    cos_t, sin_t = _rope_tables(t)
    sink_v = sinks.reshape(Q_HEADS)

    proj, hn, w_out_all = _inproj(hpad, norm_pre_w, w_re, w_out_shard)
    w_out = w_out_all.reshape(D_MIX, D_MODEL)
    xbc = _conv_fwd(proj, conv_w, conv_b)
    dt, acs, acst = _dt_prep(proj, dt_bias_l, a_log_l)
    y_ssd, ymix, states = _ssd_fwd(xbc, proj, dt, acs, acst, d_skip_l, ssd_norm_w)
    qr, kr = _rope(proj, OQ, proj, OK, cos_t, sin_t)
    amix = _attn_fwd(qr, kr, proj, sink_v)
    out = _outproj(ymix, amix, w_out)
    dout, dy_t, loss_blk, g_norm_post = _post_loss(out, x, target, norm_post_w)

    g_out_y = _tn_matmul(ymix, dout, "gw_out_y")
    g_out_a, got_y = _tn_matmul(amix, dout, "gw_out_a", carry=g_out_y)
    dmix, got_a = _nt_matmul(dout, w_out, "dmix", carry=g_out_a)
    ga_out = _reduce_pair([g_out_y, g_out_a], [got_y, got_a], [(j, 0, W_OUT_SHARD, W_OUT_SHARD * j) for j in range(N_SHARD)],
                          W_OUT_SHARD, place, "gw_out")
    dq_r, dg, dk_r, dv, gs, slabs_out = _attn_bwd(qr, kr, proj, dmix, sink_v, ga_out)
    g_w_out = _reduce_finish(ga_out, slabs_out, place, "gw_out")
    dq, dk = _rope(dq_r, 0, dk_r, 0, cos_t, -sin_t)
    dz, dxs, db, dc, dacs_g, ddt_g, g_ssd_norm, gdsk = _ssd_bwd(dmix, y_ssd, xbc, proj, dt, acs, acst, states,
                                                                d_skip_l, ssd_norm_w)
    draw, ga, gb = _dt_bwd(dacs_g, ddt_g, dt, proj, dt_bias_l, a_log_l)
    dxs_p, gcw0, gcb0 = _conv_bwd([dxs], proj, conv_w, conv_b, 0, "conv_bwd_x")
    dbc_p, gcw1, gcb1 = _conv_bwd([db, dc], proj, conv_w, conv_b, D_SSD, "conv_bwd_bc")
    tail = jnp.concatenate([dk, _bf(dv), draw, jnp.zeros((t, DT_SLAB - LANES), BF16)], axis=1)
    segs = [dz, dxs_p, dbc_p, dq, dg, tail]
    g_parts, got_parts = [_tn_matmul(segs[0], hn, "gw_in_0")], []
    for s in range(1, len(segs)):
        part, got = _tn_matmul(segs[s], hn, "gw_in_%d" % s, carry=g_parts[-1])
        g_parts.append(part)
        got_parts.append(got)
    ga_in = _reduce_pair(g_parts, got_parts, _column_pieces(), W_IN_SHARD, place, "gw_in")
    dh, g_norm_pre, slabs_in = _dinproj(segs, w_re, hpad, norm_pre_w, dy_t, ga_in)
    g_w_in_half = _chip_sum(ga_in, slabs_in, place, "gw_in_chip_sum")

    gdsk_l = jnp.concatenate([gdsk[g, 0:1, 0:GROUPS] for g in range(GROUPS)], axis=1)
    gdsk_l = jnp.pad(gdsk_l, ((0, 0), (0, LANES - GROUPS * GROUPS)))
    grads = dict(
        meta_tokens=dh[PAD_LEAD:ROW0], norm_pre_w=g_norm_pre, w_in_half=g_w_in_half,
        conv_w=jnp.concatenate([gcw0, gcw1], axis=1), conv_b=jnp.concatenate([gcb0, gcb1], axis=1),
        dt_bias=_gather_heads(gb), a_log=_gather_heads(ga), d_skip=_gather_heads(gdsk_l), ssd_norm_w=g_ssd_norm,
        attn_sinks=gs[0:1, :Q_HEADS], w_out=g_w_out, norm_post_w=g_norm_post)
    return loss_blk[0, 0], dh[ROW0:], grads


ANY = pl.BlockSpec(memory_space=pl.ANY)
MESH = pl.DeviceIdType.MESH
GATHER_CHUNKS = 4
PAIR_CHUNKS = 8
JOIN_CHUNKS = 8


def _rcopy(src, dst, ssem, rsem, dev):
    return pltpu.make_async_remote_copy(src_ref=src, dst_ref=dst, send_sem=ssem, recv_sem=rsem, device_id=dev,
                                        device_id_type=MESH)


def _place():
    x, y, c = lax.axis_index("x"), lax.axis_index("y"), lax.axis_index("c")
    chips = [(1 - x, y), (x, 1 - y), (1 - x, 1 - y)]
    return x, y, c, chips


def _gather_plan(x_ref, out_ref, send_sems, recv_sems, local_sems, hr, kc):
    ch = hr // kc
    assert ch * kc == hr and ch % 16 == 0
    x, y, c, chips = _place()
    me = 2 * x + y
    sibling = (x, y, 1 - c)

    def piece(chip, hc, k):
        return out_ref.at[chip, pl.ds(hc * hr + k * ch, ch), :]

    def local():
        return [pltpu.make_async_copy(x_ref.at[pl.ds(k * ch, ch), :], out_ref.at[me, pl.ds(k * ch, ch), :],
                                      local_sems.at[k]) for k in range(2 * kc)]

    def first():
        return [_rcopy(x_ref.at[pl.ds(c * hr + k * ch, ch), :], piece(me, c, k), send_sems.at[j * kc + k],
                       recv_sems.at[j * kc + k], (*chip, c)) for j, chip in enumerate(chips) for k in range(kc)]

    def passed(hc):
        return [_rcopy(piece(2 * chip[0] + chip[1], hc, k), piece(2 * chip[0] + chip[1], hc, k),
                       send_sems.at[(3 + j) * kc + k], recv_sems.at[(3 + j) * kc + k], sibling)
                for j, chip in enumerate(chips) for k in range(kc)]

    def arrivals():
        return [_rcopy(piece(2 * chip[0] + chip[1], c, k), piece(2 * chip[0] + chip[1], c, k), send_sems.at[j * kc + k],
                       recv_sems.at[j * kc + k], (*chip, c)) for j, chip in enumerate(chips) for k in range(kc)]

    def start():
        for cp in local() + first():
            cp.start()

    def forward():
        for arrived in arrivals():
            arrived.wait_recv()
        for fw in passed(c):
            fw.start()

    def finish():
        for cp in passed(1 - c):
            cp.wait_recv()
        for cp in first() + passed(c):
            cp.wait_send()
        for cp in local():
            cp.wait()

    return start, forward, finish


def _gather_shards(shard, name, kc, chip, small):
    r, n = shard.shape
    hr = r // 2
    qr = hr // 2
    ch = qr // kc
    assert ch * kc == qr and ch % 16 == 0
    nflow = 12
    tr = 256

    def body(x_ref, p_ref, out_ref, slots_ref, send_sems, recv_sems, *small_sems):
        start_small, wait_small = _chip_small_exchange(p_ref, slots_ref, *small_sems)
        start_small()
        x, y, c, _ = _place()
        me, cxn, cyn, cdg = 2 * x + y, 2 * (1 - x) + y, 2 * x + 1 - y, 2 * (1 - x) + 1 - y
        xn, yn, sibling = (1 - x, y, c), (x, 1 - y, c), (x, y, 1 - c)

        def piece(chip, hc, part, k):
            return out_ref.at[chip, pl.ds(hc * hr + part * qr + k * ch, ch), :]

        def own(part, k):
            return x_ref.at[pl.ds(c * hr + part * qr + k * ch, ch), :]

        def sems(flow, k):
            return send_sems.at[flow * kc + k], recv_sems.at[flow * kc + k]

        def arrival(flow, chip, hc, part, k):
            return _rcopy(piece(chip, hc, part, k), piece(chip, hc, part, k), *sems(flow, k), sibling)

        sends = []
        for flow, part, peer in ((0, 0, xn), (1, 1, yn), (2, 0, yn), (3, 1, xn)):
            sends += [_rcopy(own(part, k), piece(me, c, part, k), *sems(flow, k), peer) for k in range(kc)]
        for cp in sends:
            cp.start()
        landing = ((0, cxn, 0), (1, cyn, 1), (2, cyn, 0), (3, cxn, 1), (4, cdg, 0), (5, cdg, 1))
        for i, (flow, chip, part) in enumerate(landing):
            for k in range(kc):
                arrival(flow, chip, c, part, k).wait_recv()
                if flow < 2:
                    on = _rcopy(piece(chip, c, part, k), piece(chip, c, part, k), *sems(4 + flow, k),
                                yn if flow == 0 else xn)
                    on.start()
                    sends.append(on)
                fw = _rcopy(piece(chip, c, part, k), piece(chip, c, part, k), *sems(6 + i, k), sibling)
                fw.start()
                sends.append(fw)
        for i, (flow, chip, part) in enumerate(landing):
            for k in range(kc):
                arrival(6 + i, chip, 1 - c, part, k).wait_recv()
        for cp in sends:
            cp.wait_send()
        wait_small()

    full = jax.ShapeDtypeStruct((N_SHARD, r, n), shard.dtype)
    others, slots = _pallas(
        body, name=name, in_specs=[ANY, ANY], out_specs=[ANY, ANY],
        out_shape=[full, jax.ShapeDtypeStruct((N_SHARD,) + small.shape, F32)],
        scratch_shapes=[pltpu.SemaphoreType.DMA((nflow * kc,)), pltpu.SemaphoreType.DMA((nflow * kc,)),
                        pltpu.SemaphoreType.DMA((3,)), pltpu.SemaphoreType.DMA((3,)), pltpu.SemaphoreType.DMA])(
                            shard, small)

    def place(chip_ref, own_ref, all_ref, o_ref):
        o_ref[0] = own_ref[...]

    gathered = _pallas(
        place, name=name + "_own",
        grid_spec=pltpu.PrefetchScalarGridSpec(
            num_scalar_prefetch=1, grid=(r // tr,),
            in_specs=[pl.BlockSpec((tr, n), lambda i, chip_ref: (i, 0)), ANY],
            out_specs=pl.BlockSpec((1, tr, n), lambda i, chip_ref: (chip_ref[0], i, 0))),
        out_shape=full, input_output_aliases={2: 0}, compiler_params=_cp("parallel"))(chip, shard, others)
    return gathered, slots


def _pair_copies(src_ref, dst_ref, send_sems, recv_sems):
    hn = src_ref.shape[1] // 2
    cw = hn // PAIR_CHUNKS
    assert cw * PAIR_CHUNKS == hn and cw % LANES == 0
    x, y, c, _ = _place()
    return [_rcopy(src_ref.at[:, pl.ds((1 - c) * hn + k * cw, cw)], dst_ref.at[:, pl.ds(k * cw, cw)],
                   send_sems.at[k], recv_sems.at[k], (x, y, 1 - c)) for k in range(PAIR_CHUNKS)]


def _pair_send(parts, name):
    n = parts[0].shape[1]
    hn = n // 2
    kc = PAIR_CHUNKS
    cw = hn // kc
    assert cw * kc == hn and cw % LANES == 0
    ns = len(parts)

    def body(*refs):
        srcs, dsts, send_sems, recv_sems = refs[:ns], refs[ns:2 * ns], refs[2 * ns], refs[2 * ns + 1]
        x, y, c, _ = _place()
        cps = [_rcopy(srcs[s].at[:, pl.ds((1 - c) * hn + k * cw, cw)], dsts[s].at[:, pl.ds(k * cw, cw)],
                      send_sems.at[s * kc + k], recv_sems.at[s * kc + k], (x, y, 1 - c))
               for s in range(ns) for k in range(kc)]
        for cp in cps:
            cp.start()
        for cp in cps:
            cp.wait()

    return _pallas(
        body, name=name, in_specs=[ANY] * ns, out_specs=[ANY] * ns,
        out_shape=[jax.ShapeDtypeStruct((p.shape[0], hn), F32) for p in parts],
        scratch_shapes=[pltpu.SemaphoreType.DMA((ns * kc,)), pltpu.SemaphoreType.DMA((ns * kc,))])(*parts)


REDUCE_TILE = 256


def _exchange_copies(g_ref, got_ref, send_sems, recv_sems):
    hn = g_ref.shape[2]
    kc = GATHER_CHUNKS
    cw = hn // kc
    assert cw * kc == hn and cw % LANES == 0
    x, y, c, chips = _place()
    return [_rcopy(g_ref.at[2 * chip[0] + chip[1], :, pl.ds(k * cw, cw)], got_ref.at[j, :, pl.ds(k * cw, cw)],
                   send_sems.at[j * kc + k], recv_sems.at[j * kc + k], (*chip, c))
            for j, chip in enumerate(chips) for k in range(kc)]


def _exchange_scratch():
    return [pltpu.SemaphoreType.DMA((3 * GATHER_CHUNKS,)), pltpu.SemaphoreType.DMA((3 * GATHER_CHUNKS,))]


def _exchange_shape(ga):
    return jax.ShapeDtypeStruct((3,) + ga.shape[1:], ga.dtype)


def _chip_sum(ga, got, place, name):
    _, r, hn = ga.shape
    tc = REDUCE_TILE
    nt = hn // tc

    def body(place_ref, own_ref, got_ref, o_ref):
        acc = own_ref[0].astype(F32)
        for j in range(3):
            acc = acc + got_ref[j].astype(F32)
        o_ref[...] = acc

    return _pallas(
        body, name=name,
        grid_spec=pltpu.PrefetchScalarGridSpec(
            num_scalar_prefetch=1, grid=(nt,),
            in_specs=[pl.BlockSpec((1, r, tc), lambda i, place_ref: (place_ref[0], 0, i)),
                      pl.BlockSpec((3, r, tc), lambda i, place_ref: (0, 0, i))],
            out_specs=pl.BlockSpec((r, tc), lambda i, place_ref: (0, place_ref[1] * nt + i))),
        out_shape=jax.ShapeDtypeStruct((r, 2 * hn), F32), compiler_params=_cp("parallel"))(place, ga, got)


def _pair_join(buf, name, small=None):
    r, n = buf.shape
    hn = n // 2
    kc = JOIN_CHUNKS
    cw = hn // kc
    assert cw * kc == hn and cw % LANES == 0

    def body(in_ref, *refs):
        if small is None:
            out_ref, send_sems, recv_sems = refs
        else:
            p_ref, out_ref, slots_ref, send_sems, recv_sems = refs[:5]
            start_small, wait_small = _small_exchange(p_ref, slots_ref, *refs[5:])
            start_small()
        x, y, c, _ = _place()
        cps = [_rcopy(out_ref.at[:, pl.ds(c * hn + k * cw, cw)], out_ref.at[:, pl.ds(c * hn + k * cw, cw)],
                      send_sems.at[k], recv_sems.at[k], (x, y, 1 - c)) for k in range(kc)]
        for cp in cps:
            cp.start()
        for k in range(kc):
            cols = out_ref.at[:, pl.ds((1 - c) * hn + k * cw, cw)]
            _rcopy(cols, cols, send_sems.at[k], recv_sems.at[k], (x, y, 1 - c)).wait_recv()
        for cp in cps:
            cp.wait_send()
        if small is not None:
            wait_small()

    sems = [pltpu.SemaphoreType.DMA((kc,)), pltpu.SemaphoreType.DMA((kc,))]
    if small is None:
        return _pallas(body, name=name, in_specs=[ANY], out_specs=ANY, out_shape=jax.ShapeDtypeStruct((r, n), F32),
                       input_output_aliases={0: 0}, scratch_shapes=sems)(buf)
    return _pallas(
        body, name=name, in_specs=[ANY, ANY], out_specs=[ANY, ANY],
        out_shape=[jax.ShapeDtypeStruct((r, n), F32), jax.ShapeDtypeStruct((N_DEV,) + small.shape, F32)],
        input_output_aliases={0: 0}, scratch_shapes=sems + _small_scratch())(buf, small)


def _reduce_pair(parts, got, pieces, shard_rows, place, tag):
    if len(got) < len(parts):
        got = list(got) + list(_pair_send(parts[len(got):], tag + "_pair_send"))
    return _pair_add_to_shards(parts, got, pieces, shard_rows, place[1:2], tag + "_pair_add")


def _reduce_finish(ga, slabs, place, tag):
    return _pair_join(_chip_sum(ga, slabs, place, tag + "_chip_sum"), tag + "_pair_join")


N_DEV = 8


def _small_exchange(p_ref, slots_ref, send_sems, recv_sems, local_sem):
    x, y, c, _ = _place()
    my = 4 * x + 2 * y + c

    def sends():
        return [_rcopy(p_ref, slots_ref.at[my], send_sems.at[k - 1], recv_sems.at[k - 1],
                       (x ^ ((k >> 2) & 1), y ^ ((k >> 1) & 1), c ^ (k & 1))) for k in range(1, N_DEV)]

    def local():
        return pltpu.make_async_copy(p_ref, slots_ref.at[my], local_sem)

    def start():
        local().start()
        for cp in sends():
            cp.start()

    def wait():
        for k in range(1, N_DEV):
            _rcopy(p_ref, slots_ref.at[my ^ k], send_sems.at[k - 1], recv_sems.at[k - 1], (x, y, c)).wait_recv()
        for cp in sends():
            cp.wait_send()
        local().wait()

    return start, wait


def _chip_small_exchange(p_ref, slots_ref, send_sems, recv_sems, local_sem):
    x, y, c, chips = _place()
    me = 2 * x + y

    def sends():
        return [_rcopy(p_ref, slots_ref.at[me], send_sems.at[j], recv_sems.at[j], (*chip, c))
                for j, chip in enumerate(chips)]

    def local():
        return pltpu.make_async_copy(p_ref, slots_ref.at[me], local_sem)

    def start():
        local().start()
        for cp in sends():
            cp.start()

    def wait():
        for j, chip in enumerate(chips):
            slot = slots_ref.at[2 * chip[0] + chip[1]]
            _rcopy(slot, slot, send_sems.at[j], recv_sems.at[j], (*chip, c)).wait_recv()
        for cp in sends():
            cp.wait_send()
        local().wait()

    return start, wait


def _small_scratch():
    return [pltpu.SemaphoreType.DMA((N_DEV - 1,)), pltpu.SemaphoreType.DMA((N_DEV - 1,)), pltpu.SemaphoreType.DMA]


def _sum_slots(slots, name):
    _, rows, n = slots.shape

    def body(s_ref, o_ref):
        acc = s_ref[0]
        for j in range(1, N_DEV):
            acc = acc + s_ref[j]
        o_ref[...] = acc

    vm = pl.BlockSpec(memory_space=pltpu.VMEM)
    return _pallas(body, name=name, in_specs=[vm], out_specs=vm, out_shape=jax.ShapeDtypeStruct((rows, n), F32))(slots)


def _adamw(w, g, m, v, name):
    r, n = w.shape
    tr = _tile(r, 256, 8)
    c1 = 1.0 / (1.0 - ADAM_B1 ** ADAM_STEP)
    c2 = 1.0 / (1.0 - ADAM_B2 ** ADAM_STEP)

    def body(w_ref, g_ref, m_ref, v_ref, d_ref, mo_ref, vo_ref, go_ref):
        gv = g_ref[...]
        mn = ADAM_B1 * m_ref[...] + (1.0 - ADAM_B1) * gv
        vn = ADAM_B2 * v_ref[...] + (1.0 - ADAM_B2) * (gv * gv)
        d_ref[...] = -ADAM_LR * ((mn * c1) / (jnp.sqrt(vn * c2) + ADAM_EPS) + ADAM_WD * w_ref[...])
        mo_ref[...] = mn
        vo_ref[...] = vn
        go_ref[...] = gv

    spec = pl.BlockSpec((tr, n), lambda i: (i, 0))
    shp = jax.ShapeDtypeStruct((r, n), F32)
    return _pallas(body, name=name, grid=(r // tr,), in_specs=[spec] * 4, out_specs=[spec] * 4, out_shape=[shp] * 4,
                   compiler_params=_cp("parallel"))(w, g, m, v)


PACK_W = 1024
SMALL_REPL = ("norm_pre_w", "conv_b", "ssd_norm_w", "norm_post_w")
SMALL_HEAD = ("dt_bias", "a_log", "d_skip", "attn_sinks")


def _rows(a):
    return a.reshape(-1, PACK_W)


def _head_row(vals, extra=None):
    parts = [vals[n].reshape(1, -1) for n in SMALL_HEAD]
    if extra is not None:
        parts.append(extra.reshape(1, 1))
    row = jnp.concatenate(parts, axis=1)
    return jnp.pad(row, ((0, 0), (0, PACK_W - row.shape[1])))


def _pad_rows(a, rows):
    return jnp.pad(a, ((0, rows - a.shape[0]), (0, 0)))


def _pack_repl(vals, extra=None):
    body = jnp.concatenate([_rows(vals[n]) for n in SMALL_REPL] + [_head_row(vals, extra)], axis=0)
    return _pad_rows(body, 16)


def _unpack_repl(buf):
    out, r = {}, 0
    for n, k in zip(SMALL_REPL, (2, 4, 2, 2)):
        out[n] = buf[r:r + k].reshape(1, k * PACK_W)
        r += k
    col = 0
    for n, k in zip(SMALL_HEAD, (32, 32, 32, 16)):
        out[n] = buf[r:r + 1, col:col + k]
        col += k
    return out, buf[r, col]


def kernel(x, meta_tokens, norm_pre_w, w_in, conv_w, conv_b, dt_bias, a_log, d_skip, ssd_norm_w, attn_sinks, w_out, norm_post_w, loss_target, m_meta_tokens, m_norm_pre_w, m_w_in, m_conv_w, m_conv_b, m_dt_bias, m_a_log, m_d_skip, m_ssd_norm_w, m_attn_sinks, m_w_out, m_norm_post_w, v_meta_tokens, v_norm_pre_w, v_w_in, v_conv_w, v_conv_b, v_dt_bias, v_a_log, v_d_skip, v_ssd_norm_w, v_attn_sinks, v_w_out, v_norm_post_w):
    names = ("meta_tokens", "norm_pre_w", "w_in", "conv_w", "conv_b", "dt_bias", "a_log", "d_skip", "ssd_norm_w",
             "attn_sinks", "w_out", "norm_post_w")
    w = dict(zip(names, (meta_tokens, norm_pre_w, w_in, conv_w, conv_b, dt_bias, a_log, d_skip, ssd_norm_w, attn_sinks,
                         w_out, norm_post_w)))
    m = dict(zip(names, (m_meta_tokens, m_norm_pre_w, m_w_in, m_conv_w, m_conv_b, m_dt_bias, m_a_log, m_d_skip,
                         m_ssd_norm_w, m_attn_sinks, m_w_out, m_norm_post_w)))
    v = dict(zip(names, (v_meta_tokens, v_norm_pre_w, v_w_in, v_conv_w, v_conv_b, v_dt_bias, v_a_log, v_d_skip,
                         v_ssd_norm_w, v_attn_sinks, v_w_out, v_norm_post_w)))
    cx, cy, cc = lax.axis_index("x"), lax.axis_index("y"), lax.axis_index("c")
    chip = 2 * cx + cy
    meta_cols = D_MODEL // N_SHARD
    conv_cols = D_CONV // N_SHARD

    place = jnp.stack([chip, cc]).astype(jnp.int32)
    small = jnp.concatenate([_pad_rows(conv_w[0], 8), _rows(meta_tokens)], axis=0)
    w_in_all, small_all = _gather_shards(_bf(w_in[0]), "gather_w_in", GATHER_CHUNKS, place[0:1], small)
    w_re = _shards_to_re(w_in_all)
    conv_full = jnp.transpose(small_all[:, 0:CONV_WIDTH], (1, 0, 2)).reshape(CONV_WIDTH, D_CONV)
    meta_full = jnp.transpose(small_all[:, 8:16].reshape(N_SHARD, N_META, meta_cols), (1, 0, 2)).reshape(N_META, D_MODEL)

    loss_dev, grad_x, g = _local_step(x[0], loss_target[0], meta_full, norm_pre_w, w_re, conv_full, conv_b, dt_bias,
                                      a_log, d_skip, ssd_norm_w, attn_sinks, _bf(w_out[0]), norm_post_w, place)
    g_w_out = g["w_out"]

    packed = jnp.concatenate([_rows(g["conv_w"]), _rows(g["meta_tokens"]), _pack_repl(g, loss_dev)], axis=0)
    g_w_in, slots = _pair_join(g["w_in_half"], "gw_in_pair_join", small=packed)
    red = _sum_slots(slots, "reduce_small")
    g_conv_full = red[0:16].reshape(CONV_WIDTH, D_CONV)
    g_meta_full = red[16:48].reshape(N_META, D_MODEL)
    g_small, loss = _unpack_repl(red[48:64])
    grads = dict(g_small)
    grads["w_in"] = g_w_in
    grads["w_out"] = g_w_out
    grads["conv_w"] = lax.dynamic_slice(g_conv_full, (0, chip * conv_cols), (CONV_WIDTH, conv_cols))
    grads["meta_tokens"] = lax.dynamic_slice(g_meta_full, (0, chip * meta_cols), (N_META, meta_cols))

    upd = {}
    upd["w_in"] = [jnp.swapaxes(a, 0, 1) for a in _adamw(jnp.swapaxes(w_in[0], 0, 1), g_w_in, jnp.swapaxes(m_w_in[0], 0, 1),
                                                         jnp.swapaxes(v_w_in[0], 0, 1), "adamw_w_in")]
    grads["w_in"] = upd["w_in"][3]
    upd["w_out"] = _adamw(w_out[0], g_w_out, m_w_out[0], v_w_out[0], "adamw_w_out")
    grads["w_out"] = upd["w_out"][3]

    def pack_small(vals, conv, meta):
        return jnp.concatenate([_pad_rows(conv.reshape(CONV_WIDTH, conv_cols), 8), _rows(meta), _pack_repl(vals)], axis=0)

    sm = _adamw(pack_small(w, w["conv_w"], w["meta_tokens"]), pack_small(grads, grads["conv_w"], grads["meta_tokens"]),
                pack_small(m, m["conv_w"], m["meta_tokens"]), pack_small(v, v["conv_w"], v["meta_tokens"]),
                "adamw_small")
    for n in names:
        if n not in ("w_in", "w_out"):
            upd[n] = [None, None, None]
    for k, buf in enumerate(sm[:3]):
        upd["conv_w"][k] = buf[0:CONV_WIDTH]
        upd["meta_tokens"][k] = buf[8:16].reshape(N_META, meta_cols)
        rest, _ = _unpack_repl(buf[16:32])
        for n in SMALL_REPL + SMALL_HEAD:
            upd[n][k] = rest[n]

    def shaped(n, a):
        return a.reshape(w[n].shape)

    outs = [loss, grad_x[None]]
    outs += [shaped(n, grads[n]) for n in names]
    for k in range(3):
        outs += [shaped(n, upd[n][k]) for n in names]
    return tuple(outs)
```

```python
import functools

import jax
import jax.numpy as jnp
from jax import lax
from jax.experimental import pallas as pl
from jax.experimental.pallas import tpu as pltpu

F32 = jnp.float32
BF16 = jnp.bfloat16

D_MODEL = 2048
CHUNK = 64
N_META = 16
PAD_LEAD = CHUNK - N_META
ROW0 = PAD_LEAD + N_META
EPS = 1e-6
SSD_HEADS = 32
HEAD_DIM = 64
GROUPS = 8
HPG = SSD_HEADS // GROUPS
D_STATE = 128
D_SSD = 2048
GROUP_W = D_SSD // GROUPS
CONV_WIDTH = 4
D_CONV = 4096
Q_HEADS = 16
KV_HEADS = 4
REP = Q_HEADS // KV_HEADS
D_ATT = 1024
D_KV = 256
BAND_CHUNKS = 3
ROPE_THETA = 10000.0
D_MIX = D_SSD + D_ATT
D_IN = 8736
N_SHARD = 4
W_IN_SHARD = D_IN // N_SHARD
W_OUT_SHARD = D_MIX // N_SHARD

OZ, OXS, OB, OC, OQ, OG, OK, OV, ODT = 0, 2048, 4096, 5120, 6144, 7168, 8192, 8448, 8704
DT_SLAB = 512
N_RE = ODT + DT_SLAB
LANES = 128

ADAM_LR, ADAM_B1, ADAM_B2, ADAM_EPS, ADAM_WD, ADAM_STEP = 0.001, 0.9, 0.999, 1e-08, 0.01, 10

SSD_FWD_GROUPS_PER_STEP = 4
SSD_BWD_GROUPS_PER_STEP = 8
SEG_TILE = 1024
VMEM_LIMIT = 52 * 1024 * 1024
NEG = -1e30
HI = lax.Precision.HIGHEST


def _pallas(body, **kw):
    return pl.pallas_call(body, **kw)


def _cp(*sem):
    return pltpu.CompilerParams(dimension_semantics=sem, vmem_limit_bytes=VMEM_LIMIT)


def _tile(n, cap, mult=16):
    best = None
    for d in range(mult, min(n, cap) + 1, mult):
        if n % d == 0:
            best = d
    assert best is not None, (n, cap)
    return best


def _nt(a, b):
    return lax.dot_general(a, b, (((1,), (1,)), ((), ())), preferred_element_type=F32)


def _tn(a, b):
    return lax.dot_general(a, b, (((0,), (0,)), ((), ())), preferred_element_type=F32)


def _mm(a, b):
    return jnp.dot(a, b, preferred_element_type=F32)


def _sigmoid(x):
    return 1.0 / (1.0 + jnp.exp(-x))


def _bf(x):
    return x.astype(BF16)


def _inproj(hpad, norm_w, w_re, w_out_shard):
    t, d = hpad.shape
    n = w_re.shape[1]
    tm, tn = _tile(t, 1040), 1024
    ni, nj = t // tm, n // tn
    r_out, n_out = w_out_shard.shape
    kc = GATHER_CHUNKS

    def body(h_ref, nw_ref, w_ref, ws_ref, proj_ref, hn_ref, wall_ref, hn_s, send_sems, recv_sems, local_sems):
        i, j = pl.program_id(0), pl.program_id(1)
        start, forward, finish = _gather_plan(ws_ref, wall_ref, send_sems, recv_sems, local_sems, r_out // 2, kc)
        pl.when((i == 0) & (j == 0))(start)
        pl.when((i == ni // 2) & (j == 0))(forward)

        @pl.when(j == 0)
        def _():
            h = h_ref[...]
            ms = jnp.mean(h * h, axis=-1, keepdims=True)
            hn = _bf(h * lax.rsqrt(ms + EPS) * nw_ref[...])
            hn_s[...] = hn
            hn_ref[...] = hn
        proj_ref[...] = _mm(hn_s[...], w_ref[...])
        pl.when((i == ni - 1) & (j == nj - 1))(finish)

    return _pallas(
        body, name="inproj", grid=(ni, nj),
        in_specs=[pl.BlockSpec((tm, d), lambda i, j: (i, 0)), pl.BlockSpec((1, d), lambda i, j: (0, 0)),
                  pl.BlockSpec((d, tn), lambda i, j: (0, j)), ANY],
        out_specs=[pl.BlockSpec((tm, tn), lambda i, j: (i, j)), pl.BlockSpec((tm, d), lambda i, j: (i, 0)), ANY],
        out_shape=[jax.ShapeDtypeStruct((t, n), F32), jax.ShapeDtypeStruct((t, d), BF16),
                   jax.ShapeDtypeStruct((N_SHARD, r_out, n_out), w_out_shard.dtype)],
        scratch_shapes=[pltpu.VMEM((tm, d), BF16), pltpu.SemaphoreType.DMA((6 * kc,)), pltpu.SemaphoreType.DMA((6 * kc,)),
                        pltpu.SemaphoreType.DMA((2 * kc,))],
        compiler_params=_cp("arbitrary", "arbitrary"))(hpad, norm_w, w_re, w_out_shard)


def _conv_fwd(proj, conv_w, conv_b):
    t = proj.shape[0]
    tc = 256
    off = OXS // tc

    def body(x_ref, w_ref, b_ref, o_ref):
        x = x_ref[...]
        w = w_ref[...]
        row = lax.broadcasted_iota(jnp.int32, (t, tc), 0)
        u = b_ref[...] + w[3:4, :] * x
        for k in range(1, CONV_WIDTH):
            u = u + w[3 - k:4 - k, :] * jnp.where(row >= k, pltpu.roll(x, k, 0), 0.0)
        h = 0.5 * u
        o_ref[...] = h + h * jnp.tanh(h)

    return _pallas(
        body, name="conv_fwd", grid=(D_CONV // tc,),
        in_specs=[pl.BlockSpec((t, tc), lambda j: (0, j + off)), pl.BlockSpec((CONV_WIDTH, tc), lambda j: (0, j)),
                  pl.BlockSpec((1, tc), lambda j: (0, j))],
        out_specs=pl.BlockSpec((t, tc), lambda j: (0, j)),
        out_shape=jax.ShapeDtypeStruct((t, D_CONV), F32),
        compiler_params=_cp("parallel"))(proj, conv_w, conv_b)


def _softplus(u):
    e = jnp.exp(-jnp.abs(u))
    w = 1.0 + e
    l1p = jnp.where(w == 1.0, e, jnp.log(w) * (e / jnp.where(w == 1.0, 1.0, w - 1.0)))
    return jnp.maximum(u, 0.0) + l1p


def _chunks_per_step(nc):
    return max(d for d in range(1, 14) if nc % d == 0)


def _dt_prep(proj, dt_bias_l, a_log_l):
    t = proj.shape[0]
    nc = t // CHUNK
    q = CHUNK
    cps = _chunks_per_step(nc)
    rows = cps * q

    def body(raw_ref, bias_ref, alog_ref, dt_ref, acs_ref, acst_ref):
        ri = lax.broadcasted_iota(jnp.int32, (q, q), 0)
        ci = lax.broadcasted_iota(jnp.int32, (q, q), 1)
        tri = (ri >= ci).astype(F32)
        neg_a = -jnp.exp(alog_ref[...])
        for k in range(cps):
            rk = slice(q * k, q * (k + 1))
            sp = _softplus(raw_ref[rk, :] + bias_ref[...])
            row = pl.program_id(0) * rows + q * k + lax.broadcasted_iota(jnp.int32, (q, LANES), 0)
            dt = jnp.where(row >= PAD_LEAD, sp, 0.0)
            acs = jnp.dot(tri, dt * neg_a, preferred_element_type=F32, precision=HI)
            dt_ref[rk, :] = dt
            acs_ref[rk, :] = acs
            acst_ref[k] = acs.T

    return _pallas(
        body, name="dt_prep", grid=(nc // cps,),
        in_specs=[pl.BlockSpec((rows, LANES), lambda c: (c, ODT // LANES)), pl.BlockSpec((1, LANES), lambda c: (0, 0)),
                  pl.BlockSpec((1, LANES), lambda c: (0, 0))],
        out_specs=[pl.BlockSpec((rows, LANES), lambda c: (c, 0)), pl.BlockSpec((rows, LANES), lambda c: (c, 0)),
                   pl.BlockSpec((cps, LANES, q), lambda c: (c, 0, 0))],
        out_shape=[jax.ShapeDtypeStruct((t, LANES), F32), jax.ShapeDtypeStruct((t, LANES), F32),
                   jax.ShapeDtypeStruct((nc, LANES, q), F32)],
        compiler_params=_cp("parallel"))(proj, dt_bias_l, a_log_l)


def _head_cols(blk, idx):
    lane = lax.broadcasted_iota(jnp.int32, blk.shape, 1)
    return jnp.sum(jnp.where(lane == idx, blk, 0.0), axis=1, keepdims=True)


class _HeadVals:
    pass


def _lane_head(shape):
    return lax.broadcasted_iota(jnp.int32, shape, len(shape) - 1) >> 6


def _group_heads(g, gi, dtb, acsb, acst_ref, dskb):
    q = dtb.shape[0]
    hv = _HeadVals()
    lh = _lane_head((1, GROUP_W))
    hv.dt = jnp.zeros((q, GROUP_W), F32)
    hv.acs = jnp.zeros((q, GROUP_W), F32)
    hv.acs_last = jnp.zeros((1, GROUP_W), F32)
    hv.dsk = jnp.zeros((1, GROUP_W), F32)
    rows = []
    for r in range(HPG):
        idx = GROUPS * g + r
        sel = lh == r
        acs_r = acst_ref[0, GROUPS * gi + r:GROUPS * gi + r + 1, :]
        rows.append(acs_r)
        hv.dt = jnp.where(sel, _head_cols(dtb, idx), hv.dt)
        hv.acs = jnp.where(sel, _head_cols(acsb, idx), hv.acs)
        hv.acs_last = jnp.where(sel, acs_r[:, q - 1:q], hv.acs_last)
        hv.dsk = jnp.where(sel, _head_cols(dskb, idx), hv.dsk)
    hv.acs_row = jnp.concatenate(rows, axis=1)
    return hv


def _head_tri(q, lower):
    ri = lax.broadcasted_iota(jnp.int32, (q, GROUP_W), 0)
    li = lax.broadcasted_iota(jnp.int32, (q, GROUP_W), 1) & (HEAD_DIM - 1)
    return ri >= li if lower else ri <= li


def _block_diag_mask():
    rb = lax.broadcasted_iota(jnp.int32, (GROUP_W, GROUP_W), 0) >> 6
    cb = lax.broadcasted_iota(jnp.int32, (GROUP_W, GROUP_W), 1) >> 6
    return rb == cb


def _block_diag(v, mask):
    return jnp.where(mask, jnp.concatenate([v] * HPG, axis=0), jnp.zeros((), v.dtype))


def _head_sums(v, r):
    return jnp.sum(jnp.where(_lane_head((1, GROUP_W)) == r, v, 0.0), axis=1, keepdims=True)


def _ssd_fwd(xbc, proj, dt, acs, acst, d_skip_l, ssd_norm_w):
    t = xbc.shape[0]
    q = CHUNK
    nc = t // q

    gps = SSD_FWD_GROUPS_PER_STEP
    gw, sw = gps * GROUP_W, gps * D_STATE

    def body(xs_ref, b_ref, c_ref, dt_ref, acs_ref, acst_ref, z_ref, dsk_ref, nw_ref,
             y_ref, ymix_ref, st_ref, state):
        @pl.when(pl.program_id(1) == 0)
        def _():
            state[...] = jnp.zeros_like(state)

        lower = _head_tri(q, True)
        bd_mask = _block_diag_mask()
        for gi in range(gps):
            g = gps * pl.program_id(0) + gi
            cols = slice(GROUP_W * gi, GROUP_W * (gi + 1))
            x = xs_ref[:, cols]
            bmb = _bf(b_ref[:, D_STATE * gi:D_STATE * (gi + 1)])
            cmb = _bf(c_ref[:, D_STATE * gi:D_STATE * (gi + 1)])
            hv = _group_heads(g, gi, dt_ref[...], acs_ref[...], acst_ref, dsk_ref[...])
            decay = jnp.exp(jnp.where(lower, hv.acs - hv.acs_row, NEG))
            m_all = _bf(_nt(cmb, jnp.concatenate([bmb] * HPG, axis=0)) * decay)
            xdt = x * hv.dt
            s_prev = state[gi]
            st_ref[0, gi] = s_prev
            y = (_mm(m_all, _block_diag(_bf(xdt), bd_mask)) + _mm(cmb, _bf(s_prev)) * jnp.exp(hv.acs) + hv.dsk * x)
            state[gi] = jnp.exp(hv.acs_last) * s_prev + _tn(bmb, _bf(xdt * jnp.exp(hv.acs_last - hv.acs)))
            y_ref[:, cols] = y
            z = z_ref[:, cols]
            yg = y * (z * _sigmoid(z))
            ms = jnp.mean(yg * yg, axis=-1, keepdims=True)
            ymix_ref[:, cols] = _bf(yg * lax.rsqrt(ms + EPS) * nw_ref[:, cols])

    return _pallas(
        body, name="ssd_fwd", grid=(GROUPS // gps, nc),
        in_specs=[pl.BlockSpec((q, gw), lambda g, c: (c, g)),
                  pl.BlockSpec((q, sw), lambda g, c: (c, D_SSD // sw + g)),
                  pl.BlockSpec((q, sw), lambda g, c: (c, (D_SSD + GROUPS * D_STATE) // sw + g)),
                  pl.BlockSpec((q, LANES), lambda g, c: (c, 0)), pl.BlockSpec((q, LANES), lambda g, c: (c, 0)),
                  pl.BlockSpec((1, gps * GROUPS, q), lambda g, c: (c, g, 0)),
                  pl.BlockSpec((q, gw), lambda g, c: (c, g)),
                  pl.BlockSpec((1, LANES), lambda g, c: (0, 0)), pl.BlockSpec((1, gw), lambda g, c: (0, g))],
        out_specs=[pl.BlockSpec((q, gw), lambda g, c: (c, g)), pl.BlockSpec((q, gw), lambda g, c: (c, g)),
                   pl.BlockSpec((1, gps, D_STATE, GROUP_W), lambda g, c: (c, g, 0, 0))],
        out_shape=[jax.ShapeDtypeStruct((t, D_SSD), F32), jax.ShapeDtypeStruct((t, D_SSD), BF16),
                   jax.ShapeDtypeStruct((nc, GROUPS, D_STATE, GROUP_W), F32)],
        scratch_shapes=[pltpu.VMEM((gps, D_STATE, GROUP_W), F32)],
        compiler_params=_cp("parallel", "arbitrary"))(xbc, xbc, xbc, dt, acs, acst, proj, d_skip_l, ssd_norm_w)


def _swap_halves(v):
    lane = lax.broadcasted_iota(jnp.int32, v.shape, 1)
    return jnp.where((lane & (HEAD_DIM - 1)) < HEAD_DIM // 2, pltpu.roll(v, LANES - HEAD_DIM // 2, 1),
                     pltpu.roll(v, HEAD_DIM // 2, 1))


def _rope(qsrc, q_off, ksrc, k_off, cos_t, sin_t):
    t = qsrc.shape[0]
    tr = _tile(t, 832)
    q_scale = HEAD_DIM ** -0.5

    def body(q_ref, k_ref, cos_ref, sin_ref, qo_ref, ko_ref):
        cs = cos_ref[...]
        sn = sin_ref[...]
        for src, dst, width, scale in ((q_ref, qo_ref, D_ATT, q_scale), (k_ref, ko_ref, D_KV, 1.0)):
            for s in range(width // LANES):
                v = src[:, LANES * s:LANES * (s + 1)].astype(F32)
                dst[:, LANES * s:LANES * (s + 1)] = _bf((v * cs + _swap_halves(v) * sn) * scale)

    return _pallas(
        body, name="rope", grid=(t // tr,),
        in_specs=[pl.BlockSpec((tr, D_ATT), lambda i: (i, q_off // D_ATT)),
                  pl.BlockSpec((tr, D_KV), lambda i: (i, k_off // D_KV)),
                  pl.BlockSpec((tr, LANES), lambda i: (i, 0)), pl.BlockSpec((tr, LANES), lambda i: (i, 0))],
        out_specs=[pl.BlockSpec((tr, D_ATT), lambda i: (i, 0)), pl.BlockSpec((tr, D_KV), lambda i: (i, 0))],
        out_shape=[jax.ShapeDtypeStruct((t, D_ATT), BF16), jax.ShapeDtypeStruct((t, D_KV), BF16)],
        compiler_params=_cp("parallel"))(qsrc, ksrc, cos_t, sin_t)


def _attn_chunks_per_step(nc):
    return max(d for d in range(1, 6) if nc % d == 0)


def _band(ref, c):
    return [ref[pl.ds(pl.multiple_of(jnp.maximum(c - j, 0) * CHUNK, CHUNK), CHUNK), :] for j in (2, 1, 0)]


def _attn_probs(qh, kb, sink_col, valid):
    s = jnp.where(valid, _nt(qh, kb), NEG)
    m = jnp.maximum(jnp.max(s, axis=1, keepdims=True), sink_col)
    p = jnp.exp(s - m)
    psink = jnp.exp(sink_col - m)
    return p, psink, 1.0 / (jnp.sum(p, axis=1, keepdims=True) + psink)


def _attn_operands(c, q, k_refs, v_refs, sink_ref, h):
    qh = jnp.concatenate([q[:, HEAD_DIM * (REP * h + r):HEAD_DIM * (REP * h + r + 1)] for r in range(REP)], axis=0)
    kb = jnp.concatenate([k[:, HEAD_DIM * h:HEAD_DIM * (h + 1)] for k in k_refs], axis=0)
    vb = jnp.concatenate([_bf(v[:, HEAD_DIM * h:HEAD_DIM * (h + 1)]) for v in v_refs], axis=0)
    rows = lax.broadcasted_iota(jnp.int32, (REP * CHUNK, 1), 0) >> 6
    sink_col = jnp.zeros((REP * CHUNK, 1), F32)
    for r in range(REP):
        sink_col = jnp.where(rows == r, sink_ref[REP * h + r], sink_col)
    key_abs = (c - (BAND_CHUNKS - 1)) * CHUNK + lax.broadcasted_iota(jnp.int32, (1, BAND_CHUNKS * CHUNK), 1)
    return qh, kb, vb, sink_col, key_abs >= PAD_LEAD


def _attn_fwd(qr, kr, proj, sinks):
    t = qr.shape[0]
    nc = t // CHUNK
    cps = _attn_chunks_per_step(nc)
    rows = cps * CHUNK

    def body(q_ref, k_ref, v_ref, g_ref, sink_ref, o_ref):
        for j in range(cps):
            c = pl.program_id(0) * cps + j
            rj = slice(CHUNK * j, CHUNK * (j + 1))
            ks, vs = _band(k_ref, c), _band(v_ref, c)
            q = q_ref[rj, :]
            outs = []
            for h in range(KV_HEADS):
                qh, kb, vb, sink_col, valid = _attn_operands(c, q, ks, vs, sink_ref, h)
                p, _, inv = _attn_probs(qh, kb, sink_col, valid)
                o = _mm(_bf(p), vb) * inv
                outs += [o[CHUNK * r:CHUNK * (r + 1)] for r in range(REP)]
            att = jnp.concatenate(outs, axis=1)
            gate = g_ref[rj, :]
            o_ref[rj, :] = _bf(att * (gate * _sigmoid(gate)))

    return _pallas(
        body, name="attn_fwd", grid=(nc // cps,),
        in_specs=[pl.BlockSpec((rows, D_ATT), lambda i: (i, 0)), pl.BlockSpec((t, D_KV), lambda i: (0, 0)),
                  pl.BlockSpec((t, D_KV), lambda i: (0, OV // D_KV)),
                  pl.BlockSpec((rows, D_ATT), lambda i: (i, OG // D_ATT)), pl.BlockSpec(memory_space=pltpu.SMEM)],
        out_specs=pl.BlockSpec((rows, D_ATT), lambda i: (i, 0)),
        out_shape=jax.ShapeDtypeStruct((t, D_ATT), BF16),
        compiler_params=_cp("parallel"))(qr, kr, proj, proj, sinks)


def _outproj(ymix, amix, w_out):
    t = ymix.shape[0]
    tm, tn = _tile(t, 832), 1024

    def body(y_ref, a_ref, wy_ref, wa_ref, o_ref):
        o_ref[...] = _mm(y_ref[...], wy_ref[...]) + _mm(a_ref[...], wa_ref[...])

    return _pallas(
        body, name="outproj", grid=(t // tm, D_MODEL // tn),
        in_specs=[pl.BlockSpec((tm, D_SSD), lambda i, j: (i, 0)), pl.BlockSpec((tm, D_ATT), lambda i, j: (i, 0)),
                  pl.BlockSpec((D_SSD, tn), lambda i, j: (0, j)),
                  pl.BlockSpec((D_ATT, tn), lambda i, j: (D_SSD // D_ATT, j))],
        out_specs=pl.BlockSpec((tm, tn), lambda i, j: (i, j)),
        out_shape=jax.ShapeDtypeStruct((t, D_MODEL), F32),
        compiler_params=_cp("parallel", "parallel"))(ymix, amix, w_out, w_out)


def _post_loss(out, x, target, norm_post_w):
    t = out.shape[0]
    nc = t // CHUNK
    cps = _attn_chunks_per_step(nc)
    rows = cps * CHUNK

    def body(o_ref, *refs):
        x_refs, tg_refs = refs[:cps], refs[cps:2 * cps]
        nw_ref, dout_ref, dy_ref, loss_ref, gnw_ref = refs[2 * cps:]
        i = pl.program_id(0)

        @pl.when(i == 0)
        def _():
            loss_ref[...] = jnp.zeros_like(loss_ref)
            gnw_ref[...] = jnp.zeros_like(gnw_ref)

        nw = nw_ref[...]
        loss = jnp.zeros((), F32)
        gnw = jnp.zeros((1, D_MODEL), F32)
        for k in range(cps):
            rk = slice(CHUNK * k, CHUNK * (k + 1))
            frames = i * cps + k > 0
            o = o_ref[rk, :]
            rstd = lax.rsqrt(jnp.mean(o * o, axis=-1, keepdims=True) + EPS)
            n = o * rstd
            err = jnp.where(frames, x_refs[k][...] + n * nw - tg_refs[k][...], 0.0)
            loss = loss + jnp.sum(err * err)
            dy = err * (1.0 / D_MODEL)
            dy_ref[rk, :] = dy
            gnw = gnw + jnp.sum(dy * n, axis=0, keepdims=True)
            dn = dy * nw
            dout_ref[rk, :] = _bf(rstd * (dn - n * jnp.mean(dn * n, axis=-1, keepdims=True)))
        loss_ref[...] += loss * (0.5 / D_MODEL)
        gnw_ref[...] += gnw

    lower = [pl.BlockSpec((CHUNK, D_MODEL), functools.partial(lambda i, k: (jnp.maximum(i * cps + k - 1, 0), 0), k=k))
             for k in range(cps)]
    return _pallas(
        body, name="post_loss", grid=(nc // cps,),
        in_specs=[pl.BlockSpec((rows, D_MODEL), lambda i: (i, 0))] + lower + lower
        + [pl.BlockSpec((1, D_MODEL), lambda i: (0, 0))],
        out_specs=[pl.BlockSpec((rows, D_MODEL), lambda i: (i, 0)), pl.BlockSpec((rows, D_MODEL), lambda i: (i, 0)),
                   pl.BlockSpec((8, LANES), lambda i: (0, 0)), pl.BlockSpec((1, D_MODEL), lambda i: (0, 0))],
        out_shape=[jax.ShapeDtypeStruct((t, D_MODEL), BF16), jax.ShapeDtypeStruct((t, D_MODEL), F32),
                   jax.ShapeDtypeStruct((8, LANES), F32), jax.ShapeDtypeStruct((1, D_MODEL), F32)],
        compiler_params=_cp("arbitrary"))(out, *([x] * cps), *([target] * cps), norm_post_w)


def _carried(grid, carry):
    if carry is None:
        return [], [], [], [], lambda refs: None, lambda refs: None
    hn = carry.shape[1] // 2

    def at(ids, which):
        cond = None
        for d, size in enumerate(grid):
            here = pl.program_id(d) == (0 if which == "first" else size - 1)
            cond = here if cond is None else cond & here
        return cond

    def start(refs):
        @pl.when(at(grid, "first"))
        def _():
            for cp in _pair_copies(*refs):
                cp.start()

    def finish(refs):
        @pl.when(at(grid, "last"))
        def _():
            for cp in _pair_copies(*refs):
                cp.wait()

    return ([ANY], [ANY], [jax.ShapeDtypeStruct((carry.shape[0], hn), F32)],
            [pltpu.SemaphoreType.DMA((PAIR_CHUNKS,)), pltpu.SemaphoreType.DMA((PAIR_CHUNKS,))], start, finish)


def _nt_matmul(a, b, name, carry=None):
    t, k = a.shape
    n = b.shape[0]
    tm, tn = _tile(t, 832), 1024
    grid = (t // tm, n // tn)
    cin, cout, cshape, cscratch, start, finish = _carried(grid, carry)

    def body(a_ref, b_ref, *refs):
        o_ref = refs[len(cin)]
        comm = (refs[0], refs[2], refs[3], refs[4]) if carry is not None else None
        start(comm)
        o_ref[...] = _nt(a_ref[...], b_ref[...])
        finish(comm)

    res = _pallas(
        body, name=name, grid=grid,
        in_specs=[pl.BlockSpec((tm, k), lambda i, j: (i, 0)), pl.BlockSpec((tn, k), lambda i, j: (j, 0))] + cin,
        out_specs=[pl.BlockSpec((tm, tn), lambda i, j: (i, j))] + cout,
        out_shape=[jax.ShapeDtypeStruct((t, n), F32)] + cshape, scratch_shapes=cscratch,
        compiler_params=_cp("arbitrary", "arbitrary"))(a, b, *([carry] if carry is not None else []))
    return res if carry is not None else res[0]


def _tn_matmul(a, b, name, carry=None):
    t, m = a.shape
    n = b.shape[1]
    tk, tm, tn = _tile(t, 832), min(m, 2048), min(n, 2048)
    nk = t // tk
    grid = (m // tm, n // tn, nk)
    cin, cout, cshape, cscratch, start, finish = _carried(grid, carry)

    def body(a_ref, b_ref, *refs):
        o_ref = refs[len(cin)]
        comm = (refs[0], refs[2], refs[3], refs[4]) if carry is not None else None
        start(comm)

        @pl.when(pl.program_id(2) == 0)
        def _():
            o_ref[...] = jnp.zeros_like(o_ref)
        o_ref[...] += _tn(a_ref[...], b_ref[...])
        finish(comm)

    res = _pallas(
        body, name=name, grid=grid,
        in_specs=[pl.BlockSpec((tk, tm), lambda i, j, k: (k, i)), pl.BlockSpec((tk, tn), lambda i, j, k: (k, j))] + cin,
        out_specs=[pl.BlockSpec((tm, tn), lambda i, j, k: (i, j))] + cout,
        out_shape=[jax.ShapeDtypeStruct((m, n), F32)] + cshape, scratch_shapes=cscratch,
        compiler_params=_cp("arbitrary", "arbitrary", "arbitrary"))(a, b, *([carry] if carry is not None else []))
    return res if carry is not None else res[0]


def _attn_bwd(qr, kr, proj, dmix, sinks, ga):
    t = qr.shape[0]
    nc = t // CHUNK
    cps = _attn_chunks_per_step(nc)
    nsteps = nc // cps
    rows_step = cps * CHUNK

    def body(q_ref, k_ref, v_ref, g_ref, da_ref, sink_ref, ga_ref, dq_ref, dg_ref, dk_ref, dv_ref, gs_ref,
             got_ref, send_sems, recv_sems):
        step = pl.program_id(0)

        @pl.when(step == 0)
        def _():
            for cp in _exchange_copies(ga_ref, got_ref, send_sems, recv_sems):
                cp.start()
            dk_ref[...] = jnp.zeros_like(dk_ref)
            dv_ref[...] = jnp.zeros_like(dv_ref)
            gs_ref[...] = jnp.zeros_like(gs_ref)

        lane = lax.broadcasted_iota(jnp.int32, (1, LANES), 1)
        rows = lax.broadcasted_iota(jnp.int32, (REP * CHUNK, 1), 0) >> 6
        gs = jnp.zeros((1, LANES), F32)
        dk_parts = [[] for _ in range(cps + BAND_CHUNKS - 1)]
        dv_parts = [[] for _ in range(cps + BAND_CHUNKS - 1)]
        for j in range(cps):
            c = step * cps + j
            rj = slice(CHUNK * j, CHUNK * (j + 1))
            ks, vs = _band(k_ref, c), _band(v_ref, c)
            q = q_ref[rj, :]
            gate = g_ref[rj, :]
            sg = _sigmoid(gate)
            da = da_ref[rj, :]
            datt = da * (gate * sg)
            dqs, atts, dks, dvs = [], [], [], []
            for h in range(KV_HEADS):
                qh, kb, vb, sink_col, valid = _attn_operands(c, q, ks, vs, sink_ref, h)
                p, psink, inv = _attn_probs(qh, kb, sink_col, valid)
                pb = _bf(p)
                o = _mm(pb, vb) * inv
                do = jnp.concatenate([datt[:, HEAD_DIM * (REP * h + r):HEAD_DIM * (REP * h + r + 1)]
                                      for r in range(REP)], axis=0)
                dob = _bf(do * inv)
                delta = jnp.sum(do * o, axis=1, keepdims=True) * inv
                ds = _bf(p * (_nt(dob, vb) - delta))
                gsink = -psink * delta
                for r in range(REP):
                    gs = gs + jnp.where(lane == REP * h + r, jnp.sum(jnp.where(rows == r, gsink, 0.0)), 0.0)
                dqh = _mm(ds, kb)
                dqs += [dqh[CHUNK * r:CHUNK * (r + 1)] for r in range(REP)]
                atts += [o[CHUNK * r:CHUNK * (r + 1)] for r in range(REP)]
                dks.append(_tn(ds, qh))
                dvs.append(_tn(pb, dob))
            dq_ref[rj, :] = jnp.concatenate(dqs, axis=1)
            att = jnp.concatenate(atts, axis=1)
            dg_ref[rj, :] = _bf(da * att * (sg * (1.0 + gate * (1.0 - sg))))
            dkf = jnp.concatenate(dks, axis=1)
            dvf = jnp.concatenate(dvs, axis=1)
            for b in range(BAND_CHUNKS):
                dk_parts[j + b].append(dkf[CHUNK * b:CHUNK * (b + 1)])
                dv_parts[j + b].append(dvf[CHUNK * b:CHUNK * (b + 1)])
        gs_ref[0:1, :] += gs
        for rel in range(cps + BAND_CHUNKS - 1):
            r0 = pl.multiple_of(jnp.maximum(step * cps - (BAND_CHUNKS - 1) + rel, 0) * CHUNK, CHUNK)
            dk_ref[pl.ds(r0, CHUNK), :] += sum(dk_parts[rel][1:], dk_parts[rel][0])
            dv_ref[pl.ds(r0, CHUNK), :] += sum(dv_parts[rel][1:], dv_parts[rel][0])

        @pl.when(step == nsteps - 1)
        def _():
            for cp in _exchange_copies(ga_ref, got_ref, send_sems, recv_sems):
                cp.wait()

    return _pallas(
        body, name="attn_bwd", grid=(nsteps,),
        in_specs=[pl.BlockSpec((rows_step, D_ATT), lambda i: (i, 0)), pl.BlockSpec((t, D_KV), lambda i: (0, 0)),
                  pl.BlockSpec((t, D_KV), lambda i: (0, OV // D_KV)),
                  pl.BlockSpec((rows_step, D_ATT), lambda i: (i, OG // D_ATT)),
                  pl.BlockSpec((rows_step, D_ATT), lambda i: (i, D_SSD // D_ATT)),
                  pl.BlockSpec(memory_space=pltpu.SMEM), ANY],
        out_specs=[pl.BlockSpec((rows_step, D_ATT), lambda i: (i, 0)), pl.BlockSpec((rows_step, D_ATT), lambda i: (i, 0)),
                   pl.BlockSpec((t, D_KV), lambda i: (0, 0)), pl.BlockSpec((t, D_KV), lambda i: (0, 0)),
                   pl.BlockSpec((8, LANES), lambda i: (0, 0)), ANY],
        out_shape=[jax.ShapeDtypeStruct((t, D_ATT), F32), jax.ShapeDtypeStruct((t, D_ATT), BF16),
                   jax.ShapeDtypeStruct((t, D_KV), F32), jax.ShapeDtypeStruct((t, D_KV), F32),
                   jax.ShapeDtypeStruct((8, LANES), F32), _exchange_shape(ga)],
        scratch_shapes=_exchange_scratch(),
        compiler_params=_cp("arbitrary"))(qr, kr, proj, proj, dmix, sinks, ga)


def _ssd_bwd(dmix, y_ssd, xbc, proj, dt, acs, acst, states, d_skip_l, ssd_norm_w):
    t = xbc.shape[0]
    q = CHUNK
    nc = t // q
    gps = SSD_BWD_GROUPS_PER_STEP
    gw, sw = gps * GROUP_W, gps * D_STATE

    def body(dmix_ref, y_ref, z_ref, nw_ref, xs_ref, b_ref, c_ref, dt_ref, acs_ref, acst_ref, st_ref, dsk_ref,
             dz_ref, dxs_ref, db_ref, dc_ref, dacs_ref, ddt_ref, gnw_ref, gdsk_ref, dstate):
        @pl.when(pl.program_id(1) == 0)
        def _():
            dstate[...] = jnp.zeros_like(dstate)
            gnw_ref[...] = jnp.zeros_like(gnw_ref)
            gdsk_ref[...] = jnp.zeros_like(gdsk_ref)

        last_row = lax.broadcasted_iota(jnp.int32, (q, 1), 0) == q - 1
        lane = lax.broadcasted_iota(jnp.int32, (q, LANES), 1)
        lane1 = lax.broadcasted_iota(jnp.int32, (8, LANES), 1)
        lower, upper = _head_tri(q, True), _head_tri(q, False)
        bd_mask = _block_diag_mask()
        for gi in range(gps):
            g = gps * pl.program_id(0) + gi
            cols = slice(GROUP_W * gi, GROUP_W * (gi + 1))
            scols = slice(D_STATE * gi, D_STATE * (gi + 1))
            y = y_ref[:, cols]
            z = z_ref[:, cols]
            sz = _sigmoid(z)
            silu_z = z * sz
            yg = y * silu_z
            rstd = lax.rsqrt(jnp.mean(yg * yg, axis=-1, keepdims=True) + EPS)
            n = yg * rstd
            dout = dmix_ref[:, cols]
            gnw_ref[:, cols] += jnp.sum(dout * n, axis=0, keepdims=True)
            dn = dout * nw_ref[:, cols]
            dyg = rstd * (dn - n * jnp.mean(dn * n, axis=-1, keepdims=True))
            dy = dyg * silu_z
            dz_ref[:, cols] = _bf(dyg * y * (sz * (1.0 + z * (1.0 - sz))))

            x = xs_ref[:, cols]
            bmb, cmb = _bf(b_ref[:, scols]), _bf(c_ref[:, scols])
            hv = _group_heads(g, gi, dt_ref[...], acs_ref[...], acst_ref, dsk_ref[...])
            dec = jnp.exp(jnp.where(lower, hv.acs - hv.acs_row, NEG))
            dect = jnp.exp(jnp.where(upper, hv.acs_row - hv.acs, NEG))
            b4 = jnp.concatenate([bmb] * HPG, axis=0)
            c4 = jnp.concatenate([cmb] * HPG, axis=0)
            m_all = _nt(cmb, b4) * dec
            mt_all = _nt(bmb, c4) * dect
            xdt = x * hv.dt
            xdt_b, dyb = _bf(xdt), _bf(dy)
            x_bd, dy_bd = _block_diag(xdt_b, bd_mask), _block_diag(dyb, bd_mask)
            s_prev = st_ref[0, gi]
            spb = _bf(s_prev)
            ds_new = dstate[gi]
            dsb = _bf(ds_new)
            e = jnp.exp(hv.acs)
            elast = jnp.exp(hv.acs_last)
            dte = jnp.exp(hv.acs_last - hv.acs)
            bds = _mm(bmb, dsb)
            dxdt = _mm(_bf(mt_all), dy_bd) + bds * dte
            dm = _nt(dyb, x_bd)
            dmt = _nt(xdt_b, dy_bd)
            dye = _bf(dy * e)
            dc_ref[:, scols] = _mm(_bf(dm * dec), b4) + _nt(dye, spb)
            db_ref[:, scols] = _mm(_bf(dmt * dect), c4) + _nt(_bf(xdt * dte), dsb)
            dstate[gi] = elast * ds_new + _tn(cmb, dye)
            dxs_ref[:, cols] = dxdt * hv.dt + hv.dsk * dy
            ddte_dte = bds * xdt * dte
            dacs_l = dm * m_all - dmt * mt_all + dy * _mm(cmb, spb) * e - ddte_dte
            dlast_l = (jnp.sum(ddte_dte, axis=0, keepdims=True)
                       + jnp.sum(s_prev * ds_new, axis=0, keepdims=True) * elast)
            ddt_l = dxdt * x
            gdsk_l = jnp.sum(dy * x, axis=0, keepdims=True)
            dacs_out = jnp.zeros((q, LANES), F32)
            ddt_out = jnp.zeros((q, LANES), F32)
            gdsk = jnp.zeros((8, LANES), F32)
            for r in range(HPG):
                dacs = _head_sums(dacs_l, r) + jnp.where(last_row, _head_sums(dlast_l, r), 0.0)
                dacs_out = jnp.where(lane == r, dacs, dacs_out)
                ddt_out = jnp.where(lane == r, _head_sums(ddt_l, r), ddt_out)
                gdsk = gdsk + jnp.where(lane1 == r, _head_sums(gdsk_l, r), 0.0)
            dacs_ref[:, LANES * gi:LANES * (gi + 1)] = dacs_out
            ddt_ref[:, LANES * gi:LANES * (gi + 1)] = ddt_out
            gdsk_ref[gi] += gdsk

    rev = lambda c: nc - 1 - c
    wide = pl.BlockSpec((q, gw), lambda g, c: (rev(c), g))
    return _pallas(
        body, name="ssd_bwd", grid=(GROUPS // gps, nc),
        in_specs=[wide, wide, wide, pl.BlockSpec((1, gw), lambda g, c: (0, g)), wide,
                  pl.BlockSpec((q, sw), lambda g, c: (rev(c), D_SSD // sw + g)),
                  pl.BlockSpec((q, sw), lambda g, c: (rev(c), (D_SSD + GROUPS * D_STATE) // sw + g)),
                  pl.BlockSpec((q, LANES), lambda g, c: (rev(c), 0)), pl.BlockSpec((q, LANES), lambda g, c: (rev(c), 0)),
                  pl.BlockSpec((1, gps * GROUPS, q), lambda g, c: (rev(c), g, 0)),
                  pl.BlockSpec((1, gps, D_STATE, GROUP_W), lambda g, c: (rev(c), g, 0, 0)),
                  pl.BlockSpec((1, LANES), lambda g, c: (0, 0))],
        out_specs=[wide, wide,
                   pl.BlockSpec((q, sw), lambda g, c: (rev(c), g)), pl.BlockSpec((q, sw), lambda g, c: (rev(c), g)),
                   pl.BlockSpec((q, gps * LANES), lambda g, c: (rev(c), g)),
                   pl.BlockSpec((q, gps * LANES), lambda g, c: (rev(c), g)),
                   pl.BlockSpec((1, gw), lambda g, c: (0, g)), pl.BlockSpec((gps, 8, LANES), lambda g, c: (g, 0, 0))],
        out_shape=[jax.ShapeDtypeStruct((t, D_SSD), BF16), jax.ShapeDtypeStruct((t, D_SSD), F32),
                   jax.ShapeDtypeStruct((t, GROUPS * D_STATE), F32), jax.ShapeDtypeStruct((t, GROUPS * D_STATE), F32),
                   jax.ShapeDtypeStruct((t, GROUPS * LANES), F32), jax.ShapeDtypeStruct((t, GROUPS * LANES), F32),
                   jax.ShapeDtypeStruct((1, D_SSD), F32), jax.ShapeDtypeStruct((GROUPS, 8, LANES), F32)],
        scratch_shapes=[pltpu.VMEM((gps, D_STATE, GROUP_W), F32)],
        compiler_params=_cp("parallel", "arbitrary"))(dmix, y_ssd, proj, ssd_norm_w, xbc, xbc, xbc, dt, acs, acst,
                                                      states, d_skip_l)


def _dt_bwd(dacs_g, ddt_g, dt, proj, dt_bias_l, a_log_l):
    t = dt.shape[0]
    q = CHUNK
    nc = t // q
    cps = _chunks_per_step(nc)
    rows = cps * q

    def body(dacs_ref, ddt_ref, dt_ref, raw_ref, bias_ref, alog_ref, draw_ref, ga_ref, gb_ref):
        @pl.when(pl.program_id(0) == 0)
        def _():
            ga_ref[...] = jnp.zeros_like(ga_ref)
            gb_ref[...] = jnp.zeros_like(gb_ref)

        lane = lax.broadcasted_iota(jnp.int32, (q, LANES), 1)
        ri = lax.broadcasted_iota(jnp.int32, (q, q), 0)
        ci = lax.broadcasted_iota(jnp.int32, (q, q), 1)
        triu = (ri <= ci).astype(F32)
        a = -jnp.exp(alog_ref[...])
        used = (lane & (GROUPS - 1)) < HPG
        ga = jnp.zeros((1, LANES), F32)
        gb = jnp.zeros((1, LANES), F32)
        for k in range(cps):
            rk = slice(q * k, q * (k + 1))
            dacs = jnp.zeros((q, LANES), F32)
            ddt = jnp.zeros((q, LANES), F32)
            for g in range(GROUPS):
                mask = (lane >= GROUPS * g) & (lane < GROUPS * g + HPG)
                sl = slice(LANES * g, LANES * (g + 1))
                if g == 0:
                    dacs = jnp.where(mask, dacs_ref[rk, sl], dacs)
                    ddt = jnp.where(mask, ddt_ref[rk, sl], ddt)
                else:
                    dacs = jnp.where(mask, pltpu.roll(dacs_ref[rk, sl], GROUPS * g, 1), dacs)
                    ddt = jnp.where(mask, pltpu.roll(ddt_ref[rk, sl], GROUPS * g, 1), ddt)
            dda = jnp.dot(triu, dacs, preferred_element_type=F32, precision=HI)
            row = pl.program_id(0) * rows + q * k + lax.broadcasted_iota(jnp.int32, (q, LANES), 0)
            dsp = jnp.where((row >= PAD_LEAD) & used, dda * a + ddt, 0.0)
            draw = dsp * _sigmoid(raw_ref[rk, :] + bias_ref[...])
            draw_ref[rk, :] = _bf(draw)
            gb = gb + jnp.sum(draw, axis=0, keepdims=True)
            ga = ga + jnp.sum(jnp.where(used, dda * dt_ref[rk, :], 0.0), axis=0, keepdims=True)
        gb_ref[0:1, :] += gb
        ga_ref[0:1, :] += ga * a

    return _pallas(
        body, name="dt_bwd", grid=(nc // cps,),
        in_specs=[pl.BlockSpec((rows, GROUPS * LANES), lambda c: (c, 0)),
                  pl.BlockSpec((rows, GROUPS * LANES), lambda c: (c, 0)),
                  pl.BlockSpec((rows, LANES), lambda c: (c, 0)), pl.BlockSpec((rows, LANES), lambda c: (c, ODT // LANES)),
                  pl.BlockSpec((1, LANES), lambda c: (0, 0)), pl.BlockSpec((1, LANES), lambda c: (0, 0))],
        out_specs=[pl.BlockSpec((rows, LANES), lambda c: (c, 0)), pl.BlockSpec((8, LANES), lambda c: (0, 0)),
                   pl.BlockSpec((8, LANES), lambda c: (0, 0))],
        out_shape=[jax.ShapeDtypeStruct((t, LANES), BF16), jax.ShapeDtypeStruct((8, LANES), F32),
                   jax.ShapeDtypeStruct((8, LANES), F32)],
        compiler_params=_cp("arbitrary"))(dacs_g, ddt_g, dt, proj, dt_bias_l, a_log_l)


def _conv_bwd(dsegs, proj, conv_w, conv_b, col_off, name):
    t, seg_w = dsegs[0].shape
    nseg = len(dsegs)
    width = nseg * seg_w
    tc = 128
    per = seg_w // tc
    rt = _tile(t, 320)
    off_p = (OXS + col_off) // tc
    off_w = col_off // tc

    def body(*refs):
        d_refs = refs[:nseg]
        x_ref, w_ref, b_ref, dx_ref, gw_ref, gb_ref, xp, dup = refs[nseg:]
        assert per & (per - 1) == 0
        seg = pl.program_id(0) >> (per.bit_length() - 1)

        def d_rows(r0):
            d = d_refs[nseg - 1][pl.ds(r0, rt), :]
            for s in range(nseg - 2, -1, -1):
                d = jnp.where(seg == s, d_refs[s][pl.ds(r0, rt), :], d)
            return d

        xp[0:8, :] = jnp.zeros((8, tc), F32)
        xp[8:t + 8, :] = x_ref[...]
        dup[t:t + 8, :] = jnp.zeros((8, tc), F32)
        w = w_ref[...]
        bias = b_ref[...]

        def first(i, acc):
            r0 = pl.multiple_of(i * rt, 8)
            xs = [xp[pl.ds(r0 + 5 + k, rt), :] for k in range(CONV_WIDTH)]
            u = bias + w[3:4, :] * xs[3] + w[2:3, :] * xs[2] + w[1:2, :] * xs[1] + w[0:1, :] * xs[0]
            su = 0.5 + 0.5 * jnp.tanh(0.5 * u)
            du = d_rows(r0) * (su * (1.0 + u * (1.0 - su)))
            dup[pl.ds(r0, rt), :] = du
            return tuple(acc[k] + jnp.sum(du * xs[k], axis=0, keepdims=True) for k in range(CONV_WIDTH)) + (
                acc[CONV_WIDTH] + jnp.sum(du, axis=0, keepdims=True),)

        zero = jnp.zeros((1, tc), F32)
        acc = lax.fori_loop(0, t // rt, first, (zero,) * (CONV_WIDTH + 1))
        gw_ref[...] = jnp.concatenate(acc[:CONV_WIDTH], axis=0)
        gb_ref[...] = acc[CONV_WIDTH]

        def second(i, carry):
            r0 = pl.multiple_of(i * rt, 16)
            dx_ref[pl.ds(r0, rt), :] = _bf(w[3:4, :] * dup[pl.ds(r0, rt), :] + w[2:3, :] * dup[pl.ds(r0 + 1, rt), :]
                                          + w[1:2, :] * dup[pl.ds(r0 + 2, rt), :] + w[0:1, :] * dup[pl.ds(r0 + 3, rt), :])
            return carry

        lax.fori_loop(0, t // rt, second, 0)

    d_specs = [pl.BlockSpec((t, tc), functools.partial(lambda j, s: (0, jnp.clip(j - s * per, 0, per - 1)), s=s))
               for s in range(nseg)]
    return _pallas(
        body, name=name, grid=(width // tc,),
        in_specs=d_specs + [pl.BlockSpec((t, tc), lambda j: (0, j + off_p)),
                            pl.BlockSpec((CONV_WIDTH, tc), lambda j: (0, j + off_w)),
                            pl.BlockSpec((1, tc), lambda j: (0, j + off_w))],
        out_specs=[pl.BlockSpec((t, tc), lambda j: (0, j)), pl.BlockSpec((CONV_WIDTH, tc), lambda j: (0, j)),
                   pl.BlockSpec((1, tc), lambda j: (0, j))],
        out_shape=[jax.ShapeDtypeStruct((t, width), BF16), jax.ShapeDtypeStruct((CONV_WIDTH, width), F32),
                   jax.ShapeDtypeStruct((1, width), F32)],
        scratch_shapes=[pltpu.VMEM((t + 8, tc), F32), pltpu.VMEM((t + 8, tc), F32)],
        compiler_params=_cp("parallel"))(*dsegs, proj, conv_w, conv_b)


def _dinproj(segs, w_re, hpad, norm_w, dy_t, ga):
    t = segs[0].shape[0]
    d = hpad.shape[1]
    tm, tk = _tile(t, 832), SEG_TILE // 2
    rs = tm // 8
    assert rs % 8 == 0
    counts = [s.shape[1] // tk for s in segs]
    firsts = [sum(counts[:s]) for s in range(len(segs))]
    nk = sum(counts)
    assert nk * tk == w_re.shape[1]
    ni = t // tm
    ns = len(segs)

    def body(*refs):
        seg_refs = refs[:ns]
        (w_ref, h_hbm, nw_ref, dy_hbm, ga_ref, dh_ref, gnw_ref, got_ref, send_sems, recv_sems, h_buf, dy_buf,
         row_sems) = refs[ns:]
        i, k = pl.program_id(0), pl.program_id(1)

        def row_copies():
            rows = pl.ds(pl.multiple_of(i * tm, 16), tm)
            return [pltpu.make_async_copy(h_hbm.at[rows, :], h_buf, row_sems.at[0]),
                    pltpu.make_async_copy(dy_hbm.at[rows, :], dy_buf, row_sems.at[1])]

        @pl.when((i == 0) & (k == 0))
        def _():
            for cp in _exchange_copies(ga_ref, got_ref, send_sems, recv_sems):
                cp.start()
            gnw_ref[...] = jnp.zeros_like(gnw_ref)

        @pl.when(k == 0)
        def _():
            for cp in row_copies():
                cp.start()
            dh_ref[...] = jnp.zeros_like(dh_ref)

        for s in range(ns):
            @pl.when((k >= firsts[s]) & (k < firsts[s] + counts[s]))
            def _(s=s):
                dh_ref[...] += _nt(seg_refs[s][...], w_ref[...])

        @pl.when(k == nk - 1)
        def _():
            for cp in row_copies():
                cp.wait()
            nw = nw_ref[...]

            def rows_of(r, gnw):
                rows = pl.ds(pl.multiple_of(r * rs, 8), rs)
                h = h_buf[rows, :]
                rstd = lax.rsqrt(jnp.mean(h * h, axis=-1, keepdims=True) + EPS)
                nrm = h * rstd
                dhn = dh_ref[rows, :]
                dn = dhn * nw
                dh_ref[rows, :] = rstd * (dn - nrm * jnp.mean(dn * nrm, axis=-1, keepdims=True)) + dy_buf[rows, :]
                return gnw + jnp.sum(dhn * nrm, axis=0, keepdims=True)

            gnw_ref[...] += lax.fori_loop(0, tm // rs, rows_of, jnp.zeros((1, d), F32))

        @pl.when((i == ni - 1) & (k == nk - 1))
        def _():
            for cp in _exchange_copies(ga_ref, got_ref, send_sems, recv_sems):
                cp.wait()

    seg_specs = [pl.BlockSpec((tm, tk), functools.partial(lambda i, k, f0, n0: (i, jnp.clip(k - f0, 0, n0 - 1)),
                                                          f0=firsts[s], n0=counts[s])) for s in range(ns)]
    return _pallas(
        body, name="dinproj", grid=(ni, nk),
        in_specs=seg_specs + [pl.BlockSpec((d, tk), lambda i, k: (0, k)), ANY, pl.BlockSpec((1, d), lambda i, k: (0, 0)),
                              ANY, ANY],
        out_specs=[pl.BlockSpec((tm, d), lambda i, k: (i, 0)), pl.BlockSpec((1, d), lambda i, k: (0, 0)), ANY],
        out_shape=[jax.ShapeDtypeStruct((t, d), F32), jax.ShapeDtypeStruct((1, d), F32), _exchange_shape(ga)],
        scratch_shapes=_exchange_scratch() + [pltpu.VMEM((tm, d), F32), pltpu.VMEM((tm, d), F32),
                                              pltpu.SemaphoreType.DMA((2,))],
        compiler_params=_cp("arbitrary", "arbitrary"))(*segs, w_re, hpad, norm_w, dy_t, ga)


def _spread_heads(v):
    v = jnp.pad(v.reshape(GROUPS, HPG), ((0, 0), (0, GROUPS - HPG))).reshape(1, GROUPS * GROUPS)
    return jnp.pad(v, ((0, 0), (0, LANES - GROUPS * GROUPS)))


def _gather_heads(v):
    return v[0:1, :GROUPS * GROUPS].reshape(GROUPS, GROUPS)[:, :HPG].reshape(1, SSD_HEADS)


def _rope_tables(t):
    half = HEAD_DIM // 2
    inv = ROPE_THETA ** (-jnp.arange(half, dtype=F32) / half)
    pos = (jnp.arange(t) - PAD_LEAD).astype(F32)
    ang = pos[:, None] * inv[None, :]
    cos, sin = jnp.cos(ang), jnp.sin(ang)
    cos_t = jnp.concatenate([cos, cos, cos, cos], axis=1)
    sin_t = jnp.concatenate([-sin, sin, -sin, sin], axis=1)
    return cos_t, sin_t


def _column_pieces():
    runs = [(0, OB + 2 * GROUPS * D_STATE, 0)]
    o = OB + 2 * GROUPS * D_STATE
    runs += [(o + HPG * g, HPG, ODT + GROUPS * g) for g in range(GROUPS)]
    o += SSD_HEADS
    for width, dst in ((D_ATT, OQ), (D_KV, OK), (D_KV, OV), (D_ATT, OG)):
        runs.append((o, width, dst))
        o += width
    assert o == D_IN
    pieces = []
    for o0, width, dst in runs:
        for j in range(N_SHARD):
            lo, hi = max(o0, W_IN_SHARD * j), min(o0 + width, W_IN_SHARD * (j + 1))
            if lo < hi:
                pieces.append((j, lo - W_IN_SHARD * j, hi - W_IN_SHARD * j, dst + lo - o0))
    return pieces


def _shards_to_re(w_all):
    _, k, _ = w_all.shape
    tr = 256

    def body(x_ref, o_ref):
        o_ref[:, ODT:ODT + DT_SLAB] = jnp.zeros((tr, DT_SLAB), o_ref.dtype)
        for j, c0, c1, d0 in _column_pieces():
            o_ref[:, d0:d0 + c1 - c0] = x_ref[j, :, c0:c1]

    return _pallas(body, name="shards_to_re", grid=(k // tr,),
                   in_specs=[pl.BlockSpec((N_SHARD, tr, W_IN_SHARD), lambda i: (0, i, 0))],
                   out_specs=pl.BlockSpec((tr, N_RE), lambda i: (i, 0)),
                   out_shape=jax.ShapeDtypeStruct((k, N_RE), w_all.dtype), compiler_params=_cp("parallel"))(w_all)


def _pair_add_to_shards(parts, got, pieces, shard_rows, core, name):
    n = parts[0].shape[1]
    hn = n // 2
    tc = 128
    nt = hn // tc
    ns = len(parts)
    starts = [sum(p.shape[0] for p in parts[:s]) for s in range(ns)]
    moves = []
    for j, c0, c1, d0 in pieces:
        for s, p in enumerate(parts):
            lo, hi = max(d0, starts[s]), min(d0 + c1 - c0, starts[s] + p.shape[0])
            if lo < hi:
                moves.append((s, lo - starts[s], j, c0 + lo - d0, hi - lo))
    assert sum(m[4] for m in moves) == N_SHARD * shard_rows

    def body(core_ref, *refs):
        own, theirs, o_ref, acc = refs[:ns], refs[ns:2 * ns], refs[2 * ns], refs[2 * ns + 1]
        for s, r0, j, c0, rows in moves:
            acc[j, c0:c0 + rows, :] = own[s][r0:r0 + rows, :] + theirs[s][r0:r0 + rows, :]
        o_ref[...] = _bf(acc[...])

    return _pallas(
        body, name=name,
        grid_spec=pltpu.PrefetchScalarGridSpec(
            num_scalar_prefetch=1, grid=(nt,),
            in_specs=[pl.BlockSpec((p.shape[0], tc), lambda i, core_ref: (0, core_ref[0] * nt + i)) for p in parts]
            + [pl.BlockSpec((p.shape[0], tc), lambda i, core_ref: (0, i)) for p in parts],
            out_specs=pl.BlockSpec((N_SHARD, shard_rows, tc), lambda i, core_ref: (0, 0, i)),
            scratch_shapes=[pltpu.VMEM((N_SHARD, shard_rows, tc), F32)]),
        out_shape=jax.ShapeDtypeStruct((N_SHARD, shard_rows, hn), BF16),
        compiler_params=_cp("parallel"))(core, *parts, *got)


def _local_step(x, target, meta, norm_pre_w, w_re, conv_w, conv_b, dt_bias, a_log, d_skip, ssd_norm_w, sinks,
                w_out_shard, norm_post_w, place):
    seq = x.shape[0]
    t = PAD_LEAD + N_META + seq
    hpad = jnp.concatenate([jnp.zeros((PAD_LEAD, D_MODEL), F32), meta, x], axis=0)
    dt_bias_l, a_log_l, d_skip_l = _spread_heads(dt_bias), _spread_heads(a_log), _spread_heads(d_skip)
    cos_t, sin_t = _rope_tables(t)
    sink_v = sinks.reshape(Q_HEADS)

    proj, hn, w_out_all = _inproj(hpad, norm_pre_w, w_re, w_out_shard)
    w_out = w_out_all.reshape(D_MIX, D_MODEL)
    xbc = _conv_fwd(proj, conv_w, conv_b)
    dt, acs, acst = _dt_prep(proj, dt_bias_l, a_log_l)
    y_ssd, ymix, states = _ssd_fwd(xbc, proj, dt, acs, acst, d_skip_l, ssd_norm_w)
    qr, kr = _rope(proj, OQ, proj, OK, cos_t, sin_t)
    amix = _attn_fwd(qr, kr, proj, sink_v)
    out = _outproj(ymix, amix, w_out)
    dout, dy_t, loss_blk, g_norm_post = _post_loss(out, x, target, norm_post_w)

    g_out_y = _tn_matmul(ymix, dout, "gw_out_y")
    g_out_a, got_y = _tn_matmul(amix, dout, "gw_out_a", carry=g_out_y)
    dmix, got_a = _nt_matmul(dout, w_out, "dmix", carry=g_out_a)
    ga_out = _reduce_pair([g_out_y, g_out_a], [got_y, got_a], [(j, 0, W_OUT_SHARD, W_OUT_SHARD * j) for j in range(N_SHARD)],
                          W_OUT_SHARD, place, "gw_out")
    dq_r, dg, dk_r, dv, gs, slabs_out = _attn_bwd(qr, kr, proj, dmix, sink_v, ga_out)
    g_w_out = _reduce_finish(ga_out, slabs_out, place, "gw_out")
    dq, dk = _rope(dq_r, 0, dk_r, 0, cos_t, -sin_t)
    dz, dxs, db, dc, dacs_g, ddt_g, g_ssd_norm, gdsk = _ssd_bwd(dmix, y_ssd, xbc, proj, dt, acs, acst, states,
                                                                d_skip_l, ssd_norm_w)
    draw, ga, gb = _dt_bwd(dacs_g, ddt_g, dt, proj, dt_bias_l, a_log_l)
    dxs_p, gcw0, gcb0 = _conv_bwd([dxs], proj, conv_w, conv_b, 0, "conv_bwd_x")
    dbc_p, gcw1, gcb1 = _conv_bwd([db, dc], proj, conv_w, conv_b, D_SSD, "conv_bwd_bc")
    tail = jnp.concatenate([dk, _bf(dv), draw, jnp.zeros((t, DT_SLAB - LANES), BF16)], axis=1)
    segs = [dz, dxs_p, dbc_p, dq, dg, tail]
    g_parts, got_parts = [_tn_matmul(segs[0], hn, "gw_in_0")], []
    for s in range(1, len(segs)):
        part, got = _tn_matmul(segs[s], hn, "gw_in_%d" % s, carry=g_parts[-1])
        g_parts.append(part)
        got_parts.append(got)
    ga_in = _reduce_pair(g_parts, got_parts, _column_pieces(), W_IN_SHARD, place, "gw_in")
    dh, g_norm_pre, slabs_in = _dinproj(segs, w_re, hpad, norm_pre_w, dy_t, ga_in)
    g_w_in_half = _chip_sum(ga_in, slabs_in, place, "gw_in_chip_sum")

    gdsk_l = jnp.concatenate([gdsk[g, 0:1, 0:GROUPS] for g in range(GROUPS)], axis=1)
    gdsk_l = jnp.pad(gdsk_l, ((0, 0), (0, LANES - GROUPS * GROUPS)))
    grads = dict(
        meta_tokens=dh[PAD_LEAD:ROW0], norm_pre_w=g_norm_pre, w_in_half=g_w_in_half,
        conv_w=jnp.concatenate([gcw0, gcw1], axis=1), conv_b=jnp.concatenate([gcb0, gcb1], axis=1),
        dt_bias=_gather_heads(gb), a_log=_gather_heads(ga), d_skip=_gather_heads(gdsk_l), ssd_norm_w=g_ssd_norm,
        attn_sinks=gs[0:1, :Q_HEADS], w_out=g_w_out, norm_post_w=g_norm_post)
    return loss_blk[0, 0], dh[ROW0:], grads


ANY = pl.BlockSpec(memory_space=pl.ANY)
MESH = pl.DeviceIdType.MESH
GATHER_CHUNKS = 4
PAIR_CHUNKS = 8
JOIN_CHUNKS = 8


def _rcopy(src, dst, ssem, rsem, dev):
    return pltpu.make_async_remote_copy(src_ref=src, dst_ref=dst, send_sem=ssem, recv_sem=rsem, device_id=dev,
                                        device_id_type=MESH)


def _place():
    x, y, c = lax.axis_index("x"), lax.axis_index("y"), lax.axis_index("c")
    chips = [(1 - x, y), (x, 1 - y), (1 - x, 1 - y)]
    return x, y, c, chips


def _gather_plan(x_ref, out_ref, send_sems, recv_sems, local_sems, hr, kc):
    ch = hr // kc
    assert ch * kc == hr and ch % 16 == 0
    x, y, c, chips = _place()
    me = 2 * x + y
    sibling = (x, y, 1 - c)

    def piece(chip, hc, k):
        return out_ref.at[chip, pl.ds(hc * hr + k * ch, ch), :]

    def local():
        return [pltpu.make_async_copy(x_ref.at[pl.ds(k * ch, ch), :], out_ref.at[me, pl.ds(k * ch, ch), :],
                                      local_sems.at[k]) for k in range(2 * kc)]

    def first():
        return [_rcopy(x_ref.at[pl.ds(c * hr + k * ch, ch), :], piece(me, c, k), send_sems.at[j * kc + k],
                       recv_sems.at[j * kc + k], (*chip, c)) for j, chip in enumerate(chips) for k in range(kc)]

    def passed(hc):
        return [_rcopy(piece(2 * chip[0] + chip[1], hc, k), piece(2 * chip[0] + chip[1], hc, k),
                       send_sems.at[(3 + j) * kc + k], recv_sems.at[(3 + j) * kc + k], sibling)
                for j, chip in enumerate(chips) for k in range(kc)]

    def arrivals():
        return [_rcopy(piece(2 * chip[0] + chip[1], c, k), piece(2 * chip[0] + chip[1], c, k), send_sems.at[j * kc + k],
                       recv_sems.at[j * kc + k], (*chip, c)) for j, chip in enumerate(chips) for k in range(kc)]

    def start():
        for cp in local() + first():
            cp.start()

    def forward():
        for arrived in arrivals():
            arrived.wait_recv()
        for fw in passed(c):
            fw.start()

    def finish():
        for cp in passed(1 - c):
            cp.wait_recv()
        for cp in first() + passed(c):
            cp.wait_send()
        for cp in local():
            cp.wait()

    return start, forward, finish


def _gather_shards(shard, name, kc, chip, small):
    r, n = shard.shape
    hr = r // 2
    qr = hr // 2
    ch = qr // kc
    assert ch * kc == qr and ch % 16 == 0
    nflow = 12
    tr = 256

    def body(x_ref, p_ref, out_ref, slots_ref, send_sems, recv_sems, *small_sems):
        start_small, wait_small = _chip_small_exchange(p_ref, slots_ref, *small_sems)
        start_small()
        x, y, c, _ = _place()
        me, cxn, cyn, cdg = 2 * x + y, 2 * (1 - x) + y, 2 * x + 1 - y, 2 * (1 - x) + 1 - y
        xn, yn, sibling = (1 - x, y, c), (x, 1 - y, c), (x, y, 1 - c)

        def piece(chip, hc, part, k):
            return out_ref.at[chip, pl.ds(hc * hr + part * qr + k * ch, ch), :]

        def own(part, k):
            return x_ref.at[pl.ds(c * hr + part * qr + k * ch, ch), :]

        def sems(flow, k):
            return send_sems.at[flow * kc + k], recv_sems.at[flow * kc + k]

        def arrival(flow, chip, hc, part, k):
            return _rcopy(piece(chip, hc, part, k), piece(chip, hc, part, k), *sems(flow, k), sibling)

        sends = []
        for flow, part, peer in ((0, 0, xn), (1, 1, yn), (2, 0, yn), (3, 1, xn)):
            sends += [_rcopy(own(part, k), piece(me, c, part, k), *sems(flow, k), peer) for k in range(kc)]
        for cp in sends:
            cp.start()
        landing = ((0, cxn, 0), (1, cyn, 1), (2, cyn, 0), (3, cxn, 1), (4, cdg, 0), (5, cdg, 1))
        for i, (flow, chip, part) in enumerate(landing):
            for k in range(kc):
                arrival(flow, chip, c, part, k).wait_recv()
                if flow < 2:
                    on = _rcopy(piece(chip, c, part, k), piece(chip, c, part, k), *sems(4 + flow, k),
                                yn if flow == 0 else xn)
                    on.start()
                    sends.append(on)
                fw = _rcopy(piece(chip, c, part, k), piece(chip, c, part, k), *sems(6 + i, k), sibling)
                fw.start()
                sends.append(fw)
        for i, (flow, chip, part) in enumerate(landing):
            for k in range(kc):
                arrival(6 + i, chip, 1 - c, part, k).wait_recv()
        for cp in sends:
            cp.wait_send()
        wait_small()

    full = jax.ShapeDtypeStruct((N_SHARD, r, n), shard.dtype)
    others, slots = _pallas(
        body, name=name, in_specs=[ANY, ANY], out_specs=[ANY, ANY],
        out_shape=[full, jax.ShapeDtypeStruct((N_SHARD,) + small.shape, F32)],
        scratch_shapes=[pltpu.SemaphoreType.DMA((nflow * kc,)), pltpu.SemaphoreType.DMA((nflow * kc,)),
                        pltpu.SemaphoreType.DMA((3,)), pltpu.SemaphoreType.DMA((3,)), pltpu.SemaphoreType.DMA])(
                            shard, small)

    def place(chip_ref, own_ref, all_ref, o_ref):
        o_ref[0] = own_ref[...]

    gathered = _pallas(
        place, name=name + "_own",
        grid_spec=pltpu.PrefetchScalarGridSpec(
            num_scalar_prefetch=1, grid=(r // tr,),
            in_specs=[pl.BlockSpec((tr, n), lambda i, chip_ref: (i, 0)), ANY],
            out_specs=pl.BlockSpec((1, tr, n), lambda i, chip_ref: (chip_ref[0], i, 0))),
        out_shape=full, input_output_aliases={2: 0}, compiler_params=_cp("parallel"))(chip, shard, others)
    return gathered, slots


def _pair_copies(src_ref, dst_ref, send_sems, recv_sems):
    hn = src_ref.shape[1] // 2
    cw = hn // PAIR_CHUNKS
    assert cw * PAIR_CHUNKS == hn and cw % LANES == 0
    x, y, c, _ = _place()
    return [_rcopy(src_ref.at[:, pl.ds((1 - c) * hn + k * cw, cw)], dst_ref.at[:, pl.ds(k * cw, cw)],
                   send_sems.at[k], recv_sems.at[k], (x, y, 1 - c)) for k in range(PAIR_CHUNKS)]


def _pair_send(parts, name):
    n = parts[0].shape[1]
    hn = n // 2
    kc = PAIR_CHUNKS
    cw = hn // kc
    assert cw * kc == hn and cw % LANES == 0
    ns = len(parts)

    def body(*refs):
        srcs, dsts, send_sems, recv_sems = refs[:ns], refs[ns:2 * ns], refs[2 * ns], refs[2 * ns + 1]
        x, y, c, _ = _place()
        cps = [_rcopy(srcs[s].at[:, pl.ds((1 - c) * hn + k * cw, cw)], dsts[s].at[:, pl.ds(k * cw, cw)],
                      send_sems.at[s * kc + k], recv_sems.at[s * kc + k], (x, y, 1 - c))
               for s in range(ns) for k in range(kc)]
        for cp in cps:
            cp.start()
        for cp in cps:
            cp.wait()

    return _pallas(
        body, name=name, in_specs=[ANY] * ns, out_specs=[ANY] * ns,
        out_shape=[jax.ShapeDtypeStruct((p.shape[0], hn), F32) for p in parts],
        scratch_shapes=[pltpu.SemaphoreType.DMA((ns * kc,)), pltpu.SemaphoreType.DMA((ns * kc,))])(*parts)


REDUCE_TILE = 256


def _exchange_copies(g_ref, got_ref, send_sems, recv_sems):
    hn = g_ref.shape[2]
    kc = GATHER_CHUNKS
    cw = hn // kc
    assert cw * kc == hn and cw % LANES == 0
    x, y, c, chips = _place()
    return [_rcopy(g_ref.at[2 * chip[0] + chip[1], :, pl.ds(k * cw, cw)], got_ref.at[j, :, pl.ds(k * cw, cw)],
                   send_sems.at[j * kc + k], recv_sems.at[j * kc + k], (*chip, c))
            for j, chip in enumerate(chips) for k in range(kc)]


def _exchange_scratch():
    return [pltpu.SemaphoreType.DMA((3 * GATHER_CHUNKS,)), pltpu.SemaphoreType.DMA((3 * GATHER_CHUNKS,))]


def _exchange_shape(ga):
    return jax.ShapeDtypeStruct((3,) + ga.shape[1:], ga.dtype)


def _chip_sum(ga, got, place, name):
    _, r, hn = ga.shape
    tc = REDUCE_TILE
    nt = hn // tc

    def body(place_ref, own_ref, got_ref, o_ref):
        acc = own_ref[0].astype(F32)
        for j in range(3):
            acc = acc + got_ref[j].astype(F32)
        o_ref[...] = acc

    return _pallas(
        body, name=name,
        grid_spec=pltpu.PrefetchScalarGridSpec(
            num_scalar_prefetch=1, grid=(nt,),
            in_specs=[pl.BlockSpec((1, r, tc), lambda i, place_ref: (place_ref[0], 0, i)),
                      pl.BlockSpec((3, r, tc), lambda i, place_ref: (0, 0, i))],
            out_specs=pl.BlockSpec((r, tc), lambda i, place_ref: (0, place_ref[1] * nt + i))),
        out_shape=jax.ShapeDtypeStruct((r, 2 * hn), F32), compiler_params=_cp("parallel"))(place, ga, got)


def _pair_join(buf, name, small=None):
    r, n = buf.shape
    hn = n // 2
    kc = JOIN_CHUNKS
    cw = hn // kc
    assert cw * kc == hn and cw % LANES == 0

    def body(in_ref, *refs):
        if small is None:
            out_ref, send_sems, recv_sems = refs
        else:
            p_ref, out_ref, slots_ref, send_sems, recv_sems = refs[:5]
            start_small, wait_small = _small_exchange(p_ref, slots_ref, *refs[5:])
            start_small()
        x, y, c, _ = _place()
        cps = [_rcopy(out_ref.at[:, pl.ds(c * hn + k * cw, cw)], out_ref.at[:, pl.ds(c * hn + k * cw, cw)],
                      send_sems.at[k], recv_sems.at[k], (x, y, 1 - c)) for k in range(kc)]
        for cp in cps:
            cp.start()
        for k in range(kc):
            cols = out_ref.at[:, pl.ds((1 - c) * hn + k * cw, cw)]
            _rcopy(cols, cols, send_sems.at[k], recv_sems.at[k], (x, y, 1 - c)).wait_recv()
        for cp in cps:
            cp.wait_send()
        if small is not None:
            wait_small()

    sems = [pltpu.SemaphoreType.DMA((kc,)), pltpu.SemaphoreType.DMA((kc,))]
    if small is None:
        return _pallas(body, name=name, in_specs=[ANY], out_specs=ANY, out_shape=jax.ShapeDtypeStruct((r, n), F32),
                       input_output_aliases={0: 0}, scratch_shapes=sems)(buf)
    return _pallas(
        body, name=name, in_specs=[ANY, ANY], out_specs=[ANY, ANY],
        out_shape=[jax.ShapeDtypeStruct((r, n), F32), jax.ShapeDtypeStruct((N_DEV,) + small.shape, F32)],
        input_output_aliases={0: 0}, scratch_shapes=sems + _small_scratch())(buf, small)


def _reduce_pair(parts, got, pieces, shard_rows, place, tag):
    if len(got) < len(parts):
        got = list(got) + list(_pair_send(parts[len(got):], tag + "_pair_send"))
    return _pair_add_to_shards(parts, got, pieces, shard_rows, place[1:2], tag + "_pair_add")


def _reduce_finish(ga, slabs, place, tag):
    return _pair_join(_chip_sum(ga, slabs, place, tag + "_chip_sum"), tag + "_pair_join")


N_DEV = 8


def _small_exchange(p_ref, slots_ref, send_sems, recv_sems, local_sem):
    x, y, c, _ = _place()
    my = 4 * x + 2 * y + c

    def sends():
        return [_rcopy(p_ref, slots_ref.at[my], send_sems.at[k - 1], recv_sems.at[k - 1],
                       (x ^ ((k >> 2) & 1), y ^ ((k >> 1) & 1), c ^ (k & 1))) for k in range(1, N_DEV)]

    def local():
        return pltpu.make_async_copy(p_ref, slots_ref.at[my], local_sem)

    def start():
        local().start()
        for cp in sends():
            cp.start()

    def wait():
        for k in range(1, N_DEV):
            _rcopy(p_ref, slots_ref.at[my ^ k], send_sems.at[k - 1], recv_sems.at[k - 1], (x, y, c)).wait_recv()
        for cp in sends():
            cp.wait_send()
        local().wait()

    return start, wait


def _chip_small_exchange(p_ref, slots_ref, send_sems, recv_sems, local_sem):
    x, y, c, chips = _place()
    me = 2 * x + y

    def sends():
        return [_rcopy(p_ref, slots_ref.at[me], send_sems.at[j], recv_sems.at[j], (*chip, c))
                for j, chip in enumerate(chips)]

    def local():
        return pltpu.make_async_copy(p_ref, slots_ref.at[me], local_sem)

    def start():
        local().start()
        for cp in sends():
            cp.start()

    def wait():
        for j, chip in enumerate(chips):
            slot = slots_ref.at[2 * chip[0] + chip[1]]
            _rcopy(slot, slot, send_sems.at[j], recv_sems.at[j], (*chip, c)).wait_recv()
        for cp in sends():
            cp.wait_send()
        local().wait()

    return start, wait


def _small_scratch():
    return [pltpu.SemaphoreType.DMA((N_DEV - 1,)), pltpu.SemaphoreType.DMA((N_DEV - 1,)), pltpu.SemaphoreType.DMA]


def _sum_slots(slots, name):
    _, rows, n = slots.shape

    def body(s_ref, o_ref):
        acc = s_ref[0]
        for j in range(1, N_DEV):
            acc = acc + s_ref[j]
        o_ref[...] = acc

    vm = pl.BlockSpec(memory_space=pltpu.VMEM)
    return _pallas(body, name=name, in_specs=[vm], out_specs=vm, out_shape=jax.ShapeDtypeStruct((rows, n), F32))(slots)


def _adamw(w, g, m, v, name):
    r, n = w.shape
    tr = _tile(r, 256, 8)
    c1 = 1.0 / (1.0 - ADAM_B1 ** ADAM_STEP)
    c2 = 1.0 / (1.0 - ADAM_B2 ** ADAM_STEP)

    def body(w_ref, g_ref, m_ref, v_ref, d_ref, mo_ref, vo_ref, go_ref):
        gv = g_ref[...]
        mn = ADAM_B1 * m_ref[...] + (1.0 - ADAM_B1) * gv
        vn = ADAM_B2 * v_ref[...] + (1.0 - ADAM_B2) * (gv * gv)
        d_ref[...] = -ADAM_LR * ((mn * c1) / (jnp.sqrt(vn * c2) + ADAM_EPS) + ADAM_WD * w_ref[...])
        mo_ref[...] = mn
        vo_ref[...] = vn
        go_ref[...] = gv

    spec = pl.BlockSpec((tr, n), lambda i: (i, 0))
    shp = jax.ShapeDtypeStruct((r, n), F32)
    return _pallas(body, name=name, grid=(r // tr,), in_specs=[spec] * 4, out_specs=[spec] * 4, out_shape=[shp] * 4,
                   compiler_params=_cp("parallel"))(w, g, m, v)


PACK_W = 1024
SMALL_REPL = ("norm_pre_w", "conv_b", "ssd_norm_w", "norm_post_w")
SMALL_HEAD = ("dt_bias", "a_log", "d_skip", "attn_sinks")


def _rows(a):
    return a.reshape(-1, PACK_W)


def _head_row(vals, extra=None):
    parts = [vals[n].reshape(1, -1) for n in SMALL_HEAD]
    if extra is not None:
        parts.append(extra.reshape(1, 1))
    row = jnp.concatenate(parts, axis=1)
    return jnp.pad(row, ((0, 0), (0, PACK_W - row.shape[1])))


def _pad_rows(a, rows):
    return jnp.pad(a, ((0, rows - a.shape[0]), (0, 0)))


def _pack_repl(vals, extra=None):
    body = jnp.concatenate([_rows(vals[n]) for n in SMALL_REPL] + [_head_row(vals, extra)], axis=0)
    return _pad_rows(body, 16)


def _unpack_repl(buf):
    out, r = {}, 0
    for n, k in zip(SMALL_REPL, (2, 4, 2, 2)):
        out[n] = buf[r:r + k].reshape(1, k * PACK_W)
        r += k
    col = 0
    for n, k in zip(SMALL_HEAD, (32, 32, 32, 16)):
        out[n] = buf[r:r + 1, col:col + k]
        col += k
    return out, buf[r, col]


def kernel(x, meta_tokens, norm_pre_w, w_in, conv_w, conv_b, dt_bias, a_log, d_skip, ssd_norm_w, attn_sinks, w_out, norm_post_w, loss_target, m_meta_tokens, m_norm_pre_w, m_w_in, m_conv_w, m_conv_b, m_dt_bias, m_a_log, m_d_skip, m_ssd_norm_w, m_attn_sinks, m_w_out, m_norm_post_w, v_meta_tokens, v_norm_pre_w, v_w_in, v_conv_w, v_conv_b, v_dt_bias, v_a_log, v_d_skip, v_ssd_norm_w, v_attn_sinks, v_w_out, v_norm_post_w):
    names = ("meta_tokens", "norm_pre_w", "w_in", "conv_w", "conv_b", "dt_bias", "a_log", "d_skip", "ssd_norm_w",
             "attn_sinks", "w_out", "norm_post_w")
    w = dict(zip(names, (meta_tokens, norm_pre_w, w_in, conv_w, conv_b, dt_bias, a_log, d_skip, ssd_norm_w, attn_sinks,
                         w_out, norm_post_w)))
    m = dict(zip(names, (m_meta_tokens, m_norm_pre_w, m_w_in, m_conv_w, m_conv_b, m_dt_bias, m_a_log, m_d_skip,
                         m_ssd_norm_w, m_attn_sinks, m_w_out, m_norm_post_w)))
    v = dict(zip(names, (v_meta_tokens, v_norm_pre_w, v_w_in, v_conv_w, v_conv_b, v_dt_bias, v_a_log, v_d_skip,
                         v_ssd_norm_w, v_attn_sinks, v_w_out, v_norm_post_w)))
    cx, cy, cc = lax.axis_index("x"), lax.axis_index("y"), lax.axis_index("c")
    chip = 2 * cx + cy
    meta_cols = D_MODEL // N_SHARD
    conv_cols = D_CONV // N_SHARD

    place = jnp.stack([chip, cc]).astype(jnp.int32)
    small = jnp.concatenate([_pad_rows(conv_w[0], 8), _rows(meta_tokens)], axis=0)
    w_in_all, small_all = _gather_shards(_bf(w_in[0]), "gather_w_in", GATHER_CHUNKS, place[0:1], small)
    w_re = _shards_to_re(w_in_all)
    conv_full = jnp.transpose(small_all[:, 0:CONV_WIDTH], (1, 0, 2)).reshape(CONV_WIDTH, D_CONV)
    meta_full = jnp.transpose(small_all[:, 8:16].reshape(N_SHARD, N_META, meta_cols), (1, 0, 2)).reshape(N_META, D_MODEL)

    loss_dev, grad_x, g = _local_step(x[0], loss_target[0], meta_full, norm_pre_w, w_re, conv_full, conv_b, dt_bias,
                                      a_log, d_skip, ssd_norm_w, attn_sinks, _bf(w_out[0]), norm_post_w, place)
    g_w_out = g["w_out"]

    packed = jnp.concatenate([_rows(g["conv_w"]), _rows(g["meta_tokens"]), _pack_repl(g, loss_dev)], axis=0)
    g_w_in, slots = _pair_join(g["w_in_half"], "gw_in_pair_join", small=packed)
    red = _sum_slots(slots, "reduce_small")
    g_conv_full = red[0:16].reshape(CONV_WIDTH, D_CONV)
    g_meta_full = red[16:48].reshape(N_META, D_MODEL)
    g_small, loss = _unpack_repl(red[48:64])
    grads = dict(g_small)
    grads["w_in"] = g_w_in
    grads["w_out"] = g_w_out
    grads["conv_w"] = lax.dynamic_slice(g_conv_full, (0, chip * conv_cols), (CONV_WIDTH, conv_cols))
    grads["meta_tokens"] = lax.dynamic_slice(g_meta_full, (0, chip * meta_cols), (N_META, meta_cols))

    upd = {}
    upd["w_in"] = [jnp.swapaxes(a, 0, 1) for a in _adamw(jnp.swapaxes(w_in[0], 0, 1), g_w_in, jnp.swapaxes(m_w_in[0], 0, 1),
                                                         jnp.swapaxes(v_w_in[0], 0, 1), "adamw_w_in")]
    grads["w_in"] = upd["w_in"][3]
    upd["w_out"] = _adamw(w_out[0], g_w_out, m_w_out[0], v_w_out[0], "adamw_w_out")
    grads["w_out"] = upd["w_out"][3]

    def pack_small(vals, conv, meta):
        return jnp.concatenate([_pad_rows(conv.reshape(CONV_WIDTH, conv_cols), 8), _rows(meta), _pack_repl(vals)], axis=0)

    sm = _adamw(pack_small(w, w["conv_w"], w["meta_tokens"]), pack_small(grads, grads["conv_w"], grads["meta_tokens"]),
                pack_small(m, m["conv_w"], m["meta_tokens"]), pack_small(v, v["conv_w"], v["meta_tokens"]),
                "adamw_small")
    for n in names:
        if n not in ("w_in", "w_out"):
            upd[n] = [None, None, None]
    for k, buf in enumerate(sm[:3]):
        upd["conv_w"][k] = buf[0:CONV_WIDTH]
        upd["meta_tokens"][k] = buf[8:16].reshape(N_META, meta_cols)
        rest, _ = _unpack_repl(buf[16:32])
        for n in SMALL_REPL + SMALL_HEAD:
            upd[n][k] = rest[n]

    def shaped(n, a):
        return a.reshape(w[n].shape)

    outs = [loss, grad_x[None]]
    outs += [shaped(n, grads[n]) for n in names]
    for k in range(3):
        outs += [shaped(n, upd[n][k]) for n in names]
    return tuple(outs)
```

```python
import functools

import jax
import jax.numpy as jnp
from jax import lax
from jax.experimental import pallas as pl
from jax.experimental.pallas import tpu as pltpu

F32 = jnp.float32
BF16 = jnp.bfloat16

D_MODEL = 2048
CHUNK = 64
N_META = 16
PAD_LEAD = CHUNK - N_META
ROW0 = PAD_LEAD + N_META
EPS = 1e-6
SSD_HEADS = 32
HEAD_DIM = 64
GROUPS = 8
HPG = SSD_HEADS // GROUPS
D_STATE = 128
D_SSD = 2048
GROUP_W = D_SSD // GROUPS
CONV_WIDTH = 4
D_CONV = 4096
Q_HEADS = 16
KV_HEADS = 4
REP = Q_HEADS // KV_HEADS
D_ATT = 1024
D_KV = 256
BAND_CHUNKS = 3
ROPE_THETA = 10000.0
D_MIX = D_SSD + D_ATT
D_IN = 8736
N_SHARD = 4
W_IN_SHARD = D_IN // N_SHARD
W_OUT_SHARD = D_MIX // N_SHARD

OZ, OXS, OB, OC, OQ, OG, OK, OV, ODT = 0, 2048, 4096, 5120, 6144, 7168, 8192, 8448, 8704
DT_SLAB = 512
N_RE = ODT + DT_SLAB
LANES = 128

ADAM_LR, ADAM_B1, ADAM_B2, ADAM_EPS, ADAM_WD, ADAM_STEP = 0.001, 0.9, 0.999, 1e-08, 0.01, 10

SSD_FWD_GROUPS_PER_STEP = 4
SSD_BWD_GROUPS_PER_STEP = 8
SEG_TILE = 1024
VMEM_LIMIT = 52 * 1024 * 1024
NEG = -1e30
HI = lax.Precision.HIGHEST


def _pallas(body, **kw):
    return pl.pallas_call(body, **kw)


def _cp(*sem):
    return pltpu.CompilerParams(dimension_semantics=sem, vmem_limit_bytes=VMEM_LIMIT)


def _tile(n, cap, mult=16):
    best = None
    for d in range(mult, min(n, cap) + 1, mult):
        if n % d == 0:
            best = d
    assert best is not None, (n, cap)
    return best


def _nt(a, b):
    return lax.dot_general(a, b, (((1,), (1,)), ((), ())), preferred_element_type=F32)


def _tn(a, b):
    return lax.dot_general(a, b, (((0,), (0,)), ((), ())), preferred_element_type=F32)


def _mm(a, b):
    return jnp.dot(a, b, preferred_element_type=F32)


def _sigmoid(x):
    return 1.0 / (1.0 + jnp.exp(-x))


def _bf(x):
    return x.astype(BF16)


def _inproj(hpad, norm_w, w_re, w_out_shard):
    t, d = hpad.shape
    n = w_re.shape[1]
    tm, tn = _tile(t, 1040), 1024
    ni, nj = t // tm, n // tn
    r_out, n_out = w_out_shard.shape
    kc = GATHER_CHUNKS

    def body(h_ref, nw_ref, w_ref, ws_ref, proj_ref, hn_ref, wall_ref, hn_s, send_sems, recv_sems, local_sems):
        i, j = pl.program_id(0), pl.program_id(1)
        start, forward, finish = _gather_plan(ws_ref, wall_ref, send_sems, recv_sems, local_sems, r_out // 2, kc)
        pl.when((i == 0) & (j == 0))(start)
        pl.when((i == ni // 2) & (j == 0))(forward)

        @pl.when(j == 0)
        def _():
            h = h_ref[...]
            ms = jnp.mean(h * h, axis=-1, keepdims=True)
            hn = _bf(h * lax.rsqrt(ms + EPS) * nw_ref[...])
            hn_s[...] = hn
            hn_ref[...] = hn
        proj_ref[...] = _mm(hn_s[...], w_ref[...])
        pl.when((i == ni - 1) & (j == nj - 1))(finish)

    return _pallas(
        body, name="inproj", grid=(ni, nj),
        in_specs=[pl.BlockSpec((tm, d), lambda i, j: (i, 0)), pl.BlockSpec((1, d), lambda i, j: (0, 0)),
                  pl.BlockSpec((d, tn), lambda i, j: (0, j)), ANY],
        out_specs=[pl.BlockSpec((tm, tn), lambda i, j: (i, j)), pl.BlockSpec((tm, d), lambda i, j: (i, 0)), ANY],
        out_shape=[jax.ShapeDtypeStruct((t, n), F32), jax.ShapeDtypeStruct((t, d), BF16),
                   jax.ShapeDtypeStruct((N_SHARD, r_out, n_out), w_out_shard.dtype)],
        scratch_shapes=[pltpu.VMEM((tm, d), BF16), pltpu.SemaphoreType.DMA((6 * kc,)), pltpu.SemaphoreType.DMA((6 * kc,)),
                        pltpu.SemaphoreType.DMA((2 * kc,))],
        compiler_params=_cp("arbitrary", "arbitrary"))(hpad, norm_w, w_re, w_out_shard)


def _conv_fwd(proj, conv_w, conv_b):
    t = proj.shape[0]
    tc = 256
    off = OXS // tc

    def body(x_ref, w_ref, b_ref, o_ref):
        x = x_ref[...]
        w = w_ref[...]
        row = lax.broadcasted_iota(jnp.int32, (t, tc), 0)
        u = b_ref[...] + w[3:4, :] * x
        for k in range(1, CONV_WIDTH):
            u = u + w[3 - k:4 - k, :] * jnp.where(row >= k, pltpu.roll(x, k, 0), 0.0)
        h = 0.5 * u
        o_ref[...] = h + h * jnp.tanh(h)

    return _pallas(
        body, name="conv_fwd", grid=(D_CONV // tc,),
        in_specs=[pl.BlockSpec((t, tc), lambda j: (0, j + off)), pl.BlockSpec((CONV_WIDTH, tc), lambda j: (0, j)),
                  pl.BlockSpec((1, tc), lambda j: (0, j))],
        out_specs=pl.BlockSpec((t, tc), lambda j: (0, j)),
        out_shape=jax.ShapeDtypeStruct((t, D_CONV), F32),
        compiler_params=_cp("parallel"))(proj, conv_w, conv_b)


def _softplus(u):
    e = jnp.exp(-jnp.abs(u))
    w = 1.0 + e
    l1p = jnp.where(w == 1.0, e, jnp.log(w) * (e / jnp.where(w == 1.0, 1.0, w - 1.0)))
    return jnp.maximum(u, 0.0) + l1p


def _chunks_per_step(nc):
    return max(d for d in range(1, 14) if nc % d == 0)


def _dt_prep(proj, dt_bias_l, a_log_l):
    t = proj.shape[0]
    nc = t // CHUNK
    q = CHUNK
    cps = _chunks_per_step(nc)
    rows = cps * q

    def body(raw_ref, bias_ref, alog_ref, dt_ref, acs_ref, acst_ref):
        ri = lax.broadcasted_iota(jnp.int32, (q, q), 0)
        ci = lax.broadcasted_iota(jnp.int32, (q, q), 1)
        tri = (ri >= ci).astype(F32)
        neg_a = -jnp.exp(alog_ref[...])
        for k in range(cps):
            rk = slice(q * k, q * (k + 1))
            sp = _softplus(raw_ref[rk, :] + bias_ref[...])
            row = pl.program_id(0) * rows + q * k + lax.broadcasted_iota(jnp.int32, (q, LANES), 0)
            dt = jnp.where(row >= PAD_LEAD, sp, 0.0)
            acs = jnp.dot(tri, dt * neg_a, preferred_element_type=F32, precision=HI)
            dt_ref[rk, :] = dt
            acs_ref[rk, :] = acs
            acst_ref[k] = acs.T

    return _pallas(
        body, name="dt_prep", grid=(nc // cps,),
        in_specs=[pl.BlockSpec((rows, LANES), lambda c: (c, ODT // LANES)), pl.BlockSpec((1, LANES), lambda c: (0, 0)),
                  pl.BlockSpec((1, LANES), lambda c: (0, 0))],
        out_specs=[pl.BlockSpec((rows, LANES), lambda c: (c, 0)), pl.BlockSpec((rows, LANES), lambda c: (c, 0)),
                   pl.BlockSpec((cps, LANES, q), lambda c: (c, 0, 0))],
        out_shape=[jax.ShapeDtypeStruct((t, LANES), F32), jax.ShapeDtypeStruct((t, LANES), F32),
                   jax.ShapeDtypeStruct((nc, LANES, q), F32)],
        compiler_params=_cp("parallel"))(proj, dt_bias_l, a_log_l)


def _head_cols(blk, idx):
    lane = lax.broadcasted_iota(jnp.int32, blk.shape, 1)
    return jnp.sum(jnp.where(lane == idx, blk, 0.0), axis=1, keepdims=True)


class _HeadVals:
    pass


def _lane_head(shape):
    return lax.broadcasted_iota(jnp.int32, shape, len(shape) - 1) >> 6


def _group_heads(g, gi, dtb, acsb, acst_ref, dskb):
    q = dtb.shape[0]
    hv = _HeadVals()
    lh = _lane_head((1, GROUP_W))
    hv.dt = jnp.zeros((q, GROUP_W), F32)
    hv.acs = jnp.zeros((q, GROUP_W), F32)
    hv.acs_last = jnp.zeros((1, GROUP_W), F32)
    hv.dsk = jnp.zeros((1, GROUP_W), F32)
    rows = []
    for r in range(HPG):
        idx = GROUPS * g + r
        sel = lh == r
        acs_r = acst_ref[0, GROUPS * gi + r:GROUPS * gi + r + 1, :]
        rows.append(acs_r)
        hv.dt = jnp.where(sel, _head_cols(dtb, idx), hv.dt)
        hv.acs = jnp.where(sel, _head_cols(acsb, idx), hv.acs)
        hv.acs_last = jnp.where(sel, acs_r[:, q - 1:q], hv.acs_last)
        hv.dsk = jnp.where(sel, _head_cols(dskb, idx), hv.dsk)
    hv.acs_row = jnp.concatenate(rows, axis=1)
    return hv


def _head_tri(q, lower):
    ri = lax.broadcasted_iota(jnp.int32, (q, GROUP_W), 0)
    li = lax.broadcasted_iota(jnp.int32, (q, GROUP_W), 1) & (HEAD_DIM - 1)
    return ri >= li if lower else ri <= li


def _block_diag_mask():
    rb = lax.broadcasted_iota(jnp.int32, (GROUP_W, GROUP_W), 0) >> 6
    cb = lax.broadcasted_iota(jnp.int32, (GROUP_W, GROUP_W), 1) >> 6
    return rb == cb


def _block_diag(v, mask):
    return jnp.where(mask, jnp.concatenate([v] * HPG, axis=0), jnp.zeros((), v.dtype))


def _head_sums(v, r):
    return jnp.sum(jnp.where(_lane_head((1, GROUP_W)) == r, v, 0.0), axis=1, keepdims=True)


def _ssd_fwd(xbc, proj, dt, acs, acst, d_skip_l, ssd_norm_w):
    t = xbc.shape[0]
    q = CHUNK
    nc = t // q

    gps = SSD_FWD_GROUPS_PER_STEP
    gw, sw = gps * GROUP_W, gps * D_STATE

    def body(xs_ref, b_ref, c_ref, dt_ref, acs_ref, acst_ref, z_ref, dsk_ref, nw_ref,
             y_ref, ymix_ref, st_ref, state):
        @pl.when(pl.program_id(1) == 0)
        def _():
            state[...] = jnp.zeros_like(state)

        lower = _head_tri(q, True)
        bd_mask = _block_diag_mask()
        for gi in range(gps):
            g = gps * pl.program_id(0) + gi
            cols = slice(GROUP_W * gi, GROUP_W * (gi + 1))
            x = xs_ref[:, cols]
            bmb = _bf(b_ref[:, D_STATE * gi:D_STATE * (gi + 1)])
            cmb = _bf(c_ref[:, D_STATE * gi:D_STATE * (gi + 1)])
            hv = _group_heads(g, gi, dt_ref[...], acs_ref[...], acst_ref, dsk_ref[...])
            decay = jnp.exp(jnp.where(lower, hv.acs - hv.acs_row, NEG))
            m_all = _bf(_nt(cmb, jnp.concatenate([bmb] * HPG, axis=0)) * decay)
            xdt = x * hv.dt
            s_prev = state[gi]
            st_ref[0, gi] = s_prev
            y = (_mm(m_all, _block_diag(_bf(xdt), bd_mask)) + _mm(cmb, _bf(s_prev)) * jnp.exp(hv.acs) + hv.dsk * x)
            state[gi] = jnp.exp(hv.acs_last) * s_prev + _tn(bmb, _bf(xdt * jnp.exp(hv.acs_last - hv.acs)))
            y_ref[:, cols] = y
            z = z_ref[:, cols]
            yg = y * (z * _sigmoid(z))
            ms = jnp.mean(yg * yg, axis=-1, keepdims=True)
            ymix_ref[:, cols] = _bf(yg * lax.rsqrt(ms + EPS) * nw_ref[:, cols])

    return _pallas(
        body, name="ssd_fwd", grid=(GROUPS // gps, nc),
        in_specs=[pl.BlockSpec((q, gw), lambda g, c: (c, g)),
                  pl.BlockSpec((q, sw), lambda g, c: (c, D_SSD // sw + g)),
                  pl.BlockSpec((q, sw), lambda g, c: (c, (D_SSD + GROUPS * D_STATE) // sw + g)),
                  pl.BlockSpec((q, LANES), lambda g, c: (c, 0)), pl.BlockSpec((q, LANES), lambda g, c: (c, 0)),
                  pl.BlockSpec((1, gps * GROUPS, q), lambda g, c: (c, g, 0)),
                  pl.BlockSpec((q, gw), lambda g, c: (c, g)),
                  pl.BlockSpec((1, LANES), lambda g, c: (0, 0)), pl.BlockSpec((1, gw), lambda g, c: (0, g))],
        out_specs=[pl.BlockSpec((q, gw), lambda g, c: (c, g)), pl.BlockSpec((q, gw), lambda g, c: (c, g)),
                   pl.BlockSpec((1, gps, D_STATE, GROUP_W), lambda g, c: (c, g, 0, 0))],
        out_shape=[jax.ShapeDtypeStruct((t, D_SSD), F32), jax.ShapeDtypeStruct((t, D_SSD), BF16),
                   jax.ShapeDtypeStruct((nc, GROUPS, D_STATE, GROUP_W), F32)],
        scratch_shapes=[pltpu.VMEM((gps, D_STATE, GROUP_W), F32)],
        compiler_params=_cp("parallel", "arbitrary"))(xbc, xbc, xbc, dt, acs, acst, proj, d_skip_l, ssd_norm_w)


def _swap_halves(v):
    lane = lax.broadcasted_iota(jnp.int32, v.shape, 1)
    return jnp.where((lane & (HEAD_DIM - 1)) < HEAD_DIM // 2, pltpu.roll(v, LANES - HEAD_DIM // 2, 1),
                     pltpu.roll(v, HEAD_DIM // 2, 1))


def _rope(qsrc, q_off, ksrc, k_off, cos_t, sin_t, beside=None):
    t = qsrc.shape[0]
    tr = _tile(t, 832)
    q_scale = HEAD_DIM ** -0.5
    qw = D_ATT if beside is None else 2 * D_ATT

    def body(q_ref, k_ref, cos_ref, sin_ref, *refs):
        qo_ref, ko_ref = refs[-2:]
        cs = cos_ref[...]
        sn = sin_ref[...]
        for src, dst, width, scale in ((q_ref, qo_ref, D_ATT, q_scale), (k_ref, ko_ref, D_KV, 1.0)):
            for s in range(width // LANES):
                v = src[:, LANES * s:LANES * (s + 1)].astype(F32)
                dst[:, LANES * s:LANES * (s + 1)] = _bf((v * cs + _swap_halves(v) * sn) * scale)
        if beside is not None:
            qo_ref[:, D_ATT:2 * D_ATT] = refs[0][...]

    extra = [] if beside is None else [beside]
    return _pallas(
        body, name="rope", grid=(t // tr,),
        in_specs=[pl.BlockSpec((tr, D_ATT), lambda i: (i, q_off // D_ATT)),
                  pl.BlockSpec((tr, D_KV), lambda i: (i, k_off // D_KV)),
                  pl.BlockSpec((tr, LANES), lambda i: (i, 0)), pl.BlockSpec((tr, LANES), lambda i: (i, 0))]
        + [pl.BlockSpec((tr, D_ATT), lambda i: (i, 0)) for _ in extra],
        out_specs=[pl.BlockSpec((tr, qw), lambda i: (i, 0)), pl.BlockSpec((tr, D_KV), lambda i: (i, 0))],
        out_shape=[jax.ShapeDtypeStruct((t, qw), BF16), jax.ShapeDtypeStruct((t, D_KV), BF16)],
        compiler_params=_cp("parallel"))(qsrc, ksrc, cos_t, sin_t, *extra)


def _attn_chunks_per_step(nc):
    return max(d for d in range(1, 6) if nc % d == 0)


def _band(ref, c):
    return [ref[pl.ds(pl.multiple_of(jnp.maximum(c - j, 0) * CHUNK, CHUNK), CHUNK), :] for j in (2, 1, 0)]


def _attn_probs(qh, kb, sink_col, valid):
    s = jnp.where(valid, _nt(qh, kb), NEG)
    m = jnp.maximum(jnp.max(s, axis=1, keepdims=True), sink_col)
    p = jnp.exp(s - m)
    psink = jnp.exp(sink_col - m)
    return p, psink, 1.0 / (jnp.sum(p, axis=1, keepdims=True) + psink)


def _attn_operands(c, q, k_refs, v_refs, sink_ref, h):
    qh = jnp.concatenate([q[:, HEAD_DIM * (REP * h + r):HEAD_DIM * (REP * h + r + 1)] for r in range(REP)], axis=0)
    kb = jnp.concatenate([k[:, HEAD_DIM * h:HEAD_DIM * (h + 1)] for k in k_refs], axis=0)
    vb = jnp.concatenate([_bf(v[:, HEAD_DIM * h:HEAD_DIM * (h + 1)]) for v in v_refs], axis=0)
    rows = lax.broadcasted_iota(jnp.int32, (REP * CHUNK, 1), 0) >> 6
    sink_col = jnp.zeros((REP * CHUNK, 1), F32)
    for r in range(REP):
        sink_col = jnp.where(rows == r, sink_ref[REP * h + r], sink_col)
    key_abs = (c - (BAND_CHUNKS - 1)) * CHUNK + lax.broadcasted_iota(jnp.int32, (1, BAND_CHUNKS * CHUNK), 1)
    return qh, kb, vb, sink_col, key_abs >= PAD_LEAD


def _attn_fwd(qr, kr, proj, sinks):
    t = qr.shape[0]
    nc = t // CHUNK
    cps = _attn_chunks_per_step(nc)
    rows = cps * CHUNK

    def body(q_ref, k_ref, v_ref, g_ref, sink_ref, o_ref):
        for j in range(cps):
            c = pl.program_id(0) * cps + j
            rj = slice(CHUNK * j, CHUNK * (j + 1))
            ks, vs = _band(k_ref, c), _band(v_ref, c)
            q = q_ref[rj, :]
            outs = []
            for h in range(KV_HEADS):
                qh, kb, vb, sink_col, valid = _attn_operands(c, q, ks, vs, sink_ref, h)
                p, _, inv = _attn_probs(qh, kb, sink_col, valid)
                o = _mm(_bf(p), vb) * inv
                outs += [o[CHUNK * r:CHUNK * (r + 1)] for r in range(REP)]
            att = jnp.concatenate(outs, axis=1)
            gate = g_ref[rj, :]
            o_ref[rj, :] = _bf(att * (gate * _sigmoid(gate)))

    return _pallas(
        body, name="attn_fwd", grid=(nc // cps,),
        in_specs=[pl.BlockSpec((rows, D_ATT), lambda i: (i, 0)), pl.BlockSpec((t, D_KV), lambda i: (0, 0)),
                  pl.BlockSpec((t, D_KV), lambda i: (0, OV // D_KV)),
                  pl.BlockSpec((rows, D_ATT), lambda i: (i, OG // D_ATT)), pl.BlockSpec(memory_space=pltpu.SMEM)],
        out_specs=pl.BlockSpec((rows, D_ATT), lambda i: (i, 0)),
        out_shape=jax.ShapeDtypeStruct((t, D_ATT), BF16),
        compiler_params=_cp("parallel"))(qr, kr, proj, proj, sinks)


def _outproj(ymix, amix, w_out):
    t = ymix.shape[0]
    tm, tn = _tile(t, 832), 1024

    def body(y_ref, a_ref, wy_ref, wa_ref, o_ref):
        o_ref[...] = _mm(y_ref[...], wy_ref[...]) + _mm(a_ref[...], wa_ref[...])

    return _pallas(
        body, name="outproj", grid=(t // tm, D_MODEL // tn),
        in_specs=[pl.BlockSpec((tm, D_SSD), lambda i, j: (i, 0)), pl.BlockSpec((tm, D_ATT), lambda i, j: (i, 0)),
                  pl.BlockSpec((D_SSD, tn), lambda i, j: (0, j)),
                  pl.BlockSpec((D_ATT, tn), lambda i, j: (D_SSD // D_ATT, j))],
        out_specs=pl.BlockSpec((tm, tn), lambda i, j: (i, j)),
        out_shape=jax.ShapeDtypeStruct((t, D_MODEL), F32),
        compiler_params=_cp("parallel", "parallel"))(ymix, amix, w_out, w_out)


def _post_loss(out, x, target, norm_post_w):
    t = out.shape[0]
    nc = t // CHUNK
    cps = _attn_chunks_per_step(nc)
    rows = cps * CHUNK

    def body(o_ref, *refs):
        x_refs, tg_refs = refs[:cps], refs[cps:2 * cps]
        nw_ref, dout_ref, dy_ref, loss_ref, gnw_ref = refs[2 * cps:]
        i = pl.program_id(0)

        @pl.when(i == 0)
        def _():
            loss_ref[...] = jnp.zeros_like(loss_ref)
            gnw_ref[...] = jnp.zeros_like(gnw_ref)

        nw = nw_ref[...]
        loss = jnp.zeros((), F32)
        gnw = jnp.zeros((1, D_MODEL), F32)
        for k in range(cps):
            rk = slice(CHUNK * k, CHUNK * (k + 1))
            frames = i * cps + k > 0
            o = o_ref[rk, :]
            rstd = lax.rsqrt(jnp.mean(o * o, axis=-1, keepdims=True) + EPS)
            n = o * rstd
            err = jnp.where(frames, x_refs[k][...] + n * nw - tg_refs[k][...], 0.0)
            loss = loss + jnp.sum(err * err)
            dy = err * (1.0 / D_MODEL)
            dy_ref[rk, :] = dy
            gnw = gnw + jnp.sum(dy * n, axis=0, keepdims=True)
            dn = dy * nw
            dout_ref[rk, :] = _bf(rstd * (dn - n * jnp.mean(dn * n, axis=-1, keepdims=True)))
        loss_ref[...] += loss * (0.5 / D_MODEL)
        gnw_ref[...] += gnw

    lower = [pl.BlockSpec((CHUNK, D_MODEL), functools.partial(lambda i, k: (jnp.maximum(i * cps + k - 1, 0), 0), k=k))
             for k in range(cps)]
    return _pallas(
        body, name="post_loss", grid=(nc // cps,),
        in_specs=[pl.BlockSpec((rows, D_MODEL), lambda i: (i, 0))] + lower + lower
        + [pl.BlockSpec((1, D_MODEL), lambda i: (0, 0))],
        out_specs=[pl.BlockSpec((rows, D_MODEL), lambda i: (i, 0)), pl.BlockSpec((rows, D_MODEL), lambda i: (i, 0)),
                   pl.BlockSpec((8, LANES), lambda i: (0, 0)), pl.BlockSpec((1, D_MODEL), lambda i: (0, 0))],
        out_shape=[jax.ShapeDtypeStruct((t, D_MODEL), BF16), jax.ShapeDtypeStruct((t, D_MODEL), F32),
                   jax.ShapeDtypeStruct((8, LANES), F32), jax.ShapeDtypeStruct((1, D_MODEL), F32)],
        compiler_params=_cp("arbitrary"))(out, *([x] * cps), *([target] * cps), norm_post_w)


def _carried(grid, carry):
    if carry is None:
        return [], [], [], [], lambda refs: None, lambda refs: None
    hn = carry.shape[1] // 2

    def at(ids, which):
        cond = None
        for d, size in enumerate(grid):
            here = pl.program_id(d) == (0 if which == "first" else size - 1)
            cond = here if cond is None else cond & here
        return cond

    def start(refs):
        @pl.when(at(grid, "first"))
        def _():
            for cp in _pair_copies(*refs):
                cp.start()

    def finish(refs):
        @pl.when(at(grid, "last"))
        def _():
            for cp in _pair_copies(*refs):
                cp.wait()

    return ([ANY], [ANY], [jax.ShapeDtypeStruct((carry.shape[0], hn), F32)],
            [pltpu.SemaphoreType.DMA((PAIR_CHUNKS,)), pltpu.SemaphoreType.DMA((PAIR_CHUNKS,))], start, finish)


def _nt_matmul(a, b, name, carry=None):
    t, k = a.shape
    n = b.shape[0]
    tm, tn = _tile(t, 832), 1024
    grid = (t // tm, n // tn)
    cin, cout, cshape, cscratch, start, finish = _carried(grid, carry)

    def body(a_ref, b_ref, *refs):
        o_ref = refs[len(cin)]
        comm = (refs[0], refs[2], refs[3], refs[4]) if carry is not None else None
        start(comm)
        o_ref[...] = _nt(a_ref[...], b_ref[...])
        finish(comm)

    res = _pallas(
        body, name=name, grid=grid,
        in_specs=[pl.BlockSpec((tm, k), lambda i, j: (i, 0)), pl.BlockSpec((tn, k), lambda i, j: (j, 0))] + cin,
        out_specs=[pl.BlockSpec((tm, tn), lambda i, j: (i, j))] + cout,
        out_shape=[jax.ShapeDtypeStruct((t, n), F32)] + cshape, scratch_shapes=cscratch,
        compiler_params=_cp("arbitrary", "arbitrary"))(a, b, *([carry] if carry is not None else []))
    return res if carry is not None else res[0]


def _tn_matmul(a, b, name, carry=None):
    t, m = a.shape
    n = b.shape[1]
    tk, tm, tn = _tile(t, 832), min(m, 2048), min(n, 2048)
    nk = t // tk
    grid = (m // tm, n // tn, nk)
    cin, cout, cshape, cscratch, start, finish = _carried(grid, carry)

    def body(a_ref, b_ref, *refs):
        o_ref = refs[len(cin)]
        comm = (refs[0], refs[2], refs[3], refs[4]) if carry is not None else None
        start(comm)

        @pl.when(pl.program_id(2) == 0)
        def _():
            o_ref[...] = jnp.zeros_like(o_ref)
        o_ref[...] += _tn(a_ref[...], b_ref[...])
        finish(comm)

    res = _pallas(
        body, name=name, grid=grid,
        in_specs=[pl.BlockSpec((tk, tm), lambda i, j, k: (k, i)), pl.BlockSpec((tk, tn), lambda i, j, k: (k, j))] + cin,
        out_specs=[pl.BlockSpec((tm, tn), lambda i, j, k: (i, j))] + cout,
        out_shape=[jax.ShapeDtypeStruct((m, n), F32)] + cshape, scratch_shapes=cscratch,
        compiler_params=_cp("arbitrary", "arbitrary", "arbitrary"))(a, b, *([carry] if carry is not None else []))
    return res if carry is not None else res[0]


def _attn_bwd(qr, kr, proj, dmix, sinks, ga):
    t = qr.shape[0]
    nc = t // CHUNK
    cps = _attn_chunks_per_step(nc)
    nsteps = nc // cps
    rows_step = cps * CHUNK

    def body(q_ref, k_ref, v_ref, g_ref, da_ref, sink_ref, ga_ref, dq_ref, dg_ref, dk_ref, dv_ref, gs_ref,
             got_ref, send_sems, recv_sems):
        step = pl.program_id(0)

        @pl.when(step == 0)
        def _():
            for cp in _exchange_copies(ga_ref, got_ref, send_sems, recv_sems):
                cp.start()
            dk_ref[...] = jnp.zeros_like(dk_ref)
            dv_ref[...] = jnp.zeros_like(dv_ref)
            gs_ref[...] = jnp.zeros_like(gs_ref)

        lane = lax.broadcasted_iota(jnp.int32, (1, LANES), 1)
        rows = lax.broadcasted_iota(jnp.int32, (REP * CHUNK, 1), 0) >> 6
        gs = jnp.zeros((1, LANES), F32)
        dk_parts = [[] for _ in range(cps + BAND_CHUNKS - 1)]
        dv_parts = [[] for _ in range(cps + BAND_CHUNKS - 1)]
        for j in range(cps):
            c = step * cps + j
            rj = slice(CHUNK * j, CHUNK * (j + 1))
            ks, vs = _band(k_ref, c), _band(v_ref, c)
            q = q_ref[rj, :]
            gate = g_ref[rj, :]
            sg = _sigmoid(gate)
            da = da_ref[rj, :]
            datt = da * (gate * sg)
            dqs, atts, dks, dvs = [], [], [], []
            for h in range(KV_HEADS):
                qh, kb, vb, sink_col, valid = _attn_operands(c, q, ks, vs, sink_ref, h)
                p, psink, inv = _attn_probs(qh, kb, sink_col, valid)
                pb = _bf(p)
                o = _mm(pb, vb) * inv
                do = jnp.concatenate([datt[:, HEAD_DIM * (REP * h + r):HEAD_DIM * (REP * h + r + 1)]
                                      for r in range(REP)], axis=0)
                dob = _bf(do * inv)
                delta = jnp.sum(do * o, axis=1, keepdims=True) * inv
                ds = _bf(p * (_nt(dob, vb) - delta))
                gsink = -psink * delta
                for r in range(REP):
                    gs = gs + jnp.where(lane == REP * h + r, jnp.sum(jnp.where(rows == r, gsink, 0.0)), 0.0)
                dqh = _mm(ds, kb)
                dqs += [dqh[CHUNK * r:CHUNK * (r + 1)] for r in range(REP)]
                atts += [o[CHUNK * r:CHUNK * (r + 1)] for r in range(REP)]
                dks.append(_tn(ds, qh))
                dvs.append(_tn(pb, dob))
            dq_ref[rj, :] = jnp.concatenate(dqs, axis=1)
            att = jnp.concatenate(atts, axis=1)
            dg_ref[rj, :] = _bf(da * att * (sg * (1.0 + gate * (1.0 - sg))))
            dkf = jnp.concatenate(dks, axis=1)
            dvf = jnp.concatenate(dvs, axis=1)
            for b in range(BAND_CHUNKS):
                dk_parts[j + b].append(dkf[CHUNK * b:CHUNK * (b + 1)])
                dv_parts[j + b].append(dvf[CHUNK * b:CHUNK * (b + 1)])
        gs_ref[0:1, :] += gs
        for rel in range(cps + BAND_CHUNKS - 1):
            r0 = pl.multiple_of(jnp.maximum(step * cps - (BAND_CHUNKS - 1) + rel, 0) * CHUNK, CHUNK)
            dk_ref[pl.ds(r0, CHUNK), :] += sum(dk_parts[rel][1:], dk_parts[rel][0])
            dv_ref[pl.ds(r0, CHUNK), :] += sum(dv_parts[rel][1:], dv_parts[rel][0])

        @pl.when(step == nsteps - 1)
        def _():
            for cp in _exchange_copies(ga_ref, got_ref, send_sems, recv_sems):
                cp.wait()

    return _pallas(
        body, name="attn_bwd", grid=(nsteps,),
        in_specs=[pl.BlockSpec((rows_step, D_ATT), lambda i: (i, 0)), pl.BlockSpec((t, D_KV), lambda i: (0, 0)),
                  pl.BlockSpec((t, D_KV), lambda i: (0, OV // D_KV)),
                  pl.BlockSpec((rows_step, D_ATT), lambda i: (i, OG // D_ATT)),
                  pl.BlockSpec((rows_step, D_ATT), lambda i: (i, D_SSD // D_ATT)),
                  pl.BlockSpec(memory_space=pltpu.SMEM), ANY],
        out_specs=[pl.BlockSpec((rows_step, D_ATT), lambda i: (i, 0)), pl.BlockSpec((rows_step, D_ATT), lambda i: (i, 0)),
                   pl.BlockSpec((t, D_KV), lambda i: (0, 0)), pl.BlockSpec((t, D_KV), lambda i: (0, 0)),
                   pl.BlockSpec((8, LANES), lambda i: (0, 0)), ANY],
        out_shape=[jax.ShapeDtypeStruct((t, D_ATT), F32), jax.ShapeDtypeStruct((t, D_ATT), BF16),
                   jax.ShapeDtypeStruct((t, D_KV), F32), jax.ShapeDtypeStruct((t, D_KV), F32),
                   jax.ShapeDtypeStruct((8, LANES), F32), _exchange_shape(ga)],
        scratch_shapes=_exchange_scratch(),
        compiler_params=_cp("arbitrary"))(qr, kr, proj, proj, dmix, sinks, ga)


def _ssd_bwd(dmix, y_ssd, xbc, proj, dt, acs, acst, states, d_skip_l, ssd_norm_w):
    t = xbc.shape[0]
    q = CHUNK
    nc = t // q
    gps = SSD_BWD_GROUPS_PER_STEP
    gw, sw = gps * GROUP_W, gps * D_STATE

    def body(dmix_ref, y_ref, z_ref, nw_ref, xs_ref, b_ref, c_ref, dt_ref, acs_ref, acst_ref, st_ref, dsk_ref,
             dz_ref, dxs_ref, db_ref, dc_ref, dacs_ref, ddt_ref, gnw_ref, gdsk_ref, dstate):
        @pl.when(pl.program_id(1) == 0)
        def _():
            dstate[...] = jnp.zeros_like(dstate)
            gnw_ref[...] = jnp.zeros_like(gnw_ref)
            gdsk_ref[...] = jnp.zeros_like(gdsk_ref)

        last_row = lax.broadcasted_iota(jnp.int32, (q, 1), 0) == q - 1
        lane = lax.broadcasted_iota(jnp.int32, (q, LANES), 1)
        lane1 = lax.broadcasted_iota(jnp.int32, (8, LANES), 1)
        lower, upper = _head_tri(q, True), _head_tri(q, False)
        bd_mask = _block_diag_mask()
        for gi in range(gps):
            g = gps * pl.program_id(0) + gi
            cols = slice(GROUP_W * gi, GROUP_W * (gi + 1))
            scols = slice(D_STATE * gi, D_STATE * (gi + 1))
            y = y_ref[:, cols]
            z = z_ref[:, cols]
            sz = _sigmoid(z)
            silu_z = z * sz
            yg = y * silu_z
            rstd = lax.rsqrt(jnp.mean(yg * yg, axis=-1, keepdims=True) + EPS)
            n = yg * rstd
            dout = dmix_ref[:, cols]
            gnw_ref[:, cols] += jnp.sum(dout * n, axis=0, keepdims=True)
            dn = dout * nw_ref[:, cols]
            dyg = rstd * (dn - n * jnp.mean(dn * n, axis=-1, keepdims=True))
            dy = dyg * silu_z
            dz_ref[:, cols] = _bf(dyg * y * (sz * (1.0 + z * (1.0 - sz))))

            x = xs_ref[:, cols]
            bmb, cmb = _bf(b_ref[:, scols]), _bf(c_ref[:, scols])
            hv = _group_heads(g, gi, dt_ref[...], acs_ref[...], acst_ref, dsk_ref[...])
            dec = jnp.exp(jnp.where(lower, hv.acs - hv.acs_row, NEG))
            dect = jnp.exp(jnp.where(upper, hv.acs_row - hv.acs, NEG))
            b4 = jnp.concatenate([bmb] * HPG, axis=0)
            c4 = jnp.concatenate([cmb] * HPG, axis=0)
            m_all = _nt(cmb, b4) * dec
            mt_all = _nt(bmb, c4) * dect
            xdt = x * hv.dt
            xdt_b, dyb = _bf(xdt), _bf(dy)
            x_bd, dy_bd = _block_diag(xdt_b, bd_mask), _block_diag(dyb, bd_mask)
            s_prev = st_ref[0, gi]
            spb = _bf(s_prev)
            ds_new = dstate[gi]
            dsb = _bf(ds_new)
            e = jnp.exp(hv.acs)
            elast = jnp.exp(hv.acs_last)
            dte = jnp.exp(hv.acs_last - hv.acs)
            bds = _mm(bmb, dsb)
            dxdt = _mm(_bf(mt_all), dy_bd) + bds * dte
            dm = _nt(dyb, x_bd)
            dmt = _nt(xdt_b, dy_bd)
            dye = _bf(dy * e)
            dc_ref[:, scols] = _mm(_bf(dm * dec), b4) + _nt(dye, spb)
            db_ref[:, scols] = _mm(_bf(dmt * dect), c4) + _nt(_bf(xdt * dte), dsb)
            dstate[gi] = elast * ds_new + _tn(cmb, dye)
            dxs_ref[:, cols] = dxdt * hv.dt + hv.dsk * dy
            ddte_dte = bds * xdt * dte
            dacs_l = dm * m_all - dmt * mt_all + dy * _mm(cmb, spb) * e - ddte_dte
            dlast_l = (jnp.sum(ddte_dte, axis=0, keepdims=True)
                       + jnp.sum(s_prev * ds_new, axis=0, keepdims=True) * elast)
            ddt_l = dxdt * x
            gdsk_l = jnp.sum(dy * x, axis=0, keepdims=True)
            dacs_out = jnp.zeros((q, LANES), F32)
            ddt_out = jnp.zeros((q, LANES), F32)
            gdsk = jnp.zeros((8, LANES), F32)
            for r in range(HPG):
                dacs = _head_sums(dacs_l, r) + jnp.where(last_row, _head_sums(dlast_l, r), 0.0)
                dacs_out = jnp.where(lane == r, dacs, dacs_out)
                ddt_out = jnp.where(lane == r, _head_sums(ddt_l, r), ddt_out)
                gdsk = gdsk + jnp.where(lane1 == r, _head_sums(gdsk_l, r), 0.0)
            dacs_ref[:, LANES * gi:LANES * (gi + 1)] = dacs_out
            ddt_ref[:, LANES * gi:LANES * (gi + 1)] = ddt_out
            gdsk_ref[gi] += gdsk

    rev = lambda c: nc - 1 - c
    wide = pl.BlockSpec((q, gw), lambda g, c: (rev(c), g))
    return _pallas(
        body, name="ssd_bwd", grid=(GROUPS // gps, nc),
        in_specs=[wide, wide, wide, pl.BlockSpec((1, gw), lambda g, c: (0, g)), wide,
                  pl.BlockSpec((q, sw), lambda g, c: (rev(c), D_SSD // sw + g)),
                  pl.BlockSpec((q, sw), lambda g, c: (rev(c), (D_SSD + GROUPS * D_STATE) // sw + g)),
                  pl.BlockSpec((q, LANES), lambda g, c: (rev(c), 0)), pl.BlockSpec((q, LANES), lambda g, c: (rev(c), 0)),
                  pl.BlockSpec((1, gps * GROUPS, q), lambda g, c: (rev(c), g, 0)),
                  pl.BlockSpec((1, gps, D_STATE, GROUP_W), lambda g, c: (rev(c), g, 0, 0)),
                  pl.BlockSpec((1, LANES), lambda g, c: (0, 0))],
        out_specs=[wide, wide,
                   pl.BlockSpec((q, sw), lambda g, c: (rev(c), g)), pl.BlockSpec((q, sw), lambda g, c: (rev(c), g)),
                   pl.BlockSpec((q, gps * LANES), lambda g, c: (rev(c), g)),
                   pl.BlockSpec((q, gps * LANES), lambda g, c: (rev(c), g)),
                   pl.BlockSpec((1, gw), lambda g, c: (0, g)), pl.BlockSpec((gps, 8, LANES), lambda g, c: (g, 0, 0))],
        out_shape=[jax.ShapeDtypeStruct((t, D_SSD), BF16), jax.ShapeDtypeStruct((t, D_SSD), F32),
                   jax.ShapeDtypeStruct((t, GROUPS * D_STATE), F32), jax.ShapeDtypeStruct((t, GROUPS * D_STATE), F32),
                   jax.ShapeDtypeStruct((t, GROUPS * LANES), F32), jax.ShapeDtypeStruct((t, GROUPS * LANES), F32),
                   jax.ShapeDtypeStruct((1, D_SSD), F32), jax.ShapeDtypeStruct((GROUPS, 8, LANES), F32)],
        scratch_shapes=[pltpu.VMEM((gps, D_STATE, GROUP_W), F32)],
        compiler_params=_cp("parallel", "arbitrary"))(dmix, y_ssd, proj, ssd_norm_w, xbc, xbc, xbc, dt, acs, acst,
                                                      states, d_skip_l)


def _dt_bwd(dacs_g, ddt_g, dt, proj, dt_bias_l, a_log_l):
    t = dt.shape[0]
    q = CHUNK
    nc = t // q
    cps = _chunks_per_step(nc)
    rows = cps * q

    def body(dacs_ref, ddt_ref, dt_ref, raw_ref, bias_ref, alog_ref, draw_ref, ga_ref, gb_ref):
        @pl.when(pl.program_id(0) == 0)
        def _():
            ga_ref[...] = jnp.zeros_like(ga_ref)
            gb_ref[...] = jnp.zeros_like(gb_ref)

        lane = lax.broadcasted_iota(jnp.int32, (q, LANES), 1)
        ri = lax.broadcasted_iota(jnp.int32, (q, q), 0)
        ci = lax.broadcasted_iota(jnp.int32, (q, q), 1)
        triu = (ri <= ci).astype(F32)
        a = -jnp.exp(alog_ref[...])
        used = (lane & (GROUPS - 1)) < HPG
        ga = jnp.zeros((1, LANES), F32)
        gb = jnp.zeros((1, LANES), F32)
        for k in range(cps):
            rk = slice(q * k, q * (k + 1))
            dacs = jnp.zeros((q, LANES), F32)
            ddt = jnp.zeros((q, LANES), F32)
            for g in range(GROUPS):
                mask = (lane >= GROUPS * g) & (lane < GROUPS * g + HPG)
                sl = slice(LANES * g, LANES * (g + 1))
                if g == 0:
                    dacs = jnp.where(mask, dacs_ref[rk, sl], dacs)
                    ddt = jnp.where(mask, ddt_ref[rk, sl], ddt)
                else:
                    dacs = jnp.where(mask, pltpu.roll(dacs_ref[rk, sl], GROUPS * g, 1), dacs)
                    ddt = jnp.where(mask, pltpu.roll(ddt_ref[rk, sl], GROUPS * g, 1), ddt)
            dda = jnp.dot(triu, dacs, preferred_element_type=F32, precision=HI)
            row = pl.program_id(0) * rows + q * k + lax.broadcasted_iota(jnp.int32, (q, LANES), 0)
            dsp = jnp.where((row >= PAD_LEAD) & used, dda * a + ddt, 0.0)
            draw = dsp * _sigmoid(raw_ref[rk, :] + bias_ref[...])
            draw_ref[rk, :] = _bf(draw)
            gb = gb + jnp.sum(draw, axis=0, keepdims=True)
            ga = ga + jnp.sum(jnp.where(used, dda * dt_ref[rk, :], 0.0), axis=0, keepdims=True)
        gb_ref[0:1, :] += gb
        ga_ref[0:1, :] += ga * a

    return _pallas(
        body, name="dt_bwd", grid=(nc // cps,),
        in_specs=[pl.BlockSpec((rows, GROUPS * LANES), lambda c: (c, 0)),
                  pl.BlockSpec((rows, GROUPS * LANES), lambda c: (c, 0)),
                  pl.BlockSpec((rows, LANES), lambda c: (c, 0)), pl.BlockSpec((rows, LANES), lambda c: (c, ODT // LANES)),
                  pl.BlockSpec((1, LANES), lambda c: (0, 0)), pl.BlockSpec((1, LANES), lambda c: (0, 0))],
        out_specs=[pl.BlockSpec((rows, LANES), lambda c: (c, 0)), pl.BlockSpec((8, LANES), lambda c: (0, 0)),
                   pl.BlockSpec((8, LANES), lambda c: (0, 0))],
        out_shape=[jax.ShapeDtypeStruct((t, LANES), BF16), jax.ShapeDtypeStruct((8, LANES), F32),
                   jax.ShapeDtypeStruct((8, LANES), F32)],
        compiler_params=_cp("arbitrary"))(dacs_g, ddt_g, dt, proj, dt_bias_l, a_log_l)


def _conv_bwd(dsegs, proj, conv_w, conv_b, col_off, name):
    t, seg_w = dsegs[0].shape
    nseg = len(dsegs)
    width = nseg * seg_w
    tc = 128
    per = seg_w // tc
    rt = _tile(t, 320)
    off_p = (OXS + col_off) // tc
    off_w = col_off // tc

    def body(*refs):
        d_refs = refs[:nseg]
        x_ref, w_ref, b_ref, dx_ref, gw_ref, gb_ref, xp, dup = refs[nseg:]
        assert per & (per - 1) == 0
        seg = pl.program_id(0) >> (per.bit_length() - 1)

        def d_rows(r0):
            d = d_refs[nseg - 1][pl.ds(r0, rt), :]
            for s in range(nseg - 2, -1, -1):
                d = jnp.where(seg == s, d_refs[s][pl.ds(r0, rt), :], d)
            return d

        xp[0:8, :] = jnp.zeros((8, tc), F32)
        xp[8:t + 8, :] = x_ref[...]
        dup[t:t + 8, :] = jnp.zeros((8, tc), F32)
        w = w_ref[...]
        bias = b_ref[...]

        def first(i, acc):
            r0 = pl.multiple_of(i * rt, 8)
            xs = [xp[pl.ds(r0 + 5 + k, rt), :] for k in range(CONV_WIDTH)]
            u = bias + w[3:4, :] * xs[3] + w[2:3, :] * xs[2] + w[1:2, :] * xs[1] + w[0:1, :] * xs[0]
            su = 0.5 + 0.5 * jnp.tanh(0.5 * u)
            du = d_rows(r0) * (su * (1.0 + u * (1.0 - su)))
            dup[pl.ds(r0, rt), :] = du
            return tuple(acc[k] + jnp.sum(du * xs[k], axis=0, keepdims=True) for k in range(CONV_WIDTH)) + (
                acc[CONV_WIDTH] + jnp.sum(du, axis=0, keepdims=True),)

        zero = jnp.zeros((1, tc), F32)
        acc = lax.fori_loop(0, t // rt, first, (zero,) * (CONV_WIDTH + 1))
        gw_ref[...] = jnp.concatenate(acc[:CONV_WIDTH], axis=0)
        gb_ref[...] = acc[CONV_WIDTH]

        def second(i, carry):
            r0 = pl.multiple_of(i * rt, 16)
            dx_ref[pl.ds(r0, rt), :] = _bf(w[3:4, :] * dup[pl.ds(r0, rt), :] + w[2:3, :] * dup[pl.ds(r0 + 1, rt), :]
                                          + w[1:2, :] * dup[pl.ds(r0 + 2, rt), :] + w[0:1, :] * dup[pl.ds(r0 + 3, rt), :])
            return carry

        lax.fori_loop(0, t // rt, second, 0)

    d_specs = [pl.BlockSpec((t, tc), functools.partial(lambda j, s: (0, jnp.clip(j - s * per, 0, per - 1)), s=s))
               for s in range(nseg)]
    return _pallas(
        body, name=name, grid=(width // tc,),
        in_specs=d_specs + [pl.BlockSpec((t, tc), lambda j: (0, j + off_p)),
                            pl.BlockSpec((CONV_WIDTH, tc), lambda j: (0, j + off_w)),
                            pl.BlockSpec((1, tc), lambda j: (0, j + off_w))],
        out_specs=[pl.BlockSpec((t, tc), lambda j: (0, j)), pl.BlockSpec((CONV_WIDTH, tc), lambda j: (0, j)),
                   pl.BlockSpec((1, tc), lambda j: (0, j))],
        out_shape=[jax.ShapeDtypeStruct((t, width), BF16), jax.ShapeDtypeStruct((CONV_WIDTH, width), F32),
                   jax.ShapeDtypeStruct((1, width), F32)],
        scratch_shapes=[pltpu.VMEM((t + 8, tc), F32), pltpu.VMEM((t + 8, tc), F32)],
        compiler_params=_cp("parallel"))(*dsegs, proj, conv_w, conv_b)


def _dinproj(segs, w_re, hpad, norm_w, dy_t, ga):
    t = segs[0].shape[0]
    d = hpad.shape[1]
    tm, tk = _tile(t, 832), SEG_TILE // 2
    rs = tm // 8
    assert rs % 8 == 0
    counts = [s.shape[1] // tk for s in segs]
    firsts = [sum(counts[:s]) for s in range(len(segs))]
    nk = sum(counts)
    assert nk * tk == w_re.shape[1]
    ni = t // tm
    ns = len(segs)

    def body(*refs):
        seg_refs = refs[:ns]
        (w_ref, h_hbm, nw_ref, dy_hbm, ga_ref, dh_ref, gnw_ref, got_ref, send_sems, recv_sems, h_buf, dy_buf,
         row_sems) = refs[ns:]
        i, k = pl.program_id(0), pl.program_id(1)

        def row_copies():
            rows = pl.ds(pl.multiple_of(i * tm, 16), tm)
            return [pltpu.make_async_copy(h_hbm.at[rows, :], h_buf, row_sems.at[0]),
                    pltpu.make_async_copy(dy_hbm.at[rows, :], dy_buf, row_sems.at[1])]

        @pl.when((i == 0) & (k == 0))
        def _():
            for cp in _exchange_copies(ga_ref, got_ref, send_sems, recv_sems):
                cp.start()
            gnw_ref[...] = jnp.zeros_like(gnw_ref)

        @pl.when(k == 0)
        def _():
            for cp in row_copies():
                cp.start()
            dh_ref[...] = jnp.zeros_like(dh_ref)

        for s in range(ns):
            @pl.when((k >= firsts[s]) & (k < firsts[s] + counts[s]))
            def _(s=s):
                dh_ref[...] += _nt(seg_refs[s][...], w_ref[...])

        @pl.when(k == nk - 1)
        def _():
            for cp in row_copies():
                cp.wait()
            nw = nw_ref[...]

            def rows_of(r, gnw):
                rows = pl.ds(pl.multiple_of(r * rs, 8), rs)
                h = h_buf[rows, :]
                rstd = lax.rsqrt(jnp.mean(h * h, axis=-1, keepdims=True) + EPS)
                nrm = h * rstd
                dhn = dh_ref[rows, :]
                dn = dhn * nw
                dh_ref[rows, :] = rstd * (dn - nrm * jnp.mean(dn * nrm, axis=-1, keepdims=True)) + dy_buf[rows, :]
                return gnw + jnp.sum(dhn * nrm, axis=0, keepdims=True)

            gnw_ref[...] += lax.fori_loop(0, tm // rs, rows_of, jnp.zeros((1, d), F32))

        @pl.when((i == ni - 1) & (k == nk - 1))
        def _():
            for cp in _exchange_copies(ga_ref, got_ref, send_sems, recv_sems):
                cp.wait()

    seg_specs = [pl.BlockSpec((tm, tk), functools.partial(lambda i, k, f0, n0: (i, jnp.clip(k - f0, 0, n0 - 1)),
                                                          f0=firsts[s], n0=counts[s])) for s in range(ns)]
    return _pallas(
        body, name="dinproj", grid=(ni, nk),
        in_specs=seg_specs + [pl.BlockSpec((d, tk), lambda i, k: (0, k)), ANY, pl.BlockSpec((1, d), lambda i, k: (0, 0)),
                              ANY, ANY],
        out_specs=[pl.BlockSpec((tm, d), lambda i, k: (i, 0)), pl.BlockSpec((1, d), lambda i, k: (0, 0)), ANY],
        out_shape=[jax.ShapeDtypeStruct((t, d), F32), jax.ShapeDtypeStruct((1, d), F32), _exchange_shape(ga)],
        scratch_shapes=_exchange_scratch() + [pltpu.VMEM((tm, d), F32), pltpu.VMEM((tm, d), F32),
                                              pltpu.SemaphoreType.DMA((2,))],
        compiler_params=_cp("arbitrary", "arbitrary"))(*segs, w_re, hpad, norm_w, dy_t, ga)


def _spread_heads(v):
    v = jnp.pad(v.reshape(GROUPS, HPG), ((0, 0), (0, GROUPS - HPG))).reshape(1, GROUPS * GROUPS)
    return jnp.pad(v, ((0, 0), (0, LANES - GROUPS * GROUPS)))


def _gather_heads(v):
    return v[0:1, :GROUPS * GROUPS].reshape(GROUPS, GROUPS)[:, :HPG].reshape(1, SSD_HEADS)


def _rope_tables(t):
    half = HEAD_DIM // 2
    inv = ROPE_THETA ** (-jnp.arange(half, dtype=F32) / half)
    pos = (jnp.arange(t) - PAD_LEAD).astype(F32)
    ang = pos[:, None] * inv[None, :]
    cos, sin = jnp.cos(ang), jnp.sin(ang)
    cos_t = jnp.concatenate([cos, cos, cos, cos], axis=1)
    sin_t = jnp.concatenate([-sin, sin, -sin, sin], axis=1)
    return cos_t, sin_t


def _column_pieces():
    runs = [(0, OB + 2 * GROUPS * D_STATE, 0)]
    o = OB + 2 * GROUPS * D_STATE
    runs += [(o + HPG * g, HPG, ODT + GROUPS * g) for g in range(GROUPS)]
    o += SSD_HEADS
    for width, dst in ((D_ATT, OQ), (D_KV, OK), (D_KV, OV), (D_ATT, OG)):
        runs.append((o, width, dst))
        o += width
    assert o == D_IN
    pieces = []
    for o0, width, dst in runs:
        for j in range(N_SHARD):
            lo, hi = max(o0, W_IN_SHARD * j), min(o0 + width, W_IN_SHARD * (j + 1))
            if lo < hi:
                pieces.append((j, lo - W_IN_SHARD * j, hi - W_IN_SHARD * j, dst + lo - o0))
    return pieces


def _shards_to_re(w_all):
    _, k, _ = w_all.shape
    tr = 256

    def body(x_ref, o_ref):
        o_ref[:, ODT:ODT + DT_SLAB] = jnp.zeros((tr, DT_SLAB), o_ref.dtype)
        for j, c0, c1, d0 in _column_pieces():
            o_ref[:, d0:d0 + c1 - c0] = x_ref[j, :, c0:c1]

    return _pallas(body, name="shards_to_re", grid=(k // tr,),
                   in_specs=[pl.BlockSpec((N_SHARD, tr, W_IN_SHARD), lambda i: (0, i, 0))],
                   out_specs=pl.BlockSpec((tr, N_RE), lambda i: (i, 0)),
                   out_shape=jax.ShapeDtypeStruct((k, N_RE), w_all.dtype), compiler_params=_cp("parallel"))(w_all)


def _pair_add_to_shards(parts, got, pieces, shard_rows, core, name):
    n = parts[0].shape[1]
    hn = n // 2
    tc = 128
    nt = hn // tc
    ns = len(parts)
    starts = [sum(p.shape[0] for p in parts[:s]) for s in range(ns)]
    moves = []
    for j, c0, c1, d0 in pieces:
        for s, p in enumerate(parts):
            lo, hi = max(d0, starts[s]), min(d0 + c1 - c0, starts[s] + p.shape[0])
            if lo < hi:
                moves.append((s, lo - starts[s], j, c0 + lo - d0, hi - lo))
    assert sum(m[4] for m in moves) == N_SHARD * shard_rows

    def body(core_ref, *refs):
        own, theirs, o_ref, acc = refs[:ns], refs[ns:2 * ns], refs[2 * ns], refs[2 * ns + 1]
        for s, r0, j, c0, rows in moves:
            acc[j, c0:c0 + rows, :] = own[s][r0:r0 + rows, :] + theirs[s][r0:r0 + rows, :]
        o_ref[...] = _bf(acc[...])

    return _pallas(
        body, name=name,
        grid_spec=pltpu.PrefetchScalarGridSpec(
            num_scalar_prefetch=1, grid=(nt,),
            in_specs=[pl.BlockSpec((p.shape[0], tc), lambda i, core_ref: (0, core_ref[0] * nt + i)) for p in parts]
            + [pl.BlockSpec((p.shape[0], tc), lambda i, core_ref: (0, i)) for p in parts],
            out_specs=pl.BlockSpec((N_SHARD, shard_rows, tc), lambda i, core_ref: (0, 0, i)),
            scratch_shapes=[pltpu.VMEM((N_SHARD, shard_rows, tc), F32)]),
        out_shape=jax.ShapeDtypeStruct((N_SHARD, shard_rows, hn), BF16),
        compiler_params=_cp("parallel"))(core, *parts, *got)


def _local_step(x, target, meta, norm_pre_w, w_re, conv_w, conv_b, dt_bias, a_log, d_skip, ssd_norm_w, sinks,
                w_out_shard, norm_post_w, place):
    seq = x.shape[0]
    t = PAD_LEAD + N_META + seq
    hpad = jnp.concatenate([jnp.zeros((PAD_LEAD, D_MODEL), F32), meta, x], axis=0)
    dt_bias_l, a_log_l, d_skip_l = _spread_heads(dt_bias), _spread_heads(a_log), _spread_heads(d_skip)
    cos_t, sin_t = _rope_tables(t)
    sink_v = sinks.reshape(Q_HEADS)

    proj, hn, w_out_all = _inproj(hpad, norm_pre_w, w_re, w_out_shard)
    w_out = w_out_all.reshape(D_MIX, D_MODEL)
    xbc = _conv_fwd(proj, conv_w, conv_b)
    dt, acs, acst = _dt_prep(proj, dt_bias_l, a_log_l)
    y_ssd, ymix, states = _ssd_fwd(xbc, proj, dt, acs, acst, d_skip_l, ssd_norm_w)
    qr, kr = _rope(proj, OQ, proj, OK, cos_t, sin_t)
    amix = _attn_fwd(qr, kr, proj, sink_v)
    out = _outproj(ymix, amix, w_out)
    dout, dy_t, loss_blk, g_norm_post = _post_loss(out, x, target, norm_post_w)

    g_out_y = _tn_matmul(ymix, dout, "gw_out_y")
    g_out_a, got_y = _tn_matmul(amix, dout, "gw_out_a", carry=g_out_y)
    dmix, got_a = _nt_matmul(dout, w_out, "dmix", carry=g_out_a)
    ga_out = _reduce_pair([g_out_y, g_out_a], [got_y, got_a], [(j, 0, W_OUT_SHARD, W_OUT_SHARD * j) for j in range(N_SHARD)],
                          W_OUT_SHARD, place, "gw_out")
    dq_r, dg, dk_r, dv, gs, slabs_out = _attn_bwd(qr, kr, proj, dmix, sink_v, ga_out)
    g_w_out = _reduce_finish(ga_out, slabs_out, place, "gw_out")
    dqg, dk = _rope(dq_r, 0, dk_r, 0, cos_t, -sin_t, beside=dg)
    dz, dxs, db, dc, dacs_g, ddt_g, g_ssd_norm, gdsk = _ssd_bwd(dmix, y_ssd, xbc, proj, dt, acs, acst, states,
                                                                d_skip_l, ssd_norm_w)
    draw, ga, gb = _dt_bwd(dacs_g, ddt_g, dt, proj, dt_bias_l, a_log_l)
    dxs_p, gcw0, gcb0 = _conv_bwd([dxs], proj, conv_w, conv_b, 0, "conv_bwd_x")
    dbc_p, gcw1, gcb1 = _conv_bwd([db, dc], proj, conv_w, conv_b, D_SSD, "conv_bwd_bc")
    tail = jnp.concatenate([dk, _bf(dv), draw, jnp.zeros((t, DT_SLAB - LANES), BF16)], axis=1)
    segs = [dz, dxs_p, dbc_p, dqg, tail]
    g_parts, got_parts = [_tn_matmul(segs[0], hn, "gw_in_0")], []
    for s in range(1, len(segs)):
        part, got = _tn_matmul(segs[s], hn, "gw_in_%d" % s, carry=g_parts[-1])
        g_parts.append(part)
        got_parts.append(got)
    ga_in = _reduce_pair(g_parts, got_parts, _column_pieces(), W_IN_SHARD, place, "gw_in")
    dh, g_norm_pre, slabs_in = _dinproj(segs, w_re, hpad, norm_pre_w, dy_t, ga_in)
    g_w_in_half = _chip_sum(ga_in, slabs_in, place, "gw_in_chip_sum")

    gdsk_l = jnp.concatenate([gdsk[g, 0:1, 0:GROUPS] for g in range(GROUPS)], axis=1)
    gdsk_l = jnp.pad(gdsk_l, ((0, 0), (0, LANES - GROUPS * GROUPS)))
    grads = dict(
        meta_tokens=dh[PAD_LEAD:ROW0], norm_pre_w=g_norm_pre, w_in_half=g_w_in_half,
        conv_w=jnp.concatenate([gcw0, gcw1], axis=1), conv_b=jnp.concatenate([gcb0, gcb1], axis=1),
        dt_bias=_gather_heads(gb), a_log=_gather_heads(ga), d_skip=_gather_heads(gdsk_l), ssd_norm_w=g_ssd_norm,
        attn_sinks=gs[0:1, :Q_HEADS], w_out=g_w_out, norm_post_w=g_norm_post)
    return loss_blk[0, 0], dh[ROW0:], grads


ANY = pl.BlockSpec(memory_space=pl.ANY)
MESH = pl.DeviceIdType.MESH
GATHER_CHUNKS = 4
PAIR_CHUNKS = 8
JOIN_CHUNKS = 8


def _rcopy(src, dst, ssem, rsem, dev):
    return pltpu.make_async_remote_copy(src_ref=src, dst_ref=dst, send_sem=ssem, recv_sem=rsem, device_id=dev,
                                        device_id_type=MESH)


def _place():
    x, y, c = lax.axis_index("x"), lax.axis_index("y"), lax.axis_index("c")
    chips = [(1 - x, y), (x, 1 - y), (1 - x, 1 - y)]
    return x, y, c, chips


def _gather_plan(x_ref, out_ref, send_sems, recv_sems, local_sems, hr, kc):
    ch = hr // kc
    assert ch * kc == hr and ch % 16 == 0
    x, y, c, chips = _place()
    me = 2 * x + y
    sibling = (x, y, 1 - c)

    def piece(chip, hc, k):
        return out_ref.at[chip, pl.ds(hc * hr + k * ch, ch), :]

    def local():
        return [pltpu.make_async_copy(x_ref.at[pl.ds(k * ch, ch), :], out_ref.at[me, pl.ds(k * ch, ch), :],
                                      local_sems.at[k]) for k in range(2 * kc)]

    def first():
        return [_rcopy(x_ref.at[pl.ds(c * hr + k * ch, ch), :], piece(me, c, k), send_sems.at[j * kc + k],
                       recv_sems.at[j * kc + k], (*chip, c)) for j, chip in enumerate(chips) for k in range(kc)]

    def passed(hc):
        return [_rcopy(piece(2 * chip[0] + chip[1], hc, k), piece(2 * chip[0] + chip[1], hc, k),
                       send_sems.at[(3 + j) * kc + k], recv_sems.at[(3 + j) * kc + k], sibling)
                for j, chip in enumerate(chips) for k in range(kc)]

    def arrivals():
        return [_rcopy(piece(2 * chip[0] + chip[1], c, k), piece(2 * chip[0] + chip[1], c, k), send_sems.at[j * kc + k],
                       recv_sems.at[j * kc + k], (*chip, c)) for j, chip in enumerate(chips) for k in range(kc)]

    def start():
        for cp in local() + first():
            cp.start()

    def forward():
        for arrived in arrivals():
            arrived.wait_recv()
        for fw in passed(c):
            fw.start()

    def finish():
        for cp in passed(1 - c):
            cp.wait_recv()
        for cp in first() + passed(c):
            cp.wait_send()
        for cp in local():
            cp.wait()

    return start, forward, finish


def _gather_shards(shard, name, kc, chip, small):
    r, n = shard.shape
    hr = r // 2
    qr = hr // 2
    ch = qr // kc
    assert ch * kc == qr and ch % 16 == 0
    nflow = 12
    tr = 256

    def body(x_ref, p_ref, out_ref, slots_ref, send_sems, recv_sems, *small_sems):
        start_small, wait_small = _chip_small_exchange(p_ref, slots_ref, *small_sems)
        start_small()
        x, y, c, _ = _place()
        me, cxn, cyn, cdg = 2 * x + y, 2 * (1 - x) + y, 2 * x + 1 - y, 2 * (1 - x) + 1 - y
        xn, yn, sibling = (1 - x, y, c), (x, 1 - y, c), (x, y, 1 - c)

        def piece(chip, hc, part, k):
            return out_ref.at[chip, pl.ds(hc * hr + part * qr + k * ch, ch), :]

        def own(part, k):
            return x_ref.at[pl.ds(c * hr + part * qr + k * ch, ch), :]

        def sems(flow, k):
            return send_sems.at[flow * kc + k], recv_sems.at[flow * kc + k]

        def arrival(flow, chip, hc, part, k):
            return _rcopy(piece(chip, hc, part, k), piece(chip, hc, part, k), *sems(flow, k), sibling)

        sends = []
        for flow, part, peer in ((0, 0, xn), (1, 1, yn), (2, 0, yn), (3, 1, xn)):
            sends += [_rcopy(own(part, k), piece(me, c, part, k), *sems(flow, k), peer) for k in range(kc)]
        for cp in sends:
            cp.start()
        landing = ((0, cxn, 0), (1, cyn, 1), (2, cyn, 0), (3, cxn, 1), (4, cdg, 0), (5, cdg, 1))
        for i, (flow, chip, part) in enumerate(landing):
            for k in range(kc):
                arrival(flow, chip, c, part, k).wait_recv()
                if flow < 2:
                    on = _rcopy(piece(chip, c, part, k), piece(chip, c, part, k), *sems(4 + flow, k),
                                yn if flow == 0 else xn)
                    on.start()
                    sends.append(on)
                fw = _rcopy(piece(chip, c, part, k), piece(chip, c, part, k), *sems(6 + i, k), sibling)
                fw.start()
                sends.append(fw)
        for i, (flow, chip, part) in enumerate(landing):
            for k in range(kc):
                arrival(6 + i, chip, 1 - c, part, k).wait_recv()
        for cp in sends:
            cp.wait_send()
        wait_small()

    full = jax.ShapeDtypeStruct((N_SHARD, r, n), shard.dtype)
    others, slots = _pallas(
        body, name=name, in_specs=[ANY, ANY], out_specs=[ANY, ANY],
        out_shape=[full, jax.ShapeDtypeStruct((N_SHARD,) + small.shape, F32)],
        scratch_shapes=[pltpu.SemaphoreType.DMA((nflow * kc,)), pltpu.SemaphoreType.DMA((nflow * kc,)),
                        pltpu.SemaphoreType.DMA((3,)), pltpu.SemaphoreType.DMA((3,)), pltpu.SemaphoreType.DMA])(
                            shard, small)

    def place(chip_ref, own_ref, all_ref, o_ref):
        o_ref[0] = own_ref[...]

    gathered = _pallas(
        place, name=name + "_own",
        grid_spec=pltpu.PrefetchScalarGridSpec(
            num_scalar_prefetch=1, grid=(r // tr,),
            in_specs=[pl.BlockSpec((tr, n), lambda i, chip_ref: (i, 0)), ANY],
            out_specs=pl.BlockSpec((1, tr, n), lambda i, chip_ref: (chip_ref[0], i, 0))),
        out_shape=full, input_output_aliases={2: 0}, compiler_params=_cp("parallel"))(chip, shard, others)
    return gathered, slots


def _pair_copies(src_ref, dst_ref, send_sems, recv_sems):
    hn = src_ref.shape[1] // 2
    cw = hn // PAIR_CHUNKS
    assert cw * PAIR_CHUNKS == hn and cw % LANES == 0
    x, y, c, _ = _place()
    return [_rcopy(src_ref.at[:, pl.ds((1 - c) * hn + k * cw, cw)], dst_ref.at[:, pl.ds(k * cw, cw)],
                   send_sems.at[k], recv_sems.at[k], (x, y, 1 - c)) for k in range(PAIR_CHUNKS)]


def _pair_send(parts, name):
    n = parts[0].shape[1]
    hn = n // 2
    kc = PAIR_CHUNKS
    cw = hn // kc
    assert cw * kc == hn and cw % LANES == 0
    ns = len(parts)

    def body(*refs):
        srcs, dsts, send_sems, recv_sems = refs[:ns], refs[ns:2 * ns], refs[2 * ns], refs[2 * ns + 1]
        x, y, c, _ = _place()
        cps = [_rcopy(srcs[s].at[:, pl.ds((1 - c) * hn + k * cw, cw)], dsts[s].at[:, pl.ds(k * cw, cw)],
                      send_sems.at[s * kc + k], recv_sems.at[s * kc + k], (x, y, 1 - c))
               for s in range(ns) for k in range(kc)]
        for cp in cps:
            cp.start()
        for cp in cps:
            cp.wait()

    return _pallas(
        body, name=name, in_specs=[ANY] * ns, out_specs=[ANY] * ns,
        out_shape=[jax.ShapeDtypeStruct((p.shape[0], hn), F32) for p in parts],
        scratch_shapes=[pltpu.SemaphoreType.DMA((ns * kc,)), pltpu.SemaphoreType.DMA((ns * kc,))])(*parts)


REDUCE_TILE = 256


def _exchange_copies(g_ref, got_ref, send_sems, recv_sems):
    hn = g_ref.shape[2]
    kc = GATHER_CHUNKS
    cw = hn // kc
    assert cw * kc == hn and cw % LANES == 0
    x, y, c, chips = _place()
    return [_rcopy(g_ref.at[2 * chip[0] + chip[1], :, pl.ds(k * cw, cw)], got_ref.at[j, :, pl.ds(k * cw, cw)],
                   send_sems.at[j * kc + k], recv_sems.at[j * kc + k], (*chip, c))
            for j, chip in enumerate(chips) for k in range(kc)]


def _exchange_scratch():
    return [pltpu.SemaphoreType.DMA((3 * GATHER_CHUNKS,)), pltpu.SemaphoreType.DMA((3 * GATHER_CHUNKS,))]


def _exchange_shape(ga):
    return jax.ShapeDtypeStruct((3,) + ga.shape[1:], ga.dtype)


def _chip_sum(ga, got, place, name):
    _, r, hn = ga.shape
    tc = REDUCE_TILE
    nt = hn // tc

    def body(place_ref, own_ref, got_ref, o_ref):
        acc = own_ref[0].astype(F32)
        for j in range(3):
            acc = acc + got_ref[j].astype(F32)
        o_ref[...] = acc

    return _pallas(
        body, name=name,
        grid_spec=pltpu.PrefetchScalarGridSpec(
            num_scalar_prefetch=1, grid=(nt,),
            in_specs=[pl.BlockSpec((1, r, tc), lambda i, place_ref: (place_ref[0], 0, i)),
                      pl.BlockSpec((3, r, tc), lambda i, place_ref: (0, 0, i))],
            out_specs=pl.BlockSpec((r, tc), lambda i, place_ref: (0, place_ref[1] * nt + i))),
        out_shape=jax.ShapeDtypeStruct((r, 2 * hn), F32), compiler_params=_cp("parallel"))(place, ga, got)


def _pair_join(buf, name, small=None):
    r, n = buf.shape
    hn = n // 2
    kc = JOIN_CHUNKS
    cw = hn // kc
    assert cw * kc == hn and cw % LANES == 0

    def body(in_ref, *refs):
        if small is None:
            out_ref, send_sems, recv_sems = refs
        else:
            p_ref, out_ref, slots_ref, send_sems, recv_sems = refs[:5]
            start_small, wait_small = _small_exchange(p_ref, slots_ref, *refs[5:])
            start_small()
        x, y, c, _ = _place()
        cps = [_rcopy(out_ref.at[:, pl.ds(c * hn + k * cw, cw)], out_ref.at[:, pl.ds(c * hn + k * cw, cw)],
                      send_sems.at[k], recv_sems.at[k], (x, y, 1 - c)) for k in range(kc)]
        for cp in cps:
            cp.start()
        for k in range(kc):
            cols = out_ref.at[:, pl.ds((1 - c) * hn + k * cw, cw)]
            _rcopy(cols, cols, send_sems.at[k], recv_sems.at[k], (x, y, 1 - c)).wait_recv()
        for cp in cps:
            cp.wait_send()
        if small is not None:
            wait_small()

    sems = [pltpu.SemaphoreType.DMA((kc,)), pltpu.SemaphoreType.DMA((kc,))]
    if small is None:
        return _pallas(body, name=name, in_specs=[ANY], out_specs=ANY, out_shape=jax.ShapeDtypeStruct((r, n), F32),
                       input_output_aliases={0: 0}, scratch_shapes=sems)(buf)
    return _pallas(
        body, name=name, in_specs=[ANY, ANY], out_specs=[ANY, ANY],
        out_shape=[jax.ShapeDtypeStruct((r, n), F32), jax.ShapeDtypeStruct((N_DEV,) + small.shape, F32)],
        input_output_aliases={0: 0}, scratch_shapes=sems + _small_scratch())(buf, small)


def _reduce_pair(parts, got, pieces, shard_rows, place, tag):
    if len(got) < len(parts):
        got = list(got) + list(_pair_send(parts[len(got):], tag + "_pair_send"))
    return _pair_add_to_shards(parts, got, pieces, shard_rows, place[1:2], tag + "_pair_add")


def _reduce_finish(ga, slabs, place, tag):
    return _pair_join(_chip_sum(ga, slabs, place, tag + "_chip_sum"), tag + "_pair_join")


N_DEV = 8


def _small_exchange(p_ref, slots_ref, send_sems, recv_sems, local_sem):
    x, y, c, _ = _place()
    my = 4 * x + 2 * y + c

    def sends():
        return [_rcopy(p_ref, slots_ref.at[my], send_sems.at[k - 1], recv_sems.at[k - 1],
                       (x ^ ((k >> 2) & 1), y ^ ((k >> 1) & 1), c ^ (k & 1))) for k in range(1, N_DEV)]

    def local():
        return pltpu.make_async_copy(p_ref, slots_ref.at[my], local_sem)

    def start():
        local().start()
        for cp in sends():
            cp.start()

    def wait():
        for k in range(1, N_DEV):
            _rcopy(p_ref, slots_ref.at[my ^ k], send_sems.at[k - 1], recv_sems.at[k - 1], (x, y, c)).wait_recv()
        for cp in sends():
            cp.wait_send()
        local().wait()

    return start, wait


def _chip_small_exchange(p_ref, slots_ref, send_sems, recv_sems, local_sem):
    x, y, c, chips = _place()
    me = 2 * x + y

    def sends():
        return [_rcopy(p_ref, slots_ref.at[me], send_sems.at[j], recv_sems.at[j], (*chip, c))
                for j, chip in enumerate(chips)]

    def local():
        return pltpu.make_async_copy(p_ref, slots_ref.at[me], local_sem)

    def start():
        local().start()
        for cp in sends():
            cp.start()

    def wait():
        for j, chip in enumerate(chips):
            slot = slots_ref.at[2 * chip[0] + chip[1]]
            _rcopy(slot, slot, send_sems.at[j], recv_sems.at[j], (*chip, c)).wait_recv()
        for cp in sends():
            cp.wait_send()
        local().wait()

    return start, wait


def _small_scratch():
    return [pltpu.SemaphoreType.DMA((N_DEV - 1,)), pltpu.SemaphoreType.DMA((N_DEV - 1,)), pltpu.SemaphoreType.DMA]


def _sum_slots(slots, name):
    _, rows, n = slots.shape

    def body(s_ref, o_ref):
        acc = s_ref[0]
        for j in range(1, N_DEV):
            acc = acc + s_ref[j]
        o_ref[...] = acc

    vm = pl.BlockSpec(memory_space=pltpu.VMEM)
    return _pallas(body, name=name, in_specs=[vm], out_specs=vm, out_shape=jax.ShapeDtypeStruct((rows, n), F32))(slots)


def _adamw(w, g, m, v, name):
    r, n = w.shape
    tr = _tile(r, 256, 8)
    c1 = 1.0 / (1.0 - ADAM_B1 ** ADAM_STEP)
    c2 = 1.0 / (1.0 - ADAM_B2 ** ADAM_STEP)

    def body(w_ref, g_ref, m_ref, v_ref, d_ref, mo_ref, vo_ref, go_ref):
        gv = g_ref[...]
        mn = ADAM_B1 * m_ref[...] + (1.0 - ADAM_B1) * gv
        vn = ADAM_B2 * v_ref[...] + (1.0 - ADAM_B2) * (gv * gv)
        d_ref[...] = -ADAM_LR * ((mn * c1) / (jnp.sqrt(vn * c2) + ADAM_EPS) + ADAM_WD * w_ref[...])
        mo_ref[...] = mn
        vo_ref[...] = vn
        go_ref[...] = gv

    spec = pl.BlockSpec((tr, n), lambda i: (i, 0))
    shp = jax.ShapeDtypeStruct((r, n), F32)
    return _pallas(body, name=name, grid=(r // tr,), in_specs=[spec] * 4, out_specs=[spec] * 4, out_shape=[shp] * 4,
                   compiler_params=_cp("parallel"))(w, g, m, v)


PACK_W = 1024
SMALL_REPL = ("norm_pre_w", "conv_b", "ssd_norm_w", "norm_post_w")
SMALL_HEAD = ("dt_bias", "a_log", "d_skip", "attn_sinks")


def _rows(a):
    return a.reshape(-1, PACK_W)


def _head_row(vals, extra=None):
    parts = [vals[n].reshape(1, -1) for n in SMALL_HEAD]
    if extra is not None:
        parts.append(extra.reshape(1, 1))
    row = jnp.concatenate(parts, axis=1)
    return jnp.pad(row, ((0, 0), (0, PACK_W - row.shape[1])))


def _pad_rows(a, rows):
    return jnp.pad(a, ((0, rows - a.shape[0]), (0, 0)))


def _pack_repl(vals, extra=None):
    body = jnp.concatenate([_rows(vals[n]) for n in SMALL_REPL] + [_head_row(vals, extra)], axis=0)
    return _pad_rows(body, 16)


def _unpack_repl(buf):
    out, r = {}, 0
    for n, k in zip(SMALL_REPL, (2, 4, 2, 2)):
        out[n] = buf[r:r + k].reshape(1, k * PACK_W)
        r += k
    col = 0
    for n, k in zip(SMALL_HEAD, (32, 32, 32, 16)):
        out[n] = buf[r:r + 1, col:col + k]
        col += k
    return out, buf[r, col]


def kernel(x, meta_tokens, norm_pre_w, w_in, conv_w, conv_b, dt_bias, a_log, d_skip, ssd_norm_w, attn_sinks, w_out, norm_post_w, loss_target, m_meta_tokens, m_norm_pre_w, m_w_in, m_conv_w, m_conv_b, m_dt_bias, m_a_log, m_d_skip, m_ssd_norm_w, m_attn_sinks, m_w_out, m_norm_post_w, v_meta_tokens, v_norm_pre_w, v_w_in, v_conv_w, v_conv_b, v_dt_bias, v_a_log, v_d_skip, v_ssd_norm_w, v_attn_sinks, v_w_out, v_norm_post_w):
    names = ("meta_tokens", "norm_pre_w", "w_in", "conv_w", "conv_b", "dt_bias", "a_log", "d_skip", "ssd_norm_w",
             "attn_sinks", "w_out", "norm_post_w")
    w = dict(zip(names, (meta_tokens, norm_pre_w, w_in, conv_w, conv_b, dt_bias, a_log, d_skip, ssd_norm_w, attn_sinks,
                         w_out, norm_post_w)))
    m = dict(zip(names, (m_meta_tokens, m_norm_pre_w, m_w_in, m_conv_w, m_conv_b, m_dt_bias, m_a_log, m_d_skip,
                         m_ssd_norm_w, m_attn_sinks, m_w_out, m_norm_post_w)))
    v = dict(zip(names, (v_meta_tokens, v_norm_pre_w, v_w_in, v_conv_w, v_conv_b, v_dt_bias, v_a_log, v_d_skip,
                         v_ssd_norm_w, v_attn_sinks, v_w_out, v_norm_post_w)))
    cx, cy, cc = lax.axis_index("x"), lax.axis_index("y"), lax.axis_index("c")
    chip = 2 * cx + cy
    meta_cols = D_MODEL // N_SHARD
    conv_cols = D_CONV // N_SHARD

    place = jnp.stack([chip, cc]).astype(jnp.int32)
    small = jnp.concatenate([_pad_rows(conv_w[0], 8), _rows(meta_tokens)], axis=0)
    w_in_all, small_all = _gather_shards(_bf(w_in[0]), "gather_w_in", GATHER_CHUNKS, place[0:1], small)
    w_re = _shards_to_re(w_in_all)
    conv_full = jnp.transpose(small_all[:, 0:CONV_WIDTH], (1, 0, 2)).reshape(CONV_WIDTH, D_CONV)
    meta_full = jnp.transpose(small_all[:, 8:16].reshape(N_SHARD, N_META, meta_cols), (1, 0, 2)).reshape(N_META, D_MODEL)

    loss_dev, grad_x, g = _local_step(x[0], loss_target[0], meta_full, norm_pre_w, w_re, conv_full, conv_b, dt_bias,
                                      a_log, d_skip, ssd_norm_w, attn_sinks, _bf(w_out[0]), norm_post_w, place)
    g_w_out = g["w_out"]

    packed = jnp.concatenate([_rows(g["conv_w"]), _rows(g["meta_tokens"]), _pack_repl(g, loss_dev)], axis=0)
    g_w_in, slots = _pair_join(g["w_in_half"], "gw_in_pair_join", small=packed)
    red = _sum_slots(slots, "reduce_small")
    g_conv_full = red[0:16].reshape(CONV_WIDTH, D_CONV)
    g_meta_full = red[16:48].reshape(N_META, D_MODEL)
    g_small, loss = _unpack_repl(red[48:64])
    grads = dict(g_small)
    grads["w_in"] = g_w_in
    grads["w_out"] = g_w_out
    grads["conv_w"] = lax.dynamic_slice(g_conv_full, (0, chip * conv_cols), (CONV_WIDTH, conv_cols))
    grads["meta_tokens"] = lax.dynamic_slice(g_meta_full, (0, chip * meta_cols), (N_META, meta_cols))

    upd = {}
    upd["w_in"] = [jnp.swapaxes(a, 0, 1) for a in _adamw(jnp.swapaxes(w_in[0], 0, 1), g_w_in, jnp.swapaxes(m_w_in[0], 0, 1),
                                                         jnp.swapaxes(v_w_in[0], 0, 1), "adamw_w_in")]
    grads["w_in"] = upd["w_in"][3]
    upd["w_out"] = _adamw(w_out[0], g_w_out, m_w_out[0], v_w_out[0], "adamw_w_out")
    grads["w_out"] = upd["w_out"][3]

    def pack_small(vals, conv, meta):
        return jnp.concatenate([_pad_rows(conv.reshape(CONV_WIDTH, conv_cols), 8), _rows(meta), _pack_repl(vals)], axis=0)

    sm = _adamw(pack_small(w, w["conv_w"], w["meta_tokens"]), pack_small(grads, grads["conv_w"], grads["meta_tokens"]),
                pack_small(m, m["conv_w"], m["meta_tokens"]), pack_small(v, v["conv_w"], v["meta_tokens"]),
                "adamw_small")
    for n in names:
        if n not in ("w_in", "w_out"):
            upd[n] = [None, None, None]
    for k, buf in enumerate(sm[:3]):
        upd["conv_w"][k] = buf[0:CONV_WIDTH]
        upd["meta_tokens"][k] = buf[8:16].reshape(N_META, meta_cols)
        rest, _ = _unpack_repl(buf[16:32])
        for n in SMALL_REPL + SMALL_HEAD:
            upd[n][k] = rest[n]

    def shaped(n, a):
        return a.reshape(w[n].shape)

    outs = [loss, grad_x[None]]
    outs += [shaped(n, grads[n]) for n in names]
    for k in range(3):
        outs += [shaped(n, upd[n][k]) for n in names]
    return tuple(outs)
```
